```python
import math
import jax, jax.numpy as jnp
from jax import lax
import numpy as np

D_MODEL = 1024
BATCH = 8
SEQ = 2048
DEPTH = 1

N_META = 16
D_MIX = D_MODEL
D_CONV = D_MIX // 2
CONV_HEAD_DIM = 64
N_CONV_HEADS = D_CONV // CONV_HEAD_DIM
D_SSM = D_MIX - D_CONV
SSM_GROUP = 16
N_SSM_GROUPS = D_SSM // SSM_GROUP
SSM_STATE = 64
CONV_WIDTH = 3
D_FF = 2816
D_IN_PROJ = 3 * D_CONV + D_SSM
RMS_EPS = 1e-6
DT_MIN = 1e-3
DT_MAX = 1e-1

kernel_name = "hymba_conv_s5_hybrid_layer"


def rms_norm(x, g):
    xf = x.astype(jnp.float32)
    y = xf * lax.rsqrt(jnp.mean(xf * xf, axis=-1, keepdims=True) + RMS_EPS)
    return (y * g.astype(jnp.float32)).astype(x.dtype)


def causal_dwconv(x, w, b=None):
    c = x.shape[-1]
    y = lax.conv_general_dilated(
        x, w[:, None, :].astype(x.dtype), window_strides=(1,),
        padding=[(CONV_WIDTH - 1, 0)], dimension_numbers=("NWC", "WIO", "NWC"),
        feature_group_count=c)
    if b is not None:
        y = y + b.astype(x.dtype)
    return y


def s5_group_ssm(u, lam_re, lam_im, log_dt, b_re, b_im, c_re, c_im, d_skip, w_glu):
    bsz, seq_len, _ = u.shape
    f32 = jnp.float32
    uf = u.astype(f32).reshape(bsz, seq_len, N_SSM_GROUPS, SSM_GROUP)
    lr = lam_re.astype(f32)
    li = lam_im.astype(f32)
    dt = jnp.exp(log_dt.astype(f32))[:, None]
    mag = jnp.exp(lr * dt)
    ang = li * dt
    a_re = mag * jnp.cos(ang)
    a_im = mag * jnp.sin(ang)
    den = lr * lr + li * li
    nr = a_re - 1.0
    f_re = (nr * lr + a_im * li) / den
    f_im = (a_im * lr - nr * li) / den
    br = b_re.astype(f32)
    bi = b_im.astype(f32)
    bb_re = f_re[..., None] * br - f_im[..., None] * bi
    bb_im = f_re[..., None] * bi + f_im[..., None] * br
    bu_re = jnp.einsum("blgh,gph->blgp", uf, bb_re)
    bu_im = jnp.einsum("blgh,gph->blgp", uf, bb_im)
    a_re_t = jnp.broadcast_to(a_re[None, None], (1, seq_len, N_SSM_GROUPS, SSM_STATE))
    a_im_t = jnp.broadcast_to(a_im[None, None], (1, seq_len, N_SSM_GROUPS, SSM_STATE))

    def combine(e1, e2):
        ar1, ai1, sr1, si1 = e1
        ar2, ai2, sr2, si2 = e2
        return (ar1 * ar2 - ai1 * ai2,
                ar1 * ai2 + ai1 * ar2,
                ar2 * sr1 - ai2 * si1 + sr2,
                ar2 * si1 + ai2 * sr1 + si2)

    _, _, s_re, s_im = lax.associative_scan(combine, (a_re_t, a_im_t, bu_re, bu_im), axis=1)
    y = (jnp.einsum("blgp,ghp->blgh", s_re, c_re.astype(f32))
         - jnp.einsum("blgp,ghp->blgh", s_im, c_im.astype(f32))
         + d_skip.astype(f32) * uf)
    y = y.reshape(bsz, seq_len, D_SSM)
    g = jax.nn.gelu(y)
    out = g * jax.nn.sigmoid(g @ w_glu.astype(f32))
    return out.astype(u.dtype)


def _fwd_setup_inputs(seed: int = 0) -> dict:
    key = jax.random.key(seed)
    ks = jax.random.split(key, 24)
    f32 = jnp.float32
    nrm = lambda k, shape, s: jax.random.normal(k, shape, f32) * s
    x = jax.random.normal(ks[0], (BATCH, SEQ, D_MODEL), f32)
    meta_tokens = nrm(ks[1], (N_META, D_MODEL), 1.0)
    norm_mix_g = 1.0 + nrm(ks[2], (DEPTH, D_MODEL), 0.02)
    w_in = nrm(ks[3], (DEPTH, D_MODEL, D_IN_PROJ), D_MODEL ** -0.5)
    conv_w = nrm(ks[4], (DEPTH, CONV_WIDTH, D_CONV), CONV_WIDTH ** -0.5)
    n = jnp.arange(SSM_STATE, dtype=f32)
    ssm_lam_re = -0.5 + nrm(ks[5], (DEPTH, N_SSM_GROUPS, SSM_STATE), 1e-3)
    ssm_lam_im = math.pi * n + nrm(ks[6], (DEPTH, N_SSM_GROUPS, SSM_STATE), 1e-3)
    ssm_log_dt = jax.random.uniform(ks[7], (DEPTH, N_SSM_GROUPS), f32,
                                    math.log(DT_MIN), math.log(DT_MAX))
    b_scale = (2.0 * SSM_GROUP) ** -0.5
    ssm_b_re = nrm(ks[8], (DEPTH, N_SSM_GROUPS, SSM_STATE, SSM_GROUP), b_scale)
    ssm_b_im = nrm(ks[9], (DEPTH, N_SSM_GROUPS, SSM_STATE, SSM_GROUP), b_scale)
    c_scale = (2.0 * SSM_STATE) ** -0.5
    ssm_c_re = nrm(ks[10], (DEPTH, N_SSM_GROUPS, SSM_GROUP, SSM_STATE), c_scale)
    ssm_c_im = nrm(ks[11], (DEPTH, N_SSM_GROUPS, SSM_GROUP, SSM_STATE), c_scale)
    ssm_d = nrm(ks[12], (DEPTH, N_SSM_GROUPS, SSM_GROUP), 1.0)
    ssm_w_glu = nrm(ks[13], (DEPTH, D_SSM, D_SSM), D_SSM ** -0.5)
    gain_conv_out = 1.0 + nrm(ks[14], (DEPTH, D_CONV), 0.02)
    gain_ssm_out = 1.0 + nrm(ks[15], (DEPTH, D_SSM), 0.02)
    w_out = nrm(ks[16], (DEPTH, D_MIX, D_MODEL), D_MIX ** -0.5)
    norm_ffn_g = 1.0 + nrm(ks[17], (DEPTH, D_MODEL), 0.02)
    w_up = nrm(ks[18], (DEPTH, D_MODEL, 2 * D_FF), D_MODEL ** -0.5)
    ffn_conv_w = nrm(ks[19], (DEPTH, CONV_WIDTH, 2 * D_FF), CONV_WIDTH ** -0.5)
    ffn_conv_b = nrm(ks[20], (DEPTH, 2 * D_FF), 0.01)
    w_down = nrm(ks[21], (DEPTH, D_FF, D_MODEL), D_FF ** -0.5)
    norm_final_g = 1.0 + nrm(ks[22], (D_MODEL,), 0.02)
    return {"x": x, "meta_tokens": meta_tokens, "norm_mix_g": norm_mix_g, "w_in": w_in,
            "conv_w": conv_w, "ssm_lam_re": ssm_lam_re, "ssm_lam_im": ssm_lam_im,
            "ssm_log_dt": ssm_log_dt, "ssm_b_re": ssm_b_re, "ssm_b_im": ssm_b_im,
            "ssm_c_re": ssm_c_re, "ssm_c_im": ssm_c_im, "ssm_d": ssm_d,
            "ssm_w_glu": ssm_w_glu, "gain_conv_out": gain_conv_out,
            "gain_ssm_out": gain_ssm_out, "w_out": w_out, "norm_ffn_g": norm_ffn_g,
            "w_up": w_up, "ffn_conv_w": ffn_conv_w, "ffn_conv_b": ffn_conv_b,
            "w_down": w_down, "norm_final_g": norm_final_g}


def _fwd_reference(x, meta_tokens, norm_mix_g, w_in, conv_w, ssm_lam_re, ssm_lam_im, ssm_log_dt,
              ssm_b_re, ssm_b_im, ssm_c_re, ssm_c_im, ssm_d, ssm_w_glu, gain_conv_out,
              gain_ssm_out, w_out, norm_ffn_g, w_up, ffn_conv_w, ffn_conv_b, w_down,
              norm_final_g):
    bsz = x.shape[0]
    meta = jnp.broadcast_to(meta_tokens.astype(x.dtype)[None], (bsz, N_META, D_MODEL))
    h = jnp.concatenate([meta, x], axis=1)
    for i in range(DEPTH):
        hn = rms_norm(h, norm_mix_g[i])
        proj = hn @ w_in[i].astype(h.dtype)
        b_gate = proj[..., :D_CONV]
        c_gate = proj[..., D_CONV:2 * D_CONV]
        v = proj[..., 2 * D_CONV:3 * D_CONV]
        u = proj[..., 3 * D_CONV:]
        conv_out = b_gate * causal_dwconv(c_gate * v, conv_w[i])
        ssm_out = s5_group_ssm(u, ssm_lam_re[i], ssm_lam_im[i], ssm_log_dt[i],
                               ssm_b_re[i], ssm_b_im[i], ssm_c_re[i], ssm_c_im[i],
                               ssm_d[i], ssm_w_glu[i])
        mixed = jnp.concatenate([rms_norm(conv_out, gain_conv_out[i]),
                                 rms_norm(ssm_out, gain_ssm_out[i])], axis=-1)
        h = h + mixed @ w_out[i].astype(h.dtype)
        hn = rms_norm(h, norm_ffn_g[i])
        up = causal_dwconv(hn @ w_up[i].astype(h.dtype), ffn_conv_w[i], ffn_conv_b[i])
        a = up[..., :D_FF]
        val = up[..., D_FF:]
        h = h + (jax.nn.silu(a) * val) @ w_down[i].astype(h.dtype)
    y = rms_norm(h, norm_final_g)
    return y[:, N_META:]


import jax as _jax
import jax.numpy as _jnp

TWIN_FORMAT = 'train_step'
FWD_PARAMS = ['x', 'meta_tokens', 'norm_mix_g', 'w_in', 'conv_w', 'ssm_lam_re', 'ssm_lam_im', 'ssm_log_dt', 'ssm_b_re', 'ssm_b_im', 'ssm_c_re', 'ssm_c_im', 'ssm_d', 'ssm_w_glu', 'gain_conv_out', 'gain_ssm_out', 'w_out', 'norm_ffn_g', 'w_up', 'ffn_conv_w', 'ffn_conv_b', 'w_down', 'norm_final_g']
TWIN_WEIGHTS = ['meta_tokens', 'norm_mix_g', 'w_in', 'conv_w', 'ssm_lam_re', 'ssm_lam_im', 'ssm_log_dt', 'ssm_b_re', 'ssm_b_im', 'ssm_c_re', 'ssm_c_im', 'ssm_d', 'ssm_w_glu', 'gain_conv_out', 'gain_ssm_out', 'w_out', 'norm_ffn_g', 'w_up', 'ffn_conv_w', 'ffn_conv_b', 'w_down', 'norm_final_g']
TWIN_DIFF_INPUT = 'x'
TWIN_INPUTS = ['x', 'meta_tokens', 'norm_mix_g', 'w_in', 'conv_w', 'ssm_lam_re', 'ssm_lam_im', 'ssm_log_dt', 'ssm_b_re', 'ssm_b_im', 'ssm_c_re', 'ssm_c_im', 'ssm_d', 'ssm_w_glu', 'gain_conv_out', 'gain_ssm_out', 'w_out', 'norm_ffn_g', 'w_up', 'ffn_conv_w', 'ffn_conv_b', 'w_down', 'norm_final_g', 'loss_target', 'm_meta_tokens', 'm_norm_mix_g', 'm_w_in', 'm_conv_w', 'm_ssm_lam_re', 'm_ssm_lam_im', 'm_ssm_log_dt', 'm_ssm_b_re', 'm_ssm_b_im', 'm_ssm_c_re', 'm_ssm_c_im', 'm_ssm_d', 'm_ssm_w_glu', 'm_gain_conv_out', 'm_gain_ssm_out', 'm_w_out', 'm_norm_ffn_g', 'm_w_up', 'm_ffn_conv_w', 'm_ffn_conv_b', 'm_w_down', 'm_norm_final_g', 'v_meta_tokens', 'v_norm_mix_g', 'v_w_in', 'v_conv_w', 'v_ssm_lam_re', 'v_ssm_lam_im', 'v_ssm_log_dt', 'v_ssm_b_re', 'v_ssm_b_im', 'v_ssm_c_re', 'v_ssm_c_im', 'v_ssm_d', 'v_ssm_w_glu', 'v_gain_conv_out', 'v_gain_ssm_out', 'v_w_out', 'v_norm_ffn_g', 'v_w_up', 'v_ffn_conv_w', 'v_ffn_conv_b', 'v_w_down', 'v_norm_final_g']
TWIN_OUTPUTS = ['loss', 'grad_x', 'grad_meta_tokens', 'grad_norm_mix_g', 'grad_w_in', 'grad_conv_w', 'grad_ssm_lam_re', 'grad_ssm_lam_im', 'grad_ssm_log_dt', 'grad_ssm_b_re', 'grad_ssm_b_im', 'grad_ssm_c_re', 'grad_ssm_c_im', 'grad_ssm_d', 'grad_ssm_w_glu', 'grad_gain_conv_out', 'grad_gain_ssm_out', 'grad_w_out', 'grad_norm_ffn_g', 'grad_w_up', 'grad_ffn_conv_w', 'grad_ffn_conv_b', 'grad_w_down', 'grad_norm_final_g', 'delta_meta_tokens', 'delta_norm_mix_g', 'delta_w_in', 'delta_conv_w', 'delta_ssm_lam_re', 'delta_ssm_lam_im', 'delta_ssm_log_dt', 'delta_ssm_b_re', 'delta_ssm_b_im', 'delta_ssm_c_re', 'delta_ssm_c_im', 'delta_ssm_d', 'delta_ssm_w_glu', 'delta_gain_conv_out', 'delta_gain_ssm_out', 'delta_w_out', 'delta_norm_ffn_g', 'delta_w_up', 'delta_ffn_conv_w', 'delta_ffn_conv_b', 'delta_w_down', 'delta_norm_final_g', 'new_m_meta_tokens', 'new_m_norm_mix_g', 'new_m_w_in', 'new_m_conv_w', 'new_m_ssm_lam_re', 'new_m_ssm_lam_im', 'new_m_ssm_log_dt', 'new_m_ssm_b_re', 'new_m_ssm_b_im', 'new_m_ssm_c_re', 'new_m_ssm_c_im', 'new_m_ssm_d', 'new_m_ssm_w_glu', 'new_m_gain_conv_out', 'new_m_gain_ssm_out', 'new_m_w_out', 'new_m_norm_ffn_g', 'new_m_w_up', 'new_m_ffn_conv_w', 'new_m_ffn_conv_b', 'new_m_w_down', 'new_m_norm_final_g', 'new_v_meta_tokens', 'new_v_norm_mix_g', 'new_v_w_in', 'new_v_conv_w', 'new_v_ssm_lam_re', 'new_v_ssm_lam_im', 'new_v_ssm_log_dt', 'new_v_ssm_b_re', 'new_v_ssm_b_im', 'new_v_ssm_c_re', 'new_v_ssm_c_im', 'new_v_ssm_d', 'new_v_ssm_w_glu', 'new_v_gain_conv_out', 'new_v_gain_ssm_out', 'new_v_w_out', 'new_v_norm_ffn_g', 'new_v_w_up', 'new_v_ffn_conv_w', 'new_v_ffn_conv_b', 'new_v_w_down', 'new_v_norm_final_g']
TWIN_LEAF_KINDS = {'loss': 'loss', 'grad_x': 'grad_x', 'grad_meta_tokens': 'grad_w', 'grad_norm_mix_g': 'grad_w', 'grad_w_in': 'grad_w', 'grad_conv_w': 'grad_w', 'grad_ssm_lam_re': 'grad_w', 'grad_ssm_lam_im': 'grad_w', 'grad_ssm_log_dt': 'grad_w', 'grad_ssm_b_re': 'grad_w', 'grad_ssm_b_im': 'grad_w', 'grad_ssm_c_re': 'grad_w', 'grad_ssm_c_im': 'grad_w', 'grad_ssm_d': 'grad_w', 'grad_ssm_w_glu': 'grad_w', 'grad_gain_conv_out': 'grad_w', 'grad_gain_ssm_out': 'grad_w', 'grad_w_out': 'grad_w', 'grad_norm_ffn_g': 'grad_w', 'grad_w_up': 'grad_w', 'grad_ffn_conv_w': 'grad_w', 'grad_ffn_conv_b': 'grad_w', 'grad_w_down': 'grad_w', 'grad_norm_final_g': 'grad_w', 'delta_meta_tokens': 'delta_w', 'delta_norm_mix_g': 'delta_w', 'delta_w_in': 'delta_w', 'delta_conv_w': 'delta_w', 'delta_ssm_lam_re': 'delta_w', 'delta_ssm_lam_im': 'delta_w', 'delta_ssm_log_dt': 'delta_w', 'delta_ssm_b_re': 'delta_w', 'delta_ssm_b_im': 'delta_w', 'delta_ssm_c_re': 'delta_w', 'delta_ssm_c_im': 'delta_w', 'delta_ssm_d': 'delta_w', 'delta_ssm_w_glu': 'delta_w', 'delta_gain_conv_out': 'delta_w', 'delta_gain_ssm_out': 'delta_w', 'delta_w_out': 'delta_w', 'delta_norm_ffn_g': 'delta_w', 'delta_w_up': 'delta_w', 'delta_ffn_conv_w': 'delta_w', 'delta_ffn_conv_b': 'delta_w', 'delta_w_down': 'delta_w', 'delta_norm_final_g': 'delta_w', 'new_m_meta_tokens': 'new_m', 'new_m_norm_mix_g': 'new_m', 'new_m_w_in': 'new_m', 'new_m_conv_w': 'new_m', 'new_m_ssm_lam_re': 'new_m', 'new_m_ssm_lam_im': 'new_m', 'new_m_ssm_log_dt': 'new_m', 'new_m_ssm_b_re': 'new_m', 'new_m_ssm_b_im': 'new_m', 'new_m_ssm_c_re': 'new_m', 'new_m_ssm_c_im': 'new_m', 'new_m_ssm_d': 'new_m', 'new_m_ssm_w_glu': 'new_m', 'new_m_gain_conv_out': 'new_m', 'new_m_gain_ssm_out': 'new_m', 'new_m_w_out': 'new_m', 'new_m_norm_ffn_g': 'new_m', 'new_m_w_up': 'new_m', 'new_m_ffn_conv_w': 'new_m', 'new_m_ffn_conv_b': 'new_m', 'new_m_w_down': 'new_m', 'new_m_norm_final_g': 'new_m', 'new_v_meta_tokens': 'new_v', 'new_v_norm_mix_g': 'new_v', 'new_v_w_in': 'new_v', 'new_v_conv_w': 'new_v', 'new_v_ssm_lam_re': 'new_v', 'new_v_ssm_lam_im': 'new_v', 'new_v_ssm_log_dt': 'new_v', 'new_v_ssm_b_re': 'new_v', 'new_v_ssm_b_im': 'new_v', 'new_v_ssm_c_re': 'new_v', 'new_v_ssm_c_im': 'new_v', 'new_v_ssm_d': 'new_v', 'new_v_ssm_w_glu': 'new_v', 'new_v_gain_conv_out': 'new_v', 'new_v_gain_ssm_out': 'new_v', 'new_v_w_out': 'new_v', 'new_v_norm_ffn_g': 'new_v', 'new_v_w_up': 'new_v', 'new_v_ffn_conv_w': 'new_v', 'new_v_ffn_conv_b': 'new_v', 'new_v_w_down': 'new_v', 'new_v_norm_final_g': 'new_v'}


def _forward(args):
    return _fwd_reference(*[args[k] for k in FWD_PARAMS])


def _output_shape():
    out = _jax.eval_shape(lambda: _forward(_fwd_setup_inputs(0)))
    return out.shape, out.dtype

N_MICROBATCH = 1
ADAM_LR = 0.001
ADAM_B1 = 0.9
ADAM_B2 = 0.999
ADAM_EPS = 1e-08
ADAM_WD = 0.01
ADAM_STEP = 10
PER_EXAMPLE_BATCH_AXIS = {'x': 0, 'loss_target': 0}
SHARED_INPUTS = []
_WEIGHT_DTYPES = {'meta_tokens': _jnp.float32, 'norm_mix_g': _jnp.float32, 'w_in': _jnp.float32, 'conv_w': _jnp.float32, 'ssm_lam_re': _jnp.float32, 'ssm_lam_im': _jnp.float32, 'ssm_log_dt': _jnp.float32, 'ssm_b_re': _jnp.float32, 'ssm_b_im': _jnp.float32, 'ssm_c_re': _jnp.float32, 'ssm_c_im': _jnp.float32, 'ssm_d': _jnp.float32, 'ssm_w_glu': _jnp.float32, 'gain_conv_out': _jnp.float32, 'gain_ssm_out': _jnp.float32, 'w_out': _jnp.float32, 'norm_ffn_g': _jnp.float32, 'w_up': _jnp.float32, 'ffn_conv_w': _jnp.float32, 'ffn_conv_b': _jnp.float32, 'w_down': _jnp.float32, 'norm_final_g': _jnp.float32}
MOMENT_SCALE = {'meta_tokens': 2.005843e-03, 'norm_mix_g': 1.425768e-01, 'w_in': 9.837124e-02, 'conv_w': 9.808079e-02, 'ssm_lam_re': 4.830826e-03, 'ssm_lam_im': 4.697845e-03, 'ssm_log_dt': 2.564544e+00, 'ssm_b_re': 3.170420e-03, 'ssm_b_im': 3.168234e-03, 'ssm_c_re': 6.372958e-03, 'ssm_c_im': 6.427217e-03, 'ssm_d': 1.087164e-01, 'ssm_w_glu': 2.735097e-02, 'gain_conv_out': 9.851658e-02, 'gain_ssm_out': 1.115475e-01, 'w_out': 9.798890e-02, 'norm_ffn_g': 7.035075e-02, 'w_up': 3.046901e-02, 'ffn_conv_w': 3.043082e-02, 'ffn_conv_b': 2.971571e-02, 'w_down': 5.033962e-02, 'norm_final_g': 1.602704e+01}


def _to_microbatches(a, axis):
    t = _jnp.moveaxis(a, axis, 0)
    t = t.reshape((N_MICROBATCH, t.shape[0] // N_MICROBATCH) + t.shape[1:])
    return _jnp.moveaxis(t, 1, axis + 1)


def setup_inputs(seed: int = 0) -> dict:
    inp = _fwd_setup_inputs(seed)
    key = _jax.random.fold_in(_jax.random.key(seed), 7919)
    shape, _ = _output_shape()
    out = dict(inp)
    out["loss_target"] = _jax.random.normal(_jax.random.fold_in(key, 0), shape, _jnp.float32)
    for i, name in enumerate(TWIN_WEIGHTS):
        w = inp[name].astype(_jnp.float32)
        if MOMENT_SCALE is None:
            s = _jnp.sqrt(_jnp.mean(_jnp.square(w)) + 1e-30)
        else:
            s = MOMENT_SCALE[name]
        km, kv = _jax.random.split(_jax.random.fold_in(key, i + 1))
        out[name] = w
        out["m_" + name] = s * _jax.random.normal(km, w.shape, _jnp.float32)
        out["v_" + name] = (s * s) * _jax.random.uniform(kv, w.shape, _jnp.float32, 0.5, 1.5)
    if N_MICROBATCH > 1:
        for name, axis in PER_EXAMPLE_BATCH_AXIS.items():
            out[name] = _to_microbatches(out[name], axis)
    return {'x': out['x'], 'meta_tokens': out['meta_tokens'], 'norm_mix_g': out['norm_mix_g'], 'w_in': out['w_in'], 'conv_w': out['conv_w'], 'ssm_lam_re': out['ssm_lam_re'], 'ssm_lam_im': out['ssm_lam_im'], 'ssm_log_dt': out['ssm_log_dt'], 'ssm_b_re': out['ssm_b_re'], 'ssm_b_im': out['ssm_b_im'], 'ssm_c_re': out['ssm_c_re'], 'ssm_c_im': out['ssm_c_im'], 'ssm_d': out['ssm_d'], 'ssm_w_glu': out['ssm_w_glu'], 'gain_conv_out': out['gain_conv_out'], 'gain_ssm_out': out['gain_ssm_out'], 'w_out': out['w_out'], 'norm_ffn_g': out['norm_ffn_g'], 'w_up': out['w_up'], 'ffn_conv_w': out['ffn_conv_w'], 'ffn_conv_b': out['ffn_conv_b'], 'w_down': out['w_down'], 'norm_final_g': out['norm_final_g'], 'loss_target': out['loss_target'], 'm_meta_tokens': out['m_meta_tokens'], 'm_norm_mix_g': out['m_norm_mix_g'], 'm_w_in': out['m_w_in'], 'm_conv_w': out['m_conv_w'], 'm_ssm_lam_re': out['m_ssm_lam_re'], 'm_ssm_lam_im': out['m_ssm_lam_im'], 'm_ssm_log_dt': out['m_ssm_log_dt'], 'm_ssm_b_re': out['m_ssm_b_re'], 'm_ssm_b_im': out['m_ssm_b_im'], 'm_ssm_c_re': out['m_ssm_c_re'], 'm_ssm_c_im': out['m_ssm_c_im'], 'm_ssm_d': out['m_ssm_d'], 'm_ssm_w_glu': out['m_ssm_w_glu'], 'm_gain_conv_out': out['m_gain_conv_out'], 'm_gain_ssm_out': out['m_gain_ssm_out'], 'm_w_out': out['m_w_out'], 'm_norm_ffn_g': out['m_norm_ffn_g'], 'm_w_up': out['m_w_up'], 'm_ffn_conv_w': out['m_ffn_conv_w'], 'm_ffn_conv_b': out['m_ffn_conv_b'], 'm_w_down': out['m_w_down'], 'm_norm_final_g': out['m_norm_final_g'], 'v_meta_tokens': out['v_meta_tokens'], 'v_norm_mix_g': out['v_norm_mix_g'], 'v_w_in': out['v_w_in'], 'v_conv_w': out['v_conv_w'], 'v_ssm_lam_re': out['v_ssm_lam_re'], 'v_ssm_lam_im': out['v_ssm_lam_im'], 'v_ssm_log_dt': out['v_ssm_log_dt'], 'v_ssm_b_re': out['v_ssm_b_re'], 'v_ssm_b_im': out['v_ssm_b_im'], 'v_ssm_c_re': out['v_ssm_c_re'], 'v_ssm_c_im': out['v_ssm_c_im'], 'v_ssm_d': out['v_ssm_d'], 'v_ssm_w_glu': out['v_ssm_w_glu'], 'v_gain_conv_out': out['v_gain_conv_out'], 'v_gain_ssm_out': out['v_gain_ssm_out'], 'v_w_out': out['v_w_out'], 'v_norm_ffn_g': out['v_norm_ffn_g'], 'v_w_up': out['v_w_up'], 'v_ffn_conv_w': out['v_ffn_conv_w'], 'v_ffn_conv_b': out['v_ffn_conv_b'], 'v_w_down': out['v_w_down'], 'v_norm_final_g': out['v_norm_final_g']}


def _loss(weights, diff, rest, loss_target):
    with _jax.named_scope("forward"):
        args = {**rest, TWIN_DIFF_INPUT: diff, **{k: w.astype(_WEIGHT_DTYPES[k]) for k, w in weights.items()}}
        y = _forward(args)
    with _jax.named_scope("loss_head"):
        err = _jnp.square(y.astype(_jnp.float32) - loss_target)
        return 0.5 * _jnp.sum(_jnp.mean(err, axis=-1)) if err.ndim else 0.5 * err


def _adamw(w, g, m, v):
    m = ADAM_B1 * m + (1.0 - ADAM_B1) * g
    v = ADAM_B2 * v + (1.0 - ADAM_B2) * _jnp.square(g)
    m_hat = m / (1.0 - ADAM_B1 ** ADAM_STEP)
    v_hat = v / (1.0 - ADAM_B2 ** ADAM_STEP)
    delta = -ADAM_LR * (m_hat / (_jnp.sqrt(v_hat) + ADAM_EPS) + ADAM_WD * w)
    return delta, m, v


def reference(x, meta_tokens, norm_mix_g, w_in, conv_w, ssm_lam_re, ssm_lam_im, ssm_log_dt, ssm_b_re, ssm_b_im, ssm_c_re, ssm_c_im, ssm_d, ssm_w_glu, gain_conv_out, gain_ssm_out, w_out, norm_ffn_g, w_up, ffn_conv_w, ffn_conv_b, w_down, norm_final_g, loss_target, m_meta_tokens, m_norm_mix_g, m_w_in, m_conv_w, m_ssm_lam_re, m_ssm_lam_im, m_ssm_log_dt, m_ssm_b_re, m_ssm_b_im, m_ssm_c_re, m_ssm_c_im, m_ssm_d, m_ssm_w_glu, m_gain_conv_out, m_gain_ssm_out, m_w_out, m_norm_ffn_g, m_w_up, m_ffn_conv_w, m_ffn_conv_b, m_w_down, m_norm_final_g, v_meta_tokens, v_norm_mix_g, v_w_in, v_conv_w, v_ssm_lam_re, v_ssm_lam_im, v_ssm_log_dt, v_ssm_b_re, v_ssm_b_im, v_ssm_c_re, v_ssm_c_im, v_ssm_d, v_ssm_w_glu, v_gain_conv_out, v_gain_ssm_out, v_w_out, v_norm_ffn_g, v_w_up, v_ffn_conv_w, v_ffn_conv_b, v_w_down, v_norm_final_g):
    given = dict(x=x, meta_tokens=meta_tokens, norm_mix_g=norm_mix_g, w_in=w_in, conv_w=conv_w, ssm_lam_re=ssm_lam_re, ssm_lam_im=ssm_lam_im, ssm_log_dt=ssm_log_dt, ssm_b_re=ssm_b_re, ssm_b_im=ssm_b_im, ssm_c_re=ssm_c_re, ssm_c_im=ssm_c_im, ssm_d=ssm_d, ssm_w_glu=ssm_w_glu, gain_conv_out=gain_conv_out, gain_ssm_out=gain_ssm_out, w_out=w_out, norm_ffn_g=norm_ffn_g, w_up=w_up, ffn_conv_w=ffn_conv_w, ffn_conv_b=ffn_conv_b, w_down=w_down, norm_final_g=norm_final_g, loss_target=loss_target, m_meta_tokens=m_meta_tokens, m_norm_mix_g=m_norm_mix_g, m_w_in=m_w_in, m_conv_w=m_conv_w, m_ssm_lam_re=m_ssm_lam_re, m_ssm_lam_im=m_ssm_lam_im, m_ssm_log_dt=m_ssm_log_dt, m_ssm_b_re=m_ssm_b_re, m_ssm_b_im=m_ssm_b_im, m_ssm_c_re=m_ssm_c_re, m_ssm_c_im=m_ssm_c_im, m_ssm_d=m_ssm_d, m_ssm_w_glu=m_ssm_w_glu, m_gain_conv_out=m_gain_conv_out, m_gain_ssm_out=m_gain_ssm_out, m_w_out=m_w_out, m_norm_ffn_g=m_norm_ffn_g, m_w_up=m_w_up, m_ffn_conv_w=m_ffn_conv_w, m_ffn_conv_b=m_ffn_conv_b, m_w_down=m_w_down, m_norm_final_g=m_norm_final_g, v_meta_tokens=v_meta_tokens, v_norm_mix_g=v_norm_mix_g, v_w_in=v_w_in, v_conv_w=v_conv_w, v_ssm_lam_re=v_ssm_lam_re, v_ssm_lam_im=v_ssm_lam_im, v_ssm_log_dt=v_ssm_log_dt, v_ssm_b_re=v_ssm_b_re, v_ssm_b_im=v_ssm_b_im, v_ssm_c_re=v_ssm_c_re, v_ssm_c_im=v_ssm_c_im, v_ssm_d=v_ssm_d, v_ssm_w_glu=v_ssm_w_glu, v_gain_conv_out=v_gain_conv_out, v_gain_ssm_out=v_gain_ssm_out, v_w_out=v_w_out, v_norm_ffn_g=v_norm_ffn_g, v_w_up=v_w_up, v_ffn_conv_w=v_ffn_conv_w, v_ffn_conv_b=v_ffn_conv_b, v_w_down=v_w_down, v_norm_final_g=v_norm_final_g)
    weights = {n: given[n] for n in TWIN_WEIGHTS}
    shared = {n: given[n] for n in SHARED_INPUTS}
    per_example = {n: given[n] for n in ['x']}
    grad_fn = _jax.value_and_grad(_loss, argnums=(0, 1))

    def one_microbatch(ex, loss_target):
        ex = dict(ex)
        diff = ex.pop(TWIN_DIFF_INPUT)
        return grad_fn(weights, diff, {**shared, **ex}, loss_target)

    if N_MICROBATCH == 1:
        loss, (grad_w, grad_x) = one_microbatch(per_example, given["loss_target"])
    else:
        def body(carry, xs):
            loss_sum, grad_sum = carry
            l_k, (gw_k, gx_k) = one_microbatch(xs[0], xs[1])
            with _jax.named_scope("update"):
                return (loss_sum + l_k, _jax.tree.map(_jnp.add, grad_sum, gw_k)), gx_k

        init = (_jnp.zeros((), _jnp.float32), _jax.tree.map(_jnp.zeros_like, weights))
        (loss, grad_w), grad_x = _jax.lax.scan(body, init, (per_example, given["loss_target"]))
    with _jax.named_scope("update"):
        delta_w, new_m, new_v = {}, {}, {}
        for n in TWIN_WEIGHTS:
            delta_w[n], new_m[n], new_v[n] = _adamw(weights[n], grad_w[n], given["m_" + n], given["v_" + n])
    return (loss, grad_x, *[grad_w[n] for n in TWIN_WEIGHTS], *[delta_w[n] for n in TWIN_WEIGHTS],
            *[new_m[n] for n in TWIN_WEIGHTS], *[new_v[n] for n in TWIN_WEIGHTS])
```

```python
import functools
import math

import jax
import jax.numpy as jnp
from jax import lax
from jax.experimental import pallas as pl
from jax.experimental.pallas import tpu as pltpu

F32 = jnp.float32
BF16 = jnp.bfloat16
MESH = pl.DeviceIdType.MESH

N_META = 16
N_GROUPS = 32
GROUP = 16
STATE = 64
RMS_EPS = 1e-6
ADAM_LR = 0.001
ADAM_B1 = 0.9
ADAM_B2 = 0.999
ADAM_EPS = 1e-08
ADAM_WD = 0.01
ADAM_STEP = 10

LANES = 128
SUBLANES = 8
ROW_ALIGN = 128
ROW_TILES = 4
VMEM_LIMIT = 52 * 1024 * 1024
GELU_C = math.sqrt(2.0 / math.pi)
GELU_A = 0.044715


def _cparams(*sem):
    return pltpu.CompilerParams(dimension_semantics=sem, vmem_limit_bytes=VMEM_LIMIT)


def _pick_tile(dim, cap, mult):
    best = None
    for t in range(mult, min(dim, cap) + 1, mult):
        if dim % t == 0:
            best = t
    return best if best is not None else dim


def _mm(a, b, mode, name, out_dtype=F32, acc_in=None):
    if mode == "tn":
        kdim, m = a.shape
    else:
        m, kdim = a.shape
    n = b.shape[0] if mode == "nt" else b.shape[1]
    tm = _pick_tile(m, 1408, LANES if mode == "tn" else 16)
    tn = _pick_tile(n, 512, LANES)
    tk = _pick_tile(kdim, 2816, LANES)
    nk = kdim // tk
    has_acc = acc_in is not None

    def body(*refs):
        if has_acc:
            a_ref, b_ref, c_ref, o_ref = refs[:4]
            rest = refs[4:]
        else:
            a_ref, b_ref, o_ref = refs[:3]
            c_ref = None
            rest = refs[3:]
        if mode == "nn":
            p = jnp.dot(a_ref[...], b_ref[...], preferred_element_type=F32)
        elif mode == "nt":
            p = lax.dot_general(a_ref[...], b_ref[...], (((1,), (1,)), ((), ())), preferred_element_type=F32)
        else:
            p = lax.dot_general(a_ref[...], b_ref[...], (((0,), (0,)), ((), ())), preferred_element_type=F32)
        if nk == 1:
            if has_acc:
                p = p + c_ref[...]
            o_ref[...] = p.astype(out_dtype)
        else:
            acc_ref = rest[0]
            k = pl.program_id(2)

            @pl.when(k == 0)
            def _():
                acc_ref[...] = p + c_ref[...] if has_acc else p

            @pl.when(k > 0)
            def _():
                acc_ref[...] += p

            @pl.when(k == nk - 1)
            def _():
                o_ref[...] = acc_ref[...].astype(out_dtype)

    if mode == "tn":
        a_spec = pl.BlockSpec((tk, tm), lambda i, j, k: (k, i))
    else:
        a_spec = pl.BlockSpec((tm, tk), lambda i, j, k: (i, k))
    if mode == "nt":
        b_spec = pl.BlockSpec((tn, tk), lambda i, j, k: (j, k))
    else:
        b_spec = pl.BlockSpec((tk, tn), lambda i, j, k: (k, j))
    o_spec = pl.BlockSpec((tm, tn), lambda i, j, k: (i, j))
    in_specs = [a_spec, b_spec] + ([o_spec] if has_acc else [])
    args = (a, b) + ((acc_in,) if has_acc else ())
    return pl.pallas_call(
        body, name=name, grid=(m // tm, n // tn, nk),
        in_specs=in_specs, out_specs=o_spec,
        out_shape=jax.ShapeDtypeStruct((m, n), out_dtype),
        scratch_shapes=[pltpu.VMEM((tm, tn), F32)] if nk > 1 else [],
        compiler_params=_cparams("parallel", "parallel", "arbitrary"),
    )(*args)


def _rows(shape_cols, tr, dtype=None):
    return pl.BlockSpec((tr, shape_cols), lambda i: (i, 0))


def _const(shape):
    return pl.BlockSpec(shape, lambda i: (0,) * len(shape))


def _rms(x):
    return lax.rsqrt(jnp.mean(x * x, axis=-1, keepdims=True) + RMS_EPS)


def _rms_bwd(x, r, g, dy):
    xn = x * r
    dxn = dy * g
    dx = r * (dxn - xn * jnp.mean(dxn * xn, axis=-1, keepdims=True))
    return dx, dy * xn


def _gelu(y):
    return 0.5 * y * (1.0 + jnp.tanh(GELU_C * (y + GELU_A * y * y * y)))


def _gelu_grad(y):
    t = jnp.tanh(GELU_C * (y + GELU_A * y * y * y))
    return 0.5 * (1.0 + t) + 0.5 * y * (1.0 - t * t) * GELU_C * (1.0 + 3.0 * GELU_A * y * y)


def _sigmoid(z):
    return 1.0 / (1.0 + jnp.exp(-z))


def _norm_fwd(h, g, name, res=None):
    tp, d = h.shape
    tr = tp // ROW_TILES
    has_res = res is not None

    def body(*refs):
        if has_res:
            h_ref, r_ref, g_ref, s_ref, hn_ref = refs
            x = h_ref[...] + r_ref[...]
            s_ref[...] = x
        else:
            h_ref, g_ref, hn_ref = refs
            x = h_ref[...]
        hn_ref[...] = (x * _rms(x) * g_ref[...]).astype(BF16)

    in_specs = [_rows(d, tr)] + ([_rows(d, tr)] if has_res else []) + [_const((1, d))]
    out_specs = ([_rows(d, tr)] if has_res else []) + [_rows(d, tr)]
    out_shape = ([jax.ShapeDtypeStruct((tp, d), F32)] if has_res else []) + [jax.ShapeDtypeStruct((tp, d), BF16)]
    args = (h,) + ((res,) if has_res else ()) + (g,)
    out = pl.pallas_call(body, name=name, grid=(ROW_TILES,), in_specs=in_specs, out_specs=out_specs,
                         out_shape=out_shape, compiler_params=_cparams("parallel"))(*args)
    return out if has_res else out[0]


def _norm_bwd(h, g, dhn, dres, name):
    tp, d = h.shape
    tr = tp // ROW_TILES

    def body(h_ref, g_ref, dhn_ref, dres_ref, dh_ref, dhb_ref, dg_ref):
        x = h_ref[...]
        dx, dgs = _rms_bwd(x, _rms(x), g_ref[...], dhn_ref[...])
        dh = dres_ref[...] + dx
        dh_ref[...] = dh
        dhb_ref[...] = dh.astype(BF16)

        @pl.when(pl.program_id(0) == 0)
        def _():
            dg_ref[...] = jnp.zeros_like(dg_ref)

        dg_ref[...] += jnp.sum(dgs, axis=0, keepdims=True)

    return pl.pallas_call(
        body, name=name, grid=(ROW_TILES,),
        in_specs=[_rows(d, tr), _const((1, d)), _rows(d, tr), _rows(d, tr)],
        out_specs=[_rows(d, tr), _rows(d, tr), _const((1, d))],
        out_shape=[jax.ShapeDtypeStruct((tp, d), F32), jax.ShapeDtypeStruct((tp, d), BF16),
                   jax.ShapeDtypeStruct((1, d), F32)],
        compiler_params=_cparams("arbitrary"))(h, g, dhn, dres)


def _loss_bwd(h1, dn, tgt, g, n_real, name):
    tp, d = h1.shape
    tr = tp // ROW_TILES

    def body(h1_ref, dn_ref, t_ref, g_ref, loss_ref, dh_ref, dhb_ref, dg_ref):
        i = pl.program_id(0)
        x = h1_ref[...] + dn_ref[...]
        r = _rms(x)
        row = i * tr + lax.broadcasted_iota(jnp.int32, (tr, d), 0)
        valid = (row >= N_META) & (row < n_real)
        e = jnp.where(valid, x * r * g_ref[...] - t_ref[...], 0.0)
        dx, dgs = _rms_bwd(x, r, g_ref[...], e * (1.0 / d))
        dh_ref[...] = dx
        dhb_ref[...] = dx.astype(BF16)

        @pl.when(i == 0)
        def _():
            dg_ref[...] = jnp.zeros_like(dg_ref)
            loss_ref[...] = jnp.zeros_like(loss_ref)

        dg_ref[...] += jnp.sum(dgs, axis=0, keepdims=True)
        loss_ref[...] += (0.5 / d) * jnp.sum(jnp.sum(e * e, axis=0, keepdims=True), axis=1, keepdims=True)

    return pl.pallas_call(
        body, name=name, grid=(ROW_TILES,),
        in_specs=[_rows(d, tr), _rows(d, tr), _rows(d, tr), _const((1, d))],
        out_specs=[_const((1, LANES)), _rows(d, tr), _rows(d, tr), _const((1, d))],
        out_shape=[jax.ShapeDtypeStruct((1, LANES), F32), jax.ShapeDtypeStruct((tp, d), F32),
                   jax.ShapeDtypeStruct((tp, d), BF16), jax.ShapeDtypeStruct((1, d), F32)],
        compiler_params=_cparams("arbitrary"))(h1, dn, tgt, g)


def _mix_fwd(co, y, z, gc, gs, name):
    tp, dh = co.shape
    tr = tp // ROW_TILES

    def body(co_ref, y_ref, z_ref, gc_ref, gs_ref, m_ref):
        c = co_ref[...]
        m_ref[:, :dh] = (c * _rms(c) * gc_ref[...]).astype(BF16)
        so = _gelu(y_ref[...]) * _sigmoid(z_ref[...])
        m_ref[:, dh:] = (so * _rms(so) * gs_ref[...]).astype(BF16)

    return pl.pallas_call(
        body, name=name, grid=(ROW_TILES,),
        in_specs=[_rows(dh, tr)] * 3 + [_const((1, dh))] * 2,
        out_specs=_rows(2 * dh, tr),
        out_shape=jax.ShapeDtypeStruct((tp, 2 * dh), BF16),
        compiler_params=_cparams("parallel"))(co, y, z, gc, gs)


def _mix_bwd(dm, co, y, z, gc, gs, name):
    tp, dh = co.shape
    tr = tp // ROW_TILES

    def body(dm_ref, co_ref, y_ref, z_ref, gc_ref, gs_ref, dco_ref, dz_ref, dgp_ref, dgc_ref, dgs_ref):
        c = co_ref[...]
        dco, dgc = _rms_bwd(c, _rms(c), gc_ref[...], dm_ref[:, :dh])
        dco_ref[...] = dco
        gl = _gelu(y_ref[...])
        sg = _sigmoid(z_ref[...])
        so = gl * sg
        dso, dgs = _rms_bwd(so, _rms(so), gs_ref[...], dm_ref[:, dh:])
        dz_ref[...] = (dso * gl * sg * (1.0 - sg)).astype(BF16)
        dgp_ref[...] = dso * sg

        @pl.when(pl.program_id(0) == 0)
        def _():
            dgc_ref[...] = jnp.zeros_like(dgc_ref)
            dgs_ref[...] = jnp.zeros_like(dgs_ref)

        dgc_ref[...] += jnp.sum(dgc, axis=0, keepdims=True)
        dgs_ref[...] += jnp.sum(dgs, axis=0, keepdims=True)

    return pl.pallas_call(
        body, name=name, grid=(ROW_TILES,),
        in_specs=[_rows(2 * dh, tr)] + [_rows(dh, tr)] * 3 + [_const((1, dh))] * 2,
        out_specs=[_rows(dh, tr), _rows(dh, tr), _rows(dh, tr), _const((1, dh)), _const((1, dh))],
        out_shape=[jax.ShapeDtypeStruct((tp, dh), F32), jax.ShapeDtypeStruct((tp, dh), BF16),
                   jax.ShapeDtypeStruct((tp, dh), F32), jax.ShapeDtypeStruct((1, dh), F32),
                   jax.ShapeDtypeStruct((1, dh), F32)],
        compiler_params=_cparams("arbitrary"))(dm, co, y, z, gc, gs)


def _shift_down(x, k):
    row = lax.broadcasted_iota(jnp.int32, x.shape, 0)
    return jnp.where(row >= k, pltpu.roll(x, k, 0), 0.0)


def _shift_up(x, k):
    n = x.shape[0]
    row = lax.broadcasted_iota(jnp.int32, x.shape, 0)
    return jnp.where(row < n - k, pltpu.roll(x, n - k, 0), 0.0)


def _dwconv(x, w_ref):
    return w_ref[2:3, :] * x + w_ref[1:2, :] * _shift_down(x, 1) + w_ref[0:1, :] * _shift_down(x, 2)


def _dwconv_bwd(x, dy, w_ref):
    dx = w_ref[2:3, :] * dy + w_ref[1:2, :] * _shift_up(dy, 1) + w_ref[0:1, :] * _shift_up(dy, 2)
    dw = jnp.concatenate([jnp.sum(dy * _shift_down(x, 2), axis=0, keepdims=True),
                          jnp.sum(dy * _shift_down(x, 1), axis=0, keepdims=True),
                          jnp.sum(dy * x, axis=0, keepdims=True)], axis=0)
    return dx, dw


def _scan(s_re, s_im, tab_ref, reverse):
    n_chunks = s_re.shape[0] // SUBLANES
    width = s_re.shape[1]
    for st in range(width // LANES):
        lanes = slice(st * LANES, (st + 1) * LANES)
        a1r, a1i, a2r, a2i, a4r, a4i, pwr, pwi = [tab_ref[k, :, lanes] for k in range(8)]

        def body(i, carry, lanes=lanes, a1r=a1r, a1i=a1i, a2r=a2r, a2i=a2i, a4r=a4r, a4i=a4i, pwr=pwr, pwi=pwi):
            cr, ci = carry
            chunk = (n_chunks - 1 - i) if reverse else i
            r0 = pl.multiple_of(chunk * SUBLANES, SUBLANES)
            xr = s_re[pl.ds(r0, SUBLANES), lanes]
            xi = s_im[pl.ds(r0, SUBLANES), lanes]
            for mr, mi, k in ((a1r, a1i, 1), (a2r, a2i, 2), (a4r, a4i, 4)):
                sh = SUBLANES - k if reverse else k
                rr = pltpu.roll(xr, sh, 0)
                ri = pltpu.roll(xi, sh, 0)
                xr, xi = xr + (mr * rr - mi * ri), xi + (mr * ri + mi * rr)
            xr, xi = xr + (pwr * cr - pwi * ci), xi + (pwr * ci + pwi * cr)
            s_re[pl.ds(r0, SUBLANES), lanes] = xr
            s_im[pl.ds(r0, SUBLANES), lanes] = xi
            last = 0 if reverse else SUBLANES - 1
            return (jnp.broadcast_to(xr[last:last + 1, :], (SUBLANES, LANES)),
                    jnp.broadcast_to(xi[last:last + 1, :], (SUBLANES, LANES)))

        zero = jnp.zeros((SUBLANES, LANES), F32)
        lax.fori_loop(0, n_chunks, body, (zero, zero))


def _seq_fwd(proj, conv_w, bc_re, bc_im, cc_re, cc_im, dskip, tab_f, name):
    tp = proj.shape[0]
    dh = proj.shape[1] // 4
    nq = dh // LANES
    sw = STATE * N_GROUPS // nq

    def body(b_ref, c_ref, v_ref, u_ref, w_ref, bre_ref, bim_ref, cre_ref, cim_ref, d_ref, tab_ref,
             co_ref, y_ref, g_ref, s_re, s_im):
        co_ref[...] = b_ref[...] * _dwconv(c_ref[...] * v_ref[...], w_ref)
        u = u_ref[...]
        ub = u.astype(BF16)
        s_re[...] = jnp.dot(ub, bre_ref[...], preferred_element_type=F32)
        s_im[...] = jnp.dot(ub, bim_ref[...], preferred_element_type=F32)
        _scan(s_re, s_im, tab_ref, False)
        y = (jnp.dot(s_re[...].astype(BF16), cre_ref[...], preferred_element_type=F32)
             - jnp.dot(s_im[...].astype(BF16), cim_ref[...], preferred_element_type=F32)
             + d_ref[...] * u)
        y_ref[...] = y
        g_ref[...] = _gelu(y).astype(BF16)

    col = lambda off: pl.BlockSpec((tp, LANES), lambda q, off=off: (0, off * nq + q))
    blk = pl.BlockSpec((tp, LANES), lambda q: (0, q))
    return pl.pallas_call(
        body, name=name, grid=(nq,),
        in_specs=[col(0), col(1), col(2), col(3),
                  pl.BlockSpec((3, LANES), lambda q: (0, q)),
                  pl.BlockSpec((LANES, sw), lambda q: (0, q)), pl.BlockSpec((LANES, sw), lambda q: (0, q)),
                  pl.BlockSpec((sw, LANES), lambda q: (q, 0)), pl.BlockSpec((sw, LANES), lambda q: (q, 0)),
                  pl.BlockSpec((1, LANES), lambda q: (0, q)),
                  pl.BlockSpec((8, SUBLANES, sw), lambda q: (0, 0, q))],
        out_specs=[blk, blk, blk],
        out_shape=[jax.ShapeDtypeStruct((tp, dh), F32), jax.ShapeDtypeStruct((tp, dh), F32),
                   jax.ShapeDtypeStruct((tp, dh), BF16)],
        scratch_shapes=[pltpu.VMEM((tp, sw), F32), pltpu.VMEM((tp, sw), F32)],
        compiler_params=_cparams("parallel"),
    )(proj, proj, proj, proj, conv_w, bc_re, bc_im, cc_re, cc_im, dskip, tab_f)


def _conv_bwd(proj, dco, conv_w, name):
    tp = proj.shape[0]
    dh = proj.shape[1] // 4
    nq = dh // LANES

    def body(b_ref, c_ref, v_ref, dco_ref, w_ref, dproj_ref, dw_ref, stage, sem):
        q = pl.program_id(0)
        cg = c_ref[...]
        vg = v_ref[...]
        cv = cg * vg
        dco_v = dco_ref[...]
        dcv, dw = _dwconv_bwd(cv, dco_v * b_ref[...], w_ref)
        dw_ref[...] = dw
        stage[0] = (dco_v * _dwconv(cv, w_ref)).astype(BF16)
        stage[1] = (dcv * vg).astype(BF16)
        stage[2] = (dcv * cg).astype(BF16)
        copies = [pltpu.make_async_copy(stage.at[p], dproj_ref.at[:, pl.ds((p * nq + q) * LANES, LANES)], sem.at[p])
                  for p in range(3)]
        for cp in copies:
            cp.start()
        for cp in copies:
            cp.wait()

    col = lambda off: pl.BlockSpec((tp, LANES), lambda q, off=off: (0, off * nq + q))
    return pl.pallas_call(
        body, name=name, grid=(nq,),
        in_specs=[col(0), col(1), col(2), pl.BlockSpec((tp, LANES), lambda q: (0, q)),
                  pl.BlockSpec((3, LANES), lambda q: (0, q))],
        out_specs=[pl.BlockSpec(memory_space=pl.ANY), pl.BlockSpec((3, LANES), lambda q: (0, q))],
        out_shape=[jax.ShapeDtypeStruct((tp, 4 * dh), BF16), jax.ShapeDtypeStruct((3, dh), F32)],
        scratch_shapes=[pltpu.VMEM((3, tp, LANES), BF16), pltpu.SemaphoreType.DMA((3,))],
        compiler_params=_cparams("arbitrary"),
    )(proj, proj, proj, dco, conv_w)


def _ssm_bwd(proj, y, dg, dproj, bc_re, bc_im, cc_re, cc_im, dskip, tab_f, tab_r, name):
    tp = proj.shape[0]
    dh = proj.shape[1] // 4
    nq = dh // LANES
    sw = STATE * N_GROUPS // nq

    def body(u_ref, y_ref, dg_ref, dproj_in, bre_ref, bim_ref, cre_ref, cim_ref, d_ref, tabf_ref, tabr_ref,
             dproj_ref, dbre_ref, dbim_ref, dcre_ref, dcim_ref, dd_ref, dar_ref, dai_ref,
             s_re, s_im, l_re, l_im, stage, sem):
        del dproj_in
        q = pl.program_id(0)
        nt = (((1,), (1,)), ((), ()))
        tn = (((0,), (0,)), ((), ()))
        u = u_ref[...]
        ub = u.astype(BF16)
        s_re[...] = jnp.dot(ub, bre_ref[...], preferred_element_type=F32)
        s_im[...] = jnp.dot(ub, bim_ref[...], preferred_element_type=F32)
        _scan(s_re, s_im, tabf_ref, False)
        dy = dg_ref[...] * _gelu_grad(y_ref[...])
        dyb = dy.astype(BF16)
        dd_ref[...] = jnp.sum(dy * u, axis=0, keepdims=True)
        l_re[...] = lax.dot_general(dyb, cre_ref[...], nt, preferred_element_type=F32)
        l_im[...] = -lax.dot_general(dyb, cim_ref[...], nt, preferred_element_type=F32)
        dcre_ref[...] = lax.dot_general(s_re[...].astype(BF16), dyb, tn, preferred_element_type=F32)
        dcim_ref[...] = -lax.dot_general(s_im[...].astype(BF16), dyb, tn, preferred_element_type=F32)
        _scan(l_re, l_im, tabr_ref, True)
        for st in range(sw // LANES):
            lanes = slice(st * LANES, (st + 1) * LANES)
            lr = l_re[:, lanes]
            li = l_im[:, lanes]
            pr = _shift_down(s_re[:, lanes], 1)
            pi = _shift_down(s_im[:, lanes], 1)
            dar_ref[:, lanes] = jnp.sum(lr * pr + li * pi, axis=0, keepdims=True)
            dai_ref[:, lanes] = jnp.sum(li * pr - lr * pi, axis=0, keepdims=True)
        lrb = l_re[...].astype(BF16)
        lib = l_im[...].astype(BF16)
        du = (dy * d_ref[...] + lax.dot_general(lrb, bre_ref[...], nt, preferred_element_type=F32)
              + lax.dot_general(lib, bim_ref[...], nt, preferred_element_type=F32))
        stage[...] = du.astype(BF16)
        dbre_ref[...] = lax.dot_general(ub, lrb, tn, preferred_element_type=F32)
        dbim_ref[...] = lax.dot_general(ub, lib, tn, preferred_element_type=F32)
        cp = pltpu.make_async_copy(stage, dproj_ref.at[:, pl.ds((3 * nq + q) * LANES, LANES)], sem)
        cp.start()
        cp.wait()

    blk = pl.BlockSpec((tp, LANES), lambda q: (0, q))
    bspec = pl.BlockSpec((LANES, sw), lambda q: (0, q))
    cspec = pl.BlockSpec((sw, LANES), lambda q: (q, 0))
    tspec = pl.BlockSpec((8, SUBLANES, sw), lambda q: (0, 0, q))
    nstate = STATE * N_GROUPS
    return pl.pallas_call(
        body, name=name, grid=(nq,),
        in_specs=[pl.BlockSpec((tp, LANES), lambda q: (0, 3 * nq + q)), blk, blk, pl.BlockSpec(memory_space=pl.ANY),
                  bspec, bspec, cspec, cspec, pl.BlockSpec((1, LANES), lambda q: (0, q)), tspec, tspec],
        out_specs=[pl.BlockSpec(memory_space=pl.ANY), bspec, bspec, cspec, cspec,
                   pl.BlockSpec((1, LANES), lambda q: (0, q)),
                   pl.BlockSpec((1, sw), lambda q: (0, q)), pl.BlockSpec((1, sw), lambda q: (0, q))],
        out_shape=[jax.ShapeDtypeStruct((tp, 4 * dh), BF16),
                   jax.ShapeDtypeStruct((LANES, nstate), F32), jax.ShapeDtypeStruct((LANES, nstate), F32),
                   jax.ShapeDtypeStruct((nstate, LANES), F32), jax.ShapeDtypeStruct((nstate, LANES), F32),
                   jax.ShapeDtypeStruct((1, dh), F32),
                   jax.ShapeDtypeStruct((1, nstate), F32), jax.ShapeDtypeStruct((1, nstate), F32)],
        input_output_aliases={3: 0},
        scratch_shapes=[pltpu.VMEM((tp, sw), F32)] * 4 + [pltpu.VMEM((tp, LANES), BF16), pltpu.SemaphoreType.DMA],
        compiler_params=_cparams("arbitrary"),
    )(proj, y, dg, dproj, bc_re, bc_im, cc_re, cc_im, dskip, tab_f, tab_r)


FFN_TILE = 256


def _ffn_act(up, fw, fb, name):
    tp, two_ff = up.shape
    dff = two_ff // 2
    tc = FFN_TILE
    nj = dff // tc

    def body(ua_ref, uv_ref, wa_ref, wv_ref, ba_ref, bv_ref, act_ref):
        a = _dwconv(ua_ref[...], wa_ref) + ba_ref[...]
        v = _dwconv(uv_ref[...], wv_ref) + bv_ref[...]
        act_ref[...] = (a * _sigmoid(a) * v).astype(BF16)

    lo = lambda r: pl.BlockSpec((r, tc), lambda j: (0, j))
    hi = lambda r: pl.BlockSpec((r, tc), lambda j: (0, nj + j))
    return pl.pallas_call(
        body, name=name, grid=(nj,),
        in_specs=[lo(tp), hi(tp), lo(3), hi(3), lo(1), hi(1)],
        out_specs=lo(tp),
        out_shape=jax.ShapeDtypeStruct((tp, dff), BF16),
        compiler_params=_cparams("parallel"))(up, up, fw, fw, fb, fb)


def _ffn_bwd(up, dact, fw, fb, name):
    tp, two_ff = up.shape
    dff = two_ff // 2
    tc = FFN_TILE
    nj = dff // tc

    def body(ua_ref, uv_ref, da_ref, wa_ref, wv_ref, ba_ref, bv_ref,
             dup_ref, dwa_ref, dwv_ref, dba_ref, dbv_ref, stage, sem):
        j = pl.program_id(0)
        ua = ua_ref[...]
        uv = uv_ref[...]
        a = _dwconv(ua, wa_ref) + ba_ref[...]
        v = _dwconv(uv, wv_ref) + bv_ref[...]
        sg = _sigmoid(a)
        dact_v = da_ref[...]
        da = dact_v * v * sg * (1.0 + a * (1.0 - sg))
        dv = dact_v * a * sg
        dba_ref[...] = jnp.sum(da, axis=0, keepdims=True)
        dbv_ref[...] = jnp.sum(dv, axis=0, keepdims=True)
        dua, dwa = _dwconv_bwd(ua, da, wa_ref)
        duv, dwv = _dwconv_bwd(uv, dv, wv_ref)
        dwa_ref[...] = dwa
        dwv_ref[...] = dwv
        stage[0] = dua.astype(BF16)
        stage[1] = duv.astype(BF16)
        copies = [pltpu.make_async_copy(stage.at[p], dup_ref.at[:, pl.ds((p * nj + j) * tc, tc)], sem.at[p])
                  for p in range(2)]
        for cp in copies:
            cp.start()
        for cp in copies:
            cp.wait()

    lo = lambda r: pl.BlockSpec((r, tc), lambda j: (0, j))
    hi = lambda r: pl.BlockSpec((r, tc), lambda j: (0, nj + j))
    return pl.pallas_call(
        body, name=name, grid=(nj,),
        in_specs=[lo(tp), hi(tp), lo(tp), lo(3), hi(3), lo(1), hi(1)],
        out_specs=[pl.BlockSpec(memory_space=pl.ANY), lo(3), lo(3), lo(1), lo(1)],
        out_shape=[jax.ShapeDtypeStruct((tp, two_ff), BF16),
                   jax.ShapeDtypeStruct((3, dff), F32), jax.ShapeDtypeStruct((3, dff), F32),
                   jax.ShapeDtypeStruct((1, dff), F32), jax.ShapeDtypeStruct((1, dff), F32)],
        scratch_shapes=[pltpu.VMEM((2, tp, tc), BF16), pltpu.SemaphoreType.DMA((2,))],
        compiler_params=_cparams("arbitrary"))(up, up, dact, fw, fw, fb, fb)


def _s5_discretize(lam_re, lam_im, log_dt, b_re, b_im):
    dt = jnp.exp(log_dt)[:, None]
    mag = jnp.exp(lam_re * dt)
    ang = lam_im * dt
    a_re = mag * jnp.cos(ang)
    a_im = mag * jnp.sin(ang)
    den = lam_re * lam_re + lam_im * lam_im
    nr = a_re - 1.0
    f_re = (nr * lam_re + a_im * lam_im) / den
    f_im = (a_im * lam_re - nr * lam_im) / den
    bb_re = f_re[..., None] * b_re - f_im[..., None] * b_im
    bb_im = f_re[..., None] * b_im + f_im[..., None] * b_re
    return a_re, a_im, bb_re, bb_im


def _scan_tables(a_re, a_im, reverse):
    ar = a_re.reshape(-1)
    ai = -a_im.reshape(-1) if reverse else a_im.reshape(-1)
    pows = [(ar, ai)]
    for _ in range(SUBLANES - 1):
        pr, pi = pows[-1]
        pows.append((pr * ar - pi * ai, pr * ai + pi * ar))
    row = jnp.arange(SUBLANES)[:, None]
    out = []
    for k in (1, 2, 4):
        mask = (row <= SUBLANES - 1 - k) if reverse else (row >= k)
        out.append(jnp.where(mask, pows[k - 1][0][None, :], 0.0))
        out.append(jnp.where(mask, pows[k - 1][1][None, :], 0.0))
    order = list(range(SUBLANES - 1, -1, -1)) if reverse else list(range(SUBLANES))
    out.append(jnp.stack([pows[k][0] for k in order]))
    out.append(jnp.stack([pows[k][1] for k in order]))
    return jnp.stack(out).astype(F32)


def _compact_b(bb):
    bq = bb.reshape(N_GROUPS // 8, 8, STATE, GROUP)
    m = jnp.einsum("ab,qbph->qahbp", jnp.eye(8, dtype=bb.dtype), bq).reshape(N_GROUPS // 8, LANES, 8 * STATE)
    return m.transpose(1, 0, 2).reshape(LANES, N_GROUPS * STATE)


def _expand_b(m):
    d = m.reshape(8, GROUP, N_GROUPS // 8, 8, STATE)
    return jnp.einsum("ahqap->qaph", d).reshape(N_GROUPS, STATE, GROUP)


def _compact_c(c):
    cq = c.reshape(N_GROUPS // 8, 8, GROUP, STATE)
    return jnp.einsum("ab,qbhp->qbpah", jnp.eye(8, dtype=c.dtype), cq).reshape(N_GROUPS * STATE, LANES)


def _expand_c(m):
    d = m.reshape(N_GROUPS // 8, 8, STATE, 8, GROUP)
    return jnp.einsum("qbpbh->qbhp", d).reshape(N_GROUPS, GROUP, STATE)


def _local_step(x, target, p):
    seq, d = x.shape
    n_real = N_META + seq
    tp = -(-n_real // ROW_ALIGN) * ROW_ALIGN
    pad = jnp.zeros((tp - n_real, d), F32)
    h0 = jnp.concatenate([p["meta_tokens"], x, pad], axis=0)
    tgt = jnp.concatenate([jnp.zeros((N_META, d), F32), target, pad], axis=0)

    s5 = (p["ssm_lam_re"], p["ssm_lam_im"], p["ssm_log_dt"], p["ssm_b_re"], p["ssm_b_im"])
    (a_re, a_im, bb_re, bb_im), s5_vjp = jax.vjp(_s5_discretize, *s5)
    tab_f = _scan_tables(a_re, a_im, False)
    tab_r = _scan_tables(a_re, a_im, True)
    bc_re = _compact_b(bb_re).astype(BF16)
    bc_im = _compact_b(bb_im).astype(BF16)
    cc_re = _compact_c(p["ssm_c_re"]).astype(BF16)
    cc_im = _compact_c(p["ssm_c_im"]).astype(BF16)
    dskip = p["ssm_d"].reshape(1, -1)
    dh = dskip.shape[1]

    hn1 = _norm_fwd(h0, p["norm_mix_g"], "norm_mix")
    proj = _mm(hn1, p["w_in"], "nn", "proj")
    co, y, g = _seq_fwd(proj, p["conv_w"], bc_re, bc_im, cc_re, cc_im, dskip, tab_f, "seq_fwd")
    z = _mm(g, p["ssm_w_glu"], "nn", "glu")
    mixed = _mix_fwd(co, y, z, p["gain_conv_out"], p["gain_ssm_out"], "mix_fwd")
    mo = _mm(mixed, p["w_out"], "nn", "out_proj")
    h1, hn2 = _norm_fwd(h0, p["norm_ffn_g"], "norm_ffn", res=mo)
    up = _mm(hn2, p["w_up"], "nn", "up_proj")
    act = _ffn_act(up, p["ffn_conv_w"], p["ffn_conv_b"], "ffn_act")
    dn = _mm(act, p["w_down"], "nn", "down_proj")
    loss, dh2, dh2b, d_gfin = _loss_bwd(h1, dn, tgt, p["norm_final_g"], n_real, "loss_bwd")

    g_w_down = _mm(act, dh2b, "tn", "g_w_down")
    dact = _mm(dh2b, p["w_down"], "nt", "d_act")
    dup, dfw_a, dfw_v, dfb_a, dfb_v = _ffn_bwd(up, dact, p["ffn_conv_w"], p["ffn_conv_b"], "ffn_bwd")
    g_w_up = _mm(hn2, dup, "tn", "g_w_up")
    dhn2 = _mm(dup, p["w_up"], "nt", "d_hn2")
    dh1, dh1b, d_gffn = _norm_bwd(h1, p["norm_ffn_g"], dhn2, dh2, "norm_ffn_bwd")
    g_w_out = _mm(mixed, dh1b, "tn", "g_w_out")
    dmixed = _mm(dh1b, p["w_out"], "nt", "d_mixed")
    dco, dz, dgp, d_gc, d_gs = _mix_bwd(dmixed, co, y, z, p["gain_conv_out"], p["gain_ssm_out"], "mix_bwd")
    g_w_glu = _mm(g, dz, "tn", "g_w_glu")
    dg = _mm(dz, p["ssm_w_glu"], "nt", "d_gelu", acc_in=dgp)
    dproj, d_conv_w = _conv_bwd(proj, dco, p["conv_w"], "conv_bwd")
    (dproj, dbc_re, dbc_im, dcc_re, dcc_im, d_dskip, da_re, da_im) = _ssm_bwd(
        proj, y, dg, dproj, bc_re, bc_im, cc_re, cc_im, dskip, tab_f, tab_r, "ssm_bwd")
    g_w_in = _mm(hn1, dproj, "tn", "g_w_in")
    dhn1 = _mm(dproj, p["w_in"], "nt", "d_hn1")
    dh0, _, d_gmix = _norm_bwd(h0, p["norm_mix_g"], dhn1, dh1, "norm_mix_bwd")

    d_lam_re, d_lam_im, d_log_dt, d_b_re, d_b_im = s5_vjp(
        (da_re.reshape(N_GROUPS, STATE), da_im.reshape(N_GROUPS, STATE), _expand_b(dbc_re), _expand_b(dbc_im)))
    grads = {
        "meta_tokens": dh0[:N_META], "norm_mix_g": d_gmix, "w_in": g_w_in, "conv_w": d_conv_w,
        "ssm_lam_re": d_lam_re, "ssm_lam_im": d_lam_im, "ssm_log_dt": d_log_dt,
        "ssm_b_re": d_b_re, "ssm_b_im": d_b_im, "ssm_c_re": _expand_c(dcc_re), "ssm_c_im": _expand_c(dcc_im),
        "ssm_d": d_dskip.reshape(N_GROUPS, GROUP), "ssm_w_glu": g_w_glu,
        "gain_conv_out": d_gc, "gain_ssm_out": d_gs, "w_out": g_w_out, "norm_ffn_g": d_gffn,
        "w_up": g_w_up, "ffn_conv_w": jnp.concatenate([dfw_a, dfw_v], axis=1),
        "ffn_conv_b": jnp.concatenate([dfb_a, dfb_v], axis=1), "w_down": g_w_down, "norm_final_g": d_gfin,
    }
    return loss[0, 0], dh0[N_META:n_real], grads


def _view(ref, axis, start, size):
    idx = [slice(None)] * len(ref.shape)
    idx[axis] = pl.ds(start, size)
    return ref.at[tuple(idx)]


def _exchange(name, ins, outs, aliases, local_copies, remote_copies):
    ni, no = len(ins), len(outs)
    nl, nr = len(local_copies), len(remote_copies)

    def body(*refs):
        in_refs, out_refs = refs[:ni], refs[ni:ni + no]
        send_sems, recv_sems, local_sems = refs[ni + no:]
        x, y, c = lax.axis_index("x"), lax.axis_index("y"), lax.axis_index("c")
        pos = (x, y, c, 2 * x + y)
        locals_ = [pltpu.make_async_copy(s(in_refs, out_refs, pos), d(in_refs, out_refs, pos), local_sems.at[i])
                   for i, (s, d) in enumerate(local_copies)]
        remotes = []
        for i, (s, d, flip) in enumerate(remote_copies):
            peer = (1 - x if "x" in flip else x, 1 - y if "y" in flip else y, 1 - c if "c" in flip else c)
            remotes.append(pltpu.make_async_remote_copy(
                src_ref=s(in_refs, out_refs, pos), dst_ref=d(in_refs, out_refs, pos),
                send_sem=send_sems.at[i], recv_sem=recv_sems.at[i], device_id=peer, device_id_type=MESH))
        for cp in locals_ + remotes:
            cp.start()
        for cp in remotes:
            cp.wait_recv()
        for cp in remotes:
            cp.wait_send()
        for cp in locals_:
            cp.wait()

    hbm = pl.BlockSpec(memory_space=pl.ANY)
    return pl.pallas_call(
        body, name=name, in_specs=[hbm] * ni, out_specs=[hbm] * no, out_shape=outs,
        input_output_aliases=aliases,
        scratch_shapes=[pltpu.SemaphoreType.DMA((nr,)), pltpu.SemaphoreType.DMA((nr,)),
                        pltpu.SemaphoreType.DMA((max(nl, 1),))],
    )(*ins)


BIG = {"w_in": (0, 1), "ssm_w_glu": (1, 0), "w_out": (1, 0), "w_up": (0, 1), "w_down": (1, 0)}
BIG_NAMES = tuple(BIG)
FLIPS = ("y", "x", "xy")


def _peer_chip(pos, flip):
    x, y, _, _ = pos
    return 2 * (1 - x if "x" in flip else x) + (1 - y if "y" in flip else y)


def _block_rows(rows, cols, itemsize, mult):
    return _pick_tile(rows, max(mult, (2 * 1024 * 1024) // (cols * itemsize)), mult)


def _cast_bf16(w, name):
    r, cdim = w.shape
    tr = _block_rows(r, cdim, 4, 16)

    def body(w_ref, o_ref):
        o_ref[...] = w_ref[...].astype(BF16)

    return pl.pallas_call(body, name=name, grid=(r // tr,), in_specs=[_rows(cdim, tr)], out_specs=_rows(cdim, tr),
                          out_shape=jax.ShapeDtypeStruct((r, cdim), BF16), compiler_params=_cparams("parallel"))(w)


def _pair_sum(g, recv, kc, half_axis, name, out_dtype):
    hr, hc = recv.shape
    tr = _block_rows(hr, hc, 4, 16)
    nb = hr // tr

    def body(kc_ref, g_ref, r_ref, o_ref):
        o_ref[...] = (g_ref[...] + r_ref[...]).astype(out_dtype)

    if half_axis == 0:
        g_spec = pl.BlockSpec((tr, hc), lambda i, kc: (kc[1] * nb + i, 0))
    elif half_axis == 1:
        g_spec = pl.BlockSpec((tr, hc), lambda i, kc: (i, kc[1]))
    else:
        g_spec = pl.BlockSpec((tr, hc), lambda i, kc: (i, 0))
    same = pl.BlockSpec((tr, hc), lambda i, kc: (i, 0))
    return pl.pallas_call(
        body, name=name,
        grid_spec=pltpu.PrefetchScalarGridSpec(num_scalar_prefetch=1, grid=(nb,), in_specs=[g_spec, same],
                                               out_specs=same),
        out_shape=jax.ShapeDtypeStruct((hr, hc), out_dtype), compiler_params=_cparams("parallel"))(kc, g, recv)


def _chip_sum(own, recv, kc, own_axis, name):
    _, sr, sc = recv.shape
    tr = _block_rows(sr, sc, 4, 16)
    nb = sr // tr

    def body(kc_ref, o_ref, r_ref, t_ref):
        k = kc_ref[0]
        own_v = o_ref[...].astype(F32)
        r = [r_ref[m].astype(F32) for m in range(3)]
        terms = []
        for kk in range(4):
            m = jnp.bitwise_xor(k, kk)
            terms.append(jnp.where(m == 0, own_v, jnp.where(m == 1, r[0], jnp.where(m == 2, r[1], r[2]))))
        t_ref[...] = (terms[0] + terms[1]) + (terms[2] + terms[3])

    if own_axis == 0:
        own_spec = pl.BlockSpec((tr, sc), lambda i, kc: (kc[0] * nb + i, 0))
    elif own_axis == 1:
        own_spec = pl.BlockSpec((tr, sc), lambda i, kc: (i, kc[0]))
    else:
        own_spec = pl.BlockSpec((tr, sc), lambda i, kc: (kc[1] * nb + i, 0))
    return pl.pallas_call(
        body, name=name,
        grid_spec=pltpu.PrefetchScalarGridSpec(
            num_scalar_prefetch=1, grid=(nb,),
            in_specs=[own_spec, pl.BlockSpec((3, tr, sc), lambda i, kc: (0, i, 0))],
            out_specs=pl.BlockSpec((tr, sc), lambda i, kc: (i, 0))),
        out_shape=jax.ShapeDtypeStruct((sr, sc), F32), compiler_params=_cparams("parallel"))(kc, own, recv)


def _adamw(w, g, m, v, name):
    r, cdim = w.shape
    tr = _block_rows(r, cdim, 4, 8)
    c1 = 1.0 - ADAM_B1 ** ADAM_STEP
    c2 = 1.0 - ADAM_B2 ** ADAM_STEP

    def body(w_ref, g_ref, m_ref, v_ref, d_ref, nm_ref, nv_ref):
        gv = g_ref[...]
        nm = ADAM_B1 * m_ref[...] + (1.0 - ADAM_B1) * gv
        nv = ADAM_B2 * v_ref[...] + (1.0 - ADAM_B2) * (gv * gv)
        d_ref[...] = -ADAM_LR * ((nm / c1) / (jnp.sqrt(nv / c2) + ADAM_EPS) + ADAM_WD * w_ref[...])
        nm_ref[...] = nm
        nv_ref[...] = nv

    spec = _rows(cdim, tr)
    return pl.pallas_call(body, name=name, grid=(r // tr,), in_specs=[spec] * 4, out_specs=[spec] * 3,
                          out_shape=[jax.ShapeDtypeStruct((r, cdim), F32)] * 3,
                          compiler_params=_cparams("parallel"))(w, g, m, v)


def _all_gather_weights(shards, tiny):
    wb = [_cast_bf16(shards[n], "cast_" + n) for n in BIG_NAMES]
    full_shapes = []
    for n in BIG_NAMES:
        r, cdim = shards[n].shape
        full_shapes.append((r, 4 * cdim) if BIG[n][1] == 1 else (4 * r, cdim))
    nbig = len(BIG_NAMES)

    def region(i, chip, c=None):
        half_axis, shard_axis = BIG[BIG_NAMES[i]]
        sr, sc = shards[BIG_NAMES[i]].shape
        ssize = (sr, sc)[shard_axis]
        hsize = (sr, sc)[half_axis] // 2

        def f(ref):
            v = _view(ref, shard_axis, chip * ssize, ssize)
            return v if c is None else _view(v, half_axis, c * hsize, hsize)
        return f

    def shard_half(i, c):
        half_axis = BIG[BIG_NAMES[i]][0]
        hsize = shards[BIG_NAMES[i]].shape[half_axis] // 2
        return lambda ref: _view(ref, half_axis, c * hsize, hsize)

    local, remote = [], []
    for i in range(nbig):
        local.append((lambda I, O, pos, i=i: I[i], lambda I, O, pos, i=i: region(i, pos[3])(O[i])))
        for flip in FLIPS:
            remote.append((lambda I, O, pos, i=i: shard_half(i, pos[2])(I[i]),
                           lambda I, O, pos, i=i: region(i, pos[3], pos[2])(O[i]), flip))
    local.append((lambda I, O, pos: I[nbig], lambda I, O, pos: O[nbig].at[pos[3]]))
    for flip in FLIPS:
        remote.append((lambda I, O, pos: I[nbig], lambda I, O, pos: O[nbig].at[pos[3]], flip))
    outs = [jax.ShapeDtypeStruct(s, BF16) for s in full_shapes] + [jax.ShapeDtypeStruct((4,) + tiny.shape, F32)]
    gathered = _exchange("gather_ici", wb + [tiny], outs, {}, local, remote)

    remote = []
    for i in range(nbig):
        for flip in FLIPS:
            reg = lambda I, O, pos, i=i, flip=flip, src=True: region(i, _peer_chip(pos, flip), pos[2])(I[i] if src else O[i])
            remote.append((functools.partial(reg, src=True), functools.partial(reg, src=False), "c"))
    full = _exchange("gather_d2d", list(gathered[:nbig]), outs[:nbig], {i: i for i in range(nbig)}, [], remote)
    return dict(zip(BIG_NAMES, full)), gathered[nbig]


def _reduce_gradients(grads_big, pack, kc):
    nbig = len(BIG_NAMES)
    gs = [grads_big[n] for n in BIG_NAMES]

    def half_shape(n):
        r, cdim = grads_big[n].shape
        return (r // 2, cdim) if BIG[n][0] == 0 else (r, cdim // 2)

    def sub_shape(n):
        hr, hc = half_shape(n)
        return (hr, hc // 4) if BIG[n][1] == 1 else (hr // 4, hc)

    def half_of(i, c_of):
        half_axis = BIG[BIG_NAMES[i]][0]
        hsize = gs[i].shape[half_axis] // 2
        return lambda ref, pos: _view(ref, half_axis, c_of(pos) * hsize, hsize)

    remote = [(lambda I, O, pos, i=i: half_of(i, lambda p: 1 - p[2])(I[i], pos), lambda I, O, pos, i=i: O[i], "c")
              for i in range(nbig)]
    remote.append((lambda I, O, pos: I[nbig], lambda I, O, pos: O[nbig], "c"))
    outs = [jax.ShapeDtypeStruct(half_shape(n), F32) for n in BIG_NAMES] + [jax.ShapeDtypeStruct(pack.shape, F32)]
    recv_a = _exchange("reduce_d2d", gs + [pack], outs, {}, [], remote)
    chip = [_pair_sum(gs[i], recv_a[i], kc, BIG[n][0], "pair_sum_" + n, BF16) for i, n in enumerate(BIG_NAMES)]
    chip_pack = _pair_sum(pack, recv_a[nbig], kc, None, "pair_sum_pack", F32)

    prow = pack.shape[0] // 2

    def sub_of(i, chip_of):
        shard_axis = BIG[BIG_NAMES[i]][1]
        ssize = sub_shape(BIG_NAMES[i])[shard_axis]
        return lambda ref, pos: _view(ref, shard_axis, chip_of(pos) * ssize, ssize)

    remote = []
    for i in range(nbig):
        for slot, flip in enumerate(FLIPS):
            remote.append((lambda I, O, pos, i=i, flip=flip: sub_of(i, lambda p: _peer_chip(p, flip))(I[i], pos),
                           lambda I, O, pos, i=i, slot=slot: O[i].at[slot], flip))
    for slot, flip in enumerate(FLIPS):
        remote.append((lambda I, O, pos: _view(I[nbig], 0, pos[2] * prow, prow),
                       lambda I, O, pos, slot=slot: O[nbig].at[slot], flip))
    outs = ([jax.ShapeDtypeStruct((3,) + sub_shape(n), BF16) for n in BIG_NAMES]
            + [jax.ShapeDtypeStruct((3, prow, pack.shape[1]), F32)])
    recv_b = _exchange("reduce_ici", chip + [chip_pack], outs, {}, [], remote)
    total = [_chip_sum(chip[i], recv_b[i], kc, BIG[n][1], "chip_sum_" + n) for i, n in enumerate(BIG_NAMES)]
    total_pack = _chip_sum(chip_pack, recv_b[nbig], kc, None, "chip_sum_pack")

    def shard_half(i, ref, pos):
        half_axis = BIG[BIG_NAMES[i]][0]
        hsize = sub_shape(BIG_NAMES[i])[half_axis]
        return _view(ref, half_axis, pos[2] * hsize, hsize)

    local, remote = [], []
    for i in range(nbig + 1):
        src = lambda I, O, pos, i=i: I[i]
        if i < nbig:
            dst = lambda I, O, pos, i=i: shard_half(i, O[i], pos)
        else:
            dst = lambda I, O, pos: _view(O[nbig], 0, pos[2] * prow, prow)
        local.append((src, dst))
        remote.append((src, dst, "c"))
    shard_shapes = []
    for n in BIG_NAMES:
        sr, sc = sub_shape(n)
        shard_shapes.append((2 * sr, sc) if BIG[n][0] == 0 else (sr, 2 * sc))
    outs = [jax.ShapeDtypeStruct(s, F32) for s in shard_shapes] + [jax.ShapeDtypeStruct(pack.shape, F32)]
    out = _exchange("swap_d2d", total + [total_pack], outs, {}, local, remote)
    return dict(zip(BIG_NAMES, out[:nbig])), out[nbig]


WEIGHTS = ("meta_tokens", "norm_mix_g", "w_in", "conv_w", "ssm_lam_re", "ssm_lam_im", "ssm_log_dt", "ssm_b_re",
           "ssm_b_im", "ssm_c_re", "ssm_c_im", "ssm_d", "ssm_w_glu", "gain_conv_out", "gain_ssm_out", "w_out",
           "norm_ffn_g", "w_up", "ffn_conv_w", "ffn_conv_b", "w_down", "norm_final_g")
TINY_SHARDED = ("meta_tokens", "conv_w", "ffn_conv_w")
REPLICATED = tuple(n for n in WEIGHTS if n not in BIG and n not in TINY_SHARDED)
PACK_COLS = 512


def _pack(arrays, row_mult, cols):
    flat = jnp.concatenate([a.reshape(-1).astype(F32) for a in arrays])
    n = flat.shape[0]
    total = -(-n // (row_mult * cols)) * (row_mult * cols)
    return jnp.concatenate([flat, jnp.zeros((total - n,), F32)]).reshape(total // cols, cols)


def _unpack(packed, shapes):
    flat = packed.reshape(-1)
    out, off = [], 0
    for s in shapes:
        n = math.prod(s)
        out.append(flat[off:off + n].reshape(s))
        off += n
    return out


def kernel(x, meta_tokens, norm_mix_g, w_in, conv_w, ssm_lam_re, ssm_lam_im, ssm_log_dt, ssm_b_re, ssm_b_im, ssm_c_re, ssm_c_im, ssm_d, ssm_w_glu, gain_conv_out, gain_ssm_out, w_out, norm_ffn_g, w_up, ffn_conv_w, ffn_conv_b, w_down, norm_final_g, loss_target, m_meta_tokens, m_norm_mix_g, m_w_in, m_conv_w, m_ssm_lam_re, m_ssm_lam_im, m_ssm_log_dt, m_ssm_b_re, m_ssm_b_im, m_ssm_c_re, m_ssm_c_im, m_ssm_d, m_ssm_w_glu, m_gain_conv_out, m_gain_ssm_out, m_w_out, m_norm_ffn_g, m_w_up, m_ffn_conv_w, m_ffn_conv_b, m_w_down, m_norm_final_g, v_meta_tokens, v_norm_mix_g, v_w_in, v_conv_w, v_ssm_lam_re, v_ssm_lam_im, v_ssm_log_dt, v_ssm_b_re, v_ssm_b_im, v_ssm_c_re, v_ssm_c_im, v_ssm_d, v_ssm_w_glu, v_gain_conv_out, v_gain_ssm_out, v_w_out, v_norm_ffn_g, v_w_up, v_ffn_conv_w, v_ffn_conv_b, v_w_down, v_norm_final_g):
    args = dict(locals())
    w = {n: args[n] for n in WEIGHTS}
    mom = {n: args["m_" + n] for n in WEIGHTS}
    var = {n: args["v_" + n] for n in WEIGHTS}
    kx, ky, kc_ = lax.axis_index("x"), lax.axis_index("y"), lax.axis_index("c")
    chip = 2 * kx + ky
    kc = jnp.stack([chip, kc_]).astype(jnp.int32)

    def squeeze(n, a):
        if n == "meta_tokens":
            return a
        if n == "norm_final_g":
            return a.reshape(1, -1)
        a = a[0]
        return a.reshape(1, -1) if a.ndim == 1 else a

    wl = {n: squeeze(n, w[n]) for n in WEIGHTS}
    ml = {n: squeeze(n, mom[n]) for n in WEIGHTS}
    vl = {n: squeeze(n, var[n]) for n in WEIGHTS}

    tiny = _pack([wl[n] for n in TINY_SHARDED], SUBLANES, LANES)
    full, tiny_all = _all_gather_weights({n: wl[n] for n in BIG_NAMES}, tiny)
    tiny_shapes = [wl[n].shape for n in TINY_SHARDED]
    tiny_parts = [_unpack(tiny_all[k], tiny_shapes) for k in range(4)]
    p = dict(wl)
    p.update(full)
    for j, n in enumerate(TINY_SHARDED):
        p[n] = jnp.concatenate([tiny_parts[k][j] for k in range(4)], axis=1)
    p["ssm_log_dt"] = wl["ssm_log_dt"].reshape(-1)

    loss_local, grad_x, grads = _local_step(x[0], loss_target[0], p)
    loss = lax.psum(loss_local, ("x", "y", "c"))

    small_names = REPLICATED + TINY_SHARDED
    small_shapes = [tuple(grads[n].shape) for n in small_names]
    pack = _pack([grads[n] for n in small_names], 2 * 16, PACK_COLS)
    g_big, g_pack = _reduce_gradients({n: grads[n] for n in BIG_NAMES}, pack, kc)
    g_small = dict(zip(small_names, _unpack(g_pack, small_shapes)))
    g = dict(g_big)
    for n in REPLICATED:
        g[n] = g_small[n].reshape(wl[n].shape)
    for n in TINY_SHARDED:
        cols = wl[n].shape[1]
        g[n] = lax.dynamic_slice_in_dim(g_small[n], chip * cols, cols, axis=1)

    delta, new_m, new_v = {}, {}, {}
    for n in BIG_NAMES:
        delta[n], new_m[n], new_v[n] = _adamw(wl[n], g[n], ml[n], vl[n], "adamw_" + n)
    packs = [_pack([d[n] for n in small_names], SUBLANES, PACK_COLS) for d in (wl, g, ml, vl)]
    shapes = [wl[n].shape for n in small_names]
    for d, packed in zip((delta, new_m, new_v), _adamw(*packs, "adamw_small")):
        d.update(zip(small_names, _unpack(packed, shapes)))

    def like(n, a):
        return a.reshape(w[n].shape)

    return (loss, grad_x[None], *[like(n, g[n]) for n in WEIGHTS], *[like(n, delta[n]) for n in WEIGHTS],
            *[like(n, new_m[n]) for n in WEIGHTS], *[like(n, new_v[n]) for n in WEIGHTS])
```

```python
import functools
import math

import jax
import jax.numpy as jnp
from jax import lax
from jax.experimental import pallas as pl
from jax.experimental.pallas import tpu as pltpu

F32 = jnp.float32
BF16 = jnp.bfloat16
MESH = pl.DeviceIdType.MESH

N_META = 16
N_GROUPS = 32
GROUP = 16
STATE = 64
RMS_EPS = 1e-6
ADAM_LR = 0.001
ADAM_B1 = 0.9
ADAM_B2 = 0.999
ADAM_EPS = 1e-08
ADAM_WD = 0.01
ADAM_STEP = 10

LANES = 128
SUBLANES = 8
ROW_ALIGN = 128
ROW_TILES = 4
VMEM_LIMIT = 52 * 1024 * 1024
GELU_C = math.sqrt(2.0 / math.pi)
GELU_A = 0.044715


def _cparams(*sem):
    return pltpu.CompilerParams(dimension_semantics=sem, vmem_limit_bytes=VMEM_LIMIT)


def _pick_tile(dim, cap, mult):
    best = None
    for t in range(mult, min(dim, cap) + 1, mult):
        if dim % t == 0:
            best = t
    return best if best is not None else dim


def _mm(a, b, mode, name, out_dtype=F32, acc_in=None):
    if mode == "tn":
        kdim, m = a.shape
    else:
        m, kdim = a.shape
    n = b.shape[0] if mode == "nt" else b.shape[1]
    tm = _pick_tile(m, 1408, LANES if mode == "tn" else 16)
    tn = _pick_tile(n, 512, LANES)
    tk = _pick_tile(kdim, 2816, LANES)
    nk = kdim // tk
    has_acc = acc_in is not None

    def body(*refs):
        if has_acc:
            a_ref, b_ref, c_ref, o_ref = refs[:4]
            rest = refs[4:]
        else:
            a_ref, b_ref, o_ref = refs[:3]
            c_ref = None
            rest = refs[3:]
        if mode == "nn":
            p = jnp.dot(a_ref[...], b_ref[...], preferred_element_type=F32)
        elif mode == "nt":
            p = lax.dot_general(a_ref[...], b_ref[...], (((1,), (1,)), ((), ())), preferred_element_type=F32)
        else:
            p = lax.dot_general(a_ref[...], b_ref[...], (((0,), (0,)), ((), ())), preferred_element_type=F32)
        if nk == 1:
            if has_acc:
                p = p + c_ref[...]
            o_ref[...] = p.astype(out_dtype)
        else:
            acc_ref = rest[0]
            k = pl.program_id(2)

            @pl.when(k == 0)
            def _():
                acc_ref[...] = p + c_ref[...] if has_acc else p

            @pl.when(k > 0)
            def _():
                acc_ref[...] += p

            @pl.when(k == nk - 1)
            def _():
                o_ref[...] = acc_ref[...].astype(out_dtype)

    if mode == "tn":
        a_spec = pl.BlockSpec((tk, tm), lambda i, j, k: (k, i))
    else:
        a_spec = pl.BlockSpec((tm, tk), lambda i, j, k: (i, k))
    if mode == "nt":
        b_spec = pl.BlockSpec((tn, tk), lambda i, j, k: (j, k))
    else:
        b_spec = pl.BlockSpec((tk, tn), lambda i, j, k: (k, j))
    o_spec = pl.BlockSpec((tm, tn), lambda i, j, k: (i, j))
    in_specs = [a_spec, b_spec] + ([o_spec] if has_acc else [])
    args = (a, b) + ((acc_in,) if has_acc else ())
    return pl.pallas_call(
        body, name=name, grid=(m // tm, n // tn, nk),
        in_specs=in_specs, out_specs=o_spec,
        out_shape=jax.ShapeDtypeStruct((m, n), out_dtype),
        scratch_shapes=[pltpu.VMEM((tm, tn), F32)] if nk > 1 else [],
        compiler_params=_cparams("parallel", "parallel", "arbitrary"),
    )(*args)


def _rows(shape_cols, tr, dtype=None):
    return pl.BlockSpec((tr, shape_cols), lambda i: (i, 0))


def _const(shape):
    return pl.BlockSpec(shape, lambda i: (0,) * len(shape))


def _rms(x):
    return lax.rsqrt(jnp.mean(x * x, axis=-1, keepdims=True) + RMS_EPS)


def _rms_bwd(x, r, g, dy):
    xn = x * r
    dxn = dy * g
    dx = r * (dxn - xn * jnp.mean(dxn * xn, axis=-1, keepdims=True))
    return dx, dy * xn


def _gelu(y):
    return 0.5 * y * (1.0 + jnp.tanh(GELU_C * (y + GELU_A * y * y * y)))


def _gelu_grad(y):
    t = jnp.tanh(GELU_C * (y + GELU_A * y * y * y))
    return 0.5 * (1.0 + t) + 0.5 * y * (1.0 - t * t) * GELU_C * (1.0 + 3.0 * GELU_A * y * y)


def _sigmoid(z):
    return 1.0 / (1.0 + jnp.exp(-z))


def _norm_fwd(h, g, name, res=None):
    tp, d = h.shape
    tr = tp // ROW_TILES
    has_res = res is not None

    def body(*refs):
        if has_res:
            h_ref, r_ref, g_ref, s_ref, hn_ref = refs
            x = h_ref[...] + r_ref[...]
            s_ref[...] = x
        else:
            h_ref, g_ref, hn_ref = refs
            x = h_ref[...]
        hn_ref[...] = (x * _rms(x) * g_ref[...]).astype(BF16)

    in_specs = [_rows(d, tr)] + ([_rows(d, tr)] if has_res else []) + [_const((1, d))]
    out_specs = ([_rows(d, tr)] if has_res else []) + [_rows(d, tr)]
    out_shape = ([jax.ShapeDtypeStruct((tp, d), F32)] if has_res else []) + [jax.ShapeDtypeStruct((tp, d), BF16)]
    args = (h,) + ((res,) if has_res else ()) + (g,)
    out = pl.pallas_call(body, name=name, grid=(ROW_TILES,), in_specs=in_specs, out_specs=out_specs,
                         out_shape=out_shape, compiler_params=_cparams("parallel"))(*args)
    return out if has_res else out[0]


def _norm_bwd(h, g, dhn, dres, name):
    tp, d = h.shape
    tr = tp // ROW_TILES

    def body(h_ref, g_ref, dhn_ref, dres_ref, dh_ref, dhb_ref, dg_ref):
        x = h_ref[...]
        dx, dgs = _rms_bwd(x, _rms(x), g_ref[...], dhn_ref[...])
        dh = dres_ref[...] + dx
        dh_ref[...] = dh
        dhb_ref[...] = dh.astype(BF16)

        @pl.when(pl.program_id(0) == 0)
        def _():
            dg_ref[...] = jnp.zeros_like(dg_ref)

        dg_ref[...] += jnp.sum(dgs, axis=0, keepdims=True)

    return pl.pallas_call(
        body, name=name, grid=(ROW_TILES,),
        in_specs=[_rows(d, tr), _const((1, d)), _rows(d, tr), _rows(d, tr)],
        out_specs=[_rows(d, tr), _rows(d, tr), _const((1, d))],
        out_shape=[jax.ShapeDtypeStruct((tp, d), F32), jax.ShapeDtypeStruct((tp, d), BF16),
                   jax.ShapeDtypeStruct((1, d), F32)],
        compiler_params=_cparams("arbitrary"))(h, g, dhn, dres)


def _loss_bwd(h1, dn, tgt, g, n_real, name):
    tp, d = h1.shape
    tr = tp // ROW_TILES

    def body(h1_ref, dn_ref, t_ref, g_ref, loss_ref, dh_ref, dhb_ref, dg_ref):
        i = pl.program_id(0)
        x = h1_ref[...] + dn_ref[...]
        r = _rms(x)
        row = i * tr + lax.broadcasted_iota(jnp.int32, (tr, d), 0)
        valid = (row >= N_META) & (row < n_real)
        e = jnp.where(valid, x * r * g_ref[...] - t_ref[...], 0.0)
        dx, dgs = _rms_bwd(x, r, g_ref[...], e * (1.0 / d))
        dh_ref[...] = dx
        dhb_ref[...] = dx.astype(BF16)

        @pl.when(i == 0)
        def _():
            dg_ref[...] = jnp.zeros_like(dg_ref)
            loss_ref[...] = jnp.zeros_like(loss_ref)

        dg_ref[...] += jnp.sum(dgs, axis=0, keepdims=True)
        loss_ref[...] += (0.5 / d) * jnp.sum(jnp.sum(e * e, axis=0, keepdims=True), axis=1, keepdims=True)

    return pl.pallas_call(
        body, name=name, grid=(ROW_TILES,),
        in_specs=[_rows(d, tr), _rows(d, tr), _rows(d, tr), _const((1, d))],
        out_specs=[_const((1, LANES)), _rows(d, tr), _rows(d, tr), _const((1, d))],
        out_shape=[jax.ShapeDtypeStruct((1, LANES), F32), jax.ShapeDtypeStruct((tp, d), F32),
                   jax.ShapeDtypeStruct((tp, d), BF16), jax.ShapeDtypeStruct((1, d), F32)],
        compiler_params=_cparams("arbitrary"))(h1, dn, tgt, g)


def _mix_fwd(co, y, z, gc, gs, name):
    tp, dh = co.shape
    tr = tp // ROW_TILES

    def body(co_ref, y_ref, z_ref, gc_ref, gs_ref, m_ref):
        c = co_ref[...]
        m_ref[:, :dh] = (c * _rms(c) * gc_ref[...]).astype(BF16)
        so = _gelu(y_ref[...]) * _sigmoid(z_ref[...])
        m_ref[:, dh:] = (so * _rms(so) * gs_ref[...]).astype(BF16)

    return pl.pallas_call(
        body, name=name, grid=(ROW_TILES,),
        in_specs=[_rows(dh, tr)] * 3 + [_const((1, dh))] * 2,
        out_specs=_rows(2 * dh, tr),
        out_shape=jax.ShapeDtypeStruct((tp, 2 * dh), BF16),
        compiler_params=_cparams("parallel"))(co, y, z, gc, gs)


def _mix_bwd(dm, co, y, z, gc, gs, name):
    tp, dh = co.shape
    tr = tp // ROW_TILES

    def body(dm_ref, co_ref, y_ref, z_ref, gc_ref, gs_ref, dco_ref, dz_ref, dgp_ref, dgc_ref, dgs_ref):
        c = co_ref[...]
        dco, dgc = _rms_bwd(c, _rms(c), gc_ref[...], dm_ref[:, :dh])
        dco_ref[...] = dco
        gl = _gelu(y_ref[...])
        sg = _sigmoid(z_ref[...])
        so = gl * sg
        dso, dgs = _rms_bwd(so, _rms(so), gs_ref[...], dm_ref[:, dh:])
        dz_ref[...] = (dso * gl * sg * (1.0 - sg)).astype(BF16)
        dgp_ref[...] = dso * sg

        @pl.when(pl.program_id(0) == 0)
        def _():
            dgc_ref[...] = jnp.zeros_like(dgc_ref)
            dgs_ref[...] = jnp.zeros_like(dgs_ref)

        dgc_ref[...] += jnp.sum(dgc, axis=0, keepdims=True)
        dgs_ref[...] += jnp.sum(dgs, axis=0, keepdims=True)

    return pl.pallas_call(
        body, name=name, grid=(ROW_TILES,),
        in_specs=[_rows(2 * dh, tr)] + [_rows(dh, tr)] * 3 + [_const((1, dh))] * 2,
        out_specs=[_rows(dh, tr), _rows(dh, tr), _rows(dh, tr), _const((1, dh)), _const((1, dh))],
        out_shape=[jax.ShapeDtypeStruct((tp, dh), F32), jax.ShapeDtypeStruct((tp, dh), BF16),
                   jax.ShapeDtypeStruct((tp, dh), F32), jax.ShapeDtypeStruct((1, dh), F32),
                   jax.ShapeDtypeStruct((1, dh), F32)],
        compiler_params=_cparams("arbitrary"))(dm, co, y, z, gc, gs)


def _shift_down(x, k):
    row = lax.broadcasted_iota(jnp.int32, x.shape, 0)
    return jnp.where(row >= k, pltpu.roll(x, k, 0), 0.0)


def _shift_up(x, k):
    n = x.shape[0]
    row = lax.broadcasted_iota(jnp.int32, x.shape, 0)
    return jnp.where(row < n - k, pltpu.roll(x, n - k, 0), 0.0)


def _dwconv(x, w_ref):
    return w_ref[2:3, :] * x + w_ref[1:2, :] * _shift_down(x, 1) + w_ref[0:1, :] * _shift_down(x, 2)


def _dwconv_bwd(x, dy, w_ref):
    dx = w_ref[2:3, :] * dy + w_ref[1:2, :] * _shift_up(dy, 1) + w_ref[0:1, :] * _shift_up(dy, 2)
    dw = jnp.concatenate([jnp.sum(dy * _shift_down(x, 2), axis=0, keepdims=True),
                          jnp.sum(dy * _shift_down(x, 1), axis=0, keepdims=True),
                          jnp.sum(dy * x, axis=0, keepdims=True)], axis=0)
    return dx, dw


def _scan(s_re, s_im, tab_ref, reverse):
    n_chunks = s_re.shape[0] // SUBLANES
    n_strips = s_re.shape[1] // LANES
    last = 0 if reverse else SUBLANES - 1

    def body(i, carry):
        chunk = (n_chunks - 1 - i) if reverse else i
        r0 = pl.multiple_of(chunk * SUBLANES, SUBLANES)
        out = []
        for st in range(n_strips):
            lanes = slice(st * LANES, (st + 1) * LANES)
            cr, ci = carry[2 * st], carry[2 * st + 1]
            xr = s_re[pl.ds(r0, SUBLANES), lanes]
            xi = s_im[pl.ds(r0, SUBLANES), lanes]
            for level, k in enumerate((1, 2, 4)):
                mr = tab_ref[2 * level, :, lanes]
                mi = tab_ref[2 * level + 1, :, lanes]
                sh = SUBLANES - k if reverse else k
                rr = pltpu.roll(xr, sh, 0)
                ri = pltpu.roll(xi, sh, 0)
                xr, xi = xr + (mr * rr - mi * ri), xi + (mr * ri + mi * rr)
            pwr = tab_ref[6, :, lanes]
            pwi = tab_ref[7, :, lanes]
            xr, xi = xr + (pwr * cr - pwi * ci), xi + (pwr * ci + pwi * cr)
            s_re[pl.ds(r0, SUBLANES), lanes] = xr
            s_im[pl.ds(r0, SUBLANES), lanes] = xi
            out.append(jnp.broadcast_to(xr[last:last + 1, :], (SUBLANES, LANES)))
            out.append(jnp.broadcast_to(xi[last:last + 1, :], (SUBLANES, LANES)))
        return tuple(out)

    zero = jnp.zeros((SUBLANES, LANES), F32)
    lax.fori_loop(0, n_chunks, body, (zero,) * (2 * n_strips))


def _seq_fwd(proj, conv_w, bc_re, bc_im, cc_re, cc_im, dskip, tab_f, name):
    tp = proj.shape[0]
    dh = proj.shape[1] // 4
    nq = dh // LANES
    sw = STATE * N_GROUPS // nq

    def body(b_ref, c_ref, v_ref, u_ref, w_ref, bre_ref, bim_ref, cre_ref, cim_ref, d_ref, tab_ref,
             co_ref, y_ref, g_ref, s_re, s_im):
        co_ref[...] = b_ref[...] * _dwconv(c_ref[...] * v_ref[...], w_ref)
        u = u_ref[...]
        ub = u.astype(BF16)
        s_re[...] = jnp.dot(ub, bre_ref[...], preferred_element_type=F32)
        s_im[...] = jnp.dot(ub, bim_ref[...], preferred_element_type=F32)
        _scan(s_re, s_im, tab_ref, False)
        y = (jnp.dot(s_re[...].astype(BF16), cre_ref[...], preferred_element_type=F32)
             - jnp.dot(s_im[...].astype(BF16), cim_ref[...], preferred_element_type=F32)
             + d_ref[...] * u)
        y_ref[...] = y
        g_ref[...] = _gelu(y).astype(BF16)

    col = lambda off: pl.BlockSpec((tp, LANES), lambda q, off=off: (0, off * nq + q))
    blk = pl.BlockSpec((tp, LANES), lambda q: (0, q))
    return pl.pallas_call(
        body, name=name, grid=(nq,),
        in_specs=[col(0), col(1), col(2), col(3),
                  pl.BlockSpec((3, LANES), lambda q: (0, q)),
                  pl.BlockSpec((LANES, sw), lambda q: (0, q)), pl.BlockSpec((LANES, sw), lambda q: (0, q)),
                  pl.BlockSpec((sw, LANES), lambda q: (q, 0)), pl.BlockSpec((sw, LANES), lambda q: (q, 0)),
                  pl.BlockSpec((1, LANES), lambda q: (0, q)),
                  pl.BlockSpec((8, SUBLANES, sw), lambda q: (0, 0, q))],
        out_specs=[blk, blk, blk],
        out_shape=[jax.ShapeDtypeStruct((tp, dh), F32), jax.ShapeDtypeStruct((tp, dh), F32),
                   jax.ShapeDtypeStruct((tp, dh), BF16)],
        scratch_shapes=[pltpu.VMEM((tp, sw), F32), pltpu.VMEM((tp, sw), F32)],
        compiler_params=_cparams("parallel"),
    )(proj, proj, proj, proj, conv_w, bc_re, bc_im, cc_re, cc_im, dskip, tab_f)


def _conv_bwd(proj, dco, conv_w, name):
    tp = proj.shape[0]
    dh = proj.shape[1] // 4
    nq = dh // LANES

    def body(b_ref, c_ref, v_ref, dco_ref, w_ref, dproj_ref, dw_ref, stage, sem):
        q = pl.program_id(0)
        cg = c_ref[...]
        vg = v_ref[...]
        cv = cg * vg
        dco_v = dco_ref[...]
        dcv, dw = _dwconv_bwd(cv, dco_v * b_ref[...], w_ref)
        dw_ref[...] = dw
        stage[0] = (dco_v * _dwconv(cv, w_ref)).astype(BF16)
        stage[1] = (dcv * vg).astype(BF16)
        stage[2] = (dcv * cg).astype(BF16)
        copies = [pltpu.make_async_copy(stage.at[p], dproj_ref.at[:, pl.ds((p * nq + q) * LANES, LANES)], sem.at[p])
                  for p in range(3)]
        for cp in copies:
            cp.start()
        for cp in copies:
            cp.wait()

    col = lambda off: pl.BlockSpec((tp, LANES), lambda q, off=off: (0, off * nq + q))
    return pl.pallas_call(
        body, name=name, grid=(nq,),
        in_specs=[col(0), col(1), col(2), pl.BlockSpec((tp, LANES), lambda q: (0, q)),
                  pl.BlockSpec((3, LANES), lambda q: (0, q))],
        out_specs=[pl.BlockSpec(memory_space=pl.ANY), pl.BlockSpec((3, LANES), lambda q: (0, q))],
        out_shape=[jax.ShapeDtypeStruct((tp, 4 * dh), BF16), jax.ShapeDtypeStruct((3, dh), F32)],
        scratch_shapes=[pltpu.VMEM((3, tp, LANES), BF16), pltpu.SemaphoreType.DMA((3,))],
        compiler_params=_cparams("arbitrary"),
    )(proj, proj, proj, dco, conv_w)


def _ssm_bwd(proj, y, dg, dproj, bc_re, bc_im, cc_re, cc_im, dskip, tab_f, tab_r, name):
    tp = proj.shape[0]
    dh = proj.shape[1] // 4
    nq = dh // LANES
    sw = STATE * N_GROUPS // nq

    def body(u_ref, y_ref, dg_ref, dproj_in, bre_ref, bim_ref, cre_ref, cim_ref, d_ref, tabf_ref, tabr_ref,
             dproj_ref, dbre_ref, dbim_ref, dcre_ref, dcim_ref, dd_ref, dar_ref, dai_ref,
             s_re, s_im, l_re, l_im, stage, sem):
        del dproj_in
        q = pl.program_id(0)
        nt = (((1,), (1,)), ((), ()))
        tn = (((0,), (0,)), ((), ()))
        u = u_ref[...]
        ub = u.astype(BF16)
        s_re[...] = jnp.dot(ub, bre_ref[...], preferred_element_type=F32)
        s_im[...] = jnp.dot(ub, bim_ref[...], preferred_element_type=F32)
        _scan(s_re, s_im, tabf_ref, False)
        dy = dg_ref[...] * _gelu_grad(y_ref[...])
        dyb = dy.astype(BF16)
        dd_ref[...] = jnp.sum(dy * u, axis=0, keepdims=True)
        l_re[...] = lax.dot_general(dyb, cre_ref[...], nt, preferred_element_type=F32)
        l_im[...] = -lax.dot_general(dyb, cim_ref[...], nt, preferred_element_type=F32)
        dcre_ref[...] = lax.dot_general(s_re[...].astype(BF16), dyb, tn, preferred_element_type=F32)
        dcim_ref[...] = -lax.dot_general(s_im[...].astype(BF16), dyb, tn, preferred_element_type=F32)
        _scan(l_re, l_im, tabr_ref, True)
        for st in range(sw // LANES):
            lanes = slice(st * LANES, (st + 1) * LANES)
            lr = l_re[:, lanes]
            li = l_im[:, lanes]
            pr = _shift_down(s_re[:, lanes], 1)
            pi = _shift_down(s_im[:, lanes], 1)
            dar_ref[:, lanes] = jnp.sum(lr * pr + li * pi, axis=0, keepdims=True)
            dai_ref[:, lanes] = jnp.sum(li * pr - lr * pi, axis=0, keepdims=True)
        lrb = l_re[...].astype(BF16)
        lib = l_im[...].astype(BF16)
        du = (dy * d_ref[...] + lax.dot_general(lrb, bre_ref[...], nt, preferred_element_type=F32)
              + lax.dot_general(lib, bim_ref[...], nt, preferred_element_type=F32))
        stage[...] = du.astype(BF16)
        dbre_ref[...] = lax.dot_general(ub, lrb, tn, preferred_element_type=F32)
        dbim_ref[...] = lax.dot_general(ub, lib, tn, preferred_element_type=F32)
        cp = pltpu.make_async_copy(stage, dproj_ref.at[:, pl.ds((3 * nq + q) * LANES, LANES)], sem)
        cp.start()
        cp.wait()

    blk = pl.BlockSpec((tp, LANES), lambda q: (0, q))
    bspec = pl.BlockSpec((LANES, sw), lambda q: (0, q))
    cspec = pl.BlockSpec((sw, LANES), lambda q: (q, 0))
    tspec = pl.BlockSpec((8, SUBLANES, sw), lambda q: (0, 0, q))
    nstate = STATE * N_GROUPS
    return pl.pallas_call(
        body, name=name, grid=(nq,),
        in_specs=[pl.BlockSpec((tp, LANES), lambda q: (0, 3 * nq + q)), blk, blk, pl.BlockSpec(memory_space=pl.ANY),
                  bspec, bspec, cspec, cspec, pl.BlockSpec((1, LANES), lambda q: (0, q)), tspec, tspec],
        out_specs=[pl.BlockSpec(memory_space=pl.ANY), bspec, bspec, cspec, cspec,
                   pl.BlockSpec((1, LANES), lambda q: (0, q)),
                   pl.BlockSpec((1, sw), lambda q: (0, q)), pl.BlockSpec((1, sw), lambda q: (0, q))],
        out_shape=[jax.ShapeDtypeStruct((tp, 4 * dh), BF16),
                   jax.ShapeDtypeStruct((LANES, nstate), F32), jax.ShapeDtypeStruct((LANES, nstate), F32),
                   jax.ShapeDtypeStruct((nstate, LANES), F32), jax.ShapeDtypeStruct((nstate, LANES), F32),
                   jax.ShapeDtypeStruct((1, dh), F32),
                   jax.ShapeDtypeStruct((1, nstate), F32), jax.ShapeDtypeStruct((1, nstate), F32)],
        input_output_aliases={3: 0},
        scratch_shapes=[pltpu.VMEM((tp, sw), F32)] * 4 + [pltpu.VMEM((tp, LANES), BF16), pltpu.SemaphoreType.DMA],
        compiler_params=_cparams("arbitrary"),
    )(proj, y, dg, dproj, bc_re, bc_im, cc_re, cc_im, dskip, tab_f, tab_r)


FFN_TILE = 256


def _ffn_act(up, fw, fb, name):
    tp, two_ff = up.shape
    dff = two_ff // 2
    tc = FFN_TILE
    nj = dff // tc

    def body(ua_ref, uv_ref, wa_ref, wv_ref, ba_ref, bv_ref, act_ref):
        a = _dwconv(ua_ref[...], wa_ref) + ba_ref[...]
        v = _dwconv(uv_ref[...], wv_ref) + bv_ref[...]
        act_ref[...] = (a * _sigmoid(a) * v).astype(BF16)

    lo = lambda r: pl.BlockSpec((r, tc), lambda j: (0, j))
    hi = lambda r: pl.BlockSpec((r, tc), lambda j: (0, nj + j))
    return pl.pallas_call(
        body, name=name, grid=(nj,),
        in_specs=[lo(tp), hi(tp), lo(3), hi(3), lo(1), hi(1)],
        out_specs=lo(tp),
        out_shape=jax.ShapeDtypeStruct((tp, dff), BF16),
        compiler_params=_cparams("parallel"))(up, up, fw, fw, fb, fb)


def _ffn_bwd(up, dact, fw, fb, name):
    tp, two_ff = up.shape
    dff = two_ff // 2
    tc = FFN_TILE
    nj = dff // tc

    def body(ua_ref, uv_ref, da_ref, wa_ref, wv_ref, ba_ref, bv_ref,
             dup_ref, dwa_ref, dwv_ref, dba_ref, dbv_ref, stage, sem):
        j = pl.program_id(0)
        ua = ua_ref[...]
        uv = uv_ref[...]
        a = _dwconv(ua, wa_ref) + ba_ref[...]
        v = _dwconv(uv, wv_ref) + bv_ref[...]
        sg = _sigmoid(a)
        dact_v = da_ref[...]
        da = dact_v * v * sg * (1.0 + a * (1.0 - sg))
        dv = dact_v * a * sg
        dba_ref[...] = jnp.sum(da, axis=0, keepdims=True)
        dbv_ref[...] = jnp.sum(dv, axis=0, keepdims=True)
        dua, dwa = _dwconv_bwd(ua, da, wa_ref)
        duv, dwv = _dwconv_bwd(uv, dv, wv_ref)
        dwa_ref[...] = dwa
        dwv_ref[...] = dwv
        stage[0] = dua.astype(BF16)
        stage[1] = duv.astype(BF16)
        copies = [pltpu.make_async_copy(stage.at[p], dup_ref.at[:, pl.ds((p * nj + j) * tc, tc)], sem.at[p])
                  for p in range(2)]
        for cp in copies:
            cp.start()
        for cp in copies:
            cp.wait()

    lo = lambda r: pl.BlockSpec((r, tc), lambda j: (0, j))
    hi = lambda r: pl.BlockSpec((r, tc), lambda j: (0, nj + j))
    return pl.pallas_call(
        body, name=name, grid=(nj,),
        in_specs=[lo(tp), hi(tp), lo(tp), lo(3), hi(3), lo(1), hi(1)],
        out_specs=[pl.BlockSpec(memory_space=pl.ANY), lo(3), lo(3), lo(1), lo(1)],
        out_shape=[jax.ShapeDtypeStruct((tp, two_ff), BF16),
                   jax.ShapeDtypeStruct((3, dff), F32), jax.ShapeDtypeStruct((3, dff), F32),
                   jax.ShapeDtypeStruct((1, dff), F32), jax.ShapeDtypeStruct((1, dff), F32)],
        scratch_shapes=[pltpu.VMEM((2, tp, tc), BF16), pltpu.SemaphoreType.DMA((2,))],
        compiler_params=_cparams("arbitrary"))(up, up, dact, fw, fw, fb, fb)


def _s5_discretize(lam_re, lam_im, log_dt, b_re, b_im):
    dt = jnp.exp(log_dt)[:, None]
    mag = jnp.exp(lam_re * dt)
    ang = lam_im * dt
    a_re = mag * jnp.cos(ang)
    a_im = mag * jnp.sin(ang)
    den = lam_re * lam_re + lam_im * lam_im
    nr = a_re - 1.0
    f_re = (nr * lam_re + a_im * lam_im) / den
    f_im = (a_im * lam_re - nr * lam_im) / den
    bb_re = f_re[..., None] * b_re - f_im[..., None] * b_im
    bb_im = f_re[..., None] * b_im + f_im[..., None] * b_re
    return a_re, a_im, bb_re, bb_im


def _scan_tables(a_re, a_im, reverse):
    ar = a_re.reshape(-1)
    ai = -a_im.reshape(-1) if reverse else a_im.reshape(-1)
    pows = [(ar, ai)]
    for _ in range(SUBLANES - 1):
        pr, pi = pows[-1]
        pows.append((pr * ar - pi * ai, pr * ai + pi * ar))
    row = jnp.arange(SUBLANES)[:, None]
    out = []
    for k in (1, 2, 4):
        mask = (row <= SUBLANES - 1 - k) if reverse else (row >= k)
        out.append(jnp.where(mask, pows[k - 1][0][None, :], 0.0))
        out.append(jnp.where(mask, pows[k - 1][1][None, :], 0.0))
    order = list(range(SUBLANES - 1, -1, -1)) if reverse else list(range(SUBLANES))
    out.append(jnp.stack([pows[k][0] for k in order]))
    out.append(jnp.stack([pows[k][1] for k in order]))
    return jnp.stack(out).astype(F32)


def _compact_b(bb):
    bq = bb.reshape(N_GROUPS // 8, 8, STATE, GROUP)
    m = jnp.einsum("ab,qbph->qahbp", jnp.eye(8, dtype=bb.dtype), bq).reshape(N_GROUPS // 8, LANES, 8 * STATE)
    return m.transpose(1, 0, 2).reshape(LANES, N_GROUPS * STATE)


def _expand_b(m):
    d = m.reshape(8, GROUP, N_GROUPS // 8, 8, STATE)
    return jnp.einsum("ahqap->qaph", d).reshape(N_GROUPS, STATE, GROUP)


def _compact_c(c):
    cq = c.reshape(N_GROUPS // 8, 8, GROUP, STATE)
    return jnp.einsum("ab,qbhp->qbpah", jnp.eye(8, dtype=c.dtype), cq).reshape(N_GROUPS * STATE, LANES)


def _expand_c(m):
    d = m.reshape(N_GROUPS // 8, 8, STATE, 8, GROUP)
    return jnp.einsum("qbpbh->qbhp", d).reshape(N_GROUPS, GROUP, STATE)


def _local_step(x, target, p):
    seq, d = x.shape
    n_real = N_META + seq
    tp = -(-n_real // ROW_ALIGN) * ROW_ALIGN
    pad = jnp.zeros((tp - n_real, d), F32)
    h0 = jnp.concatenate([p["meta_tokens"], x, pad], axis=0)
    tgt = jnp.concatenate([jnp.zeros((N_META, d), F32), target, pad], axis=0)

    s5 = (p["ssm_lam_re"], p["ssm_lam_im"], p["ssm_log_dt"], p["ssm_b_re"], p["ssm_b_im"])
    (a_re, a_im, bb_re, bb_im), s5_vjp = jax.vjp(_s5_discretize, *s5)
    tab_f = _scan_tables(a_re, a_im, False)
    tab_r = _scan_tables(a_re, a_im, True)
    bc_re = _compact_b(bb_re).astype(BF16)
    bc_im = _compact_b(bb_im).astype(BF16)
    cc_re = _compact_c(p["ssm_c_re"]).astype(BF16)
    cc_im = _compact_c(p["ssm_c_im"]).astype(BF16)
    dskip = p["ssm_d"].reshape(1, -1)
    dh = dskip.shape[1]

    hn1 = _norm_fwd(h0, p["norm_mix_g"], "norm_mix")
    proj = _mm(hn1, p["w_in"], "nn", "proj")
    co, y, g = _seq_fwd(proj, p["conv_w"], bc_re, bc_im, cc_re, cc_im, dskip, tab_f, "seq_fwd")
    z = _mm(g, p["ssm_w_glu"], "nn", "glu")
    mixed = _mix_fwd(co, y, z, p["gain_conv_out"], p["gain_ssm_out"], "mix_fwd")
    mo = _mm(mixed, p["w_out"], "nn", "out_proj")
    h1, hn2 = _norm_fwd(h0, p["norm_ffn_g"], "norm_ffn", res=mo)
    up = _mm(hn2, p["w_up"], "nn", "up_proj")
    act = _ffn_act(up, p["ffn_conv_w"], p["ffn_conv_b"], "ffn_act")
    dn = _mm(act, p["w_down"], "nn", "down_proj")
    loss, dh2, dh2b, d_gfin = _loss_bwd(h1, dn, tgt, p["norm_final_g"], n_real, "loss_bwd")

    g_w_down = _mm(act, dh2b, "tn", "g_w_down")
    dact = _mm(dh2b, p["w_down"], "nt", "d_act")
    dup, dfw_a, dfw_v, dfb_a, dfb_v = _ffn_bwd(up, dact, p["ffn_conv_w"], p["ffn_conv_b"], "ffn_bwd")
    g_w_up = _mm(hn2, dup, "tn", "g_w_up")
    dhn2 = _mm(dup, p["w_up"], "nt", "d_hn2")
    dh1, dh1b, d_gffn = _norm_bwd(h1, p["norm_ffn_g"], dhn2, dh2, "norm_ffn_bwd")
    g_w_out = _mm(mixed, dh1b, "tn", "g_w_out")
    dmixed = _mm(dh1b, p["w_out"], "nt", "d_mixed")
    dco, dz, dgp, d_gc, d_gs = _mix_bwd(dmixed, co, y, z, p["gain_conv_out"], p["gain_ssm_out"], "mix_bwd")
    g_w_glu = _mm(g, dz, "tn", "g_w_glu")
    dg = _mm(dz, p["ssm_w_glu"], "nt", "d_gelu", acc_in=dgp)
    dproj, d_conv_w = _conv_bwd(proj, dco, p["conv_w"], "conv_bwd")
    (dproj, dbc_re, dbc_im, dcc_re, dcc_im, d_dskip, da_re, da_im) = _ssm_bwd(
        proj, y, dg, dproj, bc_re, bc_im, cc_re, cc_im, dskip, tab_f, tab_r, "ssm_bwd")
    g_w_in = _mm(hn1, dproj, "tn", "g_w_in")
    dhn1 = _mm(dproj, p["w_in"], "nt", "d_hn1")
    dh0, _, d_gmix = _norm_bwd(h0, p["norm_mix_g"], dhn1, dh1, "norm_mix_bwd")

    d_lam_re, d_lam_im, d_log_dt, d_b_re, d_b_im = s5_vjp(
        (da_re.reshape(N_GROUPS, STATE), da_im.reshape(N_GROUPS, STATE), _expand_b(dbc_re), _expand_b(dbc_im)))
    grads = {
        "meta_tokens": dh0[:N_META], "norm_mix_g": d_gmix, "w_in": g_w_in, "conv_w": d_conv_w,
        "ssm_lam_re": d_lam_re, "ssm_lam_im": d_lam_im, "ssm_log_dt": d_log_dt,
        "ssm_b_re": d_b_re, "ssm_b_im": d_b_im, "ssm_c_re": _expand_c(dcc_re), "ssm_c_im": _expand_c(dcc_im),
        "ssm_d": d_dskip.reshape(N_GROUPS, GROUP), "ssm_w_glu": g_w_glu,
        "gain_conv_out": d_gc, "gain_ssm_out": d_gs, "w_out": g_w_out, "norm_ffn_g": d_gffn,
        "w_up": g_w_up, "ffn_conv_w": jnp.concatenate([dfw_a, dfw_v], axis=1),
        "ffn_conv_b": jnp.concatenate([dfb_a, dfb_v], axis=1), "w_down": g_w_down, "norm_final_g": d_gfin,
    }
    return loss[0, 0], dh0[N_META:n_real], grads


def _view(ref, axis, start, size):
    idx = [slice(None)] * len(ref.shape)
    idx[axis] = pl.ds(start, size)
    return ref.at[tuple(idx)]


def _exchange(name, ins, outs, aliases, local_copies, remote_copies):
    ni, no = len(ins), len(outs)
    nl, nr = len(local_copies), len(remote_copies)

    def body(*refs):
        in_refs, out_refs = refs[:ni], refs[ni:ni + no]
        send_sems, recv_sems, local_sems = refs[ni + no:]
        x, y, c = lax.axis_index("x"), lax.axis_index("y"), lax.axis_index("c")
        pos = (x, y, c, 2 * x + y)
        locals_ = [pltpu.make_async_copy(s(in_refs, out_refs, pos), d(in_refs, out_refs, pos), local_sems.at[i])
                   for i, (s, d) in enumerate(local_copies)]
        remotes = []
        for i, (s, d, flip) in enumerate(remote_copies):
            peer = (1 - x if "x" in flip else x, 1 - y if "y" in flip else y, 1 - c if "c" in flip else c)
            remotes.append(pltpu.make_async_remote_copy(
                src_ref=s(in_refs, out_refs, pos), dst_ref=d(in_refs, out_refs, pos),
                send_sem=send_sems.at[i], recv_sem=recv_sems.at[i], device_id=peer, device_id_type=MESH))
        for cp in locals_ + remotes:
            cp.start()
        for cp in remotes:
            cp.wait_recv()
        for cp in remotes:
            cp.wait_send()
        for cp in locals_:
            cp.wait()

    hbm = pl.BlockSpec(memory_space=pl.ANY)
    return pl.pallas_call(
        body, name=name, in_specs=[hbm] * ni, out_specs=[hbm] * no, out_shape=outs,
        input_output_aliases=aliases,
        scratch_shapes=[pltpu.SemaphoreType.DMA((nr,)), pltpu.SemaphoreType.DMA((nr,)),
                        pltpu.SemaphoreType.DMA((max(nl, 1),))],
    )(*ins)


BIG = {"w_in": (0, 1), "ssm_w_glu": (1, 0), "w_out": (1, 0), "w_up": (0, 1), "w_down": (1, 0)}
BIG_NAMES = tuple(BIG)
FLIPS = ("y", "x", "xy")


def _peer_chip(pos, flip):
    x, y, _, _ = pos
    return 2 * (1 - x if "x" in flip else x) + (1 - y if "y" in flip else y)


def _block_rows(rows, cols, itemsize, mult):
    return _pick_tile(rows, max(mult, (2 * 1024 * 1024) // (cols * itemsize)), mult)


def _cast_into_full(w, kc, shard_axis, name):
    r, cdim = w.shape
    tr = _block_rows(r, cdim, 4, 16)
    nb = r // tr

    def body(kc_ref, w_ref, o_ref):
        o_ref[...] = w_ref[...].astype(BF16)

    if shard_axis == 1:
        full, o_spec = (r, 4 * cdim), pl.BlockSpec((tr, cdim), lambda i, kc: (i, kc[0]))
    else:
        full, o_spec = (4 * r, cdim), pl.BlockSpec((tr, cdim), lambda i, kc: (kc[0] * nb + i, 0))
    return pl.pallas_call(
        body, name=name,
        grid_spec=pltpu.PrefetchScalarGridSpec(
            num_scalar_prefetch=1, grid=(nb,), in_specs=[pl.BlockSpec((tr, cdim), lambda i, kc: (i, 0))],
            out_specs=o_spec),
        out_shape=jax.ShapeDtypeStruct(full, BF16), compiler_params=_cparams("parallel"))(kc, w)


def _pair_sum(g, recv, kc, half_axis, name, out_dtype):
    hr, hc = recv.shape
    tr = _block_rows(hr, hc, 4, 16)
    nb = hr // tr

    def body(kc_ref, g_ref, r_ref, o_ref):
        o_ref[...] = (g_ref[...] + r_ref[...]).astype(out_dtype)

    if half_axis == 0:
        g_spec = pl.BlockSpec((tr, hc), lambda i, kc: (kc[1] * nb + i, 0))
    elif half_axis == 1:
        g_spec = pl.BlockSpec((tr, hc), lambda i, kc: (i, kc[1]))
    else:
        g_spec = pl.BlockSpec((tr, hc), lambda i, kc: (i, 0))
    same = pl.BlockSpec((tr, hc), lambda i, kc: (i, 0))
    return pl.pallas_call(
        body, name=name,
        grid_spec=pltpu.PrefetchScalarGridSpec(num_scalar_prefetch=1, grid=(nb,), in_specs=[g_spec, same],
                                               out_specs=same),
        out_shape=jax.ShapeDtypeStruct((hr, hc), out_dtype), compiler_params=_cparams("parallel"))(kc, g, recv)


def _chip_sum(own, recv, kc, own_axis, out_axis, name):
    _, sr, sc = recv.shape
    tr = _block_rows(sr, sc, 4, 16)
    nb = sr // tr

    def body(kc_ref, o_ref, r_ref, t_ref):
        k = kc_ref[0]
        own_v = o_ref[...].astype(F32)
        r = [r_ref[m].astype(F32) for m in range(3)]
        terms = []
        for kk in range(4):
            m = jnp.bitwise_xor(k, kk)
            terms.append(jnp.where(m == 0, own_v, jnp.where(m == 1, r[0], jnp.where(m == 2, r[1], r[2]))))
        t_ref[...] = (terms[0] + terms[1]) + (terms[2] + terms[3])

    if own_axis == 0:
        own_spec = pl.BlockSpec((tr, sc), lambda i, kc: (kc[0] * nb + i, 0))
    elif own_axis == 1:
        own_spec = pl.BlockSpec((tr, sc), lambda i, kc: (i, kc[0]))
    else:
        own_spec = pl.BlockSpec((tr, sc), lambda i, kc: (kc[1] * nb + i, 0))
    if out_axis == 0:
        out_full, out_spec = (2 * sr, sc), pl.BlockSpec((tr, sc), lambda i, kc: (kc[1] * nb + i, 0))
    else:
        out_full, out_spec = (sr, 2 * sc), pl.BlockSpec((tr, sc), lambda i, kc: (i, kc[1]))
    return pl.pallas_call(
        body, name=name,
        grid_spec=pltpu.PrefetchScalarGridSpec(
            num_scalar_prefetch=1, grid=(nb,),
            in_specs=[own_spec, pl.BlockSpec((3, tr, sc), lambda i, kc: (0, i, 0))],
            out_specs=out_spec),
        out_shape=jax.ShapeDtypeStruct(out_full, F32), compiler_params=_cparams("parallel"))(kc, own, recv)


def _adamw(w, g, m, v, name):
    r, cdim = w.shape
    tr = _block_rows(r, cdim, 4, 8)
    c1 = 1.0 - ADAM_B1 ** ADAM_STEP
    c2 = 1.0 - ADAM_B2 ** ADAM_STEP

    def body(w_ref, g_ref, m_ref, v_ref, d_ref, nm_ref, nv_ref):
        gv = g_ref[...]
        nm = ADAM_B1 * m_ref[...] + (1.0 - ADAM_B1) * gv
        nv = ADAM_B2 * v_ref[...] + (1.0 - ADAM_B2) * (gv * gv)
        d_ref[...] = -ADAM_LR * ((nm / c1) / (jnp.sqrt(nv / c2) + ADAM_EPS) + ADAM_WD * w_ref[...])
        nm_ref[...] = nm
        nv_ref[...] = nv

    spec = _rows(cdim, tr)
    return pl.pallas_call(body, name=name, grid=(r // tr,), in_specs=[spec] * 4, out_specs=[spec] * 3,
                          out_shape=[jax.ShapeDtypeStruct((r, cdim), F32)] * 3,
                          compiler_params=_cparams("parallel"))(w, g, m, v)


def _all_gather_weights(shards, tiny, kc):
    wb = [_cast_into_full(shards[n], kc, BIG[n][1], "cast_" + n) for n in BIG_NAMES]
    nbig = len(BIG_NAMES)

    def region(i, chip, c=None):
        half_axis, shard_axis = BIG[BIG_NAMES[i]]
        sr, sc = shards[BIG_NAMES[i]].shape
        ssize = (sr, sc)[shard_axis]
        hsize = (sr, sc)[half_axis] // 2

        def f(ref):
            v = _view(ref, shard_axis, chip * ssize, ssize)
            return v if c is None else _view(v, half_axis, c * hsize, hsize)
        return f

    local, remote = [], []
    for i in range(nbig):
        for flip in FLIPS:
            remote.append((lambda I, O, pos, i=i: region(i, pos[3], pos[2])(I[i]),
                           lambda I, O, pos, i=i: region(i, pos[3], pos[2])(O[i]), flip))
    local.append((lambda I, O, pos: I[nbig], lambda I, O, pos: O[nbig].at[pos[3]]))
    for flip in FLIPS:
        remote.append((lambda I, O, pos: I[nbig], lambda I, O, pos: O[nbig].at[pos[3]], flip))
    outs = [jax.ShapeDtypeStruct(a.shape, BF16) for a in wb] + [jax.ShapeDtypeStruct((4,) + tiny.shape, F32)]
    gathered = _exchange("gather_ici", wb + [tiny], outs, {i: i for i in range(nbig)}, local, remote)

    remote = []
    for i in range(nbig):
        for flip in FLIPS:
            reg = lambda I, O, pos, i=i, flip=flip, src=True: region(i, _peer_chip(pos, flip), pos[2])(I[i] if src else O[i])
            remote.append((functools.partial(reg, src=True), functools.partial(reg, src=False), "c"))
    full = _exchange("gather_d2d", list(gathered[:nbig]), outs[:nbig], {i: i for i in range(nbig)}, [], remote)
    return dict(zip(BIG_NAMES, full)), gathered[nbig]


def _reduce_gradients(grads_big, pack, kc):
    nbig = len(BIG_NAMES)
    gs = [grads_big[n] for n in BIG_NAMES]

    def half_shape(n):
        r, cdim = grads_big[n].shape
        return (r // 2, cdim) if BIG[n][0] == 0 else (r, cdim // 2)

    def sub_shape(n):
        hr, hc = half_shape(n)
        return (hr, hc // 4) if BIG[n][1] == 1 else (hr // 4, hc)

    def half_of(i, c_of):
        half_axis = BIG[BIG_NAMES[i]][0]
        hsize = gs[i].shape[half_axis] // 2
        return lambda ref, pos: _view(ref, half_axis, c_of(pos) * hsize, hsize)

    remote = [(lambda I, O, pos, i=i: half_of(i, lambda p: 1 - p[2])(I[i], pos), lambda I, O, pos, i=i: O[i], "c")
              for i in range(nbig)]
    remote.append((lambda I, O, pos: I[nbig], lambda I, O, pos: O[nbig], "c"))
    outs = [jax.ShapeDtypeStruct(half_shape(n), F32) for n in BIG_NAMES] + [jax.ShapeDtypeStruct(pack.shape, F32)]
    recv_a = _exchange("reduce_d2d", gs + [pack], outs, {}, [], remote)
    chip = [_pair_sum(gs[i], recv_a[i], kc, BIG[n][0], "pair_sum_" + n, BF16) for i, n in enumerate(BIG_NAMES)]
    chip_pack = _pair_sum(pack, recv_a[nbig], kc, None, "pair_sum_pack", F32)

    prow = pack.shape[0] // 2

    def sub_of(i, chip_of):
        shard_axis = BIG[BIG_NAMES[i]][1]
        ssize = sub_shape(BIG_NAMES[i])[shard_axis]
        return lambda ref, pos: _view(ref, shard_axis, chip_of(pos) * ssize, ssize)

    remote = []
    for i in range(nbig):
        for slot, flip in enumerate(FLIPS):
            remote.append((lambda I, O, pos, i=i, flip=flip: sub_of(i, lambda p: _peer_chip(p, flip))(I[i], pos),
                           lambda I, O, pos, i=i, slot=slot: O[i].at[slot], flip))
    for slot, flip in enumerate(FLIPS):
        remote.append((lambda I, O, pos: _view(I[nbig], 0, pos[2] * prow, prow),
                       lambda I, O, pos, slot=slot: O[nbig].at[slot], flip))
    outs = ([jax.ShapeDtypeStruct((3,) + sub_shape(n), BF16) for n in BIG_NAMES]
            + [jax.ShapeDtypeStruct((3, prow, pack.shape[1]), F32)])
    recv_b = _exchange("reduce_ici", chip + [chip_pack], outs, {}, [], remote)
    total = [_chip_sum(chip[i], recv_b[i], kc, BIG[n][1], BIG[n][0], "chip_sum_" + n)
             for i, n in enumerate(BIG_NAMES)]
    total.append(_chip_sum(chip_pack, recv_b[nbig], kc, None, 0, "chip_sum_pack"))

    def my_half(i, ref, pos):
        half_axis = BIG[BIG_NAMES[i]][0] if i < nbig else 0
        hsize = ref.shape[half_axis] // 2
        return _view(ref, half_axis, pos[2] * hsize, hsize)

    remote = [(lambda I, O, pos, i=i: my_half(i, I[i], pos), lambda I, O, pos, i=i: my_half(i, O[i], pos), "c")
              for i in range(nbig + 1)]
    outs = [jax.ShapeDtypeStruct(t.shape, F32) for t in total]
    out = _exchange("swap_d2d", total, outs, {i: i for i in range(nbig + 1)}, [], remote)
    return dict(zip(BIG_NAMES, out[:nbig])), out[nbig]


WEIGHTS = ("meta_tokens", "norm_mix_g", "w_in", "conv_w", "ssm_lam_re", "ssm_lam_im", "ssm_log_dt", "ssm_b_re",
           "ssm_b_im", "ssm_c_re", "ssm_c_im", "ssm_d", "ssm_w_glu", "gain_conv_out", "gain_ssm_out", "w_out",
           "norm_ffn_g", "w_up", "ffn_conv_w", "ffn_conv_b", "w_down", "norm_final_g")
TINY_SHARDED = ("meta_tokens", "conv_w", "ffn_conv_w")
REPLICATED = tuple(n for n in WEIGHTS if n not in BIG and n not in TINY_SHARDED)
PACK_COLS = 512


def _pack(arrays, row_mult, cols):
    flat = jnp.concatenate([a.reshape(-1).astype(F32) for a in arrays])
    n = flat.shape[0]
    total = -(-n // (row_mult * cols)) * (row_mult * cols)
    return jnp.concatenate([flat, jnp.zeros((total - n,), F32)]).reshape(total // cols, cols)


def _unpack(packed, shapes):
    flat = packed.reshape(-1)
    out, off = [], 0
    for s in shapes:
        n = math.prod(s)
        out.append(flat[off:off + n].reshape(s))
        off += n
    return out


def kernel(x, meta_tokens, norm_mix_g, w_in, conv_w, ssm_lam_re, ssm_lam_im, ssm_log_dt, ssm_b_re, ssm_b_im, ssm_c_re, ssm_c_im, ssm_d, ssm_w_glu, gain_conv_out, gain_ssm_out, w_out, norm_ffn_g, w_up, ffn_conv_w, ffn_conv_b, w_down, norm_final_g, loss_target, m_meta_tokens, m_norm_mix_g, m_w_in, m_conv_w, m_ssm_lam_re, m_ssm_lam_im, m_ssm_log_dt, m_ssm_b_re, m_ssm_b_im, m_ssm_c_re, m_ssm_c_im, m_ssm_d, m_ssm_w_glu, m_gain_conv_out, m_gain_ssm_out, m_w_out, m_norm_ffn_g, m_w_up, m_ffn_conv_w, m_ffn_conv_b, m_w_down, m_norm_final_g, v_meta_tokens, v_norm_mix_g, v_w_in, v_conv_w, v_ssm_lam_re, v_ssm_lam_im, v_ssm_log_dt, v_ssm_b_re, v_ssm_b_im, v_ssm_c_re, v_ssm_c_im, v_ssm_d, v_ssm_w_glu, v_gain_conv_out, v_gain_ssm_out, v_w_out, v_norm_ffn_g, v_w_up, v_ffn_conv_w, v_ffn_conv_b, v_w_down, v_norm_final_g):
    args = dict(locals())
    w = {n: args[n] for n in WEIGHTS}
    mom = {n: args["m_" + n] for n in WEIGHTS}
    var = {n: args["v_" + n] for n in WEIGHTS}
    kx, ky, kc_ = lax.axis_index("x"), lax.axis_index("y"), lax.axis_index("c")
    chip = 2 * kx + ky
    kc = jnp.stack([chip, kc_]).astype(jnp.int32)

    def squeeze(n, a):
        if n == "meta_tokens":
            return a
        if n == "norm_final_g":
            return a.reshape(1, -1)
        a = a[0]
        return a.reshape(1, -1) if a.ndim == 1 else a

    wl = {n: squeeze(n, w[n]) for n in WEIGHTS}
    ml = {n: squeeze(n, mom[n]) for n in WEIGHTS}
    vl = {n: squeeze(n, var[n]) for n in WEIGHTS}

    tiny = _pack([wl[n] for n in TINY_SHARDED], SUBLANES, LANES)
    full, tiny_all = _all_gather_weights({n: wl[n] for n in BIG_NAMES}, tiny, kc)
    tiny_shapes = [wl[n].shape for n in TINY_SHARDED]
    tiny_parts = [_unpack(tiny_all[k], tiny_shapes) for k in range(4)]
    p = dict(wl)
    p.update(full)
    for j, n in enumerate(TINY_SHARDED):
        p[n] = jnp.concatenate([tiny_parts[k][j] for k in range(4)], axis=1)
    p["ssm_log_dt"] = wl["ssm_log_dt"].reshape(-1)

    loss_local, grad_x, grads = _local_step(x[0], loss_target[0], p)
    loss = lax.psum(loss_local, ("x", "y", "c"))

    small_names = REPLICATED + TINY_SHARDED
    small_shapes = [tuple(grads[n].shape) for n in small_names]
    pack = _pack([grads[n] for n in small_names], 2 * 16, PACK_COLS)
    g_big, g_pack = _reduce_gradients({n: grads[n] for n in BIG_NAMES}, pack, kc)
    g_small = dict(zip(small_names, _unpack(g_pack, small_shapes)))
    g = dict(g_big)
    for n in REPLICATED:
        g[n] = g_small[n].reshape(wl[n].shape)
    for n in TINY_SHARDED:
        cols = wl[n].shape[1]
        g[n] = lax.dynamic_slice_in_dim(g_small[n], chip * cols, cols, axis=1)

    delta, new_m, new_v = {}, {}, {}
    for n in BIG_NAMES:
        delta[n], new_m[n], new_v[n] = _adamw(wl[n], g[n], ml[n], vl[n], "adamw_" + n)
    packs = [_pack([d[n] for n in small_names], SUBLANES, PACK_COLS) for d in (wl, g, ml, vl)]
    shapes = [wl[n].shape for n in small_names]
    for d, packed in zip((delta, new_m, new_v), _adamw(*packs, "adamw_small")):
        d.update(zip(small_names, _unpack(packed, shapes)))

    def like(n, a):
        return a.reshape(w[n].shape)

    return (loss, grad_x[None], *[like(n, g[n]) for n in WEIGHTS], *[like(n, delta[n]) for n in WEIGHTS],
            *[like(n, new_m[n]) for n in WEIGHTS], *[like(n, new_v[n]) for n in WEIGHTS])
```

```python
import functools
import math

import jax
import jax.numpy as jnp
from jax import lax
from jax.experimental import pallas as pl
from jax.experimental.pallas import tpu as pltpu

F32 = jnp.float32
BF16 = jnp.bfloat16
MESH = pl.DeviceIdType.MESH

N_META = 16
N_GROUPS = 32
GROUP = 16
STATE = 64
RMS_EPS = 1e-6
ADAM_LR = 0.001
ADAM_B1 = 0.9
ADAM_B2 = 0.999
ADAM_EPS = 1e-08
ADAM_WD = 0.01
ADAM_STEP = 10

LANES = 128
SUBLANES = 8
ROW_ALIGN = 128
ROW_TILES = 4
VMEM_LIMIT = 52 * 1024 * 1024
GELU_C = math.sqrt(2.0 / math.pi)
GELU_A = 0.044715


def _cparams(*sem):
    return pltpu.CompilerParams(dimension_semantics=sem, vmem_limit_bytes=VMEM_LIMIT)


def _pick_tile(dim, cap, mult):
    best = None
    for t in range(mult, min(dim, cap) + 1, mult):
        if dim % t == 0:
            best = t
    return best if best is not None else dim


def _mm(a, b, mode, name, out_dtype=F32, acc_in=None):
    if mode == "tn":
        kdim, m = a.shape
    else:
        m, kdim = a.shape
    n = b.shape[0] if mode == "nt" else b.shape[1]
    tm = _pick_tile(m, 1408, LANES if mode == "tn" else 16)
    tn = _pick_tile(n, 512, LANES)
    tk = _pick_tile(kdim, 2816, LANES)
    nk = kdim // tk
    has_acc = acc_in is not None

    def body(*refs):
        if has_acc:
            a_ref, b_ref, c_ref, o_ref = refs[:4]
            rest = refs[4:]
        else:
            a_ref, b_ref, o_ref = refs[:3]
            c_ref = None
            rest = refs[3:]
        if mode == "nn":
            p = jnp.dot(a_ref[...], b_ref[...], preferred_element_type=F32)
        elif mode == "nt":
            p = lax.dot_general(a_ref[...], b_ref[...], (((1,), (1,)), ((), ())), preferred_element_type=F32)
        else:
            p = lax.dot_general(a_ref[...], b_ref[...], (((0,), (0,)), ((), ())), preferred_element_type=F32)
        if nk == 1:
            if has_acc:
                p = p + c_ref[...]
            o_ref[...] = p.astype(out_dtype)
        else:
            acc_ref = rest[0]
            k = pl.program_id(2)

            @pl.when(k == 0)
            def _():
                acc_ref[...] = p + c_ref[...] if has_acc else p

            @pl.when(k > 0)
            def _():
                acc_ref[...] += p

            @pl.when(k == nk - 1)
            def _():
                o_ref[...] = acc_ref[...].astype(out_dtype)

    if mode == "tn":
        a_spec = pl.BlockSpec((tk, tm), lambda i, j, k: (k, i))
    else:
        a_spec = pl.BlockSpec((tm, tk), lambda i, j, k: (i, k))
    if mode == "nt":
        b_spec = pl.BlockSpec((tn, tk), lambda i, j, k: (j, k))
    else:
        b_spec = pl.BlockSpec((tk, tn), lambda i, j, k: (k, j))
    o_spec = pl.BlockSpec((tm, tn), lambda i, j, k: (i, j))
    in_specs = [a_spec, b_spec] + ([o_spec] if has_acc else [])
    args = (a, b) + ((acc_in,) if has_acc else ())
    return pl.pallas_call(
        body, name=name, grid=(m // tm, n // tn, nk),
        in_specs=in_specs, out_specs=o_spec,
        out_shape=jax.ShapeDtypeStruct((m, n), out_dtype),
        scratch_shapes=[pltpu.VMEM((tm, tn), F32)] if nk > 1 else [],
        compiler_params=_cparams("parallel", "parallel", "arbitrary"),
    )(*args)


def _rows(shape_cols, tr, dtype=None):
    return pl.BlockSpec((tr, shape_cols), lambda i: (i, 0))


def _const(shape):
    return pl.BlockSpec(shape, lambda i: (0,) * len(shape))


def _rms(x):
    return lax.rsqrt(jnp.mean(x * x, axis=-1, keepdims=True) + RMS_EPS)


def _rms_bwd(x, r, g, dy):
    xn = x * r
    dxn = dy * g
    dx = r * (dxn - xn * jnp.mean(dxn * xn, axis=-1, keepdims=True))
    return dx, dy * xn


def _gelu(y):
    return 0.5 * y * (1.0 + jnp.tanh(GELU_C * (y + GELU_A * y * y * y)))


def _gelu_grad(y):
    t = jnp.tanh(GELU_C * (y + GELU_A * y * y * y))
    return 0.5 * (1.0 + t) + 0.5 * y * (1.0 - t * t) * GELU_C * (1.0 + 3.0 * GELU_A * y * y)


def _sigmoid(z):
    return 1.0 / (1.0 + jnp.exp(-z))


def _norm_fwd(h, g, name, res=None):
    tp, d = h.shape
    tr = tp // ROW_TILES
    has_res = res is not None

    def body(*refs):
        if has_res:
            h_ref, r_ref, g_ref, s_ref, hn_ref = refs
            x = h_ref[...] + r_ref[...]
            s_ref[...] = x
        else:
            h_ref, g_ref, hn_ref = refs
            x = h_ref[...]
        hn_ref[...] = (x * _rms(x) * g_ref[...]).astype(BF16)

    in_specs = [_rows(d, tr)] + ([_rows(d, tr)] if has_res else []) + [_const((1, d))]
    out_specs = ([_rows(d, tr)] if has_res else []) + [_rows(d, tr)]
    out_shape = ([jax.ShapeDtypeStruct((tp, d), F32)] if has_res else []) + [jax.ShapeDtypeStruct((tp, d), BF16)]
    args = (h,) + ((res,) if has_res else ()) + (g,)
    out = pl.pallas_call(body, name=name, grid=(ROW_TILES,), in_specs=in_specs, out_specs=out_specs,
                         out_shape=out_shape, compiler_params=_cparams("parallel"))(*args)
    return out if has_res else out[0]


def _norm_bwd(h, g, dhn, dres, name):
    tp, d = h.shape
    tr = tp // ROW_TILES

    def body(h_ref, g_ref, dhn_ref, dres_ref, dh_ref, dhb_ref, dg_ref):
        x = h_ref[...]
        dx, dgs = _rms_bwd(x, _rms(x), g_ref[...], dhn_ref[...])
        dh = dres_ref[...] + dx
        dh_ref[...] = dh
        dhb_ref[...] = dh.astype(BF16)

        @pl.when(pl.program_id(0) == 0)
        def _():
            dg_ref[...] = jnp.zeros_like(dg_ref)

        dg_ref[...] += jnp.sum(dgs, axis=0, keepdims=True)

    return pl.pallas_call(
        body, name=name, grid=(ROW_TILES,),
        in_specs=[_rows(d, tr), _const((1, d)), _rows(d, tr), _rows(d, tr)],
        out_specs=[_rows(d, tr), _rows(d, tr), _const((1, d))],
        out_shape=[jax.ShapeDtypeStruct((tp, d), F32), jax.ShapeDtypeStruct((tp, d), BF16),
                   jax.ShapeDtypeStruct((1, d), F32)],
        compiler_params=_cparams("arbitrary"))(h, g, dhn, dres)


def _loss_bwd(h1, dn, tgt, g, n_real, name):
    tp, d = h1.shape
    tr = tp // ROW_TILES

    def body(h1_ref, dn_ref, t_ref, g_ref, loss_ref, dh_ref, dhb_ref, dg_ref):
        i = pl.program_id(0)
        x = h1_ref[...] + dn_ref[...]
        r = _rms(x)
        row = i * tr + lax.broadcasted_iota(jnp.int32, (tr, d), 0)
        valid = (row >= N_META) & (row < n_real)
        e = jnp.where(valid, x * r * g_ref[...] - t_ref[...], 0.0)
        dx, dgs = _rms_bwd(x, r, g_ref[...], e * (1.0 / d))
        dh_ref[...] = dx
        dhb_ref[...] = dx.astype(BF16)

        @pl.when(i == 0)
        def _():
            dg_ref[...] = jnp.zeros_like(dg_ref)
            loss_ref[...] = jnp.zeros_like(loss_ref)

        dg_ref[...] += jnp.sum(dgs, axis=0, keepdims=True)
        loss_ref[...] += (0.5 / d) * jnp.sum(jnp.sum(e * e, axis=0, keepdims=True), axis=1, keepdims=True)

    return pl.pallas_call(
        body, name=name, grid=(ROW_TILES,),
        in_specs=[_rows(d, tr), _rows(d, tr), _rows(d, tr), _const((1, d))],
        out_specs=[_const((1, LANES)), _rows(d, tr), _rows(d, tr), _const((1, d))],
        out_shape=[jax.ShapeDtypeStruct((1, LANES), F32), jax.ShapeDtypeStruct((tp, d), F32),
                   jax.ShapeDtypeStruct((tp, d), BF16), jax.ShapeDtypeStruct((1, d), F32)],
        compiler_params=_cparams("arbitrary"))(h1, dn, tgt, g)


def _mix_fwd(co, y, z, gc, gs, name):
    tp, dh = co.shape
    tr = tp // ROW_TILES

    def body(co_ref, y_ref, z_ref, gc_ref, gs_ref, m_ref):
        c = co_ref[...]
        m_ref[:, :dh] = (c * _rms(c) * gc_ref[...]).astype(BF16)
        so = _gelu(y_ref[...]) * _sigmoid(z_ref[...])
        m_ref[:, dh:] = (so * _rms(so) * gs_ref[...]).astype(BF16)

    return pl.pallas_call(
        body, name=name, grid=(ROW_TILES,),
        in_specs=[_rows(dh, tr)] * 3 + [_const((1, dh))] * 2,
        out_specs=_rows(2 * dh, tr),
        out_shape=jax.ShapeDtypeStruct((tp, 2 * dh), BF16),
        compiler_params=_cparams("parallel"))(co, y, z, gc, gs)


def _mix_bwd(dm, co, y, z, gc, gs, name):
    tp, dh = co.shape
    tr = tp // ROW_TILES

    def body(dm_ref, co_ref, y_ref, z_ref, gc_ref, gs_ref, dco_ref, dz_ref, dgp_ref, dgc_ref, dgs_ref):
        c = co_ref[...]
        dco, dgc = _rms_bwd(c, _rms(c), gc_ref[...], dm_ref[:, :dh])
        dco_ref[...] = dco
        gl = _gelu(y_ref[...])
        sg = _sigmoid(z_ref[...])
        so = gl * sg
        dso, dgs = _rms_bwd(so, _rms(so), gs_ref[...], dm_ref[:, dh:])
        dz_ref[...] = (dso * gl * sg * (1.0 - sg)).astype(BF16)
        dgp_ref[...] = dso * sg

        @pl.when(pl.program_id(0) == 0)
        def _():
            dgc_ref[...] = jnp.zeros_like(dgc_ref)
            dgs_ref[...] = jnp.zeros_like(dgs_ref)

        dgc_ref[...] += jnp.sum(dgc, axis=0, keepdims=True)
        dgs_ref[...] += jnp.sum(dgs, axis=0, keepdims=True)

    return pl.pallas_call(
        body, name=name, grid=(ROW_TILES,),
        in_specs=[_rows(2 * dh, tr)] + [_rows(dh, tr)] * 3 + [_const((1, dh))] * 2,
        out_specs=[_rows(dh, tr), _rows(dh, tr), _rows(dh, tr), _const((1, dh)), _const((1, dh))],
        out_shape=[jax.ShapeDtypeStruct((tp, dh), F32), jax.ShapeDtypeStruct((tp, dh), BF16),
                   jax.ShapeDtypeStruct((tp, dh), F32), jax.ShapeDtypeStruct((1, dh), F32),
                   jax.ShapeDtypeStruct((1, dh), F32)],
        compiler_params=_cparams("arbitrary"))(dm, co, y, z, gc, gs)


def _shift_down(x, k):
    row = lax.broadcasted_iota(jnp.int32, x.shape, 0)
    return jnp.where(row >= k, pltpu.roll(x, k, 0), 0.0)


def _shift_up(x, k):
    n = x.shape[0]
    row = lax.broadcasted_iota(jnp.int32, x.shape, 0)
    return jnp.where(row < n - k, pltpu.roll(x, n - k, 0), 0.0)


def _dwconv(x, w_ref):
    return w_ref[2:3, :] * x + w_ref[1:2, :] * _shift_down(x, 1) + w_ref[0:1, :] * _shift_down(x, 2)


def _dwconv_bwd(x, dy, w_ref):
    dx = w_ref[2:3, :] * dy + w_ref[1:2, :] * _shift_up(dy, 1) + w_ref[0:1, :] * _shift_up(dy, 2)
    dw = jnp.concatenate([jnp.sum(dy * _shift_down(x, 2), axis=0, keepdims=True),
                          jnp.sum(dy * _shift_down(x, 1), axis=0, keepdims=True),
                          jnp.sum(dy * x, axis=0, keepdims=True)], axis=0)
    return dx, dw


def _scan(s_re, s_im, tab_ref, reverse):
    n_chunks = s_re.shape[0] // SUBLANES
    n_strips = s_re.shape[1] // LANES
    last = 0 if reverse else SUBLANES - 1

    def body(i, carry):
        chunk = (n_chunks - 1 - i) if reverse else i
        r0 = pl.multiple_of(chunk * SUBLANES, SUBLANES)
        out = []
        for st in range(n_strips):
            lanes = slice(st * LANES, (st + 1) * LANES)
            cr, ci = carry[2 * st], carry[2 * st + 1]
            xr = s_re[pl.ds(r0, SUBLANES), lanes]
            xi = s_im[pl.ds(r0, SUBLANES), lanes]
            for level, k in enumerate((1, 2, 4)):
                mr = tab_ref[2 * level, :, lanes]
                mi = tab_ref[2 * level + 1, :, lanes]
                sh = SUBLANES - k if reverse else k
                rr = pltpu.roll(xr, sh, 0)
                ri = pltpu.roll(xi, sh, 0)
                xr, xi = xr + (mr * rr - mi * ri), xi + (mr * ri + mi * rr)
            pwr = tab_ref[6, :, lanes]
            pwi = tab_ref[7, :, lanes]
            xr, xi = xr + (pwr * cr - pwi * ci), xi + (pwr * ci + pwi * cr)
            s_re[pl.ds(r0, SUBLANES), lanes] = xr
            s_im[pl.ds(r0, SUBLANES), lanes] = xi
            out.append(jnp.broadcast_to(xr[last:last + 1, :], (SUBLANES, LANES)))
            out.append(jnp.broadcast_to(xi[last:last + 1, :], (SUBLANES, LANES)))
        return tuple(out)

    zero = jnp.zeros((SUBLANES, LANES), F32)
    lax.fori_loop(0, n_chunks, body, (zero,) * (2 * n_strips))


def _seq_fwd(proj, conv_w, bc_re, bc_im, cc_re, cc_im, dskip, tab_f, name):
    tp = proj.shape[0]
    dh = proj.shape[1] // 4
    nq = dh // LANES
    sw = STATE * N_GROUPS // nq

    def body(b_ref, c_ref, v_ref, u_ref, w_ref, bre_ref, bim_ref, cre_ref, cim_ref, d_ref, tab_ref,
             co_ref, y_ref, g_ref, s_re, s_im):
        co_ref[...] = b_ref[...] * _dwconv(c_ref[...] * v_ref[...], w_ref)
        u = u_ref[...]
        ub = u.astype(BF16)
        s_re[...] = jnp.dot(ub, bre_ref[...], preferred_element_type=F32)
        s_im[...] = jnp.dot(ub, bim_ref[...], preferred_element_type=F32)
        _scan(s_re, s_im, tab_ref, False)
        y = (jnp.dot(s_re[...].astype(BF16), cre_ref[...], preferred_element_type=F32)
             - jnp.dot(s_im[...].astype(BF16), cim_ref[...], preferred_element_type=F32)
             + d_ref[...] * u)
        y_ref[...] = y
        g_ref[...] = _gelu(y).astype(BF16)

    col = lambda off: pl.BlockSpec((tp, LANES), lambda q, off=off: (0, off * nq + q))
    blk = pl.BlockSpec((tp, LANES), lambda q: (0, q))
    return pl.pallas_call(
        body, name=name, grid=(nq,),
        in_specs=[col(0), col(1), col(2), col(3),
                  pl.BlockSpec((3, LANES), lambda q: (0, q)),
                  pl.BlockSpec((LANES, sw), lambda q: (0, q)), pl.BlockSpec((LANES, sw), lambda q: (0, q)),
                  pl.BlockSpec((sw, LANES), lambda q: (q, 0)), pl.BlockSpec((sw, LANES), lambda q: (q, 0)),
                  pl.BlockSpec((1, LANES), lambda q: (0, q)),
                  pl.BlockSpec((8, SUBLANES, sw), lambda q: (0, 0, q))],
        out_specs=[blk, blk, blk],
        out_shape=[jax.ShapeDtypeStruct((tp, dh), F32), jax.ShapeDtypeStruct((tp, dh), F32),
                   jax.ShapeDtypeStruct((tp, dh), BF16)],
        scratch_shapes=[pltpu.VMEM((tp, sw), F32), pltpu.VMEM((tp, sw), F32)],
        compiler_params=_cparams("parallel"),
    )(proj, proj, proj, proj, conv_w, bc_re, bc_im, cc_re, cc_im, dskip, tab_f)


def _conv_bwd(proj, dco, conv_w, name):
    tp = proj.shape[0]
    dh = proj.shape[1] // 4
    nq = dh // LANES

    def body(b_ref, c_ref, v_ref, dco_ref, w_ref, dproj_ref, dw_ref, stage, sem):
        q = pl.program_id(0)
        cg = c_ref[...]
        vg = v_ref[...]
        cv = cg * vg
        dco_v = dco_ref[...]
        dcv, dw = _dwconv_bwd(cv, dco_v * b_ref[...], w_ref)
        dw_ref[...] = dw
        stage[0] = (dco_v * _dwconv(cv, w_ref)).astype(BF16)
        stage[1] = (dcv * vg).astype(BF16)
        stage[2] = (dcv * cg).astype(BF16)
        copies = [pltpu.make_async_copy(stage.at[p], dproj_ref.at[:, pl.ds((p * nq + q) * LANES, LANES)], sem.at[p])
                  for p in range(3)]
        for cp in copies:
            cp.start()
        for cp in copies:
            cp.wait()

    col = lambda off: pl.BlockSpec((tp, LANES), lambda q, off=off: (0, off * nq + q))
    return pl.pallas_call(
        body, name=name, grid=(nq,),
        in_specs=[col(0), col(1), col(2), pl.BlockSpec((tp, LANES), lambda q: (0, q)),
                  pl.BlockSpec((3, LANES), lambda q: (0, q))],
        out_specs=[pl.BlockSpec(memory_space=pl.ANY), pl.BlockSpec((3, LANES), lambda q: (0, q))],
        out_shape=[jax.ShapeDtypeStruct((tp, 4 * dh), BF16), jax.ShapeDtypeStruct((3, dh), F32)],
        scratch_shapes=[pltpu.VMEM((3, tp, LANES), BF16), pltpu.SemaphoreType.DMA((3,))],
        compiler_params=_cparams("arbitrary"),
    )(proj, proj, proj, dco, conv_w)


def _ssm_bwd(proj, y, dg, dproj, bc_re, bc_im, cc_re, cc_im, dskip, tab_f, tab_r, name):
    tp = proj.shape[0]
    dh = proj.shape[1] // 4
    nq = dh // LANES
    sw = STATE * N_GROUPS // nq

    def body(u_ref, y_ref, dg_ref, dproj_in, bre_ref, bim_ref, cre_ref, cim_ref, d_ref, tabf_ref, tabr_ref,
             dproj_ref, dbre_ref, dbim_ref, dcre_ref, dcim_ref, dd_ref, dar_ref, dai_ref,
             s_re, s_im, l_re, l_im, stage, sem):
        del dproj_in
        q = pl.program_id(0)
        nt = (((1,), (1,)), ((), ()))
        tn = (((0,), (0,)), ((), ()))
        u = u_ref[...]
        ub = u.astype(BF16)
        s_re[...] = jnp.dot(ub, bre_ref[...], preferred_element_type=F32)
        s_im[...] = jnp.dot(ub, bim_ref[...], preferred_element_type=F32)
        _scan(s_re, s_im, tabf_ref, False)
        dy = dg_ref[...] * _gelu_grad(y_ref[...])
        dyb = dy.astype(BF16)
        dd_ref[...] = jnp.sum(dy * u, axis=0, keepdims=True)
        l_re[...] = lax.dot_general(dyb, cre_ref[...], nt, preferred_element_type=F32)
        l_im[...] = -lax.dot_general(dyb, cim_ref[...], nt, preferred_element_type=F32)
        dcre_ref[...] = lax.dot_general(s_re[...].astype(BF16), dyb, tn, preferred_element_type=F32)
        dcim_ref[...] = -lax.dot_general(s_im[...].astype(BF16), dyb, tn, preferred_element_type=F32)
        _scan(l_re, l_im, tabr_ref, True)
        for st in range(sw // LANES):
            lanes = slice(st * LANES, (st + 1) * LANES)
            lr = l_re[:, lanes]
            li = l_im[:, lanes]
            pr = _shift_down(s_re[:, lanes], 1)
            pi = _shift_down(s_im[:, lanes], 1)
            dar_ref[:, lanes] = jnp.sum(lr * pr + li * pi, axis=0, keepdims=True)
            dai_ref[:, lanes] = jnp.sum(li * pr - lr * pi, axis=0, keepdims=True)
        lrb = l_re[...].astype(BF16)
        lib = l_im[...].astype(BF16)
        du = (dy * d_ref[...] + lax.dot_general(lrb, bre_ref[...], nt, preferred_element_type=F32)
              + lax.dot_general(lib, bim_ref[...], nt, preferred_element_type=F32))
        stage[...] = du.astype(BF16)
        dbre_ref[...] = lax.dot_general(ub, lrb, tn, preferred_element_type=F32)
        dbim_ref[...] = lax.dot_general(ub, lib, tn, preferred_element_type=F32)
        cp = pltpu.make_async_copy(stage, dproj_ref.at[:, pl.ds((3 * nq + q) * LANES, LANES)], sem)
        cp.start()
        cp.wait()

    blk = pl.BlockSpec((tp, LANES), lambda q: (0, q))
    bspec = pl.BlockSpec((LANES, sw), lambda q: (0, q))
    cspec = pl.BlockSpec((sw, LANES), lambda q: (q, 0))
    tspec = pl.BlockSpec((8, SUBLANES, sw), lambda q: (0, 0, q))
    nstate = STATE * N_GROUPS
    return pl.pallas_call(
        body, name=name, grid=(nq,),
        in_specs=[pl.BlockSpec((tp, LANES), lambda q: (0, 3 * nq + q)), blk, blk, pl.BlockSpec(memory_space=pl.ANY),
                  bspec, bspec, cspec, cspec, pl.BlockSpec((1, LANES), lambda q: (0, q)), tspec, tspec],
        out_specs=[pl.BlockSpec(memory_space=pl.ANY), bspec, bspec, cspec, cspec,
                   pl.BlockSpec((1, LANES), lambda q: (0, q)),
                   pl.BlockSpec((1, sw), lambda q: (0, q)), pl.BlockSpec((1, sw), lambda q: (0, q))],
        out_shape=[jax.ShapeDtypeStruct((tp, 4 * dh), BF16),
                   jax.ShapeDtypeStruct((LANES, nstate), F32), jax.ShapeDtypeStruct((LANES, nstate), F32),
                   jax.ShapeDtypeStruct((nstate, LANES), F32), jax.ShapeDtypeStruct((nstate, LANES), F32),
                   jax.ShapeDtypeStruct((1, dh), F32),
                   jax.ShapeDtypeStruct((1, nstate), F32), jax.ShapeDtypeStruct((1, nstate), F32)],
        input_output_aliases={3: 0},
        scratch_shapes=[pltpu.VMEM((tp, sw), F32)] * 4 + [pltpu.VMEM((tp, LANES), BF16), pltpu.SemaphoreType.DMA],
        compiler_params=_cparams("arbitrary"),
    )(proj, y, dg, dproj, bc_re, bc_im, cc_re, cc_im, dskip, tab_f, tab_r)


FFN_TILE = 256


def _ffn_act(up, fw, fb, name):
    tp, two_ff = up.shape
    dff = two_ff // 2
    tc = FFN_TILE
    nj = dff // tc

    def body(ua_ref, uv_ref, wa_ref, wv_ref, ba_ref, bv_ref, act_ref):
        a = _dwconv(ua_ref[...], wa_ref) + ba_ref[...]
        v = _dwconv(uv_ref[...], wv_ref) + bv_ref[...]
        act_ref[...] = (a * _sigmoid(a) * v).astype(BF16)

    lo = lambda r: pl.BlockSpec((r, tc), lambda j: (0, j))
    hi = lambda r: pl.BlockSpec((r, tc), lambda j: (0, nj + j))
    return pl.pallas_call(
        body, name=name, grid=(nj,),
        in_specs=[lo(tp), hi(tp), lo(3), hi(3), lo(1), hi(1)],
        out_specs=lo(tp),
        out_shape=jax.ShapeDtypeStruct((tp, dff), BF16),
        compiler_params=_cparams("parallel"))(up, up, fw, fw, fb, fb)


def _ffn_bwd(up, dact, fw, fb, name):
    tp, two_ff = up.shape
    dff = two_ff // 2
    tc = FFN_TILE
    nj = dff // tc

    def body(ua_ref, uv_ref, da_ref, wa_ref, wv_ref, ba_ref, bv_ref,
             dup_ref, dwa_ref, dwv_ref, dba_ref, dbv_ref, stage, sem):
        j = pl.program_id(0)
        ua = ua_ref[...]
        uv = uv_ref[...]
        a = _dwconv(ua, wa_ref) + ba_ref[...]
        v = _dwconv(uv, wv_ref) + bv_ref[...]
        sg = _sigmoid(a)
        dact_v = da_ref[...]
        da = dact_v * v * sg * (1.0 + a * (1.0 - sg))
        dv = dact_v * a * sg
        dba_ref[...] = jnp.sum(da, axis=0, keepdims=True)
        dbv_ref[...] = jnp.sum(dv, axis=0, keepdims=True)
        dua, dwa = _dwconv_bwd(ua, da, wa_ref)
        duv, dwv = _dwconv_bwd(uv, dv, wv_ref)
        dwa_ref[...] = dwa
        dwv_ref[...] = dwv
        stage[0] = dua.astype(BF16)
        stage[1] = duv.astype(BF16)
        copies = [pltpu.make_async_copy(stage.at[p], dup_ref.at[:, pl.ds((p * nj + j) * tc, tc)], sem.at[p])
                  for p in range(2)]
        for cp in copies:
            cp.start()
        for cp in copies:
            cp.wait()

    lo = lambda r: pl.BlockSpec((r, tc), lambda j: (0, j))
    hi = lambda r: pl.BlockSpec((r, tc), lambda j: (0, nj + j))
    return pl.pallas_call(
        body, name=name, grid=(nj,),
        in_specs=[lo(tp), hi(tp), lo(tp), lo(3), hi(3), lo(1), hi(1)],
        out_specs=[pl.BlockSpec(memory_space=pl.ANY), lo(3), lo(3), lo(1), lo(1)],
        out_shape=[jax.ShapeDtypeStruct((tp, two_ff), BF16),
                   jax.ShapeDtypeStruct((3, dff), F32), jax.ShapeDtypeStruct((3, dff), F32),
                   jax.ShapeDtypeStruct((1, dff), F32), jax.ShapeDtypeStruct((1, dff), F32)],
        scratch_shapes=[pltpu.VMEM((2, tp, tc), BF16), pltpu.SemaphoreType.DMA((2,))],
        compiler_params=_cparams("arbitrary"))(up, up, dact, fw, fw, fb, fb)


def _s5_discretize(lam_re, lam_im, log_dt, b_re, b_im):
    dt = jnp.exp(log_dt)[:, None]
    mag = jnp.exp(lam_re * dt)
    ang = lam_im * dt
    a_re = mag * jnp.cos(ang)
    a_im = mag * jnp.sin(ang)
    den = lam_re * lam_re + lam_im * lam_im
    nr = a_re - 1.0
    f_re = (nr * lam_re + a_im * lam_im) / den
    f_im = (a_im * lam_re - nr * lam_im) / den
    bb_re = f_re[..., None] * b_re - f_im[..., None] * b_im
    bb_im = f_re[..., None] * b_im + f_im[..., None] * b_re
    return a_re, a_im, bb_re, bb_im


def _scan_tables(a_re, a_im, reverse):
    ar = a_re.reshape(-1)
    ai = -a_im.reshape(-1) if reverse else a_im.reshape(-1)
    pows = [(ar, ai)]
    for _ in range(SUBLANES - 1):
        pr, pi = pows[-1]
        pows.append((pr * ar - pi * ai, pr * ai + pi * ar))
    row = jnp.arange(SUBLANES)[:, None]
    out = []
    for k in (1, 2, 4):
        mask = (row <= SUBLANES - 1 - k) if reverse else (row >= k)
        out.append(jnp.where(mask, pows[k - 1][0][None, :], 0.0))
        out.append(jnp.where(mask, pows[k - 1][1][None, :], 0.0))
    order = list(range(SUBLANES - 1, -1, -1)) if reverse else list(range(SUBLANES))
    out.append(jnp.stack([pows[k][0] for k in order]))
    out.append(jnp.stack([pows[k][1] for k in order]))
    return jnp.stack(out).astype(F32)


def _compact_b(bb):
    bq = bb.reshape(N_GROUPS // 8, 8, STATE, GROUP)
    m = jnp.einsum("ab,qbph->qahbp", jnp.eye(8, dtype=bb.dtype), bq).reshape(N_GROUPS // 8, LANES, 8 * STATE)
    return m.transpose(1, 0, 2).reshape(LANES, N_GROUPS * STATE)


def _expand_b(m):
    d = m.reshape(8, GROUP, N_GROUPS // 8, 8, STATE)
    return jnp.einsum("ahqap->qaph", d).reshape(N_GROUPS, STATE, GROUP)


def _compact_c(c):
    cq = c.reshape(N_GROUPS // 8, 8, GROUP, STATE)
    return jnp.einsum("ab,qbhp->qbpah", jnp.eye(8, dtype=c.dtype), cq).reshape(N_GROUPS * STATE, LANES)


def _expand_c(m):
    d = m.reshape(N_GROUPS // 8, 8, STATE, 8, GROUP)
    return jnp.einsum("qbpbh->qbhp", d).reshape(N_GROUPS, GROUP, STATE)


def _local_step(x, target, p, late_weights, on_ffn_grads):
    seq, d = x.shape
    n_real = N_META + seq
    tp = -(-n_real // ROW_ALIGN) * ROW_ALIGN
    pad = jnp.zeros((tp - n_real, d), F32)
    h0 = jnp.concatenate([p["meta_tokens"], x, pad], axis=0)
    tgt = jnp.concatenate([jnp.zeros((N_META, d), F32), target, pad], axis=0)

    s5 = (p["ssm_lam_re"], p["ssm_lam_im"], p["ssm_log_dt"], p["ssm_b_re"], p["ssm_b_im"])
    (a_re, a_im, bb_re, bb_im), s5_vjp = jax.vjp(_s5_discretize, *s5)
    tab_f = _scan_tables(a_re, a_im, False)
    tab_r = _scan_tables(a_re, a_im, True)
    bc_re = _compact_b(bb_re).astype(BF16)
    bc_im = _compact_b(bb_im).astype(BF16)
    cc_re = _compact_c(p["ssm_c_re"]).astype(BF16)
    cc_im = _compact_c(p["ssm_c_im"]).astype(BF16)
    dskip = p["ssm_d"].reshape(1, -1)
    dh = dskip.shape[1]

    hn1 = _norm_fwd(h0, p["norm_mix_g"], "norm_mix")
    proj = _mm(hn1, p["w_in"], "nn", "proj")
    co, y, g = _seq_fwd(proj, p["conv_w"], bc_re, bc_im, cc_re, cc_im, dskip, tab_f, "seq_fwd")
    z = _mm(g, p["ssm_w_glu"], "nn", "glu")
    mixed = _mix_fwd(co, y, z, p["gain_conv_out"], p["gain_ssm_out"], "mix_fwd")
    mo = _mm(mixed, p["w_out"], "nn", "out_proj")
    h1, hn2 = _norm_fwd(h0, p["norm_ffn_g"], "norm_ffn", res=mo)
    late = late_weights(hn2)
    up = _mm(hn2, late["w_up"], "nn", "up_proj")
    act = _ffn_act(up, p["ffn_conv_w"], p["ffn_conv_b"], "ffn_act")
    dn = _mm(act, late["w_down"], "nn", "down_proj")
    loss, dh2, dh2b, d_gfin = _loss_bwd(h1, dn, tgt, p["norm_final_g"], n_real, "loss_bwd")

    g_w_down = _mm(act, dh2b, "tn", "g_w_down")
    dact = _mm(dh2b, late["w_down"], "nt", "d_act")
    dup, dfw_a, dfw_v, dfb_a, dfb_v = _ffn_bwd(up, dact, p["ffn_conv_w"], p["ffn_conv_b"], "ffn_bwd")
    g_w_up = _mm(hn2, dup, "tn", "g_w_up")
    dhn2 = _mm(dup, late["w_up"], "nt", "d_hn2")
    zero = on_ffn_grads(g_w_up, g_w_down)
    dh1, dh1b, d_gffn = _norm_bwd(h1, p["norm_ffn_g"] + zero, dhn2, dh2, "norm_ffn_bwd")
    g_w_out = _mm(mixed, dh1b, "tn", "g_w_out")
    dmixed = _mm(dh1b, p["w_out"], "nt", "d_mixed")
    dco, dz, dgp, d_gc, d_gs = _mix_bwd(dmixed, co, y, z, p["gain_conv_out"], p["gain_ssm_out"], "mix_bwd")
    g_w_glu = _mm(g, dz, "tn", "g_w_glu")
    dg = _mm(dz, p["ssm_w_glu"], "nt", "d_gelu", acc_in=dgp)
    dproj, d_conv_w = _conv_bwd(proj, dco, p["conv_w"], "conv_bwd")
    (dproj, dbc_re, dbc_im, dcc_re, dcc_im, d_dskip, da_re, da_im) = _ssm_bwd(
        proj, y, dg, dproj, bc_re, bc_im, cc_re, cc_im, dskip, tab_f, tab_r, "ssm_bwd")
    g_w_in = _mm(hn1, dproj, "tn", "g_w_in")
    dhn1 = _mm(dproj, p["w_in"], "nt", "d_hn1")
    dh0, _, d_gmix = _norm_bwd(h0, p["norm_mix_g"], dhn1, dh1, "norm_mix_bwd")

    d_lam_re, d_lam_im, d_log_dt, d_b_re, d_b_im = s5_vjp(
        (da_re.reshape(N_GROUPS, STATE), da_im.reshape(N_GROUPS, STATE), _expand_b(dbc_re), _expand_b(dbc_im)))
    grads = {
        "meta_tokens": dh0[:N_META], "norm_mix_g": d_gmix, "w_in": g_w_in, "conv_w": d_conv_w,
        "ssm_lam_re": d_lam_re, "ssm_lam_im": d_lam_im, "ssm_log_dt": d_log_dt,
        "ssm_b_re": d_b_re, "ssm_b_im": d_b_im, "ssm_c_re": _expand_c(dcc_re), "ssm_c_im": _expand_c(dcc_im),
        "ssm_d": d_dskip.reshape(N_GROUPS, GROUP), "ssm_w_glu": g_w_glu,
        "gain_conv_out": d_gc, "gain_ssm_out": d_gs, "w_out": g_w_out, "norm_ffn_g": d_gffn,
        "w_up": g_w_up, "ffn_conv_w": jnp.concatenate([dfw_a, dfw_v], axis=1),
        "ffn_conv_b": jnp.concatenate([dfb_a, dfb_v], axis=1), "w_down": g_w_down, "norm_final_g": d_gfin,
    }
    return loss[0, 0], dh0[N_META:n_real], grads


def _view(ref, axis, start, size):
    idx = [slice(None)] * len(ref.shape)
    idx[axis] = pl.ds(start, size)
    return ref.at[tuple(idx)]


def _exchange(name, ins, outs, aliases, local_copies, remote_copies):
    ni, no = len(ins), len(outs)
    nl, nr = len(local_copies), len(remote_copies)

    def body(*refs):
        in_refs, out_refs = refs[:ni], refs[ni:ni + no]
        send_sems, recv_sems, local_sems = refs[ni + no:]
        x, y, c = lax.axis_index("x"), lax.axis_index("y"), lax.axis_index("c")
        pos = (x, y, c, 2 * x + y)
        locals_ = [pltpu.make_async_copy(s(in_refs, out_refs, pos), d(in_refs, out_refs, pos), local_sems.at[i])
                   for i, (s, d) in enumerate(local_copies)]
        remotes = []
        for i, (s, d, flip) in enumerate(remote_copies):
            peer = (1 - x if "x" in flip else x, 1 - y if "y" in flip else y, 1 - c if "c" in flip else c)
            remotes.append(pltpu.make_async_remote_copy(
                src_ref=s(in_refs, out_refs, pos), dst_ref=d(in_refs, out_refs, pos),
                send_sem=send_sems.at[i], recv_sem=recv_sems.at[i], device_id=peer, device_id_type=MESH))
        for cp in locals_ + remotes:
            cp.start()
        for cp in remotes:
            cp.wait_recv()
        for cp in remotes:
            cp.wait_send()
        for cp in locals_:
            cp.wait()

    hbm = pl.BlockSpec(memory_space=pl.ANY)
    return pl.pallas_call(
        body, name=name, in_specs=[hbm] * ni, out_specs=[hbm] * no, out_shape=outs,
        input_output_aliases=aliases,
        scratch_shapes=[pltpu.SemaphoreType.DMA((nr,)), pltpu.SemaphoreType.DMA((nr,)),
                        pltpu.SemaphoreType.DMA((max(nl, 1),))],
    )(*ins)


BIG = {"w_in": (0, 1), "ssm_w_glu": (1, 0), "w_out": (1, 0), "w_up": (0, 1), "w_down": (1, 0)}
BIG_NAMES = tuple(BIG)
FLIPS = ("y", "x", "xy")


def _peer_chip(pos, flip):
    x, y, _, _ = pos
    return 2 * (1 - x if "x" in flip else x) + (1 - y if "y" in flip else y)


def _block_rows(rows, cols, itemsize, mult):
    return _pick_tile(rows, max(mult, (2 * 1024 * 1024) // (cols * itemsize)), mult)


def _cast_into_full(w, kc, shard_axis, name):
    r, cdim = w.shape
    tr = _block_rows(r, cdim, 4, 16)
    nb = r // tr

    def body(kc_ref, w_ref, o_ref):
        o_ref[...] = w_ref[...].astype(BF16)

    if shard_axis == 1:
        full, o_spec = (r, 4 * cdim), pl.BlockSpec((tr, cdim), lambda i, kc: (i, kc[0]))
    else:
        full, o_spec = (4 * r, cdim), pl.BlockSpec((tr, cdim), lambda i, kc: (kc[0] * nb + i, 0))
    return pl.pallas_call(
        body, name=name,
        grid_spec=pltpu.PrefetchScalarGridSpec(
            num_scalar_prefetch=1, grid=(nb,), in_specs=[pl.BlockSpec((tr, cdim), lambda i, kc: (i, 0))],
            out_specs=o_spec),
        out_shape=jax.ShapeDtypeStruct(full, BF16), compiler_params=_cparams("parallel"))(kc, w)


def _pair_sum(g, recv, kc, half_axis, name, out_dtype):
    hr, hc = recv.shape
    tr = _block_rows(hr, hc, 4, 16)
    nb = hr // tr

    def body(kc_ref, g_ref, r_ref, o_ref):
        o_ref[...] = (g_ref[...] + r_ref[...]).astype(out_dtype)

    if half_axis == 0:
        g_spec = pl.BlockSpec((tr, hc), lambda i, kc: (kc[1] * nb + i, 0))
    elif half_axis == 1:
        g_spec = pl.BlockSpec((tr, hc), lambda i, kc: (i, kc[1]))
    else:
        g_spec = pl.BlockSpec((tr, hc), lambda i, kc: (i, 0))
    same = pl.BlockSpec((tr, hc), lambda i, kc: (i, 0))
    return pl.pallas_call(
        body, name=name,
        grid_spec=pltpu.PrefetchScalarGridSpec(num_scalar_prefetch=1, grid=(nb,), in_specs=[g_spec, same],
                                               out_specs=same),
        out_shape=jax.ShapeDtypeStruct((hr, hc), out_dtype), compiler_params=_cparams("parallel"))(kc, g, recv)


def _chip_sum(own, recv, kc, own_axis, out_axis, name):
    _, sr, sc = recv.shape
    tr = _block_rows(sr, sc, 4, 16)
    nb = sr // tr

    def body(kc_ref, o_ref, r_ref, t_ref):
        k = kc_ref[0]
        own_v = o_ref[...].astype(F32)
        r = [r_ref[m].astype(F32) for m in range(3)]
        terms = []
        for kk in range(4):
            m = jnp.bitwise_xor(k, kk)
            terms.append(jnp.where(m == 0, own_v, jnp.where(m == 1, r[0], jnp.where(m == 2, r[1], r[2]))))
        t_ref[...] = (terms[0] + terms[1]) + (terms[2] + terms[3])

    if own_axis == 0:
        own_spec = pl.BlockSpec((tr, sc), lambda i, kc: (kc[0] * nb + i, 0))
    elif own_axis == 1:
        own_spec = pl.BlockSpec((tr, sc), lambda i, kc: (i, kc[0]))
    else:
        own_spec = pl.BlockSpec((tr, sc), lambda i, kc: (kc[1] * nb + i, 0))
    if out_axis == 0:
        out_full, out_spec = (2 * sr, sc), pl.BlockSpec((tr, sc), lambda i, kc: (kc[1] * nb + i, 0))
    else:
        out_full, out_spec = (sr, 2 * sc), pl.BlockSpec((tr, sc), lambda i, kc: (i, kc[1]))
    return pl.pallas_call(
        body, name=name,
        grid_spec=pltpu.PrefetchScalarGridSpec(
            num_scalar_prefetch=1, grid=(nb,),
            in_specs=[own_spec, pl.BlockSpec((3, tr, sc), lambda i, kc: (0, i, 0))],
            out_specs=out_spec),
        out_shape=jax.ShapeDtypeStruct(out_full, F32), compiler_params=_cparams("parallel"))(kc, own, recv)


def _adamw(w, g, m, v, name):
    r, cdim = w.shape
    tr = _block_rows(r, cdim, 4, 8)
    c1 = 1.0 - ADAM_B1 ** ADAM_STEP
    c2 = 1.0 - ADAM_B2 ** ADAM_STEP

    def body(w_ref, g_ref, m_ref, v_ref, d_ref, nm_ref, nv_ref):
        gv = g_ref[...]
        nm = ADAM_B1 * m_ref[...] + (1.0 - ADAM_B1) * gv
        nv = ADAM_B2 * v_ref[...] + (1.0 - ADAM_B2) * (gv * gv)
        d_ref[...] = -ADAM_LR * ((nm / c1) / (jnp.sqrt(nv / c2) + ADAM_EPS) + ADAM_WD * w_ref[...])
        nm_ref[...] = nm
        nv_ref[...] = nv

    spec = _rows(cdim, tr)
    return pl.pallas_call(body, name=name, grid=(r // tr,), in_specs=[spec] * 4, out_specs=[spec] * 3,
                          out_shape=[jax.ShapeDtypeStruct((r, cdim), F32)] * 3,
                          compiler_params=_cparams("parallel"))(w, g, m, v)


SIDE_EFFECT = pltpu.SideEffectType.DATAFLOW_SIDE_EFFECTING


def _descriptors(copies, refs, send_sems, recv_sems):
    x, y, c = lax.axis_index("x"), lax.axis_index("y"), lax.axis_index("c")
    pos = (x, y, c, 2 * x + y)
    out = []
    for i, (s, d, flip) in enumerate(copies):
        peer = (1 - x if "x" in flip else x, 1 - y if "y" in flip else y, 1 - c if "c" in flip else c)
        out.append(pltpu.make_async_remote_copy(
            src_ref=s(refs, refs, pos), dst_ref=d(refs, refs, pos),
            send_sem=send_sems.at[i], recv_sem=recv_sems.at[i], device_id=peer, device_id_type=MESH))
    return out


def _exchange_start(name, bufs, copies, after):
    n, nr = len(bufs), len(copies)

    def body(*refs):
        for cp in _descriptors(copies, refs[:n], refs[n + 1], refs[n + 2]):
            cp.start()
        token = refs[2 * n + 3]
        token[...] = jnp.zeros_like(token)

    hbm = pl.BlockSpec(memory_space=pltpu.HBM)
    sem = pl.BlockSpec(memory_space=pltpu.SEMAPHORE)
    out = pl.pallas_call(
        body, name=name,
        in_specs=[hbm] * n + [pl.BlockSpec(memory_space=pl.ANY)],
        out_specs=(sem, sem, *[hbm] * n, pl.BlockSpec(memory_space=pltpu.VMEM)),
        out_shape=(pltpu.SemaphoreType.DMA((nr,)), pltpu.SemaphoreType.DMA((nr,)),
                   *[pltpu.HBM(b.shape, b.dtype) for b in bufs], jax.ShapeDtypeStruct((SUBLANES, LANES), F32)),
        input_output_aliases={i: 2 + i for i in range(n)},
        compiler_params=pltpu.CompilerParams(has_side_effects=SIDE_EFFECT),
    )(*[pltpu.with_memory_space_constraint(b, pltpu.HBM) for b in bufs], after)
    return out[0], out[1], list(out[2:2 + n]), out[2 + n]


def _exchange_wait(name, send_sems, recv_sems, bufs, copies, after):
    n = len(bufs)

    def body(*refs):
        for cp in _descriptors(copies, refs[:n], refs[n], refs[n + 1]):
            cp.wait_send()
            cp.wait_recv()

    hbm = pl.BlockSpec(memory_space=pltpu.HBM)
    sem = pl.BlockSpec(memory_space=pltpu.SEMAPHORE)
    out = pl.pallas_call(
        body, name=name,
        in_specs=[hbm] * n + [sem, sem, pl.BlockSpec(memory_space=pl.ANY)],
        out_specs=tuple([hbm] * n),
        out_shape=tuple(pltpu.HBM(b.shape, b.dtype) for b in bufs),
        input_output_aliases={i: i for i in range(n)},
        compiler_params=pltpu.CompilerParams(has_side_effects=SIDE_EFFECT),
    )(*bufs, send_sems, recv_sems, after)
    return list(out)


EARLY = ("w_in", "ssm_w_glu", "w_out")
LATE = ("w_up", "w_down")


def _gather_copies(names, shard_shapes):
    def region(i, chip, c):
        half_axis, shard_axis = BIG[names[i]]
        ssize = shard_shapes[i][shard_axis]
        hsize = shard_shapes[i][half_axis] // 2
        return lambda ref: _view(_view(ref, shard_axis, chip * ssize, ssize), half_axis, c * hsize, hsize)

    ici, d2d = [], []
    for i in range(len(names)):
        for flip in FLIPS:
            ici.append((lambda I, O, pos, i=i: region(i, pos[3], pos[2])(I[i]),
                        lambda I, O, pos, i=i: region(i, pos[3], pos[2])(O[i]), flip))
            d2d.append((lambda I, O, pos, i=i, flip=flip: region(i, _peer_chip(pos, flip), pos[2])(I[i]),
                        lambda I, O, pos, i=i, flip=flip: region(i, _peer_chip(pos, flip), pos[2])(O[i]), "c"))
    return ici, d2d


def _gather_early(shards, tiny, kc):
    wb = {n: _cast_into_full(shards[n], kc, BIG[n][1], "cast_" + n) for n in BIG_NAMES}
    ne = len(EARLY)
    ici, d2d = _gather_copies(EARLY, [shards[n].shape for n in EARLY])
    local = [(lambda I, O, pos: I[ne], lambda I, O, pos: O[ne].at[pos[3]])]
    ici += [(lambda I, O, pos: I[ne], lambda I, O, pos: O[ne].at[pos[3]], flip) for flip in FLIPS]
    outs = [jax.ShapeDtypeStruct(wb[n].shape, BF16) for n in EARLY] + [jax.ShapeDtypeStruct((4,) + tiny.shape, F32)]
    got = _exchange("gather_ici", [wb[n] for n in EARLY] + [tiny], outs, {i: i for i in range(ne)}, local, ici)
    full = _exchange("gather_d2d", list(got[:ne]), outs[:ne], {i: i for i in range(ne)}, [], d2d)
    ici, d2d = _gather_copies(LATE, [shards[n].shape for n in LATE])
    started = _exchange_start("gather_late_start", [wb[n] for n in LATE], ici, full[0])
    return dict(zip(EARLY, full)), got[ne], (started, ici, d2d)


def _gather_late(pending, after):
    (send_sems, recv_sems, bufs, _), ici, d2d = pending
    got = _exchange_wait("gather_late_wait", send_sems, recv_sems, bufs, ici, after)
    outs = [jax.ShapeDtypeStruct(b.shape, BF16) for b in got]
    full = _exchange("gather_late_d2d", got, outs, {i: i for i in range(len(got))}, [], d2d)
    return dict(zip(LATE, full))


def _half_shape(n, shape):
    r, cdim = shape
    return (r // 2, cdim) if BIG[n][0] == 0 else (r, cdim // 2)


def _sub_shape(n, shape):
    hr, hc = _half_shape(n, shape)
    return (hr, hc // 4) if BIG[n][1] == 1 else (hr // 4, hc)


def _pair_exchange(names, gs, pack, kc, tag, after=None):
    n = len(names)

    def other_half(i, ref, pos):
        half_axis = BIG[names[i]][0]
        hsize = gs[i].shape[half_axis] // 2
        return _view(ref, half_axis, (1 - pos[2]) * hsize, hsize)

    remote = [(lambda I, O, pos, i=i: other_half(i, I[i], pos), lambda I, O, pos, i=i: O[i], "c") for i in range(n)]
    ins = list(gs)
    outs = [jax.ShapeDtypeStruct(_half_shape(names[i], gs[i].shape), F32) for i in range(n)]
    if pack is not None:
        remote.append((lambda I, O, pos: I[n], lambda I, O, pos: O[n], "c"))
        ins.append(pack)
        outs.append(jax.ShapeDtypeStruct(pack.shape, F32))
    if after is not None:
        ins.append(after)
    recv = _exchange("reduce_d2d" + tag, ins, outs, {}, [], remote)
    chip = [_pair_sum(gs[i], recv[i], kc, BIG[names[i]][0], "pair_sum_" + names[i], BF16) for i in range(n)]
    chip_pack = _pair_sum(pack, recv[n], kc, None, "pair_sum_pack", F32) if pack is not None else None
    return chip, chip_pack


def _chip_copies(names, shapes, pack_rows, dst_off):
    n = len(names)

    def piece(i, ref, chip):
        shard_axis = BIG[names[i]][1]
        ssize = _sub_shape(names[i], shapes[i])[shard_axis]
        return _view(ref, shard_axis, chip * ssize, ssize)

    copies = []
    for i in range(n):
        for slot, flip in enumerate(FLIPS):
            copies.append((lambda I, O, pos, i=i, flip=flip: piece(i, I[i], _peer_chip(pos, flip)),
                           lambda I, O, pos, i=i, slot=slot: O[dst_off + i].at[slot], flip))
    if pack_rows:
        for slot, flip in enumerate(FLIPS):
            copies.append((lambda I, O, pos: _view(I[n], 0, pos[2] * (pack_rows // 2), pack_rows // 2),
                           lambda I, O, pos, slot=slot: O[dst_off + n].at[slot], flip))
    return copies


def _reduce_late_start(grads, kc):
    gs = [grads[n] for n in LATE]
    chip, _ = _pair_exchange(LATE, gs, None, kc, "_late")
    land = [lax.empty((3,) + _sub_shape(n, g.shape), BF16) for n, g in zip(LATE, gs)]
    copies = _chip_copies(LATE, [g.shape for g in gs], 0, len(LATE))
    started = _exchange_start("reduce_late_start", chip + land, copies, chip[0])
    return started, copies


def _reduce_finish(pending, grads, pack, kc, after):
    (send_sems, recv_sems, bufs, _), copies = pending
    bufs = _exchange_wait("reduce_late_wait", send_sems, recv_sems, bufs, copies, after)
    nl, ne = len(LATE), len(EARLY)
    chip_late, recv_late = bufs[:nl], bufs[nl:]

    gs = [grads[n] for n in EARLY]
    chip, chip_pack = _pair_exchange(EARLY, gs, pack, kc, "", after=recv_late[0])
    copies = _chip_copies(EARLY, [g.shape for g in gs], pack.shape[0], 0)
    outs = ([jax.ShapeDtypeStruct((3,) + _sub_shape(n, g.shape), BF16) for n, g in zip(EARLY, gs)]
            + [jax.ShapeDtypeStruct((3, pack.shape[0] // 2, pack.shape[1]), F32)])
    recv = _exchange("reduce_ici", chip + [chip_pack], outs, {}, [], copies)

    names = EARLY + LATE
    chips, recvs = chip + chip_late, list(recv[:ne]) + recv_late
    total = [_chip_sum(chips[i], recvs[i], kc, BIG[n][1], BIG[n][0], "chip_sum_" + n) for i, n in enumerate(names)]
    total.append(_chip_sum(chip_pack, recv[ne], kc, None, 0, "chip_sum_pack"))

    def my_half(i, ref, pos):
        half_axis = BIG[names[i]][0] if i < len(names) else 0
        hsize = ref.shape[half_axis] // 2
        return _view(ref, half_axis, pos[2] * hsize, hsize)

    remote = [(lambda I, O, pos, i=i: my_half(i, I[i], pos), lambda I, O, pos, i=i: my_half(i, O[i], pos), "c")
              for i in range(len(total))]
    outs = [jax.ShapeDtypeStruct(t.shape, F32) for t in total]
    out = _exchange("swap_d2d", total, outs, {i: i for i in range(len(total))}, [], remote)
    return dict(zip(names, out[:len(names)])), out[len(names)]


WEIGHTS = ("meta_tokens", "norm_mix_g", "w_in", "conv_w", "ssm_lam_re", "ssm_lam_im", "ssm_log_dt", "ssm_b_re",
           "ssm_b_im", "ssm_c_re", "ssm_c_im", "ssm_d", "ssm_w_glu", "gain_conv_out", "gain_ssm_out", "w_out",
           "norm_ffn_g", "w_up", "ffn_conv_w", "ffn_conv_b", "w_down", "norm_final_g")
TINY_SHARDED = ("meta_tokens", "conv_w", "ffn_conv_w")
REPLICATED = tuple(n for n in WEIGHTS if n not in BIG and n not in TINY_SHARDED)
PACK_COLS = 512


def _pack(arrays, row_mult, cols):
    flat = jnp.concatenate([a.reshape(-1).astype(F32) for a in arrays])
    n = flat.shape[0]
    total = -(-n // (row_mult * cols)) * (row_mult * cols)
    return jnp.concatenate([flat, jnp.zeros((total - n,), F32)]).reshape(total // cols, cols)


def _unpack(packed, shapes):
    flat = packed.reshape(-1)
    out, off = [], 0
    for s in shapes:
        n = math.prod(s)
        out.append(flat[off:off + n].reshape(s))
        off += n
    return out


def kernel(x, meta_tokens, norm_mix_g, w_in, conv_w, ssm_lam_re, ssm_lam_im, ssm_log_dt, ssm_b_re, ssm_b_im, ssm_c_re, ssm_c_im, ssm_d, ssm_w_glu, gain_conv_out, gain_ssm_out, w_out, norm_ffn_g, w_up, ffn_conv_w, ffn_conv_b, w_down, norm_final_g, loss_target, m_meta_tokens, m_norm_mix_g, m_w_in, m_conv_w, m_ssm_lam_re, m_ssm_lam_im, m_ssm_log_dt, m_ssm_b_re, m_ssm_b_im, m_ssm_c_re, m_ssm_c_im, m_ssm_d, m_ssm_w_glu, m_gain_conv_out, m_gain_ssm_out, m_w_out, m_norm_ffn_g, m_w_up, m_ffn_conv_w, m_ffn_conv_b, m_w_down, m_norm_final_g, v_meta_tokens, v_norm_mix_g, v_w_in, v_conv_w, v_ssm_lam_re, v_ssm_lam_im, v_ssm_log_dt, v_ssm_b_re, v_ssm_b_im, v_ssm_c_re, v_ssm_c_im, v_ssm_d, v_ssm_w_glu, v_gain_conv_out, v_gain_ssm_out, v_w_out, v_norm_ffn_g, v_w_up, v_ffn_conv_w, v_ffn_conv_b, v_w_down, v_norm_final_g):
    args = dict(locals())
    w = {n: args[n] for n in WEIGHTS}
    mom = {n: args["m_" + n] for n in WEIGHTS}
    var = {n: args["v_" + n] for n in WEIGHTS}
    kx, ky, kc_ = lax.axis_index("x"), lax.axis_index("y"), lax.axis_index("c")
    chip = 2 * kx + ky
    kc = jnp.stack([chip, kc_]).astype(jnp.int32)

    def squeeze(n, a):
        if n == "meta_tokens":
            return a
        if n == "norm_final_g":
            return a.reshape(1, -1)
        a = a[0]
        return a.reshape(1, -1) if a.ndim == 1 else a

    wl = {n: squeeze(n, w[n]) for n in WEIGHTS}
    ml = {n: squeeze(n, mom[n]) for n in WEIGHTS}
    vl = {n: squeeze(n, var[n]) for n in WEIGHTS}

    tiny = _pack([wl[n] for n in TINY_SHARDED], SUBLANES, LANES)
    full, tiny_all, gathering = _gather_early({n: wl[n] for n in BIG_NAMES}, tiny, kc)
    tiny_shapes = [wl[n].shape for n in TINY_SHARDED]
    tiny_parts = [_unpack(tiny_all[k], tiny_shapes) for k in range(4)]
    p = {n: wl[n] for n in WEIGHTS if n not in LATE}
    p.update(full)
    for j, n in enumerate(TINY_SHARDED):
        p[n] = jnp.concatenate([tiny_parts[k][j] for k in range(4)], axis=1)
    p["ssm_log_dt"] = wl["ssm_log_dt"].reshape(-1)
    p["norm_mix_g"] = wl["norm_mix_g"] + gathering[0][3][0, 0]

    reducing = []

    def on_ffn_grads(g_w_up, g_w_down):
        reducing.append(_reduce_late_start({"w_up": g_w_up, "w_down": g_w_down}, kc))
        return reducing[0][0][3][0, 0]

    loss_local, grad_x, grads = _local_step(x[0], loss_target[0], p, functools.partial(_gather_late, gathering),
                                            on_ffn_grads)
    loss = lax.psum(loss_local, ("x", "y", "c"))

    small_names = REPLICATED + TINY_SHARDED
    small_shapes = [tuple(grads[n].shape) for n in small_names]
    pack = _pack([grads[n] for n in small_names], 2 * 16, PACK_COLS)
    g_big, g_pack = _reduce_finish(reducing[0], {n: grads[n] for n in EARLY}, pack, kc, pack)
    g_small = dict(zip(small_names, _unpack(g_pack, small_shapes)))
    g = dict(g_big)
    for n in REPLICATED:
        g[n] = g_small[n].reshape(wl[n].shape)
    for n in TINY_SHARDED:
        cols = wl[n].shape[1]
        g[n] = lax.dynamic_slice_in_dim(g_small[n], chip * cols, cols, axis=1)

    delta, new_m, new_v = {}, {}, {}
    for n in BIG_NAMES:
        delta[n], new_m[n], new_v[n] = _adamw(wl[n], g[n], ml[n], vl[n], "adamw_" + n)
    packs = [_pack([d[n] for n in small_names], SUBLANES, PACK_COLS) for d in (wl, g, ml, vl)]
    shapes = [wl[n].shape for n in small_names]
    for d, packed in zip((delta, new_m, new_v), _adamw(*packs, "adamw_small")):
        d.update(zip(small_names, _unpack(packed, shapes)))

    def like(n, a):
        return a.reshape(w[n].shape)

    return (loss, grad_x[None], *[like(n, g[n]) for n in WEIGHTS], *[like(n, delta[n]) for n in WEIGHTS],
            *[like(n, new_m[n]) for n in WEIGHTS], *[like(n, new_v[n]) for n in WEIGHTS])
```

```python
import functools
import math

import jax
import jax.numpy as jnp
from jax import lax
from jax.experimental import pallas as pl
from jax.experimental.pallas import tpu as pltpu

F32 = jnp.float32
BF16 = jnp.bfloat16
MESH = pl.DeviceIdType.MESH

N_META = 16
N_GROUPS = 32
GROUP = 16
STATE = 64
RMS_EPS = 1e-6
ADAM_LR = 0.001
ADAM_B1 = 0.9
ADAM_B2 = 0.999
ADAM_EPS = 1e-08
ADAM_WD = 0.01
ADAM_STEP = 10

LANES = 128
SUBLANES = 8
ROW_ALIGN = 128
ROW_TILES = 4
VMEM_LIMIT = 52 * 1024 * 1024
GELU_C = math.sqrt(2.0 / math.pi)
GELU_A = 0.044715


def _cparams(*sem):
    return pltpu.CompilerParams(dimension_semantics=sem, vmem_limit_bytes=VMEM_LIMIT)


def _pick_tile(dim, cap, mult):
    best = None
    for t in range(mult, min(dim, cap) + 1, mult):
        if dim % t == 0:
            best = t
    return best if best is not None else dim


def _mm(a, b, mode, name, out_dtype=F32, acc_in=None):
    if mode == "tn":
        kdim, m = a.shape
    else:
        m, kdim = a.shape
    n = b.shape[0] if mode == "nt" else b.shape[1]
    tm = _pick_tile(m, 1408, LANES if mode == "tn" else 16)
    tn = _pick_tile(n, 512, LANES)
    tk = _pick_tile(kdim, 2816, LANES)
    nk = kdim // tk
    has_acc = acc_in is not None

    def body(*refs):
        if has_acc:
            a_ref, b_ref, c_ref, o_ref = refs[:4]
            rest = refs[4:]
        else:
            a_ref, b_ref, o_ref = refs[:3]
            c_ref = None
            rest = refs[3:]
        if mode == "nn":
            p = jnp.dot(a_ref[...], b_ref[...], preferred_element_type=F32)
        elif mode == "nt":
            p = lax.dot_general(a_ref[...], b_ref[...], (((1,), (1,)), ((), ())), preferred_element_type=F32)
        else:
            p = lax.dot_general(a_ref[...], b_ref[...], (((0,), (0,)), ((), ())), preferred_element_type=F32)
        if nk == 1:
            if has_acc:
                p = p + c_ref[...]
            o_ref[...] = p.astype(out_dtype)
        else:
            acc_ref = rest[0]
            k = pl.program_id(2)

            @pl.when(k == 0)
            def _():
                acc_ref[...] = p + c_ref[...] if has_acc else p

            @pl.when(k > 0)
            def _():
                acc_ref[...] += p

            @pl.when(k == nk - 1)
            def _():
                o_ref[...] = acc_ref[...].astype(out_dtype)

    if mode == "tn":
        a_spec = pl.BlockSpec((tk, tm), lambda i, j, k: (k, i))
    else:
        a_spec = pl.BlockSpec((tm, tk), lambda i, j, k: (i, k))
    if mode == "nt":
        b_spec = pl.BlockSpec((tn, tk), lambda i, j, k: (j, k))
    else:
        b_spec = pl.BlockSpec((tk, tn), lambda i, j, k: (k, j))
    o_spec = pl.BlockSpec((tm, tn), lambda i, j, k: (i, j))
    in_specs = [a_spec, b_spec] + ([o_spec] if has_acc else [])
    args = (a, b) + ((acc_in,) if has_acc else ())
    return pl.pallas_call(
        body, name=name, grid=(m // tm, n // tn, nk),
        in_specs=in_specs, out_specs=o_spec,
        out_shape=jax.ShapeDtypeStruct((m, n), out_dtype),
        scratch_shapes=[pltpu.VMEM((tm, tn), F32)] if nk > 1 else [],
        compiler_params=_cparams("parallel", "parallel", "arbitrary"),
    )(*args)


def _rows(shape_cols, tr, dtype=None):
    return pl.BlockSpec((tr, shape_cols), lambda i: (i, 0))


def _const(shape):
    return pl.BlockSpec(shape, lambda i: (0,) * len(shape))


def _rms(x):
    return lax.rsqrt(jnp.mean(x * x, axis=-1, keepdims=True) + RMS_EPS)


def _rms_bwd(x, r, g, dy):
    xn = x * r
    dxn = dy * g
    dx = r * (dxn - xn * jnp.mean(dxn * xn, axis=-1, keepdims=True))
    return dx, dy * xn


def _gelu(y):
    return 0.5 * y * (1.0 + jnp.tanh(GELU_C * (y + GELU_A * y * y * y)))


def _gelu_grad(y):
    t = jnp.tanh(GELU_C * (y + GELU_A * y * y * y))
    return 0.5 * (1.0 + t) + 0.5 * y * (1.0 - t * t) * GELU_C * (1.0 + 3.0 * GELU_A * y * y)


def _sigmoid(z):
    return 1.0 / (1.0 + jnp.exp(-z))


def _norm_fwd(h, g, name, res=None):
    tp, d = h.shape
    tr = tp // ROW_TILES
    has_res = res is not None

    def body(*refs):
        if has_res:
            h_ref, r_ref, g_ref, s_ref, hn_ref = refs
            x = h_ref[...] + r_ref[...]
            s_ref[...] = x
        else:
            h_ref, g_ref, hn_ref = refs
            x = h_ref[...]
        hn_ref[...] = (x * _rms(x) * g_ref[...]).astype(BF16)

    in_specs = [_rows(d, tr)] + ([_rows(d, tr)] if has_res else []) + [_const((1, d))]
    out_specs = ([_rows(d, tr)] if has_res else []) + [_rows(d, tr)]
    out_shape = ([jax.ShapeDtypeStruct((tp, d), F32)] if has_res else []) + [jax.ShapeDtypeStruct((tp, d), BF16)]
    args = (h,) + ((res,) if has_res else ()) + (g,)
    out = pl.pallas_call(body, name=name, grid=(ROW_TILES,), in_specs=in_specs, out_specs=out_specs,
                         out_shape=out_shape, compiler_params=_cparams("parallel"))(*args)
    return out if has_res else out[0]


def _norm_bwd(h, g, dhn, dres, name):
    tp, d = h.shape
    tr = tp // ROW_TILES

    def body(h_ref, g_ref, dhn_ref, dres_ref, dh_ref, dhb_ref, dg_ref):
        x = h_ref[...]
        dx, dgs = _rms_bwd(x, _rms(x), g_ref[...], dhn_ref[...])
        dh = dres_ref[...] + dx
        dh_ref[...] = dh
        dhb_ref[...] = dh.astype(BF16)

        @pl.when(pl.program_id(0) == 0)
        def _():
            dg_ref[...] = jnp.zeros_like(dg_ref)

        dg_ref[...] += jnp.sum(dgs, axis=0, keepdims=True)

    return pl.pallas_call(
        body, name=name, grid=(ROW_TILES,),
        in_specs=[_rows(d, tr), _const((1, d)), _rows(d, tr), _rows(d, tr)],
        out_specs=[_rows(d, tr), _rows(d, tr), _const((1, d))],
        out_shape=[jax.ShapeDtypeStruct((tp, d), F32), jax.ShapeDtypeStruct((tp, d), BF16),
                   jax.ShapeDtypeStruct((1, d), F32)],
        compiler_params=_cparams("arbitrary"))(h, g, dhn, dres)


def _input_norm_bwd(h, g, dhn, dres, n_real, name):
    tp, d = h.shape
    tr = tp // ROW_TILES

    def body(h_ref, g_ref, dhn_ref, dres_ref, dx_ref, dmeta_ref, dg_ref, stage, sem):
        i = pl.program_id(0)
        x = h_ref[...]
        dx, dgs = _rms_bwd(x, _rms(x), g_ref[...], dhn_ref[...])
        stage[...] = dres_ref[...] + dx

        @pl.when(i == 0)
        def _():
            dg_ref[...] = jnp.zeros_like(dg_ref)
            dmeta_ref[...] = stage[:N_META, :]

        dg_ref[...] += jnp.sum(dgs, axis=0, keepdims=True)
        for t in range(ROW_TILES):
            lo, hi = max(t * tr, N_META), min((t + 1) * tr, n_real)
            if hi > lo:
                @pl.when(i == t)
                def _(t=t, lo=lo, hi=hi):
                    cp = pltpu.make_async_copy(stage.at[pl.ds(lo - t * tr, hi - lo), :],
                                               dx_ref.at[pl.ds(lo - N_META, hi - lo), :], sem)
                    cp.start()
                    cp.wait()

    return pl.pallas_call(
        body, name=name, grid=(ROW_TILES,),
        in_specs=[_rows(d, tr), _const((1, d)), _rows(d, tr), _rows(d, tr)],
        out_specs=[pl.BlockSpec(memory_space=pl.ANY), _const((N_META, d)), _const((1, d))],
        out_shape=[jax.ShapeDtypeStruct((n_real - N_META, d), F32), jax.ShapeDtypeStruct((N_META, d), F32),
                   jax.ShapeDtypeStruct((1, d), F32)],
        scratch_shapes=[pltpu.VMEM((tr, d), F32), pltpu.SemaphoreType.DMA],
        compiler_params=_cparams("arbitrary"))(h, g, dhn, dres)


def _loss_bwd(h1, dn, tgt, g, n_real, name):
    tp, d = h1.shape
    tr = tp // ROW_TILES

    def body(h1_ref, dn_ref, t_ref, g_ref, loss_ref, dh_ref, dhb_ref, dg_ref):
        i = pl.program_id(0)
        x = h1_ref[...] + dn_ref[...]
        r = _rms(x)
        row = i * tr + lax.broadcasted_iota(jnp.int32, (tr, d), 0)
        valid = (row >= N_META) & (row < n_real)
        e = jnp.where(valid, x * r * g_ref[...] - t_ref[...], 0.0)
        dx, dgs = _rms_bwd(x, r, g_ref[...], e * (1.0 / d))
        dh_ref[...] = dx
        dhb_ref[...] = dx.astype(BF16)

        @pl.when(i == 0)
        def _():
            dg_ref[...] = jnp.zeros_like(dg_ref)
            loss_ref[...] = jnp.zeros_like(loss_ref)

        dg_ref[...] += jnp.sum(dgs, axis=0, keepdims=True)
        loss_ref[...] += (0.5 / d) * jnp.sum(jnp.sum(e * e, axis=0, keepdims=True), axis=1, keepdims=True)

    return pl.pallas_call(
        body, name=name, grid=(ROW_TILES,),
        in_specs=[_rows(d, tr), _rows(d, tr), _rows(d, tr), _const((1, d))],
        out_specs=[_const((1, LANES)), _rows(d, tr), _rows(d, tr), _const((1, d))],
        out_shape=[jax.ShapeDtypeStruct((1, LANES), F32), jax.ShapeDtypeStruct((tp, d), F32),
                   jax.ShapeDtypeStruct((tp, d), BF16), jax.ShapeDtypeStruct((1, d), F32)],
        compiler_params=_cparams("arbitrary"))(h1, dn, tgt, g)


def _mix_fwd(co, y, z, gc, gs, name):
    tp, dh = co.shape
    tr = tp // ROW_TILES

    def body(co_ref, y_ref, z_ref, gc_ref, gs_ref, m_ref):
        c = co_ref[...]
        m_ref[:, :dh] = (c * _rms(c) * gc_ref[...]).astype(BF16)
        so = _gelu(y_ref[...]) * _sigmoid(z_ref[...])
        m_ref[:, dh:] = (so * _rms(so) * gs_ref[...]).astype(BF16)

    return pl.pallas_call(
        body, name=name, grid=(ROW_TILES,),
        in_specs=[_rows(dh, tr)] * 3 + [_const((1, dh))] * 2,
        out_specs=_rows(2 * dh, tr),
        out_shape=jax.ShapeDtypeStruct((tp, 2 * dh), BF16),
        compiler_params=_cparams("parallel"))(co, y, z, gc, gs)


def _mix_bwd(dm, co, y, z, gc, gs, name):
    tp, dh = co.shape
    tr = tp // ROW_TILES

    def body(dm_ref, co_ref, y_ref, z_ref, gc_ref, gs_ref, dco_ref, dz_ref, dgp_ref, dgc_ref, dgs_ref):
        c = co_ref[...]
        dco, dgc = _rms_bwd(c, _rms(c), gc_ref[...], dm_ref[:, :dh])
        dco_ref[...] = dco
        gl = _gelu(y_ref[...])
        sg = _sigmoid(z_ref[...])
        so = gl * sg
        dso, dgs = _rms_bwd(so, _rms(so), gs_ref[...], dm_ref[:, dh:])
        dz_ref[...] = (dso * gl * sg * (1.0 - sg)).astype(BF16)
        dgp_ref[...] = dso * sg

        @pl.when(pl.program_id(0) == 0)
        def _():
            dgc_ref[...] = jnp.zeros_like(dgc_ref)
            dgs_ref[...] = jnp.zeros_like(dgs_ref)

        dgc_ref[...] += jnp.sum(dgc, axis=0, keepdims=True)
        dgs_ref[...] += jnp.sum(dgs, axis=0, keepdims=True)

    return pl.pallas_call(
        body, name=name, grid=(ROW_TILES,),
        in_specs=[_rows(2 * dh, tr)] + [_rows(dh, tr)] * 3 + [_const((1, dh))] * 2,
        out_specs=[_rows(dh, tr), _rows(dh, tr), _rows(dh, tr), _const((1, dh)), _const((1, dh))],
        out_shape=[jax.ShapeDtypeStruct((tp, dh), F32), jax.ShapeDtypeStruct((tp, dh), BF16),
                   jax.ShapeDtypeStruct((tp, dh), F32), jax.ShapeDtypeStruct((1, dh), F32),
                   jax.ShapeDtypeStruct((1, dh), F32)],
        compiler_params=_cparams("arbitrary"))(dm, co, y, z, gc, gs)


def _shift_down(x, k):
    row = lax.broadcasted_iota(jnp.int32, x.shape, 0)
    return jnp.where(row >= k, pltpu.roll(x, k, 0), 0.0)


def _shift_up(x, k):
    n = x.shape[0]
    row = lax.broadcasted_iota(jnp.int32, x.shape, 0)
    return jnp.where(row < n - k, pltpu.roll(x, n - k, 0), 0.0)


def _dwconv(x, w_ref):
    return w_ref[2:3, :] * x + w_ref[1:2, :] * _shift_down(x, 1) + w_ref[0:1, :] * _shift_down(x, 2)


def _dwconv_bwd(x, dy, w_ref):
    dx = w_ref[2:3, :] * dy + w_ref[1:2, :] * _shift_up(dy, 1) + w_ref[0:1, :] * _shift_up(dy, 2)
    dw = jnp.concatenate([jnp.sum(dy * _shift_down(x, 2), axis=0, keepdims=True),
                          jnp.sum(dy * _shift_down(x, 1), axis=0, keepdims=True),
                          jnp.sum(dy * x, axis=0, keepdims=True)], axis=0)
    return dx, dw


def _scan(s_re, s_im, tab_ref, reverse):
    n_chunks = s_re.shape[0] // SUBLANES
    n_strips = s_re.shape[1] // LANES
    last = 0 if reverse else SUBLANES - 1

    def body(i, carry):
        chunk = (n_chunks - 1 - i) if reverse else i
        r0 = pl.multiple_of(chunk * SUBLANES, SUBLANES)
        out = []
        for st in range(n_strips):
            lanes = slice(st * LANES, (st + 1) * LANES)
            cr, ci = carry[2 * st], carry[2 * st + 1]
            xr = s_re[pl.ds(r0, SUBLANES), lanes]
            xi = s_im[pl.ds(r0, SUBLANES), lanes]
            for level, k in enumerate((1, 2, 4)):
                mr = tab_ref[2 * level, :, lanes]
                mi = tab_ref[2 * level + 1, :, lanes]
                sh = SUBLANES - k if reverse else k
                rr = pltpu.roll(xr, sh, 0)
                ri = pltpu.roll(xi, sh, 0)
                xr, xi = xr + (mr * rr - mi * ri), xi + (mr * ri + mi * rr)
            pwr = tab_ref[6, :, lanes]
            pwi = tab_ref[7, :, lanes]
            xr, xi = xr + (pwr * cr - pwi * ci), xi + (pwr * ci + pwi * cr)
            s_re[pl.ds(r0, SUBLANES), lanes] = xr
            s_im[pl.ds(r0, SUBLANES), lanes] = xi
            out.append(jnp.broadcast_to(xr[last:last + 1, :], (SUBLANES, LANES)))
            out.append(jnp.broadcast_to(xi[last:last + 1, :], (SUBLANES, LANES)))
        return tuple(out)

    zero = jnp.zeros((SUBLANES, LANES), F32)
    lax.fori_loop(0, n_chunks, body, (zero,) * (2 * n_strips))


def _seq_fwd(proj, conv_w, bc_re, bc_im, cc_re, cc_im, dskip, tab_f, name):
    tp = proj.shape[0]
    dh = proj.shape[1] // 4
    nq = dh // LANES
    sw = STATE * N_GROUPS // nq

    def body(b_ref, c_ref, v_ref, u_ref, w_ref, bre_ref, bim_ref, cre_ref, cim_ref, d_ref, tab_ref,
             co_ref, y_ref, g_ref, s_re, s_im):
        co_ref[...] = b_ref[...] * _dwconv(c_ref[...] * v_ref[...], w_ref)
        u = u_ref[...]
        ub = u.astype(BF16)
        s_re[...] = jnp.dot(ub, bre_ref[...], preferred_element_type=F32)
        s_im[...] = jnp.dot(ub, bim_ref[...], preferred_element_type=F32)
        _scan(s_re, s_im, tab_ref, False)
        y = (jnp.dot(s_re[...].astype(BF16), cre_ref[...], preferred_element_type=F32)
             - jnp.dot(s_im[...].astype(BF16), cim_ref[...], preferred_element_type=F32)
             + d_ref[...] * u)
        y_ref[...] = y
        g_ref[...] = _gelu(y).astype(BF16)

    col = lambda off: pl.BlockSpec((tp, LANES), lambda q, off=off: (0, off * nq + q))
    blk = pl.BlockSpec((tp, LANES), lambda q: (0, q))
    return pl.pallas_call(
        body, name=name, grid=(nq,),
        in_specs=[col(0), col(1), col(2), col(3),
                  pl.BlockSpec((3, LANES), lambda q: (0, q)),
                  pl.BlockSpec((LANES, sw), lambda q: (0, q)), pl.BlockSpec((LANES, sw), lambda q: (0, q)),
                  pl.BlockSpec((sw, LANES), lambda q: (q, 0)), pl.BlockSpec((sw, LANES), lambda q: (q, 0)),
                  pl.BlockSpec((1, LANES), lambda q: (0, q)),
                  pl.BlockSpec((8, SUBLANES, sw), lambda q: (0, 0, q))],
        out_specs=[blk, blk, blk],
        out_shape=[jax.ShapeDtypeStruct((tp, dh), F32), jax.ShapeDtypeStruct((tp, dh), F32),
                   jax.ShapeDtypeStruct((tp, dh), BF16)],
        scratch_shapes=[pltpu.VMEM((tp, sw), F32), pltpu.VMEM((tp, sw), F32)],
        compiler_params=_cparams("parallel"),
    )(proj, proj, proj, proj, conv_w, bc_re, bc_im, cc_re, cc_im, dskip, tab_f)


def _conv_bwd(proj, dco, conv_w, name):
    tp = proj.shape[0]
    dh = proj.shape[1] // 4
    nq = dh // LANES

    def body(b_ref, c_ref, v_ref, dco_ref, w_ref, dproj_ref, dw_ref, stage, sem):
        q = pl.program_id(0)
        cg = c_ref[...]
        vg = v_ref[...]
        cv = cg * vg
        dco_v = dco_ref[...]
        dcv, dw = _dwconv_bwd(cv, dco_v * b_ref[...], w_ref)
        dw_ref[...] = dw
        stage[0] = (dco_v * _dwconv(cv, w_ref)).astype(BF16)
        stage[1] = (dcv * vg).astype(BF16)
        stage[2] = (dcv * cg).astype(BF16)
        copies = [pltpu.make_async_copy(stage.at[p], dproj_ref.at[:, pl.ds((p * nq + q) * LANES, LANES)], sem.at[p])
                  for p in range(3)]
        for cp in copies:
            cp.start()
        for cp in copies:
            cp.wait()

    col = lambda off: pl.BlockSpec((tp, LANES), lambda q, off=off: (0, off * nq + q))
    return pl.pallas_call(
        body, name=name, grid=(nq,),
        in_specs=[col(0), col(1), col(2), pl.BlockSpec((tp, LANES), lambda q: (0, q)),
                  pl.BlockSpec((3, LANES), lambda q: (0, q))],
        out_specs=[pl.BlockSpec(memory_space=pl.ANY), pl.BlockSpec((3, LANES), lambda q: (0, q))],
        out_shape=[jax.ShapeDtypeStruct((tp, 4 * dh), BF16), jax.ShapeDtypeStruct((3, dh), F32)],
        scratch_shapes=[pltpu.VMEM((3, tp, LANES), BF16), pltpu.SemaphoreType.DMA((3,))],
        compiler_params=_cparams("arbitrary"),
    )(proj, proj, proj, dco, conv_w)


def _ssm_bwd(proj, y, dg, dproj, bc_re, bc_im, cc_re, cc_im, dskip, tab_f, tab_r, name):
    tp = proj.shape[0]
    dh = proj.shape[1] // 4
    nq = dh // LANES
    sw = STATE * N_GROUPS // nq

    def body(u_ref, y_ref, dg_ref, dproj_in, bre_ref, bim_ref, cre_ref, cim_ref, d_ref, tabf_ref, tabr_ref,
             dproj_ref, dbre_ref, dbim_ref, dcre_ref, dcim_ref, dd_ref, dar_ref, dai_ref,
             s_re, s_im, l_re, l_im, stage, sem):
        del dproj_in
        q = pl.program_id(0)
        nt = (((1,), (1,)), ((), ()))
        tn = (((0,), (0,)), ((), ()))
        u = u_ref[...]
        ub = u.astype(BF16)
        s_re[...] = jnp.dot(ub, bre_ref[...], preferred_element_type=F32)
        s_im[...] = jnp.dot(ub, bim_ref[...], preferred_element_type=F32)
        _scan(s_re, s_im, tabf_ref, False)
        dy = dg_ref[...] * _gelu_grad(y_ref[...])
        dyb = dy.astype(BF16)
        dd_ref[...] = jnp.sum(dy * u, axis=0, keepdims=True)
        l_re[...] = lax.dot_general(dyb, cre_ref[...], nt, preferred_element_type=F32)
        l_im[...] = -lax.dot_general(dyb, cim_ref[...], nt, preferred_element_type=F32)
        dcre_ref[...] = lax.dot_general(s_re[...].astype(BF16), dyb, tn, preferred_element_type=F32)
        dcim_ref[...] = -lax.dot_general(s_im[...].astype(BF16), dyb, tn, preferred_element_type=F32)
        _scan(l_re, l_im, tabr_ref, True)
        for st in range(sw // LANES):
            lanes = slice(st * LANES, (st + 1) * LANES)
            lr = l_re[:, lanes]
            li = l_im[:, lanes]
            pr = _shift_down(s_re[:, lanes], 1)
            pi = _shift_down(s_im[:, lanes], 1)
            dar_ref[:, lanes] = jnp.sum(lr * pr + li * pi, axis=0, keepdims=True)
            dai_ref[:, lanes] = jnp.sum(li * pr - lr * pi, axis=0, keepdims=True)
        lrb = l_re[...].astype(BF16)
        lib = l_im[...].astype(BF16)
        du = (dy * d_ref[...] + lax.dot_general(lrb, bre_ref[...], nt, preferred_element_type=F32)
              + lax.dot_general(lib, bim_ref[...], nt, preferred_element_type=F32))
        stage[...] = du.astype(BF16)
        dbre_ref[...] = lax.dot_general(ub, lrb, tn, preferred_element_type=F32)
        dbim_ref[...] = lax.dot_general(ub, lib, tn, preferred_element_type=F32)
        cp = pltpu.make_async_copy(stage, dproj_ref.at[:, pl.ds((3 * nq + q) * LANES, LANES)], sem)
        cp.start()
        cp.wait()

    blk = pl.BlockSpec((tp, LANES), lambda q: (0, q))
    bspec = pl.BlockSpec((LANES, sw), lambda q: (0, q))
    cspec = pl.BlockSpec((sw, LANES), lambda q: (q, 0))
    tspec = pl.BlockSpec((8, SUBLANES, sw), lambda q: (0, 0, q))
    nstate = STATE * N_GROUPS
    return pl.pallas_call(
        body, name=name, grid=(nq,),
        in_specs=[pl.BlockSpec((tp, LANES), lambda q: (0, 3 * nq + q)), blk, blk, pl.BlockSpec(memory_space=pl.ANY),
                  bspec, bspec, cspec, cspec, pl.BlockSpec((1, LANES), lambda q: (0, q)), tspec, tspec],
        out_specs=[pl.BlockSpec(memory_space=pl.ANY), bspec, bspec, cspec, cspec,
                   pl.BlockSpec((1, LANES), lambda q: (0, q)),
                   pl.BlockSpec((1, sw), lambda q: (0, q)), pl.BlockSpec((1, sw), lambda q: (0, q))],
        out_shape=[jax.ShapeDtypeStruct((tp, 4 * dh), BF16),
                   jax.ShapeDtypeStruct((LANES, nstate), F32), jax.ShapeDtypeStruct((LANES, nstate), F32),
                   jax.ShapeDtypeStruct((nstate, LANES), F32), jax.ShapeDtypeStruct((nstate, LANES), F32),
                   jax.ShapeDtypeStruct((1, dh), F32),
                   jax.ShapeDtypeStruct((1, nstate), F32), jax.ShapeDtypeStruct((1, nstate), F32)],
        input_output_aliases={3: 0},
        scratch_shapes=[pltpu.VMEM((tp, sw), F32)] * 4 + [pltpu.VMEM((tp, LANES), BF16), pltpu.SemaphoreType.DMA],
        compiler_params=_cparams("arbitrary"),
    )(proj, y, dg, dproj, bc_re, bc_im, cc_re, cc_im, dskip, tab_f, tab_r)


FFN_TILE = 256


def _ffn_act(up, fw, fb, name):
    tp, two_ff = up.shape
    dff = two_ff // 2
    tc = FFN_TILE
    nj = dff // tc

    def body(ua_ref, uv_ref, wa_ref, wv_ref, ba_ref, bv_ref, act_ref):
        a = _dwconv(ua_ref[...], wa_ref) + ba_ref[...]
        v = _dwconv(uv_ref[...], wv_ref) + bv_ref[...]
        act_ref[...] = (a * _sigmoid(a) * v).astype(BF16)

    lo = lambda r: pl.BlockSpec((r, tc), lambda j: (0, j))
    hi = lambda r: pl.BlockSpec((r, tc), lambda j: (0, nj + j))
    return pl.pallas_call(
        body, name=name, grid=(nj,),
        in_specs=[lo(tp), hi(tp), lo(3), hi(3), lo(1), hi(1)],
        out_specs=lo(tp),
        out_shape=jax.ShapeDtypeStruct((tp, dff), BF16),
        compiler_params=_cparams("parallel"))(up, up, fw, fw, fb, fb)


def _ffn_bwd(up, dact, fw, fb, name):
    tp, two_ff = up.shape
    dff = two_ff // 2
    tc = FFN_TILE
    nj = dff // tc

    def body(ua_ref, uv_ref, da_ref, wa_ref, wv_ref, ba_ref, bv_ref,
             dup_ref, dwa_ref, dwv_ref, dba_ref, dbv_ref, stage, sem):
        j = pl.program_id(0)
        ua = ua_ref[...]
        uv = uv_ref[...]
        a = _dwconv(ua, wa_ref) + ba_ref[...]
        v = _dwconv(uv, wv_ref) + bv_ref[...]
        sg = _sigmoid(a)
        dact_v = da_ref[...]
        da = dact_v * v * sg * (1.0 + a * (1.0 - sg))
        dv = dact_v * a * sg
        dba_ref[...] = jnp.sum(da, axis=0, keepdims=True)
        dbv_ref[...] = jnp.sum(dv, axis=0, keepdims=True)
        dua, dwa = _dwconv_bwd(ua, da, wa_ref)
        duv, dwv = _dwconv_bwd(uv, dv, wv_ref)
        dwa_ref[...] = dwa
        dwv_ref[...] = dwv
        stage[0] = dua.astype(BF16)
        stage[1] = duv.astype(BF16)
        copies = [pltpu.make_async_copy(stage.at[p], dup_ref.at[:, pl.ds((p * nj + j) * tc, tc)], sem.at[p])
                  for p in range(2)]
        for cp in copies:
            cp.start()
        for cp in copies:
            cp.wait()

    lo = lambda r: pl.BlockSpec((r, tc), lambda j: (0, j))
    hi = lambda r: pl.BlockSpec((r, tc), lambda j: (0, nj + j))
    return pl.pallas_call(
        body, name=name, grid=(nj,),
        in_specs=[lo(tp), hi(tp), lo(tp), lo(3), hi(3), lo(1), hi(1)],
        out_specs=[pl.BlockSpec(memory_space=pl.ANY), lo(3), lo(3), lo(1), lo(1)],
        out_shape=[jax.ShapeDtypeStruct((tp, two_ff), BF16),
                   jax.ShapeDtypeStruct((3, dff), F32), jax.ShapeDtypeStruct((3, dff), F32),
                   jax.ShapeDtypeStruct((1, dff), F32), jax.ShapeDtypeStruct((1, dff), F32)],
        scratch_shapes=[pltpu.VMEM((2, tp, tc), BF16), pltpu.SemaphoreType.DMA((2,))],
        compiler_params=_cparams("arbitrary"))(up, up, dact, fw, fw, fb, fb)


def _zoh(lr, li, ld):
    dt = jnp.exp(ld)
    mag = jnp.exp(lr * dt)
    ang = li * dt
    ar = mag * jnp.cos(ang)
    ai = mag * jnp.sin(ang)
    den = lr * lr + li * li
    nr = ar - 1.0
    fr = (nr * lr + ai * li) / den
    fi = (ai * lr - nr * li) / den
    return dt, ar, ai, den, nr, fr, fi


def _s5_prep(lr, li, ld, b_re, b_im, name):
    nstate = lr.shape[1]

    def tables(tab_ref, ar, ai, reverse):
        pows = [(ar, ai)]
        for _ in range(SUBLANES - 1):
            pr, pi = pows[-1]
            pows.append((pr * ar - pi * ai, pr * ai + pi * ar))
        row = lax.broadcasted_iota(jnp.int32, (SUBLANES, nstate), 0)
        for level, k in enumerate((1, 2, 4)):
            mask = (row <= SUBLANES - 1 - k) if reverse else (row >= k)
            tab_ref[2 * level] = jnp.where(mask, pows[k - 1][0], 0.0)
            tab_ref[2 * level + 1] = jnp.where(mask, pows[k - 1][1], 0.0)
        pr = jnp.zeros((SUBLANES, nstate), F32)
        pi = jnp.zeros((SUBLANES, nstate), F32)
        for t in range(SUBLANES):
            k = SUBLANES - 1 - t if reverse else t
            pr = jnp.where(row == t, pows[k][0], pr)
            pi = jnp.where(row == t, pows[k][1], pi)
        tab_ref[6] = pr
        tab_ref[7] = pi

    def body(lr_ref, li_ref, ld_ref, bre_ref, bim_ref, tabf_ref, tabr_ref, bcre_ref, bcim_ref):
        _, ar, ai, _, _, fr, fi = _zoh(lr_ref[...], li_ref[...], ld_ref[...])
        tables(tabf_ref, ar, ai, False)
        tables(tabr_ref, ar, -ai, True)
        bre = bre_ref[...]
        bim = bim_ref[...]
        bcre_ref[...] = (fr * bre - fi * bim).astype(BF16)
        bcim_ref[...] = (fr * bim + fi * bre).astype(BF16)

    vmem = pl.BlockSpec(memory_space=pltpu.VMEM)
    return pl.pallas_call(
        body, name=name, in_specs=[vmem] * 5, out_specs=[vmem] * 4,
        out_shape=[jax.ShapeDtypeStruct((8, SUBLANES, nstate), F32)] * 2
        + [jax.ShapeDtypeStruct(b_re.shape, BF16)] * 2)(lr, li, ld, b_re, b_im)


def _s5_prep_bwd(lr, li, ld, b_re, b_im, da_re, da_im, dbc_re, dbc_im, name):
    def body(lr_ref, li_ref, ld_ref, bre_ref, bim_ref, dar_ref, dai_ref, dbcre_ref, dbcim_ref,
             dlr_ref, dli_ref, dld_ref, dbre_ref, dbim_ref):
        lr, li = lr_ref[...], li_ref[...]
        dt, ar, ai, den, nr, fr, fi = _zoh(lr, li, ld_ref[...])
        bre, bim = bre_ref[...], bim_ref[...]
        gre, gim = dbcre_ref[...], dbcim_ref[...]
        dbre_ref[...] = fr * gre + fi * gim
        dbim_ref[...] = fr * gim - fi * gre
        g_fr = jnp.sum(gre * bre + gim * bim, axis=0, keepdims=True)
        g_fi = jnp.sum(gim * bre - gre * bim, axis=0, keepdims=True)
        g_ar = dar_ref[...] + (g_fr * lr - g_fi * li) / den
        g_ai = dai_ref[...] + (g_fr * li + g_fi * lr) / den
        d_lr = (g_fr * (nr - 2.0 * fr * lr) + g_fi * (ai - 2.0 * fi * lr)) / den
        d_li = (g_fr * (ai - 2.0 * fr * li) - g_fi * (nr + 2.0 * fi * li)) / den
        g_logmag = g_ar * ar + g_ai * ai
        g_ang = g_ai * ar - g_ar * ai
        dlr_ref[...] = d_lr + g_logmag * dt
        dli_ref[...] = d_li + g_ang * dt
        d_ld = (g_logmag * lr + g_ang * li) * dt
        n = d_ld.shape[1]
        sh = 1
        while sh < STATE:
            d_ld = d_ld + pltpu.roll(d_ld, n - sh, 1)
            sh *= 2
        dld_ref[...] = d_ld

    vmem = pl.BlockSpec(memory_space=pltpu.VMEM)
    row = jax.ShapeDtypeStruct(lr.shape, F32)
    return pl.pallas_call(
        body, name=name, in_specs=[vmem] * 9, out_specs=[vmem] * 5,
        out_shape=[row, row, row, jax.ShapeDtypeStruct(b_re.shape, F32), jax.ShapeDtypeStruct(b_re.shape, F32)],
    )(lr, li, ld, b_re, b_im, da_re, da_im, dbc_re, dbc_im)


def _compact_b(bb):
    bq = bb.reshape(N_GROUPS // 8, 8, STATE, GROUP)
    m = jnp.einsum("ab,qbph->qahbp", jnp.eye(8, dtype=bb.dtype), bq).reshape(N_GROUPS // 8, LANES, 8 * STATE)
    return m.transpose(1, 0, 2).reshape(LANES, N_GROUPS * STATE)


def _expand_b(m):
    d = m.reshape(8, GROUP, N_GROUPS // 8, 8, STATE)
    return jnp.einsum("ahqap->qaph", d).reshape(N_GROUPS, STATE, GROUP)


def _compact_c(c):
    cq = c.reshape(N_GROUPS // 8, 8, GROUP, STATE)
    return jnp.einsum("ab,qbhp->qbpah", jnp.eye(8, dtype=c.dtype), cq).reshape(N_GROUPS * STATE, LANES)


def _expand_c(m):
    d = m.reshape(N_GROUPS // 8, 8, STATE, 8, GROUP)
    return jnp.einsum("qbpbh->qbhp", d).reshape(N_GROUPS, GROUP, STATE)


def _local_step(x, target, p, late_weights, on_ffn_grads):
    seq, d = x.shape
    n_real = N_META + seq
    tp = -(-n_real // ROW_ALIGN) * ROW_ALIGN
    pad = jnp.zeros((tp - n_real, d), F32)
    h0 = jnp.concatenate([p["meta_tokens"], x, pad], axis=0)
    tgt = jnp.concatenate([jnp.zeros((N_META, d), F32), target, pad], axis=0)

    nstate = N_GROUPS * STATE
    s5 = (p["ssm_lam_re"].reshape(1, nstate), p["ssm_lam_im"].reshape(1, nstate),
          jnp.repeat(p["ssm_log_dt"].reshape(-1), STATE).reshape(1, nstate),
          _compact_b(p["ssm_b_re"]), _compact_b(p["ssm_b_im"]))
    tab_f, tab_r, bc_re, bc_im = _s5_prep(*s5, "s5_prep")
    cc_re = _compact_c(p["ssm_c_re"]).astype(BF16)
    cc_im = _compact_c(p["ssm_c_im"]).astype(BF16)
    dskip = p["ssm_d"].reshape(1, -1)
    dh = dskip.shape[1]

    hn1 = _norm_fwd(h0, p["norm_mix_g"], "norm_mix")
    proj = _mm(hn1, p["w_in"], "nn", "proj")
    co, y, g = _seq_fwd(proj, p["conv_w"], bc_re, bc_im, cc_re, cc_im, dskip, tab_f, "seq_fwd")
    z = _mm(g, p["ssm_w_glu"], "nn", "glu")
    mixed = _mix_fwd(co, y, z, p["gain_conv_out"], p["gain_ssm_out"], "mix_fwd")
    mo = _mm(mixed, p["w_out"], "nn", "out_proj")
    h1, hn2 = _norm_fwd(h0, p["norm_ffn_g"], "norm_ffn", res=mo)
    late = late_weights(hn2)
    up = _mm(hn2, late["w_up"], "nn", "up_proj")
    act = _ffn_act(up, p["ffn_conv_w"], p["ffn_conv_b"], "ffn_act")
    dn = _mm(act, late["w_down"], "nn", "down_proj")
    loss, dh2, dh2b, d_gfin = _loss_bwd(h1, dn, tgt, p["norm_final_g"], n_real, "loss_bwd")

    g_w_down = _mm(act, dh2b, "tn", "g_w_down")
    dact = _mm(dh2b, late["w_down"], "nt", "d_act")
    dup, dfw_a, dfw_v, dfb_a, dfb_v = _ffn_bwd(up, dact, p["ffn_conv_w"], p["ffn_conv_b"], "ffn_bwd")
    g_w_up = _mm(hn2, dup, "tn", "g_w_up")
    dhn2 = _mm(dup, late["w_up"], "nt", "d_hn2")
    zero = on_ffn_grads(g_w_up, g_w_down)
    dh1, dh1b, d_gffn = _norm_bwd(h1, p["norm_ffn_g"] + zero, dhn2, dh2, "norm_ffn_bwd")
    g_w_out = _mm(mixed, dh1b, "tn", "g_w_out")
    dmixed = _mm(dh1b, p["w_out"], "nt", "d_mixed")
    dco, dz, dgp, d_gc, d_gs = _mix_bwd(dmixed, co, y, z, p["gain_conv_out"], p["gain_ssm_out"], "mix_bwd")
    g_w_glu = _mm(g, dz, "tn", "g_w_glu")
    dg = _mm(dz, p["ssm_w_glu"], "nt", "d_gelu", acc_in=dgp)
    dproj, d_conv_w = _conv_bwd(proj, dco, p["conv_w"], "conv_bwd")
    (dproj, dbc_re, dbc_im, dcc_re, dcc_im, d_dskip, da_re, da_im) = _ssm_bwd(
        proj, y, dg, dproj, bc_re, bc_im, cc_re, cc_im, dskip, tab_f, tab_r, "ssm_bwd")
    g_w_in = _mm(hn1, dproj, "tn", "g_w_in")
    dhn1 = _mm(dproj, p["w_in"], "nt", "d_hn1")
    grad_x, d_meta, d_gmix = _input_norm_bwd(h0, p["norm_mix_g"], dhn1, dh1, n_real, "norm_mix_bwd")

    d_lam_re, d_lam_im, d_log_dt, d_b_re, d_b_im = _s5_prep_bwd(*s5, da_re, da_im, dbc_re, dbc_im, "s5_prep_bwd")
    d_lam_re, d_lam_im = d_lam_re.reshape(N_GROUPS, STATE), d_lam_im.reshape(N_GROUPS, STATE)
    d_log_dt = d_log_dt[0, ::STATE]
    d_b_re, d_b_im = _expand_b(d_b_re), _expand_b(d_b_im)
    grads = {
        "meta_tokens": d_meta, "norm_mix_g": d_gmix, "w_in": g_w_in, "conv_w": d_conv_w,
        "ssm_lam_re": d_lam_re, "ssm_lam_im": d_lam_im, "ssm_log_dt": d_log_dt,
        "ssm_b_re": d_b_re, "ssm_b_im": d_b_im, "ssm_c_re": _expand_c(dcc_re), "ssm_c_im": _expand_c(dcc_im),
        "ssm_d": d_dskip.reshape(N_GROUPS, GROUP), "ssm_w_glu": g_w_glu,
        "gain_conv_out": d_gc, "gain_ssm_out": d_gs, "w_out": g_w_out, "norm_ffn_g": d_gffn,
        "w_up": g_w_up, "ffn_conv_w": jnp.concatenate([dfw_a, dfw_v], axis=1),
        "ffn_conv_b": jnp.concatenate([dfb_a, dfb_v], axis=1), "w_down": g_w_down, "norm_final_g": d_gfin,
    }
    return loss[0, 0], grad_x, grads


def _view(ref, axis, start, size):
    idx = [slice(None)] * len(ref.shape)
    idx[axis] = pl.ds(start, size)
    return ref.at[tuple(idx)]


def _exchange(name, ins, outs, aliases, local_copies, remote_copies):
    ni, no = len(ins), len(outs)
    nl, nr = len(local_copies), len(remote_copies)

    def body(*refs):
        in_refs, out_refs = refs[:ni], refs[ni:ni + no]
        send_sems, recv_sems, local_sems = refs[ni + no:]
        x, y, c = lax.axis_index("x"), lax.axis_index("y"), lax.axis_index("c")
        pos = (x, y, c, 2 * x + y)
        locals_ = [pltpu.make_async_copy(s(in_refs, out_refs, pos), d(in_refs, out_refs, pos), local_sems.at[i])
                   for i, (s, d) in enumerate(local_copies)]
        remotes = []
        for i, (s, d, flip) in enumerate(remote_copies):
            peer = (1 - x if "x" in flip else x, 1 - y if "y" in flip else y, 1 - c if "c" in flip else c)
            remotes.append(pltpu.make_async_remote_copy(
                src_ref=s(in_refs, out_refs, pos), dst_ref=d(in_refs, out_refs, pos),
                send_sem=send_sems.at[i], recv_sem=recv_sems.at[i], device_id=peer, device_id_type=MESH))
        for cp in locals_ + remotes:
            cp.start()
        for cp in remotes:
            cp.wait_recv()
        for cp in remotes:
            cp.wait_send()
        for cp in locals_:
            cp.wait()

    hbm = pl.BlockSpec(memory_space=pl.ANY)
    return pl.pallas_call(
        body, name=name, in_specs=[hbm] * ni, out_specs=[hbm] * no, out_shape=outs,
        input_output_aliases=aliases,
        scratch_shapes=[pltpu.SemaphoreType.DMA((nr,)), pltpu.SemaphoreType.DMA((nr,)),
                        pltpu.SemaphoreType.DMA((max(nl, 1),))],
    )(*ins)


BIG = {"w_in": (0, 1), "ssm_w_glu": (1, 0), "w_out": (1, 0), "w_up": (0, 1), "w_down": (1, 0)}
BIG_NAMES = tuple(BIG)
FLIPS = ("y", "x", "xy")


def _peer_chip(pos, flip):
    x, y, _, _ = pos
    return 2 * (1 - x if "x" in flip else x) + (1 - y if "y" in flip else y)


def _block_rows(rows, cols, itemsize, mult):
    return _pick_tile(rows, max(mult, (2 * 1024 * 1024) // (cols * itemsize)), mult)


def _cast_into_full(w, kc, shard_axis, name):
    r, cdim = w.shape
    tr = _block_rows(r, cdim, 4, 16)
    nb = r // tr

    def body(kc_ref, w_ref, o_ref):
        o_ref[...] = w_ref[...].astype(BF16)

    if shard_axis == 1:
        full, o_spec = (r, 4 * cdim), pl.BlockSpec((tr, cdim), lambda i, kc: (i, kc[0]))
    else:
        full, o_spec = (4 * r, cdim), pl.BlockSpec((tr, cdim), lambda i, kc: (kc[0] * nb + i, 0))
    return pl.pallas_call(
        body, name=name,
        grid_spec=pltpu.PrefetchScalarGridSpec(
            num_scalar_prefetch=1, grid=(nb,), in_specs=[pl.BlockSpec((tr, cdim), lambda i, kc: (i, 0))],
            out_specs=o_spec),
        out_shape=jax.ShapeDtypeStruct(full, BF16), compiler_params=_cparams("parallel"))(kc, w)


def _pair_sum(g, recv, kc, half_axis, name, out_dtype):
    hr, hc = recv.shape
    tr = _block_rows(hr, hc, 4, 16)
    nb = hr // tr

    def body(kc_ref, g_ref, r_ref, o_ref):
        o_ref[...] = (g_ref[...] + r_ref[...]).astype(out_dtype)

    if half_axis == 0:
        g_spec = pl.BlockSpec((tr, hc), lambda i, kc: (kc[1] * nb + i, 0))
    elif half_axis == 1:
        g_spec = pl.BlockSpec((tr, hc), lambda i, kc: (i, kc[1]))
    else:
        g_spec = pl.BlockSpec((tr, hc), lambda i, kc: (i, 0))
    same = pl.BlockSpec((tr, hc), lambda i, kc: (i, 0))
    return pl.pallas_call(
        body, name=name,
        grid_spec=pltpu.PrefetchScalarGridSpec(num_scalar_prefetch=1, grid=(nb,), in_specs=[g_spec, same],
                                               out_specs=same),
        out_shape=jax.ShapeDtypeStruct((hr, hc), out_dtype), compiler_params=_cparams("parallel"))(kc, g, recv)


def _chip_sum(own, recv, kc, own_axis, out_axis, name):
    _, sr, sc = recv.shape
    tr = _block_rows(sr, sc, 4, 16)
    nb = sr // tr

    def body(kc_ref, o_ref, r_ref, t_ref):
        k = kc_ref[0]
        own_v = o_ref[...].astype(F32)
        r = [r_ref[m].astype(F32) for m in range(3)]
        terms = []
        for kk in range(4):
            m = jnp.bitwise_xor(k, kk)
            terms.append(jnp.where(m == 0, own_v, jnp.where(m == 1, r[0], jnp.where(m == 2, r[1], r[2]))))
        t_ref[...] = (terms[0] + terms[1]) + (terms[2] + terms[3])

    if own_axis == 0:
        own_spec = pl.BlockSpec((tr, sc), lambda i, kc: (kc[0] * nb + i, 0))
    elif own_axis == 1:
        own_spec = pl.BlockSpec((tr, sc), lambda i, kc: (i, kc[0]))
    else:
        own_spec = pl.BlockSpec((tr, sc), lambda i, kc: (kc[1] * nb + i, 0))
    if out_axis == 0:
        out_full, out_spec = (2 * sr, sc), pl.BlockSpec((tr, sc), lambda i, kc: (kc[1] * nb + i, 0))
    else:
        out_full, out_spec = (sr, 2 * sc), pl.BlockSpec((tr, sc), lambda i, kc: (i, kc[1]))
    return pl.pallas_call(
        body, name=name,
        grid_spec=pltpu.PrefetchScalarGridSpec(
            num_scalar_prefetch=1, grid=(nb,),
            in_specs=[own_spec, pl.BlockSpec((3, tr, sc), lambda i, kc: (0, i, 0))],
            out_specs=out_spec),
        out_shape=jax.ShapeDtypeStruct(out_full, F32), compiler_params=_cparams("parallel"))(kc, own, recv)


def _adamw(w, g, m, v, name):
    r, cdim = w.shape
    tr = _block_rows(r, cdim, 4, 8)
    c1 = 1.0 - ADAM_B1 ** ADAM_STEP
    c2 = 1.0 - ADAM_B2 ** ADAM_STEP

    def body(w_ref, g_ref, m_ref, v_ref, d_ref, nm_ref, nv_ref):
        gv = g_ref[...]
        nm = ADAM_B1 * m_ref[...] + (1.0 - ADAM_B1) * gv
        nv = ADAM_B2 * v_ref[...] + (1.0 - ADAM_B2) * (gv * gv)
        d_ref[...] = -ADAM_LR * ((nm / c1) / (jnp.sqrt(nv / c2) + ADAM_EPS) + ADAM_WD * w_ref[...])
        nm_ref[...] = nm
        nv_ref[...] = nv

    spec = _rows(cdim, tr)
    return pl.pallas_call(body, name=name, grid=(r // tr,), in_specs=[spec] * 4, out_specs=[spec] * 3,
                          out_shape=[jax.ShapeDtypeStruct((r, cdim), F32)] * 3,
                          compiler_params=_cparams("parallel"))(w, g, m, v)


def _adamw_whole(ws, gs, ms, vs, name):
    n = len(ws)
    c1 = 1.0 - ADAM_B1 ** ADAM_STEP
    c2 = 1.0 - ADAM_B2 ** ADAM_STEP

    def body(*refs):
        for i in range(n):
            w_ref, g_ref, m_ref, v_ref, d_ref, nm_ref, nv_ref = [refs[j * n + i] for j in range(7)]
            gv = g_ref[...]
            nm = ADAM_B1 * m_ref[...] + (1.0 - ADAM_B1) * gv
            nv = ADAM_B2 * v_ref[...] + (1.0 - ADAM_B2) * (gv * gv)
            d_ref[...] = -ADAM_LR * ((nm / c1) / (jnp.sqrt(nv / c2) + ADAM_EPS) + ADAM_WD * w_ref[...])
            nm_ref[...] = nm
            nv_ref[...] = nv

    vmem = pl.BlockSpec(memory_space=pltpu.VMEM)
    out = pl.pallas_call(body, name=name, in_specs=[vmem] * (4 * n), out_specs=[vmem] * (3 * n),
                         out_shape=[jax.ShapeDtypeStruct(a.shape, F32) for a in ws] * 3,
                         compiler_params=pltpu.CompilerParams(vmem_limit_bytes=VMEM_LIMIT))(*ws, *gs, *ms, *vs)
    return out[:n], out[n:2 * n], out[2 * n:]


SIDE_EFFECT = pltpu.SideEffectType.DATAFLOW_SIDE_EFFECTING


def _descriptors(copies, refs, send_sems, recv_sems):
    x, y, c = lax.axis_index("x"), lax.axis_index("y"), lax.axis_index("c")
    pos = (x, y, c, 2 * x + y)
    out = []
    for i, (s, d, flip) in enumerate(copies):
        peer = (1 - x if "x" in flip else x, 1 - y if "y" in flip else y, 1 - c if "c" in flip else c)
        out.append(pltpu.make_async_remote_copy(
            src_ref=s(refs, refs, pos), dst_ref=d(refs, refs, pos),
            send_sem=send_sems.at[i], recv_sem=recv_sems.at[i], device_id=peer, device_id_type=MESH))
    return out


def _exchange_start(name, bufs, copies, after):
    n, nr = len(bufs), len(copies)

    def body(*refs):
        for cp in _descriptors(copies, refs[:n], refs[n + 1], refs[n + 2]):
            cp.start()
        token = refs[2 * n + 3]
        token[...] = jnp.zeros_like(token)

    hbm = pl.BlockSpec(memory_space=pltpu.HBM)
    sem = pl.BlockSpec(memory_space=pltpu.SEMAPHORE)
    out = pl.pallas_call(
        body, name=name,
        in_specs=[hbm] * n + [pl.BlockSpec(memory_space=pl.ANY)],
        out_specs=(sem, sem, *[hbm] * n, pl.BlockSpec(memory_space=pltpu.VMEM)),
        out_shape=(pltpu.SemaphoreType.DMA((nr,)), pltpu.SemaphoreType.DMA((nr,)),
                   *[pltpu.HBM(b.shape, b.dtype) for b in bufs], jax.ShapeDtypeStruct((SUBLANES, LANES), F32)),
        input_output_aliases={i: 2 + i for i in range(n)},
        compiler_params=pltpu.CompilerParams(has_side_effects=SIDE_EFFECT),
    )(*[pltpu.with_memory_space_constraint(b, pltpu.HBM) for b in bufs], after)
    return out[0], out[1], list(out[2:2 + n]), out[2 + n]


def _exchange_wait(name, send_sems, recv_sems, bufs, copies, after):
    n = len(bufs)

    def body(*refs):
        for cp in _descriptors(copies, refs[:n], refs[n], refs[n + 1]):
            cp.wait_send()
            cp.wait_recv()

    hbm = pl.BlockSpec(memory_space=pltpu.HBM)
    sem = pl.BlockSpec(memory_space=pltpu.SEMAPHORE)
    out = pl.pallas_call(
        body, name=name,
        in_specs=[hbm] * n + [sem, sem, pl.BlockSpec(memory_space=pl.ANY)],
        out_specs=tuple([hbm] * n),
        out_shape=tuple(pltpu.HBM(b.shape, b.dtype) for b in bufs),
        input_output_aliases={i: i for i in range(n)},
        compiler_params=pltpu.CompilerParams(has_side_effects=SIDE_EFFECT),
    )(*bufs, send_sems, recv_sems, after)
    return list(out)


EARLY = ("w_in", "ssm_w_glu", "w_out")
LATE = ("w_up", "w_down")


def _gather_copies(names, shard_shapes):
    def region(i, chip, c):
        half_axis, shard_axis = BIG[names[i]]
        ssize = shard_shapes[i][shard_axis]
        hsize = shard_shapes[i][half_axis] // 2
        return lambda ref: _view(_view(ref, shard_axis, chip * ssize, ssize), half_axis, c * hsize, hsize)

    ici, d2d = [], []
    for i in range(len(names)):
        for flip in FLIPS:
            ici.append((lambda I, O, pos, i=i: region(i, pos[3], pos[2])(I[i]),
                        lambda I, O, pos, i=i: region(i, pos[3], pos[2])(O[i]), flip))
            d2d.append((lambda I, O, pos, i=i, flip=flip: region(i, _peer_chip(pos, flip), pos[2])(I[i]),
                        lambda I, O, pos, i=i, flip=flip: region(i, _peer_chip(pos, flip), pos[2])(O[i]), "c"))
    return ici, d2d


def _gather_early(shards, tiny, kc):
    wb = {n: _cast_into_full(shards[n], kc, BIG[n][1], "cast_" + n) for n in BIG_NAMES}
    ne = len(EARLY)
    ici, d2d = _gather_copies(EARLY, [shards[n].shape for n in EARLY])
    local = [(lambda I, O, pos: I[ne], lambda I, O, pos: O[ne].at[pos[3]])]
    ici += [(lambda I, O, pos: I[ne], lambda I, O, pos: O[ne].at[pos[3]], flip) for flip in FLIPS]
    outs = [jax.ShapeDtypeStruct(wb[n].shape, BF16) for n in EARLY] + [jax.ShapeDtypeStruct((4,) + tiny.shape, F32)]
    got = _exchange("gather_ici", [wb[n] for n in EARLY] + [tiny], outs, {i: i for i in range(ne)}, local, ici)
    full = _exchange("gather_d2d", list(got[:ne]), outs[:ne], {i: i for i in range(ne)}, [], d2d)
    ici, d2d = _gather_copies(LATE, [shards[n].shape for n in LATE])
    started = _exchange_start("gather_late_start", [wb[n] for n in LATE], ici, full[0])
    return dict(zip(EARLY, full)), got[ne], (started, ici, d2d)


def _gather_late(pending, after):
    (send_sems, recv_sems, bufs, _), ici, d2d = pending
    got = _exchange_wait("gather_late_wait", send_sems, recv_sems, bufs, ici, after)
    outs = [jax.ShapeDtypeStruct(b.shape, BF16) for b in got]
    full = _exchange("gather_late_d2d", got, outs, {i: i for i in range(len(got))}, [], d2d)
    return dict(zip(LATE, full))


def _half_shape(n, shape):
    r, cdim = shape
    return (r // 2, cdim) if BIG[n][0] == 0 else (r, cdim // 2)


def _sub_shape(n, shape):
    hr, hc = _half_shape(n, shape)
    return (hr, hc // 4) if BIG[n][1] == 1 else (hr // 4, hc)


def _pair_exchange(names, gs, pack, kc, tag, after=None):
    n = len(names)

    def other_half(i, ref, pos):
        half_axis = BIG[names[i]][0]
        hsize = gs[i].shape[half_axis] // 2
        return _view(ref, half_axis, (1 - pos[2]) * hsize, hsize)

    remote = [(lambda I, O, pos, i=i: other_half(i, I[i], pos), lambda I, O, pos, i=i: O[i], "c") for i in range(n)]
    ins = list(gs)
    outs = [jax.ShapeDtypeStruct(_half_shape(names[i], gs[i].shape), F32) for i in range(n)]
    if pack is not None:
        remote.append((lambda I, O, pos: I[n], lambda I, O, pos: O[n], "c"))
        ins.append(pack)
        outs.append(jax.ShapeDtypeStruct(pack.shape, F32))
    if after is not None:
        ins.append(after)
    recv = _exchange("reduce_d2d" + tag, ins, outs, {}, [], remote)
    chip = [_pair_sum(gs[i], recv[i], kc, BIG[names[i]][0], "pair_sum_" + names[i], BF16) for i in range(n)]
    chip_pack = _pair_sum(pack, recv[n], kc, None, "pair_sum_pack", F32) if pack is not None else None
    return chip, chip_pack


def _chip_copies(names, shapes, pack_rows, dst_off):
    n = len(names)

    def piece(i, ref, chip):
        shard_axis = BIG[names[i]][1]
        ssize = _sub_shape(names[i], shapes[i])[shard_axis]
        return _view(ref, shard_axis, chip * ssize, ssize)

    copies = []
    for i in range(n):
        for slot, flip in enumerate(FLIPS):
            copies.append((lambda I, O, pos, i=i, flip=flip: piece(i, I[i], _peer_chip(pos, flip)),
                           lambda I, O, pos, i=i, slot=slot: O[dst_off + i].at[slot], flip))
    if pack_rows:
        for slot, flip in enumerate(FLIPS):
            copies.append((lambda I, O, pos: _view(I[n], 0, pos[2] * (pack_rows // 2), pack_rows // 2),
                           lambda I, O, pos, slot=slot: O[dst_off + n].at[slot], flip))
    return copies


def _reduce_late_start(grads, kc):
    gs = [grads[n] for n in LATE]
    chip, _ = _pair_exchange(LATE, gs, None, kc, "_late")
    land = [lax.empty((3,) + _sub_shape(n, g.shape), BF16) for n, g in zip(LATE, gs)]
    copies = _chip_copies(LATE, [g.shape for g in gs], 0, len(LATE))
    started = _exchange_start("reduce_late_start", chip + land, copies, chip[0])
    return started, copies


def _reduce_finish(pending, grads, pack, kc, after):
    (send_sems, recv_sems, bufs, _), copies = pending
    bufs = _exchange_wait("reduce_late_wait", send_sems, recv_sems, bufs, copies, after)
    nl, ne = len(LATE), len(EARLY)
    chip_late, recv_late = bufs[:nl], bufs[nl:]

    gs = [grads[n] for n in EARLY]
    chip, chip_pack = _pair_exchange(EARLY, gs, pack, kc, "", after=recv_late[0])
    copies = _chip_copies(EARLY, [g.shape for g in gs], pack.shape[0], 0)
    outs = ([jax.ShapeDtypeStruct((3,) + _sub_shape(n, g.shape), BF16) for n, g in zip(EARLY, gs)]
            + [jax.ShapeDtypeStruct((3, pack.shape[0] // 2, pack.shape[1]), F32)])
    recv = _exchange("reduce_ici", chip + [chip_pack], outs, {}, [], copies)

    names = EARLY + LATE
    chips, recvs = chip + chip_late, list(recv[:ne]) + recv_late
    total = [_chip_sum(chips[i], recvs[i], kc, BIG[n][1], BIG[n][0], "chip_sum_" + n) for i, n in enumerate(names)]
    total.append(_chip_sum(chip_pack, recv[ne], kc, None, 0, "chip_sum_pack"))

    def my_half(i, ref, pos):
        half_axis = BIG[names[i]][0] if i < len(names) else 0
        hsize = ref.shape[half_axis] // 2
        return _view(ref, half_axis, pos[2] * hsize, hsize)

    remote = [(lambda I, O, pos, i=i: my_half(i, I[i], pos), lambda I, O, pos, i=i: my_half(i, O[i], pos), "c")
              for i in range(len(total))]
    outs = [jax.ShapeDtypeStruct(t.shape, F32) for t in total]
    out = _exchange("swap_d2d", total, outs, {i: i for i in range(len(total))}, [], remote)
    return dict(zip(names, out[:len(names)])), out[len(names)]


WEIGHTS = ("meta_tokens", "norm_mix_g", "w_in", "conv_w", "ssm_lam_re", "ssm_lam_im", "ssm_log_dt", "ssm_b_re",
           "ssm_b_im", "ssm_c_re", "ssm_c_im", "ssm_d", "ssm_w_glu", "gain_conv_out", "gain_ssm_out", "w_out",
           "norm_ffn_g", "w_up", "ffn_conv_w", "ffn_conv_b", "w_down", "norm_final_g")
TINY_SHARDED = ("meta_tokens", "conv_w", "ffn_conv_w")
REPLICATED = tuple(n for n in WEIGHTS if n not in BIG and n not in TINY_SHARDED)
PACK_COLS = 512


def _pack(arrays, row_mult, cols):
    flat = jnp.concatenate([a.reshape(-1).astype(F32) for a in arrays])
    n = flat.shape[0]
    total = -(-n // (row_mult * cols)) * (row_mult * cols)
    return jnp.concatenate([flat, jnp.zeros((total - n,), F32)]).reshape(total // cols, cols)


def _unpack(packed, shapes):
    flat = packed.reshape(-1)
    out, off = [], 0
    for s in shapes:
        n = math.prod(s)
        out.append(flat[off:off + n].reshape(s))
        off += n
    return out


def kernel(x, meta_tokens, norm_mix_g, w_in, conv_w, ssm_lam_re, ssm_lam_im, ssm_log_dt, ssm_b_re, ssm_b_im, ssm_c_re, ssm_c_im, ssm_d, ssm_w_glu, gain_conv_out, gain_ssm_out, w_out, norm_ffn_g, w_up, ffn_conv_w, ffn_conv_b, w_down, norm_final_g, loss_target, m_meta_tokens, m_norm_mix_g, m_w_in, m_conv_w, m_ssm_lam_re, m_ssm_lam_im, m_ssm_log_dt, m_ssm_b_re, m_ssm_b_im, m_ssm_c_re, m_ssm_c_im, m_ssm_d, m_ssm_w_glu, m_gain_conv_out, m_gain_ssm_out, m_w_out, m_norm_ffn_g, m_w_up, m_ffn_conv_w, m_ffn_conv_b, m_w_down, m_norm_final_g, v_meta_tokens, v_norm_mix_g, v_w_in, v_conv_w, v_ssm_lam_re, v_ssm_lam_im, v_ssm_log_dt, v_ssm_b_re, v_ssm_b_im, v_ssm_c_re, v_ssm_c_im, v_ssm_d, v_ssm_w_glu, v_gain_conv_out, v_gain_ssm_out, v_w_out, v_norm_ffn_g, v_w_up, v_ffn_conv_w, v_ffn_conv_b, v_w_down, v_norm_final_g):
    args = dict(locals())
    w = {n: args[n] for n in WEIGHTS}
    mom = {n: args["m_" + n] for n in WEIGHTS}
    var = {n: args["v_" + n] for n in WEIGHTS}
    kx, ky, kc_ = lax.axis_index("x"), lax.axis_index("y"), lax.axis_index("c")
    chip = 2 * kx + ky
    kc = jnp.stack([chip, kc_]).astype(jnp.int32)

    def squeeze(n, a):
        if n == "meta_tokens":
            return a
        if n == "norm_final_g":
            return a.reshape(1, -1)
        a = a[0]
        return a.reshape(1, -1) if a.ndim == 1 else a

    wl = {n: squeeze(n, w[n]) for n in WEIGHTS}
    ml = {n: squeeze(n, mom[n]) for n in WEIGHTS}
    vl = {n: squeeze(n, var[n]) for n in WEIGHTS}

    tiny = _pack([wl[n] for n in TINY_SHARDED], SUBLANES, LANES)
    full, tiny_all, gathering = _gather_early({n: wl[n] for n in BIG_NAMES}, tiny, kc)
    tiny_shapes = [wl[n].shape for n in TINY_SHARDED]
    tiny_parts = [_unpack(tiny_all[k], tiny_shapes) for k in range(4)]
    p = {n: wl[n] for n in WEIGHTS if n not in LATE}
    p.update(full)
    for j, n in enumerate(TINY_SHARDED):
        p[n] = jnp.concatenate([tiny_parts[k][j] for k in range(4)], axis=1)
    p["ssm_log_dt"] = wl["ssm_log_dt"].reshape(-1)
    p["norm_mix_g"] = wl["norm_mix_g"] + gathering[0][3][0, 0]

    reducing = []

    def on_ffn_grads(g_w_up, g_w_down):
        reducing.append(_reduce_late_start({"w_up": g_w_up, "w_down": g_w_down}, kc))
        return reducing[0][0][3][0, 0]

    loss_local, grad_x, grads = _local_step(x[0], loss_target[0], p, functools.partial(_gather_late, gathering),
                                            on_ffn_grads)

    small_names = REPLICATED + TINY_SHARDED
    small_shapes = [tuple(grads[n].shape) for n in small_names] + [(1,)]
    pack = _pack([grads[n] for n in small_names] + [loss_local.reshape(1)], 2 * 16, PACK_COLS)
    g_big, g_pack = _reduce_finish(reducing[0], {n: grads[n] for n in EARLY}, pack, kc, pack)
    g_small = dict(zip(small_names + ("loss",), _unpack(g_pack, small_shapes)))
    loss = g_small["loss"][0]
    g = dict(g_big)
    for n in REPLICATED:
        g[n] = g_small[n].reshape(w[n].shape)
    for n in TINY_SHARDED:
        cols = wl[n].shape[1]
        g[n] = lax.dynamic_slice_in_dim(g_small[n], chip * cols, cols, axis=1).reshape(w[n].shape)

    delta, new_m, new_v = {}, {}, {}
    for n in BIG_NAMES:
        delta[n], new_m[n], new_v[n] = _adamw(wl[n], g[n], ml[n], vl[n], "adamw_" + n)
    rank2 = lambda a: a.reshape(1, -1) if a.ndim == 1 else a
    small = [[rank2(d[n]) for n in small_names] for d in (w, g, mom, var)]
    for d, outs in zip((delta, new_m, new_v), _adamw_whole(*small, "adamw_small")):
        d.update(zip(small_names, outs))

    def like(n, a):
        return a.reshape(w[n].shape)

    return (loss, grad_x[None], *[like(n, g[n]) for n in WEIGHTS], *[like(n, delta[n]) for n in WEIGHTS],
            *[like(n, new_m[n]) for n in WEIGHTS], *[like(n, new_v[n]) for n in WEIGHTS])
```

```python
import functools
import math

import jax
import jax.numpy as jnp
from jax import lax
from jax.experimental import pallas as pl
from jax.experimental.pallas import tpu as pltpu

F32 = jnp.float32
BF16 = jnp.bfloat16
MESH = pl.DeviceIdType.MESH

N_META = 16
N_GROUPS = 32
GROUP = 16
STATE = 64
RMS_EPS = 1e-6
ADAM_LR = 0.001
ADAM_B1 = 0.9
ADAM_B2 = 0.999
ADAM_EPS = 1e-08
ADAM_WD = 0.01
ADAM_STEP = 10

LANES = 128
SUBLANES = 8
ROW_ALIGN = 128
ROW_TILES = 4
VMEM_LIMIT = 52 * 1024 * 1024
GELU_C = math.sqrt(2.0 / math.pi)
GELU_A = 0.044715


def _cparams(*sem):
    return pltpu.CompilerParams(dimension_semantics=sem, vmem_limit_bytes=VMEM_LIMIT)


def _pick_tile(dim, cap, mult):
    best = None
    for t in range(mult, min(dim, cap) + 1, mult):
        if dim % t == 0:
            best = t
    return best if best is not None else dim


def _mm(a, b, mode, name, out_dtype=F32, acc_in=None):
    if mode == "tn":
        kdim, m = a.shape
    else:
        m, kdim = a.shape
    n = b.shape[0] if mode == "nt" else b.shape[1]
    tm = _pick_tile(m, 1408, LANES if mode == "tn" else 16)
    tn = _pick_tile(n, 512, LANES)
    tk = _pick_tile(kdim, 2816, LANES)
    nk = kdim // tk
    has_acc = acc_in is not None

    def body(*refs):
        if has_acc:
            a_ref, b_ref, c_ref, o_ref = refs[:4]
            rest = refs[4:]
        else:
            a_ref, b_ref, o_ref = refs[:3]
            c_ref = None
            rest = refs[3:]
        if mode == "nn":
            p = jnp.dot(a_ref[...], b_ref[...], preferred_element_type=F32)
        elif mode == "nt":
            p = lax.dot_general(a_ref[...], b_ref[...], (((1,), (1,)), ((), ())), preferred_element_type=F32)
        else:
            p = lax.dot_general(a_ref[...], b_ref[...], (((0,), (0,)), ((), ())), preferred_element_type=F32)
        if nk == 1:
            if has_acc:
                p = p + c_ref[...]
            o_ref[...] = p.astype(out_dtype)
        else:
            acc_ref = rest[0]
            k = pl.program_id(2)

            @pl.when(k == 0)
            def _():
                acc_ref[...] = p + c_ref[...] if has_acc else p

            @pl.when(k > 0)
            def _():
                acc_ref[...] += p

            @pl.when(k == nk - 1)
            def _():
                o_ref[...] = acc_ref[...].astype(out_dtype)

    if mode == "tn":
        a_spec = pl.BlockSpec((tk, tm), lambda i, j, k: (k, i))
    else:
        a_spec = pl.BlockSpec((tm, tk), lambda i, j, k: (i, k))
    if mode == "nt":
        b_spec = pl.BlockSpec((tn, tk), lambda i, j, k: (j, k))
    else:
        b_spec = pl.BlockSpec((tk, tn), lambda i, j, k: (k, j))
    o_spec = pl.BlockSpec((tm, tn), lambda i, j, k: (i, j))
    in_specs = [a_spec, b_spec] + ([o_spec] if has_acc else [])
    args = (a, b) + ((acc_in,) if has_acc else ())
    return pl.pallas_call(
        body, name=name, grid=(m // tm, n // tn, nk),
        in_specs=in_specs, out_specs=o_spec,
        out_shape=jax.ShapeDtypeStruct((m, n), out_dtype),
        scratch_shapes=[pltpu.VMEM((tm, tn), F32)] if nk > 1 else [],
        compiler_params=_cparams("parallel", "parallel", "arbitrary"),
    )(*args)


def _rows(shape_cols, tr, dtype=None):
    return pl.BlockSpec((tr, shape_cols), lambda i: (i, 0))


def _const(shape):
    return pl.BlockSpec(shape, lambda i: (0,) * len(shape))


def _rms(x):
    return lax.rsqrt(jnp.mean(x * x, axis=-1, keepdims=True) + RMS_EPS)


def _rms_bwd(x, r, g, dy):
    xn = x * r
    dxn = dy * g
    dx = r * (dxn - xn * jnp.mean(dxn * xn, axis=-1, keepdims=True))
    return dx, dy * xn


def _gelu(y):
    return 0.5 * y * (1.0 + jnp.tanh(GELU_C * (y + GELU_A * y * y * y)))


def _gelu_grad(y):
    t = jnp.tanh(GELU_C * (y + GELU_A * y * y * y))
    return 0.5 * (1.0 + t) + 0.5 * y * (1.0 - t * t) * GELU_C * (1.0 + 3.0 * GELU_A * y * y)


def _sigmoid(z):
    return 1.0 / (1.0 + jnp.exp(-z))


def _norm_fwd(h, g, name, res=None):
    tp, d = h.shape
    tr = tp // ROW_TILES
    has_res = res is not None

    def body(*refs):
        if has_res:
            h_ref, r_ref, g_ref, s_ref, hn_ref = refs
            x = h_ref[...] + r_ref[...]
            s_ref[...] = x
        else:
            h_ref, g_ref, hn_ref = refs
            x = h_ref[...]
        hn_ref[...] = (x * _rms(x) * g_ref[...]).astype(BF16)

    in_specs = [_rows(d, tr)] + ([_rows(d, tr)] if has_res else []) + [_const((1, d))]
    out_specs = ([_rows(d, tr)] if has_res else []) + [_rows(d, tr)]
    out_shape = ([jax.ShapeDtypeStruct((tp, d), F32)] if has_res else []) + [jax.ShapeDtypeStruct((tp, d), BF16)]
    args = (h,) + ((res,) if has_res else ()) + (g,)
    out = pl.pallas_call(body, name=name, grid=(ROW_TILES,), in_specs=in_specs, out_specs=out_specs,
                         out_shape=out_shape, compiler_params=_cparams("parallel"))(*args)
    return out if has_res else out[0]


def _norm_bwd(h, g, dhn, dres, name):
    tp, d = h.shape
    tr = tp // ROW_TILES

    def body(h_ref, g_ref, dhn_ref, dres_ref, dh_ref, dhb_ref, dg_ref):
        x = h_ref[...]
        dx, dgs = _rms_bwd(x, _rms(x), g_ref[...], dhn_ref[...])
        dh = dres_ref[...] + dx
        dh_ref[...] = dh
        dhb_ref[...] = dh.astype(BF16)

        @pl.when(pl.program_id(0) == 0)
        def _():
            dg_ref[...] = jnp.zeros_like(dg_ref)

        dg_ref[...] += jnp.sum(dgs, axis=0, keepdims=True)

    return pl.pallas_call(
        body, name=name, grid=(ROW_TILES,),
        in_specs=[_rows(d, tr), _const((1, d)), _rows(d, tr), _rows(d, tr)],
        out_specs=[_rows(d, tr), _rows(d, tr), _const((1, d))],
        out_shape=[jax.ShapeDtypeStruct((tp, d), F32), jax.ShapeDtypeStruct((tp, d), BF16),
                   jax.ShapeDtypeStruct((1, d), F32)],
        compiler_params=_cparams("arbitrary"))(h, g, dhn, dres)


def _input_norm_bwd(h, g, dhn, dres, n_real, name):
    tp, d = h.shape
    tr = tp // ROW_TILES

    def body(h_ref, g_ref, dhn_ref, dres_ref, dx_ref, dmeta_ref, dg_ref, stage, sem):
        i = pl.program_id(0)
        x = h_ref[...]
        dx, dgs = _rms_bwd(x, _rms(x), g_ref[...], dhn_ref[...])
        stage[...] = dres_ref[...] + dx

        @pl.when(i == 0)
        def _():
            dg_ref[...] = jnp.zeros_like(dg_ref)
            dmeta_ref[...] = stage[:N_META, :]

        dg_ref[...] += jnp.sum(dgs, axis=0, keepdims=True)
        for t in range(ROW_TILES):
            lo, hi = max(t * tr, N_META), min((t + 1) * tr, n_real)
            if hi > lo:
                @pl.when(i == t)
                def _(t=t, lo=lo, hi=hi):
                    cp = pltpu.make_async_copy(stage.at[pl.ds(lo - t * tr, hi - lo), :],
                                               dx_ref.at[pl.ds(lo - N_META, hi - lo), :], sem)
                    cp.start()
                    cp.wait()

    return pl.pallas_call(
        body, name=name, grid=(ROW_TILES,),
        in_specs=[_rows(d, tr), _const((1, d)), _rows(d, tr), _rows(d, tr)],
        out_specs=[pl.BlockSpec(memory_space=pl.ANY), _const((N_META, d)), _const((1, d))],
        out_shape=[jax.ShapeDtypeStruct((n_real - N_META, d), F32), jax.ShapeDtypeStruct((N_META, d), F32),
                   jax.ShapeDtypeStruct((1, d), F32)],
        scratch_shapes=[pltpu.VMEM((tr, d), F32), pltpu.SemaphoreType.DMA],
        compiler_params=_cparams("arbitrary"))(h, g, dhn, dres)


def _loss_bwd(h1, dn, tgt, g, n_real, name):
    tp, d = h1.shape
    tr = tp // ROW_TILES

    def body(h1_ref, dn_ref, t_ref, g_ref, loss_ref, dh_ref, dhb_ref, dg_ref):
        i = pl.program_id(0)
        x = h1_ref[...] + dn_ref[...]
        r = _rms(x)
        row = i * tr + lax.broadcasted_iota(jnp.int32, (tr, d), 0)
        valid = (row >= N_META) & (row < n_real)
        e = jnp.where(valid, x * r * g_ref[...] - t_ref[...], 0.0)
        dx, dgs = _rms_bwd(x, r, g_ref[...], e * (1.0 / d))
        dh_ref[...] = dx
        dhb_ref[...] = dx.astype(BF16)

        @pl.when(i == 0)
        def _():
            dg_ref[...] = jnp.zeros_like(dg_ref)
            loss_ref[...] = jnp.zeros_like(loss_ref)

        dg_ref[...] += jnp.sum(dgs, axis=0, keepdims=True)
        loss_ref[...] += (0.5 / d) * jnp.sum(jnp.sum(e * e, axis=0, keepdims=True), axis=1, keepdims=True)

    return pl.pallas_call(
        body, name=name, grid=(ROW_TILES,),
        in_specs=[_rows(d, tr), _rows(d, tr), _rows(d, tr), _const((1, d))],
        out_specs=[_const((1, LANES)), _rows(d, tr), _rows(d, tr), _const((1, d))],
        out_shape=[jax.ShapeDtypeStruct((1, LANES), F32), jax.ShapeDtypeStruct((tp, d), F32),
                   jax.ShapeDtypeStruct((tp, d), BF16), jax.ShapeDtypeStruct((1, d), F32)],
        compiler_params=_cparams("arbitrary"))(h1, dn, tgt, g)


def _mix_fwd(co, y, z, gc, gs, name):
    tp, dh = co.shape
    tr = tp // ROW_TILES

    def body(co_ref, y_ref, z_ref, gc_ref, gs_ref, m_ref):
        c = co_ref[...]
        m_ref[:, :dh] = (c * _rms(c) * gc_ref[...]).astype(BF16)
        so = _gelu(y_ref[...]) * _sigmoid(z_ref[...])
        m_ref[:, dh:] = (so * _rms(so) * gs_ref[...]).astype(BF16)

    return pl.pallas_call(
        body, name=name, grid=(ROW_TILES,),
        in_specs=[_rows(dh, tr)] * 3 + [_const((1, dh))] * 2,
        out_specs=_rows(2 * dh, tr),
        out_shape=jax.ShapeDtypeStruct((tp, 2 * dh), BF16),
        compiler_params=_cparams("parallel"))(co, y, z, gc, gs)


def _mix_bwd(dm, co, y, z, gc, gs, name):
    tp, dh = co.shape
    tr = tp // ROW_TILES

    def body(dm_ref, co_ref, y_ref, z_ref, gc_ref, gs_ref, dco_ref, dz_ref, dgp_ref, dgc_ref, dgs_ref):
        c = co_ref[...]
        dco, dgc = _rms_bwd(c, _rms(c), gc_ref[...], dm_ref[:, :dh])
        dco_ref[...] = dco
        gl = _gelu(y_ref[...])
        sg = _sigmoid(z_ref[...])
        so = gl * sg
        dso, dgs = _rms_bwd(so, _rms(so), gs_ref[...], dm_ref[:, dh:])
        dz_ref[...] = (dso * gl * sg * (1.0 - sg)).astype(BF16)
        dgp_ref[...] = dso * sg

        @pl.when(pl.program_id(0) == 0)
        def _():
            dgc_ref[...] = jnp.zeros_like(dgc_ref)
            dgs_ref[...] = jnp.zeros_like(dgs_ref)

        dgc_ref[...] += jnp.sum(dgc, axis=0, keepdims=True)
        dgs_ref[...] += jnp.sum(dgs, axis=0, keepdims=True)

    return pl.pallas_call(
        body, name=name, grid=(ROW_TILES,),
        in_specs=[_rows(2 * dh, tr)] + [_rows(dh, tr)] * 3 + [_const((1, dh))] * 2,
        out_specs=[_rows(dh, tr), _rows(dh, tr), _rows(dh, tr), _const((1, dh)), _const((1, dh))],
        out_shape=[jax.ShapeDtypeStruct((tp, dh), F32), jax.ShapeDtypeStruct((tp, dh), BF16),
                   jax.ShapeDtypeStruct((tp, dh), F32), jax.ShapeDtypeStruct((1, dh), F32),
                   jax.ShapeDtypeStruct((1, dh), F32)],
        compiler_params=_cparams("arbitrary"))(dm, co, y, z, gc, gs)


def _shift_down(x, k):
    row = lax.broadcasted_iota(jnp.int32, x.shape, 0)
    return jnp.where(row >= k, pltpu.roll(x, k, 0), 0.0)


def _shift_up(x, k):
    n = x.shape[0]
    row = lax.broadcasted_iota(jnp.int32, x.shape, 0)
    return jnp.where(row < n - k, pltpu.roll(x, n - k, 0), 0.0)


def _dwconv(x, w_ref):
    return w_ref[2:3, :] * x + w_ref[1:2, :] * _shift_down(x, 1) + w_ref[0:1, :] * _shift_down(x, 2)


def _dwconv_bwd(x, dy, w_ref):
    dx = w_ref[2:3, :] * dy + w_ref[1:2, :] * _shift_up(dy, 1) + w_ref[0:1, :] * _shift_up(dy, 2)
    dw = jnp.concatenate([jnp.sum(dy * _shift_down(x, 2), axis=0, keepdims=True),
                          jnp.sum(dy * _shift_down(x, 1), axis=0, keepdims=True),
                          jnp.sum(dy * x, axis=0, keepdims=True)], axis=0)
    return dx, dw


def _scan(s_re, s_im, tab_ref, reverse):
    n_chunks = s_re.shape[0] // SUBLANES
    n_strips = s_re.shape[1] // LANES
    last = 0 if reverse else SUBLANES - 1

    def body(i, carry):
        chunk = (n_chunks - 1 - i) if reverse else i
        r0 = pl.multiple_of(chunk * SUBLANES, SUBLANES)
        out = []
        for st in range(n_strips):
            lanes = slice(st * LANES, (st + 1) * LANES)
            cr, ci = carry[2 * st], carry[2 * st + 1]
            xr = s_re[pl.ds(r0, SUBLANES), lanes]
            xi = s_im[pl.ds(r0, SUBLANES), lanes]
            for level, k in enumerate((1, 2, 4)):
                mr = tab_ref[2 * level, :, lanes]
                mi = tab_ref[2 * level + 1, :, lanes]
                sh = SUBLANES - k if reverse else k
                rr = pltpu.roll(xr, sh, 0)
                ri = pltpu.roll(xi, sh, 0)
                xr, xi = xr + (mr * rr - mi * ri), xi + (mr * ri + mi * rr)
            pwr = tab_ref[6, :, lanes]
            pwi = tab_ref[7, :, lanes]
            xr, xi = xr + (pwr * cr - pwi * ci), xi + (pwr * ci + pwi * cr)
            s_re[pl.ds(r0, SUBLANES), lanes] = xr
            s_im[pl.ds(r0, SUBLANES), lanes] = xi
            out.append(jnp.broadcast_to(xr[last:last + 1, :], (SUBLANES, LANES)))
            out.append(jnp.broadcast_to(xi[last:last + 1, :], (SUBLANES, LANES)))
        return tuple(out)

    zero = jnp.zeros((SUBLANES, LANES), F32)
    lax.fori_loop(0, n_chunks, body, (zero,) * (2 * n_strips))


def _seq_fwd(proj, conv_w, bc_re, bc_im, cc_re, cc_im, dskip, tab_f, name):
    tp = proj.shape[0]
    dh = proj.shape[1] // 4
    nq = dh // LANES
    sw = STATE * N_GROUPS // nq

    def body(b_ref, c_ref, v_ref, u_ref, w_ref, bre_ref, bim_ref, cre_ref, cim_ref, d_ref, tab_ref,
             co_ref, y_ref, g_ref, s_re, s_im):
        co_ref[...] = b_ref[...] * _dwconv(c_ref[...] * v_ref[...], w_ref)
        u = u_ref[...]
        ub = u.astype(BF16)
        s_re[...] = jnp.dot(ub, bre_ref[...], preferred_element_type=F32)
        s_im[...] = jnp.dot(ub, bim_ref[...], preferred_element_type=F32)
        _scan(s_re, s_im, tab_ref, False)
        y = (jnp.dot(s_re[...].astype(BF16), cre_ref[...], preferred_element_type=F32)
             - jnp.dot(s_im[...].astype(BF16), cim_ref[...], preferred_element_type=F32)
             + d_ref[...] * u)
        y_ref[...] = y
        g_ref[...] = _gelu(y).astype(BF16)

    col = lambda off: pl.BlockSpec((tp, LANES), lambda q, off=off: (0, off * nq + q))
    blk = pl.BlockSpec((tp, LANES), lambda q: (0, q))
    return pl.pallas_call(
        body, name=name, grid=(nq,),
        in_specs=[col(0), col(1), col(2), col(3),
                  pl.BlockSpec((3, LANES), lambda q: (0, q)),
                  pl.BlockSpec((LANES, sw), lambda q: (0, q)), pl.BlockSpec((LANES, sw), lambda q: (0, q)),
                  pl.BlockSpec((sw, LANES), lambda q: (q, 0)), pl.BlockSpec((sw, LANES), lambda q: (q, 0)),
                  pl.BlockSpec((1, LANES), lambda q: (0, q)),
                  pl.BlockSpec((8, SUBLANES, sw), lambda q: (0, 0, q))],
        out_specs=[blk, blk, blk],
        out_shape=[jax.ShapeDtypeStruct((tp, dh), F32), jax.ShapeDtypeStruct((tp, dh), F32),
                   jax.ShapeDtypeStruct((tp, dh), BF16)],
        scratch_shapes=[pltpu.VMEM((tp, sw), F32), pltpu.VMEM((tp, sw), F32)],
        compiler_params=_cparams("parallel"),
    )(proj, proj, proj, proj, conv_w, bc_re, bc_im, cc_re, cc_im, dskip, tab_f)


def _conv_bwd(proj, dco, conv_w, name):
    tp = proj.shape[0]
    dh = proj.shape[1] // 4
    nq = dh // LANES

    def body(b_ref, c_ref, v_ref, dco_ref, w_ref, dproj_ref, dw_ref, stage, sem):
        q = pl.program_id(0)
        cg = c_ref[...]
        vg = v_ref[...]
        cv = cg * vg
        dco_v = dco_ref[...]
        dcv, dw = _dwconv_bwd(cv, dco_v * b_ref[...], w_ref)
        dw_ref[...] = dw
        stage[0] = (dco_v * _dwconv(cv, w_ref)).astype(BF16)
        stage[1] = (dcv * vg).astype(BF16)
        stage[2] = (dcv * cg).astype(BF16)
        copies = [pltpu.make_async_copy(stage.at[p], dproj_ref.at[:, pl.ds((p * nq + q) * LANES, LANES)], sem.at[p])
                  for p in range(3)]
        for cp in copies:
            cp.start()
        for cp in copies:
            cp.wait()

    col = lambda off: pl.BlockSpec((tp, LANES), lambda q, off=off: (0, off * nq + q))
    return pl.pallas_call(
        body, name=name, grid=(nq,),
        in_specs=[col(0), col(1), col(2), pl.BlockSpec((tp, LANES), lambda q: (0, q)),
                  pl.BlockSpec((3, LANES), lambda q: (0, q))],
        out_specs=[pl.BlockSpec(memory_space=pl.ANY), pl.BlockSpec((3, LANES), lambda q: (0, q))],
        out_shape=[jax.ShapeDtypeStruct((tp, 4 * dh), BF16), jax.ShapeDtypeStruct((3, dh), F32)],
        scratch_shapes=[pltpu.VMEM((3, tp, LANES), BF16), pltpu.SemaphoreType.DMA((3,))],
        compiler_params=_cparams("arbitrary"),
    )(proj, proj, proj, dco, conv_w)


def _ssm_bwd(proj, y, dg, dproj, bc_re, bc_im, cc_re, cc_im, dskip, tab_f, tab_r, name):
    tp = proj.shape[0]
    dh = proj.shape[1] // 4
    nq = dh // LANES
    sw = STATE * N_GROUPS // nq

    def body(u_ref, y_ref, dg_ref, dproj_in, bre_ref, bim_ref, cre_ref, cim_ref, d_ref, tabf_ref, tabr_ref,
             dproj_ref, dbre_ref, dbim_ref, dcre_ref, dcim_ref, dd_ref, dar_ref, dai_ref,
             s_re, s_im, l_re, l_im, stage, sem):
        del dproj_in
        q = pl.program_id(0)
        nt = (((1,), (1,)), ((), ()))
        tn = (((0,), (0,)), ((), ()))
        u = u_ref[...]
        ub = u.astype(BF16)
        s_re[...] = jnp.dot(ub, bre_ref[...], preferred_element_type=F32)
        s_im[...] = jnp.dot(ub, bim_ref[...], preferred_element_type=F32)
        _scan(s_re, s_im, tabf_ref, False)
        dy = dg_ref[...] * _gelu_grad(y_ref[...])
        dyb = dy.astype(BF16)
        dd_ref[...] = jnp.sum(dy * u, axis=0, keepdims=True)
        l_re[...] = lax.dot_general(dyb, cre_ref[...], nt, preferred_element_type=F32)
        l_im[...] = -lax.dot_general(dyb, cim_ref[...], nt, preferred_element_type=F32)
        dcre_ref[...] = lax.dot_general(s_re[...].astype(BF16), dyb, tn, preferred_element_type=F32)
        dcim_ref[...] = -lax.dot_general(s_im[...].astype(BF16), dyb, tn, preferred_element_type=F32)
        _scan(l_re, l_im, tabr_ref, True)
        for st in range(sw // LANES):
            lanes = slice(st * LANES, (st + 1) * LANES)
            lr = l_re[:, lanes]
            li = l_im[:, lanes]
            pr = _shift_down(s_re[:, lanes], 1)
            pi = _shift_down(s_im[:, lanes], 1)
            dar_ref[:, lanes] = jnp.sum(lr * pr + li * pi, axis=0, keepdims=True)
            dai_ref[:, lanes] = jnp.sum(li * pr - lr * pi, axis=0, keepdims=True)
        lrb = l_re[...].astype(BF16)
        lib = l_im[...].astype(BF16)
        du = (dy * d_ref[...] + lax.dot_general(lrb, bre_ref[...], nt, preferred_element_type=F32)
              + lax.dot_general(lib, bim_ref[...], nt, preferred_element_type=F32))
        stage[...] = du.astype(BF16)
        dbre_ref[...] = lax.dot_general(ub, lrb, tn, preferred_element_type=F32)
        dbim_ref[...] = lax.dot_general(ub, lib, tn, preferred_element_type=F32)
        cp = pltpu.make_async_copy(stage, dproj_ref.at[:, pl.ds((3 * nq + q) * LANES, LANES)], sem)
        cp.start()
        cp.wait()

    blk = pl.BlockSpec((tp, LANES), lambda q: (0, q))
    bspec = pl.BlockSpec((LANES, sw), lambda q: (0, q))
    cspec = pl.BlockSpec((sw, LANES), lambda q: (q, 0))
    tspec = pl.BlockSpec((8, SUBLANES, sw), lambda q: (0, 0, q))
    nstate = STATE * N_GROUPS
    return pl.pallas_call(
        body, name=name, grid=(nq,),
        in_specs=[pl.BlockSpec((tp, LANES), lambda q: (0, 3 * nq + q)), blk, blk, pl.BlockSpec(memory_space=pl.ANY),
                  bspec, bspec, cspec, cspec, pl.BlockSpec((1, LANES), lambda q: (0, q)), tspec, tspec],
        out_specs=[pl.BlockSpec(memory_space=pl.ANY), bspec, bspec, cspec, cspec,
                   pl.BlockSpec((1, LANES), lambda q: (0, q)),
                   pl.BlockSpec((1, sw), lambda q: (0, q)), pl.BlockSpec((1, sw), lambda q: (0, q))],
        out_shape=[jax.ShapeDtypeStruct((tp, 4 * dh), BF16),
                   jax.ShapeDtypeStruct((LANES, nstate), F32), jax.ShapeDtypeStruct((LANES, nstate), F32),
                   jax.ShapeDtypeStruct((nstate, LANES), F32), jax.ShapeDtypeStruct((nstate, LANES), F32),
                   jax.ShapeDtypeStruct((1, dh), F32),
                   jax.ShapeDtypeStruct((1, nstate), F32), jax.ShapeDtypeStruct((1, nstate), F32)],
        input_output_aliases={3: 0},
        scratch_shapes=[pltpu.VMEM((tp, sw), F32)] * 4 + [pltpu.VMEM((tp, LANES), BF16), pltpu.SemaphoreType.DMA],
        compiler_params=_cparams("arbitrary"),
    )(proj, y, dg, dproj, bc_re, bc_im, cc_re, cc_im, dskip, tab_f, tab_r)


FFN_TILE = 256


def _ffn_act(up, fw, fb, name):
    tp, two_ff = up.shape
    dff = two_ff // 2
    tc = FFN_TILE
    nj = dff // tc

    def body(ua_ref, uv_ref, wa_ref, wv_ref, ba_ref, bv_ref, act_ref):
        a = _dwconv(ua_ref[...], wa_ref) + ba_ref[...]
        v = _dwconv(uv_ref[...], wv_ref) + bv_ref[...]
        act_ref[...] = (a * _sigmoid(a) * v).astype(BF16)

    lo = lambda r: pl.BlockSpec((r, tc), lambda j: (0, j))
    hi = lambda r: pl.BlockSpec((r, tc), lambda j: (0, nj + j))
    return pl.pallas_call(
        body, name=name, grid=(nj,),
        in_specs=[lo(tp), hi(tp), lo(3), hi(3), lo(1), hi(1)],
        out_specs=lo(tp),
        out_shape=jax.ShapeDtypeStruct((tp, dff), BF16),
        compiler_params=_cparams("parallel"))(up, up, fw, fw, fb, fb)


def _ffn_bwd(up, dact, fw, fb, name):
    tp, two_ff = up.shape
    dff = two_ff // 2
    tc = FFN_TILE
    nj = dff // tc

    def body(ua_ref, uv_ref, da_ref, wa_ref, wv_ref, ba_ref, bv_ref,
             dup_ref, dwa_ref, dwv_ref, dba_ref, dbv_ref, stage, sem):
        j = pl.program_id(0)
        ua = ua_ref[...]
        uv = uv_ref[...]
        a = _dwconv(ua, wa_ref) + ba_ref[...]
        v = _dwconv(uv, wv_ref) + bv_ref[...]
        sg = _sigmoid(a)
        dact_v = da_ref[...]
        da = dact_v * v * sg * (1.0 + a * (1.0 - sg))
        dv = dact_v * a * sg
        dba_ref[...] = jnp.sum(da, axis=0, keepdims=True)
        dbv_ref[...] = jnp.sum(dv, axis=0, keepdims=True)
        dua, dwa = _dwconv_bwd(ua, da, wa_ref)
        duv, dwv = _dwconv_bwd(uv, dv, wv_ref)
        dwa_ref[...] = dwa
        dwv_ref[...] = dwv
        stage[0] = dua.astype(BF16)
        stage[1] = duv.astype(BF16)
        copies = [pltpu.make_async_copy(stage.at[p], dup_ref.at[:, pl.ds((p * nj + j) * tc, tc)], sem.at[p])
                  for p in range(2)]
        for cp in copies:
            cp.start()
        for cp in copies:
            cp.wait()

    lo = lambda r: pl.BlockSpec((r, tc), lambda j: (0, j))
    hi = lambda r: pl.BlockSpec((r, tc), lambda j: (0, nj + j))
    return pl.pallas_call(
        body, name=name, grid=(nj,),
        in_specs=[lo(tp), hi(tp), lo(tp), lo(3), hi(3), lo(1), hi(1)],
        out_specs=[pl.BlockSpec(memory_space=pl.ANY), lo(3), lo(3), lo(1), lo(1)],
        out_shape=[jax.ShapeDtypeStruct((tp, two_ff), BF16),
                   jax.ShapeDtypeStruct((3, dff), F32), jax.ShapeDtypeStruct((3, dff), F32),
                   jax.ShapeDtypeStruct((1, dff), F32), jax.ShapeDtypeStruct((1, dff), F32)],
        scratch_shapes=[pltpu.VMEM((2, tp, tc), BF16), pltpu.SemaphoreType.DMA((2,))],
        compiler_params=_cparams("arbitrary"))(up, up, dact, fw, fw, fb, fb)


def _zoh(lr, li, ld):
    dt = jnp.exp(ld)
    mag = jnp.exp(lr * dt)
    ang = li * dt
    ar = mag * jnp.cos(ang)
    ai = mag * jnp.sin(ang)
    den = lr * lr + li * li
    nr = ar - 1.0
    fr = (nr * lr + ai * li) / den
    fi = (ai * lr - nr * li) / den
    return dt, ar, ai, den, nr, fr, fi


def _s5_prep(lr, li, ld, b_re, b_im, name):
    nstate = lr.shape[1]

    def tables(tab_ref, ar, ai, reverse):
        pows = [(ar, ai)]
        for _ in range(SUBLANES - 1):
            pr, pi = pows[-1]
            pows.append((pr * ar - pi * ai, pr * ai + pi * ar))
        row = lax.broadcasted_iota(jnp.int32, (SUBLANES, nstate), 0)
        for level, k in enumerate((1, 2, 4)):
            mask = (row <= SUBLANES - 1 - k) if reverse else (row >= k)
            tab_ref[2 * level] = jnp.where(mask, pows[k - 1][0], 0.0)
            tab_ref[2 * level + 1] = jnp.where(mask, pows[k - 1][1], 0.0)
        pr = jnp.zeros((SUBLANES, nstate), F32)
        pi = jnp.zeros((SUBLANES, nstate), F32)
        for t in range(SUBLANES):
            k = SUBLANES - 1 - t if reverse else t
            pr = jnp.where(row == t, pows[k][0], pr)
            pi = jnp.where(row == t, pows[k][1], pi)
        tab_ref[6] = pr
        tab_ref[7] = pi

    def body(lr_ref, li_ref, ld_ref, bre_ref, bim_ref, tabf_ref, tabr_ref, bcre_ref, bcim_ref):
        _, ar, ai, _, _, fr, fi = _zoh(lr_ref[...], li_ref[...], ld_ref[...])
        tables(tabf_ref, ar, ai, False)
        tables(tabr_ref, ar, -ai, True)
        bre = bre_ref[...]
        bim = bim_ref[...]
        bcre_ref[...] = (fr * bre - fi * bim).astype(BF16)
        bcim_ref[...] = (fr * bim + fi * bre).astype(BF16)

    vmem = pl.BlockSpec(memory_space=pltpu.VMEM)
    return pl.pallas_call(
        body, name=name, in_specs=[vmem] * 5, out_specs=[vmem] * 4,
        out_shape=[jax.ShapeDtypeStruct((8, SUBLANES, nstate), F32)] * 2
        + [jax.ShapeDtypeStruct(b_re.shape, BF16)] * 2)(lr, li, ld, b_re, b_im)


def _s5_prep_bwd(lr, li, ld, b_re, b_im, da_re, da_im, dbc_re, dbc_im, name):
    def body(lr_ref, li_ref, ld_ref, bre_ref, bim_ref, dar_ref, dai_ref, dbcre_ref, dbcim_ref,
             dlr_ref, dli_ref, dld_ref, dbre_ref, dbim_ref):
        lr, li = lr_ref[...], li_ref[...]
        dt, ar, ai, den, nr, fr, fi = _zoh(lr, li, ld_ref[...])
        bre, bim = bre_ref[...], bim_ref[...]
        gre, gim = dbcre_ref[...], dbcim_ref[...]
        dbre_ref[...] = fr * gre + fi * gim
        dbim_ref[...] = fr * gim - fi * gre
        g_fr = jnp.sum(gre * bre + gim * bim, axis=0, keepdims=True)
        g_fi = jnp.sum(gim * bre - gre * bim, axis=0, keepdims=True)
        g_ar = dar_ref[...] + (g_fr * lr - g_fi * li) / den
        g_ai = dai_ref[...] + (g_fr * li + g_fi * lr) / den
        d_lr = (g_fr * (nr - 2.0 * fr * lr) + g_fi * (ai - 2.0 * fi * lr)) / den
        d_li = (g_fr * (ai - 2.0 * fr * li) - g_fi * (nr + 2.0 * fi * li)) / den
        g_logmag = g_ar * ar + g_ai * ai
        g_ang = g_ai * ar - g_ar * ai
        dlr_ref[...] = d_lr + g_logmag * dt
        dli_ref[...] = d_li + g_ang * dt
        d_ld = (g_logmag * lr + g_ang * li) * dt
        n = d_ld.shape[1]
        sh = 1
        while sh < STATE:
            d_ld = d_ld + pltpu.roll(d_ld, n - sh, 1)
            sh *= 2
        dld_ref[...] = d_ld

    vmem = pl.BlockSpec(memory_space=pltpu.VMEM)
    row = jax.ShapeDtypeStruct(lr.shape, F32)
    return pl.pallas_call(
        body, name=name, in_specs=[vmem] * 9, out_specs=[vmem] * 5,
        out_shape=[row, row, row, jax.ShapeDtypeStruct(b_re.shape, F32), jax.ShapeDtypeStruct(b_re.shape, F32)],
    )(lr, li, ld, b_re, b_im, da_re, da_im, dbc_re, dbc_im)


def _compact_b(bb):
    bq = bb.reshape(N_GROUPS // 8, 8, STATE, GROUP)
    m = jnp.einsum("ab,qbph->qahbp", jnp.eye(8, dtype=bb.dtype), bq).reshape(N_GROUPS // 8, LANES, 8 * STATE)
    return m.transpose(1, 0, 2).reshape(LANES, N_GROUPS * STATE)


def _expand_b(m):
    d = m.reshape(8, GROUP, N_GROUPS // 8, 8, STATE)
    return jnp.einsum("ahqap->qaph", d).reshape(N_GROUPS, STATE, GROUP)


def _compact_c(c):
    cq = c.reshape(N_GROUPS // 8, 8, GROUP, STATE)
    return jnp.einsum("ab,qbhp->qbpah", jnp.eye(8, dtype=c.dtype), cq).reshape(N_GROUPS * STATE, LANES)


def _expand_c(m):
    d = m.reshape(N_GROUPS // 8, 8, STATE, 8, GROUP)
    return jnp.einsum("qbpbh->qbhp", d).reshape(N_GROUPS, GROUP, STATE)


def _local_step(x, target, p, ex):
    seq, d = x.shape
    n_real = N_META + seq
    tp = -(-n_real // ROW_ALIGN) * ROW_ALIGN
    pad = jnp.zeros((tp - n_real, d), F32)
    h0 = jnp.concatenate([p["meta_tokens"], x, pad], axis=0)
    tgt = jnp.concatenate([jnp.zeros((N_META, d), F32), target, pad], axis=0)

    nstate = N_GROUPS * STATE
    s5 = (p["ssm_lam_re"].reshape(1, nstate), p["ssm_lam_im"].reshape(1, nstate),
          jnp.repeat(p["ssm_log_dt"].reshape(-1), STATE).reshape(1, nstate),
          _compact_b(p["ssm_b_re"]), _compact_b(p["ssm_b_im"]))
    tab_f, tab_r, bc_re, bc_im = _s5_prep(*s5, "s5_prep")
    cc_re = _compact_c(p["ssm_c_re"]).astype(BF16)
    cc_im = _compact_c(p["ssm_c_im"]).astype(BF16)
    dskip = p["ssm_d"].reshape(1, -1)
    dh = dskip.shape[1]

    hn1 = _norm_fwd(h0, p["norm_mix_g"] + ex.zero, "norm_mix")
    proj = _mm(hn1, p["w_in"], "nn", "proj")
    co, y, g = _seq_fwd(proj, p["conv_w"], bc_re, bc_im, cc_re, cc_im, dskip, tab_f, "seq_fwd")
    mid = ex.weights("mid", g)
    z = _mm(g, mid["ssm_w_glu"], "nn", "glu")
    mixed = _mix_fwd(co, y, z, p["gain_conv_out"], p["gain_ssm_out"], "mix_fwd")
    mo = _mm(mixed, mid["w_out"], "nn", "out_proj")
    h1, hn2 = _norm_fwd(h0, p["norm_ffn_g"], "norm_ffn", res=mo)
    late = ex.weights("late", hn2)
    up = _mm(hn2, late["w_up"], "nn", "up_proj")
    act = _ffn_act(up, p["ffn_conv_w"], p["ffn_conv_b"], "ffn_act")
    dn = _mm(act, late["w_down"], "nn", "down_proj")
    loss, dh2, dh2b, d_gfin = _loss_bwd(h1, dn, tgt, p["norm_final_g"], n_real, "loss_bwd")

    g_w_down = _mm(act, dh2b, "tn", "g_w_down")
    dact = _mm(dh2b, late["w_down"], "nt", "d_act")
    dup, dfw_a, dfw_v, dfb_a, dfb_v = _ffn_bwd(up, dact, p["ffn_conv_w"], p["ffn_conv_b"], "ffn_bwd")
    g_w_up = _mm(hn2, dup, "tn", "g_w_up")
    zero = ex.grads_ready("late", {"w_up": g_w_up, "w_down": g_w_down})
    dhn2 = _mm(dup, late["w_up"], "nt", "d_hn2")
    zero = zero + ex.grads_send("late", dhn2)
    dh1, dh1b, d_gffn = _norm_bwd(h1, p["norm_ffn_g"] + zero, dhn2, dh2, "norm_ffn_bwd")
    g_w_out = _mm(mixed, dh1b, "tn", "g_w_out")
    dmixed = _mm(dh1b, mid["w_out"], "nt", "d_mixed")
    dco, dz, dgp, d_gc, d_gs = _mix_bwd(dmixed, co, y, z, p["gain_conv_out"], p["gain_ssm_out"], "mix_bwd")
    g_w_glu = _mm(g, dz, "tn", "g_w_glu")
    zero = ex.grads_ready("mid", {"ssm_w_glu": g_w_glu, "w_out": g_w_out})
    dg = _mm(dz, mid["ssm_w_glu"], "nt", "d_gelu", acc_in=dgp)
    zero = zero + ex.grads_send("mid", dg)
    dproj, d_conv_w = _conv_bwd(proj, dco, p["conv_w"] + zero, "conv_bwd")
    (dproj, dbc_re, dbc_im, dcc_re, dcc_im, d_dskip, da_re, da_im) = _ssm_bwd(
        proj, y, dg, dproj, bc_re, bc_im, cc_re, cc_im, dskip, tab_f, tab_r, "ssm_bwd")
    g_w_in = _mm(hn1, dproj, "tn", "g_w_in")
    dhn1 = _mm(dproj, p["w_in"], "nt", "d_hn1")
    grad_x, d_meta, d_gmix = _input_norm_bwd(h0, p["norm_mix_g"], dhn1, dh1, n_real, "norm_mix_bwd")

    d_lam_re, d_lam_im, d_log_dt, d_b_re, d_b_im = _s5_prep_bwd(*s5, da_re, da_im, dbc_re, dbc_im, "s5_prep_bwd")
    d_lam_re, d_lam_im = d_lam_re.reshape(N_GROUPS, STATE), d_lam_im.reshape(N_GROUPS, STATE)
    d_log_dt = d_log_dt[0, ::STATE]
    d_b_re, d_b_im = _expand_b(d_b_re), _expand_b(d_b_im)
    grads = {
        "meta_tokens": d_meta, "norm_mix_g": d_gmix, "w_in": g_w_in, "conv_w": d_conv_w,
        "ssm_lam_re": d_lam_re, "ssm_lam_im": d_lam_im, "ssm_log_dt": d_log_dt,
        "ssm_b_re": d_b_re, "ssm_b_im": d_b_im, "ssm_c_re": _expand_c(dcc_re), "ssm_c_im": _expand_c(dcc_im),
        "ssm_d": d_dskip.reshape(N_GROUPS, GROUP), "ssm_w_glu": g_w_glu,
        "gain_conv_out": d_gc, "gain_ssm_out": d_gs, "w_out": g_w_out, "norm_ffn_g": d_gffn,
        "w_up": g_w_up, "ffn_conv_w": jnp.concatenate([dfw_a, dfw_v], axis=1),
        "ffn_conv_b": jnp.concatenate([dfb_a, dfb_v], axis=1), "w_down": g_w_down, "norm_final_g": d_gfin,
    }
    return loss[0, 0], grad_x, grads


def _view(ref, axis, start, size):
    idx = [slice(None)] * len(ref.shape)
    idx[axis] = pl.ds(start, size)
    return ref.at[tuple(idx)]


def _exchange(name, ins, outs, aliases, local_copies, remote_copies):
    ni, no = len(ins), len(outs)
    nl, nr = len(local_copies), len(remote_copies)

    def body(*refs):
        in_refs, out_refs = refs[:ni], refs[ni:ni + no]
        send_sems, recv_sems, local_sems = refs[ni + no:]
        x, y, c = lax.axis_index("x"), lax.axis_index("y"), lax.axis_index("c")
        pos = (x, y, c, 2 * x + y)
        locals_ = [pltpu.make_async_copy(s(in_refs, out_refs, pos), d(in_refs, out_refs, pos), local_sems.at[i])
                   for i, (s, d) in enumerate(local_copies)]
        remotes = []
        for i, (s, d, flip) in enumerate(remote_copies):
            peer = (1 - x if "x" in flip else x, 1 - y if "y" in flip else y, 1 - c if "c" in flip else c)
            remotes.append(pltpu.make_async_remote_copy(
                src_ref=s(in_refs, out_refs, pos), dst_ref=d(in_refs, out_refs, pos),
                send_sem=send_sems.at[i], recv_sem=recv_sems.at[i], device_id=peer, device_id_type=MESH))
        for cp in locals_ + remotes:
            cp.start()
        for cp in remotes:
            cp.wait_recv()
        for cp in remotes:
            cp.wait_send()
        for cp in locals_:
            cp.wait()

    hbm = pl.BlockSpec(memory_space=pl.ANY)
    return pl.pallas_call(
        body, name=name, in_specs=[hbm] * ni, out_specs=[hbm] * no, out_shape=outs,
        input_output_aliases=aliases,
        scratch_shapes=[pltpu.SemaphoreType.DMA((nr,)), pltpu.SemaphoreType.DMA((nr,)),
                        pltpu.SemaphoreType.DMA((max(nl, 1),))],
    )(*ins)


BIG = {"w_in": (0, 1), "ssm_w_glu": (1, 0), "w_out": (1, 0), "w_up": (0, 1), "w_down": (1, 0)}
BIG_NAMES = tuple(BIG)
FLIPS = ("y", "x", "xy")


def _peer_chip(pos, flip):
    x, y, _, _ = pos
    return 2 * (1 - x if "x" in flip else x) + (1 - y if "y" in flip else y)


def _block_rows(rows, cols, itemsize, mult):
    return _pick_tile(rows, max(mult, (2 * 1024 * 1024) // (cols * itemsize)), mult)


def _cast_into_full(w, kc, shard_axis, name):
    r, cdim = w.shape
    tr = _block_rows(r, cdim, 4, 16)
    nb = r // tr

    def body(kc_ref, w_ref, o_ref):
        o_ref[...] = w_ref[...].astype(BF16)

    if shard_axis == 1:
        full, o_spec = (r, 4 * cdim), pl.BlockSpec((tr, cdim), lambda i, kc: (i, kc[0]))
    else:
        full, o_spec = (4 * r, cdim), pl.BlockSpec((tr, cdim), lambda i, kc: (kc[0] * nb + i, 0))
    return pl.pallas_call(
        body, name=name,
        grid_spec=pltpu.PrefetchScalarGridSpec(
            num_scalar_prefetch=1, grid=(nb,), in_specs=[pl.BlockSpec((tr, cdim), lambda i, kc: (i, 0))],
            out_specs=o_spec),
        out_shape=jax.ShapeDtypeStruct(full, BF16), compiler_params=_cparams("parallel"))(kc, w)


def _pair_sum(g, recv, kc, half_axis, name, out_dtype):
    hr, hc = recv.shape
    tr = _block_rows(hr, hc, 4, 16)
    nb = hr // tr

    def body(kc_ref, g_ref, r_ref, o_ref):
        o_ref[...] = (g_ref[...] + r_ref[...]).astype(out_dtype)

    if half_axis == 0:
        g_spec = pl.BlockSpec((tr, hc), lambda i, kc: (kc[1] * nb + i, 0))
    elif half_axis == 1:
        g_spec = pl.BlockSpec((tr, hc), lambda i, kc: (i, kc[1]))
    else:
        g_spec = pl.BlockSpec((tr, hc), lambda i, kc: (i, 0))
    same = pl.BlockSpec((tr, hc), lambda i, kc: (i, 0))
    return pl.pallas_call(
        body, name=name,
        grid_spec=pltpu.PrefetchScalarGridSpec(num_scalar_prefetch=1, grid=(nb,), in_specs=[g_spec, same],
                                               out_specs=same),
        out_shape=jax.ShapeDtypeStruct((hr, hc), out_dtype), compiler_params=_cparams("parallel"))(kc, g, recv)


def _chip_sum(own, recv, kc, own_axis, out_axis, name):
    _, sr, sc = recv.shape
    tr = _block_rows(sr, sc, 4, 16)
    nb = sr // tr

    def body(kc_ref, o_ref, r_ref, t_ref):
        k = kc_ref[0]
        own_v = o_ref[...].astype(F32)
        r = [r_ref[m].astype(F32) for m in range(3)]
        terms = []
        for kk in range(4):
            m = jnp.bitwise_xor(k, kk)
            terms.append(jnp.where(m == 0, own_v, jnp.where(m == 1, r[0], jnp.where(m == 2, r[1], r[2]))))
        t_ref[...] = (terms[0] + terms[1]) + (terms[2] + terms[3])

    if own_axis == 0:
        own_spec = pl.BlockSpec((tr, sc), lambda i, kc: (kc[0] * nb + i, 0))
    elif own_axis == 1:
        own_spec = pl.BlockSpec((tr, sc), lambda i, kc: (i, kc[0]))
    else:
        own_spec = pl.BlockSpec((tr, sc), lambda i, kc: (kc[1] * nb + i, 0))
    if out_axis == 0:
        out_full, out_spec = (2 * sr, sc), pl.BlockSpec((tr, sc), lambda i, kc: (kc[1] * nb + i, 0))
    else:
        out_full, out_spec = (sr, 2 * sc), pl.BlockSpec((tr, sc), lambda i, kc: (i, kc[1]))
    return pl.pallas_call(
        body, name=name,
        grid_spec=pltpu.PrefetchScalarGridSpec(
            num_scalar_prefetch=1, grid=(nb,),
            in_specs=[own_spec, pl.BlockSpec((3, tr, sc), lambda i, kc: (0, i, 0))],
            out_specs=out_spec),
        out_shape=jax.ShapeDtypeStruct(out_full, F32), compiler_params=_cparams("parallel"))(kc, own, recv)


def _adamw(w, g, m, v, name):
    r, cdim = w.shape
    tr = _block_rows(r, cdim, 4, 8)
    c1 = 1.0 - ADAM_B1 ** ADAM_STEP
    c2 = 1.0 - ADAM_B2 ** ADAM_STEP

    def body(w_ref, g_ref, m_ref, v_ref, d_ref, nm_ref, nv_ref):
        gv = g_ref[...]
        nm = ADAM_B1 * m_ref[...] + (1.0 - ADAM_B1) * gv
        nv = ADAM_B2 * v_ref[...] + (1.0 - ADAM_B2) * (gv * gv)
        d_ref[...] = -ADAM_LR * ((nm / c1) / (jnp.sqrt(nv / c2) + ADAM_EPS) + ADAM_WD * w_ref[...])
        nm_ref[...] = nm
        nv_ref[...] = nv

    spec = _rows(cdim, tr)
    return pl.pallas_call(body, name=name, grid=(r // tr,), in_specs=[spec] * 4, out_specs=[spec] * 3,
                          out_shape=[jax.ShapeDtypeStruct((r, cdim), F32)] * 3,
                          compiler_params=_cparams("parallel"))(w, g, m, v)


def _adamw_whole(ws, gs, ms, vs, name):
    n = len(ws)
    c1 = 1.0 - ADAM_B1 ** ADAM_STEP
    c2 = 1.0 - ADAM_B2 ** ADAM_STEP

    def body(*refs):
        for i in range(n):
            w_ref, g_ref, m_ref, v_ref, d_ref, nm_ref, nv_ref = [refs[j * n + i] for j in range(7)]
            gv = g_ref[...]
            nm = ADAM_B1 * m_ref[...] + (1.0 - ADAM_B1) * gv
            nv = ADAM_B2 * v_ref[...] + (1.0 - ADAM_B2) * (gv * gv)
            d_ref[...] = -ADAM_LR * ((nm / c1) / (jnp.sqrt(nv / c2) + ADAM_EPS) + ADAM_WD * w_ref[...])
            nm_ref[...] = nm
            nv_ref[...] = nv

    vmem = pl.BlockSpec(memory_space=pltpu.VMEM)
    out = pl.pallas_call(body, name=name, in_specs=[vmem] * (4 * n), out_specs=[vmem] * (3 * n),
                         out_shape=[jax.ShapeDtypeStruct(a.shape, F32) for a in ws] * 3,
                         compiler_params=pltpu.CompilerParams(vmem_limit_bytes=VMEM_LIMIT))(*ws, *gs, *ms, *vs)
    return out[:n], out[n:2 * n], out[2 * n:]


SIDE_EFFECT = pltpu.SideEffectType.DATAFLOW_SIDE_EFFECTING


def _descriptors(copies, refs, send_sems, recv_sems):
    x, y, c = lax.axis_index("x"), lax.axis_index("y"), lax.axis_index("c")
    pos = (x, y, c, 2 * x + y)
    out = []
    for i, (s, d, flip) in enumerate(copies):
        peer = (1 - x if "x" in flip else x, 1 - y if "y" in flip else y, 1 - c if "c" in flip else c)
        out.append(pltpu.make_async_remote_copy(
            src_ref=s(refs, refs, pos), dst_ref=d(refs, refs, pos),
            send_sem=send_sems.at[i], recv_sem=recv_sems.at[i], device_id=peer, device_id_type=MESH))
    return out


def _exchange_start(name, bufs, copies, after):
    n, nr = len(bufs), len(copies)

    def body(*refs):
        for cp in _descriptors(copies, refs[:n], refs[n + 1], refs[n + 2]):
            cp.start()
        token = refs[2 * n + 3]
        token[...] = jnp.zeros_like(token)

    hbm = pl.BlockSpec(memory_space=pltpu.HBM)
    sem = pl.BlockSpec(memory_space=pltpu.SEMAPHORE)
    out = pl.pallas_call(
        body, name=name,
        in_specs=[hbm] * n + [pl.BlockSpec(memory_space=pl.ANY)],
        out_specs=(sem, sem, *[hbm] * n, pl.BlockSpec(memory_space=pltpu.VMEM)),
        out_shape=(pltpu.SemaphoreType.DMA((nr,)), pltpu.SemaphoreType.DMA((nr,)),
                   *[pltpu.HBM(b.shape, b.dtype) for b in bufs], jax.ShapeDtypeStruct((SUBLANES, LANES), F32)),
        input_output_aliases={i: 2 + i for i in range(n)},
        compiler_params=pltpu.CompilerParams(has_side_effects=SIDE_EFFECT),
    )(*[pltpu.with_memory_space_constraint(b, pltpu.HBM) for b in bufs], after)
    return out[0], out[1], list(out[2:2 + n]), out[2 + n]


def _exchange_wait(name, send_sems, recv_sems, bufs, copies, after):
    n = len(bufs)

    def body(*refs):
        for cp in _descriptors(copies, refs[:n], refs[n], refs[n + 1]):
            cp.wait_send()
            cp.wait_recv()

    hbm = pl.BlockSpec(memory_space=pltpu.HBM)
    sem = pl.BlockSpec(memory_space=pltpu.SEMAPHORE)
    out = pl.pallas_call(
        body, name=name,
        in_specs=[hbm] * n + [sem, sem, pl.BlockSpec(memory_space=pl.ANY)],
        out_specs=tuple([hbm] * n),
        out_shape=tuple(pltpu.HBM(b.shape, b.dtype) for b in bufs),
        input_output_aliases={i: i for i in range(n)},
        compiler_params=pltpu.CompilerParams(has_side_effects=SIDE_EFFECT),
    )(*bufs, send_sems, recv_sems, after)
    return list(out)


FIRST = ("w_in",)
MID = ("ssm_w_glu", "w_out")
LATE = ("w_up", "w_down")
GROUPS = {"mid": MID, "late": LATE}


def _gather_copies(names, shard_shapes):
    def region(i, chip, c):
        half_axis, shard_axis = BIG[names[i]]
        ssize = shard_shapes[i][shard_axis]
        hsize = shard_shapes[i][half_axis] // 2
        return lambda ref: _view(_view(ref, shard_axis, chip * ssize, ssize), half_axis, c * hsize, hsize)

    ici, d2d = [], []
    for i in range(len(names)):
        for flip in FLIPS:
            ici.append((lambda I, O, pos, i=i: region(i, pos[3], pos[2])(I[i]),
                        lambda I, O, pos, i=i: region(i, pos[3], pos[2])(O[i]), flip))
            d2d.append((lambda I, O, pos, i=i, flip=flip: region(i, _peer_chip(pos, flip), pos[2])(I[i]),
                        lambda I, O, pos, i=i, flip=flip: region(i, _peer_chip(pos, flip), pos[2])(O[i]), "c"))
    return ici, d2d


def _half_shape(n, shape):
    r, cdim = shape
    return (r // 2, cdim) if BIG[n][0] == 0 else (r, cdim // 2)


def _sub_shape(n, shape):
    hr, hc = _half_shape(n, shape)
    return (hr, hc // 4) if BIG[n][1] == 1 else (hr // 4, hc)


def _pair_copies(names, shapes, with_pack, dst_off):
    n = len(names)

    def other_half(i, ref, pos):
        half_axis = BIG[names[i]][0]
        hsize = shapes[i][half_axis] // 2
        return _view(ref, half_axis, (1 - pos[2]) * hsize, hsize)

    copies = [(lambda I, O, pos, i=i: other_half(i, I[i], pos), lambda I, O, pos, i=i: O[dst_off + i], "c")
              for i in range(n)]
    if with_pack:
        copies.append((lambda I, O, pos: I[n], lambda I, O, pos: O[dst_off + n], "c"))
    return copies


def _chip_copies(names, shapes, pack_rows, dst_off):
    n = len(names)

    def piece(i, ref, chip):
        shard_axis = BIG[names[i]][1]
        ssize = _sub_shape(names[i], shapes[i])[shard_axis]
        return _view(ref, shard_axis, chip * ssize, ssize)

    copies = []
    for i in range(n):
        for slot, flip in enumerate(FLIPS):
            copies.append((lambda I, O, pos, i=i, flip=flip: piece(i, I[i], _peer_chip(pos, flip)),
                           lambda I, O, pos, i=i, slot=slot: O[dst_off + i].at[slot], flip))
    if pack_rows:
        for slot, flip in enumerate(FLIPS):
            copies.append((lambda I, O, pos: _view(I[n], 0, pos[2] * (pack_rows // 2), pack_rows // 2),
                           lambda I, O, pos, slot=slot: O[dst_off + n].at[slot], flip))
    return copies


class _Exchanges:
    def __init__(self, shards, tiny, kc):
        self.kc = kc
        wb = {n: _cast_into_full(shards[n], kc, BIG[n][1], "cast_" + n) for n in BIG_NAMES}
        nf = len(FIRST)
        ici, d2d = _gather_copies(FIRST, [shards[n].shape for n in FIRST])
        local = [(lambda I, O, pos: I[nf], lambda I, O, pos: O[nf].at[pos[3]])]
        ici += [(lambda I, O, pos: I[nf], lambda I, O, pos: O[nf].at[pos[3]], flip) for flip in FLIPS]
        outs = ([jax.ShapeDtypeStruct(wb[n].shape, BF16) for n in FIRST]
                + [jax.ShapeDtypeStruct((4,) + tiny.shape, F32)])
        got = _exchange("gather_ici", [wb[n] for n in FIRST] + [tiny], outs, {i: i for i in range(nf)}, local, ici)
        full = _exchange("gather_d2d", list(got[:nf]), outs[:nf], {i: i for i in range(nf)}, [], d2d)
        self.first = dict(zip(FIRST, full))
        self.tiny_all = got[nf]
        self.gathering, self.pairing, self.reducing = {}, {}, {}
        after = full[0]
        self.zero = 0.0
        for group, names in GROUPS.items():
            copies = _gather_copies(names, [shards[n].shape for n in names])
            started = _exchange_start("gather_%s_start" % group, [wb[n] for n in names], copies[0], after)
            self.gathering[group] = (started, copies)
            after = started[2][0]
            self.zero = self.zero + started[3][0, 0]

    def weights(self, group, after):
        (send_sems, recv_sems, bufs, _), (ici, d2d) = self.gathering[group]
        got = _exchange_wait("gather_%s_wait" % group, send_sems, recv_sems, bufs, ici, after)
        outs = [jax.ShapeDtypeStruct(b.shape, BF16) for b in got]
        full = _exchange("gather_%s_d2d" % group, got, outs, {i: i for i in range(len(got))}, [], d2d)
        return dict(zip(GROUPS[group], full))

    def grads_ready(self, group, grads):
        names = GROUPS[group]
        gs = [grads[n] for n in names]
        land = [lax.empty(_half_shape(n, g.shape), F32) for n, g in zip(names, gs)]
        copies = _pair_copies(names, [g.shape for g in gs], False, len(names))
        started = _exchange_start("pair_%s_start" % group, gs + land, copies, gs[0])
        self.pairing[group] = (started, copies)
        return started[3][0, 0]

    def grads_send(self, group, after):
        names = GROUPS[group]
        n = len(names)
        (send_sems, recv_sems, bufs, _), copies = self.pairing[group]
        bufs = _exchange_wait("pair_%s_wait" % group, send_sems, recv_sems, bufs, copies, after)
        chip = [_pair_sum(bufs[i], bufs[n + i], self.kc, BIG[names[i]][0], "pair_sum_" + names[i], BF16)
                for i in range(n)]
        shapes = [bufs[i].shape for i in range(n)]
        land = [lax.empty((3,) + _sub_shape(names[i], shapes[i]), BF16) for i in range(n)]
        copies = _chip_copies(names, shapes, 0, n)
        started = _exchange_start("reduce_%s_start" % group, chip + land, copies, chip[0])
        self.reducing[group] = (started, copies)
        return started[3][0, 0]

    def finish(self, grads, pack, after):
        kc = self.kc
        names, chips, recvs = (), [], []
        for group, group_names in GROUPS.items():
            (send_sems, recv_sems, bufs, _), copies = self.reducing[group]
            bufs = _exchange_wait("reduce_%s_wait" % group, send_sems, recv_sems, bufs, copies, after)
            n = len(group_names)
            names, chips, recvs = names + group_names, chips + bufs[:n], recvs + bufs[n:]
            after = bufs[n]

        nf = len(FIRST)
        gs = [grads[n] for n in FIRST]
        shapes = [g.shape for g in gs]
        outs = ([jax.ShapeDtypeStruct(_half_shape(n, s), F32) for n, s in zip(FIRST, shapes)]
                + [jax.ShapeDtypeStruct(pack.shape, F32)])
        recv = _exchange("reduce_d2d", gs + [pack, after], outs, {}, [], _pair_copies(FIRST, shapes, True, 0))
        chip = [_pair_sum(gs[i], recv[i], kc, BIG[n][0], "pair_sum_" + n, BF16) for i, n in enumerate(FIRST)]
        chip_pack = _pair_sum(pack, recv[nf], kc, None, "pair_sum_pack", F32)
        outs = ([jax.ShapeDtypeStruct((3,) + _sub_shape(n, s), BF16) for n, s in zip(FIRST, shapes)]
                + [jax.ShapeDtypeStruct((3, pack.shape[0] // 2, pack.shape[1]), F32)])
        recv = _exchange("reduce_ici", chip + [chip_pack], outs, {}, [],
                         _chip_copies(FIRST, shapes, pack.shape[0], 0))

        names, chips, recvs = FIRST + names, chip + chips, list(recv[:nf]) + recvs
        total = [_chip_sum(chips[i], recvs[i], kc, BIG[n][1], BIG[n][0], "chip_sum_" + n)
                 for i, n in enumerate(names)]
        total.append(_chip_sum(chip_pack, recv[nf], kc, None, 0, "chip_sum_pack"))

        def my_half(i, ref, pos):
            half_axis = BIG[names[i]][0] if i < len(names) else 0
            hsize = ref.shape[half_axis] // 2
            return _view(ref, half_axis, pos[2] * hsize, hsize)

        remote = [(lambda I, O, pos, i=i: my_half(i, I[i], pos), lambda I, O, pos, i=i: my_half(i, O[i], pos), "c")
                  for i in range(len(total))]
        outs = [jax.ShapeDtypeStruct(t.shape, F32) for t in total]
        out = _exchange("swap_d2d", total, outs, {i: i for i in range(len(total))}, [], remote)
        return dict(zip(names, out[:len(names)])), out[len(names)]


WEIGHTS = ("meta_tokens", "norm_mix_g", "w_in", "conv_w", "ssm_lam_re", "ssm_lam_im", "ssm_log_dt", "ssm_b_re",
           "ssm_b_im", "ssm_c_re", "ssm_c_im", "ssm_d", "ssm_w_glu", "gain_conv_out", "gain_ssm_out", "w_out",
           "norm_ffn_g", "w_up", "ffn_conv_w", "ffn_conv_b", "w_down", "norm_final_g")
TINY_SHARDED = ("meta_tokens", "conv_w", "ffn_conv_w")
REPLICATED = tuple(n for n in WEIGHTS if n not in BIG and n not in TINY_SHARDED)
PACK_COLS = 512


def _pack(arrays, row_mult, cols):
    flat = jnp.concatenate([a.reshape(-1).astype(F32) for a in arrays])
    n = flat.shape[0]
    total = -(-n // (row_mult * cols)) * (row_mult * cols)
    return jnp.concatenate([flat, jnp.zeros((total - n,), F32)]).reshape(total // cols, cols)


def _unpack(packed, shapes):
    flat = packed.reshape(-1)
    out, off = [], 0
    for s in shapes:
        n = math.prod(s)
        out.append(flat[off:off + n].reshape(s))
        off += n
    return out


def kernel(x, meta_tokens, norm_mix_g, w_in, conv_w, ssm_lam_re, ssm_lam_im, ssm_log_dt, ssm_b_re, ssm_b_im, ssm_c_re, ssm_c_im, ssm_d, ssm_w_glu, gain_conv_out, gain_ssm_out, w_out, norm_ffn_g, w_up, ffn_conv_w, ffn_conv_b, w_down, norm_final_g, loss_target, m_meta_tokens, m_norm_mix_g, m_w_in, m_conv_w, m_ssm_lam_re, m_ssm_lam_im, m_ssm_log_dt, m_ssm_b_re, m_ssm_b_im, m_ssm_c_re, m_ssm_c_im, m_ssm_d, m_ssm_w_glu, m_gain_conv_out, m_gain_ssm_out, m_w_out, m_norm_ffn_g, m_w_up, m_ffn_conv_w, m_ffn_conv_b, m_w_down, m_norm_final_g, v_meta_tokens, v_norm_mix_g, v_w_in, v_conv_w, v_ssm_lam_re, v_ssm_lam_im, v_ssm_log_dt, v_ssm_b_re, v_ssm_b_im, v_ssm_c_re, v_ssm_c_im, v_ssm_d, v_ssm_w_glu, v_gain_conv_out, v_gain_ssm_out, v_w_out, v_norm_ffn_g, v_w_up, v_ffn_conv_w, v_ffn_conv_b, v_w_down, v_norm_final_g):
    args = dict(locals())
    w = {n: args[n] for n in WEIGHTS}
    mom = {n: args["m_" + n] for n in WEIGHTS}
    var = {n: args["v_" + n] for n in WEIGHTS}
    kx, ky, kc_ = lax.axis_index("x"), lax.axis_index("y"), lax.axis_index("c")
    chip = 2 * kx + ky
    kc = jnp.stack([chip, kc_]).astype(jnp.int32)

    def squeeze(n, a):
        if n == "meta_tokens":
            return a
        if n == "norm_final_g":
            return a.reshape(1, -1)
        a = a[0]
        return a.reshape(1, -1) if a.ndim == 1 else a

    wl = {n: squeeze(n, w[n]) for n in WEIGHTS}
    ml = {n: squeeze(n, mom[n]) for n in WEIGHTS}
    vl = {n: squeeze(n, var[n]) for n in WEIGHTS}

    tiny = _pack([wl[n] for n in TINY_SHARDED], SUBLANES, LANES)
    ex = _Exchanges({n: wl[n] for n in BIG_NAMES}, tiny, kc)
    tiny_shapes = [wl[n].shape for n in TINY_SHARDED]
    tiny_parts = [_unpack(ex.tiny_all[k], tiny_shapes) for k in range(4)]
    p = {n: wl[n] for n in WEIGHTS if n not in BIG}
    p.update(ex.first)
    for j, n in enumerate(TINY_SHARDED):
        p[n] = jnp.concatenate([tiny_parts[k][j] for k in range(4)], axis=1)
    p["ssm_log_dt"] = wl["ssm_log_dt"].reshape(-1)

    loss_local, grad_x, grads = _local_step(x[0], loss_target[0], p, ex)

    small_names = REPLICATED + TINY_SHARDED
    small_shapes = [tuple(grads[n].shape) for n in small_names] + [(1,)]
    pack = _pack([grads[n] for n in small_names] + [loss_local.reshape(1)], 2 * 16, PACK_COLS)
    g_big, g_pack = ex.finish({n: grads[n] for n in FIRST}, pack, pack)
    g_small = dict(zip(small_names + ("loss",), _unpack(g_pack, small_shapes)))
    loss = g_small["loss"][0]
    g = dict(g_big)
    for n in REPLICATED:
        g[n] = g_small[n].reshape(w[n].shape)
    for n in TINY_SHARDED:
        cols = wl[n].shape[1]
        g[n] = lax.dynamic_slice_in_dim(g_small[n], chip * cols, cols, axis=1).reshape(w[n].shape)

    delta, new_m, new_v = {}, {}, {}
    for n in BIG_NAMES:
        delta[n], new_m[n], new_v[n] = _adamw(wl[n], g[n], ml[n], vl[n], "adamw_" + n)
    rank2 = lambda a: a.reshape(1, -1) if a.ndim == 1 else a
    small = [[rank2(d[n]) for n in small_names] for d in (w, g, mom, var)]
    for d, outs in zip((delta, new_m, new_v), _adamw_whole(*small, "adamw_small")):
        d.update(zip(small_names, outs))

    def like(n, a):
        return a.reshape(w[n].shape)

    return (loss, grad_x[None], *[like(n, g[n]) for n in WEIGHTS], *[like(n, delta[n]) for n in WEIGHTS],
            *[like(n, new_m[n]) for n in WEIGHTS], *[like(n, new_v[n]) for n in WEIGHTS])
```

```python
import functools
import math

import jax
import jax.numpy as jnp
from jax import lax
from jax.experimental import pallas as pl
from jax.experimental.pallas import tpu as pltpu

F32 = jnp.float32
BF16 = jnp.bfloat16
MESH = pl.DeviceIdType.MESH

N_META = 16
N_GROUPS = 32
GROUP = 16
STATE = 64
RMS_EPS = 1e-6
ADAM_LR = 0.001
ADAM_B1 = 0.9
ADAM_B2 = 0.999
ADAM_EPS = 1e-08
ADAM_WD = 0.01
ADAM_STEP = 10

LANES = 128
SUBLANES = 8
ROW_ALIGN = 128
ROW_TILES = 4
VMEM_LIMIT = 52 * 1024 * 1024
GELU_C = math.sqrt(2.0 / math.pi)
GELU_A = 0.044715


def _cparams(*sem):
    return pltpu.CompilerParams(dimension_semantics=sem, vmem_limit_bytes=VMEM_LIMIT)


def _pick_tile(dim, cap, mult):
    best = None
    for t in range(mult, min(dim, cap) + 1, mult):
        if dim % t == 0:
            best = t
    return best if best is not None else dim


def _mm(a, b, mode, name, out_dtype=F32, acc_in=None, after=None):
    if mode == "tn":
        kdim, m = a.shape
    else:
        m, kdim = a.shape
    n = b.shape[0] if mode == "nt" else b.shape[1]
    tm = _pick_tile(m, 1408, LANES if mode == "tn" else 16)
    tn = _pick_tile(n, 512, LANES)
    tk = _pick_tile(kdim, 2816, LANES)
    nk = kdim // tk
    has_acc = acc_in is not None

    def body(*refs):
        if after is not None:
            refs = refs[1:]
        if has_acc:
            a_ref, b_ref, c_ref, o_ref = refs[:4]
            rest = refs[4:]
        else:
            a_ref, b_ref, o_ref = refs[:3]
            c_ref = None
            rest = refs[3:]
        if mode == "nn":
            p = jnp.dot(a_ref[...], b_ref[...], preferred_element_type=F32)
        elif mode == "nt":
            p = lax.dot_general(a_ref[...], b_ref[...], (((1,), (1,)), ((), ())), preferred_element_type=F32)
        else:
            p = lax.dot_general(a_ref[...], b_ref[...], (((0,), (0,)), ((), ())), preferred_element_type=F32)
        if nk == 1:
            if has_acc:
                p = p + c_ref[...]
            o_ref[...] = p.astype(out_dtype)
        else:
            acc_ref = rest[0]
            k = pl.program_id(2)

            @pl.when(k == 0)
            def _():
                acc_ref[...] = p + c_ref[...] if has_acc else p

            @pl.when(k > 0)
            def _():
                acc_ref[...] += p

            @pl.when(k == nk - 1)
            def _():
                o_ref[...] = acc_ref[...].astype(out_dtype)

    if mode == "tn":
        a_spec = pl.BlockSpec((tk, tm), lambda i, j, k: (k, i))
    else:
        a_spec = pl.BlockSpec((tm, tk), lambda i, j, k: (i, k))
    if mode == "nt":
        b_spec = pl.BlockSpec((tn, tk), lambda i, j, k: (j, k))
    else:
        b_spec = pl.BlockSpec((tk, tn), lambda i, j, k: (k, j))
    o_spec = pl.BlockSpec((tm, tn), lambda i, j, k: (i, j))
    in_specs = [a_spec, b_spec] + ([o_spec] if has_acc else [])
    args = (a, b) + ((acc_in,) if has_acc else ())
    if after is not None:
        in_specs = [pl.BlockSpec(memory_space=pl.ANY)] + in_specs
        args = (after,) + args
    return pl.pallas_call(
        body, name=name, grid=(m // tm, n // tn, nk),
        in_specs=in_specs, out_specs=o_spec,
        out_shape=jax.ShapeDtypeStruct((m, n), out_dtype),
        scratch_shapes=[pltpu.VMEM((tm, tn), F32)] if nk > 1 else [],
        compiler_params=_cparams("parallel", "parallel", "arbitrary"),
    )(*args)


def _rows(shape_cols, tr, dtype=None):
    return pl.BlockSpec((tr, shape_cols), lambda i: (i, 0))


def _const(shape):
    return pl.BlockSpec(shape, lambda i: (0,) * len(shape))


def _rms(x):
    return lax.rsqrt(jnp.mean(x * x, axis=-1, keepdims=True) + RMS_EPS)


def _rms_bwd(x, r, g, dy):
    xn = x * r
    dxn = dy * g
    dx = r * (dxn - xn * jnp.mean(dxn * xn, axis=-1, keepdims=True))
    return dx, dy * xn


def _gelu(y):
    return 0.5 * y * (1.0 + jnp.tanh(GELU_C * (y + GELU_A * y * y * y)))


def _gelu_grad(y):
    t = jnp.tanh(GELU_C * (y + GELU_A * y * y * y))
    return 0.5 * (1.0 + t) + 0.5 * y * (1.0 - t * t) * GELU_C * (1.0 + 3.0 * GELU_A * y * y)


def _sigmoid(z):
    return 1.0 / (1.0 + jnp.exp(-z))


def _norm_fwd(h, g, name, res=None):
    tp, d = h.shape
    tr = tp // ROW_TILES
    has_res = res is not None

    def body(*refs):
        if has_res:
            h_ref, r_ref, g_ref, s_ref, hn_ref = refs
            x = h_ref[...] + r_ref[...]
            s_ref[...] = x
        else:
            h_ref, g_ref, hn_ref = refs
            x = h_ref[...]
        hn_ref[...] = (x * _rms(x) * g_ref[...]).astype(BF16)

    in_specs = [_rows(d, tr)] + ([_rows(d, tr)] if has_res else []) + [_const((1, d))]
    out_specs = ([_rows(d, tr)] if has_res else []) + [_rows(d, tr)]
    out_shape = ([jax.ShapeDtypeStruct((tp, d), F32)] if has_res else []) + [jax.ShapeDtypeStruct((tp, d), BF16)]
    args = (h,) + ((res,) if has_res else ()) + (g,)
    out = pl.pallas_call(body, name=name, grid=(ROW_TILES,), in_specs=in_specs, out_specs=out_specs,
                         out_shape=out_shape, compiler_params=_cparams("parallel"))(*args)
    return out if has_res else out[0]


def _norm_bwd(h, g, dhn, dres, name):
    tp, d = h.shape
    tr = tp // ROW_TILES

    def body(h_ref, g_ref, dhn_ref, dres_ref, dh_ref, dhb_ref, dg_ref):
        x = h_ref[...]
        dx, dgs = _rms_bwd(x, _rms(x), g_ref[...], dhn_ref[...])
        dh = dres_ref[...] + dx
        dh_ref[...] = dh
        dhb_ref[...] = dh.astype(BF16)

        @pl.when(pl.program_id(0) == 0)
        def _():
            dg_ref[...] = jnp.zeros_like(dg_ref)

        dg_ref[...] += jnp.sum(dgs, axis=0, keepdims=True)

    return pl.pallas_call(
        body, name=name, grid=(ROW_TILES,),
        in_specs=[_rows(d, tr), _const((1, d)), _rows(d, tr), _rows(d, tr)],
        out_specs=[_rows(d, tr), _rows(d, tr), _const((1, d))],
        out_shape=[jax.ShapeDtypeStruct((tp, d), F32), jax.ShapeDtypeStruct((tp, d), BF16),
                   jax.ShapeDtypeStruct((1, d), F32)],
        compiler_params=_cparams("arbitrary"))(h, g, dhn, dres)


def _input_norm_bwd(h, g, dhn, dres, n_real, name):
    tp, d = h.shape
    tr = tp // ROW_TILES

    def body(h_ref, g_ref, dhn_ref, dres_ref, dx_ref, dmeta_ref, dg_ref, stage, sem):
        i = pl.program_id(0)
        x = h_ref[...]
        dx, dgs = _rms_bwd(x, _rms(x), g_ref[...], dhn_ref[...])
        stage[...] = dres_ref[...] + dx

        @pl.when(i == 0)
        def _():
            dg_ref[...] = jnp.zeros_like(dg_ref)
            dmeta_ref[...] = stage[:N_META, :]

        dg_ref[...] += jnp.sum(dgs, axis=0, keepdims=True)
        for t in range(ROW_TILES):
            lo, hi = max(t * tr, N_META), min((t + 1) * tr, n_real)
            if hi > lo:
                @pl.when(i == t)
                def _(t=t, lo=lo, hi=hi):
                    cp = pltpu.make_async_copy(stage.at[pl.ds(lo - t * tr, hi - lo), :],
                                               dx_ref.at[pl.ds(lo - N_META, hi - lo), :], sem)
                    cp.start()
                    cp.wait()

    return pl.pallas_call(
        body, name=name, grid=(ROW_TILES,),
        in_specs=[_rows(d, tr), _const((1, d)), _rows(d, tr), _rows(d, tr)],
        out_specs=[pl.BlockSpec(memory_space=pl.ANY), _const((N_META, d)), _const((1, d))],
        out_shape=[jax.ShapeDtypeStruct((n_real - N_META, d), F32), jax.ShapeDtypeStruct((N_META, d), F32),
                   jax.ShapeDtypeStruct((1, d), F32)],
        scratch_shapes=[pltpu.VMEM((tr, d), F32), pltpu.SemaphoreType.DMA],
        compiler_params=_cparams("arbitrary"))(h, g, dhn, dres)


def _loss_bwd(h1, dn, tgt, g, n_real, name):
    tp, d = h1.shape
    tr = tp // ROW_TILES

    def body(h1_ref, dn_ref, t_ref, g_ref, loss_ref, dh_ref, dhb_ref, dg_ref):
        i = pl.program_id(0)
        x = h1_ref[...] + dn_ref[...]
        r = _rms(x)
        row = i * tr + lax.broadcasted_iota(jnp.int32, (tr, d), 0)
        valid = (row >= N_META) & (row < n_real)
        e = jnp.where(valid, x * r * g_ref[...] - t_ref[...], 0.0)
        dx, dgs = _rms_bwd(x, r, g_ref[...], e * (1.0 / d))
        dh_ref[...] = dx
        dhb_ref[...] = dx.astype(BF16)

        @pl.when(i == 0)
        def _():
            dg_ref[...] = jnp.zeros_like(dg_ref)
            loss_ref[...] = jnp.zeros_like(loss_ref)

        dg_ref[...] += jnp.sum(dgs, axis=0, keepdims=True)
        loss_ref[...] += (0.5 / d) * jnp.sum(jnp.sum(e * e, axis=0, keepdims=True), axis=1, keepdims=True)

    return pl.pallas_call(
        body, name=name, grid=(ROW_TILES,),
        in_specs=[_rows(d, tr), _rows(d, tr), _rows(d, tr), _const((1, d))],
        out_specs=[_const((1, LANES)), _rows(d, tr), _rows(d, tr), _const((1, d))],
        out_shape=[jax.ShapeDtypeStruct((1, LANES), F32), jax.ShapeDtypeStruct((tp, d), F32),
                   jax.ShapeDtypeStruct((tp, d), BF16), jax.ShapeDtypeStruct((1, d), F32)],
        compiler_params=_cparams("arbitrary"))(h1, dn, tgt, g)


def _mix_fwd(co, y, z, gc, gs, name):
    tp, dh = co.shape
    tr = tp // ROW_TILES

    def body(co_ref, y_ref, z_ref, gc_ref, gs_ref, m_ref):
        c = co_ref[...]
        m_ref[:, :dh] = (c * _rms(c) * gc_ref[...]).astype(BF16)
        so = _gelu(y_ref[...]) * _sigmoid(z_ref[...])
        m_ref[:, dh:] = (so * _rms(so) * gs_ref[...]).astype(BF16)

    return pl.pallas_call(
        body, name=name, grid=(ROW_TILES,),
        in_specs=[_rows(dh, tr)] * 3 + [_const((1, dh))] * 2,
        out_specs=_rows(2 * dh, tr),
        out_shape=jax.ShapeDtypeStruct((tp, 2 * dh), BF16),
        compiler_params=_cparams("parallel"))(co, y, z, gc, gs)


def _mix_bwd(dm, co, y, z, gc, gs, name):
    tp, dh = co.shape
    tr = tp // ROW_TILES

    def body(dm_ref, co_ref, y_ref, z_ref, gc_ref, gs_ref, dco_ref, dz_ref, dgp_ref, dgc_ref, dgs_ref):
        c = co_ref[...]
        dco, dgc = _rms_bwd(c, _rms(c), gc_ref[...], dm_ref[:, :dh])
        dco_ref[...] = dco
        gl = _gelu(y_ref[...])
        sg = _sigmoid(z_ref[...])
        so = gl * sg
        dso, dgs = _rms_bwd(so, _rms(so), gs_ref[...], dm_ref[:, dh:])
        dz_ref[...] = (dso * gl * sg * (1.0 - sg)).astype(BF16)
        dgp_ref[...] = dso * sg

        @pl.when(pl.program_id(0) == 0)
        def _():
            dgc_ref[...] = jnp.zeros_like(dgc_ref)
            dgs_ref[...] = jnp.zeros_like(dgs_ref)

        dgc_ref[...] += jnp.sum(dgc, axis=0, keepdims=True)
        dgs_ref[...] += jnp.sum(dgs, axis=0, keepdims=True)

    return pl.pallas_call(
        body, name=name, grid=(ROW_TILES,),
        in_specs=[_rows(2 * dh, tr)] + [_rows(dh, tr)] * 3 + [_const((1, dh))] * 2,
        out_specs=[_rows(dh, tr), _rows(dh, tr), _rows(dh, tr), _const((1, dh)), _const((1, dh))],
        out_shape=[jax.ShapeDtypeStruct((tp, dh), F32), jax.ShapeDtypeStruct((tp, dh), BF16),
                   jax.ShapeDtypeStruct((tp, dh), F32), jax.ShapeDtypeStruct((1, dh), F32),
                   jax.ShapeDtypeStruct((1, dh), F32)],
        compiler_params=_cparams("arbitrary"))(dm, co, y, z, gc, gs)


def _shift_down(x, k):
    row = lax.broadcasted_iota(jnp.int32, x.shape, 0)
    return jnp.where(row >= k, pltpu.roll(x, k, 0), 0.0)


def _shift_up(x, k):
    n = x.shape[0]
    row = lax.broadcasted_iota(jnp.int32, x.shape, 0)
    return jnp.where(row < n - k, pltpu.roll(x, n - k, 0), 0.0)


def _dwconv(x, w_ref):
    return w_ref[2:3, :] * x + w_ref[1:2, :] * _shift_down(x, 1) + w_ref[0:1, :] * _shift_down(x, 2)


def _dwconv_bwd(x, dy, w_ref):
    dx = w_ref[2:3, :] * dy + w_ref[1:2, :] * _shift_up(dy, 1) + w_ref[0:1, :] * _shift_up(dy, 2)
    dw = jnp.concatenate([jnp.sum(dy * _shift_down(x, 2), axis=0, keepdims=True),
                          jnp.sum(dy * _shift_down(x, 1), axis=0, keepdims=True),
                          jnp.sum(dy * x, axis=0, keepdims=True)], axis=0)
    return dx, dw


def _scan(s_re, s_im, tab_ref, reverse):
    n_chunks = s_re.shape[0] // SUBLANES
    n_strips = s_re.shape[1] // LANES
    last = 0 if reverse else SUBLANES - 1

    def body(i, carry):
        chunk = (n_chunks - 1 - i) if reverse else i
        r0 = pl.multiple_of(chunk * SUBLANES, SUBLANES)
        out = []
        for st in range(n_strips):
            lanes = slice(st * LANES, (st + 1) * LANES)
            cr, ci = carry[2 * st], carry[2 * st + 1]
            xr = s_re[pl.ds(r0, SUBLANES), lanes]
            xi = s_im[pl.ds(r0, SUBLANES), lanes]
            for level, k in enumerate((1, 2, 4)):
                mr = tab_ref[2 * level, :, lanes]
                mi = tab_ref[2 * level + 1, :, lanes]
                sh = SUBLANES - k if reverse else k
                rr = pltpu.roll(xr, sh, 0)
                ri = pltpu.roll(xi, sh, 0)
                xr, xi = xr + (mr * rr - mi * ri), xi + (mr * ri + mi * rr)
            pwr = tab_ref[6, :, lanes]
            pwi = tab_ref[7, :, lanes]
            xr, xi = xr + (pwr * cr - pwi * ci), xi + (pwr * ci + pwi * cr)
            s_re[pl.ds(r0, SUBLANES), lanes] = xr
            s_im[pl.ds(r0, SUBLANES), lanes] = xi
            out.append(jnp.broadcast_to(xr[last:last + 1, :], (SUBLANES, LANES)))
            out.append(jnp.broadcast_to(xi[last:last + 1, :], (SUBLANES, LANES)))
        return tuple(out)

    zero = jnp.zeros((SUBLANES, LANES), F32)
    lax.fori_loop(0, n_chunks, body, (zero,) * (2 * n_strips))


def _seq_fwd(proj, conv_w, bc_re, bc_im, cc_re, cc_im, dskip, tab_f, name):
    tp = proj.shape[0]
    dh = proj.shape[1] // 4
    nq = dh // LANES
    sw = STATE * N_GROUPS // nq

    def body(b_ref, c_ref, v_ref, u_ref, w_ref, bre_ref, bim_ref, cre_ref, cim_ref, d_ref, tab_ref,
             co_ref, y_ref, g_ref, s_re, s_im):
        co_ref[...] = b_ref[...] * _dwconv(c_ref[...] * v_ref[...], w_ref)
        u = u_ref[...]
        ub = u.astype(BF16)
        s_re[...] = jnp.dot(ub, bre_ref[...], preferred_element_type=F32)
        s_im[...] = jnp.dot(ub, bim_ref[...], preferred_element_type=F32)
        _scan(s_re, s_im, tab_ref, False)
        y = (jnp.dot(s_re[...].astype(BF16), cre_ref[...], preferred_element_type=F32)
             - jnp.dot(s_im[...].astype(BF16), cim_ref[...], preferred_element_type=F32)
             + d_ref[...] * u)
        y_ref[...] = y
        g_ref[...] = _gelu(y).astype(BF16)

    col = lambda off: pl.BlockSpec((tp, LANES), lambda q, off=off: (0, off * nq + q))
    blk = pl.BlockSpec((tp, LANES), lambda q: (0, q))
    return pl.pallas_call(
        body, name=name, grid=(nq,),
        in_specs=[col(0), col(1), col(2), col(3),
                  pl.BlockSpec((3, LANES), lambda q: (0, q)),
                  pl.BlockSpec((LANES, sw), lambda q: (0, q)), pl.BlockSpec((LANES, sw), lambda q: (0, q)),
                  pl.BlockSpec((sw, LANES), lambda q: (q, 0)), pl.BlockSpec((sw, LANES), lambda q: (q, 0)),
                  pl.BlockSpec((1, LANES), lambda q: (0, q)),
                  pl.BlockSpec((8, SUBLANES, sw), lambda q: (0, 0, q))],
        out_specs=[blk, blk, blk],
        out_shape=[jax.ShapeDtypeStruct((tp, dh), F32), jax.ShapeDtypeStruct((tp, dh), F32),
                   jax.ShapeDtypeStruct((tp, dh), BF16)],
        scratch_shapes=[pltpu.VMEM((tp, sw), F32), pltpu.VMEM((tp, sw), F32)],
        compiler_params=_cparams("parallel"),
    )(proj, proj, proj, proj, conv_w, bc_re, bc_im, cc_re, cc_im, dskip, tab_f)


def _conv_bwd(proj, dco, conv_w, name):
    tp = proj.shape[0]
    dh = proj.shape[1] // 4
    nq = dh // LANES

    def body(b_ref, c_ref, v_ref, dco_ref, w_ref, dproj_ref, dw_ref, stage, sem):
        q = pl.program_id(0)
        cg = c_ref[...]
        vg = v_ref[...]
        cv = cg * vg
        dco_v = dco_ref[...]
        dcv, dw = _dwconv_bwd(cv, dco_v * b_ref[...], w_ref)
        dw_ref[...] = dw
        stage[0] = (dco_v * _dwconv(cv, w_ref)).astype(BF16)
        stage[1] = (dcv * vg).astype(BF16)
        stage[2] = (dcv * cg).astype(BF16)
        copies = [pltpu.make_async_copy(stage.at[p], dproj_ref.at[:, pl.ds((p * nq + q) * LANES, LANES)], sem.at[p])
                  for p in range(3)]
        for cp in copies:
            cp.start()
        for cp in copies:
            cp.wait()

    col = lambda off: pl.BlockSpec((tp, LANES), lambda q, off=off: (0, off * nq + q))
    return pl.pallas_call(
        body, name=name, grid=(nq,),
        in_specs=[col(0), col(1), col(2), pl.BlockSpec((tp, LANES), lambda q: (0, q)),
                  pl.BlockSpec((3, LANES), lambda q: (0, q))],
        out_specs=[pl.BlockSpec(memory_space=pl.ANY), pl.BlockSpec((3, LANES), lambda q: (0, q))],
        out_shape=[jax.ShapeDtypeStruct((tp, 4 * dh), BF16), jax.ShapeDtypeStruct((3, dh), F32)],
        scratch_shapes=[pltpu.VMEM((3, tp, LANES), BF16), pltpu.SemaphoreType.DMA((3,))],
        compiler_params=_cparams("arbitrary"),
    )(proj, proj, proj, dco, conv_w)


def _ssm_bwd(proj, y, dg, dproj, bc_re, bc_im, cc_re, cc_im, dskip, tab_f, tab_r, name):
    tp = proj.shape[0]
    dh = proj.shape[1] // 4
    nq = dh // LANES
    sw = STATE * N_GROUPS // nq

    def body(u_ref, y_ref, dg_ref, dproj_in, bre_ref, bim_ref, cre_ref, cim_ref, d_ref, tabf_ref, tabr_ref,
             dproj_ref, dbre_ref, dbim_ref, dcre_ref, dcim_ref, dd_ref, dar_ref, dai_ref,
             s_re, s_im, l_re, l_im, stage, sem):
        del dproj_in
        q = pl.program_id(0)
        nt = (((1,), (1,)), ((), ()))
        tn = (((0,), (0,)), ((), ()))
        u = u_ref[...]
        ub = u.astype(BF16)
        s_re[...] = jnp.dot(ub, bre_ref[...], preferred_element_type=F32)
        s_im[...] = jnp.dot(ub, bim_ref[...], preferred_element_type=F32)
        _scan(s_re, s_im, tabf_ref, False)
        dy = dg_ref[...] * _gelu_grad(y_ref[...])
        dyb = dy.astype(BF16)
        dd_ref[...] = jnp.sum(dy * u, axis=0, keepdims=True)
        l_re[...] = lax.dot_general(dyb, cre_ref[...], nt, preferred_element_type=F32)
        l_im[...] = -lax.dot_general(dyb, cim_ref[...], nt, preferred_element_type=F32)
        dcre_ref[...] = lax.dot_general(s_re[...].astype(BF16), dyb, tn, preferred_element_type=F32)
        dcim_ref[...] = -lax.dot_general(s_im[...].astype(BF16), dyb, tn, preferred_element_type=F32)
        _scan(l_re, l_im, tabr_ref, True)
        for st in range(sw // LANES):
            lanes = slice(st * LANES, (st + 1) * LANES)
            lr = l_re[:, lanes]
            li = l_im[:, lanes]
            pr = _shift_down(s_re[:, lanes], 1)
            pi = _shift_down(s_im[:, lanes], 1)
            dar_ref[:, lanes] = jnp.sum(lr * pr + li * pi, axis=0, keepdims=True)
            dai_ref[:, lanes] = jnp.sum(li * pr - lr * pi, axis=0, keepdims=True)
        lrb = l_re[...].astype(BF16)
        lib = l_im[...].astype(BF16)
        du = (dy * d_ref[...] + lax.dot_general(lrb, bre_ref[...], nt, preferred_element_type=F32)
              + lax.dot_general(lib, bim_ref[...], nt, preferred_element_type=F32))
        stage[...] = du.astype(BF16)
        dbre_ref[...] = lax.dot_general(ub, lrb, tn, preferred_element_type=F32)
        dbim_ref[...] = lax.dot_general(ub, lib, tn, preferred_element_type=F32)
        cp = pltpu.make_async_copy(stage, dproj_ref.at[:, pl.ds((3 * nq + q) * LANES, LANES)], sem)
        cp.start()
        cp.wait()

    blk = pl.BlockSpec((tp, LANES), lambda q: (0, q))
    bspec = pl.BlockSpec((LANES, sw), lambda q: (0, q))
    cspec = pl.BlockSpec((sw, LANES), lambda q: (q, 0))
    tspec = pl.BlockSpec((8, SUBLANES, sw), lambda q: (0, 0, q))
    nstate = STATE * N_GROUPS
    return pl.pallas_call(
        body, name=name, grid=(nq,),
        in_specs=[pl.BlockSpec((tp, LANES), lambda q: (0, 3 * nq + q)), blk, blk, pl.BlockSpec(memory_space=pl.ANY),
                  bspec, bspec, cspec, cspec, pl.BlockSpec((1, LANES), lambda q: (0, q)), tspec, tspec],
        out_specs=[pl.BlockSpec(memory_space=pl.ANY), bspec, bspec, cspec, cspec,
                   pl.BlockSpec((1, LANES), lambda q: (0, q)),
                   pl.BlockSpec((1, sw), lambda q: (0, q)), pl.BlockSpec((1, sw), lambda q: (0, q))],
        out_shape=[jax.ShapeDtypeStruct((tp, 4 * dh), BF16),
                   jax.ShapeDtypeStruct((LANES, nstate), F32), jax.ShapeDtypeStruct((LANES, nstate), F32),
                   jax.ShapeDtypeStruct((nstate, LANES), F32), jax.ShapeDtypeStruct((nstate, LANES), F32),
                   jax.ShapeDtypeStruct((1, dh), F32),
                   jax.ShapeDtypeStruct((1, nstate), F32), jax.ShapeDtypeStruct((1, nstate), F32)],
        input_output_aliases={3: 0},
        scratch_shapes=[pltpu.VMEM((tp, sw), F32)] * 4 + [pltpu.VMEM((tp, LANES), BF16), pltpu.SemaphoreType.DMA],
        compiler_params=_cparams("arbitrary"),
    )(proj, y, dg, dproj, bc_re, bc_im, cc_re, cc_im, dskip, tab_f, tab_r)


FFN_TILE = 256


def _ffn_act(up, fw, fb, name):
    tp, two_ff = up.shape
    dff = two_ff // 2
    tc = FFN_TILE
    nj = dff // tc

    def body(ua_ref, uv_ref, wa_ref, wv_ref, ba_ref, bv_ref, act_ref):
        a = _dwconv(ua_ref[...], wa_ref) + ba_ref[...]
        v = _dwconv(uv_ref[...], wv_ref) + bv_ref[...]
        act_ref[...] = (a * _sigmoid(a) * v).astype(BF16)

    lo = lambda r: pl.BlockSpec((r, tc), lambda j: (0, j))
    hi = lambda r: pl.BlockSpec((r, tc), lambda j: (0, nj + j))
    return pl.pallas_call(
        body, name=name, grid=(nj,),
        in_specs=[lo(tp), hi(tp), lo(3), hi(3), lo(1), hi(1)],
        out_specs=lo(tp),
        out_shape=jax.ShapeDtypeStruct((tp, dff), BF16),
        compiler_params=_cparams("parallel"))(up, up, fw, fw, fb, fb)


def _ffn_bwd(up, dact, fw, fb, name):
    tp, two_ff = up.shape
    dff = two_ff // 2
    tc = FFN_TILE
    nj = dff // tc

    def body(ua_ref, uv_ref, da_ref, wa_ref, wv_ref, ba_ref, bv_ref,
             dup_ref, dwa_ref, dwv_ref, dba_ref, dbv_ref, stage, sem):
        j = pl.program_id(0)
        ua = ua_ref[...]
        uv = uv_ref[...]
        a = _dwconv(ua, wa_ref) + ba_ref[...]
        v = _dwconv(uv, wv_ref) + bv_ref[...]
        sg = _sigmoid(a)
        dact_v = da_ref[...]
        da = dact_v * v * sg * (1.0 + a * (1.0 - sg))
        dv = dact_v * a * sg
        dba_ref[...] = jnp.sum(da, axis=0, keepdims=True)
        dbv_ref[...] = jnp.sum(dv, axis=0, keepdims=True)
        dua, dwa = _dwconv_bwd(ua, da, wa_ref)
        duv, dwv = _dwconv_bwd(uv, dv, wv_ref)
        dwa_ref[...] = dwa
        dwv_ref[...] = dwv
        stage[0] = dua.astype(BF16)
        stage[1] = duv.astype(BF16)
        copies = [pltpu.make_async_copy(stage.at[p], dup_ref.at[:, pl.ds((p * nj + j) * tc, tc)], sem.at[p])
                  for p in range(2)]
        for cp in copies:
            cp.start()
        for cp in copies:
            cp.wait()

    lo = lambda r: pl.BlockSpec((r, tc), lambda j: (0, j))
    hi = lambda r: pl.BlockSpec((r, tc), lambda j: (0, nj + j))
    return pl.pallas_call(
        body, name=name, grid=(nj,),
        in_specs=[lo(tp), hi(tp), lo(tp), lo(3), hi(3), lo(1), hi(1)],
        out_specs=[pl.BlockSpec(memory_space=pl.ANY), lo(3), lo(3), lo(1), lo(1)],
        out_shape=[jax.ShapeDtypeStruct((tp, two_ff), BF16),
                   jax.ShapeDtypeStruct((3, dff), F32), jax.ShapeDtypeStruct((3, dff), F32),
                   jax.ShapeDtypeStruct((1, dff), F32), jax.ShapeDtypeStruct((1, dff), F32)],
        scratch_shapes=[pltpu.VMEM((2, tp, tc), BF16), pltpu.SemaphoreType.DMA((2,))],
        compiler_params=_cparams("arbitrary"))(up, up, dact, fw, fw, fb, fb)


def _zoh(lr, li, ld):
    dt = jnp.exp(ld)
    mag = jnp.exp(lr * dt)
    ang = li * dt
    ar = mag * jnp.cos(ang)
    ai = mag * jnp.sin(ang)
    den = lr * lr + li * li
    nr = ar - 1.0
    fr = (nr * lr + ai * li) / den
    fi = (ai * lr - nr * li) / den
    return dt, ar, ai, den, nr, fr, fi


def _s5_prep(lr, li, ld, b_re, b_im, name):
    nstate = lr.shape[1]

    def tables(tab_ref, ar, ai, reverse):
        pows = [(ar, ai)]
        for _ in range(SUBLANES - 1):
            pr, pi = pows[-1]
            pows.append((pr * ar - pi * ai, pr * ai + pi * ar))
        row = lax.broadcasted_iota(jnp.int32, (SUBLANES, nstate), 0)
        for level, k in enumerate((1, 2, 4)):
            mask = (row <= SUBLANES - 1 - k) if reverse else (row >= k)
            tab_ref[2 * level] = jnp.where(mask, pows[k - 1][0], 0.0)
            tab_ref[2 * level + 1] = jnp.where(mask, pows[k - 1][1], 0.0)
        pr = jnp.zeros((SUBLANES, nstate), F32)
        pi = jnp.zeros((SUBLANES, nstate), F32)
        for t in range(SUBLANES):
            k = SUBLANES - 1 - t if reverse else t
            pr = jnp.where(row == t, pows[k][0], pr)
            pi = jnp.where(row == t, pows[k][1], pi)
        tab_ref[6] = pr
        tab_ref[7] = pi

    def body(lr_ref, li_ref, ld_ref, bre_ref, bim_ref, tabf_ref, tabr_ref, bcre_ref, bcim_ref):
        _, ar, ai, _, _, fr, fi = _zoh(lr_ref[...], li_ref[...], ld_ref[...])
        tables(tabf_ref, ar, ai, False)
        tables(tabr_ref, ar, -ai, True)
        bre = bre_ref[...]
        bim = bim_ref[...]
        bcre_ref[...] = (fr * bre - fi * bim).astype(BF16)
        bcim_ref[...] = (fr * bim + fi * bre).astype(BF16)

    vmem = pl.BlockSpec(memory_space=pltpu.VMEM)
    return pl.pallas_call(
        body, name=name, in_specs=[vmem] * 5, out_specs=[vmem] * 4,
        out_shape=[jax.ShapeDtypeStruct((8, SUBLANES, nstate), F32)] * 2
        + [jax.ShapeDtypeStruct(b_re.shape, BF16)] * 2)(lr, li, ld, b_re, b_im)


def _s5_prep_bwd(lr, li, ld, b_re, b_im, da_re, da_im, dbc_re, dbc_im, name):
    def body(lr_ref, li_ref, ld_ref, bre_ref, bim_ref, dar_ref, dai_ref, dbcre_ref, dbcim_ref,
             dlr_ref, dli_ref, dld_ref, dbre_ref, dbim_ref):
        lr, li = lr_ref[...], li_ref[...]
        dt, ar, ai, den, nr, fr, fi = _zoh(lr, li, ld_ref[...])
        bre, bim = bre_ref[...], bim_ref[...]
        gre, gim = dbcre_ref[...], dbcim_ref[...]
        dbre_ref[...] = fr * gre + fi * gim
        dbim_ref[...] = fr * gim - fi * gre
        g_fr = jnp.sum(gre * bre + gim * bim, axis=0, keepdims=True)
        g_fi = jnp.sum(gim * bre - gre * bim, axis=0, keepdims=True)
        g_ar = dar_ref[...] + (g_fr * lr - g_fi * li) / den
        g_ai = dai_ref[...] + (g_fr * li + g_fi * lr) / den
        d_lr = (g_fr * (nr - 2.0 * fr * lr) + g_fi * (ai - 2.0 * fi * lr)) / den
        d_li = (g_fr * (ai - 2.0 * fr * li) - g_fi * (nr + 2.0 * fi * li)) / den
        g_logmag = g_ar * ar + g_ai * ai
        g_ang = g_ai * ar - g_ar * ai
        dlr_ref[...] = d_lr + g_logmag * dt
        dli_ref[...] = d_li + g_ang * dt
        d_ld = (g_logmag * lr + g_ang * li) * dt
        n = d_ld.shape[1]
        sh = 1
        while sh < STATE:
            d_ld = d_ld + pltpu.roll(d_ld, n - sh, 1)
            sh *= 2
        dld_ref[...] = d_ld

    vmem = pl.BlockSpec(memory_space=pltpu.VMEM)
    row = jax.ShapeDtypeStruct(lr.shape, F32)
    return pl.pallas_call(
        body, name=name, in_specs=[vmem] * 9, out_specs=[vmem] * 5,
        out_shape=[row, row, row, jax.ShapeDtypeStruct(b_re.shape, F32), jax.ShapeDtypeStruct(b_re.shape, F32)],
    )(lr, li, ld, b_re, b_im, da_re, da_im, dbc_re, dbc_im)


def _compact_b(bb):
    bq = bb.reshape(N_GROUPS // 8, 8, STATE, GROUP)
    m = jnp.einsum("ab,qbph->qahbp", jnp.eye(8, dtype=bb.dtype), bq).reshape(N_GROUPS // 8, LANES, 8 * STATE)
    return m.transpose(1, 0, 2).reshape(LANES, N_GROUPS * STATE)


def _expand_b(m):
    d = m.reshape(8, GROUP, N_GROUPS // 8, 8, STATE)
    return jnp.einsum("ahqap->qaph", d).reshape(N_GROUPS, STATE, GROUP)


def _compact_c(c):
    cq = c.reshape(N_GROUPS // 8, 8, GROUP, STATE)
    return jnp.einsum("ab,qbhp->qbpah", jnp.eye(8, dtype=c.dtype), cq).reshape(N_GROUPS * STATE, LANES)


def _expand_c(m):
    d = m.reshape(N_GROUPS // 8, 8, STATE, 8, GROUP)
    return jnp.einsum("qbpbh->qbhp", d).reshape(N_GROUPS, GROUP, STATE)


def _local_step(x, target, p, ex):
    seq, d = x.shape
    n_real = N_META + seq
    tp = -(-n_real // ROW_ALIGN) * ROW_ALIGN
    pad = jnp.zeros((tp - n_real, d), F32)
    h0 = jnp.concatenate([p["meta_tokens"], x, pad], axis=0)
    tgt = jnp.concatenate([jnp.zeros((N_META, d), F32), target, pad], axis=0)

    nstate = N_GROUPS * STATE
    s5 = (p["ssm_lam_re"].reshape(1, nstate), p["ssm_lam_im"].reshape(1, nstate),
          jnp.repeat(p["ssm_log_dt"].reshape(-1), STATE).reshape(1, nstate),
          _compact_b(p["ssm_b_re"]), _compact_b(p["ssm_b_im"]))
    tab_f, tab_r, bc_re, bc_im = _s5_prep(*s5, "s5_prep")
    cc_re = _compact_c(p["ssm_c_re"]).astype(BF16)
    cc_im = _compact_c(p["ssm_c_im"]).astype(BF16)
    dskip = p["ssm_d"].reshape(1, -1)
    dh = dskip.shape[1]

    hn1 = _norm_fwd(h0, p["norm_mix_g"] + ex.zero, "norm_mix")
    proj = _mm(hn1, p["w_in"], "nn", "proj")
    co, y, g = _seq_fwd(proj, p["conv_w"], bc_re, bc_im, cc_re, cc_im, dskip, tab_f, "seq_fwd")
    mid = ex.weights("mid", g)
    z = _mm(g, mid["ssm_w_glu"], "nn", "glu")
    mixed = _mix_fwd(co, y, z, p["gain_conv_out"], p["gain_ssm_out"], "mix_fwd")
    mo = _mm(mixed, mid["w_out"], "nn", "out_proj")
    h1, hn2 = _norm_fwd(h0, p["norm_ffn_g"], "norm_ffn", res=mo)
    late = ex.weights("late", hn2)
    up = _mm(hn2, late["w_up"], "nn", "up_proj")
    act = _ffn_act(up, p["ffn_conv_w"], p["ffn_conv_b"], "ffn_act")
    dn = _mm(act, late["w_down"], "nn", "down_proj")
    loss, dh2, dh2b, d_gfin = _loss_bwd(h1, dn, tgt, p["norm_final_g"], n_real, "loss_bwd")

    g_w_down = _mm(act, dh2b, "tn", "g_w_down")
    dact = _mm(dh2b, late["w_down"], "nt", "d_act")
    dup, dfw_a, dfw_v, dfb_a, dfb_v = _ffn_bwd(up, dact, p["ffn_conv_w"], p["ffn_conv_b"], "ffn_bwd")
    g_w_up = _mm(hn2, dup, "tn", "g_w_up")
    started = ex.grads_ready("late", {"w_up": g_w_up, "w_down": g_w_down})
    dhn2 = _mm(dup, late["w_up"], "nt", "d_hn2", after=started)
    started = ex.grads_send("late", dhn2)
    dh1, dh1b, d_gffn = _norm_bwd(h1, p["norm_ffn_g"] + started[0, 0], dhn2, dh2, "norm_ffn_bwd")
    g_w_out = _mm(mixed, dh1b, "tn", "g_w_out")
    dmixed = _mm(dh1b, mid["w_out"], "nt", "d_mixed")
    dco, dz, dgp, d_gc, d_gs = _mix_bwd(dmixed, co, y, z, p["gain_conv_out"], p["gain_ssm_out"], "mix_bwd")
    g_w_glu = _mm(g, dz, "tn", "g_w_glu")
    started = ex.grads_ready("mid", {"ssm_w_glu": g_w_glu, "w_out": g_w_out})
    dg = _mm(dz, mid["ssm_w_glu"], "nt", "d_gelu", acc_in=dgp, after=started)
    started = ex.grads_send("mid", dg)
    dproj, d_conv_w = _conv_bwd(proj, dco, p["conv_w"] + started[0, 0], "conv_bwd")
    (dproj, dbc_re, dbc_im, dcc_re, dcc_im, d_dskip, da_re, da_im) = _ssm_bwd(
        proj, y, dg, dproj, bc_re, bc_im, cc_re, cc_im, dskip, tab_f, tab_r, "ssm_bwd")
    g_w_in = _mm(hn1, dproj, "tn", "g_w_in")
    started = ex.grads_ready("first", {"w_in": g_w_in})
    dhn1 = _mm(dproj, p["w_in"], "nt", "d_hn1", after=started)
    started = ex.grads_send("first", dhn1)
    grad_x, d_meta, d_gmix = _input_norm_bwd(h0, p["norm_mix_g"] + started[0, 0], dhn1, dh1, n_real, "norm_mix_bwd")

    d_lam_re, d_lam_im, d_log_dt, d_b_re, d_b_im = _s5_prep_bwd(*s5, da_re, da_im, dbc_re, dbc_im, "s5_prep_bwd")
    d_lam_re, d_lam_im = d_lam_re.reshape(N_GROUPS, STATE), d_lam_im.reshape(N_GROUPS, STATE)
    d_log_dt = d_log_dt[0, ::STATE]
    d_b_re, d_b_im = _expand_b(d_b_re), _expand_b(d_b_im)
    grads = {
        "meta_tokens": d_meta, "norm_mix_g": d_gmix, "w_in": g_w_in, "conv_w": d_conv_w,
        "ssm_lam_re": d_lam_re, "ssm_lam_im": d_lam_im, "ssm_log_dt": d_log_dt,
        "ssm_b_re": d_b_re, "ssm_b_im": d_b_im, "ssm_c_re": _expand_c(dcc_re), "ssm_c_im": _expand_c(dcc_im),
        "ssm_d": d_dskip.reshape(N_GROUPS, GROUP), "ssm_w_glu": g_w_glu,
        "gain_conv_out": d_gc, "gain_ssm_out": d_gs, "w_out": g_w_out, "norm_ffn_g": d_gffn,
        "w_up": g_w_up, "ffn_conv_w": jnp.concatenate([dfw_a, dfw_v], axis=1),
        "ffn_conv_b": jnp.concatenate([dfb_a, dfb_v], axis=1), "w_down": g_w_down, "norm_final_g": d_gfin,
    }
    return loss[0, 0], grad_x, grads


def _view(ref, axis, start, size):
    idx = [slice(None)] * len(ref.shape)
    idx[axis] = pl.ds(start, size)
    return ref.at[tuple(idx)]


def _exchange(name, ins, outs, aliases, local_copies, remote_copies):
    ni, no = len(ins), len(outs)
    nl, nr = len(local_copies), len(remote_copies)

    def body(*refs):
        in_refs, out_refs = refs[:ni], refs[ni:ni + no]
        send_sems, recv_sems, local_sems = refs[ni + no:]
        x, y, c = lax.axis_index("x"), lax.axis_index("y"), lax.axis_index("c")
        pos = (x, y, c, 2 * x + y)
        locals_ = [pltpu.make_async_copy(s(in_refs, out_refs, pos), d(in_refs, out_refs, pos), local_sems.at[i])
                   for i, (s, d) in enumerate(local_copies)]
        remotes = []
        for i, (s, d, flip) in enumerate(remote_copies):
            peer = (1 - x if "x" in flip else x, 1 - y if "y" in flip else y, 1 - c if "c" in flip else c)
            remotes.append(pltpu.make_async_remote_copy(
                src_ref=s(in_refs, out_refs, pos), dst_ref=d(in_refs, out_refs, pos),
                send_sem=send_sems.at[i], recv_sem=recv_sems.at[i], device_id=peer, device_id_type=MESH))
        for cp in locals_ + remotes:
            cp.start()
        for cp in remotes:
            cp.wait_recv()
        for cp in remotes:
            cp.wait_send()
        for cp in locals_:
            cp.wait()

    hbm = pl.BlockSpec(memory_space=pl.ANY)
    return pl.pallas_call(
        body, name=name, in_specs=[hbm] * ni, out_specs=[hbm] * no, out_shape=outs,
        input_output_aliases=aliases,
        scratch_shapes=[pltpu.SemaphoreType.DMA((nr,)), pltpu.SemaphoreType.DMA((nr,)),
                        pltpu.SemaphoreType.DMA((max(nl, 1),))],
    )(*ins)


BIG = {"w_in": (0, 1), "ssm_w_glu": (1, 0), "w_out": (1, 0), "w_up": (0, 1), "w_down": (1, 0)}
BIG_NAMES = tuple(BIG)
FLIPS = ("y", "x", "xy")


def _peer_chip(pos, flip):
    x, y, _, _ = pos
    return 2 * (1 - x if "x" in flip else x) + (1 - y if "y" in flip else y)


def _block_rows(rows, cols, itemsize, mult):
    return _pick_tile(rows, max(mult, (2 * 1024 * 1024) // (cols * itemsize)), mult)


def _cast_into_full(w, kc, shard_axis, name):
    r, cdim = w.shape
    tr = _block_rows(r, cdim, 4, 16)
    nb = r // tr

    def body(kc_ref, w_ref, o_ref):
        o_ref[...] = w_ref[...].astype(BF16)

    if shard_axis == 1:
        full, o_spec = (r, 4 * cdim), pl.BlockSpec((tr, cdim), lambda i, kc: (i, kc[0]))
    else:
        full, o_spec = (4 * r, cdim), pl.BlockSpec((tr, cdim), lambda i, kc: (kc[0] * nb + i, 0))
    return pl.pallas_call(
        body, name=name,
        grid_spec=pltpu.PrefetchScalarGridSpec(
            num_scalar_prefetch=1, grid=(nb,), in_specs=[pl.BlockSpec((tr, cdim), lambda i, kc: (i, 0))],
            out_specs=o_spec),
        out_shape=jax.ShapeDtypeStruct(full, BF16), compiler_params=_cparams("parallel"))(kc, w)


def _pair_sum(g, recv, kc, half_axis, name, out_dtype):
    hr, hc = recv.shape
    tr = _block_rows(hr, hc, 4, 16)
    nb = hr // tr

    def body(kc_ref, g_ref, r_ref, o_ref):
        o_ref[...] = (g_ref[...] + r_ref[...]).astype(out_dtype)

    if half_axis == 0:
        g_spec = pl.BlockSpec((tr, hc), lambda i, kc: (kc[1] * nb + i, 0))
    elif half_axis == 1:
        g_spec = pl.BlockSpec((tr, hc), lambda i, kc: (i, kc[1]))
    else:
        g_spec = pl.BlockSpec((tr, hc), lambda i, kc: (i, 0))
    same = pl.BlockSpec((tr, hc), lambda i, kc: (i, 0))
    return pl.pallas_call(
        body, name=name,
        grid_spec=pltpu.PrefetchScalarGridSpec(num_scalar_prefetch=1, grid=(nb,), in_specs=[g_spec, same],
                                               out_specs=same),
        out_shape=jax.ShapeDtypeStruct((hr, hc), out_dtype), compiler_params=_cparams("parallel"))(kc, g, recv)


def _chip_sum(own, recv, kc, own_axis, out_axis, name):
    _, sr, sc = recv.shape
    tr = _block_rows(sr, sc, 4, 16)
    nb = sr // tr

    def body(kc_ref, o_ref, r_ref, t_ref):
        k = kc_ref[0]
        own_v = o_ref[...].astype(F32)
        r = [r_ref[m].astype(F32) for m in range(3)]
        terms = []
        for kk in range(4):
            m = jnp.bitwise_xor(k, kk)
            terms.append(jnp.where(m == 0, own_v, jnp.where(m == 1, r[0], jnp.where(m == 2, r[1], r[2]))))
        t_ref[...] = (terms[0] + terms[1]) + (terms[2] + terms[3])

    if own_axis == 0:
        own_spec = pl.BlockSpec((tr, sc), lambda i, kc: (kc[0] * nb + i, 0))
    elif own_axis == 1:
        own_spec = pl.BlockSpec((tr, sc), lambda i, kc: (i, kc[0]))
    else:
        own_spec = pl.BlockSpec((tr, sc), lambda i, kc: (kc[1] * nb + i, 0))
    if out_axis == 0:
        out_full, out_spec = (2 * sr, sc), pl.BlockSpec((tr, sc), lambda i, kc: (kc[1] * nb + i, 0))
    else:
        out_full, out_spec = (sr, 2 * sc), pl.BlockSpec((tr, sc), lambda i, kc: (i, kc[1]))
    return pl.pallas_call(
        body, name=name,
        grid_spec=pltpu.PrefetchScalarGridSpec(
            num_scalar_prefetch=1, grid=(nb,),
            in_specs=[own_spec, pl.BlockSpec((3, tr, sc), lambda i, kc: (0, i, 0))],
            out_specs=out_spec),
        out_shape=jax.ShapeDtypeStruct(out_full, F32), compiler_params=_cparams("parallel"))(kc, own, recv)


def _adamw(w, g, m, v, name):
    r, cdim = w.shape
    tr = _block_rows(r, cdim, 4, 8)
    c1 = 1.0 - ADAM_B1 ** ADAM_STEP
    c2 = 1.0 - ADAM_B2 ** ADAM_STEP

    def body(w_ref, g_ref, m_ref, v_ref, d_ref, nm_ref, nv_ref):
        gv = g_ref[...]
        nm = ADAM_B1 * m_ref[...] + (1.0 - ADAM_B1) * gv
        nv = ADAM_B2 * v_ref[...] + (1.0 - ADAM_B2) * (gv * gv)
        d_ref[...] = -ADAM_LR * ((nm / c1) / (jnp.sqrt(nv / c2) + ADAM_EPS) + ADAM_WD * w_ref[...])
        nm_ref[...] = nm
        nv_ref[...] = nv

    spec = _rows(cdim, tr)
    return pl.pallas_call(body, name=name, grid=(r // tr,), in_specs=[spec] * 4, out_specs=[spec] * 3,
                          out_shape=[jax.ShapeDtypeStruct((r, cdim), F32)] * 3,
                          compiler_params=_cparams("parallel"))(w, g, m, v)


def _adamw_whole(ws, gs, ms, vs, name):
    n = len(ws)
    c1 = 1.0 - ADAM_B1 ** ADAM_STEP
    c2 = 1.0 - ADAM_B2 ** ADAM_STEP

    def body(*refs):
        for i in range(n):
            w_ref, g_ref, m_ref, v_ref, d_ref, nm_ref, nv_ref = [refs[j * n + i] for j in range(7)]
            gv = g_ref[...]
            nm = ADAM_B1 * m_ref[...] + (1.0 - ADAM_B1) * gv
            nv = ADAM_B2 * v_ref[...] + (1.0 - ADAM_B2) * (gv * gv)
            d_ref[...] = -ADAM_LR * ((nm / c1) / (jnp.sqrt(nv / c2) + ADAM_EPS) + ADAM_WD * w_ref[...])
            nm_ref[...] = nm
            nv_ref[...] = nv

    vmem = pl.BlockSpec(memory_space=pltpu.VMEM)
    out = pl.pallas_call(body, name=name, in_specs=[vmem] * (4 * n), out_specs=[vmem] * (3 * n),
                         out_shape=[jax.ShapeDtypeStruct(a.shape, F32) for a in ws] * 3,
                         compiler_params=pltpu.CompilerParams(vmem_limit_bytes=VMEM_LIMIT))(*ws, *gs, *ms, *vs)
    return out[:n], out[n:2 * n], out[2 * n:]


SIDE_EFFECT = pltpu.SideEffectType.DATAFLOW_SIDE_EFFECTING


def _descriptors(copies, refs, send_sems, recv_sems):
    x, y, c = lax.axis_index("x"), lax.axis_index("y"), lax.axis_index("c")
    pos = (x, y, c, 2 * x + y)
    out = []
    for i, (s, d, flip) in enumerate(copies):
        peer = (1 - x if "x" in flip else x, 1 - y if "y" in flip else y, 1 - c if "c" in flip else c)
        out.append(pltpu.make_async_remote_copy(
            src_ref=s(refs, refs, pos), dst_ref=d(refs, refs, pos),
            send_sem=send_sems.at[i], recv_sem=recv_sems.at[i], device_id=peer, device_id_type=MESH))
    return out


def _exchange_start(name, bufs, copies, after=None):
    n, nr = len(bufs), len(copies)
    na = 0 if after is None else 1

    def body(*refs):
        for cp in _descriptors(copies, refs[:n], refs[n + na], refs[n + na + 1]):
            cp.start()
        token = refs[2 * n + na + 2]
        token[...] = jnp.zeros_like(token)

    hbm = pl.BlockSpec(memory_space=pltpu.HBM)
    sem = pl.BlockSpec(memory_space=pltpu.SEMAPHORE)
    out = pl.pallas_call(
        body, name=name,
        in_specs=[hbm] * n + [pl.BlockSpec(memory_space=pl.ANY)] * na,
        out_specs=(sem, sem, *[hbm] * n, pl.BlockSpec(memory_space=pltpu.VMEM)),
        out_shape=(pltpu.SemaphoreType.DMA((nr,)), pltpu.SemaphoreType.DMA((nr,)),
                   *[pltpu.HBM(b.shape, b.dtype) for b in bufs], jax.ShapeDtypeStruct((SUBLANES, LANES), F32)),
        input_output_aliases={i: 2 + i for i in range(n)},
        compiler_params=pltpu.CompilerParams(has_side_effects=SIDE_EFFECT),
    )(*[pltpu.with_memory_space_constraint(b, pltpu.HBM) for b in bufs], *([after] * na))
    return out[0], out[1], list(out[2:2 + n]), out[2 + n]


def _exchange_wait(name, send_sems, recv_sems, bufs, copies, after):
    n = len(bufs)

    def body(*refs):
        for cp in _descriptors(copies, refs[:n], refs[n], refs[n + 1]):
            cp.wait_send()
            cp.wait_recv()

    hbm = pl.BlockSpec(memory_space=pltpu.HBM)
    sem = pl.BlockSpec(memory_space=pltpu.SEMAPHORE)
    out = pl.pallas_call(
        body, name=name,
        in_specs=[hbm] * n + [sem, sem, pl.BlockSpec(memory_space=pl.ANY)],
        out_specs=tuple([hbm] * n),
        out_shape=tuple(pltpu.HBM(b.shape, b.dtype) for b in bufs),
        input_output_aliases={i: i for i in range(n)},
        compiler_params=pltpu.CompilerParams(has_side_effects=SIDE_EFFECT),
    )(*bufs, send_sems, recv_sems, after)
    return list(out)


FIRST = ("w_in",)
MID = ("ssm_w_glu", "w_out")
LATE = ("w_up", "w_down")
GROUPS = {"first": FIRST, "mid": MID, "late": LATE}


def _gather_copies(names, shard_shapes):
    def region(i, chip, c):
        half_axis, shard_axis = BIG[names[i]]
        ssize = shard_shapes[i][shard_axis]
        hsize = shard_shapes[i][half_axis] // 2
        return lambda ref: _view(_view(ref, shard_axis, chip * ssize, ssize), half_axis, c * hsize, hsize)

    ici, d2d = [], []
    for i in range(len(names)):
        for flip in FLIPS:
            ici.append((lambda I, O, pos, i=i: region(i, pos[3], pos[2])(I[i]),
                        lambda I, O, pos, i=i: region(i, pos[3], pos[2])(O[i]), flip))
            d2d.append((lambda I, O, pos, i=i, flip=flip: region(i, _peer_chip(pos, flip), pos[2])(I[i]),
                        lambda I, O, pos, i=i, flip=flip: region(i, _peer_chip(pos, flip), pos[2])(O[i]), "c"))
    return ici, d2d


def _half_shape(n, shape):
    r, cdim = shape
    return (r // 2, cdim) if BIG[n][0] == 0 else (r, cdim // 2)


def _sub_shape(n, shape):
    hr, hc = _half_shape(n, shape)
    return (hr, hc // 4) if BIG[n][1] == 1 else (hr // 4, hc)


def _pair_copies(names, shapes, with_pack, dst_off):
    n = len(names)

    def other_half(i, ref, pos):
        half_axis = BIG[names[i]][0]
        hsize = shapes[i][half_axis] // 2
        return _view(ref, half_axis, (1 - pos[2]) * hsize, hsize)

    copies = [(lambda I, O, pos, i=i: other_half(i, I[i], pos), lambda I, O, pos, i=i: O[dst_off + i], "c")
              for i in range(n)]
    if with_pack:
        copies.append((lambda I, O, pos: I[n], lambda I, O, pos: O[dst_off + n], "c"))
    return copies


def _chip_copies(names, shapes, pack_rows, dst_off):
    n = len(names)

    def piece(i, ref, chip):
        shard_axis = BIG[names[i]][1]
        ssize = _sub_shape(names[i], shapes[i])[shard_axis]
        return _view(ref, shard_axis, chip * ssize, ssize)

    copies = []
    for i in range(n):
        for slot, flip in enumerate(FLIPS):
            copies.append((lambda I, O, pos, i=i, flip=flip: piece(i, I[i], _peer_chip(pos, flip)),
                           lambda I, O, pos, i=i, slot=slot: O[dst_off + i].at[slot], flip))
    if pack_rows:
        for slot, flip in enumerate(FLIPS):
            copies.append((lambda I, O, pos: _view(I[n], 0, pos[2] * (pack_rows // 2), pack_rows // 2),
                           lambda I, O, pos, slot=slot: O[dst_off + n].at[slot], flip))
    return copies


class _Exchanges:
    def __init__(self, shards, tiny, kc):
        self.kc = kc
        wb = {n: _cast_into_full(shards[n], kc, BIG[n][1], "cast_" + n) for n in BIG_NAMES}
        nf = len(FIRST)
        ici, d2d = _gather_copies(FIRST, [shards[n].shape for n in FIRST])
        local = [(lambda I, O, pos: I[nf], lambda I, O, pos: O[nf].at[pos[3]])]
        ici += [(lambda I, O, pos: I[nf], lambda I, O, pos: O[nf].at[pos[3]], flip) for flip in FLIPS]
        outs = ([jax.ShapeDtypeStruct(wb[n].shape, BF16) for n in FIRST]
                + [jax.ShapeDtypeStruct((4,) + tiny.shape, F32)])
        got = _exchange("gather_ici", [wb[n] for n in FIRST] + [tiny], outs, {i: i for i in range(nf)}, local, ici)
        full = _exchange("gather_d2d", list(got[:nf]), outs[:nf], {i: i for i in range(nf)}, [], d2d)
        self.first = dict(zip(FIRST, full))
        self.tiny_all = got[nf]
        self.gathering, self.pairing, self.reducing = {}, {}, {}
        after = full[0]
        self.zero = 0.0
        for group in ("mid", "late"):
            names = GROUPS[group]
            copies = _gather_copies(names, [shards[n].shape for n in names])
            started = _exchange_start("gather_%s_start" % group, [wb[n] for n in names], copies[0], after)
            self.gathering[group] = (started, copies)
            after = started[2][0]
            self.zero = self.zero + started[3][0, 0]

    def weights(self, group, after):
        (send_sems, recv_sems, bufs, _), (ici, d2d) = self.gathering[group]
        got = _exchange_wait("gather_%s_wait" % group, send_sems, recv_sems, bufs, ici, after)
        outs = [jax.ShapeDtypeStruct(b.shape, BF16) for b in got]
        full = _exchange("gather_%s_d2d" % group, got, outs, {i: i for i in range(len(got))}, [], d2d)
        return dict(zip(GROUPS[group], full))

    def grads_ready(self, group, grads):
        names = GROUPS[group]
        gs = [grads[n] for n in names]
        land = [lax.empty(_half_shape(n, g.shape), F32) for n, g in zip(names, gs)]
        copies = _pair_copies(names, [g.shape for g in gs], False, len(names))
        started = _exchange_start("pair_%s_start" % group, gs + land, copies)
        self.pairing[group] = (started, copies)
        return started[3]

    def grads_send(self, group, after):
        names = GROUPS[group]
        n = len(names)
        (send_sems, recv_sems, bufs, _), copies = self.pairing[group]
        bufs = _exchange_wait("pair_%s_wait" % group, send_sems, recv_sems, bufs, copies, after)
        chip = [_pair_sum(bufs[i], bufs[n + i], self.kc, BIG[names[i]][0], "pair_sum_" + names[i], BF16)
                for i in range(n)]
        shapes = [bufs[i].shape for i in range(n)]
        land = [lax.empty((3,) + _sub_shape(names[i], shapes[i]), BF16) for i in range(n)]
        copies = _chip_copies(names, shapes, 0, n)
        started = _exchange_start("reduce_%s_start" % group, chip + land, copies)
        self.reducing[group] = (started, copies)
        return started[3]

    def finish(self, pack, after):
        kc = self.kc
        names, chips, recvs = (), [], []
        for group, group_names in GROUPS.items():
            (send_sems, recv_sems, bufs, _), copies = self.reducing[group]
            bufs = _exchange_wait("reduce_%s_wait" % group, send_sems, recv_sems, bufs, copies, after)
            n = len(group_names)
            names, chips, recvs = names + group_names, chips + bufs[:n], recvs + bufs[n:]
            after = bufs[n]

        outs = [jax.ShapeDtypeStruct(pack.shape, F32)]
        recv = _exchange("reduce_d2d", [pack, after], outs, {}, [], _pair_copies((), [], True, 0))
        chip_pack = _pair_sum(pack, recv[0], kc, None, "pair_sum_pack", F32)
        outs = [jax.ShapeDtypeStruct((3, pack.shape[0] // 2, pack.shape[1]), F32)]
        recv = _exchange("reduce_ici", [chip_pack], outs, {}, [], _chip_copies((), [], pack.shape[0], 0))

        total = [_chip_sum(chips[i], recvs[i], kc, BIG[n][1], BIG[n][0], "chip_sum_" + n)
                 for i, n in enumerate(names)]
        total.append(_chip_sum(chip_pack, recv[0], kc, None, 0, "chip_sum_pack"))

        def my_half(i, ref, pos):
            half_axis = BIG[names[i]][0] if i < len(names) else 0
            hsize = ref.shape[half_axis] // 2
            return _view(ref, half_axis, pos[2] * hsize, hsize)

        remote = [(lambda I, O, pos, i=i: my_half(i, I[i], pos), lambda I, O, pos, i=i: my_half(i, O[i], pos), "c")
                  for i in range(len(total))]
        outs = [jax.ShapeDtypeStruct(t.shape, F32) for t in total]
        out = _exchange("swap_d2d", total, outs, {i: i for i in range(len(total))}, [], remote)
        return dict(zip(names, out[:len(names)])), out[len(names)]


WEIGHTS = ("meta_tokens", "norm_mix_g", "w_in", "conv_w", "ssm_lam_re", "ssm_lam_im", "ssm_log_dt", "ssm_b_re",
           "ssm_b_im", "ssm_c_re", "ssm_c_im", "ssm_d", "ssm_w_glu", "gain_conv_out", "gain_ssm_out", "w_out",
           "norm_ffn_g", "w_up", "ffn_conv_w", "ffn_conv_b", "w_down", "norm_final_g")
TINY_SHARDED = ("meta_tokens", "conv_w", "ffn_conv_w")
REPLICATED = tuple(n for n in WEIGHTS if n not in BIG and n not in TINY_SHARDED)
PACK_COLS = 512


def _pack(arrays, row_mult, cols):
    flat = jnp.concatenate([a.reshape(-1).astype(F32) for a in arrays])
    n = flat.shape[0]
    total = -(-n // (row_mult * cols)) * (row_mult * cols)
    return jnp.concatenate([flat, jnp.zeros((total - n,), F32)]).reshape(total // cols, cols)


def _unpack(packed, shapes):
    flat = packed.reshape(-1)
    out, off = [], 0
    for s in shapes:
        n = math.prod(s)
        out.append(flat[off:off + n].reshape(s))
        off += n
    return out


def kernel(x, meta_tokens, norm_mix_g, w_in, conv_w, ssm_lam_re, ssm_lam_im, ssm_log_dt, ssm_b_re, ssm_b_im, ssm_c_re, ssm_c_im, ssm_d, ssm_w_glu, gain_conv_out, gain_ssm_out, w_out, norm_ffn_g, w_up, ffn_conv_w, ffn_conv_b, w_down, norm_final_g, loss_target, m_meta_tokens, m_norm_mix_g, m_w_in, m_conv_w, m_ssm_lam_re, m_ssm_lam_im, m_ssm_log_dt, m_ssm_b_re, m_ssm_b_im, m_ssm_c_re, m_ssm_c_im, m_ssm_d, m_ssm_w_glu, m_gain_conv_out, m_gain_ssm_out, m_w_out, m_norm_ffn_g, m_w_up, m_ffn_conv_w, m_ffn_conv_b, m_w_down, m_norm_final_g, v_meta_tokens, v_norm_mix_g, v_w_in, v_conv_w, v_ssm_lam_re, v_ssm_lam_im, v_ssm_log_dt, v_ssm_b_re, v_ssm_b_im, v_ssm_c_re, v_ssm_c_im, v_ssm_d, v_ssm_w_glu, v_gain_conv_out, v_gain_ssm_out, v_w_out, v_norm_ffn_g, v_w_up, v_ffn_conv_w, v_ffn_conv_b, v_w_down, v_norm_final_g):
    args = dict(locals())
    w = {n: args[n] for n in WEIGHTS}
    mom = {n: args["m_" + n] for n in WEIGHTS}
    var = {n: args["v_" + n] for n in WEIGHTS}
    kx, ky, kc_ = lax.axis_index("x"), lax.axis_index("y"), lax.axis_index("c")
    chip = 2 * kx + ky
    kc = jnp.stack([chip, kc_]).astype(jnp.int32)

    def squeeze(n, a):
        if n == "meta_tokens":
            return a
        if n == "norm_final_g":
            return a.reshape(1, -1)
        a = a[0]
        return a.reshape(1, -1) if a.ndim == 1 else a

    wl = {n: squeeze(n, w[n]) for n in WEIGHTS}
    ml = {n: squeeze(n, mom[n]) for n in WEIGHTS}
    vl = {n: squeeze(n, var[n]) for n in WEIGHTS}

    tiny = _pack([wl[n] for n in TINY_SHARDED], SUBLANES, LANES)
    ex = _Exchanges({n: wl[n] for n in BIG_NAMES}, tiny, kc)
    tiny_shapes = [wl[n].shape for n in TINY_SHARDED]
    tiny_parts = [_unpack(ex.tiny_all[k], tiny_shapes) for k in range(4)]
    p = {n: wl[n] for n in WEIGHTS if n not in BIG}
    p.update(ex.first)
    for j, n in enumerate(TINY_SHARDED):
        p[n] = jnp.concatenate([tiny_parts[k][j] for k in range(4)], axis=1)
    p["ssm_log_dt"] = wl["ssm_log_dt"].reshape(-1)

    loss_local, grad_x, grads = _local_step(x[0], loss_target[0], p, ex)

    small_names = REPLICATED + TINY_SHARDED
    small_shapes = [tuple(grads[n].shape) for n in small_names] + [(1,)]
    pack = _pack([grads[n] for n in small_names] + [loss_local.reshape(1)], 2 * 16, PACK_COLS)
    g_big, g_pack = ex.finish(pack, pack)
    g_small = dict(zip(small_names + ("loss",), _unpack(g_pack, small_shapes)))
    loss = g_small["loss"][0]
    g = dict(g_big)
    for n in REPLICATED:
        g[n] = g_small[n].reshape(w[n].shape)
    for n in TINY_SHARDED:
        cols = wl[n].shape[1]
        g[n] = lax.dynamic_slice_in_dim(g_small[n], chip * cols, cols, axis=1).reshape(w[n].shape)

    delta, new_m, new_v = {}, {}, {}
    for n in BIG_NAMES:
        delta[n], new_m[n], new_v[n] = _adamw(wl[n], g[n], ml[n], vl[n], "adamw_" + n)
    rank2 = lambda a: a.reshape(1, -1) if a.ndim == 1 else a
    small = [[rank2(d[n]) for n in small_names] for d in (w, g, mom, var)]
    for d, outs in zip((delta, new_m, new_v), _adamw_whole(*small, "adamw_small")):
        d.update(zip(small_names, outs))

    def like(n, a):
        return a.reshape(w[n].shape)

    return (loss, grad_x[None], *[like(n, g[n]) for n in WEIGHTS], *[like(n, delta[n]) for n in WEIGHTS],
            *[like(n, new_m[n]) for n in WEIGHTS], *[like(n, new_v[n]) for n in WEIGHTS])
```

```python
import functools
import math

import jax
import jax.numpy as jnp
from jax import lax
from jax.experimental import pallas as pl
from jax.experimental.pallas import tpu as pltpu

F32 = jnp.float32
BF16 = jnp.bfloat16
MESH = pl.DeviceIdType.MESH

N_META = 16
N_GROUPS = 32
GROUP = 16
STATE = 64
RMS_EPS = 1e-6
ADAM_LR = 0.001
ADAM_B1 = 0.9
ADAM_B2 = 0.999
ADAM_EPS = 1e-08
ADAM_WD = 0.01
ADAM_STEP = 10

LANES = 128
SUBLANES = 8
ROW_ALIGN = 128
ROW_TILES = 4
VMEM_LIMIT = 52 * 1024 * 1024
GELU_C = math.sqrt(2.0 / math.pi)
GELU_A = 0.044715


def _cparams(*sem):
    return pltpu.CompilerParams(dimension_semantics=sem, vmem_limit_bytes=VMEM_LIMIT)


def _pick_tile(dim, cap, mult):
    best = None
    for t in range(mult, min(dim, cap) + 1, mult):
        if dim % t == 0:
            best = t
    return best if best is not None else dim


def _mm(a, b, mode, name, out_dtype=F32, acc_in=None, after=None):
    if mode == "tn":
        kdim, m = a.shape
    else:
        m, kdim = a.shape
    n = b.shape[0] if mode == "nt" else b.shape[1]
    tm = _pick_tile(m, 1408, LANES if mode == "tn" else 16)
    tn = _pick_tile(n, 512, LANES)
    tk = _pick_tile(kdim, 2816, LANES)
    nk = kdim // tk
    has_acc = acc_in is not None

    def body(*refs):
        if after is not None:
            refs = refs[1:]
        if has_acc:
            a_ref, b_ref, c_ref, o_ref = refs[:4]
            rest = refs[4:]
        else:
            a_ref, b_ref, o_ref = refs[:3]
            c_ref = None
            rest = refs[3:]
        if mode == "nn":
            p = jnp.dot(a_ref[...], b_ref[...], preferred_element_type=F32)
        elif mode == "nt":
            p = lax.dot_general(a_ref[...], b_ref[...], (((1,), (1,)), ((), ())), preferred_element_type=F32)
        else:
            p = lax.dot_general(a_ref[...], b_ref[...], (((0,), (0,)), ((), ())), preferred_element_type=F32)
        if nk == 1:
            if has_acc:
                p = p + c_ref[...]
            o_ref[...] = p.astype(out_dtype)
        else:
            acc_ref = rest[0]
            k = pl.program_id(2)

            @pl.when(k == 0)
            def _():
                acc_ref[...] = p + c_ref[...] if has_acc else p

            @pl.when(k > 0)
            def _():
                acc_ref[...] += p

            @pl.when(k == nk - 1)
            def _():
                o_ref[...] = acc_ref[...].astype(out_dtype)

    if mode == "tn":
        a_spec = pl.BlockSpec((tk, tm), lambda i, j, k: (k, i))
    else:
        a_spec = pl.BlockSpec((tm, tk), lambda i, j, k: (i, k))
    if mode == "nt":
        b_spec = pl.BlockSpec((tn, tk), lambda i, j, k: (j, k))
    else:
        b_spec = pl.BlockSpec((tk, tn), lambda i, j, k: (k, j))
    o_spec = pl.BlockSpec((tm, tn), lambda i, j, k: (i, j))
    in_specs = [a_spec, b_spec] + ([o_spec] if has_acc else [])
    args = (a, b) + ((acc_in,) if has_acc else ())
    if after is not None:
        in_specs = [pl.BlockSpec(memory_space=pl.ANY)] + in_specs
        args = (after,) + args
    return pl.pallas_call(
        body, name=name, grid=(m // tm, n // tn, nk),
        in_specs=in_specs, out_specs=o_spec,
        out_shape=jax.ShapeDtypeStruct((m, n), out_dtype),
        scratch_shapes=[pltpu.VMEM((tm, tn), F32)] if nk > 1 else [],
        compiler_params=_cparams("parallel", "parallel", "arbitrary"),
    )(*args)


def _rows(shape_cols, tr, dtype=None):
    return pl.BlockSpec((tr, shape_cols), lambda i: (i, 0))


def _const(shape):
    return pl.BlockSpec(shape, lambda i: (0,) * len(shape))


def _rms(x):
    return lax.rsqrt(jnp.mean(x * x, axis=-1, keepdims=True) + RMS_EPS)


def _rms_bwd(x, r, g, dy):
    xn = x * r
    dxn = dy * g
    dx = r * (dxn - xn * jnp.mean(dxn * xn, axis=-1, keepdims=True))
    return dx, dy * xn


def _gelu(y):
    return 0.5 * y * (1.0 + jnp.tanh(GELU_C * (y + GELU_A * y * y * y)))


def _gelu_grad(y):
    t = jnp.tanh(GELU_C * (y + GELU_A * y * y * y))
    return 0.5 * (1.0 + t) + 0.5 * y * (1.0 - t * t) * GELU_C * (1.0 + 3.0 * GELU_A * y * y)


def _sigmoid(z):
    return 1.0 / (1.0 + jnp.exp(-z))


def _norm_fwd(h, g, name, res=None):
    tp, d = h.shape
    tr = tp // ROW_TILES
    has_res = res is not None

    def body(*refs):
        if has_res:
            h_ref, r_ref, g_ref, s_ref, hn_ref = refs
            x = h_ref[...] + r_ref[...]
            s_ref[...] = x
        else:
            h_ref, g_ref, hn_ref = refs
            x = h_ref[...]
        hn_ref[...] = (x * _rms(x) * g_ref[...]).astype(BF16)

    in_specs = [_rows(d, tr)] + ([_rows(d, tr)] if has_res else []) + [_const((1, d))]
    out_specs = ([_rows(d, tr)] if has_res else []) + [_rows(d, tr)]
    out_shape = ([jax.ShapeDtypeStruct((tp, d), F32)] if has_res else []) + [jax.ShapeDtypeStruct((tp, d), BF16)]
    args = (h,) + ((res,) if has_res else ()) + (g,)
    out = pl.pallas_call(body, name=name, grid=(ROW_TILES,), in_specs=in_specs, out_specs=out_specs,
                         out_shape=out_shape, compiler_params=_cparams("parallel"))(*args)
    return out if has_res else out[0]


def _norm_bwd(h, g, dhn, dres, name):
    tp, d = h.shape
    tr = tp // ROW_TILES

    def body(h_ref, g_ref, dhn_ref, dres_ref, dh_ref, dhb_ref, dg_ref):
        x = h_ref[...]
        dx, dgs = _rms_bwd(x, _rms(x), g_ref[...], dhn_ref[...])
        dh = dres_ref[...] + dx
        dh_ref[...] = dh
        dhb_ref[...] = dh.astype(BF16)

        @pl.when(pl.program_id(0) == 0)
        def _():
            dg_ref[...] = jnp.zeros_like(dg_ref)

        dg_ref[...] += jnp.sum(dgs, axis=0, keepdims=True)

    return pl.pallas_call(
        body, name=name, grid=(ROW_TILES,),
        in_specs=[_rows(d, tr), _const((1, d)), _rows(d, tr), _rows(d, tr)],
        out_specs=[_rows(d, tr), _rows(d, tr), _const((1, d))],
        out_shape=[jax.ShapeDtypeStruct((tp, d), F32), jax.ShapeDtypeStruct((tp, d), BF16),
                   jax.ShapeDtypeStruct((1, d), F32)],
        compiler_params=_cparams("arbitrary"))(h, g, dhn, dres)


def _input_norm_bwd(h, g, dhn, dres, n_real, name):
    tp, d = h.shape
    tr = tp // ROW_TILES

    def body(h_ref, g_ref, dhn_ref, dres_ref, dx_ref, dmeta_ref, dg_ref, stage, sem):
        i = pl.program_id(0)
        x = h_ref[...]
        dx, dgs = _rms_bwd(x, _rms(x), g_ref[...], dhn_ref[...])
        stage[...] = dres_ref[...] + dx

        @pl.when(i == 0)
        def _():
            dg_ref[...] = jnp.zeros_like(dg_ref)
            dmeta_ref[...] = stage[:N_META, :]

        dg_ref[...] += jnp.sum(dgs, axis=0, keepdims=True)
        for t in range(ROW_TILES):
            lo, hi = max(t * tr, N_META), min((t + 1) * tr, n_real)
            if hi > lo:
                @pl.when(i == t)
                def _(t=t, lo=lo, hi=hi):
                    cp = pltpu.make_async_copy(stage.at[pl.ds(lo - t * tr, hi - lo), :],
                                               dx_ref.at[pl.ds(lo - N_META, hi - lo), :], sem)
                    cp.start()
                    cp.wait()

    return pl.pallas_call(
        body, name=name, grid=(ROW_TILES,),
        in_specs=[_rows(d, tr), _const((1, d)), _rows(d, tr), _rows(d, tr)],
        out_specs=[pl.BlockSpec(memory_space=pl.ANY), _const((N_META, d)), _const((1, d))],
        out_shape=[jax.ShapeDtypeStruct((n_real - N_META, d), F32), jax.ShapeDtypeStruct((N_META, d), F32),
                   jax.ShapeDtypeStruct((1, d), F32)],
        scratch_shapes=[pltpu.VMEM((tr, d), F32), pltpu.SemaphoreType.DMA],
        compiler_params=_cparams("arbitrary"))(h, g, dhn, dres)


def _load_token_rows(tok_hbm, buf, sem, tr, n_real, head=None):
    i = pl.program_id(0)
    for t in range(ROW_TILES):
        base = t * tr
        lo, hi = max(base, N_META), min(base + tr, n_real)

        @pl.when(i == t)
        def _(base=base, lo=lo, hi=hi):
            if base < N_META:
                buf[0:N_META - base, :] = (jnp.zeros((N_META - base, buf.shape[1]), F32) if head is None
                                           else head[base:N_META, :])
            if hi < base + tr:
                buf[max(hi, base) - base:tr, :] = jnp.zeros((base + tr - max(hi, base), buf.shape[1]), F32)
            if hi > lo:
                cp = pltpu.make_async_copy(tok_hbm.at[pl.ds(lo - N_META, hi - lo), :],
                                           buf.at[pl.ds(lo - base, hi - lo), :], sem)
                cp.start()
                cp.wait()


def _input_norm_fwd(x, meta, g, tp, name):
    seq, d = x.shape
    tr = tp // ROW_TILES
    n_real = N_META + seq

    def body(x_hbm, meta_ref, g_ref, h_ref, hn_ref, buf, sem):
        _load_token_rows(x_hbm, buf, sem, tr, n_real, head=meta_ref)
        h = buf[...]
        h_ref[...] = h
        hn_ref[...] = (h * _rms(h) * g_ref[...]).astype(BF16)

    return pl.pallas_call(
        body, name=name, grid=(ROW_TILES,),
        in_specs=[pl.BlockSpec(memory_space=pl.ANY), _const((N_META, d)), _const((1, d))],
        out_specs=[_rows(d, tr), _rows(d, tr)],
        out_shape=[jax.ShapeDtypeStruct((tp, d), F32), jax.ShapeDtypeStruct((tp, d), BF16)],
        scratch_shapes=[pltpu.VMEM((tr, d), F32), pltpu.SemaphoreType.DMA],
        compiler_params=_cparams("arbitrary"))(x, meta, g)


def _loss_bwd(h1, dn, target, g, n_real, name):
    tp, d = h1.shape
    tr = tp // ROW_TILES

    def body(h1_ref, dn_ref, t_hbm, g_ref, loss_ref, dh_ref, dhb_ref, dg_ref, t_buf, sem):
        i = pl.program_id(0)
        _load_token_rows(t_hbm, t_buf, sem, tr, n_real)
        x = h1_ref[...] + dn_ref[...]
        r = _rms(x)
        row = i * tr + lax.broadcasted_iota(jnp.int32, (tr, d), 0)
        valid = (row >= N_META) & (row < n_real)
        e = jnp.where(valid, x * r * g_ref[...] - t_buf[...], 0.0)
        dx, dgs = _rms_bwd(x, r, g_ref[...], e * (1.0 / d))
        dh_ref[...] = dx
        dhb_ref[...] = dx.astype(BF16)

        @pl.when(i == 0)
        def _():
            dg_ref[...] = jnp.zeros_like(dg_ref)
            loss_ref[...] = jnp.zeros_like(loss_ref)

        dg_ref[...] += jnp.sum(dgs, axis=0, keepdims=True)
        loss_ref[...] += (0.5 / d) * jnp.sum(jnp.sum(e * e, axis=0, keepdims=True), axis=1, keepdims=True)

    return pl.pallas_call(
        body, name=name, grid=(ROW_TILES,),
        in_specs=[_rows(d, tr), _rows(d, tr), pl.BlockSpec(memory_space=pl.ANY), _const((1, d))],
        out_specs=[_const((1, LANES)), _rows(d, tr), _rows(d, tr), _const((1, d))],
        out_shape=[jax.ShapeDtypeStruct((1, LANES), F32), jax.ShapeDtypeStruct((tp, d), F32),
                   jax.ShapeDtypeStruct((tp, d), BF16), jax.ShapeDtypeStruct((1, d), F32)],
        scratch_shapes=[pltpu.VMEM((tr, d), F32), pltpu.SemaphoreType.DMA],
        compiler_params=_cparams("arbitrary"))(h1, dn, target, g)


def _mix_fwd(co, y, z, gc, gs, name):
    tp, dh = co.shape
    tr = tp // ROW_TILES

    def body(co_ref, y_ref, z_ref, gc_ref, gs_ref, m_ref):
        c = co_ref[...]
        m_ref[:, :dh] = (c * _rms(c) * gc_ref[...]).astype(BF16)
        so = _gelu(y_ref[...]) * _sigmoid(z_ref[...])
        m_ref[:, dh:] = (so * _rms(so) * gs_ref[...]).astype(BF16)

    return pl.pallas_call(
        body, name=name, grid=(ROW_TILES,),
        in_specs=[_rows(dh, tr)] * 3 + [_const((1, dh))] * 2,
        out_specs=_rows(2 * dh, tr),
        out_shape=jax.ShapeDtypeStruct((tp, 2 * dh), BF16),
        compiler_params=_cparams("parallel"))(co, y, z, gc, gs)


def _mix_bwd(dm, co, y, z, gc, gs, name):
    tp, dh = co.shape
    tr = tp // ROW_TILES

    def body(dm_ref, co_ref, y_ref, z_ref, gc_ref, gs_ref, dco_ref, dz_ref, dgp_ref, dgc_ref, dgs_ref):
        c = co_ref[...]
        dco, dgc = _rms_bwd(c, _rms(c), gc_ref[...], dm_ref[:, :dh])
        dco_ref[...] = dco
        gl = _gelu(y_ref[...])
        sg = _sigmoid(z_ref[...])
        so = gl * sg
        dso, dgs = _rms_bwd(so, _rms(so), gs_ref[...], dm_ref[:, dh:])
        dz_ref[...] = (dso * gl * sg * (1.0 - sg)).astype(BF16)
        dgp_ref[...] = dso * sg

        @pl.when(pl.program_id(0) == 0)
        def _():
            dgc_ref[...] = jnp.zeros_like(dgc_ref)
            dgs_ref[...] = jnp.zeros_like(dgs_ref)

        dgc_ref[...] += jnp.sum(dgc, axis=0, keepdims=True)
        dgs_ref[...] += jnp.sum(dgs, axis=0, keepdims=True)

    return pl.pallas_call(
        body, name=name, grid=(ROW_TILES,),
        in_specs=[_rows(2 * dh, tr)] + [_rows(dh, tr)] * 3 + [_const((1, dh))] * 2,
        out_specs=[_rows(dh, tr), _rows(dh, tr), _rows(dh, tr), _const((1, dh)), _const((1, dh))],
        out_shape=[jax.ShapeDtypeStruct((tp, dh), F32), jax.ShapeDtypeStruct((tp, dh), BF16),
                   jax.ShapeDtypeStruct((tp, dh), F32), jax.ShapeDtypeStruct((1, dh), F32),
                   jax.ShapeDtypeStruct((1, dh), F32)],
        compiler_params=_cparams("arbitrary"))(dm, co, y, z, gc, gs)


def _shift_down(x, k):
    row = lax.broadcasted_iota(jnp.int32, x.shape, 0)
    return jnp.where(row >= k, pltpu.roll(x, k, 0), 0.0)


def _shift_up(x, k):
    n = x.shape[0]
    row = lax.broadcasted_iota(jnp.int32, x.shape, 0)
    return jnp.where(row < n - k, pltpu.roll(x, n - k, 0), 0.0)


def _dwconv(x, w_ref):
    return w_ref[2:3, :] * x + w_ref[1:2, :] * _shift_down(x, 1) + w_ref[0:1, :] * _shift_down(x, 2)


def _dwconv_bwd(x, dy, w_ref):
    dx = w_ref[2:3, :] * dy + w_ref[1:2, :] * _shift_up(dy, 1) + w_ref[0:1, :] * _shift_up(dy, 2)
    dw = jnp.concatenate([jnp.sum(dy * _shift_down(x, 2), axis=0, keepdims=True),
                          jnp.sum(dy * _shift_down(x, 1), axis=0, keepdims=True),
                          jnp.sum(dy * x, axis=0, keepdims=True)], axis=0)
    return dx, dw


def _scan(s_re, s_im, tab_ref, reverse):
    n_chunks = s_re.shape[0] // SUBLANES
    n_strips = s_re.shape[1] // LANES
    last = 0 if reverse else SUBLANES - 1

    def body(i, carry):
        chunk = (n_chunks - 1 - i) if reverse else i
        r0 = pl.multiple_of(chunk * SUBLANES, SUBLANES)
        out = []
        for st in range(n_strips):
            lanes = slice(st * LANES, (st + 1) * LANES)
            cr, ci = carry[2 * st], carry[2 * st + 1]
            xr = s_re[pl.ds(r0, SUBLANES), lanes]
            xi = s_im[pl.ds(r0, SUBLANES), lanes]
            for level, k in enumerate((1, 2, 4)):
                mr = tab_ref[2 * level, :, lanes]
                mi = tab_ref[2 * level + 1, :, lanes]
                sh = SUBLANES - k if reverse else k
                rr = pltpu.roll(xr, sh, 0)
                ri = pltpu.roll(xi, sh, 0)
                xr, xi = xr + (mr * rr - mi * ri), xi + (mr * ri + mi * rr)
            pwr = tab_ref[6, :, lanes]
            pwi = tab_ref[7, :, lanes]
            xr, xi = xr + (pwr * cr - pwi * ci), xi + (pwr * ci + pwi * cr)
            s_re[pl.ds(r0, SUBLANES), lanes] = xr
            s_im[pl.ds(r0, SUBLANES), lanes] = xi
            out.append(jnp.broadcast_to(xr[last:last + 1, :], (SUBLANES, LANES)))
            out.append(jnp.broadcast_to(xi[last:last + 1, :], (SUBLANES, LANES)))
        return tuple(out)

    zero = jnp.zeros((SUBLANES, LANES), F32)
    lax.fori_loop(0, n_chunks, body, (zero,) * (2 * n_strips))


def _seq_fwd(proj, conv_w, bc_re, bc_im, cc_re, cc_im, dskip, tab_f, name):
    tp = proj.shape[0]
    dh = proj.shape[1] // 4
    nq = dh // LANES
    sw = STATE * N_GROUPS // nq

    def body(b_ref, c_ref, v_ref, u_ref, w_ref, bre_ref, bim_ref, cre_ref, cim_ref, d_ref, tab_ref,
             co_ref, y_ref, g_ref, s_re, s_im):
        co_ref[...] = b_ref[...] * _dwconv(c_ref[...] * v_ref[...], w_ref)
        u = u_ref[...]
        ub = u.astype(BF16)
        s_re[...] = jnp.dot(ub, bre_ref[...], preferred_element_type=F32)
        s_im[...] = jnp.dot(ub, bim_ref[...], preferred_element_type=F32)
        _scan(s_re, s_im, tab_ref, False)
        y = (jnp.dot(s_re[...].astype(BF16), cre_ref[...], preferred_element_type=F32)
             - jnp.dot(s_im[...].astype(BF16), cim_ref[...], preferred_element_type=F32)
             + d_ref[...] * u)
        y_ref[...] = y
        g_ref[...] = _gelu(y).astype(BF16)

    col = lambda off: pl.BlockSpec((tp, LANES), lambda q, off=off: (0, off * nq + q))
    blk = pl.BlockSpec((tp, LANES), lambda q: (0, q))
    return pl.pallas_call(
        body, name=name, grid=(nq,),
        in_specs=[col(0), col(1), col(2), col(3),
                  pl.BlockSpec((3, LANES), lambda q: (0, q)),
                  pl.BlockSpec((LANES, sw), lambda q: (0, q)), pl.BlockSpec((LANES, sw), lambda q: (0, q)),
                  pl.BlockSpec((sw, LANES), lambda q: (q, 0)), pl.BlockSpec((sw, LANES), lambda q: (q, 0)),
                  pl.BlockSpec((1, LANES), lambda q: (0, q)),
                  pl.BlockSpec((8, SUBLANES, sw), lambda q: (0, 0, q))],
        out_specs=[blk, blk, blk],
        out_shape=[jax.ShapeDtypeStruct((tp, dh), F32), jax.ShapeDtypeStruct((tp, dh), F32),
                   jax.ShapeDtypeStruct((tp, dh), BF16)],
        scratch_shapes=[pltpu.VMEM((tp, sw), F32), pltpu.VMEM((tp, sw), F32)],
        compiler_params=_cparams("parallel"),
    )(proj, proj, proj, proj, conv_w, bc_re, bc_im, cc_re, cc_im, dskip, tab_f)


def _conv_bwd(proj, dco, conv_w, name):
    tp = proj.shape[0]
    dh = proj.shape[1] // 4
    nq = dh // LANES

    def body(b_ref, c_ref, v_ref, dco_ref, w_ref, dproj_ref, dw_ref, stage, sem):
        q = pl.program_id(0)
        cg = c_ref[...]
        vg = v_ref[...]
        cv = cg * vg
        dco_v = dco_ref[...]
        dcv, dw = _dwconv_bwd(cv, dco_v * b_ref[...], w_ref)
        dw_ref[...] = dw
        stage[0] = (dco_v * _dwconv(cv, w_ref)).astype(BF16)
        stage[1] = (dcv * vg).astype(BF16)
        stage[2] = (dcv * cg).astype(BF16)
        copies = [pltpu.make_async_copy(stage.at[p], dproj_ref.at[:, pl.ds((p * nq + q) * LANES, LANES)], sem.at[p])
                  for p in range(3)]
        for cp in copies:
            cp.start()
        for cp in copies:
            cp.wait()

    col = lambda off: pl.BlockSpec((tp, LANES), lambda q, off=off: (0, off * nq + q))
    return pl.pallas_call(
        body, name=name, grid=(nq,),
        in_specs=[col(0), col(1), col(2), pl.BlockSpec((tp, LANES), lambda q: (0, q)),
                  pl.BlockSpec((3, LANES), lambda q: (0, q))],
        out_specs=[pl.BlockSpec(memory_space=pl.ANY), pl.BlockSpec((3, LANES), lambda q: (0, q))],
        out_shape=[jax.ShapeDtypeStruct((tp, 4 * dh), BF16), jax.ShapeDtypeStruct((3, dh), F32)],
        scratch_shapes=[pltpu.VMEM((3, tp, LANES), BF16), pltpu.SemaphoreType.DMA((3,))],
        compiler_params=_cparams("arbitrary"),
    )(proj, proj, proj, dco, conv_w)


def _ssm_bwd(proj, y, dg, dproj, bc_re, bc_im, cc_re, cc_im, dskip, tab_f, tab_r, name):
    tp = proj.shape[0]
    dh = proj.shape[1] // 4
    nq = dh // LANES
    sw = STATE * N_GROUPS // nq

    def body(u_ref, y_ref, dg_ref, dproj_in, bre_ref, bim_ref, cre_ref, cim_ref, d_ref, tabf_ref, tabr_ref,
             dproj_ref, dbre_ref, dbim_ref, dcre_ref, dcim_ref, dd_ref, dar_ref, dai_ref,
             s_re, s_im, l_re, l_im, stage, sem):
        del dproj_in
        q = pl.program_id(0)
        nt = (((1,), (1,)), ((), ()))
        tn = (((0,), (0,)), ((), ()))
        u = u_ref[...]
        ub = u.astype(BF16)
        s_re[...] = jnp.dot(ub, bre_ref[...], preferred_element_type=F32)
        s_im[...] = jnp.dot(ub, bim_ref[...], preferred_element_type=F32)
        _scan(s_re, s_im, tabf_ref, False)
        dy = dg_ref[...] * _gelu_grad(y_ref[...])
        dyb = dy.astype(BF16)
        dd_ref[...] = jnp.sum(dy * u, axis=0, keepdims=True)
        l_re[...] = lax.dot_general(dyb, cre_ref[...], nt, preferred_element_type=F32)
        l_im[...] = -lax.dot_general(dyb, cim_ref[...], nt, preferred_element_type=F32)
        dcre_ref[...] = lax.dot_general(s_re[...].astype(BF16), dyb, tn, preferred_element_type=F32)
        dcim_ref[...] = -lax.dot_general(s_im[...].astype(BF16), dyb, tn, preferred_element_type=F32)
        _scan(l_re, l_im, tabr_ref, True)
        for st in range(sw // LANES):
            lanes = slice(st * LANES, (st + 1) * LANES)
            lr = l_re[:, lanes]
            li = l_im[:, lanes]
            pr = _shift_down(s_re[:, lanes], 1)
            pi = _shift_down(s_im[:, lanes], 1)
            dar_ref[:, lanes] = jnp.sum(lr * pr + li * pi, axis=0, keepdims=True)
            dai_ref[:, lanes] = jnp.sum(li * pr - lr * pi, axis=0, keepdims=True)
        lrb = l_re[...].astype(BF16)
        lib = l_im[...].astype(BF16)
        du = (dy * d_ref[...] + lax.dot_general(lrb, bre_ref[...], nt, preferred_element_type=F32)
              + lax.dot_general(lib, bim_ref[...], nt, preferred_element_type=F32))
        stage[...] = du.astype(BF16)
        dbre_ref[...] = lax.dot_general(ub, lrb, tn, preferred_element_type=F32)
        dbim_ref[...] = lax.dot_general(ub, lib, tn, preferred_element_type=F32)
        cp = pltpu.make_async_copy(stage, dproj_ref.at[:, pl.ds((3 * nq + q) * LANES, LANES)], sem)
        cp.start()
        cp.wait()

    blk = pl.BlockSpec((tp, LANES), lambda q: (0, q))
    bspec = pl.BlockSpec((LANES, sw), lambda q: (0, q))
    cspec = pl.BlockSpec((sw, LANES), lambda q: (q, 0))
    tspec = pl.BlockSpec((8, SUBLANES, sw), lambda q: (0, 0, q))
    nstate = STATE * N_GROUPS
    return pl.pallas_call(
        body, name=name, grid=(nq,),
        in_specs=[pl.BlockSpec((tp, LANES), lambda q: (0, 3 * nq + q)), blk, blk, pl.BlockSpec(memory_space=pl.ANY),
                  bspec, bspec, cspec, cspec, pl.BlockSpec((1, LANES), lambda q: (0, q)), tspec, tspec],
        out_specs=[pl.BlockSpec(memory_space=pl.ANY), bspec, bspec, cspec, cspec,
                   pl.BlockSpec((1, LANES), lambda q: (0, q)),
                   pl.BlockSpec((1, sw), lambda q: (0, q)), pl.BlockSpec((1, sw), lambda q: (0, q))],
        out_shape=[jax.ShapeDtypeStruct((tp, 4 * dh), BF16),
                   jax.ShapeDtypeStruct((LANES, nstate), F32), jax.ShapeDtypeStruct((LANES, nstate), F32),
                   jax.ShapeDtypeStruct((nstate, LANES), F32), jax.ShapeDtypeStruct((nstate, LANES), F32),
                   jax.ShapeDtypeStruct((1, dh), F32),
                   jax.ShapeDtypeStruct((1, nstate), F32), jax.ShapeDtypeStruct((1, nstate), F32)],
        input_output_aliases={3: 0},
        scratch_shapes=[pltpu.VMEM((tp, sw), F32)] * 4 + [pltpu.VMEM((tp, LANES), BF16), pltpu.SemaphoreType.DMA],
        compiler_params=_cparams("arbitrary"),
    )(proj, y, dg, dproj, bc_re, bc_im, cc_re, cc_im, dskip, tab_f, tab_r)


FFN_TILE = 256


def _ffn_act(up, fw, fb, name):
    tp, two_ff = up.shape
    dff = two_ff // 2
    tc = FFN_TILE
    nj = dff // tc

    def body(ua_ref, uv_ref, wa_ref, wv_ref, ba_ref, bv_ref, act_ref):
        a = _dwconv(ua_ref[...], wa_ref) + ba_ref[...]
        v = _dwconv(uv_ref[...], wv_ref) + bv_ref[...]
        act_ref[...] = (a * _sigmoid(a) * v).astype(BF16)

    lo = lambda r: pl.BlockSpec((r, tc), lambda j: (0, j))
    hi = lambda r: pl.BlockSpec((r, tc), lambda j: (0, nj + j))
    return pl.pallas_call(
        body, name=name, grid=(nj,),
        in_specs=[lo(tp), hi(tp), lo(3), hi(3), lo(1), hi(1)],
        out_specs=lo(tp),
        out_shape=jax.ShapeDtypeStruct((tp, dff), BF16),
        compiler_params=_cparams("parallel"))(up, up, fw, fw, fb, fb)


def _ffn_bwd(up, dact, fw, fb, name):
    tp, two_ff = up.shape
    dff = two_ff // 2
    tc = FFN_TILE
    nj = dff // tc

    def body(ua_ref, uv_ref, da_ref, wa_ref, wv_ref, ba_ref, bv_ref,
             dup_ref, dwa_ref, dwv_ref, dba_ref, dbv_ref, stage, sem):
        j = pl.program_id(0)
        ua = ua_ref[...]
        uv = uv_ref[...]
        a = _dwconv(ua, wa_ref) + ba_ref[...]
        v = _dwconv(uv, wv_ref) + bv_ref[...]
        sg = _sigmoid(a)
        dact_v = da_ref[...]
        da = dact_v * v * sg * (1.0 + a * (1.0 - sg))
        dv = dact_v * a * sg
        dba_ref[...] = jnp.sum(da, axis=0, keepdims=True)
        dbv_ref[...] = jnp.sum(dv, axis=0, keepdims=True)
        dua, dwa = _dwconv_bwd(ua, da, wa_ref)
        duv, dwv = _dwconv_bwd(uv, dv, wv_ref)
        dwa_ref[...] = dwa
        dwv_ref[...] = dwv
        stage[0] = dua.astype(BF16)
        stage[1] = duv.astype(BF16)
        copies = [pltpu.make_async_copy(stage.at[p], dup_ref.at[:, pl.ds((p * nj + j) * tc, tc)], sem.at[p])
                  for p in range(2)]
        for cp in copies:
            cp.start()
        for cp in copies:
            cp.wait()

    lo = lambda r: pl.BlockSpec((r, tc), lambda j: (0, j))
    hi = lambda r: pl.BlockSpec((r, tc), lambda j: (0, nj + j))
    return pl.pallas_call(
        body, name=name, grid=(nj,),
        in_specs=[lo(tp), hi(tp), lo(tp), lo(3), hi(3), lo(1), hi(1)],
        out_specs=[pl.BlockSpec(memory_space=pl.ANY), lo(3), lo(3), lo(1), lo(1)],
        out_shape=[jax.ShapeDtypeStruct((tp, two_ff), BF16),
                   jax.ShapeDtypeStruct((3, dff), F32), jax.ShapeDtypeStruct((3, dff), F32),
                   jax.ShapeDtypeStruct((1, dff), F32), jax.ShapeDtypeStruct((1, dff), F32)],
        scratch_shapes=[pltpu.VMEM((2, tp, tc), BF16), pltpu.SemaphoreType.DMA((2,))],
        compiler_params=_cparams("arbitrary"))(up, up, dact, fw, fw, fb, fb)


def _zoh(lr, li, ld):
    dt = jnp.exp(ld)
    mag = jnp.exp(lr * dt)
    ang = li * dt
    ar = mag * jnp.cos(ang)
    ai = mag * jnp.sin(ang)
    den = lr * lr + li * li
    nr = ar - 1.0
    fr = (nr * lr + ai * li) / den
    fi = (ai * lr - nr * li) / den
    return dt, ar, ai, den, nr, fr, fi


def _s5_prep(lr, li, ld, b_re, b_im, name):
    nstate = lr.shape[1]

    def tables(tab_ref, ar, ai, reverse):
        pows = [(ar, ai)]
        for _ in range(SUBLANES - 1):
            pr, pi = pows[-1]
            pows.append((pr * ar - pi * ai, pr * ai + pi * ar))
        row = lax.broadcasted_iota(jnp.int32, (SUBLANES, nstate), 0)
        for level, k in enumerate((1, 2, 4)):
            mask = (row <= SUBLANES - 1 - k) if reverse else (row >= k)
            tab_ref[2 * level] = jnp.where(mask, pows[k - 1][0], 0.0)
            tab_ref[2 * level + 1] = jnp.where(mask, pows[k - 1][1], 0.0)
        pr = jnp.zeros((SUBLANES, nstate), F32)
        pi = jnp.zeros((SUBLANES, nstate), F32)
        for t in range(SUBLANES):
            k = SUBLANES - 1 - t if reverse else t
            pr = jnp.where(row == t, pows[k][0], pr)
            pi = jnp.where(row == t, pows[k][1], pi)
        tab_ref[6] = pr
        tab_ref[7] = pi

    def body(lr_ref, li_ref, ld_ref, bre_ref, bim_ref, tabf_ref, tabr_ref, bcre_ref, bcim_ref):
        _, ar, ai, _, _, fr, fi = _zoh(lr_ref[...], li_ref[...], ld_ref[...])
        tables(tabf_ref, ar, ai, False)
        tables(tabr_ref, ar, -ai, True)
        bre = bre_ref[...]
        bim = bim_ref[...]
        bcre_ref[...] = (fr * bre - fi * bim).astype(BF16)
        bcim_ref[...] = (fr * bim + fi * bre).astype(BF16)

    vmem = pl.BlockSpec(memory_space=pltpu.VMEM)
    return pl.pallas_call(
        body, name=name, in_specs=[vmem] * 5, out_specs=[vmem] * 4,
        out_shape=[jax.ShapeDtypeStruct((8, SUBLANES, nstate), F32)] * 2
        + [jax.ShapeDtypeStruct(b_re.shape, BF16)] * 2)(lr, li, ld, b_re, b_im)


def _s5_prep_bwd(lr, li, ld, b_re, b_im, da_re, da_im, dbc_re, dbc_im, name):
    def body(lr_ref, li_ref, ld_ref, bre_ref, bim_ref, dar_ref, dai_ref, dbcre_ref, dbcim_ref,
             dlr_ref, dli_ref, dld_ref, dbre_ref, dbim_ref):
        lr, li = lr_ref[...], li_ref[...]
        dt, ar, ai, den, nr, fr, fi = _zoh(lr, li, ld_ref[...])
        bre, bim = bre_ref[...], bim_ref[...]
        gre, gim = dbcre_ref[...], dbcim_ref[...]
        dbre_ref[...] = fr * gre + fi * gim
        dbim_ref[...] = fr * gim - fi * gre
        g_fr = jnp.sum(gre * bre + gim * bim, axis=0, keepdims=True)
        g_fi = jnp.sum(gim * bre - gre * bim, axis=0, keepdims=True)
        g_ar = dar_ref[...] + (g_fr * lr - g_fi * li) / den
        g_ai = dai_ref[...] + (g_fr * li + g_fi * lr) / den
        d_lr = (g_fr * (nr - 2.0 * fr * lr) + g_fi * (ai - 2.0 * fi * lr)) / den
        d_li = (g_fr * (ai - 2.0 * fr * li) - g_fi * (nr + 2.0 * fi * li)) / den
        g_logmag = g_ar * ar + g_ai * ai
        g_ang = g_ai * ar - g_ar * ai
        dlr_ref[...] = d_lr + g_logmag * dt
        dli_ref[...] = d_li + g_ang * dt
        d_ld = (g_logmag * lr + g_ang * li) * dt
        n = d_ld.shape[1]
        sh = 1
        while sh < STATE:
            d_ld = d_ld + pltpu.roll(d_ld, n - sh, 1)
            sh *= 2
        dld_ref[...] = d_ld

    vmem = pl.BlockSpec(memory_space=pltpu.VMEM)
    row = jax.ShapeDtypeStruct(lr.shape, F32)
    return pl.pallas_call(
        body, name=name, in_specs=[vmem] * 9, out_specs=[vmem] * 5,
        out_shape=[row, row, row, jax.ShapeDtypeStruct(b_re.shape, F32), jax.ShapeDtypeStruct(b_re.shape, F32)],
    )(lr, li, ld, b_re, b_im, da_re, da_im, dbc_re, dbc_im)


def _compact_b(bb):
    bq = bb.reshape(N_GROUPS // 8, 8, STATE, GROUP)
    m = jnp.einsum("ab,qbph->qahbp", jnp.eye(8, dtype=bb.dtype), bq).reshape(N_GROUPS // 8, LANES, 8 * STATE)
    return m.transpose(1, 0, 2).reshape(LANES, N_GROUPS * STATE)


def _expand_b(m):
    d = m.reshape(8, GROUP, N_GROUPS // 8, 8, STATE)
    return jnp.einsum("ahqap->qaph", d).reshape(N_GROUPS, STATE, GROUP)


def _compact_c(c):
    cq = c.reshape(N_GROUPS // 8, 8, GROUP, STATE)
    return jnp.einsum("ab,qbhp->qbpah", jnp.eye(8, dtype=c.dtype), cq).reshape(N_GROUPS * STATE, LANES)


def _expand_c(m):
    d = m.reshape(N_GROUPS // 8, 8, STATE, 8, GROUP)
    return jnp.einsum("qbpbh->qbhp", d).reshape(N_GROUPS, GROUP, STATE)


def _local_step(x, target, p, ex):
    seq, d = x.shape
    n_real = N_META + seq
    tp = -(-n_real // ROW_ALIGN) * ROW_ALIGN

    nstate = N_GROUPS * STATE
    s5 = (p["ssm_lam_re"].reshape(1, nstate), p["ssm_lam_im"].reshape(1, nstate),
          jnp.repeat(p["ssm_log_dt"].reshape(-1), STATE).reshape(1, nstate),
          _compact_b(p["ssm_b_re"]), _compact_b(p["ssm_b_im"]))
    tab_f, tab_r, bc_re, bc_im = _s5_prep(*s5, "s5_prep")
    cc_re = _compact_c(p["ssm_c_re"]).astype(BF16)
    cc_im = _compact_c(p["ssm_c_im"]).astype(BF16)
    dskip = p["ssm_d"].reshape(1, -1)
    dh = dskip.shape[1]

    h0, hn1 = _input_norm_fwd(x, p["meta_tokens"], p["norm_mix_g"] + ex.zero, tp, "norm_mix")
    proj = _mm(hn1, p["w_in"], "nn", "proj")
    co, y, g = _seq_fwd(proj, p["conv_w"], bc_re, bc_im, cc_re, cc_im, dskip, tab_f, "seq_fwd")
    mid = ex.weights("mid", g)
    z = _mm(g, mid["ssm_w_glu"], "nn", "glu")
    mixed = _mix_fwd(co, y, z, p["gain_conv_out"], p["gain_ssm_out"], "mix_fwd")
    mo = _mm(mixed, mid["w_out"], "nn", "out_proj")
    h1, hn2 = _norm_fwd(h0, p["norm_ffn_g"], "norm_ffn", res=mo)
    late = ex.weights("late", hn2)
    up = _mm(hn2, late["w_up"], "nn", "up_proj")
    act = _ffn_act(up, p["ffn_conv_w"], p["ffn_conv_b"], "ffn_act")
    dn = _mm(act, late["w_down"], "nn", "down_proj")
    loss, dh2, dh2b, d_gfin = _loss_bwd(h1, dn, target, p["norm_final_g"], n_real, "loss_bwd")

    g_w_down = _mm(act, dh2b, "tn", "g_w_down")
    dact = _mm(dh2b, late["w_down"], "nt", "d_act")
    dup, dfw_a, dfw_v, dfb_a, dfb_v = _ffn_bwd(up, dact, p["ffn_conv_w"], p["ffn_conv_b"], "ffn_bwd")
    g_w_up = _mm(hn2, dup, "tn", "g_w_up")
    started = ex.grads_ready("late", {"w_up": g_w_up, "w_down": g_w_down})
    dhn2 = _mm(dup, late["w_up"], "nt", "d_hn2", after=started)
    started = ex.grads_send("late", dhn2)
    dh1, dh1b, d_gffn = _norm_bwd(h1, p["norm_ffn_g"] + started[0, 0], dhn2, dh2, "norm_ffn_bwd")
    g_w_out = _mm(mixed, dh1b, "tn", "g_w_out")
    dmixed = _mm(dh1b, mid["w_out"], "nt", "d_mixed")
    dco, dz, dgp, d_gc, d_gs = _mix_bwd(dmixed, co, y, z, p["gain_conv_out"], p["gain_ssm_out"], "mix_bwd")
    g_w_glu = _mm(g, dz, "tn", "g_w_glu")
    started = ex.grads_ready("mid", {"ssm_w_glu": g_w_glu, "w_out": g_w_out})
    dg = _mm(dz, mid["ssm_w_glu"], "nt", "d_gelu", acc_in=dgp, after=started)
    started = ex.grads_send("mid", dg)
    dproj, d_conv_w = _conv_bwd(proj, dco, p["conv_w"] + started[0, 0], "conv_bwd")
    (dproj, dbc_re, dbc_im, dcc_re, dcc_im, d_dskip, da_re, da_im) = _ssm_bwd(
        proj, y, dg, dproj, bc_re, bc_im, cc_re, cc_im, dskip, tab_f, tab_r, "ssm_bwd")
    g_w_in = _mm(hn1, dproj, "tn", "g_w_in")
    started = ex.grads_ready("first", {"w_in": g_w_in})
    dhn1 = _mm(dproj, p["w_in"], "nt", "d_hn1", after=started)
    started = ex.grads_send("first", dhn1)
    grad_x, d_meta, d_gmix = _input_norm_bwd(h0, p["norm_mix_g"] + started[0, 0], dhn1, dh1, n_real, "norm_mix_bwd")

    d_lam_re, d_lam_im, d_log_dt, d_b_re, d_b_im = _s5_prep_bwd(*s5, da_re, da_im, dbc_re, dbc_im, "s5_prep_bwd")
    d_lam_re, d_lam_im = d_lam_re.reshape(N_GROUPS, STATE), d_lam_im.reshape(N_GROUPS, STATE)
    d_log_dt = d_log_dt[0, ::STATE]
    d_b_re, d_b_im = _expand_b(d_b_re), _expand_b(d_b_im)
    grads = {
        "meta_tokens": d_meta, "norm_mix_g": d_gmix, "w_in": g_w_in, "conv_w": d_conv_w,
        "ssm_lam_re": d_lam_re, "ssm_lam_im": d_lam_im, "ssm_log_dt": d_log_dt,
        "ssm_b_re": d_b_re, "ssm_b_im": d_b_im, "ssm_c_re": _expand_c(dcc_re), "ssm_c_im": _expand_c(dcc_im),
        "ssm_d": d_dskip.reshape(N_GROUPS, GROUP), "ssm_w_glu": g_w_glu,
        "gain_conv_out": d_gc, "gain_ssm_out": d_gs, "w_out": g_w_out, "norm_ffn_g": d_gffn,
        "w_up": g_w_up, "ffn_conv_w": jnp.concatenate([dfw_a, dfw_v], axis=1),
        "ffn_conv_b": jnp.concatenate([dfb_a, dfb_v], axis=1), "w_down": g_w_down, "norm_final_g": d_gfin,
    }
    return loss[0, 0], grad_x, grads


def _view(ref, axis, start, size):
    idx = [slice(None)] * len(ref.shape)
    idx[axis] = pl.ds(start, size)
    return ref.at[tuple(idx)]


def _exchange(name, ins, outs, aliases, local_copies, remote_copies):
    ni, no = len(ins), len(outs)
    nl, nr = len(local_copies), len(remote_copies)

    def body(*refs):
        in_refs, out_refs = refs[:ni], refs[ni:ni + no]
        send_sems, recv_sems, local_sems = refs[ni + no:]
        x, y, c = lax.axis_index("x"), lax.axis_index("y"), lax.axis_index("c")
        pos = (x, y, c, 2 * x + y)
        locals_ = [pltpu.make_async_copy(s(in_refs, out_refs, pos), d(in_refs, out_refs, pos), local_sems.at[i])
                   for i, (s, d) in enumerate(local_copies)]
        remotes = []
        for i, (s, d, flip) in enumerate(remote_copies):
            peer = (1 - x if "x" in flip else x, 1 - y if "y" in flip else y, 1 - c if "c" in flip else c)
            remotes.append(pltpu.make_async_remote_copy(
                src_ref=s(in_refs, out_refs, pos), dst_ref=d(in_refs, out_refs, pos),
                send_sem=send_sems.at[i], recv_sem=recv_sems.at[i], device_id=peer, device_id_type=MESH))
        for cp in locals_ + remotes:
            cp.start()
        for cp in remotes:
            cp.wait_recv()
        for cp in remotes:
            cp.wait_send()
        for cp in locals_:
            cp.wait()

    hbm = pl.BlockSpec(memory_space=pl.ANY)
    return pl.pallas_call(
        body, name=name, in_specs=[hbm] * ni, out_specs=[hbm] * no, out_shape=outs,
        input_output_aliases=aliases,
        scratch_shapes=[pltpu.SemaphoreType.DMA((nr,)), pltpu.SemaphoreType.DMA((nr,)),
                        pltpu.SemaphoreType.DMA((max(nl, 1),))],
    )(*ins)


BIG = {"w_in": (0, 1), "ssm_w_glu": (1, 0), "w_out": (1, 0), "w_up": (0, 1), "w_down": (1, 0)}
BIG_NAMES = tuple(BIG)
FLIPS = ("y", "x", "xy")


def _peer_chip(pos, flip):
    x, y, _, _ = pos
    return 2 * (1 - x if "x" in flip else x) + (1 - y if "y" in flip else y)


def _block_rows(rows, cols, itemsize, mult):
    return _pick_tile(rows, max(mult, (2 * 1024 * 1024) // (cols * itemsize)), mult)


def _cast_into_full(w, kc, shard_axis, name):
    r, cdim = w.shape
    tr = _block_rows(r, cdim, 4, 16)
    nb = r // tr

    def body(kc_ref, w_ref, o_ref):
        o_ref[...] = w_ref[...].astype(BF16)

    if shard_axis == 1:
        full, o_spec = (r, 4 * cdim), pl.BlockSpec((tr, cdim), lambda i, kc: (i, kc[0]))
    else:
        full, o_spec = (4 * r, cdim), pl.BlockSpec((tr, cdim), lambda i, kc: (kc[0] * nb + i, 0))
    return pl.pallas_call(
        body, name=name,
        grid_spec=pltpu.PrefetchScalarGridSpec(
            num_scalar_prefetch=1, grid=(nb,), in_specs=[pl.BlockSpec((tr, cdim), lambda i, kc: (i, 0))],
            out_specs=o_spec),
        out_shape=jax.ShapeDtypeStruct(full, BF16), compiler_params=_cparams("parallel"))(kc, w)


def _pair_sum(g, recv, kc, half_axis, name, out_dtype):
    hr, hc = recv.shape
    tr = _block_rows(hr, hc, 4, 16)
    nb = hr // tr

    def body(kc_ref, g_ref, r_ref, o_ref):
        o_ref[...] = (g_ref[...] + r_ref[...]).astype(out_dtype)

    if half_axis == 0:
        g_spec = pl.BlockSpec((tr, hc), lambda i, kc: (kc[1] * nb + i, 0))
    elif half_axis == 1:
        g_spec = pl.BlockSpec((tr, hc), lambda i, kc: (i, kc[1]))
    else:
        g_spec = pl.BlockSpec((tr, hc), lambda i, kc: (i, 0))
    same = pl.BlockSpec((tr, hc), lambda i, kc: (i, 0))
    return pl.pallas_call(
        body, name=name,
        grid_spec=pltpu.PrefetchScalarGridSpec(num_scalar_prefetch=1, grid=(nb,), in_specs=[g_spec, same],
                                               out_specs=same),
        out_shape=jax.ShapeDtypeStruct((hr, hc), out_dtype), compiler_params=_cparams("parallel"))(kc, g, recv)


def _chip_sum(own, recv, kc, own_axis, out_axis, name):
    _, sr, sc = recv.shape
    tr = _block_rows(sr, sc, 4, 16)
    nb = sr // tr

    def body(kc_ref, o_ref, r_ref, t_ref):
        k = kc_ref[0]
        own_v = o_ref[...].astype(F32)
        r = [r_ref[m].astype(F32) for m in range(3)]
        terms = []
        for kk in range(4):
            m = jnp.bitwise_xor(k, kk)
            terms.append(jnp.where(m == 0, own_v, jnp.where(m == 1, r[0], jnp.where(m == 2, r[1], r[2]))))
        t_ref[...] = (terms[0] + terms[1]) + (terms[2] + terms[3])

    if own_axis == 0:
        own_spec = pl.BlockSpec((tr, sc), lambda i, kc: (kc[0] * nb + i, 0))
    elif own_axis == 1:
        own_spec = pl.BlockSpec((tr, sc), lambda i, kc: (i, kc[0]))
    else:
        own_spec = pl.BlockSpec((tr, sc), lambda i, kc: (kc[1] * nb + i, 0))
    if out_axis == 0:
        out_full, out_spec = (2 * sr, sc), pl.BlockSpec((tr, sc), lambda i, kc: (kc[1] * nb + i, 0))
    else:
        out_full, out_spec = (sr, 2 * sc), pl.BlockSpec((tr, sc), lambda i, kc: (i, kc[1]))
    return pl.pallas_call(
        body, name=name,
        grid_spec=pltpu.PrefetchScalarGridSpec(
            num_scalar_prefetch=1, grid=(nb,),
            in_specs=[own_spec, pl.BlockSpec((3, tr, sc), lambda i, kc: (0, i, 0))],
            out_specs=out_spec),
        out_shape=jax.ShapeDtypeStruct(out_full, F32), compiler_params=_cparams("parallel"))(kc, own, recv)


def _adamw(w, g, m, v, name):
    r, cdim = w.shape
    tr = _block_rows(r, cdim, 4, 8)
    c1 = 1.0 - ADAM_B1 ** ADAM_STEP
    c2 = 1.0 - ADAM_B2 ** ADAM_STEP

    def body(w_ref, g_ref, m_ref, v_ref, go_ref, d_ref, nm_ref, nv_ref):
        gv = g_ref[...]
        go_ref[...] = gv
        nm = ADAM_B1 * m_ref[...] + (1.0 - ADAM_B1) * gv
        nv = ADAM_B2 * v_ref[...] + (1.0 - ADAM_B2) * (gv * gv)
        d_ref[...] = -ADAM_LR * ((nm / c1) / (jnp.sqrt(nv / c2) + ADAM_EPS) + ADAM_WD * w_ref[...])
        nm_ref[...] = nm
        nv_ref[...] = nv

    spec = _rows(cdim, tr)
    return pl.pallas_call(body, name=name, grid=(r // tr,), in_specs=[spec] * 4, out_specs=[spec] * 4,
                          out_shape=[jax.ShapeDtypeStruct((r, cdim), F32)] * 4,
                          compiler_params=_cparams("parallel"))(w, g, m, v)


def _adamw_whole(ws, gs, ms, vs, name):
    n = len(ws)
    c1 = 1.0 - ADAM_B1 ** ADAM_STEP
    c2 = 1.0 - ADAM_B2 ** ADAM_STEP

    def body(*refs):
        for i in range(n):
            w_ref, g_ref, m_ref, v_ref, d_ref, nm_ref, nv_ref = [refs[j * n + i] for j in range(7)]
            gv = g_ref[...]
            nm = ADAM_B1 * m_ref[...] + (1.0 - ADAM_B1) * gv
            nv = ADAM_B2 * v_ref[...] + (1.0 - ADAM_B2) * (gv * gv)
            d_ref[...] = -ADAM_LR * ((nm / c1) / (jnp.sqrt(nv / c2) + ADAM_EPS) + ADAM_WD * w_ref[...])
            nm_ref[...] = nm
            nv_ref[...] = nv

    vmem = pl.BlockSpec(memory_space=pltpu.VMEM)
    out = pl.pallas_call(body, name=name, in_specs=[vmem] * (4 * n), out_specs=[vmem] * (3 * n),
                         out_shape=[jax.ShapeDtypeStruct(a.shape, F32) for a in ws] * 3,
                         compiler_params=pltpu.CompilerParams(vmem_limit_bytes=VMEM_LIMIT))(*ws, *gs, *ms, *vs)
    return out[:n], out[n:2 * n], out[2 * n:]


SIDE_EFFECT = pltpu.SideEffectType.DATAFLOW_SIDE_EFFECTING


def _descriptors(copies, refs, send_sems, recv_sems):
    x, y, c = lax.axis_index("x"), lax.axis_index("y"), lax.axis_index("c")
    pos = (x, y, c, 2 * x + y)
    out = []
    for i, (s, d, flip) in enumerate(copies):
        peer = (1 - x if "x" in flip else x, 1 - y if "y" in flip else y, 1 - c if "c" in flip else c)
        out.append(pltpu.make_async_remote_copy(
            src_ref=s(refs, refs, pos), dst_ref=d(refs, refs, pos),
            send_sem=send_sems.at[i], recv_sem=recv_sems.at[i], device_id=peer, device_id_type=MESH))
    return out


def _exchange_start(name, bufs, copies, after=None):
    n, nr = len(bufs), len(copies)
    na = 0 if after is None else 1

    def body(*refs):
        for cp in _descriptors(copies, refs[:n], refs[n + na], refs[n + na + 1]):
            cp.start()
        token = refs[2 * n + na + 2]
        token[...] = jnp.zeros_like(token)

    hbm = pl.BlockSpec(memory_space=pltpu.HBM)
    sem = pl.BlockSpec(memory_space=pltpu.SEMAPHORE)
    out = pl.pallas_call(
        body, name=name,
        in_specs=[hbm] * n + [pl.BlockSpec(memory_space=pl.ANY)] * na,
        out_specs=(sem, sem, *[hbm] * n, pl.BlockSpec(memory_space=pltpu.VMEM)),
        out_shape=(pltpu.SemaphoreType.DMA((nr,)), pltpu.SemaphoreType.DMA((nr,)),
                   *[pltpu.HBM(b.shape, b.dtype) for b in bufs], jax.ShapeDtypeStruct((SUBLANES, LANES), F32)),
        input_output_aliases={i: 2 + i for i in range(n)},
        compiler_params=pltpu.CompilerParams(has_side_effects=SIDE_EFFECT),
    )(*[pltpu.with_memory_space_constraint(b, pltpu.HBM) for b in bufs], *([after] * na))
    return out[0], out[1], list(out[2:2 + n]), out[2 + n]


def _exchange_wait(name, send_sems, recv_sems, bufs, copies, after):
    n = len(bufs)

    def body(*refs):
        for cp in _descriptors(copies, refs[:n], refs[n], refs[n + 1]):
            cp.wait_send()
            cp.wait_recv()

    hbm = pl.BlockSpec(memory_space=pltpu.HBM)
    sem = pl.BlockSpec(memory_space=pltpu.SEMAPHORE)
    out = pl.pallas_call(
        body, name=name,
        in_specs=[hbm] * n + [sem, sem, pl.BlockSpec(memory_space=pl.ANY)],
        out_specs=tuple([hbm] * n),
        out_shape=tuple(pltpu.HBM(b.shape, b.dtype) for b in bufs),
        input_output_aliases={i: i for i in range(n)},
        compiler_params=pltpu.CompilerParams(has_side_effects=SIDE_EFFECT),
    )(*bufs, send_sems, recv_sems, after)
    return list(out)


FIRST = ("w_in",)
MID = ("ssm_w_glu", "w_out")
LATE = ("w_up", "w_down")
GROUPS = {"first": FIRST, "mid": MID, "late": LATE}


def _gather_copies(names, shard_shapes):
    def region(i, chip, c):
        half_axis, shard_axis = BIG[names[i]]
        ssize = shard_shapes[i][shard_axis]
        hsize = shard_shapes[i][half_axis] // 2
        return lambda ref: _view(_view(ref, shard_axis, chip * ssize, ssize), half_axis, c * hsize, hsize)

    ici, d2d = [], []
    for i in range(len(names)):
        for flip in FLIPS:
            ici.append((lambda I, O, pos, i=i: region(i, pos[3], pos[2])(I[i]),
                        lambda I, O, pos, i=i: region(i, pos[3], pos[2])(O[i]), flip))
            d2d.append((lambda I, O, pos, i=i, flip=flip: region(i, _peer_chip(pos, flip), pos[2])(I[i]),
                        lambda I, O, pos, i=i, flip=flip: region(i, _peer_chip(pos, flip), pos[2])(O[i]), "c"))
    return ici, d2d


def _half_shape(n, shape):
    r, cdim = shape
    return (r // 2, cdim) if BIG[n][0] == 0 else (r, cdim // 2)


def _sub_shape(n, shape):
    hr, hc = _half_shape(n, shape)
    return (hr, hc // 4) if BIG[n][1] == 1 else (hr // 4, hc)


def _pair_copies(names, shapes, with_pack, dst_off):
    n = len(names)

    def other_half(i, ref, pos):
        half_axis = BIG[names[i]][0]
        hsize = shapes[i][half_axis] // 2
        return _view(ref, half_axis, (1 - pos[2]) * hsize, hsize)

    copies = [(lambda I, O, pos, i=i: other_half(i, I[i], pos), lambda I, O, pos, i=i: O[dst_off + i], "c")
              for i in range(n)]
    if with_pack:
        copies.append((lambda I, O, pos: I[n], lambda I, O, pos: O[dst_off + n], "c"))
    return copies


def _chip_copies(names, shapes, pack_rows, dst_off):
    n = len(names)

    def piece(i, ref, chip):
        shard_axis = BIG[names[i]][1]
        ssize = _sub_shape(names[i], shapes[i])[shard_axis]
        return _view(ref, shard_axis, chip * ssize, ssize)

    copies = []
    for i in range(n):
        for slot, flip in enumerate(FLIPS):
            copies.append((lambda I, O, pos, i=i, flip=flip: piece(i, I[i], _peer_chip(pos, flip)),
                           lambda I, O, pos, i=i, slot=slot: O[dst_off + i].at[slot], flip))
    if pack_rows:
        for slot, flip in enumerate(FLIPS):
            copies.append((lambda I, O, pos: _view(I[n], 0, pos[2] * (pack_rows // 2), pack_rows // 2),
                           lambda I, O, pos, slot=slot: O[dst_off + n].at[slot], flip))
    return copies


class _Exchanges:
    def __init__(self, shards, tiny, kc):
        self.kc = kc
        wb = {n: _cast_into_full(shards[n], kc, BIG[n][1], "cast_" + n) for n in BIG_NAMES}
        nf = len(FIRST)
        ici, d2d = _gather_copies(FIRST, [shards[n].shape for n in FIRST])
        local = [(lambda I, O, pos: I[nf], lambda I, O, pos: O[nf].at[pos[3]])]
        ici += [(lambda I, O, pos: I[nf], lambda I, O, pos: O[nf].at[pos[3]], flip) for flip in FLIPS]
        outs = ([jax.ShapeDtypeStruct(wb[n].shape, BF16) for n in FIRST]
                + [jax.ShapeDtypeStruct((4,) + tiny.shape, F32)])
        got = _exchange("gather_ici", [wb[n] for n in FIRST] + [tiny], outs, {i: i for i in range(nf)}, local, ici)
        full = _exchange("gather_d2d", list(got[:nf]), outs[:nf], {i: i for i in range(nf)}, [], d2d)
        self.first = dict(zip(FIRST, full))
        self.tiny_all = got[nf]
        self.gathering, self.pairing, self.reducing = {}, {}, {}
        after = full[0]
        self.zero = 0.0
        for group in ("mid", "late"):
            names = GROUPS[group]
            copies = _gather_copies(names, [shards[n].shape for n in names])
            started = _exchange_start("gather_%s_start" % group, [wb[n] for n in names], copies[0], after)
            self.gathering[group] = (started, copies)
            after = started[2][0]
            self.zero = self.zero + started[3][0, 0]

    def weights(self, group, after):
        (send_sems, recv_sems, bufs, _), (ici, d2d) = self.gathering[group]
        got = _exchange_wait("gather_%s_wait" % group, send_sems, recv_sems, bufs, ici, after)
        outs = [jax.ShapeDtypeStruct(b.shape, BF16) for b in got]
        full = _exchange("gather_%s_d2d" % group, got, outs, {i: i for i in range(len(got))}, [], d2d)
        return dict(zip(GROUPS[group], full))

    def grads_ready(self, group, grads):
        names = GROUPS[group]
        gs = [grads[n] for n in names]
        land = [lax.empty(_half_shape(n, g.shape), F32) for n, g in zip(names, gs)]
        copies = _pair_copies(names, [g.shape for g in gs], False, len(names))
        started = _exchange_start("pair_%s_start" % group, gs + land, copies)
        self.pairing[group] = (started, copies)
        return started[3]

    def grads_send(self, group, after):
        names = GROUPS[group]
        n = len(names)
        (send_sems, recv_sems, bufs, _), copies = self.pairing[group]
        bufs = _exchange_wait("pair_%s_wait" % group, send_sems, recv_sems, bufs, copies, after)
        chip = [_pair_sum(bufs[i], bufs[n + i], self.kc, BIG[names[i]][0], "pair_sum_" + names[i], BF16)
                for i in range(n)]
        shapes = [bufs[i].shape for i in range(n)]
        land = [lax.empty((3,) + _sub_shape(names[i], shapes[i]), BF16) for i in range(n)]
        copies = _chip_copies(names, shapes, 0, n)
        started = _exchange_start("reduce_%s_start" % group, chip + land, copies)
        self.reducing[group] = (started, copies)
        return started[3]

    def finish_pack(self, pack):
        kc = self.kc
        prow = pack.shape[0] // 2
        recv = _exchange("reduce_d2d", [pack], [jax.ShapeDtypeStruct(pack.shape, F32)], {}, [],
                         _pair_copies((), [], True, 0))
        chip_pack = _pair_sum(pack, recv[0], kc, None, "pair_sum_pack", F32)
        copies = _chip_copies((), [], pack.shape[0], 1)
        land = lax.empty((3, prow, pack.shape[1]), F32)
        pack_sems_s, pack_sems_r, pack_bufs, after = _exchange_start("reduce_pack_start", [chip_pack, land], copies)

        names, chips, recvs = (), [], []
        for group, group_names in GROUPS.items():
            (send_sems, recv_sems, bufs, _), group_copies = self.reducing[group]
            bufs = _exchange_wait("reduce_%s_wait" % group, send_sems, recv_sems, bufs, group_copies, after)
            n = len(group_names)
            names, chips, recvs = names + group_names, chips + bufs[:n], recvs + bufs[n:]
            after = bufs[n]
        total = [_chip_sum(chips[i], recvs[i], kc, BIG[n][1], BIG[n][0], "chip_sum_" + n)
                 for i, n in enumerate(names)]

        def my_half(half_axis, ref, pos):
            hsize = ref.shape[half_axis] // 2
            return _view(ref, half_axis, pos[2] * hsize, hsize)

        swap = [(lambda I, O, pos, i=i, n=n: my_half(BIG[n][0], I[i], pos),
                 lambda I, O, pos, i=i, n=n: my_half(BIG[n][0], O[i], pos), "c") for i, n in enumerate(names)]
        self.swapping = (_exchange_start("swap_start", total, swap), swap, names)

        chip_pack, recv_pack = _exchange_wait("reduce_pack_wait", pack_sems_s, pack_sems_r, pack_bufs, copies,
                                              self.swapping[0][3])
        total_pack = _chip_sum(chip_pack, recv_pack, kc, None, 0, "chip_sum_pack")
        swap = [(lambda I, O, pos: my_half(0, I[0], pos), lambda I, O, pos: my_half(0, O[0], pos), "c")]
        return _exchange("swap_pack", [total_pack], [jax.ShapeDtypeStruct(pack.shape, F32)], {0: 0}, [], swap)[0]

    def finish_big(self, after):
        (send_sems, recv_sems, bufs, _), swap, names = self.swapping
        return dict(zip(names, _exchange_wait("swap_wait", send_sems, recv_sems, bufs, swap, after)))


WEIGHTS = ("meta_tokens", "norm_mix_g", "w_in", "conv_w", "ssm_lam_re", "ssm_lam_im", "ssm_log_dt", "ssm_b_re",
           "ssm_b_im", "ssm_c_re", "ssm_c_im", "ssm_d", "ssm_w_glu", "gain_conv_out", "gain_ssm_out", "w_out",
           "norm_ffn_g", "w_up", "ffn_conv_w", "ffn_conv_b", "w_down", "norm_final_g")
TINY_SHARDED = ("meta_tokens", "conv_w", "ffn_conv_w")
REPLICATED = tuple(n for n in WEIGHTS if n not in BIG and n not in TINY_SHARDED)
PACK_COLS = 512


def _pack(arrays, row_mult, cols):
    flat = jnp.concatenate([a.reshape(-1).astype(F32) for a in arrays])
    n = flat.shape[0]
    total = -(-n // (row_mult * cols)) * (row_mult * cols)
    return jnp.concatenate([flat, jnp.zeros((total - n,), F32)]).reshape(total // cols, cols)


def _unpack(packed, shapes):
    flat = packed.reshape(-1)
    out, off = [], 0
    for s in shapes:
        n = math.prod(s)
        out.append(flat[off:off + n].reshape(s))
        off += n
    return out


def kernel(x, meta_tokens, norm_mix_g, w_in, conv_w, ssm_lam_re, ssm_lam_im, ssm_log_dt, ssm_b_re, ssm_b_im, ssm_c_re, ssm_c_im, ssm_d, ssm_w_glu, gain_conv_out, gain_ssm_out, w_out, norm_ffn_g, w_up, ffn_conv_w, ffn_conv_b, w_down, norm_final_g, loss_target, m_meta_tokens, m_norm_mix_g, m_w_in, m_conv_w, m_ssm_lam_re, m_ssm_lam_im, m_ssm_log_dt, m_ssm_b_re, m_ssm_b_im, m_ssm_c_re, m_ssm_c_im, m_ssm_d, m_ssm_w_glu, m_gain_conv_out, m_gain_ssm_out, m_w_out, m_norm_ffn_g, m_w_up, m_ffn_conv_w, m_ffn_conv_b, m_w_down, m_norm_final_g, v_meta_tokens, v_norm_mix_g, v_w_in, v_conv_w, v_ssm_lam_re, v_ssm_lam_im, v_ssm_log_dt, v_ssm_b_re, v_ssm_b_im, v_ssm_c_re, v_ssm_c_im, v_ssm_d, v_ssm_w_glu, v_gain_conv_out, v_gain_ssm_out, v_w_out, v_norm_ffn_g, v_w_up, v_ffn_conv_w, v_ffn_conv_b, v_w_down, v_norm_final_g):
    args = dict(locals())
    w = {n: args[n] for n in WEIGHTS}
    mom = {n: args["m_" + n] for n in WEIGHTS}
    var = {n: args["v_" + n] for n in WEIGHTS}
    kx, ky, kc_ = lax.axis_index("x"), lax.axis_index("y"), lax.axis_index("c")
    chip = 2 * kx + ky
    kc = jnp.stack([chip, kc_]).astype(jnp.int32)

    def squeeze(n, a):
        if n == "meta_tokens":
            return a
        if n == "norm_final_g":
            return a.reshape(1, -1)
        a = a[0]
        return a.reshape(1, -1) if a.ndim == 1 else a

    wl = {n: squeeze(n, w[n]) for n in WEIGHTS}
    ml = {n: squeeze(n, mom[n]) for n in WEIGHTS}
    vl = {n: squeeze(n, var[n]) for n in WEIGHTS}

    tiny = _pack([wl[n] for n in TINY_SHARDED], SUBLANES, LANES)
    ex = _Exchanges({n: wl[n] for n in BIG_NAMES}, tiny, kc)
    tiny_shapes = [wl[n].shape for n in TINY_SHARDED]
    tiny_parts = [_unpack(ex.tiny_all[k], tiny_shapes) for k in range(4)]
    p = {n: wl[n] for n in WEIGHTS if n not in BIG}
    p.update(ex.first)
    for j, n in enumerate(TINY_SHARDED):
        p[n] = jnp.concatenate([tiny_parts[k][j] for k in range(4)], axis=1)
    p["ssm_log_dt"] = wl["ssm_log_dt"].reshape(-1)

    loss_local, grad_x, grads = _local_step(x[0], loss_target[0], p, ex)

    small_names = REPLICATED + TINY_SHARDED
    small_shapes = [tuple(grads[n].shape) for n in small_names] + [(1,)]
    pack = _pack([grads[n] for n in small_names] + [loss_local.reshape(1)], 2 * 16, PACK_COLS)
    g_pack = ex.finish_pack(pack)
    g_small = dict(zip(small_names + ("loss",), _unpack(g_pack, small_shapes)))
    loss = g_small["loss"][0]
    g = {}
    for n in REPLICATED:
        g[n] = g_small[n].reshape(w[n].shape)
    for n in TINY_SHARDED:
        cols = wl[n].shape[1]
        g[n] = lax.dynamic_slice_in_dim(g_small[n], chip * cols, cols, axis=1).reshape(w[n].shape)

    delta, new_m, new_v = {}, {}, {}
    rank2 = lambda a: a.reshape(1, -1) if a.ndim == 1 else a
    small = [[rank2(d[n]) for n in small_names] for d in (w, g, mom, var)]
    for d, outs in zip((delta, new_m, new_v), _adamw_whole(*small, "adamw_small")):
        d.update(zip(small_names, outs))
    g_big = ex.finish_big(delta[small_names[0]])
    for n in BIG_NAMES:
        g[n], delta[n], new_m[n], new_v[n] = _adamw(wl[n], g_big[n], ml[n], vl[n], "adamw_" + n)

    def like(n, a):
        return a.reshape(w[n].shape)

    return (loss, grad_x[None], *[like(n, g[n]) for n in WEIGHTS], *[like(n, delta[n]) for n in WEIGHTS],
            *[like(n, new_m[n]) for n in WEIGHTS], *[like(n, new_v[n]) for n in WEIGHTS])
```

```python
import functools
import math

import jax
import jax.numpy as jnp
from jax import lax
from jax.experimental import pallas as pl
from jax.experimental.pallas import tpu as pltpu

F32 = jnp.float32
BF16 = jnp.bfloat16
MESH = pl.DeviceIdType.MESH

N_META = 16
N_GROUPS = 32
GROUP = 16
STATE = 64
RMS_EPS = 1e-6
ADAM_LR = 0.001
ADAM_B1 = 0.9
ADAM_B2 = 0.999
ADAM_EPS = 1e-08
ADAM_WD = 0.01
ADAM_STEP = 10

LANES = 128
SUBLANES = 8
ROW_ALIGN = 128
ROW_TILES = 4
VMEM_LIMIT = 52 * 1024 * 1024
MM_VMEM_BUDGET = 40 * 1024 * 1024
GELU_C = math.sqrt(2.0 / math.pi)
GELU_A = 0.044715


def _cparams(*sem):
    return pltpu.CompilerParams(dimension_semantics=sem, vmem_limit_bytes=VMEM_LIMIT)


def _pick_tile(dim, cap, mult):
    best = None
    for t in range(mult, min(dim, cap) + 1, mult):
        if dim % t == 0:
            best = t
    return best if best is not None else dim


def _mm(a, b, mode, name, out_dtype=F32, acc_in=None, after=None):
    if mode == "tn":
        kdim, m = a.shape
    else:
        m, kdim = a.shape
    n = b.shape[0] if mode == "nt" else b.shape[1]
    tm = _pick_tile(m, 1408, LANES if mode == "tn" else 16)
    tk = _pick_tile(kdim, 2816, LANES)
    nk = kdim // tk
    out_bytes = jnp.dtype(out_dtype).itemsize
    for cap in (1408, 1024, 512, 256, LANES):
        tn = _pick_tile(n, cap, LANES)
        blocks = 2 * (tm * tk * 2 + tk * tn * 2 + tm * tn * out_bytes * (2 if acc_in is not None else 1))
        if blocks + (tm * tn * 4 if nk > 1 else 0) <= MM_VMEM_BUDGET:
            break
    has_acc = acc_in is not None

    def body(*refs):
        if after is not None:
            refs = refs[1:]
        if has_acc:
            a_ref, b_ref, c_ref, o_ref = refs[:4]
            rest = refs[4:]
        else:
            a_ref, b_ref, o_ref = refs[:3]
            c_ref = None
            rest = refs[3:]
        if mode == "nn":
            p = jnp.dot(a_ref[...], b_ref[...], preferred_element_type=F32)
        elif mode == "nt":
            p = lax.dot_general(a_ref[...], b_ref[...], (((1,), (1,)), ((), ())), preferred_element_type=F32)
        else:
            p = lax.dot_general(a_ref[...], b_ref[...], (((0,), (0,)), ((), ())), preferred_element_type=F32)
        if nk == 1:
            if has_acc:
                p = p + c_ref[...]
            o_ref[...] = p.astype(out_dtype)
        else:
            acc_ref = rest[0]
            k = pl.program_id(2)

            @pl.when(k == 0)
            def _():
                acc_ref[...] = p + c_ref[...] if has_acc else p

            @pl.when(k > 0)
            def _():
                acc_ref[...] += p

            @pl.when(k == nk - 1)
            def _():
                o_ref[...] = acc_ref[...].astype(out_dtype)

    if mode == "tn":
        a_spec = pl.BlockSpec((tk, tm), lambda i, j, k: (k, i))
    else:
        a_spec = pl.BlockSpec((tm, tk), lambda i, j, k: (i, k))
    if mode == "nt":
        b_spec = pl.BlockSpec((tn, tk), lambda i, j, k: (j, k))
    else:
        b_spec = pl.BlockSpec((tk, tn), lambda i, j, k: (k, j))
    o_spec = pl.BlockSpec((tm, tn), lambda i, j, k: (i, j))
    in_specs = [a_spec, b_spec] + ([o_spec] if has_acc else [])
    args = (a, b) + ((acc_in,) if has_acc else ())
    if after is not None:
        in_specs = [pl.BlockSpec(memory_space=pl.ANY)] + in_specs
        args = (after,) + args
    return pl.pallas_call(
        body, name=name, grid=(m // tm, n // tn, nk),
        in_specs=in_specs, out_specs=o_spec,
        out_shape=jax.ShapeDtypeStruct((m, n), out_dtype),
        scratch_shapes=[pltpu.VMEM((tm, tn), F32)] if nk > 1 else [],
        compiler_params=_cparams("parallel", "parallel", "arbitrary"),
    )(*args)


def _rows(shape_cols, tr, dtype=None):
    return pl.BlockSpec((tr, shape_cols), lambda i: (i, 0))


def _const(shape):
    return pl.BlockSpec(shape, lambda i: (0,) * len(shape))


def _rms(x):
    return lax.rsqrt(jnp.mean(x * x, axis=-1, keepdims=True) + RMS_EPS)


def _rms_bwd(x, r, g, dy):
    xn = x * r
    dxn = dy * g
    dx = r * (dxn - xn * jnp.mean(dxn * xn, axis=-1, keepdims=True))
    return dx, dy * xn


def _gelu(y):
    return 0.5 * y * (1.0 + jnp.tanh(GELU_C * (y + GELU_A * y * y * y)))


def _gelu_grad(y):
    t = jnp.tanh(GELU_C * (y + GELU_A * y * y * y))
    return 0.5 * (1.0 + t) + 0.5 * y * (1.0 - t * t) * GELU_C * (1.0 + 3.0 * GELU_A * y * y)


def _sigmoid(z):
    return 1.0 / (1.0 + jnp.exp(-z))


def _norm_fwd(h, g, name, res=None):
    tp, d = h.shape
    tr = tp // ROW_TILES
    has_res = res is not None

    def body(*refs):
        if has_res:
            h_ref, r_ref, g_ref, s_ref, hn_ref = refs
            x = h_ref[...] + r_ref[...]
            s_ref[...] = x
        else:
            h_ref, g_ref, hn_ref = refs
            x = h_ref[...]
        hn_ref[...] = (x * _rms(x) * g_ref[...]).astype(BF16)

    in_specs = [_rows(d, tr)] + ([_rows(d, tr)] if has_res else []) + [_const((1, d))]
    out_specs = ([_rows(d, tr)] if has_res else []) + [_rows(d, tr)]
    out_shape = ([jax.ShapeDtypeStruct((tp, d), F32)] if has_res else []) + [jax.ShapeDtypeStruct((tp, d), BF16)]
    args = (h,) + ((res,) if has_res else ()) + (g,)
    out = pl.pallas_call(body, name=name, grid=(ROW_TILES,), in_specs=in_specs, out_specs=out_specs,
                         out_shape=out_shape, compiler_params=_cparams("parallel"))(*args)
    return out if has_res else out[0]


def _norm_bwd(h, g, dhn, dres, name):
    tp, d = h.shape
    tr = tp // ROW_TILES

    def body(h_ref, g_ref, dhn_ref, dres_ref, dh_ref, dhb_ref, dg_ref):
        x = h_ref[...]
        dx, dgs = _rms_bwd(x, _rms(x), g_ref[...], dhn_ref[...])
        dh = dres_ref[...] + dx
        dh_ref[...] = dh
        dhb_ref[...] = dh.astype(BF16)

        @pl.when(pl.program_id(0) == 0)
        def _():
            dg_ref[...] = jnp.zeros_like(dg_ref)

        dg_ref[...] += jnp.sum(dgs, axis=0, keepdims=True)

    return pl.pallas_call(
        body, name=name, grid=(ROW_TILES,),
        in_specs=[_rows(d, tr), _const((1, d)), _rows(d, tr), _rows(d, tr)],
        out_specs=[_rows(d, tr), _rows(d, tr), _const((1, d))],
        out_shape=[jax.ShapeDtypeStruct((tp, d), F32), jax.ShapeDtypeStruct((tp, d), BF16),
                   jax.ShapeDtypeStruct((1, d), F32)],
        compiler_params=_cparams("arbitrary"))(h, g, dhn, dres)


def _input_norm_bwd(h, g, dhn, dres, n_real, name):
    tp, d = h.shape
    tr = tp // ROW_TILES

    def body(h_ref, g_ref, dhn_ref, dres_ref, dx_ref, dmeta_ref, dg_ref, stage, sem):
        i = pl.program_id(0)
        x = h_ref[...]
        dx, dgs = _rms_bwd(x, _rms(x), g_ref[...], dhn_ref[...])
        stage[...] = dres_ref[...] + dx

        @pl.when(i == 0)
        def _():
            dg_ref[...] = jnp.zeros_like(dg_ref)
            dmeta_ref[...] = stage[:N_META, :]

        dg_ref[...] += jnp.sum(dgs, axis=0, keepdims=True)
        for t in range(ROW_TILES):
            lo, hi = max(t * tr, N_META), min((t + 1) * tr, n_real)
            if hi > lo:
                @pl.when(i == t)
                def _(t=t, lo=lo, hi=hi):
                    cp = pltpu.make_async_copy(stage.at[pl.ds(lo - t * tr, hi - lo), :],
                                               dx_ref.at[pl.ds(lo - N_META, hi - lo), :], sem)
                    cp.start()
                    cp.wait()

    return pl.pallas_call(
        body, name=name, grid=(ROW_TILES,),
        in_specs=[_rows(d, tr), _const((1, d)), _rows(d, tr), _rows(d, tr)],
        out_specs=[pl.BlockSpec(memory_space=pl.ANY), _const((N_META, d)), _const((1, d))],
        out_shape=[jax.ShapeDtypeStruct((n_real - N_META, d), F32), jax.ShapeDtypeStruct((N_META, d), F32),
                   jax.ShapeDtypeStruct((1, d), F32)],
        scratch_shapes=[pltpu.VMEM((tr, d), F32), pltpu.SemaphoreType.DMA],
        compiler_params=_cparams("arbitrary"))(h, g, dhn, dres)


def _load_token_rows(tok_hbm, buf, sem, tr, n_real, head=None):
    i = pl.program_id(0)
    for t in range(ROW_TILES):
        base = t * tr
        lo, hi = max(base, N_META), min(base + tr, n_real)

        @pl.when(i == t)
        def _(base=base, lo=lo, hi=hi):
            if base < N_META:
                buf[0:N_META - base, :] = (jnp.zeros((N_META - base, buf.shape[1]), F32) if head is None
                                           else head[base:N_META, :])
            if hi < base + tr:
                buf[max(hi, base) - base:tr, :] = jnp.zeros((base + tr - max(hi, base), buf.shape[1]), F32)
            if hi > lo:
                cp = pltpu.make_async_copy(tok_hbm.at[pl.ds(lo - N_META, hi - lo), :],
                                           buf.at[pl.ds(lo - base, hi - lo), :], sem)
                cp.start()
                cp.wait()


def _input_norm_fwd(x, meta, g, tp, name):
    seq, d = x.shape
    tr = tp // ROW_TILES
    n_real = N_META + seq

    def body(x_hbm, meta_ref, g_ref, h_ref, hn_ref, buf, sem):
        _load_token_rows(x_hbm, buf, sem, tr, n_real, head=meta_ref)
        h = buf[...]
        h_ref[...] = h
        hn_ref[...] = (h * _rms(h) * g_ref[...]).astype(BF16)

    return pl.pallas_call(
        body, name=name, grid=(ROW_TILES,),
        in_specs=[pl.BlockSpec(memory_space=pl.ANY), _const((N_META, d)), _const((1, d))],
        out_specs=[_rows(d, tr), _rows(d, tr)],
        out_shape=[jax.ShapeDtypeStruct((tp, d), F32), jax.ShapeDtypeStruct((tp, d), BF16)],
        scratch_shapes=[pltpu.VMEM((tr, d), F32), pltpu.SemaphoreType.DMA],
        compiler_params=_cparams("arbitrary"))(x, meta, g)


def _loss_bwd(h1, dn, target, g, n_real, name):
    tp, d = h1.shape
    tr = tp // ROW_TILES

    def body(h1_ref, dn_ref, t_hbm, g_ref, loss_ref, dh_ref, dhb_ref, dg_ref, t_buf, sem):
        i = pl.program_id(0)
        _load_token_rows(t_hbm, t_buf, sem, tr, n_real)
        x = h1_ref[...] + dn_ref[...]
        r = _rms(x)
        row = i * tr + lax.broadcasted_iota(jnp.int32, (tr, d), 0)
        valid = (row >= N_META) & (row < n_real)
        e = jnp.where(valid, x * r * g_ref[...] - t_buf[...], 0.0)
        dx, dgs = _rms_bwd(x, r, g_ref[...], e * (1.0 / d))
        dh_ref[...] = dx
        dhb_ref[...] = dx.astype(BF16)

        @pl.when(i == 0)
        def _():
            dg_ref[...] = jnp.zeros_like(dg_ref)
            loss_ref[...] = jnp.zeros_like(loss_ref)

        dg_ref[...] += jnp.sum(dgs, axis=0, keepdims=True)
        loss_ref[...] += (0.5 / d) * jnp.sum(jnp.sum(e * e, axis=0, keepdims=True), axis=1, keepdims=True)

    return pl.pallas_call(
        body, name=name, grid=(ROW_TILES,),
        in_specs=[_rows(d, tr), _rows(d, tr), pl.BlockSpec(memory_space=pl.ANY), _const((1, d))],
        out_specs=[_const((1, LANES)), _rows(d, tr), _rows(d, tr), _const((1, d))],
        out_shape=[jax.ShapeDtypeStruct((1, LANES), F32), jax.ShapeDtypeStruct((tp, d), F32),
                   jax.ShapeDtypeStruct((tp, d), BF16), jax.ShapeDtypeStruct((1, d), F32)],
        scratch_shapes=[pltpu.VMEM((tr, d), F32), pltpu.SemaphoreType.DMA],
        compiler_params=_cparams("arbitrary"))(h1, dn, target, g)


def _mix_fwd(co, y, z, gc, gs, name):
    tp, dh = co.shape
    tr = tp // ROW_TILES

    def body(co_ref, y_ref, z_ref, gc_ref, gs_ref, m_ref):
        c = co_ref[...]
        m_ref[:, :dh] = (c * _rms(c) * gc_ref[...]).astype(BF16)
        so = _gelu(y_ref[...]) * _sigmoid(z_ref[...])
        m_ref[:, dh:] = (so * _rms(so) * gs_ref[...]).astype(BF16)

    return pl.pallas_call(
        body, name=name, grid=(ROW_TILES,),
        in_specs=[_rows(dh, tr)] * 3 + [_const((1, dh))] * 2,
        out_specs=_rows(2 * dh, tr),
        out_shape=jax.ShapeDtypeStruct((tp, 2 * dh), BF16),
        compiler_params=_cparams("parallel"))(co, y, z, gc, gs)


def _mix_bwd(dm, co, y, z, gc, gs, name):
    tp, dh = co.shape
    tr = tp // ROW_TILES

    def body(dm_ref, co_ref, y_ref, z_ref, gc_ref, gs_ref, dco_ref, dz_ref, dgp_ref, dgc_ref, dgs_ref):
        c = co_ref[...]
        dco, dgc = _rms_bwd(c, _rms(c), gc_ref[...], dm_ref[:, :dh])
        dco_ref[...] = dco
        gl = _gelu(y_ref[...])
        sg = _sigmoid(z_ref[...])
        so = gl * sg
        dso, dgs = _rms_bwd(so, _rms(so), gs_ref[...], dm_ref[:, dh:])
        dz_ref[...] = (dso * gl * sg * (1.0 - sg)).astype(BF16)
        dgp_ref[...] = dso * sg

        @pl.when(pl.program_id(0) == 0)
        def _():
            dgc_ref[...] = jnp.zeros_like(dgc_ref)
            dgs_ref[...] = jnp.zeros_like(dgs_ref)

        dgc_ref[...] += jnp.sum(dgc, axis=0, keepdims=True)
        dgs_ref[...] += jnp.sum(dgs, axis=0, keepdims=True)

    return pl.pallas_call(
        body, name=name, grid=(ROW_TILES,),
        in_specs=[_rows(2 * dh, tr)] + [_rows(dh, tr)] * 3 + [_const((1, dh))] * 2,
        out_specs=[_rows(dh, tr), _rows(dh, tr), _rows(dh, tr), _const((1, dh)), _const((1, dh))],
        out_shape=[jax.ShapeDtypeStruct((tp, dh), F32), jax.ShapeDtypeStruct((tp, dh), BF16),
                   jax.ShapeDtypeStruct((tp, dh), F32), jax.ShapeDtypeStruct((1, dh), F32),
                   jax.ShapeDtypeStruct((1, dh), F32)],
        compiler_params=_cparams("arbitrary"))(dm, co, y, z, gc, gs)


def _shift_down(x, k):
    row = lax.broadcasted_iota(jnp.int32, x.shape, 0)
    return jnp.where(row >= k, pltpu.roll(x, k, 0), 0.0)


def _shift_up(x, k):
    n = x.shape[0]
    row = lax.broadcasted_iota(jnp.int32, x.shape, 0)
    return jnp.where(row < n - k, pltpu.roll(x, n - k, 0), 0.0)


def _dwconv(x, w_ref):
    return w_ref[2:3, :] * x + w_ref[1:2, :] * _shift_down(x, 1) + w_ref[0:1, :] * _shift_down(x, 2)


def _dwconv_bwd(x, dy, w_ref):
    dx = w_ref[2:3, :] * dy + w_ref[1:2, :] * _shift_up(dy, 1) + w_ref[0:1, :] * _shift_up(dy, 2)
    dw = jnp.concatenate([jnp.sum(dy * _shift_down(x, 2), axis=0, keepdims=True),
                          jnp.sum(dy * _shift_down(x, 1), axis=0, keepdims=True),
                          jnp.sum(dy * x, axis=0, keepdims=True)], axis=0)
    return dx, dw


def _scan(s_re, s_im, tab_ref, reverse):
    n_chunks = s_re.shape[0] // SUBLANES
    n_strips = s_re.shape[1] // LANES
    last = 0 if reverse else SUBLANES - 1

    def body(i, carry):
        chunk = (n_chunks - 1 - i) if reverse else i
        r0 = pl.multiple_of(chunk * SUBLANES, SUBLANES)
        out = []
        for st in range(n_strips):
            lanes = slice(st * LANES, (st + 1) * LANES)
            cr, ci = carry[2 * st], carry[2 * st + 1]
            xr = s_re[pl.ds(r0, SUBLANES), lanes]
            xi = s_im[pl.ds(r0, SUBLANES), lanes]
            for level, k in enumerate((1, 2, 4)):
                mr = tab_ref[2 * level, :, lanes]
                mi = tab_ref[2 * level + 1, :, lanes]
                sh = SUBLANES - k if reverse else k
                rr = pltpu.roll(xr, sh, 0)
                ri = pltpu.roll(xi, sh, 0)
                xr, xi = xr + (mr * rr - mi * ri), xi + (mr * ri + mi * rr)
            pwr = tab_ref[6, :, lanes]
            pwi = tab_ref[7, :, lanes]
            xr, xi = xr + (pwr * cr - pwi * ci), xi + (pwr * ci + pwi * cr)
            s_re[pl.ds(r0, SUBLANES), lanes] = xr
            s_im[pl.ds(r0, SUBLANES), lanes] = xi
            out.append(jnp.broadcast_to(xr[last:last + 1, :], (SUBLANES, LANES)))
            out.append(jnp.broadcast_to(xi[last:last + 1, :], (SUBLANES, LANES)))
        return tuple(out)

    zero = jnp.zeros((SUBLANES, LANES), F32)
    lax.fori_loop(0, n_chunks, body, (zero,) * (2 * n_strips))


def _seq_fwd(proj, conv_w, bc_re, bc_im, cc_re, cc_im, dskip, tab_f, name):
    tp = proj.shape[0]
    dh = proj.shape[1] // 4
    nq = dh // LANES
    sw = STATE * N_GROUPS // nq

    def body(b_ref, c_ref, v_ref, u_ref, w_ref, bre_ref, bim_ref, cre_ref, cim_ref, d_ref, tab_ref,
             co_ref, y_ref, g_ref, s_re, s_im):
        co_ref[...] = b_ref[...] * _dwconv(c_ref[...] * v_ref[...], w_ref)
        u = u_ref[...]
        ub = u.astype(BF16)
        s_re[...] = jnp.dot(ub, bre_ref[...], preferred_element_type=F32)
        s_im[...] = jnp.dot(ub, bim_ref[...], preferred_element_type=F32)
        _scan(s_re, s_im, tab_ref, False)
        y = (jnp.dot(s_re[...].astype(BF16), cre_ref[...], preferred_element_type=F32)
             - jnp.dot(s_im[...].astype(BF16), cim_ref[...], preferred_element_type=F32)
             + d_ref[...] * u)
        y_ref[...] = y
        g_ref[...] = _gelu(y).astype(BF16)

    col = lambda off: pl.BlockSpec((tp, LANES), lambda q, off=off: (0, off * nq + q))
    blk = pl.BlockSpec((tp, LANES), lambda q: (0, q))
    return pl.pallas_call(
        body, name=name, grid=(nq,),
        in_specs=[col(0), col(1), col(2), col(3),
                  pl.BlockSpec((3, LANES), lambda q: (0, q)),
                  pl.BlockSpec((LANES, sw), lambda q: (0, q)), pl.BlockSpec((LANES, sw), lambda q: (0, q)),
                  pl.BlockSpec((sw, LANES), lambda q: (q, 0)), pl.BlockSpec((sw, LANES), lambda q: (q, 0)),
                  pl.BlockSpec((1, LANES), lambda q: (0, q)),
                  pl.BlockSpec((8, SUBLANES, sw), lambda q: (0, 0, q))],
        out_specs=[blk, blk, blk],
        out_shape=[jax.ShapeDtypeStruct((tp, dh), F32), jax.ShapeDtypeStruct((tp, dh), F32),
                   jax.ShapeDtypeStruct((tp, dh), BF16)],
        scratch_shapes=[pltpu.VMEM((tp, sw), F32), pltpu.VMEM((tp, sw), F32)],
        compiler_params=_cparams("parallel"),
    )(proj, proj, proj, proj, conv_w, bc_re, bc_im, cc_re, cc_im, dskip, tab_f)


def _conv_bwd(proj, dco, conv_w, name):
    tp = proj.shape[0]
    dh = proj.shape[1] // 4
    nq = dh // LANES

    def body(b_ref, c_ref, v_ref, dco_ref, w_ref, dproj_ref, dw_ref, stage, sem):
        q = pl.program_id(0)
        cg = c_ref[...]
        vg = v_ref[...]
        cv = cg * vg
        dco_v = dco_ref[...]
        dcv, dw = _dwconv_bwd(cv, dco_v * b_ref[...], w_ref)
        dw_ref[...] = dw
        stage[0] = (dco_v * _dwconv(cv, w_ref)).astype(BF16)
        stage[1] = (dcv * vg).astype(BF16)
        stage[2] = (dcv * cg).astype(BF16)
        copies = [pltpu.make_async_copy(stage.at[p], dproj_ref.at[:, pl.ds((p * nq + q) * LANES, LANES)], sem.at[p])
                  for p in range(3)]
        for cp in copies:
            cp.start()
        for cp in copies:
            cp.wait()

    col = lambda off: pl.BlockSpec((tp, LANES), lambda q, off=off: (0, off * nq + q))
    return pl.pallas_call(
        body, name=name, grid=(nq,),
        in_specs=[col(0), col(1), col(2), pl.BlockSpec((tp, LANES), lambda q: (0, q)),
                  pl.BlockSpec((3, LANES), lambda q: (0, q))],
        out_specs=[pl.BlockSpec(memory_space=pl.ANY), pl.BlockSpec((3, LANES), lambda q: (0, q))],
        out_shape=[jax.ShapeDtypeStruct((tp, 4 * dh), BF16), jax.ShapeDtypeStruct((3, dh), F32)],
        scratch_shapes=[pltpu.VMEM((3, tp, LANES), BF16), pltpu.SemaphoreType.DMA((3,))],
        compiler_params=_cparams("arbitrary"),
    )(proj, proj, proj, dco, conv_w)


def _ssm_bwd(proj, y, dg, dproj, bc_re, bc_im, cc_re, cc_im, dskip, tab_f, tab_r, name):
    tp = proj.shape[0]
    dh = proj.shape[1] // 4
    nq = dh // LANES
    sw = STATE * N_GROUPS // nq

    def body(u_ref, y_ref, dg_ref, dproj_in, bre_ref, bim_ref, cre_ref, cim_ref, d_ref, tabf_ref, tabr_ref,
             dproj_ref, dbre_ref, dbim_ref, dcre_ref, dcim_ref, dd_ref, dar_ref, dai_ref,
             s_re, s_im, l_re, l_im, stage, sem):
        del dproj_in
        q = pl.program_id(0)
        nt = (((1,), (1,)), ((), ()))
        tn = (((0,), (0,)), ((), ()))
        u = u_ref[...]
        ub = u.astype(BF16)
        s_re[...] = jnp.dot(ub, bre_ref[...], preferred_element_type=F32)
        s_im[...] = jnp.dot(ub, bim_ref[...], preferred_element_type=F32)
        _scan(s_re, s_im, tabf_ref, False)
        dy = dg_ref[...] * _gelu_grad(y_ref[...])
        dyb = dy.astype(BF16)
        dd_ref[...] = jnp.sum(dy * u, axis=0, keepdims=True)
        l_re[...] = lax.dot_general(dyb, cre_ref[...], nt, preferred_element_type=F32)
        l_im[...] = -lax.dot_general(dyb, cim_ref[...], nt, preferred_element_type=F32)
        dcre_ref[...] = lax.dot_general(s_re[...].astype(BF16), dyb, tn, preferred_element_type=F32)
        dcim_ref[...] = -lax.dot_general(s_im[...].astype(BF16), dyb, tn, preferred_element_type=F32)
        _scan(l_re, l_im, tabr_ref, True)
        for st in range(sw // LANES):
            lanes = slice(st * LANES, (st + 1) * LANES)
            lr = l_re[:, lanes]
            li = l_im[:, lanes]
            pr = _shift_down(s_re[:, lanes], 1)
            pi = _shift_down(s_im[:, lanes], 1)
            dar_ref[:, lanes] = jnp.sum(lr * pr + li * pi, axis=0, keepdims=True)
            dai_ref[:, lanes] = jnp.sum(li * pr - lr * pi, axis=0, keepdims=True)
        lrb = l_re[...].astype(BF16)
        lib = l_im[...].astype(BF16)
        du = (dy * d_ref[...] + lax.dot_general(lrb, bre_ref[...], nt, preferred_element_type=F32)
              + lax.dot_general(lib, bim_ref[...], nt, preferred_element_type=F32))
        stage[...] = du.astype(BF16)
        dbre_ref[...] = lax.dot_general(ub, lrb, tn, preferred_element_type=F32)
        dbim_ref[...] = lax.dot_general(ub, lib, tn, preferred_element_type=F32)
        cp = pltpu.make_async_copy(stage, dproj_ref.at[:, pl.ds((3 * nq + q) * LANES, LANES)], sem)
        cp.start()
        cp.wait()

    blk = pl.BlockSpec((tp, LANES), lambda q: (0, q))
    bspec = pl.BlockSpec((LANES, sw), lambda q: (0, q))
    cspec = pl.BlockSpec((sw, LANES), lambda q: (q, 0))
    tspec = pl.BlockSpec((8, SUBLANES, sw), lambda q: (0, 0, q))
    nstate = STATE * N_GROUPS
    return pl.pallas_call(
        body, name=name, grid=(nq,),
        in_specs=[pl.BlockSpec((tp, LANES), lambda q: (0, 3 * nq + q)), blk, blk, pl.BlockSpec(memory_space=pl.ANY),
                  bspec, bspec, cspec, cspec, pl.BlockSpec((1, LANES), lambda q: (0, q)), tspec, tspec],
        out_specs=[pl.BlockSpec(memory_space=pl.ANY), bspec, bspec, cspec, cspec,
                   pl.BlockSpec((1, LANES), lambda q: (0, q)),
                   pl.BlockSpec((1, sw), lambda q: (0, q)), pl.BlockSpec((1, sw), lambda q: (0, q))],
        out_shape=[jax.ShapeDtypeStruct((tp, 4 * dh), BF16),
                   jax.ShapeDtypeStruct((LANES, nstate), F32), jax.ShapeDtypeStruct((LANES, nstate), F32),
                   jax.ShapeDtypeStruct((nstate, LANES), F32), jax.ShapeDtypeStruct((nstate, LANES), F32),
                   jax.ShapeDtypeStruct((1, dh), F32),
                   jax.ShapeDtypeStruct((1, nstate), F32), jax.ShapeDtypeStruct((1, nstate), F32)],
        input_output_aliases={3: 0},
        scratch_shapes=[pltpu.VMEM((tp, sw), F32)] * 4 + [pltpu.VMEM((tp, LANES), BF16), pltpu.SemaphoreType.DMA],
        compiler_params=_cparams("arbitrary"),
    )(proj, y, dg, dproj, bc_re, bc_im, cc_re, cc_im, dskip, tab_f, tab_r)


FFN_TILE = 256


def _ffn_act(up, fw, fb, name):
    tp, two_ff = up.shape
    dff = two_ff // 2
    tc = FFN_TILE
    nj = dff // tc

    def body(ua_ref, uv_ref, wa_ref, wv_ref, ba_ref, bv_ref, act_ref):
        a = _dwconv(ua_ref[...], wa_ref) + ba_ref[...]
        v = _dwconv(uv_ref[...], wv_ref) + bv_ref[...]
        act_ref[...] = (a * _sigmoid(a) * v).astype(BF16)

    lo = lambda r: pl.BlockSpec((r, tc), lambda j: (0, j))
    hi = lambda r: pl.BlockSpec((r, tc), lambda j: (0, nj + j))
    return pl.pallas_call(
        body, name=name, grid=(nj,),
        in_specs=[lo(tp), hi(tp), lo(3), hi(3), lo(1), hi(1)],
        out_specs=lo(tp),
        out_shape=jax.ShapeDtypeStruct((tp, dff), BF16),
        compiler_params=_cparams("parallel"))(up, up, fw, fw, fb, fb)


def _ffn_bwd(up, dact, fw, fb, name):
    tp, two_ff = up.shape
    dff = two_ff // 2
    tc = FFN_TILE
    nj = dff // tc

    def body(ua_ref, uv_ref, da_ref, wa_ref, wv_ref, ba_ref, bv_ref,
             dup_ref, dwa_ref, dwv_ref, dba_ref, dbv_ref, stage, sem):
        j = pl.program_id(0)
        ua = ua_ref[...]
        uv = uv_ref[...]
        a = _dwconv(ua, wa_ref) + ba_ref[...]
        v = _dwconv(uv, wv_ref) + bv_ref[...]
        sg = _sigmoid(a)
        dact_v = da_ref[...]
        da = dact_v * v * sg * (1.0 + a * (1.0 - sg))
        dv = dact_v * a * sg
        dba_ref[...] = jnp.sum(da, axis=0, keepdims=True)
        dbv_ref[...] = jnp.sum(dv, axis=0, keepdims=True)
        dua, dwa = _dwconv_bwd(ua, da, wa_ref)
        duv, dwv = _dwconv_bwd(uv, dv, wv_ref)
        dwa_ref[...] = dwa
        dwv_ref[...] = dwv
        stage[0] = dua.astype(BF16)
        stage[1] = duv.astype(BF16)
        copies = [pltpu.make_async_copy(stage.at[p], dup_ref.at[:, pl.ds((p * nj + j) * tc, tc)], sem.at[p])
                  for p in range(2)]
        for cp in copies:
            cp.start()
        for cp in copies:
            cp.wait()

    lo = lambda r: pl.BlockSpec((r, tc), lambda j: (0, j))
    hi = lambda r: pl.BlockSpec((r, tc), lambda j: (0, nj + j))
    return pl.pallas_call(
        body, name=name, grid=(nj,),
        in_specs=[lo(tp), hi(tp), lo(tp), lo(3), hi(3), lo(1), hi(1)],
        out_specs=[pl.BlockSpec(memory_space=pl.ANY), lo(3), lo(3), lo(1), lo(1)],
        out_shape=[jax.ShapeDtypeStruct((tp, two_ff), BF16),
                   jax.ShapeDtypeStruct((3, dff), F32), jax.ShapeDtypeStruct((3, dff), F32),
                   jax.ShapeDtypeStruct((1, dff), F32), jax.ShapeDtypeStruct((1, dff), F32)],
        scratch_shapes=[pltpu.VMEM((2, tp, tc), BF16), pltpu.SemaphoreType.DMA((2,))],
        compiler_params=_cparams("arbitrary"))(up, up, dact, fw, fw, fb, fb)


def _zoh(lr, li, ld):
    dt = jnp.exp(ld)
    mag = jnp.exp(lr * dt)
    ang = li * dt
    ar = mag * jnp.cos(ang)
    ai = mag * jnp.sin(ang)
    den = lr * lr + li * li
    nr = ar - 1.0
    fr = (nr * lr + ai * li) / den
    fi = (ai * lr - nr * li) / den
    return dt, ar, ai, den, nr, fr, fi


def _s5_prep(lr, li, ld, b_re, b_im, name):
    nstate = lr.shape[1]

    def tables(tab_ref, ar, ai, reverse):
        pows = [(ar, ai)]
        for _ in range(SUBLANES - 1):
            pr, pi = pows[-1]
            pows.append((pr * ar - pi * ai, pr * ai + pi * ar))
        row = lax.broadcasted_iota(jnp.int32, (SUBLANES, nstate), 0)
        for level, k in enumerate((1, 2, 4)):
            mask = (row <= SUBLANES - 1 - k) if reverse else (row >= k)
            tab_ref[2 * level] = jnp.where(mask, pows[k - 1][0], 0.0)
            tab_ref[2 * level + 1] = jnp.where(mask, pows[k - 1][1], 0.0)
        pr = jnp.zeros((SUBLANES, nstate), F32)
        pi = jnp.zeros((SUBLANES, nstate), F32)
        for t in range(SUBLANES):
            k = SUBLANES - 1 - t if reverse else t
            pr = jnp.where(row == t, pows[k][0], pr)
            pi = jnp.where(row == t, pows[k][1], pi)
        tab_ref[6] = pr
        tab_ref[7] = pi

    def body(lr_ref, li_ref, ld_ref, bre_ref, bim_ref, tabf_ref, tabr_ref, bcre_ref, bcim_ref):
        _, ar, ai, _, _, fr, fi = _zoh(lr_ref[...], li_ref[...], ld_ref[...])
        tables(tabf_ref, ar, ai, False)
        tables(tabr_ref, ar, -ai, True)
        bre = bre_ref[...]
        bim = bim_ref[...]
        bcre_ref[...] = (fr * bre - fi * bim).astype(BF16)
        bcim_ref[...] = (fr * bim + fi * bre).astype(BF16)

    vmem = pl.BlockSpec(memory_space=pltpu.VMEM)
    return pl.pallas_call(
        body, name=name, in_specs=[vmem] * 5, out_specs=[vmem] * 4,
        out_shape=[jax.ShapeDtypeStruct((8, SUBLANES, nstate), F32)] * 2
        + [jax.ShapeDtypeStruct(b_re.shape, BF16)] * 2)(lr, li, ld, b_re, b_im)


def _s5_prep_bwd(lr, li, ld, b_re, b_im, da_re, da_im, dbc_re, dbc_im, name):
    def body(lr_ref, li_ref, ld_ref, bre_ref, bim_ref, dar_ref, dai_ref, dbcre_ref, dbcim_ref,
             dlr_ref, dli_ref, dld_ref, dbre_ref, dbim_ref):
        lr, li = lr_ref[...], li_ref[...]
        dt, ar, ai, den, nr, fr, fi = _zoh(lr, li, ld_ref[...])
        bre, bim = bre_ref[...], bim_ref[...]
        gre, gim = dbcre_ref[...], dbcim_ref[...]
        dbre_ref[...] = fr * gre + fi * gim
        dbim_ref[...] = fr * gim - fi * gre
        g_fr = jnp.sum(gre * bre + gim * bim, axis=0, keepdims=True)
        g_fi = jnp.sum(gim * bre - gre * bim, axis=0, keepdims=True)
        g_ar = dar_ref[...] + (g_fr * lr - g_fi * li) / den
        g_ai = dai_ref[...] + (g_fr * li + g_fi * lr) / den
        d_lr = (g_fr * (nr - 2.0 * fr * lr) + g_fi * (ai - 2.0 * fi * lr)) / den
        d_li = (g_fr * (ai - 2.0 * fr * li) - g_fi * (nr + 2.0 * fi * li)) / den
        g_logmag = g_ar * ar + g_ai * ai
        g_ang = g_ai * ar - g_ar * ai
        dlr_ref[...] = d_lr + g_logmag * dt
        dli_ref[...] = d_li + g_ang * dt
        d_ld = (g_logmag * lr + g_ang * li) * dt
        n = d_ld.shape[1]
        sh = 1
        while sh < STATE:
            d_ld = d_ld + pltpu.roll(d_ld, n - sh, 1)
            sh *= 2
        dld_ref[...] = d_ld

    vmem = pl.BlockSpec(memory_space=pltpu.VMEM)
    row = jax.ShapeDtypeStruct(lr.shape, F32)
    return pl.pallas_call(
        body, name=name, in_specs=[vmem] * 9, out_specs=[vmem] * 5,
        out_shape=[row, row, row, jax.ShapeDtypeStruct(b_re.shape, F32), jax.ShapeDtypeStruct(b_re.shape, F32)],
    )(lr, li, ld, b_re, b_im, da_re, da_im, dbc_re, dbc_im)


def _compact_b(bb):
    bq = bb.reshape(N_GROUPS // 8, 8, STATE, GROUP)
    m = jnp.einsum("ab,qbph->qahbp", jnp.eye(8, dtype=bb.dtype), bq).reshape(N_GROUPS // 8, LANES, 8 * STATE)
    return m.transpose(1, 0, 2).reshape(LANES, N_GROUPS * STATE)


def _expand_b(m):
    d = m.reshape(8, GROUP, N_GROUPS // 8, 8, STATE)
    return jnp.einsum("ahqap->qahp", d).reshape(N_GROUPS, GROUP, STATE)


def _compact_c(c):
    cq = c.reshape(N_GROUPS // 8, 8, GROUP, STATE)
    return jnp.einsum("ab,qbhp->qbpah", jnp.eye(8, dtype=c.dtype), cq).reshape(N_GROUPS * STATE, LANES)


def _expand_c(m):
    d = m.reshape(N_GROUPS // 8, 8, STATE, 8, GROUP)
    return jnp.einsum("qbpbh->qbhp", d).reshape(N_GROUPS, GROUP, STATE)


def _local_step(x, target, p, ex):
    seq, d = x.shape
    n_real = N_META + seq
    tp = -(-n_real // ROW_ALIGN) * ROW_ALIGN

    nstate = N_GROUPS * STATE
    s5 = (p["ssm_lam_re"].reshape(1, nstate), p["ssm_lam_im"].reshape(1, nstate),
          jnp.repeat(p["ssm_log_dt"].reshape(-1), STATE).reshape(1, nstate),
          _compact_b(p["ssm_b_re"]), _compact_b(p["ssm_b_im"]))
    tab_f, tab_r, bc_re, bc_im = _s5_prep(*s5, "s5_prep")
    cc_re = _compact_c(p["ssm_c_re"]).astype(BF16)
    cc_im = _compact_c(p["ssm_c_im"]).astype(BF16)
    dskip = p["ssm_d"].reshape(1, -1)
    dh = dskip.shape[1]

    h0, hn1 = _input_norm_fwd(x, p["meta_tokens"], p["norm_mix_g"] + ex.zero, tp, "norm_mix")
    proj = _mm(hn1, p["w_in"], "nn", "proj")
    co, y, g = _seq_fwd(proj, p["conv_w"], bc_re, bc_im, cc_re, cc_im, dskip, tab_f, "seq_fwd")
    mid = ex.weights("mid", g)
    z = _mm(g, mid["ssm_w_glu"], "nn", "glu")
    mixed = _mix_fwd(co, y, z, p["gain_conv_out"], p["gain_ssm_out"], "mix_fwd")
    mo = _mm(mixed, mid["w_out"], "nn", "out_proj")
    h1, hn2 = _norm_fwd(h0, p["norm_ffn_g"], "norm_ffn", res=mo)
    late = ex.weights("late", hn2)
    up = _mm(hn2, late["w_up"], "nn", "up_proj")
    act = _ffn_act(up, p["ffn_conv_w"], p["ffn_conv_b"], "ffn_act")
    dn = _mm(act, late["w_down"], "nn", "down_proj")
    loss, dh2, dh2b, d_gfin = _loss_bwd(h1, dn, target, p["norm_final_g"], n_real, "loss_bwd")

    g_w_down = _mm(act, dh2b, "tn", "g_w_down")
    dact = _mm(dh2b, late["w_down"], "nt", "d_act")
    dup, dfw_a, dfw_v, dfb_a, dfb_v = _ffn_bwd(up, dact, p["ffn_conv_w"], p["ffn_conv_b"], "ffn_bwd")
    g_w_up = _mm(hn2, dup, "tn", "g_w_up")
    started = ex.grads_ready("late", {"w_up": g_w_up, "w_down": g_w_down})
    dhn2 = _mm(dup, late["w_up"], "nt", "d_hn2", after=started)
    started = ex.grads_send("late", dhn2)
    dh1, dh1b, d_gffn = _norm_bwd(h1, p["norm_ffn_g"] + started[0, 0], dhn2, dh2, "norm_ffn_bwd")
    g_w_out = _mm(mixed, dh1b, "tn", "g_w_out")
    dmixed = _mm(dh1b, mid["w_out"], "nt", "d_mixed")
    dco, dz, dgp, d_gc, d_gs = _mix_bwd(dmixed, co, y, z, p["gain_conv_out"], p["gain_ssm_out"], "mix_bwd")
    g_w_glu = _mm(g, dz, "tn", "g_w_glu")
    started = ex.grads_ready("mid", {"ssm_w_glu": g_w_glu, "w_out": g_w_out})
    dg = _mm(dz, mid["ssm_w_glu"], "nt", "d_gelu", acc_in=dgp, after=started)
    started = ex.grads_send("mid", dg)
    dproj, d_conv_w = _conv_bwd(proj, dco, p["conv_w"] + started[0, 0], "conv_bwd")
    (dproj, dbc_re, dbc_im, dcc_re, dcc_im, d_dskip, da_re, da_im) = _ssm_bwd(
        proj, y, dg, dproj, bc_re, bc_im, cc_re, cc_im, dskip, tab_f, tab_r, "ssm_bwd")
    g_w_in = _mm(hn1, dproj, "tn", "g_w_in")
    started = ex.grads_ready("first", {"w_in": g_w_in})
    dhn1 = _mm(dproj, p["w_in"], "nt", "d_hn1", after=started)
    started = ex.grads_send("first", dhn1)
    grad_x, d_meta, d_gmix = _input_norm_bwd(h0, p["norm_mix_g"] + started[0, 0], dhn1, dh1, n_real, "norm_mix_bwd")

    d_lam_re, d_lam_im, d_log_dt, d_b_re, d_b_im = _s5_prep_bwd(*s5, da_re, da_im, dbc_re, dbc_im, "s5_prep_bwd")
    d_lam_re, d_lam_im = d_lam_re.reshape(N_GROUPS, STATE), d_lam_im.reshape(N_GROUPS, STATE)
    d_log_dt = d_log_dt[0, ::STATE]
    d_b_re, d_b_im = _expand_b(d_b_re), _expand_b(d_b_im)
    grads = {
        "meta_tokens": d_meta, "norm_mix_g": d_gmix, "w_in": g_w_in, "conv_w": d_conv_w,
        "ssm_lam_re": d_lam_re, "ssm_lam_im": d_lam_im, "ssm_log_dt": d_log_dt,
        "ssm_b_re": d_b_re, "ssm_b_im": d_b_im, "ssm_c_re": _expand_c(dcc_re), "ssm_c_im": _expand_c(dcc_im),
        "ssm_d": d_dskip.reshape(N_GROUPS, GROUP), "ssm_w_glu": g_w_glu,
        "gain_conv_out": d_gc, "gain_ssm_out": d_gs, "w_out": g_w_out, "norm_ffn_g": d_gffn,
        "w_up": g_w_up, "ffn_conv_w": jnp.concatenate([dfw_a, dfw_v], axis=1),
        "ffn_conv_b": jnp.concatenate([dfb_a, dfb_v], axis=1), "w_down": g_w_down, "norm_final_g": d_gfin,
    }
    return loss[0, 0], grad_x, grads


def _view(ref, axis, start, size):
    idx = [slice(None)] * len(ref.shape)
    idx[axis] = pl.ds(start, size)
    return ref.at[tuple(idx)]


def _exchange(name, ins, outs, aliases, local_copies, remote_copies):
    ni, no = len(ins), len(outs)
    nl, nr = len(local_copies), len(remote_copies)

    def body(*refs):
        in_refs, out_refs = refs[:ni], refs[ni:ni + no]
        send_sems, recv_sems, local_sems = refs[ni + no:]
        x, y, c = lax.axis_index("x"), lax.axis_index("y"), lax.axis_index("c")
        pos = (x, y, c, 2 * x + y)
        locals_ = [pltpu.make_async_copy(s(in_refs, out_refs, pos), d(in_refs, out_refs, pos), local_sems.at[i])
                   for i, (s, d) in enumerate(local_copies)]
        remotes = []
        for i, (s, d, flip) in enumerate(remote_copies):
            peer = (1 - x if "x" in flip else x, 1 - y if "y" in flip else y, 1 - c if "c" in flip else c)
            remotes.append(pltpu.make_async_remote_copy(
                src_ref=s(in_refs, out_refs, pos), dst_ref=d(in_refs, out_refs, pos),
                send_sem=send_sems.at[i], recv_sem=recv_sems.at[i], device_id=peer, device_id_type=MESH))
        for cp in locals_ + remotes:
            cp.start()
        for cp in remotes:
            cp.wait_recv()
        for cp in remotes:
            cp.wait_send()
        for cp in locals_:
            cp.wait()

    hbm = pl.BlockSpec(memory_space=pl.ANY)
    return pl.pallas_call(
        body, name=name, in_specs=[hbm] * ni, out_specs=[hbm] * no, out_shape=outs,
        input_output_aliases=aliases,
        scratch_shapes=[pltpu.SemaphoreType.DMA((nr,)), pltpu.SemaphoreType.DMA((nr,)),
                        pltpu.SemaphoreType.DMA((max(nl, 1),))],
    )(*ins)


BIG = {"w_in": (0, 1), "ssm_w_glu": (1, 0), "w_out": (1, 0), "w_up": (0, 1), "w_down": (1, 0)}
BIG_NAMES = tuple(BIG)
FLIPS = ("y", "x", "xy")


def _peer_chip(pos, flip):
    x, y, _, _ = pos
    return 2 * (1 - x if "x" in flip else x) + (1 - y if "y" in flip else y)


def _block_rows(rows, cols, itemsize, mult):
    return _pick_tile(rows, max(mult, (2 * 1024 * 1024) // (cols * itemsize)), mult)


def _cast_into_full(w, kc, shard_axis, name):
    r, cdim = w.shape
    tr = _block_rows(r, cdim, 4, 16)
    nb = r // tr

    def body(kc_ref, w_ref, o_ref):
        o_ref[...] = w_ref[...].astype(BF16)

    if shard_axis == 1:
        full, o_spec = (r, 4 * cdim), pl.BlockSpec((tr, cdim), lambda i, kc: (i, kc[0]))
    else:
        full, o_spec = (4 * r, cdim), pl.BlockSpec((tr, cdim), lambda i, kc: (kc[0] * nb + i, 0))
    return pl.pallas_call(
        body, name=name,
        grid_spec=pltpu.PrefetchScalarGridSpec(
            num_scalar_prefetch=1, grid=(nb,), in_specs=[pl.BlockSpec((tr, cdim), lambda i, kc: (i, 0))],
            out_specs=o_spec),
        out_shape=jax.ShapeDtypeStruct(full, BF16), compiler_params=_cparams("parallel"))(kc, w)


def _pair_sum(g, recv, kc, half_axis, name, out_dtype):
    hr, hc = recv.shape
    tr = _block_rows(hr, hc, 4, 16)
    nb = hr // tr

    def body(kc_ref, g_ref, r_ref, o_ref):
        o_ref[...] = (g_ref[...] + r_ref[...]).astype(out_dtype)

    if half_axis == 0:
        g_spec = pl.BlockSpec((tr, hc), lambda i, kc: (kc[1] * nb + i, 0))
    elif half_axis == 1:
        g_spec = pl.BlockSpec((tr, hc), lambda i, kc: (i, kc[1]))
    else:
        g_spec = pl.BlockSpec((tr, hc), lambda i, kc: (i, 0))
    same = pl.BlockSpec((tr, hc), lambda i, kc: (i, 0))
    return pl.pallas_call(
        body, name=name,
        grid_spec=pltpu.PrefetchScalarGridSpec(num_scalar_prefetch=1, grid=(nb,), in_specs=[g_spec, same],
                                               out_specs=same),
        out_shape=jax.ShapeDtypeStruct((hr, hc), out_dtype), compiler_params=_cparams("parallel"))(kc, g, recv)


def _chip_sum(own, recv, kc, own_axis, out_axis, name):
    _, sr, sc = recv.shape
    tr = _block_rows(sr, sc, 4, 16)
    nb = sr // tr

    def body(kc_ref, o_ref, r_ref, t_ref):
        k = kc_ref[0]
        own_v = o_ref[...].astype(F32)
        r = [r_ref[m].astype(F32) for m in range(3)]
        terms = []
        for kk in range(4):
            m = jnp.bitwise_xor(k, kk)
            terms.append(jnp.where(m == 0, own_v, jnp.where(m == 1, r[0], jnp.where(m == 2, r[1], r[2]))))
        t_ref[...] = (terms[0] + terms[1]) + (terms[2] + terms[3])

    if own_axis == 0:
        own_spec = pl.BlockSpec((tr, sc), lambda i, kc: (kc[0] * nb + i, 0))
    elif own_axis == 1:
        own_spec = pl.BlockSpec((tr, sc), lambda i, kc: (i, kc[0]))
    else:
        own_spec = pl.BlockSpec((tr, sc), lambda i, kc: (kc[1] * nb + i, 0))
    if out_axis == 0:
        out_full, out_spec = (2 * sr, sc), pl.BlockSpec((tr, sc), lambda i, kc: (kc[1] * nb + i, 0))
    else:
        out_full, out_spec = (sr, 2 * sc), pl.BlockSpec((tr, sc), lambda i, kc: (i, kc[1]))
    return pl.pallas_call(
        body, name=name,
        grid_spec=pltpu.PrefetchScalarGridSpec(
            num_scalar_prefetch=1, grid=(nb,),
            in_specs=[own_spec, pl.BlockSpec((3, tr, sc), lambda i, kc: (0, i, 0))],
            out_specs=out_spec),
        out_shape=jax.ShapeDtypeStruct(out_full, F32), compiler_params=_cparams("parallel"))(kc, own, recv)


def _adamw(w, g, m, v, name):
    r, cdim = w.shape
    tr = _block_rows(r, cdim, 4, 8)
    c1 = 1.0 - ADAM_B1 ** ADAM_STEP
    c2 = 1.0 - ADAM_B2 ** ADAM_STEP

    def body(w_ref, g_ref, m_ref, v_ref, go_ref, d_ref, nm_ref, nv_ref):
        gv = g_ref[...]
        go_ref[...] = gv
        nm = ADAM_B1 * m_ref[...] + (1.0 - ADAM_B1) * gv
        nv = ADAM_B2 * v_ref[...] + (1.0 - ADAM_B2) * (gv * gv)
        d_ref[...] = -ADAM_LR * ((nm / c1) / (jnp.sqrt(nv / c2) + ADAM_EPS) + ADAM_WD * w_ref[...])
        nm_ref[...] = nm
        nv_ref[...] = nv

    spec = _rows(cdim, tr)
    return pl.pallas_call(body, name=name, grid=(r // tr,), in_specs=[spec] * 4, out_specs=[spec] * 4,
                          out_shape=[jax.ShapeDtypeStruct((r, cdim), F32)] * 4,
                          compiler_params=_cparams("parallel"))(w, g, m, v)


def _adamw_whole(ws, gs, ms, vs, name):
    n = len(ws)
    c1 = 1.0 - ADAM_B1 ** ADAM_STEP
    c2 = 1.0 - ADAM_B2 ** ADAM_STEP

    def body(*refs):
        for i in range(n):
            w_ref, g_ref, m_ref, v_ref, d_ref, nm_ref, nv_ref = [refs[j * n + i] for j in range(7)]
            gv = g_ref[...]
            nm = ADAM_B1 * m_ref[...] + (1.0 - ADAM_B1) * gv
            nv = ADAM_B2 * v_ref[...] + (1.0 - ADAM_B2) * (gv * gv)
            d_ref[...] = -ADAM_LR * ((nm / c1) / (jnp.sqrt(nv / c2) + ADAM_EPS) + ADAM_WD * w_ref[...])
            nm_ref[...] = nm
            nv_ref[...] = nv

    vmem = pl.BlockSpec(memory_space=pltpu.VMEM)
    out = pl.pallas_call(body, name=name, in_specs=[vmem] * (4 * n), out_specs=[vmem] * (3 * n),
                         out_shape=[jax.ShapeDtypeStruct(a.shape, F32) for a in ws] * 3,
                         compiler_params=pltpu.CompilerParams(vmem_limit_bytes=VMEM_LIMIT))(*ws, *gs, *ms, *vs)
    return out[:n], out[n:2 * n], out[2 * n:]


SIDE_EFFECT = pltpu.SideEffectType.DATAFLOW_SIDE_EFFECTING


def _descriptors(copies, refs, send_sems, recv_sems):
    x, y, c = lax.axis_index("x"), lax.axis_index("y"), lax.axis_index("c")
    pos = (x, y, c, 2 * x + y)
    out = []
    for i, (s, d, flip) in enumerate(copies):
        peer = (1 - x if "x" in flip else x, 1 - y if "y" in flip else y, 1 - c if "c" in flip else c)
        out.append(pltpu.make_async_remote_copy(
            src_ref=s(refs, refs, pos), dst_ref=d(refs, refs, pos),
            send_sem=send_sems.at[i], recv_sem=recv_sems.at[i], device_id=peer, device_id_type=MESH))
    return out


def _exchange_start(name, bufs, copies, after=None):
    n, nr = len(bufs), len(copies)
    na = 0 if after is None else 1

    def body(*refs):
        for cp in _descriptors(copies, refs[:n], refs[n + na], refs[n + na + 1]):
            cp.start()
        token = refs[2 * n + na + 2]
        token[...] = jnp.zeros_like(token)

    hbm = pl.BlockSpec(memory_space=pltpu.HBM)
    sem = pl.BlockSpec(memory_space=pltpu.SEMAPHORE)
    out = pl.pallas_call(
        body, name=name,
        in_specs=[hbm] * n + [pl.BlockSpec(memory_space=pl.ANY)] * na,
        out_specs=(sem, sem, *[hbm] * n, pl.BlockSpec(memory_space=pltpu.VMEM)),
        out_shape=(pltpu.SemaphoreType.DMA((nr,)), pltpu.SemaphoreType.DMA((nr,)),
                   *[pltpu.HBM(b.shape, b.dtype) for b in bufs], jax.ShapeDtypeStruct((SUBLANES, LANES), F32)),
        input_output_aliases={i: 2 + i for i in range(n)},
        compiler_params=pltpu.CompilerParams(has_side_effects=SIDE_EFFECT),
    )(*[pltpu.with_memory_space_constraint(b, pltpu.HBM) for b in bufs], *([after] * na))
    return out[0], out[1], list(out[2:2 + n]), out[2 + n]


def _exchange_wait(name, send_sems, recv_sems, bufs, copies, after):
    n = len(bufs)

    def body(*refs):
        for cp in _descriptors(copies, refs[:n], refs[n], refs[n + 1]):
            cp.wait_send()
            cp.wait_recv()

    hbm = pl.BlockSpec(memory_space=pltpu.HBM)
    sem = pl.BlockSpec(memory_space=pltpu.SEMAPHORE)
    out = pl.pallas_call(
        body, name=name,
        in_specs=[hbm] * n + [sem, sem, pl.BlockSpec(memory_space=pl.ANY)],
        out_specs=tuple([hbm] * n),
        out_shape=tuple(pltpu.HBM(b.shape, b.dtype) for b in bufs),
        input_output_aliases={i: i for i in range(n)},
        compiler_params=pltpu.CompilerParams(has_side_effects=SIDE_EFFECT),
    )(*bufs, send_sems, recv_sems, after)
    return list(out)


FIRST = ("w_in",)
MID = ("ssm_w_glu", "w_out")
LATE = ("w_up", "w_down")
GROUPS = {"first": FIRST, "mid": MID, "late": LATE}


def _gather_copies(names, shard_shapes):
    def region(i, chip, c):
        half_axis, shard_axis = BIG[names[i]]
        ssize = shard_shapes[i][shard_axis]
        hsize = shard_shapes[i][half_axis] // 2
        return lambda ref: _view(_view(ref, shard_axis, chip * ssize, ssize), half_axis, c * hsize, hsize)

    ici, d2d = [], []
    for i in range(len(names)):
        for flip in FLIPS:
            ici.append((lambda I, O, pos, i=i: region(i, pos[3], pos[2])(I[i]),
                        lambda I, O, pos, i=i: region(i, pos[3], pos[2])(O[i]), flip))
            d2d.append((lambda I, O, pos, i=i, flip=flip: region(i, _peer_chip(pos, flip), pos[2])(I[i]),
                        lambda I, O, pos, i=i, flip=flip: region(i, _peer_chip(pos, flip), pos[2])(O[i]), "c"))
    return ici, d2d


def _half_shape(n, shape):
    r, cdim = shape
    return (r // 2, cdim) if BIG[n][0] == 0 else (r, cdim // 2)


def _sub_shape(n, shape):
    hr, hc = _half_shape(n, shape)
    return (hr, hc // 4) if BIG[n][1] == 1 else (hr // 4, hc)


def _pair_copies(names, shapes, with_pack, dst_off):
    n = len(names)

    def other_half(i, ref, pos):
        half_axis = BIG[names[i]][0]
        hsize = shapes[i][half_axis] // 2
        return _view(ref, half_axis, (1 - pos[2]) * hsize, hsize)

    copies = [(lambda I, O, pos, i=i: other_half(i, I[i], pos), lambda I, O, pos, i=i: O[dst_off + i], "c")
              for i in range(n)]
    if with_pack:
        copies.append((lambda I, O, pos: I[n], lambda I, O, pos: O[dst_off + n], "c"))
    return copies


def _chip_copies(names, shapes, pack_rows, dst_off):
    n = len(names)

    def piece(i, ref, chip):
        shard_axis = BIG[names[i]][1]
        ssize = _sub_shape(names[i], shapes[i])[shard_axis]
        return _view(ref, shard_axis, chip * ssize, ssize)

    copies = []
    for i in range(n):
        for slot, flip in enumerate(FLIPS):
            copies.append((lambda I, O, pos, i=i, flip=flip: piece(i, I[i], _peer_chip(pos, flip)),
                           lambda I, O, pos, i=i, slot=slot: O[dst_off + i].at[slot], flip))
    if pack_rows:
        for slot, flip in enumerate(FLIPS):
            copies.append((lambda I, O, pos: _view(I[n], 0, pos[2] * (pack_rows // 2), pack_rows // 2),
                           lambda I, O, pos, slot=slot: O[dst_off + n].at[slot], flip))
    return copies


class _Exchanges:
    def __init__(self, shards, tiny, kc):
        self.kc = kc
        wb = {n: _cast_into_full(shards[n], kc, BIG[n][1], "cast_" + n) for n in BIG_NAMES}
        nf = len(FIRST)
        ici, d2d = _gather_copies(FIRST, [shards[n].shape for n in FIRST])
        local = [(lambda I, O, pos: I[nf], lambda I, O, pos: O[nf].at[pos[3]])]
        ici += [(lambda I, O, pos: I[nf], lambda I, O, pos: O[nf].at[pos[3]], flip) for flip in FLIPS]
        outs = ([jax.ShapeDtypeStruct(wb[n].shape, BF16) for n in FIRST]
                + [jax.ShapeDtypeStruct((4,) + tiny.shape, F32)])
        got = _exchange("gather_ici", [wb[n] for n in FIRST] + [tiny], outs, {i: i for i in range(nf)}, local, ici)
        full = _exchange("gather_d2d", list(got[:nf]), outs[:nf], {i: i for i in range(nf)}, [], d2d)
        self.first = dict(zip(FIRST, full))
        self.tiny_all = got[nf]
        self.gathering, self.pairing, self.reducing = {}, {}, {}
        after = full[0]
        self.zero = 0.0
        for group in ("mid", "late"):
            names = GROUPS[group]
            copies = _gather_copies(names, [shards[n].shape for n in names])
            started = _exchange_start("gather_%s_start" % group, [wb[n] for n in names], copies[0], after)
            self.gathering[group] = (started, copies)
            after = started[2][0]
            self.zero = self.zero + started[3][0, 0]

    def weights(self, group, after):
        (send_sems, recv_sems, bufs, _), (ici, d2d) = self.gathering[group]
        got = _exchange_wait("gather_%s_wait" % group, send_sems, recv_sems, bufs, ici, after)
        outs = [jax.ShapeDtypeStruct(b.shape, BF16) for b in got]
        full = _exchange("gather_%s_d2d" % group, got, outs, {i: i for i in range(len(got))}, [], d2d)
        return dict(zip(GROUPS[group], full))

    def grads_ready(self, group, grads):
        names = GROUPS[group]
        gs = [grads[n] for n in names]
        land = [lax.empty(_half_shape(n, g.shape), F32) for n, g in zip(names, gs)]
        copies = _pair_copies(names, [g.shape for g in gs], False, len(names))
        started = _exchange_start("pair_%s_start" % group, gs + land, copies)
        self.pairing[group] = (started, copies)
        return started[3]

    def grads_send(self, group, after):
        names = GROUPS[group]
        n = len(names)
        (send_sems, recv_sems, bufs, _), copies = self.pairing[group]
        bufs = _exchange_wait("pair_%s_wait" % group, send_sems, recv_sems, bufs, copies, after)
        chip = [_pair_sum(bufs[i], bufs[n + i], self.kc, BIG[names[i]][0], "pair_sum_" + names[i], BF16)
                for i in range(n)]
        shapes = [bufs[i].shape for i in range(n)]
        land = [lax.empty((3,) + _sub_shape(names[i], shapes[i]), BF16) for i in range(n)]
        copies = _chip_copies(names, shapes, 0, n)
        started = _exchange_start("reduce_%s_start" % group, chip + land, copies)
        self.reducing[group] = (started, copies)
        return started[3]

    def finish_pack(self, pack):
        kc = self.kc
        prow = pack.shape[0] // 2
        recv = _exchange("reduce_d2d", [pack], [jax.ShapeDtypeStruct(pack.shape, F32)], {}, [],
                         _pair_copies((), [], True, 0))
        chip_pack = _pair_sum(pack, recv[0], kc, None, "pair_sum_pack", F32)
        copies = _chip_copies((), [], pack.shape[0], 1)
        land = lax.empty((3, prow, pack.shape[1]), F32)
        pack_sems_s, pack_sems_r, pack_bufs, after = _exchange_start("reduce_pack_start", [chip_pack, land], copies)

        names, chips, recvs = (), [], []
        for group, group_names in GROUPS.items():
            (send_sems, recv_sems, bufs, _), group_copies = self.reducing[group]
            bufs = _exchange_wait("reduce_%s_wait" % group, send_sems, recv_sems, bufs, group_copies, after)
            n = len(group_names)
            names, chips, recvs = names + group_names, chips + bufs[:n], recvs + bufs[n:]
            after = bufs[n]
        total = [_chip_sum(chips[i], recvs[i], kc, BIG[n][1], BIG[n][0], "chip_sum_" + n)
                 for i, n in enumerate(names)]

        def my_half(half_axis, ref, pos):
            hsize = ref.shape[half_axis] // 2
            return _view(ref, half_axis, pos[2] * hsize, hsize)

        swap = [(lambda I, O, pos, i=i, n=n: my_half(BIG[n][0], I[i], pos),
                 lambda I, O, pos, i=i, n=n: my_half(BIG[n][0], O[i], pos), "c") for i, n in enumerate(names)]
        self.swapping = (_exchange_start("swap_start", total, swap), swap, names)

        chip_pack, recv_pack = _exchange_wait("reduce_pack_wait", pack_sems_s, pack_sems_r, pack_bufs, copies,
                                              self.swapping[0][3])
        total_pack = _chip_sum(chip_pack, recv_pack, kc, None, 0, "chip_sum_pack")
        swap = [(lambda I, O, pos: my_half(0, I[0], pos), lambda I, O, pos: my_half(0, O[0], pos), "c")]
        return _exchange("swap_pack", [total_pack], [jax.ShapeDtypeStruct(pack.shape, F32)], {0: 0}, [], swap)[0]

    def finish_big(self, after):
        (send_sems, recv_sems, bufs, _), swap, names = self.swapping
        return dict(zip(names, _exchange_wait("swap_wait", send_sems, recv_sems, bufs, swap, after)))


WEIGHTS = ("meta_tokens", "norm_mix_g", "w_in", "conv_w", "ssm_lam_re", "ssm_lam_im", "ssm_log_dt", "ssm_b_re",
           "ssm_b_im", "ssm_c_re", "ssm_c_im", "ssm_d", "ssm_w_glu", "gain_conv_out", "gain_ssm_out", "w_out",
           "norm_ffn_g", "w_up", "ffn_conv_w", "ffn_conv_b", "w_down", "norm_final_g")
TINY_SHARDED = ("meta_tokens", "conv_w", "ffn_conv_w")
REPLICATED = tuple(n for n in WEIGHTS if n not in BIG and n not in TINY_SHARDED)
PACK_COLS = 512


def _pack(arrays, row_mult, cols):
    flat = jnp.concatenate([a.reshape(-1).astype(F32) for a in arrays])
    n = flat.shape[0]
    total = -(-n // (row_mult * cols)) * (row_mult * cols)
    return jnp.concatenate([flat, jnp.zeros((total - n,), F32)]).reshape(total // cols, cols)


def _unpack(packed, shapes):
    flat = packed.reshape(-1)
    out, off = [], 0
    for s in shapes:
        n = math.prod(s)
        out.append(flat[off:off + n].reshape(s))
        off += n
    return out


def kernel(x, meta_tokens, norm_mix_g, w_in, conv_w, ssm_lam_re, ssm_lam_im, ssm_log_dt, ssm_b_re, ssm_b_im, ssm_c_re, ssm_c_im, ssm_d, ssm_w_glu, gain_conv_out, gain_ssm_out, w_out, norm_ffn_g, w_up, ffn_conv_w, ffn_conv_b, w_down, norm_final_g, loss_target, m_meta_tokens, m_norm_mix_g, m_w_in, m_conv_w, m_ssm_lam_re, m_ssm_lam_im, m_ssm_log_dt, m_ssm_b_re, m_ssm_b_im, m_ssm_c_re, m_ssm_c_im, m_ssm_d, m_ssm_w_glu, m_gain_conv_out, m_gain_ssm_out, m_w_out, m_norm_ffn_g, m_w_up, m_ffn_conv_w, m_ffn_conv_b, m_w_down, m_norm_final_g, v_meta_tokens, v_norm_mix_g, v_w_in, v_conv_w, v_ssm_lam_re, v_ssm_lam_im, v_ssm_log_dt, v_ssm_b_re, v_ssm_b_im, v_ssm_c_re, v_ssm_c_im, v_ssm_d, v_ssm_w_glu, v_gain_conv_out, v_gain_ssm_out, v_w_out, v_norm_ffn_g, v_w_up, v_ffn_conv_w, v_ffn_conv_b, v_w_down, v_norm_final_g):
    args = dict(locals())
    w = {n: args[n] for n in WEIGHTS}
    mom = {n: args["m_" + n] for n in WEIGHTS}
    var = {n: args["v_" + n] for n in WEIGHTS}
    kx, ky, kc_ = lax.axis_index("x"), lax.axis_index("y"), lax.axis_index("c")
    chip = 2 * kx + ky
    kc = jnp.stack([chip, kc_]).astype(jnp.int32)

    def squeeze(n, a):
        if n == "meta_tokens":
            return a
        if n == "norm_final_g":
            return a.reshape(1, -1)
        a = a[0]
        return a.reshape(1, -1) if a.ndim == 1 else a

    wl = {n: squeeze(n, w[n]) for n in WEIGHTS}
    ml = {n: squeeze(n, mom[n]) for n in WEIGHTS}
    vl = {n: squeeze(n, var[n]) for n in WEIGHTS}

    tiny = _pack([wl[n] for n in TINY_SHARDED], SUBLANES, LANES)
    ex = _Exchanges({n: wl[n] for n in BIG_NAMES}, tiny, kc)
    tiny_shapes = [wl[n].shape for n in TINY_SHARDED]
    tiny_parts = [_unpack(ex.tiny_all[k], tiny_shapes) for k in range(4)]
    p = {n: wl[n] for n in WEIGHTS if n not in BIG}
    p.update(ex.first)
    for j, n in enumerate(TINY_SHARDED):
        p[n] = jnp.concatenate([tiny_parts[k][j] for k in range(4)], axis=1)
    p["ssm_log_dt"] = wl["ssm_log_dt"].reshape(-1)

    loss_local, grad_x, grads = _local_step(x[0], loss_target[0], p, ex)

    small_names = REPLICATED + TINY_SHARDED
    small_shapes = [tuple(grads[n].shape) for n in small_names] + [(1,)]
    pack = _pack([grads[n] for n in small_names] + [loss_local.reshape(1)], 2 * 16, PACK_COLS)
    g_pack = ex.finish_pack(pack)
    g_small = dict(zip(small_names + ("loss",), _unpack(g_pack, small_shapes)))
    loss = g_small["loss"][0]
    swapped = ("ssm_b_re", "ssm_b_im")

    def view(n, a):
        if n in swapped:
            return jnp.swapaxes(a, -1, -2)
        return a.reshape(1, -1) if a.ndim == 1 else a

    g = {}
    for n in REPLICATED:
        g[n] = g_small[n].reshape(view(n, w[n]).shape)
    for n in TINY_SHARDED:
        cols = wl[n].shape[1]
        g[n] = lax.dynamic_slice_in_dim(g_small[n], chip * cols, cols, axis=1).reshape(w[n].shape)
    delta, new_m, new_v = {}, {}, {}
    small = [[view(n, d[n]) for n in small_names] for d in (w, mom, var)]
    small.insert(1, [g[n] for n in small_names])
    for d, outs in zip((delta, new_m, new_v), _adamw_whole(*small, "adamw_small")):
        d.update(zip(small_names, outs))
    for d in (g, delta, new_m, new_v):
        d.update({n: jnp.swapaxes(d[n], -1, -2) for n in swapped})
    g_big = ex.finish_big(delta[small_names[0]])
    for n in BIG_NAMES:
        g[n], delta[n], new_m[n], new_v[n] = _adamw(wl[n], g_big[n], ml[n], vl[n], "adamw_" + n)

    def like(n, a):
        return a.reshape(w[n].shape)

    return (loss, grad_x[None], *[like(n, g[n]) for n in WEIGHTS], *[like(n, delta[n]) for n in WEIGHTS],
            *[like(n, new_m[n]) for n in WEIGHTS], *[like(n, new_v[n]) for n in WEIGHTS])
```

```python
import functools
import math

import jax
import jax.numpy as jnp
from jax import lax
from jax.experimental import pallas as pl
from jax.experimental.pallas import tpu as pltpu

F32 = jnp.float32
BF16 = jnp.bfloat16
MESH = pl.DeviceIdType.MESH

N_META = 16
N_GROUPS = 32
GROUP = 16
STATE = 64
RMS_EPS = 1e-6
ADAM_LR = 0.001
ADAM_B1 = 0.9
ADAM_B2 = 0.999
ADAM_EPS = 1e-08
ADAM_WD = 0.01
ADAM_STEP = 10

LANES = 128
SUBLANES = 8
ROW_ALIGN = 128
ROW_TILES = 4
VMEM_LIMIT = 52 * 1024 * 1024
MM_VMEM_BUDGET = 40 * 1024 * 1024
GELU_C = math.sqrt(2.0 / math.pi)
GELU_A = 0.044715


def _cparams(*sem):
    return pltpu.CompilerParams(dimension_semantics=sem, vmem_limit_bytes=VMEM_LIMIT)


def _pick_tile(dim, cap, mult):
    best = None
    for t in range(mult, min(dim, cap) + 1, mult):
        if dim % t == 0:
            best = t
    return best if best is not None else dim


def _mm(a, b, mode, name, out_dtype=F32, acc_in=None, after=None):
    if mode == "tn":
        kdim, m = a.shape
    else:
        m, kdim = a.shape
    n = b.shape[0] if mode == "nt" else b.shape[1]
    tm = _pick_tile(m, 1408, LANES if mode == "tn" else 16)
    tk = _pick_tile(kdim, 2816, LANES)
    nk = kdim // tk
    out_bytes = jnp.dtype(out_dtype).itemsize
    for cap in (1408, 1024, 512, 256, LANES):
        tn = _pick_tile(n, cap, LANES)
        blocks = 2 * (tm * tk * 2 + tk * tn * 2 + tm * tn * out_bytes * (2 if acc_in is not None else 1))
        if blocks + (tm * tn * 4 if nk > 1 else 0) <= MM_VMEM_BUDGET:
            break
    has_acc = acc_in is not None

    def body(*refs):
        if after is not None:
            refs = refs[1:]
        if has_acc:
            a_ref, b_ref, c_ref, o_ref = refs[:4]
            rest = refs[4:]
        else:
            a_ref, b_ref, o_ref = refs[:3]
            c_ref = None
            rest = refs[3:]
        if mode == "nn":
            p = jnp.dot(a_ref[...], b_ref[...], preferred_element_type=F32)
        elif mode == "nt":
            p = lax.dot_general(a_ref[...], b_ref[...], (((1,), (1,)), ((), ())), preferred_element_type=F32)
        else:
            p = lax.dot_general(a_ref[...], b_ref[...], (((0,), (0,)), ((), ())), preferred_element_type=F32)
        if nk == 1:
            if has_acc:
                p = p + c_ref[...]
            o_ref[...] = p.astype(out_dtype)
        else:
            acc_ref = rest[0]
            k = pl.program_id(2)

            @pl.when(k == 0)
            def _():
                acc_ref[...] = p + c_ref[...] if has_acc else p

            @pl.when(k > 0)
            def _():
                acc_ref[...] += p

            @pl.when(k == nk - 1)
            def _():
                o_ref[...] = acc_ref[...].astype(out_dtype)

    if mode == "tn":
        a_spec = pl.BlockSpec((tk, tm), lambda i, j, k: (k, i))
    else:
        a_spec = pl.BlockSpec((tm, tk), lambda i, j, k: (i, k))
    if mode == "nt":
        b_spec = pl.BlockSpec((tn, tk), lambda i, j, k: (j, k))
    else:
        b_spec = pl.BlockSpec((tk, tn), lambda i, j, k: (k, j))
    o_spec = pl.BlockSpec((tm, tn), lambda i, j, k: (i, j))
    in_specs = [a_spec, b_spec] + ([o_spec] if has_acc else [])
    args = (a, b) + ((acc_in,) if has_acc else ())
    if after is not None:
        in_specs = [pl.BlockSpec(memory_space=pl.ANY)] + in_specs
        args = (after,) + args
    return pl.pallas_call(
        body, name=name, grid=(m // tm, n // tn, nk),
        in_specs=in_specs, out_specs=o_spec,
        out_shape=jax.ShapeDtypeStruct((m, n), out_dtype),
        scratch_shapes=[pltpu.VMEM((tm, tn), F32)] if nk > 1 else [],
        compiler_params=_cparams("parallel", "parallel", "arbitrary"),
    )(*args)


def _rows(shape_cols, tr, dtype=None):
    return pl.BlockSpec((tr, shape_cols), lambda i: (i, 0))


def _const(shape):
    return pl.BlockSpec(shape, lambda i: (0,) * len(shape))


def _rms(x):
    return lax.rsqrt(jnp.mean(x * x, axis=-1, keepdims=True) + RMS_EPS)


def _rms_bwd(x, r, g, dy):
    xn = x * r
    dxn = dy * g
    dx = r * (dxn - xn * jnp.mean(dxn * xn, axis=-1, keepdims=True))
    return dx, dy * xn


def _gelu(y):
    return 0.5 * y * (1.0 + jnp.tanh(GELU_C * (y + GELU_A * y * y * y)))


def _gelu_grad(y):
    t = jnp.tanh(GELU_C * (y + GELU_A * y * y * y))
    return 0.5 * (1.0 + t) + 0.5 * y * (1.0 - t * t) * GELU_C * (1.0 + 3.0 * GELU_A * y * y)


def _sigmoid(z):
    return 1.0 / (1.0 + jnp.exp(-z))


def _norm_fwd(h, g, name, res=None):
    tp, d = h.shape
    tr = tp // ROW_TILES
    has_res = res is not None

    def body(*refs):
        if has_res:
            h_ref, r_ref, g_ref, s_ref, hn_ref = refs
            x = h_ref[...] + r_ref[...]
            s_ref[...] = x
        else:
            h_ref, g_ref, hn_ref = refs
            x = h_ref[...]
        hn_ref[...] = (x * _rms(x) * g_ref[...]).astype(BF16)

    in_specs = [_rows(d, tr)] + ([_rows(d, tr)] if has_res else []) + [_const((1, d))]
    out_specs = ([_rows(d, tr)] if has_res else []) + [_rows(d, tr)]
    out_shape = ([jax.ShapeDtypeStruct((tp, d), F32)] if has_res else []) + [jax.ShapeDtypeStruct((tp, d), BF16)]
    args = (h,) + ((res,) if has_res else ()) + (g,)
    out = pl.pallas_call(body, name=name, grid=(ROW_TILES,), in_specs=in_specs, out_specs=out_specs,
                         out_shape=out_shape, compiler_params=_cparams("parallel"))(*args)
    return out if has_res else out[0]


def _norm_bwd(h, g, dhn, dres, name):
    tp, d = h.shape
    tr = tp // ROW_TILES

    def body(h_ref, g_ref, dhn_ref, dres_ref, dh_ref, dhb_ref, dg_ref):
        x = h_ref[...]
        dx, dgs = _rms_bwd(x, _rms(x), g_ref[...], dhn_ref[...])
        dh = dres_ref[...] + dx
        dh_ref[...] = dh
        dhb_ref[...] = dh.astype(BF16)

        @pl.when(pl.program_id(0) == 0)
        def _():
            dg_ref[...] = jnp.zeros_like(dg_ref)

        dg_ref[...] += jnp.sum(dgs, axis=0, keepdims=True)

    return pl.pallas_call(
        body, name=name, grid=(ROW_TILES,),
        in_specs=[_rows(d, tr), _const((1, d)), _rows(d, tr), _rows(d, tr)],
        out_specs=[_rows(d, tr), _rows(d, tr), _const((1, d))],
        out_shape=[jax.ShapeDtypeStruct((tp, d), F32), jax.ShapeDtypeStruct((tp, d), BF16),
                   jax.ShapeDtypeStruct((1, d), F32)],
        compiler_params=_cparams("arbitrary"))(h, g, dhn, dres)


def _input_norm_bwd(h, g, dhn, dres, n_real, name):
    tp, d = h.shape
    tr = tp // ROW_TILES

    def body(h_ref, g_ref, dhn_ref, dres_ref, dx_ref, dmeta_ref, dg_ref, stage, sem):
        i = pl.program_id(0)
        x = h_ref[...]
        dx, dgs = _rms_bwd(x, _rms(x), g_ref[...], dhn_ref[...])
        stage[...] = dres_ref[...] + dx

        @pl.when(i == 0)
        def _():
            dg_ref[...] = jnp.zeros_like(dg_ref)
            dmeta_ref[...] = stage[:N_META, :]

        dg_ref[...] += jnp.sum(dgs, axis=0, keepdims=True)
        for t in range(ROW_TILES):
            lo, hi = max(t * tr, N_META), min((t + 1) * tr, n_real)
            if hi > lo:
                @pl.when(i == t)
                def _(t=t, lo=lo, hi=hi):
                    cp = pltpu.make_async_copy(stage.at[pl.ds(lo - t * tr, hi - lo), :],
                                               dx_ref.at[pl.ds(lo - N_META, hi - lo), :], sem)
                    cp.start()
                    cp.wait()

    return pl.pallas_call(
        body, name=name, grid=(ROW_TILES,),
        in_specs=[_rows(d, tr), _const((1, d)), _rows(d, tr), _rows(d, tr)],
        out_specs=[pl.BlockSpec(memory_space=pl.ANY), _const((N_META, d)), _const((1, d))],
        out_shape=[jax.ShapeDtypeStruct((n_real - N_META, d), F32), jax.ShapeDtypeStruct((N_META, d), F32),
                   jax.ShapeDtypeStruct((1, d), F32)],
        scratch_shapes=[pltpu.VMEM((tr, d), F32), pltpu.SemaphoreType.DMA],
        compiler_params=_cparams("arbitrary"))(h, g, dhn, dres)


def _load_token_rows(tok_hbm, buf, sem, tr, n_real, head=None, wait=False):
    i = pl.program_id(0)
    for t in range(ROW_TILES):
        base = t * tr
        lo, hi = max(base, N_META), min(base + tr, n_real)

        @pl.when(i == t)
        def _(base=base, lo=lo, hi=hi):
            if hi > lo:
                cp = pltpu.make_async_copy(tok_hbm.at[pl.ds(lo - N_META, hi - lo), :],
                                           buf.at[pl.ds(lo - base, hi - lo), :], sem)
                if wait:
                    cp.wait()
                    return
                cp.start()
            if wait:
                return
            if base < N_META:
                buf[0:N_META - base, :] = (jnp.zeros((N_META - base, buf.shape[1]), F32) if head is None
                                           else head[base:N_META, :])
            if hi < base + tr:
                buf[max(hi, base) - base:tr, :] = jnp.zeros((base + tr - max(hi, base), buf.shape[1]), F32)


def _input_norm_fwd(x, meta, g, tp, name):
    seq, d = x.shape
    tr = tp // ROW_TILES
    n_real = N_META + seq

    def body(x_hbm, meta_ref, g_ref, h_ref, hn_ref, buf, sem):
        _load_token_rows(x_hbm, buf, sem, tr, n_real, head=meta_ref)
        _load_token_rows(x_hbm, buf, sem, tr, n_real, wait=True)
        h = buf[...]
        h_ref[...] = h
        hn_ref[...] = (h * _rms(h) * g_ref[...]).astype(BF16)

    return pl.pallas_call(
        body, name=name, grid=(ROW_TILES,),
        in_specs=[pl.BlockSpec(memory_space=pl.ANY), _const((N_META, d)), _const((1, d))],
        out_specs=[_rows(d, tr), _rows(d, tr)],
        out_shape=[jax.ShapeDtypeStruct((tp, d), F32), jax.ShapeDtypeStruct((tp, d), BF16)],
        scratch_shapes=[pltpu.VMEM((tr, d), F32), pltpu.SemaphoreType.DMA],
        compiler_params=_cparams("arbitrary"))(x, meta, g)


def _loss_bwd(h1, dn, target, g, n_real, name):
    tp, d = h1.shape
    tr = tp // ROW_TILES

    def body(h1_ref, dn_ref, t_hbm, g_ref, loss_ref, dh_ref, dhb_ref, dg_ref, t_buf, sem):
        i = pl.program_id(0)
        _load_token_rows(t_hbm, t_buf, sem, tr, n_real)
        x = h1_ref[...] + dn_ref[...]
        r = _rms(x)
        row = i * tr + lax.broadcasted_iota(jnp.int32, (tr, d), 0)
        valid = (row >= N_META) & (row < n_real)
        _load_token_rows(t_hbm, t_buf, sem, tr, n_real, wait=True)
        e = jnp.where(valid, x * r * g_ref[...] - t_buf[...], 0.0)
        dx, dgs = _rms_bwd(x, r, g_ref[...], e * (1.0 / d))
        dh_ref[...] = dx
        dhb_ref[...] = dx.astype(BF16)

        @pl.when(i == 0)
        def _():
            dg_ref[...] = jnp.zeros_like(dg_ref)
            loss_ref[...] = jnp.zeros_like(loss_ref)

        dg_ref[...] += jnp.sum(dgs, axis=0, keepdims=True)
        loss_ref[...] += (0.5 / d) * jnp.sum(jnp.sum(e * e, axis=0, keepdims=True), axis=1, keepdims=True)

    return pl.pallas_call(
        body, name=name, grid=(ROW_TILES,),
        in_specs=[_rows(d, tr), _rows(d, tr), pl.BlockSpec(memory_space=pl.ANY), _const((1, d))],
        out_specs=[_const((1, LANES)), _rows(d, tr), _rows(d, tr), _const((1, d))],
        out_shape=[jax.ShapeDtypeStruct((1, LANES), F32), jax.ShapeDtypeStruct((tp, d), F32),
                   jax.ShapeDtypeStruct((tp, d), BF16), jax.ShapeDtypeStruct((1, d), F32)],
        scratch_shapes=[pltpu.VMEM((tr, d), F32), pltpu.SemaphoreType.DMA],
        compiler_params=_cparams("arbitrary"))(h1, dn, target, g)


def _mix_fwd(co, y, z, gc, gs, name):
    tp, dh = co.shape
    tr = tp // ROW_TILES

    def body(co_ref, y_ref, z_ref, gc_ref, gs_ref, m_ref):
        c = co_ref[...]
        m_ref[:, :dh] = (c * _rms(c) * gc_ref[...]).astype(BF16)
        so = _gelu(y_ref[...]) * _sigmoid(z_ref[...])
        m_ref[:, dh:] = (so * _rms(so) * gs_ref[...]).astype(BF16)

    return pl.pallas_call(
        body, name=name, grid=(ROW_TILES,),
        in_specs=[_rows(dh, tr)] * 3 + [_const((1, dh))] * 2,
        out_specs=_rows(2 * dh, tr),
        out_shape=jax.ShapeDtypeStruct((tp, 2 * dh), BF16),
        compiler_params=_cparams("parallel"))(co, y, z, gc, gs)


def _mix_bwd(dm, co, y, z, gc, gs, name):
    tp, dh = co.shape
    tr = tp // ROW_TILES

    def body(dm_ref, co_ref, y_ref, z_ref, gc_ref, gs_ref, dco_ref, dz_ref, dgp_ref, dgc_ref, dgs_ref):
        c = co_ref[...]
        dco, dgc = _rms_bwd(c, _rms(c), gc_ref[...], dm_ref[:, :dh])
        dco_ref[...] = dco
        gl = _gelu(y_ref[...])
        sg = _sigmoid(z_ref[...])
        so = gl * sg
        dso, dgs = _rms_bwd(so, _rms(so), gs_ref[...], dm_ref[:, dh:])
        dz_ref[...] = (dso * gl * sg * (1.0 - sg)).astype(BF16)
        dgp_ref[...] = dso * sg

        @pl.when(pl.program_id(0) == 0)
        def _():
            dgc_ref[...] = jnp.zeros_like(dgc_ref)
            dgs_ref[...] = jnp.zeros_like(dgs_ref)

        dgc_ref[...] += jnp.sum(dgc, axis=0, keepdims=True)
        dgs_ref[...] += jnp.sum(dgs, axis=0, keepdims=True)

    return pl.pallas_call(
        body, name=name, grid=(ROW_TILES,),
        in_specs=[_rows(2 * dh, tr)] + [_rows(dh, tr)] * 3 + [_const((1, dh))] * 2,
        out_specs=[_rows(dh, tr), _rows(dh, tr), _rows(dh, tr), _const((1, dh)), _const((1, dh))],
        out_shape=[jax.ShapeDtypeStruct((tp, dh), F32), jax.ShapeDtypeStruct((tp, dh), BF16),
                   jax.ShapeDtypeStruct((tp, dh), F32), jax.ShapeDtypeStruct((1, dh), F32),
                   jax.ShapeDtypeStruct((1, dh), F32)],
        compiler_params=_cparams("arbitrary"))(dm, co, y, z, gc, gs)


def _shift_down(x, k):
    row = lax.broadcasted_iota(jnp.int32, x.shape, 0)
    return jnp.where(row >= k, pltpu.roll(x, k, 0), 0.0)


def _shift_up(x, k):
    n = x.shape[0]
    row = lax.broadcasted_iota(jnp.int32, x.shape, 0)
    return jnp.where(row < n - k, pltpu.roll(x, n - k, 0), 0.0)


def _dwconv(x, w_ref):
    return w_ref[2:3, :] * x + w_ref[1:2, :] * _shift_down(x, 1) + w_ref[0:1, :] * _shift_down(x, 2)


def _dwconv_bwd(x, dy, w_ref):
    dx = w_ref[2:3, :] * dy + w_ref[1:2, :] * _shift_up(dy, 1) + w_ref[0:1, :] * _shift_up(dy, 2)
    dw = jnp.concatenate([jnp.sum(dy * _shift_down(x, 2), axis=0, keepdims=True),
                          jnp.sum(dy * _shift_down(x, 1), axis=0, keepdims=True),
                          jnp.sum(dy * x, axis=0, keepdims=True)], axis=0)
    return dx, dw


def _scan(s_re, s_im, tab_ref, reverse):
    n_chunks = s_re.shape[0] // SUBLANES
    n_strips = s_re.shape[1] // LANES
    last = 0 if reverse else SUBLANES - 1

    def body(i, carry):
        chunk = (n_chunks - 1 - i) if reverse else i
        r0 = pl.multiple_of(chunk * SUBLANES, SUBLANES)
        out = []
        for st in range(n_strips):
            lanes = slice(st * LANES, (st + 1) * LANES)
            cr, ci = carry[2 * st], carry[2 * st + 1]
            xr = s_re[pl.ds(r0, SUBLANES), lanes]
            xi = s_im[pl.ds(r0, SUBLANES), lanes]
            for level, k in enumerate((1, 2, 4)):
                mr = tab_ref[2 * level, :, lanes]
                mi = tab_ref[2 * level + 1, :, lanes]
                sh = SUBLANES - k if reverse else k
                rr = pltpu.roll(xr, sh, 0)
                ri = pltpu.roll(xi, sh, 0)
                xr, xi = xr + (mr * rr - mi * ri), xi + (mr * ri + mi * rr)
            pwr = tab_ref[6, :, lanes]
            pwi = tab_ref[7, :, lanes]
            xr, xi = xr + (pwr * cr - pwi * ci), xi + (pwr * ci + pwi * cr)
            s_re[pl.ds(r0, SUBLANES), lanes] = xr
            s_im[pl.ds(r0, SUBLANES), lanes] = xi
            out.append(jnp.broadcast_to(xr[last:last + 1, :], (SUBLANES, LANES)))
            out.append(jnp.broadcast_to(xi[last:last + 1, :], (SUBLANES, LANES)))
        return tuple(out)

    zero = jnp.zeros((SUBLANES, LANES), F32)
    lax.fori_loop(0, n_chunks, body, (zero,) * (2 * n_strips))


def _seq_fwd(proj, conv_w, bc_re, bc_im, cc_re, cc_im, dskip, tab_f, name):
    tp = proj.shape[0]
    dh = proj.shape[1] // 4
    nq = dh // LANES
    sw = STATE * N_GROUPS // nq

    def body(b_ref, c_ref, v_ref, u_ref, w_ref, bre_ref, bim_ref, cre_ref, cim_ref, d_ref, tab_ref,
             co_ref, y_ref, g_ref, s_re, s_im):
        co_ref[...] = b_ref[...] * _dwconv(c_ref[...] * v_ref[...], w_ref)
        u = u_ref[...]
        ub = u.astype(BF16)
        s_re[...] = jnp.dot(ub, bre_ref[...], preferred_element_type=F32)
        s_im[...] = jnp.dot(ub, bim_ref[...], preferred_element_type=F32)
        _scan(s_re, s_im, tab_ref, False)
        y = (jnp.dot(s_re[...].astype(BF16), cre_ref[...], preferred_element_type=F32)
             - jnp.dot(s_im[...].astype(BF16), cim_ref[...], preferred_element_type=F32)
             + d_ref[...] * u)
        y_ref[...] = y
        g_ref[...] = _gelu(y).astype(BF16)

    col = lambda off: pl.BlockSpec((tp, LANES), lambda q, off=off: (0, off * nq + q))
    blk = pl.BlockSpec((tp, LANES), lambda q: (0, q))
    return pl.pallas_call(
        body, name=name, grid=(nq,),
        in_specs=[col(0), col(1), col(2), col(3),
                  pl.BlockSpec((3, LANES), lambda q: (0, q)),
                  pl.BlockSpec((LANES, sw), lambda q: (0, q)), pl.BlockSpec((LANES, sw), lambda q: (0, q)),
                  pl.BlockSpec((sw, LANES), lambda q: (q, 0)), pl.BlockSpec((sw, LANES), lambda q: (q, 0)),
                  pl.BlockSpec((1, LANES), lambda q: (0, q)),
                  pl.BlockSpec((8, SUBLANES, sw), lambda q: (0, 0, q))],
        out_specs=[blk, blk, blk],
        out_shape=[jax.ShapeDtypeStruct((tp, dh), F32), jax.ShapeDtypeStruct((tp, dh), F32),
                   jax.ShapeDtypeStruct((tp, dh), BF16)],
        scratch_shapes=[pltpu.VMEM((tp, sw), F32), pltpu.VMEM((tp, sw), F32)],
        compiler_params=_cparams("parallel"),
    )(proj, proj, proj, proj, conv_w, bc_re, bc_im, cc_re, cc_im, dskip, tab_f)


def _conv_bwd(proj, dco, conv_w, name):
    tp = proj.shape[0]
    dh = proj.shape[1] // 4
    nq = dh // LANES

    def body(b_ref, c_ref, v_ref, dco_ref, w_ref, dproj_ref, dw_ref, stage, sem):
        q = pl.program_id(0)
        cg = c_ref[...]
        vg = v_ref[...]
        cv = cg * vg
        dco_v = dco_ref[...]
        dcv, dw = _dwconv_bwd(cv, dco_v * b_ref[...], w_ref)
        dw_ref[...] = dw
        stage[0] = (dco_v * _dwconv(cv, w_ref)).astype(BF16)
        stage[1] = (dcv * vg).astype(BF16)
        stage[2] = (dcv * cg).astype(BF16)
        copies = [pltpu.make_async_copy(stage.at[p], dproj_ref.at[:, pl.ds((p * nq + q) * LANES, LANES)], sem.at[p])
                  for p in range(3)]
        for cp in copies:
            cp.start()
        for cp in copies:
            cp.wait()

    col = lambda off: pl.BlockSpec((tp, LANES), lambda q, off=off: (0, off * nq + q))
    return pl.pallas_call(
        body, name=name, grid=(nq,),
        in_specs=[col(0), col(1), col(2), pl.BlockSpec((tp, LANES), lambda q: (0, q)),
                  pl.BlockSpec((3, LANES), lambda q: (0, q))],
        out_specs=[pl.BlockSpec(memory_space=pl.ANY), pl.BlockSpec((3, LANES), lambda q: (0, q))],
        out_shape=[jax.ShapeDtypeStruct((tp, 4 * dh), BF16), jax.ShapeDtypeStruct((3, dh), F32)],
        scratch_shapes=[pltpu.VMEM((3, tp, LANES), BF16), pltpu.SemaphoreType.DMA((3,))],
        compiler_params=_cparams("arbitrary"),
    )(proj, proj, proj, dco, conv_w)


def _ssm_bwd(proj, y, dg, dproj, bc_re, bc_im, cc_re, cc_im, dskip, tab_f, tab_r, name):
    tp = proj.shape[0]
    dh = proj.shape[1] // 4
    nq = dh // LANES
    sw = STATE * N_GROUPS // nq

    def body(u_ref, y_ref, dg_ref, dproj_in, bre_ref, bim_ref, cre_ref, cim_ref, d_ref, tabf_ref, tabr_ref,
             dproj_ref, dbre_ref, dbim_ref, dcre_ref, dcim_ref, dd_ref, dar_ref, dai_ref,
             s_re, s_im, l_re, l_im, stage, sem):
        del dproj_in
        q = pl.program_id(0)
        nt = (((1,), (1,)), ((), ()))
        tn = (((0,), (0,)), ((), ()))
        u = u_ref[...]
        ub = u.astype(BF16)
        s_re[...] = jnp.dot(ub, bre_ref[...], preferred_element_type=F32)
        s_im[...] = jnp.dot(ub, bim_ref[...], preferred_element_type=F32)
        _scan(s_re, s_im, tabf_ref, False)
        dy = dg_ref[...] * _gelu_grad(y_ref[...])
        dyb = dy.astype(BF16)
        dd_ref[...] = jnp.sum(dy * u, axis=0, keepdims=True)
        l_re[...] = lax.dot_general(dyb, cre_ref[...], nt, preferred_element_type=F32)
        l_im[...] = -lax.dot_general(dyb, cim_ref[...], nt, preferred_element_type=F32)
        dcre_ref[...] = lax.dot_general(s_re[...].astype(BF16), dyb, tn, preferred_element_type=F32)
        dcim_ref[...] = -lax.dot_general(s_im[...].astype(BF16), dyb, tn, preferred_element_type=F32)
        _scan(l_re, l_im, tabr_ref, True)
        for st in range(sw // LANES):
            lanes = slice(st * LANES, (st + 1) * LANES)
            lr = l_re[:, lanes]
            li = l_im[:, lanes]
            pr = _shift_down(s_re[:, lanes], 1)
            pi = _shift_down(s_im[:, lanes], 1)
            dar_ref[:, lanes] = jnp.sum(lr * pr + li * pi, axis=0, keepdims=True)
            dai_ref[:, lanes] = jnp.sum(li * pr - lr * pi, axis=0, keepdims=True)
        lrb = l_re[...].astype(BF16)
        lib = l_im[...].astype(BF16)
        du = (dy * d_ref[...] + lax.dot_general(lrb, bre_ref[...], nt, preferred_element_type=F32)
              + lax.dot_general(lib, bim_ref[...], nt, preferred_element_type=F32))
        stage[...] = du.astype(BF16)
        dbre_ref[...] = lax.dot_general(ub, lrb, tn, preferred_element_type=F32)
        dbim_ref[...] = lax.dot_general(ub, lib, tn, preferred_element_type=F32)
        cp = pltpu.make_async_copy(stage, dproj_ref.at[:, pl.ds((3 * nq + q) * LANES, LANES)], sem)
        cp.start()
        cp.wait()

    blk = pl.BlockSpec((tp, LANES), lambda q: (0, q))
    bspec = pl.BlockSpec((LANES, sw), lambda q: (0, q))
    cspec = pl.BlockSpec((sw, LANES), lambda q: (q, 0))
    tspec = pl.BlockSpec((8, SUBLANES, sw), lambda q: (0, 0, q))
    nstate = STATE * N_GROUPS
    return pl.pallas_call(
        body, name=name, grid=(nq,),
        in_specs=[pl.BlockSpec((tp, LANES), lambda q: (0, 3 * nq + q)), blk, blk, pl.BlockSpec(memory_space=pl.ANY),
                  bspec, bspec, cspec, cspec, pl.BlockSpec((1, LANES), lambda q: (0, q)), tspec, tspec],
        out_specs=[pl.BlockSpec(memory_space=pl.ANY), bspec, bspec, cspec, cspec,
                   pl.BlockSpec((1, LANES), lambda q: (0, q)),
                   pl.BlockSpec((1, sw), lambda q: (0, q)), pl.BlockSpec((1, sw), lambda q: (0, q))],
        out_shape=[jax.ShapeDtypeStruct((tp, 4 * dh), BF16),
                   jax.ShapeDtypeStruct((LANES, nstate), F32), jax.ShapeDtypeStruct((LANES, nstate), F32),
                   jax.ShapeDtypeStruct((nstate, LANES), F32), jax.ShapeDtypeStruct((nstate, LANES), F32),
                   jax.ShapeDtypeStruct((1, dh), F32),
                   jax.ShapeDtypeStruct((1, nstate), F32), jax.ShapeDtypeStruct((1, nstate), F32)],
        input_output_aliases={3: 0},
        scratch_shapes=[pltpu.VMEM((tp, sw), F32)] * 4 + [pltpu.VMEM((tp, LANES), BF16), pltpu.SemaphoreType.DMA],
        compiler_params=_cparams("arbitrary"),
    )(proj, y, dg, dproj, bc_re, bc_im, cc_re, cc_im, dskip, tab_f, tab_r)


FFN_TILE = 256


def _ffn_act(up, fw, fb, name):
    tp, two_ff = up.shape
    dff = two_ff // 2
    tc = FFN_TILE
    nj = dff // tc

    def body(ua_ref, uv_ref, wa_ref, wv_ref, ba_ref, bv_ref, act_ref):
        a = _dwconv(ua_ref[...], wa_ref) + ba_ref[...]
        v = _dwconv(uv_ref[...], wv_ref) + bv_ref[...]
        act_ref[...] = (a * _sigmoid(a) * v).astype(BF16)

    lo = lambda r: pl.BlockSpec((r, tc), lambda j: (0, j))
    hi = lambda r: pl.BlockSpec((r, tc), lambda j: (0, nj + j))
    return pl.pallas_call(
        body, name=name, grid=(nj,),
        in_specs=[lo(tp), hi(tp), lo(3), hi(3), lo(1), hi(1)],
        out_specs=lo(tp),
        out_shape=jax.ShapeDtypeStruct((tp, dff), BF16),
        compiler_params=_cparams("parallel"))(up, up, fw, fw, fb, fb)


def _ffn_bwd(up, dact, fw, fb, name):
    tp, two_ff = up.shape
    dff = two_ff // 2
    tc = FFN_TILE
    nj = dff // tc

    def body(ua_ref, uv_ref, da_ref, wa_ref, wv_ref, ba_ref, bv_ref,
             dup_ref, dwa_ref, dwv_ref, dba_ref, dbv_ref, stage, sem):
        j = pl.program_id(0)
        ua = ua_ref[...]
        uv = uv_ref[...]
        a = _dwconv(ua, wa_ref) + ba_ref[...]
        v = _dwconv(uv, wv_ref) + bv_ref[...]
        sg = _sigmoid(a)
        dact_v = da_ref[...]
        da = dact_v * v * sg * (1.0 + a * (1.0 - sg))
        dv = dact_v * a * sg
        dba_ref[...] = jnp.sum(da, axis=0, keepdims=True)
        dbv_ref[...] = jnp.sum(dv, axis=0, keepdims=True)
        dua, dwa = _dwconv_bwd(ua, da, wa_ref)
        duv, dwv = _dwconv_bwd(uv, dv, wv_ref)
        dwa_ref[...] = dwa
        dwv_ref[...] = dwv
        stage[0] = dua.astype(BF16)
        stage[1] = duv.astype(BF16)
        copies = [pltpu.make_async_copy(stage.at[p], dup_ref.at[:, pl.ds((p * nj + j) * tc, tc)], sem.at[p])
                  for p in range(2)]
        for cp in copies:
            cp.start()
        for cp in copies:
            cp.wait()

    lo = lambda r: pl.BlockSpec((r, tc), lambda j: (0, j))
    hi = lambda r: pl.BlockSpec((r, tc), lambda j: (0, nj + j))
    return pl.pallas_call(
        body, name=name, grid=(nj,),
        in_specs=[lo(tp), hi(tp), lo(tp), lo(3), hi(3), lo(1), hi(1)],
        out_specs=[pl.BlockSpec(memory_space=pl.ANY), lo(3), lo(3), lo(1), lo(1)],
        out_shape=[jax.ShapeDtypeStruct((tp, two_ff), BF16),
                   jax.ShapeDtypeStruct((3, dff), F32), jax.ShapeDtypeStruct((3, dff), F32),
                   jax.ShapeDtypeStruct((1, dff), F32), jax.ShapeDtypeStruct((1, dff), F32)],
        scratch_shapes=[pltpu.VMEM((2, tp, tc), BF16), pltpu.SemaphoreType.DMA((2,))],
        compiler_params=_cparams("arbitrary"))(up, up, dact, fw, fw, fb, fb)


def _zoh(lr, li, ld):
    dt = jnp.exp(ld)
    mag = jnp.exp(lr * dt)
    ang = li * dt
    ar = mag * jnp.cos(ang)
    ai = mag * jnp.sin(ang)
    den = lr * lr + li * li
    nr = ar - 1.0
    fr = (nr * lr + ai * li) / den
    fi = (ai * lr - nr * li) / den
    return dt, ar, ai, den, nr, fr, fi


def _s5_prep(lr, li, ld, b_re, b_im, name):
    nstate = lr.shape[1]

    def tables(tab_ref, ar, ai, reverse):
        pows = [(ar, ai)]
        for _ in range(SUBLANES - 1):
            pr, pi = pows[-1]
            pows.append((pr * ar - pi * ai, pr * ai + pi * ar))
        row = lax.broadcasted_iota(jnp.int32, (SUBLANES, nstate), 0)
        for level, k in enumerate((1, 2, 4)):
            mask = (row <= SUBLANES - 1 - k) if reverse else (row >= k)
            tab_ref[2 * level] = jnp.where(mask, pows[k - 1][0], 0.0)
            tab_ref[2 * level + 1] = jnp.where(mask, pows[k - 1][1], 0.0)
        pr = jnp.zeros((SUBLANES, nstate), F32)
        pi = jnp.zeros((SUBLANES, nstate), F32)
        for t in range(SUBLANES):
            k = SUBLANES - 1 - t if reverse else t
            pr = jnp.where(row == t, pows[k][0], pr)
            pi = jnp.where(row == t, pows[k][1], pi)
        tab_ref[6] = pr
        tab_ref[7] = pi

    def body(lr_ref, li_ref, ld_ref, bre_ref, bim_ref, tabf_ref, tabr_ref, bcre_ref, bcim_ref):
        _, ar, ai, _, _, fr, fi = _zoh(lr_ref[...], li_ref[...], ld_ref[...])
        tables(tabf_ref, ar, ai, False)
        tables(tabr_ref, ar, -ai, True)
        bre = bre_ref[...]
        bim = bim_ref[...]
        bcre_ref[...] = (fr * bre - fi * bim).astype(BF16)
        bcim_ref[...] = (fr * bim + fi * bre).astype(BF16)

    vmem = pl.BlockSpec(memory_space=pltpu.VMEM)
    return pl.pallas_call(
        body, name=name, in_specs=[vmem] * 5, out_specs=[vmem] * 4,
        out_shape=[jax.ShapeDtypeStruct((8, SUBLANES, nstate), F32)] * 2
        + [jax.ShapeDtypeStruct(b_re.shape, BF16)] * 2)(lr, li, ld, b_re, b_im)


def _s5_prep_bwd(lr, li, ld, b_re, b_im, da_re, da_im, dbc_re, dbc_im, name):
    def body(lr_ref, li_ref, ld_ref, bre_ref, bim_ref, dar_ref, dai_ref, dbcre_ref, dbcim_ref,
             dlr_ref, dli_ref, dld_ref, dbre_ref, dbim_ref):
        lr, li = lr_ref[...], li_ref[...]
        dt, ar, ai, den, nr, fr, fi = _zoh(lr, li, ld_ref[...])
        bre, bim = bre_ref[...], bim_ref[...]
        gre, gim = dbcre_ref[...], dbcim_ref[...]
        dbre_ref[...] = fr * gre + fi * gim
        dbim_ref[...] = fr * gim - fi * gre
        g_fr = jnp.sum(gre * bre + gim * bim, axis=0, keepdims=True)
        g_fi = jnp.sum(gim * bre - gre * bim, axis=0, keepdims=True)
        g_ar = dar_ref[...] + (g_fr * lr - g_fi * li) / den
        g_ai = dai_ref[...] + (g_fr * li + g_fi * lr) / den
        d_lr = (g_fr * (nr - 2.0 * fr * lr) + g_fi * (ai - 2.0 * fi * lr)) / den
        d_li = (g_fr * (ai - 2.0 * fr * li) - g_fi * (nr + 2.0 * fi * li)) / den
        g_logmag = g_ar * ar + g_ai * ai
        g_ang = g_ai * ar - g_ar * ai
        dlr_ref[...] = d_lr + g_logmag * dt
        dli_ref[...] = d_li + g_ang * dt
        d_ld = (g_logmag * lr + g_ang * li) * dt
        n = d_ld.shape[1]
        sh = 1
        while sh < STATE:
            d_ld = d_ld + pltpu.roll(d_ld, n - sh, 1)
            sh *= 2
        dld_ref[...] = d_ld

    vmem = pl.BlockSpec(memory_space=pltpu.VMEM)
    row = jax.ShapeDtypeStruct(lr.shape, F32)
    return pl.pallas_call(
        body, name=name, in_specs=[vmem] * 9, out_specs=[vmem] * 5,
        out_shape=[row, row, row, jax.ShapeDtypeStruct(b_re.shape, F32), jax.ShapeDtypeStruct(b_re.shape, F32)],
    )(lr, li, ld, b_re, b_im, da_re, da_im, dbc_re, dbc_im)


def _compact_b(bb):
    bq = bb.reshape(N_GROUPS // 8, 8, STATE, GROUP)
    m = jnp.einsum("ab,qbph->qahbp", jnp.eye(8, dtype=bb.dtype), bq).reshape(N_GROUPS // 8, LANES, 8 * STATE)
    return m.transpose(1, 0, 2).reshape(LANES, N_GROUPS * STATE)


def _expand_b(m):
    d = m.reshape(8, GROUP, N_GROUPS // 8, 8, STATE)
    return jnp.einsum("ahqap->qahp", d).reshape(N_GROUPS, GROUP, STATE)


def _compact_c(c):
    cq = c.reshape(N_GROUPS // 8, 8, GROUP, STATE)
    return jnp.einsum("ab,qbhp->qbpah", jnp.eye(8, dtype=c.dtype), cq).reshape(N_GROUPS * STATE, LANES)


def _expand_c(m):
    d = m.reshape(N_GROUPS // 8, 8, STATE, 8, GROUP)
    return jnp.einsum("qbpbh->qbhp", d).reshape(N_GROUPS, GROUP, STATE)


def _local_step(x, target, p, ex):
    seq, d = x.shape
    n_real = N_META + seq
    tp = -(-n_real // ROW_ALIGN) * ROW_ALIGN

    nstate = N_GROUPS * STATE
    s5 = (p["ssm_lam_re"].reshape(1, nstate), p["ssm_lam_im"].reshape(1, nstate),
          jnp.repeat(p["ssm_log_dt"].reshape(-1), STATE).reshape(1, nstate),
          _compact_b(p["ssm_b_re"]), _compact_b(p["ssm_b_im"]))
    tab_f, tab_r, bc_re, bc_im = _s5_prep(*s5, "s5_prep")
    cc_re = _compact_c(p["ssm_c_re"]).astype(BF16)
    cc_im = _compact_c(p["ssm_c_im"]).astype(BF16)
    dskip = p["ssm_d"].reshape(1, -1)
    dh = dskip.shape[1]

    h0, hn1 = _input_norm_fwd(x, p["meta_tokens"], p["norm_mix_g"] + ex.zero, tp, "norm_mix")
    first = ex.weights("first", hn1)
    proj = _mm(hn1, first["w_in"], "nn", "proj")
    started = ex.forward("mid", proj)
    co, y, g = _seq_fwd(proj, p["conv_w"] + started[0, 0], bc_re, bc_im, cc_re, cc_im, dskip, tab_f, "seq_fwd")
    mid = ex.weights("mid", g)
    started = ex.forward("late", g)
    z = _mm(g, mid["ssm_w_glu"], "nn", "glu", after=started)
    mixed = _mix_fwd(co, y, z, p["gain_conv_out"], p["gain_ssm_out"], "mix_fwd")
    mo = _mm(mixed, mid["w_out"], "nn", "out_proj")
    h1, hn2 = _norm_fwd(h0, p["norm_ffn_g"], "norm_ffn", res=mo)
    late = ex.weights("late", hn2)
    up = _mm(hn2, late["w_up"], "nn", "up_proj")
    act = _ffn_act(up, p["ffn_conv_w"], p["ffn_conv_b"], "ffn_act")
    dn = _mm(act, late["w_down"], "nn", "down_proj")
    loss, dh2, dh2b, d_gfin = _loss_bwd(h1, dn, target, p["norm_final_g"], n_real, "loss_bwd")

    g_w_down = _mm(act, dh2b, "tn", "g_w_down")
    dact = _mm(dh2b, late["w_down"], "nt", "d_act")
    dup, dfw_a, dfw_v, dfb_a, dfb_v = _ffn_bwd(up, dact, p["ffn_conv_w"], p["ffn_conv_b"], "ffn_bwd")
    g_w_up = _mm(hn2, dup, "tn", "g_w_up")
    started = ex.grads_ready("late", {"w_up": g_w_up, "w_down": g_w_down})
    dhn2 = _mm(dup, late["w_up"], "nt", "d_hn2", after=started)
    started = ex.grads_send("late", dhn2)
    dh1, dh1b, d_gffn = _norm_bwd(h1, p["norm_ffn_g"] + started[0, 0], dhn2, dh2, "norm_ffn_bwd")
    g_w_out = _mm(mixed, dh1b, "tn", "g_w_out")
    dmixed = _mm(dh1b, mid["w_out"], "nt", "d_mixed")
    dco, dz, dgp, d_gc, d_gs = _mix_bwd(dmixed, co, y, z, p["gain_conv_out"], p["gain_ssm_out"], "mix_bwd")
    g_w_glu = _mm(g, dz, "tn", "g_w_glu")
    started = ex.grads_ready("mid", {"ssm_w_glu": g_w_glu, "w_out": g_w_out})
    dg = _mm(dz, mid["ssm_w_glu"], "nt", "d_gelu", acc_in=dgp, after=started)
    started = ex.grads_send("mid", dg)
    dproj, d_conv_w = _conv_bwd(proj, dco, p["conv_w"] + started[0, 0], "conv_bwd")
    (dproj, dbc_re, dbc_im, dcc_re, dcc_im, d_dskip, da_re, da_im) = _ssm_bwd(
        proj, y, dg, dproj, bc_re, bc_im, cc_re, cc_im, dskip, tab_f, tab_r, "ssm_bwd")
    g_w_in = _mm(hn1, dproj, "tn", "g_w_in")
    started = ex.grads_ready("first", {"w_in": g_w_in})
    dhn1 = _mm(dproj, first["w_in"], "nt", "d_hn1", after=started)
    started = ex.grads_send("first", dhn1)
    grad_x, d_meta, d_gmix = _input_norm_bwd(h0, p["norm_mix_g"] + started[0, 0], dhn1, dh1, n_real, "norm_mix_bwd")

    d_lam_re, d_lam_im, d_log_dt, d_b_re, d_b_im = _s5_prep_bwd(*s5, da_re, da_im, dbc_re, dbc_im, "s5_prep_bwd")
    d_lam_re, d_lam_im = d_lam_re.reshape(N_GROUPS, STATE), d_lam_im.reshape(N_GROUPS, STATE)
    d_log_dt = d_log_dt[0, ::STATE]
    d_b_re, d_b_im = _expand_b(d_b_re), _expand_b(d_b_im)
    grads = {
        "meta_tokens": d_meta, "norm_mix_g": d_gmix, "w_in": g_w_in, "conv_w": d_conv_w,
        "ssm_lam_re": d_lam_re, "ssm_lam_im": d_lam_im, "ssm_log_dt": d_log_dt,
        "ssm_b_re": d_b_re, "ssm_b_im": d_b_im, "ssm_c_re": _expand_c(dcc_re), "ssm_c_im": _expand_c(dcc_im),
        "ssm_d": d_dskip.reshape(N_GROUPS, GROUP), "ssm_w_glu": g_w_glu,
        "gain_conv_out": d_gc, "gain_ssm_out": d_gs, "w_out": g_w_out, "norm_ffn_g": d_gffn,
        "w_up": g_w_up, "ffn_conv_w": jnp.concatenate([dfw_a, dfw_v], axis=1),
        "ffn_conv_b": jnp.concatenate([dfb_a, dfb_v], axis=1), "w_down": g_w_down, "norm_final_g": d_gfin,
    }
    return loss[0, 0], grad_x, grads


def _view(ref, axis, start, size):
    idx = [slice(None)] * len(ref.shape)
    idx[axis] = pl.ds(start, size)
    return ref.at[tuple(idx)]


def _exchange(name, ins, outs, aliases, local_copies, remote_copies):
    ni, no = len(ins), len(outs)
    nl, nr = len(local_copies), len(remote_copies)

    def body(*refs):
        in_refs, out_refs = refs[:ni], refs[ni:ni + no]
        send_sems, recv_sems, local_sems = refs[ni + no:]
        x, y, c = lax.axis_index("x"), lax.axis_index("y"), lax.axis_index("c")
        pos = (x, y, c, 2 * x + y)
        locals_ = [pltpu.make_async_copy(s(in_refs, out_refs, pos), d(in_refs, out_refs, pos), local_sems.at[i])
                   for i, (s, d) in enumerate(local_copies)]
        remotes = []
        for i, (s, d, flip) in enumerate(remote_copies):
            peer = (1 - x if "x" in flip else x, 1 - y if "y" in flip else y, 1 - c if "c" in flip else c)
            remotes.append(pltpu.make_async_remote_copy(
                src_ref=s(in_refs, out_refs, pos), dst_ref=d(in_refs, out_refs, pos),
                send_sem=send_sems.at[i], recv_sem=recv_sems.at[i], device_id=peer, device_id_type=MESH))
        for cp in locals_ + remotes:
            cp.start()
        for cp in remotes:
            cp.wait_recv()
        for cp in remotes:
            cp.wait_send()
        for cp in locals_:
            cp.wait()

    hbm = pl.BlockSpec(memory_space=pl.ANY)
    return pl.pallas_call(
        body, name=name, in_specs=[hbm] * ni, out_specs=[hbm] * no, out_shape=outs,
        input_output_aliases=aliases,
        scratch_shapes=[pltpu.SemaphoreType.DMA((nr,)), pltpu.SemaphoreType.DMA((nr,)),
                        pltpu.SemaphoreType.DMA((max(nl, 1),))],
    )(*ins)


BIG = {"w_in": (0, 1), "ssm_w_glu": (1, 0), "w_out": (1, 0), "w_up": (0, 1), "w_down": (1, 0)}
BIG_NAMES = tuple(BIG)
FLIPS = ("y", "x", "xy")


def _peer_chip(pos, flip):
    x, y, _, _ = pos
    return 2 * (1 - x if "x" in flip else x) + (1 - y if "y" in flip else y)


def _block_rows(rows, cols, itemsize, mult):
    return _pick_tile(rows, max(mult, (2 * 1024 * 1024) // (cols * itemsize)), mult)


def _cast_into_full(w, kc, shard_axis, name):
    r, cdim = w.shape
    tr = _block_rows(r, cdim, 4, 16)
    nb = r // tr

    def body(kc_ref, w_ref, o_ref):
        o_ref[...] = w_ref[...].astype(BF16)

    if shard_axis == 1:
        full, o_spec = (r, 4 * cdim), pl.BlockSpec((tr, cdim), lambda i, kc: (i, kc[0]))
    else:
        full, o_spec = (4 * r, cdim), pl.BlockSpec((tr, cdim), lambda i, kc: (kc[0] * nb + i, 0))
    return pl.pallas_call(
        body, name=name,
        grid_spec=pltpu.PrefetchScalarGridSpec(
            num_scalar_prefetch=1, grid=(nb,), in_specs=[pl.BlockSpec((tr, cdim), lambda i, kc: (i, 0))],
            out_specs=o_spec),
        out_shape=jax.ShapeDtypeStruct(full, BF16), compiler_params=_cparams("parallel"))(kc, w)


def _pair_sum(g, recv, kc, half_axis, name, out_dtype):
    hr, hc = recv.shape
    tr = _block_rows(hr, hc, 4, 16)
    nb = hr // tr

    def body(kc_ref, g_ref, r_ref, o_ref):
        o_ref[...] = (g_ref[...] + r_ref[...]).astype(out_dtype)

    if half_axis == 0:
        g_spec = pl.BlockSpec((tr, hc), lambda i, kc: (kc[1] * nb + i, 0))
    elif half_axis == 1:
        g_spec = pl.BlockSpec((tr, hc), lambda i, kc: (i, kc[1]))
    else:
        g_spec = pl.BlockSpec((tr, hc), lambda i, kc: (i, 0))
    same = pl.BlockSpec((tr, hc), lambda i, kc: (i, 0))
    return pl.pallas_call(
        body, name=name,
        grid_spec=pltpu.PrefetchScalarGridSpec(num_scalar_prefetch=1, grid=(nb,), in_specs=[g_spec, same],
                                               out_specs=same),
        out_shape=jax.ShapeDtypeStruct((hr, hc), out_dtype), compiler_params=_cparams("parallel"))(kc, g, recv)


def _chip_sum(own, recv, kc, own_axis, out_axis, name):
    _, sr, sc = recv.shape
    tr = _block_rows(sr, sc, 4, 16)
    nb = sr // tr

    def body(kc_ref, o_ref, r_ref, t_ref):
        k = kc_ref[0]
        own_v = o_ref[...].astype(F32)
        r = [r_ref[m].astype(F32) for m in range(3)]
        terms = []
        for kk in range(4):
            m = jnp.bitwise_xor(k, kk)
            terms.append(jnp.where(m == 0, own_v, jnp.where(m == 1, r[0], jnp.where(m == 2, r[1], r[2]))))
        t_ref[...] = (terms[0] + terms[1]) + (terms[2] + terms[3])

    if own_axis == 0:
        own_spec = pl.BlockSpec((tr, sc), lambda i, kc: (kc[0] * nb + i, 0))
    elif own_axis == 1:
        own_spec = pl.BlockSpec((tr, sc), lambda i, kc: (i, kc[0]))
    else:
        own_spec = pl.BlockSpec((tr, sc), lambda i, kc: (kc[1] * nb + i, 0))
    if out_axis == 0:
        out_full, out_spec = (2 * sr, sc), pl.BlockSpec((tr, sc), lambda i, kc: (kc[1] * nb + i, 0))
    else:
        out_full, out_spec = (sr, 2 * sc), pl.BlockSpec((tr, sc), lambda i, kc: (i, kc[1]))
    return pl.pallas_call(
        body, name=name,
        grid_spec=pltpu.PrefetchScalarGridSpec(
            num_scalar_prefetch=1, grid=(nb,),
            in_specs=[own_spec, pl.BlockSpec((3, tr, sc), lambda i, kc: (0, i, 0))],
            out_specs=out_spec),
        out_shape=jax.ShapeDtypeStruct(out_full, F32), compiler_params=_cparams("parallel"))(kc, own, recv)


def _adamw(w, g, m, v, name):
    r, cdim = w.shape
    tr = _block_rows(r, cdim, 4, 8)
    c1 = 1.0 - ADAM_B1 ** ADAM_STEP
    c2 = 1.0 - ADAM_B2 ** ADAM_STEP

    def body(w_ref, g_ref, m_ref, v_ref, go_ref, d_ref, nm_ref, nv_ref):
        gv = g_ref[...]
        go_ref[...] = gv
        nm = ADAM_B1 * m_ref[...] + (1.0 - ADAM_B1) * gv
        nv = ADAM_B2 * v_ref[...] + (1.0 - ADAM_B2) * (gv * gv)
        d_ref[...] = -ADAM_LR * ((nm / c1) / (jnp.sqrt(nv / c2) + ADAM_EPS) + ADAM_WD * w_ref[...])
        nm_ref[...] = nm
        nv_ref[...] = nv

    spec = _rows(cdim, tr)
    return pl.pallas_call(body, name=name, grid=(r // tr,), in_specs=[spec] * 4, out_specs=[spec] * 4,
                          out_shape=[jax.ShapeDtypeStruct((r, cdim), F32)] * 4,
                          compiler_params=_cparams("parallel"))(w, g, m, v)


def _adamw_whole(ws, gs, ms, vs, name):
    n = len(ws)
    c1 = 1.0 - ADAM_B1 ** ADAM_STEP
    c2 = 1.0 - ADAM_B2 ** ADAM_STEP

    def body(*refs):
        for i in range(n):
            w_ref, g_ref, m_ref, v_ref, d_ref, nm_ref, nv_ref = [refs[j * n + i] for j in range(7)]
            gv = g_ref[...]
            nm = ADAM_B1 * m_ref[...] + (1.0 - ADAM_B1) * gv
            nv = ADAM_B2 * v_ref[...] + (1.0 - ADAM_B2) * (gv * gv)
            d_ref[...] = -ADAM_LR * ((nm / c1) / (jnp.sqrt(nv / c2) + ADAM_EPS) + ADAM_WD * w_ref[...])
            nm_ref[...] = nm
            nv_ref[...] = nv

    vmem = pl.BlockSpec(memory_space=pltpu.VMEM)
    out = pl.pallas_call(body, name=name, in_specs=[vmem] * (4 * n), out_specs=[vmem] * (3 * n),
                         out_shape=[jax.ShapeDtypeStruct(a.shape, F32) for a in ws] * 3,
                         compiler_params=pltpu.CompilerParams(vmem_limit_bytes=VMEM_LIMIT))(*ws, *gs, *ms, *vs)
    return out[:n], out[n:2 * n], out[2 * n:]


SIDE_EFFECT = pltpu.SideEffectType.DATAFLOW_SIDE_EFFECTING


def _descriptors(copies, refs, send_sems, recv_sems):
    x, y, c = lax.axis_index("x"), lax.axis_index("y"), lax.axis_index("c")
    pos = (x, y, c, 2 * x + y)
    out = []
    for i, (s, d, flip) in enumerate(copies):
        peer = (1 - x if "x" in flip else x, 1 - y if "y" in flip else y, 1 - c if "c" in flip else c)
        out.append(pltpu.make_async_remote_copy(
            src_ref=s(refs, refs, pos), dst_ref=d(refs, refs, pos),
            send_sem=send_sems.at[i], recv_sem=recv_sems.at[i], device_id=peer, device_id_type=MESH))
    return out


def _exchange_start(name, bufs, copies, after=None):
    n, nr = len(bufs), len(copies)
    na = 0 if after is None else 1

    def body(*refs):
        for cp in _descriptors(copies, refs[:n], refs[n + na], refs[n + na + 1]):
            cp.start()
        token = refs[2 * n + na + 2]
        token[...] = jnp.zeros_like(token)

    hbm = pl.BlockSpec(memory_space=pltpu.HBM)
    sem = pl.BlockSpec(memory_space=pltpu.SEMAPHORE)
    out = pl.pallas_call(
        body, name=name,
        in_specs=[hbm] * n + [pl.BlockSpec(memory_space=pl.ANY)] * na,
        out_specs=(sem, sem, *[hbm] * n, pl.BlockSpec(memory_space=pltpu.VMEM)),
        out_shape=(pltpu.SemaphoreType.DMA((nr,)), pltpu.SemaphoreType.DMA((nr,)),
                   *[pltpu.HBM(b.shape, b.dtype) for b in bufs], jax.ShapeDtypeStruct((SUBLANES, LANES), F32)),
        input_output_aliases={i: 2 + i for i in range(n)},
        compiler_params=pltpu.CompilerParams(has_side_effects=SIDE_EFFECT),
    )(*[pltpu.with_memory_space_constraint(b, pltpu.HBM) for b in bufs], *([after] * na))
    return out[0], out[1], list(out[2:2 + n]), out[2 + n]


def _exchange_wait(name, send_sems, recv_sems, bufs, copies, after):
    n = len(bufs)

    def body(*refs):
        for cp in _descriptors(copies, refs[:n], refs[n], refs[n + 1]):
            cp.wait_send()
            cp.wait_recv()

    hbm = pl.BlockSpec(memory_space=pltpu.HBM)
    sem = pl.BlockSpec(memory_space=pltpu.SEMAPHORE)
    out = pl.pallas_call(
        body, name=name,
        in_specs=[hbm] * n + [sem, sem, pl.BlockSpec(memory_space=pl.ANY)],
        out_specs=tuple([hbm] * n),
        out_shape=tuple(pltpu.HBM(b.shape, b.dtype) for b in bufs),
        input_output_aliases={i: i for i in range(n)},
        compiler_params=pltpu.CompilerParams(has_side_effects=SIDE_EFFECT),
    )(*bufs, send_sems, recv_sems, after)
    return list(out)


FIRST = ("w_in",)
MID = ("ssm_w_glu", "w_out")
LATE = ("w_up", "w_down")
GROUPS = {"first": FIRST, "mid": MID, "late": LATE}


def _gather_copies(names, shard_shapes):
    def region(i, chip, c):
        half_axis, shard_axis = BIG[names[i]]
        ssize = shard_shapes[i][shard_axis]
        hsize = shard_shapes[i][half_axis] // 2
        return lambda ref: _view(_view(ref, shard_axis, chip * ssize, ssize), half_axis, c * hsize, hsize)

    ici, d2d = [], []
    for i in range(len(names)):
        for flip in FLIPS:
            ici.append((lambda I, O, pos, i=i: region(i, pos[3], pos[2])(I[i]),
                        lambda I, O, pos, i=i: region(i, pos[3], pos[2])(O[i]), flip))
            d2d.append((lambda I, O, pos, i=i, flip=flip: region(i, _peer_chip(pos, flip), pos[2])(I[i]),
                        lambda I, O, pos, i=i, flip=flip: region(i, _peer_chip(pos, flip), pos[2])(O[i]), "c"))
    return ici, d2d


def _half_shape(n, shape):
    r, cdim = shape
    return (r // 2, cdim) if BIG[n][0] == 0 else (r, cdim // 2)


def _sub_shape(n, shape):
    hr, hc = _half_shape(n, shape)
    return (hr, hc // 4) if BIG[n][1] == 1 else (hr // 4, hc)


def _pair_copies(names, shapes, with_pack, dst_off):
    n = len(names)

    def other_half(i, ref, pos):
        half_axis = BIG[names[i]][0]
        hsize = shapes[i][half_axis] // 2
        return _view(ref, half_axis, (1 - pos[2]) * hsize, hsize)

    copies = [(lambda I, O, pos, i=i: other_half(i, I[i], pos), lambda I, O, pos, i=i: O[dst_off + i], "c")
              for i in range(n)]
    if with_pack:
        copies.append((lambda I, O, pos: I[n], lambda I, O, pos: O[dst_off + n], "c"))
    return copies


def _chip_copies(names, shapes, pack_rows, dst_off):
    n = len(names)

    def piece(i, ref, chip):
        shard_axis = BIG[names[i]][1]
        ssize = _sub_shape(names[i], shapes[i])[shard_axis]
        return _view(ref, shard_axis, chip * ssize, ssize)

    copies = []
    for i in range(n):
        for slot, flip in enumerate(FLIPS):
            copies.append((lambda I, O, pos, i=i, flip=flip: piece(i, I[i], _peer_chip(pos, flip)),
                           lambda I, O, pos, i=i, slot=slot: O[dst_off + i].at[slot], flip))
    if pack_rows:
        for slot, flip in enumerate(FLIPS):
            copies.append((lambda I, O, pos: _view(I[n], 0, pos[2] * (pack_rows // 2), pack_rows // 2),
                           lambda I, O, pos, slot=slot: O[dst_off + n].at[slot], flip))
    return copies


class _Exchanges:
    def __init__(self, shards, tiny, kc):
        self.kc = kc
        wb = {n: _cast_into_full(shards[n], kc, BIG[n][1], "cast_" + n) for n in BIG_NAMES}
        own = (lambda I, O, pos: I[0], lambda I, O, pos: O[0].at[pos[3]])
        self.tiny_all = _exchange("gather_tiny", [tiny], [jax.ShapeDtypeStruct((4,) + tiny.shape, F32)], {},
                                  [own], [own + (flip,) for flip in FLIPS])[0]
        self.gathering, self.forwarding, self.pairing, self.reducing = {}, {}, {}, {}
        after = self.tiny_all
        self.zero = 0.0
        for group, names in GROUPS.items():
            copies = _gather_copies(names, [shards[n].shape for n in names])
            started = _exchange_start("gather_%s_start" % group, [wb[n] for n in names], copies[0], after)
            self.gathering[group] = (started, copies)
            after = started[2][0]
            self.zero = self.zero + started[3][0, 0]

    def forward(self, group, after):
        (send_sems, recv_sems, bufs, _), (ici, d2d) = self.gathering[group]
        got = _exchange_wait("gather_%s_wait" % group, send_sems, recv_sems, bufs, ici, after)
        self.forwarding[group] = (_exchange_start("forward_%s_start" % group, got, d2d), d2d)
        return self.forwarding[group][0][3]

    def weights(self, group, after):
        if group not in self.forwarding:
            after = self.forward(group, after)
        (send_sems, recv_sems, bufs, _), d2d = self.forwarding[group]
        full = _exchange_wait("forward_%s_wait" % group, send_sems, recv_sems, bufs, d2d, after)
        return dict(zip(GROUPS[group], full))

    def grads_ready(self, group, grads):
        names = GROUPS[group]
        gs = [grads[n] for n in names]
        land = [lax.empty(_half_shape(n, g.shape), F32) for n, g in zip(names, gs)]
        copies = _pair_copies(names, [g.shape for g in gs], False, len(names))
        started = _exchange_start("pair_%s_start" % group, gs + land, copies)
        self.pairing[group] = (started, copies)
        return started[3]

    def grads_send(self, group, after):
        names = GROUPS[group]
        n = len(names)
        (send_sems, recv_sems, bufs, _), copies = self.pairing[group]
        bufs = _exchange_wait("pair_%s_wait" % group, send_sems, recv_sems, bufs, copies, after)
        chip = [_pair_sum(bufs[i], bufs[n + i], self.kc, BIG[names[i]][0], "pair_sum_" + names[i], BF16)
                for i in range(n)]
        shapes = [bufs[i].shape for i in range(n)]
        land = [lax.empty((3,) + _sub_shape(names[i], shapes[i]), BF16) for i in range(n)]
        copies = _chip_copies(names, shapes, 0, n)
        started = _exchange_start("reduce_%s_start" % group, chip + land, copies)
        self.reducing[group] = (started, copies)
        return started[3]

    def finish_pack(self, pack):
        kc = self.kc
        prow = pack.shape[0] // 2
        recv = _exchange("reduce_d2d", [pack], [jax.ShapeDtypeStruct(pack.shape, F32)], {}, [],
                         _pair_copies((), [], True, 0))
        chip_pack = _pair_sum(pack, recv[0], kc, None, "pair_sum_pack", F32)
        copies = _chip_copies((), [], pack.shape[0], 1)
        land = lax.empty((3, prow, pack.shape[1]), F32)
        pack_sems_s, pack_sems_r, pack_bufs, after = _exchange_start("reduce_pack_start", [chip_pack, land], copies)

        names, chips, recvs = (), [], []
        for group, group_names in GROUPS.items():
            (send_sems, recv_sems, bufs, _), group_copies = self.reducing[group]
            bufs = _exchange_wait("reduce_%s_wait" % group, send_sems, recv_sems, bufs, group_copies, after)
            n = len(group_names)
            names, chips, recvs = names + group_names, chips + bufs[:n], recvs + bufs[n:]
            after = bufs[n]
        total = [_chip_sum(chips[i], recvs[i], kc, BIG[n][1], BIG[n][0], "chip_sum_" + n)
                 for i, n in enumerate(names)]

        def my_half(half_axis, ref, pos):
            hsize = ref.shape[half_axis] // 2
            return _view(ref, half_axis, pos[2] * hsize, hsize)

        swap = [(lambda I, O, pos, i=i, n=n: my_half(BIG[n][0], I[i], pos),
                 lambda I, O, pos, i=i, n=n: my_half(BIG[n][0], O[i], pos), "c") for i, n in enumerate(names)]
        self.swapping = (_exchange_start("swap_start", total, swap), swap, names)

        chip_pack, recv_pack = _exchange_wait("reduce_pack_wait", pack_sems_s, pack_sems_r, pack_bufs, copies,
                                              self.swapping[0][3])
        total_pack = _chip_sum(chip_pack, recv_pack, kc, None, 0, "chip_sum_pack")
        swap = [(lambda I, O, pos: my_half(0, I[0], pos), lambda I, O, pos: my_half(0, O[0], pos), "c")]
        return _exchange("swap_pack", [total_pack], [jax.ShapeDtypeStruct(pack.shape, F32)], {0: 0}, [], swap)[0]

    def finish_big(self, after):
        (send_sems, recv_sems, bufs, _), swap, names = self.swapping
        return dict(zip(names, _exchange_wait("swap_wait", send_sems, recv_sems, bufs, swap, after)))


WEIGHTS = ("meta_tokens", "norm_mix_g", "w_in", "conv_w", "ssm_lam_re", "ssm_lam_im", "ssm_log_dt", "ssm_b_re",
           "ssm_b_im", "ssm_c_re", "ssm_c_im", "ssm_d", "ssm_w_glu", "gain_conv_out", "gain_ssm_out", "w_out",
           "norm_ffn_g", "w_up", "ffn_conv_w", "ffn_conv_b", "w_down", "norm_final_g")
TINY_SHARDED = ("meta_tokens", "conv_w", "ffn_conv_w")
REPLICATED = tuple(n for n in WEIGHTS if n not in BIG and n not in TINY_SHARDED)
PACK_COLS = 512


def _pack(arrays, row_mult, cols):
    flat = jnp.concatenate([a.reshape(-1).astype(F32) for a in arrays])
    n = flat.shape[0]
    total = -(-n // (row_mult * cols)) * (row_mult * cols)
    return jnp.concatenate([flat, jnp.zeros((total - n,), F32)]).reshape(total // cols, cols)


def _unpack(packed, shapes):
    flat = packed.reshape(-1)
    out, off = [], 0
    for s in shapes:
        n = math.prod(s)
        out.append(flat[off:off + n].reshape(s))
        off += n
    return out


def kernel(x, meta_tokens, norm_mix_g, w_in, conv_w, ssm_lam_re, ssm_lam_im, ssm_log_dt, ssm_b_re, ssm_b_im, ssm_c_re, ssm_c_im, ssm_d, ssm_w_glu, gain_conv_out, gain_ssm_out, w_out, norm_ffn_g, w_up, ffn_conv_w, ffn_conv_b, w_down, norm_final_g, loss_target, m_meta_tokens, m_norm_mix_g, m_w_in, m_conv_w, m_ssm_lam_re, m_ssm_lam_im, m_ssm_log_dt, m_ssm_b_re, m_ssm_b_im, m_ssm_c_re, m_ssm_c_im, m_ssm_d, m_ssm_w_glu, m_gain_conv_out, m_gain_ssm_out, m_w_out, m_norm_ffn_g, m_w_up, m_ffn_conv_w, m_ffn_conv_b, m_w_down, m_norm_final_g, v_meta_tokens, v_norm_mix_g, v_w_in, v_conv_w, v_ssm_lam_re, v_ssm_lam_im, v_ssm_log_dt, v_ssm_b_re, v_ssm_b_im, v_ssm_c_re, v_ssm_c_im, v_ssm_d, v_ssm_w_glu, v_gain_conv_out, v_gain_ssm_out, v_w_out, v_norm_ffn_g, v_w_up, v_ffn_conv_w, v_ffn_conv_b, v_w_down, v_norm_final_g):
    args = dict(locals())
    w = {n: args[n] for n in WEIGHTS}
    mom = {n: args["m_" + n] for n in WEIGHTS}
    var = {n: args["v_" + n] for n in WEIGHTS}
    kx, ky, kc_ = lax.axis_index("x"), lax.axis_index("y"), lax.axis_index("c")
    chip = 2 * kx + ky
    kc = jnp.stack([chip, kc_]).astype(jnp.int32)

    def squeeze(n, a):
        if n == "meta_tokens":
            return a
        if n == "norm_final_g":
            return a.reshape(1, -1)
        a = a[0]
        return a.reshape(1, -1) if a.ndim == 1 else a

    wl = {n: squeeze(n, w[n]) for n in WEIGHTS}
    ml = {n: squeeze(n, mom[n]) for n in WEIGHTS}
    vl = {n: squeeze(n, var[n]) for n in WEIGHTS}

    tiny = _pack([wl[n] for n in TINY_SHARDED], SUBLANES, LANES)
    ex = _Exchanges({n: wl[n] for n in BIG_NAMES}, tiny, kc)
    tiny_shapes = [wl[n].shape for n in TINY_SHARDED]
    tiny_parts = [_unpack(ex.tiny_all[k], tiny_shapes) for k in range(4)]
    p = {n: wl[n] for n in WEIGHTS if n not in BIG}
    for j, n in enumerate(TINY_SHARDED):
        p[n] = jnp.concatenate([tiny_parts[k][j] for k in range(4)], axis=1)
    p["ssm_log_dt"] = wl["ssm_log_dt"].reshape(-1)

    loss_local, grad_x, grads = _local_step(x[0], loss_target[0], p, ex)

    small_names = REPLICATED + TINY_SHARDED
    small_shapes = [tuple(grads[n].shape) for n in small_names] + [(1,)]
    pack = _pack([grads[n] for n in small_names] + [loss_local.reshape(1)], 2 * 16, PACK_COLS)
    g_pack = ex.finish_pack(pack)
    g_small = dict(zip(small_names + ("loss",), _unpack(g_pack, small_shapes)))
    loss = g_small["loss"][0]
    swapped = ("ssm_b_re", "ssm_b_im")

    def view(n, a):
        if n in swapped:
            return jnp.swapaxes(a, -1, -2)
        return a.reshape(1, -1) if a.ndim == 1 else a

    g = {}
    for n in REPLICATED:
        g[n] = g_small[n].reshape(view(n, w[n]).shape)
    for n in TINY_SHARDED:
        cols = wl[n].shape[1]
        g[n] = lax.dynamic_slice_in_dim(g_small[n], chip * cols, cols, axis=1).reshape(w[n].shape)
    delta, new_m, new_v = {}, {}, {}
    small = [[view(n, d[n]) for n in small_names] for d in (w, mom, var)]
    small.insert(1, [g[n] for n in small_names])
    for d, outs in zip((delta, new_m, new_v), _adamw_whole(*small, "adamw_small")):
        d.update(zip(small_names, outs))
    for d in (g, delta, new_m, new_v):
        d.update({n: jnp.swapaxes(d[n], -1, -2) for n in swapped})
    g_big = ex.finish_big(delta[small_names[0]])
    for n in BIG_NAMES:
        g[n], delta[n], new_m[n], new_v[n] = _adamw(wl[n], g_big[n], ml[n], vl[n], "adamw_" + n)

    def like(n, a):
        return a.reshape(w[n].shape)

    return (loss, grad_x[None], *[like(n, g[n]) for n in WEIGHTS], *[like(n, delta[n]) for n in WEIGHTS],
            *[like(n, new_m[n]) for n in WEIGHTS], *[like(n, new_v[n]) for n in WEIGHTS])
```

```python
import functools
import math

import jax
import jax.numpy as jnp
from jax import lax
from jax.experimental import pallas as pl
from jax.experimental.pallas import tpu as pltpu

F32 = jnp.float32
BF16 = jnp.bfloat16
MESH = pl.DeviceIdType.MESH

N_META = 16
N_GROUPS = 32
GROUP = 16
STATE = 64
RMS_EPS = 1e-6
ADAM_LR = 0.001
ADAM_B1 = 0.9
ADAM_B2 = 0.999
ADAM_EPS = 1e-08
ADAM_WD = 0.01
ADAM_STEP = 10

LANES = 128
SUBLANES = 8
ROW_ALIGN = 128
ROW_TILES = 4
VMEM_LIMIT = 52 * 1024 * 1024
MM_VMEM_BUDGET = 40 * 1024 * 1024
GELU_C = math.sqrt(2.0 / math.pi)
GELU_A = 0.044715


def _cparams(*sem):
    return pltpu.CompilerParams(dimension_semantics=sem, vmem_limit_bytes=VMEM_LIMIT)


def _pick_tile(dim, cap, mult):
    best = None
    for t in range(mult, min(dim, cap) + 1, mult):
        if dim % t == 0:
            best = t
    return best if best is not None else dim


def _mm(a, b, mode, name, out_dtype=F32, acc_in=None, after=None):
    if mode == "tn":
        kdim, m = a.shape
    else:
        m, kdim = a.shape
    n = b.shape[0] if mode == "nt" else b.shape[1]
    tm = _pick_tile(m, 1408, LANES if mode == "tn" else 16)
    tk = _pick_tile(kdim, 2816, LANES)
    nk = kdim // tk
    out_bytes = jnp.dtype(out_dtype).itemsize
    for cap in (1408, 1024, 512, 256, LANES):
        tn = _pick_tile(n, cap, LANES)
        blocks = 2 * (tm * tk * 2 + tk * tn * 2 + tm * tn * out_bytes * (2 if acc_in is not None else 1))
        if blocks + (tm * tn * 4 if nk > 1 else 0) <= MM_VMEM_BUDGET:
            break
    has_acc = acc_in is not None

    def body(*refs):
        if after is not None:
            refs = refs[1:]
        if has_acc:
            a_ref, b_ref, c_ref, o_ref = refs[:4]
            rest = refs[4:]
        else:
            a_ref, b_ref, o_ref = refs[:3]
            c_ref = None
            rest = refs[3:]
        if mode == "nn":
            p = jnp.dot(a_ref[...], b_ref[...], preferred_element_type=F32)
        elif mode == "nt":
            p = lax.dot_general(a_ref[...], b_ref[...], (((1,), (1,)), ((), ())), preferred_element_type=F32)
        else:
            p = lax.dot_general(a_ref[...], b_ref[...], (((0,), (0,)), ((), ())), preferred_element_type=F32)
        if nk == 1:
            if has_acc:
                p = p + c_ref[...]
            o_ref[...] = p.astype(out_dtype)
        else:
            acc_ref = rest[0]
            k = pl.program_id(2)

            @pl.when(k == 0)
            def _():
                acc_ref[...] = p + c_ref[...] if has_acc else p

            @pl.when(k > 0)
            def _():
                acc_ref[...] += p

            @pl.when(k == nk - 1)
            def _():
                o_ref[...] = acc_ref[...].astype(out_dtype)

    if mode == "tn":
        a_spec = pl.BlockSpec((tk, tm), lambda i, j, k: (k, i))
    else:
        a_spec = pl.BlockSpec((tm, tk), lambda i, j, k: (i, k))
    if mode == "nt":
        b_spec = pl.BlockSpec((tn, tk), lambda i, j, k: (j, k))
    else:
        b_spec = pl.BlockSpec((tk, tn), lambda i, j, k: (k, j))
    o_spec = pl.BlockSpec((tm, tn), lambda i, j, k: (i, j))
    in_specs = [a_spec, b_spec] + ([o_spec] if has_acc else [])
    args = (a, b) + ((acc_in,) if has_acc else ())
    if after is not None:
        in_specs = [pl.BlockSpec(memory_space=pl.ANY)] + in_specs
        args = (after,) + args
    return pl.pallas_call(
        body, name=name, grid=(m // tm, n // tn, nk),
        in_specs=in_specs, out_specs=o_spec,
        out_shape=jax.ShapeDtypeStruct((m, n), out_dtype),
        scratch_shapes=[pltpu.VMEM((tm, tn), F32)] if nk > 1 else [],
        compiler_params=_cparams("parallel", "parallel", "arbitrary"),
    )(*args)


def _mm_rows(a, b, mode, name, ins, outs, epilogue, scratch=()):
    m, kdim = a.shape
    n = b.shape[0] if mode == "nt" else b.shape[1]
    tm = m // ROW_TILES
    tk = _pick_tile(kdim, 2816, LANES)
    nk = kdim // tk
    ni, no = len(ins), len(outs)

    def body(*refs):
        a_ref, b_ref = refs[:2]
        in_refs, out_refs, rest = refs[2:2 + ni], refs[2 + ni:2 + ni + no], refs[2 + ni + no:]
        i = pl.program_id(0)
        if mode == "nn":
            p = jnp.dot(a_ref[...], b_ref[...], preferred_element_type=F32)
        else:
            p = lax.dot_general(a_ref[...], b_ref[...], (((1,), (1,)), ((), ())), preferred_element_type=F32)
        if nk == 1:
            epilogue(p, i, in_refs, out_refs, rest)
        else:
            acc_ref = rest[0]
            k = pl.program_id(1)

            @pl.when(k == 0)
            def _():
                acc_ref[...] = p

            @pl.when(k > 0)
            def _():
                acc_ref[...] += p

            @pl.when(k == nk - 1)
            def _():
                epilogue(acc_ref[...], i, in_refs, out_refs, rest[1:])

    def spec(shape, kind):
        if kind == "rows":
            return pl.BlockSpec((tm,) + tuple(shape[1:]), lambda i, k: (i,) + (0,) * (len(shape) - 1))
        if kind == "whole":
            return pl.BlockSpec(tuple(shape), lambda i, k: (0,) * len(shape))
        return pl.BlockSpec(memory_space=pl.ANY)

    a_spec = pl.BlockSpec((tm, tk), lambda i, k: (i, k))
    b_spec = pl.BlockSpec((n, tk), lambda i, k: (0, k)) if mode == "nt" else pl.BlockSpec((tk, n), lambda i, k: (k, 0))
    return pl.pallas_call(
        body, name=name, grid=(ROW_TILES, nk),
        in_specs=[a_spec, b_spec] + [spec(x.shape, kind) for x, kind in ins],
        out_specs=[spec(shape, kind) for shape, _, kind in outs],
        out_shape=[jax.ShapeDtypeStruct(shape, dtype) for shape, dtype, _ in outs],
        scratch_shapes=([pltpu.VMEM((tm, n), F32)] if nk > 1 else []) + list(scratch),
        compiler_params=_cparams("arbitrary", "arbitrary"),
    )(a, b, *[x for x, _ in ins])


def _rows(shape_cols, tr, dtype=None):
    return pl.BlockSpec((tr, shape_cols), lambda i: (i, 0))


def _const(shape):
    return pl.BlockSpec(shape, lambda i: (0,) * len(shape))


def _rms(x):
    return lax.rsqrt(jnp.mean(x * x, axis=-1, keepdims=True) + RMS_EPS)


def _rms_bwd(x, r, g, dy):
    xn = x * r
    dxn = dy * g
    dx = r * (dxn - xn * jnp.mean(dxn * xn, axis=-1, keepdims=True))
    return dx, dy * xn


def _gelu(y):
    return 0.5 * y * (1.0 + jnp.tanh(GELU_C * (y + GELU_A * y * y * y)))


def _gelu_grad(y):
    t = jnp.tanh(GELU_C * (y + GELU_A * y * y * y))
    return 0.5 * (1.0 + t) + 0.5 * y * (1.0 - t * t) * GELU_C * (1.0 + 3.0 * GELU_A * y * y)


def _sigmoid(z):
    return 1.0 / (1.0 + jnp.exp(-z))


def _proj_res_norm(a, w, h, g, name):
    def epilogue(p, i, ins, outs, _):
        x = ins[0][...] + p
        outs[0][...] = x
        outs[1][...] = (x * _rms(x) * ins[1][...]).astype(BF16)

    return _mm_rows(a, w, "nn", name, [(h, "rows"), (g, "whole")],
                    [(h.shape, F32, "rows"), (h.shape, BF16, "rows")], epilogue)


def _proj_norm_bwd(da, w, h, g, dres, after, name):
    d = h.shape[1]

    def epilogue(p, i, ins, outs, _):
        x = ins[0][...]
        dx, dgs = _rms_bwd(x, _rms(x), ins[1][...], p)
        dh = ins[2][...] + dx
        outs[0][...] = dh
        outs[1][...] = dh.astype(BF16)

        @pl.when(i == 0)
        def _():
            outs[2][...] = jnp.zeros_like(outs[2])

        outs[2][...] += jnp.sum(dgs, axis=0, keepdims=True)

    return _mm_rows(da, w, "nt", name, [(h, "rows"), (g, "whole"), (dres, "rows"), (after, "hbm")],
                    [(h.shape, F32, "rows"), (h.shape, BF16, "rows"), ((1, d), F32, "whole")], epilogue)


def _input_norm_bwd(h, g, dhn, dres, n_real, name):
    tp, d = h.shape
    tr = tp // ROW_TILES

    def body(h_ref, g_ref, dhn_ref, dres_ref, dx_ref, dmeta_ref, dg_ref, stage, sem):
        i = pl.program_id(0)
        x = h_ref[...]
        dx, dgs = _rms_bwd(x, _rms(x), g_ref[...], dhn_ref[...])
        stage[...] = dres_ref[...] + dx

        @pl.when(i == 0)
        def _():
            dg_ref[...] = jnp.zeros_like(dg_ref)
            dmeta_ref[...] = stage[:N_META, :]

        dg_ref[...] += jnp.sum(dgs, axis=0, keepdims=True)
        for t in range(ROW_TILES):
            lo, hi = max(t * tr, N_META), min((t + 1) * tr, n_real)
            if hi > lo:
                @pl.when(i == t)
                def _(t=t, lo=lo, hi=hi):
                    cp = pltpu.make_async_copy(stage.at[pl.ds(lo - t * tr, hi - lo), :],
                                               dx_ref.at[pl.ds(lo - N_META, hi - lo), :], sem)
                    cp.start()
                    cp.wait()

    return pl.pallas_call(
        body, name=name, grid=(ROW_TILES,),
        in_specs=[_rows(d, tr), _const((1, d)), _rows(d, tr), _rows(d, tr)],
        out_specs=[pl.BlockSpec(memory_space=pl.ANY), _const((N_META, d)), _const((1, d))],
        out_shape=[jax.ShapeDtypeStruct((n_real - N_META, d), F32), jax.ShapeDtypeStruct((N_META, d), F32),
                   jax.ShapeDtypeStruct((1, d), F32)],
        scratch_shapes=[pltpu.VMEM((tr, d), F32), pltpu.SemaphoreType.DMA],
        compiler_params=_cparams("arbitrary"))(h, g, dhn, dres)


def _load_token_rows(tok_hbm, buf, sem, tr, n_real, head=None, wait=False, i=None):
    i = pl.program_id(0) if i is None else i
    for t in range(ROW_TILES):
        base = t * tr
        lo, hi = max(base, N_META), min(base + tr, n_real)

        @pl.when(i == t)
        def _(base=base, lo=lo, hi=hi):
            if hi > lo:
                cp = pltpu.make_async_copy(tok_hbm.at[pl.ds(lo - N_META, hi - lo), :],
                                           buf.at[pl.ds(lo - base, hi - lo), :], sem)
                if wait:
                    cp.wait()
                    return
                cp.start()
            if wait:
                return
            if base < N_META:
                buf[0:N_META - base, :] = (jnp.zeros((N_META - base, buf.shape[1]), F32) if head is None
                                           else head[base:N_META, :])
            if hi < base + tr:
                buf[max(hi, base) - base:tr, :] = jnp.zeros((base + tr - max(hi, base), buf.shape[1]), F32)


def _input_norm_fwd(x, meta, g, tp, name):
    seq, d = x.shape
    tr = tp // ROW_TILES
    n_real = N_META + seq

    def body(x_hbm, meta_ref, g_ref, h_ref, hn_ref, buf, sem):
        _load_token_rows(x_hbm, buf, sem, tr, n_real, head=meta_ref)
        _load_token_rows(x_hbm, buf, sem, tr, n_real, wait=True)
        h = buf[...]
        h_ref[...] = h
        hn_ref[...] = (h * _rms(h) * g_ref[...]).astype(BF16)

    return pl.pallas_call(
        body, name=name, grid=(ROW_TILES,),
        in_specs=[pl.BlockSpec(memory_space=pl.ANY), _const((N_META, d)), _const((1, d))],
        out_specs=[_rows(d, tr), _rows(d, tr)],
        out_shape=[jax.ShapeDtypeStruct((tp, d), F32), jax.ShapeDtypeStruct((tp, d), BF16)],
        scratch_shapes=[pltpu.VMEM((tr, d), F32), pltpu.SemaphoreType.DMA],
        compiler_params=_cparams("arbitrary"))(x, meta, g)


def _proj_loss_bwd(act, w, h1, target, g, n_real, name):
    tp, d = h1.shape
    tr = tp // ROW_TILES

    def epilogue(p, i, ins, outs, scratch):
        h1_ref, t_hbm, g_ref = ins
        loss_ref, dh_ref, dhb_ref, dg_ref = outs
        t_buf, sem = scratch
        _load_token_rows(t_hbm, t_buf, sem, tr, n_real, i=i)
        x = h1_ref[...] + p
        r = _rms(x)
        row = i * tr + lax.broadcasted_iota(jnp.int32, (tr, d), 0)
        valid = (row >= N_META) & (row < n_real)
        _load_token_rows(t_hbm, t_buf, sem, tr, n_real, wait=True, i=i)
        e = jnp.where(valid, x * r * g_ref[...] - t_buf[...], 0.0)
        dx, dgs = _rms_bwd(x, r, g_ref[...], e * (1.0 / d))
        dh_ref[...] = dx
        dhb_ref[...] = dx.astype(BF16)

        @pl.when(i == 0)
        def _():
            dg_ref[...] = jnp.zeros_like(dg_ref)
            loss_ref[...] = jnp.zeros_like(loss_ref)

        dg_ref[...] += jnp.sum(dgs, axis=0, keepdims=True)
        loss_ref[...] += (0.5 / d) * jnp.sum(jnp.sum(e * e, axis=0, keepdims=True), axis=1, keepdims=True)

    return _mm_rows(act, w, "nn", name, [(h1, "rows"), (target, "hbm"), (g, "whole")],
                    [((1, LANES), F32, "whole"), ((tp, d), F32, "rows"), ((tp, d), BF16, "rows"),
                     ((1, d), F32, "whole")],
                    epilogue, scratch=[pltpu.VMEM((tr, d), F32), pltpu.SemaphoreType.DMA])


def _mix_fwd(co, y, z, gc, gs, name):
    tp, dh = co.shape
    tr = tp // ROW_TILES

    def body(co_ref, y_ref, z_ref, gc_ref, gs_ref, m_ref):
        c = co_ref[...]
        m_ref[:, :dh] = (c * _rms(c) * gc_ref[...]).astype(BF16)
        so = _gelu(y_ref[...]) * _sigmoid(z_ref[...])
        m_ref[:, dh:] = (so * _rms(so) * gs_ref[...]).astype(BF16)

    return pl.pallas_call(
        body, name=name, grid=(ROW_TILES,),
        in_specs=[_rows(dh, tr)] * 3 + [_const((1, dh))] * 2,
        out_specs=_rows(2 * dh, tr),
        out_shape=jax.ShapeDtypeStruct((tp, 2 * dh), BF16),
        compiler_params=_cparams("parallel"))(co, y, z, gc, gs)


def _proj_mix_bwd(dh1b, w, co, y, z, gc, gs, name):
    tp, dh = co.shape

    def epilogue(p, i, ins, outs, _):
        co_ref, y_ref, z_ref, gc_ref, gs_ref = ins
        dco_ref, dz_ref, dgp_ref, dgc_ref, dgs_ref = outs
        c = co_ref[...]
        dco, dgc = _rms_bwd(c, _rms(c), gc_ref[...], p[:, :dh])
        dco_ref[...] = dco
        gl = _gelu(y_ref[...])
        sg = _sigmoid(z_ref[...])
        so = gl * sg
        dso, dgs = _rms_bwd(so, _rms(so), gs_ref[...], p[:, dh:])
        dz_ref[...] = (dso * gl * sg * (1.0 - sg)).astype(BF16)
        dgp_ref[...] = dso * sg

        @pl.when(i == 0)
        def _():
            dgc_ref[...] = jnp.zeros_like(dgc_ref)
            dgs_ref[...] = jnp.zeros_like(dgs_ref)

        dgc_ref[...] += jnp.sum(dgc, axis=0, keepdims=True)
        dgs_ref[...] += jnp.sum(dgs, axis=0, keepdims=True)

    return _mm_rows(dh1b, w, "nt", name,
                    [(co, "rows"), (y, "rows"), (z, "rows"), (gc, "whole"), (gs, "whole")],
                    [((tp, dh), F32, "rows"), ((tp, dh), BF16, "rows"), ((tp, dh), F32, "rows"),
                     ((1, dh), F32, "whole"), ((1, dh), F32, "whole")], epilogue)


def _shift_down(x, k):
    row = lax.broadcasted_iota(jnp.int32, x.shape, 0)
    return jnp.where(row >= k, pltpu.roll(x, k, 0), 0.0)


def _shift_up(x, k):
    n = x.shape[0]
    row = lax.broadcasted_iota(jnp.int32, x.shape, 0)
    return jnp.where(row < n - k, pltpu.roll(x, n - k, 0), 0.0)


def _dwconv(x, w_ref):
    return w_ref[2:3, :] * x + w_ref[1:2, :] * _shift_down(x, 1) + w_ref[0:1, :] * _shift_down(x, 2)


def _dwconv_bwd(x, dy, w_ref):
    dx = w_ref[2:3, :] * dy + w_ref[1:2, :] * _shift_up(dy, 1) + w_ref[0:1, :] * _shift_up(dy, 2)
    dw = jnp.concatenate([jnp.sum(dy * _shift_down(x, 2), axis=0, keepdims=True),
                          jnp.sum(dy * _shift_down(x, 1), axis=0, keepdims=True),
                          jnp.sum(dy * x, axis=0, keepdims=True)], axis=0)
    return dx, dw


def _scan(s_re, s_im, tab_ref, reverse):
    n_chunks = s_re.shape[0] // SUBLANES
    n_strips = s_re.shape[1] // LANES
    last = 0 if reverse else SUBLANES - 1

    def body(i, carry):
        chunk = (n_chunks - 1 - i) if reverse else i
        r0 = pl.multiple_of(chunk * SUBLANES, SUBLANES)
        out = []
        for st in range(n_strips):
            lanes = slice(st * LANES, (st + 1) * LANES)
            cr, ci = carry[2 * st], carry[2 * st + 1]
            xr = s_re[pl.ds(r0, SUBLANES), lanes]
            xi = s_im[pl.ds(r0, SUBLANES), lanes]
            for level, k in enumerate((1, 2, 4)):
                mr = tab_ref[2 * level, :, lanes]
                mi = tab_ref[2 * level + 1, :, lanes]
                sh = SUBLANES - k if reverse else k
                rr = pltpu.roll(xr, sh, 0)
                ri = pltpu.roll(xi, sh, 0)
                xr, xi = xr + (mr * rr - mi * ri), xi + (mr * ri + mi * rr)
            pwr = tab_ref[6, :, lanes]
            pwi = tab_ref[7, :, lanes]
            xr, xi = xr + (pwr * cr - pwi * ci), xi + (pwr * ci + pwi * cr)
            s_re[pl.ds(r0, SUBLANES), lanes] = xr
            s_im[pl.ds(r0, SUBLANES), lanes] = xi
            out.append(jnp.broadcast_to(xr[last:last + 1, :], (SUBLANES, LANES)))
            out.append(jnp.broadcast_to(xi[last:last + 1, :], (SUBLANES, LANES)))
        return tuple(out)

    zero = jnp.zeros((SUBLANES, LANES), F32)
    lax.fori_loop(0, n_chunks, body, (zero,) * (2 * n_strips))


def _seq_fwd(proj, conv_w, bc_re, bc_im, cc_re, cc_im, dskip, tab_f, name):
    tp = proj.shape[0]
    dh = proj.shape[1] // 4
    nq = dh // LANES
    sw = STATE * N_GROUPS // nq

    def body(b_ref, c_ref, v_ref, u_ref, w_ref, bre_ref, bim_ref, cre_ref, cim_ref, d_ref, tab_ref,
             co_ref, y_ref, g_ref, s_re, s_im):
        co_ref[...] = b_ref[...] * _dwconv(c_ref[...] * v_ref[...], w_ref)
        u = u_ref[...]
        ub = u.astype(BF16)
        s_re[...] = jnp.dot(ub, bre_ref[...], preferred_element_type=F32)
        s_im[...] = jnp.dot(ub, bim_ref[...], preferred_element_type=F32)
        _scan(s_re, s_im, tab_ref, False)
        y = (jnp.dot(s_re[...].astype(BF16), cre_ref[...], preferred_element_type=F32)
             - jnp.dot(s_im[...].astype(BF16), cim_ref[...], preferred_element_type=F32)
             + d_ref[...] * u)
        y_ref[...] = y
        g_ref[...] = _gelu(y).astype(BF16)

    col = lambda off: pl.BlockSpec((tp, LANES), lambda q, off=off: (0, off * nq + q))
    blk = pl.BlockSpec((tp, LANES), lambda q: (0, q))
    return pl.pallas_call(
        body, name=name, grid=(nq,),
        in_specs=[col(0), col(1), col(2), col(3),
                  pl.BlockSpec((3, LANES), lambda q: (0, q)),
                  pl.BlockSpec((LANES, sw), lambda q: (0, q)), pl.BlockSpec((LANES, sw), lambda q: (0, q)),
                  pl.BlockSpec((sw, LANES), lambda q: (q, 0)), pl.BlockSpec((sw, LANES), lambda q: (q, 0)),
                  pl.BlockSpec((1, LANES), lambda q: (0, q)),
                  pl.BlockSpec((8, SUBLANES, sw), lambda q: (0, 0, q))],
        out_specs=[blk, blk, blk],
        out_shape=[jax.ShapeDtypeStruct((tp, dh), F32), jax.ShapeDtypeStruct((tp, dh), F32),
                   jax.ShapeDtypeStruct((tp, dh), BF16)],
        scratch_shapes=[pltpu.VMEM((tp, sw), F32), pltpu.VMEM((tp, sw), F32)],
        compiler_params=_cparams("parallel"),
    )(proj, proj, proj, proj, conv_w, bc_re, bc_im, cc_re, cc_im, dskip, tab_f)


def _conv_bwd(proj, dco, conv_w, name):
    tp = proj.shape[0]
    dh = proj.shape[1] // 4
    nq = dh // LANES

    def body(b_ref, c_ref, v_ref, dco_ref, w_ref, dproj_ref, dw_ref, stage, sem):
        q = pl.program_id(0)
        cg = c_ref[...]
        vg = v_ref[...]
        cv = cg * vg
        dco_v = dco_ref[...]
        dcv, dw = _dwconv_bwd(cv, dco_v * b_ref[...], w_ref)
        dw_ref[...] = dw
        stage[0] = (dco_v * _dwconv(cv, w_ref)).astype(BF16)
        stage[1] = (dcv * vg).astype(BF16)
        stage[2] = (dcv * cg).astype(BF16)
        copies = [pltpu.make_async_copy(stage.at[p], dproj_ref.at[:, pl.ds((p * nq + q) * LANES, LANES)], sem.at[p])
                  for p in range(3)]
        for cp in copies:
            cp.start()
        for cp in copies:
            cp.wait()

    col = lambda off: pl.BlockSpec((tp, LANES), lambda q, off=off: (0, off * nq + q))
    return pl.pallas_call(
        body, name=name, grid=(nq,),
        in_specs=[col(0), col(1), col(2), pl.BlockSpec((tp, LANES), lambda q: (0, q)),
                  pl.BlockSpec((3, LANES), lambda q: (0, q))],
        out_specs=[pl.BlockSpec(memory_space=pl.ANY), pl.BlockSpec((3, LANES), lambda q: (0, q))],
        out_shape=[jax.ShapeDtypeStruct((tp, 4 * dh), BF16), jax.ShapeDtypeStruct((3, dh), F32)],
        scratch_shapes=[pltpu.VMEM((3, tp, LANES), BF16), pltpu.SemaphoreType.DMA((3,))],
        compiler_params=_cparams("arbitrary"),
    )(proj, proj, proj, dco, conv_w)


def _ssm_bwd(proj, y, dg, dproj, bc_re, bc_im, cc_re, cc_im, dskip, tab_f, tab_r, name):
    tp = proj.shape[0]
    dh = proj.shape[1] // 4
    nq = dh // LANES
    sw = STATE * N_GROUPS // nq

    def body(u_ref, y_ref, dg_ref, dproj_in, bre_ref, bim_ref, cre_ref, cim_ref, d_ref, tabf_ref, tabr_ref,
             dproj_ref, dbre_ref, dbim_ref, dcre_ref, dcim_ref, dd_ref, dar_ref, dai_ref,
             s_re, s_im, l_re, l_im, stage, sem):
        del dproj_in
        q = pl.program_id(0)
        nt = (((1,), (1,)), ((), ()))
        tn = (((0,), (0,)), ((), ()))
        u = u_ref[...]
        ub = u.astype(BF16)
        s_re[...] = jnp.dot(ub, bre_ref[...], preferred_element_type=F32)
        s_im[...] = jnp.dot(ub, bim_ref[...], preferred_element_type=F32)
        _scan(s_re, s_im, tabf_ref, False)
        dy = dg_ref[...] * _gelu_grad(y_ref[...])
        dyb = dy.astype(BF16)
        dd_ref[...] = jnp.sum(dy * u, axis=0, keepdims=True)
        l_re[...] = lax.dot_general(dyb, cre_ref[...], nt, preferred_element_type=F32)
        l_im[...] = -lax.dot_general(dyb, cim_ref[...], nt, preferred_element_type=F32)
        dcre_ref[...] = lax.dot_general(s_re[...].astype(BF16), dyb, tn, preferred_element_type=F32)
        dcim_ref[...] = -lax.dot_general(s_im[...].astype(BF16), dyb, tn, preferred_element_type=F32)
        _scan(l_re, l_im, tabr_ref, True)
        for st in range(sw // LANES):
            lanes = slice(st * LANES, (st + 1) * LANES)
            lr = l_re[:, lanes]
            li = l_im[:, lanes]
            pr = _shift_down(s_re[:, lanes], 1)
            pi = _shift_down(s_im[:, lanes], 1)
            dar_ref[:, lanes] = jnp.sum(lr * pr + li * pi, axis=0, keepdims=True)
            dai_ref[:, lanes] = jnp.sum(li * pr - lr * pi, axis=0, keepdims=True)
        lrb = l_re[...].astype(BF16)
        lib = l_im[...].astype(BF16)
        du = (dy * d_ref[...] + lax.dot_general(lrb, bre_ref[...], nt, preferred_element_type=F32)
              + lax.dot_general(lib, bim_ref[...], nt, preferred_element_type=F32))
        stage[...] = du.astype(BF16)
        dbre_ref[...] = lax.dot_general(ub, lrb, tn, preferred_element_type=F32)
        dbim_ref[...] = lax.dot_general(ub, lib, tn, preferred_element_type=F32)
        cp = pltpu.make_async_copy(stage, dproj_ref.at[:, pl.ds((3 * nq + q) * LANES, LANES)], sem)
        cp.start()
        cp.wait()

    blk = pl.BlockSpec((tp, LANES), lambda q: (0, q))
    bspec = pl.BlockSpec((LANES, sw), lambda q: (0, q))
    cspec = pl.BlockSpec((sw, LANES), lambda q: (q, 0))
    tspec = pl.BlockSpec((8, SUBLANES, sw), lambda q: (0, 0, q))
    nstate = STATE * N_GROUPS
    return pl.pallas_call(
        body, name=name, grid=(nq,),
        in_specs=[pl.BlockSpec((tp, LANES), lambda q: (0, 3 * nq + q)), blk, blk, pl.BlockSpec(memory_space=pl.ANY),
                  bspec, bspec, cspec, cspec, pl.BlockSpec((1, LANES), lambda q: (0, q)), tspec, tspec],
        out_specs=[pl.BlockSpec(memory_space=pl.ANY), bspec, bspec, cspec, cspec,
                   pl.BlockSpec((1, LANES), lambda q: (0, q)),
                   pl.BlockSpec((1, sw), lambda q: (0, q)), pl.BlockSpec((1, sw), lambda q: (0, q))],
        out_shape=[jax.ShapeDtypeStruct((tp, 4 * dh), BF16),
                   jax.ShapeDtypeStruct((LANES, nstate), F32), jax.ShapeDtypeStruct((LANES, nstate), F32),
                   jax.ShapeDtypeStruct((nstate, LANES), F32), jax.ShapeDtypeStruct((nstate, LANES), F32),
                   jax.ShapeDtypeStruct((1, dh), F32),
                   jax.ShapeDtypeStruct((1, nstate), F32), jax.ShapeDtypeStruct((1, nstate), F32)],
        input_output_aliases={3: 0},
        scratch_shapes=[pltpu.VMEM((tp, sw), F32)] * 4 + [pltpu.VMEM((tp, LANES), BF16), pltpu.SemaphoreType.DMA],
        compiler_params=_cparams("arbitrary"),
    )(proj, y, dg, dproj, bc_re, bc_im, cc_re, cc_im, dskip, tab_f, tab_r)


FFN_TILE = 256


def _ffn_act(up, fw, fb, name):
    tp, two_ff = up.shape
    dff = two_ff // 2
    tc = FFN_TILE
    nj = dff // tc

    def body(ua_ref, uv_ref, wa_ref, wv_ref, ba_ref, bv_ref, act_ref):
        a = _dwconv(ua_ref[...], wa_ref) + ba_ref[...]
        v = _dwconv(uv_ref[...], wv_ref) + bv_ref[...]
        act_ref[...] = (a * _sigmoid(a) * v).astype(BF16)

    lo = lambda r: pl.BlockSpec((r, tc), lambda j: (0, j))
    hi = lambda r: pl.BlockSpec((r, tc), lambda j: (0, nj + j))
    return pl.pallas_call(
        body, name=name, grid=(nj,),
        in_specs=[lo(tp), hi(tp), lo(3), hi(3), lo(1), hi(1)],
        out_specs=lo(tp),
        out_shape=jax.ShapeDtypeStruct((tp, dff), BF16),
        compiler_params=_cparams("parallel"))(up, up, fw, fw, fb, fb)


def _ffn_bwd(up, dact, fw, fb, name):
    tp, two_ff = up.shape
    dff = two_ff // 2
    tc = FFN_TILE
    nj = dff // tc

    def body(ua_ref, uv_ref, da_ref, wa_ref, wv_ref, ba_ref, bv_ref,
             dup_ref, dwa_ref, dwv_ref, dba_ref, dbv_ref, stage, sem):
        j = pl.program_id(0)
        ua = ua_ref[...]
        uv = uv_ref[...]
        a = _dwconv(ua, wa_ref) + ba_ref[...]
        v = _dwconv(uv, wv_ref) + bv_ref[...]
        sg = _sigmoid(a)
        dact_v = da_ref[...]
        da = dact_v * v * sg * (1.0 + a * (1.0 - sg))
        dv = dact_v * a * sg
        dba_ref[...] = jnp.sum(da, axis=0, keepdims=True)
        dbv_ref[...] = jnp.sum(dv, axis=0, keepdims=True)
        dua, dwa = _dwconv_bwd(ua, da, wa_ref)
        duv, dwv = _dwconv_bwd(uv, dv, wv_ref)
        dwa_ref[...] = dwa
        dwv_ref[...] = dwv
        stage[0] = dua.astype(BF16)
        stage[1] = duv.astype(BF16)
        copies = [pltpu.make_async_copy(stage.at[p], dup_ref.at[:, pl.ds((p * nj + j) * tc, tc)], sem.at[p])
                  for p in range(2)]
        for cp in copies:
            cp.start()
        for cp in copies:
            cp.wait()

    lo = lambda r: pl.BlockSpec((r, tc), lambda j: (0, j))
    hi = lambda r: pl.BlockSpec((r, tc), lambda j: (0, nj + j))
    return pl.pallas_call(
        body, name=name, grid=(nj,),
        in_specs=[lo(tp), hi(tp), lo(tp), lo(3), hi(3), lo(1), hi(1)],
        out_specs=[pl.BlockSpec(memory_space=pl.ANY), lo(3), lo(3), lo(1), lo(1)],
        out_shape=[jax.ShapeDtypeStruct((tp, two_ff), BF16),
                   jax.ShapeDtypeStruct((3, dff), F32), jax.ShapeDtypeStruct((3, dff), F32),
                   jax.ShapeDtypeStruct((1, dff), F32), jax.ShapeDtypeStruct((1, dff), F32)],
        scratch_shapes=[pltpu.VMEM((2, tp, tc), BF16), pltpu.SemaphoreType.DMA((2,))],
        compiler_params=_cparams("arbitrary"))(up, up, dact, fw, fw, fb, fb)


def _zoh(lr, li, ld):
    dt = jnp.exp(ld)
    mag = jnp.exp(lr * dt)
    ang = li * dt
    ar = mag * jnp.cos(ang)
    ai = mag * jnp.sin(ang)
    den = lr * lr + li * li
    nr = ar - 1.0
    fr = (nr * lr + ai * li) / den
    fi = (ai * lr - nr * li) / den
    return dt, ar, ai, den, nr, fr, fi


def _s5_prep(lr, li, ld, b_re, b_im, name):
    nstate = lr.shape[1]

    def tables(tab_ref, ar, ai, reverse):
        pows = [(ar, ai)]
        for _ in range(SUBLANES - 1):
            pr, pi = pows[-1]
            pows.append((pr * ar - pi * ai, pr * ai + pi * ar))
        row = lax.broadcasted_iota(jnp.int32, (SUBLANES, nstate), 0)
        for level, k in enumerate((1, 2, 4)):
            mask = (row <= SUBLANES - 1 - k) if reverse else (row >= k)
            tab_ref[2 * level] = jnp.where(mask, pows[k - 1][0], 0.0)
            tab_ref[2 * level + 1] = jnp.where(mask, pows[k - 1][1], 0.0)
        pr = jnp.zeros((SUBLANES, nstate), F32)
        pi = jnp.zeros((SUBLANES, nstate), F32)
        for t in range(SUBLANES):
            k = SUBLANES - 1 - t if reverse else t
            pr = jnp.where(row == t, pows[k][0], pr)
            pi = jnp.where(row == t, pows[k][1], pi)
        tab_ref[6] = pr
        tab_ref[7] = pi

    def body(lr_ref, li_ref, ld_ref, bre_ref, bim_ref, tabf_ref, tabr_ref, bcre_ref, bcim_ref):
        _, ar, ai, _, _, fr, fi = _zoh(lr_ref[...], li_ref[...], ld_ref[...])
        tables(tabf_ref, ar, ai, False)
        tables(tabr_ref, ar, -ai, True)
        bre = bre_ref[...]
        bim = bim_ref[...]
        bcre_ref[...] = (fr * bre - fi * bim).astype(BF16)
        bcim_ref[...] = (fr * bim + fi * bre).astype(BF16)

    vmem = pl.BlockSpec(memory_space=pltpu.VMEM)
    return pl.pallas_call(
        body, name=name, in_specs=[vmem] * 5, out_specs=[vmem] * 4,
        out_shape=[jax.ShapeDtypeStruct((8, SUBLANES, nstate), F32)] * 2
        + [jax.ShapeDtypeStruct(b_re.shape, BF16)] * 2)(lr, li, ld, b_re, b_im)


def _s5_prep_bwd(lr, li, ld, b_re, b_im, da_re, da_im, dbc_re, dbc_im, name):
    def body(lr_ref, li_ref, ld_ref, bre_ref, bim_ref, dar_ref, dai_ref, dbcre_ref, dbcim_ref,
             dlr_ref, dli_ref, dld_ref, dbre_ref, dbim_ref):
        lr, li = lr_ref[...], li_ref[...]
        dt, ar, ai, den, nr, fr, fi = _zoh(lr, li, ld_ref[...])
        bre, bim = bre_ref[...], bim_ref[...]
        gre, gim = dbcre_ref[...], dbcim_ref[...]
        dbre_ref[...] = fr * gre + fi * gim
        dbim_ref[...] = fr * gim - fi * gre
        g_fr = jnp.sum(gre * bre + gim * bim, axis=0, keepdims=True)
        g_fi = jnp.sum(gim * bre - gre * bim, axis=0, keepdims=True)
        g_ar = dar_ref[...] + (g_fr * lr - g_fi * li) / den
        g_ai = dai_ref[...] + (g_fr * li + g_fi * lr) / den
        d_lr = (g_fr * (nr - 2.0 * fr * lr) + g_fi * (ai - 2.0 * fi * lr)) / den
        d_li = (g_fr * (ai - 2.0 * fr * li) - g_fi * (nr + 2.0 * fi * li)) / den
        g_logmag = g_ar * ar + g_ai * ai
        g_ang = g_ai * ar - g_ar * ai
        dlr_ref[...] = d_lr + g_logmag * dt
        dli_ref[...] = d_li + g_ang * dt
        d_ld = (g_logmag * lr + g_ang * li) * dt
        n = d_ld.shape[1]
        sh = 1
        while sh < STATE:
            d_ld = d_ld + pltpu.roll(d_ld, n - sh, 1)
            sh *= 2
        dld_ref[...] = d_ld

    vmem = pl.BlockSpec(memory_space=pltpu.VMEM)
    row = jax.ShapeDtypeStruct(lr.shape, F32)
    return pl.pallas_call(
        body, name=name, in_specs=[vmem] * 9, out_specs=[vmem] * 5,
        out_shape=[row, row, row, jax.ShapeDtypeStruct(b_re.shape, F32), jax.ShapeDtypeStruct(b_re.shape, F32)],
    )(lr, li, ld, b_re, b_im, da_re, da_im, dbc_re, dbc_im)


def _compact_b(bb):
    bq = bb.reshape(N_GROUPS // 8, 8, STATE, GROUP)
    m = jnp.einsum("ab,qbph->qahbp", jnp.eye(8, dtype=bb.dtype), bq).reshape(N_GROUPS // 8, LANES, 8 * STATE)
    return m.transpose(1, 0, 2).reshape(LANES, N_GROUPS * STATE)


def _expand_b(m):
    d = m.reshape(8, GROUP, N_GROUPS // 8, 8, STATE)
    return jnp.einsum("ahqap->qahp", d).reshape(N_GROUPS, GROUP, STATE)


def _compact_c(c):
    cq = c.reshape(N_GROUPS // 8, 8, GROUP, STATE)
    return jnp.einsum("ab,qbhp->qbpah", jnp.eye(8, dtype=c.dtype), cq).reshape(N_GROUPS * STATE, LANES)


def _expand_c(m):
    d = m.reshape(N_GROUPS // 8, 8, STATE, 8, GROUP)
    return jnp.einsum("qbpbh->qbhp", d).reshape(N_GROUPS, GROUP, STATE)


def _local_step(x, target, p, ex):
    seq, d = x.shape
    n_real = N_META + seq
    tp = -(-n_real // ROW_ALIGN) * ROW_ALIGN

    nstate = N_GROUPS * STATE
    s5 = (p["ssm_lam_re"].reshape(1, nstate), p["ssm_lam_im"].reshape(1, nstate),
          jnp.repeat(p["ssm_log_dt"].reshape(-1), STATE).reshape(1, nstate),
          _compact_b(p["ssm_b_re"]), _compact_b(p["ssm_b_im"]))
    tab_f, tab_r, bc_re, bc_im = _s5_prep(*s5, "s5_prep")
    cc_re = _compact_c(p["ssm_c_re"]).astype(BF16)
    cc_im = _compact_c(p["ssm_c_im"]).astype(BF16)
    dskip = p["ssm_d"].reshape(1, -1)
    dh = dskip.shape[1]

    h0, hn1 = _input_norm_fwd(x, p["meta_tokens"], p["norm_mix_g"] + ex.zero, tp, "norm_mix")
    first = ex.weights("first", hn1)
    proj = _mm(hn1, first["w_in"], "nn", "proj")
    started = ex.forward("mid", proj)
    co, y, g = _seq_fwd(proj, p["conv_w"] + started[0, 0], bc_re, bc_im, cc_re, cc_im, dskip, tab_f, "seq_fwd")
    mid = ex.weights("mid", g)
    started = ex.forward("late", g)
    z = _mm(g, mid["ssm_w_glu"], "nn", "glu", after=started)
    mixed = _mix_fwd(co, y, z, p["gain_conv_out"], p["gain_ssm_out"], "mix_fwd")
    h1, hn2 = _proj_res_norm(mixed, mid["w_out"], h0, p["norm_ffn_g"], "out_proj_norm")
    late = ex.weights("late", hn2)
    up = _mm(hn2, late["w_up"], "nn", "up_proj")
    act = _ffn_act(up, p["ffn_conv_w"], p["ffn_conv_b"], "ffn_act")
    loss, dh2, dh2b, d_gfin = _proj_loss_bwd(act, late["w_down"], h1, target, p["norm_final_g"], n_real,
                                             "down_proj_loss")

    g_w_down = _mm(act, dh2b, "tn", "g_w_down")
    dact = _mm(dh2b, late["w_down"], "nt", "d_act")
    dup, dfw_a, dfw_v, dfb_a, dfb_v = _ffn_bwd(up, dact, p["ffn_conv_w"], p["ffn_conv_b"], "ffn_bwd")
    g_w_up = _mm(hn2, dup, "tn", "g_w_up")
    started = ex.grads_ready("late", {"w_up": g_w_up, "w_down": g_w_down})
    dh1, dh1b, d_gffn = _proj_norm_bwd(dup, late["w_up"], h1, p["norm_ffn_g"], dh2, started, "d_hn2_norm_bwd")
    started = ex.grads_send("late", dh1)
    g_w_out = _mm(mixed, dh1b, "tn", "g_w_out", after=started)
    dco, dz, dgp, d_gc, d_gs = _proj_mix_bwd(dh1b, mid["w_out"], co, y, z, p["gain_conv_out"],
                                             p["gain_ssm_out"], "d_mixed_mix_bwd")
    g_w_glu = _mm(g, dz, "tn", "g_w_glu")
    started = ex.grads_ready("mid", {"ssm_w_glu": g_w_glu, "w_out": g_w_out})
    dg = _mm(dz, mid["ssm_w_glu"], "nt", "d_gelu", acc_in=dgp, after=started)
    started = ex.grads_send("mid", dg)
    dproj, d_conv_w = _conv_bwd(proj, dco, p["conv_w"] + started[0, 0], "conv_bwd")
    (dproj, dbc_re, dbc_im, dcc_re, dcc_im, d_dskip, da_re, da_im) = _ssm_bwd(
        proj, y, dg, dproj, bc_re, bc_im, cc_re, cc_im, dskip, tab_f, tab_r, "ssm_bwd")
    g_w_in = _mm(hn1, dproj, "tn", "g_w_in")
    started = ex.grads_ready("first", {"w_in": g_w_in})
    dhn1 = _mm(dproj, first["w_in"], "nt", "d_hn1", after=started)
    started = ex.grads_send("first", dhn1)
    grad_x, d_meta, d_gmix = _input_norm_bwd(h0, p["norm_mix_g"] + started[0, 0], dhn1, dh1, n_real, "norm_mix_bwd")

    d_lam_re, d_lam_im, d_log_dt, d_b_re, d_b_im = _s5_prep_bwd(*s5, da_re, da_im, dbc_re, dbc_im, "s5_prep_bwd")
    d_lam_re, d_lam_im = d_lam_re.reshape(N_GROUPS, STATE), d_lam_im.reshape(N_GROUPS, STATE)
    d_log_dt = d_log_dt[0, ::STATE]
    d_b_re, d_b_im = _expand_b(d_b_re), _expand_b(d_b_im)
    grads = {
        "meta_tokens": d_meta, "norm_mix_g": d_gmix, "w_in": g_w_in, "conv_w": d_conv_w,
        "ssm_lam_re": d_lam_re, "ssm_lam_im": d_lam_im, "ssm_log_dt": d_log_dt,
        "ssm_b_re": d_b_re, "ssm_b_im": d_b_im, "ssm_c_re": _expand_c(dcc_re), "ssm_c_im": _expand_c(dcc_im),
        "ssm_d": d_dskip.reshape(N_GROUPS, GROUP), "ssm_w_glu": g_w_glu,
        "gain_conv_out": d_gc, "gain_ssm_out": d_gs, "w_out": g_w_out, "norm_ffn_g": d_gffn,
        "w_up": g_w_up, "ffn_conv_w": jnp.concatenate([dfw_a, dfw_v], axis=1),
        "ffn_conv_b": jnp.concatenate([dfb_a, dfb_v], axis=1), "w_down": g_w_down, "norm_final_g": d_gfin,
    }
    return loss[0, 0], grad_x, grads


def _view(ref, axis, start, size):
    idx = [slice(None)] * len(ref.shape)
    idx[axis] = pl.ds(start, size)
    return ref.at[tuple(idx)]


def _exchange(name, ins, outs, aliases, local_copies, remote_copies):
    ni, no = len(ins), len(outs)
    nl, nr = len(local_copies), len(remote_copies)

    def body(*refs):
        in_refs, out_refs = refs[:ni], refs[ni:ni + no]
        send_sems, recv_sems, local_sems = refs[ni + no:]
        x, y, c = lax.axis_index("x"), lax.axis_index("y"), lax.axis_index("c")
        pos = (x, y, c, 2 * x + y)
        locals_ = [pltpu.make_async_copy(s(in_refs, out_refs, pos), d(in_refs, out_refs, pos), local_sems.at[i])
                   for i, (s, d) in enumerate(local_copies)]
        remotes = []
        for i, (s, d, flip) in enumerate(remote_copies):
            peer = (1 - x if "x" in flip else x, 1 - y if "y" in flip else y, 1 - c if "c" in flip else c)
            remotes.append(pltpu.make_async_remote_copy(
                src_ref=s(in_refs, out_refs, pos), dst_ref=d(in_refs, out_refs, pos),
                send_sem=send_sems.at[i], recv_sem=recv_sems.at[i], device_id=peer, device_id_type=MESH))
        for cp in locals_ + remotes:
            cp.start()
        for cp in remotes:
            cp.wait_recv()
        for cp in remotes:
            cp.wait_send()
        for cp in locals_:
            cp.wait()

    hbm = pl.BlockSpec(memory_space=pl.ANY)
    return pl.pallas_call(
        body, name=name, in_specs=[hbm] * ni, out_specs=[hbm] * no, out_shape=outs,
        input_output_aliases=aliases,
        scratch_shapes=[pltpu.SemaphoreType.DMA((nr,)), pltpu.SemaphoreType.DMA((nr,)),
                        pltpu.SemaphoreType.DMA((max(nl, 1),))],
    )(*ins)


BIG = {"w_in": (0, 1), "ssm_w_glu": (1, 0), "w_out": (1, 0), "w_up": (0, 1), "w_down": (1, 0)}
BIG_NAMES = tuple(BIG)
FLIPS = ("y", "x", "xy")


def _peer_chip(pos, flip):
    x, y, _, _ = pos
    return 2 * (1 - x if "x" in flip else x) + (1 - y if "y" in flip else y)


def _block_rows(rows, cols, itemsize, mult):
    return _pick_tile(rows, max(mult, (2 * 1024 * 1024) // (cols * itemsize)), mult)


def _cast_into_full(w, kc, shard_axis, name):
    r, cdim = w.shape
    tr = _block_rows(r, cdim, 4, 16)
    nb = r // tr

    def body(kc_ref, w_ref, o_ref):
        o_ref[...] = w_ref[...].astype(BF16)

    if shard_axis == 1:
        full, o_spec = (r, 4 * cdim), pl.BlockSpec((tr, cdim), lambda i, kc: (i, kc[0]))
    else:
        full, o_spec = (4 * r, cdim), pl.BlockSpec((tr, cdim), lambda i, kc: (kc[0] * nb + i, 0))
    return pl.pallas_call(
        body, name=name,
        grid_spec=pltpu.PrefetchScalarGridSpec(
            num_scalar_prefetch=1, grid=(nb,), in_specs=[pl.BlockSpec((tr, cdim), lambda i, kc: (i, 0))],
            out_specs=o_spec),
        out_shape=jax.ShapeDtypeStruct(full, BF16), compiler_params=_cparams("parallel"))(kc, w)


def _pair_sum(g, recv, kc, half_axis, name, out_dtype):
    hr, hc = recv.shape
    tr = _block_rows(hr, hc, 4, 16)
    nb = hr // tr

    def body(kc_ref, g_ref, r_ref, o_ref):
        o_ref[...] = (g_ref[...] + r_ref[...]).astype(out_dtype)

    if half_axis == 0:
        g_spec = pl.BlockSpec((tr, hc), lambda i, kc: (kc[1] * nb + i, 0))
    elif half_axis == 1:
        g_spec = pl.BlockSpec((tr, hc), lambda i, kc: (i, kc[1]))
    else:
        g_spec = pl.BlockSpec((tr, hc), lambda i, kc: (i, 0))
    same = pl.BlockSpec((tr, hc), lambda i, kc: (i, 0))
    return pl.pallas_call(
        body, name=name,
        grid_spec=pltpu.PrefetchScalarGridSpec(num_scalar_prefetch=1, grid=(nb,), in_specs=[g_spec, same],
                                               out_specs=same),
        out_shape=jax.ShapeDtypeStruct((hr, hc), out_dtype), compiler_params=_cparams("parallel"))(kc, g, recv)


def _chip_sum(own, recv, kc, own_axis, out_axis, name):
    _, sr, sc = recv.shape
    tr = _block_rows(sr, sc, 4, 16)
    nb = sr // tr

    def body(kc_ref, o_ref, r_ref, t_ref):
        k = kc_ref[0]
        own_v = o_ref[...].astype(F32)
        r = [r_ref[m].astype(F32) for m in range(3)]
        terms = []
        for kk in range(4):
            m = jnp.bitwise_xor(k, kk)
            terms.append(jnp.where(m == 0, own_v, jnp.where(m == 1, r[0], jnp.where(m == 2, r[1], r[2]))))
        t_ref[...] = (terms[0] + terms[1]) + (terms[2] + terms[3])

    if own_axis == 0:
        own_spec = pl.BlockSpec((tr, sc), lambda i, kc: (kc[0] * nb + i, 0))
    elif own_axis == 1:
        own_spec = pl.BlockSpec((tr, sc), lambda i, kc: (i, kc[0]))
    else:
        own_spec = pl.BlockSpec((tr, sc), lambda i, kc: (kc[1] * nb + i, 0))
    if out_axis == 0:
        out_full, out_spec = (2 * sr, sc), pl.BlockSpec((tr, sc), lambda i, kc: (kc[1] * nb + i, 0))
    else:
        out_full, out_spec = (sr, 2 * sc), pl.BlockSpec((tr, sc), lambda i, kc: (i, kc[1]))
    return pl.pallas_call(
        body, name=name,
        grid_spec=pltpu.PrefetchScalarGridSpec(
            num_scalar_prefetch=1, grid=(nb,),
            in_specs=[own_spec, pl.BlockSpec((3, tr, sc), lambda i, kc: (0, i, 0))],
            out_specs=out_spec),
        out_shape=jax.ShapeDtypeStruct(out_full, F32), compiler_params=_cparams("parallel"))(kc, own, recv)


def _adamw(w, g, m, v, name):
    r, cdim = w.shape
    tr = _block_rows(r, cdim, 4, 8)
    c1 = 1.0 - ADAM_B1 ** ADAM_STEP
    c2 = 1.0 - ADAM_B2 ** ADAM_STEP

    def body(w_ref, g_ref, m_ref, v_ref, go_ref, d_ref, nm_ref, nv_ref):
        gv = g_ref[...]
        go_ref[...] = gv
        nm = ADAM_B1 * m_ref[...] + (1.0 - ADAM_B1) * gv
        nv = ADAM_B2 * v_ref[...] + (1.0 - ADAM_B2) * (gv * gv)
        d_ref[...] = -ADAM_LR * ((nm / c1) / (jnp.sqrt(nv / c2) + ADAM_EPS) + ADAM_WD * w_ref[...])
        nm_ref[...] = nm
        nv_ref[...] = nv

    spec = _rows(cdim, tr)
    return pl.pallas_call(body, name=name, grid=(r // tr,), in_specs=[spec] * 4, out_specs=[spec] * 4,
                          out_shape=[jax.ShapeDtypeStruct((r, cdim), F32)] * 4,
                          compiler_params=_cparams("parallel"))(w, g, m, v)


def _adamw_whole(ws, gs, ms, vs, name):
    n = len(ws)
    c1 = 1.0 - ADAM_B1 ** ADAM_STEP
    c2 = 1.0 - ADAM_B2 ** ADAM_STEP

    def body(*refs):
        for i in range(n):
            w_ref, g_ref, m_ref, v_ref, d_ref, nm_ref, nv_ref = [refs[j * n + i] for j in range(7)]
            gv = g_ref[...]
            nm = ADAM_B1 * m_ref[...] + (1.0 - ADAM_B1) * gv
            nv = ADAM_B2 * v_ref[...] + (1.0 - ADAM_B2) * (gv * gv)
            d_ref[...] = -ADAM_LR * ((nm / c1) / (jnp.sqrt(nv / c2) + ADAM_EPS) + ADAM_WD * w_ref[...])
            nm_ref[...] = nm
            nv_ref[...] = nv

    vmem = pl.BlockSpec(memory_space=pltpu.VMEM)
    out = pl.pallas_call(body, name=name, in_specs=[vmem] * (4 * n), out_specs=[vmem] * (3 * n),
                         out_shape=[jax.ShapeDtypeStruct(a.shape, F32) for a in ws] * 3,
                         compiler_params=pltpu.CompilerParams(vmem_limit_bytes=VMEM_LIMIT))(*ws, *gs, *ms, *vs)
    return out[:n], out[n:2 * n], out[2 * n:]


SIDE_EFFECT = pltpu.SideEffectType.DATAFLOW_SIDE_EFFECTING


def _descriptors(copies, refs, send_sems, recv_sems):
    x, y, c = lax.axis_index("x"), lax.axis_index("y"), lax.axis_index("c")
    pos = (x, y, c, 2 * x + y)
    out = []
    for i, (s, d, flip) in enumerate(copies):
        peer = (1 - x if "x" in flip else x, 1 - y if "y" in flip else y, 1 - c if "c" in flip else c)
        out.append(pltpu.make_async_remote_copy(
            src_ref=s(refs, refs, pos), dst_ref=d(refs, refs, pos),
            send_sem=send_sems.at[i], recv_sem=recv_sems.at[i], device_id=peer, device_id_type=MESH))
    return out


def _exchange_start(name, bufs, copies, after=None):
    n, nr = len(bufs), len(copies)
    na = 0 if after is None else 1

    def body(*refs):
        for cp in _descriptors(copies, refs[:n], refs[n + na], refs[n + na + 1]):
            cp.start()
        token = refs[2 * n + na + 2]
        token[...] = jnp.zeros_like(token)

    hbm = pl.BlockSpec(memory_space=pltpu.HBM)
    sem = pl.BlockSpec(memory_space=pltpu.SEMAPHORE)
    out = pl.pallas_call(
        body, name=name,
        in_specs=[hbm] * n + [pl.BlockSpec(memory_space=pl.ANY)] * na,
        out_specs=(sem, sem, *[hbm] * n, pl.BlockSpec(memory_space=pltpu.VMEM)),
        out_shape=(pltpu.SemaphoreType.DMA((nr,)), pltpu.SemaphoreType.DMA((nr,)),
                   *[pltpu.HBM(b.shape, b.dtype) for b in bufs], jax.ShapeDtypeStruct((SUBLANES, LANES), F32)),
        input_output_aliases={i: 2 + i for i in range(n)},
        compiler_params=pltpu.CompilerParams(has_side_effects=SIDE_EFFECT),
    )(*[pltpu.with_memory_space_constraint(b, pltpu.HBM) for b in bufs], *([after] * na))
    return out[0], out[1], list(out[2:2 + n]), out[2 + n]


def _exchange_wait(name, send_sems, recv_sems, bufs, copies, after):
    n = len(bufs)

    def body(*refs):
        for cp in _descriptors(copies, refs[:n], refs[n], refs[n + 1]):
            cp.wait_send()
            cp.wait_recv()

    hbm = pl.BlockSpec(memory_space=pltpu.HBM)
    sem = pl.BlockSpec(memory_space=pltpu.SEMAPHORE)
    out = pl.pallas_call(
        body, name=name,
        in_specs=[hbm] * n + [sem, sem, pl.BlockSpec(memory_space=pl.ANY)],
        out_specs=tuple([hbm] * n),
        out_shape=tuple(pltpu.HBM(b.shape, b.dtype) for b in bufs),
        input_output_aliases={i: i for i in range(n)},
        compiler_params=pltpu.CompilerParams(has_side_effects=SIDE_EFFECT),
    )(*bufs, send_sems, recv_sems, after)
    return list(out)


FIRST = ("w_in",)
MID = ("ssm_w_glu", "w_out")
LATE = ("w_up", "w_down")
GROUPS = {"first": FIRST, "mid": MID, "late": LATE}


def _gather_copies(names, shard_shapes):
    def region(i, chip, c):
        half_axis, shard_axis = BIG[names[i]]
        ssize = shard_shapes[i][shard_axis]
        hsize = shard_shapes[i][half_axis] // 2
        return lambda ref: _view(_view(ref, shard_axis, chip * ssize, ssize), half_axis, c * hsize, hsize)

    ici, d2d = [], []
    for i in range(len(names)):
        for flip in FLIPS:
            ici.append((lambda I, O, pos, i=i: region(i, pos[3], pos[2])(I[i]),
                        lambda I, O, pos, i=i: region(i, pos[3], pos[2])(O[i]), flip))
            d2d.append((lambda I, O, pos, i=i, flip=flip: region(i, _peer_chip(pos, flip), pos[2])(I[i]),
                        lambda I, O, pos, i=i, flip=flip: region(i, _peer_chip(pos, flip), pos[2])(O[i]), "c"))
    return ici, d2d


def _half_shape(n, shape):
    r, cdim = shape
    return (r // 2, cdim) if BIG[n][0] == 0 else (r, cdim // 2)


def _sub_shape(n, shape):
    hr, hc = _half_shape(n, shape)
    return (hr, hc // 4) if BIG[n][1] == 1 else (hr // 4, hc)


def _pair_copies(names, shapes, with_pack, dst_off):
    n = len(names)

    def other_half(i, ref, pos):
        half_axis = BIG[names[i]][0]
        hsize = shapes[i][half_axis] // 2
        return _view(ref, half_axis, (1 - pos[2]) * hsize, hsize)

    copies = [(lambda I, O, pos, i=i: other_half(i, I[i], pos), lambda I, O, pos, i=i: O[dst_off + i], "c")
              for i in range(n)]
    if with_pack:
        copies.append((lambda I, O, pos: I[n], lambda I, O, pos: O[dst_off + n], "c"))
    return copies


def _chip_copies(names, shapes, pack_rows, dst_off):
    n = len(names)

    def piece(i, ref, chip):
        shard_axis = BIG[names[i]][1]
        ssize = _sub_shape(names[i], shapes[i])[shard_axis]
        return _view(ref, shard_axis, chip * ssize, ssize)

    copies = []
    for i in range(n):
        for slot, flip in enumerate(FLIPS):
            copies.append((lambda I, O, pos, i=i, flip=flip: piece(i, I[i], _peer_chip(pos, flip)),
                           lambda I, O, pos, i=i, slot=slot: O[dst_off + i].at[slot], flip))
    if pack_rows:
        for slot, flip in enumerate(FLIPS):
            copies.append((lambda I, O, pos: _view(I[n], 0, pos[2] * (pack_rows // 2), pack_rows // 2),
                           lambda I, O, pos, slot=slot: O[dst_off + n].at[slot], flip))
    return copies


class _Exchanges:
    def __init__(self, shards, tiny, kc):
        self.kc = kc
        wb = {n: _cast_into_full(shards[n], kc, BIG[n][1], "cast_" + n) for n in BIG_NAMES}
        own = (lambda I, O, pos: I[0], lambda I, O, pos: O[0].at[pos[3]])
        self.tiny_all = _exchange("gather_tiny", [tiny], [jax.ShapeDtypeStruct((4,) + tiny.shape, F32)], {},
                                  [own], [own + (flip,) for flip in FLIPS])[0]
        self.gathering, self.forwarding, self.pairing, self.reducing = {}, {}, {}, {}
        after = self.tiny_all
        self.zero = 0.0
        for group, names in GROUPS.items():
            copies = _gather_copies(names, [shards[n].shape for n in names])
            started = _exchange_start("gather_%s_start" % group, [wb[n] for n in names], copies[0], after)
            self.gathering[group] = (started, copies)
            after = started[2][0]
            self.zero = self.zero + started[3][0, 0]

    def forward(self, group, after):
        (send_sems, recv_sems, bufs, _), (ici, d2d) = self.gathering[group]
        got = _exchange_wait("gather_%s_wait" % group, send_sems, recv_sems, bufs, ici, after)
        self.forwarding[group] = (_exchange_start("forward_%s_start" % group, got, d2d), d2d)
        return self.forwarding[group][0][3]

    def weights(self, group, after):
        if group not in self.forwarding:
            after = self.forward(group, after)
        (send_sems, recv_sems, bufs, _), d2d = self.forwarding[group]
        full = _exchange_wait("forward_%s_wait" % group, send_sems, recv_sems, bufs, d2d, after)
        return dict(zip(GROUPS[group], full))

    def grads_ready(self, group, grads):
        names = GROUPS[group]
        gs = [grads[n] for n in names]
        land = [lax.empty(_half_shape(n, g.shape), F32) for n, g in zip(names, gs)]
        copies = _pair_copies(names, [g.shape for g in gs], False, len(names))
        started = _exchange_start("pair_%s_start" % group, gs + land, copies)
        self.pairing[group] = (started, copies)
        return started[3]

    def grads_send(self, group, after):
        names = GROUPS[group]
        n = len(names)
        (send_sems, recv_sems, bufs, _), copies = self.pairing[group]
        bufs = _exchange_wait("pair_%s_wait" % group, send_sems, recv_sems, bufs, copies, after)
        chip = [_pair_sum(bufs[i], bufs[n + i], self.kc, BIG[names[i]][0], "pair_sum_" + names[i], BF16)
                for i in range(n)]
        shapes = [bufs[i].shape for i in range(n)]
        land = [lax.empty((3,) + _sub_shape(names[i], shapes[i]), BF16) for i in range(n)]
        copies = _chip_copies(names, shapes, 0, n)
        started = _exchange_start("reduce_%s_start" % group, chip + land, copies)
        self.reducing[group] = (started, copies)
        return started[3]

    def finish_pack(self, pack):
        kc = self.kc
        prow = pack.shape[0] // 2
        recv = _exchange("reduce_d2d", [pack], [jax.ShapeDtypeStruct(pack.shape, F32)], {}, [],
                         _pair_copies((), [], True, 0))
        chip_pack = _pair_sum(pack, recv[0], kc, None, "pair_sum_pack", F32)
        copies = _chip_copies((), [], pack.shape[0], 1)
        land = lax.empty((3, prow, pack.shape[1]), F32)
        pack_sems_s, pack_sems_r, pack_bufs, after = _exchange_start("reduce_pack_start", [chip_pack, land], copies)

        names, chips, recvs = (), [], []
        for group, group_names in GROUPS.items():
            (send_sems, recv_sems, bufs, _), group_copies = self.reducing[group]
            bufs = _exchange_wait("reduce_%s_wait" % group, send_sems, recv_sems, bufs, group_copies, after)
            n = len(group_names)
            names, chips, recvs = names + group_names, chips + bufs[:n], recvs + bufs[n:]
            after = bufs[n]
        total = [_chip_sum(chips[i], recvs[i], kc, BIG[n][1], BIG[n][0], "chip_sum_" + n)
                 for i, n in enumerate(names)]

        def my_half(half_axis, ref, pos):
            hsize = ref.shape[half_axis] // 2
            return _view(ref, half_axis, pos[2] * hsize, hsize)

        swap = [(lambda I, O, pos, i=i, n=n: my_half(BIG[n][0], I[i], pos),
                 lambda I, O, pos, i=i, n=n: my_half(BIG[n][0], O[i], pos), "c") for i, n in enumerate(names)]
        self.swapping = (_exchange_start("swap_start", total, swap), swap, names)

        chip_pack, recv_pack = _exchange_wait("reduce_pack_wait", pack_sems_s, pack_sems_r, pack_bufs, copies,
                                              self.swapping[0][3])
        total_pack = _chip_sum(chip_pack, recv_pack, kc, None, 0, "chip_sum_pack")
        swap = [(lambda I, O, pos: my_half(0, I[0], pos), lambda I, O, pos: my_half(0, O[0], pos), "c")]
        return _exchange("swap_pack", [total_pack], [jax.ShapeDtypeStruct(pack.shape, F32)], {0: 0}, [], swap)[0]

    def finish_big(self, after):
        (send_sems, recv_sems, bufs, _), swap, names = self.swapping
        return dict(zip(names, _exchange_wait("swap_wait", send_sems, recv_sems, bufs, swap, after)))


WEIGHTS = ("meta_tokens", "norm_mix_g", "w_in", "conv_w", "ssm_lam_re", "ssm_lam_im", "ssm_log_dt", "ssm_b_re",
           "ssm_b_im", "ssm_c_re", "ssm_c_im", "ssm_d", "ssm_w_glu", "gain_conv_out", "gain_ssm_out", "w_out",
           "norm_ffn_g", "w_up", "ffn_conv_w", "ffn_conv_b", "w_down", "norm_final_g")
TINY_SHARDED = ("meta_tokens", "conv_w", "ffn_conv_w")
REPLICATED = tuple(n for n in WEIGHTS if n not in BIG and n not in TINY_SHARDED)
PACK_COLS = 512


def _pack(arrays, row_mult, cols):
    flat = jnp.concatenate([a.reshape(-1).astype(F32) for a in arrays])
    n = flat.shape[0]
    total = -(-n // (row_mult * cols)) * (row_mult * cols)
    return jnp.concatenate([flat, jnp.zeros((total - n,), F32)]).reshape(total // cols, cols)


def _unpack(packed, shapes):
    flat = packed.reshape(-1)
    out, off = [], 0
    for s in shapes:
        n = math.prod(s)
        out.append(flat[off:off + n].reshape(s))
        off += n
    return out


def kernel(x, meta_tokens, norm_mix_g, w_in, conv_w, ssm_lam_re, ssm_lam_im, ssm_log_dt, ssm_b_re, ssm_b_im, ssm_c_re, ssm_c_im, ssm_d, ssm_w_glu, gain_conv_out, gain_ssm_out, w_out, norm_ffn_g, w_up, ffn_conv_w, ffn_conv_b, w_down, norm_final_g, loss_target, m_meta_tokens, m_norm_mix_g, m_w_in, m_conv_w, m_ssm_lam_re, m_ssm_lam_im, m_ssm_log_dt, m_ssm_b_re, m_ssm_b_im, m_ssm_c_re, m_ssm_c_im, m_ssm_d, m_ssm_w_glu, m_gain_conv_out, m_gain_ssm_out, m_w_out, m_norm_ffn_g, m_w_up, m_ffn_conv_w, m_ffn_conv_b, m_w_down, m_norm_final_g, v_meta_tokens, v_norm_mix_g, v_w_in, v_conv_w, v_ssm_lam_re, v_ssm_lam_im, v_ssm_log_dt, v_ssm_b_re, v_ssm_b_im, v_ssm_c_re, v_ssm_c_im, v_ssm_d, v_ssm_w_glu, v_gain_conv_out, v_gain_ssm_out, v_w_out, v_norm_ffn_g, v_w_up, v_ffn_conv_w, v_ffn_conv_b, v_w_down, v_norm_final_g):
    args = dict(locals())
    w = {n: args[n] for n in WEIGHTS}
    mom = {n: args["m_" + n] for n in WEIGHTS}
    var = {n: args["v_" + n] for n in WEIGHTS}
    kx, ky, kc_ = lax.axis_index("x"), lax.axis_index("y"), lax.axis_index("c")
    chip = 2 * kx + ky
    kc = jnp.stack([chip, kc_]).astype(jnp.int32)

    def squeeze(n, a):
        if n == "meta_tokens":
            return a
        if n == "norm_final_g":
            return a.reshape(1, -1)
        a = a[0]
        return a.reshape(1, -1) if a.ndim == 1 else a

    wl = {n: squeeze(n, w[n]) for n in WEIGHTS}
    ml = {n: squeeze(n, mom[n]) for n in WEIGHTS}
    vl = {n: squeeze(n, var[n]) for n in WEIGHTS}

    tiny = _pack([wl[n] for n in TINY_SHARDED], SUBLANES, LANES)
    ex = _Exchanges({n: wl[n] for n in BIG_NAMES}, tiny, kc)
    tiny_shapes = [wl[n].shape for n in TINY_SHARDED]
    tiny_parts = [_unpack(ex.tiny_all[k], tiny_shapes) for k in range(4)]
    p = {n: wl[n] for n in WEIGHTS if n not in BIG}
    for j, n in enumerate(TINY_SHARDED):
        p[n] = jnp.concatenate([tiny_parts[k][j] for k in range(4)], axis=1)
    p["ssm_log_dt"] = wl["ssm_log_dt"].reshape(-1)

    loss_local, grad_x, grads = _local_step(x[0], loss_target[0], p, ex)

    small_names = REPLICATED + TINY_SHARDED
    small_shapes = [tuple(grads[n].shape) for n in small_names] + [(1,)]
    pack = _pack([grads[n] for n in small_names] + [loss_local.reshape(1)], 2 * 16, PACK_COLS)
    g_pack = ex.finish_pack(pack)
    g_small = dict(zip(small_names + ("loss",), _unpack(g_pack, small_shapes)))
    loss = g_small["loss"][0]
    swapped = ("ssm_b_re", "ssm_b_im")

    def view(n, a):
        if n in swapped:
            return jnp.swapaxes(a, -1, -2)
        return a.reshape(1, -1) if a.ndim == 1 else a

    g = {}
    for n in REPLICATED:
        g[n] = g_small[n].reshape(view(n, w[n]).shape)
    for n in TINY_SHARDED:
        cols = wl[n].shape[1]
        g[n] = lax.dynamic_slice_in_dim(g_small[n], chip * cols, cols, axis=1).reshape(w[n].shape)
    delta, new_m, new_v = {}, {}, {}
    small = [[view(n, d[n]) for n in small_names] for d in (w, mom, var)]
    small.insert(1, [g[n] for n in small_names])
    for d, outs in zip((delta, new_m, new_v), _adamw_whole(*small, "adamw_small")):
        d.update(zip(small_names, outs))
    for d in (g, delta, new_m, new_v):
        d.update({n: jnp.swapaxes(d[n], -1, -2) for n in swapped})
    g_big = ex.finish_big(delta[small_names[0]])
    for n in BIG_NAMES:
        g[n], delta[n], new_m[n], new_v[n] = _adamw(wl[n], g_big[n], ml[n], vl[n], "adamw_" + n)

    def like(n, a):
        return a.reshape(w[n].shape)

    return (loss, grad_x[None], *[like(n, g[n]) for n in WEIGHTS], *[like(n, delta[n]) for n in WEIGHTS],
            *[like(n, new_m[n]) for n in WEIGHTS], *[like(n, new_v[n]) for n in WEIGHTS])
```

```python
import functools
import math

import jax
import jax.numpy as jnp
from jax import lax
from jax.experimental import pallas as pl
from jax.experimental.pallas import tpu as pltpu

F32 = jnp.float32
BF16 = jnp.bfloat16
MESH = pl.DeviceIdType.MESH

N_META = 16
N_GROUPS = 32
GROUP = 16
STATE = 64
RMS_EPS = 1e-6
ADAM_LR = 0.001
ADAM_B1 = 0.9
ADAM_B2 = 0.999
ADAM_EPS = 1e-08
ADAM_WD = 0.01
ADAM_STEP = 10

LANES = 128
SUBLANES = 8
ROW_ALIGN = 128
ROW_TILES = 4
VMEM_LIMIT = 52 * 1024 * 1024
MM_VMEM_BUDGET = 40 * 1024 * 1024
GELU_C = math.sqrt(2.0 / math.pi)
GELU_A = 0.044715


def _cparams(*sem):
    return pltpu.CompilerParams(dimension_semantics=sem, vmem_limit_bytes=VMEM_LIMIT)


def _pick_tile(dim, cap, mult):
    best = None
    for t in range(mult, min(dim, cap) + 1, mult):
        if dim % t == 0:
            best = t
    return best if best is not None else dim


def _mm(a, b, mode, name, out_dtype=F32, acc_in=None, after=None):
    if mode == "tn":
        kdim, m = a.shape
    else:
        m, kdim = a.shape
    n = b.shape[0] if mode == "nt" else b.shape[1]
    tm = _pick_tile(m, 1408, LANES if mode == "tn" else 16)
    tk = _pick_tile(kdim, 2816, LANES)
    nk = kdim // tk
    out_bytes = jnp.dtype(out_dtype).itemsize
    for cap in (1408, 1024, 512, 256, LANES):
        tn = _pick_tile(n, cap, LANES)
        blocks = 2 * (tm * tk * 2 + tk * tn * 2 + tm * tn * out_bytes * (2 if acc_in is not None else 1))
        if blocks + (tm * tn * 4 if nk > 1 else 0) <= MM_VMEM_BUDGET:
            break
    has_acc = acc_in is not None

    def body(*refs):
        if after is not None:
            refs = refs[1:]
        if has_acc:
            a_ref, b_ref, c_ref, o_ref = refs[:4]
            rest = refs[4:]
        else:
            a_ref, b_ref, o_ref = refs[:3]
            c_ref = None
            rest = refs[3:]
        if mode == "nn":
            p = jnp.dot(a_ref[...], b_ref[...], preferred_element_type=F32)
        elif mode == "nt":
            p = lax.dot_general(a_ref[...], b_ref[...], (((1,), (1,)), ((), ())), preferred_element_type=F32)
        else:
            p = lax.dot_general(a_ref[...], b_ref[...], (((0,), (0,)), ((), ())), preferred_element_type=F32)
        if nk == 1:
            if has_acc:
                p = p + c_ref[...]
            o_ref[...] = p.astype(out_dtype)
        else:
            acc_ref = rest[0]
            k = pl.program_id(2)

            @pl.when(k == 0)
            def _():
                acc_ref[...] = p + c_ref[...] if has_acc else p

            @pl.when(k > 0)
            def _():
                acc_ref[...] += p

            @pl.when(k == nk - 1)
            def _():
                o_ref[...] = acc_ref[...].astype(out_dtype)

    if mode == "tn":
        a_spec = pl.BlockSpec((tk, tm), lambda i, j, k: (k, i))
    else:
        a_spec = pl.BlockSpec((tm, tk), lambda i, j, k: (i, k))
    if mode == "nt":
        b_spec = pl.BlockSpec((tn, tk), lambda i, j, k: (j, k))
    else:
        b_spec = pl.BlockSpec((tk, tn), lambda i, j, k: (k, j))
    o_spec = pl.BlockSpec((tm, tn), lambda i, j, k: (i, j))
    in_specs = [a_spec, b_spec] + ([o_spec] if has_acc else [])
    args = (a, b) + ((acc_in,) if has_acc else ())
    if after is not None:
        in_specs = [pl.BlockSpec(memory_space=pl.ANY)] + in_specs
        args = (after,) + args
    return pl.pallas_call(
        body, name=name, grid=(m // tm, n // tn, nk),
        in_specs=in_specs, out_specs=o_spec,
        out_shape=jax.ShapeDtypeStruct((m, n), out_dtype),
        scratch_shapes=[pltpu.VMEM((tm, tn), F32)] if nk > 1 else [],
        compiler_params=_cparams("parallel", "parallel", "arbitrary"),
    )(*args)


def _mm_rows(a, b, mode, name, ins, outs, epilogue, scratch=()):
    m, kdim = a.shape
    n = b.shape[0] if mode == "nt" else b.shape[1]
    tm = m // ROW_TILES
    tk = _pick_tile(kdim, 2816, LANES)
    nk = kdim // tk
    ni, no = len(ins), len(outs)

    def body(*refs):
        a_ref, b_ref = refs[:2]
        in_refs, out_refs, rest = refs[2:2 + ni], refs[2 + ni:2 + ni + no], refs[2 + ni + no:]
        i = pl.program_id(0)
        if mode == "nn":
            p = jnp.dot(a_ref[...], b_ref[...], preferred_element_type=F32)
        else:
            p = lax.dot_general(a_ref[...], b_ref[...], (((1,), (1,)), ((), ())), preferred_element_type=F32)
        if nk == 1:
            epilogue(p, i, in_refs, out_refs, rest)
        else:
            acc_ref = rest[0]
            k = pl.program_id(1)

            @pl.when(k == 0)
            def _():
                acc_ref[...] = p

            @pl.when(k > 0)
            def _():
                acc_ref[...] += p

            @pl.when(k == nk - 1)
            def _():
                epilogue(acc_ref[...], i, in_refs, out_refs, rest[1:])

    def spec(shape, kind):
        if kind == "rows":
            return pl.BlockSpec((tm,) + tuple(shape[1:]), lambda i, k: (i,) + (0,) * (len(shape) - 1))
        if kind == "whole":
            return pl.BlockSpec(tuple(shape), lambda i, k: (0,) * len(shape))
        return pl.BlockSpec(memory_space=pl.ANY)

    a_spec = pl.BlockSpec((tm, tk), lambda i, k: (i, k))
    b_spec = pl.BlockSpec((n, tk), lambda i, k: (0, k)) if mode == "nt" else pl.BlockSpec((tk, n), lambda i, k: (k, 0))
    return pl.pallas_call(
        body, name=name, grid=(ROW_TILES, nk),
        in_specs=[a_spec, b_spec] + [spec(x.shape, kind) for x, kind in ins],
        out_specs=[spec(shape, kind) for shape, _, kind in outs],
        out_shape=[jax.ShapeDtypeStruct(shape, dtype) for shape, dtype, _ in outs],
        scratch_shapes=([pltpu.VMEM((tm, n), F32)] if nk > 1 else []) + list(scratch),
        compiler_params=_cparams("arbitrary", "arbitrary"),
    )(a, b, *[x for x, _ in ins])


def _rows(shape_cols, tr, dtype=None):
    return pl.BlockSpec((tr, shape_cols), lambda i: (i, 0))


def _const(shape):
    return pl.BlockSpec(shape, lambda i: (0,) * len(shape))


def _rms(x):
    return lax.rsqrt(jnp.mean(x * x, axis=-1, keepdims=True) + RMS_EPS)


def _rms_bwd(x, r, g, dy):
    xn = x * r
    dxn = dy * g
    dx = r * (dxn - xn * jnp.mean(dxn * xn, axis=-1, keepdims=True))
    return dx, dy * xn


def _gelu(y):
    return 0.5 * y * (1.0 + jnp.tanh(GELU_C * (y + GELU_A * y * y * y)))


def _gelu_grad(y):
    t = jnp.tanh(GELU_C * (y + GELU_A * y * y * y))
    return 0.5 * (1.0 + t) + 0.5 * y * (1.0 - t * t) * GELU_C * (1.0 + 3.0 * GELU_A * y * y)


def _sigmoid(z):
    return 1.0 / (1.0 + jnp.exp(-z))


def _proj_res_norm(a, w, h, g, name):
    def epilogue(p, i, ins, outs, _):
        x = ins[0][...] + p
        outs[0][...] = x
        outs[1][...] = (x * _rms(x) * ins[1][...]).astype(BF16)

    return _mm_rows(a, w, "nn", name, [(h, "rows"), (g, "whole")],
                    [(h.shape, F32, "rows"), (h.shape, BF16, "rows")], epilogue)


def _proj_norm_bwd(da, w, h, g, dres, after, name):
    d = h.shape[1]

    def epilogue(p, i, ins, outs, _):
        x = ins[0][...]
        dx, dgs = _rms_bwd(x, _rms(x), ins[1][...], p)
        dh = ins[2][...] + dx
        outs[0][...] = dh
        outs[1][...] = dh.astype(BF16)

        @pl.when(i == 0)
        def _():
            outs[2][...] = jnp.zeros_like(outs[2])

        outs[2][...] += jnp.sum(dgs, axis=0, keepdims=True)

    return _mm_rows(da, w, "nt", name, [(h, "rows"), (g, "whole"), (dres, "rows"), (after, "hbm")],
                    [(h.shape, F32, "rows"), (h.shape, BF16, "rows"), ((1, d), F32, "whole")], epilogue)


def _input_norm_bwd(h, g, dhn, dres, n_real, name):
    tp, d = h.shape
    tr = tp // ROW_TILES

    def body(h_ref, g_ref, dhn_ref, dres_ref, dx_ref, dmeta_ref, dg_ref, stage, sem):
        i = pl.program_id(0)
        x = h_ref[...]
        dx, dgs = _rms_bwd(x, _rms(x), g_ref[...], dhn_ref[...])
        stage[...] = dres_ref[...] + dx

        @pl.when(i == 0)
        def _():
            dg_ref[...] = jnp.zeros_like(dg_ref)
            dmeta_ref[...] = stage[:N_META, :]

        dg_ref[...] += jnp.sum(dgs, axis=0, keepdims=True)
        for t in range(ROW_TILES):
            lo, hi = max(t * tr, N_META), min((t + 1) * tr, n_real)
            if hi > lo:
                @pl.when(i == t)
                def _(t=t, lo=lo, hi=hi):
                    cp = pltpu.make_async_copy(stage.at[pl.ds(lo - t * tr, hi - lo), :],
                                               dx_ref.at[pl.ds(lo - N_META, hi - lo), :], sem)
                    cp.start()
                    cp.wait()

    return pl.pallas_call(
        body, name=name, grid=(ROW_TILES,),
        in_specs=[_rows(d, tr), _const((1, d)), _rows(d, tr), _rows(d, tr)],
        out_specs=[pl.BlockSpec(memory_space=pl.ANY), _const((N_META, d)), _const((1, d))],
        out_shape=[jax.ShapeDtypeStruct((n_real - N_META, d), F32), jax.ShapeDtypeStruct((N_META, d), F32),
                   jax.ShapeDtypeStruct((1, d), F32)],
        scratch_shapes=[pltpu.VMEM((tr, d), F32), pltpu.SemaphoreType.DMA],
        compiler_params=_cparams("arbitrary"))(h, g, dhn, dres)


def _load_token_rows(tok_hbm, buf, sem, tr, n_real, head=None, wait=False, i=None):
    i = pl.program_id(0) if i is None else i
    for t in range(ROW_TILES):
        base = t * tr
        lo, hi = max(base, N_META), min(base + tr, n_real)

        @pl.when(i == t)
        def _(base=base, lo=lo, hi=hi):
            if hi > lo:
                cp = pltpu.make_async_copy(tok_hbm.at[pl.ds(lo - N_META, hi - lo), :],
                                           buf.at[pl.ds(lo - base, hi - lo), :], sem)
                if wait:
                    cp.wait()
                    return
                cp.start()
            if wait:
                return
            if base < N_META:
                buf[0:N_META - base, :] = (jnp.zeros((N_META - base, buf.shape[1]), F32) if head is None
                                           else head[base:N_META, :])
            if hi < base + tr:
                buf[max(hi, base) - base:tr, :] = jnp.zeros((base + tr - max(hi, base), buf.shape[1]), F32)


def _input_norm_fwd(x, meta, g, tp, name):
    seq, d = x.shape
    tr = tp // ROW_TILES
    n_real = N_META + seq

    def body(x_hbm, meta_ref, g_ref, h_ref, hn_ref, buf, sem):
        _load_token_rows(x_hbm, buf, sem, tr, n_real, head=meta_ref)
        _load_token_rows(x_hbm, buf, sem, tr, n_real, wait=True)
        h = buf[...]
        h_ref[...] = h
        hn_ref[...] = (h * _rms(h) * g_ref[...]).astype(BF16)

    return pl.pallas_call(
        body, name=name, grid=(ROW_TILES,),
        in_specs=[pl.BlockSpec(memory_space=pl.ANY), _const((N_META, d)), _const((1, d))],
        out_specs=[_rows(d, tr), _rows(d, tr)],
        out_shape=[jax.ShapeDtypeStruct((tp, d), F32), jax.ShapeDtypeStruct((tp, d), BF16)],
        scratch_shapes=[pltpu.VMEM((tr, d), F32), pltpu.SemaphoreType.DMA],
        compiler_params=_cparams("arbitrary"))(x, meta, g)


def _proj_loss_bwd(act, w, h1, target, g, n_real, name):
    tp, d = h1.shape
    tr = tp // ROW_TILES

    def epilogue(p, i, ins, outs, scratch):
        h1_ref, t_hbm, g_ref = ins
        loss_ref, dh_ref, dhb_ref, dg_ref = outs
        t_buf, sem = scratch
        _load_token_rows(t_hbm, t_buf, sem, tr, n_real, i=i)
        x = h1_ref[...] + p
        r = _rms(x)
        row = i * tr + lax.broadcasted_iota(jnp.int32, (tr, d), 0)
        valid = (row >= N_META) & (row < n_real)
        _load_token_rows(t_hbm, t_buf, sem, tr, n_real, wait=True, i=i)
        e = jnp.where(valid, x * r * g_ref[...] - t_buf[...], 0.0)
        dx, dgs = _rms_bwd(x, r, g_ref[...], e * (1.0 / d))
        dh_ref[...] = dx
        dhb_ref[...] = dx.astype(BF16)

        @pl.when(i == 0)
        def _():
            dg_ref[...] = jnp.zeros_like(dg_ref)
            loss_ref[...] = jnp.zeros_like(loss_ref)

        dg_ref[...] += jnp.sum(dgs, axis=0, keepdims=True)
        loss_ref[...] += (0.5 / d) * jnp.sum(jnp.sum(e * e, axis=0, keepdims=True), axis=1, keepdims=True)

    return _mm_rows(act, w, "nn", name, [(h1, "rows"), (target, "hbm"), (g, "whole")],
                    [((1, LANES), F32, "whole"), ((tp, d), F32, "rows"), ((tp, d), BF16, "rows"),
                     ((1, d), F32, "whole")],
                    epilogue, scratch=[pltpu.VMEM((tr, d), F32), pltpu.SemaphoreType.DMA])


def _mix_fwd(co, y, z, gc, gs, name):
    tp, dh = co.shape
    tr = tp // ROW_TILES

    def body(co_ref, y_ref, z_ref, gc_ref, gs_ref, m_ref):
        c = co_ref[...]
        m_ref[:, :dh] = (c * _rms(c) * gc_ref[...]).astype(BF16)
        so = _gelu(y_ref[...]) * _sigmoid(z_ref[...])
        m_ref[:, dh:] = (so * _rms(so) * gs_ref[...]).astype(BF16)

    return pl.pallas_call(
        body, name=name, grid=(ROW_TILES,),
        in_specs=[_rows(dh, tr)] * 3 + [_const((1, dh))] * 2,
        out_specs=_rows(2 * dh, tr),
        out_shape=jax.ShapeDtypeStruct((tp, 2 * dh), BF16),
        compiler_params=_cparams("parallel"))(co, y, z, gc, gs)


def _proj_mix_bwd(dh1b, w, co, y, z, gc, gs, name):
    tp, dh = co.shape

    def epilogue(p, i, ins, outs, _):
        co_ref, y_ref, z_ref, gc_ref, gs_ref = ins
        dco_ref, dz_ref, dgp_ref, dgc_ref, dgs_ref = outs
        c = co_ref[...]
        dco, dgc = _rms_bwd(c, _rms(c), gc_ref[...], p[:, :dh])
        dco_ref[...] = dco
        gl = _gelu(y_ref[...])
        sg = _sigmoid(z_ref[...])
        so = gl * sg
        dso, dgs = _rms_bwd(so, _rms(so), gs_ref[...], p[:, dh:])
        dz_ref[...] = (dso * gl * sg * (1.0 - sg)).astype(BF16)
        dgp_ref[...] = dso * sg

        @pl.when(i == 0)
        def _():
            dgc_ref[...] = jnp.zeros_like(dgc_ref)
            dgs_ref[...] = jnp.zeros_like(dgs_ref)

        dgc_ref[...] += jnp.sum(dgc, axis=0, keepdims=True)
        dgs_ref[...] += jnp.sum(dgs, axis=0, keepdims=True)

    return _mm_rows(dh1b, w, "nt", name,
                    [(co, "rows"), (y, "rows"), (z, "rows"), (gc, "whole"), (gs, "whole")],
                    [((tp, dh), F32, "rows"), ((tp, dh), BF16, "rows"), ((tp, dh), F32, "rows"),
                     ((1, dh), F32, "whole"), ((1, dh), F32, "whole")], epilogue)


def _shift_down(x, k):
    row = lax.broadcasted_iota(jnp.int32, x.shape, 0)
    return jnp.where(row >= k, pltpu.roll(x, k, 0), 0.0)


def _shift_up(x, k):
    n = x.shape[0]
    row = lax.broadcasted_iota(jnp.int32, x.shape, 0)
    return jnp.where(row < n - k, pltpu.roll(x, n - k, 0), 0.0)


def _dwconv(x, w_ref):
    return w_ref[2:3, :] * x + w_ref[1:2, :] * _shift_down(x, 1) + w_ref[0:1, :] * _shift_down(x, 2)


def _dwconv_bwd(x, dy, w_ref):
    dx = w_ref[2:3, :] * dy + w_ref[1:2, :] * _shift_up(dy, 1) + w_ref[0:1, :] * _shift_up(dy, 2)
    dw = jnp.concatenate([jnp.sum(dy * _shift_down(x, 2), axis=0, keepdims=True),
                          jnp.sum(dy * _shift_down(x, 1), axis=0, keepdims=True),
                          jnp.sum(dy * x, axis=0, keepdims=True)], axis=0)
    return dx, dw


def _scan(s_re, s_im, tab_ref, reverse):
    n_chunks = s_re.shape[0] // SUBLANES
    n_strips = s_re.shape[1] // LANES
    last = 0 if reverse else SUBLANES - 1

    def body(i, carry):
        chunk = (n_chunks - 1 - i) if reverse else i
        r0 = pl.multiple_of(chunk * SUBLANES, SUBLANES)
        out = []
        for st in range(n_strips):
            lanes = slice(st * LANES, (st + 1) * LANES)
            cr, ci = carry[2 * st], carry[2 * st + 1]
            xr = s_re[pl.ds(r0, SUBLANES), lanes]
            xi = s_im[pl.ds(r0, SUBLANES), lanes]
            for level, k in enumerate((1, 2, 4)):
                mr = tab_ref[2 * level, :, lanes]
                mi = tab_ref[2 * level + 1, :, lanes]
                sh = SUBLANES - k if reverse else k
                rr = pltpu.roll(xr, sh, 0)
                ri = pltpu.roll(xi, sh, 0)
                xr, xi = xr + (mr * rr - mi * ri), xi + (mr * ri + mi * rr)
            pwr = tab_ref[6, :, lanes]
            pwi = tab_ref[7, :, lanes]
            xr, xi = xr + (pwr * cr - pwi * ci), xi + (pwr * ci + pwi * cr)
            s_re[pl.ds(r0, SUBLANES), lanes] = xr
            s_im[pl.ds(r0, SUBLANES), lanes] = xi
            out.append(jnp.broadcast_to(xr[last:last + 1, :], (SUBLANES, LANES)))
            out.append(jnp.broadcast_to(xi[last:last + 1, :], (SUBLANES, LANES)))
        return tuple(out)

    zero = jnp.zeros((SUBLANES, LANES), F32)
    lax.fori_loop(0, n_chunks, body, (zero,) * (2 * n_strips))


def _seq_fwd(proj, conv_w, bc_re, bc_im, cc_re, cc_im, dskip, tab_f, name):
    tp = proj.shape[0]
    dh = proj.shape[1] // 4
    nq = dh // LANES
    sw = STATE * N_GROUPS // nq

    def body(b_ref, c_ref, v_ref, u_ref, w_ref, bre_ref, bim_ref, cre_ref, cim_ref, d_ref, tab_ref,
             co_ref, y_ref, g_ref, s_re, s_im):
        co_ref[...] = b_ref[...] * _dwconv(c_ref[...] * v_ref[...], w_ref)
        u = u_ref[...]
        ub = u.astype(BF16)
        s_re[...] = jnp.dot(ub, bre_ref[...], preferred_element_type=F32)
        s_im[...] = jnp.dot(ub, bim_ref[...], preferred_element_type=F32)
        _scan(s_re, s_im, tab_ref, False)
        y = (jnp.dot(s_re[...].astype(BF16), cre_ref[...], preferred_element_type=F32)
             - jnp.dot(s_im[...].astype(BF16), cim_ref[...], preferred_element_type=F32)
             + d_ref[...] * u)
        y_ref[...] = y
        g_ref[...] = _gelu(y).astype(BF16)

    col = lambda off: pl.BlockSpec((tp, LANES), lambda q, off=off: (0, off * nq + q))
    blk = pl.BlockSpec((tp, LANES), lambda q: (0, q))
    return pl.pallas_call(
        body, name=name, grid=(nq,),
        in_specs=[col(0), col(1), col(2), col(3),
                  pl.BlockSpec((3, LANES), lambda q: (0, q)),
                  pl.BlockSpec((LANES, sw), lambda q: (0, q)), pl.BlockSpec((LANES, sw), lambda q: (0, q)),
                  pl.BlockSpec((sw, LANES), lambda q: (q, 0)), pl.BlockSpec((sw, LANES), lambda q: (q, 0)),
                  pl.BlockSpec((1, LANES), lambda q: (0, q)),
                  pl.BlockSpec((8, SUBLANES, sw), lambda q: (0, 0, q))],
        out_specs=[blk, blk, blk],
        out_shape=[jax.ShapeDtypeStruct((tp, dh), F32), jax.ShapeDtypeStruct((tp, dh), F32),
                   jax.ShapeDtypeStruct((tp, dh), BF16)],
        scratch_shapes=[pltpu.VMEM((tp, sw), F32), pltpu.VMEM((tp, sw), F32)],
        compiler_params=_cparams("parallel"),
    )(proj, proj, proj, proj, conv_w, bc_re, bc_im, cc_re, cc_im, dskip, tab_f)


def _conv_bwd(proj, dco, conv_w, name):
    tp = proj.shape[0]
    dh = proj.shape[1] // 4
    nq = dh // LANES

    def body(b_ref, c_ref, v_ref, dco_ref, w_ref, dproj_ref, dw_ref, stage, sem):
        q = pl.program_id(0)
        cg = c_ref[...]
        vg = v_ref[...]
        cv = cg * vg
        dco_v = dco_ref[...]
        dcv, dw = _dwconv_bwd(cv, dco_v * b_ref[...], w_ref)
        dw_ref[...] = dw
        stage[0] = (dco_v * _dwconv(cv, w_ref)).astype(BF16)
        stage[1] = (dcv * vg).astype(BF16)
        stage[2] = (dcv * cg).astype(BF16)
        copies = [pltpu.make_async_copy(stage.at[p], dproj_ref.at[:, pl.ds((p * nq + q) * LANES, LANES)], sem.at[p])
                  for p in range(3)]
        for cp in copies:
            cp.start()
        for cp in copies:
            cp.wait()

    col = lambda off: pl.BlockSpec((tp, LANES), lambda q, off=off: (0, off * nq + q))
    return pl.pallas_call(
        body, name=name, grid=(nq,),
        in_specs=[col(0), col(1), col(2), pl.BlockSpec((tp, LANES), lambda q: (0, q)),
                  pl.BlockSpec((3, LANES), lambda q: (0, q))],
        out_specs=[pl.BlockSpec(memory_space=pl.ANY), pl.BlockSpec((3, LANES), lambda q: (0, q))],
        out_shape=[jax.ShapeDtypeStruct((tp, 4 * dh), BF16), jax.ShapeDtypeStruct((3, dh), F32)],
        scratch_shapes=[pltpu.VMEM((3, tp, LANES), BF16), pltpu.SemaphoreType.DMA((3,))],
        compiler_params=_cparams("arbitrary"),
    )(proj, proj, proj, dco, conv_w)


def _ssm_bwd(proj, y, dg, dproj, bc_re, bc_im, cc_re, cc_im, dskip, tab_f, tab_r, name):
    tp = proj.shape[0]
    dh = proj.shape[1] // 4
    nq = dh // LANES
    sw = STATE * N_GROUPS // nq

    def body(u_ref, y_ref, dg_ref, dproj_in, bre_ref, bim_ref, cre_ref, cim_ref, d_ref, tabf_ref, tabr_ref,
             dproj_ref, dbre_ref, dbim_ref, dcre_ref, dcim_ref, dd_ref, dar_ref, dai_ref,
             s_re, s_im, l_re, l_im, stage, sem):
        del dproj_in
        q = pl.program_id(0)
        nt = (((1,), (1,)), ((), ()))
        tn = (((0,), (0,)), ((), ()))
        u = u_ref[...]
        ub = u.astype(BF16)
        s_re[...] = jnp.dot(ub, bre_ref[...], preferred_element_type=F32)
        s_im[...] = jnp.dot(ub, bim_ref[...], preferred_element_type=F32)
        _scan(s_re, s_im, tabf_ref, False)
        dy = dg_ref[...] * _gelu_grad(y_ref[...])
        dyb = dy.astype(BF16)
        dd_ref[...] = jnp.sum(dy * u, axis=0, keepdims=True)
        l_re[...] = lax.dot_general(dyb, cre_ref[...], nt, preferred_element_type=F32)
        l_im[...] = -lax.dot_general(dyb, cim_ref[...], nt, preferred_element_type=F32)
        dcre_ref[...] = lax.dot_general(s_re[...].astype(BF16), dyb, tn, preferred_element_type=F32)
        dcim_ref[...] = -lax.dot_general(s_im[...].astype(BF16), dyb, tn, preferred_element_type=F32)
        _scan(l_re, l_im, tabr_ref, True)
        for st in range(sw // LANES):
            lanes = slice(st * LANES, (st + 1) * LANES)
            lr = l_re[:, lanes]
            li = l_im[:, lanes]
            pr = _shift_down(s_re[:, lanes], 1)
            pi = _shift_down(s_im[:, lanes], 1)
            dar_ref[:, lanes] = jnp.sum(lr * pr + li * pi, axis=0, keepdims=True)
            dai_ref[:, lanes] = jnp.sum(li * pr - lr * pi, axis=0, keepdims=True)
        lrb = l_re[...].astype(BF16)
        lib = l_im[...].astype(BF16)
        du = (dy * d_ref[...] + lax.dot_general(lrb, bre_ref[...], nt, preferred_element_type=F32)
              + lax.dot_general(lib, bim_ref[...], nt, preferred_element_type=F32))
        stage[...] = du.astype(BF16)
        dbre_ref[...] = lax.dot_general(ub, lrb, tn, preferred_element_type=F32)
        dbim_ref[...] = lax.dot_general(ub, lib, tn, preferred_element_type=F32)
        cp = pltpu.make_async_copy(stage, dproj_ref.at[:, pl.ds((3 * nq + q) * LANES, LANES)], sem)
        cp.start()
        cp.wait()

    blk = pl.BlockSpec((tp, LANES), lambda q: (0, q))
    bspec = pl.BlockSpec((LANES, sw), lambda q: (0, q))
    cspec = pl.BlockSpec((sw, LANES), lambda q: (q, 0))
    tspec = pl.BlockSpec((8, SUBLANES, sw), lambda q: (0, 0, q))
    nstate = STATE * N_GROUPS
    return pl.pallas_call(
        body, name=name, grid=(nq,),
        in_specs=[pl.BlockSpec((tp, LANES), lambda q: (0, 3 * nq + q)), blk, blk, pl.BlockSpec(memory_space=pl.ANY),
                  bspec, bspec, cspec, cspec, pl.BlockSpec((1, LANES), lambda q: (0, q)), tspec, tspec],
        out_specs=[pl.BlockSpec(memory_space=pl.ANY), bspec, bspec, cspec, cspec,
                   pl.BlockSpec((1, LANES), lambda q: (0, q)),
                   pl.BlockSpec((1, sw), lambda q: (0, q)), pl.BlockSpec((1, sw), lambda q: (0, q))],
        out_shape=[jax.ShapeDtypeStruct((tp, 4 * dh), BF16),
                   jax.ShapeDtypeStruct((LANES, nstate), F32), jax.ShapeDtypeStruct((LANES, nstate), F32),
                   jax.ShapeDtypeStruct((nstate, LANES), F32), jax.ShapeDtypeStruct((nstate, LANES), F32),
                   jax.ShapeDtypeStruct((1, dh), F32),
                   jax.ShapeDtypeStruct((1, nstate), F32), jax.ShapeDtypeStruct((1, nstate), F32)],
        input_output_aliases={3: 0},
        scratch_shapes=[pltpu.VMEM((tp, sw), F32)] * 4 + [pltpu.VMEM((tp, LANES), BF16), pltpu.SemaphoreType.DMA],
        compiler_params=_cparams("arbitrary"),
    )(proj, y, dg, dproj, bc_re, bc_im, cc_re, cc_im, dskip, tab_f, tab_r)


FFN_TILE = 256
CHUNK = 2 * SUBLANES


def _piece_down(cur, prev, k, row):
    return jnp.where(row < k, pltpu.roll(prev, k, 0), pltpu.roll(cur, k, 0))


def _piece_up(cur, nxt, k, row):
    return jnp.where(row >= SUBLANES - k, pltpu.roll(nxt, SUBLANES - k, 0), pltpu.roll(cur, SUBLANES - k, 0))


def _conv_taps(x_ref, c, row):
    r0 = pl.multiple_of(c * CHUNK, CHUNK)
    xa = x_ref[pl.ds(r0, SUBLANES), :]
    xb = x_ref[pl.ds(pl.multiple_of(r0 + SUBLANES, SUBLANES), SUBLANES), :]
    before = x_ref[pl.ds(pl.multiple_of(jnp.maximum(r0 - SUBLANES, 0), SUBLANES), SUBLANES), :]
    before = jnp.where(c > 0, before, 0.0)
    return ((xa, _piece_down(xa, before, 1, row), _piece_down(xa, before, 2, row)),
            (xb, _piece_down(xb, xa, 1, row), _piece_down(xb, xa, 2, row)))


def _conv_piece(taps, w_ref):
    x, x1, x2 = taps
    return w_ref[2:3, :] * x + w_ref[1:2, :] * x1 + w_ref[0:1, :] * x2


def _ffn_act(up, fw, fb, name):
    tp, two_ff = up.shape
    dff = two_ff // 2
    tc = FFN_TILE
    nj = dff // tc

    def body(ua_ref, uv_ref, wa_ref, wv_ref, ba_ref, bv_ref, act_ref):
        row = lax.broadcasted_iota(jnp.int32, (SUBLANES, tc), 0)

        def chunk(c, carry):
            ta = _conv_taps(ua_ref, c, row)
            tv = _conv_taps(uv_ref, c, row)
            out = []
            for p in range(2):
                a = _conv_piece(ta[p], wa_ref) + ba_ref[...]
                v = _conv_piece(tv[p], wv_ref) + bv_ref[...]
                out.append(a * _sigmoid(a) * v)
            r0 = pl.multiple_of(c * CHUNK, CHUNK)
            act_ref[pl.ds(r0, CHUNK), :] = jnp.concatenate(out, axis=0).astype(BF16)
            return carry

        lax.fori_loop(0, tp // CHUNK, chunk, 0, unroll=2)

    lo = lambda r: pl.BlockSpec((r, tc), lambda j: (0, j))
    hi = lambda r: pl.BlockSpec((r, tc), lambda j: (0, nj + j))
    return pl.pallas_call(
        body, name=name, grid=(nj,),
        in_specs=[lo(tp), hi(tp), lo(3), hi(3), lo(1), hi(1)],
        out_specs=lo(tp),
        out_shape=jax.ShapeDtypeStruct((tp, dff), BF16),
        compiler_params=_cparams("parallel"))(up, up, fw, fw, fb, fb)


def _ffn_bwd(up, dact, fw, fb, name):
    tp, two_ff = up.shape
    dff = two_ff // 2
    tc = FFN_TILE
    nj = dff // tc

    def body(ua_ref, uv_ref, da_ref, wa_ref, wv_ref, ba_ref, bv_ref,
             dup_ref, dwa_ref, dwv_ref, dba_ref, dbv_ref, stage, sem):
        j = pl.program_id(0)
        n_chunks = tp // CHUNK
        row = lax.broadcasted_iota(jnp.int32, (SUBLANES, tc), 0)
        zero = jnp.zeros((SUBLANES, tc), F32)

        def chunk(i, carry):
            c = n_chunks - 1 - i
            da_next, dv_next, acc = carry
            ta = _conv_taps(ua_ref, c, row)
            tv = _conv_taps(uv_ref, c, row)
            r0 = pl.multiple_of(c * CHUNK, CHUNK)
            da, dv = [], []
            for p in range(2):
                a = _conv_piece(ta[p], wa_ref) + ba_ref[...]
                v = _conv_piece(tv[p], wv_ref) + bv_ref[...]
                sg = _sigmoid(a)
                dact_p = da_ref[pl.ds(pl.multiple_of(r0 + p * SUBLANES, SUBLANES), SUBLANES), :]
                da.append(dact_p * v * sg * (1.0 + a * (1.0 - sg)))
                dv.append(dact_p * a * sg)
            for s, (d, d_next, w_ref) in enumerate(((da, da_next, wa_ref), (dv, dv_next, wv_ref))):
                after = (d[1], d_next)
                dx = [w_ref[2:3, :] * d[p] + w_ref[1:2, :] * _piece_up(d[p], after[p], 1, row)
                      + w_ref[0:1, :] * _piece_up(d[p], after[p], 2, row) for p in range(2)]
                stage[s, pl.ds(r0, CHUNK), :] = jnp.concatenate(dx, axis=0).astype(BF16)
            acc = list(acc)
            for s, (d, taps) in enumerate(((da, ta), (dv, tv))):
                for k in range(3):
                    acc[4 * s + k] = acc[4 * s + k] + d[0] * taps[0][2 - k] + d[1] * taps[1][2 - k]
                acc[4 * s + 3] = acc[4 * s + 3] + d[0] + d[1]
            return da[0], dv[0], tuple(acc)

        _, _, acc = lax.fori_loop(0, n_chunks, chunk, (zero, zero, (zero,) * 8), unroll=2)
        sums = [jnp.sum(x, axis=0, keepdims=True) for x in acc]
        dwa_ref[...] = jnp.concatenate(sums[0:3], axis=0)
        dba_ref[...] = sums[3]
        dwv_ref[...] = jnp.concatenate(sums[4:7], axis=0)
        dbv_ref[...] = sums[7]
        copies = [pltpu.make_async_copy(stage.at[p], dup_ref.at[:, pl.ds((p * nj + j) * tc, tc)], sem.at[p])
                  for p in range(2)]
        for cp in copies:
            cp.start()
        for cp in copies:
            cp.wait()

    lo = lambda r: pl.BlockSpec((r, tc), lambda j: (0, j))
    hi = lambda r: pl.BlockSpec((r, tc), lambda j: (0, nj + j))
    return pl.pallas_call(
        body, name=name, grid=(nj,),
        in_specs=[lo(tp), hi(tp), lo(tp), lo(3), hi(3), lo(1), hi(1)],
        out_specs=[pl.BlockSpec(memory_space=pl.ANY), lo(3), lo(3), lo(1), lo(1)],
        out_shape=[jax.ShapeDtypeStruct((tp, two_ff), BF16),
                   jax.ShapeDtypeStruct((3, dff), F32), jax.ShapeDtypeStruct((3, dff), F32),
                   jax.ShapeDtypeStruct((1, dff), F32), jax.ShapeDtypeStruct((1, dff), F32)],
        scratch_shapes=[pltpu.VMEM((2, tp, tc), BF16), pltpu.SemaphoreType.DMA((2,))],
        compiler_params=_cparams("arbitrary"))(up, up, dact, fw, fw, fb, fb)


def _zoh(lr, li, ld):
    dt = jnp.exp(ld)
    mag = jnp.exp(lr * dt)
    ang = li * dt
    ar = mag * jnp.cos(ang)
    ai = mag * jnp.sin(ang)
    den = lr * lr + li * li
    nr = ar - 1.0
    fr = (nr * lr + ai * li) / den
    fi = (ai * lr - nr * li) / den
    return dt, ar, ai, den, nr, fr, fi


def _s5_prep(lr, li, ld, b_re, b_im, name):
    nstate = lr.shape[1]

    def tables(tab_ref, ar, ai, reverse):
        pows = [(ar, ai)]
        for _ in range(SUBLANES - 1):
            pr, pi = pows[-1]
            pows.append((pr * ar - pi * ai, pr * ai + pi * ar))
        row = lax.broadcasted_iota(jnp.int32, (SUBLANES, nstate), 0)
        for level, k in enumerate((1, 2, 4)):
            mask = (row <= SUBLANES - 1 - k) if reverse else (row >= k)
            tab_ref[2 * level] = jnp.where(mask, pows[k - 1][0], 0.0)
            tab_ref[2 * level + 1] = jnp.where(mask, pows[k - 1][1], 0.0)
        pr = jnp.zeros((SUBLANES, nstate), F32)
        pi = jnp.zeros((SUBLANES, nstate), F32)
        for t in range(SUBLANES):
            k = SUBLANES - 1 - t if reverse else t
            pr = jnp.where(row == t, pows[k][0], pr)
            pi = jnp.where(row == t, pows[k][1], pi)
        tab_ref[6] = pr
        tab_ref[7] = pi

    def body(lr_ref, li_ref, ld_ref, bre_ref, bim_ref, tabf_ref, tabr_ref, bcre_ref, bcim_ref):
        _, ar, ai, _, _, fr, fi = _zoh(lr_ref[...], li_ref[...], ld_ref[...])
        tables(tabf_ref, ar, ai, False)
        tables(tabr_ref, ar, -ai, True)
        bre = bre_ref[...]
        bim = bim_ref[...]
        bcre_ref[...] = (fr * bre - fi * bim).astype(BF16)
        bcim_ref[...] = (fr * bim + fi * bre).astype(BF16)

    vmem = pl.BlockSpec(memory_space=pltpu.VMEM)
    return pl.pallas_call(
        body, name=name, in_specs=[vmem] * 5, out_specs=[vmem] * 4,
        out_shape=[jax.ShapeDtypeStruct((8, SUBLANES, nstate), F32)] * 2
        + [jax.ShapeDtypeStruct(b_re.shape, BF16)] * 2)(lr, li, ld, b_re, b_im)


def _s5_prep_bwd(lr, li, ld, b_re, b_im, da_re, da_im, dbc_re, dbc_im, name):
    def body(lr_ref, li_ref, ld_ref, bre_ref, bim_ref, dar_ref, dai_ref, dbcre_ref, dbcim_ref,
             dlr_ref, dli_ref, dld_ref, dbre_ref, dbim_ref):
        lr, li = lr_ref[...], li_ref[...]
        dt, ar, ai, den, nr, fr, fi = _zoh(lr, li, ld_ref[...])
        bre, bim = bre_ref[...], bim_ref[...]
        gre, gim = dbcre_ref[...], dbcim_ref[...]
        dbre_ref[...] = fr * gre + fi * gim
        dbim_ref[...] = fr * gim - fi * gre
        g_fr = jnp.sum(gre * bre + gim * bim, axis=0, keepdims=True)
        g_fi = jnp.sum(gim * bre - gre * bim, axis=0, keepdims=True)
        g_ar = dar_ref[...] + (g_fr * lr - g_fi * li) / den
        g_ai = dai_ref[...] + (g_fr * li + g_fi * lr) / den
        d_lr = (g_fr * (nr - 2.0 * fr * lr) + g_fi * (ai - 2.0 * fi * lr)) / den
        d_li = (g_fr * (ai - 2.0 * fr * li) - g_fi * (nr + 2.0 * fi * li)) / den
        g_logmag = g_ar * ar + g_ai * ai
        g_ang = g_ai * ar - g_ar * ai
        dlr_ref[...] = d_lr + g_logmag * dt
        dli_ref[...] = d_li + g_ang * dt
        d_ld = (g_logmag * lr + g_ang * li) * dt
        n = d_ld.shape[1]
        sh = 1
        while sh < STATE:
            d_ld = d_ld + pltpu.roll(d_ld, n - sh, 1)
            sh *= 2
        dld_ref[...] = d_ld

    vmem = pl.BlockSpec(memory_space=pltpu.VMEM)
    row = jax.ShapeDtypeStruct(lr.shape, F32)
    return pl.pallas_call(
        body, name=name, in_specs=[vmem] * 9, out_specs=[vmem] * 5,
        out_shape=[row, row, row, jax.ShapeDtypeStruct(b_re.shape, F32), jax.ShapeDtypeStruct(b_re.shape, F32)],
    )(lr, li, ld, b_re, b_im, da_re, da_im, dbc_re, dbc_im)


def _compact_b(bb):
    bq = bb.reshape(N_GROUPS // 8, 8, STATE, GROUP)
    m = jnp.einsum("ab,qbph->qahbp", jnp.eye(8, dtype=bb.dtype), bq).reshape(N_GROUPS // 8, LANES, 8 * STATE)
    return m.transpose(1, 0, 2).reshape(LANES, N_GROUPS * STATE)


def _expand_b(m):
    d = m.reshape(8, GROUP, N_GROUPS // 8, 8, STATE)
    return jnp.einsum("ahqap->qahp", d).reshape(N_GROUPS, GROUP, STATE)


def _compact_c(c):
    cq = c.reshape(N_GROUPS // 8, 8, GROUP, STATE)
    return jnp.einsum("ab,qbhp->qbpah", jnp.eye(8, dtype=c.dtype), cq).reshape(N_GROUPS * STATE, LANES)


def _expand_c(m):
    d = m.reshape(N_GROUPS // 8, 8, STATE, 8, GROUP)
    return jnp.einsum("qbpbh->qbhp", d).reshape(N_GROUPS, GROUP, STATE)


def _local_step(x, target, p, ex):
    seq, d = x.shape
    n_real = N_META + seq
    tp = -(-n_real // ROW_ALIGN) * ROW_ALIGN

    nstate = N_GROUPS * STATE
    s5 = (p["ssm_lam_re"].reshape(1, nstate), p["ssm_lam_im"].reshape(1, nstate),
          jnp.repeat(p["ssm_log_dt"].reshape(-1), STATE).reshape(1, nstate),
          _compact_b(p["ssm_b_re"]), _compact_b(p["ssm_b_im"]))
    tab_f, tab_r, bc_re, bc_im = _s5_prep(*s5, "s5_prep")
    cc_re = _compact_c(p["ssm_c_re"]).astype(BF16)
    cc_im = _compact_c(p["ssm_c_im"]).astype(BF16)
    dskip = p["ssm_d"].reshape(1, -1)
    dh = dskip.shape[1]

    h0, hn1 = _input_norm_fwd(x, p["meta_tokens"], p["norm_mix_g"] + ex.zero, tp, "norm_mix")
    first = ex.weights("first", hn1)
    proj = _mm(hn1, first["w_in"], "nn", "proj")
    started = ex.forward("mid", proj)
    co, y, g = _seq_fwd(proj, p["conv_w"] + started[0, 0], bc_re, bc_im, cc_re, cc_im, dskip, tab_f, "seq_fwd")
    mid = ex.weights("mid", g)
    started = ex.forward("late", g)
    z = _mm(g, mid["ssm_w_glu"], "nn", "glu", after=started)
    mixed = _mix_fwd(co, y, z, p["gain_conv_out"], p["gain_ssm_out"], "mix_fwd")
    h1, hn2 = _proj_res_norm(mixed, mid["w_out"], h0, p["norm_ffn_g"], "out_proj_norm")
    late = ex.weights("late", hn2)
    up = _mm(hn2, late["w_up"], "nn", "up_proj")
    act = _ffn_act(up, p["ffn_conv_w"], p["ffn_conv_b"], "ffn_act")
    loss, dh2, dh2b, d_gfin = _proj_loss_bwd(act, late["w_down"], h1, target, p["norm_final_g"], n_real,
                                             "down_proj_loss")

    g_w_down = _mm(act, dh2b, "tn", "g_w_down")
    dact = _mm(dh2b, late["w_down"], "nt", "d_act")
    dup, dfw_a, dfw_v, dfb_a, dfb_v = _ffn_bwd(up, dact, p["ffn_conv_w"], p["ffn_conv_b"], "ffn_bwd")
    g_w_up = _mm(hn2, dup, "tn", "g_w_up")
    started = ex.grads_ready("late", {"w_up": g_w_up, "w_down": g_w_down})
    dh1, dh1b, d_gffn = _proj_norm_bwd(dup, late["w_up"], h1, p["norm_ffn_g"], dh2, started, "d_hn2_norm_bwd")
    started = ex.grads_send("late", dh1)
    g_w_out = _mm(mixed, dh1b, "tn", "g_w_out", after=started)
    dco, dz, dgp, d_gc, d_gs = _proj_mix_bwd(dh1b, mid["w_out"], co, y, z, p["gain_conv_out"],
                                             p["gain_ssm_out"], "d_mixed_mix_bwd")
    g_w_glu = _mm(g, dz, "tn", "g_w_glu")
    started = ex.grads_ready("mid", {"ssm_w_glu": g_w_glu, "w_out": g_w_out})
    dg = _mm(dz, mid["ssm_w_glu"], "nt", "d_gelu", acc_in=dgp, after=started)
    started = ex.grads_send("mid", dg)
    dproj, d_conv_w = _conv_bwd(proj, dco, p["conv_w"] + started[0, 0], "conv_bwd")
    (dproj, dbc_re, dbc_im, dcc_re, dcc_im, d_dskip, da_re, da_im) = _ssm_bwd(
        proj, y, dg, dproj, bc_re, bc_im, cc_re, cc_im, dskip, tab_f, tab_r, "ssm_bwd")
    g_w_in = _mm(hn1, dproj, "tn", "g_w_in")
    started = ex.grads_ready("first", {"w_in": g_w_in})
    dhn1 = _mm(dproj, first["w_in"], "nt", "d_hn1", after=started)
    started = ex.grads_send("first", dhn1)
    grad_x, d_meta, d_gmix = _input_norm_bwd(h0, p["norm_mix_g"] + started[0, 0], dhn1, dh1, n_real, "norm_mix_bwd")

    d_lam_re, d_lam_im, d_log_dt, d_b_re, d_b_im = _s5_prep_bwd(*s5, da_re, da_im, dbc_re, dbc_im, "s5_prep_bwd")
    d_lam_re, d_lam_im = d_lam_re.reshape(N_GROUPS, STATE), d_lam_im.reshape(N_GROUPS, STATE)
    d_log_dt = d_log_dt[0, ::STATE]
    d_b_re, d_b_im = _expand_b(d_b_re), _expand_b(d_b_im)
    grads = {
        "meta_tokens": d_meta, "norm_mix_g": d_gmix, "w_in": g_w_in, "conv_w": d_conv_w,
        "ssm_lam_re": d_lam_re, "ssm_lam_im": d_lam_im, "ssm_log_dt": d_log_dt,
        "ssm_b_re": d_b_re, "ssm_b_im": d_b_im, "ssm_c_re": _expand_c(dcc_re), "ssm_c_im": _expand_c(dcc_im),
        "ssm_d": d_dskip.reshape(N_GROUPS, GROUP), "ssm_w_glu": g_w_glu,
        "gain_conv_out": d_gc, "gain_ssm_out": d_gs, "w_out": g_w_out, "norm_ffn_g": d_gffn,
        "w_up": g_w_up, "ffn_conv_w": jnp.concatenate([dfw_a, dfw_v], axis=1),
        "ffn_conv_b": jnp.concatenate([dfb_a, dfb_v], axis=1), "w_down": g_w_down, "norm_final_g": d_gfin,
    }
    return loss[0, 0], grad_x, grads


def _view(ref, axis, start, size):
    idx = [slice(None)] * len(ref.shape)
    idx[axis] = pl.ds(start, size)
    return ref.at[tuple(idx)]


def _exchange(name, ins, outs, aliases, local_copies, remote_copies):
    ni, no = len(ins), len(outs)
    nl, nr = len(local_copies), len(remote_copies)

    def body(*refs):
        in_refs, out_refs = refs[:ni], refs[ni:ni + no]
        send_sems, recv_sems, local_sems = refs[ni + no:]
        x, y, c = lax.axis_index("x"), lax.axis_index("y"), lax.axis_index("c")
        pos = (x, y, c, 2 * x + y)
        locals_ = [pltpu.make_async_copy(s(in_refs, out_refs, pos), d(in_refs, out_refs, pos), local_sems.at[i])
                   for i, (s, d) in enumerate(local_copies)]
        remotes = []
        for i, (s, d, flip) in enumerate(remote_copies):
            peer = (1 - x if "x" in flip else x, 1 - y if "y" in flip else y, 1 - c if "c" in flip else c)
            remotes.append(pltpu.make_async_remote_copy(
                src_ref=s(in_refs, out_refs, pos), dst_ref=d(in_refs, out_refs, pos),
                send_sem=send_sems.at[i], recv_sem=recv_sems.at[i], device_id=peer, device_id_type=MESH))
        for cp in locals_ + remotes:
            cp.start()
        for cp in remotes:
            cp.wait_recv()
        for cp in remotes:
            cp.wait_send()
        for cp in locals_:
            cp.wait()

    hbm = pl.BlockSpec(memory_space=pl.ANY)
    return pl.pallas_call(
        body, name=name, in_specs=[hbm] * ni, out_specs=[hbm] * no, out_shape=outs,
        input_output_aliases=aliases,
        scratch_shapes=[pltpu.SemaphoreType.DMA((nr,)), pltpu.SemaphoreType.DMA((nr,)),
                        pltpu.SemaphoreType.DMA((max(nl, 1),))],
    )(*ins)


BIG = {"w_in": (0, 1), "ssm_w_glu": (1, 0), "w_out": (1, 0), "w_up": (0, 1), "w_down": (1, 0)}
BIG_NAMES = tuple(BIG)
FLIPS = ("y", "x", "xy")


def _peer_chip(pos, flip):
    x, y, _, _ = pos
    return 2 * (1 - x if "x" in flip else x) + (1 - y if "y" in flip else y)


def _block_rows(rows, cols, itemsize, mult):
    return _pick_tile(rows, max(mult, (2 * 1024 * 1024) // (cols * itemsize)), mult)


def _cast_into_full(w, kc, shard_axis, name):
    r, cdim = w.shape
    tr = _block_rows(r, cdim, 4, 16)
    nb = r // tr

    def body(kc_ref, w_ref, o_ref):
        o_ref[...] = w_ref[...].astype(BF16)

    if shard_axis == 1:
        full, o_spec = (r, 4 * cdim), pl.BlockSpec((tr, cdim), lambda i, kc: (i, kc[0]))
    else:
        full, o_spec = (4 * r, cdim), pl.BlockSpec((tr, cdim), lambda i, kc: (kc[0] * nb + i, 0))
    return pl.pallas_call(
        body, name=name,
        grid_spec=pltpu.PrefetchScalarGridSpec(
            num_scalar_prefetch=1, grid=(nb,), in_specs=[pl.BlockSpec((tr, cdim), lambda i, kc: (i, 0))],
            out_specs=o_spec),
        out_shape=jax.ShapeDtypeStruct(full, BF16), compiler_params=_cparams("parallel"))(kc, w)


def _pair_sum(g, recv, kc, half_axis, name, out_dtype):
    hr, hc = recv.shape
    tr = _block_rows(hr, hc, 4, 16)
    nb = hr // tr

    def body(kc_ref, g_ref, r_ref, o_ref):
        o_ref[...] = (g_ref[...] + r_ref[...]).astype(out_dtype)

    if half_axis == 0:
        g_spec = pl.BlockSpec((tr, hc), lambda i, kc: (kc[1] * nb + i, 0))
    elif half_axis == 1:
        g_spec = pl.BlockSpec((tr, hc), lambda i, kc: (i, kc[1]))
    else:
        g_spec = pl.BlockSpec((tr, hc), lambda i, kc: (i, 0))
    same = pl.BlockSpec((tr, hc), lambda i, kc: (i, 0))
    return pl.pallas_call(
        body, name=name,
        grid_spec=pltpu.PrefetchScalarGridSpec(num_scalar_prefetch=1, grid=(nb,), in_specs=[g_spec, same],
                                               out_specs=same),
        out_shape=jax.ShapeDtypeStruct((hr, hc), out_dtype), compiler_params=_cparams("parallel"))(kc, g, recv)


def _chip_sum(own, recv, kc, own_axis, out_axis, name):
    _, sr, sc = recv.shape
    tr = _block_rows(sr, sc, 4, 16)
    nb = sr // tr

    def body(kc_ref, o_ref, r_ref, t_ref):
        k = kc_ref[0]
        own_v = o_ref[...].astype(F32)
        r = [r_ref[m].astype(F32) for m in range(3)]
        terms = []
        for kk in range(4):
            m = jnp.bitwise_xor(k, kk)
            terms.append(jnp.where(m == 0, own_v, jnp.where(m == 1, r[0], jnp.where(m == 2, r[1], r[2]))))
        t_ref[...] = (terms[0] + terms[1]) + (terms[2] + terms[3])

    if own_axis == 0:
        own_spec = pl.BlockSpec((tr, sc), lambda i, kc: (kc[0] * nb + i, 0))
    elif own_axis == 1:
        own_spec = pl.BlockSpec((tr, sc), lambda i, kc: (i, kc[0]))
    else:
        own_spec = pl.BlockSpec((tr, sc), lambda i, kc: (kc[1] * nb + i, 0))
    if out_axis == 0:
        out_full, out_spec = (2 * sr, sc), pl.BlockSpec((tr, sc), lambda i, kc: (kc[1] * nb + i, 0))
    else:
        out_full, out_spec = (sr, 2 * sc), pl.BlockSpec((tr, sc), lambda i, kc: (i, kc[1]))
    return pl.pallas_call(
        body, name=name,
        grid_spec=pltpu.PrefetchScalarGridSpec(
            num_scalar_prefetch=1, grid=(nb,),
            in_specs=[own_spec, pl.BlockSpec((3, tr, sc), lambda i, kc: (0, i, 0))],
            out_specs=out_spec),
        out_shape=jax.ShapeDtypeStruct(out_full, F32), compiler_params=_cparams("parallel"))(kc, own, recv)


def _adamw(w, g, m, v, name):
    r, cdim = w.shape
    tr = _block_rows(r, cdim, 4, 8)
    c1 = 1.0 - ADAM_B1 ** ADAM_STEP
    c2 = 1.0 - ADAM_B2 ** ADAM_STEP

    def body(w_ref, g_ref, m_ref, v_ref, go_ref, d_ref, nm_ref, nv_ref):
        gv = g_ref[...]
        go_ref[...] = gv
        nm = ADAM_B1 * m_ref[...] + (1.0 - ADAM_B1) * gv
        nv = ADAM_B2 * v_ref[...] + (1.0 - ADAM_B2) * (gv * gv)
        d_ref[...] = -ADAM_LR * ((nm / c1) / (jnp.sqrt(nv / c2) + ADAM_EPS) + ADAM_WD * w_ref[...])
        nm_ref[...] = nm
        nv_ref[...] = nv

    spec = _rows(cdim, tr)
    return pl.pallas_call(body, name=name, grid=(r // tr,), in_specs=[spec] * 4, out_specs=[spec] * 4,
                          out_shape=[jax.ShapeDtypeStruct((r, cdim), F32)] * 4,
                          compiler_params=_cparams("parallel"))(w, g, m, v)


def _adamw_whole(ws, gs, ms, vs, name):
    n = len(ws)
    c1 = 1.0 - ADAM_B1 ** ADAM_STEP
    c2 = 1.0 - ADAM_B2 ** ADAM_STEP

    def body(*refs):
        for i in range(n):
            w_ref, g_ref, m_ref, v_ref, d_ref, nm_ref, nv_ref = [refs[j * n + i] for j in range(7)]
            gv = g_ref[...]
            nm = ADAM_B1 * m_ref[...] + (1.0 - ADAM_B1) * gv
            nv = ADAM_B2 * v_ref[...] + (1.0 - ADAM_B2) * (gv * gv)
            d_ref[...] = -ADAM_LR * ((nm / c1) / (jnp.sqrt(nv / c2) + ADAM_EPS) + ADAM_WD * w_ref[...])
            nm_ref[...] = nm
            nv_ref[...] = nv

    vmem = pl.BlockSpec(memory_space=pltpu.VMEM)
    out = pl.pallas_call(body, name=name, in_specs=[vmem] * (4 * n), out_specs=[vmem] * (3 * n),
                         out_shape=[jax.ShapeDtypeStruct(a.shape, F32) for a in ws] * 3,
                         compiler_params=pltpu.CompilerParams(vmem_limit_bytes=VMEM_LIMIT))(*ws, *gs, *ms, *vs)
    return out[:n], out[n:2 * n], out[2 * n:]


SIDE_EFFECT = pltpu.SideEffectType.DATAFLOW_SIDE_EFFECTING


def _descriptors(copies, refs, send_sems, recv_sems):
    x, y, c = lax.axis_index("x"), lax.axis_index("y"), lax.axis_index("c")
    pos = (x, y, c, 2 * x + y)
    out = []
    for i, (s, d, flip) in enumerate(copies):
        peer = (1 - x if "x" in flip else x, 1 - y if "y" in flip else y, 1 - c if "c" in flip else c)
        out.append(pltpu.make_async_remote_copy(
            src_ref=s(refs, refs, pos), dst_ref=d(refs, refs, pos),
            send_sem=send_sems.at[i], recv_sem=recv_sems.at[i], device_id=peer, device_id_type=MESH))
    return out


def _exchange_start(name, bufs, copies, after=None):
    n, nr = len(bufs), len(copies)
    na = 0 if after is None else 1

    def body(*refs):
        for cp in _descriptors(copies, refs[:n], refs[n + na], refs[n + na + 1]):
            cp.start()
        token = refs[2 * n + na + 2]
        token[...] = jnp.zeros_like(token)

    hbm = pl.BlockSpec(memory_space=pltpu.HBM)
    sem = pl.BlockSpec(memory_space=pltpu.SEMAPHORE)
    out = pl.pallas_call(
        body, name=name,
        in_specs=[hbm] * n + [pl.BlockSpec(memory_space=pl.ANY)] * na,
        out_specs=(sem, sem, *[hbm] * n, pl.BlockSpec(memory_space=pltpu.VMEM)),
        out_shape=(pltpu.SemaphoreType.DMA((nr,)), pltpu.SemaphoreType.DMA((nr,)),
                   *[pltpu.HBM(b.shape, b.dtype) for b in bufs], jax.ShapeDtypeStruct((SUBLANES, LANES), F32)),
        input_output_aliases={i: 2 + i for i in range(n)},
        compiler_params=pltpu.CompilerParams(has_side_effects=SIDE_EFFECT),
    )(*[pltpu.with_memory_space_constraint(b, pltpu.HBM) for b in bufs], *([after] * na))
    return out[0], out[1], list(out[2:2 + n]), out[2 + n]


def _exchange_wait(name, send_sems, recv_sems, bufs, copies, after):
    n = len(bufs)

    def body(*refs):
        for cp in _descriptors(copies, refs[:n], refs[n], refs[n + 1]):
            cp.wait_send()
            cp.wait_recv()

    hbm = pl.BlockSpec(memory_space=pltpu.HBM)
    sem = pl.BlockSpec(memory_space=pltpu.SEMAPHORE)
    out = pl.pallas_call(
        body, name=name,
        in_specs=[hbm] * n + [sem, sem, pl.BlockSpec(memory_space=pl.ANY)],
        out_specs=tuple([hbm] * n),
        out_shape=tuple(pltpu.HBM(b.shape, b.dtype) for b in bufs),
        input_output_aliases={i: i for i in range(n)},
        compiler_params=pltpu.CompilerParams(has_side_effects=SIDE_EFFECT),
    )(*bufs, send_sems, recv_sems, after)
    return list(out)


FIRST = ("w_in",)
MID = ("ssm_w_glu", "w_out")
LATE = ("w_up", "w_down")
GROUPS = {"first": FIRST, "mid": MID, "late": LATE}


def _gather_copies(names, shard_shapes):
    def region(i, chip, c):
        half_axis, shard_axis = BIG[names[i]]
        ssize = shard_shapes[i][shard_axis]
        hsize = shard_shapes[i][half_axis] // 2
        return lambda ref: _view(_view(ref, shard_axis, chip * ssize, ssize), half_axis, c * hsize, hsize)

    ici, d2d = [], []
    for i in range(len(names)):
        for flip in FLIPS:
            ici.append((lambda I, O, pos, i=i: region(i, pos[3], pos[2])(I[i]),
                        lambda I, O, pos, i=i: region(i, pos[3], pos[2])(O[i]), flip))
            d2d.append((lambda I, O, pos, i=i, flip=flip: region(i, _peer_chip(pos, flip), pos[2])(I[i]),
                        lambda I, O, pos, i=i, flip=flip: region(i, _peer_chip(pos, flip), pos[2])(O[i]), "c"))
    return ici, d2d


def _half_shape(n, shape):
    r, cdim = shape
    return (r // 2, cdim) if BIG[n][0] == 0 else (r, cdim // 2)


def _sub_shape(n, shape):
    hr, hc = _half_shape(n, shape)
    return (hr, hc // 4) if BIG[n][1] == 1 else (hr // 4, hc)


def _pair_copies(names, shapes, with_pack, dst_off):
    n = len(names)

    def other_half(i, ref, pos):
        half_axis = BIG[names[i]][0]
        hsize = shapes[i][half_axis] // 2
        return _view(ref, half_axis, (1 - pos[2]) * hsize, hsize)

    copies = [(lambda I, O, pos, i=i: other_half(i, I[i], pos), lambda I, O, pos, i=i: O[dst_off + i], "c")
              for i in range(n)]
    if with_pack:
        copies.append((lambda I, O, pos: I[n], lambda I, O, pos: O[dst_off + n], "c"))
    return copies


def _chip_copies(names, shapes, pack_rows, dst_off):
    n = len(names)

    def piece(i, ref, chip):
        shard_axis = BIG[names[i]][1]
        ssize = _sub_shape(names[i], shapes[i])[shard_axis]
        return _view(ref, shard_axis, chip * ssize, ssize)

    copies = []
    for i in range(n):
        for slot, flip in enumerate(FLIPS):
            copies.append((lambda I, O, pos, i=i, flip=flip: piece(i, I[i], _peer_chip(pos, flip)),
                           lambda I, O, pos, i=i, slot=slot: O[dst_off + i].at[slot], flip))
    if pack_rows:
        for slot, flip in enumerate(FLIPS):
            copies.append((lambda I, O, pos: _view(I[n], 0, pos[2] * (pack_rows // 2), pack_rows // 2),
                           lambda I, O, pos, slot=slot: O[dst_off + n].at[slot], flip))
    return copies


class _Exchanges:
    def __init__(self, shards, tiny, kc):
        self.kc = kc
        wb = {n: _cast_into_full(shards[n], kc, BIG[n][1], "cast_" + n) for n in BIG_NAMES}
        own = (lambda I, O, pos: I[0], lambda I, O, pos: O[0].at[pos[3]])
        self.tiny_all = _exchange("gather_tiny", [tiny], [jax.ShapeDtypeStruct((4,) + tiny.shape, F32)], {},
                                  [own], [own + (flip,) for flip in FLIPS])[0]
        self.gathering, self.forwarding, self.pairing, self.reducing = {}, {}, {}, {}
        after = self.tiny_all
        self.zero = 0.0
        for group, names in GROUPS.items():
            copies = _gather_copies(names, [shards[n].shape for n in names])
            started = _exchange_start("gather_%s_start" % group, [wb[n] for n in names], copies[0], after)
            self.gathering[group] = (started, copies)
            after = started[2][0]
            self.zero = self.zero + started[3][0, 0]

    def forward(self, group, after):
        (send_sems, recv_sems, bufs, _), (ici, d2d) = self.gathering[group]
        got = _exchange_wait("gather_%s_wait" % group, send_sems, recv_sems, bufs, ici, after)
        self.forwarding[group] = (_exchange_start("forward_%s_start" % group, got, d2d), d2d)
        return self.forwarding[group][0][3]

    def weights(self, group, after):
        if group not in self.forwarding:
            after = self.forward(group, after)
        (send_sems, recv_sems, bufs, _), d2d = self.forwarding[group]
        full = _exchange_wait("forward_%s_wait" % group, send_sems, recv_sems, bufs, d2d, after)
        return dict(zip(GROUPS[group], full))

    def grads_ready(self, group, grads):
        names = GROUPS[group]
        gs = [grads[n] for n in names]
        land = [lax.empty(_half_shape(n, g.shape), F32) for n, g in zip(names, gs)]
        copies = _pair_copies(names, [g.shape for g in gs], False, len(names))
        started = _exchange_start("pair_%s_start" % group, gs + land, copies)
        self.pairing[group] = (started, copies)
        return started[3]

    def grads_send(self, group, after):
        names = GROUPS[group]
        n = len(names)
        (send_sems, recv_sems, bufs, _), copies = self.pairing[group]
        bufs = _exchange_wait("pair_%s_wait" % group, send_sems, recv_sems, bufs, copies, after)
        chip = [_pair_sum(bufs[i], bufs[n + i], self.kc, BIG[names[i]][0], "pair_sum_" + names[i], BF16)
                for i in range(n)]
        shapes = [bufs[i].shape for i in range(n)]
        land = [lax.empty((3,) + _sub_shape(names[i], shapes[i]), BF16) for i in range(n)]
        copies = _chip_copies(names, shapes, 0, n)
        started = _exchange_start("reduce_%s_start" % group, chip + land, copies)
        self.reducing[group] = (started, copies)
        return started[3]

    def finish_pack(self, pack):
        kc = self.kc
        prow = pack.shape[0] // 2
        recv = _exchange("reduce_d2d", [pack], [jax.ShapeDtypeStruct(pack.shape, F32)], {}, [],
                         _pair_copies((), [], True, 0))
        chip_pack = _pair_sum(pack, recv[0], kc, None, "pair_sum_pack", F32)
        copies = _chip_copies((), [], pack.shape[0], 1)
        land = lax.empty((3, prow, pack.shape[1]), F32)
        pack_sems_s, pack_sems_r, pack_bufs, after = _exchange_start("reduce_pack_start", [chip_pack, land], copies)

        names, chips, recvs = (), [], []
        for group, group_names in GROUPS.items():
            (send_sems, recv_sems, bufs, _), group_copies = self.reducing[group]
            bufs = _exchange_wait("reduce_%s_wait" % group, send_sems, recv_sems, bufs, group_copies, after)
            n = len(group_names)
            names, chips, recvs = names + group_names, chips + bufs[:n], recvs + bufs[n:]
            after = bufs[n]
        total = [_chip_sum(chips[i], recvs[i], kc, BIG[n][1], BIG[n][0], "chip_sum_" + n)
                 for i, n in enumerate(names)]

        def my_half(half_axis, ref, pos):
            hsize = ref.shape[half_axis] // 2
            return _view(ref, half_axis, pos[2] * hsize, hsize)

        swap = [(lambda I, O, pos, i=i, n=n: my_half(BIG[n][0], I[i], pos),
                 lambda I, O, pos, i=i, n=n: my_half(BIG[n][0], O[i], pos), "c") for i, n in enumerate(names)]
        self.swapping = (_exchange_start("swap_start", total, swap), swap, names)

        chip_pack, recv_pack = _exchange_wait("reduce_pack_wait", pack_sems_s, pack_sems_r, pack_bufs, copies,
                                              self.swapping[0][3])
        total_pack = _chip_sum(chip_pack, recv_pack, kc, None, 0, "chip_sum_pack")
        swap = [(lambda I, O, pos: my_half(0, I[0], pos), lambda I, O, pos: my_half(0, O[0], pos), "c")]
        return _exchange("swap_pack", [total_pack], [jax.ShapeDtypeStruct(pack.shape, F32)], {0: 0}, [], swap)[0]

    def finish_big(self, after):
        (send_sems, recv_sems, bufs, _), swap, names = self.swapping
        return dict(zip(names, _exchange_wait("swap_wait", send_sems, recv_sems, bufs, swap, after)))


WEIGHTS = ("meta_tokens", "norm_mix_g", "w_in", "conv_w", "ssm_lam_re", "ssm_lam_im", "ssm_log_dt", "ssm_b_re",
           "ssm_b_im", "ssm_c_re", "ssm_c_im", "ssm_d", "ssm_w_glu", "gain_conv_out", "gain_ssm_out", "w_out",
           "norm_ffn_g", "w_up", "ffn_conv_w", "ffn_conv_b", "w_down", "norm_final_g")
TINY_SHARDED = ("meta_tokens", "conv_w", "ffn_conv_w")
REPLICATED = tuple(n for n in WEIGHTS if n not in BIG and n not in TINY_SHARDED)
PACK_COLS = 512


def _pack(arrays, row_mult, cols):
    flat = jnp.concatenate([a.reshape(-1).astype(F32) for a in arrays])
    n = flat.shape[0]
    total = -(-n // (row_mult * cols)) * (row_mult * cols)
    return jnp.concatenate([flat, jnp.zeros((total - n,), F32)]).reshape(total // cols, cols)


def _unpack(packed, shapes):
    flat = packed.reshape(-1)
    out, off = [], 0
    for s in shapes:
        n = math.prod(s)
        out.append(flat[off:off + n].reshape(s))
        off += n
    return out


def kernel(x, meta_tokens, norm_mix_g, w_in, conv_w, ssm_lam_re, ssm_lam_im, ssm_log_dt, ssm_b_re, ssm_b_im, ssm_c_re, ssm_c_im, ssm_d, ssm_w_glu, gain_conv_out, gain_ssm_out, w_out, norm_ffn_g, w_up, ffn_conv_w, ffn_conv_b, w_down, norm_final_g, loss_target, m_meta_tokens, m_norm_mix_g, m_w_in, m_conv_w, m_ssm_lam_re, m_ssm_lam_im, m_ssm_log_dt, m_ssm_b_re, m_ssm_b_im, m_ssm_c_re, m_ssm_c_im, m_ssm_d, m_ssm_w_glu, m_gain_conv_out, m_gain_ssm_out, m_w_out, m_norm_ffn_g, m_w_up, m_ffn_conv_w, m_ffn_conv_b, m_w_down, m_norm_final_g, v_meta_tokens, v_norm_mix_g, v_w_in, v_conv_w, v_ssm_lam_re, v_ssm_lam_im, v_ssm_log_dt, v_ssm_b_re, v_ssm_b_im, v_ssm_c_re, v_ssm_c_im, v_ssm_d, v_ssm_w_glu, v_gain_conv_out, v_gain_ssm_out, v_w_out, v_norm_ffn_g, v_w_up, v_ffn_conv_w, v_ffn_conv_b, v_w_down, v_norm_final_g):
    args = dict(locals())
    w = {n: args[n] for n in WEIGHTS}
    mom = {n: args["m_" + n] for n in WEIGHTS}
    var = {n: args["v_" + n] for n in WEIGHTS}
    kx, ky, kc_ = lax.axis_index("x"), lax.axis_index("y"), lax.axis_index("c")
    chip = 2 * kx + ky
    kc = jnp.stack([chip, kc_]).astype(jnp.int32)

    def squeeze(n, a):
        if n == "meta_tokens":
            return a
        if n == "norm_final_g":
            return a.reshape(1, -1)
        a = a[0]
        return a.reshape(1, -1) if a.ndim == 1 else a

    wl = {n: squeeze(n, w[n]) for n in WEIGHTS}
    ml = {n: squeeze(n, mom[n]) for n in WEIGHTS}
    vl = {n: squeeze(n, var[n]) for n in WEIGHTS}

    tiny = _pack([wl[n] for n in TINY_SHARDED], SUBLANES, LANES)
    ex = _Exchanges({n: wl[n] for n in BIG_NAMES}, tiny, kc)
    tiny_shapes = [wl[n].shape for n in TINY_SHARDED]
    tiny_parts = [_unpack(ex.tiny_all[k], tiny_shapes) for k in range(4)]
    p = {n: wl[n] for n in WEIGHTS if n not in BIG}
    for j, n in enumerate(TINY_SHARDED):
        p[n] = jnp.concatenate([tiny_parts[k][j] for k in range(4)], axis=1)
    p["ssm_log_dt"] = wl["ssm_log_dt"].reshape(-1)

    loss_local, grad_x, grads = _local_step(x[0], loss_target[0], p, ex)

    small_names = REPLICATED + TINY_SHARDED
    small_shapes = [tuple(grads[n].shape) for n in small_names] + [(1,)]
    pack = _pack([grads[n] for n in small_names] + [loss_local.reshape(1)], 2 * 16, PACK_COLS)
    g_pack = ex.finish_pack(pack)
    g_small = dict(zip(small_names + ("loss",), _unpack(g_pack, small_shapes)))
    loss = g_small["loss"][0]
    swapped = ("ssm_b_re", "ssm_b_im")

    def view(n, a):
        if n in swapped:
            return jnp.swapaxes(a, -1, -2)
        return a.reshape(1, -1) if a.ndim == 1 else a

    g = {}
    for n in REPLICATED:
        g[n] = g_small[n].reshape(view(n, w[n]).shape)
    for n in TINY_SHARDED:
        cols = wl[n].shape[1]
        g[n] = lax.dynamic_slice_in_dim(g_small[n], chip * cols, cols, axis=1).reshape(w[n].shape)
    delta, new_m, new_v = {}, {}, {}
    small = [[view(n, d[n]) for n in small_names] for d in (w, mom, var)]
    small.insert(1, [g[n] for n in small_names])
    for d, outs in zip((delta, new_m, new_v), _adamw_whole(*small, "adamw_small")):
        d.update(zip(small_names, outs))
    for d in (g, delta, new_m, new_v):
        d.update({n: jnp.swapaxes(d[n], -1, -2) for n in swapped})
    g_big = ex.finish_big(delta[small_names[0]])
    for n in BIG_NAMES:
        g[n], delta[n], new_m[n], new_v[n] = _adamw(wl[n], g_big[n], ml[n], vl[n], "adamw_" + n)

    def like(n, a):
        return a.reshape(w[n].shape)

    return (loss, grad_x[None], *[like(n, g[n]) for n in WEIGHTS], *[like(n, delta[n]) for n in WEIGHTS],
            *[like(n, new_m[n]) for n in WEIGHTS], *[like(n, new_v[n]) for n in WEIGHTS])
```

```python
import functools
import math

import jax
import jax.numpy as jnp
from jax import lax
from jax.experimental import pallas as pl
from jax.experimental.pallas import tpu as pltpu

F32 = jnp.float32
BF16 = jnp.bfloat16
MESH = pl.DeviceIdType.MESH

N_META = 16
N_GROUPS = 32
GROUP = 16
STATE = 64
RMS_EPS = 1e-6
ADAM_LR = 0.001
ADAM_B1 = 0.9
ADAM_B2 = 0.999
ADAM_EPS = 1e-08
ADAM_WD = 0.01
ADAM_STEP = 10

LANES = 128
SUBLANES = 8
ROW_ALIGN = 128
ROW_TILES = 4
VMEM_LIMIT = 52 * 1024 * 1024
MM_VMEM_BUDGET = 40 * 1024 * 1024
GELU_C = math.sqrt(2.0 / math.pi)
GELU_A = 0.044715


def _cparams(*sem):
    return pltpu.CompilerParams(dimension_semantics=sem, vmem_limit_bytes=VMEM_LIMIT)


def _pick_tile(dim, cap, mult):
    best = None
    for t in range(mult, min(dim, cap) + 1, mult):
        if dim % t == 0:
            best = t
    return best if best is not None else dim


def _mm(a, b, mode, name, out_dtype=F32, acc_in=None, after=None):
    if mode == "tn":
        kdim, m = a.shape
    else:
        m, kdim = a.shape
    n = b.shape[0] if mode == "nt" else b.shape[1]
    tm = _pick_tile(m, 1408, LANES if mode == "tn" else 16)
    tk = _pick_tile(kdim, 2816, LANES)
    nk = kdim // tk
    out_bytes = jnp.dtype(out_dtype).itemsize
    for cap in (1408, 1024, 512, 256, LANES):
        tn = _pick_tile(n, cap, LANES)
        blocks = 2 * (tm * tk * 2 + tk * tn * 2 + tm * tn * out_bytes * (2 if acc_in is not None else 1))
        if blocks + (tm * tn * 4 if nk > 1 else 0) <= MM_VMEM_BUDGET:
            break
    has_acc = acc_in is not None

    def body(*refs):
        if after is not None:
            refs = refs[1:]
        if has_acc:
            a_ref, b_ref, c_ref, o_ref = refs[:4]
            rest = refs[4:]
        else:
            a_ref, b_ref, o_ref = refs[:3]
            c_ref = None
            rest = refs[3:]
        if mode == "nn":
            p = jnp.dot(a_ref[...], b_ref[...], preferred_element_type=F32)
        elif mode == "nt":
            p = lax.dot_general(a_ref[...], b_ref[...], (((1,), (1,)), ((), ())), preferred_element_type=F32)
        else:
            p = lax.dot_general(a_ref[...], b_ref[...], (((0,), (0,)), ((), ())), preferred_element_type=F32)
        if nk == 1:
            if has_acc:
                p = p + c_ref[...]
            o_ref[...] = p.astype(out_dtype)
        else:
            acc_ref = rest[0]
            k = pl.program_id(2)

            @pl.when(k == 0)
            def _():
                acc_ref[...] = p + c_ref[...] if has_acc else p

            @pl.when(k > 0)
            def _():
                acc_ref[...] += p

            @pl.when(k == nk - 1)
            def _():
                o_ref[...] = acc_ref[...].astype(out_dtype)

    if mode == "tn":
        a_spec = pl.BlockSpec((tk, tm), lambda i, j, k: (k, i))
    else:
        a_spec = pl.BlockSpec((tm, tk), lambda i, j, k: (i, k))
    if mode == "nt":
        b_spec = pl.BlockSpec((tn, tk), lambda i, j, k: (j, k))
    else:
        b_spec = pl.BlockSpec((tk, tn), lambda i, j, k: (k, j))
    o_spec = pl.BlockSpec((tm, tn), lambda i, j, k: (i, j))
    in_specs = [a_spec, b_spec] + ([o_spec] if has_acc else [])
    args = (a, b) + ((acc_in,) if has_acc else ())
    if after is not None:
        in_specs = [pl.BlockSpec(memory_space=pl.ANY)] + in_specs
        args = (after,) + args
    return pl.pallas_call(
        body, name=name, grid=(m // tm, n // tn, nk),
        in_specs=in_specs, out_specs=o_spec,
        out_shape=jax.ShapeDtypeStruct((m, n), out_dtype),
        scratch_shapes=[pltpu.VMEM((tm, tn), F32)] if nk > 1 else [],
        compiler_params=_cparams("parallel", "parallel", "arbitrary"),
    )(*args)


def _mm_rows(a, b, mode, name, ins, outs, epilogue, scratch=()):
    m, kdim = a.shape
    n = b.shape[0] if mode == "nt" else b.shape[1]
    tm = m // ROW_TILES
    tk = _pick_tile(kdim, 2816, LANES)
    nk = kdim // tk
    ni, no = len(ins), len(outs)

    def body(*refs):
        a_ref, b_ref = refs[:2]
        in_refs, out_refs, rest = refs[2:2 + ni], refs[2 + ni:2 + ni + no], refs[2 + ni + no:]
        i = pl.program_id(0)
        if mode == "nn":
            p = jnp.dot(a_ref[...], b_ref[...], preferred_element_type=F32)
        else:
            p = lax.dot_general(a_ref[...], b_ref[...], (((1,), (1,)), ((), ())), preferred_element_type=F32)
        if nk == 1:
            epilogue(p, i, in_refs, out_refs, rest)
        else:
            acc_ref = rest[0]
            k = pl.program_id(1)

            @pl.when(k == 0)
            def _():
                acc_ref[...] = p

            @pl.when(k > 0)
            def _():
                acc_ref[...] += p

            @pl.when(k == nk - 1)
            def _():
                epilogue(acc_ref[...], i, in_refs, out_refs, rest[1:])

    def spec(shape, kind):
        if kind == "rows":
            return pl.BlockSpec((tm,) + tuple(shape[1:]), lambda i, k: (i,) + (0,) * (len(shape) - 1))
        if kind == "whole":
            return pl.BlockSpec(tuple(shape), lambda i, k: (0,) * len(shape))
        return pl.BlockSpec(memory_space=pl.ANY)

    a_spec = pl.BlockSpec((tm, tk), lambda i, k: (i, k))
    b_spec = pl.BlockSpec((n, tk), lambda i, k: (0, k)) if mode == "nt" else pl.BlockSpec((tk, n), lambda i, k: (k, 0))
    return pl.pallas_call(
        body, name=name, grid=(ROW_TILES, nk),
        in_specs=[a_spec, b_spec] + [spec(x.shape, kind) for x, kind in ins],
        out_specs=[spec(shape, kind) for shape, _, kind in outs],
        out_shape=[jax.ShapeDtypeStruct(shape, dtype) for shape, dtype, _ in outs],
        scratch_shapes=([pltpu.VMEM((tm, n), F32)] if nk > 1 else []) + list(scratch),
        compiler_params=_cparams("arbitrary", "arbitrary"),
    )(a, b, *[x for x, _ in ins])


def _rows(shape_cols, tr, dtype=None):
    return pl.BlockSpec((tr, shape_cols), lambda i: (i, 0))


def _const(shape):
    return pl.BlockSpec(shape, lambda i: (0,) * len(shape))


def _rms(x):
    return lax.rsqrt(jnp.mean(x * x, axis=-1, keepdims=True) + RMS_EPS)


def _rms_bwd(x, r, g, dy):
    xn = x * r
    dxn = dy * g
    dx = r * (dxn - xn * jnp.mean(dxn * xn, axis=-1, keepdims=True))
    return dx, dy * xn


def _gelu(y):
    return 0.5 * y * (1.0 + jnp.tanh(GELU_C * (y + GELU_A * y * y * y)))


def _gelu_grad(y):
    t = jnp.tanh(GELU_C * (y + GELU_A * y * y * y))
    return 0.5 * (1.0 + t) + 0.5 * y * (1.0 - t * t) * GELU_C * (1.0 + 3.0 * GELU_A * y * y)


def _sigmoid(z):
    return 1.0 / (1.0 + jnp.exp(-z))


def _proj_res_norm(a, w, h, g, name):
    def epilogue(p, i, ins, outs, _):
        x = ins[0][...] + p
        outs[0][...] = x
        outs[1][...] = (x * _rms(x) * ins[1][...]).astype(BF16)

    return _mm_rows(a, w, "nn", name, [(h, "rows"), (g, "whole")],
                    [(h.shape, F32, "rows"), (h.shape, BF16, "rows")], epilogue)


def _proj_norm_bwd(da, w, h, g, dres, after, name):
    d = h.shape[1]

    def epilogue(p, i, ins, outs, _):
        x = ins[0][...]
        dx, dgs = _rms_bwd(x, _rms(x), ins[1][...], p)
        dh = ins[2][...] + dx
        outs[0][...] = dh
        outs[1][...] = dh.astype(BF16)

        @pl.when(i == 0)
        def _():
            outs[2][...] = jnp.zeros_like(outs[2])

        outs[2][...] += jnp.sum(dgs, axis=0, keepdims=True)

    return _mm_rows(da, w, "nt", name, [(h, "rows"), (g, "whole"), (dres, "rows"), (after, "hbm")],
                    [(h.shape, F32, "rows"), (h.shape, BF16, "rows"), ((1, d), F32, "whole")], epilogue)


def _input_norm_bwd(h, g, dhn, dres, n_real, name):
    tp, d = h.shape
    tr = tp // ROW_TILES

    def body(h_ref, g_ref, dhn_ref, dres_ref, dx_ref, dmeta_ref, dg_ref, stage, sem):
        i = pl.program_id(0)
        x = h_ref[...]
        dx, dgs = _rms_bwd(x, _rms(x), g_ref[...], dhn_ref[...])
        stage[...] = dres_ref[...] + dx

        @pl.when(i == 0)
        def _():
            dg_ref[...] = jnp.zeros_like(dg_ref)
            dmeta_ref[...] = stage[:N_META, :]

        dg_ref[...] += jnp.sum(dgs, axis=0, keepdims=True)
        for t in range(ROW_TILES):
            lo, hi = max(t * tr, N_META), min((t + 1) * tr, n_real)
            if hi > lo:
                @pl.when(i == t)
                def _(t=t, lo=lo, hi=hi):
                    cp = pltpu.make_async_copy(stage.at[pl.ds(lo - t * tr, hi - lo), :],
                                               dx_ref.at[pl.ds(lo - N_META, hi - lo), :], sem)
                    cp.start()
                    cp.wait()

    return pl.pallas_call(
        body, name=name, grid=(ROW_TILES,),
        in_specs=[_rows(d, tr), _const((1, d)), _rows(d, tr), _rows(d, tr)],
        out_specs=[pl.BlockSpec(memory_space=pl.ANY), _const((N_META, d)), _const((1, d))],
        out_shape=[jax.ShapeDtypeStruct((n_real - N_META, d), F32), jax.ShapeDtypeStruct((N_META, d), F32),
                   jax.ShapeDtypeStruct((1, d), F32)],
        scratch_shapes=[pltpu.VMEM((tr, d), F32), pltpu.SemaphoreType.DMA],
        compiler_params=_cparams("arbitrary"))(h, g, dhn, dres)


def _load_token_rows(tok_hbm, buf, sem, tr, n_real, head=None, wait=False, i=None):
    i = pl.program_id(0) if i is None else i
    for t in range(ROW_TILES):
        base = t * tr
        lo, hi = max(base, N_META), min(base + tr, n_real)

        @pl.when(i == t)
        def _(base=base, lo=lo, hi=hi):
            if hi > lo:
                cp = pltpu.make_async_copy(tok_hbm.at[pl.ds(lo - N_META, hi - lo), :],
                                           buf.at[pl.ds(lo - base, hi - lo), :], sem)
                if wait:
                    cp.wait()
                    return
                cp.start()
            if wait:
                return
            if base < N_META:
                buf[0:N_META - base, :] = (jnp.zeros((N_META - base, buf.shape[1]), F32) if head is None
                                           else head[base:N_META, :])
            if hi < base + tr:
                buf[max(hi, base) - base:tr, :] = jnp.zeros((base + tr - max(hi, base), buf.shape[1]), F32)


def _input_norm_fwd(x, meta, g, tp, name):
    seq, d = x.shape
    tr = tp // ROW_TILES
    n_real = N_META + seq

    def body(x_hbm, meta_ref, g_ref, h_ref, hn_ref, buf, sem):
        _load_token_rows(x_hbm, buf, sem, tr, n_real, head=meta_ref)
        _load_token_rows(x_hbm, buf, sem, tr, n_real, wait=True)
        h = buf[...]
        h_ref[...] = h
        hn_ref[...] = (h * _rms(h) * g_ref[...]).astype(BF16)

    return pl.pallas_call(
        body, name=name, grid=(ROW_TILES,),
        in_specs=[pl.BlockSpec(memory_space=pl.ANY), _const((N_META, d)), _const((1, d))],
        out_specs=[_rows(d, tr), _rows(d, tr)],
        out_shape=[jax.ShapeDtypeStruct((tp, d), F32), jax.ShapeDtypeStruct((tp, d), BF16)],
        scratch_shapes=[pltpu.VMEM((tr, d), F32), pltpu.SemaphoreType.DMA],
        compiler_params=_cparams("arbitrary"))(x, meta, g)


def _proj_loss_bwd(act, w, h1, target, g, n_real, name):
    tp, d = h1.shape
    tr = tp // ROW_TILES

    def epilogue(p, i, ins, outs, scratch):
        h1_ref, t_hbm, g_ref = ins
        loss_ref, dh_ref, dhb_ref, dg_ref = outs
        t_buf, sem = scratch
        _load_token_rows(t_hbm, t_buf, sem, tr, n_real, i=i)
        x = h1_ref[...] + p
        r = _rms(x)
        row = i * tr + lax.broadcasted_iota(jnp.int32, (tr, d), 0)
        valid = (row >= N_META) & (row < n_real)
        _load_token_rows(t_hbm, t_buf, sem, tr, n_real, wait=True, i=i)
        e = jnp.where(valid, x * r * g_ref[...] - t_buf[...], 0.0)
        dx, dgs = _rms_bwd(x, r, g_ref[...], e * (1.0 / d))
        dh_ref[...] = dx
        dhb_ref[...] = dx.astype(BF16)

        @pl.when(i == 0)
        def _():
            dg_ref[...] = jnp.zeros_like(dg_ref)
            loss_ref[...] = jnp.zeros_like(loss_ref)

        dg_ref[...] += jnp.sum(dgs, axis=0, keepdims=True)
        loss_ref[...] += (0.5 / d) * jnp.sum(jnp.sum(e * e, axis=0, keepdims=True), axis=1, keepdims=True)

    return _mm_rows(act, w, "nn", name, [(h1, "rows"), (target, "hbm"), (g, "whole")],
                    [((1, LANES), F32, "whole"), ((tp, d), F32, "rows"), ((tp, d), BF16, "rows"),
                     ((1, d), F32, "whole")],
                    epilogue, scratch=[pltpu.VMEM((tr, d), F32), pltpu.SemaphoreType.DMA])


def _mix_fwd(co, y, z, gc, gs, name):
    tp, dh = co.shape
    tr = tp // ROW_TILES

    def body(co_ref, y_ref, z_ref, gc_ref, gs_ref, m_ref):
        c = co_ref[...]
        m_ref[:, :dh] = (c * _rms(c) * gc_ref[...]).astype(BF16)
        so = _gelu(y_ref[...]) * _sigmoid(z_ref[...])
        m_ref[:, dh:] = (so * _rms(so) * gs_ref[...]).astype(BF16)

    return pl.pallas_call(
        body, name=name, grid=(ROW_TILES,),
        in_specs=[_rows(dh, tr)] * 3 + [_const((1, dh))] * 2,
        out_specs=_rows(2 * dh, tr),
        out_shape=jax.ShapeDtypeStruct((tp, 2 * dh), BF16),
        compiler_params=_cparams("parallel"))(co, y, z, gc, gs)


def _proj_mix_bwd(dh1b, w, co, y, z, gc, gs, name):
    tp, dh = co.shape

    def epilogue(p, i, ins, outs, _):
        co_ref, y_ref, z_ref, gc_ref, gs_ref = ins
        dco_ref, dz_ref, dgp_ref, dgc_ref, dgs_ref = outs
        c = co_ref[...]
        dco, dgc = _rms_bwd(c, _rms(c), gc_ref[...], p[:, :dh])
        dco_ref[...] = dco
        gl = _gelu(y_ref[...])
        sg = _sigmoid(z_ref[...])
        so = gl * sg
        dso, dgs = _rms_bwd(so, _rms(so), gs_ref[...], p[:, dh:])
        dz_ref[...] = (dso * gl * sg * (1.0 - sg)).astype(BF16)
        dgp_ref[...] = dso * sg

        @pl.when(i == 0)
        def _():
            dgc_ref[...] = jnp.zeros_like(dgc_ref)
            dgs_ref[...] = jnp.zeros_like(dgs_ref)

        dgc_ref[...] += jnp.sum(dgc, axis=0, keepdims=True)
        dgs_ref[...] += jnp.sum(dgs, axis=0, keepdims=True)

    return _mm_rows(dh1b, w, "nt", name,
                    [(co, "rows"), (y, "rows"), (z, "rows"), (gc, "whole"), (gs, "whole")],
                    [((tp, dh), F32, "rows"), ((tp, dh), BF16, "rows"), ((tp, dh), F32, "rows"),
                     ((1, dh), F32, "whole"), ((1, dh), F32, "whole")], epilogue)


def _shift_down(x, k):
    row = lax.broadcasted_iota(jnp.int32, x.shape, 0)
    return jnp.where(row >= k, pltpu.roll(x, k, 0), 0.0)


def _shift_up(x, k):
    n = x.shape[0]
    row = lax.broadcasted_iota(jnp.int32, x.shape, 0)
    return jnp.where(row < n - k, pltpu.roll(x, n - k, 0), 0.0)


def _dwconv(x, w_ref):
    return w_ref[2:3, :] * x + w_ref[1:2, :] * _shift_down(x, 1) + w_ref[0:1, :] * _shift_down(x, 2)


def _dwconv_bwd(x, dy, w_ref):
    dx = w_ref[2:3, :] * dy + w_ref[1:2, :] * _shift_up(dy, 1) + w_ref[0:1, :] * _shift_up(dy, 2)
    dw = jnp.concatenate([jnp.sum(dy * _shift_down(x, 2), axis=0, keepdims=True),
                          jnp.sum(dy * _shift_down(x, 1), axis=0, keepdims=True),
                          jnp.sum(dy * x, axis=0, keepdims=True)], axis=0)
    return dx, dw


def _scan(s_re, s_im, tab_ref, reverse):
    n_chunks = s_re.shape[0] // SUBLANES
    n_strips = s_re.shape[1] // LANES
    last = 0 if reverse else SUBLANES - 1

    def step(chunk, carry):
        r0 = pl.multiple_of(chunk * SUBLANES, SUBLANES)
        out = []
        for st in range(n_strips):
            lanes = slice(st * LANES, (st + 1) * LANES)
            cr, ci = carry[2 * st], carry[2 * st + 1]
            xr = s_re[pl.ds(r0, SUBLANES), lanes]
            xi = s_im[pl.ds(r0, SUBLANES), lanes]
            for level, k in enumerate((1, 2, 4)):
                mr = tab_ref[2 * level, :, lanes]
                mi = tab_ref[2 * level + 1, :, lanes]
                sh = SUBLANES - k if reverse else k
                rr = pltpu.roll(xr, sh, 0)
                ri = pltpu.roll(xi, sh, 0)
                xr, xi = xr + (mr * rr - mi * ri), xi + (mr * ri + mi * rr)
            pwr = tab_ref[6, :, lanes]
            pwi = tab_ref[7, :, lanes]
            xr, xi = xr + (pwr * cr - pwi * ci), xi + (pwr * ci + pwi * cr)
            s_re[pl.ds(r0, SUBLANES), lanes] = xr
            s_im[pl.ds(r0, SUBLANES), lanes] = xi
            out.append(jnp.broadcast_to(xr[last:last + 1, :], (SUBLANES, LANES)))
            out.append(jnp.broadcast_to(xi[last:last + 1, :], (SUBLANES, LANES)))
        return tuple(out)

    def body(i, carry):
        for half in range(2):
            j = 2 * i + half
            carry = step((n_chunks - 1 - j) if reverse else j, carry)
        return carry

    zero = jnp.zeros((SUBLANES, LANES), F32)
    lax.fori_loop(0, n_chunks // 2, body, (zero,) * (2 * n_strips))


def _seq_fwd(proj, conv_w, bc_re, bc_im, cc_re, cc_im, dskip, tab_f, name):
    tp = proj.shape[0]
    dh = proj.shape[1] // 4
    nq = dh // LANES
    sw = STATE * N_GROUPS // nq

    def body(b_ref, c_ref, v_ref, u_ref, w_ref, bre_ref, bim_ref, cre_ref, cim_ref, d_ref, tab_ref,
             co_ref, y_ref, g_ref, s_re, s_im):
        co_ref[...] = b_ref[...] * _dwconv(c_ref[...] * v_ref[...], w_ref)
        u = u_ref[...]
        ub = u.astype(BF16)
        s_re[...] = jnp.dot(ub, bre_ref[...], preferred_element_type=F32)
        s_im[...] = jnp.dot(ub, bim_ref[...], preferred_element_type=F32)
        _scan(s_re, s_im, tab_ref, False)
        y = (jnp.dot(s_re[...].astype(BF16), cre_ref[...], preferred_element_type=F32)
             - jnp.dot(s_im[...].astype(BF16), cim_ref[...], preferred_element_type=F32)
             + d_ref[...] * u)
        y_ref[...] = y
        g_ref[...] = _gelu(y).astype(BF16)

    col = lambda off: pl.BlockSpec((tp, LANES), lambda q, off=off: (0, off * nq + q))
    blk = pl.BlockSpec((tp, LANES), lambda q: (0, q))
    return pl.pallas_call(
        body, name=name, grid=(nq,),
        in_specs=[col(0), col(1), col(2), col(3),
                  pl.BlockSpec((3, LANES), lambda q: (0, q)),
                  pl.BlockSpec((LANES, sw), lambda q: (0, q)), pl.BlockSpec((LANES, sw), lambda q: (0, q)),
                  pl.BlockSpec((sw, LANES), lambda q: (q, 0)), pl.BlockSpec((sw, LANES), lambda q: (q, 0)),
                  pl.BlockSpec((1, LANES), lambda q: (0, q)),
                  pl.BlockSpec((8, SUBLANES, sw), lambda q: (0, 0, q))],
        out_specs=[blk, blk, blk],
        out_shape=[jax.ShapeDtypeStruct((tp, dh), F32), jax.ShapeDtypeStruct((tp, dh), F32),
                   jax.ShapeDtypeStruct((tp, dh), BF16)],
        scratch_shapes=[pltpu.VMEM((tp, sw), F32), pltpu.VMEM((tp, sw), F32)],
        compiler_params=_cparams("parallel"),
    )(proj, proj, proj, proj, conv_w, bc_re, bc_im, cc_re, cc_im, dskip, tab_f)


def _conv_bwd(proj, dco, conv_w, name):
    tp = proj.shape[0]
    dh = proj.shape[1] // 4
    nq = dh // LANES

    def body(b_ref, c_ref, v_ref, dco_ref, w_ref, dproj_ref, dw_ref, stage, sem):
        q = pl.program_id(0)
        cg = c_ref[...]
        vg = v_ref[...]
        cv = cg * vg
        dco_v = dco_ref[...]
        dcv, dw = _dwconv_bwd(cv, dco_v * b_ref[...], w_ref)
        dw_ref[...] = dw
        stage[0] = (dco_v * _dwconv(cv, w_ref)).astype(BF16)
        stage[1] = (dcv * vg).astype(BF16)
        stage[2] = (dcv * cg).astype(BF16)
        copies = [pltpu.make_async_copy(stage.at[p], dproj_ref.at[:, pl.ds((p * nq + q) * LANES, LANES)], sem.at[p])
                  for p in range(3)]
        for cp in copies:
            cp.start()
        for cp in copies:
            cp.wait()

    col = lambda off: pl.BlockSpec((tp, LANES), lambda q, off=off: (0, off * nq + q))
    return pl.pallas_call(
        body, name=name, grid=(nq,),
        in_specs=[col(0), col(1), col(2), pl.BlockSpec((tp, LANES), lambda q: (0, q)),
                  pl.BlockSpec((3, LANES), lambda q: (0, q))],
        out_specs=[pl.BlockSpec(memory_space=pl.ANY), pl.BlockSpec((3, LANES), lambda q: (0, q))],
        out_shape=[jax.ShapeDtypeStruct((tp, 4 * dh), BF16), jax.ShapeDtypeStruct((3, dh), F32)],
        scratch_shapes=[pltpu.VMEM((3, tp, LANES), BF16), pltpu.SemaphoreType.DMA((3,))],
        compiler_params=_cparams("arbitrary"),
    )(proj, proj, proj, dco, conv_w)


def _ssm_bwd(proj, y, dg, dproj, bc_re, bc_im, cc_re, cc_im, dskip, tab_f, tab_r, name):
    tp = proj.shape[0]
    dh = proj.shape[1] // 4
    nq = dh // LANES
    sw = STATE * N_GROUPS // nq

    def body(u_ref, y_ref, dg_ref, dproj_in, bre_ref, bim_ref, cre_ref, cim_ref, d_ref, tabf_ref, tabr_ref,
             dproj_ref, dbre_ref, dbim_ref, dcre_ref, dcim_ref, dd_ref, dar_ref, dai_ref,
             s_re, s_im, l_re, l_im, stage, sem):
        del dproj_in
        q = pl.program_id(0)
        nt = (((1,), (1,)), ((), ()))
        tn = (((0,), (0,)), ((), ()))
        u = u_ref[...]
        ub = u.astype(BF16)
        s_re[...] = jnp.dot(ub, bre_ref[...], preferred_element_type=F32)
        s_im[...] = jnp.dot(ub, bim_ref[...], preferred_element_type=F32)
        _scan(s_re, s_im, tabf_ref, False)
        dy = dg_ref[...] * _gelu_grad(y_ref[...])
        dyb = dy.astype(BF16)
        dd_ref[...] = jnp.sum(dy * u, axis=0, keepdims=True)
        l_re[...] = lax.dot_general(dyb, cre_ref[...], nt, preferred_element_type=F32)
        l_im[...] = -lax.dot_general(dyb, cim_ref[...], nt, preferred_element_type=F32)
        dcre_ref[...] = lax.dot_general(s_re[...].astype(BF16), dyb, tn, preferred_element_type=F32)
        dcim_ref[...] = -lax.dot_general(s_im[...].astype(BF16), dyb, tn, preferred_element_type=F32)
        _scan(l_re, l_im, tabr_ref, True)
        for st in range(sw // LANES):
            lanes = slice(st * LANES, (st + 1) * LANES)
            lr = l_re[:, lanes]
            li = l_im[:, lanes]
            pr = _shift_down(s_re[:, lanes], 1)
            pi = _shift_down(s_im[:, lanes], 1)
            dar_ref[:, lanes] = jnp.sum(lr * pr + li * pi, axis=0, keepdims=True)
            dai_ref[:, lanes] = jnp.sum(li * pr - lr * pi, axis=0, keepdims=True)
        lrb = l_re[...].astype(BF16)
        lib = l_im[...].astype(BF16)
        du = (dy * d_ref[...] + lax.dot_general(lrb, bre_ref[...], nt, preferred_element_type=F32)
              + lax.dot_general(lib, bim_ref[...], nt, preferred_element_type=F32))
        stage[...] = du.astype(BF16)
        dbre_ref[...] = lax.dot_general(ub, lrb, tn, preferred_element_type=F32)
        dbim_ref[...] = lax.dot_general(ub, lib, tn, preferred_element_type=F32)
        cp = pltpu.make_async_copy(stage, dproj_ref.at[:, pl.ds((3 * nq + q) * LANES, LANES)], sem)
        cp.start()
        cp.wait()

    blk = pl.BlockSpec((tp, LANES), lambda q: (0, q))
    bspec = pl.BlockSpec((LANES, sw), lambda q: (0, q))
    cspec = pl.BlockSpec((sw, LANES), lambda q: (q, 0))
    tspec = pl.BlockSpec((8, SUBLANES, sw), lambda q: (0, 0, q))
    nstate = STATE * N_GROUPS
    return pl.pallas_call(
        body, name=name, grid=(nq,),
        in_specs=[pl.BlockSpec((tp, LANES), lambda q: (0, 3 * nq + q)), blk, blk, pl.BlockSpec(memory_space=pl.ANY),
                  bspec, bspec, cspec, cspec, pl.BlockSpec((1, LANES), lambda q: (0, q)), tspec, tspec],
        out_specs=[pl.BlockSpec(memory_space=pl.ANY), bspec, bspec, cspec, cspec,
                   pl.BlockSpec((1, LANES), lambda q: (0, q)),
                   pl.BlockSpec((1, sw), lambda q: (0, q)), pl.BlockSpec((1, sw), lambda q: (0, q))],
        out_shape=[jax.ShapeDtypeStruct((tp, 4 * dh), BF16),
                   jax.ShapeDtypeStruct((LANES, nstate), F32), jax.ShapeDtypeStruct((LANES, nstate), F32),
                   jax.ShapeDtypeStruct((nstate, LANES), F32), jax.ShapeDtypeStruct((nstate, LANES), F32),
                   jax.ShapeDtypeStruct((1, dh), F32),
                   jax.ShapeDtypeStruct((1, nstate), F32), jax.ShapeDtypeStruct((1, nstate), F32)],
        input_output_aliases={3: 0},
        scratch_shapes=[pltpu.VMEM((tp, sw), F32)] * 4 + [pltpu.VMEM((tp, LANES), BF16), pltpu.SemaphoreType.DMA],
        compiler_params=_cparams("arbitrary"),
    )(proj, y, dg, dproj, bc_re, bc_im, cc_re, cc_im, dskip, tab_f, tab_r)


FFN_TILE = 256


def _ffn_act(up, fw, fb, name):
    tp, two_ff = up.shape
    dff = two_ff // 2
    tc = FFN_TILE
    nj = dff // tc

    def body(ua_ref, uv_ref, wa_ref, wv_ref, ba_ref, bv_ref, act_ref):
        a = _dwconv(ua_ref[...], wa_ref) + ba_ref[...]
        v = _dwconv(uv_ref[...], wv_ref) + bv_ref[...]
        act_ref[...] = (a * _sigmoid(a) * v).astype(BF16)

    lo = lambda r: pl.BlockSpec((r, tc), lambda j: (0, j))
    hi = lambda r: pl.BlockSpec((r, tc), lambda j: (0, nj + j))
    return pl.pallas_call(
        body, name=name, grid=(nj,),
        in_specs=[lo(tp), hi(tp), lo(3), hi(3), lo(1), hi(1)],
        out_specs=lo(tp),
        out_shape=jax.ShapeDtypeStruct((tp, dff), BF16),
        compiler_params=_cparams("parallel"))(up, up, fw, fw, fb, fb)


def _ffn_bwd(up, dact, fw, fb, name):
    tp, two_ff = up.shape
    dff = two_ff // 2
    tc = FFN_TILE
    nj = dff // tc

    def body(ua_ref, uv_ref, da_ref, wa_ref, wv_ref, ba_ref, bv_ref,
             dup_ref, dwa_ref, dwv_ref, dba_ref, dbv_ref, stage, sem):
        j = pl.program_id(0)
        ua = ua_ref[...]
        uv = uv_ref[...]
        a = _dwconv(ua, wa_ref) + ba_ref[...]
        v = _dwconv(uv, wv_ref) + bv_ref[...]
        sg = _sigmoid(a)
        dact_v = da_ref[...]
        da = dact_v * v * sg * (1.0 + a * (1.0 - sg))
        dv = dact_v * a * sg
        dba_ref[...] = jnp.sum(da, axis=0, keepdims=True)
        dbv_ref[...] = jnp.sum(dv, axis=0, keepdims=True)
        dua, dwa = _dwconv_bwd(ua, da, wa_ref)
        duv, dwv = _dwconv_bwd(uv, dv, wv_ref)
        dwa_ref[...] = dwa
        dwv_ref[...] = dwv
        stage[0] = dua.astype(BF16)
        stage[1] = duv.astype(BF16)
        copies = [pltpu.make_async_copy(stage.at[p], dup_ref.at[:, pl.ds((p * nj + j) * tc, tc)], sem.at[p])
                  for p in range(2)]
        for cp in copies:
            cp.start()
        for cp in copies:
            cp.wait()

    lo = lambda r: pl.BlockSpec((r, tc), lambda j: (0, j))
    hi = lambda r: pl.BlockSpec((r, tc), lambda j: (0, nj + j))
    return pl.pallas_call(
        body, name=name, grid=(nj,),
        in_specs=[lo(tp), hi(tp), lo(tp), lo(3), hi(3), lo(1), hi(1)],
        out_specs=[pl.BlockSpec(memory_space=pl.ANY), lo(3), lo(3), lo(1), lo(1)],
        out_shape=[jax.ShapeDtypeStruct((tp, two_ff), BF16),
                   jax.ShapeDtypeStruct((3, dff), F32), jax.ShapeDtypeStruct((3, dff), F32),
                   jax.ShapeDtypeStruct((1, dff), F32), jax.ShapeDtypeStruct((1, dff), F32)],
        scratch_shapes=[pltpu.VMEM((2, tp, tc), BF16), pltpu.SemaphoreType.DMA((2,))],
        compiler_params=_cparams("arbitrary"))(up, up, dact, fw, fw, fb, fb)


def _zoh(lr, li, ld):
    dt = jnp.exp(ld)
    mag = jnp.exp(lr * dt)
    ang = li * dt
    ar = mag * jnp.cos(ang)
    ai = mag * jnp.sin(ang)
    den = lr * lr + li * li
    nr = ar - 1.0
    fr = (nr * lr + ai * li) / den
    fi = (ai * lr - nr * li) / den
    return dt, ar, ai, den, nr, fr, fi


def _s5_prep(lr, li, ld, b_re, b_im, name):
    nstate = lr.shape[1]

    def tables(tab_ref, ar, ai, reverse):
        pows = [(ar, ai)]
        for _ in range(SUBLANES - 1):
            pr, pi = pows[-1]
            pows.append((pr * ar - pi * ai, pr * ai + pi * ar))
        row = lax.broadcasted_iota(jnp.int32, (SUBLANES, nstate), 0)
        for level, k in enumerate((1, 2, 4)):
            mask = (row <= SUBLANES - 1 - k) if reverse else (row >= k)
            tab_ref[2 * level] = jnp.where(mask, pows[k - 1][0], 0.0)
            tab_ref[2 * level + 1] = jnp.where(mask, pows[k - 1][1], 0.0)
        pr = jnp.zeros((SUBLANES, nstate), F32)
        pi = jnp.zeros((SUBLANES, nstate), F32)
        for t in range(SUBLANES):
            k = SUBLANES - 1 - t if reverse else t
            pr = jnp.where(row == t, pows[k][0], pr)
            pi = jnp.where(row == t, pows[k][1], pi)
        tab_ref[6] = pr
        tab_ref[7] = pi

    def body(lr_ref, li_ref, ld_ref, bre_ref, bim_ref, tabf_ref, tabr_ref, bcre_ref, bcim_ref):
        _, ar, ai, _, _, fr, fi = _zoh(lr_ref[...], li_ref[...], ld_ref[...])
        tables(tabf_ref, ar, ai, False)
        tables(tabr_ref, ar, -ai, True)
        bre = bre_ref[...]
        bim = bim_ref[...]
        bcre_ref[...] = (fr * bre - fi * bim).astype(BF16)
        bcim_ref[...] = (fr * bim + fi * bre).astype(BF16)

    vmem = pl.BlockSpec(memory_space=pltpu.VMEM)
    return pl.pallas_call(
        body, name=name, in_specs=[vmem] * 5, out_specs=[vmem] * 4,
        out_shape=[jax.ShapeDtypeStruct((8, SUBLANES, nstate), F32)] * 2
        + [jax.ShapeDtypeStruct(b_re.shape, BF16)] * 2)(lr, li, ld, b_re, b_im)


def _s5_prep_bwd(lr, li, ld, b_re, b_im, da_re, da_im, dbc_re, dbc_im, name):
    def body(lr_ref, li_ref, ld_ref, bre_ref, bim_ref, dar_ref, dai_ref, dbcre_ref, dbcim_ref,
             dlr_ref, dli_ref, dld_ref, dbre_ref, dbim_ref):
        lr, li = lr_ref[...], li_ref[...]
        dt, ar, ai, den, nr, fr, fi = _zoh(lr, li, ld_ref[...])
        bre, bim = bre_ref[...], bim_ref[...]
        gre, gim = dbcre_ref[...], dbcim_ref[...]
        dbre_ref[...] = fr * gre + fi * gim
        dbim_ref[...] = fr * gim - fi * gre
        g_fr = jnp.sum(gre * bre + gim * bim, axis=0, keepdims=True)
        g_fi = jnp.sum(gim * bre - gre * bim, axis=0, keepdims=True)
        g_ar = dar_ref[...] + (g_fr * lr - g_fi * li) / den
        g_ai = dai_ref[...] + (g_fr * li + g_fi * lr) / den
        d_lr = (g_fr * (nr - 2.0 * fr * lr) + g_fi * (ai - 2.0 * fi * lr)) / den
        d_li = (g_fr * (ai - 2.0 * fr * li) - g_fi * (nr + 2.0 * fi * li)) / den
        g_logmag = g_ar * ar + g_ai * ai
        g_ang = g_ai * ar - g_ar * ai
        dlr_ref[...] = d_lr + g_logmag * dt
        dli_ref[...] = d_li + g_ang * dt
        d_ld = (g_logmag * lr + g_ang * li) * dt
        n = d_ld.shape[1]
        sh = 1
        while sh < STATE:
            d_ld = d_ld + pltpu.roll(d_ld, n - sh, 1)
            sh *= 2
        dld_ref[...] = d_ld

    vmem = pl.BlockSpec(memory_space=pltpu.VMEM)
    row = jax.ShapeDtypeStruct(lr.shape, F32)
    return pl.pallas_call(
        body, name=name, in_specs=[vmem] * 9, out_specs=[vmem] * 5,
        out_shape=[row, row, row, jax.ShapeDtypeStruct(b_re.shape, F32), jax.ShapeDtypeStruct(b_re.shape, F32)],
    )(lr, li, ld, b_re, b_im, da_re, da_im, dbc_re, dbc_im)


def _compact_b(bb):
    bq = bb.reshape(N_GROUPS // 8, 8, STATE, GROUP)
    m = jnp.einsum("ab,qbph->qahbp", jnp.eye(8, dtype=bb.dtype), bq).reshape(N_GROUPS // 8, LANES, 8 * STATE)
    return m.transpose(1, 0, 2).reshape(LANES, N_GROUPS * STATE)


def _expand_b(m):
    d = m.reshape(8, GROUP, N_GROUPS // 8, 8, STATE)
    return jnp.einsum("ahqap->qahp", d).reshape(N_GROUPS, GROUP, STATE)


def _compact_c(c):
    cq = c.reshape(N_GROUPS // 8, 8, GROUP, STATE)
    return jnp.einsum("ab,qbhp->qbpah", jnp.eye(8, dtype=c.dtype), cq).reshape(N_GROUPS * STATE, LANES)


def _expand_c(m):
    d = m.reshape(N_GROUPS // 8, 8, STATE, 8, GROUP)
    return jnp.einsum("qbpbh->qbhp", d).reshape(N_GROUPS, GROUP, STATE)


def _local_step(x, target, p, ex):
    seq, d = x.shape
    n_real = N_META + seq
    tp = -(-n_real // ROW_ALIGN) * ROW_ALIGN

    nstate = N_GROUPS * STATE
    s5 = (p["ssm_lam_re"].reshape(1, nstate), p["ssm_lam_im"].reshape(1, nstate),
          jnp.repeat(p["ssm_log_dt"].reshape(-1), STATE).reshape(1, nstate),
          _compact_b(p["ssm_b_re"]), _compact_b(p["ssm_b_im"]))
    tab_f, tab_r, bc_re, bc_im = _s5_prep(*s5, "s5_prep")
    cc_re = _compact_c(p["ssm_c_re"]).astype(BF16)
    cc_im = _compact_c(p["ssm_c_im"]).astype(BF16)
    dskip = p["ssm_d"].reshape(1, -1)
    dh = dskip.shape[1]

    h0, hn1 = _input_norm_fwd(x, p["meta_tokens"], p["norm_mix_g"] + ex.zero, tp, "norm_mix")
    first = ex.weights("first", hn1)
    proj = _mm(hn1, first["w_in"], "nn", "proj")
    started = ex.forward("mid", proj)
    co, y, g = _seq_fwd(proj, p["conv_w"] + started[0, 0], bc_re, bc_im, cc_re, cc_im, dskip, tab_f, "seq_fwd")
    mid = ex.weights("mid", g)
    started = ex.forward("late", g)
    z = _mm(g, mid["ssm_w_glu"], "nn", "glu", after=started)
    mixed = _mix_fwd(co, y, z, p["gain_conv_out"], p["gain_ssm_out"], "mix_fwd")
    h1, hn2 = _proj_res_norm(mixed, mid["w_out"], h0, p["norm_ffn_g"], "out_proj_norm")
    late = ex.weights("late", hn2)
    up = _mm(hn2, late["w_up"], "nn", "up_proj")
    act = _ffn_act(up, p["ffn_conv_w"], p["ffn_conv_b"], "ffn_act")
    loss, dh2, dh2b, d_gfin = _proj_loss_bwd(act, late["w_down"], h1, target, p["norm_final_g"], n_real,
                                             "down_proj_loss")

    g_w_down = _mm(act, dh2b, "tn", "g_w_down")
    dact = _mm(dh2b, late["w_down"], "nt", "d_act")
    dup, dfw_a, dfw_v, dfb_a, dfb_v = _ffn_bwd(up, dact, p["ffn_conv_w"], p["ffn_conv_b"], "ffn_bwd")
    g_w_up = _mm(hn2, dup, "tn", "g_w_up")
    started = ex.grads_ready("late", {"w_up": g_w_up, "w_down": g_w_down})
    dh1, dh1b, d_gffn = _proj_norm_bwd(dup, late["w_up"], h1, p["norm_ffn_g"], dh2, started, "d_hn2_norm_bwd")
    started = ex.grads_send("late", dh1)
    g_w_out = _mm(mixed, dh1b, "tn", "g_w_out", after=started)
    dco, dz, dgp, d_gc, d_gs = _proj_mix_bwd(dh1b, mid["w_out"], co, y, z, p["gain_conv_out"],
                                             p["gain_ssm_out"], "d_mixed_mix_bwd")
    g_w_glu = _mm(g, dz, "tn", "g_w_glu")
    started = ex.grads_ready("mid", {"ssm_w_glu": g_w_glu, "w_out": g_w_out})
    dg = _mm(dz, mid["ssm_w_glu"], "nt", "d_gelu", acc_in=dgp, after=started)
    started = ex.grads_send("mid", dg)
    dproj, d_conv_w = _conv_bwd(proj, dco, p["conv_w"] + started[0, 0], "conv_bwd")
    (dproj, dbc_re, dbc_im, dcc_re, dcc_im, d_dskip, da_re, da_im) = _ssm_bwd(
        proj, y, dg, dproj, bc_re, bc_im, cc_re, cc_im, dskip, tab_f, tab_r, "ssm_bwd")
    g_w_in = _mm(hn1, dproj, "tn", "g_w_in")
    started = ex.grads_ready("first", {"w_in": g_w_in})
    dhn1 = _mm(dproj, first["w_in"], "nt", "d_hn1", after=started)
    started = ex.grads_send("first", dhn1)
    grad_x, d_meta, d_gmix = _input_norm_bwd(h0, p["norm_mix_g"] + started[0, 0], dhn1, dh1, n_real, "norm_mix_bwd")

    d_lam_re, d_lam_im, d_log_dt, d_b_re, d_b_im = _s5_prep_bwd(*s5, da_re, da_im, dbc_re, dbc_im, "s5_prep_bwd")
    d_lam_re, d_lam_im = d_lam_re.reshape(N_GROUPS, STATE), d_lam_im.reshape(N_GROUPS, STATE)
    d_log_dt = d_log_dt[0, ::STATE]
    d_b_re, d_b_im = _expand_b(d_b_re), _expand_b(d_b_im)
    grads = {
        "meta_tokens": d_meta, "norm_mix_g": d_gmix, "w_in": g_w_in, "conv_w": d_conv_w,
        "ssm_lam_re": d_lam_re, "ssm_lam_im": d_lam_im, "ssm_log_dt": d_log_dt,
        "ssm_b_re": d_b_re, "ssm_b_im": d_b_im, "ssm_c_re": _expand_c(dcc_re), "ssm_c_im": _expand_c(dcc_im),
        "ssm_d": d_dskip.reshape(N_GROUPS, GROUP), "ssm_w_glu": g_w_glu,
        "gain_conv_out": d_gc, "gain_ssm_out": d_gs, "w_out": g_w_out, "norm_ffn_g": d_gffn,
        "w_up": g_w_up, "ffn_conv_w": jnp.concatenate([dfw_a, dfw_v], axis=1),
        "ffn_conv_b": jnp.concatenate([dfb_a, dfb_v], axis=1), "w_down": g_w_down, "norm_final_g": d_gfin,
    }
    return loss[0, 0], grad_x, grads


def _view(ref, axis, start, size):
    idx = [slice(None)] * len(ref.shape)
    idx[axis] = pl.ds(start, size)
    return ref.at[tuple(idx)]


def _exchange(name, ins, outs, aliases, local_copies, remote_copies):
    ni, no = len(ins), len(outs)
    nl, nr = len(local_copies), len(remote_copies)

    def body(*refs):
        in_refs, out_refs = refs[:ni], refs[ni:ni + no]
        send_sems, recv_sems, local_sems = refs[ni + no:]
        x, y, c = lax.axis_index("x"), lax.axis_index("y"), lax.axis_index("c")
        pos = (x, y, c, 2 * x + y)
        locals_ = [pltpu.make_async_copy(s(in_refs, out_refs, pos), d(in_refs, out_refs, pos), local_sems.at[i])
                   for i, (s, d) in enumerate(local_copies)]
        remotes = []
        for i, (s, d, flip) in enumerate(remote_copies):
            peer = (1 - x if "x" in flip else x, 1 - y if "y" in flip else y, 1 - c if "c" in flip else c)
            remotes.append(pltpu.make_async_remote_copy(
                src_ref=s(in_refs, out_refs, pos), dst_ref=d(in_refs, out_refs, pos),
                send_sem=send_sems.at[i], recv_sem=recv_sems.at[i], device_id=peer, device_id_type=MESH))
        for cp in locals_ + remotes:
            cp.start()
        for cp in remotes:
            cp.wait_recv()
        for cp in remotes:
            cp.wait_send()
        for cp in locals_:
            cp.wait()

    hbm = pl.BlockSpec(memory_space=pl.ANY)
    return pl.pallas_call(
        body, name=name, in_specs=[hbm] * ni, out_specs=[hbm] * no, out_shape=outs,
        input_output_aliases=aliases,
        scratch_shapes=[pltpu.SemaphoreType.DMA((nr,)), pltpu.SemaphoreType.DMA((nr,)),
                        pltpu.SemaphoreType.DMA((max(nl, 1),))],
    )(*ins)


BIG = {"w_in": (0, 1), "ssm_w_glu": (1, 0), "w_out": (1, 0), "w_up": (0, 1), "w_down": (1, 0)}
BIG_NAMES = tuple(BIG)
FLIPS = ("y", "x", "xy")


def _peer_chip(pos, flip):
    x, y, _, _ = pos
    return 2 * (1 - x if "x" in flip else x) + (1 - y if "y" in flip else y)


def _block_rows(rows, cols, itemsize, mult):
    return _pick_tile(rows, max(mult, (2 * 1024 * 1024) // (cols * itemsize)), mult)


def _cast_into_full(w, kc, shard_axis, name):
    r, cdim = w.shape
    tr = _block_rows(r, cdim, 4, 16)
    nb = r // tr

    def body(kc_ref, w_ref, o_ref):
        o_ref[...] = w_ref[...].astype(BF16)

    if shard_axis == 1:
        full, o_spec = (r, 4 * cdim), pl.BlockSpec((tr, cdim), lambda i, kc: (i, kc[0]))
    else:
        full, o_spec = (4 * r, cdim), pl.BlockSpec((tr, cdim), lambda i, kc: (kc[0] * nb + i, 0))
    return pl.pallas_call(
        body, name=name,
        grid_spec=pltpu.PrefetchScalarGridSpec(
            num_scalar_prefetch=1, grid=(nb,), in_specs=[pl.BlockSpec((tr, cdim), lambda i, kc: (i, 0))],
            out_specs=o_spec),
        out_shape=jax.ShapeDtypeStruct(full, BF16), compiler_params=_cparams("parallel"))(kc, w)


def _pair_sum(g, recv, kc, half_axis, name, out_dtype):
    hr, hc = recv.shape
    tr = _block_rows(hr, hc, 4, 16)
    nb = hr // tr

    def body(kc_ref, g_ref, r_ref, o_ref):
        o_ref[...] = (g_ref[...] + r_ref[...]).astype(out_dtype)

    if half_axis == 0:
        g_spec = pl.BlockSpec((tr, hc), lambda i, kc: (kc[1] * nb + i, 0))
    elif half_axis == 1:
        g_spec = pl.BlockSpec((tr, hc), lambda i, kc: (i, kc[1]))
    else:
        g_spec = pl.BlockSpec((tr, hc), lambda i, kc: (i, 0))
    same = pl.BlockSpec((tr, hc), lambda i, kc: (i, 0))
    return pl.pallas_call(
        body, name=name,
        grid_spec=pltpu.PrefetchScalarGridSpec(num_scalar_prefetch=1, grid=(nb,), in_specs=[g_spec, same],
                                               out_specs=same),
        out_shape=jax.ShapeDtypeStruct((hr, hc), out_dtype), compiler_params=_cparams("parallel"))(kc, g, recv)


def _chip_sum(own, recv, kc, own_axis, out_axis, name):
    _, sr, sc = recv.shape
    tr = _block_rows(sr, sc, 4, 16)
    nb = sr // tr

    def body(kc_ref, o_ref, r_ref, t_ref):
        k = kc_ref[0]
        own_v = o_ref[...].astype(F32)
        r = [r_ref[m].astype(F32) for m in range(3)]
        terms = []
        for kk in range(4):
            m = jnp.bitwise_xor(k, kk)
            terms.append(jnp.where(m == 0, own_v, jnp.where(m == 1, r[0], jnp.where(m == 2, r[1], r[2]))))
        t_ref[...] = (terms[0] + terms[1]) + (terms[2] + terms[3])

    if own_axis == 0:
        own_spec = pl.BlockSpec((tr, sc), lambda i, kc: (kc[0] * nb + i, 0))
    elif own_axis == 1:
        own_spec = pl.BlockSpec((tr, sc), lambda i, kc: (i, kc[0]))
    else:
        own_spec = pl.BlockSpec((tr, sc), lambda i, kc: (kc[1] * nb + i, 0))
    if out_axis == 0:
        out_full, out_spec = (2 * sr, sc), pl.BlockSpec((tr, sc), lambda i, kc: (kc[1] * nb + i, 0))
    else:
        out_full, out_spec = (sr, 2 * sc), pl.BlockSpec((tr, sc), lambda i, kc: (i, kc[1]))
    return pl.pallas_call(
        body, name=name,
        grid_spec=pltpu.PrefetchScalarGridSpec(
            num_scalar_prefetch=1, grid=(nb,),
            in_specs=[own_spec, pl.BlockSpec((3, tr, sc), lambda i, kc: (0, i, 0))],
            out_specs=out_spec),
        out_shape=jax.ShapeDtypeStruct(out_full, F32), compiler_params=_cparams("parallel"))(kc, own, recv)


def _adamw(w, g, m, v, name):
    r, cdim = w.shape
    tr = _block_rows(r, cdim, 4, 8)
    c1 = 1.0 - ADAM_B1 ** ADAM_STEP
    c2 = 1.0 - ADAM_B2 ** ADAM_STEP

    def body(w_ref, g_ref, m_ref, v_ref, go_ref, d_ref, nm_ref, nv_ref):
        gv = g_ref[...]
        go_ref[...] = gv
        nm = ADAM_B1 * m_ref[...] + (1.0 - ADAM_B1) * gv
        nv = ADAM_B2 * v_ref[...] + (1.0 - ADAM_B2) * (gv * gv)
        d_ref[...] = -ADAM_LR * ((nm / c1) / (jnp.sqrt(nv / c2) + ADAM_EPS) + ADAM_WD * w_ref[...])
        nm_ref[...] = nm
        nv_ref[...] = nv

    spec = _rows(cdim, tr)
    return pl.pallas_call(body, name=name, grid=(r // tr,), in_specs=[spec] * 4, out_specs=[spec] * 4,
                          out_shape=[jax.ShapeDtypeStruct((r, cdim), F32)] * 4,
                          compiler_params=_cparams("parallel"))(w, g, m, v)


def _adamw_whole(ws, gs, ms, vs, name):
    n = len(ws)
    c1 = 1.0 - ADAM_B1 ** ADAM_STEP
    c2 = 1.0 - ADAM_B2 ** ADAM_STEP

    def body(*refs):
        for i in range(n):
            w_ref, g_ref, m_ref, v_ref, d_ref, nm_ref, nv_ref = [refs[j * n + i] for j in range(7)]
            gv = g_ref[...]
            nm = ADAM_B1 * m_ref[...] + (1.0 - ADAM_B1) * gv
            nv = ADAM_B2 * v_ref[...] + (1.0 - ADAM_B2) * (gv * gv)
            d_ref[...] = -ADAM_LR * ((nm / c1) / (jnp.sqrt(nv / c2) + ADAM_EPS) + ADAM_WD * w_ref[...])
            nm_ref[...] = nm
            nv_ref[...] = nv

    vmem = pl.BlockSpec(memory_space=pltpu.VMEM)
    out = pl.pallas_call(body, name=name, in_specs=[vmem] * (4 * n), out_specs=[vmem] * (3 * n),
                         out_shape=[jax.ShapeDtypeStruct(a.shape, F32) for a in ws] * 3,
                         compiler_params=pltpu.CompilerParams(vmem_limit_bytes=VMEM_LIMIT))(*ws, *gs, *ms, *vs)
    return out[:n], out[n:2 * n], out[2 * n:]


SIDE_EFFECT = pltpu.SideEffectType.DATAFLOW_SIDE_EFFECTING


def _descriptors(copies, refs, send_sems, recv_sems, sem_off=0):
    x, y, c = lax.axis_index("x"), lax.axis_index("y"), lax.axis_index("c")
    pos = (x, y, c, 2 * x + y)
    out = []
    for i, (s, d, flip) in enumerate(copies):
        peer = (1 - x if "x" in flip else x, 1 - y if "y" in flip else y, 1 - c if "c" in flip else c)
        out.append(pltpu.make_async_remote_copy(
            src_ref=s(refs, refs, pos), dst_ref=d(refs, refs, pos),
            send_sem=send_sems.at[sem_off + i], recv_sem=recv_sems.at[sem_off + i],
            device_id=peer, device_id_type=MESH))
    return out


def _shifted(copies, off):
    return [(lambda I, O, pos, s=s: s(I[off:], O[off:], pos), lambda I, O, pos, d=d: d(I[off:], O[off:], pos), flip)
            for s, d, flip in copies]


def _exchange_start(name, bufs, copies, after=None):
    n, nr = len(bufs), len(copies)
    na = 0 if after is None else 1

    def body(*refs):
        for cp in _descriptors(copies, refs[:n], refs[n + na], refs[n + na + 1]):
            cp.start()
        token = refs[2 * n + na + 2]
        token[...] = jnp.zeros_like(token)

    hbm = pl.BlockSpec(memory_space=pltpu.HBM)
    sem = pl.BlockSpec(memory_space=pltpu.SEMAPHORE)
    out = pl.pallas_call(
        body, name=name,
        in_specs=[hbm] * n + [pl.BlockSpec(memory_space=pl.ANY)] * na,
        out_specs=(sem, sem, *[hbm] * n, pl.BlockSpec(memory_space=pltpu.VMEM)),
        out_shape=(pltpu.SemaphoreType.DMA((nr,)), pltpu.SemaphoreType.DMA((nr,)),
                   *[pltpu.HBM(b.shape, b.dtype) for b in bufs], jax.ShapeDtypeStruct((SUBLANES, LANES), F32)),
        input_output_aliases={i: 2 + i for i in range(n)},
        compiler_params=pltpu.CompilerParams(has_side_effects=SIDE_EFFECT),
    )(*[pltpu.with_memory_space_constraint(b, pltpu.HBM) for b in bufs], *([after] * na))
    return out[0], out[1], list(out[2:2 + n]), out[2 + n]


def _exchange_wait(name, send_sems, recv_sems, bufs, copies, after, sem_off=0):
    n = len(bufs)

    def body(*refs):
        for cp in _descriptors(copies, refs[:n], refs[n], refs[n + 1], sem_off):
            cp.wait_send()
            cp.wait_recv()

    hbm = pl.BlockSpec(memory_space=pltpu.HBM)
    sem = pl.BlockSpec(memory_space=pltpu.SEMAPHORE)
    out = pl.pallas_call(
        body, name=name,
        in_specs=[hbm] * n + [sem, sem, pl.BlockSpec(memory_space=pl.ANY)],
        out_specs=tuple([hbm] * n),
        out_shape=tuple(pltpu.HBM(b.shape, b.dtype) for b in bufs),
        input_output_aliases={i: i for i in range(n)},
        compiler_params=pltpu.CompilerParams(has_side_effects=SIDE_EFFECT),
    )(*bufs, send_sems, recv_sems, after)
    return list(out)


FIRST = ("w_in",)
MID = ("ssm_w_glu", "w_out")
LATE = ("w_up", "w_down")
GROUPS = {"first": FIRST, "mid": MID, "late": LATE}


def _gather_copies(names, shard_shapes):
    def region(i, chip, c):
        half_axis, shard_axis = BIG[names[i]]
        ssize = shard_shapes[i][shard_axis]
        hsize = shard_shapes[i][half_axis] // 2
        return lambda ref: _view(_view(ref, shard_axis, chip * ssize, ssize), half_axis, c * hsize, hsize)

    ici, d2d = [], []
    for i in range(len(names)):
        for flip in FLIPS:
            ici.append((lambda I, O, pos, i=i: region(i, pos[3], pos[2])(I[i]),
                        lambda I, O, pos, i=i: region(i, pos[3], pos[2])(O[i]), flip))
            d2d.append((lambda I, O, pos, i=i, flip=flip: region(i, _peer_chip(pos, flip), pos[2])(I[i]),
                        lambda I, O, pos, i=i, flip=flip: region(i, _peer_chip(pos, flip), pos[2])(O[i]), "c"))
    return ici, d2d


def _half_shape(n, shape):
    r, cdim = shape
    return (r // 2, cdim) if BIG[n][0] == 0 else (r, cdim // 2)


def _sub_shape(n, shape):
    hr, hc = _half_shape(n, shape)
    return (hr, hc // 4) if BIG[n][1] == 1 else (hr // 4, hc)


def _pair_copies(names, shapes, with_pack, dst_off):
    n = len(names)

    def other_half(i, ref, pos):
        half_axis = BIG[names[i]][0]
        hsize = shapes[i][half_axis] // 2
        return _view(ref, half_axis, (1 - pos[2]) * hsize, hsize)

    copies = [(lambda I, O, pos, i=i: other_half(i, I[i], pos), lambda I, O, pos, i=i: O[dst_off + i], "c")
              for i in range(n)]
    if with_pack:
        copies.append((lambda I, O, pos: I[n], lambda I, O, pos: O[dst_off + n], "c"))
    return copies


def _chip_copies(names, shapes, pack_rows, dst_off):
    n = len(names)

    def piece(i, ref, chip):
        shard_axis = BIG[names[i]][1]
        ssize = _sub_shape(names[i], shapes[i])[shard_axis]
        return _view(ref, shard_axis, chip * ssize, ssize)

    copies = []
    for i in range(n):
        for slot, flip in enumerate(FLIPS):
            copies.append((lambda I, O, pos, i=i, flip=flip: piece(i, I[i], _peer_chip(pos, flip)),
                           lambda I, O, pos, i=i, slot=slot: O[dst_off + i].at[slot], flip))
    if pack_rows:
        for slot, flip in enumerate(FLIPS):
            copies.append((lambda I, O, pos: _view(I[n], 0, pos[2] * (pack_rows // 2), pack_rows // 2),
                           lambda I, O, pos, slot=slot: O[dst_off + n].at[slot], flip))
    return copies


class _Exchanges:
    def __init__(self, shards, tiny, kc):
        self.kc = kc
        wb = {n: _cast_into_full(shards[n], kc, BIG[n][1], "cast_" + n) for n in BIG_NAMES}
        self.gathering, self.forwarding, self.pairing, self.reducing = {}, {}, {}, {}
        tiny_copies = [(lambda I, O, pos: I[0], lambda I, O, pos: O[1].at[pos[3]], flip) for flip in FLIPS]
        self.gathering["tiny"] = (0, 0, 2, tiny_copies, None)
        bufs, copies = [tiny, lax.empty((4,) + tiny.shape, F32)], list(tiny_copies)
        for group, names in GROUPS.items():
            ici, d2d = _gather_copies(names, [shards[n].shape for n in names])
            self.gathering[group] = (len(bufs), len(copies), len(names), ici, d2d)
            copies += _shifted(ici, len(bufs))
            bufs += [wb[n] for n in names]
        self.started = _exchange_start("gather_start", bufs, copies)
        self.zero = self.started[3][0, 0]

    def _arrived(self, group, after):
        buf_off, sem_off, n, ici, _ = self.gathering[group]
        send_sems, recv_sems, bufs, _ = self.started
        return _exchange_wait("gather_%s_wait" % group, send_sems, recv_sems, bufs[buf_off:buf_off + n], ici, after,
                              sem_off)

    def small_params(self, kc):
        tiny, got = self._arrived("tiny", self.started[3])
        return lax.dynamic_update_index_in_dim(got, tiny, kc[0], 0)

    def forward(self, group, after):
        d2d = self.gathering[group][4]
        self.forwarding[group] = (_exchange_start("forward_%s_start" % group, self._arrived(group, after), d2d), d2d)
        return self.forwarding[group][0][3]

    def weights(self, group, after):
        if group not in self.forwarding:
            after = self.forward(group, after)
        (send_sems, recv_sems, bufs, _), d2d = self.forwarding[group]
        full = _exchange_wait("forward_%s_wait" % group, send_sems, recv_sems, bufs, d2d, after)
        return dict(zip(GROUPS[group], full))

    def grads_ready(self, group, grads):
        names = GROUPS[group]
        gs = [grads[n] for n in names]
        land = [lax.empty(_half_shape(n, g.shape), F32) for n, g in zip(names, gs)]
        copies = _pair_copies(names, [g.shape for g in gs], False, len(names))
        started = _exchange_start("pair_%s_start" % group, gs + land, copies)
        self.pairing[group] = (started, copies)
        return started[3]

    def grads_send(self, group, after):
        names = GROUPS[group]
        n = len(names)
        (send_sems, recv_sems, bufs, _), copies = self.pairing[group]
        bufs = _exchange_wait("pair_%s_wait" % group, send_sems, recv_sems, bufs, copies, after)
        chip = [_pair_sum(bufs[i], bufs[n + i], self.kc, BIG[names[i]][0], "pair_sum_" + names[i], BF16)
                for i in range(n)]
        shapes = [bufs[i].shape for i in range(n)]
        land = [lax.empty((3,) + _sub_shape(names[i], shapes[i]), BF16) for i in range(n)]
        copies = _chip_copies(names, shapes, 0, n)
        started = _exchange_start("reduce_%s_start" % group, chip + land, copies)
        self.reducing[group] = (started, copies)
        return started[3]

    def finish_pack(self, pack):
        kc = self.kc
        prow = pack.shape[0] // 2
        recv = _exchange("reduce_d2d", [pack], [jax.ShapeDtypeStruct(pack.shape, F32)], {}, [],
                         _pair_copies((), [], True, 0))
        chip_pack = _pair_sum(pack, recv[0], kc, None, "pair_sum_pack", F32)
        copies = _chip_copies((), [], pack.shape[0], 1)
        land = lax.empty((3, prow, pack.shape[1]), F32)
        pack_sems_s, pack_sems_r, pack_bufs, after = _exchange_start("reduce_pack_start", [chip_pack, land], copies)

        names, chips, recvs = (), [], []
        for group, group_names in GROUPS.items():
            (send_sems, recv_sems, bufs, _), group_copies = self.reducing[group]
            bufs = _exchange_wait("reduce_%s_wait" % group, send_sems, recv_sems, bufs, group_copies, after)
            n = len(group_names)
            names, chips, recvs = names + group_names, chips + bufs[:n], recvs + bufs[n:]
            after = bufs[n]
        total = [_chip_sum(chips[i], recvs[i], kc, BIG[n][1], BIG[n][0], "chip_sum_" + n)
                 for i, n in enumerate(names)]

        def my_half(half_axis, ref, pos):
            hsize = ref.shape[half_axis] // 2
            return _view(ref, half_axis, pos[2] * hsize, hsize)

        swap = [(lambda I, O, pos, i=i, n=n: my_half(BIG[n][0], I[i], pos),
                 lambda I, O, pos, i=i, n=n: my_half(BIG[n][0], O[i], pos), "c") for i, n in enumerate(names)]
        self.swapping = (_exchange_start("swap_start", total, swap), swap, names)

        chip_pack, recv_pack = _exchange_wait("reduce_pack_wait", pack_sems_s, pack_sems_r, pack_bufs, copies,
                                              self.swapping[0][3])
        total_pack = _chip_sum(chip_pack, recv_pack, kc, None, 0, "chip_sum_pack")
        swap = [(lambda I, O, pos: my_half(0, I[0], pos), lambda I, O, pos: my_half(0, O[0], pos), "c")]
        return _exchange("swap_pack", [total_pack], [jax.ShapeDtypeStruct(pack.shape, F32)], {0: 0}, [], swap)[0]

    def finish_big(self, after):
        (send_sems, recv_sems, bufs, _), swap, names = self.swapping
        return dict(zip(names, _exchange_wait("swap_wait", send_sems, recv_sems, bufs, swap, after)))


WEIGHTS = ("meta_tokens", "norm_mix_g", "w_in", "conv_w", "ssm_lam_re", "ssm_lam_im", "ssm_log_dt", "ssm_b_re",
           "ssm_b_im", "ssm_c_re", "ssm_c_im", "ssm_d", "ssm_w_glu", "gain_conv_out", "gain_ssm_out", "w_out",
           "norm_ffn_g", "w_up", "ffn_conv_w", "ffn_conv_b", "w_down", "norm_final_g")
TINY_SHARDED = ("meta_tokens", "conv_w", "ffn_conv_w")
REPLICATED = tuple(n for n in WEIGHTS if n not in BIG and n not in TINY_SHARDED)
PACK_COLS = 512


def _pack(arrays, row_mult, cols):
    flat = jnp.concatenate([a.reshape(-1).astype(F32) for a in arrays])
    n = flat.shape[0]
    total = -(-n // (row_mult * cols)) * (row_mult * cols)
    return jnp.concatenate([flat, jnp.zeros((total - n,), F32)]).reshape(total // cols, cols)


def _unpack(packed, shapes):
    flat = packed.reshape(-1)
    out, off = [], 0
    for s in shapes:
        n = math.prod(s)
        out.append(flat[off:off + n].reshape(s))
        off += n
    return out


def kernel(x, meta_tokens, norm_mix_g, w_in, conv_w, ssm_lam_re, ssm_lam_im, ssm_log_dt, ssm_b_re, ssm_b_im, ssm_c_re, ssm_c_im, ssm_d, ssm_w_glu, gain_conv_out, gain_ssm_out, w_out, norm_ffn_g, w_up, ffn_conv_w, ffn_conv_b, w_down, norm_final_g, loss_target, m_meta_tokens, m_norm_mix_g, m_w_in, m_conv_w, m_ssm_lam_re, m_ssm_lam_im, m_ssm_log_dt, m_ssm_b_re, m_ssm_b_im, m_ssm_c_re, m_ssm_c_im, m_ssm_d, m_ssm_w_glu, m_gain_conv_out, m_gain_ssm_out, m_w_out, m_norm_ffn_g, m_w_up, m_ffn_conv_w, m_ffn_conv_b, m_w_down, m_norm_final_g, v_meta_tokens, v_norm_mix_g, v_w_in, v_conv_w, v_ssm_lam_re, v_ssm_lam_im, v_ssm_log_dt, v_ssm_b_re, v_ssm_b_im, v_ssm_c_re, v_ssm_c_im, v_ssm_d, v_ssm_w_glu, v_gain_conv_out, v_gain_ssm_out, v_w_out, v_norm_ffn_g, v_w_up, v_ffn_conv_w, v_ffn_conv_b, v_w_down, v_norm_final_g):
    args = dict(locals())
    w = {n: args[n] for n in WEIGHTS}
    mom = {n: args["m_" + n] for n in WEIGHTS}
    var = {n: args["v_" + n] for n in WEIGHTS}
    kx, ky, kc_ = lax.axis_index("x"), lax.axis_index("y"), lax.axis_index("c")
    chip = 2 * kx + ky
    kc = jnp.stack([chip, kc_]).astype(jnp.int32)

    def squeeze(n, a):
        if n == "meta_tokens":
            return a
        if n == "norm_final_g":
            return a.reshape(1, -1)
        a = a[0]
        return a.reshape(1, -1) if a.ndim == 1 else a

    wl = {n: squeeze(n, w[n]) for n in WEIGHTS}
    ml = {n: squeeze(n, mom[n]) for n in WEIGHTS}
    vl = {n: squeeze(n, var[n]) for n in WEIGHTS}

    tiny = _pack([wl[n] for n in TINY_SHARDED], SUBLANES, LANES)
    ex = _Exchanges({n: wl[n] for n in BIG_NAMES}, tiny, kc)
    tiny_shapes = [wl[n].shape for n in TINY_SHARDED]
    tiny_all = ex.small_params(kc)
    tiny_parts = [_unpack(tiny_all[k], tiny_shapes) for k in range(4)]
    p = {n: wl[n] for n in WEIGHTS if n not in BIG}
    for j, n in enumerate(TINY_SHARDED):
        p[n] = jnp.concatenate([tiny_parts[k][j] for k in range(4)], axis=1)
    p["ssm_log_dt"] = wl["ssm_log_dt"].reshape(-1)

    loss_local, grad_x, grads = _local_step(x[0], loss_target[0], p, ex)

    small_names = REPLICATED + TINY_SHARDED
    small_shapes = [tuple(grads[n].shape) for n in small_names] + [(1,)]
    pack = _pack([grads[n] for n in small_names] + [loss_local.reshape(1)], 2 * 16, PACK_COLS)
    g_pack = ex.finish_pack(pack)
    g_small = dict(zip(small_names + ("loss",), _unpack(g_pack, small_shapes)))
    loss = g_small["loss"][0]
    swapped = ("ssm_b_re", "ssm_b_im")

    def view(n, a):
        if n in swapped:
            return jnp.swapaxes(a, -1, -2)
        return a.reshape(1, -1) if a.ndim == 1 else a

    g = {}
    for n in REPLICATED:
        g[n] = g_small[n].reshape(view(n, w[n]).shape)
    for n in TINY_SHARDED:
        cols = wl[n].shape[1]
        g[n] = lax.dynamic_slice_in_dim(g_small[n], chip * cols, cols, axis=1).reshape(w[n].shape)
    delta, new_m, new_v = {}, {}, {}
    small = [[view(n, d[n]) for n in small_names] for d in (w, mom, var)]
    small.insert(1, [g[n] for n in small_names])
    for d, outs in zip((delta, new_m, new_v), _adamw_whole(*small, "adamw_small")):
        d.update(zip(small_names, outs))
    for d in (g, delta, new_m, new_v):
        d.update({n: jnp.swapaxes(d[n], -1, -2) for n in swapped})
    g_big = ex.finish_big(delta[small_names[0]])
    for n in BIG_NAMES:
        g[n], delta[n], new_m[n], new_v[n] = _adamw(wl[n], g_big[n], ml[n], vl[n], "adamw_" + n)

    def like(n, a):
        return a.reshape(w[n].shape)

    return (loss, grad_x[None], *[like(n, g[n]) for n in WEIGHTS], *[like(n, delta[n]) for n in WEIGHTS],
            *[like(n, new_m[n]) for n in WEIGHTS], *[like(n, new_v[n]) for n in WEIGHTS])
```

```python
import functools
import math

import jax
import jax.numpy as jnp
from jax import lax
from jax.experimental import pallas as pl
from jax.experimental.pallas import tpu as pltpu

F32 = jnp.float32
BF16 = jnp.bfloat16
MESH = pl.DeviceIdType.MESH

N_META = 16
N_GROUPS = 32
GROUP = 16
STATE = 64
RMS_EPS = 1e-6
ADAM_LR = 0.001
ADAM_B1 = 0.9
ADAM_B2 = 0.999
ADAM_EPS = 1e-08
ADAM_WD = 0.01
ADAM_STEP = 10

LANES = 128
SUBLANES = 8
ROW_ALIGN = 128
ROW_TILES = 4
VMEM_LIMIT = 52 * 1024 * 1024
MM_VMEM_BUDGET = 40 * 1024 * 1024
GELU_C = math.sqrt(2.0 / math.pi)
GELU_A = 0.044715


def _cparams(*sem):
    return pltpu.CompilerParams(dimension_semantics=sem, vmem_limit_bytes=VMEM_LIMIT)


def _pick_tile(dim, cap, mult):
    best = None
    for t in range(mult, min(dim, cap) + 1, mult):
        if dim % t == 0:
            best = t
    return best if best is not None else dim


def _mm(a, b, mode, name, out_dtype=F32, acc_in=None, after=None):
    if mode == "tn":
        kdim, m = a.shape
    else:
        m, kdim = a.shape
    n = b.shape[0] if mode == "nt" else b.shape[1]
    tm = _pick_tile(m, 1408, LANES if mode == "tn" else 16)
    tk = _pick_tile(kdim, 2816, LANES)
    nk = kdim // tk
    out_bytes = jnp.dtype(out_dtype).itemsize
    for cap in (1408, 1024, 512, 256, LANES):
        tn = _pick_tile(n, cap, LANES)
        blocks = 2 * (tm * tk * 2 + tk * tn * 2 + tm * tn * out_bytes * (2 if acc_in is not None else 1))
        if blocks + (tm * tn * 4 if nk > 1 else 0) <= MM_VMEM_BUDGET:
            break
    has_acc = acc_in is not None

    def body(*refs):
        if after is not None:
            refs = refs[1:]
        if has_acc:
            a_ref, b_ref, c_ref, o_ref = refs[:4]
            rest = refs[4:]
        else:
            a_ref, b_ref, o_ref = refs[:3]
            c_ref = None
            rest = refs[3:]
        if mode == "nn":
            p = jnp.dot(a_ref[...], b_ref[...], preferred_element_type=F32)
        elif mode == "nt":
            p = lax.dot_general(a_ref[...], b_ref[...], (((1,), (1,)), ((), ())), preferred_element_type=F32)
        else:
            p = lax.dot_general(a_ref[...], b_ref[...], (((0,), (0,)), ((), ())), preferred_element_type=F32)
        if nk == 1:
            if has_acc:
                p = p + c_ref[...]
            o_ref[...] = p.astype(out_dtype)
        else:
            acc_ref = rest[0]
            k = pl.program_id(2)

            @pl.when(k == 0)
            def _():
                acc_ref[...] = p + c_ref[...] if has_acc else p

            @pl.when(k > 0)
            def _():
                acc_ref[...] += p

            @pl.when(k == nk - 1)
            def _():
                o_ref[...] = acc_ref[...].astype(out_dtype)

    if mode == "tn":
        a_spec = pl.BlockSpec((tk, tm), lambda i, j, k: (k, i))
    else:
        a_spec = pl.BlockSpec((tm, tk), lambda i, j, k: (i, k))
    if mode == "nt":
        b_spec = pl.BlockSpec((tn, tk), lambda i, j, k: (j, k))
    else:
        b_spec = pl.BlockSpec((tk, tn), lambda i, j, k: (k, j))
    o_spec = pl.BlockSpec((tm, tn), lambda i, j, k: (i, j))
    in_specs = [a_spec, b_spec] + ([o_spec] if has_acc else [])
    args = (a, b) + ((acc_in,) if has_acc else ())
    if after is not None:
        in_specs = [pl.BlockSpec(memory_space=pl.ANY)] + in_specs
        args = (after,) + args
    return pl.pallas_call(
        body, name=name, grid=(m // tm, n // tn, nk),
        in_specs=in_specs, out_specs=o_spec,
        out_shape=jax.ShapeDtypeStruct((m, n), out_dtype),
        scratch_shapes=[pltpu.VMEM((tm, tn), F32)] if nk > 1 else [],
        compiler_params=_cparams("parallel", "parallel", "arbitrary"),
    )(*args)


def _mm_rows(a, b, mode, name, ins, outs, epilogue, scratch=()):
    m, kdim = a.shape
    n = b.shape[0] if mode == "nt" else b.shape[1]
    tm = m // ROW_TILES
    tk = _pick_tile(kdim, 2816, LANES)
    nk = kdim // tk
    ni, no = len(ins), len(outs)

    def body(*refs):
        a_ref, b_ref = refs[:2]
        in_refs, out_refs, rest = refs[2:2 + ni], refs[2 + ni:2 + ni + no], refs[2 + ni + no:]
        i = pl.program_id(0)
        if mode == "nn":
            p = jnp.dot(a_ref[...], b_ref[...], preferred_element_type=F32)
        else:
            p = lax.dot_general(a_ref[...], b_ref[...], (((1,), (1,)), ((), ())), preferred_element_type=F32)
        if nk == 1:
            epilogue(p, i, in_refs, out_refs, rest)
        else:
            acc_ref = rest[0]
            k = pl.program_id(1)

            @pl.when(k == 0)
            def _():
                acc_ref[...] = p

            @pl.when(k > 0)
            def _():
                acc_ref[...] += p

            @pl.when(k == nk - 1)
            def _():
                epilogue(acc_ref[...], i, in_refs, out_refs, rest[1:])

    def spec(shape, kind):
        if kind == "rows":
            return pl.BlockSpec((tm,) + tuple(shape[1:]), lambda i, k: (i,) + (0,) * (len(shape) - 1))
        if kind == "whole":
            return pl.BlockSpec(tuple(shape), lambda i, k: (0,) * len(shape))
        return pl.BlockSpec(memory_space=pl.ANY)

    a_spec = pl.BlockSpec((tm, tk), lambda i, k: (i, k))
    b_spec = pl.BlockSpec((n, tk), lambda i, k: (0, k)) if mode == "nt" else pl.BlockSpec((tk, n), lambda i, k: (k, 0))
    return pl.pallas_call(
        body, name=name, grid=(ROW_TILES, nk),
        in_specs=[a_spec, b_spec] + [spec(x.shape, kind) for x, kind in ins],
        out_specs=[spec(shape, kind) for shape, _, kind in outs],
        out_shape=[jax.ShapeDtypeStruct(shape, dtype) for shape, dtype, _ in outs],
        scratch_shapes=([pltpu.VMEM((tm, n), F32)] if nk > 1 else []) + list(scratch),
        compiler_params=_cparams("arbitrary", "arbitrary"),
    )(a, b, *[x for x, _ in ins])


def _rows(shape_cols, tr, dtype=None):
    return pl.BlockSpec((tr, shape_cols), lambda i: (i, 0))


def _const(shape):
    return pl.BlockSpec(shape, lambda i: (0,) * len(shape))


def _rms(x):
    return lax.rsqrt(jnp.mean(x * x, axis=-1, keepdims=True) + RMS_EPS)


def _rms_bwd(x, r, g, dy):
    xn = x * r
    dxn = dy * g
    dx = r * (dxn - xn * jnp.mean(dxn * xn, axis=-1, keepdims=True))
    return dx, dy * xn


def _gelu(y):
    return 0.5 * y * (1.0 + jnp.tanh(GELU_C * (y + GELU_A * y * y * y)))


def _gelu_grad(y):
    t = jnp.tanh(GELU_C * (y + GELU_A * y * y * y))
    return 0.5 * (1.0 + t) + 0.5 * y * (1.0 - t * t) * GELU_C * (1.0 + 3.0 * GELU_A * y * y)


def _sigmoid(z):
    return 1.0 / (1.0 + jnp.exp(-z))


def _proj_res_norm(a, w, h, g, after, name):
    def epilogue(p, i, ins, outs, _):
        x = ins[0][...] + p
        outs[0][...] = x
        outs[1][...] = (x * _rms(x) * ins[1][...]).astype(BF16)

    return _mm_rows(a, w, "nn", name, [(h, "rows"), (g, "whole"), (after, "hbm")],
                    [(h.shape, F32, "rows"), (h.shape, BF16, "rows")], epilogue)


def _proj_norm_bwd(da, w, h, g, dres, after, name):
    d = h.shape[1]

    def epilogue(p, i, ins, outs, _):
        x = ins[0][...]
        dx, dgs = _rms_bwd(x, _rms(x), ins[1][...], p)
        dh = ins[2][...] + dx
        outs[0][...] = dh
        outs[1][...] = dh.astype(BF16)

        @pl.when(i == 0)
        def _():
            outs[2][...] = jnp.zeros_like(outs[2])

        outs[2][...] += jnp.sum(dgs, axis=0, keepdims=True)

    return _mm_rows(da, w, "nt", name, [(h, "rows"), (g, "whole"), (dres, "rows"), (after, "hbm")],
                    [(h.shape, F32, "rows"), (h.shape, BF16, "rows"), ((1, d), F32, "whole")], epilogue)


def _input_norm_bwd(h, g, dhn, dres, n_real, name):
    tp, d = h.shape
    tr = tp // ROW_TILES

    def body(h_ref, g_ref, dhn_ref, dres_ref, dx_ref, dmeta_ref, dg_ref, stage, sem):
        i = pl.program_id(0)
        x = h_ref[...]
        dx, dgs = _rms_bwd(x, _rms(x), g_ref[...], dhn_ref[...])
        stage[...] = dres_ref[...] + dx

        @pl.when(i == 0)
        def _():
            dg_ref[...] = jnp.zeros_like(dg_ref)
            dmeta_ref[...] = stage[:N_META, :]

        dg_ref[...] += jnp.sum(dgs, axis=0, keepdims=True)
        for t in range(ROW_TILES):
            lo, hi = max(t * tr, N_META), min((t + 1) * tr, n_real)
            if hi > lo:
                @pl.when(i == t)
                def _(t=t, lo=lo, hi=hi):
                    cp = pltpu.make_async_copy(stage.at[pl.ds(lo - t * tr, hi - lo), :],
                                               dx_ref.at[pl.ds(lo - N_META, hi - lo), :], sem)
                    cp.start()
                    cp.wait()

    return pl.pallas_call(
        body, name=name, grid=(ROW_TILES,),
        in_specs=[_rows(d, tr), _const((1, d)), _rows(d, tr), _rows(d, tr)],
        out_specs=[pl.BlockSpec(memory_space=pl.ANY), _const((N_META, d)), _const((1, d))],
        out_shape=[jax.ShapeDtypeStruct((n_real - N_META, d), F32), jax.ShapeDtypeStruct((N_META, d), F32),
                   jax.ShapeDtypeStruct((1, d), F32)],
        scratch_shapes=[pltpu.VMEM((tr, d), F32), pltpu.SemaphoreType.DMA],
        compiler_params=_cparams("arbitrary"))(h, g, dhn, dres)


def _load_token_rows(tok_hbm, buf, sem, tr, n_real, head=None, wait=False, i=None):
    i = pl.program_id(0) if i is None else i
    for t in range(ROW_TILES):
        base = t * tr
        lo, hi = max(base, N_META), min(base + tr, n_real)

        @pl.when(i == t)
        def _(base=base, lo=lo, hi=hi):
            if hi > lo:
                cp = pltpu.make_async_copy(tok_hbm.at[pl.ds(lo - N_META, hi - lo), :],
                                           buf.at[pl.ds(lo - base, hi - lo), :], sem)
                if wait:
                    cp.wait()
                    return
                cp.start()
            if wait:
                return
            if base < N_META:
                buf[0:N_META - base, :] = (jnp.zeros((N_META - base, buf.shape[1]), F32) if head is None
                                           else head[base:N_META, :])
            if hi < base + tr:
                buf[max(hi, base) - base:tr, :] = jnp.zeros((base + tr - max(hi, base), buf.shape[1]), F32)


def _input_norm_fwd(x, meta, g, tp, name):
    seq, d = x.shape
    tr = tp // ROW_TILES
    n_real = N_META + seq

    def body(x_hbm, meta_ref, g_ref, h_ref, hn_ref, buf, sem):
        _load_token_rows(x_hbm, buf, sem, tr, n_real, head=meta_ref)
        _load_token_rows(x_hbm, buf, sem, tr, n_real, wait=True)
        h = buf[...]
        h_ref[...] = h
        hn_ref[...] = (h * _rms(h) * g_ref[...]).astype(BF16)

    return pl.pallas_call(
        body, name=name, grid=(ROW_TILES,),
        in_specs=[pl.BlockSpec(memory_space=pl.ANY), _const((N_META, d)), _const((1, d))],
        out_specs=[_rows(d, tr), _rows(d, tr)],
        out_shape=[jax.ShapeDtypeStruct((tp, d), F32), jax.ShapeDtypeStruct((tp, d), BF16)],
        scratch_shapes=[pltpu.VMEM((tr, d), F32), pltpu.SemaphoreType.DMA],
        compiler_params=_cparams("arbitrary"))(x, meta, g)


def _proj_loss_bwd(act, w, h1, target, g, n_real, name):
    tp, d = h1.shape
    tr = tp // ROW_TILES

    def epilogue(p, i, ins, outs, scratch):
        h1_ref, t_hbm, g_ref = ins
        loss_ref, dh_ref, dhb_ref, dg_ref = outs
        t_buf, sem = scratch
        _load_token_rows(t_hbm, t_buf, sem, tr, n_real, i=i)
        x = h1_ref[...] + p
        r = _rms(x)
        row = i * tr + lax.broadcasted_iota(jnp.int32, (tr, d), 0)
        valid = (row >= N_META) & (row < n_real)
        _load_token_rows(t_hbm, t_buf, sem, tr, n_real, wait=True, i=i)
        e = jnp.where(valid, x * r * g_ref[...] - t_buf[...], 0.0)
        dx, dgs = _rms_bwd(x, r, g_ref[...], e * (1.0 / d))
        dh_ref[...] = dx
        dhb_ref[...] = dx.astype(BF16)

        @pl.when(i == 0)
        def _():
            dg_ref[...] = jnp.zeros_like(dg_ref)
            loss_ref[...] = jnp.zeros_like(loss_ref)

        dg_ref[...] += jnp.sum(dgs, axis=0, keepdims=True)
        loss_ref[...] += (0.5 / d) * jnp.sum(jnp.sum(e * e, axis=0, keepdims=True), axis=1, keepdims=True)

    return _mm_rows(act, w, "nn", name, [(h1, "rows"), (target, "hbm"), (g, "whole")],
                    [((1, LANES), F32, "whole"), ((tp, d), F32, "rows"), ((tp, d), BF16, "rows"),
                     ((1, d), F32, "whole")],
                    epilogue, scratch=[pltpu.VMEM((tr, d), F32), pltpu.SemaphoreType.DMA])


def _mix_fwd(co, y, z, gc, gs, name):
    tp, dh = co.shape
    tr = tp // ROW_TILES

    def body(co_ref, y_ref, z_ref, gc_ref, gs_ref, m_ref):
        c = co_ref[...]
        m_ref[:, :dh] = (c * _rms(c) * gc_ref[...]).astype(BF16)
        so = _gelu(y_ref[...]) * _sigmoid(z_ref[...])
        m_ref[:, dh:] = (so * _rms(so) * gs_ref[...]).astype(BF16)

    return pl.pallas_call(
        body, name=name, grid=(ROW_TILES,),
        in_specs=[_rows(dh, tr)] * 3 + [_const((1, dh))] * 2,
        out_specs=_rows(2 * dh, tr),
        out_shape=jax.ShapeDtypeStruct((tp, 2 * dh), BF16),
        compiler_params=_cparams("parallel"))(co, y, z, gc, gs)


def _proj_mix_bwd(dh1b, w, co, y, z, gc, gs, name):
    tp, dh = co.shape

    def epilogue(p, i, ins, outs, _):
        co_ref, y_ref, z_ref, gc_ref, gs_ref = ins
        dco_ref, dz_ref, dgp_ref, dgc_ref, dgs_ref = outs
        c = co_ref[...]
        dco, dgc = _rms_bwd(c, _rms(c), gc_ref[...], p[:, :dh])
        dco_ref[...] = dco
        gl = _gelu(y_ref[...])
        sg = _sigmoid(z_ref[...])
        so = gl * sg
        dso, dgs = _rms_bwd(so, _rms(so), gs_ref[...], p[:, dh:])
        dz_ref[...] = (dso * gl * sg * (1.0 - sg)).astype(BF16)
        dgp_ref[...] = dso * sg

        @pl.when(i == 0)
        def _():
            dgc_ref[...] = jnp.zeros_like(dgc_ref)
            dgs_ref[...] = jnp.zeros_like(dgs_ref)

        dgc_ref[...] += jnp.sum(dgc, axis=0, keepdims=True)
        dgs_ref[...] += jnp.sum(dgs, axis=0, keepdims=True)

    return _mm_rows(dh1b, w, "nt", name,
                    [(co, "rows"), (y, "rows"), (z, "rows"), (gc, "whole"), (gs, "whole")],
                    [((tp, dh), F32, "rows"), ((tp, dh), BF16, "rows"), ((tp, dh), F32, "rows"),
                     ((1, dh), F32, "whole"), ((1, dh), F32, "whole")], epilogue)


def _shift_down(x, k):
    row = lax.broadcasted_iota(jnp.int32, x.shape, 0)
    return jnp.where(row >= k, pltpu.roll(x, k, 0), 0.0)


def _shift_up(x, k):
    n = x.shape[0]
    row = lax.broadcasted_iota(jnp.int32, x.shape, 0)
    return jnp.where(row < n - k, pltpu.roll(x, n - k, 0), 0.0)


def _dwconv(x, w_ref):
    return w_ref[2:3, :] * x + w_ref[1:2, :] * _shift_down(x, 1) + w_ref[0:1, :] * _shift_down(x, 2)


def _dwconv_bwd(x, dy, w_ref):
    dx = w_ref[2:3, :] * dy + w_ref[1:2, :] * _shift_up(dy, 1) + w_ref[0:1, :] * _shift_up(dy, 2)
    dw = jnp.concatenate([jnp.sum(dy * _shift_down(x, 2), axis=0, keepdims=True),
                          jnp.sum(dy * _shift_down(x, 1), axis=0, keepdims=True),
                          jnp.sum(dy * x, axis=0, keepdims=True)], axis=0)
    return dx, dw


def _scan(s_re, s_im, tab_ref, reverse):
    n_chunks = s_re.shape[0] // SUBLANES
    n_strips = s_re.shape[1] // LANES
    last = 0 if reverse else SUBLANES - 1

    def step(chunk, carry):
        r0 = pl.multiple_of(chunk * SUBLANES, SUBLANES)
        out = []
        for st in range(n_strips):
            lanes = slice(st * LANES, (st + 1) * LANES)
            cr, ci = carry[2 * st], carry[2 * st + 1]
            xr = s_re[pl.ds(r0, SUBLANES), lanes]
            xi = s_im[pl.ds(r0, SUBLANES), lanes]
            for level, k in enumerate((1, 2, 4)):
                mr = tab_ref[2 * level, :, lanes]
                mi = tab_ref[2 * level + 1, :, lanes]
                sh = SUBLANES - k if reverse else k
                rr = pltpu.roll(xr, sh, 0)
                ri = pltpu.roll(xi, sh, 0)
                xr, xi = xr + (mr * rr - mi * ri), xi + (mr * ri + mi * rr)
            pwr = tab_ref[6, :, lanes]
            pwi = tab_ref[7, :, lanes]
            xr, xi = xr + (pwr * cr - pwi * ci), xi + (pwr * ci + pwi * cr)
            s_re[pl.ds(r0, SUBLANES), lanes] = xr
            s_im[pl.ds(r0, SUBLANES), lanes] = xi
            out.append(jnp.broadcast_to(xr[last:last + 1, :], (SUBLANES, LANES)))
            out.append(jnp.broadcast_to(xi[last:last + 1, :], (SUBLANES, LANES)))
        return tuple(out)

    def body(i, carry):
        for half in range(2):
            j = 2 * i + half
            carry = step((n_chunks - 1 - j) if reverse else j, carry)
        return carry

    zero = jnp.zeros((SUBLANES, LANES), F32)
    lax.fori_loop(0, n_chunks // 2, body, (zero,) * (2 * n_strips))


def _seq_fwd(proj, conv_w, bc_re, bc_im, cc_re, cc_im, dskip, tab_f, name):
    tp = proj.shape[0]
    dh = proj.shape[1] // 4
    nq = dh // LANES
    sw = STATE * N_GROUPS // nq

    def body(b_ref, c_ref, v_ref, u_ref, w_ref, bre_ref, bim_ref, cre_ref, cim_ref, d_ref, tab_ref,
             co_ref, y_ref, g_ref, s_re, s_im):
        co_ref[...] = b_ref[...] * _dwconv(c_ref[...] * v_ref[...], w_ref)
        u = u_ref[...]
        ub = u.astype(BF16)
        s_re[...] = jnp.dot(ub, bre_ref[...], preferred_element_type=F32)
        s_im[...] = jnp.dot(ub, bim_ref[...], preferred_element_type=F32)
        _scan(s_re, s_im, tab_ref, False)
        y = (jnp.dot(s_re[...].astype(BF16), cre_ref[...], preferred_element_type=F32)
             - jnp.dot(s_im[...].astype(BF16), cim_ref[...], preferred_element_type=F32)
             + d_ref[...] * u)
        y_ref[...] = y
        g_ref[...] = _gelu(y).astype(BF16)

    col = lambda off: pl.BlockSpec((tp, LANES), lambda q, off=off: (0, off * nq + q))
    blk = pl.BlockSpec((tp, LANES), lambda q: (0, q))
    return pl.pallas_call(
        body, name=name, grid=(nq,),
        in_specs=[col(0), col(1), col(2), col(3),
                  pl.BlockSpec((3, LANES), lambda q: (0, q)),
                  pl.BlockSpec((LANES, sw), lambda q: (0, q)), pl.BlockSpec((LANES, sw), lambda q: (0, q)),
                  pl.BlockSpec((sw, LANES), lambda q: (q, 0)), pl.BlockSpec((sw, LANES), lambda q: (q, 0)),
                  pl.BlockSpec((1, LANES), lambda q: (0, q)),
                  pl.BlockSpec((8, SUBLANES, sw), lambda q: (0, 0, q))],
        out_specs=[blk, blk, blk],
        out_shape=[jax.ShapeDtypeStruct((tp, dh), F32), jax.ShapeDtypeStruct((tp, dh), F32),
                   jax.ShapeDtypeStruct((tp, dh), BF16)],
        scratch_shapes=[pltpu.VMEM((tp, sw), F32), pltpu.VMEM((tp, sw), F32)],
        compiler_params=_cparams("parallel"),
    )(proj, proj, proj, proj, conv_w, bc_re, bc_im, cc_re, cc_im, dskip, tab_f)


def _conv_bwd(proj, dco, conv_w, name):
    tp = proj.shape[0]
    dh = proj.shape[1] // 4
    nq = dh // LANES

    def body(b_ref, c_ref, v_ref, dco_ref, w_ref, dproj_ref, dw_ref, stage, sem):
        q = pl.program_id(0)
        cg = c_ref[...]
        vg = v_ref[...]
        cv = cg * vg
        dco_v = dco_ref[...]
        dcv, dw = _dwconv_bwd(cv, dco_v * b_ref[...], w_ref)
        dw_ref[...] = dw
        stage[0] = (dco_v * _dwconv(cv, w_ref)).astype(BF16)
        stage[1] = (dcv * vg).astype(BF16)
        stage[2] = (dcv * cg).astype(BF16)
        copies = [pltpu.make_async_copy(stage.at[p], dproj_ref.at[:, pl.ds((p * nq + q) * LANES, LANES)], sem.at[p])
                  for p in range(3)]
        for cp in copies:
            cp.start()
        for cp in copies:
            cp.wait()

    col = lambda off: pl.BlockSpec((tp, LANES), lambda q, off=off: (0, off * nq + q))
    return pl.pallas_call(
        body, name=name, grid=(nq,),
        in_specs=[col(0), col(1), col(2), pl.BlockSpec((tp, LANES), lambda q: (0, q)),
                  pl.BlockSpec((3, LANES), lambda q: (0, q))],
        out_specs=[pl.BlockSpec(memory_space=pl.ANY), pl.BlockSpec((3, LANES), lambda q: (0, q))],
        out_shape=[jax.ShapeDtypeStruct((tp, 4 * dh), BF16), jax.ShapeDtypeStruct((3, dh), F32)],
        scratch_shapes=[pltpu.VMEM((3, tp, LANES), BF16), pltpu.SemaphoreType.DMA((3,))],
        compiler_params=_cparams("arbitrary"),
    )(proj, proj, proj, dco, conv_w)


def _ssm_bwd(proj, y, dg, dproj, bc_re, bc_im, cc_re, cc_im, dskip, tab_f, tab_r, name):
    tp = proj.shape[0]
    dh = proj.shape[1] // 4
    nq = dh // LANES
    sw = STATE * N_GROUPS // nq

    def body(u_ref, y_ref, dg_ref, dproj_in, bre_ref, bim_ref, cre_ref, cim_ref, d_ref, tabf_ref, tabr_ref,
             dproj_ref, dbre_ref, dbim_ref, dcre_ref, dcim_ref, dd_ref, dar_ref, dai_ref,
             s_re, s_im, l_re, l_im, stage, sem):
        del dproj_in
        q = pl.program_id(0)
        nt = (((1,), (1,)), ((), ()))
        tn = (((0,), (0,)), ((), ()))
        u = u_ref[...]
        ub = u.astype(BF16)
        s_re[...] = jnp.dot(ub, bre_ref[...], preferred_element_type=F32)
        s_im[...] = jnp.dot(ub, bim_ref[...], preferred_element_type=F32)
        _scan(s_re, s_im, tabf_ref, False)
        dy = dg_ref[...] * _gelu_grad(y_ref[...])
        dyb = dy.astype(BF16)
        dd_ref[...] = jnp.sum(dy * u, axis=0, keepdims=True)
        l_re[...] = lax.dot_general(dyb, cre_ref[...], nt, preferred_element_type=F32)
        l_im[...] = -lax.dot_general(dyb, cim_ref[...], nt, preferred_element_type=F32)
        dcre_ref[...] = lax.dot_general(s_re[...].astype(BF16), dyb, tn, preferred_element_type=F32)
        dcim_ref[...] = -lax.dot_general(s_im[...].astype(BF16), dyb, tn, preferred_element_type=F32)
        _scan(l_re, l_im, tabr_ref, True)
        for st in range(sw // LANES):
            lanes = slice(st * LANES, (st + 1) * LANES)
            lr = l_re[:, lanes]
            li = l_im[:, lanes]
            pr = _shift_down(s_re[:, lanes], 1)
            pi = _shift_down(s_im[:, lanes], 1)
            dar_ref[:, lanes] = jnp.sum(lr * pr + li * pi, axis=0, keepdims=True)
            dai_ref[:, lanes] = jnp.sum(li * pr - lr * pi, axis=0, keepdims=True)
        lrb = l_re[...].astype(BF16)
        lib = l_im[...].astype(BF16)
        du = (dy * d_ref[...] + lax.dot_general(lrb, bre_ref[...], nt, preferred_element_type=F32)
              + lax.dot_general(lib, bim_ref[...], nt, preferred_element_type=F32))
        stage[...] = du.astype(BF16)
        dbre_ref[...] = lax.dot_general(ub, lrb, tn, preferred_element_type=F32)
        dbim_ref[...] = lax.dot_general(ub, lib, tn, preferred_element_type=F32)
        cp = pltpu.make_async_copy(stage, dproj_ref.at[:, pl.ds((3 * nq + q) * LANES, LANES)], sem)
        cp.start()
        cp.wait()

    blk = pl.BlockSpec((tp, LANES), lambda q: (0, q))
    bspec = pl.BlockSpec((LANES, sw), lambda q: (0, q))
    cspec = pl.BlockSpec((sw, LANES), lambda q: (q, 0))
    tspec = pl.BlockSpec((8, SUBLANES, sw), lambda q: (0, 0, q))
    nstate = STATE * N_GROUPS
    return pl.pallas_call(
        body, name=name, grid=(nq,),
        in_specs=[pl.BlockSpec((tp, LANES), lambda q: (0, 3 * nq + q)), blk, blk, pl.BlockSpec(memory_space=pl.ANY),
                  bspec, bspec, cspec, cspec, pl.BlockSpec((1, LANES), lambda q: (0, q)), tspec, tspec],
        out_specs=[pl.BlockSpec(memory_space=pl.ANY), bspec, bspec, cspec, cspec,
                   pl.BlockSpec((1, LANES), lambda q: (0, q)),
                   pl.BlockSpec((1, sw), lambda q: (0, q)), pl.BlockSpec((1, sw), lambda q: (0, q))],
        out_shape=[jax.ShapeDtypeStruct((tp, 4 * dh), BF16),
                   jax.ShapeDtypeStruct((LANES, nstate), F32), jax.ShapeDtypeStruct((LANES, nstate), F32),
                   jax.ShapeDtypeStruct((nstate, LANES), F32), jax.ShapeDtypeStruct((nstate, LANES), F32),
                   jax.ShapeDtypeStruct((1, dh), F32),
                   jax.ShapeDtypeStruct((1, nstate), F32), jax.ShapeDtypeStruct((1, nstate), F32)],
        input_output_aliases={3: 0},
        scratch_shapes=[pltpu.VMEM((tp, sw), F32)] * 4 + [pltpu.VMEM((tp, LANES), BF16), pltpu.SemaphoreType.DMA],
        compiler_params=_cparams("arbitrary"),
    )(proj, y, dg, dproj, bc_re, bc_im, cc_re, cc_im, dskip, tab_f, tab_r)


FFN_TILE = 256


def _ffn_act(up, fw, fb, name):
    tp, two_ff = up.shape
    dff = two_ff // 2
    tc = FFN_TILE
    nj = dff // tc

    def body(ua_ref, uv_ref, wa_ref, wv_ref, ba_ref, bv_ref, act_ref):
        a = _dwconv(ua_ref[...], wa_ref) + ba_ref[...]
        v = _dwconv(uv_ref[...], wv_ref) + bv_ref[...]
        act_ref[...] = (a * _sigmoid(a) * v).astype(BF16)

    lo = lambda r: pl.BlockSpec((r, tc), lambda j: (0, j))
    hi = lambda r: pl.BlockSpec((r, tc), lambda j: (0, nj + j))
    return pl.pallas_call(
        body, name=name, grid=(nj,),
        in_specs=[lo(tp), hi(tp), lo(3), hi(3), lo(1), hi(1)],
        out_specs=lo(tp),
        out_shape=jax.ShapeDtypeStruct((tp, dff), BF16),
        compiler_params=_cparams("parallel"))(up, up, fw, fw, fb, fb)


def _ffn_bwd(up, dact, fw, fb, name):
    tp, two_ff = up.shape
    dff = two_ff // 2
    tc = FFN_TILE
    nj = dff // tc

    def body(ua_ref, uv_ref, da_ref, wa_ref, wv_ref, ba_ref, bv_ref,
             dup_ref, dwa_ref, dwv_ref, dba_ref, dbv_ref, stage, sem):
        j = pl.program_id(0)
        ua = ua_ref[...]
        uv = uv_ref[...]
        a = _dwconv(ua, wa_ref) + ba_ref[...]
        v = _dwconv(uv, wv_ref) + bv_ref[...]
        sg = _sigmoid(a)
        dact_v = da_ref[...]
        da = dact_v * v * sg * (1.0 + a * (1.0 - sg))
        dv = dact_v * a * sg
        dba_ref[...] = jnp.sum(da, axis=0, keepdims=True)
        dbv_ref[...] = jnp.sum(dv, axis=0, keepdims=True)
        dua, dwa = _dwconv_bwd(ua, da, wa_ref)
        duv, dwv = _dwconv_bwd(uv, dv, wv_ref)
        dwa_ref[...] = dwa
        dwv_ref[...] = dwv
        stage[0] = dua.astype(BF16)
        stage[1] = duv.astype(BF16)
        copies = [pltpu.make_async_copy(stage.at[p], dup_ref.at[:, pl.ds((p * nj + j) * tc, tc)], sem.at[p])
                  for p in range(2)]
        for cp in copies:
            cp.start()
        for cp in copies:
            cp.wait()

    lo = lambda r: pl.BlockSpec((r, tc), lambda j: (0, j))
    hi = lambda r: pl.BlockSpec((r, tc), lambda j: (0, nj + j))
    return pl.pallas_call(
        body, name=name, grid=(nj,),
        in_specs=[lo(tp), hi(tp), lo(tp), lo(3), hi(3), lo(1), hi(1)],
        out_specs=[pl.BlockSpec(memory_space=pl.ANY), lo(3), lo(3), lo(1), lo(1)],
        out_shape=[jax.ShapeDtypeStruct((tp, two_ff), BF16),
                   jax.ShapeDtypeStruct((3, dff), F32), jax.ShapeDtypeStruct((3, dff), F32),
                   jax.ShapeDtypeStruct((1, dff), F32), jax.ShapeDtypeStruct((1, dff), F32)],
        scratch_shapes=[pltpu.VMEM((2, tp, tc), BF16), pltpu.SemaphoreType.DMA((2,))],
        compiler_params=_cparams("arbitrary"))(up, up, dact, fw, fw, fb, fb)


def _zoh(lr, li, ld):
    dt = jnp.exp(ld)
    mag = jnp.exp(lr * dt)
    ang = li * dt
    ar = mag * jnp.cos(ang)
    ai = mag * jnp.sin(ang)
    den = lr * lr + li * li
    nr = ar - 1.0
    fr = (nr * lr + ai * li) / den
    fi = (ai * lr - nr * li) / den
    return dt, ar, ai, den, nr, fr, fi


def _s5_prep(lr, li, ld, b_re, b_im, name):
    nstate = lr.shape[1]

    def tables(tab_ref, ar, ai, reverse):
        pows = [(ar, ai)]
        for _ in range(SUBLANES - 1):
            pr, pi = pows[-1]
            pows.append((pr * ar - pi * ai, pr * ai + pi * ar))
        row = lax.broadcasted_iota(jnp.int32, (SUBLANES, nstate), 0)
        for level, k in enumerate((1, 2, 4)):
            mask = (row <= SUBLANES - 1 - k) if reverse else (row >= k)
            tab_ref[2 * level] = jnp.where(mask, pows[k - 1][0], 0.0)
            tab_ref[2 * level + 1] = jnp.where(mask, pows[k - 1][1], 0.0)
        pr = jnp.zeros((SUBLANES, nstate), F32)
        pi = jnp.zeros((SUBLANES, nstate), F32)
        for t in range(SUBLANES):
            k = SUBLANES - 1 - t if reverse else t
            pr = jnp.where(row == t, pows[k][0], pr)
            pi = jnp.where(row == t, pows[k][1], pi)
        tab_ref[6] = pr
        tab_ref[7] = pi

    def body(lr_ref, li_ref, ld_ref, bre_ref, bim_ref, tabf_ref, tabr_ref, bcre_ref, bcim_ref):
        _, ar, ai, _, _, fr, fi = _zoh(lr_ref[...], li_ref[...], ld_ref[...])
        tables(tabf_ref, ar, ai, False)
        tables(tabr_ref, ar, -ai, True)
        bre = bre_ref[...]
        bim = bim_ref[...]
        bcre_ref[...] = (fr * bre - fi * bim).astype(BF16)
        bcim_ref[...] = (fr * bim + fi * bre).astype(BF16)

    vmem = pl.BlockSpec(memory_space=pltpu.VMEM)
    return pl.pallas_call(
        body, name=name, in_specs=[vmem] * 5, out_specs=[vmem] * 4,
        out_shape=[jax.ShapeDtypeStruct((8, SUBLANES, nstate), F32)] * 2
        + [jax.ShapeDtypeStruct(b_re.shape, BF16)] * 2)(lr, li, ld, b_re, b_im)


def _s5_prep_bwd(lr, li, ld, b_re, b_im, da_re, da_im, dbc_re, dbc_im, name):
    def body(lr_ref, li_ref, ld_ref, bre_ref, bim_ref, dar_ref, dai_ref, dbcre_ref, dbcim_ref,
             dlr_ref, dli_ref, dld_ref, dbre_ref, dbim_ref):
        lr, li = lr_ref[...], li_ref[...]
        dt, ar, ai, den, nr, fr, fi = _zoh(lr, li, ld_ref[...])
        bre, bim = bre_ref[...], bim_ref[...]
        gre, gim = dbcre_ref[...], dbcim_ref[...]
        dbre_ref[...] = fr * gre + fi * gim
        dbim_ref[...] = fr * gim - fi * gre
        g_fr = jnp.sum(gre * bre + gim * bim, axis=0, keepdims=True)
        g_fi = jnp.sum(gim * bre - gre * bim, axis=0, keepdims=True)
        g_ar = dar_ref[...] + (g_fr * lr - g_fi * li) / den
        g_ai = dai_ref[...] + (g_fr * li + g_fi * lr) / den
        d_lr = (g_fr * (nr - 2.0 * fr * lr) + g_fi * (ai - 2.0 * fi * lr)) / den
        d_li = (g_fr * (ai - 2.0 * fr * li) - g_fi * (nr + 2.0 * fi * li)) / den
        g_logmag = g_ar * ar + g_ai * ai
        g_ang = g_ai * ar - g_ar * ai
        dlr_ref[...] = d_lr + g_logmag * dt
        dli_ref[...] = d_li + g_ang * dt
        d_ld = (g_logmag * lr + g_ang * li) * dt
        n = d_ld.shape[1]
        sh = 1
        while sh < STATE:
            d_ld = d_ld + pltpu.roll(d_ld, n - sh, 1)
            sh *= 2
        dld_ref[...] = d_ld

    vmem = pl.BlockSpec(memory_space=pltpu.VMEM)
    row = jax.ShapeDtypeStruct(lr.shape, F32)
    return pl.pallas_call(
        body, name=name, in_specs=[vmem] * 9, out_specs=[vmem] * 5,
        out_shape=[row, row, row, jax.ShapeDtypeStruct(b_re.shape, F32), jax.ShapeDtypeStruct(b_re.shape, F32)],
    )(lr, li, ld, b_re, b_im, da_re, da_im, dbc_re, dbc_im)


def _compact_b(bb):
    bq = bb.reshape(N_GROUPS // 8, 8, STATE, GROUP)
    m = jnp.einsum("ab,qbph->qahbp", jnp.eye(8, dtype=bb.dtype), bq).reshape(N_GROUPS // 8, LANES, 8 * STATE)
    return m.transpose(1, 0, 2).reshape(LANES, N_GROUPS * STATE)


def _expand_b(m):
    d = m.reshape(8, GROUP, N_GROUPS // 8, 8, STATE)
    return jnp.einsum("ahqap->qahp", d).reshape(N_GROUPS, GROUP, STATE)


def _compact_c(c):
    cq = c.reshape(N_GROUPS // 8, 8, GROUP, STATE)
    return jnp.einsum("ab,qbhp->qbpah", jnp.eye(8, dtype=c.dtype), cq).reshape(N_GROUPS * STATE, LANES)


def _expand_c(m):
    d = m.reshape(N_GROUPS // 8, 8, STATE, 8, GROUP)
    return jnp.einsum("qbpbh->qbhp", d).reshape(N_GROUPS, GROUP, STATE)


def _local_step(x, target, p, ex):
    seq, d = x.shape
    n_real = N_META + seq
    tp = -(-n_real // ROW_ALIGN) * ROW_ALIGN

    nstate = N_GROUPS * STATE
    s5 = (p["ssm_lam_re"].reshape(1, nstate), p["ssm_lam_im"].reshape(1, nstate),
          jnp.repeat(p["ssm_log_dt"].reshape(-1), STATE).reshape(1, nstate),
          _compact_b(p["ssm_b_re"]), _compact_b(p["ssm_b_im"]))
    tab_f, tab_r, bc_re, bc_im = _s5_prep(*s5, "s5_prep")
    cc_re = _compact_c(p["ssm_c_re"]).astype(BF16)
    cc_im = _compact_c(p["ssm_c_im"]).astype(BF16)
    dskip = p["ssm_d"].reshape(1, -1)
    dh = dskip.shape[1]

    h0, hn1 = _input_norm_fwd(x, p["meta_tokens"], p["norm_mix_g"] + ex.zero, tp, "norm_mix")
    first = ex.weights("first", hn1)
    proj = _mm(hn1, first["w_in"], "nn", "proj")
    started = ex.forward("mid", proj)
    co, y, g = _seq_fwd(proj, p["conv_w"] + started[0, 0], bc_re, bc_im, cc_re, cc_im, dskip, tab_f, "seq_fwd")
    mid = ex.weights("mid", g)
    z = _mm(g, mid["ssm_w_glu"], "nn", "glu")
    mixed = _mix_fwd(co, y, z, p["gain_conv_out"], p["gain_ssm_out"], "mix_fwd")
    started = ex.forward("late", mixed)
    h1, hn2 = _proj_res_norm(mixed, mid["w_out"], h0, p["norm_ffn_g"], started, "out_proj_norm")
    late = ex.weights("late", hn2)
    up = _mm(hn2, late["w_up"], "nn", "up_proj")
    act = _ffn_act(up, p["ffn_conv_w"], p["ffn_conv_b"], "ffn_act")
    loss, dh2, dh2b, d_gfin = _proj_loss_bwd(act, late["w_down"], h1, target, p["norm_final_g"], n_real,
                                             "down_proj_loss")

    g_w_down = _mm(act, dh2b, "tn", "g_w_down")
    dact = _mm(dh2b, late["w_down"], "nt", "d_act")
    dup, dfw_a, dfw_v, dfb_a, dfb_v = _ffn_bwd(up, dact, p["ffn_conv_w"], p["ffn_conv_b"], "ffn_bwd")
    g_w_up = _mm(hn2, dup, "tn", "g_w_up")
    started = ex.grads_ready("late", {"w_up": g_w_up, "w_down": g_w_down})
    dh1, dh1b, d_gffn = _proj_norm_bwd(dup, late["w_up"], h1, p["norm_ffn_g"], dh2, started, "d_hn2_norm_bwd")
    started = ex.grads_send("late", dh1)
    g_w_out = _mm(mixed, dh1b, "tn", "g_w_out", after=started)
    dco, dz, dgp, d_gc, d_gs = _proj_mix_bwd(dh1b, mid["w_out"], co, y, z, p["gain_conv_out"],
                                             p["gain_ssm_out"], "d_mixed_mix_bwd")
    g_w_glu = _mm(g, dz, "tn", "g_w_glu")
    started = ex.grads_ready("mid", {"ssm_w_glu": g_w_glu, "w_out": g_w_out})
    dg = _mm(dz, mid["ssm_w_glu"], "nt", "d_gelu", acc_in=dgp, after=started)
    started = ex.grads_send("mid", dg)
    dproj, d_conv_w = _conv_bwd(proj, dco, p["conv_w"] + started[0, 0], "conv_bwd")
    (dproj, dbc_re, dbc_im, dcc_re, dcc_im, d_dskip, da_re, da_im) = _ssm_bwd(
        proj, y, dg, dproj, bc_re, bc_im, cc_re, cc_im, dskip, tab_f, tab_r, "ssm_bwd")
    g_w_in = _mm(hn1, dproj, "tn", "g_w_in")
    started = ex.grads_ready("first", {"w_in": g_w_in})
    dhn1 = _mm(dproj, first["w_in"], "nt", "d_hn1", after=started)
    started = ex.grads_send("first", dhn1)
    grad_x, d_meta, d_gmix = _input_norm_bwd(h0, p["norm_mix_g"] + started[0, 0], dhn1, dh1, n_real, "norm_mix_bwd")

    d_lam_re, d_lam_im, d_log_dt, d_b_re, d_b_im = _s5_prep_bwd(*s5, da_re, da_im, dbc_re, dbc_im, "s5_prep_bwd")
    d_lam_re, d_lam_im = d_lam_re.reshape(N_GROUPS, STATE), d_lam_im.reshape(N_GROUPS, STATE)
    d_log_dt = d_log_dt[0, ::STATE]
    d_b_re, d_b_im = _expand_b(d_b_re), _expand_b(d_b_im)
    grads = {
        "meta_tokens": d_meta, "norm_mix_g": d_gmix, "w_in": g_w_in, "conv_w": d_conv_w,
        "ssm_lam_re": d_lam_re, "ssm_lam_im": d_lam_im, "ssm_log_dt": d_log_dt,
        "ssm_b_re": d_b_re, "ssm_b_im": d_b_im, "ssm_c_re": _expand_c(dcc_re), "ssm_c_im": _expand_c(dcc_im),
        "ssm_d": d_dskip.reshape(N_GROUPS, GROUP), "ssm_w_glu": g_w_glu,
        "gain_conv_out": d_gc, "gain_ssm_out": d_gs, "w_out": g_w_out, "norm_ffn_g": d_gffn,
        "w_up": g_w_up, "ffn_conv_w": jnp.concatenate([dfw_a, dfw_v], axis=1),
        "ffn_conv_b": jnp.concatenate([dfb_a, dfb_v], axis=1), "w_down": g_w_down, "norm_final_g": d_gfin,
    }
    return loss[0, 0], grad_x, grads


def _view(ref, axis, start, size):
    idx = [slice(None)] * len(ref.shape)
    idx[axis] = pl.ds(start, size)
    return ref.at[tuple(idx)]


def _exchange(name, ins, outs, aliases, local_copies, remote_copies):
    ni, no = len(ins), len(outs)
    nl, nr = len(local_copies), len(remote_copies)

    def body(*refs):
        in_refs, out_refs = refs[:ni], refs[ni:ni + no]
        send_sems, recv_sems, local_sems = refs[ni + no:]
        x, y, c = lax.axis_index("x"), lax.axis_index("y"), lax.axis_index("c")
        pos = (x, y, c, 2 * x + y)
        locals_ = [pltpu.make_async_copy(s(in_refs, out_refs, pos), d(in_refs, out_refs, pos), local_sems.at[i])
                   for i, (s, d) in enumerate(local_copies)]
        remotes = []
        for i, (s, d, flip) in enumerate(remote_copies):
            peer = (1 - x if "x" in flip else x, 1 - y if "y" in flip else y, 1 - c if "c" in flip else c)
            remotes.append(pltpu.make_async_remote_copy(
                src_ref=s(in_refs, out_refs, pos), dst_ref=d(in_refs, out_refs, pos),
                send_sem=send_sems.at[i], recv_sem=recv_sems.at[i], device_id=peer, device_id_type=MESH))
        for cp in locals_ + remotes:
            cp.start()
        for cp in remotes:
            cp.wait_recv()
        for cp in remotes:
            cp.wait_send()
        for cp in locals_:
            cp.wait()

    hbm = pl.BlockSpec(memory_space=pl.ANY)
    return pl.pallas_call(
        body, name=name, in_specs=[hbm] * ni, out_specs=[hbm] * no, out_shape=outs,
        input_output_aliases=aliases,
        scratch_shapes=[pltpu.SemaphoreType.DMA((nr,)), pltpu.SemaphoreType.DMA((nr,)),
                        pltpu.SemaphoreType.DMA((max(nl, 1),))],
    )(*ins)


BIG = {"w_in": (0, 1), "ssm_w_glu": (1, 0), "w_out": (1, 0), "w_up": (0, 1), "w_down": (1, 0)}
BIG_NAMES = tuple(BIG)
FLIPS = ("y", "x", "xy")


def _peer_chip(pos, flip):
    x, y, _, _ = pos
    return 2 * (1 - x if "x" in flip else x) + (1 - y if "y" in flip else y)


def _block_rows(rows, cols, itemsize, mult):
    return _pick_tile(rows, max(mult, (2 * 1024 * 1024) // (cols * itemsize)), mult)


def _cast_into_full(w, kc, shard_axis, name):
    r, cdim = w.shape
    tr = _block_rows(r, cdim, 4, 16)
    nb = r // tr

    def body(kc_ref, w_ref, o_ref):
        o_ref[...] = w_ref[...].astype(BF16)

    if shard_axis == 1:
        full, o_spec = (r, 4 * cdim), pl.BlockSpec((tr, cdim), lambda i, kc: (i, kc[0]))
    else:
        full, o_spec = (4 * r, cdim), pl.BlockSpec((tr, cdim), lambda i, kc: (kc[0] * nb + i, 0))
    return pl.pallas_call(
        body, name=name,
        grid_spec=pltpu.PrefetchScalarGridSpec(
            num_scalar_prefetch=1, grid=(nb,), in_specs=[pl.BlockSpec((tr, cdim), lambda i, kc: (i, 0))],
            out_specs=o_spec),
        out_shape=jax.ShapeDtypeStruct(full, BF16), compiler_params=_cparams("parallel"))(kc, w)


def _pair_sum(g, recv, kc, half_axis, name, out_dtype):
    hr, hc = recv.shape
    tr = _block_rows(hr, hc, 4, 16)
    nb = hr // tr

    def body(kc_ref, g_ref, r_ref, o_ref):
        o_ref[...] = (g_ref[...] + r_ref[...]).astype(out_dtype)

    if half_axis == 0:
        g_spec = pl.BlockSpec((tr, hc), lambda i, kc: (kc[1] * nb + i, 0))
    elif half_axis == 1:
        g_spec = pl.BlockSpec((tr, hc), lambda i, kc: (i, kc[1]))
    else:
        g_spec = pl.BlockSpec((tr, hc), lambda i, kc: (i, 0))
    same = pl.BlockSpec((tr, hc), lambda i, kc: (i, 0))
    return pl.pallas_call(
        body, name=name,
        grid_spec=pltpu.PrefetchScalarGridSpec(num_scalar_prefetch=1, grid=(nb,), in_specs=[g_spec, same],
                                               out_specs=same),
        out_shape=jax.ShapeDtypeStruct((hr, hc), out_dtype), compiler_params=_cparams("parallel"))(kc, g, recv)


def _chip_sum(own, recv, kc, own_axis, out_axis, name):
    _, sr, sc = recv.shape
    tr = _block_rows(sr, sc, 4, 16)
    nb = sr // tr

    def body(kc_ref, o_ref, r_ref, t_ref):
        k = kc_ref[0]
        own_v = o_ref[...].astype(F32)
        r = [r_ref[m].astype(F32) for m in range(3)]
        terms = []
        for kk in range(4):
            m = jnp.bitwise_xor(k, kk)
            terms.append(jnp.where(m == 0, own_v, jnp.where(m == 1, r[0], jnp.where(m == 2, r[1], r[2]))))
        t_ref[...] = (terms[0] + terms[1]) + (terms[2] + terms[3])

    if own_axis == 0:
        own_spec = pl.BlockSpec((tr, sc), lambda i, kc: (kc[0] * nb + i, 0))
    elif own_axis == 1:
        own_spec = pl.BlockSpec((tr, sc), lambda i, kc: (i, kc[0]))
    else:
        own_spec = pl.BlockSpec((tr, sc), lambda i, kc: (kc[1] * nb + i, 0))
    if out_axis == 0:
        out_full, out_spec = (2 * sr, sc), pl.BlockSpec((tr, sc), lambda i, kc: (kc[1] * nb + i, 0))
    else:
        out_full, out_spec = (sr, 2 * sc), pl.BlockSpec((tr, sc), lambda i, kc: (i, kc[1]))
    return pl.pallas_call(
        body, name=name,
        grid_spec=pltpu.PrefetchScalarGridSpec(
            num_scalar_prefetch=1, grid=(nb,),
            in_specs=[own_spec, pl.BlockSpec((3, tr, sc), lambda i, kc: (0, i, 0))],
            out_specs=out_spec),
        out_shape=jax.ShapeDtypeStruct(out_full, F32), compiler_params=_cparams("parallel"))(kc, own, recv)


def _adamw(w, g, m, v, name):
    r, cdim = w.shape
    tr = _block_rows(r, cdim, 4, 8)
    c1 = 1.0 - ADAM_B1 ** ADAM_STEP
    c2 = 1.0 - ADAM_B2 ** ADAM_STEP

    def body(w_ref, g_ref, m_ref, v_ref, go_ref, d_ref, nm_ref, nv_ref):
        gv = g_ref[...]
        go_ref[...] = gv
        nm = ADAM_B1 * m_ref[...] + (1.0 - ADAM_B1) * gv
        nv = ADAM_B2 * v_ref[...] + (1.0 - ADAM_B2) * (gv * gv)
        d_ref[...] = -ADAM_LR * ((nm / c1) / (jnp.sqrt(nv / c2) + ADAM_EPS) + ADAM_WD * w_ref[...])
        nm_ref[...] = nm
        nv_ref[...] = nv

    spec = _rows(cdim, tr)
    return pl.pallas_call(body, name=name, grid=(r // tr,), in_specs=[spec] * 4, out_specs=[spec] * 4,
                          out_shape=[jax.ShapeDtypeStruct((r, cdim), F32)] * 4,
                          compiler_params=_cparams("parallel"))(w, g, m, v)


def _adamw_whole(ws, gs, ms, vs, name):
    n = len(ws)
    c1 = 1.0 - ADAM_B1 ** ADAM_STEP
    c2 = 1.0 - ADAM_B2 ** ADAM_STEP

    def body(*refs):
        for i in range(n):
            w_ref, g_ref, m_ref, v_ref, d_ref, nm_ref, nv_ref = [refs[j * n + i] for j in range(7)]
            gv = g_ref[...]
            nm = ADAM_B1 * m_ref[...] + (1.0 - ADAM_B1) * gv
            nv = ADAM_B2 * v_ref[...] + (1.0 - ADAM_B2) * (gv * gv)
            d_ref[...] = -ADAM_LR * ((nm / c1) / (jnp.sqrt(nv / c2) + ADAM_EPS) + ADAM_WD * w_ref[...])
            nm_ref[...] = nm
            nv_ref[...] = nv

    vmem = pl.BlockSpec(memory_space=pltpu.VMEM)
    out = pl.pallas_call(body, name=name, in_specs=[vmem] * (4 * n), out_specs=[vmem] * (3 * n),
                         out_shape=[jax.ShapeDtypeStruct(a.shape, F32) for a in ws] * 3,
                         compiler_params=pltpu.CompilerParams(vmem_limit_bytes=VMEM_LIMIT))(*ws, *gs, *ms, *vs)
    return out[:n], out[n:2 * n], out[2 * n:]


SIDE_EFFECT = pltpu.SideEffectType.DATAFLOW_SIDE_EFFECTING


def _descriptors(copies, refs, send_sems, recv_sems, sem_off=0):
    x, y, c = lax.axis_index("x"), lax.axis_index("y"), lax.axis_index("c")
    pos = (x, y, c, 2 * x + y)
    out = []
    for i, (s, d, flip) in enumerate(copies):
        peer = (1 - x if "x" in flip else x, 1 - y if "y" in flip else y, 1 - c if "c" in flip else c)
        out.append(pltpu.make_async_remote_copy(
            src_ref=s(refs, refs, pos), dst_ref=d(refs, refs, pos),
            send_sem=send_sems.at[sem_off + i], recv_sem=recv_sems.at[sem_off + i],
            device_id=peer, device_id_type=MESH))
    return out


def _shifted(copies, off):
    return [(lambda I, O, pos, s=s: s(I[off:], O[off:], pos), lambda I, O, pos, d=d: d(I[off:], O[off:], pos), flip)
            for s, d, flip in copies]


BARRIER_IDS = {"c": (1, 2), "ici": (3, 4)}


def _exchange_start(name, bufs, copies, turns, after=None):
    n, nr = len(bufs), len(copies)
    na = 0 if after is None else 1
    flips = sorted({flip for _, _, flip in copies})
    kind = "c" if flips == ["c"] else "ici"
    collective_id = BARRIER_IDS[kind][turns[kind] % 2]
    turns[kind] += 1

    def body(*refs):
        x, y, c = lax.axis_index("x"), lax.axis_index("y"), lax.axis_index("c")
        barrier = pltpu.get_barrier_semaphore()
        for flip in flips:
            peer = (1 - x if "x" in flip else x, 1 - y if "y" in flip else y, 1 - c if "c" in flip else c)
            pl.semaphore_signal(barrier, inc=1, device_id=peer, device_id_type=MESH)
        pl.semaphore_wait(barrier, len(flips))
        for cp in _descriptors(copies, refs[:n], refs[n + na], refs[n + na + 1]):
            cp.start()
        token = refs[2 * n + na + 2]
        token[...] = jnp.zeros_like(token)

    hbm = pl.BlockSpec(memory_space=pltpu.HBM)
    sem = pl.BlockSpec(memory_space=pltpu.SEMAPHORE)
    out = pl.pallas_call(
        body, name=name,
        in_specs=[hbm] * n + [pl.BlockSpec(memory_space=pl.ANY)] * na,
        out_specs=(sem, sem, *[hbm] * n, pl.BlockSpec(memory_space=pltpu.VMEM)),
        out_shape=(pltpu.SemaphoreType.DMA((nr,)), pltpu.SemaphoreType.DMA((nr,)),
                   *[pltpu.HBM(b.shape, b.dtype) for b in bufs], jax.ShapeDtypeStruct((SUBLANES, LANES), F32)),
        input_output_aliases={i: 2 + i for i in range(n)},
        compiler_params=pltpu.CompilerParams(has_side_effects=SIDE_EFFECT, collective_id=collective_id),
    )(*[pltpu.with_memory_space_constraint(b, pltpu.HBM) for b in bufs], *([after] * na))
    return out[0], out[1], list(out[2:2 + n]), out[2 + n]


def _exchange_wait(name, send_sems, recv_sems, bufs, copies, after, sem_off=0):
    n = len(bufs)

    def body(*refs):
        for cp in _descriptors(copies, refs[:n], refs[n], refs[n + 1], sem_off):
            cp.wait_send()
            cp.wait_recv()

    hbm = pl.BlockSpec(memory_space=pltpu.HBM)
    sem = pl.BlockSpec(memory_space=pltpu.SEMAPHORE)
    out = pl.pallas_call(
        body, name=name,
        in_specs=[hbm] * n + [sem, sem, pl.BlockSpec(memory_space=pl.ANY)],
        out_specs=tuple([hbm] * n),
        out_shape=tuple(pltpu.HBM(b.shape, b.dtype) for b in bufs),
        input_output_aliases={i: i for i in range(n)},
        compiler_params=pltpu.CompilerParams(has_side_effects=SIDE_EFFECT),
    )(*bufs, send_sems, recv_sems, after)
    return list(out)


FIRST = ("w_in",)
MID = ("ssm_w_glu", "w_out")
LATE = ("w_up", "w_down")
GROUPS = {"first": FIRST, "mid": MID, "late": LATE}


def _gather_copies(names, shard_shapes):
    def region(i, chip, c):
        half_axis, shard_axis = BIG[names[i]]
        ssize = shard_shapes[i][shard_axis]
        hsize = shard_shapes[i][half_axis] // 2
        return lambda ref: _view(_view(ref, shard_axis, chip * ssize, ssize), half_axis, c * hsize, hsize)

    ici, d2d = [], []
    for i in range(len(names)):
        for flip in FLIPS:
            ici.append((lambda I, O, pos, i=i: region(i, pos[3], pos[2])(I[i]),
                        lambda I, O, pos, i=i: region(i, pos[3], pos[2])(O[i]), flip))
            d2d.append((lambda I, O, pos, i=i, flip=flip: region(i, _peer_chip(pos, flip), pos[2])(I[i]),
                        lambda I, O, pos, i=i, flip=flip: region(i, _peer_chip(pos, flip), pos[2])(O[i]), "c"))
    return ici, d2d


def _half_shape(n, shape):
    r, cdim = shape
    return (r // 2, cdim) if BIG[n][0] == 0 else (r, cdim // 2)


def _sub_shape(n, shape):
    hr, hc = _half_shape(n, shape)
    return (hr, hc // 4) if BIG[n][1] == 1 else (hr // 4, hc)


def _pair_copies(names, shapes, with_pack, dst_off):
    n = len(names)

    def other_half(i, ref, pos):
        half_axis = BIG[names[i]][0]
        hsize = shapes[i][half_axis] // 2
        return _view(ref, half_axis, (1 - pos[2]) * hsize, hsize)

    copies = [(lambda I, O, pos, i=i: other_half(i, I[i], pos), lambda I, O, pos, i=i: O[dst_off + i], "c")
              for i in range(n)]
    if with_pack:
        copies.append((lambda I, O, pos: I[n], lambda I, O, pos: O[dst_off + n], "c"))
    return copies


def _chip_copies(names, shapes, pack_rows, dst_off):
    n = len(names)

    def piece(i, ref, chip):
        shard_axis = BIG[names[i]][1]
        ssize = _sub_shape(names[i], shapes[i])[shard_axis]
        return _view(ref, shard_axis, chip * ssize, ssize)

    copies = []
    for i in range(n):
        for slot, flip in enumerate(FLIPS):
            copies.append((lambda I, O, pos, i=i, flip=flip: piece(i, I[i], _peer_chip(pos, flip)),
                           lambda I, O, pos, i=i, slot=slot: O[dst_off + i].at[slot], flip))
    if pack_rows:
        for slot, flip in enumerate(FLIPS):
            copies.append((lambda I, O, pos: _view(I[n], 0, pos[2] * (pack_rows // 2), pack_rows // 2),
                           lambda I, O, pos, slot=slot: O[dst_off + n].at[slot], flip))
    return copies


class _Exchanges:
    def __init__(self, shards, tiny, kc):
        self.kc = kc
        wb = {n: _cast_into_full(shards[n], kc, BIG[n][1], "cast_" + n) for n in BIG_NAMES}
        self.gathering, self.forwarding, self.pairing, self.reducing = {}, {}, {}, {}
        self.turns = {"c": 0, "ici": 0}
        tiny_copies = [(lambda I, O, pos: I[0], lambda I, O, pos: O[1].at[pos[3]], flip) for flip in FLIPS]
        self.gathering["tiny"] = (0, 0, 2, tiny_copies, None)
        bufs, copies = [tiny, lax.empty((4,) + tiny.shape, F32)], list(tiny_copies)
        for group, names in GROUPS.items():
            ici, d2d = _gather_copies(names, [shards[n].shape for n in names])
            self.gathering[group] = (len(bufs), len(copies), len(names), ici, d2d)
            copies += _shifted(ici, len(bufs))
            bufs += [wb[n] for n in names]
        self.started = _exchange_start("gather_start", bufs, copies, self.turns)
        self.zero = self.started[3][0, 0]

    def _arrived(self, group, after):
        buf_off, sem_off, n, ici, _ = self.gathering[group]
        send_sems, recv_sems, bufs, _ = self.started
        return _exchange_wait("gather_%s_wait" % group, send_sems, recv_sems, bufs[buf_off:buf_off + n], ici, after,
                              sem_off)

    def small_params(self, kc):
        tiny, got = self._arrived("tiny", self.started[3])
        return lax.dynamic_update_index_in_dim(got, tiny, kc[0], 0)

    def forward(self, group, after):
        d2d = self.gathering[group][4]
        self.forwarding[group] = (_exchange_start("forward_%s_start" % group, self._arrived(group, after), d2d,
                                                  self.turns), d2d)
        return self.forwarding[group][0][3]

    def weights(self, group, after):
        if group not in self.forwarding:
            after = self.forward(group, after)
        (send_sems, recv_sems, bufs, _), d2d = self.forwarding[group]
        full = _exchange_wait("forward_%s_wait" % group, send_sems, recv_sems, bufs, d2d, after)
        return dict(zip(GROUPS[group], full))

    def grads_ready(self, group, grads):
        names = GROUPS[group]
        gs = [grads[n] for n in names]
        land = [lax.empty(_half_shape(n, g.shape), F32) for n, g in zip(names, gs)]
        copies = _pair_copies(names, [g.shape for g in gs], False, len(names))
        started = _exchange_start("pair_%s_start" % group, gs + land, copies, self.turns)
        self.pairing[group] = (started, copies)
        return started[3]

    def grads_send(self, group, after):
        names = GROUPS[group]
        n = len(names)
        (send_sems, recv_sems, bufs, _), copies = self.pairing[group]
        bufs = _exchange_wait("pair_%s_wait" % group, send_sems, recv_sems, bufs, copies, after)
        chip = [_pair_sum(bufs[i], bufs[n + i], self.kc, BIG[names[i]][0], "pair_sum_" + names[i], BF16)
                for i in range(n)]
        shapes = [bufs[i].shape for i in range(n)]
        land = [lax.empty((3,) + _sub_shape(names[i], shapes[i]), BF16) for i in range(n)]
        copies = _chip_copies(names, shapes, 0, n)
        started = _exchange_start("reduce_%s_start" % group, chip + land, copies, self.turns)
        self.reducing[group] = (started, copies)
        return started[3]

    def finish_pack(self, pack):
        kc = self.kc
        prow = pack.shape[0] // 2
        recv = _exchange("reduce_d2d", [pack], [jax.ShapeDtypeStruct(pack.shape, F32)], {}, [],
                         _pair_copies((), [], True, 0))
        chip_pack = _pair_sum(pack, recv[0], kc, None, "pair_sum_pack", F32)
        copies = _chip_copies((), [], pack.shape[0], 1)
        land = lax.empty((3, prow, pack.shape[1]), F32)
        pack_sems_s, pack_sems_r, pack_bufs, after = _exchange_start("reduce_pack_start", [chip_pack, land], copies,
                                                                     self.turns)

        names, chips, recvs = (), [], []
        for group, group_names in GROUPS.items():
            (send_sems, recv_sems, bufs, _), group_copies = self.reducing[group]
            bufs = _exchange_wait("reduce_%s_wait" % group, send_sems, recv_sems, bufs, group_copies, after)
            n = len(group_names)
            names, chips, recvs = names + group_names, chips + bufs[:n], recvs + bufs[n:]
            after = bufs[n]
        total = [_chip_sum(chips[i], recvs[i], kc, BIG[n][1], BIG[n][0], "chip_sum_" + n)
                 for i, n in enumerate(names)]

        def my_half(half_axis, ref, pos):
            hsize = ref.shape[half_axis] // 2
            return _view(ref, half_axis, pos[2] * hsize, hsize)

        swap = [(lambda I, O, pos, i=i, n=n: my_half(BIG[n][0], I[i], pos),
                 lambda I, O, pos, i=i, n=n: my_half(BIG[n][0], O[i], pos), "c") for i, n in enumerate(names)]
        self.swapping = (_exchange_start("swap_start", total, swap, self.turns), swap, names)

        chip_pack, recv_pack = _exchange_wait("reduce_pack_wait", pack_sems_s, pack_sems_r, pack_bufs, copies,
                                              self.swapping[0][3])
        total_pack = _chip_sum(chip_pack, recv_pack, kc, None, 0, "chip_sum_pack")
        swap = [(lambda I, O, pos: my_half(0, I[0], pos), lambda I, O, pos: my_half(0, O[0], pos), "c")]
        return _exchange("swap_pack", [total_pack], [jax.ShapeDtypeStruct(pack.shape, F32)], {0: 0}, [], swap)[0]

    def finish_big(self, after):
        (send_sems, recv_sems, bufs, _), swap, names = self.swapping
        return dict(zip(names, _exchange_wait("swap_wait", send_sems, recv_sems, bufs, swap, after)))


WEIGHTS = ("meta_tokens", "norm_mix_g", "w_in", "conv_w", "ssm_lam_re", "ssm_lam_im", "ssm_log_dt", "ssm_b_re",
           "ssm_b_im", "ssm_c_re", "ssm_c_im", "ssm_d", "ssm_w_glu", "gain_conv_out", "gain_ssm_out", "w_out",
           "norm_ffn_g", "w_up", "ffn_conv_w", "ffn_conv_b", "w_down", "norm_final_g")
TINY_SHARDED = ("meta_tokens", "conv_w", "ffn_conv_w")
REPLICATED = tuple(n for n in WEIGHTS if n not in BIG and n not in TINY_SHARDED)
PACK_COLS = 512


def _pack(arrays, row_mult, cols):
    flat = jnp.concatenate([a.reshape(-1).astype(F32) for a in arrays])
    n = flat.shape[0]
    total = -(-n // (row_mult * cols)) * (row_mult * cols)
    return jnp.concatenate([flat, jnp.zeros((total - n,), F32)]).reshape(total // cols, cols)


def _unpack(packed, shapes):
    flat = packed.reshape(-1)
    out, off = [], 0
    for s in shapes:
        n = math.prod(s)
        out.append(flat[off:off + n].reshape(s))
        off += n
    return out


def kernel(x, meta_tokens, norm_mix_g, w_in, conv_w, ssm_lam_re, ssm_lam_im, ssm_log_dt, ssm_b_re, ssm_b_im, ssm_c_re, ssm_c_im, ssm_d, ssm_w_glu, gain_conv_out, gain_ssm_out, w_out, norm_ffn_g, w_up, ffn_conv_w, ffn_conv_b, w_down, norm_final_g, loss_target, m_meta_tokens, m_norm_mix_g, m_w_in, m_conv_w, m_ssm_lam_re, m_ssm_lam_im, m_ssm_log_dt, m_ssm_b_re, m_ssm_b_im, m_ssm_c_re, m_ssm_c_im, m_ssm_d, m_ssm_w_glu, m_gain_conv_out, m_gain_ssm_out, m_w_out, m_norm_ffn_g, m_w_up, m_ffn_conv_w, m_ffn_conv_b, m_w_down, m_norm_final_g, v_meta_tokens, v_norm_mix_g, v_w_in, v_conv_w, v_ssm_lam_re, v_ssm_lam_im, v_ssm_log_dt, v_ssm_b_re, v_ssm_b_im, v_ssm_c_re, v_ssm_c_im, v_ssm_d, v_ssm_w_glu, v_gain_conv_out, v_gain_ssm_out, v_w_out, v_norm_ffn_g, v_w_up, v_ffn_conv_w, v_ffn_conv_b, v_w_down, v_norm_final_g):
    args = dict(locals())
    w = {n: args[n] for n in WEIGHTS}
    mom = {n: args["m_" + n] for n in WEIGHTS}
    var = {n: args["v_" + n] for n in WEIGHTS}
    kx, ky, kc_ = lax.axis_index("x"), lax.axis_index("y"), lax.axis_index("c")
    chip = 2 * kx + ky
    kc = jnp.stack([chip, kc_]).astype(jnp.int32)

    def squeeze(n, a):
        if n == "meta_tokens":
            return a
        if n == "norm_final_g":
            return a.reshape(1, -1)
        a = a[0]
        return a.reshape(1, -1) if a.ndim == 1 else a

    wl = {n: squeeze(n, w[n]) for n in WEIGHTS}
    ml = {n: squeeze(n, mom[n]) for n in WEIGHTS}
    vl = {n: squeeze(n, var[n]) for n in WEIGHTS}

    tiny = _pack([wl[n] for n in TINY_SHARDED], SUBLANES, LANES)
    ex = _Exchanges({n: wl[n] for n in BIG_NAMES}, tiny, kc)
    tiny_shapes = [wl[n].shape for n in TINY_SHARDED]
    tiny_all = ex.small_params(kc)
    tiny_parts = [_unpack(tiny_all[k], tiny_shapes) for k in range(4)]
    p = {n: wl[n] for n in WEIGHTS if n not in BIG}
    for j, n in enumerate(TINY_SHARDED):
        p[n] = jnp.concatenate([tiny_parts[k][j] for k in range(4)], axis=1)
    p["ssm_log_dt"] = wl["ssm_log_dt"].reshape(-1)

    loss_local, grad_x, grads = _local_step(x[0], loss_target[0], p, ex)

    small_names = REPLICATED + TINY_SHARDED
    small_shapes = [tuple(grads[n].shape) for n in small_names] + [(1,)]
    pack = _pack([grads[n] for n in small_names] + [loss_local.reshape(1)], 2 * 16, PACK_COLS)
    g_pack = ex.finish_pack(pack)
    g_small = dict(zip(small_names + ("loss",), _unpack(g_pack, small_shapes)))
    loss = g_small["loss"][0]
    swapped = ("ssm_b_re", "ssm_b_im")

    def view(n, a):
        if n in swapped:
            return jnp.swapaxes(a, -1, -2)
        return a.reshape(1, -1) if a.ndim == 1 else a

    g = {}
    for n in REPLICATED:
        g[n] = g_small[n].reshape(view(n, w[n]).shape)
    for n in TINY_SHARDED:
        cols = wl[n].shape[1]
        g[n] = lax.dynamic_slice_in_dim(g_small[n], chip * cols, cols, axis=1).reshape(w[n].shape)
    delta, new_m, new_v = {}, {}, {}
    small = [[view(n, d[n]) for n in small_names] for d in (w, mom, var)]
    small.insert(1, [g[n] for n in small_names])
    for d, outs in zip((delta, new_m, new_v), _adamw_whole(*small, "adamw_small")):
        d.update(zip(small_names, outs))
    for d in (g, delta, new_m, new_v):
        d.update({n: jnp.swapaxes(d[n], -1, -2) for n in swapped})
    g_big = ex.finish_big(delta[small_names[0]])
    for n in BIG_NAMES:
        g[n], delta[n], new_m[n], new_v[n] = _adamw(wl[n], g_big[n], ml[n], vl[n], "adamw_" + n)

    def like(n, a):
        return a.reshape(w[n].shape)

    return (loss, grad_x[None], *[like(n, g[n]) for n in WEIGHTS], *[like(n, delta[n]) for n in WEIGHTS],
            *[like(n, new_m[n]) for n in WEIGHTS], *[like(n, new_v[n]) for n in WEIGHTS])
```

```python
import functools
import math

import jax
import jax.numpy as jnp
from jax import lax
from jax.experimental import pallas as pl
from jax.experimental.pallas import tpu as pltpu

F32 = jnp.float32
BF16 = jnp.bfloat16
MESH = pl.DeviceIdType.MESH

N_META = 16
N_GROUPS = 32
GROUP = 16
STATE = 64
RMS_EPS = 1e-6
ADAM_LR = 0.001
ADAM_B1 = 0.9
ADAM_B2 = 0.999
ADAM_EPS = 1e-08
ADAM_WD = 0.01
ADAM_STEP = 10

LANES = 128
SUBLANES = 8
ROW_ALIGN = 128
ROW_TILES = 4
VMEM_LIMIT = 52 * 1024 * 1024
MM_VMEM_BUDGET = 40 * 1024 * 1024
GELU_C = math.sqrt(2.0 / math.pi)
GELU_A = 0.044715


def _cparams(*sem):
    return pltpu.CompilerParams(dimension_semantics=sem, vmem_limit_bytes=VMEM_LIMIT)


def _pick_tile(dim, cap, mult):
    best = None
    for t in range(mult, min(dim, cap) + 1, mult):
        if dim % t == 0:
            best = t
    return best if best is not None else dim


def _mm(a, b, mode, name, out_dtype=F32, acc_in=None, after=None):
    if mode == "tn":
        kdim, m = a.shape
    else:
        m, kdim = a.shape
    n = b.shape[0] if mode == "nt" else b.shape[1]
    tm = _pick_tile(m, 1408, LANES if mode == "tn" else 16)
    tk = _pick_tile(kdim, 2816, LANES)
    nk = kdim // tk
    out_bytes = jnp.dtype(out_dtype).itemsize
    for cap in (1408, 1024, 512, 256, LANES):
        tn = _pick_tile(n, cap, LANES)
        blocks = 2 * (tm * tk * 2 + tk * tn * 2 + tm * tn * out_bytes * (2 if acc_in is not None else 1))
        if blocks + (tm * tn * 4 if nk > 1 else 0) <= MM_VMEM_BUDGET:
            break
    has_acc = acc_in is not None

    def body(*refs):
        if after is not None:
            refs = refs[1:]
        if has_acc:
            a_ref, b_ref, c_ref, o_ref = refs[:4]
            rest = refs[4:]
        else:
            a_ref, b_ref, o_ref = refs[:3]
            c_ref = None
            rest = refs[3:]
        if mode == "nn":
            p = jnp.dot(a_ref[...], b_ref[...], preferred_element_type=F32)
        elif mode == "nt":
            p = lax.dot_general(a_ref[...], b_ref[...], (((1,), (1,)), ((), ())), preferred_element_type=F32)
        else:
            p = lax.dot_general(a_ref[...], b_ref[...], (((0,), (0,)), ((), ())), preferred_element_type=F32)
        if nk == 1:
            if has_acc:
                p = p + c_ref[...]
            o_ref[...] = p.astype(out_dtype)
        else:
            acc_ref = rest[0]
            k = pl.program_id(2)

            @pl.when(k == 0)
            def _():
                acc_ref[...] = p + c_ref[...] if has_acc else p

            @pl.when(k > 0)
            def _():
                acc_ref[...] += p

            @pl.when(k == nk - 1)
            def _():
                o_ref[...] = acc_ref[...].astype(out_dtype)

    if mode == "tn":
        a_spec = pl.BlockSpec((tk, tm), lambda i, j, k: (k, i))
    else:
        a_spec = pl.BlockSpec((tm, tk), lambda i, j, k: (i, k))
    if mode == "nt":
        b_spec = pl.BlockSpec((tn, tk), lambda i, j, k: (j, k))
    else:
        b_spec = pl.BlockSpec((tk, tn), lambda i, j, k: (k, j))
    o_spec = pl.BlockSpec((tm, tn), lambda i, j, k: (i, j))
    in_specs = [a_spec, b_spec] + ([o_spec] if has_acc else [])
    args = (a, b) + ((acc_in,) if has_acc else ())
    if after is not None:
        in_specs = [pl.BlockSpec(memory_space=pl.ANY)] + in_specs
        args = (after,) + args
    return pl.pallas_call(
        body, name=name, grid=(m // tm, n // tn, nk),
        in_specs=in_specs, out_specs=o_spec,
        out_shape=jax.ShapeDtypeStruct((m, n), out_dtype),
        scratch_shapes=[pltpu.VMEM((tm, tn), F32)] if nk > 1 else [],
        compiler_params=_cparams("parallel", "parallel", "arbitrary"),
    )(*args)


def _mm_rows(a, b, mode, name, ins, outs, epilogue, scratch=()):
    m, kdim = a.shape
    n = b.shape[0] if mode == "nt" else b.shape[1]
    tm = m // ROW_TILES
    tk = _pick_tile(kdim, 2816, LANES)
    nk = kdim // tk
    ni, no = len(ins), len(outs)

    def body(*refs):
        a_ref, b_ref = refs[:2]
        in_refs, out_refs, rest = refs[2:2 + ni], refs[2 + ni:2 + ni + no], refs[2 + ni + no:]
        i = pl.program_id(0)
        if mode == "nn":
            p = jnp.dot(a_ref[...], b_ref[...], preferred_element_type=F32)
        else:
            p = lax.dot_general(a_ref[...], b_ref[...], (((1,), (1,)), ((), ())), preferred_element_type=F32)
        if nk == 1:
            epilogue(p, i, in_refs, out_refs, rest)
        else:
            acc_ref = rest[0]
            k = pl.program_id(1)

            @pl.when(k == 0)
            def _():
                acc_ref[...] = p

            @pl.when(k > 0)
            def _():
                acc_ref[...] += p

            @pl.when(k == nk - 1)
            def _():
                epilogue(acc_ref[...], i, in_refs, out_refs, rest[1:])

    def spec(shape, kind):
        if kind == "rows":
            return pl.BlockSpec((tm,) + tuple(shape[1:]), lambda i, k: (i,) + (0,) * (len(shape) - 1))
        if kind == "whole":
            return pl.BlockSpec(tuple(shape), lambda i, k: (0,) * len(shape))
        return pl.BlockSpec(memory_space=pl.ANY)

    a_spec = pl.BlockSpec((tm, tk), lambda i, k: (i, k))
    b_spec = pl.BlockSpec((n, tk), lambda i, k: (0, k)) if mode == "nt" else pl.BlockSpec((tk, n), lambda i, k: (k, 0))
    return pl.pallas_call(
        body, name=name, grid=(ROW_TILES, nk),
        in_specs=[a_spec, b_spec] + [spec(x.shape, kind) for x, kind in ins],
        out_specs=[spec(shape, kind) for shape, _, kind in outs],
        out_shape=[jax.ShapeDtypeStruct(shape, dtype) for shape, dtype, _ in outs],
        scratch_shapes=([pltpu.VMEM((tm, n), F32)] if nk > 1 else []) + list(scratch),
        compiler_params=_cparams("arbitrary", "arbitrary"),
    )(a, b, *[x for x, _ in ins])


def _rows(shape_cols, tr, dtype=None):
    return pl.BlockSpec((tr, shape_cols), lambda i: (i, 0))


def _const(shape):
    return pl.BlockSpec(shape, lambda i: (0,) * len(shape))


def _rms(x):
    return lax.rsqrt(jnp.mean(x * x, axis=-1, keepdims=True) + RMS_EPS)


def _rms_bwd(x, r, g, dy):
    xn = x * r
    dxn = dy * g
    dx = r * (dxn - xn * jnp.mean(dxn * xn, axis=-1, keepdims=True))
    return dx, dy * xn


def _gelu(y):
    return 0.5 * y * (1.0 + jnp.tanh(GELU_C * (y + GELU_A * y * y * y)))


def _gelu_grad(y):
    t = jnp.tanh(GELU_C * (y + GELU_A * y * y * y))
    return 0.5 * (1.0 + t) + 0.5 * y * (1.0 - t * t) * GELU_C * (1.0 + 3.0 * GELU_A * y * y)


def _sigmoid(z):
    return 1.0 / (1.0 + jnp.exp(-z))


def _proj_res_norm(a, w, h, g, after, name):
    def epilogue(p, i, ins, outs, _):
        x = ins[0][...] + p
        outs[0][...] = x
        outs[1][...] = (x * _rms(x) * ins[1][...]).astype(BF16)

    return _mm_rows(a, w, "nn", name, [(h, "rows"), (g, "whole"), (after, "hbm")],
                    [(h.shape, F32, "rows"), (h.shape, BF16, "rows")], epilogue)


def _proj_norm_bwd(da, w, h, g, dres, after, name):
    d = h.shape[1]

    def epilogue(p, i, ins, outs, _):
        x = ins[0][...]
        dx, dgs = _rms_bwd(x, _rms(x), ins[1][...], p)
        dh = ins[2][...] + dx
        outs[0][...] = dh
        outs[1][...] = dh.astype(BF16)

        @pl.when(i == 0)
        def _():
            outs[2][...] = jnp.zeros_like(outs[2])

        outs[2][...] += jnp.sum(dgs, axis=0, keepdims=True)

    return _mm_rows(da, w, "nt", name, [(h, "rows"), (g, "whole"), (dres, "rows"), (after, "hbm")],
                    [(h.shape, F32, "rows"), (h.shape, BF16, "rows"), ((1, d), F32, "whole")], epilogue)


def _input_norm_bwd(h, g, dhn, dres, n_real, name):
    tp, d = h.shape
    tr = tp // ROW_TILES

    def body(h_ref, g_ref, dhn_ref, dres_ref, dx_ref, dmeta_ref, dg_ref, stage, sem):
        i = pl.program_id(0)
        x = h_ref[...]
        dx, dgs = _rms_bwd(x, _rms(x), g_ref[...], dhn_ref[...])
        stage[...] = dres_ref[...] + dx

        @pl.when(i == 0)
        def _():
            dg_ref[...] = jnp.zeros_like(dg_ref)
            dmeta_ref[...] = stage[:N_META, :]

        dg_ref[...] += jnp.sum(dgs, axis=0, keepdims=True)
        for t in range(ROW_TILES):
            lo, hi = max(t * tr, N_META), min((t + 1) * tr, n_real)
            if hi > lo:
                @pl.when(i == t)
                def _(t=t, lo=lo, hi=hi):
                    cp = pltpu.make_async_copy(stage.at[pl.ds(lo - t * tr, hi - lo), :],
                                               dx_ref.at[pl.ds(lo - N_META, hi - lo), :], sem)
                    cp.start()
                    cp.wait()

    return pl.pallas_call(
        body, name=name, grid=(ROW_TILES,),
        in_specs=[_rows(d, tr), _const((1, d)), _rows(d, tr), _rows(d, tr)],
        out_specs=[pl.BlockSpec(memory_space=pl.ANY), _const((N_META, d)), _const((1, d))],
        out_shape=[jax.ShapeDtypeStruct((n_real - N_META, d), F32), jax.ShapeDtypeStruct((N_META, d), F32),
                   jax.ShapeDtypeStruct((1, d), F32)],
        scratch_shapes=[pltpu.VMEM((tr, d), F32), pltpu.SemaphoreType.DMA],
        compiler_params=_cparams("arbitrary"))(h, g, dhn, dres)


def _load_token_rows(tok_hbm, buf, sem, tr, n_real, head=None, wait=False, i=None):
    i = pl.program_id(0) if i is None else i
    for t in range(ROW_TILES):
        base = t * tr
        lo, hi = max(base, N_META), min(base + tr, n_real)

        @pl.when(i == t)
        def _(base=base, lo=lo, hi=hi):
            if hi > lo:
                cp = pltpu.make_async_copy(tok_hbm.at[pl.ds(lo - N_META, hi - lo), :],
                                           buf.at[pl.ds(lo - base, hi - lo), :], sem)
                if wait:
                    cp.wait()
                    return
                cp.start()
            if wait:
                return
            if base < N_META:
                buf[0:N_META - base, :] = (jnp.zeros((N_META - base, buf.shape[1]), F32) if head is None
                                           else head[base:N_META, :])
            if hi < base + tr:
                buf[max(hi, base) - base:tr, :] = jnp.zeros((base + tr - max(hi, base), buf.shape[1]), F32)


def _input_norm_fwd(x, meta, g, tp, name):
    seq, d = x.shape
    tr = tp // ROW_TILES
    n_real = N_META + seq

    def body(x_hbm, meta_ref, g_ref, h_ref, hn_ref, buf, sem):
        _load_token_rows(x_hbm, buf, sem, tr, n_real, head=meta_ref)
        _load_token_rows(x_hbm, buf, sem, tr, n_real, wait=True)
        h = buf[...]
        h_ref[...] = h
        hn_ref[...] = (h * _rms(h) * g_ref[...]).astype(BF16)

    return pl.pallas_call(
        body, name=name, grid=(ROW_TILES,),
        in_specs=[pl.BlockSpec(memory_space=pl.ANY), _const((N_META, d)), _const((1, d))],
        out_specs=[_rows(d, tr), _rows(d, tr)],
        out_shape=[jax.ShapeDtypeStruct((tp, d), F32), jax.ShapeDtypeStruct((tp, d), BF16)],
        scratch_shapes=[pltpu.VMEM((tr, d), F32), pltpu.SemaphoreType.DMA],
        compiler_params=_cparams("arbitrary"))(x, meta, g)


def _proj_loss_bwd(act, w, h1, target, g, n_real, name):
    tp, d = h1.shape
    tr = tp // ROW_TILES

    def epilogue(p, i, ins, outs, scratch):
        h1_ref, t_hbm, g_ref = ins
        loss_ref, dh_ref, dhb_ref, dg_ref = outs
        t_buf, sem = scratch
        _load_token_rows(t_hbm, t_buf, sem, tr, n_real, i=i)
        x = h1_ref[...] + p
        r = _rms(x)
        row = i * tr + lax.broadcasted_iota(jnp.int32, (tr, d), 0)
        valid = (row >= N_META) & (row < n_real)
        _load_token_rows(t_hbm, t_buf, sem, tr, n_real, wait=True, i=i)
        e = jnp.where(valid, x * r * g_ref[...] - t_buf[...], 0.0)
        dx, dgs = _rms_bwd(x, r, g_ref[...], e * (1.0 / d))
        dh_ref[...] = dx
        dhb_ref[...] = dx.astype(BF16)

        @pl.when(i == 0)
        def _():
            dg_ref[...] = jnp.zeros_like(dg_ref)
            loss_ref[...] = jnp.zeros_like(loss_ref)

        dg_ref[...] += jnp.sum(dgs, axis=0, keepdims=True)
        loss_ref[...] += (0.5 / d) * jnp.sum(jnp.sum(e * e, axis=0, keepdims=True), axis=1, keepdims=True)

    return _mm_rows(act, w, "nn", name, [(h1, "rows"), (target, "hbm"), (g, "whole")],
                    [((1, LANES), F32, "whole"), ((tp, d), F32, "rows"), ((tp, d), BF16, "rows"),
                     ((1, d), F32, "whole")],
                    epilogue, scratch=[pltpu.VMEM((tr, d), F32), pltpu.SemaphoreType.DMA])


def _mix_fwd(co, y, z, gc, gs, name):
    tp, dh = co.shape
    tr = tp // ROW_TILES

    def body(co_ref, y_ref, z_ref, gc_ref, gs_ref, m_ref):
        c = co_ref[...]
        m_ref[:, :dh] = (c * _rms(c) * gc_ref[...]).astype(BF16)
        so = _gelu(y_ref[...]) * _sigmoid(z_ref[...])
        m_ref[:, dh:] = (so * _rms(so) * gs_ref[...]).astype(BF16)

    return pl.pallas_call(
        body, name=name, grid=(ROW_TILES,),
        in_specs=[_rows(dh, tr)] * 3 + [_const((1, dh))] * 2,
        out_specs=_rows(2 * dh, tr),
        out_shape=jax.ShapeDtypeStruct((tp, 2 * dh), BF16),
        compiler_params=_cparams("parallel"))(co, y, z, gc, gs)


def _proj_mix_bwd(dh1b, w, co, y, z, gc, gs, name):
    tp, dh = co.shape

    def epilogue(p, i, ins, outs, _):
        co_ref, y_ref, z_ref, gc_ref, gs_ref = ins
        dco_ref, dz_ref, dgp_ref, dgc_ref, dgs_ref = outs
        c = co_ref[...]
        dco, dgc = _rms_bwd(c, _rms(c), gc_ref[...], p[:, :dh])
        dco_ref[...] = dco
        gl = _gelu(y_ref[...])
        sg = _sigmoid(z_ref[...])
        so = gl * sg
        dso, dgs = _rms_bwd(so, _rms(so), gs_ref[...], p[:, dh:])
        dz_ref[...] = (dso * gl * sg * (1.0 - sg)).astype(BF16)
        dgp_ref[...] = dso * sg

        @pl.when(i == 0)
        def _():
            dgc_ref[...] = jnp.zeros_like(dgc_ref)
            dgs_ref[...] = jnp.zeros_like(dgs_ref)

        dgc_ref[...] += jnp.sum(dgc, axis=0, keepdims=True)
        dgs_ref[...] += jnp.sum(dgs, axis=0, keepdims=True)

    return _mm_rows(dh1b, w, "nt", name,
                    [(co, "rows"), (y, "rows"), (z, "rows"), (gc, "whole"), (gs, "whole")],
                    [((tp, dh), F32, "rows"), ((tp, dh), BF16, "rows"), ((tp, dh), F32, "rows"),
                     ((1, dh), F32, "whole"), ((1, dh), F32, "whole")], epilogue)


def _shift_down(x, k):
    row = lax.broadcasted_iota(jnp.int32, x.shape, 0)
    return jnp.where(row >= k, pltpu.roll(x, k, 0), 0.0)


def _shift_up(x, k):
    n = x.shape[0]
    row = lax.broadcasted_iota(jnp.int32, x.shape, 0)
    return jnp.where(row < n - k, pltpu.roll(x, n - k, 0), 0.0)


def _dwconv(x, w_ref):
    return w_ref[2:3, :] * x + w_ref[1:2, :] * _shift_down(x, 1) + w_ref[0:1, :] * _shift_down(x, 2)


def _dwconv_bwd(x, dy, w_ref):
    dx = w_ref[2:3, :] * dy + w_ref[1:2, :] * _shift_up(dy, 1) + w_ref[0:1, :] * _shift_up(dy, 2)
    dw = jnp.concatenate([jnp.sum(dy * _shift_down(x, 2), axis=0, keepdims=True),
                          jnp.sum(dy * _shift_down(x, 1), axis=0, keepdims=True),
                          jnp.sum(dy * x, axis=0, keepdims=True)], axis=0)
    return dx, dw


def _interleave(dst, src):
    seg_rows = src.shape[0] // SUBLANES
    for seg in range(SUBLANES):
        dst[pl.ds(seg, seg_rows, stride=SUBLANES), :] = src[seg * seg_rows:(seg + 1) * seg_rows, :]


def _deinterleave(dst, src):
    seg_rows = src.shape[0] // SUBLANES
    for seg in range(SUBLANES):
        dst[seg * seg_rows:(seg + 1) * seg_rows, :] = src[pl.ds(seg, seg_rows, stride=SUBLANES), :]


def _segment_shift(x, reverse):
    row = lax.broadcasted_iota(jnp.int32, x.shape, 0)
    if reverse:
        return jnp.where(row < SUBLANES - 1, pltpu.roll(x, SUBLANES - 1, 0), 0.0)
    return jnp.where(row >= 1, pltpu.roll(x, 1, 0), 0.0)


def _scan(s_re, s_im, pw_ref, reverse):
    n_steps = s_re.shape[0] // SUBLANES
    n_strips = s_re.shape[1] // LANES
    sign = -1.0 if reverse else 1.0
    strips = [slice(st * LANES, (st + 1) * LANES) for st in range(n_strips)]

    def rows_of(j):
        step = (n_steps - 1 - j) if reverse else j
        return pl.ds(pl.multiple_of(step * SUBLANES, SUBLANES), SUBLANES)

    a = [(jnp.broadcast_to(pw_ref[0, 0:1, lanes], (SUBLANES, LANES)),
          sign * jnp.broadcast_to(pw_ref[1, 0:1, lanes], (SUBLANES, LANES))) for lanes in strips]

    def local(i, carry):
        for half in range(2):
            rows = rows_of(2 * i + half)
            out = []
            for st, lanes in enumerate(strips):
                (ar, ai), cr, ci = a[st], carry[2 * st], carry[2 * st + 1]
                xr = s_re[rows, lanes] + (ar * cr - ai * ci)
                xi = s_im[rows, lanes] + (ar * ci + ai * cr)
                s_re[rows, lanes] = xr
                s_im[rows, lanes] = xi
                out += [xr, xi]
            carry = tuple(out)
        return carry

    zero = jnp.zeros((SUBLANES, LANES), F32)
    ends = lax.fori_loop(0, n_steps // 2, local, (zero,) * (2 * n_strips))

    entering = []
    row = lax.broadcasted_iota(jnp.int32, (SUBLANES, LANES), 0)
    for st, lanes in enumerate(strips):
        tr, ti = ends[2 * st], ends[2 * st + 1]
        mr = jnp.broadcast_to(pw_ref[0, n_steps - 1:n_steps, lanes], (SUBLANES, LANES))
        mi = sign * jnp.broadcast_to(pw_ref[1, n_steps - 1:n_steps, lanes], (SUBLANES, LANES))
        for k in (1, 2, 4):
            keep = (row < SUBLANES - k) if reverse else (row >= k)
            rr = jnp.where(keep, pltpu.roll(tr, SUBLANES - k if reverse else k, 0), 0.0)
            ri = jnp.where(keep, pltpu.roll(ti, SUBLANES - k if reverse else k, 0), 0.0)
            tr, ti = tr + (mr * rr - mi * ri), ti + (mr * ri + mi * rr)
            mr, mi = mr * mr - mi * mi, 2.0 * mr * mi
        entering += [_segment_shift(tr, reverse), _segment_shift(ti, reverse)]

    def fix(i, carry):
        base = pl.multiple_of(i * SUBLANES, SUBLANES)
        for st, lanes in enumerate(strips):
            pw_r = pw_ref[0, pl.ds(base, SUBLANES), lanes]
            pw_i = sign * pw_ref[1, pl.ds(base, SUBLANES), lanes]
            cr, ci = entering[2 * st], entering[2 * st + 1]
            for k in range(SUBLANES):
                rows = rows_of(i * SUBLANES + k)
                pr, pi = pw_r[k:k + 1, :], pw_i[k:k + 1, :]
                s_re[rows, lanes] = s_re[rows, lanes] + (pr * cr - pi * ci)
                s_im[rows, lanes] = s_im[rows, lanes] + (pr * ci + pi * cr)
        return carry

    lax.fori_loop(0, n_steps // SUBLANES, fix, 0)


def _seq_fwd(proj, conv_w, bc_re, bc_im, cc_re, cc_im, dskip, a_pow, name):
    tp = proj.shape[0]
    dh = proj.shape[1] // 4
    nq = dh // LANES
    sw = STATE * N_GROUPS // nq

    def body(b_ref, c_ref, v_ref, u_ref, w_ref, bre_ref, bim_ref, cre_ref, cim_ref, d_ref, pw_ref,
             co_ref, y_ref, g_ref, s_re, s_im, u_il, y_il):
        co_ref[...] = b_ref[...] * _dwconv(c_ref[...] * v_ref[...], w_ref)
        _interleave(u_il, u_ref)
        ub = u_il[...].astype(BF16)
        s_re[...] = jnp.dot(ub, bre_ref[...], preferred_element_type=F32)
        s_im[...] = jnp.dot(ub, bim_ref[...], preferred_element_type=F32)
        _scan(s_re, s_im, pw_ref, False)
        y_il[...] = (jnp.dot(s_re[...].astype(BF16), cre_ref[...], preferred_element_type=F32)
                     - jnp.dot(s_im[...].astype(BF16), cim_ref[...], preferred_element_type=F32))
        _deinterleave(y_ref, y_il)
        y = y_ref[...] + d_ref[...] * u_ref[...]
        y_ref[...] = y
        g_ref[...] = _gelu(y).astype(BF16)

    col = lambda off: pl.BlockSpec((tp, LANES), lambda q, off=off: (0, off * nq + q))
    blk = pl.BlockSpec((tp, LANES), lambda q: (0, q))
    return pl.pallas_call(
        body, name=name, grid=(nq,),
        in_specs=[col(0), col(1), col(2), col(3),
                  pl.BlockSpec((3, LANES), lambda q: (0, q)),
                  pl.BlockSpec((LANES, sw), lambda q: (0, q)), pl.BlockSpec((LANES, sw), lambda q: (0, q)),
                  pl.BlockSpec((sw, LANES), lambda q: (q, 0)), pl.BlockSpec((sw, LANES), lambda q: (q, 0)),
                  pl.BlockSpec((1, LANES), lambda q: (0, q)),
                  pl.BlockSpec((2, tp // SUBLANES, sw), lambda q: (0, 0, q))],
        out_specs=[blk, blk, blk],
        out_shape=[jax.ShapeDtypeStruct((tp, dh), F32), jax.ShapeDtypeStruct((tp, dh), F32),
                   jax.ShapeDtypeStruct((tp, dh), BF16)],
        scratch_shapes=[pltpu.VMEM((tp, sw), F32), pltpu.VMEM((tp, sw), F32),
                        pltpu.VMEM((tp, LANES), F32), pltpu.VMEM((tp, LANES), F32)],
        compiler_params=_cparams("parallel"),
    )(proj, proj, proj, proj, conv_w, bc_re, bc_im, cc_re, cc_im, dskip, a_pow)


def _conv_bwd(proj, dco, conv_w, name):
    tp = proj.shape[0]
    dh = proj.shape[1] // 4
    nq = dh // LANES

    def body(b_ref, c_ref, v_ref, dco_ref, w_ref, dproj_ref, dw_ref, stage, sem):
        q = pl.program_id(0)
        cg = c_ref[...]
        vg = v_ref[...]
        cv = cg * vg
        dco_v = dco_ref[...]
        dcv, dw = _dwconv_bwd(cv, dco_v * b_ref[...], w_ref)
        dw_ref[...] = dw
        stage[0] = (dco_v * _dwconv(cv, w_ref)).astype(BF16)
        stage[1] = (dcv * vg).astype(BF16)
        stage[2] = (dcv * cg).astype(BF16)
        copies = [pltpu.make_async_copy(stage.at[p], dproj_ref.at[:, pl.ds((p * nq + q) * LANES, LANES)], sem.at[p])
                  for p in range(3)]
        for cp in copies:
            cp.start()
        for cp in copies:
            cp.wait()

    col = lambda off: pl.BlockSpec((tp, LANES), lambda q, off=off: (0, off * nq + q))
    return pl.pallas_call(
        body, name=name, grid=(nq,),
        in_specs=[col(0), col(1), col(2), pl.BlockSpec((tp, LANES), lambda q: (0, q)),
                  pl.BlockSpec((3, LANES), lambda q: (0, q))],
        out_specs=[pl.BlockSpec(memory_space=pl.ANY), pl.BlockSpec((3, LANES), lambda q: (0, q))],
        out_shape=[jax.ShapeDtypeStruct((tp, 4 * dh), BF16), jax.ShapeDtypeStruct((3, dh), F32)],
        scratch_shapes=[pltpu.VMEM((3, tp, LANES), BF16), pltpu.SemaphoreType.DMA((3,))],
        compiler_params=_cparams("arbitrary"),
    )(proj, proj, proj, dco, conv_w)


def _ssm_bwd(proj, y, dg, dproj, bc_re, bc_im, cc_re, cc_im, dskip, a_pow, name):
    tp = proj.shape[0]
    dh = proj.shape[1] // 4
    nq = dh // LANES
    sw = STATE * N_GROUPS // nq

    def body(u_ref, y_ref, dg_ref, dproj_in, bre_ref, bim_ref, cre_ref, cim_ref, d_ref, pw_ref,
             dproj_ref, dbre_ref, dbim_ref, dcre_ref, dcim_ref, dd_ref, dar_ref, dai_ref,
             s_re, s_im, l_re, l_im, a_il, b_il, stage, sem):
        del dproj_in
        q = pl.program_id(0)
        nt = (((1,), (1,)), ((), ()))
        tn = (((0,), (0,)), ((), ()))
        _interleave(a_il, u_ref)
        ub = a_il[...].astype(BF16)
        s_re[...] = jnp.dot(ub, bre_ref[...], preferred_element_type=F32)
        s_im[...] = jnp.dot(ub, bim_ref[...], preferred_element_type=F32)
        _scan(s_re, s_im, pw_ref, False)
        dy_rows = dg_ref[...] * _gelu_grad(y_ref[...])
        dd_ref[...] = jnp.sum(dy_rows * u_ref[...], axis=0, keepdims=True)
        _interleave(b_il, dy_rows)
        dy = b_il[...]
        dyb = dy.astype(BF16)
        l_re[...] = lax.dot_general(dyb, cre_ref[...], nt, preferred_element_type=F32)
        l_im[...] = -lax.dot_general(dyb, cim_ref[...], nt, preferred_element_type=F32)
        dcre_ref[...] = lax.dot_general(s_re[...].astype(BF16), dyb, tn, preferred_element_type=F32)
        dcim_ref[...] = -lax.dot_general(s_im[...].astype(BF16), dyb, tn, preferred_element_type=F32)
        _scan(l_re, l_im, pw_ref, True)
        rest = tp - SUBLANES
        for st in range(sw // LANES):
            lanes = slice(st * LANES, (st + 1) * LANES)
            lr, li = l_re[SUBLANES:, lanes], l_im[SUBLANES:, lanes]
            pr, pi = s_re[:rest, lanes], s_im[:rest, lanes]
            lr0, li0 = l_re[:SUBLANES, lanes], l_im[:SUBLANES, lanes]
            pr0, pi0 = _segment_shift(s_re[rest:, lanes], False), _segment_shift(s_im[rest:, lanes], False)
            dar_ref[:, lanes] = (jnp.sum(lr * pr + li * pi, axis=0, keepdims=True)
                                 + jnp.sum(lr0 * pr0 + li0 * pi0, axis=0, keepdims=True))
            dai_ref[:, lanes] = (jnp.sum(li * pr - lr * pi, axis=0, keepdims=True)
                                 + jnp.sum(li0 * pr0 - lr0 * pi0, axis=0, keepdims=True))
        lrb = l_re[...].astype(BF16)
        lib = l_im[...].astype(BF16)
        a_il[...] = (dy * d_ref[...] + lax.dot_general(lrb, bre_ref[...], nt, preferred_element_type=F32)
                     + lax.dot_general(lib, bim_ref[...], nt, preferred_element_type=F32))
        _deinterleave(b_il, a_il)
        stage[...] = b_il[...].astype(BF16)
        dbre_ref[...] = lax.dot_general(ub, lrb, tn, preferred_element_type=F32)
        dbim_ref[...] = lax.dot_general(ub, lib, tn, preferred_element_type=F32)
        cp = pltpu.make_async_copy(stage, dproj_ref.at[:, pl.ds((3 * nq + q) * LANES, LANES)], sem)
        cp.start()
        cp.wait()

    blk = pl.BlockSpec((tp, LANES), lambda q: (0, q))
    bspec = pl.BlockSpec((LANES, sw), lambda q: (0, q))
    cspec = pl.BlockSpec((sw, LANES), lambda q: (q, 0))
    tspec = pl.BlockSpec((2, tp // SUBLANES, sw), lambda q: (0, 0, q))
    nstate = STATE * N_GROUPS
    return pl.pallas_call(
        body, name=name, grid=(nq,),
        in_specs=[pl.BlockSpec((tp, LANES), lambda q: (0, 3 * nq + q)), blk, blk, pl.BlockSpec(memory_space=pl.ANY),
                  bspec, bspec, cspec, cspec, pl.BlockSpec((1, LANES), lambda q: (0, q)), tspec],
        out_specs=[pl.BlockSpec(memory_space=pl.ANY), bspec, bspec, cspec, cspec,
                   pl.BlockSpec((1, LANES), lambda q: (0, q)),
                   pl.BlockSpec((1, sw), lambda q: (0, q)), pl.BlockSpec((1, sw), lambda q: (0, q))],
        out_shape=[jax.ShapeDtypeStruct((tp, 4 * dh), BF16),
                   jax.ShapeDtypeStruct((LANES, nstate), F32), jax.ShapeDtypeStruct((LANES, nstate), F32),
                   jax.ShapeDtypeStruct((nstate, LANES), F32), jax.ShapeDtypeStruct((nstate, LANES), F32),
                   jax.ShapeDtypeStruct((1, dh), F32),
                   jax.ShapeDtypeStruct((1, nstate), F32), jax.ShapeDtypeStruct((1, nstate), F32)],
        input_output_aliases={3: 0},
        scratch_shapes=[pltpu.VMEM((tp, sw), F32)] * 4 + [pltpu.VMEM((tp, LANES), F32)] * 2
        + [pltpu.VMEM((tp, LANES), BF16), pltpu.SemaphoreType.DMA],
        compiler_params=_cparams("arbitrary"),
    )(proj, y, dg, dproj, bc_re, bc_im, cc_re, cc_im, dskip, a_pow)


FFN_TILE = 256


def _ffn_act(up, fw, fb, name):
    tp, two_ff = up.shape
    dff = two_ff // 2
    tc = FFN_TILE
    nj = dff // tc

    def body(ua_ref, uv_ref, wa_ref, wv_ref, ba_ref, bv_ref, act_ref):
        a = _dwconv(ua_ref[...], wa_ref) + ba_ref[...]
        v = _dwconv(uv_ref[...], wv_ref) + bv_ref[...]
        act_ref[...] = (a * _sigmoid(a) * v).astype(BF16)

    lo = lambda r: pl.BlockSpec((r, tc), lambda j: (0, j))
    hi = lambda r: pl.BlockSpec((r, tc), lambda j: (0, nj + j))
    return pl.pallas_call(
        body, name=name, grid=(nj,),
        in_specs=[lo(tp), hi(tp), lo(3), hi(3), lo(1), hi(1)],
        out_specs=lo(tp),
        out_shape=jax.ShapeDtypeStruct((tp, dff), BF16),
        compiler_params=_cparams("parallel"))(up, up, fw, fw, fb, fb)


def _ffn_bwd(up, dact, fw, fb, name):
    tp, two_ff = up.shape
    dff = two_ff // 2
    tc = FFN_TILE
    nj = dff // tc

    def body(ua_ref, uv_ref, da_ref, wa_ref, wv_ref, ba_ref, bv_ref,
             dup_ref, dwa_ref, dwv_ref, dba_ref, dbv_ref, stage, sem):
        j = pl.program_id(0)
        ua = ua_ref[...]
        uv = uv_ref[...]
        a = _dwconv(ua, wa_ref) + ba_ref[...]
        v = _dwconv(uv, wv_ref) + bv_ref[...]
        sg = _sigmoid(a)
        dact_v = da_ref[...]
        da = dact_v * v * sg * (1.0 + a * (1.0 - sg))
        dv = dact_v * a * sg
        dba_ref[...] = jnp.sum(da, axis=0, keepdims=True)
        dbv_ref[...] = jnp.sum(dv, axis=0, keepdims=True)
        dua, dwa = _dwconv_bwd(ua, da, wa_ref)
        duv, dwv = _dwconv_bwd(uv, dv, wv_ref)
        dwa_ref[...] = dwa
        dwv_ref[...] = dwv
        stage[0] = dua.astype(BF16)
        stage[1] = duv.astype(BF16)
        copies = [pltpu.make_async_copy(stage.at[p], dup_ref.at[:, pl.ds((p * nj + j) * tc, tc)], sem.at[p])
                  for p in range(2)]
        for cp in copies:
            cp.start()
        for cp in copies:
            cp.wait()

    lo = lambda r: pl.BlockSpec((r, tc), lambda j: (0, j))
    hi = lambda r: pl.BlockSpec((r, tc), lambda j: (0, nj + j))
    return pl.pallas_call(
        body, name=name, grid=(nj,),
        in_specs=[lo(tp), hi(tp), lo(tp), lo(3), hi(3), lo(1), hi(1)],
        out_specs=[pl.BlockSpec(memory_space=pl.ANY), lo(3), lo(3), lo(1), lo(1)],
        out_shape=[jax.ShapeDtypeStruct((tp, two_ff), BF16),
                   jax.ShapeDtypeStruct((3, dff), F32), jax.ShapeDtypeStruct((3, dff), F32),
                   jax.ShapeDtypeStruct((1, dff), F32), jax.ShapeDtypeStruct((1, dff), F32)],
        scratch_shapes=[pltpu.VMEM((2, tp, tc), BF16), pltpu.SemaphoreType.DMA((2,))],
        compiler_params=_cparams("arbitrary"))(up, up, dact, fw, fw, fb, fb)


def _zoh(lr, li, ld):
    dt = jnp.exp(ld)
    mag = jnp.exp(lr * dt)
    ang = li * dt
    ar = mag * jnp.cos(ang)
    ai = mag * jnp.sin(ang)
    den = lr * lr + li * li
    nr = ar - 1.0
    fr = (nr * lr + ai * li) / den
    fi = (ai * lr - nr * li) / den
    return dt, ar, ai, den, nr, fr, fi


def _s5_prep(lr, li, ld, b_re, b_im, n_pow, name):
    nstate = lr.shape[1]

    def body(lr_ref, li_ref, ld_ref, bre_ref, bim_ref, pw_ref, bcre_ref, bcim_ref):
        _, ar, ai, _, _, fr, fi = _zoh(lr_ref[...], li_ref[...], ld_ref[...])
        bre = bre_ref[...]
        bim = bim_ref[...]
        bcre_ref[...] = (fr * bre - fi * bim).astype(BF16)
        bcim_ref[...] = (fr * bim + fi * bre).astype(BF16)
        row = lax.broadcasted_iota(jnp.int32, (SUBLANES, nstate), 0)
        pr, pi = jnp.zeros((SUBLANES, nstate), F32), jnp.zeros((SUBLANES, nstate), F32)
        cr, ci = ar, ai
        for t in range(SUBLANES):
            pr, pi = jnp.where(row == t, cr, pr), jnp.where(row == t, ci, pi)
            cr, ci = cr * ar - ci * ai, cr * ai + ci * ar
        pw_ref[0, 0:SUBLANES, :] = pr
        pw_ref[1, 0:SUBLANES, :] = pi
        n = SUBLANES
        while n < n_pow:
            m = min(n, n_pow - n)
            tr, ti = pw_ref[0, n - 1:n, :], pw_ref[1, n - 1:n, :]
            xr, xi = pw_ref[0, 0:m, :], pw_ref[1, 0:m, :]
            pw_ref[0, n:n + m, :] = xr * tr - xi * ti
            pw_ref[1, n:n + m, :] = xr * ti + xi * tr
            n += m

    vmem = pl.BlockSpec(memory_space=pltpu.VMEM)
    return pl.pallas_call(
        body, name=name, in_specs=[vmem] * 5, out_specs=[vmem] * 3,
        out_shape=[jax.ShapeDtypeStruct((2, n_pow, nstate), F32)] + [jax.ShapeDtypeStruct(b_re.shape, BF16)] * 2,
        compiler_params=pltpu.CompilerParams(vmem_limit_bytes=VMEM_LIMIT))(lr, li, ld, b_re, b_im)


def _s5_prep_bwd(lr, li, ld, b_re, b_im, da_re, da_im, dbc_re, dbc_im, name):
    def body(lr_ref, li_ref, ld_ref, bre_ref, bim_ref, dar_ref, dai_ref, dbcre_ref, dbcim_ref,
             dlr_ref, dli_ref, dld_ref, dbre_ref, dbim_ref):
        lr, li = lr_ref[...], li_ref[...]
        dt, ar, ai, den, nr, fr, fi = _zoh(lr, li, ld_ref[...])
        bre, bim = bre_ref[...], bim_ref[...]
        gre, gim = dbcre_ref[...], dbcim_ref[...]
        dbre_ref[...] = fr * gre + fi * gim
        dbim_ref[...] = fr * gim - fi * gre
        g_fr = jnp.sum(gre * bre + gim * bim, axis=0, keepdims=True)
        g_fi = jnp.sum(gim * bre - gre * bim, axis=0, keepdims=True)
        g_ar = dar_ref[...] + (g_fr * lr - g_fi * li) / den
        g_ai = dai_ref[...] + (g_fr * li + g_fi * lr) / den
        d_lr = (g_fr * (nr - 2.0 * fr * lr) + g_fi * (ai - 2.0 * fi * lr)) / den
        d_li = (g_fr * (ai - 2.0 * fr * li) - g_fi * (nr + 2.0 * fi * li)) / den
        g_logmag = g_ar * ar + g_ai * ai
        g_ang = g_ai * ar - g_ar * ai
        dlr_ref[...] = d_lr + g_logmag * dt
        dli_ref[...] = d_li + g_ang * dt
        d_ld = (g_logmag * lr + g_ang * li) * dt
        n = d_ld.shape[1]
        sh = 1
        while sh < STATE:
            d_ld = d_ld + pltpu.roll(d_ld, n - sh, 1)
            sh *= 2
        dld_ref[...] = d_ld

    vmem = pl.BlockSpec(memory_space=pltpu.VMEM)
    row = jax.ShapeDtypeStruct(lr.shape, F32)
    return pl.pallas_call(
        body, name=name, in_specs=[vmem] * 9, out_specs=[vmem] * 5,
        out_shape=[row, row, row, jax.ShapeDtypeStruct(b_re.shape, F32), jax.ShapeDtypeStruct(b_re.shape, F32)],
    )(lr, li, ld, b_re, b_im, da_re, da_im, dbc_re, dbc_im)


def _compact_b(bb):
    bq = bb.reshape(N_GROUPS // 8, 8, STATE, GROUP)
    m = jnp.einsum("ab,qbph->qahbp", jnp.eye(8, dtype=bb.dtype), bq).reshape(N_GROUPS // 8, LANES, 8 * STATE)
    return m.transpose(1, 0, 2).reshape(LANES, N_GROUPS * STATE)


def _expand_b(m):
    d = m.reshape(8, GROUP, N_GROUPS // 8, 8, STATE)
    return jnp.einsum("ahqap->qahp", d).reshape(N_GROUPS, GROUP, STATE)


def _compact_c(c):
    cq = c.reshape(N_GROUPS // 8, 8, GROUP, STATE)
    return jnp.einsum("ab,qbhp->qbpah", jnp.eye(8, dtype=c.dtype), cq).reshape(N_GROUPS * STATE, LANES)


def _expand_c(m):
    d = m.reshape(N_GROUPS // 8, 8, STATE, 8, GROUP)
    return jnp.einsum("qbpbh->qbhp", d).reshape(N_GROUPS, GROUP, STATE)


def _local_step(x, target, p, ex):
    seq, d = x.shape
    n_real = N_META + seq
    tp = -(-n_real // ROW_ALIGN) * ROW_ALIGN

    h0, hn1 = _input_norm_fwd(x, p["meta_tokens"], p["norm_mix_g"] + ex.zero, tp, "norm_mix")
    ex.forward("first", hn1)
    nstate = N_GROUPS * STATE
    s5 = (p["ssm_lam_re"].reshape(1, nstate), p["ssm_lam_im"].reshape(1, nstate),
          jnp.repeat(p["ssm_log_dt"].reshape(-1), STATE).reshape(1, nstate),
          _compact_b(p["ssm_b_re"]), _compact_b(p["ssm_b_im"]))
    a_pow, bc_re, bc_im = _s5_prep(*s5, tp // SUBLANES, "s5_prep")
    cc_re = _compact_c(p["ssm_c_re"]).astype(BF16)
    cc_im = _compact_c(p["ssm_c_im"]).astype(BF16)
    dskip = p["ssm_d"].reshape(1, -1)
    first = ex.weights("first", bc_re)
    proj = _mm(hn1, first["w_in"], "nn", "proj")
    started = ex.forward("mid", proj)
    co, y, g = _seq_fwd(proj, p["conv_w"] + started[0, 0], bc_re, bc_im, cc_re, cc_im, dskip, a_pow, "seq_fwd")
    mid = ex.weights("mid", g)
    z = _mm(g, mid["ssm_w_glu"], "nn", "glu")
    mixed = _mix_fwd(co, y, z, p["gain_conv_out"], p["gain_ssm_out"], "mix_fwd")
    started = ex.forward("late", mixed)
    h1, hn2 = _proj_res_norm(mixed, mid["w_out"], h0, p["norm_ffn_g"], started, "out_proj_norm")
    late = ex.weights("late", hn2)
    up = _mm(hn2, late["w_up"], "nn", "up_proj")
    act = _ffn_act(up, p["ffn_conv_w"], p["ffn_conv_b"], "ffn_act")
    loss, dh2, dh2b, d_gfin = _proj_loss_bwd(act, late["w_down"], h1, target, p["norm_final_g"], n_real,
                                             "down_proj_loss")

    g_w_down = _mm(act, dh2b, "tn", "g_w_down")
    dact = _mm(dh2b, late["w_down"], "nt", "d_act")
    dup, dfw_a, dfw_v, dfb_a, dfb_v = _ffn_bwd(up, dact, p["ffn_conv_w"], p["ffn_conv_b"], "ffn_bwd")
    g_w_up = _mm(hn2, dup, "tn", "g_w_up")
    started = ex.grads_ready("late", {"w_up": g_w_up, "w_down": g_w_down})
    dh1, dh1b, d_gffn = _proj_norm_bwd(dup, late["w_up"], h1, p["norm_ffn_g"], dh2, started, "d_hn2_norm_bwd")
    started = ex.grads_send("late", dh1)
    g_w_out = _mm(mixed, dh1b, "tn", "g_w_out", after=started)
    dco, dz, dgp, d_gc, d_gs = _proj_mix_bwd(dh1b, mid["w_out"], co, y, z, p["gain_conv_out"],
                                             p["gain_ssm_out"], "d_mixed_mix_bwd")
    g_w_glu = _mm(g, dz, "tn", "g_w_glu")
    started = ex.grads_ready("mid", {"ssm_w_glu": g_w_glu, "w_out": g_w_out})
    dg = _mm(dz, mid["ssm_w_glu"], "nt", "d_gelu", acc_in=dgp, after=started)
    started = ex.grads_send("mid", dg)
    dproj, d_conv_w = _conv_bwd(proj, dco, p["conv_w"] + started[0, 0], "conv_bwd")
    (dproj, dbc_re, dbc_im, dcc_re, dcc_im, d_dskip, da_re, da_im) = _ssm_bwd(
        proj, y, dg, dproj, bc_re, bc_im, cc_re, cc_im, dskip, a_pow, "ssm_bwd")
    g_w_in = _mm(hn1, dproj, "tn", "g_w_in")
    started = ex.grads_ready("first", {"w_in": g_w_in})
    dhn1 = _mm(dproj, first["w_in"], "nt", "d_hn1", after=started)
    started = ex.grads_send("first", dhn1)
    grad_x, d_meta, d_gmix = _input_norm_bwd(h0, p["norm_mix_g"] + started[0, 0], dhn1, dh1, n_real, "norm_mix_bwd")

    d_lam_re, d_lam_im, d_log_dt, d_b_re, d_b_im = _s5_prep_bwd(*s5, da_re, da_im, dbc_re, dbc_im, "s5_prep_bwd")
    d_lam_re, d_lam_im = d_lam_re.reshape(N_GROUPS, STATE), d_lam_im.reshape(N_GROUPS, STATE)
    d_log_dt = d_log_dt[0, ::STATE]
    d_b_re, d_b_im = _expand_b(d_b_re), _expand_b(d_b_im)
    grads = {
        "meta_tokens": d_meta, "norm_mix_g": d_gmix, "w_in": g_w_in, "conv_w": d_conv_w,
        "ssm_lam_re": d_lam_re, "ssm_lam_im": d_lam_im, "ssm_log_dt": d_log_dt,
        "ssm_b_re": d_b_re, "ssm_b_im": d_b_im, "ssm_c_re": _expand_c(dcc_re), "ssm_c_im": _expand_c(dcc_im),
        "ssm_d": d_dskip.reshape(N_GROUPS, GROUP), "ssm_w_glu": g_w_glu,
        "gain_conv_out": d_gc, "gain_ssm_out": d_gs, "w_out": g_w_out, "norm_ffn_g": d_gffn,
        "w_up": g_w_up, "ffn_conv_w": jnp.concatenate([dfw_a, dfw_v], axis=1),
        "ffn_conv_b": jnp.concatenate([dfb_a, dfb_v], axis=1), "w_down": g_w_down, "norm_final_g": d_gfin,
    }
    return loss[0, 0], grad_x, grads


def _view(ref, axis, start, size):
    idx = [slice(None)] * len(ref.shape)
    idx[axis] = pl.ds(start, size)
    return ref.at[tuple(idx)]


def _exchange(name, ins, outs, aliases, local_copies, remote_copies):
    ni, no = len(ins), len(outs)
    nl, nr = len(local_copies), len(remote_copies)

    def body(*refs):
        in_refs, out_refs = refs[:ni], refs[ni:ni + no]
        send_sems, recv_sems, local_sems = refs[ni + no:]
        x, y, c = lax.axis_index("x"), lax.axis_index("y"), lax.axis_index("c")
        pos = (x, y, c, 2 * x + y)
        locals_ = [pltpu.make_async_copy(s(in_refs, out_refs, pos), d(in_refs, out_refs, pos), local_sems.at[i])
                   for i, (s, d) in enumerate(local_copies)]
        remotes = []
        for i, (s, d, flip) in enumerate(remote_copies):
            peer = (1 - x if "x" in flip else x, 1 - y if "y" in flip else y, 1 - c if "c" in flip else c)
            remotes.append(pltpu.make_async_remote_copy(
                src_ref=s(in_refs, out_refs, pos), dst_ref=d(in_refs, out_refs, pos),
                send_sem=send_sems.at[i], recv_sem=recv_sems.at[i], device_id=peer, device_id_type=MESH))
        for cp in locals_ + remotes:
            cp.start()
        for cp in remotes:
            cp.wait_recv()
        for cp in remotes:
            cp.wait_send()
        for cp in locals_:
            cp.wait()

    hbm = pl.BlockSpec(memory_space=pl.ANY)
    return pl.pallas_call(
        body, name=name, in_specs=[hbm] * ni, out_specs=[hbm] * no, out_shape=outs,
        input_output_aliases=aliases,
        scratch_shapes=[pltpu.SemaphoreType.DMA((nr,)), pltpu.SemaphoreType.DMA((nr,)),
                        pltpu.SemaphoreType.DMA((max(nl, 1),))],
    )(*ins)


BIG = {"w_in": (0, 1), "ssm_w_glu": (1, 0), "w_out": (1, 0), "w_up": (0, 1), "w_down": (1, 0)}
BIG_NAMES = tuple(BIG)
FLIPS = ("y", "x", "xy")


def _peer_chip(pos, flip):
    x, y, _, _ = pos
    return 2 * (1 - x if "x" in flip else x) + (1 - y if "y" in flip else y)


def _block_rows(rows, cols, itemsize, mult):
    return _pick_tile(rows, max(mult, (2 * 1024 * 1024) // (cols * itemsize)), mult)


def _cast_into_full(w, kc, shard_axis, name):
    r, cdim = w.shape
    tr = _block_rows(r, cdim, 4, 16)
    nb = r // tr

    def body(kc_ref, w_ref, o_ref):
        o_ref[...] = w_ref[...].astype(BF16)

    if shard_axis == 1:
        full, o_spec = (r, 4 * cdim), pl.BlockSpec((tr, cdim), lambda i, kc: (i, kc[0]))
    else:
        full, o_spec = (4 * r, cdim), pl.BlockSpec((tr, cdim), lambda i, kc: (kc[0] * nb + i, 0))
    return pl.pallas_call(
        body, name=name,
        grid_spec=pltpu.PrefetchScalarGridSpec(
            num_scalar_prefetch=1, grid=(nb,), in_specs=[pl.BlockSpec((tr, cdim), lambda i, kc: (i, 0))],
            out_specs=o_spec),
        out_shape=jax.ShapeDtypeStruct(full, BF16), compiler_params=_cparams("parallel"))(kc, w)


def _pair_sum(g, recv, kc, half_axis, name, out_dtype):
    hr, hc = recv.shape
    tr = _block_rows(hr, hc, 4, 16)
    nb = hr // tr

    def body(kc_ref, g_ref, r_ref, o_ref):
        o_ref[...] = (g_ref[...] + r_ref[...]).astype(out_dtype)

    if half_axis == 0:
        g_spec = pl.BlockSpec((tr, hc), lambda i, kc: (kc[1] * nb + i, 0))
    elif half_axis == 1:
        g_spec = pl.BlockSpec((tr, hc), lambda i, kc: (i, kc[1]))
    else:
        g_spec = pl.BlockSpec((tr, hc), lambda i, kc: (i, 0))
    same = pl.BlockSpec((tr, hc), lambda i, kc: (i, 0))
    return pl.pallas_call(
        body, name=name,
        grid_spec=pltpu.PrefetchScalarGridSpec(num_scalar_prefetch=1, grid=(nb,), in_specs=[g_spec, same],
                                               out_specs=same),
        out_shape=jax.ShapeDtypeStruct((hr, hc), out_dtype), compiler_params=_cparams("parallel"))(kc, g, recv)


def _chip_sum(own, recv, kc, own_axis, out_axis, name):
    _, sr, sc = recv.shape
    tr = _block_rows(sr, sc, 4, 16)
    nb = sr // tr

    def body(kc_ref, o_ref, r_ref, t_ref):
        k = kc_ref[0]
        own_v = o_ref[...].astype(F32)
        r = [r_ref[m].astype(F32) for m in range(3)]
        terms = []
        for kk in range(4):
            m = jnp.bitwise_xor(k, kk)
            terms.append(jnp.where(m == 0, own_v, jnp.where(m == 1, r[0], jnp.where(m == 2, r[1], r[2]))))
        t_ref[...] = (terms[0] + terms[1]) + (terms[2] + terms[3])

    if own_axis == 0:
        own_spec = pl.BlockSpec((tr, sc), lambda i, kc: (kc[0] * nb + i, 0))
    elif own_axis == 1:
        own_spec = pl.BlockSpec((tr, sc), lambda i, kc: (i, kc[0]))
    else:
        own_spec = pl.BlockSpec((tr, sc), lambda i, kc: (kc[1] * nb + i, 0))
    if out_axis == 0:
        out_full, out_spec = (2 * sr, sc), pl.BlockSpec((tr, sc), lambda i, kc: (kc[1] * nb + i, 0))
    else:
        out_full, out_spec = (sr, 2 * sc), pl.BlockSpec((tr, sc), lambda i, kc: (i, kc[1]))
    return pl.pallas_call(
        body, name=name,
        grid_spec=pltpu.PrefetchScalarGridSpec(
            num_scalar_prefetch=1, grid=(nb,),
            in_specs=[own_spec, pl.BlockSpec((3, tr, sc), lambda i, kc: (0, i, 0))],
            out_specs=out_spec),
        out_shape=jax.ShapeDtypeStruct(out_full, F32), compiler_params=_cparams("parallel"))(kc, own, recv)


def _adamw(w, g, m, v, name):
    r, cdim = w.shape
    tr = _block_rows(r, cdim, 4, 8)
    c1 = 1.0 - ADAM_B1 ** ADAM_STEP
    c2 = 1.0 - ADAM_B2 ** ADAM_STEP

    def body(w_ref, g_ref, m_ref, v_ref, go_ref, d_ref, nm_ref, nv_ref):
        gv = g_ref[...]
        go_ref[...] = gv
        nm = ADAM_B1 * m_ref[...] + (1.0 - ADAM_B1) * gv
        nv = ADAM_B2 * v_ref[...] + (1.0 - ADAM_B2) * (gv * gv)
        d_ref[...] = -ADAM_LR * ((nm / c1) / (jnp.sqrt(nv / c2) + ADAM_EPS) + ADAM_WD * w_ref[...])
        nm_ref[...] = nm
        nv_ref[...] = nv

    spec = _rows(cdim, tr)
    return pl.pallas_call(body, name=name, grid=(r // tr,), in_specs=[spec] * 4, out_specs=[spec] * 4,
                          out_shape=[jax.ShapeDtypeStruct((r, cdim), F32)] * 4,
                          compiler_params=_cparams("parallel"))(w, g, m, v)


def _adamw_whole(ws, gs, ms, vs, name):
    n = len(ws)
    c1 = 1.0 - ADAM_B1 ** ADAM_STEP
    c2 = 1.0 - ADAM_B2 ** ADAM_STEP

    def body(*refs):
        for i in range(n):
            w_ref, g_ref, m_ref, v_ref, d_ref, nm_ref, nv_ref = [refs[j * n + i] for j in range(7)]
            gv = g_ref[...]
            nm = ADAM_B1 * m_ref[...] + (1.0 - ADAM_B1) * gv
            nv = ADAM_B2 * v_ref[...] + (1.0 - ADAM_B2) * (gv * gv)
            d_ref[...] = -ADAM_LR * ((nm / c1) / (jnp.sqrt(nv / c2) + ADAM_EPS) + ADAM_WD * w_ref[...])
            nm_ref[...] = nm
            nv_ref[...] = nv

    vmem = pl.BlockSpec(memory_space=pltpu.VMEM)
    out = pl.pallas_call(body, name=name, in_specs=[vmem] * (4 * n), out_specs=[vmem] * (3 * n),
                         out_shape=[jax.ShapeDtypeStruct(a.shape, F32) for a in ws] * 3,
                         compiler_params=pltpu.CompilerParams(vmem_limit_bytes=VMEM_LIMIT))(*ws, *gs, *ms, *vs)
    return out[:n], out[n:2 * n], out[2 * n:]


SIDE_EFFECT = pltpu.SideEffectType.DATAFLOW_SIDE_EFFECTING


def _descriptors(copies, refs, send_sems, recv_sems, sem_off=0):
    x, y, c = lax.axis_index("x"), lax.axis_index("y"), lax.axis_index("c")
    pos = (x, y, c, 2 * x + y)
    out = []
    for i, (s, d, flip) in enumerate(copies):
        peer = (1 - x if "x" in flip else x, 1 - y if "y" in flip else y, 1 - c if "c" in flip else c)
        out.append(pltpu.make_async_remote_copy(
            src_ref=s(refs, refs, pos), dst_ref=d(refs, refs, pos),
            send_sem=send_sems.at[sem_off + i], recv_sem=recv_sems.at[sem_off + i],
            device_id=peer, device_id_type=MESH))
    return out


def _shifted(copies, off):
    return [(lambda I, O, pos, s=s: s(I[off:], O[off:], pos), lambda I, O, pos, d=d: d(I[off:], O[off:], pos), flip)
            for s, d, flip in copies]


BARRIER_IDS = {"c": (1, 2), "ici": (3, 4)}


def _exchange_start(name, bufs, copies, turns, after=None):
    n, nr = len(bufs), len(copies)
    na = 0 if after is None else 1
    flips = sorted({flip for _, _, flip in copies})
    kind = "c" if flips == ["c"] else "ici"
    collective_id = BARRIER_IDS[kind][turns[kind] % 2]
    turns[kind] += 1

    def body(*refs):
        x, y, c = lax.axis_index("x"), lax.axis_index("y"), lax.axis_index("c")
        barrier = pltpu.get_barrier_semaphore()
        for flip in flips:
            peer = (1 - x if "x" in flip else x, 1 - y if "y" in flip else y, 1 - c if "c" in flip else c)
            pl.semaphore_signal(barrier, inc=1, device_id=peer, device_id_type=MESH)
        pl.semaphore_wait(barrier, len(flips))
        for cp in _descriptors(copies, refs[:n], refs[n + na], refs[n + na + 1]):
            cp.start()
        token = refs[2 * n + na + 2]
        token[...] = jnp.zeros_like(token)

    hbm = pl.BlockSpec(memory_space=pltpu.HBM)
    sem = pl.BlockSpec(memory_space=pltpu.SEMAPHORE)
    out = pl.pallas_call(
        body, name=name,
        in_specs=[hbm] * n + [pl.BlockSpec(memory_space=pl.ANY)] * na,
        out_specs=(sem, sem, *[hbm] * n, pl.BlockSpec(memory_space=pltpu.VMEM)),
        out_shape=(pltpu.SemaphoreType.DMA((nr,)), pltpu.SemaphoreType.DMA((nr,)),
                   *[pltpu.HBM(b.shape, b.dtype) for b in bufs], jax.ShapeDtypeStruct((SUBLANES, LANES), F32)),
        input_output_aliases={i: 2 + i for i in range(n)},
        compiler_params=pltpu.CompilerParams(has_side_effects=SIDE_EFFECT, collective_id=collective_id),
    )(*[pltpu.with_memory_space_constraint(b, pltpu.HBM) for b in bufs], *([after] * na))
    return out[0], out[1], list(out[2:2 + n]), out[2 + n]


def _exchange_wait(name, send_sems, recv_sems, bufs, copies, after, sem_off=0):
    n = len(bufs)

    def body(*refs):
        for cp in _descriptors(copies, refs[:n], refs[n], refs[n + 1], sem_off):
            cp.wait_send()
            cp.wait_recv()

    hbm = pl.BlockSpec(memory_space=pltpu.HBM)
    sem = pl.BlockSpec(memory_space=pltpu.SEMAPHORE)
    out = pl.pallas_call(
        body, name=name,
        in_specs=[hbm] * n + [sem, sem, pl.BlockSpec(memory_space=pl.ANY)],
        out_specs=tuple([hbm] * n),
        out_shape=tuple(pltpu.HBM(b.shape, b.dtype) for b in bufs),
        input_output_aliases={i: i for i in range(n)},
        compiler_params=pltpu.CompilerParams(has_side_effects=SIDE_EFFECT),
    )(*bufs, send_sems, recv_sems, after)
    return list(out)


FIRST = ("w_in",)
MID = ("ssm_w_glu", "w_out")
LATE = ("w_up", "w_down")
GROUPS = {"first": FIRST, "mid": MID, "late": LATE}


def _gather_copies(names, shard_shapes):
    def region(i, chip, c):
        half_axis, shard_axis = BIG[names[i]]
        ssize = shard_shapes[i][shard_axis]
        hsize = shard_shapes[i][half_axis] // 2
        return lambda ref: _view(_view(ref, shard_axis, chip * ssize, ssize), half_axis, c * hsize, hsize)

    ici, d2d = [], []
    for i in range(len(names)):
        for flip in FLIPS:
            ici.append((lambda I, O, pos, i=i: region(i, pos[3], pos[2])(I[i]),
                        lambda I, O, pos, i=i: region(i, pos[3], pos[2])(O[i]), flip))
            d2d.append((lambda I, O, pos, i=i, flip=flip: region(i, _peer_chip(pos, flip), pos[2])(I[i]),
                        lambda I, O, pos, i=i, flip=flip: region(i, _peer_chip(pos, flip), pos[2])(O[i]), "c"))
    return ici, d2d


def _half_shape(n, shape):
    r, cdim = shape
    return (r // 2, cdim) if BIG[n][0] == 0 else (r, cdim // 2)


def _sub_shape(n, shape):
    hr, hc = _half_shape(n, shape)
    return (hr, hc // 4) if BIG[n][1] == 1 else (hr // 4, hc)


def _pair_copies(names, shapes, with_pack, dst_off):
    n = len(names)

    def other_half(i, ref, pos):
        half_axis = BIG[names[i]][0]
        hsize = shapes[i][half_axis] // 2
        return _view(ref, half_axis, (1 - pos[2]) * hsize, hsize)

    copies = [(lambda I, O, pos, i=i: other_half(i, I[i], pos), lambda I, O, pos, i=i: O[dst_off + i], "c")
              for i in range(n)]
    if with_pack:
        copies.append((lambda I, O, pos: I[n], lambda I, O, pos: O[dst_off + n], "c"))
    return copies


def _chip_copies(names, shapes, pack_rows, dst_off):
    n = len(names)

    def piece(i, ref, chip):
        shard_axis = BIG[names[i]][1]
        ssize = _sub_shape(names[i], shapes[i])[shard_axis]
        return _view(ref, shard_axis, chip * ssize, ssize)

    copies = []
    for i in range(n):
        for slot, flip in enumerate(FLIPS):
            copies.append((lambda I, O, pos, i=i, flip=flip: piece(i, I[i], _peer_chip(pos, flip)),
                           lambda I, O, pos, i=i, slot=slot: O[dst_off + i].at[slot], flip))
    if pack_rows:
        for slot, flip in enumerate(FLIPS):
            copies.append((lambda I, O, pos: _view(I[n], 0, pos[2] * (pack_rows // 2), pack_rows // 2),
                           lambda I, O, pos, slot=slot: O[dst_off + n].at[slot], flip))
    return copies


class _Exchanges:
    def __init__(self, shards, tiny, kc):
        self.kc = kc
        wb = {n: _cast_into_full(shards[n], kc, BIG[n][1], "cast_" + n) for n in BIG_NAMES}
        self.gathering, self.forwarding, self.pairing, self.reducing = {}, {}, {}, {}
        self.turns = {"c": 0, "ici": 0}
        tiny_copies = [(lambda I, O, pos: I[0], lambda I, O, pos: O[1].at[pos[3]], flip) for flip in FLIPS]
        self.gathering["tiny"] = (0, 0, 2, tiny_copies, None)
        bufs, copies = [tiny, lax.empty((4,) + tiny.shape, F32)], list(tiny_copies)
        for group, names in GROUPS.items():
            ici, d2d = _gather_copies(names, [shards[n].shape for n in names])
            self.gathering[group] = (len(bufs), len(copies), len(names), ici, d2d)
            copies += _shifted(ici, len(bufs))
            bufs += [wb[n] for n in names]
        self.started = _exchange_start("gather_start", bufs, copies, self.turns)
        self.zero = self.started[3][0, 0]

    def _arrived(self, group, after):
        buf_off, sem_off, n, ici, _ = self.gathering[group]
        send_sems, recv_sems, bufs, _ = self.started
        return _exchange_wait("gather_%s_wait" % group, send_sems, recv_sems, bufs[buf_off:buf_off + n], ici, after,
                              sem_off)

    def small_params(self, kc):
        tiny, got = self._arrived("tiny", self.started[3])
        return lax.dynamic_update_index_in_dim(got, tiny, kc[0], 0)

    def forward(self, group, after):
        d2d = self.gathering[group][4]
        self.forwarding[group] = (_exchange_start("forward_%s_start" % group, self._arrived(group, after), d2d,
                                                  self.turns), d2d)
        return self.forwarding[group][0][3]

    def weights(self, group, after):
        if group not in self.forwarding:
            after = self.forward(group, after)
        (send_sems, recv_sems, bufs, _), d2d = self.forwarding[group]
        full = _exchange_wait("forward_%s_wait" % group, send_sems, recv_sems, bufs, d2d, after)
        return dict(zip(GROUPS[group], full))

    def grads_ready(self, group, grads):
        names = GROUPS[group]
        gs = [grads[n] for n in names]
        land = [lax.empty(_half_shape(n, g.shape), F32) for n, g in zip(names, gs)]
        copies = _pair_copies(names, [g.shape for g in gs], False, len(names))
        started = _exchange_start("pair_%s_start" % group, gs + land, copies, self.turns)
        self.pairing[group] = (started, copies)
        return started[3]

    def grads_send(self, group, after):
        names = GROUPS[group]
        n = len(names)
        (send_sems, recv_sems, bufs, _), copies = self.pairing[group]
        bufs = _exchange_wait("pair_%s_wait" % group, send_sems, recv_sems, bufs, copies, after)
        chip = [_pair_sum(bufs[i], bufs[n + i], self.kc, BIG[names[i]][0], "pair_sum_" + names[i], BF16)
                for i in range(n)]
        shapes = [bufs[i].shape for i in range(n)]
        land = [lax.empty((3,) + _sub_shape(names[i], shapes[i]), BF16) for i in range(n)]
        copies = _chip_copies(names, shapes, 0, n)
        started = _exchange_start("reduce_%s_start" % group, chip + land, copies, self.turns)
        self.reducing[group] = (started, copies)
        return started[3]

    def finish_pack(self, pack):
        kc = self.kc
        prow = pack.shape[0] // 2
        recv = _exchange("reduce_d2d", [pack], [jax.ShapeDtypeStruct(pack.shape, F32)], {}, [],
                         _pair_copies((), [], True, 0))
        chip_pack = _pair_sum(pack, recv[0], kc, None, "pair_sum_pack", F32)
        copies = _chip_copies((), [], pack.shape[0], 1)
        land = lax.empty((3, prow, pack.shape[1]), F32)
        pack_sems_s, pack_sems_r, pack_bufs, after = _exchange_start("reduce_pack_start", [chip_pack, land], copies,
                                                                     self.turns)

        names, chips, recvs = (), [], []
        for group, group_names in GROUPS.items():
            (send_sems, recv_sems, bufs, _), group_copies = self.reducing[group]
            bufs = _exchange_wait("reduce_%s_wait" % group, send_sems, recv_sems, bufs, group_copies, after)
            n = len(group_names)
            names, chips, recvs = names + group_names, chips + bufs[:n], recvs + bufs[n:]
            after = bufs[n]
        total = [_chip_sum(chips[i], recvs[i], kc, BIG[n][1], BIG[n][0], "chip_sum_" + n)
                 for i, n in enumerate(names)]

        def my_half(half_axis, ref, pos):
            hsize = ref.shape[half_axis] // 2
            return _view(ref, half_axis, pos[2] * hsize, hsize)

        swap = [(lambda I, O, pos, i=i, n=n: my_half(BIG[n][0], I[i], pos),
                 lambda I, O, pos, i=i, n=n: my_half(BIG[n][0], O[i], pos), "c") for i, n in enumerate(names)]
        self.swapping = (_exchange_start("swap_start", total, swap, self.turns), swap, names)

        chip_pack, recv_pack = _exchange_wait("reduce_pack_wait", pack_sems_s, pack_sems_r, pack_bufs, copies,
                                              self.swapping[0][3])
        total_pack = _chip_sum(chip_pack, recv_pack, kc, None, 0, "chip_sum_pack")
        swap = [(lambda I, O, pos: my_half(0, I[0], pos), lambda I, O, pos: my_half(0, O[0], pos), "c")]
        return _exchange("swap_pack", [total_pack], [jax.ShapeDtypeStruct(pack.shape, F32)], {0: 0}, [], swap)[0]

    def finish_big(self, after):
        (send_sems, recv_sems, bufs, _), swap, names = self.swapping
        return dict(zip(names, _exchange_wait("swap_wait", send_sems, recv_sems, bufs, swap, after)))


WEIGHTS = ("meta_tokens", "norm_mix_g", "w_in", "conv_w", "ssm_lam_re", "ssm_lam_im", "ssm_log_dt", "ssm_b_re",
           "ssm_b_im", "ssm_c_re", "ssm_c_im", "ssm_d", "ssm_w_glu", "gain_conv_out", "gain_ssm_out", "w_out",
           "norm_ffn_g", "w_up", "ffn_conv_w", "ffn_conv_b", "w_down", "norm_final_g")
TINY_SHARDED = ("meta_tokens", "conv_w", "ffn_conv_w")
REPLICATED = tuple(n for n in WEIGHTS if n not in BIG and n not in TINY_SHARDED)
PACK_COLS = 512


def _pack(arrays, row_mult, cols):
    flat = jnp.concatenate([a.reshape(-1).astype(F32) for a in arrays])
    n = flat.shape[0]
    total = -(-n // (row_mult * cols)) * (row_mult * cols)
    return jnp.concatenate([flat, jnp.zeros((total - n,), F32)]).reshape(total // cols, cols)


def _unpack(packed, shapes):
    flat = packed.reshape(-1)
    out, off = [], 0
    for s in shapes:
        n = math.prod(s)
        out.append(flat[off:off + n].reshape(s))
        off += n
    return out


def kernel(x, meta_tokens, norm_mix_g, w_in, conv_w, ssm_lam_re, ssm_lam_im, ssm_log_dt, ssm_b_re, ssm_b_im, ssm_c_re, ssm_c_im, ssm_d, ssm_w_glu, gain_conv_out, gain_ssm_out, w_out, norm_ffn_g, w_up, ffn_conv_w, ffn_conv_b, w_down, norm_final_g, loss_target, m_meta_tokens, m_norm_mix_g, m_w_in, m_conv_w, m_ssm_lam_re, m_ssm_lam_im, m_ssm_log_dt, m_ssm_b_re, m_ssm_b_im, m_ssm_c_re, m_ssm_c_im, m_ssm_d, m_ssm_w_glu, m_gain_conv_out, m_gain_ssm_out, m_w_out, m_norm_ffn_g, m_w_up, m_ffn_conv_w, m_ffn_conv_b, m_w_down, m_norm_final_g, v_meta_tokens, v_norm_mix_g, v_w_in, v_conv_w, v_ssm_lam_re, v_ssm_lam_im, v_ssm_log_dt, v_ssm_b_re, v_ssm_b_im, v_ssm_c_re, v_ssm_c_im, v_ssm_d, v_ssm_w_glu, v_gain_conv_out, v_gain_ssm_out, v_w_out, v_norm_ffn_g, v_w_up, v_ffn_conv_w, v_ffn_conv_b, v_w_down, v_norm_final_g):
    args = dict(locals())
    w = {n: args[n] for n in WEIGHTS}
    mom = {n: args["m_" + n] for n in WEIGHTS}
    var = {n: args["v_" + n] for n in WEIGHTS}
    kx, ky, kc_ = lax.axis_index("x"), lax.axis_index("y"), lax.axis_index("c")
    chip = 2 * kx + ky
    kc = jnp.stack([chip, kc_]).astype(jnp.int32)

    def squeeze(n, a):
        if n == "meta_tokens":
            return a
        if n == "norm_final_g":
            return a.reshape(1, -1)
        a = a[0]
        return a.reshape(1, -1) if a.ndim == 1 else a

    wl = {n: squeeze(n, w[n]) for n in WEIGHTS}
    ml = {n: squeeze(n, mom[n]) for n in WEIGHTS}
    vl = {n: squeeze(n, var[n]) for n in WEIGHTS}

    tiny = _pack([wl[n] for n in TINY_SHARDED], SUBLANES, LANES)
    ex = _Exchanges({n: wl[n] for n in BIG_NAMES}, tiny, kc)
    tiny_shapes = [wl[n].shape for n in TINY_SHARDED]
    tiny_all = ex.small_params(kc)
    tiny_parts = [_unpack(tiny_all[k], tiny_shapes) for k in range(4)]
    p = {n: wl[n] for n in WEIGHTS if n not in BIG}
    for j, n in enumerate(TINY_SHARDED):
        p[n] = jnp.concatenate([tiny_parts[k][j] for k in range(4)], axis=1)
    p["ssm_log_dt"] = wl["ssm_log_dt"].reshape(-1)

    loss_local, grad_x, grads = _local_step(x[0], loss_target[0], p, ex)

    small_names = REPLICATED + TINY_SHARDED
    small_shapes = [tuple(grads[n].shape) for n in small_names] + [(1,)]
    pack = _pack([grads[n] for n in small_names] + [loss_local.reshape(1)], 2 * 16, PACK_COLS)
    g_pack = ex.finish_pack(pack)
    g_small = dict(zip(small_names + ("loss",), _unpack(g_pack, small_shapes)))
    loss = g_small["loss"][0]
    swapped = ("ssm_b_re", "ssm_b_im")

    def view(n, a):
        if n in swapped:
            return jnp.swapaxes(a, -1, -2)
        return a.reshape(1, -1) if a.ndim == 1 else a

    g = {}
    for n in REPLICATED:
        g[n] = g_small[n].reshape(view(n, w[n]).shape)
    for n in TINY_SHARDED:
        cols = wl[n].shape[1]
        g[n] = lax.dynamic_slice_in_dim(g_small[n], chip * cols, cols, axis=1).reshape(w[n].shape)
    delta, new_m, new_v = {}, {}, {}
    small = [[view(n, d[n]) for n in small_names] for d in (w, mom, var)]
    small.insert(1, [g[n] for n in small_names])
    for d, outs in zip((delta, new_m, new_v), _adamw_whole(*small, "adamw_small")):
        d.update(zip(small_names, outs))
    for d in (g, delta, new_m, new_v):
        d.update({n: jnp.swapaxes(d[n], -1, -2) for n in swapped})
    g_big = ex.finish_big(delta[small_names[0]])
    for n in BIG_NAMES:
        g[n], delta[n], new_m[n], new_v[n] = _adamw(wl[n], g_big[n], ml[n], vl[n], "adamw_" + n)

    def like(n, a):
        return a.reshape(w[n].shape)

    return (loss, grad_x[None], *[like(n, g[n]) for n in WEIGHTS], *[like(n, delta[n]) for n in WEIGHTS],
            *[like(n, new_m[n]) for n in WEIGHTS], *[like(n, new_v[n]) for n in WEIGHTS])
```

```python
import functools
import math

import jax
import jax.numpy as jnp
from jax import lax
from jax.experimental import pallas as pl
from jax.experimental.pallas import tpu as pltpu

F32 = jnp.float32
BF16 = jnp.bfloat16
MESH = pl.DeviceIdType.MESH

N_META = 16
N_GROUPS = 32
GROUP = 16
STATE = 64
RMS_EPS = 1e-6
ADAM_LR = 0.001
ADAM_B1 = 0.9
ADAM_B2 = 0.999
ADAM_EPS = 1e-08
ADAM_WD = 0.01
ADAM_STEP = 10

LANES = 128
SUBLANES = 8
ROW_ALIGN = 128
ROW_TILES = 4
VMEM_LIMIT = 52 * 1024 * 1024
MM_VMEM_BUDGET = 40 * 1024 * 1024
GELU_C = math.sqrt(2.0 / math.pi)
GELU_A = 0.044715


def _cparams(*sem):
    return pltpu.CompilerParams(dimension_semantics=sem, vmem_limit_bytes=VMEM_LIMIT)


def _pick_tile(dim, cap, mult):
    best = None
    for t in range(mult, min(dim, cap) + 1, mult):
        if dim % t == 0:
            best = t
    return best if best is not None else dim


def _mm(a, b, mode, name, out_dtype=F32, acc_in=None, after=None):
    if mode == "tn":
        kdim, m = a.shape
    else:
        m, kdim = a.shape
    n = b.shape[0] if mode == "nt" else b.shape[1]
    tm = _pick_tile(m, 1408, LANES if mode == "tn" else 16)
    tk = _pick_tile(kdim, 2816, LANES)
    nk = kdim // tk
    out_bytes = jnp.dtype(out_dtype).itemsize
    for cap in (1408, 1024, 512, 256, LANES):
        tn = _pick_tile(n, cap, LANES)
        blocks = 2 * (tm * tk * 2 + tk * tn * 2 + tm * tn * out_bytes * (2 if acc_in is not None else 1))
        if blocks + (tm * tn * 4 if nk > 1 else 0) <= MM_VMEM_BUDGET:
            break
    has_acc = acc_in is not None

    def body(*refs):
        if after is not None:
            refs = refs[1:]
        if has_acc:
            a_ref, b_ref, c_ref, o_ref = refs[:4]
            rest = refs[4:]
        else:
            a_ref, b_ref, o_ref = refs[:3]
            c_ref = None
            rest = refs[3:]
        if mode == "nn":
            p = jnp.dot(a_ref[...], b_ref[...], preferred_element_type=F32)
        elif mode == "nt":
            p = lax.dot_general(a_ref[...], b_ref[...], (((1,), (1,)), ((), ())), preferred_element_type=F32)
        else:
            p = lax.dot_general(a_ref[...], b_ref[...], (((0,), (0,)), ((), ())), preferred_element_type=F32)
        if nk == 1:
            if has_acc:
                p = p + c_ref[...]
            o_ref[...] = p.astype(out_dtype)
        else:
            acc_ref = rest[0]
            k = pl.program_id(2)

            @pl.when(k == 0)
            def _():
                acc_ref[...] = p + c_ref[...] if has_acc else p

            @pl.when(k > 0)
            def _():
                acc_ref[...] += p

            @pl.when(k == nk - 1)
            def _():
                o_ref[...] = acc_ref[...].astype(out_dtype)

    if mode == "tn":
        a_spec = pl.BlockSpec((tk, tm), lambda i, j, k: (k, i))
    else:
        a_spec = pl.BlockSpec((tm, tk), lambda i, j, k: (i, k))
    if mode == "nt":
        b_spec = pl.BlockSpec((tn, tk), lambda i, j, k: (j, k))
    else:
        b_spec = pl.BlockSpec((tk, tn), lambda i, j, k: (k, j))
    o_spec = pl.BlockSpec((tm, tn), lambda i, j, k: (i, j))
    in_specs = [a_spec, b_spec] + ([o_spec] if has_acc else [])
    args = (a, b) + ((acc_in,) if has_acc else ())
    if after is not None:
        in_specs = [pl.BlockSpec(memory_space=pl.ANY)] + in_specs
        args = (after,) + args
    return pl.pallas_call(
        body, name=name, grid=(m // tm, n // tn, nk),
        in_specs=in_specs, out_specs=o_spec,
        out_shape=jax.ShapeDtypeStruct((m, n), out_dtype),
        scratch_shapes=[pltpu.VMEM((tm, tn), F32)] if nk > 1 else [],
        compiler_params=_cparams("parallel", "parallel", "arbitrary"),
    )(*args)


def _mm_rows(a, b, mode, name, ins, outs, epilogue, scratch=()):
    m, kdim = a.shape
    n = b.shape[0] if mode == "nt" else b.shape[1]
    tm = m // ROW_TILES
    tk = _pick_tile(kdim, 2816, LANES)
    nk = kdim // tk
    ni, no = len(ins), len(outs)

    def body(*refs):
        a_ref, b_ref = refs[:2]
        in_refs, out_refs, rest = refs[2:2 + ni], refs[2 + ni:2 + ni + no], refs[2 + ni + no:]
        i = pl.program_id(0)
        if mode == "nn":
            p = jnp.dot(a_ref[...], b_ref[...], preferred_element_type=F32)
        else:
            p = lax.dot_general(a_ref[...], b_ref[...], (((1,), (1,)), ((), ())), preferred_element_type=F32)
        if nk == 1:
            epilogue(p, i, in_refs, out_refs, rest)
        else:
            acc_ref = rest[0]
            k = pl.program_id(1)

            @pl.when(k == 0)
            def _():
                acc_ref[...] = p

            @pl.when(k > 0)
            def _():
                acc_ref[...] += p

            @pl.when(k == nk - 1)
            def _():
                epilogue(acc_ref[...], i, in_refs, out_refs, rest[1:])

    def spec(shape, kind):
        if kind == "rows":
            return pl.BlockSpec((tm,) + tuple(shape[1:]), lambda i, k: (i,) + (0,) * (len(shape) - 1))
        if kind == "whole":
            return pl.BlockSpec(tuple(shape), lambda i, k: (0,) * len(shape))
        return pl.BlockSpec(memory_space=pl.ANY)

    a_spec = pl.BlockSpec((tm, tk), lambda i, k: (i, k))
    b_spec = pl.BlockSpec((n, tk), lambda i, k: (0, k)) if mode == "nt" else pl.BlockSpec((tk, n), lambda i, k: (k, 0))
    return pl.pallas_call(
        body, name=name, grid=(ROW_TILES, nk),
        in_specs=[a_spec, b_spec] + [spec(x.shape, kind) for x, kind in ins],
        out_specs=[spec(shape, kind) for shape, _, kind in outs],
        out_shape=[jax.ShapeDtypeStruct(shape, dtype) for shape, dtype, _ in outs],
        scratch_shapes=([pltpu.VMEM((tm, n), F32)] if nk > 1 else []) + list(scratch),
        compiler_params=_cparams("arbitrary", "arbitrary"),
    )(a, b, *[x for x, _ in ins])


def _rows(shape_cols, tr, dtype=None):
    return pl.BlockSpec((tr, shape_cols), lambda i: (i, 0))


def _const(shape):
    return pl.BlockSpec(shape, lambda i: (0,) * len(shape))


def _rms(x):
    return lax.rsqrt(jnp.mean(x * x, axis=-1, keepdims=True) + RMS_EPS)


def _rms_bwd(x, r, g, dy):
    xn = x * r
    dxn = dy * g
    dx = r * (dxn - xn * jnp.mean(dxn * xn, axis=-1, keepdims=True))
    return dx, dy * xn


def _gelu(y):
    return 0.5 * y * (1.0 + jnp.tanh(GELU_C * (y + GELU_A * y * y * y)))


def _gelu_grad(y):
    t = jnp.tanh(GELU_C * (y + GELU_A * y * y * y))
    return 0.5 * (1.0 + t) + 0.5 * y * (1.0 - t * t) * GELU_C * (1.0 + 3.0 * GELU_A * y * y)


def _sigmoid(z):
    return 1.0 / (1.0 + jnp.exp(-z))


def _proj_res_norm(a, w, h, g, after, name):
    def epilogue(p, i, ins, outs, _):
        x = ins[0][...] + p
        outs[0][...] = x
        outs[1][...] = (x * _rms(x) * ins[1][...]).astype(BF16)

    return _mm_rows(a, w, "nn", name, [(h, "rows"), (g, "whole"), (after, "hbm")],
                    [(h.shape, F32, "rows"), (h.shape, BF16, "rows")], epilogue)


def _proj_norm_bwd(da, w, h, g, dres, after, name):
    d = h.shape[1]

    def epilogue(p, i, ins, outs, _):
        x = ins[0][...]
        dx, dgs = _rms_bwd(x, _rms(x), ins[1][...], p)
        dh = ins[2][...] + dx
        outs[0][...] = dh
        outs[1][...] = dh.astype(BF16)

        @pl.when(i == 0)
        def _():
            outs[2][...] = jnp.zeros_like(outs[2])

        outs[2][...] += jnp.sum(dgs, axis=0, keepdims=True)

    return _mm_rows(da, w, "nt", name, [(h, "rows"), (g, "whole"), (dres, "rows"), (after, "hbm")],
                    [(h.shape, F32, "rows"), (h.shape, BF16, "rows"), ((1, d), F32, "whole")], epilogue)


def _input_norm_bwd(h, g, dhn, dres, n_real, name):
    tp, d = h.shape
    tr = tp // ROW_TILES

    def body(h_ref, g_ref, dhn_ref, dres_ref, dx_ref, dmeta_ref, dg_ref, stage, sem):
        i = pl.program_id(0)
        x = h_ref[...]
        dx, dgs = _rms_bwd(x, _rms(x), g_ref[...], dhn_ref[...])
        stage[...] = dres_ref[...] + dx

        @pl.when(i == 0)
        def _():
            dg_ref[...] = jnp.zeros_like(dg_ref)
            dmeta_ref[...] = stage[:N_META, :]

        dg_ref[...] += jnp.sum(dgs, axis=0, keepdims=True)
        for t in range(ROW_TILES):
            lo, hi = max(t * tr, N_META), min((t + 1) * tr, n_real)
            if hi > lo:
                @pl.when(i == t)
                def _(t=t, lo=lo, hi=hi):
                    cp = pltpu.make_async_copy(stage.at[pl.ds(lo - t * tr, hi - lo), :],
                                               dx_ref.at[pl.ds(lo - N_META, hi - lo), :], sem)
                    cp.start()
                    cp.wait()

    return pl.pallas_call(
        body, name=name, grid=(ROW_TILES,),
        in_specs=[_rows(d, tr), _const((1, d)), _rows(d, tr), _rows(d, tr)],
        out_specs=[pl.BlockSpec(memory_space=pl.ANY), _const((N_META, d)), _const((1, d))],
        out_shape=[jax.ShapeDtypeStruct((n_real - N_META, d), F32), jax.ShapeDtypeStruct((N_META, d), F32),
                   jax.ShapeDtypeStruct((1, d), F32)],
        scratch_shapes=[pltpu.VMEM((tr, d), F32), pltpu.SemaphoreType.DMA],
        compiler_params=_cparams("arbitrary"))(h, g, dhn, dres)


def _load_token_rows(tok_hbm, buf, sem, tr, n_real, head=None, wait=False, i=None):
    i = pl.program_id(0) if i is None else i
    for t in range(ROW_TILES):
        base = t * tr
        lo, hi = max(base, N_META), min(base + tr, n_real)

        @pl.when(i == t)
        def _(base=base, lo=lo, hi=hi):
            if hi > lo:
                cp = pltpu.make_async_copy(tok_hbm.at[pl.ds(lo - N_META, hi - lo), :],
                                           buf.at[pl.ds(lo - base, hi - lo), :], sem)
                if wait:
                    cp.wait()
                    return
                cp.start()
            if wait:
                return
            if base < N_META:
                buf[0:N_META - base, :] = (jnp.zeros((N_META - base, buf.shape[1]), F32) if head is None
                                           else head[base:N_META, :])
            if hi < base + tr:
                buf[max(hi, base) - base:tr, :] = jnp.zeros((base + tr - max(hi, base), buf.shape[1]), F32)


def _input_norm_fwd(x, meta, g, tp, name):
    seq, d = x.shape
    tr = tp // ROW_TILES
    n_real = N_META + seq

    def body(x_hbm, meta_ref, g_ref, h_ref, hn_ref, buf, sem):
        _load_token_rows(x_hbm, buf, sem, tr, n_real, head=meta_ref)
        _load_token_rows(x_hbm, buf, sem, tr, n_real, wait=True)
        h = buf[...]
        h_ref[...] = h
        hn_ref[...] = (h * _rms(h) * g_ref[...]).astype(BF16)

    return pl.pallas_call(
        body, name=name, grid=(ROW_TILES,),
        in_specs=[pl.BlockSpec(memory_space=pl.ANY), _const((N_META, d)), _const((1, d))],
        out_specs=[_rows(d, tr), _rows(d, tr)],
        out_shape=[jax.ShapeDtypeStruct((tp, d), F32), jax.ShapeDtypeStruct((tp, d), BF16)],
        scratch_shapes=[pltpu.VMEM((tr, d), F32), pltpu.SemaphoreType.DMA],
        compiler_params=_cparams("arbitrary"))(x, meta, g)


def _proj_loss_bwd(act, w, h1, target, g, n_real, name):
    tp, d = h1.shape
    tr = tp // ROW_TILES

    def epilogue(p, i, ins, outs, scratch):
        h1_ref, t_hbm, g_ref = ins
        loss_ref, dh_ref, dhb_ref, dg_ref = outs
        t_buf, sem = scratch
        _load_token_rows(t_hbm, t_buf, sem, tr, n_real, i=i)
        x = h1_ref[...] + p
        r = _rms(x)
        row = i * tr + lax.broadcasted_iota(jnp.int32, (tr, d), 0)
        valid = (row >= N_META) & (row < n_real)
        _load_token_rows(t_hbm, t_buf, sem, tr, n_real, wait=True, i=i)
        e = jnp.where(valid, x * r * g_ref[...] - t_buf[...], 0.0)
        dx, dgs = _rms_bwd(x, r, g_ref[...], e * (1.0 / d))
        dh_ref[...] = dx
        dhb_ref[...] = dx.astype(BF16)

        @pl.when(i == 0)
        def _():
            dg_ref[...] = jnp.zeros_like(dg_ref)
            loss_ref[...] = jnp.zeros_like(loss_ref)

        dg_ref[...] += jnp.sum(dgs, axis=0, keepdims=True)
        loss_ref[...] += (0.5 / d) * jnp.sum(jnp.sum(e * e, axis=0, keepdims=True), axis=1, keepdims=True)

    return _mm_rows(act, w, "nn", name, [(h1, "rows"), (target, "hbm"), (g, "whole")],
                    [((1, LANES), F32, "whole"), ((tp, d), F32, "rows"), ((tp, d), BF16, "rows"),
                     ((1, d), F32, "whole")],
                    epilogue, scratch=[pltpu.VMEM((tr, d), F32), pltpu.SemaphoreType.DMA])


def _mix_fwd(co, y, z, gc, gs, name):
    tp, dh = co.shape
    tr = tp // ROW_TILES

    def body(co_ref, y_ref, z_ref, gc_ref, gs_ref, m_ref):
        c = co_ref[...]
        m_ref[:, :dh] = (c * _rms(c) * gc_ref[...]).astype(BF16)
        so = _gelu(y_ref[...]) * _sigmoid(z_ref[...])
        m_ref[:, dh:] = (so * _rms(so) * gs_ref[...]).astype(BF16)

    return pl.pallas_call(
        body, name=name, grid=(ROW_TILES,),
        in_specs=[_rows(dh, tr)] * 3 + [_const((1, dh))] * 2,
        out_specs=_rows(2 * dh, tr),
        out_shape=jax.ShapeDtypeStruct((tp, 2 * dh), BF16),
        compiler_params=_cparams("parallel"))(co, y, z, gc, gs)


def _proj_mix_bwd(dh1b, w, co, y, z, gc, gs, name):
    tp, dh = co.shape

    def epilogue(p, i, ins, outs, _):
        co_ref, y_ref, z_ref, gc_ref, gs_ref = ins
        dco_ref, dz_ref, dgp_ref, dgc_ref, dgs_ref = outs
        c = co_ref[...]
        dco, dgc = _rms_bwd(c, _rms(c), gc_ref[...], p[:, :dh])
        dco_ref[...] = dco
        gl = _gelu(y_ref[...])
        sg = _sigmoid(z_ref[...])
        so = gl * sg
        dso, dgs = _rms_bwd(so, _rms(so), gs_ref[...], p[:, dh:])
        dz_ref[...] = (dso * gl * sg * (1.0 - sg)).astype(BF16)
        dgp_ref[...] = dso * sg

        @pl.when(i == 0)
        def _():
            dgc_ref[...] = jnp.zeros_like(dgc_ref)
            dgs_ref[...] = jnp.zeros_like(dgs_ref)

        dgc_ref[...] += jnp.sum(dgc, axis=0, keepdims=True)
        dgs_ref[...] += jnp.sum(dgs, axis=0, keepdims=True)

    return _mm_rows(dh1b, w, "nt", name,
                    [(co, "rows"), (y, "rows"), (z, "rows"), (gc, "whole"), (gs, "whole")],
                    [((tp, dh), F32, "rows"), ((tp, dh), BF16, "rows"), ((tp, dh), F32, "rows"),
                     ((1, dh), F32, "whole"), ((1, dh), F32, "whole")], epilogue)


def _shift_down(x, k):
    row = lax.broadcasted_iota(jnp.int32, x.shape, 0)
    return jnp.where(row >= k, pltpu.roll(x, k, 0), 0.0)


def _shift_up(x, k):
    n = x.shape[0]
    row = lax.broadcasted_iota(jnp.int32, x.shape, 0)
    return jnp.where(row < n - k, pltpu.roll(x, n - k, 0), 0.0)


def _dwconv(x, w_ref):
    return w_ref[2:3, :] * x + w_ref[1:2, :] * _shift_down(x, 1) + w_ref[0:1, :] * _shift_down(x, 2)


def _dwconv_bwd(x, dy, w_ref):
    dx = w_ref[2:3, :] * dy + w_ref[1:2, :] * _shift_up(dy, 1) + w_ref[0:1, :] * _shift_up(dy, 2)
    dw = jnp.concatenate([jnp.sum(dy * _shift_down(x, 2), axis=0, keepdims=True),
                          jnp.sum(dy * _shift_down(x, 1), axis=0, keepdims=True),
                          jnp.sum(dy * x, axis=0, keepdims=True)], axis=0)
    return dx, dw


def _interleave(dst, src):
    seg_rows = src.shape[0] // SUBLANES
    for seg in range(SUBLANES):
        dst[pl.ds(seg, seg_rows, stride=SUBLANES), :] = src[seg * seg_rows:(seg + 1) * seg_rows, :]


def _deinterleave(dst, src):
    seg_rows = src.shape[0] // SUBLANES
    for seg in range(SUBLANES):
        dst[seg * seg_rows:(seg + 1) * seg_rows, :] = src[pl.ds(seg, seg_rows, stride=SUBLANES), :]


def _segment_shift(x, reverse):
    row = lax.broadcasted_iota(jnp.int32, x.shape, 0)
    if reverse:
        return jnp.where(row < SUBLANES - 1, pltpu.roll(x, SUBLANES - 1, 0), 0.0)
    return jnp.where(row >= 1, pltpu.roll(x, 1, 0), 0.0)


def _scan(s_re, s_im, pw_ref, reverse):
    n_steps = s_re.shape[0] // SUBLANES
    n_strips = s_re.shape[1] // LANES
    sign = -1.0 if reverse else 1.0
    strips = [slice(st * LANES, (st + 1) * LANES) for st in range(n_strips)]

    def rows_of(j):
        step = (n_steps - 1 - j) if reverse else j
        return pl.ds(pl.multiple_of(step * SUBLANES, SUBLANES), SUBLANES)

    a = [(jnp.broadcast_to(pw_ref[0, 0:1, lanes], (SUBLANES, LANES)),
          sign * jnp.broadcast_to(pw_ref[1, 0:1, lanes], (SUBLANES, LANES))) for lanes in strips]

    def local(i, carry):
        for half in range(2):
            rows = rows_of(2 * i + half)
            out = []
            for st, lanes in enumerate(strips):
                (ar, ai), cr, ci = a[st], carry[2 * st], carry[2 * st + 1]
                xr = s_re[rows, lanes] + (ar * cr - ai * ci)
                xi = s_im[rows, lanes] + (ar * ci + ai * cr)
                s_re[rows, lanes] = xr
                s_im[rows, lanes] = xi
                out += [xr, xi]
            carry = tuple(out)
        return carry

    zero = jnp.zeros((SUBLANES, LANES), F32)
    ends = lax.fori_loop(0, n_steps // 2, local, (zero,) * (2 * n_strips))

    entering = []
    row = lax.broadcasted_iota(jnp.int32, (SUBLANES, LANES), 0)
    for st, lanes in enumerate(strips):
        tr, ti = ends[2 * st], ends[2 * st + 1]
        mr = jnp.broadcast_to(pw_ref[0, n_steps - 1:n_steps, lanes], (SUBLANES, LANES))
        mi = sign * jnp.broadcast_to(pw_ref[1, n_steps - 1:n_steps, lanes], (SUBLANES, LANES))
        for k in (1, 2, 4):
            keep = (row < SUBLANES - k) if reverse else (row >= k)
            rr = jnp.where(keep, pltpu.roll(tr, SUBLANES - k if reverse else k, 0), 0.0)
            ri = jnp.where(keep, pltpu.roll(ti, SUBLANES - k if reverse else k, 0), 0.0)
            tr, ti = tr + (mr * rr - mi * ri), ti + (mr * ri + mi * rr)
            mr, mi = mr * mr - mi * mi, 2.0 * mr * mi
        entering += [_segment_shift(tr, reverse), _segment_shift(ti, reverse)]

    def fix(i, carry):
        for half in range(2):
            rows = rows_of(2 * i + half)
            out = []
            for st, lanes in enumerate(strips):
                (ar, ai), cr, ci = a[st], carry[2 * st], carry[2 * st + 1]
                cr, ci = ar * cr - ai * ci, ar * ci + ai * cr
                s_re[rows, lanes] = s_re[rows, lanes] + cr
                s_im[rows, lanes] = s_im[rows, lanes] + ci
                out += [cr, ci]
            carry = tuple(out)
        return carry

    lax.fori_loop(0, n_steps // 2, fix, tuple(entering))


def _seq_fwd(proj, conv_w, bc_re, bc_im, cc_re, cc_im, dskip, a_pow, name):
    tp = proj.shape[0]
    dh = proj.shape[1] // 4
    nq = dh // LANES
    sw = STATE * N_GROUPS // nq

    def body(b_ref, c_ref, v_ref, u_ref, w_ref, bre_ref, bim_ref, cre_ref, cim_ref, d_ref, pw_ref,
             co_ref, y_ref, g_ref, s_re, s_im, u_il, y_il):
        co_ref[...] = b_ref[...] * _dwconv(c_ref[...] * v_ref[...], w_ref)
        _interleave(u_il, u_ref)
        ub = u_il[...].astype(BF16)
        s_re[...] = jnp.dot(ub, bre_ref[...], preferred_element_type=F32)
        s_im[...] = jnp.dot(ub, bim_ref[...], preferred_element_type=F32)
        _scan(s_re, s_im, pw_ref, False)
        y_il[...] = (jnp.dot(s_re[...].astype(BF16), cre_ref[...], preferred_element_type=F32)
                     - jnp.dot(s_im[...].astype(BF16), cim_ref[...], preferred_element_type=F32))
        _deinterleave(y_ref, y_il)
        y = y_ref[...] + d_ref[...] * u_ref[...]
        y_ref[...] = y
        g_ref[...] = _gelu(y).astype(BF16)

    col = lambda off: pl.BlockSpec((tp, LANES), lambda q, off=off: (0, off * nq + q))
    blk = pl.BlockSpec((tp, LANES), lambda q: (0, q))
    return pl.pallas_call(
        body, name=name, grid=(nq,),
        in_specs=[col(0), col(1), col(2), col(3),
                  pl.BlockSpec((3, LANES), lambda q: (0, q)),
                  pl.BlockSpec((LANES, sw), lambda q: (0, q)), pl.BlockSpec((LANES, sw), lambda q: (0, q)),
                  pl.BlockSpec((sw, LANES), lambda q: (q, 0)), pl.BlockSpec((sw, LANES), lambda q: (q, 0)),
                  pl.BlockSpec((1, LANES), lambda q: (0, q)),
                  pl.BlockSpec((2, tp // SUBLANES, sw), lambda q: (0, 0, q))],
        out_specs=[blk, blk, blk],
        out_shape=[jax.ShapeDtypeStruct((tp, dh), F32), jax.ShapeDtypeStruct((tp, dh), F32),
                   jax.ShapeDtypeStruct((tp, dh), BF16)],
        scratch_shapes=[pltpu.VMEM((tp, sw), F32), pltpu.VMEM((tp, sw), F32),
                        pltpu.VMEM((tp, LANES), F32), pltpu.VMEM((tp, LANES), F32)],
        compiler_params=_cparams("parallel"),
    )(proj, proj, proj, proj, conv_w, bc_re, bc_im, cc_re, cc_im, dskip, a_pow)


def _conv_bwd(proj, dco, conv_w, name):
    tp = proj.shape[0]
    dh = proj.shape[1] // 4
    nq = dh // LANES

    def body(b_ref, c_ref, v_ref, dco_ref, w_ref, dproj_ref, dw_ref, stage, sem):
        q = pl.program_id(0)
        cg = c_ref[...]
        vg = v_ref[...]
        cv = cg * vg
        dco_v = dco_ref[...]
        dcv, dw = _dwconv_bwd(cv, dco_v * b_ref[...], w_ref)
        dw_ref[...] = dw
        stage[0] = (dco_v * _dwconv(cv, w_ref)).astype(BF16)
        stage[1] = (dcv * vg).astype(BF16)
        stage[2] = (dcv * cg).astype(BF16)
        copies = [pltpu.make_async_copy(stage.at[p], dproj_ref.at[:, pl.ds((p * nq + q) * LANES, LANES)], sem.at[p])
                  for p in range(3)]
        for cp in copies:
            cp.start()
        for cp in copies:
            cp.wait()

    col = lambda off: pl.BlockSpec((tp, LANES), lambda q, off=off: (0, off * nq + q))
    return pl.pallas_call(
        body, name=name, grid=(nq,),
        in_specs=[col(0), col(1), col(2), pl.BlockSpec((tp, LANES), lambda q: (0, q)),
                  pl.BlockSpec((3, LANES), lambda q: (0, q))],
        out_specs=[pl.BlockSpec(memory_space=pl.ANY), pl.BlockSpec((3, LANES), lambda q: (0, q))],
        out_shape=[jax.ShapeDtypeStruct((tp, 4 * dh), BF16), jax.ShapeDtypeStruct((3, dh), F32)],
        scratch_shapes=[pltpu.VMEM((3, tp, LANES), BF16), pltpu.SemaphoreType.DMA((3,))],
        compiler_params=_cparams("arbitrary"),
    )(proj, proj, proj, dco, conv_w)


def _ssm_bwd(proj, y, dg, dproj, bc_re, bc_im, cc_re, cc_im, dskip, a_pow, name):
    tp = proj.shape[0]
    dh = proj.shape[1] // 4
    nq = dh // LANES
    sw = STATE * N_GROUPS // nq

    def body(u_ref, y_ref, dg_ref, dproj_in, bre_ref, bim_ref, cre_ref, cim_ref, d_ref, pw_ref,
             dproj_ref, dbre_ref, dbim_ref, dcre_ref, dcim_ref, dd_ref, dar_ref, dai_ref,
             s_re, s_im, l_re, l_im, a_il, b_il, stage, sem):
        del dproj_in
        q = pl.program_id(0)
        nt = (((1,), (1,)), ((), ()))
        tn = (((0,), (0,)), ((), ()))
        _interleave(a_il, u_ref)
        ub = a_il[...].astype(BF16)
        s_re[...] = jnp.dot(ub, bre_ref[...], preferred_element_type=F32)
        s_im[...] = jnp.dot(ub, bim_ref[...], preferred_element_type=F32)
        _scan(s_re, s_im, pw_ref, False)
        dy_rows = dg_ref[...] * _gelu_grad(y_ref[...])
        dd_ref[...] = jnp.sum(dy_rows * u_ref[...], axis=0, keepdims=True)
        _interleave(b_il, dy_rows)
        dy = b_il[...]
        dyb = dy.astype(BF16)
        l_re[...] = lax.dot_general(dyb, cre_ref[...], nt, preferred_element_type=F32)
        l_im[...] = -lax.dot_general(dyb, cim_ref[...], nt, preferred_element_type=F32)
        dcre_ref[...] = lax.dot_general(s_re[...].astype(BF16), dyb, tn, preferred_element_type=F32)
        dcim_ref[...] = -lax.dot_general(s_im[...].astype(BF16), dyb, tn, preferred_element_type=F32)
        _scan(l_re, l_im, pw_ref, True)
        rest = tp - SUBLANES
        for st in range(sw // LANES):
            lanes = slice(st * LANES, (st + 1) * LANES)
            lr, li = l_re[SUBLANES:, lanes], l_im[SUBLANES:, lanes]
            pr, pi = s_re[:rest, lanes], s_im[:rest, lanes]
            lr0, li0 = l_re[:SUBLANES, lanes], l_im[:SUBLANES, lanes]
            pr0, pi0 = _segment_shift(s_re[rest:, lanes], False), _segment_shift(s_im[rest:, lanes], False)
            dar_ref[:, lanes] = (jnp.sum(lr * pr + li * pi, axis=0, keepdims=True)
                                 + jnp.sum(lr0 * pr0 + li0 * pi0, axis=0, keepdims=True))
            dai_ref[:, lanes] = (jnp.sum(li * pr - lr * pi, axis=0, keepdims=True)
                                 + jnp.sum(li0 * pr0 - lr0 * pi0, axis=0, keepdims=True))
        lrb = l_re[...].astype(BF16)
        lib = l_im[...].astype(BF16)
        a_il[...] = (dy * d_ref[...] + lax.dot_general(lrb, bre_ref[...], nt, preferred_element_type=F32)
                     + lax.dot_general(lib, bim_ref[...], nt, preferred_element_type=F32))
        _deinterleave(b_il, a_il)
        stage[...] = b_il[...].astype(BF16)
        dbre_ref[...] = lax.dot_general(ub, lrb, tn, preferred_element_type=F32)
        dbim_ref[...] = lax.dot_general(ub, lib, tn, preferred_element_type=F32)
        cp = pltpu.make_async_copy(stage, dproj_ref.at[:, pl.ds((3 * nq + q) * LANES, LANES)], sem)
        cp.start()
        cp.wait()

    blk = pl.BlockSpec((tp, LANES), lambda q: (0, q))
    bspec = pl.BlockSpec((LANES, sw), lambda q: (0, q))
    cspec = pl.BlockSpec((sw, LANES), lambda q: (q, 0))
    tspec = pl.BlockSpec((2, tp // SUBLANES, sw), lambda q: (0, 0, q))
    nstate = STATE * N_GROUPS
    return pl.pallas_call(
        body, name=name, grid=(nq,),
        in_specs=[pl.BlockSpec((tp, LANES), lambda q: (0, 3 * nq + q)), blk, blk, pl.BlockSpec(memory_space=pl.ANY),
                  bspec, bspec, cspec, cspec, pl.BlockSpec((1, LANES), lambda q: (0, q)), tspec],
        out_specs=[pl.BlockSpec(memory_space=pl.ANY), bspec, bspec, cspec, cspec,
                   pl.BlockSpec((1, LANES), lambda q: (0, q)),
                   pl.BlockSpec((1, sw), lambda q: (0, q)), pl.BlockSpec((1, sw), lambda q: (0, q))],
        out_shape=[jax.ShapeDtypeStruct((tp, 4 * dh), BF16),
                   jax.ShapeDtypeStruct((LANES, nstate), F32), jax.ShapeDtypeStruct((LANES, nstate), F32),
                   jax.ShapeDtypeStruct((nstate, LANES), F32), jax.ShapeDtypeStruct((nstate, LANES), F32),
                   jax.ShapeDtypeStruct((1, dh), F32),
                   jax.ShapeDtypeStruct((1, nstate), F32), jax.ShapeDtypeStruct((1, nstate), F32)],
        input_output_aliases={3: 0},
        scratch_shapes=[pltpu.VMEM((tp, sw), F32)] * 4 + [pltpu.VMEM((tp, LANES), F32)] * 2
        + [pltpu.VMEM((tp, LANES), BF16), pltpu.SemaphoreType.DMA],
        compiler_params=_cparams("arbitrary"),
    )(proj, y, dg, dproj, bc_re, bc_im, cc_re, cc_im, dskip, a_pow)


FFN_TILE = 256


def _ffn_act(up, fw, fb, name):
    tp, two_ff = up.shape
    dff = two_ff // 2
    tc = FFN_TILE
    nj = dff // tc

    def body(ua_ref, uv_ref, wa_ref, wv_ref, ba_ref, bv_ref, act_ref):
        a = _dwconv(ua_ref[...], wa_ref) + ba_ref[...]
        v = _dwconv(uv_ref[...], wv_ref) + bv_ref[...]
        act_ref[...] = (a * _sigmoid(a) * v).astype(BF16)

    lo = lambda r: pl.BlockSpec((r, tc), lambda j: (0, j))
    hi = lambda r: pl.BlockSpec((r, tc), lambda j: (0, nj + j))
    return pl.pallas_call(
        body, name=name, grid=(nj,),
        in_specs=[lo(tp), hi(tp), lo(3), hi(3), lo(1), hi(1)],
        out_specs=lo(tp),
        out_shape=jax.ShapeDtypeStruct((tp, dff), BF16),
        compiler_params=_cparams("parallel"))(up, up, fw, fw, fb, fb)


def _ffn_bwd(up, dact, fw, fb, name):
    tp, two_ff = up.shape
    dff = two_ff // 2
    tc = FFN_TILE
    nj = dff // tc

    def body(ua_ref, uv_ref, da_ref, wa_ref, wv_ref, ba_ref, bv_ref,
             dup_ref, dwa_ref, dwv_ref, dba_ref, dbv_ref, stage, sem):
        j = pl.program_id(0)
        ua = ua_ref[...]
        uv = uv_ref[...]
        a = _dwconv(ua, wa_ref) + ba_ref[...]
        v = _dwconv(uv, wv_ref) + bv_ref[...]
        sg = _sigmoid(a)
        dact_v = da_ref[...]
        da = dact_v * v * sg * (1.0 + a * (1.0 - sg))
        dv = dact_v * a * sg
        dba_ref[...] = jnp.sum(da, axis=0, keepdims=True)
        dbv_ref[...] = jnp.sum(dv, axis=0, keepdims=True)
        dua, dwa = _dwconv_bwd(ua, da, wa_ref)
        duv, dwv = _dwconv_bwd(uv, dv, wv_ref)
        dwa_ref[...] = dwa
        dwv_ref[...] = dwv
        stage[0] = dua.astype(BF16)
        stage[1] = duv.astype(BF16)
        copies = [pltpu.make_async_copy(stage.at[p], dup_ref.at[:, pl.ds((p * nj + j) * tc, tc)], sem.at[p])
                  for p in range(2)]
        for cp in copies:
            cp.start()
        for cp in copies:
            cp.wait()

    lo = lambda r: pl.BlockSpec((r, tc), lambda j: (0, j))
    hi = lambda r: pl.BlockSpec((r, tc), lambda j: (0, nj + j))
    return pl.pallas_call(
        body, name=name, grid=(nj,),
        in_specs=[lo(tp), hi(tp), lo(tp), lo(3), hi(3), lo(1), hi(1)],
        out_specs=[pl.BlockSpec(memory_space=pl.ANY), lo(3), lo(3), lo(1), lo(1)],
        out_shape=[jax.ShapeDtypeStruct((tp, two_ff), BF16),
                   jax.ShapeDtypeStruct((3, dff), F32), jax.ShapeDtypeStruct((3, dff), F32),
                   jax.ShapeDtypeStruct((1, dff), F32), jax.ShapeDtypeStruct((1, dff), F32)],
        scratch_shapes=[pltpu.VMEM((2, tp, tc), BF16), pltpu.SemaphoreType.DMA((2,))],
        compiler_params=_cparams("arbitrary"))(up, up, dact, fw, fw, fb, fb)


def _zoh(lr, li, ld):
    dt = jnp.exp(ld)
    mag = jnp.exp(lr * dt)
    ang = li * dt
    ar = mag * jnp.cos(ang)
    ai = mag * jnp.sin(ang)
    den = lr * lr + li * li
    nr = ar - 1.0
    fr = (nr * lr + ai * li) / den
    fi = (ai * lr - nr * li) / den
    return dt, ar, ai, den, nr, fr, fi


def _s5_prep(lr, li, ld, b_re, b_im, n_pow, name):
    nstate = lr.shape[1]

    def body(lr_ref, li_ref, ld_ref, bre_ref, bim_ref, pw_ref, bcre_ref, bcim_ref):
        _, ar, ai, _, _, fr, fi = _zoh(lr_ref[...], li_ref[...], ld_ref[...])
        bre = bre_ref[...]
        bim = bim_ref[...]
        bcre_ref[...] = (fr * bre - fi * bim).astype(BF16)
        bcim_ref[...] = (fr * bim + fi * bre).astype(BF16)
        row = lax.broadcasted_iota(jnp.int32, (SUBLANES, nstate), 0)
        pr, pi = jnp.zeros((SUBLANES, nstate), F32), jnp.zeros((SUBLANES, nstate), F32)
        cr, ci = ar, ai
        for t in range(SUBLANES):
            pr, pi = jnp.where(row == t, cr, pr), jnp.where(row == t, ci, pi)
            cr, ci = cr * ar - ci * ai, cr * ai + ci * ar
        pw_ref[0, 0:SUBLANES, :] = pr
        pw_ref[1, 0:SUBLANES, :] = pi
        n = SUBLANES
        while n < n_pow:
            m = min(n, n_pow - n)
            tr, ti = pw_ref[0, n - 1:n, :], pw_ref[1, n - 1:n, :]
            xr, xi = pw_ref[0, 0:m, :], pw_ref[1, 0:m, :]
            pw_ref[0, n:n + m, :] = xr * tr - xi * ti
            pw_ref[1, n:n + m, :] = xr * ti + xi * tr
            n += m

    vmem = pl.BlockSpec(memory_space=pltpu.VMEM)
    return pl.pallas_call(
        body, name=name, in_specs=[vmem] * 5, out_specs=[vmem] * 3,
        out_shape=[jax.ShapeDtypeStruct((2, n_pow, nstate), F32)] + [jax.ShapeDtypeStruct(b_re.shape, BF16)] * 2,
        compiler_params=pltpu.CompilerParams(vmem_limit_bytes=VMEM_LIMIT))(lr, li, ld, b_re, b_im)


def _s5_prep_bwd(lr, li, ld, b_re, b_im, da_re, da_im, dbc_re, dbc_im, name):
    def body(lr_ref, li_ref, ld_ref, bre_ref, bim_ref, dar_ref, dai_ref, dbcre_ref, dbcim_ref,
             dlr_ref, dli_ref, dld_ref, dbre_ref, dbim_ref):
        lr, li = lr_ref[...], li_ref[...]
        dt, ar, ai, den, nr, fr, fi = _zoh(lr, li, ld_ref[...])
        bre, bim = bre_ref[...], bim_ref[...]
        gre, gim = dbcre_ref[...], dbcim_ref[...]
        dbre_ref[...] = fr * gre + fi * gim
        dbim_ref[...] = fr * gim - fi * gre
        g_fr = jnp.sum(gre * bre + gim * bim, axis=0, keepdims=True)
        g_fi = jnp.sum(gim * bre - gre * bim, axis=0, keepdims=True)
        g_ar = dar_ref[...] + (g_fr * lr - g_fi * li) / den
        g_ai = dai_ref[...] + (g_fr * li + g_fi * lr) / den
        d_lr = (g_fr * (nr - 2.0 * fr * lr) + g_fi * (ai - 2.0 * fi * lr)) / den
        d_li = (g_fr * (ai - 2.0 * fr * li) - g_fi * (nr + 2.0 * fi * li)) / den
        g_logmag = g_ar * ar + g_ai * ai
        g_ang = g_ai * ar - g_ar * ai
        dlr_ref[...] = d_lr + g_logmag * dt
        dli_ref[...] = d_li + g_ang * dt
        d_ld = (g_logmag * lr + g_ang * li) * dt
        n = d_ld.shape[1]
        sh = 1
        while sh < STATE:
            d_ld = d_ld + pltpu.roll(d_ld, n - sh, 1)
            sh *= 2
        dld_ref[...] = d_ld

    vmem = pl.BlockSpec(memory_space=pltpu.VMEM)
    row = jax.ShapeDtypeStruct(lr.shape, F32)
    return pl.pallas_call(
        body, name=name, in_specs=[vmem] * 9, out_specs=[vmem] * 5,
        out_shape=[row, row, row, jax.ShapeDtypeStruct(b_re.shape, F32), jax.ShapeDtypeStruct(b_re.shape, F32)],
    )(lr, li, ld, b_re, b_im, da_re, da_im, dbc_re, dbc_im)


def _compact_b(bb):
    bq = bb.reshape(N_GROUPS // 8, 8, STATE, GROUP)
    m = jnp.einsum("ab,qbph->qahbp", jnp.eye(8, dtype=bb.dtype), bq).reshape(N_GROUPS // 8, LANES, 8 * STATE)
    return m.transpose(1, 0, 2).reshape(LANES, N_GROUPS * STATE)


def _expand_b(m):
    d = m.reshape(8, GROUP, N_GROUPS // 8, 8, STATE)
    return jnp.einsum("ahqap->qahp", d).reshape(N_GROUPS, GROUP, STATE)


def _compact_c(c):
    cq = c.reshape(N_GROUPS // 8, 8, GROUP, STATE)
    return jnp.einsum("ab,qbhp->qbpah", jnp.eye(8, dtype=c.dtype), cq).reshape(N_GROUPS * STATE, LANES)


def _expand_c(m):
    d = m.reshape(N_GROUPS // 8, 8, STATE, 8, GROUP)
    return jnp.einsum("qbpbh->qbhp", d).reshape(N_GROUPS, GROUP, STATE)


def _local_step(x, target, p, ex):
    seq, d = x.shape
    n_real = N_META + seq
    tp = -(-n_real // ROW_ALIGN) * ROW_ALIGN

    h0, hn1 = _input_norm_fwd(x, p["meta_tokens"], p["norm_mix_g"] + ex.zero, tp, "norm_mix")
    ex.forward("first", hn1)
    nstate = N_GROUPS * STATE
    s5 = (p["ssm_lam_re"].reshape(1, nstate), p["ssm_lam_im"].reshape(1, nstate),
          jnp.repeat(p["ssm_log_dt"].reshape(-1), STATE).reshape(1, nstate),
          _compact_b(p["ssm_b_re"]), _compact_b(p["ssm_b_im"]))
    a_pow, bc_re, bc_im = _s5_prep(*s5, tp // SUBLANES, "s5_prep")
    cc_re = _compact_c(p["ssm_c_re"]).astype(BF16)
    cc_im = _compact_c(p["ssm_c_im"]).astype(BF16)
    dskip = p["ssm_d"].reshape(1, -1)
    first = ex.weights("first", bc_re)
    proj = _mm(hn1, first["w_in"], "nn", "proj")
    started = ex.forward("mid", proj)
    co, y, g = _seq_fwd(proj, p["conv_w"] + started[0, 0], bc_re, bc_im, cc_re, cc_im, dskip, a_pow, "seq_fwd")
    mid = ex.weights("mid", g)
    z = _mm(g, mid["ssm_w_glu"], "nn", "glu")
    mixed = _mix_fwd(co, y, z, p["gain_conv_out"], p["gain_ssm_out"], "mix_fwd")
    started = ex.forward("late", mixed)
    h1, hn2 = _proj_res_norm(mixed, mid["w_out"], h0, p["norm_ffn_g"], started, "out_proj_norm")
    late = ex.weights("late", hn2)
    up = _mm(hn2, late["w_up"], "nn", "up_proj")
    act = _ffn_act(up, p["ffn_conv_w"], p["ffn_conv_b"], "ffn_act")
    loss, dh2, dh2b, d_gfin = _proj_loss_bwd(act, late["w_down"], h1, target, p["norm_final_g"], n_real,
                                             "down_proj_loss")

    g_w_down = _mm(act, dh2b, "tn", "g_w_down")
    dact = _mm(dh2b, late["w_down"], "nt", "d_act")
    dup, dfw_a, dfw_v, dfb_a, dfb_v = _ffn_bwd(up, dact, p["ffn_conv_w"], p["ffn_conv_b"], "ffn_bwd")
    g_w_up = _mm(hn2, dup, "tn", "g_w_up")
    started = ex.grads_ready("late", {"w_up": g_w_up, "w_down": g_w_down})
    dh1, dh1b, d_gffn = _proj_norm_bwd(dup, late["w_up"], h1, p["norm_ffn_g"], dh2, started, "d_hn2_norm_bwd")
    started = ex.grads_send("late", dh1)
    g_w_out = _mm(mixed, dh1b, "tn", "g_w_out", after=started)
    dco, dz, dgp, d_gc, d_gs = _proj_mix_bwd(dh1b, mid["w_out"], co, y, z, p["gain_conv_out"],
                                             p["gain_ssm_out"], "d_mixed_mix_bwd")
    g_w_glu = _mm(g, dz, "tn", "g_w_glu")
    started = ex.grads_ready("mid", {"ssm_w_glu": g_w_glu, "w_out": g_w_out})
    dg = _mm(dz, mid["ssm_w_glu"], "nt", "d_gelu", acc_in=dgp, after=started)
    started = ex.grads_send("mid", dg)
    dproj, d_conv_w = _conv_bwd(proj, dco, p["conv_w"] + started[0, 0], "conv_bwd")
    (dproj, dbc_re, dbc_im, dcc_re, dcc_im, d_dskip, da_re, da_im) = _ssm_bwd(
        proj, y, dg, dproj, bc_re, bc_im, cc_re, cc_im, dskip, a_pow, "ssm_bwd")
    g_w_in = _mm(hn1, dproj, "tn", "g_w_in")
    started = ex.grads_ready("first", {"w_in": g_w_in})
    dhn1 = _mm(dproj, first["w_in"], "nt", "d_hn1", after=started)
    started = ex.grads_send("first", dhn1)
    grad_x, d_meta, d_gmix = _input_norm_bwd(h0, p["norm_mix_g"] + started[0, 0], dhn1, dh1, n_real, "norm_mix_bwd")

    d_lam_re, d_lam_im, d_log_dt, d_b_re, d_b_im = _s5_prep_bwd(*s5, da_re, da_im, dbc_re, dbc_im, "s5_prep_bwd")
    d_lam_re, d_lam_im = d_lam_re.reshape(N_GROUPS, STATE), d_lam_im.reshape(N_GROUPS, STATE)
    d_log_dt = d_log_dt[0, ::STATE]
    d_b_re, d_b_im = _expand_b(d_b_re), _expand_b(d_b_im)
    grads = {
        "meta_tokens": d_meta, "norm_mix_g": d_gmix, "w_in": g_w_in, "conv_w": d_conv_w,
        "ssm_lam_re": d_lam_re, "ssm_lam_im": d_lam_im, "ssm_log_dt": d_log_dt,
        "ssm_b_re": d_b_re, "ssm_b_im": d_b_im, "ssm_c_re": _expand_c(dcc_re), "ssm_c_im": _expand_c(dcc_im),
        "ssm_d": d_dskip.reshape(N_GROUPS, GROUP), "ssm_w_glu": g_w_glu,
        "gain_conv_out": d_gc, "gain_ssm_out": d_gs, "w_out": g_w_out, "norm_ffn_g": d_gffn,
        "w_up": g_w_up, "ffn_conv_w": jnp.concatenate([dfw_a, dfw_v], axis=1),
        "ffn_conv_b": jnp.concatenate([dfb_a, dfb_v], axis=1), "w_down": g_w_down, "norm_final_g": d_gfin,
    }
    return loss[0, 0], grad_x, grads


def _view(ref, axis, start, size):
    idx = [slice(None)] * len(ref.shape)
    idx[axis] = pl.ds(start, size)
    return ref.at[tuple(idx)]


def _exchange(name, ins, outs, aliases, local_copies, remote_copies):
    ni, no = len(ins), len(outs)
    nl, nr = len(local_copies), len(remote_copies)

    def body(*refs):
        in_refs, out_refs = refs[:ni], refs[ni:ni + no]
        send_sems, recv_sems, local_sems = refs[ni + no:]
        x, y, c = lax.axis_index("x"), lax.axis_index("y"), lax.axis_index("c")
        pos = (x, y, c, 2 * x + y)
        locals_ = [pltpu.make_async_copy(s(in_refs, out_refs, pos), d(in_refs, out_refs, pos), local_sems.at[i])
                   for i, (s, d) in enumerate(local_copies)]
        remotes = []
        for i, (s, d, flip) in enumerate(remote_copies):
            peer = (1 - x if "x" in flip else x, 1 - y if "y" in flip else y, 1 - c if "c" in flip else c)
            remotes.append(pltpu.make_async_remote_copy(
                src_ref=s(in_refs, out_refs, pos), dst_ref=d(in_refs, out_refs, pos),
                send_sem=send_sems.at[i], recv_sem=recv_sems.at[i], device_id=peer, device_id_type=MESH))
        for cp in locals_ + remotes:
            cp.start()
        for cp in remotes:
            cp.wait_recv()
        for cp in remotes:
            cp.wait_send()
        for cp in locals_:
            cp.wait()

    hbm = pl.BlockSpec(memory_space=pl.ANY)
    return pl.pallas_call(
        body, name=name, in_specs=[hbm] * ni, out_specs=[hbm] * no, out_shape=outs,
        input_output_aliases=aliases,
        scratch_shapes=[pltpu.SemaphoreType.DMA((nr,)), pltpu.SemaphoreType.DMA((nr,)),
                        pltpu.SemaphoreType.DMA((max(nl, 1),))],
    )(*ins)


BIG = {"w_in": (0, 1), "ssm_w_glu": (1, 0), "w_out": (1, 0), "w_up": (0, 1), "w_down": (1, 0)}
BIG_NAMES = tuple(BIG)
FLIPS = ("y", "x", "xy")


def _peer_chip(pos, flip):
    x, y, _, _ = pos
    return 2 * (1 - x if "x" in flip else x) + (1 - y if "y" in flip else y)


def _block_rows(rows, cols, itemsize, mult):
    return _pick_tile(rows, max(mult, (2 * 1024 * 1024) // (cols * itemsize)), mult)


def _cast_into_full(w, kc, shard_axis, name):
    r, cdim = w.shape
    tr = _block_rows(r, cdim, 4, 16)
    nb = r // tr

    def body(kc_ref, w_ref, o_ref):
        o_ref[...] = w_ref[...].astype(BF16)

    if shard_axis == 1:
        full, o_spec = (r, 4 * cdim), pl.BlockSpec((tr, cdim), lambda i, kc: (i, kc[0]))
    else:
        full, o_spec = (4 * r, cdim), pl.BlockSpec((tr, cdim), lambda i, kc: (kc[0] * nb + i, 0))
    return pl.pallas_call(
        body, name=name,
        grid_spec=pltpu.PrefetchScalarGridSpec(
            num_scalar_prefetch=1, grid=(nb,), in_specs=[pl.BlockSpec((tr, cdim), lambda i, kc: (i, 0))],
            out_specs=o_spec),
        out_shape=jax.ShapeDtypeStruct(full, BF16), compiler_params=_cparams("parallel"))(kc, w)


def _pair_sum(g, recv, kc, half_axis, name, out_dtype):
    hr, hc = recv.shape
    tr = _block_rows(hr, hc, 4, 16)
    nb = hr // tr

    def body(kc_ref, g_ref, r_ref, o_ref):
        o_ref[...] = (g_ref[...] + r_ref[...]).astype(out_dtype)

    if half_axis == 0:
        g_spec = pl.BlockSpec((tr, hc), lambda i, kc: (kc[1] * nb + i, 0))
    elif half_axis == 1:
        g_spec = pl.BlockSpec((tr, hc), lambda i, kc: (i, kc[1]))
    else:
        g_spec = pl.BlockSpec((tr, hc), lambda i, kc: (i, 0))
    same = pl.BlockSpec((tr, hc), lambda i, kc: (i, 0))
    return pl.pallas_call(
        body, name=name,
        grid_spec=pltpu.PrefetchScalarGridSpec(num_scalar_prefetch=1, grid=(nb,), in_specs=[g_spec, same],
                                               out_specs=same),
        out_shape=jax.ShapeDtypeStruct((hr, hc), out_dtype), compiler_params=_cparams("parallel"))(kc, g, recv)


def _chip_sum(own, recv, kc, own_axis, out_axis, name):
    _, sr, sc = recv.shape
    tr = _block_rows(sr, sc, 4, 16)
    nb = sr // tr

    def body(kc_ref, o_ref, r_ref, t_ref):
        k = kc_ref[0]
        own_v = o_ref[...].astype(F32)
        r = [r_ref[m].astype(F32) for m in range(3)]
        terms = []
        for kk in range(4):
            m = jnp.bitwise_xor(k, kk)
            terms.append(jnp.where(m == 0, own_v, jnp.where(m == 1, r[0], jnp.where(m == 2, r[1], r[2]))))
        t_ref[...] = (terms[0] + terms[1]) + (terms[2] + terms[3])

    if own_axis == 0:
        own_spec = pl.BlockSpec((tr, sc), lambda i, kc: (kc[0] * nb + i, 0))
    elif own_axis == 1:
        own_spec = pl.BlockSpec((tr, sc), lambda i, kc: (i, kc[0]))
    else:
        own_spec = pl.BlockSpec((tr, sc), lambda i, kc: (kc[1] * nb + i, 0))
    if out_axis == 0:
        out_full, out_spec = (2 * sr, sc), pl.BlockSpec((tr, sc), lambda i, kc: (kc[1] * nb + i, 0))
    else:
        out_full, out_spec = (sr, 2 * sc), pl.BlockSpec((tr, sc), lambda i, kc: (i, kc[1]))
    return pl.pallas_call(
        body, name=name,
        grid_spec=pltpu.PrefetchScalarGridSpec(
            num_scalar_prefetch=1, grid=(nb,),
            in_specs=[own_spec, pl.BlockSpec((3, tr, sc), lambda i, kc: (0, i, 0))],
            out_specs=out_spec),
        out_shape=jax.ShapeDtypeStruct(out_full, F32), compiler_params=_cparams("parallel"))(kc, own, recv)


def _adamw(w, g, m, v, name):
    r, cdim = w.shape
    tr = _block_rows(r, cdim, 4, 8)
    c1 = 1.0 - ADAM_B1 ** ADAM_STEP
    c2 = 1.0 - ADAM_B2 ** ADAM_STEP

    def body(w_ref, g_ref, m_ref, v_ref, go_ref, d_ref, nm_ref, nv_ref):
        gv = g_ref[...]
        go_ref[...] = gv
        nm = ADAM_B1 * m_ref[...] + (1.0 - ADAM_B1) * gv
        nv = ADAM_B2 * v_ref[...] + (1.0 - ADAM_B2) * (gv * gv)
        d_ref[...] = -ADAM_LR * ((nm / c1) / (jnp.sqrt(nv / c2) + ADAM_EPS) + ADAM_WD * w_ref[...])
        nm_ref[...] = nm
        nv_ref[...] = nv

    spec = _rows(cdim, tr)
    return pl.pallas_call(body, name=name, grid=(r // tr,), in_specs=[spec] * 4, out_specs=[spec] * 4,
                          out_shape=[jax.ShapeDtypeStruct((r, cdim), F32)] * 4,
                          compiler_params=_cparams("parallel"))(w, g, m, v)


def _adamw_whole(ws, gs, ms, vs, name):
    n = len(ws)
    c1 = 1.0 - ADAM_B1 ** ADAM_STEP
    c2 = 1.0 - ADAM_B2 ** ADAM_STEP

    def body(*refs):
        for i in range(n):
            w_ref, g_ref, m_ref, v_ref, d_ref, nm_ref, nv_ref = [refs[j * n + i] for j in range(7)]
            gv = g_ref[...]
            nm = ADAM_B1 * m_ref[...] + (1.0 - ADAM_B1) * gv
            nv = ADAM_B2 * v_ref[...] + (1.0 - ADAM_B2) * (gv * gv)
            d_ref[...] = -ADAM_LR * ((nm / c1) / (jnp.sqrt(nv / c2) + ADAM_EPS) + ADAM_WD * w_ref[...])
            nm_ref[...] = nm
            nv_ref[...] = nv

    vmem = pl.BlockSpec(memory_space=pltpu.VMEM)
    out = pl.pallas_call(body, name=name, in_specs=[vmem] * (4 * n), out_specs=[vmem] * (3 * n),
                         out_shape=[jax.ShapeDtypeStruct(a.shape, F32) for a in ws] * 3,
                         compiler_params=pltpu.CompilerParams(vmem_limit_bytes=VMEM_LIMIT))(*ws, *gs, *ms, *vs)
    return out[:n], out[n:2 * n], out[2 * n:]


SIDE_EFFECT = pltpu.SideEffectType.DATAFLOW_SIDE_EFFECTING


def _descriptors(copies, refs, send_sems, recv_sems, sem_off=0):
    x, y, c = lax.axis_index("x"), lax.axis_index("y"), lax.axis_index("c")
    pos = (x, y, c, 2 * x + y)
    out = []
    for i, (s, d, flip) in enumerate(copies):
        peer = (1 - x if "x" in flip else x, 1 - y if "y" in flip else y, 1 - c if "c" in flip else c)
        out.append(pltpu.make_async_remote_copy(
            src_ref=s(refs, refs, pos), dst_ref=d(refs, refs, pos),
            send_sem=send_sems.at[sem_off + i], recv_sem=recv_sems.at[sem_off + i],
            device_id=peer, device_id_type=MESH))
    return out


def _shifted(copies, off):
    return [(lambda I, O, pos, s=s: s(I[off:], O[off:], pos), lambda I, O, pos, d=d: d(I[off:], O[off:], pos), flip)
            for s, d, flip in copies]


BARRIER_IDS = {"c": (1, 2), "ici": (3, 4)}


def _exchange_start(name, bufs, copies, turns, after=None):
    n, nr = len(bufs), len(copies)
    na = 0 if after is None else 1
    flips = sorted({flip for _, _, flip in copies})
    kind = "c" if flips == ["c"] else "ici"
    collective_id = BARRIER_IDS[kind][turns[kind] % 2]
    turns[kind] += 1

    def body(*refs):
        x, y, c = lax.axis_index("x"), lax.axis_index("y"), lax.axis_index("c")
        barrier = pltpu.get_barrier_semaphore()
        for flip in flips:
            peer = (1 - x if "x" in flip else x, 1 - y if "y" in flip else y, 1 - c if "c" in flip else c)
            pl.semaphore_signal(barrier, inc=1, device_id=peer, device_id_type=MESH)
        pl.semaphore_wait(barrier, len(flips))
        for cp in _descriptors(copies, refs[:n], refs[n + na], refs[n + na + 1]):
            cp.start()
        token = refs[2 * n + na + 2]
        token[...] = jnp.zeros_like(token)

    hbm = pl.BlockSpec(memory_space=pltpu.HBM)
    sem = pl.BlockSpec(memory_space=pltpu.SEMAPHORE)
    out = pl.pallas_call(
        body, name=name,
        in_specs=[hbm] * n + [pl.BlockSpec(memory_space=pl.ANY)] * na,
        out_specs=(sem, sem, *[hbm] * n, pl.BlockSpec(memory_space=pltpu.VMEM)),
        out_shape=(pltpu.SemaphoreType.DMA((nr,)), pltpu.SemaphoreType.DMA((nr,)),
                   *[pltpu.HBM(b.shape, b.dtype) for b in bufs], jax.ShapeDtypeStruct((SUBLANES, LANES), F32)),
        input_output_aliases={i: 2 + i for i in range(n)},
        compiler_params=pltpu.CompilerParams(has_side_effects=SIDE_EFFECT, collective_id=collective_id),
    )(*[pltpu.with_memory_space_constraint(b, pltpu.HBM) for b in bufs], *([after] * na))
    return out[0], out[1], list(out[2:2 + n]), out[2 + n]


def _exchange_wait(name, send_sems, recv_sems, bufs, copies, after, sem_off=0):
    n = len(bufs)

    def body(*refs):
        for cp in _descriptors(copies, refs[:n], refs[n], refs[n + 1], sem_off):
            cp.wait_send()
            cp.wait_recv()

    hbm = pl.BlockSpec(memory_space=pltpu.HBM)
    sem = pl.BlockSpec(memory_space=pltpu.SEMAPHORE)
    out = pl.pallas_call(
        body, name=name,
        in_specs=[hbm] * n + [sem, sem, pl.BlockSpec(memory_space=pl.ANY)],
        out_specs=tuple([hbm] * n),
        out_shape=tuple(pltpu.HBM(b.shape, b.dtype) for b in bufs),
        input_output_aliases={i: i for i in range(n)},
        compiler_params=pltpu.CompilerParams(has_side_effects=SIDE_EFFECT),
    )(*bufs, send_sems, recv_sems, after)
    return list(out)


FIRST = ("w_in",)
MID = ("ssm_w_glu", "w_out")
LATE = ("w_up", "w_down")
GROUPS = {"first": FIRST, "mid": MID, "late": LATE}


def _gather_copies(names, shard_shapes):
    def region(i, chip, c):
        half_axis, shard_axis = BIG[names[i]]
        ssize = shard_shapes[i][shard_axis]
        hsize = shard_shapes[i][half_axis] // 2
        return lambda ref: _view(_view(ref, shard_axis, chip * ssize, ssize), half_axis, c * hsize, hsize)

    ici, d2d = [], []
    for i in range(len(names)):
        for flip in FLIPS:
            ici.append((lambda I, O, pos, i=i: region(i, pos[3], pos[2])(I[i]),
                        lambda I, O, pos, i=i: region(i, pos[3], pos[2])(O[i]), flip))
            d2d.append((lambda I, O, pos, i=i, flip=flip: region(i, _peer_chip(pos, flip), pos[2])(I[i]),
                        lambda I, O, pos, i=i, flip=flip: region(i, _peer_chip(pos, flip), pos[2])(O[i]), "c"))
    return ici, d2d


def _half_shape(n, shape):
    r, cdim = shape
    return (r // 2, cdim) if BIG[n][0] == 0 else (r, cdim // 2)


def _sub_shape(n, shape):
    hr, hc = _half_shape(n, shape)
    return (hr, hc // 4) if BIG[n][1] == 1 else (hr // 4, hc)


def _pair_copies(names, shapes, with_pack, dst_off):
    n = len(names)

    def other_half(i, ref, pos):
        half_axis = BIG[names[i]][0]
        hsize = shapes[i][half_axis] // 2
        return _view(ref, half_axis, (1 - pos[2]) * hsize, hsize)

    copies = [(lambda I, O, pos, i=i: other_half(i, I[i], pos), lambda I, O, pos, i=i: O[dst_off + i], "c")
              for i in range(n)]
    if with_pack:
        copies.append((lambda I, O, pos: I[n], lambda I, O, pos: O[dst_off + n], "c"))
    return copies


def _chip_copies(names, shapes, pack_rows, dst_off):
    n = len(names)

    def piece(i, ref, chip):
        shard_axis = BIG[names[i]][1]
        ssize = _sub_shape(names[i], shapes[i])[shard_axis]
        return _view(ref, shard_axis, chip * ssize, ssize)

    copies = []
    for i in range(n):
        for slot, flip in enumerate(FLIPS):
            copies.append((lambda I, O, pos, i=i, flip=flip: piece(i, I[i], _peer_chip(pos, flip)),
                           lambda I, O, pos, i=i, slot=slot: O[dst_off + i].at[slot], flip))
    if pack_rows:
        for slot, flip in enumerate(FLIPS):
            copies.append((lambda I, O, pos: _view(I[n], 0, pos[2] * (pack_rows // 2), pack_rows // 2),
                           lambda I, O, pos, slot=slot: O[dst_off + n].at[slot], flip))
    return copies


class _Exchanges:
    def __init__(self, shards, tiny, kc):
        self.kc = kc
        wb = {n: _cast_into_full(shards[n], kc, BIG[n][1], "cast_" + n) for n in BIG_NAMES}
        self.gathering, self.forwarding, self.pairing, self.reducing = {}, {}, {}, {}
        self.turns = {"c": 0, "ici": 0}
        tiny_copies = [(lambda I, O, pos: I[0], lambda I, O, pos: O[1].at[pos[3]], flip) for flip in FLIPS]
        self.gathering["tiny"] = (0, 0, 2, tiny_copies, None)
        bufs, copies = [tiny, lax.empty((4,) + tiny.shape, F32)], list(tiny_copies)
        for group, names in GROUPS.items():
            ici, d2d = _gather_copies(names, [shards[n].shape for n in names])
            self.gathering[group] = (len(bufs), len(copies), len(names), ici, d2d)
            copies += _shifted(ici, len(bufs))
            bufs += [wb[n] for n in names]
        self.started = _exchange_start("gather_start", bufs, copies, self.turns)
        self.zero = self.started[3][0, 0]

    def _arrived(self, group, after):
        buf_off, sem_off, n, ici, _ = self.gathering[group]
        send_sems, recv_sems, bufs, _ = self.started
        return _exchange_wait("gather_%s_wait" % group, send_sems, recv_sems, bufs[buf_off:buf_off + n], ici, after,
                              sem_off)

    def small_params(self, kc):
        tiny, got = self._arrived("tiny", self.started[3])
        return lax.dynamic_update_index_in_dim(got, tiny, kc[0], 0)

    def forward(self, group, after):
        d2d = self.gathering[group][4]
        self.forwarding[group] = (_exchange_start("forward_%s_start" % group, self._arrived(group, after), d2d,
                                                  self.turns), d2d)
        return self.forwarding[group][0][3]

    def weights(self, group, after):
        if group not in self.forwarding:
            after = self.forward(group, after)
        (send_sems, recv_sems, bufs, _), d2d = self.forwarding[group]
        full = _exchange_wait("forward_%s_wait" % group, send_sems, recv_sems, bufs, d2d, after)
        return dict(zip(GROUPS[group], full))

    def grads_ready(self, group, grads):
        names = GROUPS[group]
        gs = [grads[n] for n in names]
        land = [lax.empty(_half_shape(n, g.shape), F32) for n, g in zip(names, gs)]
        copies = _pair_copies(names, [g.shape for g in gs], False, len(names))
        started = _exchange_start("pair_%s_start" % group, gs + land, copies, self.turns)
        self.pairing[group] = (started, copies)
        return started[3]

    def grads_send(self, group, after):
        names = GROUPS[group]
        n = len(names)
        (send_sems, recv_sems, bufs, _), copies = self.pairing[group]
        bufs = _exchange_wait("pair_%s_wait" % group, send_sems, recv_sems, bufs, copies, after)
        chip = [_pair_sum(bufs[i], bufs[n + i], self.kc, BIG[names[i]][0], "pair_sum_" + names[i], BF16)
                for i in range(n)]
        shapes = [bufs[i].shape for i in range(n)]
        land = [lax.empty((3,) + _sub_shape(names[i], shapes[i]), BF16) for i in range(n)]
        copies = _chip_copies(names, shapes, 0, n)
        started = _exchange_start("reduce_%s_start" % group, chip + land, copies, self.turns)
        self.reducing[group] = (started, copies)
        return started[3]

    def finish_pack(self, pack):
        kc = self.kc
        prow = pack.shape[0] // 2
        recv = _exchange("reduce_d2d", [pack], [jax.ShapeDtypeStruct(pack.shape, F32)], {}, [],
                         _pair_copies((), [], True, 0))
        chip_pack = _pair_sum(pack, recv[0], kc, None, "pair_sum_pack", F32)
        copies = _chip_copies((), [], pack.shape[0], 1)
        land = lax.empty((3, prow, pack.shape[1]), F32)
        pack_sems_s, pack_sems_r, pack_bufs, after = _exchange_start("reduce_pack_start", [chip_pack, land], copies,
                                                                     self.turns)

        names, chips, recvs = (), [], []
        for group, group_names in GROUPS.items():
            (send_sems, recv_sems, bufs, _), group_copies = self.reducing[group]
            bufs = _exchange_wait("reduce_%s_wait" % group, send_sems, recv_sems, bufs, group_copies, after)
            n = len(group_names)
            names, chips, recvs = names + group_names, chips + bufs[:n], recvs + bufs[n:]
            after = bufs[n]
        total = [_chip_sum(chips[i], recvs[i], kc, BIG[n][1], BIG[n][0], "chip_sum_" + n)
                 for i, n in enumerate(names)]

        def my_half(half_axis, ref, pos):
            hsize = ref.shape[half_axis] // 2
            return _view(ref, half_axis, pos[2] * hsize, hsize)

        swap = [(lambda I, O, pos, i=i, n=n: my_half(BIG[n][0], I[i], pos),
                 lambda I, O, pos, i=i, n=n: my_half(BIG[n][0], O[i], pos), "c") for i, n in enumerate(names)]
        self.swapping = (_exchange_start("swap_start", total, swap, self.turns), swap, names)

        chip_pack, recv_pack = _exchange_wait("reduce_pack_wait", pack_sems_s, pack_sems_r, pack_bufs, copies,
                                              self.swapping[0][3])
        total_pack = _chip_sum(chip_pack, recv_pack, kc, None, 0, "chip_sum_pack")
        swap = [(lambda I, O, pos: my_half(0, I[0], pos), lambda I, O, pos: my_half(0, O[0], pos), "c")]
        return _exchange("swap_pack", [total_pack], [jax.ShapeDtypeStruct(pack.shape, F32)], {0: 0}, [], swap)[0]

    def finish_big(self, after):
        (send_sems, recv_sems, bufs, _), swap, names = self.swapping
        return dict(zip(names, _exchange_wait("swap_wait", send_sems, recv_sems, bufs, swap, after)))


WEIGHTS = ("meta_tokens", "norm_mix_g", "w_in", "conv_w", "ssm_lam_re", "ssm_lam_im", "ssm_log_dt", "ssm_b_re",
           "ssm_b_im", "ssm_c_re", "ssm_c_im", "ssm_d", "ssm_w_glu", "gain_conv_out", "gain_ssm_out", "w_out",
           "norm_ffn_g", "w_up", "ffn_conv_w", "ffn_conv_b", "w_down", "norm_final_g")
TINY_SHARDED = ("meta_tokens", "conv_w", "ffn_conv_w")
REPLICATED = tuple(n for n in WEIGHTS if n not in BIG and n not in TINY_SHARDED)
PACK_COLS = 512


def _pack(arrays, row_mult, cols):
    flat = jnp.concatenate([a.reshape(-1).astype(F32) for a in arrays])
    n = flat.shape[0]
    total = -(-n // (row_mult * cols)) * (row_mult * cols)
    return jnp.concatenate([flat, jnp.zeros((total - n,), F32)]).reshape(total // cols, cols)


def _unpack(packed, shapes):
    flat = packed.reshape(-1)
    out, off = [], 0
    for s in shapes:
        n = math.prod(s)
        out.append(flat[off:off + n].reshape(s))
        off += n
    return out


def kernel(x, meta_tokens, norm_mix_g, w_in, conv_w, ssm_lam_re, ssm_lam_im, ssm_log_dt, ssm_b_re, ssm_b_im, ssm_c_re, ssm_c_im, ssm_d, ssm_w_glu, gain_conv_out, gain_ssm_out, w_out, norm_ffn_g, w_up, ffn_conv_w, ffn_conv_b, w_down, norm_final_g, loss_target, m_meta_tokens, m_norm_mix_g, m_w_in, m_conv_w, m_ssm_lam_re, m_ssm_lam_im, m_ssm_log_dt, m_ssm_b_re, m_ssm_b_im, m_ssm_c_re, m_ssm_c_im, m_ssm_d, m_ssm_w_glu, m_gain_conv_out, m_gain_ssm_out, m_w_out, m_norm_ffn_g, m_w_up, m_ffn_conv_w, m_ffn_conv_b, m_w_down, m_norm_final_g, v_meta_tokens, v_norm_mix_g, v_w_in, v_conv_w, v_ssm_lam_re, v_ssm_lam_im, v_ssm_log_dt, v_ssm_b_re, v_ssm_b_im, v_ssm_c_re, v_ssm_c_im, v_ssm_d, v_ssm_w_glu, v_gain_conv_out, v_gain_ssm_out, v_w_out, v_norm_ffn_g, v_w_up, v_ffn_conv_w, v_ffn_conv_b, v_w_down, v_norm_final_g):
    args = dict(locals())
    w = {n: args[n] for n in WEIGHTS}
    mom = {n: args["m_" + n] for n in WEIGHTS}
    var = {n: args["v_" + n] for n in WEIGHTS}
    kx, ky, kc_ = lax.axis_index("x"), lax.axis_index("y"), lax.axis_index("c")
    chip = 2 * kx + ky
    kc = jnp.stack([chip, kc_]).astype(jnp.int32)

    def squeeze(n, a):
        if n == "meta_tokens":
            return a
        if n == "norm_final_g":
            return a.reshape(1, -1)
        a = a[0]
        return a.reshape(1, -1) if a.ndim == 1 else a

    wl = {n: squeeze(n, w[n]) for n in WEIGHTS}
    ml = {n: squeeze(n, mom[n]) for n in WEIGHTS}
    vl = {n: squeeze(n, var[n]) for n in WEIGHTS}

    tiny = _pack([wl[n] for n in TINY_SHARDED], SUBLANES, LANES)
    ex = _Exchanges({n: wl[n] for n in BIG_NAMES}, tiny, kc)
    tiny_shapes = [wl[n].shape for n in TINY_SHARDED]
    tiny_all = ex.small_params(kc)
    tiny_parts = [_unpack(tiny_all[k], tiny_shapes) for k in range(4)]
    p = {n: wl[n] for n in WEIGHTS if n not in BIG}
    for j, n in enumerate(TINY_SHARDED):
        p[n] = jnp.concatenate([tiny_parts[k][j] for k in range(4)], axis=1)
    p["ssm_log_dt"] = wl["ssm_log_dt"].reshape(-1)

    loss_local, grad_x, grads = _local_step(x[0], loss_target[0], p, ex)

    small_names = REPLICATED + TINY_SHARDED
    small_shapes = [tuple(grads[n].shape) for n in small_names] + [(1,)]
    pack = _pack([grads[n] for n in small_names] + [loss_local.reshape(1)], 2 * 16, PACK_COLS)
    g_pack = ex.finish_pack(pack)
    g_small = dict(zip(small_names + ("loss",), _unpack(g_pack, small_shapes)))
    loss = g_small["loss"][0]
    swapped = ("ssm_b_re", "ssm_b_im")

    def view(n, a):
        if n in swapped:
            return jnp.swapaxes(a, -1, -2)
        return a.reshape(1, -1) if a.ndim == 1 else a

    g = {}
    for n in REPLICATED:
        g[n] = g_small[n].reshape(view(n, w[n]).shape)
    for n in TINY_SHARDED:
        cols = wl[n].shape[1]
        g[n] = lax.dynamic_slice_in_dim(g_small[n], chip * cols, cols, axis=1).reshape(w[n].shape)
    delta, new_m, new_v = {}, {}, {}
    small = [[view(n, d[n]) for n in small_names] for d in (w, mom, var)]
    small.insert(1, [g[n] for n in small_names])
    for d, outs in zip((delta, new_m, new_v), _adamw_whole(*small, "adamw_small")):
        d.update(zip(small_names, outs))
    for d in (g, delta, new_m, new_v):
        d.update({n: jnp.swapaxes(d[n], -1, -2) for n in swapped})
    g_big = ex.finish_big(delta[small_names[0]])
    for n in BIG_NAMES:
        g[n], delta[n], new_m[n], new_v[n] = _adamw(wl[n], g_big[n], ml[n], vl[n], "adamw_" + n)

    def like(n, a):
        return a.reshape(w[n].shape)

    return (loss, grad_x[None], *[like(n, g[n]) for n in WEIGHTS], *[like(n, delta[n]) for n in WEIGHTS],
            *[like(n, new_m[n]) for n in WEIGHTS], *[like(n, new_v[n]) for n in WEIGHTS])
```

```python
import functools
import math

import jax
import jax.numpy as jnp
from jax import lax
from jax.experimental import pallas as pl
from jax.experimental.pallas import tpu as pltpu

F32 = jnp.float32
BF16 = jnp.bfloat16
MESH = pl.DeviceIdType.MESH

N_META = 16
N_GROUPS = 32
GROUP = 16
STATE = 64
RMS_EPS = 1e-6
ADAM_LR = 0.001
ADAM_B1 = 0.9
ADAM_B2 = 0.999
ADAM_EPS = 1e-08
ADAM_WD = 0.01
ADAM_STEP = 10

LANES = 128
SUBLANES = 8
ROW_ALIGN = 128
ROW_TILES = 4
VMEM_LIMIT = 52 * 1024 * 1024
MM_VMEM_BUDGET = 40 * 1024 * 1024
GELU_C = math.sqrt(2.0 / math.pi)
GELU_A = 0.044715


def _cparams(*sem):
    return pltpu.CompilerParams(dimension_semantics=sem, vmem_limit_bytes=VMEM_LIMIT)


def _pick_tile(dim, cap, mult):
    best = None
    for t in range(mult, min(dim, cap) + 1, mult):
        if dim % t == 0:
            best = t
    return best if best is not None else dim


def _mm(a, b, mode, name, out_dtype=F32, acc_in=None, after=None):
    if mode == "tn":
        kdim, m = a.shape
    else:
        m, kdim = a.shape
    n = b.shape[0] if mode == "nt" else b.shape[1]
    tm = _pick_tile(m, 1408, LANES if mode == "tn" else 16)
    tk = _pick_tile(kdim, 2816, LANES)
    nk = kdim // tk
    out_bytes = jnp.dtype(out_dtype).itemsize
    for cap in (1408, 1024, 512, 256, LANES):
        tn = _pick_tile(n, cap, LANES)
        blocks = 2 * (tm * tk * 2 + tk * tn * 2 + tm * tn * out_bytes * (2 if acc_in is not None else 1))
        if blocks + (tm * tn * 4 if nk > 1 else 0) <= MM_VMEM_BUDGET:
            break
    has_acc = acc_in is not None

    def body(*refs):
        if after is not None:
            refs = refs[1:]
        if has_acc:
            a_ref, b_ref, c_ref, o_ref = refs[:4]
            rest = refs[4:]
        else:
            a_ref, b_ref, o_ref = refs[:3]
            c_ref = None
            rest = refs[3:]
        if mode == "nn":
            p = jnp.dot(a_ref[...], b_ref[...], preferred_element_type=F32)
        elif mode == "nt":
            p = lax.dot_general(a_ref[...], b_ref[...], (((1,), (1,)), ((), ())), preferred_element_type=F32)
        else:
            p = lax.dot_general(a_ref[...], b_ref[...], (((0,), (0,)), ((), ())), preferred_element_type=F32)
        if nk == 1:
            if has_acc:
                p = p + c_ref[...]
            o_ref[...] = p.astype(out_dtype)
        else:
            acc_ref = rest[0]
            k = pl.program_id(2)

            @pl.when(k == 0)
            def _():
                acc_ref[...] = p + c_ref[...] if has_acc else p

            @pl.when(k > 0)
            def _():
                acc_ref[...] += p

            @pl.when(k == nk - 1)
            def _():
                o_ref[...] = acc_ref[...].astype(out_dtype)

    if mode == "tn":
        a_spec = pl.BlockSpec((tk, tm), lambda i, j, k: (k, i))
    else:
        a_spec = pl.BlockSpec((tm, tk), lambda i, j, k: (i, k))
    if mode == "nt":
        b_spec = pl.BlockSpec((tn, tk), lambda i, j, k: (j, k))
    else:
        b_spec = pl.BlockSpec((tk, tn), lambda i, j, k: (k, j))
    o_spec = pl.BlockSpec((tm, tn), lambda i, j, k: (i, j))
    in_specs = [a_spec, b_spec] + ([o_spec] if has_acc else [])
    args = (a, b) + ((acc_in,) if has_acc else ())
    if after is not None:
        in_specs = [pl.BlockSpec(memory_space=pl.ANY)] + in_specs
        args = (after,) + args
    return pl.pallas_call(
        body, name=name, grid=(m // tm, n // tn, nk),
        in_specs=in_specs, out_specs=o_spec,
        out_shape=jax.ShapeDtypeStruct((m, n), out_dtype),
        scratch_shapes=[pltpu.VMEM((tm, tn), F32)] if nk > 1 else [],
        compiler_params=_cparams("parallel", "parallel", "arbitrary"),
    )(*args)


def _mm_rows(a, b, mode, name, ins, outs, epilogue, scratch=()):
    m, kdim = a.shape
    n = b.shape[0] if mode == "nt" else b.shape[1]
    tm = m // ROW_TILES
    tk = _pick_tile(kdim, 2816, LANES)
    nk = kdim // tk
    ni, no = len(ins), len(outs)

    def body(*refs):
        a_ref, b_ref = refs[:2]
        in_refs, out_refs, rest = refs[2:2 + ni], refs[2 + ni:2 + ni + no], refs[2 + ni + no:]
        i = pl.program_id(0)
        if mode == "nn":
            p = jnp.dot(a_ref[...], b_ref[...], preferred_element_type=F32)
        else:
            p = lax.dot_general(a_ref[...], b_ref[...], (((1,), (1,)), ((), ())), preferred_element_type=F32)
        if nk == 1:
            epilogue(p, i, in_refs, out_refs, rest)
        else:
            acc_ref = rest[0]
            k = pl.program_id(1)

            @pl.when(k == 0)
            def _():
                acc_ref[...] = p

            @pl.when(k > 0)
            def _():
                acc_ref[...] += p

            @pl.when(k == nk - 1)
            def _():
                epilogue(acc_ref[...], i, in_refs, out_refs, rest[1:])

    def spec(shape, kind):
        if kind == "rows":
            return pl.BlockSpec((tm,) + tuple(shape[1:]), lambda i, k: (i,) + (0,) * (len(shape) - 1))
        if kind == "whole":
            return pl.BlockSpec(tuple(shape), lambda i, k: (0,) * len(shape))
        return pl.BlockSpec(memory_space=pl.ANY)

    a_spec = pl.BlockSpec((tm, tk), lambda i, k: (i, k))
    b_spec = pl.BlockSpec((n, tk), lambda i, k: (0, k)) if mode == "nt" else pl.BlockSpec((tk, n), lambda i, k: (k, 0))
    return pl.pallas_call(
        body, name=name, grid=(ROW_TILES, nk),
        in_specs=[a_spec, b_spec] + [spec(x.shape, kind) for x, kind in ins],
        out_specs=[spec(shape, kind) for shape, _, kind in outs],
        out_shape=[jax.ShapeDtypeStruct(shape, dtype) for shape, dtype, _ in outs],
        scratch_shapes=([pltpu.VMEM((tm, n), F32)] if nk > 1 else []) + list(scratch),
        compiler_params=_cparams("arbitrary", "arbitrary"),
    )(a, b, *[x for x, _ in ins])


def _rows(shape_cols, tr, dtype=None):
    return pl.BlockSpec((tr, shape_cols), lambda i: (i, 0))


def _const(shape):
    return pl.BlockSpec(shape, lambda i: (0,) * len(shape))


def _rms(x):
    return lax.rsqrt(jnp.mean(x * x, axis=-1, keepdims=True) + RMS_EPS)


def _rms_bwd(x, r, g, dy):
    xn = x * r
    dxn = dy * g
    dx = r * (dxn - xn * jnp.mean(dxn * xn, axis=-1, keepdims=True))
    return dx, dy * xn


def _gelu(y):
    return 0.5 * y * (1.0 + jnp.tanh(GELU_C * (y + GELU_A * y * y * y)))


def _gelu_grad(y):
    t = jnp.tanh(GELU_C * (y + GELU_A * y * y * y))
    return 0.5 * (1.0 + t) + 0.5 * y * (1.0 - t * t) * GELU_C * (1.0 + 3.0 * GELU_A * y * y)


def _sigmoid(z):
    return 1.0 / (1.0 + jnp.exp(-z))


def _proj_res_norm(a, w, h, g, after, name):
    def epilogue(p, i, ins, outs, _):
        x = ins[0][...] + p
        outs[0][...] = x
        outs[1][...] = (x * _rms(x) * ins[1][...]).astype(BF16)

    return _mm_rows(a, w, "nn", name, [(h, "rows"), (g, "whole"), (after, "hbm")],
                    [(h.shape, F32, "rows"), (h.shape, BF16, "rows")], epilogue)


def _proj_norm_bwd(da, w, h, g, dres, after, name):
    d = h.shape[1]

    def epilogue(p, i, ins, outs, _):
        x = ins[0][...]
        dx, dgs = _rms_bwd(x, _rms(x), ins[1][...], p)
        dh = ins[2][...] + dx
        outs[0][...] = dh
        outs[1][...] = dh.astype(BF16)

        @pl.when(i == 0)
        def _():
            outs[2][...] = jnp.zeros_like(outs[2])

        outs[2][...] += jnp.sum(dgs, axis=0, keepdims=True)

    return _mm_rows(da, w, "nt", name, [(h, "rows"), (g, "whole"), (dres, "rows"), (after, "hbm")],
                    [(h.shape, F32, "rows"), (h.shape, BF16, "rows"), ((1, d), F32, "whole")], epilogue)


def _input_norm_bwd(h, g, dhn, dres, n_real, name):
    tp, d = h.shape
    tr = tp // ROW_TILES

    def body(h_ref, g_ref, dhn_ref, dres_ref, dx_ref, dmeta_ref, dg_ref, stage, sem):
        i = pl.program_id(0)
        x = h_ref[...]
        dx, dgs = _rms_bwd(x, _rms(x), g_ref[...], dhn_ref[...])
        stage[...] = dres_ref[...] + dx

        @pl.when(i == 0)
        def _():
            dg_ref[...] = jnp.zeros_like(dg_ref)
            dmeta_ref[...] = stage[:N_META, :]

        dg_ref[...] += jnp.sum(dgs, axis=0, keepdims=True)
        for t in range(ROW_TILES):
            lo, hi = max(t * tr, N_META), min((t + 1) * tr, n_real)
            if hi > lo:
                @pl.when(i == t)
                def _(t=t, lo=lo, hi=hi):
                    cp = pltpu.make_async_copy(stage.at[pl.ds(lo - t * tr, hi - lo), :],
                                               dx_ref.at[pl.ds(lo - N_META, hi - lo), :], sem)
                    cp.start()
                    cp.wait()

    return pl.pallas_call(
        body, name=name, grid=(ROW_TILES,),
        in_specs=[_rows(d, tr), _const((1, d)), _rows(d, tr), _rows(d, tr)],
        out_specs=[pl.BlockSpec(memory_space=pl.ANY), _const((N_META, d)), _const((1, d))],
        out_shape=[jax.ShapeDtypeStruct((n_real - N_META, d), F32), jax.ShapeDtypeStruct((N_META, d), F32),
                   jax.ShapeDtypeStruct((1, d), F32)],
        scratch_shapes=[pltpu.VMEM((tr, d), F32), pltpu.SemaphoreType.DMA],
        compiler_params=_cparams("arbitrary"))(h, g, dhn, dres)


def _load_token_rows(tok_hbm, buf, sem, tr, n_real, head=None, wait=False, i=None):
    i = pl.program_id(0) if i is None else i
    for t in range(ROW_TILES):
        base = t * tr
        lo, hi = max(base, N_META), min(base + tr, n_real)

        @pl.when(i == t)
        def _(base=base, lo=lo, hi=hi):
            if hi > lo:
                cp = pltpu.make_async_copy(tok_hbm.at[pl.ds(lo - N_META, hi - lo), :],
                                           buf.at[pl.ds(lo - base, hi - lo), :], sem)
                if wait:
                    cp.wait()
                    return
                cp.start()
            if wait:
                return
            if base < N_META:
                buf[0:N_META - base, :] = (jnp.zeros((N_META - base, buf.shape[1]), F32) if head is None
                                           else head[base:N_META, :])
            if hi < base + tr:
                buf[max(hi, base) - base:tr, :] = jnp.zeros((base + tr - max(hi, base), buf.shape[1]), F32)


def _input_norm_fwd(x, meta, g, tp, name):
    seq, d = x.shape
    tr = tp // ROW_TILES
    n_real = N_META + seq

    def body(x_hbm, meta_ref, g_ref, h_ref, hn_ref, buf, sem):
        _load_token_rows(x_hbm, buf, sem, tr, n_real, head=meta_ref)
        _load_token_rows(x_hbm, buf, sem, tr, n_real, wait=True)
        h = buf[...]
        h_ref[...] = h
        hn_ref[...] = (h * _rms(h) * g_ref[...]).astype(BF16)

    return pl.pallas_call(
        body, name=name, grid=(ROW_TILES,),
        in_specs=[pl.BlockSpec(memory_space=pl.ANY), _const((N_META, d)), _const((1, d))],
        out_specs=[_rows(d, tr), _rows(d, tr)],
        out_shape=[jax.ShapeDtypeStruct((tp, d), F32), jax.ShapeDtypeStruct((tp, d), BF16)],
        scratch_shapes=[pltpu.VMEM((tr, d), F32), pltpu.SemaphoreType.DMA],
        compiler_params=_cparams("arbitrary"))(x, meta, g)


def _proj_loss_bwd(act, w, h1, target, g, n_real, name):
    tp, d = h1.shape
    tr = tp // ROW_TILES

    def epilogue(p, i, ins, outs, scratch):
        h1_ref, t_hbm, g_ref = ins
        loss_ref, dh_ref, dhb_ref, dg_ref = outs
        t_buf, sem = scratch
        _load_token_rows(t_hbm, t_buf, sem, tr, n_real, i=i)
        x = h1_ref[...] + p
        r = _rms(x)
        row = i * tr + lax.broadcasted_iota(jnp.int32, (tr, d), 0)
        valid = (row >= N_META) & (row < n_real)
        _load_token_rows(t_hbm, t_buf, sem, tr, n_real, wait=True, i=i)
        e = jnp.where(valid, x * r * g_ref[...] - t_buf[...], 0.0)
        dx, dgs = _rms_bwd(x, r, g_ref[...], e * (1.0 / d))
        dh_ref[...] = dx
        dhb_ref[...] = dx.astype(BF16)

        @pl.when(i == 0)
        def _():
            dg_ref[...] = jnp.zeros_like(dg_ref)
            loss_ref[...] = jnp.zeros_like(loss_ref)

        dg_ref[...] += jnp.sum(dgs, axis=0, keepdims=True)
        loss_ref[...] += (0.5 / d) * jnp.sum(jnp.sum(e * e, axis=0, keepdims=True), axis=1, keepdims=True)

    return _mm_rows(act, w, "nn", name, [(h1, "rows"), (target, "hbm"), (g, "whole")],
                    [((1, LANES), F32, "whole"), ((tp, d), F32, "rows"), ((tp, d), BF16, "rows"),
                     ((1, d), F32, "whole")],
                    epilogue, scratch=[pltpu.VMEM((tr, d), F32), pltpu.SemaphoreType.DMA])


def _mix_fwd(co, y, z, gc, gs, name):
    tp, dh = co.shape
    tr = tp // ROW_TILES

    def body(co_ref, y_ref, z_ref, gc_ref, gs_ref, m_ref):
        c = co_ref[...]
        m_ref[:, :dh] = (c * _rms(c) * gc_ref[...]).astype(BF16)
        so = _gelu(y_ref[...]) * _sigmoid(z_ref[...])
        m_ref[:, dh:] = (so * _rms(so) * gs_ref[...]).astype(BF16)

    return pl.pallas_call(
        body, name=name, grid=(ROW_TILES,),
        in_specs=[_rows(dh, tr)] * 3 + [_const((1, dh))] * 2,
        out_specs=_rows(2 * dh, tr),
        out_shape=jax.ShapeDtypeStruct((tp, 2 * dh), BF16),
        compiler_params=_cparams("parallel"))(co, y, z, gc, gs)


def _proj_mix_bwd(dh1b, w, co, y, z, gc, gs, name):
    tp, dh = co.shape

    def epilogue(p, i, ins, outs, _):
        co_ref, y_ref, z_ref, gc_ref, gs_ref = ins
        dco_ref, dz_ref, dgp_ref, dgc_ref, dgs_ref = outs
        c = co_ref[...]
        dco, dgc = _rms_bwd(c, _rms(c), gc_ref[...], p[:, :dh])
        dco_ref[...] = dco
        gl = _gelu(y_ref[...])
        sg = _sigmoid(z_ref[...])
        so = gl * sg
        dso, dgs = _rms_bwd(so, _rms(so), gs_ref[...], p[:, dh:])
        dz_ref[...] = (dso * gl * sg * (1.0 - sg)).astype(BF16)
        dgp_ref[...] = dso * sg

        @pl.when(i == 0)
        def _():
            dgc_ref[...] = jnp.zeros_like(dgc_ref)
            dgs_ref[...] = jnp.zeros_like(dgs_ref)

        dgc_ref[...] += jnp.sum(dgc, axis=0, keepdims=True)
        dgs_ref[...] += jnp.sum(dgs, axis=0, keepdims=True)

    return _mm_rows(dh1b, w, "nt", name,
                    [(co, "rows"), (y, "rows"), (z, "rows"), (gc, "whole"), (gs, "whole")],
                    [((tp, dh), F32, "rows"), ((tp, dh), BF16, "rows"), ((tp, dh), F32, "rows"),
                     ((1, dh), F32, "whole"), ((1, dh), F32, "whole")], epilogue)


def _shift_down(x, k):
    row = lax.broadcasted_iota(jnp.int32, x.shape, 0)
    return jnp.where(row >= k, pltpu.roll(x, k, 0), 0.0)


def _shift_up(x, k):
    n = x.shape[0]
    row = lax.broadcasted_iota(jnp.int32, x.shape, 0)
    return jnp.where(row < n - k, pltpu.roll(x, n - k, 0), 0.0)


def _dwconv(x, w_ref):
    return w_ref[2:3, :] * x + w_ref[1:2, :] * _shift_down(x, 1) + w_ref[0:1, :] * _shift_down(x, 2)


def _dwconv_bwd(x, dy, w_ref):
    dx = w_ref[2:3, :] * dy + w_ref[1:2, :] * _shift_up(dy, 1) + w_ref[0:1, :] * _shift_up(dy, 2)
    dw = jnp.concatenate([jnp.sum(dy * _shift_down(x, 2), axis=0, keepdims=True),
                          jnp.sum(dy * _shift_down(x, 1), axis=0, keepdims=True),
                          jnp.sum(dy * x, axis=0, keepdims=True)], axis=0)
    return dx, dw


def _interleave(dst, src):
    seg_rows = src.shape[0] // SUBLANES
    for seg in range(SUBLANES):
        dst[pl.ds(seg, seg_rows, stride=SUBLANES), :] = src[seg * seg_rows:(seg + 1) * seg_rows, :]


def _deinterleave(dst, src):
    seg_rows = src.shape[0] // SUBLANES
    for seg in range(SUBLANES):
        dst[seg * seg_rows:(seg + 1) * seg_rows, :] = src[pl.ds(seg, seg_rows, stride=SUBLANES), :]


def _segment_shift(x, reverse):
    row = lax.broadcasted_iota(jnp.int32, x.shape, 0)
    if reverse:
        return jnp.where(row < SUBLANES - 1, pltpu.roll(x, SUBLANES - 1, 0), 0.0)
    return jnp.where(row >= 1, pltpu.roll(x, 1, 0), 0.0)


def _scan(s_re, s_im, pw_ref, reverse):
    n_steps = s_re.shape[0] // SUBLANES
    n_strips = s_re.shape[1] // LANES
    sign = -1.0 if reverse else 1.0
    strips = [slice(st * LANES, (st + 1) * LANES) for st in range(n_strips)]

    def rows_of(j):
        step = (n_steps - 1 - j) if reverse else j
        return pl.ds(pl.multiple_of(step * SUBLANES, SUBLANES), SUBLANES)

    a = [(jnp.broadcast_to(pw_ref[0, 0:1, lanes], (SUBLANES, LANES)),
          sign * jnp.broadcast_to(pw_ref[1, 0:1, lanes], (SUBLANES, LANES))) for lanes in strips]

    def local(i, carry):
        for half in range(2):
            rows = rows_of(2 * i + half)
            out = []
            for st, lanes in enumerate(strips):
                (ar, ai), cr, ci = a[st], carry[2 * st], carry[2 * st + 1]
                xr = s_re[rows, lanes] + (ar * cr - ai * ci)
                xi = s_im[rows, lanes] + (ar * ci + ai * cr)
                s_re[rows, lanes] = xr
                s_im[rows, lanes] = xi
                out += [xr, xi]
            carry = tuple(out)
        return carry

    zero = jnp.zeros((SUBLANES, LANES), F32)
    ends = lax.fori_loop(0, n_steps // 2, local, (zero,) * (2 * n_strips))

    entering = []
    row = lax.broadcasted_iota(jnp.int32, (SUBLANES, LANES), 0)
    for st, lanes in enumerate(strips):
        tr, ti = ends[2 * st], ends[2 * st + 1]
        mr = jnp.broadcast_to(pw_ref[0, n_steps - 1:n_steps, lanes], (SUBLANES, LANES))
        mi = sign * jnp.broadcast_to(pw_ref[1, n_steps - 1:n_steps, lanes], (SUBLANES, LANES))
        for k in (1, 2, 4):
            keep = (row < SUBLANES - k) if reverse else (row >= k)
            rr = jnp.where(keep, pltpu.roll(tr, SUBLANES - k if reverse else k, 0), 0.0)
            ri = jnp.where(keep, pltpu.roll(ti, SUBLANES - k if reverse else k, 0), 0.0)
            tr, ti = tr + (mr * rr - mi * ri), ti + (mr * ri + mi * rr)
            mr, mi = mr * mr - mi * mi, 2.0 * mr * mi
        entering += [_segment_shift(tr, reverse), _segment_shift(ti, reverse)]

    def fix(i, carry):
        for half in range(2):
            rows = rows_of(2 * i + half)
            out = []
            for st, lanes in enumerate(strips):
                (ar, ai), cr, ci = a[st], carry[2 * st], carry[2 * st + 1]
                cr, ci = ar * cr - ai * ci, ar * ci + ai * cr
                s_re[rows, lanes] = s_re[rows, lanes] + cr
                s_im[rows, lanes] = s_im[rows, lanes] + ci
                out += [cr, ci]
            carry = tuple(out)
        return carry

    lax.fori_loop(0, n_steps // 2, fix, tuple(entering))


def _seq_fwd(proj, conv_w, bc_re, bc_im, cc_re, cc_im, dskip, a_pow, name):
    tp = proj.shape[0]
    dh = proj.shape[1] // 4
    nq = dh // LANES
    sw = STATE * N_GROUPS // nq

    def body(b_ref, c_ref, v_ref, u_ref, w_ref, bre_ref, bim_ref, cre_ref, cim_ref, d_ref, pw_ref,
             co_ref, y_ref, g_ref, s_re, s_im, u_il, y_il):
        co_ref[...] = b_ref[...] * _dwconv(c_ref[...] * v_ref[...], w_ref)
        _interleave(u_il, u_ref)
        ub = u_il[...].astype(BF16)
        s_re[...] = jnp.dot(ub, bre_ref[...], preferred_element_type=F32)
        s_im[...] = jnp.dot(ub, bim_ref[...], preferred_element_type=F32)
        _scan(s_re, s_im, pw_ref, False)
        y_il[...] = (jnp.dot(s_re[...].astype(BF16), cre_ref[...], preferred_element_type=F32)
                     - jnp.dot(s_im[...].astype(BF16), cim_ref[...], preferred_element_type=F32))
        _deinterleave(y_ref, y_il)
        y = y_ref[...] + d_ref[...] * u_ref[...]
        y_ref[...] = y
        g_ref[...] = _gelu(y).astype(BF16)

    col = lambda off: pl.BlockSpec((tp, LANES), lambda q, off=off: (0, off * nq + q))
    blk = pl.BlockSpec((tp, LANES), lambda q: (0, q))
    return pl.pallas_call(
        body, name=name, grid=(nq,),
        in_specs=[col(0), col(1), col(2), col(3),
                  pl.BlockSpec((3, LANES), lambda q: (0, q)),
                  pl.BlockSpec((LANES, sw), lambda q: (0, q)), pl.BlockSpec((LANES, sw), lambda q: (0, q)),
                  pl.BlockSpec((sw, LANES), lambda q: (q, 0)), pl.BlockSpec((sw, LANES), lambda q: (q, 0)),
                  pl.BlockSpec((1, LANES), lambda q: (0, q)),
                  pl.BlockSpec((2, tp // SUBLANES, sw), lambda q: (0, 0, q))],
        out_specs=[blk, blk, blk],
        out_shape=[jax.ShapeDtypeStruct((tp, dh), F32), jax.ShapeDtypeStruct((tp, dh), F32),
                   jax.ShapeDtypeStruct((tp, dh), BF16)],
        scratch_shapes=[pltpu.VMEM((tp, sw), F32), pltpu.VMEM((tp, sw), F32),
                        pltpu.VMEM((tp, LANES), F32), pltpu.VMEM((tp, LANES), F32)],
        compiler_params=_cparams("parallel"),
    )(proj, proj, proj, proj, conv_w, bc_re, bc_im, cc_re, cc_im, dskip, a_pow)


def _conv_bwd(proj, dco, conv_w, name):
    tp = proj.shape[0]
    dh = proj.shape[1] // 4
    nq = dh // LANES

    def body(b_ref, c_ref, v_ref, dco_ref, w_ref, dproj_ref, dw_ref, stage, sem):
        q = pl.program_id(0)
        cg = c_ref[...]
        vg = v_ref[...]
        cv = cg * vg
        dco_v = dco_ref[...]
        dcv, dw = _dwconv_bwd(cv, dco_v * b_ref[...], w_ref)
        dw_ref[...] = dw
        stage[0] = (dco_v * _dwconv(cv, w_ref)).astype(BF16)
        stage[1] = (dcv * vg).astype(BF16)
        stage[2] = (dcv * cg).astype(BF16)
        copies = [pltpu.make_async_copy(stage.at[p], dproj_ref.at[:, pl.ds((p * nq + q) * LANES, LANES)], sem.at[p])
                  for p in range(3)]
        for cp in copies:
            cp.start()
        for cp in copies:
            cp.wait()

    col = lambda off: pl.BlockSpec((tp, LANES), lambda q, off=off: (0, off * nq + q))
    return pl.pallas_call(
        body, name=name, grid=(nq,),
        in_specs=[col(0), col(1), col(2), pl.BlockSpec((tp, LANES), lambda q: (0, q)),
                  pl.BlockSpec((3, LANES), lambda q: (0, q))],
        out_specs=[pl.BlockSpec(memory_space=pl.ANY), pl.BlockSpec((3, LANES), lambda q: (0, q))],
        out_shape=[jax.ShapeDtypeStruct((tp, 4 * dh), BF16), jax.ShapeDtypeStruct((3, dh), F32)],
        scratch_shapes=[pltpu.VMEM((3, tp, LANES), BF16), pltpu.SemaphoreType.DMA((3,))],
        compiler_params=_cparams("arbitrary"),
    )(proj, proj, proj, dco, conv_w)


def _ssm_bwd(proj, y, dg, dproj, bc_re, bc_im, cc_re, cc_im, dskip, a_pow, name):
    tp = proj.shape[0]
    dh = proj.shape[1] // 4
    nq = dh // LANES
    sw = STATE * N_GROUPS // nq

    def body(u_ref, y_ref, dg_ref, dproj_in, bre_ref, bim_ref, cre_ref, cim_ref, d_ref, pw_ref,
             dproj_ref, dbre_ref, dbim_ref, dcre_ref, dcim_ref, dd_ref, dar_ref, dai_ref,
             s_re, s_im, l_re, l_im, a_il, b_il, stage, sem):
        del dproj_in
        q = pl.program_id(0)
        nt = (((1,), (1,)), ((), ()))
        tn = (((0,), (0,)), ((), ()))
        _interleave(a_il, u_ref)
        ub = a_il[...].astype(BF16)
        s_re[...] = jnp.dot(ub, bre_ref[...], preferred_element_type=F32)
        s_im[...] = jnp.dot(ub, bim_ref[...], preferred_element_type=F32)
        _scan(s_re, s_im, pw_ref, False)
        dy_rows = dg_ref[...] * _gelu_grad(y_ref[...])
        dd_ref[...] = jnp.sum(dy_rows * u_ref[...], axis=0, keepdims=True)
        _interleave(b_il, dy_rows)
        dy = b_il[...]
        dyb = dy.astype(BF16)
        l_re[...] = lax.dot_general(dyb, cre_ref[...], nt, preferred_element_type=F32)
        l_im[...] = -lax.dot_general(dyb, cim_ref[...], nt, preferred_element_type=F32)
        dcre_ref[...] = lax.dot_general(s_re[...].astype(BF16), dyb, tn, preferred_element_type=F32)
        dcim_ref[...] = -lax.dot_general(s_im[...].astype(BF16), dyb, tn, preferred_element_type=F32)
        _scan(l_re, l_im, pw_ref, True)
        rest = tp - SUBLANES
        for st in range(sw // LANES):
            lanes = slice(st * LANES, (st + 1) * LANES)
            lr, li = l_re[SUBLANES:, lanes], l_im[SUBLANES:, lanes]
            pr, pi = s_re[:rest, lanes], s_im[:rest, lanes]
            lr0, li0 = l_re[:SUBLANES, lanes], l_im[:SUBLANES, lanes]
            pr0, pi0 = _segment_shift(s_re[rest:, lanes], False), _segment_shift(s_im[rest:, lanes], False)
            dar_ref[:, lanes] = (jnp.sum(lr * pr + li * pi, axis=0, keepdims=True)
                                 + jnp.sum(lr0 * pr0 + li0 * pi0, axis=0, keepdims=True))
            dai_ref[:, lanes] = (jnp.sum(li * pr - lr * pi, axis=0, keepdims=True)
                                 + jnp.sum(li0 * pr0 - lr0 * pi0, axis=0, keepdims=True))
        lrb = l_re[...].astype(BF16)
        lib = l_im[...].astype(BF16)
        a_il[...] = (dy * d_ref[...] + lax.dot_general(lrb, bre_ref[...], nt, preferred_element_type=F32)
                     + lax.dot_general(lib, bim_ref[...], nt, preferred_element_type=F32))
        _deinterleave(b_il, a_il)
        stage[...] = b_il[...].astype(BF16)
        dbre_ref[...] = lax.dot_general(ub, lrb, tn, preferred_element_type=F32)
        dbim_ref[...] = lax.dot_general(ub, lib, tn, preferred_element_type=F32)
        cp = pltpu.make_async_copy(stage, dproj_ref.at[:, pl.ds((3 * nq + q) * LANES, LANES)], sem)
        cp.start()
        cp.wait()

    blk = pl.BlockSpec((tp, LANES), lambda q: (0, q))
    bspec = pl.BlockSpec((LANES, sw), lambda q: (0, q))
    cspec = pl.BlockSpec((sw, LANES), lambda q: (q, 0))
    tspec = pl.BlockSpec((2, tp // SUBLANES, sw), lambda q: (0, 0, q))
    nstate = STATE * N_GROUPS
    return pl.pallas_call(
        body, name=name, grid=(nq,),
        in_specs=[pl.BlockSpec((tp, LANES), lambda q: (0, 3 * nq + q)), blk, blk, pl.BlockSpec(memory_space=pl.ANY),
                  bspec, bspec, cspec, cspec, pl.BlockSpec((1, LANES), lambda q: (0, q)), tspec],
        out_specs=[pl.BlockSpec(memory_space=pl.ANY), bspec, bspec, cspec, cspec,
                   pl.BlockSpec((1, LANES), lambda q: (0, q)),
                   pl.BlockSpec((1, sw), lambda q: (0, q)), pl.BlockSpec((1, sw), lambda q: (0, q))],
        out_shape=[jax.ShapeDtypeStruct((tp, 4 * dh), BF16),
                   jax.ShapeDtypeStruct((LANES, nstate), F32), jax.ShapeDtypeStruct((LANES, nstate), F32),
                   jax.ShapeDtypeStruct((nstate, LANES), F32), jax.ShapeDtypeStruct((nstate, LANES), F32),
                   jax.ShapeDtypeStruct((1, dh), F32),
                   jax.ShapeDtypeStruct((1, nstate), F32), jax.ShapeDtypeStruct((1, nstate), F32)],
        input_output_aliases={3: 0},
        scratch_shapes=[pltpu.VMEM((tp, sw), F32)] * 4 + [pltpu.VMEM((tp, LANES), F32)] * 2
        + [pltpu.VMEM((tp, LANES), BF16), pltpu.SemaphoreType.DMA],
        compiler_params=_cparams("arbitrary"),
    )(proj, y, dg, dproj, bc_re, bc_im, cc_re, cc_im, dskip, a_pow)


FFN_TILE = 256


def _ffn_act(up, fw, fb, name):
    tp, two_ff = up.shape
    dff = two_ff // 2
    tc = FFN_TILE
    nj = dff // tc

    def body(ua_ref, uv_ref, wa_ref, wv_ref, ba_ref, bv_ref, act_ref):
        a = _dwconv(ua_ref[...], wa_ref) + ba_ref[...]
        v = _dwconv(uv_ref[...], wv_ref) + bv_ref[...]
        act_ref[...] = (a * _sigmoid(a) * v).astype(BF16)

    lo = lambda r: pl.BlockSpec((r, tc), lambda j: (0, j))
    hi = lambda r: pl.BlockSpec((r, tc), lambda j: (0, nj + j))
    return pl.pallas_call(
        body, name=name, grid=(nj,),
        in_specs=[lo(tp), hi(tp), lo(3), hi(3), lo(1), hi(1)],
        out_specs=lo(tp),
        out_shape=jax.ShapeDtypeStruct((tp, dff), BF16),
        compiler_params=_cparams("parallel"))(up, up, fw, fw, fb, fb)


def _ffn_bwd(up, dact, fw, fb, name):
    tp, two_ff = up.shape
    dff = two_ff // 2
    tc = FFN_TILE
    nj = dff // tc

    def body(ua_ref, uv_ref, da_ref, wa_ref, wv_ref, ba_ref, bv_ref,
             dup_ref, dwa_ref, dwv_ref, dba_ref, dbv_ref, stage, sem):
        j = pl.program_id(0)
        ua = ua_ref[...]
        uv = uv_ref[...]
        a = _dwconv(ua, wa_ref) + ba_ref[...]
        v = _dwconv(uv, wv_ref) + bv_ref[...]
        sg = _sigmoid(a)
        dact_v = da_ref[...]
        da = dact_v * v * sg * (1.0 + a * (1.0 - sg))
        dv = dact_v * a * sg
        dba_ref[...] = jnp.sum(da, axis=0, keepdims=True)
        dbv_ref[...] = jnp.sum(dv, axis=0, keepdims=True)
        dua, dwa = _dwconv_bwd(ua, da, wa_ref)
        duv, dwv = _dwconv_bwd(uv, dv, wv_ref)
        dwa_ref[...] = dwa
        dwv_ref[...] = dwv
        stage[0] = dua.astype(BF16)
        stage[1] = duv.astype(BF16)
        copies = [pltpu.make_async_copy(stage.at[p], dup_ref.at[:, pl.ds((p * nj + j) * tc, tc)], sem.at[p])
                  for p in range(2)]
        for cp in copies:
            cp.start()
        for cp in copies:
            cp.wait()

    lo = lambda r: pl.BlockSpec((r, tc), lambda j: (0, j))
    hi = lambda r: pl.BlockSpec((r, tc), lambda j: (0, nj + j))
    return pl.pallas_call(
        body, name=name, grid=(nj,),
        in_specs=[lo(tp), hi(tp), lo(tp), lo(3), hi(3), lo(1), hi(1)],
        out_specs=[pl.BlockSpec(memory_space=pl.ANY), lo(3), lo(3), lo(1), lo(1)],
        out_shape=[jax.ShapeDtypeStruct((tp, two_ff), BF16),
                   jax.ShapeDtypeStruct((3, dff), F32), jax.ShapeDtypeStruct((3, dff), F32),
                   jax.ShapeDtypeStruct((1, dff), F32), jax.ShapeDtypeStruct((1, dff), F32)],
        scratch_shapes=[pltpu.VMEM((2, tp, tc), BF16), pltpu.SemaphoreType.DMA((2,))],
        compiler_params=_cparams("arbitrary"))(up, up, dact, fw, fw, fb, fb)


def _zoh(lr, li, ld):
    dt = jnp.exp(ld)
    mag = jnp.exp(lr * dt)
    ang = li * dt
    ar = mag * jnp.cos(ang)
    ai = mag * jnp.sin(ang)
    den = lr * lr + li * li
    nr = ar - 1.0
    fr = (nr * lr + ai * li) / den
    fi = (ai * lr - nr * li) / den
    return dt, ar, ai, den, nr, fr, fi


def _s5_prep(lr, li, ld, b_re, b_im, n_pow, name):
    nstate = lr.shape[1]

    def body(lr_ref, li_ref, ld_ref, bre_ref, bim_ref, pw_ref, bcre_ref, bcim_ref):
        _, ar, ai, _, _, fr, fi = _zoh(lr_ref[...], li_ref[...], ld_ref[...])
        bre = bre_ref[...]
        bim = bim_ref[...]
        bcre_ref[...] = (fr * bre - fi * bim).astype(BF16)
        bcim_ref[...] = (fr * bim + fi * bre).astype(BF16)
        row = lax.broadcasted_iota(jnp.int32, (SUBLANES, nstate), 0)
        pr, pi = jnp.zeros((SUBLANES, nstate), F32), jnp.zeros((SUBLANES, nstate), F32)
        cr, ci = ar, ai
        for t in range(SUBLANES):
            pr, pi = jnp.where(row == t, cr, pr), jnp.where(row == t, ci, pi)
            cr, ci = cr * ar - ci * ai, cr * ai + ci * ar
        pw_ref[0, 0:SUBLANES, :] = pr
        pw_ref[1, 0:SUBLANES, :] = pi
        n = SUBLANES
        while n < n_pow:
            m = min(n, n_pow - n)
            tr, ti = pw_ref[0, n - 1:n, :], pw_ref[1, n - 1:n, :]
            xr, xi = pw_ref[0, 0:m, :], pw_ref[1, 0:m, :]
            pw_ref[0, n:n + m, :] = xr * tr - xi * ti
            pw_ref[1, n:n + m, :] = xr * ti + xi * tr
            n += m

    vmem = pl.BlockSpec(memory_space=pltpu.VMEM)
    return pl.pallas_call(
        body, name=name, in_specs=[vmem] * 5, out_specs=[vmem] * 3,
        out_shape=[jax.ShapeDtypeStruct((2, n_pow, nstate), F32)] + [jax.ShapeDtypeStruct(b_re.shape, BF16)] * 2,
        compiler_params=pltpu.CompilerParams(vmem_limit_bytes=VMEM_LIMIT))(lr, li, ld, b_re, b_im)


def _s5_prep_bwd(lr, li, ld, b_re, b_im, da_re, da_im, dbc_re, dbc_im, name):
    def body(lr_ref, li_ref, ld_ref, bre_ref, bim_ref, dar_ref, dai_ref, dbcre_ref, dbcim_ref,
             dlr_ref, dli_ref, dld_ref, dbre_ref, dbim_ref):
        lr, li = lr_ref[...], li_ref[...]
        dt, ar, ai, den, nr, fr, fi = _zoh(lr, li, ld_ref[...])
        bre, bim = bre_ref[...], bim_ref[...]
        gre, gim = dbcre_ref[...], dbcim_ref[...]
        dbre_ref[...] = fr * gre + fi * gim
        dbim_ref[...] = fr * gim - fi * gre
        g_fr = jnp.sum(gre * bre + gim * bim, axis=0, keepdims=True)
        g_fi = jnp.sum(gim * bre - gre * bim, axis=0, keepdims=True)
        g_ar = dar_ref[...] + (g_fr * lr - g_fi * li) / den
        g_ai = dai_ref[...] + (g_fr * li + g_fi * lr) / den
        d_lr = (g_fr * (nr - 2.0 * fr * lr) + g_fi * (ai - 2.0 * fi * lr)) / den
        d_li = (g_fr * (ai - 2.0 * fr * li) - g_fi * (nr + 2.0 * fi * li)) / den
        g_logmag = g_ar * ar + g_ai * ai
        g_ang = g_ai * ar - g_ar * ai
        dlr_ref[...] = d_lr + g_logmag * dt
        dli_ref[...] = d_li + g_ang * dt
        d_ld = (g_logmag * lr + g_ang * li) * dt
        n = d_ld.shape[1]
        sh = 1
        while sh < STATE:
            d_ld = d_ld + pltpu.roll(d_ld, n - sh, 1)
            sh *= 2
        dld_ref[...] = d_ld

    vmem = pl.BlockSpec(memory_space=pltpu.VMEM)
    row = jax.ShapeDtypeStruct(lr.shape, F32)
    return pl.pallas_call(
        body, name=name, in_specs=[vmem] * 9, out_specs=[vmem] * 5,
        out_shape=[row, row, row, jax.ShapeDtypeStruct(b_re.shape, F32), jax.ShapeDtypeStruct(b_re.shape, F32)],
    )(lr, li, ld, b_re, b_im, da_re, da_im, dbc_re, dbc_im)


def _compact_b(bb):
    bq = bb.reshape(N_GROUPS // 8, 8, STATE, GROUP)
    m = jnp.einsum("ab,qbph->qahbp", jnp.eye(8, dtype=bb.dtype), bq).reshape(N_GROUPS // 8, LANES, 8 * STATE)
    return m.transpose(1, 0, 2).reshape(LANES, N_GROUPS * STATE)


def _expand_b(m):
    d = m.reshape(8, GROUP, N_GROUPS // 8, 8, STATE)
    return jnp.einsum("ahqap->qahp", d).reshape(N_GROUPS, GROUP, STATE)


def _compact_c(c):
    cq = c.reshape(N_GROUPS // 8, 8, GROUP, STATE)
    return jnp.einsum("ab,qbhp->qbpah", jnp.eye(8, dtype=c.dtype), cq).reshape(N_GROUPS * STATE, LANES)


def _expand_c(m):
    d = m.reshape(N_GROUPS // 8, 8, STATE, 8, GROUP)
    return jnp.einsum("qbpbh->qbhp", d).reshape(N_GROUPS, GROUP, STATE)


def _local_step(x, target, p, ex):
    seq, d = x.shape
    n_real = N_META + seq
    tp = -(-n_real // ROW_ALIGN) * ROW_ALIGN

    h0, hn1 = _input_norm_fwd(x, p["meta_tokens"], p["norm_mix_g"] + ex.zero, tp, "norm_mix")
    ex.forward("first", hn1)
    nstate = N_GROUPS * STATE
    s5 = (p["ssm_lam_re"].reshape(1, nstate), p["ssm_lam_im"].reshape(1, nstate),
          jnp.repeat(p["ssm_log_dt"].reshape(-1), STATE).reshape(1, nstate),
          _compact_b(p["ssm_b_re"]), _compact_b(p["ssm_b_im"]))
    a_pow, bc_re, bc_im = _s5_prep(*s5, tp // SUBLANES, "s5_prep")
    cc_re = _compact_c(p["ssm_c_re"]).astype(BF16)
    cc_im = _compact_c(p["ssm_c_im"]).astype(BF16)
    dskip = p["ssm_d"].reshape(1, -1)
    first = ex.weights("first", bc_re)
    proj = _mm(hn1, first["w_in"], "nn", "proj")
    started = ex.forward("mid", proj)
    co, y, g = _seq_fwd(proj, p["conv_w"] + started[0, 0], bc_re, bc_im, cc_re, cc_im, dskip, a_pow, "seq_fwd")
    mid = ex.weights("mid", g)
    z = _mm(g, mid["ssm_w_glu"], "nn", "glu")
    mixed = _mix_fwd(co, y, z, p["gain_conv_out"], p["gain_ssm_out"], "mix_fwd")
    started = ex.forward("up", mixed)
    h1, hn2 = _proj_res_norm(mixed, mid["w_out"], h0, p["norm_ffn_g"], started, "out_proj_norm")
    late = ex.weights("up", hn2)
    up = _mm(hn2, late["w_up"], "nn", "up_proj")
    started = ex.forward("down", up)
    act = _ffn_act(up, p["ffn_conv_w"] + started[0, 0], p["ffn_conv_b"], "ffn_act")
    late.update(ex.weights("down", act))
    loss, dh2, dh2b, d_gfin = _proj_loss_bwd(act, late["w_down"], h1, target, p["norm_final_g"], n_real,
                                             "down_proj_loss")

    g_w_down = _mm(act, dh2b, "tn", "g_w_down")
    dact = _mm(dh2b, late["w_down"], "nt", "d_act")
    dup, dfw_a, dfw_v, dfb_a, dfb_v = _ffn_bwd(up, dact, p["ffn_conv_w"], p["ffn_conv_b"], "ffn_bwd")
    g_w_up = _mm(hn2, dup, "tn", "g_w_up")
    started = ex.grads_ready("late", {"w_up": g_w_up, "w_down": g_w_down})
    dh1, dh1b, d_gffn = _proj_norm_bwd(dup, late["w_up"], h1, p["norm_ffn_g"], dh2, started, "d_hn2_norm_bwd")
    started = ex.grads_send("late", dh1)
    g_w_out = _mm(mixed, dh1b, "tn", "g_w_out", after=started)
    dco, dz, dgp, d_gc, d_gs = _proj_mix_bwd(dh1b, mid["w_out"], co, y, z, p["gain_conv_out"],
                                             p["gain_ssm_out"], "d_mixed_mix_bwd")
    g_w_glu = _mm(g, dz, "tn", "g_w_glu")
    started = ex.grads_ready("mid", {"ssm_w_glu": g_w_glu, "w_out": g_w_out})
    dg = _mm(dz, mid["ssm_w_glu"], "nt", "d_gelu", acc_in=dgp, after=started)
    started = ex.grads_send("mid", dg)
    dproj, d_conv_w = _conv_bwd(proj, dco, p["conv_w"] + started[0, 0], "conv_bwd")
    (dproj, dbc_re, dbc_im, dcc_re, dcc_im, d_dskip, da_re, da_im) = _ssm_bwd(
        proj, y, dg, dproj, bc_re, bc_im, cc_re, cc_im, dskip, a_pow, "ssm_bwd")
    g_w_in = _mm(hn1, dproj, "tn", "g_w_in")
    started = ex.grads_ready("first", {"w_in": g_w_in})
    dhn1 = _mm(dproj, first["w_in"], "nt", "d_hn1", after=started)
    started = ex.grads_send("first", dhn1)
    grad_x, d_meta, d_gmix = _input_norm_bwd(h0, p["norm_mix_g"] + started[0, 0], dhn1, dh1, n_real, "norm_mix_bwd")

    d_lam_re, d_lam_im, d_log_dt, d_b_re, d_b_im = _s5_prep_bwd(*s5, da_re, da_im, dbc_re, dbc_im, "s5_prep_bwd")
    d_lam_re, d_lam_im = d_lam_re.reshape(N_GROUPS, STATE), d_lam_im.reshape(N_GROUPS, STATE)
    d_log_dt = d_log_dt[0, ::STATE]
    d_b_re, d_b_im = _expand_b(d_b_re), _expand_b(d_b_im)
    grads = {
        "meta_tokens": d_meta, "norm_mix_g": d_gmix, "w_in": g_w_in, "conv_w": d_conv_w,
        "ssm_lam_re": d_lam_re, "ssm_lam_im": d_lam_im, "ssm_log_dt": d_log_dt,
        "ssm_b_re": d_b_re, "ssm_b_im": d_b_im, "ssm_c_re": _expand_c(dcc_re), "ssm_c_im": _expand_c(dcc_im),
        "ssm_d": d_dskip.reshape(N_GROUPS, GROUP), "ssm_w_glu": g_w_glu,
        "gain_conv_out": d_gc, "gain_ssm_out": d_gs, "w_out": g_w_out, "norm_ffn_g": d_gffn,
        "w_up": g_w_up, "ffn_conv_w": jnp.concatenate([dfw_a, dfw_v], axis=1),
        "ffn_conv_b": jnp.concatenate([dfb_a, dfb_v], axis=1), "w_down": g_w_down, "norm_final_g": d_gfin,
    }
    return loss[0, 0], grad_x, grads


def _view(ref, axis, start, size):
    idx = [slice(None)] * len(ref.shape)
    idx[axis] = pl.ds(start, size)
    return ref.at[tuple(idx)]


def _exchange(name, ins, outs, aliases, local_copies, remote_copies):
    ni, no = len(ins), len(outs)
    nl, nr = len(local_copies), len(remote_copies)

    def body(*refs):
        in_refs, out_refs = refs[:ni], refs[ni:ni + no]
        send_sems, recv_sems, local_sems = refs[ni + no:]
        x, y, c = lax.axis_index("x"), lax.axis_index("y"), lax.axis_index("c")
        pos = (x, y, c, 2 * x + y)
        locals_ = [pltpu.make_async_copy(s(in_refs, out_refs, pos), d(in_refs, out_refs, pos), local_sems.at[i])
                   for i, (s, d) in enumerate(local_copies)]
        remotes = []
        for i, (s, d, flip) in enumerate(remote_copies):
            peer = (1 - x if "x" in flip else x, 1 - y if "y" in flip else y, 1 - c if "c" in flip else c)
            remotes.append(pltpu.make_async_remote_copy(
                src_ref=s(in_refs, out_refs, pos), dst_ref=d(in_refs, out_refs, pos),
                send_sem=send_sems.at[i], recv_sem=recv_sems.at[i], device_id=peer, device_id_type=MESH))
        for cp in locals_ + remotes:
            cp.start()
        for cp in remotes:
            cp.wait_recv()
        for cp in remotes:
            cp.wait_send()
        for cp in locals_:
            cp.wait()

    hbm = pl.BlockSpec(memory_space=pl.ANY)
    return pl.pallas_call(
        body, name=name, in_specs=[hbm] * ni, out_specs=[hbm] * no, out_shape=outs,
        input_output_aliases=aliases,
        scratch_shapes=[pltpu.SemaphoreType.DMA((nr,)), pltpu.SemaphoreType.DMA((nr,)),
                        pltpu.SemaphoreType.DMA((max(nl, 1),))],
    )(*ins)


BIG = {"w_in": (0, 1), "ssm_w_glu": (1, 0), "w_out": (1, 0), "w_up": (0, 1), "w_down": (1, 0)}
BIG_NAMES = tuple(BIG)
FLIPS = ("y", "x", "xy")


def _peer_chip(pos, flip):
    x, y, _, _ = pos
    return 2 * (1 - x if "x" in flip else x) + (1 - y if "y" in flip else y)


def _block_rows(rows, cols, itemsize, mult):
    return _pick_tile(rows, max(mult, (2 * 1024 * 1024) // (cols * itemsize)), mult)


def _cast_into_full(w, kc, shard_axis, name):
    r, cdim = w.shape
    tr = _block_rows(r, cdim, 4, 16)
    nb = r // tr

    def body(kc_ref, w_ref, o_ref):
        o_ref[...] = w_ref[...].astype(BF16)

    if shard_axis == 1:
        full, o_spec = (r, 4 * cdim), pl.BlockSpec((tr, cdim), lambda i, kc: (i, kc[0]))
    else:
        full, o_spec = (4 * r, cdim), pl.BlockSpec((tr, cdim), lambda i, kc: (kc[0] * nb + i, 0))
    return pl.pallas_call(
        body, name=name,
        grid_spec=pltpu.PrefetchScalarGridSpec(
            num_scalar_prefetch=1, grid=(nb,), in_specs=[pl.BlockSpec((tr, cdim), lambda i, kc: (i, 0))],
            out_specs=o_spec),
        out_shape=jax.ShapeDtypeStruct(full, BF16), compiler_params=_cparams("parallel"))(kc, w)


def _pair_sum(g, recv, kc, half_axis, name, out_dtype):
    hr, hc = recv.shape
    tr = _block_rows(hr, hc, 4, 16)
    nb = hr // tr

    def body(kc_ref, g_ref, r_ref, o_ref):
        o_ref[...] = (g_ref[...] + r_ref[...]).astype(out_dtype)

    if half_axis == 0:
        g_spec = pl.BlockSpec((tr, hc), lambda i, kc: (kc[1] * nb + i, 0))
    elif half_axis == 1:
        g_spec = pl.BlockSpec((tr, hc), lambda i, kc: (i, kc[1]))
    else:
        g_spec = pl.BlockSpec((tr, hc), lambda i, kc: (i, 0))
    same = pl.BlockSpec((tr, hc), lambda i, kc: (i, 0))
    return pl.pallas_call(
        body, name=name,
        grid_spec=pltpu.PrefetchScalarGridSpec(num_scalar_prefetch=1, grid=(nb,), in_specs=[g_spec, same],
                                               out_specs=same),
        out_shape=jax.ShapeDtypeStruct((hr, hc), out_dtype), compiler_params=_cparams("parallel"))(kc, g, recv)


def _chip_sum(own, recv, kc, own_axis, out_axis, name):
    _, sr, sc = recv.shape
    tr = _block_rows(sr, sc, 4, 16)
    nb = sr // tr

    def body(kc_ref, o_ref, r_ref, t_ref):
        k = kc_ref[0]
        own_v = o_ref[...].astype(F32)
        r = [r_ref[m].astype(F32) for m in range(3)]
        terms = []
        for kk in range(4):
            m = jnp.bitwise_xor(k, kk)
            terms.append(jnp.where(m == 0, own_v, jnp.where(m == 1, r[0], jnp.where(m == 2, r[1], r[2]))))
        t_ref[...] = (terms[0] + terms[1]) + (terms[2] + terms[3])

    if own_axis == 0:
        own_spec = pl.BlockSpec((tr, sc), lambda i, kc: (kc[0] * nb + i, 0))
    elif own_axis == 1:
        own_spec = pl.BlockSpec((tr, sc), lambda i, kc: (i, kc[0]))
    else:
        own_spec = pl.BlockSpec((tr, sc), lambda i, kc: (kc[1] * nb + i, 0))
    if out_axis == 0:
        out_full, out_spec = (2 * sr, sc), pl.BlockSpec((tr, sc), lambda i, kc: (kc[1] * nb + i, 0))
    else:
        out_full, out_spec = (sr, 2 * sc), pl.BlockSpec((tr, sc), lambda i, kc: (i, kc[1]))
    return pl.pallas_call(
        body, name=name,
        grid_spec=pltpu.PrefetchScalarGridSpec(
            num_scalar_prefetch=1, grid=(nb,),
            in_specs=[own_spec, pl.BlockSpec((3, tr, sc), lambda i, kc: (0, i, 0))],
            out_specs=out_spec),
        out_shape=jax.ShapeDtypeStruct(out_full, F32), compiler_params=_cparams("parallel"))(kc, own, recv)


def _adamw(w, g, m, v, name):
    r, cdim = w.shape
    tr = _block_rows(r, cdim, 4, 8)
    c1 = 1.0 - ADAM_B1 ** ADAM_STEP
    c2 = 1.0 - ADAM_B2 ** ADAM_STEP

    def body(w_ref, g_ref, m_ref, v_ref, go_ref, d_ref, nm_ref, nv_ref):
        gv = g_ref[...]
        go_ref[...] = gv
        nm = ADAM_B1 * m_ref[...] + (1.0 - ADAM_B1) * gv
        nv = ADAM_B2 * v_ref[...] + (1.0 - ADAM_B2) * (gv * gv)
        d_ref[...] = -ADAM_LR * ((nm / c1) / (jnp.sqrt(nv / c2) + ADAM_EPS) + ADAM_WD * w_ref[...])
        nm_ref[...] = nm
        nv_ref[...] = nv

    spec = _rows(cdim, tr)
    return pl.pallas_call(body, name=name, grid=(r // tr,), in_specs=[spec] * 4, out_specs=[spec] * 4,
                          out_shape=[jax.ShapeDtypeStruct((r, cdim), F32)] * 4,
                          compiler_params=_cparams("parallel"))(w, g, m, v)


def _adamw_whole(ws, gs, ms, vs, name):
    n = len(ws)
    c1 = 1.0 - ADAM_B1 ** ADAM_STEP
    c2 = 1.0 - ADAM_B2 ** ADAM_STEP

    def body(*refs):
        for i in range(n):
            w_ref, g_ref, m_ref, v_ref, d_ref, nm_ref, nv_ref = [refs[j * n + i] for j in range(7)]
            gv = g_ref[...]
            nm = ADAM_B1 * m_ref[...] + (1.0 - ADAM_B1) * gv
            nv = ADAM_B2 * v_ref[...] + (1.0 - ADAM_B2) * (gv * gv)
            d_ref[...] = -ADAM_LR * ((nm / c1) / (jnp.sqrt(nv / c2) + ADAM_EPS) + ADAM_WD * w_ref[...])
            nm_ref[...] = nm
            nv_ref[...] = nv

    vmem = pl.BlockSpec(memory_space=pltpu.VMEM)
    out = pl.pallas_call(body, name=name, in_specs=[vmem] * (4 * n), out_specs=[vmem] * (3 * n),
                         out_shape=[jax.ShapeDtypeStruct(a.shape, F32) for a in ws] * 3,
                         compiler_params=pltpu.CompilerParams(vmem_limit_bytes=VMEM_LIMIT))(*ws, *gs, *ms, *vs)
    return out[:n], out[n:2 * n], out[2 * n:]


SIDE_EFFECT = pltpu.SideEffectType.DATAFLOW_SIDE_EFFECTING


def _descriptors(copies, refs, send_sems, recv_sems, sem_off=0):
    x, y, c = lax.axis_index("x"), lax.axis_index("y"), lax.axis_index("c")
    pos = (x, y, c, 2 * x + y)
    out = []
    for i, (s, d, flip) in enumerate(copies):
        peer = (1 - x if "x" in flip else x, 1 - y if "y" in flip else y, 1 - c if "c" in flip else c)
        out.append(pltpu.make_async_remote_copy(
            src_ref=s(refs, refs, pos), dst_ref=d(refs, refs, pos),
            send_sem=send_sems.at[sem_off + i], recv_sem=recv_sems.at[sem_off + i],
            device_id=peer, device_id_type=MESH))
    return out


def _shifted(copies, off):
    return [(lambda I, O, pos, s=s: s(I[off:], O[off:], pos), lambda I, O, pos, d=d: d(I[off:], O[off:], pos), flip)
            for s, d, flip in copies]


BARRIER_IDS = {"c": (1, 2), "ici": (3, 4)}


def _exchange_start(name, bufs, copies, turns, after=None):
    n, nr = len(bufs), len(copies)
    na = 0 if after is None else 1
    flips = sorted({flip for _, _, flip in copies})
    kind = "c" if flips == ["c"] else "ici"
    collective_id = BARRIER_IDS[kind][turns[kind] % 2]
    turns[kind] += 1

    def body(*refs):
        x, y, c = lax.axis_index("x"), lax.axis_index("y"), lax.axis_index("c")
        barrier = pltpu.get_barrier_semaphore()
        for flip in flips:
            peer = (1 - x if "x" in flip else x, 1 - y if "y" in flip else y, 1 - c if "c" in flip else c)
            pl.semaphore_signal(barrier, inc=1, device_id=peer, device_id_type=MESH)
        pl.semaphore_wait(barrier, len(flips))
        for cp in _descriptors(copies, refs[:n], refs[n + na], refs[n + na + 1]):
            cp.start()
        token = refs[2 * n + na + 2]
        token[...] = jnp.zeros_like(token)

    hbm = pl.BlockSpec(memory_space=pltpu.HBM)
    sem = pl.BlockSpec(memory_space=pltpu.SEMAPHORE)
    out = pl.pallas_call(
        body, name=name,
        in_specs=[hbm] * n + [pl.BlockSpec(memory_space=pl.ANY)] * na,
        out_specs=(sem, sem, *[hbm] * n, pl.BlockSpec(memory_space=pltpu.VMEM)),
        out_shape=(pltpu.SemaphoreType.DMA((nr,)), pltpu.SemaphoreType.DMA((nr,)),
                   *[pltpu.HBM(b.shape, b.dtype) for b in bufs], jax.ShapeDtypeStruct((SUBLANES, LANES), F32)),
        input_output_aliases={i: 2 + i for i in range(n)},
        compiler_params=pltpu.CompilerParams(has_side_effects=SIDE_EFFECT, collective_id=collective_id),
    )(*[pltpu.with_memory_space_constraint(b, pltpu.HBM) for b in bufs], *([after] * na))
    return out[0], out[1], list(out[2:2 + n]), out[2 + n]


def _exchange_wait(name, send_sems, recv_sems, bufs, copies, after, sem_off=0):
    n = len(bufs)

    def body(*refs):
        for cp in _descriptors(copies, refs[:n], refs[n], refs[n + 1], sem_off):
            cp.wait_send()
            cp.wait_recv()

    hbm = pl.BlockSpec(memory_space=pltpu.HBM)
    sem = pl.BlockSpec(memory_space=pltpu.SEMAPHORE)
    out = pl.pallas_call(
        body, name=name,
        in_specs=[hbm] * n + [sem, sem, pl.BlockSpec(memory_space=pl.ANY)],
        out_specs=tuple([hbm] * n),
        out_shape=tuple(pltpu.HBM(b.shape, b.dtype) for b in bufs),
        input_output_aliases={i: i for i in range(n)},
        compiler_params=pltpu.CompilerParams(has_side_effects=SIDE_EFFECT),
    )(*bufs, send_sems, recv_sems, after)
    return list(out)


FIRST = ("w_in",)
MID = ("ssm_w_glu", "w_out")
LATE = ("w_up", "w_down")
GROUPS = {"first": FIRST, "mid": MID, "late": LATE}
ARRIVALS = {"first": FIRST, "mid": MID, "up": ("w_up",), "down": ("w_down",)}


def _gather_copies(names, shard_shapes):
    def region(i, chip, c):
        half_axis, shard_axis = BIG[names[i]]
        ssize = shard_shapes[i][shard_axis]
        hsize = shard_shapes[i][half_axis] // 2
        return lambda ref: _view(_view(ref, shard_axis, chip * ssize, ssize), half_axis, c * hsize, hsize)

    ici, d2d = [], []
    for i in range(len(names)):
        for flip in FLIPS:
            ici.append((lambda I, O, pos, i=i: region(i, pos[3], pos[2])(I[i]),
                        lambda I, O, pos, i=i: region(i, pos[3], pos[2])(O[i]), flip))
            d2d.append((lambda I, O, pos, i=i, flip=flip: region(i, _peer_chip(pos, flip), pos[2])(I[i]),
                        lambda I, O, pos, i=i, flip=flip: region(i, _peer_chip(pos, flip), pos[2])(O[i]), "c"))
    return ici, d2d


def _half_shape(n, shape):
    r, cdim = shape
    return (r // 2, cdim) if BIG[n][0] == 0 else (r, cdim // 2)


def _sub_shape(n, shape):
    hr, hc = _half_shape(n, shape)
    return (hr, hc // 4) if BIG[n][1] == 1 else (hr // 4, hc)


def _pair_copies(names, shapes, with_pack, dst_off):
    n = len(names)

    def other_half(i, ref, pos):
        half_axis = BIG[names[i]][0]
        hsize = shapes[i][half_axis] // 2
        return _view(ref, half_axis, (1 - pos[2]) * hsize, hsize)

    copies = [(lambda I, O, pos, i=i: other_half(i, I[i], pos), lambda I, O, pos, i=i: O[dst_off + i], "c")
              for i in range(n)]
    if with_pack:
        copies.append((lambda I, O, pos: I[n], lambda I, O, pos: O[dst_off + n], "c"))
    return copies


def _chip_copies(names, shapes, pack_rows, dst_off):
    n = len(names)

    def piece(i, ref, chip):
        shard_axis = BIG[names[i]][1]
        ssize = _sub_shape(names[i], shapes[i])[shard_axis]
        return _view(ref, shard_axis, chip * ssize, ssize)

    copies = []
    for i in range(n):
        for slot, flip in enumerate(FLIPS):
            copies.append((lambda I, O, pos, i=i, flip=flip: piece(i, I[i], _peer_chip(pos, flip)),
                           lambda I, O, pos, i=i, slot=slot: O[dst_off + i].at[slot], flip))
    if pack_rows:
        for slot, flip in enumerate(FLIPS):
            copies.append((lambda I, O, pos: _view(I[n], 0, pos[2] * (pack_rows // 2), pack_rows // 2),
                           lambda I, O, pos, slot=slot: O[dst_off + n].at[slot], flip))
    return copies


class _Exchanges:
    def __init__(self, shards, tiny, kc):
        self.kc = kc
        wb = {n: _cast_into_full(shards[n], kc, BIG[n][1], "cast_" + n) for n in BIG_NAMES}
        self.gathering, self.forwarding, self.pairing, self.reducing = {}, {}, {}, {}
        self.turns = {"c": 0, "ici": 0}
        tiny_copies = [(lambda I, O, pos: I[0], lambda I, O, pos: O[1].at[pos[3]], flip) for flip in FLIPS]
        self.gathering["tiny"] = (0, 0, 2, tiny_copies, None)
        bufs, copies = [tiny, lax.empty((4,) + tiny.shape, F32)], list(tiny_copies)
        for group, names in ARRIVALS.items():
            ici, d2d = _gather_copies(names, [shards[n].shape for n in names])
            self.gathering[group] = (len(bufs), len(copies), len(names), ici, d2d)
            copies += _shifted(ici, len(bufs))
            bufs += [wb[n] for n in names]
        self.started = _exchange_start("gather_start", bufs, copies, self.turns)
        self.zero = self.started[3][0, 0]

    def _arrived(self, group, after):
        buf_off, sem_off, n, ici, _ = self.gathering[group]
        send_sems, recv_sems, bufs, _ = self.started
        return _exchange_wait("gather_%s_wait" % group, send_sems, recv_sems, bufs[buf_off:buf_off + n], ici, after,
                              sem_off)

    def small_params(self, kc):
        tiny, got = self._arrived("tiny", self.started[3])
        return lax.dynamic_update_index_in_dim(got, tiny, kc[0], 0)

    def forward(self, group, after):
        d2d = self.gathering[group][4]
        self.forwarding[group] = (_exchange_start("forward_%s_start" % group, self._arrived(group, after), d2d,
                                                  self.turns), d2d)
        return self.forwarding[group][0][3]

    def weights(self, group, after):
        if group not in self.forwarding:
            after = self.forward(group, after)
        (send_sems, recv_sems, bufs, _), d2d = self.forwarding[group]
        full = _exchange_wait("forward_%s_wait" % group, send_sems, recv_sems, bufs, d2d, after)
        return dict(zip(ARRIVALS[group], full))

    def grads_ready(self, group, grads):
        names = GROUPS[group]
        gs = [grads[n] for n in names]
        land = [lax.empty(_half_shape(n, g.shape), F32) for n, g in zip(names, gs)]
        copies = _pair_copies(names, [g.shape for g in gs], False, len(names))
        started = _exchange_start("pair_%s_start" % group, gs + land, copies, self.turns)
        self.pairing[group] = (started, copies)
        return started[3]

    def grads_send(self, group, after):
        names = GROUPS[group]
        n = len(names)
        (send_sems, recv_sems, bufs, _), copies = self.pairing[group]
        bufs = _exchange_wait("pair_%s_wait" % group, send_sems, recv_sems, bufs, copies, after)
        chip = [_pair_sum(bufs[i], bufs[n + i], self.kc, BIG[names[i]][0], "pair_sum_" + names[i], BF16)
                for i in range(n)]
        shapes = [bufs[i].shape for i in range(n)]
        land = [lax.empty((3,) + _sub_shape(names[i], shapes[i]), BF16) for i in range(n)]
        copies = _chip_copies(names, shapes, 0, n)
        started = _exchange_start("reduce_%s_start" % group, chip + land, copies, self.turns)
        self.reducing[group] = (started, copies)
        return started[3]

    def finish_pack(self, pack):
        kc = self.kc
        prow = pack.shape[0] // 2
        recv = _exchange("reduce_d2d", [pack], [jax.ShapeDtypeStruct(pack.shape, F32)], {}, [],
                         _pair_copies((), [], True, 0))
        chip_pack = _pair_sum(pack, recv[0], kc, None, "pair_sum_pack", F32)
        copies = _chip_copies((), [], pack.shape[0], 1)
        land = lax.empty((3, prow, pack.shape[1]), F32)
        pack_sems_s, pack_sems_r, pack_bufs, after = _exchange_start("reduce_pack_start", [chip_pack, land], copies,
                                                                     self.turns)

        names, chips, recvs = (), [], []
        for group, group_names in GROUPS.items():
            (send_sems, recv_sems, bufs, _), group_copies = self.reducing[group]
            bufs = _exchange_wait("reduce_%s_wait" % group, send_sems, recv_sems, bufs, group_copies, after)
            n = len(group_names)
            names, chips, recvs = names + group_names, chips + bufs[:n], recvs + bufs[n:]
            after = bufs[n]
        total = [_chip_sum(chips[i], recvs[i], kc, BIG[n][1], BIG[n][0], "chip_sum_" + n)
                 for i, n in enumerate(names)]

        def my_half(half_axis, ref, pos):
            hsize = ref.shape[half_axis] // 2
            return _view(ref, half_axis, pos[2] * hsize, hsize)

        swap = [(lambda I, O, pos, i=i, n=n: my_half(BIG[n][0], I[i], pos),
                 lambda I, O, pos, i=i, n=n: my_half(BIG[n][0], O[i], pos), "c") for i, n in enumerate(names)]
        self.swapping = (_exchange_start("swap_start", total, swap, self.turns), swap, names)

        chip_pack, recv_pack = _exchange_wait("reduce_pack_wait", pack_sems_s, pack_sems_r, pack_bufs, copies,
                                              self.swapping[0][3])
        total_pack = _chip_sum(chip_pack, recv_pack, kc, None, 0, "chip_sum_pack")
        swap = [(lambda I, O, pos: my_half(0, I[0], pos), lambda I, O, pos: my_half(0, O[0], pos), "c")]
        return _exchange("swap_pack", [total_pack], [jax.ShapeDtypeStruct(pack.shape, F32)], {0: 0}, [], swap)[0]

    def finish_big(self, after):
        (send_sems, recv_sems, bufs, _), swap, names = self.swapping
        return dict(zip(names, _exchange_wait("swap_wait", send_sems, recv_sems, bufs, swap, after)))


WEIGHTS = ("meta_tokens", "norm_mix_g", "w_in", "conv_w", "ssm_lam_re", "ssm_lam_im", "ssm_log_dt", "ssm_b_re",
           "ssm_b_im", "ssm_c_re", "ssm_c_im", "ssm_d", "ssm_w_glu", "gain_conv_out", "gain_ssm_out", "w_out",
           "norm_ffn_g", "w_up", "ffn_conv_w", "ffn_conv_b", "w_down", "norm_final_g")
TINY_SHARDED = ("meta_tokens", "conv_w", "ffn_conv_w")
REPLICATED = tuple(n for n in WEIGHTS if n not in BIG and n not in TINY_SHARDED)
PACK_COLS = 512


def _pack(arrays, row_mult, cols):
    flat = jnp.concatenate([a.reshape(-1).astype(F32) for a in arrays])
    n = flat.shape[0]
    total = -(-n // (row_mult * cols)) * (row_mult * cols)
    return jnp.concatenate([flat, jnp.zeros((total - n,), F32)]).reshape(total // cols, cols)


def _unpack(packed, shapes):
    flat = packed.reshape(-1)
    out, off = [], 0
    for s in shapes:
        n = math.prod(s)
        out.append(flat[off:off + n].reshape(s))
        off += n
    return out


def kernel(x, meta_tokens, norm_mix_g, w_in, conv_w, ssm_lam_re, ssm_lam_im, ssm_log_dt, ssm_b_re, ssm_b_im, ssm_c_re, ssm_c_im, ssm_d, ssm_w_glu, gain_conv_out, gain_ssm_out, w_out, norm_ffn_g, w_up, ffn_conv_w, ffn_conv_b, w_down, norm_final_g, loss_target, m_meta_tokens, m_norm_mix_g, m_w_in, m_conv_w, m_ssm_lam_re, m_ssm_lam_im, m_ssm_log_dt, m_ssm_b_re, m_ssm_b_im, m_ssm_c_re, m_ssm_c_im, m_ssm_d, m_ssm_w_glu, m_gain_conv_out, m_gain_ssm_out, m_w_out, m_norm_ffn_g, m_w_up, m_ffn_conv_w, m_ffn_conv_b, m_w_down, m_norm_final_g, v_meta_tokens, v_norm_mix_g, v_w_in, v_conv_w, v_ssm_lam_re, v_ssm_lam_im, v_ssm_log_dt, v_ssm_b_re, v_ssm_b_im, v_ssm_c_re, v_ssm_c_im, v_ssm_d, v_ssm_w_glu, v_gain_conv_out, v_gain_ssm_out, v_w_out, v_norm_ffn_g, v_w_up, v_ffn_conv_w, v_ffn_conv_b, v_w_down, v_norm_final_g):
    args = dict(locals())
    w = {n: args[n] for n in WEIGHTS}
    mom = {n: args["m_" + n] for n in WEIGHTS}
    var = {n: args["v_" + n] for n in WEIGHTS}
    kx, ky, kc_ = lax.axis_index("x"), lax.axis_index("y"), lax.axis_index("c")
    chip = 2 * kx + ky
    kc = jnp.stack([chip, kc_]).astype(jnp.int32)

    def squeeze(n, a):
        if n == "meta_tokens":
            return a
        if n == "norm_final_g":
            return a.reshape(1, -1)
        a = a[0]
        return a.reshape(1, -1) if a.ndim == 1 else a

    wl = {n: squeeze(n, w[n]) for n in WEIGHTS}
    ml = {n: squeeze(n, mom[n]) for n in WEIGHTS}
    vl = {n: squeeze(n, var[n]) for n in WEIGHTS}

    tiny = _pack([wl[n] for n in TINY_SHARDED], SUBLANES, LANES)
    ex = _Exchanges({n: wl[n] for n in BIG_NAMES}, tiny, kc)
    tiny_shapes = [wl[n].shape for n in TINY_SHARDED]
    tiny_all = ex.small_params(kc)
    tiny_parts = [_unpack(tiny_all[k], tiny_shapes) for k in range(4)]
    p = {n: wl[n] for n in WEIGHTS if n not in BIG}
    for j, n in enumerate(TINY_SHARDED):
        p[n] = jnp.concatenate([tiny_parts[k][j] for k in range(4)], axis=1)
    p["ssm_log_dt"] = wl["ssm_log_dt"].reshape(-1)

    loss_local, grad_x, grads = _local_step(x[0], loss_target[0], p, ex)

    small_names = REPLICATED + TINY_SHARDED
    small_shapes = [tuple(grads[n].shape) for n in small_names] + [(1,)]
    pack = _pack([grads[n] for n in small_names] + [loss_local.reshape(1)], 2 * 16, PACK_COLS)
    g_pack = ex.finish_pack(pack)
    g_small = dict(zip(small_names + ("loss",), _unpack(g_pack, small_shapes)))
    loss = g_small["loss"][0]
    swapped = ("ssm_b_re", "ssm_b_im")

    def view(n, a):
        if n in swapped:
            return jnp.swapaxes(a, -1, -2)
        return a.reshape(1, -1) if a.ndim == 1 else a

    g = {}
    for n in REPLICATED:
        g[n] = g_small[n].reshape(view(n, w[n]).shape)
    for n in TINY_SHARDED:
        cols = wl[n].shape[1]
        g[n] = lax.dynamic_slice_in_dim(g_small[n], chip * cols, cols, axis=1).reshape(w[n].shape)
    delta, new_m, new_v = {}, {}, {}
    small = [[view(n, d[n]) for n in small_names] for d in (w, mom, var)]
    small.insert(1, [g[n] for n in small_names])
    for d, outs in zip((delta, new_m, new_v), _adamw_whole(*small, "adamw_small")):
        d.update(zip(small_names, outs))
    for d in (g, delta, new_m, new_v):
        d.update({n: jnp.swapaxes(d[n], -1, -2) for n in swapped})
    g_big = ex.finish_big(delta[small_names[0]])
    for n in BIG_NAMES:
        g[n], delta[n], new_m[n], new_v[n] = _adamw(wl[n], g_big[n], ml[n], vl[n], "adamw_" + n)

    def like(n, a):
        return a.reshape(w[n].shape)

    return (loss, grad_x[None], *[like(n, g[n]) for n in WEIGHTS], *[like(n, delta[n]) for n in WEIGHTS],
            *[like(n, new_m[n]) for n in WEIGHTS], *[like(n, new_v[n]) for n in WEIGHTS])
```

```python
import functools
import math

import jax
import jax.numpy as jnp
from jax import lax
from jax.experimental import pallas as pl
from jax.experimental.pallas import tpu as pltpu

F32 = jnp.float32
BF16 = jnp.bfloat16
MESH = pl.DeviceIdType.MESH

N_META = 16
N_GROUPS = 32
GROUP = 16
STATE = 64
RMS_EPS = 1e-6
ADAM_LR = 0.001
ADAM_B1 = 0.9
ADAM_B2 = 0.999
ADAM_EPS = 1e-08
ADAM_WD = 0.01
ADAM_STEP = 10

LANES = 128
SUBLANES = 8
ROW_ALIGN = 128
ROW_TILES = 4
VMEM_LIMIT = 52 * 1024 * 1024
MM_VMEM_BUDGET = 40 * 1024 * 1024
GELU_C = math.sqrt(2.0 / math.pi)
GELU_A = 0.044715


def _cparams(*sem):
    return pltpu.CompilerParams(dimension_semantics=sem, vmem_limit_bytes=VMEM_LIMIT)


def _pick_tile(dim, cap, mult):
    best = None
    for t in range(mult, min(dim, cap) + 1, mult):
        if dim % t == 0:
            best = t
    return best if best is not None else dim


def _mm(a, b, mode, name, out_dtype=F32, acc_in=None, after=None):
    if mode == "tn":
        kdim, m = a.shape
    else:
        m, kdim = a.shape
    n = b.shape[0] if mode == "nt" else b.shape[1]
    tm = _pick_tile(m, 1408, LANES if mode == "tn" else 16)
    tk = _pick_tile(kdim, 2816, LANES)
    nk = kdim // tk
    out_bytes = jnp.dtype(out_dtype).itemsize
    for cap in (1408, 1024, 512, 256, LANES):
        tn = _pick_tile(n, cap, LANES)
        blocks = 2 * (tm * tk * 2 + tk * tn * 2 + tm * tn * out_bytes * (2 if acc_in is not None else 1))
        if blocks + (tm * tn * 4 if nk > 1 else 0) <= MM_VMEM_BUDGET:
            break
    has_acc = acc_in is not None

    def body(*refs):
        if after is not None:
            refs = refs[1:]
        if has_acc:
            a_ref, b_ref, c_ref, o_ref = refs[:4]
            rest = refs[4:]
        else:
            a_ref, b_ref, o_ref = refs[:3]
            c_ref = None
            rest = refs[3:]
        if mode == "nn":
            p = jnp.dot(a_ref[...], b_ref[...], preferred_element_type=F32)
        elif mode == "nt":
            p = lax.dot_general(a_ref[...], b_ref[...], (((1,), (1,)), ((), ())), preferred_element_type=F32)
        else:
            p = lax.dot_general(a_ref[...], b_ref[...], (((0,), (0,)), ((), ())), preferred_element_type=F32)
        if nk == 1:
            if has_acc:
                p = p + c_ref[...]
            o_ref[...] = p.astype(out_dtype)
        else:
            acc_ref = rest[0]
            k = pl.program_id(2)

            @pl.when(k == 0)
            def _():
                acc_ref[...] = p + c_ref[...] if has_acc else p

            @pl.when(k > 0)
            def _():
                acc_ref[...] += p

            @pl.when(k == nk - 1)
            def _():
                o_ref[...] = acc_ref[...].astype(out_dtype)

    if mode == "tn":
        a_spec = pl.BlockSpec((tk, tm), lambda i, j, k: (k, i))
    else:
        a_spec = pl.BlockSpec((tm, tk), lambda i, j, k: (i, k))
    if mode == "nt":
        b_spec = pl.BlockSpec((tn, tk), lambda i, j, k: (j, k))
    else:
        b_spec = pl.BlockSpec((tk, tn), lambda i, j, k: (k, j))
    o_spec = pl.BlockSpec((tm, tn), lambda i, j, k: (i, j))
    in_specs = [a_spec, b_spec] + ([o_spec] if has_acc else [])
    args = (a, b) + ((acc_in,) if has_acc else ())
    if after is not None:
        in_specs = [pl.BlockSpec(memory_space=pl.ANY)] + in_specs
        args = (after,) + args
    return pl.pallas_call(
        body, name=name, grid=(m // tm, n // tn, nk),
        in_specs=in_specs, out_specs=o_spec,
        out_shape=jax.ShapeDtypeStruct((m, n), out_dtype),
        scratch_shapes=[pltpu.VMEM((tm, tn), F32)] if nk > 1 else [],
        compiler_params=_cparams("parallel", "parallel", "arbitrary"),
    )(*args)


def _mm_rows(a, b, mode, name, ins, outs, epilogue, scratch=()):
    m, kdim = a.shape
    n = b.shape[0] if mode == "nt" else b.shape[1]
    tm = m // ROW_TILES
    tk = _pick_tile(kdim, 2816, LANES)
    nk = kdim // tk
    ni, no = len(ins), len(outs)

    def body(*refs):
        a_ref, b_ref = refs[:2]
        in_refs, out_refs, rest = refs[2:2 + ni], refs[2 + ni:2 + ni + no], refs[2 + ni + no:]
        i = pl.program_id(0)
        if mode == "nn":
            p = jnp.dot(a_ref[...], b_ref[...], preferred_element_type=F32)
        else:
            p = lax.dot_general(a_ref[...], b_ref[...], (((1,), (1,)), ((), ())), preferred_element_type=F32)
        if nk == 1:
            epilogue(p, i, in_refs, out_refs, rest)
        else:
            acc_ref = rest[0]
            k = pl.program_id(1)

            @pl.when(k == 0)
            def _():
                acc_ref[...] = p

            @pl.when(k > 0)
            def _():
                acc_ref[...] += p

            @pl.when(k == nk - 1)
            def _():
                epilogue(acc_ref[...], i, in_refs, out_refs, rest[1:])

    def spec(shape, kind):
        if kind == "rows":
            return pl.BlockSpec((tm,) + tuple(shape[1:]), lambda i, k: (i,) + (0,) * (len(shape) - 1))
        if kind == "whole":
            return pl.BlockSpec(tuple(shape), lambda i, k: (0,) * len(shape))
        return pl.BlockSpec(memory_space=pl.ANY)

    a_spec = pl.BlockSpec((tm, tk), lambda i, k: (i, k))
    b_spec = pl.BlockSpec((n, tk), lambda i, k: (0, k)) if mode == "nt" else pl.BlockSpec((tk, n), lambda i, k: (k, 0))
    return pl.pallas_call(
        body, name=name, grid=(ROW_TILES, nk),
        in_specs=[a_spec, b_spec] + [spec(x.shape, kind) for x, kind in ins],
        out_specs=[spec(shape, kind) for shape, _, kind in outs],
        out_shape=[jax.ShapeDtypeStruct(shape, dtype) for shape, dtype, _ in outs],
        scratch_shapes=([pltpu.VMEM((tm, n), F32)] if nk > 1 else []) + list(scratch),
        compiler_params=_cparams("arbitrary", "arbitrary"),
    )(a, b, *[x for x, _ in ins])


def _rows(shape_cols, tr, dtype=None):
    return pl.BlockSpec((tr, shape_cols), lambda i: (i, 0))


def _const(shape):
    return pl.BlockSpec(shape, lambda i: (0,) * len(shape))


def _rms(x):
    return lax.rsqrt(jnp.mean(x * x, axis=-1, keepdims=True) + RMS_EPS)


def _rms_bwd(x, r, g, dy):
    xn = x * r
    dxn = dy * g
    dx = r * (dxn - xn * jnp.mean(dxn * xn, axis=-1, keepdims=True))
    return dx, dy * xn


def _gelu(y):
    return 0.5 * y * (1.0 + jnp.tanh(GELU_C * (y + GELU_A * y * y * y)))


def _gelu_grad(y):
    t = jnp.tanh(GELU_C * (y + GELU_A * y * y * y))
    return 0.5 * (1.0 + t) + 0.5 * y * (1.0 - t * t) * GELU_C * (1.0 + 3.0 * GELU_A * y * y)


def _sigmoid(z):
    return 1.0 / (1.0 + jnp.exp(-z))


def _proj_res_norm(a, w, h, g, after, name):
    def epilogue(p, i, ins, outs, _):
        x = ins[0][...] + p
        outs[0][...] = x
        outs[1][...] = (x * _rms(x) * ins[1][...]).astype(BF16)

    return _mm_rows(a, w, "nn", name, [(h, "rows"), (g, "whole"), (after, "hbm")],
                    [(h.shape, F32, "rows"), (h.shape, BF16, "rows")], epilogue)


def _proj_norm_bwd(da, w, h, g, dres, after, name):
    d = h.shape[1]

    def epilogue(p, i, ins, outs, _):
        x = ins[0][...]
        dx, dgs = _rms_bwd(x, _rms(x), ins[1][...], p)
        dh = ins[2][...] + dx
        outs[0][...] = dh
        outs[1][...] = dh.astype(BF16)

        @pl.when(i == 0)
        def _():
            outs[2][...] = jnp.zeros_like(outs[2])

        outs[2][...] += jnp.sum(dgs, axis=0, keepdims=True)

    return _mm_rows(da, w, "nt", name, [(h, "rows"), (g, "whole"), (dres, "rows"), (after, "hbm")],
                    [(h.shape, F32, "rows"), (h.shape, BF16, "rows"), ((1, d), F32, "whole")], epilogue)


def _input_norm_bwd(h, g, dhn, dres, n_real, name):
    tp, d = h.shape
    tr = tp // ROW_TILES

    def body(h_ref, g_ref, dhn_ref, dres_ref, dx_ref, dmeta_ref, dg_ref, stage, sem):
        i = pl.program_id(0)
        x = h_ref[...]
        dx, dgs = _rms_bwd(x, _rms(x), g_ref[...], dhn_ref[...])
        stage[...] = dres_ref[...] + dx

        @pl.when(i == 0)
        def _():
            dg_ref[...] = jnp.zeros_like(dg_ref)
            dmeta_ref[...] = stage[:N_META, :]

        dg_ref[...] += jnp.sum(dgs, axis=0, keepdims=True)
        for t in range(ROW_TILES):
            lo, hi = max(t * tr, N_META), min((t + 1) * tr, n_real)
            if hi > lo:
                @pl.when(i == t)
                def _(t=t, lo=lo, hi=hi):
                    cp = pltpu.make_async_copy(stage.at[pl.ds(lo - t * tr, hi - lo), :],
                                               dx_ref.at[pl.ds(lo - N_META, hi - lo), :], sem)
                    cp.start()
                    cp.wait()

    return pl.pallas_call(
        body, name=name, grid=(ROW_TILES,),
        in_specs=[_rows(d, tr), _const((1, d)), _rows(d, tr), _rows(d, tr)],
        out_specs=[pl.BlockSpec(memory_space=pl.ANY), _const((N_META, d)), _const((1, d))],
        out_shape=[jax.ShapeDtypeStruct((n_real - N_META, d), F32), jax.ShapeDtypeStruct((N_META, d), F32),
                   jax.ShapeDtypeStruct((1, d), F32)],
        scratch_shapes=[pltpu.VMEM((tr, d), F32), pltpu.SemaphoreType.DMA],
        compiler_params=_cparams("arbitrary"))(h, g, dhn, dres)


def _load_token_rows(tok_hbm, buf, sem, tr, n_real, head=None, wait=False, i=None):
    i = pl.program_id(0) if i is None else i
    for t in range(ROW_TILES):
        base = t * tr
        lo, hi = max(base, N_META), min(base + tr, n_real)

        @pl.when(i == t)
        def _(base=base, lo=lo, hi=hi):
            if hi > lo:
                cp = pltpu.make_async_copy(tok_hbm.at[pl.ds(lo - N_META, hi - lo), :],
                                           buf.at[pl.ds(lo - base, hi - lo), :], sem)
                if wait:
                    cp.wait()
                    return
                cp.start()
            if wait:
                return
            if base < N_META:
                buf[0:N_META - base, :] = (jnp.zeros((N_META - base, buf.shape[1]), F32) if head is None
                                           else head[base:N_META, :])
            if hi < base + tr:
                buf[max(hi, base) - base:tr, :] = jnp.zeros((base + tr - max(hi, base), buf.shape[1]), F32)


def _input_norm_fwd(x, meta, g, tp, name):
    seq, d = x.shape
    tr = tp // ROW_TILES
    n_real = N_META + seq

    def body(x_hbm, meta_ref, g_ref, h_ref, hn_ref, buf, sem):
        _load_token_rows(x_hbm, buf, sem, tr, n_real, head=meta_ref)
        _load_token_rows(x_hbm, buf, sem, tr, n_real, wait=True)
        h = buf[...]
        h_ref[...] = h
        hn_ref[...] = (h * _rms(h) * g_ref[...]).astype(BF16)

    return pl.pallas_call(
        body, name=name, grid=(ROW_TILES,),
        in_specs=[pl.BlockSpec(memory_space=pl.ANY), _const((N_META, d)), _const((1, d))],
        out_specs=[_rows(d, tr), _rows(d, tr)],
        out_shape=[jax.ShapeDtypeStruct((tp, d), F32), jax.ShapeDtypeStruct((tp, d), BF16)],
        scratch_shapes=[pltpu.VMEM((tr, d), F32), pltpu.SemaphoreType.DMA],
        compiler_params=_cparams("arbitrary"))(x, meta, g)


def _proj_loss_bwd(act, w, h1, target, g, n_real, name):
    tp, d = h1.shape
    tr = tp // ROW_TILES

    def epilogue(p, i, ins, outs, scratch):
        h1_ref, t_hbm, g_ref = ins
        loss_ref, dh_ref, dhb_ref, dg_ref = outs
        t_buf, sem = scratch
        _load_token_rows(t_hbm, t_buf, sem, tr, n_real, i=i)
        x = h1_ref[...] + p
        r = _rms(x)
        row = i * tr + lax.broadcasted_iota(jnp.int32, (tr, d), 0)
        valid = (row >= N_META) & (row < n_real)
        _load_token_rows(t_hbm, t_buf, sem, tr, n_real, wait=True, i=i)
        e = jnp.where(valid, x * r * g_ref[...] - t_buf[...], 0.0)
        dx, dgs = _rms_bwd(x, r, g_ref[...], e * (1.0 / d))
        dh_ref[...] = dx
        dhb_ref[...] = dx.astype(BF16)

        @pl.when(i == 0)
        def _():
            dg_ref[...] = jnp.zeros_like(dg_ref)
            loss_ref[...] = jnp.zeros_like(loss_ref)

        dg_ref[...] += jnp.sum(dgs, axis=0, keepdims=True)
        loss_ref[...] += (0.5 / d) * jnp.sum(jnp.sum(e * e, axis=0, keepdims=True), axis=1, keepdims=True)

    return _mm_rows(act, w, "nn", name, [(h1, "rows"), (target, "hbm"), (g, "whole")],
                    [((1, LANES), F32, "whole"), ((tp, d), F32, "rows"), ((tp, d), BF16, "rows"),
                     ((1, d), F32, "whole")],
                    epilogue, scratch=[pltpu.VMEM((tr, d), F32), pltpu.SemaphoreType.DMA])


def _mix_fwd(co, y, z, gc, gs, name):
    tp, dh = co.shape
    tr = tp // ROW_TILES

    def body(co_ref, y_ref, z_ref, gc_ref, gs_ref, m_ref):
        c = co_ref[...]
        m_ref[:, :dh] = (c * _rms(c) * gc_ref[...]).astype(BF16)
        so = _gelu(y_ref[...]) * _sigmoid(z_ref[...])
        m_ref[:, dh:] = (so * _rms(so) * gs_ref[...]).astype(BF16)

    return pl.pallas_call(
        body, name=name, grid=(ROW_TILES,),
        in_specs=[_rows(dh, tr)] * 3 + [_const((1, dh))] * 2,
        out_specs=_rows(2 * dh, tr),
        out_shape=jax.ShapeDtypeStruct((tp, 2 * dh), BF16),
        compiler_params=_cparams("parallel"))(co, y, z, gc, gs)


def _proj_mix_bwd(dh1b, w, co, y, z, gc, gs, name):
    tp, dh = co.shape

    def epilogue(p, i, ins, outs, _):
        co_ref, y_ref, z_ref, gc_ref, gs_ref = ins
        dco_ref, dz_ref, dgp_ref, dgc_ref, dgs_ref = outs
        c = co_ref[...]
        dco, dgc = _rms_bwd(c, _rms(c), gc_ref[...], p[:, :dh])
        dco_ref[...] = dco
        gl = _gelu(y_ref[...])
        sg = _sigmoid(z_ref[...])
        so = gl * sg
        dso, dgs = _rms_bwd(so, _rms(so), gs_ref[...], p[:, dh:])
        dz_ref[...] = (dso * gl * sg * (1.0 - sg)).astype(BF16)
        dgp_ref[...] = dso * sg

        @pl.when(i == 0)
        def _():
            dgc_ref[...] = jnp.zeros_like(dgc_ref)
            dgs_ref[...] = jnp.zeros_like(dgs_ref)

        dgc_ref[...] += jnp.sum(dgc, axis=0, keepdims=True)
        dgs_ref[...] += jnp.sum(dgs, axis=0, keepdims=True)

    return _mm_rows(dh1b, w, "nt", name,
                    [(co, "rows"), (y, "rows"), (z, "rows"), (gc, "whole"), (gs, "whole")],
                    [((tp, dh), F32, "rows"), ((tp, dh), BF16, "rows"), ((tp, dh), F32, "rows"),
                     ((1, dh), F32, "whole"), ((1, dh), F32, "whole")], epilogue)


def _shift_down(x, k):
    row = lax.broadcasted_iota(jnp.int32, x.shape, 0)
    return jnp.where(row >= k, pltpu.roll(x, k, 0), 0.0)


def _shift_up(x, k):
    n = x.shape[0]
    row = lax.broadcasted_iota(jnp.int32, x.shape, 0)
    return jnp.where(row < n - k, pltpu.roll(x, n - k, 0), 0.0)


def _dwconv(x, w_ref):
    return w_ref[2:3, :] * x + w_ref[1:2, :] * _shift_down(x, 1) + w_ref[0:1, :] * _shift_down(x, 2)


def _dwconv_bwd(x, dy, w_ref):
    dx = w_ref[2:3, :] * dy + w_ref[1:2, :] * _shift_up(dy, 1) + w_ref[0:1, :] * _shift_up(dy, 2)
    dw = jnp.concatenate([jnp.sum(dy * _shift_down(x, 2), axis=0, keepdims=True),
                          jnp.sum(dy * _shift_down(x, 1), axis=0, keepdims=True),
                          jnp.sum(dy * x, axis=0, keepdims=True)], axis=0)
    return dx, dw


def _interleave(dst, src):
    seg_rows = src.shape[0] // SUBLANES
    for seg in range(SUBLANES):
        dst[pl.ds(seg, seg_rows, stride=SUBLANES), :] = src[seg * seg_rows:(seg + 1) * seg_rows, :]


def _deinterleave(dst, src):
    seg_rows = src.shape[0] // SUBLANES
    for seg in range(SUBLANES):
        dst[seg * seg_rows:(seg + 1) * seg_rows, :] = src[pl.ds(seg, seg_rows, stride=SUBLANES), :]


def _segment_shift(x, reverse):
    row = lax.broadcasted_iota(jnp.int32, x.shape, 0)
    if reverse:
        return jnp.where(row < SUBLANES - 1, pltpu.roll(x, SUBLANES - 1, 0), 0.0)
    return jnp.where(row >= 1, pltpu.roll(x, 1, 0), 0.0)


def _scan(s_re, s_im, pw_ref, reverse):
    n_steps = s_re.shape[0] // SUBLANES
    n_strips = s_re.shape[1] // LANES
    sign = -1.0 if reverse else 1.0
    strips = [slice(st * LANES, (st + 1) * LANES) for st in range(n_strips)]

    def rows_of(j):
        step = (n_steps - 1 - j) if reverse else j
        return pl.ds(pl.multiple_of(step * SUBLANES, SUBLANES), SUBLANES)

    a = [(jnp.broadcast_to(pw_ref[0, 0:1, lanes], (SUBLANES, LANES)),
          sign * jnp.broadcast_to(pw_ref[1, 0:1, lanes], (SUBLANES, LANES))) for lanes in strips]

    def local(i, carry):
        for half in range(2):
            rows = rows_of(2 * i + half)
            out = []
            for st, lanes in enumerate(strips):
                (ar, ai), cr, ci = a[st], carry[2 * st], carry[2 * st + 1]
                xr = s_re[rows, lanes] + (ar * cr - ai * ci)
                xi = s_im[rows, lanes] + (ar * ci + ai * cr)
                s_re[rows, lanes] = xr
                s_im[rows, lanes] = xi
                out += [xr, xi]
            carry = tuple(out)
        return carry

    zero = jnp.zeros((SUBLANES, LANES), F32)
    ends = lax.fori_loop(0, n_steps // 2, local, (zero,) * (2 * n_strips))

    entering = []
    row = lax.broadcasted_iota(jnp.int32, (SUBLANES, LANES), 0)
    for st, lanes in enumerate(strips):
        tr, ti = ends[2 * st], ends[2 * st + 1]
        mr = jnp.broadcast_to(pw_ref[0, n_steps - 1:n_steps, lanes], (SUBLANES, LANES))
        mi = sign * jnp.broadcast_to(pw_ref[1, n_steps - 1:n_steps, lanes], (SUBLANES, LANES))
        for k in (1, 2, 4):
            keep = (row < SUBLANES - k) if reverse else (row >= k)
            rr = jnp.where(keep, pltpu.roll(tr, SUBLANES - k if reverse else k, 0), 0.0)
            ri = jnp.where(keep, pltpu.roll(ti, SUBLANES - k if reverse else k, 0), 0.0)
            tr, ti = tr + (mr * rr - mi * ri), ti + (mr * ri + mi * rr)
            mr, mi = mr * mr - mi * mi, 2.0 * mr * mi
        entering += [_segment_shift(tr, reverse), _segment_shift(ti, reverse)]

    def fix(i, carry):
        for half in range(2):
            rows = rows_of(2 * i + half)
            out = []
            for st, lanes in enumerate(strips):
                (ar, ai), cr, ci = a[st], carry[2 * st], carry[2 * st + 1]
                cr, ci = ar * cr - ai * ci, ar * ci + ai * cr
                s_re[rows, lanes] = s_re[rows, lanes] + cr
                s_im[rows, lanes] = s_im[rows, lanes] + ci
                out += [cr, ci]
            carry = tuple(out)
        return carry

    lax.fori_loop(0, n_steps // 2, fix, tuple(entering))


def _seq_fwd(proj, conv_w, bc_re, bc_im, cc_re, cc_im, dskip, a_pow, name):
    tp = proj.shape[0]
    dh = proj.shape[1] // 4
    nq = dh // LANES
    sw = STATE * N_GROUPS // nq

    def body(b_ref, c_ref, v_ref, u_ref, w_ref, bre_ref, bim_ref, cre_ref, cim_ref, d_ref, pw_ref, proj_hbm,
             co_ref, y_ref, g_ref, s_re, s_im, u_il, y_il, sems):
        seg_rows = tp // SUBLANES
        col0 = (3 * nq + pl.program_id(0)) * LANES
        fetch = [pltpu.make_async_copy(proj_hbm.at[pl.ds(seg * seg_rows, seg_rows), pl.ds(col0, LANES)],
                                       u_il.at[:, seg, :], sems.at[seg])
                 for seg in range(SUBLANES)]
        for cp in fetch:
            cp.start()
        co_ref[...] = b_ref[...] * _dwconv(c_ref[...] * v_ref[...], w_ref)
        for cp in fetch:
            cp.wait()
        ub = u_il[...].reshape(tp, LANES).astype(BF16)
        s_re[...] = jnp.dot(ub, bre_ref[...], preferred_element_type=F32)
        s_im[...] = jnp.dot(ub, bim_ref[...], preferred_element_type=F32)
        _scan(s_re, s_im, pw_ref, False)
        y_il[...] = (jnp.dot(s_re[...].astype(BF16), cre_ref[...], preferred_element_type=F32)
                     - jnp.dot(s_im[...].astype(BF16), cim_ref[...], preferred_element_type=F32))
        _deinterleave(y_ref, y_il)
        y = y_ref[...] + d_ref[...] * u_ref[...]
        y_ref[...] = y
        g_ref[...] = _gelu(y).astype(BF16)

    col = lambda off: pl.BlockSpec((tp, LANES), lambda q, off=off: (0, off * nq + q))
    blk = pl.BlockSpec((tp, LANES), lambda q: (0, q))
    return pl.pallas_call(
        body, name=name, grid=(nq,),
        in_specs=[col(0), col(1), col(2), col(3),
                  pl.BlockSpec((3, LANES), lambda q: (0, q)),
                  pl.BlockSpec((LANES, sw), lambda q: (0, q)), pl.BlockSpec((LANES, sw), lambda q: (0, q)),
                  pl.BlockSpec((sw, LANES), lambda q: (q, 0)), pl.BlockSpec((sw, LANES), lambda q: (q, 0)),
                  pl.BlockSpec((1, LANES), lambda q: (0, q)),
                  pl.BlockSpec((2, tp // SUBLANES, sw), lambda q: (0, 0, q)),
                  pl.BlockSpec(memory_space=pl.ANY)],
        out_specs=[blk, blk, blk],
        out_shape=[jax.ShapeDtypeStruct((tp, dh), F32), jax.ShapeDtypeStruct((tp, dh), F32),
                   jax.ShapeDtypeStruct((tp, dh), BF16)],
        scratch_shapes=[pltpu.VMEM((tp, sw), F32), pltpu.VMEM((tp, sw), F32),
                        pltpu.VMEM((tp // SUBLANES, SUBLANES, LANES), F32), pltpu.VMEM((tp, LANES), F32),
                        pltpu.SemaphoreType.DMA((SUBLANES,))],
        compiler_params=_cparams("parallel"),
    )(proj, proj, proj, proj, conv_w, bc_re, bc_im, cc_re, cc_im, dskip, a_pow, proj)


def _conv_bwd(proj, dco, conv_w, name):
    tp = proj.shape[0]
    dh = proj.shape[1] // 4
    nq = dh // LANES

    def body(b_ref, c_ref, v_ref, dco_ref, w_ref, dproj_ref, dw_ref, stage, sem):
        q = pl.program_id(0)
        cg = c_ref[...]
        vg = v_ref[...]
        cv = cg * vg
        dco_v = dco_ref[...]
        dcv, dw = _dwconv_bwd(cv, dco_v * b_ref[...], w_ref)
        dw_ref[...] = dw
        stage[0] = (dco_v * _dwconv(cv, w_ref)).astype(BF16)
        stage[1] = (dcv * vg).astype(BF16)
        stage[2] = (dcv * cg).astype(BF16)
        copies = [pltpu.make_async_copy(stage.at[p], dproj_ref.at[:, pl.ds((p * nq + q) * LANES, LANES)], sem.at[p])
                  for p in range(3)]
        for cp in copies:
            cp.start()
        for cp in copies:
            cp.wait()

    col = lambda off: pl.BlockSpec((tp, LANES), lambda q, off=off: (0, off * nq + q))
    return pl.pallas_call(
        body, name=name, grid=(nq,),
        in_specs=[col(0), col(1), col(2), pl.BlockSpec((tp, LANES), lambda q: (0, q)),
                  pl.BlockSpec((3, LANES), lambda q: (0, q))],
        out_specs=[pl.BlockSpec(memory_space=pl.ANY), pl.BlockSpec((3, LANES), lambda q: (0, q))],
        out_shape=[jax.ShapeDtypeStruct((tp, 4 * dh), BF16), jax.ShapeDtypeStruct((3, dh), F32)],
        scratch_shapes=[pltpu.VMEM((3, tp, LANES), BF16), pltpu.SemaphoreType.DMA((3,))],
        compiler_params=_cparams("arbitrary"),
    )(proj, proj, proj, dco, conv_w)


def _ssm_bwd(proj, y, dg, dproj, bc_re, bc_im, cc_re, cc_im, dskip, a_pow, name):
    tp = proj.shape[0]
    dh = proj.shape[1] // 4
    nq = dh // LANES
    sw = STATE * N_GROUPS // nq
    seg_rows = tp // SUBLANES

    def body(proj_hbm, y_hbm, dg_hbm, dproj_in, bre_ref, bim_ref, cre_ref, cim_ref, d_ref, pw_ref,
             dproj_ref, dbre_ref, dbim_ref, dcre_ref, dcim_ref, dd_ref, dar_ref, dai_ref,
             s_re, s_im, l_re, l_im, u_il, y_il, g_il, a_il, b_il, stage, sem, sems):
        del dproj_in
        q = pl.program_id(0)
        nt = (((1,), (1,)), ((), ()))
        tn = (((0,), (0,)), ((), ()))
        fetch = []
        for n, (src, col0, dst) in enumerate(((proj_hbm, (3 * nq + q) * LANES, u_il), (y_hbm, q * LANES, y_il),
                                              (dg_hbm, q * LANES, g_il))):
            fetch.append([pltpu.make_async_copy(src.at[pl.ds(seg * seg_rows, seg_rows), pl.ds(col0, LANES)],
                                                dst.at[:, seg, :], sems.at[n * SUBLANES + seg])
                          for seg in range(SUBLANES)])
        for cp in fetch[0] + fetch[1] + fetch[2]:
            cp.start()
        for cp in fetch[0]:
            cp.wait()
        u = u_il[...].reshape(tp, LANES)
        ub = u.astype(BF16)
        s_re[...] = jnp.dot(ub, bre_ref[...], preferred_element_type=F32)
        s_im[...] = jnp.dot(ub, bim_ref[...], preferred_element_type=F32)
        _scan(s_re, s_im, pw_ref, False)
        for cp in fetch[1] + fetch[2]:
            cp.wait()
        dy = g_il[...].reshape(tp, LANES) * _gelu_grad(y_il[...].reshape(tp, LANES))
        dd_ref[...] = jnp.sum(dy * u, axis=0, keepdims=True)
        dyb = dy.astype(BF16)
        l_re[...] = lax.dot_general(dyb, cre_ref[...], nt, preferred_element_type=F32)
        l_im[...] = -lax.dot_general(dyb, cim_ref[...], nt, preferred_element_type=F32)
        dcre_ref[...] = lax.dot_general(s_re[...].astype(BF16), dyb, tn, preferred_element_type=F32)
        dcim_ref[...] = -lax.dot_general(s_im[...].astype(BF16), dyb, tn, preferred_element_type=F32)
        _scan(l_re, l_im, pw_ref, True)
        rest = tp - SUBLANES
        for st in range(sw // LANES):
            lanes = slice(st * LANES, (st + 1) * LANES)
            lr, li = l_re[SUBLANES:, lanes], l_im[SUBLANES:, lanes]
            pr, pi = s_re[:rest, lanes], s_im[:rest, lanes]
            lr0, li0 = l_re[:SUBLANES, lanes], l_im[:SUBLANES, lanes]
            pr0, pi0 = _segment_shift(s_re[rest:, lanes], False), _segment_shift(s_im[rest:, lanes], False)
            dar_ref[:, lanes] = (jnp.sum(lr * pr + li * pi, axis=0, keepdims=True)
                                 + jnp.sum(lr0 * pr0 + li0 * pi0, axis=0, keepdims=True))
            dai_ref[:, lanes] = (jnp.sum(li * pr - lr * pi, axis=0, keepdims=True)
                                 + jnp.sum(li0 * pr0 - lr0 * pi0, axis=0, keepdims=True))
        lrb = l_re[...].astype(BF16)
        lib = l_im[...].astype(BF16)
        a_il[...] = (dy * d_ref[...] + lax.dot_general(lrb, bre_ref[...], nt, preferred_element_type=F32)
                     + lax.dot_general(lib, bim_ref[...], nt, preferred_element_type=F32))
        _deinterleave(b_il, a_il)
        stage[...] = b_il[...].astype(BF16)
        dbre_ref[...] = lax.dot_general(ub, lrb, tn, preferred_element_type=F32)
        dbim_ref[...] = lax.dot_general(ub, lib, tn, preferred_element_type=F32)
        cp = pltpu.make_async_copy(stage, dproj_ref.at[:, pl.ds((3 * nq + q) * LANES, LANES)], sem)
        cp.start()
        cp.wait()

    blk = pl.BlockSpec((tp, LANES), lambda q: (0, q))
    bspec = pl.BlockSpec((LANES, sw), lambda q: (0, q))
    cspec = pl.BlockSpec((sw, LANES), lambda q: (q, 0))
    tspec = pl.BlockSpec((2, tp // SUBLANES, sw), lambda q: (0, 0, q))
    nstate = STATE * N_GROUPS
    return pl.pallas_call(
        body, name=name, grid=(nq,),
        in_specs=[pl.BlockSpec(memory_space=pl.ANY)] * 4
        + [bspec, bspec, cspec, cspec, pl.BlockSpec((1, LANES), lambda q: (0, q)), tspec],
        out_specs=[pl.BlockSpec(memory_space=pl.ANY), bspec, bspec, cspec, cspec,
                   pl.BlockSpec((1, LANES), lambda q: (0, q)),
                   pl.BlockSpec((1, sw), lambda q: (0, q)), pl.BlockSpec((1, sw), lambda q: (0, q))],
        out_shape=[jax.ShapeDtypeStruct((tp, 4 * dh), BF16),
                   jax.ShapeDtypeStruct((LANES, nstate), F32), jax.ShapeDtypeStruct((LANES, nstate), F32),
                   jax.ShapeDtypeStruct((nstate, LANES), F32), jax.ShapeDtypeStruct((nstate, LANES), F32),
                   jax.ShapeDtypeStruct((1, dh), F32),
                   jax.ShapeDtypeStruct((1, nstate), F32), jax.ShapeDtypeStruct((1, nstate), F32)],
        input_output_aliases={3: 0},
        scratch_shapes=[pltpu.VMEM((tp, sw), F32)] * 4 + [pltpu.VMEM((seg_rows, SUBLANES, LANES), F32)] * 3
        + [pltpu.VMEM((tp, LANES), F32)] * 2
        + [pltpu.VMEM((tp, LANES), BF16), pltpu.SemaphoreType.DMA, pltpu.SemaphoreType.DMA((3 * SUBLANES,))],
        compiler_params=_cparams("arbitrary"),
    )(proj, y, dg, dproj, bc_re, bc_im, cc_re, cc_im, dskip, a_pow)


FFN_TILE = 256


def _ffn_act(up, fw, fb, name):
    tp, two_ff = up.shape
    dff = two_ff // 2
    tc = FFN_TILE
    nj = dff // tc

    def body(ua_ref, uv_ref, wa_ref, wv_ref, ba_ref, bv_ref, act_ref):
        a = _dwconv(ua_ref[...], wa_ref) + ba_ref[...]
        v = _dwconv(uv_ref[...], wv_ref) + bv_ref[...]
        act_ref[...] = (a * _sigmoid(a) * v).astype(BF16)

    lo = lambda r: pl.BlockSpec((r, tc), lambda j: (0, j))
    hi = lambda r: pl.BlockSpec((r, tc), lambda j: (0, nj + j))
    return pl.pallas_call(
        body, name=name, grid=(nj,),
        in_specs=[lo(tp), hi(tp), lo(3), hi(3), lo(1), hi(1)],
        out_specs=lo(tp),
        out_shape=jax.ShapeDtypeStruct((tp, dff), BF16),
        compiler_params=_cparams("parallel"))(up, up, fw, fw, fb, fb)


def _ffn_bwd(up, dact, fw, fb, name):
    tp, two_ff = up.shape
    dff = two_ff // 2
    tc = FFN_TILE
    nj = dff // tc

    def body(ua_ref, uv_ref, da_ref, wa_ref, wv_ref, ba_ref, bv_ref,
             dup_ref, dwa_ref, dwv_ref, dba_ref, dbv_ref, stage, sem):
        j = pl.program_id(0)
        ua = ua_ref[...]
        uv = uv_ref[...]
        a = _dwconv(ua, wa_ref) + ba_ref[...]
        v = _dwconv(uv, wv_ref) + bv_ref[...]
        sg = _sigmoid(a)
        dact_v = da_ref[...]
        da = dact_v * v * sg * (1.0 + a * (1.0 - sg))
        dv = dact_v * a * sg
        dba_ref[...] = jnp.sum(da, axis=0, keepdims=True)
        dbv_ref[...] = jnp.sum(dv, axis=0, keepdims=True)
        dua, dwa = _dwconv_bwd(ua, da, wa_ref)
        duv, dwv = _dwconv_bwd(uv, dv, wv_ref)
        dwa_ref[...] = dwa
        dwv_ref[...] = dwv
        stage[0] = dua.astype(BF16)
        stage[1] = duv.astype(BF16)
        copies = [pltpu.make_async_copy(stage.at[p], dup_ref.at[:, pl.ds((p * nj + j) * tc, tc)], sem.at[p])
                  for p in range(2)]
        for cp in copies:
            cp.start()
        for cp in copies:
            cp.wait()

    lo = lambda r: pl.BlockSpec((r, tc), lambda j: (0, j))
    hi = lambda r: pl.BlockSpec((r, tc), lambda j: (0, nj + j))
    return pl.pallas_call(
        body, name=name, grid=(nj,),
        in_specs=[lo(tp), hi(tp), lo(tp), lo(3), hi(3), lo(1), hi(1)],
        out_specs=[pl.BlockSpec(memory_space=pl.ANY), lo(3), lo(3), lo(1), lo(1)],
        out_shape=[jax.ShapeDtypeStruct((tp, two_ff), BF16),
                   jax.ShapeDtypeStruct((3, dff), F32), jax.ShapeDtypeStruct((3, dff), F32),
                   jax.ShapeDtypeStruct((1, dff), F32), jax.ShapeDtypeStruct((1, dff), F32)],
        scratch_shapes=[pltpu.VMEM((2, tp, tc), BF16), pltpu.SemaphoreType.DMA((2,))],
        compiler_params=_cparams("arbitrary"))(up, up, dact, fw, fw, fb, fb)


def _zoh(lr, li, ld):
    dt = jnp.exp(ld)
    mag = jnp.exp(lr * dt)
    ang = li * dt
    ar = mag * jnp.cos(ang)
    ai = mag * jnp.sin(ang)
    den = lr * lr + li * li
    nr = ar - 1.0
    fr = (nr * lr + ai * li) / den
    fi = (ai * lr - nr * li) / den
    return dt, ar, ai, den, nr, fr, fi


def _s5_prep(lr, li, ld, b_re, b_im, n_pow, name):
    nstate = lr.shape[1]

    def body(lr_ref, li_ref, ld_ref, bre_ref, bim_ref, pw_ref, bcre_ref, bcim_ref):
        _, ar, ai, _, _, fr, fi = _zoh(lr_ref[...], li_ref[...], ld_ref[...])
        bre = bre_ref[...]
        bim = bim_ref[...]
        bcre_ref[...] = (fr * bre - fi * bim).astype(BF16)
        bcim_ref[...] = (fr * bim + fi * bre).astype(BF16)
        row = lax.broadcasted_iota(jnp.int32, (SUBLANES, nstate), 0)
        pr, pi = jnp.zeros((SUBLANES, nstate), F32), jnp.zeros((SUBLANES, nstate), F32)
        cr, ci = ar, ai
        for t in range(SUBLANES):
            pr, pi = jnp.where(row == t, cr, pr), jnp.where(row == t, ci, pi)
            cr, ci = cr * ar - ci * ai, cr * ai + ci * ar
        pw_ref[0, 0:SUBLANES, :] = pr
        pw_ref[1, 0:SUBLANES, :] = pi
        n = SUBLANES
        while n < n_pow:
            m = min(n, n_pow - n)
            tr, ti = pw_ref[0, n - 1:n, :], pw_ref[1, n - 1:n, :]
            xr, xi = pw_ref[0, 0:m, :], pw_ref[1, 0:m, :]
            pw_ref[0, n:n + m, :] = xr * tr - xi * ti
            pw_ref[1, n:n + m, :] = xr * ti + xi * tr
            n += m

    vmem = pl.BlockSpec(memory_space=pltpu.VMEM)
    return pl.pallas_call(
        body, name=name, in_specs=[vmem] * 5, out_specs=[vmem] * 3,
        out_shape=[jax.ShapeDtypeStruct((2, n_pow, nstate), F32)] + [jax.ShapeDtypeStruct(b_re.shape, BF16)] * 2,
        compiler_params=pltpu.CompilerParams(vmem_limit_bytes=VMEM_LIMIT))(lr, li, ld, b_re, b_im)


def _s5_prep_bwd(lr, li, ld, b_re, b_im, da_re, da_im, dbc_re, dbc_im, name):
    def body(lr_ref, li_ref, ld_ref, bre_ref, bim_ref, dar_ref, dai_ref, dbcre_ref, dbcim_ref,
             dlr_ref, dli_ref, dld_ref, dbre_ref, dbim_ref):
        lr, li = lr_ref[...], li_ref[...]
        dt, ar, ai, den, nr, fr, fi = _zoh(lr, li, ld_ref[...])
        bre, bim = bre_ref[...], bim_ref[...]
        gre, gim = dbcre_ref[...], dbcim_ref[...]
        dbre_ref[...] = fr * gre + fi * gim
        dbim_ref[...] = fr * gim - fi * gre
        g_fr = jnp.sum(gre * bre + gim * bim, axis=0, keepdims=True)
        g_fi = jnp.sum(gim * bre - gre * bim, axis=0, keepdims=True)
        g_ar = dar_ref[...] + (g_fr * lr - g_fi * li) / den
        g_ai = dai_ref[...] + (g_fr * li + g_fi * lr) / den
        d_lr = (g_fr * (nr - 2.0 * fr * lr) + g_fi * (ai - 2.0 * fi * lr)) / den
        d_li = (g_fr * (ai - 2.0 * fr * li) - g_fi * (nr + 2.0 * fi * li)) / den
        g_logmag = g_ar * ar + g_ai * ai
        g_ang = g_ai * ar - g_ar * ai
        dlr_ref[...] = d_lr + g_logmag * dt
        dli_ref[...] = d_li + g_ang * dt
        d_ld = (g_logmag * lr + g_ang * li) * dt
        n = d_ld.shape[1]
        sh = 1
        while sh < STATE:
            d_ld = d_ld + pltpu.roll(d_ld, n - sh, 1)
            sh *= 2
        dld_ref[...] = d_ld

    vmem = pl.BlockSpec(memory_space=pltpu.VMEM)
    row = jax.ShapeDtypeStruct(lr.shape, F32)
    return pl.pallas_call(
        body, name=name, in_specs=[vmem] * 9, out_specs=[vmem] * 5,
        out_shape=[row, row, row, jax.ShapeDtypeStruct(b_re.shape, F32), jax.ShapeDtypeStruct(b_re.shape, F32)],
    )(lr, li, ld, b_re, b_im, da_re, da_im, dbc_re, dbc_im)


def _compact_b(bb):
    bq = bb.reshape(N_GROUPS // 8, 8, STATE, GROUP)
    m = jnp.einsum("ab,qbph->qahbp", jnp.eye(8, dtype=bb.dtype), bq).reshape(N_GROUPS // 8, LANES, 8 * STATE)
    return m.transpose(1, 0, 2).reshape(LANES, N_GROUPS * STATE)


def _expand_b(m):
    d = m.reshape(8, GROUP, N_GROUPS // 8, 8, STATE)
    return jnp.einsum("ahqap->qahp", d).reshape(N_GROUPS, GROUP, STATE)


def _compact_c(c):
    cq = c.reshape(N_GROUPS // 8, 8, GROUP, STATE)
    return jnp.einsum("ab,qbhp->qbpah", jnp.eye(8, dtype=c.dtype), cq).reshape(N_GROUPS * STATE, LANES)


def _expand_c(m):
    d = m.reshape(N_GROUPS // 8, 8, STATE, 8, GROUP)
    return jnp.einsum("qbpbh->qbhp", d).reshape(N_GROUPS, GROUP, STATE)


def _local_step(x, target, p, ex):
    seq, d = x.shape
    n_real = N_META + seq
    tp = -(-n_real // ROW_ALIGN) * ROW_ALIGN

    h0, hn1 = _input_norm_fwd(x, p["meta_tokens"], p["norm_mix_g"] + ex.zero, tp, "norm_mix")
    ex.forward("first", hn1)
    nstate = N_GROUPS * STATE
    s5 = (p["ssm_lam_re"].reshape(1, nstate), p["ssm_lam_im"].reshape(1, nstate),
          jnp.repeat(p["ssm_log_dt"].reshape(-1), STATE).reshape(1, nstate),
          _compact_b(p["ssm_b_re"]), _compact_b(p["ssm_b_im"]))
    a_pow, bc_re, bc_im = _s5_prep(*s5, tp // SUBLANES, "s5_prep")
    cc_re = _compact_c(p["ssm_c_re"]).astype(BF16)
    cc_im = _compact_c(p["ssm_c_im"]).astype(BF16)
    dskip = p["ssm_d"].reshape(1, -1)
    first = ex.weights("first", bc_re)
    proj = _mm(hn1, first["w_in"], "nn", "proj")
    started = ex.forward("mid", proj)
    co, y, g = _seq_fwd(proj, p["conv_w"] + started[0, 0], bc_re, bc_im, cc_re, cc_im, dskip, a_pow, "seq_fwd")
    mid = ex.weights("mid", g)
    z = _mm(g, mid["ssm_w_glu"], "nn", "glu")
    mixed = _mix_fwd(co, y, z, p["gain_conv_out"], p["gain_ssm_out"], "mix_fwd")
    started = ex.forward("up", mixed)
    h1, hn2 = _proj_res_norm(mixed, mid["w_out"], h0, p["norm_ffn_g"], started, "out_proj_norm")
    late = ex.weights("up", hn2)
    up = _mm(hn2, late["w_up"], "nn", "up_proj")
    started = ex.forward("down", up)
    act = _ffn_act(up, p["ffn_conv_w"] + started[0, 0], p["ffn_conv_b"], "ffn_act")
    late.update(ex.weights("down", act))
    loss, dh2, dh2b, d_gfin = _proj_loss_bwd(act, late["w_down"], h1, target, p["norm_final_g"], n_real,
                                             "down_proj_loss")

    g_w_down = _mm(act, dh2b, "tn", "g_w_down")
    dact = _mm(dh2b, late["w_down"], "nt", "d_act")
    dup, dfw_a, dfw_v, dfb_a, dfb_v = _ffn_bwd(up, dact, p["ffn_conv_w"], p["ffn_conv_b"], "ffn_bwd")
    g_w_up = _mm(hn2, dup, "tn", "g_w_up")
    started = ex.grads_ready("late", {"w_up": g_w_up, "w_down": g_w_down})
    dh1, dh1b, d_gffn = _proj_norm_bwd(dup, late["w_up"], h1, p["norm_ffn_g"], dh2, started, "d_hn2_norm_bwd")
    started = ex.grads_send("late", dh1)
    g_w_out = _mm(mixed, dh1b, "tn", "g_w_out", after=started)
    dco, dz, dgp, d_gc, d_gs = _proj_mix_bwd(dh1b, mid["w_out"], co, y, z, p["gain_conv_out"],
                                             p["gain_ssm_out"], "d_mixed_mix_bwd")
    g_w_glu = _mm(g, dz, "tn", "g_w_glu")
    started = ex.grads_ready("mid", {"ssm_w_glu": g_w_glu, "w_out": g_w_out})
    dg = _mm(dz, mid["ssm_w_glu"], "nt", "d_gelu", acc_in=dgp, after=started)
    started = ex.grads_send("mid", dg)
    dproj, d_conv_w = _conv_bwd(proj, dco, p["conv_w"] + started[0, 0], "conv_bwd")
    (dproj, dbc_re, dbc_im, dcc_re, dcc_im, d_dskip, da_re, da_im) = _ssm_bwd(
        proj, y, dg, dproj, bc_re, bc_im, cc_re, cc_im, dskip, a_pow, "ssm_bwd")
    g_w_in = _mm(hn1, dproj, "tn", "g_w_in")
    started = ex.grads_ready("first", {"w_in": g_w_in})
    dhn1 = _mm(dproj, first["w_in"], "nt", "d_hn1", after=started)
    started = ex.grads_send("first", dhn1)
    grad_x, d_meta, d_gmix = _input_norm_bwd(h0, p["norm_mix_g"] + started[0, 0], dhn1, dh1, n_real, "norm_mix_bwd")

    d_lam_re, d_lam_im, d_log_dt, d_b_re, d_b_im = _s5_prep_bwd(*s5, da_re, da_im, dbc_re, dbc_im, "s5_prep_bwd")
    d_lam_re, d_lam_im = d_lam_re.reshape(N_GROUPS, STATE), d_lam_im.reshape(N_GROUPS, STATE)
    d_log_dt = d_log_dt[0, ::STATE]
    d_b_re, d_b_im = _expand_b(d_b_re), _expand_b(d_b_im)
    grads = {
        "meta_tokens": d_meta, "norm_mix_g": d_gmix, "w_in": g_w_in, "conv_w": d_conv_w,
        "ssm_lam_re": d_lam_re, "ssm_lam_im": d_lam_im, "ssm_log_dt": d_log_dt,
        "ssm_b_re": d_b_re, "ssm_b_im": d_b_im, "ssm_c_re": _expand_c(dcc_re), "ssm_c_im": _expand_c(dcc_im),
        "ssm_d": d_dskip.reshape(N_GROUPS, GROUP), "ssm_w_glu": g_w_glu,
        "gain_conv_out": d_gc, "gain_ssm_out": d_gs, "w_out": g_w_out, "norm_ffn_g": d_gffn,
        "w_up": g_w_up, "ffn_conv_w": jnp.concatenate([dfw_a, dfw_v], axis=1),
        "ffn_conv_b": jnp.concatenate([dfb_a, dfb_v], axis=1), "w_down": g_w_down, "norm_final_g": d_gfin,
    }
    return loss[0, 0], grad_x, grads


def _view(ref, axis, start, size):
    idx = [slice(None)] * len(ref.shape)
    idx[axis] = pl.ds(start, size)
    return ref.at[tuple(idx)]


def _exchange(name, ins, outs, aliases, local_copies, remote_copies):
    ni, no = len(ins), len(outs)
    nl, nr = len(local_copies), len(remote_copies)

    def body(*refs):
        in_refs, out_refs = refs[:ni], refs[ni:ni + no]
        send_sems, recv_sems, local_sems = refs[ni + no:]
        x, y, c = lax.axis_index("x"), lax.axis_index("y"), lax.axis_index("c")
        pos = (x, y, c, 2 * x + y)
        locals_ = [pltpu.make_async_copy(s(in_refs, out_refs, pos), d(in_refs, out_refs, pos), local_sems.at[i])
                   for i, (s, d) in enumerate(local_copies)]
        remotes = []
        for i, (s, d, flip) in enumerate(remote_copies):
            peer = (1 - x if "x" in flip else x, 1 - y if "y" in flip else y, 1 - c if "c" in flip else c)
            remotes.append(pltpu.make_async_remote_copy(
                src_ref=s(in_refs, out_refs, pos), dst_ref=d(in_refs, out_refs, pos),
                send_sem=send_sems.at[i], recv_sem=recv_sems.at[i], device_id=peer, device_id_type=MESH))
        for cp in locals_ + remotes:
            cp.start()
        for cp in remotes:
            cp.wait_recv()
        for cp in remotes:
            cp.wait_send()
        for cp in locals_:
            cp.wait()

    hbm = pl.BlockSpec(memory_space=pl.ANY)
    return pl.pallas_call(
        body, name=name, in_specs=[hbm] * ni, out_specs=[hbm] * no, out_shape=outs,
        input_output_aliases=aliases,
        scratch_shapes=[pltpu.SemaphoreType.DMA((nr,)), pltpu.SemaphoreType.DMA((nr,)),
                        pltpu.SemaphoreType.DMA((max(nl, 1),))],
    )(*ins)


BIG = {"w_in": (0, 1), "ssm_w_glu": (1, 0), "w_out": (1, 0), "w_up": (0, 1), "w_down": (1, 0)}
BIG_NAMES = tuple(BIG)
FLIPS = ("y", "x", "xy")


def _peer_chip(pos, flip):
    x, y, _, _ = pos
    return 2 * (1 - x if "x" in flip else x) + (1 - y if "y" in flip else y)


def _block_rows(rows, cols, itemsize, mult):
    return _pick_tile(rows, max(mult, (2 * 1024 * 1024) // (cols * itemsize)), mult)


def _cast_into_full(w, kc, shard_axis, name):
    r, cdim = w.shape
    tr = _block_rows(r, cdim, 4, 16)
    nb = r // tr

    def body(kc_ref, w_ref, o_ref):
        o_ref[...] = w_ref[...].astype(BF16)

    if shard_axis == 1:
        full, o_spec = (r, 4 * cdim), pl.BlockSpec((tr, cdim), lambda i, kc: (i, kc[0]))
    else:
        full, o_spec = (4 * r, cdim), pl.BlockSpec((tr, cdim), lambda i, kc: (kc[0] * nb + i, 0))
    return pl.pallas_call(
        body, name=name,
        grid_spec=pltpu.PrefetchScalarGridSpec(
            num_scalar_prefetch=1, grid=(nb,), in_specs=[pl.BlockSpec((tr, cdim), lambda i, kc: (i, 0))],
            out_specs=o_spec),
        out_shape=jax.ShapeDtypeStruct(full, BF16), compiler_params=_cparams("parallel"))(kc, w)


def _pair_sum(g, recv, kc, half_axis, name, out_dtype):
    hr, hc = recv.shape
    tr = _block_rows(hr, hc, 4, 16)
    nb = hr // tr

    def body(kc_ref, g_ref, r_ref, o_ref):
        o_ref[...] = (g_ref[...] + r_ref[...]).astype(out_dtype)

    if half_axis == 0:
        g_spec = pl.BlockSpec((tr, hc), lambda i, kc: (kc[1] * nb + i, 0))
    elif half_axis == 1:
        g_spec = pl.BlockSpec((tr, hc), lambda i, kc: (i, kc[1]))
    else:
        g_spec = pl.BlockSpec((tr, hc), lambda i, kc: (i, 0))
    same = pl.BlockSpec((tr, hc), lambda i, kc: (i, 0))
    return pl.pallas_call(
        body, name=name,
        grid_spec=pltpu.PrefetchScalarGridSpec(num_scalar_prefetch=1, grid=(nb,), in_specs=[g_spec, same],
                                               out_specs=same),
        out_shape=jax.ShapeDtypeStruct((hr, hc), out_dtype), compiler_params=_cparams("parallel"))(kc, g, recv)


def _chip_sum(own, recv, kc, own_axis, out_axis, name):
    _, sr, sc = recv.shape
    tr = _block_rows(sr, sc, 4, 16)
    nb = sr // tr

    def body(kc_ref, o_ref, r_ref, t_ref):
        k = kc_ref[0]
        own_v = o_ref[...].astype(F32)
        r = [r_ref[m].astype(F32) for m in range(3)]
        terms = []
        for kk in range(4):
            m = jnp.bitwise_xor(k, kk)
            terms.append(jnp.where(m == 0, own_v, jnp.where(m == 1, r[0], jnp.where(m == 2, r[1], r[2]))))
        t_ref[...] = (terms[0] + terms[1]) + (terms[2] + terms[3])

    if own_axis == 0:
        own_spec = pl.BlockSpec((tr, sc), lambda i, kc: (kc[0] * nb + i, 0))
    elif own_axis == 1:
        own_spec = pl.BlockSpec((tr, sc), lambda i, kc: (i, kc[0]))
    else:
        own_spec = pl.BlockSpec((tr, sc), lambda i, kc: (kc[1] * nb + i, 0))
    if out_axis == 0:
        out_full, out_spec = (2 * sr, sc), pl.BlockSpec((tr, sc), lambda i, kc: (kc[1] * nb + i, 0))
    else:
        out_full, out_spec = (sr, 2 * sc), pl.BlockSpec((tr, sc), lambda i, kc: (i, kc[1]))
    return pl.pallas_call(
        body, name=name,
        grid_spec=pltpu.PrefetchScalarGridSpec(
            num_scalar_prefetch=1, grid=(nb,),
            in_specs=[own_spec, pl.BlockSpec((3, tr, sc), lambda i, kc: (0, i, 0))],
            out_specs=out_spec),
        out_shape=jax.ShapeDtypeStruct(out_full, F32), compiler_params=_cparams("parallel"))(kc, own, recv)


def _adamw(w, g, m, v, name):
    r, cdim = w.shape
    tr = _block_rows(r, cdim, 4, 8)
    c1 = 1.0 - ADAM_B1 ** ADAM_STEP
    c2 = 1.0 - ADAM_B2 ** ADAM_STEP

    def body(w_ref, g_ref, m_ref, v_ref, go_ref, d_ref, nm_ref, nv_ref):
        gv = g_ref[...]
        go_ref[...] = gv
        nm = ADAM_B1 * m_ref[...] + (1.0 - ADAM_B1) * gv
        nv = ADAM_B2 * v_ref[...] + (1.0 - ADAM_B2) * (gv * gv)
        d_ref[...] = -ADAM_LR * ((nm / c1) / (jnp.sqrt(nv / c2) + ADAM_EPS) + ADAM_WD * w_ref[...])
        nm_ref[...] = nm
        nv_ref[...] = nv

    spec = _rows(cdim, tr)
    return pl.pallas_call(body, name=name, grid=(r // tr,), in_specs=[spec] * 4, out_specs=[spec] * 4,
                          out_shape=[jax.ShapeDtypeStruct((r, cdim), F32)] * 4,
                          compiler_params=_cparams("parallel"))(w, g, m, v)


def _adamw_whole(ws, gs, ms, vs, name):
    n = len(ws)
    c1 = 1.0 - ADAM_B1 ** ADAM_STEP
    c2 = 1.0 - ADAM_B2 ** ADAM_STEP

    def body(*refs):
        for i in range(n):
            w_ref, g_ref, m_ref, v_ref, d_ref, nm_ref, nv_ref = [refs[j * n + i] for j in range(7)]
            gv = g_ref[...]
            nm = ADAM_B1 * m_ref[...] + (1.0 - ADAM_B1) * gv
            nv = ADAM_B2 * v_ref[...] + (1.0 - ADAM_B2) * (gv * gv)
            d_ref[...] = -ADAM_LR * ((nm / c1) / (jnp.sqrt(nv / c2) + ADAM_EPS) + ADAM_WD * w_ref[...])
            nm_ref[...] = nm
            nv_ref[...] = nv

    vmem = pl.BlockSpec(memory_space=pltpu.VMEM)
    out = pl.pallas_call(body, name=name, in_specs=[vmem] * (4 * n), out_specs=[vmem] * (3 * n),
                         out_shape=[jax.ShapeDtypeStruct(a.shape, F32) for a in ws] * 3,
                         compiler_params=pltpu.CompilerParams(vmem_limit_bytes=VMEM_LIMIT))(*ws, *gs, *ms, *vs)
    return out[:n], out[n:2 * n], out[2 * n:]


SIDE_EFFECT = pltpu.SideEffectType.DATAFLOW_SIDE_EFFECTING


def _descriptors(copies, refs, send_sems, recv_sems, sem_off=0):
    x, y, c = lax.axis_index("x"), lax.axis_index("y"), lax.axis_index("c")
    pos = (x, y, c, 2 * x + y)
    out = []
    for i, (s, d, flip) in enumerate(copies):
        peer = (1 - x if "x" in flip else x, 1 - y if "y" in flip else y, 1 - c if "c" in flip else c)
        out.append(pltpu.make_async_remote_copy(
            src_ref=s(refs, refs, pos), dst_ref=d(refs, refs, pos),
            send_sem=send_sems.at[sem_off + i], recv_sem=recv_sems.at[sem_off + i],
            device_id=peer, device_id_type=MESH))
    return out


def _shifted(copies, off):
    return [(lambda I, O, pos, s=s: s(I[off:], O[off:], pos), lambda I, O, pos, d=d: d(I[off:], O[off:], pos), flip)
            for s, d, flip in copies]


BARRIER_IDS = {"c": (1, 2), "ici": (3, 4)}


def _exchange_start(name, bufs, copies, turns, after=None):
    n, nr = len(bufs), len(copies)
    na = 0 if after is None else 1
    flips = sorted({flip for _, _, flip in copies})
    kind = "c" if flips == ["c"] else "ici"
    collective_id = BARRIER_IDS[kind][turns[kind] % 2]
    turns[kind] += 1

    def body(*refs):
        x, y, c = lax.axis_index("x"), lax.axis_index("y"), lax.axis_index("c")
        barrier = pltpu.get_barrier_semaphore()
        for flip in flips:
            peer = (1 - x if "x" in flip else x, 1 - y if "y" in flip else y, 1 - c if "c" in flip else c)
            pl.semaphore_signal(barrier, inc=1, device_id=peer, device_id_type=MESH)
        pl.semaphore_wait(barrier, len(flips))
        for cp in _descriptors(copies, refs[:n], refs[n + na], refs[n + na + 1]):
            cp.start()
        token = refs[2 * n + na + 2]
        token[...] = jnp.zeros_like(token)

    hbm = pl.BlockSpec(memory_space=pltpu.HBM)
    sem = pl.BlockSpec(memory_space=pltpu.SEMAPHORE)
    out = pl.pallas_call(
        body, name=name,
        in_specs=[hbm] * n + [pl.BlockSpec(memory_space=pl.ANY)] * na,
        out_specs=(sem, sem, *[hbm] * n, pl.BlockSpec(memory_space=pltpu.VMEM)),
        out_shape=(pltpu.SemaphoreType.DMA((nr,)), pltpu.SemaphoreType.DMA((nr,)),
                   *[pltpu.HBM(b.shape, b.dtype) for b in bufs], jax.ShapeDtypeStruct((SUBLANES, LANES), F32)),
        input_output_aliases={i: 2 + i for i in range(n)},
        compiler_params=pltpu.CompilerParams(has_side_effects=SIDE_EFFECT, collective_id=collective_id),
    )(*[pltpu.with_memory_space_constraint(b, pltpu.HBM) for b in bufs], *([after] * na))
    return out[0], out[1], list(out[2:2 + n]), out[2 + n]


def _exchange_wait(name, send_sems, recv_sems, bufs, copies, after, sem_off=0):
    n = len(bufs)

    def body(*refs):
        for cp in _descriptors(copies, refs[:n], refs[n], refs[n + 1], sem_off):
            cp.wait_send()
            cp.wait_recv()

    hbm = pl.BlockSpec(memory_space=pltpu.HBM)
    sem = pl.BlockSpec(memory_space=pltpu.SEMAPHORE)
    out = pl.pallas_call(
        body, name=name,
        in_specs=[hbm] * n + [sem, sem, pl.BlockSpec(memory_space=pl.ANY)],
        out_specs=tuple([hbm] * n),
        out_shape=tuple(pltpu.HBM(b.shape, b.dtype) for b in bufs),
        input_output_aliases={i: i for i in range(n)},
        compiler_params=pltpu.CompilerParams(has_side_effects=SIDE_EFFECT),
    )(*bufs, send_sems, recv_sems, after)
    return list(out)


FIRST = ("w_in",)
MID = ("ssm_w_glu", "w_out")
LATE = ("w_up", "w_down")
GROUPS = {"first": FIRST, "mid": MID, "late": LATE}
ARRIVALS = {"first": FIRST, "mid": MID, "up": ("w_up",), "down": ("w_down",)}


def _gather_copies(names, shard_shapes):
    def region(i, chip, c):
        half_axis, shard_axis = BIG[names[i]]
        ssize = shard_shapes[i][shard_axis]
        hsize = shard_shapes[i][half_axis] // 2
        return lambda ref: _view(_view(ref, shard_axis, chip * ssize, ssize), half_axis, c * hsize, hsize)

    ici, d2d = [], []
    for i in range(len(names)):
        for flip in FLIPS:
            ici.append((lambda I, O, pos, i=i: region(i, pos[3], pos[2])(I[i]),
                        lambda I, O, pos, i=i: region(i, pos[3], pos[2])(O[i]), flip))
            d2d.append((lambda I, O, pos, i=i, flip=flip: region(i, _peer_chip(pos, flip), pos[2])(I[i]),
                        lambda I, O, pos, i=i, flip=flip: region(i, _peer_chip(pos, flip), pos[2])(O[i]), "c"))
    return ici, d2d


def _half_shape(n, shape):
    r, cdim = shape
    return (r // 2, cdim) if BIG[n][0] == 0 else (r, cdim // 2)


def _sub_shape(n, shape):
    hr, hc = _half_shape(n, shape)
    return (hr, hc // 4) if BIG[n][1] == 1 else (hr // 4, hc)


def _pair_copies(names, shapes, with_pack, dst_off):
    n = len(names)

    def other_half(i, ref, pos):
        half_axis = BIG[names[i]][0]
        hsize = shapes[i][half_axis] // 2
        return _view(ref, half_axis, (1 - pos[2]) * hsize, hsize)

    copies = [(lambda I, O, pos, i=i: other_half(i, I[i], pos), lambda I, O, pos, i=i: O[dst_off + i], "c")
              for i in range(n)]
    if with_pack:
        copies.append((lambda I, O, pos: I[n], lambda I, O, pos: O[dst_off + n], "c"))
    return copies


def _chip_copies(names, shapes, pack_rows, dst_off):
    n = len(names)

    def piece(i, ref, chip):
        shard_axis = BIG[names[i]][1]
        ssize = _sub_shape(names[i], shapes[i])[shard_axis]
        return _view(ref, shard_axis, chip * ssize, ssize)

    copies = []
    for i in range(n):
        for slot, flip in enumerate(FLIPS):
            copies.append((lambda I, O, pos, i=i, flip=flip: piece(i, I[i], _peer_chip(pos, flip)),
                           lambda I, O, pos, i=i, slot=slot: O[dst_off + i].at[slot], flip))
    if pack_rows:
        for slot, flip in enumerate(FLIPS):
            copies.append((lambda I, O, pos: _view(I[n], 0, pos[2] * (pack_rows // 2), pack_rows // 2),
                           lambda I, O, pos, slot=slot: O[dst_off + n].at[slot], flip))
    return copies


class _Exchanges:
    def __init__(self, shards, tiny, kc):
        self.kc = kc
        wb = {n: _cast_into_full(shards[n], kc, BIG[n][1], "cast_" + n) for n in BIG_NAMES}
        self.gathering, self.forwarding, self.pairing, self.reducing = {}, {}, {}, {}
        self.turns = {"c": 0, "ici": 0}
        tiny_copies = [(lambda I, O, pos: I[0], lambda I, O, pos: O[1].at[pos[3]], flip) for flip in FLIPS]
        self.gathering["tiny"] = (0, 0, 2, tiny_copies, None)
        bufs, copies = [tiny, lax.empty((4,) + tiny.shape, F32)], list(tiny_copies)
        for group, names in ARRIVALS.items():
            ici, d2d = _gather_copies(names, [shards[n].shape for n in names])
            self.gathering[group] = (len(bufs), len(copies), len(names), ici, d2d)
            copies += _shifted(ici, len(bufs))
            bufs += [wb[n] for n in names]
        self.started = _exchange_start("gather_start", bufs, copies, self.turns)
        self.zero = self.started[3][0, 0]

    def _arrived(self, group, after):
        buf_off, sem_off, n, ici, _ = self.gathering[group]
        send_sems, recv_sems, bufs, _ = self.started
        return _exchange_wait("gather_%s_wait" % group, send_sems, recv_sems, bufs[buf_off:buf_off + n], ici, after,
                              sem_off)

    def small_params(self, kc):
        tiny, got = self._arrived("tiny", self.started[3])
        return lax.dynamic_update_index_in_dim(got, tiny, kc[0], 0)

    def forward(self, group, after):
        d2d = self.gathering[group][4]
        self.forwarding[group] = (_exchange_start("forward_%s_start" % group, self._arrived(group, after), d2d,
                                                  self.turns), d2d)
        return self.forwarding[group][0][3]

    def weights(self, group, after):
        if group not in self.forwarding:
            after = self.forward(group, after)
        (send_sems, recv_sems, bufs, _), d2d = self.forwarding[group]
        full = _exchange_wait("forward_%s_wait" % group, send_sems, recv_sems, bufs, d2d, after)
        return dict(zip(ARRIVALS[group], full))

    def grads_ready(self, group, grads):
        names = GROUPS[group]
        gs = [grads[n] for n in names]
        land = [lax.empty(_half_shape(n, g.shape), F32) for n, g in zip(names, gs)]
        copies = _pair_copies(names, [g.shape for g in gs], False, len(names))
        started = _exchange_start("pair_%s_start" % group, gs + land, copies, self.turns)
        self.pairing[group] = (started, copies)
        return started[3]

    def grads_send(self, group, after):
        names = GROUPS[group]
        n = len(names)
        (send_sems, recv_sems, bufs, _), copies = self.pairing[group]
        bufs = _exchange_wait("pair_%s_wait" % group, send_sems, recv_sems, bufs, copies, after)
        chip = [_pair_sum(bufs[i], bufs[n + i], self.kc, BIG[names[i]][0], "pair_sum_" + names[i], BF16)
                for i in range(n)]
        shapes = [bufs[i].shape for i in range(n)]
        land = [lax.empty((3,) + _sub_shape(names[i], shapes[i]), BF16) for i in range(n)]
        copies = _chip_copies(names, shapes, 0, n)
        started = _exchange_start("reduce_%s_start" % group, chip + land, copies, self.turns)
        self.reducing[group] = (started, copies)
        return started[3]

    def finish_pack(self, pack):
        kc = self.kc
        prow = pack.shape[0] // 2
        recv = _exchange("reduce_d2d", [pack], [jax.ShapeDtypeStruct(pack.shape, F32)], {}, [],
                         _pair_copies((), [], True, 0))
        chip_pack = _pair_sum(pack, recv[0], kc, None, "pair_sum_pack", F32)
        copies = _chip_copies((), [], pack.shape[0], 1)
        land = lax.empty((3, prow, pack.shape[1]), F32)
        pack_sems_s, pack_sems_r, pack_bufs, after = _exchange_start("reduce_pack_start", [chip_pack, land], copies,
                                                                     self.turns)

        names, chips, recvs = (), [], []
        for group, group_names in GROUPS.items():
            (send_sems, recv_sems, bufs, _), group_copies = self.reducing[group]
            bufs = _exchange_wait("reduce_%s_wait" % group, send_sems, recv_sems, bufs, group_copies, after)
            n = len(group_names)
            names, chips, recvs = names + group_names, chips + bufs[:n], recvs + bufs[n:]
            after = bufs[n]
        total = [_chip_sum(chips[i], recvs[i], kc, BIG[n][1], BIG[n][0], "chip_sum_" + n)
                 for i, n in enumerate(names)]

        def my_half(half_axis, ref, pos):
            hsize = ref.shape[half_axis] // 2
            return _view(ref, half_axis, pos[2] * hsize, hsize)

        swap = [(lambda I, O, pos, i=i, n=n: my_half(BIG[n][0], I[i], pos),
                 lambda I, O, pos, i=i, n=n: my_half(BIG[n][0], O[i], pos), "c") for i, n in enumerate(names)]
        self.swapping = (_exchange_start("swap_start", total, swap, self.turns), swap, names)

        chip_pack, recv_pack = _exchange_wait("reduce_pack_wait", pack_sems_s, pack_sems_r, pack_bufs, copies,
                                              self.swapping[0][3])
        total_pack = _chip_sum(chip_pack, recv_pack, kc, None, 0, "chip_sum_pack")
        swap = [(lambda I, O, pos: my_half(0, I[0], pos), lambda I, O, pos: my_half(0, O[0], pos), "c")]
        return _exchange("swap_pack", [total_pack], [jax.ShapeDtypeStruct(pack.shape, F32)], {0: 0}, [], swap)[0]

    def finish_big(self, after):
        (send_sems, recv_sems, bufs, _), swap, names = self.swapping
        return dict(zip(names, _exchange_wait("swap_wait", send_sems, recv_sems, bufs, swap, after)))


WEIGHTS = ("meta_tokens", "norm_mix_g", "w_in", "conv_w", "ssm_lam_re", "ssm_lam_im", "ssm_log_dt", "ssm_b_re",
           "ssm_b_im", "ssm_c_re", "ssm_c_im", "ssm_d", "ssm_w_glu", "gain_conv_out", "gain_ssm_out", "w_out",
           "norm_ffn_g", "w_up", "ffn_conv_w", "ffn_conv_b", "w_down", "norm_final_g")
TINY_SHARDED = ("meta_tokens", "conv_w", "ffn_conv_w")
REPLICATED = tuple(n for n in WEIGHTS if n not in BIG and n not in TINY_SHARDED)
PACK_COLS = 512


def _pack(arrays, row_mult, cols):
    flat = jnp.concatenate([a.reshape(-1).astype(F32) for a in arrays])
    n = flat.shape[0]
    total = -(-n // (row_mult * cols)) * (row_mult * cols)
    return jnp.concatenate([flat, jnp.zeros((total - n,), F32)]).reshape(total // cols, cols)


def _unpack(packed, shapes):
    flat = packed.reshape(-1)
    out, off = [], 0
    for s in shapes:
        n = math.prod(s)
        out.append(flat[off:off + n].reshape(s))
        off += n
    return out


def kernel(x, meta_tokens, norm_mix_g, w_in, conv_w, ssm_lam_re, ssm_lam_im, ssm_log_dt, ssm_b_re, ssm_b_im, ssm_c_re, ssm_c_im, ssm_d, ssm_w_glu, gain_conv_out, gain_ssm_out, w_out, norm_ffn_g, w_up, ffn_conv_w, ffn_conv_b, w_down, norm_final_g, loss_target, m_meta_tokens, m_norm_mix_g, m_w_in, m_conv_w, m_ssm_lam_re, m_ssm_lam_im, m_ssm_log_dt, m_ssm_b_re, m_ssm_b_im, m_ssm_c_re, m_ssm_c_im, m_ssm_d, m_ssm_w_glu, m_gain_conv_out, m_gain_ssm_out, m_w_out, m_norm_ffn_g, m_w_up, m_ffn_conv_w, m_ffn_conv_b, m_w_down, m_norm_final_g, v_meta_tokens, v_norm_mix_g, v_w_in, v_conv_w, v_ssm_lam_re, v_ssm_lam_im, v_ssm_log_dt, v_ssm_b_re, v_ssm_b_im, v_ssm_c_re, v_ssm_c_im, v_ssm_d, v_ssm_w_glu, v_gain_conv_out, v_gain_ssm_out, v_w_out, v_norm_ffn_g, v_w_up, v_ffn_conv_w, v_ffn_conv_b, v_w_down, v_norm_final_g):
    args = dict(locals())
    w = {n: args[n] for n in WEIGHTS}
    mom = {n: args["m_" + n] for n in WEIGHTS}
    var = {n: args["v_" + n] for n in WEIGHTS}
    kx, ky, kc_ = lax.axis_index("x"), lax.axis_index("y"), lax.axis_index("c")
    chip = 2 * kx + ky
    kc = jnp.stack([chip, kc_]).astype(jnp.int32)

    def squeeze(n, a):
        if n == "meta_tokens":
            return a
        if n == "norm_final_g":
            return a.reshape(1, -1)
        a = a[0]
        return a.reshape(1, -1) if a.ndim == 1 else a

    wl = {n: squeeze(n, w[n]) for n in WEIGHTS}
    ml = {n: squeeze(n, mom[n]) for n in WEIGHTS}
    vl = {n: squeeze(n, var[n]) for n in WEIGHTS}

    tiny = _pack([wl[n] for n in TINY_SHARDED], SUBLANES, LANES)
    ex = _Exchanges({n: wl[n] for n in BIG_NAMES}, tiny, kc)
    tiny_shapes = [wl[n].shape for n in TINY_SHARDED]
    tiny_all = ex.small_params(kc)
    tiny_parts = [_unpack(tiny_all[k], tiny_shapes) for k in range(4)]
    p = {n: wl[n] for n in WEIGHTS if n not in BIG}
    for j, n in enumerate(TINY_SHARDED):
        p[n] = jnp.concatenate([tiny_parts[k][j] for k in range(4)], axis=1)
    p["ssm_log_dt"] = wl["ssm_log_dt"].reshape(-1)

    loss_local, grad_x, grads = _local_step(x[0], loss_target[0], p, ex)

    small_names = REPLICATED + TINY_SHARDED
    small_shapes = [tuple(grads[n].shape) for n in small_names] + [(1,)]
    pack = _pack([grads[n] for n in small_names] + [loss_local.reshape(1)], 2 * 16, PACK_COLS)
    g_pack = ex.finish_pack(pack)
    g_small = dict(zip(small_names + ("loss",), _unpack(g_pack, small_shapes)))
    loss = g_small["loss"][0]
    swapped = ("ssm_b_re", "ssm_b_im")

    def view(n, a):
        if n in swapped:
            return jnp.swapaxes(a, -1, -2)
        return a.reshape(1, -1) if a.ndim == 1 else a

    g = {}
    for n in REPLICATED:
        g[n] = g_small[n].reshape(view(n, w[n]).shape)
    for n in TINY_SHARDED:
        cols = wl[n].shape[1]
        g[n] = lax.dynamic_slice_in_dim(g_small[n], chip * cols, cols, axis=1).reshape(w[n].shape)
    delta, new_m, new_v = {}, {}, {}
    small = [[view(n, d[n]) for n in small_names] for d in (w, mom, var)]
    small.insert(1, [g[n] for n in small_names])
    for d, outs in zip((delta, new_m, new_v), _adamw_whole(*small, "adamw_small")):
        d.update(zip(small_names, outs))
    for d in (g, delta, new_m, new_v):
        d.update({n: jnp.swapaxes(d[n], -1, -2) for n in swapped})
    g_big = ex.finish_big(delta[small_names[0]])
    for n in BIG_NAMES:
        g[n], delta[n], new_m[n], new_v[n] = _adamw(wl[n], g_big[n], ml[n], vl[n], "adamw_" + n)

    def like(n, a):
        return a.reshape(w[n].shape)

    return (loss, grad_x[None], *[like(n, g[n]) for n in WEIGHTS], *[like(n, delta[n]) for n in WEIGHTS],
            *[like(n, new_m[n]) for n in WEIGHTS], *[like(n, new_v[n]) for n in WEIGHTS])
```

```python
import functools
import math

import jax
import jax.numpy as jnp
from jax import lax
from jax.experimental import pallas as pl
from jax.experimental.pallas import tpu as pltpu

F32 = jnp.float32
BF16 = jnp.bfloat16
MESH = pl.DeviceIdType.MESH

N_META = 16
N_GROUPS = 32
GROUP = 16
STATE = 64
RMS_EPS = 1e-6
ADAM_LR = 0.001
ADAM_B1 = 0.9
ADAM_B2 = 0.999
ADAM_EPS = 1e-08
ADAM_WD = 0.01
ADAM_STEP = 10

LANES = 128
SUBLANES = 8
ROW_ALIGN = 128
ROW_TILES = 4
VMEM_LIMIT = 52 * 1024 * 1024
MM_VMEM_BUDGET = 40 * 1024 * 1024
GELU_C = math.sqrt(2.0 / math.pi)
GELU_A = 0.044715


def _cparams(*sem):
    return pltpu.CompilerParams(dimension_semantics=sem, vmem_limit_bytes=VMEM_LIMIT)


def _pick_tile(dim, cap, mult):
    best = None
    for t in range(mult, min(dim, cap) + 1, mult):
        if dim % t == 0:
            best = t
    return best if best is not None else dim


def _mm(a, b, mode, name, out_dtype=F32, acc_in=None, after=None):
    if mode == "tn":
        kdim, m = a.shape
    else:
        m, kdim = a.shape
    n = b.shape[0] if mode == "nt" else b.shape[1]
    tm = _pick_tile(m, 1408, LANES if mode == "tn" else 16)
    tk = _pick_tile(kdim, 2816, LANES)
    nk = kdim // tk
    out_bytes = jnp.dtype(out_dtype).itemsize
    for cap in (1408, 1024, 512, 256, LANES):
        tn = _pick_tile(n, cap, LANES)
        blocks = 2 * (tm * tk * 2 + tk * tn * 2 + tm * tn * out_bytes * (2 if acc_in is not None else 1))
        if blocks + (tm * tn * 4 if nk > 1 else 0) <= MM_VMEM_BUDGET:
            break
    has_acc = acc_in is not None

    def body(*refs):
        if after is not None:
            refs = refs[1:]
        if has_acc:
            a_ref, b_ref, c_ref, o_ref = refs[:4]
            rest = refs[4:]
        else:
            a_ref, b_ref, o_ref = refs[:3]
            c_ref = None
            rest = refs[3:]
        if mode == "nn":
            p = jnp.dot(a_ref[...], b_ref[...], preferred_element_type=F32)
        elif mode == "nt":
            p = lax.dot_general(a_ref[...], b_ref[...], (((1,), (1,)), ((), ())), preferred_element_type=F32)
        else:
            p = lax.dot_general(a_ref[...], b_ref[...], (((0,), (0,)), ((), ())), preferred_element_type=F32)
        if nk == 1:
            if has_acc:
                p = p + c_ref[...]
            o_ref[...] = p.astype(out_dtype)
        else:
            acc_ref = rest[0]
            k = pl.program_id(2)

            @pl.when(k == 0)
            def _():
                acc_ref[...] = p + c_ref[...] if has_acc else p

            @pl.when(k > 0)
            def _():
                acc_ref[...] += p

            @pl.when(k == nk - 1)
            def _():
                o_ref[...] = acc_ref[...].astype(out_dtype)

    if mode == "tn":
        a_spec = pl.BlockSpec((tk, tm), lambda i, j, k: (k, i))
    else:
        a_spec = pl.BlockSpec((tm, tk), lambda i, j, k: (i, k))
    if mode == "nt":
        b_spec = pl.BlockSpec((tn, tk), lambda i, j, k: (j, k))
    else:
        b_spec = pl.BlockSpec((tk, tn), lambda i, j, k: (k, j))
    o_spec = pl.BlockSpec((tm, tn), lambda i, j, k: (i, j))
    in_specs = [a_spec, b_spec] + ([o_spec] if has_acc else [])
    args = (a, b) + ((acc_in,) if has_acc else ())
    if after is not None:
        in_specs = [pl.BlockSpec(memory_space=pl.ANY)] + in_specs
        args = (after,) + args
    return pl.pallas_call(
        body, name=name, grid=(m // tm, n // tn, nk),
        in_specs=in_specs, out_specs=o_spec,
        out_shape=jax.ShapeDtypeStruct((m, n), out_dtype),
        scratch_shapes=[pltpu.VMEM((tm, tn), F32)] if nk > 1 else [],
        compiler_params=_cparams("parallel", "parallel", "arbitrary"),
    )(*args)


def _mm_rows(a, b, mode, name, ins, outs, epilogue, scratch=()):
    m, kdim = a.shape
    n = b.shape[0] if mode == "nt" else b.shape[1]
    tm = m // ROW_TILES
    tk = _pick_tile(kdim, 2816, LANES)
    nk = kdim // tk
    ni, no = len(ins), len(outs)

    def body(*refs):
        a_ref, b_ref = refs[:2]
        in_refs, out_refs, rest = refs[2:2 + ni], refs[2 + ni:2 + ni + no], refs[2 + ni + no:]
        i = pl.program_id(0)
        if mode == "nn":
            p = jnp.dot(a_ref[...], b_ref[...], preferred_element_type=F32)
        else:
            p = lax.dot_general(a_ref[...], b_ref[...], (((1,), (1,)), ((), ())), preferred_element_type=F32)
        if nk == 1:
            epilogue(p, i, in_refs, out_refs, rest)
        else:
            acc_ref = rest[0]
            k = pl.program_id(1)

            @pl.when(k == 0)
            def _():
                acc_ref[...] = p

            @pl.when(k > 0)
            def _():
                acc_ref[...] += p

            @pl.when(k == nk - 1)
            def _():
                epilogue(acc_ref[...], i, in_refs, out_refs, rest[1:])

    def spec(shape, kind):
        if kind == "rows":
            return pl.BlockSpec((tm,) + tuple(shape[1:]), lambda i, k: (i,) + (0,) * (len(shape) - 1))
        if kind == "whole":
            return pl.BlockSpec(tuple(shape), lambda i, k: (0,) * len(shape))
        return pl.BlockSpec(memory_space=pl.ANY)

    a_spec = pl.BlockSpec((tm, tk), lambda i, k: (i, k))
    b_spec = pl.BlockSpec((n, tk), lambda i, k: (0, k)) if mode == "nt" else pl.BlockSpec((tk, n), lambda i, k: (k, 0))
    return pl.pallas_call(
        body, name=name, grid=(ROW_TILES, nk),
        in_specs=[a_spec, b_spec] + [spec(x.shape, kind) for x, kind in ins],
        out_specs=[spec(shape, kind) for shape, _, kind in outs],
        out_shape=[jax.ShapeDtypeStruct(shape, dtype) for shape, dtype, _ in outs],
        scratch_shapes=([pltpu.VMEM((tm, n), F32)] if nk > 1 else []) + list(scratch),
        compiler_params=_cparams("arbitrary", "arbitrary"),
    )(a, b, *[x for x, _ in ins])


def _rows(shape_cols, tr, dtype=None):
    return pl.BlockSpec((tr, shape_cols), lambda i: (i, 0))


def _const(shape):
    return pl.BlockSpec(shape, lambda i: (0,) * len(shape))


def _rms(x):
    return lax.rsqrt(jnp.mean(x * x, axis=-1, keepdims=True) + RMS_EPS)


def _rms_bwd(x, r, g, dy):
    xn = x * r
    dxn = dy * g
    dx = r * (dxn - xn * jnp.mean(dxn * xn, axis=-1, keepdims=True))
    return dx, dy * xn


def _gelu(y):
    return 0.5 * y * (1.0 + jnp.tanh(GELU_C * (y + GELU_A * y * y * y)))


def _gelu_grad(y):
    t = jnp.tanh(GELU_C * (y + GELU_A * y * y * y))
    return 0.5 * (1.0 + t) + 0.5 * y * (1.0 - t * t) * GELU_C * (1.0 + 3.0 * GELU_A * y * y)


def _sigmoid(z):
    return 1.0 / (1.0 + jnp.exp(-z))


def _proj_res_norm(a, w, h, g, after, name):
    def epilogue(p, i, ins, outs, _):
        x = ins[0][...] + p
        outs[0][...] = x
        outs[1][...] = (x * _rms(x) * ins[1][...]).astype(BF16)

    return _mm_rows(a, w, "nn", name, [(h, "rows"), (g, "whole"), (after, "hbm")],
                    [(h.shape, F32, "rows"), (h.shape, BF16, "rows")], epilogue)


def _proj_norm_bwd(da, w, h, g, dres, after, name):
    d = h.shape[1]

    def epilogue(p, i, ins, outs, _):
        x = ins[0][...]
        dx, dgs = _rms_bwd(x, _rms(x), ins[1][...], p)
        dh = ins[2][...] + dx
        outs[0][...] = dh
        outs[1][...] = dh.astype(BF16)

        @pl.when(i == 0)
        def _():
            outs[2][...] = jnp.zeros_like(outs[2])

        outs[2][...] += jnp.sum(dgs, axis=0, keepdims=True)

    return _mm_rows(da, w, "nt", name, [(h, "rows"), (g, "whole"), (dres, "rows"), (after, "hbm")],
                    [(h.shape, F32, "rows"), (h.shape, BF16, "rows"), ((1, d), F32, "whole")], epilogue)


def _input_norm_bwd(h, g, dhn, dres, n_real, name):
    tp, d = h.shape
    tr = tp // ROW_TILES

    def body(h_ref, g_ref, dhn_ref, dres_ref, dx_ref, dmeta_ref, dg_ref, stage, sem):
        i = pl.program_id(0)
        x = h_ref[...]
        dx, dgs = _rms_bwd(x, _rms(x), g_ref[...], dhn_ref[...])
        stage[...] = dres_ref[...] + dx

        @pl.when(i == 0)
        def _():
            dg_ref[...] = jnp.zeros_like(dg_ref)
            dmeta_ref[...] = stage[:N_META, :]

        dg_ref[...] += jnp.sum(dgs, axis=0, keepdims=True)
        for t in range(ROW_TILES):
            lo, hi = max(t * tr, N_META), min((t + 1) * tr, n_real)
            if hi > lo:
                @pl.when(i == t)
                def _(t=t, lo=lo, hi=hi):
                    cp = pltpu.make_async_copy(stage.at[pl.ds(lo - t * tr, hi - lo), :],
                                               dx_ref.at[pl.ds(lo - N_META, hi - lo), :], sem)
                    cp.start()
                    cp.wait()

    return pl.pallas_call(
        body, name=name, grid=(ROW_TILES,),
        in_specs=[_rows(d, tr), _const((1, d)), _rows(d, tr), _rows(d, tr)],
        out_specs=[pl.BlockSpec(memory_space=pl.ANY), _const((N_META, d)), _const((1, d))],
        out_shape=[jax.ShapeDtypeStruct((n_real - N_META, d), F32), jax.ShapeDtypeStruct((N_META, d), F32),
                   jax.ShapeDtypeStruct((1, d), F32)],
        scratch_shapes=[pltpu.VMEM((tr, d), F32), pltpu.SemaphoreType.DMA],
        compiler_params=_cparams("arbitrary"))(h, g, dhn, dres)


def _load_token_rows(tok_hbm, buf, sem, tr, n_real, head=None, wait=False, i=None):
    i = pl.program_id(0) if i is None else i
    for t in range(ROW_TILES):
        base = t * tr
        lo, hi = max(base, N_META), min(base + tr, n_real)

        @pl.when(i == t)
        def _(base=base, lo=lo, hi=hi):
            if hi > lo:
                cp = pltpu.make_async_copy(tok_hbm.at[pl.ds(lo - N_META, hi - lo), :],
                                           buf.at[pl.ds(lo - base, hi - lo), :], sem)
                if wait:
                    cp.wait()
                    return
                cp.start()
            if wait:
                return
            if base < N_META:
                buf[0:N_META - base, :] = (jnp.zeros((N_META - base, buf.shape[1]), F32) if head is None
                                           else head[base:N_META, :])
            if hi < base + tr:
                buf[max(hi, base) - base:tr, :] = jnp.zeros((base + tr - max(hi, base), buf.shape[1]), F32)


def _input_norm_fwd(x, meta, g, tp, name):
    seq, d = x.shape
    tr = tp // ROW_TILES
    n_real = N_META + seq

    def body(x_hbm, meta_ref, g_ref, h_ref, hn_ref, buf, sem):
        _load_token_rows(x_hbm, buf, sem, tr, n_real, head=meta_ref)
        _load_token_rows(x_hbm, buf, sem, tr, n_real, wait=True)
        h = buf[...]
        h_ref[...] = h
        hn_ref[...] = (h * _rms(h) * g_ref[...]).astype(BF16)

    return pl.pallas_call(
        body, name=name, grid=(ROW_TILES,),
        in_specs=[pl.BlockSpec(memory_space=pl.ANY), _const((N_META, d)), _const((1, d))],
        out_specs=[_rows(d, tr), _rows(d, tr)],
        out_shape=[jax.ShapeDtypeStruct((tp, d), F32), jax.ShapeDtypeStruct((tp, d), BF16)],
        scratch_shapes=[pltpu.VMEM((tr, d), F32), pltpu.SemaphoreType.DMA],
        compiler_params=_cparams("arbitrary"))(x, meta, g)


def _proj_loss_bwd(act, w, h1, target, g, n_real, name):
    tp, d = h1.shape
    tr = tp // ROW_TILES

    def epilogue(p, i, ins, outs, scratch):
        h1_ref, t_hbm, g_ref = ins
        loss_ref, dh_ref, dhb_ref, dg_ref = outs
        t_buf, sem = scratch
        _load_token_rows(t_hbm, t_buf, sem, tr, n_real, i=i)
        x = h1_ref[...] + p
        r = _rms(x)
        row = i * tr + lax.broadcasted_iota(jnp.int32, (tr, d), 0)
        valid = (row >= N_META) & (row < n_real)
        _load_token_rows(t_hbm, t_buf, sem, tr, n_real, wait=True, i=i)
        e = jnp.where(valid, x * r * g_ref[...] - t_buf[...], 0.0)
        dx, dgs = _rms_bwd(x, r, g_ref[...], e * (1.0 / d))
        dh_ref[...] = dx
        dhb_ref[...] = dx.astype(BF16)

        @pl.when(i == 0)
        def _():
            dg_ref[...] = jnp.zeros_like(dg_ref)
            loss_ref[...] = jnp.zeros_like(loss_ref)

        dg_ref[...] += jnp.sum(dgs, axis=0, keepdims=True)
        loss_ref[...] += (0.5 / d) * jnp.sum(jnp.sum(e * e, axis=0, keepdims=True), axis=1, keepdims=True)

    return _mm_rows(act, w, "nn", name, [(h1, "rows"), (target, "hbm"), (g, "whole")],
                    [((1, LANES), F32, "whole"), ((tp, d), F32, "rows"), ((tp, d), BF16, "rows"),
                     ((1, d), F32, "whole")],
                    epilogue, scratch=[pltpu.VMEM((tr, d), F32), pltpu.SemaphoreType.DMA])


def _mix_fwd(co, y, z, gc, gs, name):
    tp, dh = co.shape
    tr = tp // ROW_TILES

    def body(co_ref, y_ref, z_ref, gc_ref, gs_ref, m_ref):
        c = co_ref[...]
        m_ref[:, :dh] = (c * _rms(c) * gc_ref[...]).astype(BF16)
        so = _gelu(y_ref[...]) * _sigmoid(z_ref[...])
        m_ref[:, dh:] = (so * _rms(so) * gs_ref[...]).astype(BF16)

    return pl.pallas_call(
        body, name=name, grid=(ROW_TILES,),
        in_specs=[_rows(dh, tr)] * 3 + [_const((1, dh))] * 2,
        out_specs=_rows(2 * dh, tr),
        out_shape=jax.ShapeDtypeStruct((tp, 2 * dh), BF16),
        compiler_params=_cparams("parallel"))(co, y, z, gc, gs)


def _proj_mix_bwd(dh1b, w, co, y, z, gc, gs, name):
    tp, dh = co.shape

    def epilogue(p, i, ins, outs, _):
        co_ref, y_ref, z_ref, gc_ref, gs_ref = ins
        dco_ref, dz_ref, dgp_ref, dgc_ref, dgs_ref = outs
        c = co_ref[...]
        dco, dgc = _rms_bwd(c, _rms(c), gc_ref[...], p[:, :dh])
        dco_ref[...] = dco
        gl = _gelu(y_ref[...])
        sg = _sigmoid(z_ref[...])
        so = gl * sg
        dso, dgs = _rms_bwd(so, _rms(so), gs_ref[...], p[:, dh:])
        dz_ref[...] = (dso * gl * sg * (1.0 - sg)).astype(BF16)
        dgp_ref[...] = dso * sg

        @pl.when(i == 0)
        def _():
            dgc_ref[...] = jnp.zeros_like(dgc_ref)
            dgs_ref[...] = jnp.zeros_like(dgs_ref)

        dgc_ref[...] += jnp.sum(dgc, axis=0, keepdims=True)
        dgs_ref[...] += jnp.sum(dgs, axis=0, keepdims=True)

    return _mm_rows(dh1b, w, "nt", name,
                    [(co, "rows"), (y, "rows"), (z, "rows"), (gc, "whole"), (gs, "whole")],
                    [((tp, dh), F32, "rows"), ((tp, dh), BF16, "rows"), ((tp, dh), F32, "rows"),
                     ((1, dh), F32, "whole"), ((1, dh), F32, "whole")], epilogue)


def _shift_down(x, k):
    row = lax.broadcasted_iota(jnp.int32, x.shape, 0)
    return jnp.where(row >= k, pltpu.roll(x, k, 0), 0.0)


def _shift_up(x, k):
    n = x.shape[0]
    row = lax.broadcasted_iota(jnp.int32, x.shape, 0)
    return jnp.where(row < n - k, pltpu.roll(x, n - k, 0), 0.0)


def _dwconv(x, w_ref):
    return w_ref[2:3, :] * x + w_ref[1:2, :] * _shift_down(x, 1) + w_ref[0:1, :] * _shift_down(x, 2)


def _dwconv_bwd(x, dy, w_ref):
    dx = w_ref[2:3, :] * dy + w_ref[1:2, :] * _shift_up(dy, 1) + w_ref[0:1, :] * _shift_up(dy, 2)
    dw = jnp.concatenate([jnp.sum(dy * _shift_down(x, 2), axis=0, keepdims=True),
                          jnp.sum(dy * _shift_down(x, 1), axis=0, keepdims=True),
                          jnp.sum(dy * x, axis=0, keepdims=True)], axis=0)
    return dx, dw


def _interleave(dst, src):
    seg_rows = src.shape[0] // SUBLANES
    for seg in range(SUBLANES):
        dst[pl.ds(seg, seg_rows, stride=SUBLANES), :] = src[seg * seg_rows:(seg + 1) * seg_rows, :]


def _deinterleave(dst, src):
    seg_rows = src.shape[0] // SUBLANES
    for seg in range(SUBLANES):
        dst[seg * seg_rows:(seg + 1) * seg_rows, :] = src[pl.ds(seg, seg_rows, stride=SUBLANES), :]


def _segment_shift(x, reverse):
    row = lax.broadcasted_iota(jnp.int32, x.shape, 0)
    if reverse:
        return jnp.where(row < SUBLANES - 1, pltpu.roll(x, SUBLANES - 1, 0), 0.0)
    return jnp.where(row >= 1, pltpu.roll(x, 1, 0), 0.0)


def _scan(s_re, s_im, pw_ref, reverse, pair=None):
    n_steps = s_re.shape[0] // SUBLANES
    n_strips = s_re.shape[1] // LANES
    sign = -1.0 if reverse else 1.0
    strips = [slice(st * LANES, (st + 1) * LANES) for st in range(n_strips)]

    def rows_of(j):
        step = (n_steps - 1 - j) if reverse else j
        return pl.ds(pl.multiple_of(step * SUBLANES, SUBLANES), SUBLANES)

    a = [(jnp.broadcast_to(pw_ref[0, 0:1, lanes], (SUBLANES, LANES)),
          sign * jnp.broadcast_to(pw_ref[1, 0:1, lanes], (SUBLANES, LANES))) for lanes in strips]

    def local(i, carry):
        for half in range(2):
            rows = rows_of(2 * i + half)
            out = []
            for st, lanes in enumerate(strips):
                (ar, ai), cr, ci = a[st], carry[2 * st], carry[2 * st + 1]
                xr = s_re[rows, lanes] + (ar * cr - ai * ci)
                xi = s_im[rows, lanes] + (ar * ci + ai * cr)
                s_re[rows, lanes] = xr
                s_im[rows, lanes] = xi
                out += [xr, xi]
            carry = tuple(out)
        return carry

    zero = jnp.zeros((SUBLANES, LANES), F32)
    ends = lax.fori_loop(0, n_steps // 2, local, (zero,) * (2 * n_strips))

    entering = []
    row = lax.broadcasted_iota(jnp.int32, (SUBLANES, LANES), 0)
    for st, lanes in enumerate(strips):
        tr, ti = ends[2 * st], ends[2 * st + 1]
        mr = jnp.broadcast_to(pw_ref[0, n_steps - 1:n_steps, lanes], (SUBLANES, LANES))
        mi = sign * jnp.broadcast_to(pw_ref[1, n_steps - 1:n_steps, lanes], (SUBLANES, LANES))
        for k in (1, 2, 4):
            keep = (row < SUBLANES - k) if reverse else (row >= k)
            rr = jnp.where(keep, pltpu.roll(tr, SUBLANES - k if reverse else k, 0), 0.0)
            ri = jnp.where(keep, pltpu.roll(ti, SUBLANES - k if reverse else k, 0), 0.0)
            tr, ti = tr + (mr * rr - mi * ri), ti + (mr * ri + mi * rr)
            mr, mi = mr * mr - mi * mi, 2.0 * mr * mi
        entering += [_segment_shift(tr, reverse), _segment_shift(ti, reverse)]

    def fix(i, carry):
        carry, sums = carry[:2 * n_strips], carry[2 * n_strips:]
        for half in range(2):
            j = 2 * i + half
            rows = rows_of(j)
            out, acc = [], []
            for st, lanes in enumerate(strips):
                (ar, ai), cr, ci = a[st], carry[2 * st], carry[2 * st + 1]
                cr, ci = ar * cr - ai * ci, ar * ci + ai * cr
                xr = s_re[rows, lanes] + cr
                xi = s_im[rows, lanes] + ci
                s_re[rows, lanes] = xr
                s_im[rows, lanes] = xi
                out += [cr, ci]
                if pair is not None:
                    p_rows = rows_of(jnp.minimum(j + 1, n_steps - 1))
                    keep = (j < n_steps - 1).astype(F32)
                    pr = pair[0][p_rows, lanes] * keep
                    pi = pair[1][p_rows, lanes] * keep
                    acc += [sums[2 * st] + (xr * pr + xi * pi), sums[2 * st + 1] + (xi * pr - xr * pi)]
            carry, sums = tuple(out), tuple(acc)
        return carry + sums

    n_sums = 0 if pair is None else 2 * n_strips
    out = lax.fori_loop(0, n_steps // 2, fix, tuple(entering) + (zero,) * n_sums)
    return out[2 * n_strips:]


def _seq_fwd(proj, conv_w, bc_re, bc_im, cc_re, cc_im, dskip, a_pow, name):
    tp = proj.shape[0]
    dh = proj.shape[1] // 4
    nq = dh // LANES
    sw = STATE * N_GROUPS // nq

    def body(b_ref, c_ref, v_ref, u_ref, w_ref, bre_ref, bim_ref, cre_ref, cim_ref, d_ref, pw_ref,
             co_ref, y_ref, g_ref, s_re, s_im, u_il, y_il):
        co_ref[...] = b_ref[...] * _dwconv(c_ref[...] * v_ref[...], w_ref)
        _interleave(u_il, u_ref)
        ub = u_il[...].astype(BF16)
        s_re[...] = jnp.dot(ub, bre_ref[...], preferred_element_type=F32)
        s_im[...] = jnp.dot(ub, bim_ref[...], preferred_element_type=F32)
        _scan(s_re, s_im, pw_ref, False)
        y_il[...] = (jnp.dot(s_re[...].astype(BF16), cre_ref[...], preferred_element_type=F32)
                     - jnp.dot(s_im[...].astype(BF16), cim_ref[...], preferred_element_type=F32))
        _deinterleave(y_ref, y_il)
        y = y_ref[...] + d_ref[...] * u_ref[...]
        y_ref[...] = y
        g_ref[...] = _gelu(y).astype(BF16)

    col = lambda off: pl.BlockSpec((tp, LANES), lambda q, off=off: (0, off * nq + q))
    blk = pl.BlockSpec((tp, LANES), lambda q: (0, q))
    return pl.pallas_call(
        body, name=name, grid=(nq,),
        in_specs=[col(0), col(1), col(2), col(3),
                  pl.BlockSpec((3, LANES), lambda q: (0, q)),
                  pl.BlockSpec((LANES, sw), lambda q: (0, q)), pl.BlockSpec((LANES, sw), lambda q: (0, q)),
                  pl.BlockSpec((sw, LANES), lambda q: (q, 0)), pl.BlockSpec((sw, LANES), lambda q: (q, 0)),
                  pl.BlockSpec((1, LANES), lambda q: (0, q)),
                  pl.BlockSpec((2, tp // SUBLANES, sw), lambda q: (0, 0, q))],
        out_specs=[blk, blk, blk],
        out_shape=[jax.ShapeDtypeStruct((tp, dh), F32), jax.ShapeDtypeStruct((tp, dh), F32),
                   jax.ShapeDtypeStruct((tp, dh), BF16)],
        scratch_shapes=[pltpu.VMEM((tp, sw), F32), pltpu.VMEM((tp, sw), F32),
                        pltpu.VMEM((tp, LANES), F32), pltpu.VMEM((tp, LANES), F32)],
        compiler_params=_cparams("parallel"),
    )(proj, proj, proj, proj, conv_w, bc_re, bc_im, cc_re, cc_im, dskip, a_pow)


def _conv_bwd(proj, dco, conv_w, name):
    tp = proj.shape[0]
    dh = proj.shape[1] // 4
    nq = dh // LANES

    def body(b_ref, c_ref, v_ref, dco_ref, w_ref, dproj_ref, dw_ref, stage, sem):
        q = pl.program_id(0)
        cg = c_ref[...]
        vg = v_ref[...]
        cv = cg * vg
        dco_v = dco_ref[...]
        dcv, dw = _dwconv_bwd(cv, dco_v * b_ref[...], w_ref)
        dw_ref[...] = dw
        stage[0] = (dco_v * _dwconv(cv, w_ref)).astype(BF16)
        stage[1] = (dcv * vg).astype(BF16)
        stage[2] = (dcv * cg).astype(BF16)
        copies = [pltpu.make_async_copy(stage.at[p], dproj_ref.at[:, pl.ds((p * nq + q) * LANES, LANES)], sem.at[p])
                  for p in range(3)]
        for cp in copies:
            cp.start()
        for cp in copies:
            cp.wait()

    col = lambda off: pl.BlockSpec((tp, LANES), lambda q, off=off: (0, off * nq + q))
    return pl.pallas_call(
        body, name=name, grid=(nq,),
        in_specs=[col(0), col(1), col(2), pl.BlockSpec((tp, LANES), lambda q: (0, q)),
                  pl.BlockSpec((3, LANES), lambda q: (0, q))],
        out_specs=[pl.BlockSpec(memory_space=pl.ANY), pl.BlockSpec((3, LANES), lambda q: (0, q))],
        out_shape=[jax.ShapeDtypeStruct((tp, 4 * dh), BF16), jax.ShapeDtypeStruct((3, dh), F32)],
        scratch_shapes=[pltpu.VMEM((3, tp, LANES), BF16), pltpu.SemaphoreType.DMA((3,))],
        compiler_params=_cparams("arbitrary"),
    )(proj, proj, proj, dco, conv_w)


def _ssm_bwd(proj, y, dg, dproj, bc_re, bc_im, cc_re, cc_im, dskip, a_pow, name):
    tp = proj.shape[0]
    dh = proj.shape[1] // 4
    nq = dh // LANES
    sw = STATE * N_GROUPS // nq

    def body(u_ref, y_ref, dg_ref, dproj_in, bre_ref, bim_ref, cre_ref, cim_ref, d_ref, pw_ref,
             dproj_ref, dbre_ref, dbim_ref, dcre_ref, dcim_ref, dd_ref, dar_ref, dai_ref,
             s_re, s_im, l_re, l_im, a_il, b_il, stage, sem):
        del dproj_in
        q = pl.program_id(0)
        nt = (((1,), (1,)), ((), ()))
        tn = (((0,), (0,)), ((), ()))
        _interleave(a_il, u_ref)
        ub = a_il[...].astype(BF16)
        s_re[...] = jnp.dot(ub, bre_ref[...], preferred_element_type=F32)
        s_im[...] = jnp.dot(ub, bim_ref[...], preferred_element_type=F32)
        _scan(s_re, s_im, pw_ref, False)
        dy_rows = dg_ref[...] * _gelu_grad(y_ref[...])
        dd_ref[...] = jnp.sum(dy_rows * u_ref[...], axis=0, keepdims=True)
        _interleave(b_il, dy_rows)
        dy = b_il[...]
        dyb = dy.astype(BF16)
        l_re[...] = lax.dot_general(dyb, cre_ref[...], nt, preferred_element_type=F32)
        l_im[...] = -lax.dot_general(dyb, cim_ref[...], nt, preferred_element_type=F32)
        dcre_ref[...] = lax.dot_general(s_re[...].astype(BF16), dyb, tn, preferred_element_type=F32)
        dcim_ref[...] = -lax.dot_general(s_im[...].astype(BF16), dyb, tn, preferred_element_type=F32)
        sums = _scan(l_re, l_im, pw_ref, True, pair=(s_re, s_im))
        rest = tp - SUBLANES
        for st in range(sw // LANES):
            lanes = slice(st * LANES, (st + 1) * LANES)
            lr0, li0 = l_re[:SUBLANES, lanes], l_im[:SUBLANES, lanes]
            pr0, pi0 = _segment_shift(s_re[rest:, lanes], False), _segment_shift(s_im[rest:, lanes], False)
            dar_ref[:, lanes] = jnp.sum(sums[2 * st] + (lr0 * pr0 + li0 * pi0), axis=0, keepdims=True)
            dai_ref[:, lanes] = jnp.sum(sums[2 * st + 1] + (li0 * pr0 - lr0 * pi0), axis=0, keepdims=True)
        lrb = l_re[...].astype(BF16)
        lib = l_im[...].astype(BF16)
        a_il[...] = (dy * d_ref[...] + lax.dot_general(lrb, bre_ref[...], nt, preferred_element_type=F32)
                     + lax.dot_general(lib, bim_ref[...], nt, preferred_element_type=F32))
        _deinterleave(b_il, a_il)
        stage[...] = b_il[...].astype(BF16)
        dbre_ref[...] = lax.dot_general(ub, lrb, tn, preferred_element_type=F32)
        dbim_ref[...] = lax.dot_general(ub, lib, tn, preferred_element_type=F32)
        cp = pltpu.make_async_copy(stage, dproj_ref.at[:, pl.ds((3 * nq + q) * LANES, LANES)], sem)
        cp.start()
        cp.wait()

    blk = pl.BlockSpec((tp, LANES), lambda q: (0, q))
    bspec = pl.BlockSpec((LANES, sw), lambda q: (0, q))
    cspec = pl.BlockSpec((sw, LANES), lambda q: (q, 0))
    tspec = pl.BlockSpec((2, tp // SUBLANES, sw), lambda q: (0, 0, q))
    nstate = STATE * N_GROUPS
    return pl.pallas_call(
        body, name=name, grid=(nq,),
        in_specs=[pl.BlockSpec((tp, LANES), lambda q: (0, 3 * nq + q)), blk, blk, pl.BlockSpec(memory_space=pl.ANY),
                  bspec, bspec, cspec, cspec, pl.BlockSpec((1, LANES), lambda q: (0, q)), tspec],
        out_specs=[pl.BlockSpec(memory_space=pl.ANY), bspec, bspec, cspec, cspec,
                   pl.BlockSpec((1, LANES), lambda q: (0, q)),
                   pl.BlockSpec((1, sw), lambda q: (0, q)), pl.BlockSpec((1, sw), lambda q: (0, q))],
        out_shape=[jax.ShapeDtypeStruct((tp, 4 * dh), BF16),
                   jax.ShapeDtypeStruct((LANES, nstate), F32), jax.ShapeDtypeStruct((LANES, nstate), F32),
                   jax.ShapeDtypeStruct((nstate, LANES), F32), jax.ShapeDtypeStruct((nstate, LANES), F32),
                   jax.ShapeDtypeStruct((1, dh), F32),
                   jax.ShapeDtypeStruct((1, nstate), F32), jax.ShapeDtypeStruct((1, nstate), F32)],
        input_output_aliases={3: 0},
        scratch_shapes=[pltpu.VMEM((tp, sw), F32)] * 4 + [pltpu.VMEM((tp, LANES), F32)] * 2
        + [pltpu.VMEM((tp, LANES), BF16), pltpu.SemaphoreType.DMA],
        compiler_params=_cparams("arbitrary"),
    )(proj, y, dg, dproj, bc_re, bc_im, cc_re, cc_im, dskip, a_pow)


FFN_TILE = 256


def _ffn_act(up, fw, fb, name):
    tp, two_ff = up.shape
    dff = two_ff // 2
    tc = FFN_TILE
    nj = dff // tc

    def body(ua_ref, uv_ref, wa_ref, wv_ref, ba_ref, bv_ref, act_ref):
        a = _dwconv(ua_ref[...], wa_ref) + ba_ref[...]
        v = _dwconv(uv_ref[...], wv_ref) + bv_ref[...]
        act_ref[...] = (a * _sigmoid(a) * v).astype(BF16)

    lo = lambda r: pl.BlockSpec((r, tc), lambda j: (0, j))
    hi = lambda r: pl.BlockSpec((r, tc), lambda j: (0, nj + j))
    return pl.pallas_call(
        body, name=name, grid=(nj,),
        in_specs=[lo(tp), hi(tp), lo(3), hi(3), lo(1), hi(1)],
        out_specs=lo(tp),
        out_shape=jax.ShapeDtypeStruct((tp, dff), BF16),
        compiler_params=_cparams("parallel"))(up, up, fw, fw, fb, fb)


def _ffn_bwd(up, dact, fw, fb, name):
    tp, two_ff = up.shape
    dff = two_ff // 2
    tc = FFN_TILE
    nj = dff // tc

    def body(ua_ref, uv_ref, da_ref, wa_ref, wv_ref, ba_ref, bv_ref,
             dup_ref, dwa_ref, dwv_ref, dba_ref, dbv_ref, stage, sem):
        j = pl.program_id(0)
        ua = ua_ref[...]
        uv = uv_ref[...]
        a = _dwconv(ua, wa_ref) + ba_ref[...]
        v = _dwconv(uv, wv_ref) + bv_ref[...]
        sg = _sigmoid(a)
        dact_v = da_ref[...]
        da = dact_v * v * sg * (1.0 + a * (1.0 - sg))
        dv = dact_v * a * sg
        dba_ref[...] = jnp.sum(da, axis=0, keepdims=True)
        dbv_ref[...] = jnp.sum(dv, axis=0, keepdims=True)
        dua, dwa = _dwconv_bwd(ua, da, wa_ref)
        duv, dwv = _dwconv_bwd(uv, dv, wv_ref)
        dwa_ref[...] = dwa
        dwv_ref[...] = dwv
        stage[0] = dua.astype(BF16)
        stage[1] = duv.astype(BF16)
        copies = [pltpu.make_async_copy(stage.at[p], dup_ref.at[:, pl.ds((p * nj + j) * tc, tc)], sem.at[p])
                  for p in range(2)]
        for cp in copies:
            cp.start()
        for cp in copies:
            cp.wait()

    lo = lambda r: pl.BlockSpec((r, tc), lambda j: (0, j))
    hi = lambda r: pl.BlockSpec((r, tc), lambda j: (0, nj + j))
    return pl.pallas_call(
        body, name=name, grid=(nj,),
        in_specs=[lo(tp), hi(tp), lo(tp), lo(3), hi(3), lo(1), hi(1)],
        out_specs=[pl.BlockSpec(memory_space=pl.ANY), lo(3), lo(3), lo(1), lo(1)],
        out_shape=[jax.ShapeDtypeStruct((tp, two_ff), BF16),
                   jax.ShapeDtypeStruct((3, dff), F32), jax.ShapeDtypeStruct((3, dff), F32),
                   jax.ShapeDtypeStruct((1, dff), F32), jax.ShapeDtypeStruct((1, dff), F32)],
        scratch_shapes=[pltpu.VMEM((2, tp, tc), BF16), pltpu.SemaphoreType.DMA((2,))],
        compiler_params=_cparams("arbitrary"))(up, up, dact, fw, fw, fb, fb)


def _zoh(lr, li, ld):
    dt = jnp.exp(ld)
    mag = jnp.exp(lr * dt)
    ang = li * dt
    ar = mag * jnp.cos(ang)
    ai = mag * jnp.sin(ang)
    den = lr * lr + li * li
    nr = ar - 1.0
    fr = (nr * lr + ai * li) / den
    fi = (ai * lr - nr * li) / den
    return dt, ar, ai, den, nr, fr, fi


def _s5_prep(lr, li, ld, b_re, b_im, n_pow, name):
    nstate = lr.shape[1]

    def body(lr_ref, li_ref, ld_ref, bre_ref, bim_ref, pw_ref, bcre_ref, bcim_ref):
        _, ar, ai, _, _, fr, fi = _zoh(lr_ref[...], li_ref[...], ld_ref[...])
        bre = bre_ref[...]
        bim = bim_ref[...]
        bcre_ref[...] = (fr * bre - fi * bim).astype(BF16)
        bcim_ref[...] = (fr * bim + fi * bre).astype(BF16)
        row = lax.broadcasted_iota(jnp.int32, (SUBLANES, nstate), 0)
        pr, pi = jnp.zeros((SUBLANES, nstate), F32), jnp.zeros((SUBLANES, nstate), F32)
        cr, ci = ar, ai
        for t in range(SUBLANES):
            pr, pi = jnp.where(row == t, cr, pr), jnp.where(row == t, ci, pi)
            cr, ci = cr * ar - ci * ai, cr * ai + ci * ar
        pw_ref[0, 0:SUBLANES, :] = pr
        pw_ref[1, 0:SUBLANES, :] = pi
        n = SUBLANES
        while n < n_pow:
            m = min(n, n_pow - n)
            tr, ti = pw_ref[0, n - 1:n, :], pw_ref[1, n - 1:n, :]
            xr, xi = pw_ref[0, 0:m, :], pw_ref[1, 0:m, :]
            pw_ref[0, n:n + m, :] = xr * tr - xi * ti
            pw_ref[1, n:n + m, :] = xr * ti + xi * tr
            n += m

    vmem = pl.BlockSpec(memory_space=pltpu.VMEM)
    return pl.pallas_call(
        body, name=name, in_specs=[vmem] * 5, out_specs=[vmem] * 3,
        out_shape=[jax.ShapeDtypeStruct((2, n_pow, nstate), F32)] + [jax.ShapeDtypeStruct(b_re.shape, BF16)] * 2,
        compiler_params=pltpu.CompilerParams(vmem_limit_bytes=VMEM_LIMIT))(lr, li, ld, b_re, b_im)


def _s5_prep_bwd(lr, li, ld, b_re, b_im, da_re, da_im, dbc_re, dbc_im, name):
    def body(lr_ref, li_ref, ld_ref, bre_ref, bim_ref, dar_ref, dai_ref, dbcre_ref, dbcim_ref,
             dlr_ref, dli_ref, dld_ref, dbre_ref, dbim_ref):
        lr, li = lr_ref[...], li_ref[...]
        dt, ar, ai, den, nr, fr, fi = _zoh(lr, li, ld_ref[...])
        bre, bim = bre_ref[...], bim_ref[...]
        gre, gim = dbcre_ref[...], dbcim_ref[...]
        dbre_ref[...] = fr * gre + fi * gim
        dbim_ref[...] = fr * gim - fi * gre
        g_fr = jnp.sum(gre * bre + gim * bim, axis=0, keepdims=True)
        g_fi = jnp.sum(gim * bre - gre * bim, axis=0, keepdims=True)
        g_ar = dar_ref[...] + (g_fr * lr - g_fi * li) / den
        g_ai = dai_ref[...] + (g_fr * li + g_fi * lr) / den
        d_lr = (g_fr * (nr - 2.0 * fr * lr) + g_fi * (ai - 2.0 * fi * lr)) / den
        d_li = (g_fr * (ai - 2.0 * fr * li) - g_fi * (nr + 2.0 * fi * li)) / den
        g_logmag = g_ar * ar + g_ai * ai
        g_ang = g_ai * ar - g_ar * ai
        dlr_ref[...] = d_lr + g_logmag * dt
        dli_ref[...] = d_li + g_ang * dt
        d_ld = (g_logmag * lr + g_ang * li) * dt
        n = d_ld.shape[1]
        sh = 1
        while sh < STATE:
            d_ld = d_ld + pltpu.roll(d_ld, n - sh, 1)
            sh *= 2
        dld_ref[...] = d_ld

    vmem = pl.BlockSpec(memory_space=pltpu.VMEM)
    row = jax.ShapeDtypeStruct(lr.shape, F32)
    return pl.pallas_call(
        body, name=name, in_specs=[vmem] * 9, out_specs=[vmem] * 5,
        out_shape=[row, row, row, jax.ShapeDtypeStruct(b_re.shape, F32), jax.ShapeDtypeStruct(b_re.shape, F32)],
    )(lr, li, ld, b_re, b_im, da_re, da_im, dbc_re, dbc_im)


def _compact_b(bb):
    bq = bb.reshape(N_GROUPS // 8, 8, STATE, GROUP)
    m = jnp.einsum("ab,qbph->qahbp", jnp.eye(8, dtype=bb.dtype), bq).reshape(N_GROUPS // 8, LANES, 8 * STATE)
    return m.transpose(1, 0, 2).reshape(LANES, N_GROUPS * STATE)


def _expand_b(m):
    d = m.reshape(8, GROUP, N_GROUPS // 8, 8, STATE)
    return jnp.einsum("ahqap->qahp", d).reshape(N_GROUPS, GROUP, STATE)


def _compact_c(c):
    cq = c.reshape(N_GROUPS // 8, 8, GROUP, STATE)
    return jnp.einsum("ab,qbhp->qbpah", jnp.eye(8, dtype=c.dtype), cq).reshape(N_GROUPS * STATE, LANES)


def _expand_c(m):
    d = m.reshape(N_GROUPS // 8, 8, STATE, 8, GROUP)
    return jnp.einsum("qbpbh->qbhp", d).reshape(N_GROUPS, GROUP, STATE)


def _local_step(x, target, p, ex):
    seq, d = x.shape
    n_real = N_META + seq
    tp = -(-n_real // ROW_ALIGN) * ROW_ALIGN

    h0, hn1 = _input_norm_fwd(x, p["meta_tokens"], p["norm_mix_g"] + ex.zero, tp, "norm_mix")
    ex.forward("first", hn1)
    nstate = N_GROUPS * STATE
    s5 = (p["ssm_lam_re"].reshape(1, nstate), p["ssm_lam_im"].reshape(1, nstate),
          jnp.repeat(p["ssm_log_dt"].reshape(-1), STATE).reshape(1, nstate),
          _compact_b(p["ssm_b_re"]), _compact_b(p["ssm_b_im"]))
    a_pow, bc_re, bc_im = _s5_prep(*s5, tp // SUBLANES, "s5_prep")
    cc_re = _compact_c(p["ssm_c_re"]).astype(BF16)
    cc_im = _compact_c(p["ssm_c_im"]).astype(BF16)
    dskip = p["ssm_d"].reshape(1, -1)
    first = ex.weights("first", bc_re)
    proj = _mm(hn1, first["w_in"], "nn", "proj")
    started = ex.forward("mid", proj)
    co, y, g = _seq_fwd(proj, p["conv_w"] + started[0, 0], bc_re, bc_im, cc_re, cc_im, dskip, a_pow, "seq_fwd")
    mid = ex.weights("mid", g)
    z = _mm(g, mid["ssm_w_glu"], "nn", "glu")
    mixed = _mix_fwd(co, y, z, p["gain_conv_out"], p["gain_ssm_out"], "mix_fwd")
    started = ex.forward("up", mixed)
    h1, hn2 = _proj_res_norm(mixed, mid["w_out"], h0, p["norm_ffn_g"], started, "out_proj_norm")
    late = ex.weights("up", hn2)
    up = _mm(hn2, late["w_up"], "nn", "up_proj")
    started = ex.forward("down", up)
    act = _ffn_act(up, p["ffn_conv_w"] + started[0, 0], p["ffn_conv_b"], "ffn_act")
    late.update(ex.weights("down", act))
    loss, dh2, dh2b, d_gfin = _proj_loss_bwd(act, late["w_down"], h1, target, p["norm_final_g"], n_real,
                                             "down_proj_loss")

    g_w_down = _mm(act, dh2b, "tn", "g_w_down")
    dact = _mm(dh2b, late["w_down"], "nt", "d_act")
    dup, dfw_a, dfw_v, dfb_a, dfb_v = _ffn_bwd(up, dact, p["ffn_conv_w"], p["ffn_conv_b"], "ffn_bwd")
    g_w_up = _mm(hn2, dup, "tn", "g_w_up")
    started = ex.grads_ready("late", {"w_up": g_w_up, "w_down": g_w_down})
    dh1, dh1b, d_gffn = _proj_norm_bwd(dup, late["w_up"], h1, p["norm_ffn_g"], dh2, started, "d_hn2_norm_bwd")
    started = ex.grads_send("late", dh1)
    g_w_out = _mm(mixed, dh1b, "tn", "g_w_out", after=started)
    dco, dz, dgp, d_gc, d_gs = _proj_mix_bwd(dh1b, mid["w_out"], co, y, z, p["gain_conv_out"],
                                             p["gain_ssm_out"], "d_mixed_mix_bwd")
    g_w_glu = _mm(g, dz, "tn", "g_w_glu")
    started = ex.grads_ready("mid", {"ssm_w_glu": g_w_glu, "w_out": g_w_out})
    dg = _mm(dz, mid["ssm_w_glu"], "nt", "d_gelu", acc_in=dgp, after=started)
    started = ex.grads_send("mid", dg)
    dproj, d_conv_w = _conv_bwd(proj, dco, p["conv_w"] + started[0, 0], "conv_bwd")
    (dproj, dbc_re, dbc_im, dcc_re, dcc_im, d_dskip, da_re, da_im) = _ssm_bwd(
        proj, y, dg, dproj, bc_re, bc_im, cc_re, cc_im, dskip, a_pow, "ssm_bwd")
    g_w_in = _mm(hn1, dproj, "tn", "g_w_in")
    started = ex.grads_ready("first", {"w_in": g_w_in})
    dhn1 = _mm(dproj, first["w_in"], "nt", "d_hn1", after=started)
    started = ex.grads_send("first", dhn1)
    grad_x, d_meta, d_gmix = _input_norm_bwd(h0, p["norm_mix_g"] + started[0, 0], dhn1, dh1, n_real, "norm_mix_bwd")

    d_lam_re, d_lam_im, d_log_dt, d_b_re, d_b_im = _s5_prep_bwd(*s5, da_re, da_im, dbc_re, dbc_im, "s5_prep_bwd")
    d_lam_re, d_lam_im = d_lam_re.reshape(N_GROUPS, STATE), d_lam_im.reshape(N_GROUPS, STATE)
    d_log_dt = d_log_dt[0, ::STATE]
    d_b_re, d_b_im = _expand_b(d_b_re), _expand_b(d_b_im)
    grads = {
        "meta_tokens": d_meta, "norm_mix_g": d_gmix, "w_in": g_w_in, "conv_w": d_conv_w,
        "ssm_lam_re": d_lam_re, "ssm_lam_im": d_lam_im, "ssm_log_dt": d_log_dt,
        "ssm_b_re": d_b_re, "ssm_b_im": d_b_im, "ssm_c_re": _expand_c(dcc_re), "ssm_c_im": _expand_c(dcc_im),
        "ssm_d": d_dskip.reshape(N_GROUPS, GROUP), "ssm_w_glu": g_w_glu,
        "gain_conv_out": d_gc, "gain_ssm_out": d_gs, "w_out": g_w_out, "norm_ffn_g": d_gffn,
        "w_up": g_w_up, "ffn_conv_w": jnp.concatenate([dfw_a, dfw_v], axis=1),
        "ffn_conv_b": jnp.concatenate([dfb_a, dfb_v], axis=1), "w_down": g_w_down, "norm_final_g": d_gfin,
    }
    return loss[0, 0], grad_x, grads


def _view(ref, axis, start, size):
    idx = [slice(None)] * len(ref.shape)
    idx[axis] = pl.ds(start, size)
    return ref.at[tuple(idx)]


def _exchange(name, ins, outs, aliases, local_copies, remote_copies):
    ni, no = len(ins), len(outs)
    nl, nr = len(local_copies), len(remote_copies)

    def body(*refs):
        in_refs, out_refs = refs[:ni], refs[ni:ni + no]
        send_sems, recv_sems, local_sems = refs[ni + no:]
        x, y, c = lax.axis_index("x"), lax.axis_index("y"), lax.axis_index("c")
        pos = (x, y, c, 2 * x + y)
        locals_ = [pltpu.make_async_copy(s(in_refs, out_refs, pos), d(in_refs, out_refs, pos), local_sems.at[i])
                   for i, (s, d) in enumerate(local_copies)]
        remotes = []
        for i, (s, d, flip) in enumerate(remote_copies):
            peer = (1 - x if "x" in flip else x, 1 - y if "y" in flip else y, 1 - c if "c" in flip else c)
            remotes.append(pltpu.make_async_remote_copy(
                src_ref=s(in_refs, out_refs, pos), dst_ref=d(in_refs, out_refs, pos),
                send_sem=send_sems.at[i], recv_sem=recv_sems.at[i], device_id=peer, device_id_type=MESH))
        for cp in locals_ + remotes:
            cp.start()
        for cp in remotes:
            cp.wait_recv()
        for cp in remotes:
            cp.wait_send()
        for cp in locals_:
            cp.wait()

    hbm = pl.BlockSpec(memory_space=pl.ANY)
    return pl.pallas_call(
        body, name=name, in_specs=[hbm] * ni, out_specs=[hbm] * no, out_shape=outs,
        input_output_aliases=aliases,
        scratch_shapes=[pltpu.SemaphoreType.DMA((nr,)), pltpu.SemaphoreType.DMA((nr,)),
                        pltpu.SemaphoreType.DMA((max(nl, 1),))],
    )(*ins)


BIG = {"w_in": (0, 1), "ssm_w_glu": (1, 0), "w_out": (1, 0), "w_up": (0, 1), "w_down": (1, 0)}
BIG_NAMES = tuple(BIG)
FLIPS = ("y", "x", "xy")


def _peer_chip(pos, flip):
    x, y, _, _ = pos
    return 2 * (1 - x if "x" in flip else x) + (1 - y if "y" in flip else y)


def _block_rows(rows, cols, itemsize, mult):
    return _pick_tile(rows, max(mult, (2 * 1024 * 1024) // (cols * itemsize)), mult)


def _cast_into_full(w, kc, shard_axis, name):
    r, cdim = w.shape
    tr = _block_rows(r, cdim, 4, 16)
    nb = r // tr

    def body(kc_ref, w_ref, o_ref):
        o_ref[...] = w_ref[...].astype(BF16)

    if shard_axis == 1:
        full, o_spec = (r, 4 * cdim), pl.BlockSpec((tr, cdim), lambda i, kc: (i, kc[0]))
    else:
        full, o_spec = (4 * r, cdim), pl.BlockSpec((tr, cdim), lambda i, kc: (kc[0] * nb + i, 0))
    return pl.pallas_call(
        body, name=name,
        grid_spec=pltpu.PrefetchScalarGridSpec(
            num_scalar_prefetch=1, grid=(nb,), in_specs=[pl.BlockSpec((tr, cdim), lambda i, kc: (i, 0))],
            out_specs=o_spec),
        out_shape=jax.ShapeDtypeStruct(full, BF16), compiler_params=_cparams("parallel"))(kc, w)


def _pair_sum(g, recv, kc, half_axis, name, out_dtype):
    hr, hc = recv.shape
    tr = _block_rows(hr, hc, 4, 16)
    nb = hr // tr

    def body(kc_ref, g_ref, r_ref, o_ref):
        o_ref[...] = (g_ref[...] + r_ref[...]).astype(out_dtype)

    if half_axis == 0:
        g_spec = pl.BlockSpec((tr, hc), lambda i, kc: (kc[1] * nb + i, 0))
    elif half_axis == 1:
        g_spec = pl.BlockSpec((tr, hc), lambda i, kc: (i, kc[1]))
    else:
        g_spec = pl.BlockSpec((tr, hc), lambda i, kc: (i, 0))
    same = pl.BlockSpec((tr, hc), lambda i, kc: (i, 0))
    return pl.pallas_call(
        body, name=name,
        grid_spec=pltpu.PrefetchScalarGridSpec(num_scalar_prefetch=1, grid=(nb,), in_specs=[g_spec, same],
                                               out_specs=same),
        out_shape=jax.ShapeDtypeStruct((hr, hc), out_dtype), compiler_params=_cparams("parallel"))(kc, g, recv)


def _chip_sum(own, recv, kc, own_axis, out_axis, name):
    _, sr, sc = recv.shape
    tr = _block_rows(sr, sc, 4, 16)
    nb = sr // tr

    def body(kc_ref, o_ref, r_ref, t_ref):
        k = kc_ref[0]
        own_v = o_ref[...].astype(F32)
        r = [r_ref[m].astype(F32) for m in range(3)]
        terms = []
        for kk in range(4):
            m = jnp.bitwise_xor(k, kk)
            terms.append(jnp.where(m == 0, own_v, jnp.where(m == 1, r[0], jnp.where(m == 2, r[1], r[2]))))
        t_ref[...] = (terms[0] + terms[1]) + (terms[2] + terms[3])

    if own_axis == 0:
        own_spec = pl.BlockSpec((tr, sc), lambda i, kc: (kc[0] * nb + i, 0))
    elif own_axis == 1:
        own_spec = pl.BlockSpec((tr, sc), lambda i, kc: (i, kc[0]))
    else:
        own_spec = pl.BlockSpec((tr, sc), lambda i, kc: (kc[1] * nb + i, 0))
    if out_axis == 0:
        out_full, out_spec = (2 * sr, sc), pl.BlockSpec((tr, sc), lambda i, kc: (kc[1] * nb + i, 0))
    else:
        out_full, out_spec = (sr, 2 * sc), pl.BlockSpec((tr, sc), lambda i, kc: (i, kc[1]))
    return pl.pallas_call(
        body, name=name,
        grid_spec=pltpu.PrefetchScalarGridSpec(
            num_scalar_prefetch=1, grid=(nb,),
            in_specs=[own_spec, pl.BlockSpec((3, tr, sc), lambda i, kc: (0, i, 0))],
            out_specs=out_spec),
        out_shape=jax.ShapeDtypeStruct(out_full, F32), compiler_params=_cparams("parallel"))(kc, own, recv)


def _adamw(w, g, m, v, name):
    r, cdim = w.shape
    tr = _block_rows(r, cdim, 4, 8)
    c1 = 1.0 - ADAM_B1 ** ADAM_STEP
    c2 = 1.0 - ADAM_B2 ** ADAM_STEP

    def body(w_ref, g_ref, m_ref, v_ref, go_ref, d_ref, nm_ref, nv_ref):
        gv = g_ref[...]
        go_ref[...] = gv
        nm = ADAM_B1 * m_ref[...] + (1.0 - ADAM_B1) * gv
        nv = ADAM_B2 * v_ref[...] + (1.0 - ADAM_B2) * (gv * gv)
        d_ref[...] = -ADAM_LR * ((nm / c1) / (jnp.sqrt(nv / c2) + ADAM_EPS) + ADAM_WD * w_ref[...])
        nm_ref[...] = nm
        nv_ref[...] = nv

    spec = _rows(cdim, tr)
    return pl.pallas_call(body, name=name, grid=(r // tr,), in_specs=[spec] * 4, out_specs=[spec] * 4,
                          out_shape=[jax.ShapeDtypeStruct((r, cdim), F32)] * 4,
                          compiler_params=_cparams("parallel"))(w, g, m, v)


def _adamw_whole(ws, gs, ms, vs, name):
    n = len(ws)
    c1 = 1.0 - ADAM_B1 ** ADAM_STEP
    c2 = 1.0 - ADAM_B2 ** ADAM_STEP

    def body(*refs):
        for i in range(n):
            w_ref, g_ref, m_ref, v_ref, d_ref, nm_ref, nv_ref = [refs[j * n + i] for j in range(7)]
            gv = g_ref[...]
            nm = ADAM_B1 * m_ref[...] + (1.0 - ADAM_B1) * gv
            nv = ADAM_B2 * v_ref[...] + (1.0 - ADAM_B2) * (gv * gv)
            d_ref[...] = -ADAM_LR * ((nm / c1) / (jnp.sqrt(nv / c2) + ADAM_EPS) + ADAM_WD * w_ref[...])
            nm_ref[...] = nm
            nv_ref[...] = nv

    vmem = pl.BlockSpec(memory_space=pltpu.VMEM)
    out = pl.pallas_call(body, name=name, in_specs=[vmem] * (4 * n), out_specs=[vmem] * (3 * n),
                         out_shape=[jax.ShapeDtypeStruct(a.shape, F32) for a in ws] * 3,
                         compiler_params=pltpu.CompilerParams(vmem_limit_bytes=VMEM_LIMIT))(*ws, *gs, *ms, *vs)
    return out[:n], out[n:2 * n], out[2 * n:]


SIDE_EFFECT = pltpu.SideEffectType.DATAFLOW_SIDE_EFFECTING


def _descriptors(copies, refs, send_sems, recv_sems, sem_off=0):
    x, y, c = lax.axis_index("x"), lax.axis_index("y"), lax.axis_index("c")
    pos = (x, y, c, 2 * x + y)
    out = []
    for i, (s, d, flip) in enumerate(copies):
        peer = (1 - x if "x" in flip else x, 1 - y if "y" in flip else y, 1 - c if "c" in flip else c)
        out.append(pltpu.make_async_remote_copy(
            src_ref=s(refs, refs, pos), dst_ref=d(refs, refs, pos),
            send_sem=send_sems.at[sem_off + i], recv_sem=recv_sems.at[sem_off + i],
            device_id=peer, device_id_type=MESH))
    return out


def _shifted(copies, off):
    return [(lambda I, O, pos, s=s: s(I[off:], O[off:], pos), lambda I, O, pos, d=d: d(I[off:], O[off:], pos), flip)
            for s, d, flip in copies]


BARRIER_IDS = {"c": (1, 2), "ici": (3, 4)}


def _exchange_start(name, bufs, copies, turns, after=None):
    n, nr = len(bufs), len(copies)
    na = 0 if after is None else 1
    flips = sorted({flip for _, _, flip in copies})
    kind = "c" if flips == ["c"] else "ici"
    collective_id = BARRIER_IDS[kind][turns[kind] % 2]
    turns[kind] += 1

    def body(*refs):
        x, y, c = lax.axis_index("x"), lax.axis_index("y"), lax.axis_index("c")
        barrier = pltpu.get_barrier_semaphore()
        for flip in flips:
            peer = (1 - x if "x" in flip else x, 1 - y if "y" in flip else y, 1 - c if "c" in flip else c)
            pl.semaphore_signal(barrier, inc=1, device_id=peer, device_id_type=MESH)
        pl.semaphore_wait(barrier, len(flips))
        for cp in _descriptors(copies, refs[:n], refs[n + na], refs[n + na + 1]):
            cp.start()
        token = refs[2 * n + na + 2]
        token[...] = jnp.zeros_like(token)

    hbm = pl.BlockSpec(memory_space=pltpu.HBM)
    sem = pl.BlockSpec(memory_space=pltpu.SEMAPHORE)
    out = pl.pallas_call(
        body, name=name,
        in_specs=[hbm] * n + [pl.BlockSpec(memory_space=pl.ANY)] * na,
        out_specs=(sem, sem, *[hbm] * n, pl.BlockSpec(memory_space=pltpu.VMEM)),
        out_shape=(pltpu.SemaphoreType.DMA((nr,)), pltpu.SemaphoreType.DMA((nr,)),
                   *[pltpu.HBM(b.shape, b.dtype) for b in bufs], jax.ShapeDtypeStruct((SUBLANES, LANES), F32)),
        input_output_aliases={i: 2 + i for i in range(n)},
        compiler_params=pltpu.CompilerParams(has_side_effects=SIDE_EFFECT, collective_id=collective_id),
    )(*[pltpu.with_memory_space_constraint(b, pltpu.HBM) for b in bufs], *([after] * na))
    return out[0], out[1], list(out[2:2 + n]), out[2 + n]


def _exchange_wait(name, send_sems, recv_sems, bufs, copies, after, sem_off=0):
    n = len(bufs)

    def body(*refs):
        for cp in _descriptors(copies, refs[:n], refs[n], refs[n + 1], sem_off):
            cp.wait_send()
            cp.wait_recv()

    hbm = pl.BlockSpec(memory_space=pltpu.HBM)
    sem = pl.BlockSpec(memory_space=pltpu.SEMAPHORE)
    out = pl.pallas_call(
        body, name=name,
        in_specs=[hbm] * n + [sem, sem, pl.BlockSpec(memory_space=pl.ANY)],
        out_specs=tuple([hbm] * n),
        out_shape=tuple(pltpu.HBM(b.shape, b.dtype) for b in bufs),
        input_output_aliases={i: i for i in range(n)},
        compiler_params=pltpu.CompilerParams(has_side_effects=SIDE_EFFECT),
    )(*bufs, send_sems, recv_sems, after)
    return list(out)


FIRST = ("w_in",)
MID = ("ssm_w_glu", "w_out")
LATE = ("w_up", "w_down")
GROUPS = {"first": FIRST, "mid": MID, "late": LATE}
ARRIVALS = {"first": FIRST, "mid": MID, "up": ("w_up",), "down": ("w_down",)}


def _gather_copies(names, shard_shapes):
    def region(i, chip, c):
        half_axis, shard_axis = BIG[names[i]]
        ssize = shard_shapes[i][shard_axis]
        hsize = shard_shapes[i][half_axis] // 2
        return lambda ref: _view(_view(ref, shard_axis, chip * ssize, ssize), half_axis, c * hsize, hsize)

    ici, d2d = [], []
    for i in range(len(names)):
        for flip in FLIPS:
            ici.append((lambda I, O, pos, i=i: region(i, pos[3], pos[2])(I[i]),
                        lambda I, O, pos, i=i: region(i, pos[3], pos[2])(O[i]), flip))
            d2d.append((lambda I, O, pos, i=i, flip=flip: region(i, _peer_chip(pos, flip), pos[2])(I[i]),
                        lambda I, O, pos, i=i, flip=flip: region(i, _peer_chip(pos, flip), pos[2])(O[i]), "c"))
    return ici, d2d


def _half_shape(n, shape):
    r, cdim = shape
    return (r // 2, cdim) if BIG[n][0] == 0 else (r, cdim // 2)


def _sub_shape(n, shape):
    hr, hc = _half_shape(n, shape)
    return (hr, hc // 4) if BIG[n][1] == 1 else (hr // 4, hc)


def _pair_copies(names, shapes, with_pack, dst_off):
    n = len(names)

    def other_half(i, ref, pos):
        half_axis = BIG[names[i]][0]
        hsize = shapes[i][half_axis] // 2
        return _view(ref, half_axis, (1 - pos[2]) * hsize, hsize)

    copies = [(lambda I, O, pos, i=i: other_half(i, I[i], pos), lambda I, O, pos, i=i: O[dst_off + i], "c")
              for i in range(n)]
    if with_pack:
        copies.append((lambda I, O, pos: I[n], lambda I, O, pos: O[dst_off + n], "c"))
    return copies


def _chip_copies(names, shapes, pack_rows, dst_off):
    n = len(names)

    def piece(i, ref, chip):
        shard_axis = BIG[names[i]][1]
        ssize = _sub_shape(names[i], shapes[i])[shard_axis]
        return _view(ref, shard_axis, chip * ssize, ssize)

    copies = []
    for i in range(n):
        for slot, flip in enumerate(FLIPS):
            copies.append((lambda I, O, pos, i=i, flip=flip: piece(i, I[i], _peer_chip(pos, flip)),
                           lambda I, O, pos, i=i, slot=slot: O[dst_off + i].at[slot], flip))
    if pack_rows:
        for slot, flip in enumerate(FLIPS):
            copies.append((lambda I, O, pos: _view(I[n], 0, pos[2] * (pack_rows // 2), pack_rows // 2),
                           lambda I, O, pos, slot=slot: O[dst_off + n].at[slot], flip))
    return copies


class _Exchanges:
    def __init__(self, shards, tiny, kc):
        self.kc = kc
        wb = {n: _cast_into_full(shards[n], kc, BIG[n][1], "cast_" + n) for n in BIG_NAMES}
        self.gathering, self.forwarding, self.pairing, self.reducing = {}, {}, {}, {}
        self.turns = {"c": 0, "ici": 0}
        tiny_copies = [(lambda I, O, pos: I[0], lambda I, O, pos: O[1].at[pos[3]], flip) for flip in FLIPS]
        self.gathering["tiny"] = (0, 0, 2, tiny_copies, None)
        bufs, copies = [tiny, lax.empty((4,) + tiny.shape, F32)], list(tiny_copies)
        for group, names in ARRIVALS.items():
            ici, d2d = _gather_copies(names, [shards[n].shape for n in names])
            self.gathering[group] = (len(bufs), len(copies), len(names), ici, d2d)
            copies += _shifted(ici, len(bufs))
            bufs += [wb[n] for n in names]
        self.started = _exchange_start("gather_start", bufs, copies, self.turns)
        self.zero = self.started[3][0, 0]

    def _arrived(self, group, after):
        buf_off, sem_off, n, ici, _ = self.gathering[group]
        send_sems, recv_sems, bufs, _ = self.started
        return _exchange_wait("gather_%s_wait" % group, send_sems, recv_sems, bufs[buf_off:buf_off + n], ici, after,
                              sem_off)

    def small_params(self, kc):
        tiny, got = self._arrived("tiny", self.started[3])
        return lax.dynamic_update_index_in_dim(got, tiny, kc[0], 0)

    def forward(self, group, after):
        d2d = self.gathering[group][4]
        self.forwarding[group] = (_exchange_start("forward_%s_start" % group, self._arrived(group, after), d2d,
                                                  self.turns), d2d)
        return self.forwarding[group][0][3]

    def weights(self, group, after):
        if group not in self.forwarding:
            after = self.forward(group, after)
        (send_sems, recv_sems, bufs, _), d2d = self.forwarding[group]
        full = _exchange_wait("forward_%s_wait" % group, send_sems, recv_sems, bufs, d2d, after)
        return dict(zip(ARRIVALS[group], full))

    def grads_ready(self, group, grads):
        names = GROUPS[group]
        gs = [grads[n] for n in names]
        land = [lax.empty(_half_shape(n, g.shape), F32) for n, g in zip(names, gs)]
        copies = _pair_copies(names, [g.shape for g in gs], False, len(names))
        started = _exchange_start("pair_%s_start" % group, gs + land, copies, self.turns)
        self.pairing[group] = (started, copies)
        return started[3]

    def grads_send(self, group, after):
        names = GROUPS[group]
        n = len(names)
        (send_sems, recv_sems, bufs, _), copies = self.pairing[group]
        bufs = _exchange_wait("pair_%s_wait" % group, send_sems, recv_sems, bufs, copies, after)
        chip = [_pair_sum(bufs[i], bufs[n + i], self.kc, BIG[names[i]][0], "pair_sum_" + names[i], BF16)
                for i in range(n)]
        shapes = [bufs[i].shape for i in range(n)]
        land = [lax.empty((3,) + _sub_shape(names[i], shapes[i]), BF16) for i in range(n)]
        copies = _chip_copies(names, shapes, 0, n)
        started = _exchange_start("reduce_%s_start" % group, chip + land, copies, self.turns)
        self.reducing[group] = (started, copies)
        return started[3]

    def finish_pack(self, pack):
        kc = self.kc
        prow = pack.shape[0] // 2
        recv = _exchange("reduce_d2d", [pack], [jax.ShapeDtypeStruct(pack.shape, F32)], {}, [],
                         _pair_copies((), [], True, 0))
        chip_pack = _pair_sum(pack, recv[0], kc, None, "pair_sum_pack", F32)
        copies = _chip_copies((), [], pack.shape[0], 1)
        land = lax.empty((3, prow, pack.shape[1]), F32)
        pack_sems_s, pack_sems_r, pack_bufs, after = _exchange_start("reduce_pack_start", [chip_pack, land], copies,
                                                                     self.turns)

        names, chips, recvs = (), [], []
        for group, group_names in GROUPS.items():
            (send_sems, recv_sems, bufs, _), group_copies = self.reducing[group]
            bufs = _exchange_wait("reduce_%s_wait" % group, send_sems, recv_sems, bufs, group_copies, after)
            n = len(group_names)
            names, chips, recvs = names + group_names, chips + bufs[:n], recvs + bufs[n:]
            after = bufs[n]
        total = [_chip_sum(chips[i], recvs[i], kc, BIG[n][1], BIG[n][0], "chip_sum_" + n)
                 for i, n in enumerate(names)]

        def my_half(half_axis, ref, pos):
            hsize = ref.shape[half_axis] // 2
            return _view(ref, half_axis, pos[2] * hsize, hsize)

        swap = [(lambda I, O, pos, i=i, n=n: my_half(BIG[n][0], I[i], pos),
                 lambda I, O, pos, i=i, n=n: my_half(BIG[n][0], O[i], pos), "c") for i, n in enumerate(names)]
        self.swapping = (_exchange_start("swap_start", total, swap, self.turns), swap, names)

        chip_pack, recv_pack = _exchange_wait("reduce_pack_wait", pack_sems_s, pack_sems_r, pack_bufs, copies,
                                              self.swapping[0][3])
        total_pack = _chip_sum(chip_pack, recv_pack, kc, None, 0, "chip_sum_pack")
        swap = [(lambda I, O, pos: my_half(0, I[0], pos), lambda I, O, pos: my_half(0, O[0], pos), "c")]
        return _exchange("swap_pack", [total_pack], [jax.ShapeDtypeStruct(pack.shape, F32)], {0: 0}, [], swap)[0]

    def finish_big(self, after):
        (send_sems, recv_sems, bufs, _), swap, names = self.swapping
        return dict(zip(names, _exchange_wait("swap_wait", send_sems, recv_sems, bufs, swap, after)))


WEIGHTS = ("meta_tokens", "norm_mix_g", "w_in", "conv_w", "ssm_lam_re", "ssm_lam_im", "ssm_log_dt", "ssm_b_re",
           "ssm_b_im", "ssm_c_re", "ssm_c_im", "ssm_d", "ssm_w_glu", "gain_conv_out", "gain_ssm_out", "w_out",
           "norm_ffn_g", "w_up", "ffn_conv_w", "ffn_conv_b", "w_down", "norm_final_g")
TINY_SHARDED = ("meta_tokens", "conv_w", "ffn_conv_w")
REPLICATED = tuple(n for n in WEIGHTS if n not in BIG and n not in TINY_SHARDED)
PACK_COLS = 512


def _pack(arrays, row_mult, cols):
    flat = jnp.concatenate([a.reshape(-1).astype(F32) for a in arrays])
    n = flat.shape[0]
    total = -(-n // (row_mult * cols)) * (row_mult * cols)
    return jnp.concatenate([flat, jnp.zeros((total - n,), F32)]).reshape(total // cols, cols)


def _unpack(packed, shapes):
    flat = packed.reshape(-1)
    out, off = [], 0
    for s in shapes:
        n = math.prod(s)
        out.append(flat[off:off + n].reshape(s))
        off += n
    return out


def kernel(x, meta_tokens, norm_mix_g, w_in, conv_w, ssm_lam_re, ssm_lam_im, ssm_log_dt, ssm_b_re, ssm_b_im, ssm_c_re, ssm_c_im, ssm_d, ssm_w_glu, gain_conv_out, gain_ssm_out, w_out, norm_ffn_g, w_up, ffn_conv_w, ffn_conv_b, w_down, norm_final_g, loss_target, m_meta_tokens, m_norm_mix_g, m_w_in, m_conv_w, m_ssm_lam_re, m_ssm_lam_im, m_ssm_log_dt, m_ssm_b_re, m_ssm_b_im, m_ssm_c_re, m_ssm_c_im, m_ssm_d, m_ssm_w_glu, m_gain_conv_out, m_gain_ssm_out, m_w_out, m_norm_ffn_g, m_w_up, m_ffn_conv_w, m_ffn_conv_b, m_w_down, m_norm_final_g, v_meta_tokens, v_norm_mix_g, v_w_in, v_conv_w, v_ssm_lam_re, v_ssm_lam_im, v_ssm_log_dt, v_ssm_b_re, v_ssm_b_im, v_ssm_c_re, v_ssm_c_im, v_ssm_d, v_ssm_w_glu, v_gain_conv_out, v_gain_ssm_out, v_w_out, v_norm_ffn_g, v_w_up, v_ffn_conv_w, v_ffn_conv_b, v_w_down, v_norm_final_g):
    args = dict(locals())
    w = {n: args[n] for n in WEIGHTS}
    mom = {n: args["m_" + n] for n in WEIGHTS}
    var = {n: args["v_" + n] for n in WEIGHTS}
    kx, ky, kc_ = lax.axis_index("x"), lax.axis_index("y"), lax.axis_index("c")
    chip = 2 * kx + ky
    kc = jnp.stack([chip, kc_]).astype(jnp.int32)

    def squeeze(n, a):
        if n == "meta_tokens":
            return a
        if n == "norm_final_g":
            return a.reshape(1, -1)
        a = a[0]
        return a.reshape(1, -1) if a.ndim == 1 else a

    wl = {n: squeeze(n, w[n]) for n in WEIGHTS}
    ml = {n: squeeze(n, mom[n]) for n in WEIGHTS}
    vl = {n: squeeze(n, var[n]) for n in WEIGHTS}

    tiny = _pack([wl[n] for n in TINY_SHARDED], SUBLANES, LANES)
    ex = _Exchanges({n: wl[n] for n in BIG_NAMES}, tiny, kc)
    tiny_shapes = [wl[n].shape for n in TINY_SHARDED]
    tiny_all = ex.small_params(kc)
    tiny_parts = [_unpack(tiny_all[k], tiny_shapes) for k in range(4)]
    p = {n: wl[n] for n in WEIGHTS if n not in BIG}
    for j, n in enumerate(TINY_SHARDED):
        p[n] = jnp.concatenate([tiny_parts[k][j] for k in range(4)], axis=1)
    p["ssm_log_dt"] = wl["ssm_log_dt"].reshape(-1)

    loss_local, grad_x, grads = _local_step(x[0], loss_target[0], p, ex)

    small_names = REPLICATED + TINY_SHARDED
    small_shapes = [tuple(grads[n].shape) for n in small_names] + [(1,)]
    pack = _pack([grads[n] for n in small_names] + [loss_local.reshape(1)], 2 * 16, PACK_COLS)
    g_pack = ex.finish_pack(pack)
    g_small = dict(zip(small_names + ("loss",), _unpack(g_pack, small_shapes)))
    loss = g_small["loss"][0]
    swapped = ("ssm_b_re", "ssm_b_im")

    def view(n, a):
        if n in swapped:
            return jnp.swapaxes(a, -1, -2)
        return a.reshape(1, -1) if a.ndim == 1 else a

    g = {}
    for n in REPLICATED:
        g[n] = g_small[n].reshape(view(n, w[n]).shape)
    for n in TINY_SHARDED:
        cols = wl[n].shape[1]
        g[n] = lax.dynamic_slice_in_dim(g_small[n], chip * cols, cols, axis=1).reshape(w[n].shape)
    delta, new_m, new_v = {}, {}, {}
    small = [[view(n, d[n]) for n in small_names] for d in (w, mom, var)]
    small.insert(1, [g[n] for n in small_names])
    for d, outs in zip((delta, new_m, new_v), _adamw_whole(*small, "adamw_small")):
        d.update(zip(small_names, outs))
    for d in (g, delta, new_m, new_v):
        d.update({n: jnp.swapaxes(d[n], -1, -2) for n in swapped})
    g_big = ex.finish_big(delta[small_names[0]])
    for n in BIG_NAMES:
        g[n], delta[n], new_m[n], new_v[n] = _adamw(wl[n], g_big[n], ml[n], vl[n], "adamw_" + n)

    def like(n, a):
        return a.reshape(w[n].shape)

    return (loss, grad_x[None], *[like(n, g[n]) for n in WEIGHTS], *[like(n, delta[n]) for n in WEIGHTS],
            *[like(n, new_m[n]) for n in WEIGHTS], *[like(n, new_v[n]) for n in WEIGHTS])
```

```python
import functools
import math

import jax
import jax.numpy as jnp
from jax import lax
from jax.experimental import pallas as pl
from jax.experimental.pallas import tpu as pltpu

F32 = jnp.float32
BF16 = jnp.bfloat16
MESH = pl.DeviceIdType.MESH

N_META = 16
N_GROUPS = 32
GROUP = 16
STATE = 64
RMS_EPS = 1e-6
ADAM_LR = 0.001
ADAM_B1 = 0.9
ADAM_B2 = 0.999
ADAM_EPS = 1e-08
ADAM_WD = 0.01
ADAM_STEP = 10

LANES = 128
SUBLANES = 8
ROW_ALIGN = 128
ROW_TILES = 4
VMEM_LIMIT = 52 * 1024 * 1024
MM_VMEM_BUDGET = 40 * 1024 * 1024
GELU_C = math.sqrt(2.0 / math.pi)
GELU_A = 0.044715


def _cparams(*sem):
    return pltpu.CompilerParams(dimension_semantics=sem, vmem_limit_bytes=VMEM_LIMIT)


def _pick_tile(dim, cap, mult):
    best = None
    for t in range(mult, min(dim, cap) + 1, mult):
        if dim % t == 0:
            best = t
    return best if best is not None else dim


def _mm(a, b, mode, name, out_dtype=F32, acc_in=None, after=None):
    if mode == "tn":
        kdim, m = a.shape
    else:
        m, kdim = a.shape
    n = b.shape[0] if mode == "nt" else b.shape[1]
    tk = _pick_tile(kdim, 2816, LANES)
    nk = kdim // tk
    out_bytes = jnp.dtype(out_dtype).itemsize

    def fits(tm, tn):
        blocks = 2 * (tm * tk * 2 + tk * tn * 2 + tm * tn * out_bytes * (2 if acc_in is not None else 1))
        return blocks + (tm * tn * 4 if nk > 1 else 0) <= MM_VMEM_BUDGET

    shapes = [(_pick_tile(m, rows, LANES if mode == "tn" else 16), _pick_tile(n, cols, LANES))
              for rows in (2816, 1408, 1088, 512) for cols in (1408, 1024, 512, 256, LANES)]
    tm, tn = next((s for s in shapes if s[1] >= 512 and fits(*s)), None) or next(s for s in shapes if fits(*s))
    has_acc = acc_in is not None

    def body(*refs):
        if after is not None:
            refs = refs[1:]
        if has_acc:
            a_ref, b_ref, c_ref, o_ref = refs[:4]
            rest = refs[4:]
        else:
            a_ref, b_ref, o_ref = refs[:3]
            c_ref = None
            rest = refs[3:]
        if mode == "nn":
            p = jnp.dot(a_ref[...], b_ref[...], preferred_element_type=F32)
        elif mode == "nt":
            p = lax.dot_general(a_ref[...], b_ref[...], (((1,), (1,)), ((), ())), preferred_element_type=F32)
        else:
            p = lax.dot_general(a_ref[...], b_ref[...], (((0,), (0,)), ((), ())), preferred_element_type=F32)
        if nk == 1:
            if has_acc:
                p = p + c_ref[...]
            o_ref[...] = p.astype(out_dtype)
        else:
            acc_ref = rest[0]
            k = pl.program_id(2)

            @pl.when(k == 0)
            def _():
                acc_ref[...] = p + c_ref[...] if has_acc else p

            @pl.when(k > 0)
            def _():
                acc_ref[...] += p

            @pl.when(k == nk - 1)
            def _():
                o_ref[...] = acc_ref[...].astype(out_dtype)

    if mode == "tn":
        a_spec = pl.BlockSpec((tk, tm), lambda i, j, k: (k, i))
    else:
        a_spec = pl.BlockSpec((tm, tk), lambda i, j, k: (i, k))
    if mode == "nt":
        b_spec = pl.BlockSpec((tn, tk), lambda i, j, k: (j, k))
    else:
        b_spec = pl.BlockSpec((tk, tn), lambda i, j, k: (k, j))
    o_spec = pl.BlockSpec((tm, tn), lambda i, j, k: (i, j))
    in_specs = [a_spec, b_spec] + ([o_spec] if has_acc else [])
    args = (a, b) + ((acc_in,) if has_acc else ())
    if after is not None:
        in_specs = [pl.BlockSpec(memory_space=pl.ANY)] + in_specs
        args = (after,) + args
    return pl.pallas_call(
        body, name=name, grid=(m // tm, n // tn, nk),
        in_specs=in_specs, out_specs=o_spec,
        out_shape=jax.ShapeDtypeStruct((m, n), out_dtype),
        scratch_shapes=[pltpu.VMEM((tm, tn), F32)] if nk > 1 else [],
        compiler_params=_cparams("parallel", "parallel", "arbitrary"),
    )(*args)


def _mm_rows(a, b, mode, name, ins, outs, epilogue, scratch=()):
    m, kdim = a.shape
    n = b.shape[0] if mode == "nt" else b.shape[1]
    tm = m // ROW_TILES
    tk = _pick_tile(kdim, 2816, LANES)
    nk = kdim // tk
    ni, no = len(ins), len(outs)

    def body(*refs):
        a_ref, b_ref = refs[:2]
        in_refs, out_refs, rest = refs[2:2 + ni], refs[2 + ni:2 + ni + no], refs[2 + ni + no:]
        i = pl.program_id(0)
        if mode == "nn":
            p = jnp.dot(a_ref[...], b_ref[...], preferred_element_type=F32)
        else:
            p = lax.dot_general(a_ref[...], b_ref[...], (((1,), (1,)), ((), ())), preferred_element_type=F32)
        if nk == 1:
            epilogue(p, i, in_refs, out_refs, rest)
        else:
            acc_ref = rest[0]
            k = pl.program_id(1)

            @pl.when(k == 0)
            def _():
                acc_ref[...] = p

            @pl.when(k > 0)
            def _():
                acc_ref[...] += p

            @pl.when(k == nk - 1)
            def _():
                epilogue(acc_ref[...], i, in_refs, out_refs, rest[1:])

    def spec(shape, kind):
        if kind == "rows":
            return pl.BlockSpec((tm,) + tuple(shape[1:]), lambda i, k: (i,) + (0,) * (len(shape) - 1))
        if kind == "whole":
            return pl.BlockSpec(tuple(shape), lambda i, k: (0,) * len(shape))
        return pl.BlockSpec(memory_space=pl.ANY)

    a_spec = pl.BlockSpec((tm, tk), lambda i, k: (i, k))
    b_spec = pl.BlockSpec((n, tk), lambda i, k: (0, k)) if mode == "nt" else pl.BlockSpec((tk, n), lambda i, k: (k, 0))
    return pl.pallas_call(
        body, name=name, grid=(ROW_TILES, nk),
        in_specs=[a_spec, b_spec] + [spec(x.shape, kind) for x, kind in ins],
        out_specs=[spec(shape, kind) for shape, _, kind in outs],
        out_shape=[jax.ShapeDtypeStruct(shape, dtype) for shape, dtype, _ in outs],
        scratch_shapes=([pltpu.VMEM((tm, n), F32)] if nk > 1 else []) + list(scratch),
        compiler_params=_cparams("arbitrary", "arbitrary"),
    )(a, b, *[x for x, _ in ins])


def _rows(shape_cols, tr, dtype=None):
    return pl.BlockSpec((tr, shape_cols), lambda i: (i, 0))


def _const(shape):
    return pl.BlockSpec(shape, lambda i: (0,) * len(shape))


def _rms(x):
    return lax.rsqrt(jnp.mean(x * x, axis=-1, keepdims=True) + RMS_EPS)


def _rms_bwd(x, r, g, dy):
    xn = x * r
    dxn = dy * g
    dx = r * (dxn - xn * jnp.mean(dxn * xn, axis=-1, keepdims=True))
    return dx, dy * xn


def _gelu(y):
    return 0.5 * y * (1.0 + jnp.tanh(GELU_C * (y + GELU_A * y * y * y)))


def _gelu_grad(y):
    t = jnp.tanh(GELU_C * (y + GELU_A * y * y * y))
    return 0.5 * (1.0 + t) + 0.5 * y * (1.0 - t * t) * GELU_C * (1.0 + 3.0 * GELU_A * y * y)


def _sigmoid(z):
    return 1.0 / (1.0 + jnp.exp(-z))


def _proj_res_norm(a, w, h, g, after, name):
    def epilogue(p, i, ins, outs, _):
        x = ins[0][...] + p
        outs[0][...] = x
        outs[1][...] = (x * _rms(x) * ins[1][...]).astype(BF16)

    return _mm_rows(a, w, "nn", name, [(h, "rows"), (g, "whole"), (after, "hbm")],
                    [(h.shape, F32, "rows"), (h.shape, BF16, "rows")], epilogue)


def _proj_norm_bwd(da, w, h, g, dres, after, name):
    d = h.shape[1]

    def epilogue(p, i, ins, outs, _):
        x = ins[0][...]
        dx, dgs = _rms_bwd(x, _rms(x), ins[1][...], p)
        dh = ins[2][...] + dx
        outs[0][...] = dh
        outs[1][...] = dh.astype(BF16)

        @pl.when(i == 0)
        def _():
            outs[2][...] = jnp.zeros_like(outs[2])

        outs[2][...] += jnp.sum(dgs, axis=0, keepdims=True)

    return _mm_rows(da, w, "nt", name, [(h, "rows"), (g, "whole"), (dres, "rows"), (after, "hbm")],
                    [(h.shape, F32, "rows"), (h.shape, BF16, "rows"), ((1, d), F32, "whole")], epilogue)


def _proj_input_norm_bwd(da, w, h, g, dres, after, n_real, name):
    tp, d = h.shape
    tr = tp // ROW_TILES

    def epilogue(p, i, ins, outs, scratch):
        h_ref, g_ref, dres_ref, _ = ins
        dx_ref, dmeta_ref, dg_ref = outs
        stage, sem = scratch
        x = h_ref[...]
        dx, dgs = _rms_bwd(x, _rms(x), g_ref[...], p)
        stage[...] = dres_ref[...] + dx

        @pl.when(i == 0)
        def _():
            dg_ref[...] = jnp.zeros_like(dg_ref)
            dmeta_ref[...] = stage[:N_META, :]

        dg_ref[...] += jnp.sum(dgs, axis=0, keepdims=True)
        for t in range(ROW_TILES):
            lo, hi = max(t * tr, N_META), min((t + 1) * tr, n_real)
            if hi > lo:
                @pl.when(i == t)
                def _(t=t, lo=lo, hi=hi):
                    cp = pltpu.make_async_copy(stage.at[pl.ds(lo - t * tr, hi - lo), :],
                                               dx_ref.at[pl.ds(lo - N_META, hi - lo), :], sem)
                    cp.start()
                    cp.wait()

    return _mm_rows(da, w, "nt", name, [(h, "rows"), (g, "whole"), (dres, "rows"), (after, "hbm")],
                    [((n_real - N_META, d), F32, "hbm"), ((N_META, d), F32, "whole"), ((1, d), F32, "whole")],
                    epilogue, scratch=[pltpu.VMEM((tr, d), F32), pltpu.SemaphoreType.DMA])


def _load_token_rows(tok_hbm, buf, sem, tr, n_real, head=None, wait=False, i=None):
    i = pl.program_id(0) if i is None else i
    for t in range(ROW_TILES):
        base = t * tr
        lo, hi = max(base, N_META), min(base + tr, n_real)

        @pl.when(i == t)
        def _(base=base, lo=lo, hi=hi):
            if hi > lo:
                cp = pltpu.make_async_copy(tok_hbm.at[pl.ds(lo - N_META, hi - lo), :],
                                           buf.at[pl.ds(lo - base, hi - lo), :], sem)
                if wait:
                    cp.wait()
                    return
                cp.start()
            if wait:
                return
            if base < N_META:
                buf[0:N_META - base, :] = (jnp.zeros((N_META - base, buf.shape[1]), F32) if head is None
                                           else head[base:N_META, :])
            if hi < base + tr:
                buf[max(hi, base) - base:tr, :] = jnp.zeros((base + tr - max(hi, base), buf.shape[1]), F32)


def _input_norm_fwd(x, meta, g, tp, name):
    seq, d = x.shape
    tr = tp // ROW_TILES
    n_real = N_META + seq

    def body(x_hbm, meta_ref, g_ref, h_ref, hn_ref, buf, sem):
        _load_token_rows(x_hbm, buf, sem, tr, n_real, head=meta_ref)
        _load_token_rows(x_hbm, buf, sem, tr, n_real, wait=True)
        h = buf[...]
        h_ref[...] = h
        hn_ref[...] = (h * _rms(h) * g_ref[...]).astype(BF16)

    return pl.pallas_call(
        body, name=name, grid=(ROW_TILES,),
        in_specs=[pl.BlockSpec(memory_space=pl.ANY), _const((N_META, d)), _const((1, d))],
        out_specs=[_rows(d, tr), _rows(d, tr)],
        out_shape=[jax.ShapeDtypeStruct((tp, d), F32), jax.ShapeDtypeStruct((tp, d), BF16)],
        scratch_shapes=[pltpu.VMEM((tr, d), F32), pltpu.SemaphoreType.DMA],
        compiler_params=_cparams("arbitrary"))(x, meta, g)


def _proj_loss_bwd(act, w, h1, target, g, n_real, name):
    tp, d = h1.shape
    tr = tp // ROW_TILES

    def epilogue(p, i, ins, outs, scratch):
        h1_ref, t_hbm, g_ref = ins
        loss_ref, dh_ref, dhb_ref, dg_ref = outs
        t_buf, sem = scratch
        _load_token_rows(t_hbm, t_buf, sem, tr, n_real, i=i)
        x = h1_ref[...] + p
        r = _rms(x)
        row = i * tr + lax.broadcasted_iota(jnp.int32, (tr, d), 0)
        valid = (row >= N_META) & (row < n_real)
        _load_token_rows(t_hbm, t_buf, sem, tr, n_real, wait=True, i=i)
        e = jnp.where(valid, x * r * g_ref[...] - t_buf[...], 0.0)
        dx, dgs = _rms_bwd(x, r, g_ref[...], e * (1.0 / d))
        dh_ref[...] = dx
        dhb_ref[...] = dx.astype(BF16)

        @pl.when(i == 0)
        def _():
            dg_ref[...] = jnp.zeros_like(dg_ref)
            loss_ref[...] = jnp.zeros_like(loss_ref)

        dg_ref[...] += jnp.sum(dgs, axis=0, keepdims=True)
        loss_ref[...] += (0.5 / d) * jnp.sum(jnp.sum(e * e, axis=0, keepdims=True), axis=1, keepdims=True)

    return _mm_rows(act, w, "nn", name, [(h1, "rows"), (target, "hbm"), (g, "whole")],
                    [((1, LANES), F32, "whole"), ((tp, d), F32, "rows"), ((tp, d), BF16, "rows"),
                     ((1, d), F32, "whole")],
                    epilogue, scratch=[pltpu.VMEM((tr, d), F32), pltpu.SemaphoreType.DMA])


def _mix_fwd(co, y, z, gc, gs, name):
    tp, dh = co.shape
    tr = tp // ROW_TILES

    def body(co_ref, y_ref, z_ref, gc_ref, gs_ref, m_ref):
        c = co_ref[...]
        m_ref[:, :dh] = (c * _rms(c) * gc_ref[...]).astype(BF16)
        so = _gelu(y_ref[...]) * _sigmoid(z_ref[...])
        m_ref[:, dh:] = (so * _rms(so) * gs_ref[...]).astype(BF16)

    return pl.pallas_call(
        body, name=name, grid=(ROW_TILES,),
        in_specs=[_rows(dh, tr)] * 3 + [_const((1, dh))] * 2,
        out_specs=_rows(2 * dh, tr),
        out_shape=jax.ShapeDtypeStruct((tp, 2 * dh), BF16),
        compiler_params=_cparams("parallel"))(co, y, z, gc, gs)


def _proj_mix_bwd(dh1b, w, co, y, z, gc, gs, name):
    tp, dh = co.shape

    def epilogue(p, i, ins, outs, _):
        co_ref, y_ref, z_ref, gc_ref, gs_ref = ins
        dco_ref, dz_ref, dgp_ref, dgc_ref, dgs_ref = outs
        c = co_ref[...]
        dco, dgc = _rms_bwd(c, _rms(c), gc_ref[...], p[:, :dh])
        dco_ref[...] = dco
        gl = _gelu(y_ref[...])
        sg = _sigmoid(z_ref[...])
        so = gl * sg
        dso, dgs = _rms_bwd(so, _rms(so), gs_ref[...], p[:, dh:])
        dz_ref[...] = (dso * gl * sg * (1.0 - sg)).astype(BF16)
        dgp_ref[...] = dso * sg

        @pl.when(i == 0)
        def _():
            dgc_ref[...] = jnp.zeros_like(dgc_ref)
            dgs_ref[...] = jnp.zeros_like(dgs_ref)

        dgc_ref[...] += jnp.sum(dgc, axis=0, keepdims=True)
        dgs_ref[...] += jnp.sum(dgs, axis=0, keepdims=True)

    return _mm_rows(dh1b, w, "nt", name,
                    [(co, "rows"), (y, "rows"), (z, "rows"), (gc, "whole"), (gs, "whole")],
                    [((tp, dh), F32, "rows"), ((tp, dh), BF16, "rows"), ((tp, dh), F32, "rows"),
                     ((1, dh), F32, "whole"), ((1, dh), F32, "whole")], epilogue)


def _shift_down(x, k):
    row = lax.broadcasted_iota(jnp.int32, x.shape, 0)
    return jnp.where(row >= k, pltpu.roll(x, k, 0), 0.0)


def _shift_up(x, k):
    n = x.shape[0]
    row = lax.broadcasted_iota(jnp.int32, x.shape, 0)
    return jnp.where(row < n - k, pltpu.roll(x, n - k, 0), 0.0)


def _dwconv(x, w_ref):
    return w_ref[2:3, :] * x + w_ref[1:2, :] * _shift_down(x, 1) + w_ref[0:1, :] * _shift_down(x, 2)


def _dwconv_bwd(x, dy, w_ref):
    dx = w_ref[2:3, :] * dy + w_ref[1:2, :] * _shift_up(dy, 1) + w_ref[0:1, :] * _shift_up(dy, 2)
    dw = jnp.concatenate([jnp.sum(dy * _shift_down(x, 2), axis=0, keepdims=True),
                          jnp.sum(dy * _shift_down(x, 1), axis=0, keepdims=True),
                          jnp.sum(dy * x, axis=0, keepdims=True)], axis=0)
    return dx, dw


def _interleave(dst, src):
    seg_rows = src.shape[0] // SUBLANES
    for seg in range(SUBLANES):
        dst[pl.ds(seg, seg_rows, stride=SUBLANES), :] = src[seg * seg_rows:(seg + 1) * seg_rows, :]


def _deinterleave(dst, src):
    seg_rows = src.shape[0] // SUBLANES
    for seg in range(SUBLANES):
        dst[seg * seg_rows:(seg + 1) * seg_rows, :] = src[pl.ds(seg, seg_rows, stride=SUBLANES), :]


def _segment_shift(x, reverse):
    row = lax.broadcasted_iota(jnp.int32, x.shape, 0)
    if reverse:
        return jnp.where(row < SUBLANES - 1, pltpu.roll(x, SUBLANES - 1, 0), 0.0)
    return jnp.where(row >= 1, pltpu.roll(x, 1, 0), 0.0)


def _scan(s_re, s_im, pw_ref, reverse, pair=None):
    n_steps = s_re.shape[0] // SUBLANES
    n_strips = s_re.shape[1] // LANES
    sign = -1.0 if reverse else 1.0
    strips = [slice(st * LANES, (st + 1) * LANES) for st in range(n_strips)]

    def rows_of(j):
        step = (n_steps - 1 - j) if reverse else j
        return pl.ds(pl.multiple_of(step * SUBLANES, SUBLANES), SUBLANES)

    a = [(jnp.broadcast_to(pw_ref[0, 0:1, lanes], (SUBLANES, LANES)),
          sign * jnp.broadcast_to(pw_ref[1, 0:1, lanes], (SUBLANES, LANES))) for lanes in strips]

    def local(i, carry):
        for half in range(2):
            rows = rows_of(2 * i + half)
            out = []
            for st, lanes in enumerate(strips):
                (ar, ai), cr, ci = a[st], carry[2 * st], carry[2 * st + 1]
                xr = s_re[rows, lanes] + (ar * cr - ai * ci)
                xi = s_im[rows, lanes] + (ar * ci + ai * cr)
                s_re[rows, lanes] = xr
                s_im[rows, lanes] = xi
                out += [xr, xi]
            carry = tuple(out)
        return carry

    zero = jnp.zeros((SUBLANES, LANES), F32)
    ends = lax.fori_loop(0, n_steps // 2, local, (zero,) * (2 * n_strips))

    entering = []
    row = lax.broadcasted_iota(jnp.int32, (SUBLANES, LANES), 0)
    for st, lanes in enumerate(strips):
        tr, ti = ends[2 * st], ends[2 * st + 1]
        mr = jnp.broadcast_to(pw_ref[0, n_steps - 1:n_steps, lanes], (SUBLANES, LANES))
        mi = sign * jnp.broadcast_to(pw_ref[1, n_steps - 1:n_steps, lanes], (SUBLANES, LANES))
        for k in (1, 2, 4):
            keep = (row < SUBLANES - k) if reverse else (row >= k)
            rr = jnp.where(keep, pltpu.roll(tr, SUBLANES - k if reverse else k, 0), 0.0)
            ri = jnp.where(keep, pltpu.roll(ti, SUBLANES - k if reverse else k, 0), 0.0)
            tr, ti = tr + (mr * rr - mi * ri), ti + (mr * ri + mi * rr)
            mr, mi = mr * mr - mi * mi, 2.0 * mr * mi
        entering += [_segment_shift(tr, reverse), _segment_shift(ti, reverse)]

    def fix(i, carry):
        carry, sums = carry[:2 * n_strips], carry[2 * n_strips:]
        for half in range(2):
            j = 2 * i + half
            rows = rows_of(j)
            out, acc = [], []
            for st, lanes in enumerate(strips):
                (ar, ai), cr, ci = a[st], carry[2 * st], carry[2 * st + 1]
                cr, ci = ar * cr - ai * ci, ar * ci + ai * cr
                xr = s_re[rows, lanes] + cr
                xi = s_im[rows, lanes] + ci
                s_re[rows, lanes] = xr
                s_im[rows, lanes] = xi
                out += [cr, ci]
                if pair is not None:
                    p_rows = rows_of(jnp.minimum(j + 1, n_steps - 1))
                    keep = (j < n_steps - 1).astype(F32)
                    pr = pair[0][p_rows, lanes] * keep
                    pi = pair[1][p_rows, lanes] * keep
                    acc += [sums[2 * st] + (xr * pr + xi * pi), sums[2 * st + 1] + (xi * pr - xr * pi)]
            carry, sums = tuple(out), tuple(acc)
        return carry + sums

    n_sums = 0 if pair is None else 2 * n_strips
    out = lax.fori_loop(0, n_steps // 2, fix, tuple(entering) + (zero,) * n_sums)
    return out[2 * n_strips:]


def _seq_fwd(proj, conv_w, bc_re, bc_im, cc_re, cc_im, dskip, a_pow, name):
    tp = proj.shape[0]
    dh = proj.shape[1] // 4
    nq = dh // LANES
    sw = STATE * N_GROUPS // nq

    def body(b_ref, c_ref, v_ref, u_ref, w_ref, bre_ref, bim_ref, cre_ref, cim_ref, d_ref, pw_ref,
             co_ref, y_ref, g_ref, s_re, s_im, u_il, y_il):
        co_ref[...] = b_ref[...] * _dwconv(c_ref[...] * v_ref[...], w_ref)
        _interleave(u_il, u_ref)
        ub = u_il[...].astype(BF16)
        s_re[...] = jnp.dot(ub, bre_ref[...], preferred_element_type=F32)
        s_im[...] = jnp.dot(ub, bim_ref[...], preferred_element_type=F32)
        _scan(s_re, s_im, pw_ref, False)
        y_il[...] = (jnp.dot(s_re[...].astype(BF16), cre_ref[...], preferred_element_type=F32)
                     - jnp.dot(s_im[...].astype(BF16), cim_ref[...], preferred_element_type=F32))
        _deinterleave(y_ref, y_il)
        y = y_ref[...] + d_ref[...] * u_ref[...]
        y_ref[...] = y
        g_ref[...] = _gelu(y).astype(BF16)

    col = lambda off: pl.BlockSpec((tp, LANES), lambda q, off=off: (0, off * nq + q))
    blk = pl.BlockSpec((tp, LANES), lambda q: (0, q))
    return pl.pallas_call(
        body, name=name, grid=(nq,),
        in_specs=[col(0), col(1), col(2), col(3),
                  pl.BlockSpec((3, LANES), lambda q: (0, q)),
                  pl.BlockSpec((LANES, sw), lambda q: (0, q)), pl.BlockSpec((LANES, sw), lambda q: (0, q)),
                  pl.BlockSpec((sw, LANES), lambda q: (q, 0)), pl.BlockSpec((sw, LANES), lambda q: (q, 0)),
                  pl.BlockSpec((1, LANES), lambda q: (0, q)),
                  pl.BlockSpec((2, tp // SUBLANES, sw), lambda q: (0, 0, q))],
        out_specs=[blk, blk, blk],
        out_shape=[jax.ShapeDtypeStruct((tp, dh), F32), jax.ShapeDtypeStruct((tp, dh), F32),
                   jax.ShapeDtypeStruct((tp, dh), BF16)],
        scratch_shapes=[pltpu.VMEM((tp, sw), F32), pltpu.VMEM((tp, sw), F32),
                        pltpu.VMEM((tp, LANES), F32), pltpu.VMEM((tp, LANES), F32)],
        compiler_params=_cparams("parallel"),
    )(proj, proj, proj, proj, conv_w, bc_re, bc_im, cc_re, cc_im, dskip, a_pow)


def _conv_bwd(proj, dco, conv_w, name):
    tp = proj.shape[0]
    dh = proj.shape[1] // 4
    nq = dh // LANES

    def body(b_ref, c_ref, v_ref, dco_ref, w_ref, dproj_ref, dw_ref, stage, sem):
        q = pl.program_id(0)
        cg = c_ref[...]
        vg = v_ref[...]
        cv = cg * vg
        dco_v = dco_ref[...]
        dcv, dw = _dwconv_bwd(cv, dco_v * b_ref[...], w_ref)
        dw_ref[...] = dw
        stage[0] = (dco_v * _dwconv(cv, w_ref)).astype(BF16)
        stage[1] = (dcv * vg).astype(BF16)
        stage[2] = (dcv * cg).astype(BF16)
        copies = [pltpu.make_async_copy(stage.at[p], dproj_ref.at[:, pl.ds((p * nq + q) * LANES, LANES)], sem.at[p])
                  for p in range(3)]
        for cp in copies:
            cp.start()
        for cp in copies:
            cp.wait()

    col = lambda off: pl.BlockSpec((tp, LANES), lambda q, off=off: (0, off * nq + q))
    return pl.pallas_call(
        body, name=name, grid=(nq,),
        in_specs=[col(0), col(1), col(2), pl.BlockSpec((tp, LANES), lambda q: (0, q)),
                  pl.BlockSpec((3, LANES), lambda q: (0, q))],
        out_specs=[pl.BlockSpec(memory_space=pl.ANY), pl.BlockSpec((3, LANES), lambda q: (0, q))],
        out_shape=[jax.ShapeDtypeStruct((tp, 4 * dh), BF16), jax.ShapeDtypeStruct((3, dh), F32)],
        scratch_shapes=[pltpu.VMEM((3, tp, LANES), BF16), pltpu.SemaphoreType.DMA((3,))],
        compiler_params=_cparams("arbitrary"),
    )(proj, proj, proj, dco, conv_w)


def _ssm_bwd(proj, y, dg, dproj, bc_re, bc_im, cc_re, cc_im, dskip, a_pow, name):
    tp = proj.shape[0]
    dh = proj.shape[1] // 4
    nq = dh // LANES
    sw = STATE * N_GROUPS // nq

    def body(u_ref, y_ref, dg_ref, dproj_in, bre_ref, bim_ref, cre_ref, cim_ref, d_ref, pw_ref,
             dproj_ref, dbre_ref, dbim_ref, dcre_ref, dcim_ref, dd_ref, dar_ref, dai_ref,
             s_re, s_im, l_re, l_im, a_il, b_il, stage, sem):
        del dproj_in
        q = pl.program_id(0)
        nt = (((1,), (1,)), ((), ()))
        tn = (((0,), (0,)), ((), ()))
        _interleave(a_il, u_ref)
        ub = a_il[...].astype(BF16)
        s_re[...] = jnp.dot(ub, bre_ref[...], preferred_element_type=F32)
        s_im[...] = jnp.dot(ub, bim_ref[...], preferred_element_type=F32)
        _scan(s_re, s_im, pw_ref, False)
        dy_rows = dg_ref[...] * _gelu_grad(y_ref[...])
        dd_ref[...] = jnp.sum(dy_rows * u_ref[...], axis=0, keepdims=True)
        _interleave(b_il, dy_rows)
        dy = b_il[...]
        dyb = dy.astype(BF16)
        l_re[...] = lax.dot_general(dyb, cre_ref[...], nt, preferred_element_type=F32)
        l_im[...] = -lax.dot_general(dyb, cim_ref[...], nt, preferred_element_type=F32)
        dcre_ref[...] = lax.dot_general(s_re[...].astype(BF16), dyb, tn, preferred_element_type=F32)
        dcim_ref[...] = -lax.dot_general(s_im[...].astype(BF16), dyb, tn, preferred_element_type=F32)
        sums = _scan(l_re, l_im, pw_ref, True, pair=(s_re, s_im))
        rest = tp - SUBLANES
        for st in range(sw // LANES):
            lanes = slice(st * LANES, (st + 1) * LANES)
            lr0, li0 = l_re[:SUBLANES, lanes], l_im[:SUBLANES, lanes]
            pr0, pi0 = _segment_shift(s_re[rest:, lanes], False), _segment_shift(s_im[rest:, lanes], False)
            dar_ref[:, lanes] = jnp.sum(sums[2 * st] + (lr0 * pr0 + li0 * pi0), axis=0, keepdims=True)
            dai_ref[:, lanes] = jnp.sum(sums[2 * st + 1] + (li0 * pr0 - lr0 * pi0), axis=0, keepdims=True)
        lrb = l_re[...].astype(BF16)
        lib = l_im[...].astype(BF16)
        a_il[...] = (dy * d_ref[...] + lax.dot_general(lrb, bre_ref[...], nt, preferred_element_type=F32)
                     + lax.dot_general(lib, bim_ref[...], nt, preferred_element_type=F32))
        _deinterleave(b_il, a_il)
        stage[...] = b_il[...].astype(BF16)
        dbre_ref[...] = lax.dot_general(ub, lrb, tn, preferred_element_type=F32)
        dbim_ref[...] = lax.dot_general(ub, lib, tn, preferred_element_type=F32)
        cp = pltpu.make_async_copy(stage, dproj_ref.at[:, pl.ds((3 * nq + q) * LANES, LANES)], sem)
        cp.start()
        cp.wait()

    blk = pl.BlockSpec((tp, LANES), lambda q: (0, q))
    bspec = pl.BlockSpec((LANES, sw), lambda q: (0, q))
    cspec = pl.BlockSpec((sw, LANES), lambda q: (q, 0))
    tspec = pl.BlockSpec((2, tp // SUBLANES, sw), lambda q: (0, 0, q))
    nstate = STATE * N_GROUPS
    return pl.pallas_call(
        body, name=name, grid=(nq,),
        in_specs=[pl.BlockSpec((tp, LANES), lambda q: (0, 3 * nq + q)), blk, blk, pl.BlockSpec(memory_space=pl.ANY),
                  bspec, bspec, cspec, cspec, pl.BlockSpec((1, LANES), lambda q: (0, q)), tspec],
        out_specs=[pl.BlockSpec(memory_space=pl.ANY), bspec, bspec, cspec, cspec,
                   pl.BlockSpec((1, LANES), lambda q: (0, q)),
                   pl.BlockSpec((1, sw), lambda q: (0, q)), pl.BlockSpec((1, sw), lambda q: (0, q))],
        out_shape=[jax.ShapeDtypeStruct((tp, 4 * dh), BF16),
                   jax.ShapeDtypeStruct((LANES, nstate), F32), jax.ShapeDtypeStruct((LANES, nstate), F32),
                   jax.ShapeDtypeStruct((nstate, LANES), F32), jax.ShapeDtypeStruct((nstate, LANES), F32),
                   jax.ShapeDtypeStruct((1, dh), F32),
                   jax.ShapeDtypeStruct((1, nstate), F32), jax.ShapeDtypeStruct((1, nstate), F32)],
        input_output_aliases={3: 0},
        scratch_shapes=[pltpu.VMEM((tp, sw), F32)] * 4 + [pltpu.VMEM((tp, LANES), F32)] * 2
        + [pltpu.VMEM((tp, LANES), BF16), pltpu.SemaphoreType.DMA],
        compiler_params=_cparams("arbitrary"),
    )(proj, y, dg, dproj, bc_re, bc_im, cc_re, cc_im, dskip, a_pow)


FFN_TILE = 256


def _ffn_act(up, fw, fb, name):
    tp, two_ff = up.shape
    dff = two_ff // 2
    tc = FFN_TILE
    nj = dff // tc

    def body(ua_ref, uv_ref, wa_ref, wv_ref, ba_ref, bv_ref, act_ref):
        a = _dwconv(ua_ref[...], wa_ref) + ba_ref[...]
        v = _dwconv(uv_ref[...], wv_ref) + bv_ref[...]
        act_ref[...] = (a * _sigmoid(a) * v).astype(BF16)

    lo = lambda r: pl.BlockSpec((r, tc), lambda j: (0, j))
    hi = lambda r: pl.BlockSpec((r, tc), lambda j: (0, nj + j))
    return pl.pallas_call(
        body, name=name, grid=(nj,),
        in_specs=[lo(tp), hi(tp), lo(3), hi(3), lo(1), hi(1)],
        out_specs=lo(tp),
        out_shape=jax.ShapeDtypeStruct((tp, dff), BF16),
        compiler_params=_cparams("parallel"))(up, up, fw, fw, fb, fb)


def _ffn_bwd(up, dact, fw, fb, name):
    tp, two_ff = up.shape
    dff = two_ff // 2
    tc = FFN_TILE
    nj = dff // tc

    def body(ua_ref, uv_ref, da_ref, wa_ref, wv_ref, ba_ref, bv_ref,
             dup_ref, dwa_ref, dwv_ref, dba_ref, dbv_ref, stage, sem):
        j = pl.program_id(0)
        ua = ua_ref[...]
        uv = uv_ref[...]
        a = _dwconv(ua, wa_ref) + ba_ref[...]
        v = _dwconv(uv, wv_ref) + bv_ref[...]
        sg = _sigmoid(a)
        dact_v = da_ref[...]
        da = dact_v * v * sg * (1.0 + a * (1.0 - sg))
        dv = dact_v * a * sg
        dba_ref[...] = jnp.sum(da, axis=0, keepdims=True)
        dbv_ref[...] = jnp.sum(dv, axis=0, keepdims=True)
        dua, dwa = _dwconv_bwd(ua, da, wa_ref)
        duv, dwv = _dwconv_bwd(uv, dv, wv_ref)
        dwa_ref[...] = dwa
        dwv_ref[...] = dwv
        stage[0] = dua.astype(BF16)
        stage[1] = duv.astype(BF16)
        copies = [pltpu.make_async_copy(stage.at[p], dup_ref.at[:, pl.ds((p * nj + j) * tc, tc)], sem.at[p])
                  for p in range(2)]
        for cp in copies:
            cp.start()
        for cp in copies:
            cp.wait()

    lo = lambda r: pl.BlockSpec((r, tc), lambda j: (0, j))
    hi = lambda r: pl.BlockSpec((r, tc), lambda j: (0, nj + j))
    return pl.pallas_call(
        body, name=name, grid=(nj,),
        in_specs=[lo(tp), hi(tp), lo(tp), lo(3), hi(3), lo(1), hi(1)],
        out_specs=[pl.BlockSpec(memory_space=pl.ANY), lo(3), lo(3), lo(1), lo(1)],
        out_shape=[jax.ShapeDtypeStruct((tp, two_ff), BF16),
                   jax.ShapeDtypeStruct((3, dff), F32), jax.ShapeDtypeStruct((3, dff), F32),
                   jax.ShapeDtypeStruct((1, dff), F32), jax.ShapeDtypeStruct((1, dff), F32)],
        scratch_shapes=[pltpu.VMEM((2, tp, tc), BF16), pltpu.SemaphoreType.DMA((2,))],
        compiler_params=_cparams("arbitrary"))(up, up, dact, fw, fw, fb, fb)


def _zoh(lr, li, ld):
    dt = jnp.exp(ld)
    mag = jnp.exp(lr * dt)
    ang = li * dt
    ar = mag * jnp.cos(ang)
    ai = mag * jnp.sin(ang)
    den = lr * lr + li * li
    nr = ar - 1.0
    fr = (nr * lr + ai * li) / den
    fi = (ai * lr - nr * li) / den
    return dt, ar, ai, den, nr, fr, fi


def _s5_prep(lr, li, ld, b_re, b_im, n_pow, name):
    nstate = lr.shape[1]

    def body(lr_ref, li_ref, ld_ref, bre_ref, bim_ref, pw_ref, bcre_ref, bcim_ref):
        _, ar, ai, _, _, fr, fi = _zoh(lr_ref[...], li_ref[...], ld_ref[...])
        bre = bre_ref[...]
        bim = bim_ref[...]
        bcre_ref[...] = (fr * bre - fi * bim).astype(BF16)
        bcim_ref[...] = (fr * bim + fi * bre).astype(BF16)
        row = lax.broadcasted_iota(jnp.int32, (SUBLANES, nstate), 0)
        pr, pi = jnp.zeros((SUBLANES, nstate), F32), jnp.zeros((SUBLANES, nstate), F32)
        cr, ci = ar, ai
        for t in range(SUBLANES):
            pr, pi = jnp.where(row == t, cr, pr), jnp.where(row == t, ci, pi)
            cr, ci = cr * ar - ci * ai, cr * ai + ci * ar
        pw_ref[0, 0:SUBLANES, :] = pr
        pw_ref[1, 0:SUBLANES, :] = pi
        n = SUBLANES
        while n < n_pow:
            m = min(n, n_pow - n)
            tr, ti = pw_ref[0, n - 1:n, :], pw_ref[1, n - 1:n, :]
            xr, xi = pw_ref[0, 0:m, :], pw_ref[1, 0:m, :]
            pw_ref[0, n:n + m, :] = xr * tr - xi * ti
            pw_ref[1, n:n + m, :] = xr * ti + xi * tr
            n += m

    vmem = pl.BlockSpec(memory_space=pltpu.VMEM)
    return pl.pallas_call(
        body, name=name, in_specs=[vmem] * 5, out_specs=[vmem] * 3,
        out_shape=[jax.ShapeDtypeStruct((2, n_pow, nstate), F32)] + [jax.ShapeDtypeStruct(b_re.shape, BF16)] * 2,
        compiler_params=pltpu.CompilerParams(vmem_limit_bytes=VMEM_LIMIT))(lr, li, ld, b_re, b_im)


def _s5_prep_bwd(lr, li, ld, b_re, b_im, da_re, da_im, dbc_re, dbc_im, name):
    def body(lr_ref, li_ref, ld_ref, bre_ref, bim_ref, dar_ref, dai_ref, dbcre_ref, dbcim_ref,
             dlr_ref, dli_ref, dld_ref, dbre_ref, dbim_ref):
        lr, li = lr_ref[...], li_ref[...]
        dt, ar, ai, den, nr, fr, fi = _zoh(lr, li, ld_ref[...])
        bre, bim = bre_ref[...], bim_ref[...]
        gre, gim = dbcre_ref[...], dbcim_ref[...]
        dbre_ref[...] = fr * gre + fi * gim
        dbim_ref[...] = fr * gim - fi * gre
        g_fr = jnp.sum(gre * bre + gim * bim, axis=0, keepdims=True)
        g_fi = jnp.sum(gim * bre - gre * bim, axis=0, keepdims=True)
        g_ar = dar_ref[...] + (g_fr * lr - g_fi * li) / den
        g_ai = dai_ref[...] + (g_fr * li + g_fi * lr) / den
        d_lr = (g_fr * (nr - 2.0 * fr * lr) + g_fi * (ai - 2.0 * fi * lr)) / den
        d_li = (g_fr * (ai - 2.0 * fr * li) - g_fi * (nr + 2.0 * fi * li)) / den
        g_logmag = g_ar * ar + g_ai * ai
        g_ang = g_ai * ar - g_ar * ai
        dlr_ref[...] = d_lr + g_logmag * dt
        dli_ref[...] = d_li + g_ang * dt
        d_ld = (g_logmag * lr + g_ang * li) * dt
        n = d_ld.shape[1]
        sh = 1
        while sh < STATE:
            d_ld = d_ld + pltpu.roll(d_ld, n - sh, 1)
            sh *= 2
        dld_ref[...] = d_ld

    vmem = pl.BlockSpec(memory_space=pltpu.VMEM)
    row = jax.ShapeDtypeStruct(lr.shape, F32)
    return pl.pallas_call(
        body, name=name, in_specs=[vmem] * 9, out_specs=[vmem] * 5,
        out_shape=[row, row, row, jax.ShapeDtypeStruct(b_re.shape, F32), jax.ShapeDtypeStruct(b_re.shape, F32)],
    )(lr, li, ld, b_re, b_im, da_re, da_im, dbc_re, dbc_im)


def _compact_b(bb):
    bq = bb.reshape(N_GROUPS // 8, 8, STATE, GROUP)
    m = jnp.einsum("ab,qbph->qahbp", jnp.eye(8, dtype=bb.dtype), bq).reshape(N_GROUPS // 8, LANES, 8 * STATE)
    return m.transpose(1, 0, 2).reshape(LANES, N_GROUPS * STATE)


def _expand_b(m):
    d = m.reshape(8, GROUP, N_GROUPS // 8, 8, STATE)
    return jnp.einsum("ahqap->qahp", d).reshape(N_GROUPS, GROUP, STATE)


def _compact_c(c):
    cq = c.reshape(N_GROUPS // 8, 8, GROUP, STATE)
    return jnp.einsum("ab,qbhp->qbpah", jnp.eye(8, dtype=c.dtype), cq).reshape(N_GROUPS * STATE, LANES)


def _expand_c(m):
    d = m.reshape(N_GROUPS // 8, 8, STATE, 8, GROUP)
    return jnp.einsum("qbpbh->qbhp", d).reshape(N_GROUPS, GROUP, STATE)


def _local_step(x, target, p, ex):
    seq, d = x.shape
    n_real = N_META + seq
    tp = -(-n_real // ROW_ALIGN) * ROW_ALIGN

    h0, hn1 = _input_norm_fwd(x, p["meta_tokens"], p["norm_mix_g"] + ex.zero, tp, "norm_mix")
    ex.forward("first", hn1)
    nstate = N_GROUPS * STATE
    s5 = (p["ssm_lam_re"].reshape(1, nstate), p["ssm_lam_im"].reshape(1, nstate),
          jnp.repeat(p["ssm_log_dt"].reshape(-1), STATE).reshape(1, nstate),
          _compact_b(p["ssm_b_re"]), _compact_b(p["ssm_b_im"]))
    a_pow, bc_re, bc_im = _s5_prep(*s5, tp // SUBLANES, "s5_prep")
    cc_re = _compact_c(p["ssm_c_re"]).astype(BF16)
    cc_im = _compact_c(p["ssm_c_im"]).astype(BF16)
    dskip = p["ssm_d"].reshape(1, -1)
    first = ex.weights("first", bc_re)
    proj = _mm(hn1, first["w_in"], "nn", "proj")
    started = ex.forward("mid", proj)
    co, y, g = _seq_fwd(proj, p["conv_w"] + started[0, 0], bc_re, bc_im, cc_re, cc_im, dskip, a_pow, "seq_fwd")
    mid = ex.weights("mid", g)
    z = _mm(g, mid["ssm_w_glu"], "nn", "glu")
    mixed = _mix_fwd(co, y, z, p["gain_conv_out"], p["gain_ssm_out"], "mix_fwd")
    started = ex.forward("up", mixed)
    h1, hn2 = _proj_res_norm(mixed, mid["w_out"], h0, p["norm_ffn_g"], started, "out_proj_norm")
    late = ex.weights("up", hn2)
    up = _mm(hn2, late["w_up"], "nn", "up_proj")
    started = ex.forward("down", up)
    act = _ffn_act(up, p["ffn_conv_w"] + started[0, 0], p["ffn_conv_b"], "ffn_act")
    late.update(ex.weights("down", act))
    loss, dh2, dh2b, d_gfin = _proj_loss_bwd(act, late["w_down"], h1, target, p["norm_final_g"], n_real,
                                             "down_proj_loss")

    g_w_down = _mm(act, dh2b, "tn", "g_w_down")
    dact = _mm(dh2b, late["w_down"], "nt", "d_act")
    dup, dfw_a, dfw_v, dfb_a, dfb_v = _ffn_bwd(up, dact, p["ffn_conv_w"], p["ffn_conv_b"], "ffn_bwd")
    g_w_up = _mm(hn2, dup, "tn", "g_w_up")
    started = ex.grads_ready("late", {"w_up": g_w_up, "w_down": g_w_down})
    dh1, dh1b, d_gffn = _proj_norm_bwd(dup, late["w_up"], h1, p["norm_ffn_g"], dh2, started, "d_hn2_norm_bwd")
    started = ex.grads_send("late", dh1)
    g_w_out = _mm(mixed, dh1b, "tn", "g_w_out", after=started)
    dco, dz, dgp, d_gc, d_gs = _proj_mix_bwd(dh1b, mid["w_out"], co, y, z, p["gain_conv_out"],
                                             p["gain_ssm_out"], "d_mixed_mix_bwd")
    g_w_glu = _mm(g, dz, "tn", "g_w_glu")
    started = ex.grads_ready("mid", {"ssm_w_glu": g_w_glu, "w_out": g_w_out})
    dg = _mm(dz, mid["ssm_w_glu"], "nt", "d_gelu", acc_in=dgp, after=started)
    started = ex.grads_send("mid", dg)
    dproj, d_conv_w = _conv_bwd(proj, dco, p["conv_w"] + started[0, 0], "conv_bwd")
    (dproj, dbc_re, dbc_im, dcc_re, dcc_im, d_dskip, da_re, da_im) = _ssm_bwd(
        proj, y, dg, dproj, bc_re, bc_im, cc_re, cc_im, dskip, a_pow, "ssm_bwd")
    g_w_in = _mm(hn1, dproj, "tn", "g_w_in")
    started = ex.grads_ready("first", {"w_in": g_w_in})
    grad_x, d_meta, d_gmix = _proj_input_norm_bwd(dproj, first["w_in"], h0, p["norm_mix_g"], dh1, started, n_real,
                                                  "d_hn1_norm_bwd")
    started = ex.grads_send("first", d_gmix)

    d_lam_re, d_lam_im, d_log_dt, d_b_re, d_b_im = _s5_prep_bwd(*s5, da_re, da_im, dbc_re, dbc_im, "s5_prep_bwd")
    d_lam_re, d_lam_im = d_lam_re.reshape(N_GROUPS, STATE), d_lam_im.reshape(N_GROUPS, STATE)
    d_log_dt = d_log_dt[0, ::STATE]
    d_b_re, d_b_im = _expand_b(d_b_re), _expand_b(d_b_im)
    grads = {
        "meta_tokens": d_meta, "norm_mix_g": d_gmix, "w_in": g_w_in, "conv_w": d_conv_w,
        "ssm_lam_re": d_lam_re, "ssm_lam_im": d_lam_im, "ssm_log_dt": d_log_dt,
        "ssm_b_re": d_b_re, "ssm_b_im": d_b_im, "ssm_c_re": _expand_c(dcc_re), "ssm_c_im": _expand_c(dcc_im),
        "ssm_d": d_dskip.reshape(N_GROUPS, GROUP), "ssm_w_glu": g_w_glu,
        "gain_conv_out": d_gc, "gain_ssm_out": d_gs, "w_out": g_w_out, "norm_ffn_g": d_gffn,
        "w_up": g_w_up, "ffn_conv_w": jnp.concatenate([dfw_a, dfw_v], axis=1),
        "ffn_conv_b": jnp.concatenate([dfb_a, dfb_v], axis=1), "w_down": g_w_down, "norm_final_g": d_gfin,
    }
    return loss[0, 0] + started[0, 0], grad_x, grads


def _view(ref, axis, start, size):
    idx = [slice(None)] * len(ref.shape)
    idx[axis] = pl.ds(start, size)
    return ref.at[tuple(idx)]


def _exchange(name, ins, outs, aliases, local_copies, remote_copies):
    ni, no = len(ins), len(outs)
    nl, nr = len(local_copies), len(remote_copies)

    def body(*refs):
        in_refs, out_refs = refs[:ni], refs[ni:ni + no]
        send_sems, recv_sems, local_sems = refs[ni + no:]
        x, y, c = lax.axis_index("x"), lax.axis_index("y"), lax.axis_index("c")
        pos = (x, y, c, 2 * x + y)
        locals_ = [pltpu.make_async_copy(s(in_refs, out_refs, pos), d(in_refs, out_refs, pos), local_sems.at[i])
                   for i, (s, d) in enumerate(local_copies)]
        remotes = []
        for i, (s, d, flip) in enumerate(remote_copies):
            peer = (1 - x if "x" in flip else x, 1 - y if "y" in flip else y, 1 - c if "c" in flip else c)
            remotes.append(pltpu.make_async_remote_copy(
                src_ref=s(in_refs, out_refs, pos), dst_ref=d(in_refs, out_refs, pos),
                send_sem=send_sems.at[i], recv_sem=recv_sems.at[i], device_id=peer, device_id_type=MESH))
        for cp in locals_ + remotes:
            cp.start()
        for cp in remotes:
            cp.wait_recv()
        for cp in remotes:
            cp.wait_send()
        for cp in locals_:
            cp.wait()

    hbm = pl.BlockSpec(memory_space=pl.ANY)
    return pl.pallas_call(
        body, name=name, in_specs=[hbm] * ni, out_specs=[hbm] * no, out_shape=outs,
        input_output_aliases=aliases,
        scratch_shapes=[pltpu.SemaphoreType.DMA((nr,)), pltpu.SemaphoreType.DMA((nr,)),
                        pltpu.SemaphoreType.DMA((max(nl, 1),))],
    )(*ins)


BIG = {"w_in": (0, 1), "ssm_w_glu": (1, 0), "w_out": (1, 0), "w_up": (0, 1), "w_down": (1, 0)}
BIG_NAMES = tuple(BIG)
FLIPS = ("y", "x", "xy")


def _peer_chip(pos, flip):
    x, y, _, _ = pos
    return 2 * (1 - x if "x" in flip else x) + (1 - y if "y" in flip else y)


def _block_rows(rows, cols, itemsize, mult):
    return _pick_tile(rows, max(mult, (2 * 1024 * 1024) // (cols * itemsize)), mult)


def _cast_into_full(w, kc, shard_axis, name):
    r, cdim = w.shape
    tr = _block_rows(r, cdim, 4, 16)
    nb = r // tr

    def body(kc_ref, w_ref, o_ref):
        o_ref[...] = w_ref[...].astype(BF16)

    if shard_axis == 1:
        full, o_spec = (r, 4 * cdim), pl.BlockSpec((tr, cdim), lambda i, kc: (i, kc[0]))
    else:
        full, o_spec = (4 * r, cdim), pl.BlockSpec((tr, cdim), lambda i, kc: (kc[0] * nb + i, 0))
    return pl.pallas_call(
        body, name=name,
        grid_spec=pltpu.PrefetchScalarGridSpec(
            num_scalar_prefetch=1, grid=(nb,), in_specs=[pl.BlockSpec((tr, cdim), lambda i, kc: (i, 0))],
            out_specs=o_spec),
        out_shape=jax.ShapeDtypeStruct(full, BF16), compiler_params=_cparams("parallel"))(kc, w)


def _pair_sum(g, recv, kc, half_axis, name, out_dtype):
    hr, hc = recv.shape
    tr = _block_rows(hr, hc, 4, 16)
    nb = hr // tr

    def body(kc_ref, g_ref, r_ref, o_ref):
        o_ref[...] = (g_ref[...] + r_ref[...]).astype(out_dtype)

    if half_axis == 0:
        g_spec = pl.BlockSpec((tr, hc), lambda i, kc: (kc[1] * nb + i, 0))
    elif half_axis == 1:
        g_spec = pl.BlockSpec((tr, hc), lambda i, kc: (i, kc[1]))
    else:
        g_spec = pl.BlockSpec((tr, hc), lambda i, kc: (i, 0))
    same = pl.BlockSpec((tr, hc), lambda i, kc: (i, 0))
    return pl.pallas_call(
        body, name=name,
        grid_spec=pltpu.PrefetchScalarGridSpec(num_scalar_prefetch=1, grid=(nb,), in_specs=[g_spec, same],
                                               out_specs=same),
        out_shape=jax.ShapeDtypeStruct((hr, hc), out_dtype), compiler_params=_cparams("parallel"))(kc, g, recv)


def _chip_sum(own, recv, kc, own_axis, out_axis, name):
    _, sr, sc = recv.shape
    tr = _block_rows(sr, sc, 4, 16)
    nb = sr // tr

    def body(kc_ref, o_ref, r_ref, t_ref):
        k = kc_ref[0]
        own_v = o_ref[...].astype(F32)
        r = [r_ref[m].astype(F32) for m in range(3)]
        terms = []
        for kk in range(4):
            m = jnp.bitwise_xor(k, kk)
            terms.append(jnp.where(m == 0, own_v, jnp.where(m == 1, r[0], jnp.where(m == 2, r[1], r[2]))))
        t_ref[...] = (terms[0] + terms[1]) + (terms[2] + terms[3])

    if own_axis == 0:
        own_spec = pl.BlockSpec((tr, sc), lambda i, kc: (kc[0] * nb + i, 0))
    elif own_axis == 1:
        own_spec = pl.BlockSpec((tr, sc), lambda i, kc: (i, kc[0]))
    else:
        own_spec = pl.BlockSpec((tr, sc), lambda i, kc: (kc[1] * nb + i, 0))
    if out_axis == 0:
        out_full, out_spec = (2 * sr, sc), pl.BlockSpec((tr, sc), lambda i, kc: (kc[1] * nb + i, 0))
    else:
        out_full, out_spec = (sr, 2 * sc), pl.BlockSpec((tr, sc), lambda i, kc: (i, kc[1]))
    return pl.pallas_call(
        body, name=name,
        grid_spec=pltpu.PrefetchScalarGridSpec(
            num_scalar_prefetch=1, grid=(nb,),
            in_specs=[own_spec, pl.BlockSpec((3, tr, sc), lambda i, kc: (0, i, 0))],
            out_specs=out_spec),
        out_shape=jax.ShapeDtypeStruct(out_full, F32), compiler_params=_cparams("parallel"))(kc, own, recv)


def _adamw(w, g, m, v, name):
    r, cdim = w.shape
    tr = _block_rows(r, cdim, 4, 8)
    c1 = 1.0 - ADAM_B1 ** ADAM_STEP
    c2 = 1.0 - ADAM_B2 ** ADAM_STEP

    def body(w_ref, g_ref, m_ref, v_ref, go_ref, d_ref, nm_ref, nv_ref):
        gv = g_ref[...]
        go_ref[...] = gv
        nm = ADAM_B1 * m_ref[...] + (1.0 - ADAM_B1) * gv
        nv = ADAM_B2 * v_ref[...] + (1.0 - ADAM_B2) * (gv * gv)
        d_ref[...] = -ADAM_LR * ((nm / c1) / (jnp.sqrt(nv / c2) + ADAM_EPS) + ADAM_WD * w_ref[...])
        nm_ref[...] = nm
        nv_ref[...] = nv

    spec = _rows(cdim, tr)
    return pl.pallas_call(body, name=name, grid=(r // tr,), in_specs=[spec] * 4, out_specs=[spec] * 4,
                          out_shape=[jax.ShapeDtypeStruct((r, cdim), F32)] * 4,
                          compiler_params=_cparams("parallel"))(w, g, m, v)


def _adamw_whole(ws, gs, ms, vs, name):
    n = len(ws)
    c1 = 1.0 - ADAM_B1 ** ADAM_STEP
    c2 = 1.0 - ADAM_B2 ** ADAM_STEP

    def body(*refs):
        for i in range(n):
            w_ref, g_ref, m_ref, v_ref, d_ref, nm_ref, nv_ref = [refs[j * n + i] for j in range(7)]
            gv = g_ref[...]
            nm = ADAM_B1 * m_ref[...] + (1.0 - ADAM_B1) * gv
            nv = ADAM_B2 * v_ref[...] + (1.0 - ADAM_B2) * (gv * gv)
            d_ref[...] = -ADAM_LR * ((nm / c1) / (jnp.sqrt(nv / c2) + ADAM_EPS) + ADAM_WD * w_ref[...])
            nm_ref[...] = nm
            nv_ref[...] = nv

    vmem = pl.BlockSpec(memory_space=pltpu.VMEM)
    out = pl.pallas_call(body, name=name, in_specs=[vmem] * (4 * n), out_specs=[vmem] * (3 * n),
                         out_shape=[jax.ShapeDtypeStruct(a.shape, F32) for a in ws] * 3,
                         compiler_params=pltpu.CompilerParams(vmem_limit_bytes=VMEM_LIMIT))(*ws, *gs, *ms, *vs)
    return out[:n], out[n:2 * n], out[2 * n:]


SIDE_EFFECT = pltpu.SideEffectType.DATAFLOW_SIDE_EFFECTING


def _descriptors(copies, refs, send_sems, recv_sems, sem_off=0):
    x, y, c = lax.axis_index("x"), lax.axis_index("y"), lax.axis_index("c")
    pos = (x, y, c, 2 * x + y)
    out = []
    for i, (s, d, flip) in enumerate(copies):
        peer = (1 - x if "x" in flip else x, 1 - y if "y" in flip else y, 1 - c if "c" in flip else c)
        out.append(pltpu.make_async_remote_copy(
            src_ref=s(refs, refs, pos), dst_ref=d(refs, refs, pos),
            send_sem=send_sems.at[sem_off + i], recv_sem=recv_sems.at[sem_off + i],
            device_id=peer, device_id_type=MESH))
    return out


def _shifted(copies, off):
    return [(lambda I, O, pos, s=s: s(I[off:], O[off:], pos), lambda I, O, pos, d=d: d(I[off:], O[off:], pos), flip)
            for s, d, flip in copies]


BARRIER_IDS = {"c": (1, 2), "ici": (3, 4)}


def _exchange_start(name, bufs, copies, turns, after=None):
    n, nr = len(bufs), len(copies)
    na = 0 if after is None else 1
    flips = sorted({flip for _, _, flip in copies})
    kind = "c" if flips == ["c"] else "ici"
    collective_id = BARRIER_IDS[kind][turns[kind] % 2]
    turns[kind] += 1

    def body(*refs):
        x, y, c = lax.axis_index("x"), lax.axis_index("y"), lax.axis_index("c")
        barrier = pltpu.get_barrier_semaphore()
        for flip in flips:
            peer = (1 - x if "x" in flip else x, 1 - y if "y" in flip else y, 1 - c if "c" in flip else c)
            pl.semaphore_signal(barrier, inc=1, device_id=peer, device_id_type=MESH)
        pl.semaphore_wait(barrier, len(flips))
        for cp in _descriptors(copies, refs[:n], refs[n + na], refs[n + na + 1]):
            cp.start()
        token = refs[2 * n + na + 2]
        token[...] = jnp.zeros_like(token)

    hbm = pl.BlockSpec(memory_space=pltpu.HBM)
    sem = pl.BlockSpec(memory_space=pltpu.SEMAPHORE)
    out = pl.pallas_call(
        body, name=name,
        in_specs=[hbm] * n + [pl.BlockSpec(memory_space=pl.ANY)] * na,
        out_specs=(sem, sem, *[hbm] * n, pl.BlockSpec(memory_space=pltpu.VMEM)),
        out_shape=(pltpu.SemaphoreType.DMA((nr,)), pltpu.SemaphoreType.DMA((nr,)),
                   *[pltpu.HBM(b.shape, b.dtype) for b in bufs], jax.ShapeDtypeStruct((SUBLANES, LANES), F32)),
        input_output_aliases={i: 2 + i for i in range(n)},
        compiler_params=pltpu.CompilerParams(has_side_effects=SIDE_EFFECT, collective_id=collective_id),
    )(*[pltpu.with_memory_space_constraint(b, pltpu.HBM) for b in bufs], *([after] * na))
    return out[0], out[1], list(out[2:2 + n]), out[2 + n]


def _exchange_wait(name, send_sems, recv_sems, bufs, copies, after, sem_off=0):
    n = len(bufs)

    def body(*refs):
        for cp in _descriptors(copies, refs[:n], refs[n], refs[n + 1], sem_off):
            cp.wait_send()
            cp.wait_recv()

    hbm = pl.BlockSpec(memory_space=pltpu.HBM)
    sem = pl.BlockSpec(memory_space=pltpu.SEMAPHORE)
    out = pl.pallas_call(
        body, name=name,
        in_specs=[hbm] * n + [sem, sem, pl.BlockSpec(memory_space=pl.ANY)],
        out_specs=tuple([hbm] * n),
        out_shape=tuple(pltpu.HBM(b.shape, b.dtype) for b in bufs),
        input_output_aliases={i: i for i in range(n)},
        compiler_params=pltpu.CompilerParams(has_side_effects=SIDE_EFFECT),
    )(*bufs, send_sems, recv_sems, after)
    return list(out)


FIRST = ("w_in",)
MID = ("ssm_w_glu", "w_out")
LATE = ("w_up", "w_down")
GROUPS = {"first": FIRST, "mid": MID, "late": LATE}
ARRIVALS = {"first": FIRST, "mid": MID, "up": ("w_up",), "down": ("w_down",)}


def _gather_copies(names, shard_shapes):
    def region(i, chip, c):
        half_axis, shard_axis = BIG[names[i]]
        ssize = shard_shapes[i][shard_axis]
        hsize = shard_shapes[i][half_axis] // 2
        return lambda ref: _view(_view(ref, shard_axis, chip * ssize, ssize), half_axis, c * hsize, hsize)

    ici, d2d = [], []
    for i in range(len(names)):
        for flip in FLIPS:
            ici.append((lambda I, O, pos, i=i: region(i, pos[3], pos[2])(I[i]),
                        lambda I, O, pos, i=i: region(i, pos[3], pos[2])(O[i]), flip))
            d2d.append((lambda I, O, pos, i=i, flip=flip: region(i, _peer_chip(pos, flip), pos[2])(I[i]),
                        lambda I, O, pos, i=i, flip=flip: region(i, _peer_chip(pos, flip), pos[2])(O[i]), "c"))
    return ici, d2d


def _half_shape(n, shape):
    r, cdim = shape
    return (r // 2, cdim) if BIG[n][0] == 0 else (r, cdim // 2)


def _sub_shape(n, shape):
    hr, hc = _half_shape(n, shape)
    return (hr, hc // 4) if BIG[n][1] == 1 else (hr // 4, hc)


def _pair_copies(names, shapes, with_pack, dst_off):
    n = len(names)

    def other_half(i, ref, pos):
        half_axis = BIG[names[i]][0]
        hsize = shapes[i][half_axis] // 2
        return _view(ref, half_axis, (1 - pos[2]) * hsize, hsize)

    copies = [(lambda I, O, pos, i=i: other_half(i, I[i], pos), lambda I, O, pos, i=i: O[dst_off + i], "c")
              for i in range(n)]
    if with_pack:
        copies.append((lambda I, O, pos: I[n], lambda I, O, pos: O[dst_off + n], "c"))
    return copies


def _chip_copies(names, shapes, pack_rows, dst_off):
    n = len(names)

    def piece(i, ref, chip):
        shard_axis = BIG[names[i]][1]
        ssize = _sub_shape(names[i], shapes[i])[shard_axis]
        return _view(ref, shard_axis, chip * ssize, ssize)

    copies = []
    for i in range(n):
        for slot, flip in enumerate(FLIPS):
            copies.append((lambda I, O, pos, i=i, flip=flip: piece(i, I[i], _peer_chip(pos, flip)),
                           lambda I, O, pos, i=i, slot=slot: O[dst_off + i].at[slot], flip))
    if pack_rows:
        for slot, flip in enumerate(FLIPS):
            copies.append((lambda I, O, pos: _view(I[n], 0, pos[2] * (pack_rows // 2), pack_rows // 2),
                           lambda I, O, pos, slot=slot: O[dst_off + n].at[slot], flip))
    return copies


class _Exchanges:
    def __init__(self, shards, tiny, kc):
        self.kc = kc
        wb = {n: _cast_into_full(shards[n], kc, BIG[n][1], "cast_" + n) for n in BIG_NAMES}
        self.gathering, self.forwarding, self.pairing, self.reducing = {}, {}, {}, {}
        self.turns = {"c": 0, "ici": 0}
        tiny_copies = [(lambda I, O, pos: I[0], lambda I, O, pos: O[1].at[pos[3]], flip) for flip in FLIPS]
        self.gathering["tiny"] = (0, 0, 2, tiny_copies, None)
        bufs, copies = [tiny, lax.empty((4,) + tiny.shape, F32)], list(tiny_copies)
        for group, names in ARRIVALS.items():
            ici, d2d = _gather_copies(names, [shards[n].shape for n in names])
            self.gathering[group] = (len(bufs), len(copies), len(names), ici, d2d)
            copies += _shifted(ici, len(bufs))
            bufs += [wb[n] for n in names]
        self.started = _exchange_start("gather_start", bufs, copies, self.turns)
        self.zero = self.started[3][0, 0]

    def _arrived(self, group, after):
        buf_off, sem_off, n, ici, _ = self.gathering[group]
        send_sems, recv_sems, bufs, _ = self.started
        return _exchange_wait("gather_%s_wait" % group, send_sems, recv_sems, bufs[buf_off:buf_off + n], ici, after,
                              sem_off)

    def small_params(self, kc):
        tiny, got = self._arrived("tiny", self.started[3])
        return lax.dynamic_update_index_in_dim(got, tiny, kc[0], 0)

    def forward(self, group, after):
        d2d = self.gathering[group][4]
        self.forwarding[group] = (_exchange_start("forward_%s_start" % group, self._arrived(group, after), d2d,
                                                  self.turns), d2d)
        return self.forwarding[group][0][3]

    def weights(self, group, after):
        if group not in self.forwarding:
            after = self.forward(group, after)
        (send_sems, recv_sems, bufs, _), d2d = self.forwarding[group]
        full = _exchange_wait("forward_%s_wait" % group, send_sems, recv_sems, bufs, d2d, after)
        return dict(zip(ARRIVALS[group], full))

    def grads_ready(self, group, grads):
        names = GROUPS[group]
        gs = [grads[n] for n in names]
        land = [lax.empty(_half_shape(n, g.shape), F32) for n, g in zip(names, gs)]
        copies = _pair_copies(names, [g.shape for g in gs], False, len(names))
        started = _exchange_start("pair_%s_start" % group, gs + land, copies, self.turns)
        self.pairing[group] = (started, copies)
        return started[3]

    def grads_send(self, group, after):
        names = GROUPS[group]
        n = len(names)
        (send_sems, recv_sems, bufs, _), copies = self.pairing[group]
        bufs = _exchange_wait("pair_%s_wait" % group, send_sems, recv_sems, bufs, copies, after)
        chip = [_pair_sum(bufs[i], bufs[n + i], self.kc, BIG[names[i]][0], "pair_sum_" + names[i], BF16)
                for i in range(n)]
        shapes = [bufs[i].shape for i in range(n)]
        land = [lax.empty((3,) + _sub_shape(names[i], shapes[i]), BF16) for i in range(n)]
        copies = _chip_copies(names, shapes, 0, n)
        started = _exchange_start("reduce_%s_start" % group, chip + land, copies, self.turns)
        self.reducing[group] = (started, copies)
        return started[3]

    def finish_pack(self, pack):
        kc = self.kc
        prow = pack.shape[0] // 2
        recv = _exchange("reduce_d2d", [pack], [jax.ShapeDtypeStruct(pack.shape, F32)], {}, [],
                         _pair_copies((), [], True, 0))
        chip_pack = _pair_sum(pack, recv[0], kc, None, "pair_sum_pack", F32)
        copies = _chip_copies((), [], pack.shape[0], 1)
        land = lax.empty((3, prow, pack.shape[1]), F32)
        pack_sems_s, pack_sems_r, pack_bufs, after = _exchange_start("reduce_pack_start", [chip_pack, land], copies,
                                                                     self.turns)

        names, chips, recvs = (), [], []
        for group, group_names in GROUPS.items():
            (send_sems, recv_sems, bufs, _), group_copies = self.reducing[group]
            bufs = _exchange_wait("reduce_%s_wait" % group, send_sems, recv_sems, bufs, group_copies, after)
            n = len(group_names)
            names, chips, recvs = names + group_names, chips + bufs[:n], recvs + bufs[n:]
            after = bufs[n]
        total = [_chip_sum(chips[i], recvs[i], kc, BIG[n][1], BIG[n][0], "chip_sum_" + n)
                 for i, n in enumerate(names)]

        def my_half(half_axis, ref, pos):
            hsize = ref.shape[half_axis] // 2
            return _view(ref, half_axis, pos[2] * hsize, hsize)

        swap = [(lambda I, O, pos, i=i, n=n: my_half(BIG[n][0], I[i], pos),
                 lambda I, O, pos, i=i, n=n: my_half(BIG[n][0], O[i], pos), "c") for i, n in enumerate(names)]
        self.swapping = (_exchange_start("swap_start", total, swap, self.turns), swap, names)

        chip_pack, recv_pack = _exchange_wait("reduce_pack_wait", pack_sems_s, pack_sems_r, pack_bufs, copies,
                                              self.swapping[0][3])
        total_pack = _chip_sum(chip_pack, recv_pack, kc, None, 0, "chip_sum_pack")
        swap = [(lambda I, O, pos: my_half(0, I[0], pos), lambda I, O, pos: my_half(0, O[0], pos), "c")]
        return _exchange("swap_pack", [total_pack], [jax.ShapeDtypeStruct(pack.shape, F32)], {0: 0}, [], swap)[0]

    def finish_big(self, after):
        (send_sems, recv_sems, bufs, _), swap, names = self.swapping
        return dict(zip(names, _exchange_wait("swap_wait", send_sems, recv_sems, bufs, swap, after)))


WEIGHTS = ("meta_tokens", "norm_mix_g", "w_in", "conv_w", "ssm_lam_re", "ssm_lam_im", "ssm_log_dt", "ssm_b_re",
           "ssm_b_im", "ssm_c_re", "ssm_c_im", "ssm_d", "ssm_w_glu", "gain_conv_out", "gain_ssm_out", "w_out",
           "norm_ffn_g", "w_up", "ffn_conv_w", "ffn_conv_b", "w_down", "norm_final_g")
TINY_SHARDED = ("meta_tokens", "conv_w", "ffn_conv_w")
REPLICATED = tuple(n for n in WEIGHTS if n not in BIG and n not in TINY_SHARDED)
PACK_COLS = 512


def _pack(arrays, row_mult, cols):
    flat = jnp.concatenate([a.reshape(-1).astype(F32) for a in arrays])
    n = flat.shape[0]
    total = -(-n // (row_mult * cols)) * (row_mult * cols)
    return jnp.concatenate([flat, jnp.zeros((total - n,), F32)]).reshape(total // cols, cols)


def _unpack(packed, shapes):
    flat = packed.reshape(-1)
    out, off = [], 0
    for s in shapes:
        n = math.prod(s)
        out.append(flat[off:off + n].reshape(s))
        off += n
    return out


def kernel(x, meta_tokens, norm_mix_g, w_in, conv_w, ssm_lam_re, ssm_lam_im, ssm_log_dt, ssm_b_re, ssm_b_im, ssm_c_re, ssm_c_im, ssm_d, ssm_w_glu, gain_conv_out, gain_ssm_out, w_out, norm_ffn_g, w_up, ffn_conv_w, ffn_conv_b, w_down, norm_final_g, loss_target, m_meta_tokens, m_norm_mix_g, m_w_in, m_conv_w, m_ssm_lam_re, m_ssm_lam_im, m_ssm_log_dt, m_ssm_b_re, m_ssm_b_im, m_ssm_c_re, m_ssm_c_im, m_ssm_d, m_ssm_w_glu, m_gain_conv_out, m_gain_ssm_out, m_w_out, m_norm_ffn_g, m_w_up, m_ffn_conv_w, m_ffn_conv_b, m_w_down, m_norm_final_g, v_meta_tokens, v_norm_mix_g, v_w_in, v_conv_w, v_ssm_lam_re, v_ssm_lam_im, v_ssm_log_dt, v_ssm_b_re, v_ssm_b_im, v_ssm_c_re, v_ssm_c_im, v_ssm_d, v_ssm_w_glu, v_gain_conv_out, v_gain_ssm_out, v_w_out, v_norm_ffn_g, v_w_up, v_ffn_conv_w, v_ffn_conv_b, v_w_down, v_norm_final_g):
    args = dict(locals())
    w = {n: args[n] for n in WEIGHTS}
    mom = {n: args["m_" + n] for n in WEIGHTS}
    var = {n: args["v_" + n] for n in WEIGHTS}
    kx, ky, kc_ = lax.axis_index("x"), lax.axis_index("y"), lax.axis_index("c")
    chip = 2 * kx + ky
    kc = jnp.stack([chip, kc_]).astype(jnp.int32)

    def squeeze(n, a):
        if n == "meta_tokens":
            return a
        if n == "norm_final_g":
            return a.reshape(1, -1)
        a = a[0]
        return a.reshape(1, -1) if a.ndim == 1 else a

    wl = {n: squeeze(n, w[n]) for n in WEIGHTS}
    ml = {n: squeeze(n, mom[n]) for n in WEIGHTS}
    vl = {n: squeeze(n, var[n]) for n in WEIGHTS}

    tiny = _pack([wl[n] for n in TINY_SHARDED], SUBLANES, LANES)
    ex = _Exchanges({n: wl[n] for n in BIG_NAMES}, tiny, kc)
    tiny_shapes = [wl[n].shape for n in TINY_SHARDED]
    tiny_all = ex.small_params(kc)
    tiny_parts = [_unpack(tiny_all[k], tiny_shapes) for k in range(4)]
    p = {n: wl[n] for n in WEIGHTS if n not in BIG}
    for j, n in enumerate(TINY_SHARDED):
        p[n] = jnp.concatenate([tiny_parts[k][j] for k in range(4)], axis=1)
    p["ssm_log_dt"] = wl["ssm_log_dt"].reshape(-1)

    loss_local, grad_x, grads = _local_step(x[0], loss_target[0], p, ex)

    small_names = REPLICATED + TINY_SHARDED
    small_shapes = [tuple(grads[n].shape) for n in small_names] + [(1,)]
    pack = _pack([grads[n] for n in small_names] + [loss_local.reshape(1)], 2 * 16, PACK_COLS)
    g_pack = ex.finish_pack(pack)
    g_small = dict(zip(small_names + ("loss",), _unpack(g_pack, small_shapes)))
    loss = g_small["loss"][0]
    swapped = ("ssm_b_re", "ssm_b_im")

    def view(n, a):
        if n in swapped:
            return jnp.swapaxes(a, -1, -2)
        return a.reshape(1, -1) if a.ndim == 1 else a

    g = {}
    for n in REPLICATED:
        g[n] = g_small[n].reshape(view(n, w[n]).shape)
    for n in TINY_SHARDED:
        cols = wl[n].shape[1]
        g[n] = lax.dynamic_slice_in_dim(g_small[n], chip * cols, cols, axis=1).reshape(w[n].shape)
    delta, new_m, new_v = {}, {}, {}
    small = [[view(n, d[n]) for n in small_names] for d in (w, mom, var)]
    small.insert(1, [g[n] for n in small_names])
    for d, outs in zip((delta, new_m, new_v), _adamw_whole(*small, "adamw_small")):
        d.update(zip(small_names, outs))
    for d in (g, delta, new_m, new_v):
        d.update({n: jnp.swapaxes(d[n], -1, -2) for n in swapped})
    g_big = ex.finish_big(delta[small_names[0]])
    for n in BIG_NAMES:
        g[n], delta[n], new_m[n], new_v[n] = _adamw(wl[n], g_big[n], ml[n], vl[n], "adamw_" + n)

    def like(n, a):
        return a.reshape(w[n].shape)

    return (loss, grad_x[None], *[like(n, g[n]) for n in WEIGHTS], *[like(n, delta[n]) for n in WEIGHTS],
            *[like(n, new_m[n]) for n in WEIGHTS], *[like(n, new_v[n]) for n in WEIGHTS])
```

```python
import functools
import math

import jax
import jax.numpy as jnp
from jax import lax
from jax.experimental import pallas as pl
from jax.experimental.pallas import tpu as pltpu

F32 = jnp.float32
BF16 = jnp.bfloat16
MESH = pl.DeviceIdType.MESH

N_META = 16
N_GROUPS = 32
GROUP = 16
STATE = 64
RMS_EPS = 1e-6
ADAM_LR = 0.001
ADAM_B1 = 0.9
ADAM_B2 = 0.999
ADAM_EPS = 1e-08
ADAM_WD = 0.01
ADAM_STEP = 10

LANES = 128
SUBLANES = 8
ROW_ALIGN = 128
ROW_TILES = 4
VMEM_LIMIT = 52 * 1024 * 1024
MM_VMEM_BUDGET = 40 * 1024 * 1024
GELU_C = math.sqrt(2.0 / math.pi)
GELU_A = 0.044715


def _cparams(*sem):
    return pltpu.CompilerParams(dimension_semantics=sem, vmem_limit_bytes=VMEM_LIMIT)


def _pick_tile(dim, cap, mult):
    best = None
    for t in range(mult, min(dim, cap) + 1, mult):
        if dim % t == 0:
            best = t
    return best if best is not None else dim


def _mm(a, b, mode, name, out_dtype=F32, acc_in=None, after=None):
    if mode == "tn":
        kdim, m = a.shape
    else:
        m, kdim = a.shape
    n = b.shape[0] if mode == "nt" else b.shape[1]
    tm = _pick_tile(m, 1408, LANES if mode == "tn" else 16)
    tk = _pick_tile(kdim, 2816, LANES)
    nk = kdim // tk
    out_bytes = jnp.dtype(out_dtype).itemsize
    for cap in (1408, 1024, 512, 256, LANES):
        tn = _pick_tile(n, cap, LANES)
        blocks = 2 * (tm * tk * 2 + tk * tn * 2 + tm * tn * out_bytes * (2 if acc_in is not None else 1))
        if blocks + (tm * tn * 4 if nk > 1 else 0) <= MM_VMEM_BUDGET:
            break
    has_acc = acc_in is not None

    def body(*refs):
        if after is not None:
            refs = refs[1:]
        if has_acc:
            a_ref, b_ref, c_ref, o_ref = refs[:4]
            rest = refs[4:]
        else:
            a_ref, b_ref, o_ref = refs[:3]
            c_ref = None
            rest = refs[3:]
        if mode == "nn":
            p = jnp.dot(a_ref[...], b_ref[...], preferred_element_type=F32)
        elif mode == "nt":
            p = lax.dot_general(a_ref[...], b_ref[...], (((1,), (1,)), ((), ())), preferred_element_type=F32)
        else:
            p = lax.dot_general(a_ref[...], b_ref[...], (((0,), (0,)), ((), ())), preferred_element_type=F32)
        if nk == 1:
            if has_acc:
                p = p + c_ref[...]
            o_ref[...] = p.astype(out_dtype)
        else:
            acc_ref = rest[0]
            k = pl.program_id(2)

            @pl.when(k == 0)
            def _():
                acc_ref[...] = p + c_ref[...] if has_acc else p

            @pl.when(k > 0)
            def _():
                acc_ref[...] += p

            @pl.when(k == nk - 1)
            def _():
                o_ref[...] = acc_ref[...].astype(out_dtype)

    if mode == "tn":
        a_spec = pl.BlockSpec((tk, tm), lambda i, j, k: (k, i))
    else:
        a_spec = pl.BlockSpec((tm, tk), lambda i, j, k: (i, k))
    if mode == "nt":
        b_spec = pl.BlockSpec((tn, tk), lambda i, j, k: (j, k))
    else:
        b_spec = pl.BlockSpec((tk, tn), lambda i, j, k: (k, j))
    o_spec = pl.BlockSpec((tm, tn), lambda i, j, k: (i, j))
    in_specs = [a_spec, b_spec] + ([o_spec] if has_acc else [])
    args = (a, b) + ((acc_in,) if has_acc else ())
    if after is not None:
        in_specs = [pl.BlockSpec(memory_space=pl.ANY)] + in_specs
        args = (after,) + args
    return pl.pallas_call(
        body, name=name, grid=(m // tm, n // tn, nk),
        in_specs=in_specs, out_specs=o_spec,
        out_shape=jax.ShapeDtypeStruct((m, n), out_dtype),
        scratch_shapes=[pltpu.VMEM((tm, tn), F32)] if nk > 1 else [],
        compiler_params=_cparams("parallel", "parallel", "arbitrary"),
    )(*args)


def _mm_rows(a, b, mode, name, ins, outs, epilogue, scratch=()):
    m, kdim = a.shape
    n = b.shape[0] if mode == "nt" else b.shape[1]
    tm = m // ROW_TILES
    tk = _pick_tile(kdim, 2816, LANES)
    nk = kdim // tk
    ni, no = len(ins), len(outs)

    def body(*refs):
        a_ref, b_ref = refs[:2]
        in_refs, out_refs, rest = refs[2:2 + ni], refs[2 + ni:2 + ni + no], refs[2 + ni + no:]
        i = pl.program_id(0)
        if mode == "nn":
            p = jnp.dot(a_ref[...], b_ref[...], preferred_element_type=F32)
        else:
            p = lax.dot_general(a_ref[...], b_ref[...], (((1,), (1,)), ((), ())), preferred_element_type=F32)
        if nk == 1:
            epilogue(p, i, in_refs, out_refs, rest)
        else:
            acc_ref = rest[0]
            k = pl.program_id(1)

            @pl.when(k == 0)
            def _():
                acc_ref[...] = p

            @pl.when(k > 0)
            def _():
                acc_ref[...] += p

            @pl.when(k == nk - 1)
            def _():
                epilogue(acc_ref[...], i, in_refs, out_refs, rest[1:])

    def spec(shape, kind):
        if kind == "rows":
            return pl.BlockSpec((tm,) + tuple(shape[1:]), lambda i, k: (i,) + (0,) * (len(shape) - 1))
        if kind == "whole":
            return pl.BlockSpec(tuple(shape), lambda i, k: (0,) * len(shape))
        return pl.BlockSpec(memory_space=pl.ANY)

    a_spec = pl.BlockSpec((tm, tk), lambda i, k: (i, k))
    b_spec = pl.BlockSpec((n, tk), lambda i, k: (0, k)) if mode == "nt" else pl.BlockSpec((tk, n), lambda i, k: (k, 0))
    return pl.pallas_call(
        body, name=name, grid=(ROW_TILES, nk),
        in_specs=[a_spec, b_spec] + [spec(x.shape, kind) for x, kind in ins],
        out_specs=[spec(shape, kind) for shape, _, kind in outs],
        out_shape=[jax.ShapeDtypeStruct(shape, dtype) for shape, dtype, _ in outs],
        scratch_shapes=([pltpu.VMEM((tm, n), F32)] if nk > 1 else []) + list(scratch),
        compiler_params=_cparams("arbitrary", "arbitrary"),
    )(a, b, *[x for x, _ in ins])


def _rows(shape_cols, tr, dtype=None):
    return pl.BlockSpec((tr, shape_cols), lambda i: (i, 0))


def _const(shape):
    return pl.BlockSpec(shape, lambda i: (0,) * len(shape))


def _rms(x):
    return lax.rsqrt(jnp.mean(x * x, axis=-1, keepdims=True) + RMS_EPS)


def _rms_bwd(x, r, g, dy):
    xn = x * r
    dxn = dy * g
    dx = r * (dxn - xn * jnp.mean(dxn * xn, axis=-1, keepdims=True))
    return dx, dy * xn


def _gelu(y):
    return 0.5 * y * (1.0 + jnp.tanh(GELU_C * (y + GELU_A * y * y * y)))


def _gelu_grad(y):
    t = jnp.tanh(GELU_C * (y + GELU_A * y * y * y))
    return 0.5 * (1.0 + t) + 0.5 * y * (1.0 - t * t) * GELU_C * (1.0 + 3.0 * GELU_A * y * y)


def _sigmoid(z):
    return 1.0 / (1.0 + jnp.exp(-z))


def _proj_res_norm(a, w, h, g, after, name):
    def epilogue(p, i, ins, outs, _):
        x = ins[0][...] + p
        outs[0][...] = x
        outs[1][...] = (x * _rms(x) * ins[1][...]).astype(BF16)

    return _mm_rows(a, w, "nn", name, [(h, "rows"), (g, "whole"), (after, "hbm")],
                    [(h.shape, F32, "rows"), (h.shape, BF16, "rows")], epilogue)


def _proj_norm_bwd(da, w, h, g, dres, after, name):
    d = h.shape[1]

    def epilogue(p, i, ins, outs, _):
        x = ins[0][...]
        dx, dgs = _rms_bwd(x, _rms(x), ins[1][...], p)
        dh = ins[2][...] + dx
        outs[0][...] = dh
        outs[1][...] = dh.astype(BF16)

        @pl.when(i == 0)
        def _():
            outs[2][...] = jnp.zeros_like(outs[2])

        outs[2][...] += jnp.sum(dgs, axis=0, keepdims=True)

    return _mm_rows(da, w, "nt", name, [(h, "rows"), (g, "whole"), (dres, "rows"), (after, "hbm")],
                    [(h.shape, F32, "rows"), (h.shape, BF16, "rows"), ((1, d), F32, "whole")], epilogue)


def _proj_input_norm_bwd(da, w, h, g, dres, after, n_real, name):
    tp, d = h.shape
    tr = tp // ROW_TILES

    def epilogue(p, i, ins, outs, scratch):
        h_ref, g_ref, dres_ref, _ = ins
        dx_ref, dmeta_ref, dg_ref = outs
        stage, sem = scratch
        x = h_ref[...]
        dx, dgs = _rms_bwd(x, _rms(x), g_ref[...], p)
        stage[...] = dres_ref[...] + dx

        @pl.when(i == 0)
        def _():
            dg_ref[...] = jnp.zeros_like(dg_ref)
            dmeta_ref[...] = stage[:N_META, :]

        dg_ref[...] += jnp.sum(dgs, axis=0, keepdims=True)
        for t in range(ROW_TILES):
            lo, hi = max(t * tr, N_META), min((t + 1) * tr, n_real)
            if hi > lo:
                @pl.when(i == t)
                def _(t=t, lo=lo, hi=hi):
                    cp = pltpu.make_async_copy(stage.at[pl.ds(lo - t * tr, hi - lo), :],
                                               dx_ref.at[pl.ds(lo - N_META, hi - lo), :], sem)
                    cp.start()
                    cp.wait()

    return _mm_rows(da, w, "nt", name, [(h, "rows"), (g, "whole"), (dres, "rows"), (after, "hbm")],
                    [((n_real - N_META, d), F32, "hbm"), ((N_META, d), F32, "whole"), ((1, d), F32, "whole")],
                    epilogue, scratch=[pltpu.VMEM((tr, d), F32), pltpu.SemaphoreType.DMA])


def _load_token_rows(tok_hbm, buf, sem, tr, n_real, head=None, wait=False, i=None):
    i = pl.program_id(0) if i is None else i
    for t in range(ROW_TILES):
        base = t * tr
        lo, hi = max(base, N_META), min(base + tr, n_real)

        @pl.when(i == t)
        def _(base=base, lo=lo, hi=hi):
            if hi > lo:
                cp = pltpu.make_async_copy(tok_hbm.at[pl.ds(lo - N_META, hi - lo), :],
                                           buf.at[pl.ds(lo - base, hi - lo), :], sem)
                if wait:
                    cp.wait()
                    return
                cp.start()
            if wait:
                return
            if base < N_META:
                buf[0:N_META - base, :] = (jnp.zeros((N_META - base, buf.shape[1]), F32) if head is None
                                           else head[base:N_META, :])
            if hi < base + tr:
                buf[max(hi, base) - base:tr, :] = jnp.zeros((base + tr - max(hi, base), buf.shape[1]), F32)


def _input_norm_fwd(x, meta, g, tp, name):
    seq, d = x.shape
    tr = tp // ROW_TILES
    n_real = N_META + seq

    def body(x_hbm, meta_ref, g_ref, h_ref, hn_ref, buf, sem):
        _load_token_rows(x_hbm, buf, sem, tr, n_real, head=meta_ref)
        _load_token_rows(x_hbm, buf, sem, tr, n_real, wait=True)
        h = buf[...]
        h_ref[...] = h
        hn_ref[...] = (h * _rms(h) * g_ref[...]).astype(BF16)

    return pl.pallas_call(
        body, name=name, grid=(ROW_TILES,),
        in_specs=[pl.BlockSpec(memory_space=pl.ANY), _const((N_META, d)), _const((1, d))],
        out_specs=[_rows(d, tr), _rows(d, tr)],
        out_shape=[jax.ShapeDtypeStruct((tp, d), F32), jax.ShapeDtypeStruct((tp, d), BF16)],
        scratch_shapes=[pltpu.VMEM((tr, d), F32), pltpu.SemaphoreType.DMA],
        compiler_params=_cparams("arbitrary"))(x, meta, g)


def _proj_loss_bwd(act, w, h1, target, g, n_real, name):
    tp, d = h1.shape
    tr = tp // ROW_TILES

    def epilogue(p, i, ins, outs, scratch):
        h1_ref, t_hbm, g_ref = ins
        loss_ref, dh_ref, dhb_ref, dg_ref = outs
        t_buf, sem = scratch
        _load_token_rows(t_hbm, t_buf, sem, tr, n_real, i=i)
        x = h1_ref[...] + p
        r = _rms(x)
        row = i * tr + lax.broadcasted_iota(jnp.int32, (tr, d), 0)
        valid = (row >= N_META) & (row < n_real)
        _load_token_rows(t_hbm, t_buf, sem, tr, n_real, wait=True, i=i)
        e = jnp.where(valid, x * r * g_ref[...] - t_buf[...], 0.0)
        dx, dgs = _rms_bwd(x, r, g_ref[...], e * (1.0 / d))
        dh_ref[...] = dx
        dhb_ref[...] = dx.astype(BF16)

        @pl.when(i == 0)
        def _():
            dg_ref[...] = jnp.zeros_like(dg_ref)
            loss_ref[...] = jnp.zeros_like(loss_ref)

        dg_ref[...] += jnp.sum(dgs, axis=0, keepdims=True)
        loss_ref[...] += (0.5 / d) * jnp.sum(jnp.sum(e * e, axis=0, keepdims=True), axis=1, keepdims=True)

    return _mm_rows(act, w, "nn", name, [(h1, "rows"), (target, "hbm"), (g, "whole")],
                    [((1, LANES), F32, "whole"), ((tp, d), F32, "rows"), ((tp, d), BF16, "rows"),
                     ((1, d), F32, "whole")],
                    epilogue, scratch=[pltpu.VMEM((tr, d), F32), pltpu.SemaphoreType.DMA])


def _mix_fwd(co, y, z, gc, gs, name):
    tp, dh = co.shape
    tr = tp // ROW_TILES

    def body(co_ref, y_ref, z_ref, gc_ref, gs_ref, m_ref):
        c = co_ref[...]
        m_ref[:, :dh] = (c * _rms(c) * gc_ref[...]).astype(BF16)
        so = _gelu(y_ref[...]) * _sigmoid(z_ref[...])
        m_ref[:, dh:] = (so * _rms(so) * gs_ref[...]).astype(BF16)

    return pl.pallas_call(
        body, name=name, grid=(ROW_TILES,),
        in_specs=[_rows(dh, tr)] * 3 + [_const((1, dh))] * 2,
        out_specs=_rows(2 * dh, tr),
        out_shape=jax.ShapeDtypeStruct((tp, 2 * dh), BF16),
        compiler_params=_cparams("parallel"))(co, y, z, gc, gs)


def _proj_mix_bwd(dh1b, w, co, y, z, gc, gs, name):
    tp, dh = co.shape

    def epilogue(p, i, ins, outs, _):
        co_ref, y_ref, z_ref, gc_ref, gs_ref = ins
        dco_ref, dz_ref, dgp_ref, dgc_ref, dgs_ref = outs
        c = co_ref[...]
        dco, dgc = _rms_bwd(c, _rms(c), gc_ref[...], p[:, :dh])
        dco_ref[...] = dco
        gl = _gelu(y_ref[...])
        sg = _sigmoid(z_ref[...])
        so = gl * sg
        dso, dgs = _rms_bwd(so, _rms(so), gs_ref[...], p[:, dh:])
        dz_ref[...] = (dso * gl * sg * (1.0 - sg)).astype(BF16)
        dgp_ref[...] = dso * sg

        @pl.when(i == 0)
        def _():
            dgc_ref[...] = jnp.zeros_like(dgc_ref)
            dgs_ref[...] = jnp.zeros_like(dgs_ref)

        dgc_ref[...] += jnp.sum(dgc, axis=0, keepdims=True)
        dgs_ref[...] += jnp.sum(dgs, axis=0, keepdims=True)

    return _mm_rows(dh1b, w, "nt", name,
                    [(co, "rows"), (y, "rows"), (z, "rows"), (gc, "whole"), (gs, "whole")],
                    [((tp, dh), F32, "rows"), ((tp, dh), BF16, "rows"), ((tp, dh), F32, "rows"),
                     ((1, dh), F32, "whole"), ((1, dh), F32, "whole")], epilogue)


def _shift_down(x, k):
    row = lax.broadcasted_iota(jnp.int32, x.shape, 0)
    return jnp.where(row >= k, pltpu.roll(x, k, 0), 0.0)


def _shift_up(x, k):
    n = x.shape[0]
    row = lax.broadcasted_iota(jnp.int32, x.shape, 0)
    return jnp.where(row < n - k, pltpu.roll(x, n - k, 0), 0.0)


def _dwconv(x, w_ref):
    return w_ref[2:3, :] * x + w_ref[1:2, :] * _shift_down(x, 1) + w_ref[0:1, :] * _shift_down(x, 2)


def _dwconv_bwd(x, dy, w_ref):
    dx = w_ref[2:3, :] * dy + w_ref[1:2, :] * _shift_up(dy, 1) + w_ref[0:1, :] * _shift_up(dy, 2)
    dw = jnp.concatenate([jnp.sum(dy * _shift_down(x, 2), axis=0, keepdims=True),
                          jnp.sum(dy * _shift_down(x, 1), axis=0, keepdims=True),
                          jnp.sum(dy * x, axis=0, keepdims=True)], axis=0)
    return dx, dw


def _interleave(dst, src):
    seg_rows = src.shape[0] // SUBLANES
    for seg in range(SUBLANES):
        dst[pl.ds(seg, seg_rows, stride=SUBLANES), :] = src[seg * seg_rows:(seg + 1) * seg_rows, :]


def _deinterleave(dst, src):
    seg_rows = src.shape[0] // SUBLANES
    for seg in range(SUBLANES):
        dst[seg * seg_rows:(seg + 1) * seg_rows, :] = src[pl.ds(seg, seg_rows, stride=SUBLANES), :]


def _segment_shift(x, reverse):
    row = lax.broadcasted_iota(jnp.int32, x.shape, 0)
    if reverse:
        return jnp.where(row < SUBLANES - 1, pltpu.roll(x, SUBLANES - 1, 0), 0.0)
    return jnp.where(row >= 1, pltpu.roll(x, 1, 0), 0.0)


def _scan(s_re, s_im, pw_ref, reverse, pair=None):
    n_steps = s_re.shape[0] // SUBLANES
    n_strips = s_re.shape[1] // LANES
    sign = -1.0 if reverse else 1.0
    strips = [slice(st * LANES, (st + 1) * LANES) for st in range(n_strips)]

    def rows_of(j):
        step = (n_steps - 1 - j) if reverse else j
        return pl.ds(pl.multiple_of(step * SUBLANES, SUBLANES), SUBLANES)

    a = [(jnp.broadcast_to(pw_ref[0, 0:1, lanes], (SUBLANES, LANES)),
          sign * jnp.broadcast_to(pw_ref[1, 0:1, lanes], (SUBLANES, LANES))) for lanes in strips]

    def local(i, carry):
        for half in range(2):
            rows = rows_of(2 * i + half)
            out = []
            for st, lanes in enumerate(strips):
                (ar, ai), cr, ci = a[st], carry[2 * st], carry[2 * st + 1]
                xr = s_re[rows, lanes] + (ar * cr - ai * ci)
                xi = s_im[rows, lanes] + (ar * ci + ai * cr)
                s_re[rows, lanes] = xr
                s_im[rows, lanes] = xi
                out += [xr, xi]
            carry = tuple(out)
        return carry

    zero = jnp.zeros((SUBLANES, LANES), F32)
    ends = lax.fori_loop(0, n_steps // 2, local, (zero,) * (2 * n_strips))

    entering = []
    row = lax.broadcasted_iota(jnp.int32, (SUBLANES, LANES), 0)
    for st, lanes in enumerate(strips):
        tr, ti = ends[2 * st], ends[2 * st + 1]
        mr = jnp.broadcast_to(pw_ref[0, n_steps - 1:n_steps, lanes], (SUBLANES, LANES))
        mi = sign * jnp.broadcast_to(pw_ref[1, n_steps - 1:n_steps, lanes], (SUBLANES, LANES))
        for k in (1, 2, 4):
            keep = (row < SUBLANES - k) if reverse else (row >= k)
            rr = jnp.where(keep, pltpu.roll(tr, SUBLANES - k if reverse else k, 0), 0.0)
            ri = jnp.where(keep, pltpu.roll(ti, SUBLANES - k if reverse else k, 0), 0.0)
            tr, ti = tr + (mr * rr - mi * ri), ti + (mr * ri + mi * rr)
            mr, mi = mr * mr - mi * mi, 2.0 * mr * mi
        entering += [_segment_shift(tr, reverse), _segment_shift(ti, reverse)]

    def fix(i, carry):
        carry, sums = carry[:2 * n_strips], carry[2 * n_strips:]
        for half in range(2):
            j = 2 * i + half
            rows = rows_of(j)
            out, acc = [], []
            for st, lanes in enumerate(strips):
                (ar, ai), cr, ci = a[st], carry[2 * st], carry[2 * st + 1]
                cr, ci = ar * cr - ai * ci, ar * ci + ai * cr
                xr = s_re[rows, lanes] + cr
                xi = s_im[rows, lanes] + ci
                s_re[rows, lanes] = xr
                s_im[rows, lanes] = xi
                out += [cr, ci]
                if pair is not None:
                    p_rows = rows_of(jnp.minimum(j + 1, n_steps - 1))
                    keep = (j < n_steps - 1).astype(F32)
                    pr = pair[0][p_rows, lanes] * keep
                    pi = pair[1][p_rows, lanes] * keep
                    acc += [sums[2 * st] + (xr * pr + xi * pi), sums[2 * st + 1] + (xi * pr - xr * pi)]
            carry, sums = tuple(out), tuple(acc)
        return carry + sums

    n_sums = 0 if pair is None else 2 * n_strips
    out = lax.fori_loop(0, n_steps // 2, fix, tuple(entering) + (zero,) * n_sums)
    return out[2 * n_strips:]


def _seq_fwd(proj, conv_w, bc_re, bc_im, cc_re, cc_im, dskip, a_pow, name):
    tp = proj.shape[0]
    dh = proj.shape[1] // 4
    nq = dh // LANES
    sw = STATE * N_GROUPS // nq

    def body(b_ref, c_ref, v_ref, u_ref, w_ref, bre_ref, bim_ref, cre_ref, cim_ref, d_ref, pw_ref,
             co_ref, y_ref, g_ref, s_re, s_im, u_il, y_il):
        co_ref[...] = b_ref[...] * _dwconv(c_ref[...] * v_ref[...], w_ref)
        _interleave(u_il, u_ref)
        ub = u_il[...].astype(BF16)
        s_re[...] = jnp.dot(ub, bre_ref[...], preferred_element_type=F32)
        s_im[...] = jnp.dot(ub, bim_ref[...], preferred_element_type=F32)
        _scan(s_re, s_im, pw_ref, False)
        y_il[...] = (jnp.dot(s_re[...].astype(BF16), cre_ref[...], preferred_element_type=F32)
                     - jnp.dot(s_im[...].astype(BF16), cim_ref[...], preferred_element_type=F32))
        _deinterleave(y_ref, y_il)
        y = y_ref[...] + d_ref[...] * u_ref[...]
        y_ref[...] = y
        g_ref[...] = _gelu(y).astype(BF16)

    col = lambda off: pl.BlockSpec((tp, LANES), lambda q, off=off: (0, off * nq + q))
    blk = pl.BlockSpec((tp, LANES), lambda q: (0, q))
    return pl.pallas_call(
        body, name=name, grid=(nq,),
        in_specs=[col(0), col(1), col(2), col(3),
                  pl.BlockSpec((3, LANES), lambda q: (0, q)),
                  pl.BlockSpec((LANES, sw), lambda q: (0, q)), pl.BlockSpec((LANES, sw), lambda q: (0, q)),
                  pl.BlockSpec((sw, LANES), lambda q: (q, 0)), pl.BlockSpec((sw, LANES), lambda q: (q, 0)),
                  pl.BlockSpec((1, LANES), lambda q: (0, q)),
                  pl.BlockSpec((2, tp // SUBLANES, sw), lambda q: (0, 0, q))],
        out_specs=[blk, blk, blk],
        out_shape=[jax.ShapeDtypeStruct((tp, dh), F32), jax.ShapeDtypeStruct((tp, dh), F32),
                   jax.ShapeDtypeStruct((tp, dh), BF16)],
        scratch_shapes=[pltpu.VMEM((tp, sw), F32), pltpu.VMEM((tp, sw), F32),
                        pltpu.VMEM((tp, LANES), F32), pltpu.VMEM((tp, LANES), F32)],
        compiler_params=_cparams("parallel"),
    )(proj, proj, proj, proj, conv_w, bc_re, bc_im, cc_re, cc_im, dskip, a_pow)


def _conv_bwd(proj, dco, conv_w, name):
    tp = proj.shape[0]
    dh = proj.shape[1] // 4
    nq = dh // LANES

    def body(b_ref, c_ref, v_ref, dco_ref, w_ref, dproj_ref, dw_ref, stage, sem):
        q = pl.program_id(0)
        cg = c_ref[...]
        vg = v_ref[...]
        cv = cg * vg
        dco_v = dco_ref[...]
        dcv, dw = _dwconv_bwd(cv, dco_v * b_ref[...], w_ref)
        dw_ref[...] = dw
        stage[0] = (dco_v * _dwconv(cv, w_ref)).astype(BF16)
        stage[1] = (dcv * vg).astype(BF16)
        stage[2] = (dcv * cg).astype(BF16)
        copies = [pltpu.make_async_copy(stage.at[p], dproj_ref.at[:, pl.ds((p * nq + q) * LANES, LANES)], sem.at[p])
                  for p in range(3)]
        for cp in copies:
            cp.start()
        for cp in copies:
            cp.wait()

    col = lambda off: pl.BlockSpec((tp, LANES), lambda q, off=off: (0, off * nq + q))
    return pl.pallas_call(
        body, name=name, grid=(nq,),
        in_specs=[col(0), col(1), col(2), pl.BlockSpec((tp, LANES), lambda q: (0, q)),
                  pl.BlockSpec((3, LANES), lambda q: (0, q))],
        out_specs=[pl.BlockSpec(memory_space=pl.ANY), pl.BlockSpec((3, LANES), lambda q: (0, q))],
        out_shape=[jax.ShapeDtypeStruct((tp, 4 * dh), BF16), jax.ShapeDtypeStruct((3, dh), F32)],
        scratch_shapes=[pltpu.VMEM((3, tp, LANES), BF16), pltpu.SemaphoreType.DMA((3,))],
        compiler_params=_cparams("arbitrary"),
    )(proj, proj, proj, dco, conv_w)


def _ssm_bwd(proj, y, dg, dproj, bc_re, bc_im, cc_re, cc_im, dskip, a_pow, name):
    tp = proj.shape[0]
    dh = proj.shape[1] // 4
    nq = dh // LANES
    sw = STATE * N_GROUPS // nq

    def body(u_ref, y_ref, dg_ref, dproj_in, bre_ref, bim_ref, cre_ref, cim_ref, d_ref, pw_ref,
             dproj_ref, dbre_ref, dbim_ref, dcre_ref, dcim_ref, dd_ref, dar_ref, dai_ref,
             s_re, s_im, l_re, l_im, a_il, b_il, stage, sem):
        del dproj_in
        q = pl.program_id(0)
        nt = (((1,), (1,)), ((), ()))
        tn = (((0,), (0,)), ((), ()))
        _interleave(a_il, u_ref)
        ub = a_il[...].astype(BF16)
        s_re[...] = jnp.dot(ub, bre_ref[...], preferred_element_type=F32)
        s_im[...] = jnp.dot(ub, bim_ref[...], preferred_element_type=F32)
        _scan(s_re, s_im, pw_ref, False)
        dy_rows = dg_ref[...] * _gelu_grad(y_ref[...])
        dd_ref[...] = jnp.sum(dy_rows * u_ref[...], axis=0, keepdims=True)
        _interleave(b_il, dy_rows)
        dy = b_il[...]
        dyb = dy.astype(BF16)
        l_re[...] = lax.dot_general(dyb, cre_ref[...], nt, preferred_element_type=F32)
        l_im[...] = -lax.dot_general(dyb, cim_ref[...], nt, preferred_element_type=F32)
        dcre_ref[...] = lax.dot_general(s_re[...].astype(BF16), dyb, tn, preferred_element_type=F32)
        dcim_ref[...] = -lax.dot_general(s_im[...].astype(BF16), dyb, tn, preferred_element_type=F32)
        sums = _scan(l_re, l_im, pw_ref, True, pair=(s_re, s_im))
        rest = tp - SUBLANES
        for st in range(sw // LANES):
            lanes = slice(st * LANES, (st + 1) * LANES)
            lr0, li0 = l_re[:SUBLANES, lanes], l_im[:SUBLANES, lanes]
            pr0, pi0 = _segment_shift(s_re[rest:, lanes], False), _segment_shift(s_im[rest:, lanes], False)
            dar_ref[:, lanes] = jnp.sum(sums[2 * st] + (lr0 * pr0 + li0 * pi0), axis=0, keepdims=True)
            dai_ref[:, lanes] = jnp.sum(sums[2 * st + 1] + (li0 * pr0 - lr0 * pi0), axis=0, keepdims=True)
        lrb = l_re[...].astype(BF16)
        lib = l_im[...].astype(BF16)
        a_il[...] = (dy * d_ref[...] + lax.dot_general(lrb, bre_ref[...], nt, preferred_element_type=F32)
                     + lax.dot_general(lib, bim_ref[...], nt, preferred_element_type=F32))
        _deinterleave(b_il, a_il)
        stage[...] = b_il[...].astype(BF16)
        dbre_ref[...] = lax.dot_general(ub, lrb, tn, preferred_element_type=F32)
        dbim_ref[...] = lax.dot_general(ub, lib, tn, preferred_element_type=F32)
        cp = pltpu.make_async_copy(stage, dproj_ref.at[:, pl.ds((3 * nq + q) * LANES, LANES)], sem)
        cp.start()
        cp.wait()

    blk = pl.BlockSpec((tp, LANES), lambda q: (0, q))
    bspec = pl.BlockSpec((LANES, sw), lambda q: (0, q))
    cspec = pl.BlockSpec((sw, LANES), lambda q: (q, 0))
    tspec = pl.BlockSpec((2, tp // SUBLANES, sw), lambda q: (0, 0, q))
    nstate = STATE * N_GROUPS
    return pl.pallas_call(
        body, name=name, grid=(nq,),
        in_specs=[pl.BlockSpec((tp, LANES), lambda q: (0, 3 * nq + q)), blk, blk, pl.BlockSpec(memory_space=pl.ANY),
                  bspec, bspec, cspec, cspec, pl.BlockSpec((1, LANES), lambda q: (0, q)), tspec],
        out_specs=[pl.BlockSpec(memory_space=pl.ANY), bspec, bspec, cspec, cspec,
                   pl.BlockSpec((1, LANES), lambda q: (0, q)),
                   pl.BlockSpec((1, sw), lambda q: (0, q)), pl.BlockSpec((1, sw), lambda q: (0, q))],
        out_shape=[jax.ShapeDtypeStruct((tp, 4 * dh), BF16),
                   jax.ShapeDtypeStruct((LANES, nstate), F32), jax.ShapeDtypeStruct((LANES, nstate), F32),
                   jax.ShapeDtypeStruct((nstate, LANES), F32), jax.ShapeDtypeStruct((nstate, LANES), F32),
                   jax.ShapeDtypeStruct((1, dh), F32),
                   jax.ShapeDtypeStruct((1, nstate), F32), jax.ShapeDtypeStruct((1, nstate), F32)],
        input_output_aliases={3: 0},
        scratch_shapes=[pltpu.VMEM((tp, sw), F32)] * 4 + [pltpu.VMEM((tp, LANES), F32)] * 2
        + [pltpu.VMEM((tp, LANES), BF16), pltpu.SemaphoreType.DMA],
        compiler_params=_cparams("arbitrary"),
    )(proj, y, dg, dproj, bc_re, bc_im, cc_re, cc_im, dskip, a_pow)


FFN_TILE = 256


def _ffn_act(up, fw, fb, name):
    tp, two_ff = up.shape
    dff = two_ff // 2
    tc = FFN_TILE
    nj = dff // tc

    def body(ua_ref, uv_ref, wa_ref, wv_ref, ba_ref, bv_ref, act_ref):
        a = _dwconv(ua_ref[...], wa_ref) + ba_ref[...]
        v = _dwconv(uv_ref[...], wv_ref) + bv_ref[...]
        act_ref[...] = (a * _sigmoid(a) * v).astype(BF16)

    lo = lambda r: pl.BlockSpec((r, tc), lambda j: (0, j))
    hi = lambda r: pl.BlockSpec((r, tc), lambda j: (0, nj + j))
    return pl.pallas_call(
        body, name=name, grid=(nj,),
        in_specs=[lo(tp), hi(tp), lo(3), hi(3), lo(1), hi(1)],
        out_specs=lo(tp),
        out_shape=jax.ShapeDtypeStruct((tp, dff), BF16),
        compiler_params=_cparams("parallel"))(up, up, fw, fw, fb, fb)


def _ffn_bwd(up, dact, fw, fb, name):
    tp, two_ff = up.shape
    dff = two_ff // 2
    tc = FFN_TILE
    nj = dff // tc

    def body(ua_ref, uv_ref, da_ref, wa_ref, wv_ref, ba_ref, bv_ref,
             dup_ref, dwa_ref, dwv_ref, dba_ref, dbv_ref, stage, sem):
        j = pl.program_id(0)
        ua = ua_ref[...]
        uv = uv_ref[...]
        a = _dwconv(ua, wa_ref) + ba_ref[...]
        v = _dwconv(uv, wv_ref) + bv_ref[...]
        sg = _sigmoid(a)
        dact_v = da_ref[...]
        da = dact_v * v * sg * (1.0 + a * (1.0 - sg))
        dv = dact_v * a * sg
        dba_ref[...] = jnp.sum(da, axis=0, keepdims=True)
        dbv_ref[...] = jnp.sum(dv, axis=0, keepdims=True)
        dua, dwa = _dwconv_bwd(ua, da, wa_ref)
        duv, dwv = _dwconv_bwd(uv, dv, wv_ref)
        dwa_ref[...] = dwa
        dwv_ref[...] = dwv
        stage[0] = dua.astype(BF16)
        stage[1] = duv.astype(BF16)
        copies = [pltpu.make_async_copy(stage.at[p], dup_ref.at[:, pl.ds((p * nj + j) * tc, tc)], sem.at[p])
                  for p in range(2)]
        for cp in copies:
            cp.start()
        for cp in copies:
            cp.wait()

    lo = lambda r: pl.BlockSpec((r, tc), lambda j: (0, j))
    hi = lambda r: pl.BlockSpec((r, tc), lambda j: (0, nj + j))
    return pl.pallas_call(
        body, name=name, grid=(nj,),
        in_specs=[lo(tp), hi(tp), lo(tp), lo(3), hi(3), lo(1), hi(1)],
        out_specs=[pl.BlockSpec(memory_space=pl.ANY), lo(3), lo(3), lo(1), lo(1)],
        out_shape=[jax.ShapeDtypeStruct((tp, two_ff), BF16),
                   jax.ShapeDtypeStruct((3, dff), F32), jax.ShapeDtypeStruct((3, dff), F32),
                   jax.ShapeDtypeStruct((1, dff), F32), jax.ShapeDtypeStruct((1, dff), F32)],
        scratch_shapes=[pltpu.VMEM((2, tp, tc), BF16), pltpu.SemaphoreType.DMA((2,))],
        compiler_params=_cparams("arbitrary"))(up, up, dact, fw, fw, fb, fb)


def _zoh(lr, li, ld):
    dt = jnp.exp(ld)
    mag = jnp.exp(lr * dt)
    ang = li * dt
    ar = mag * jnp.cos(ang)
    ai = mag * jnp.sin(ang)
    den = lr * lr + li * li
    nr = ar - 1.0
    fr = (nr * lr + ai * li) / den
    fi = (ai * lr - nr * li) / den
    return dt, ar, ai, den, nr, fr, fi


def _s5_prep(lr, li, ld, b_re, b_im, n_pow, name):
    nstate = lr.shape[1]

    def body(lr_ref, li_ref, ld_ref, bre_ref, bim_ref, pw_ref, bcre_ref, bcim_ref):
        _, ar, ai, _, _, fr, fi = _zoh(lr_ref[...], li_ref[...], ld_ref[...])
        bre = bre_ref[...]
        bim = bim_ref[...]
        bcre_ref[...] = (fr * bre - fi * bim).astype(BF16)
        bcim_ref[...] = (fr * bim + fi * bre).astype(BF16)
        row = lax.broadcasted_iota(jnp.int32, (SUBLANES, nstate), 0)
        pr, pi = jnp.zeros((SUBLANES, nstate), F32), jnp.zeros((SUBLANES, nstate), F32)
        cr, ci = ar, ai
        for t in range(SUBLANES):
            pr, pi = jnp.where(row == t, cr, pr), jnp.where(row == t, ci, pi)
            cr, ci = cr * ar - ci * ai, cr * ai + ci * ar
        pw_ref[0, 0:SUBLANES, :] = pr
        pw_ref[1, 0:SUBLANES, :] = pi
        n = SUBLANES
        while n < n_pow:
            m = min(n, n_pow - n)
            tr, ti = pw_ref[0, n - 1:n, :], pw_ref[1, n - 1:n, :]
            xr, xi = pw_ref[0, 0:m, :], pw_ref[1, 0:m, :]
            pw_ref[0, n:n + m, :] = xr * tr - xi * ti
            pw_ref[1, n:n + m, :] = xr * ti + xi * tr
            n += m

    vmem = pl.BlockSpec(memory_space=pltpu.VMEM)
    return pl.pallas_call(
        body, name=name, in_specs=[vmem] * 5, out_specs=[vmem] * 3,
        out_shape=[jax.ShapeDtypeStruct((2, n_pow, nstate), F32)] + [jax.ShapeDtypeStruct(b_re.shape, BF16)] * 2,
        compiler_params=pltpu.CompilerParams(vmem_limit_bytes=VMEM_LIMIT))(lr, li, ld, b_re, b_im)


def _s5_prep_bwd(lr, li, ld, b_re, b_im, da_re, da_im, dbc_re, dbc_im, name):
    def body(lr_ref, li_ref, ld_ref, bre_ref, bim_ref, dar_ref, dai_ref, dbcre_ref, dbcim_ref,
             dlr_ref, dli_ref, dld_ref, dbre_ref, dbim_ref):
        lr, li = lr_ref[...], li_ref[...]
        dt, ar, ai, den, nr, fr, fi = _zoh(lr, li, ld_ref[...])
        bre, bim = bre_ref[...], bim_ref[...]
        gre, gim = dbcre_ref[...], dbcim_ref[...]
        dbre_ref[...] = fr * gre + fi * gim
        dbim_ref[...] = fr * gim - fi * gre
        g_fr = jnp.sum(gre * bre + gim * bim, axis=0, keepdims=True)
        g_fi = jnp.sum(gim * bre - gre * bim, axis=0, keepdims=True)
        g_ar = dar_ref[...] + (g_fr * lr - g_fi * li) / den
        g_ai = dai_ref[...] + (g_fr * li + g_fi * lr) / den
        d_lr = (g_fr * (nr - 2.0 * fr * lr) + g_fi * (ai - 2.0 * fi * lr)) / den
        d_li = (g_fr * (ai - 2.0 * fr * li) - g_fi * (nr + 2.0 * fi * li)) / den
        g_logmag = g_ar * ar + g_ai * ai
        g_ang = g_ai * ar - g_ar * ai
        dlr_ref[...] = d_lr + g_logmag * dt
        dli_ref[...] = d_li + g_ang * dt
        d_ld = (g_logmag * lr + g_ang * li) * dt
        n = d_ld.shape[1]
        sh = 1
        while sh < STATE:
            d_ld = d_ld + pltpu.roll(d_ld, n - sh, 1)
            sh *= 2
        dld_ref[...] = d_ld

    vmem = pl.BlockSpec(memory_space=pltpu.VMEM)
    row = jax.ShapeDtypeStruct(lr.shape, F32)
    return pl.pallas_call(
        body, name=name, in_specs=[vmem] * 9, out_specs=[vmem] * 5,
        out_shape=[row, row, row, jax.ShapeDtypeStruct(b_re.shape, F32), jax.ShapeDtypeStruct(b_re.shape, F32)],
    )(lr, li, ld, b_re, b_im, da_re, da_im, dbc_re, dbc_im)


def _compact_b(bb):
    bq = bb.reshape(N_GROUPS // 8, 8, STATE, GROUP)
    m = jnp.einsum("ab,qbph->qahbp", jnp.eye(8, dtype=bb.dtype), bq).reshape(N_GROUPS // 8, LANES, 8 * STATE)
    return m.transpose(1, 0, 2).reshape(LANES, N_GROUPS * STATE)


def _expand_b(m):
    d = m.reshape(8, GROUP, N_GROUPS // 8, 8, STATE)
    return jnp.einsum("ahqap->qahp", d).reshape(N_GROUPS, GROUP, STATE)


def _compact_c(c):
    cq = c.reshape(N_GROUPS // 8, 8, GROUP, STATE)
    return jnp.einsum("ab,qbhp->qbpah", jnp.eye(8, dtype=c.dtype), cq).reshape(N_GROUPS * STATE, LANES)


def _expand_c(m):
    d = m.reshape(N_GROUPS // 8, 8, STATE, 8, GROUP)
    return jnp.einsum("qbpbh->qbhp", d).reshape(N_GROUPS, GROUP, STATE)


def _local_step(x, target, p, ex):
    seq, d = x.shape
    n_real = N_META + seq
    tp = -(-n_real // ROW_ALIGN) * ROW_ALIGN

    h0, hn1 = _input_norm_fwd(x, p["meta_tokens"], p["norm_mix_g"] + ex.zero, tp, "norm_mix")
    ex.forward("first", hn1)
    nstate = N_GROUPS * STATE
    s5 = (p["ssm_lam_re"].reshape(1, nstate), p["ssm_lam_im"].reshape(1, nstate),
          jnp.repeat(p["ssm_log_dt"].reshape(-1), STATE).reshape(1, nstate),
          _compact_b(p["ssm_b_re"]), _compact_b(p["ssm_b_im"]))
    a_pow, bc_re, bc_im = _s5_prep(*s5, tp // SUBLANES, "s5_prep")
    cc_re = _compact_c(p["ssm_c_re"]).astype(BF16)
    cc_im = _compact_c(p["ssm_c_im"]).astype(BF16)
    dskip = p["ssm_d"].reshape(1, -1)
    first = ex.weights("first", bc_re)
    proj = _mm(hn1, first["w_in"], "nn", "proj")
    started = ex.forward("mid", proj)
    co, y, g = _seq_fwd(proj, p["conv_w"] + started[0, 0], bc_re, bc_im, cc_re, cc_im, dskip, a_pow, "seq_fwd")
    mid = ex.weights("mid", g)
    z = _mm(g, mid["ssm_w_glu"], "nn", "glu")
    mixed = _mix_fwd(co, y, z, p["gain_conv_out"], p["gain_ssm_out"], "mix_fwd")
    started = ex.forward("up", mixed)
    h1, hn2 = _proj_res_norm(mixed, mid["w_out"], h0, p["norm_ffn_g"], started, "out_proj_norm")
    late = ex.weights("up", hn2)
    up = _mm(hn2, late["w_up"], "nn", "up_proj")
    started = ex.forward("down", up)
    act = _ffn_act(up, p["ffn_conv_w"] + started[0, 0], p["ffn_conv_b"], "ffn_act")
    late.update(ex.weights("down", act))
    loss, dh2, dh2b, d_gfin = _proj_loss_bwd(act, late["w_down"], h1, target, p["norm_final_g"], n_real,
                                             "down_proj_loss")

    g_w_down = _mm(act, dh2b, "tn", "g_w_down")
    dact = _mm(dh2b, late["w_down"], "nt", "d_act")
    dup, dfw_a, dfw_v, dfb_a, dfb_v = _ffn_bwd(up, dact, p["ffn_conv_w"], p["ffn_conv_b"], "ffn_bwd")
    g_w_up = _mm(hn2, dup, "tn", "g_w_up")
    started = ex.grads_ready("late", {"w_up": g_w_up, "w_down": g_w_down})
    dh1, dh1b, d_gffn = _proj_norm_bwd(dup, late["w_up"], h1, p["norm_ffn_g"], dh2, started, "d_hn2_norm_bwd")
    started = ex.grads_send("late", dh1)
    g_w_out = _mm(mixed, dh1b, "tn", "g_w_out", after=started)
    dco, dz, dgp, d_gc, d_gs = _proj_mix_bwd(dh1b, mid["w_out"], co, y, z, p["gain_conv_out"],
                                             p["gain_ssm_out"], "d_mixed_mix_bwd")
    g_w_glu = _mm(g, dz, "tn", "g_w_glu")
    started = ex.grads_ready("mid", {"ssm_w_glu": g_w_glu, "w_out": g_w_out})
    dg = _mm(dz, mid["ssm_w_glu"], "nt", "d_gelu", acc_in=dgp, after=started)
    started = ex.grads_send("mid", dg)
    dproj, d_conv_w = _conv_bwd(proj, dco, p["conv_w"] + started[0, 0], "conv_bwd")
    (dproj, dbc_re, dbc_im, dcc_re, dcc_im, d_dskip, da_re, da_im) = _ssm_bwd(
        proj, y, dg, dproj, bc_re, bc_im, cc_re, cc_im, dskip, a_pow, "ssm_bwd")
    g_w_in = _mm(hn1, dproj, "tn", "g_w_in")
    started = ex.grads_ready("first", {"w_in": g_w_in})
    grad_x, d_meta, d_gmix = _proj_input_norm_bwd(dproj, first["w_in"], h0, p["norm_mix_g"], dh1, started, n_real,
                                                  "d_hn1_norm_bwd")
    started = ex.grads_send("first", d_gmix)

    d_lam_re, d_lam_im, d_log_dt, d_b_re, d_b_im = _s5_prep_bwd(*s5, da_re, da_im, dbc_re, dbc_im, "s5_prep_bwd")
    d_lam_re, d_lam_im = d_lam_re.reshape(N_GROUPS, STATE), d_lam_im.reshape(N_GROUPS, STATE)
    d_log_dt = d_log_dt[0, ::STATE]
    d_b_re, d_b_im = _expand_b(d_b_re), _expand_b(d_b_im)
    grads = {
        "meta_tokens": d_meta, "norm_mix_g": d_gmix, "w_in": g_w_in, "conv_w": d_conv_w,
        "ssm_lam_re": d_lam_re, "ssm_lam_im": d_lam_im, "ssm_log_dt": d_log_dt,
        "ssm_b_re": d_b_re, "ssm_b_im": d_b_im, "ssm_c_re": _expand_c(dcc_re), "ssm_c_im": _expand_c(dcc_im),
        "ssm_d": d_dskip.reshape(N_GROUPS, GROUP), "ssm_w_glu": g_w_glu,
        "gain_conv_out": d_gc, "gain_ssm_out": d_gs, "w_out": g_w_out, "norm_ffn_g": d_gffn,
        "w_up": g_w_up, "ffn_conv_w": jnp.concatenate([dfw_a, dfw_v], axis=1),
        "ffn_conv_b": jnp.concatenate([dfb_a, dfb_v], axis=1), "w_down": g_w_down, "norm_final_g": d_gfin,
    }
    return loss[0, 0] + started[0, 0], grad_x, grads


def _view(ref, axis, start, size):
    idx = [slice(None)] * len(ref.shape)
    idx[axis] = pl.ds(start, size)
    return ref.at[tuple(idx)]


def _exchange(name, ins, outs, aliases, local_copies, remote_copies):
    ni, no = len(ins), len(outs)
    nl, nr = len(local_copies), len(remote_copies)

    def body(*refs):
        in_refs, out_refs = refs[:ni], refs[ni:ni + no]
        send_sems, recv_sems, local_sems = refs[ni + no:]
        x, y, c = lax.axis_index("x"), lax.axis_index("y"), lax.axis_index("c")
        pos = (x, y, c, 2 * x + y)
        locals_ = [pltpu.make_async_copy(s(in_refs, out_refs, pos), d(in_refs, out_refs, pos), local_sems.at[i])
                   for i, (s, d) in enumerate(local_copies)]
        remotes = []
        for i, (s, d, flip) in enumerate(remote_copies):
            peer = (1 - x if "x" in flip else x, 1 - y if "y" in flip else y, 1 - c if "c" in flip else c)
            remotes.append(pltpu.make_async_remote_copy(
                src_ref=s(in_refs, out_refs, pos), dst_ref=d(in_refs, out_refs, pos),
                send_sem=send_sems.at[i], recv_sem=recv_sems.at[i], device_id=peer, device_id_type=MESH))
        for cp in locals_ + remotes:
            cp.start()
        for cp in remotes:
            cp.wait_recv()
        for cp in remotes:
            cp.wait_send()
        for cp in locals_:
            cp.wait()

    hbm = pl.BlockSpec(memory_space=pl.ANY)
    return pl.pallas_call(
        body, name=name, in_specs=[hbm] * ni, out_specs=[hbm] * no, out_shape=outs,
        input_output_aliases=aliases,
        scratch_shapes=[pltpu.SemaphoreType.DMA((nr,)), pltpu.SemaphoreType.DMA((nr,)),
                        pltpu.SemaphoreType.DMA((max(nl, 1),))],
    )(*ins)


BIG = {"w_in": (0, 1), "ssm_w_glu": (1, 0), "w_out": (1, 0), "w_up": (0, 1), "w_down": (1, 0)}
BIG_NAMES = tuple(BIG)
FLIPS = ("y", "x", "xy")


def _peer_chip(pos, flip):
    x, y, _, _ = pos
    return 2 * (1 - x if "x" in flip else x) + (1 - y if "y" in flip else y)


def _block_rows(rows, cols, itemsize, mult):
    return _pick_tile(rows, max(mult, (2 * 1024 * 1024) // (cols * itemsize)), mult)


def _cast_into_full(w, kc, shard_axis, name):
    r, cdim = w.shape
    tr = _block_rows(r, cdim, 4, 16)
    nb = r // tr

    def body(kc_ref, w_ref, o_ref):
        o_ref[...] = w_ref[...].astype(BF16)

    if shard_axis == 1:
        full, o_spec = (r, 4 * cdim), pl.BlockSpec((tr, cdim), lambda i, kc: (i, kc[0]))
    else:
        full, o_spec = (4 * r, cdim), pl.BlockSpec((tr, cdim), lambda i, kc: (kc[0] * nb + i, 0))
    return pl.pallas_call(
        body, name=name,
        grid_spec=pltpu.PrefetchScalarGridSpec(
            num_scalar_prefetch=1, grid=(nb,), in_specs=[pl.BlockSpec((tr, cdim), lambda i, kc: (i, 0))],
            out_specs=o_spec),
        out_shape=jax.ShapeDtypeStruct(full, BF16), compiler_params=_cparams("parallel"))(kc, w)


def _pair_sum(g, recv, kc, half_axis, name, out_dtype):
    hr, hc = recv.shape
    tr = _block_rows(hr, hc, 4, 16)
    nb = hr // tr

    def body(kc_ref, g_ref, r_ref, o_ref):
        o_ref[...] = (g_ref[...] + r_ref[...]).astype(out_dtype)

    if half_axis == 0:
        g_spec = pl.BlockSpec((tr, hc), lambda i, kc: (kc[1] * nb + i, 0))
    elif half_axis == 1:
        g_spec = pl.BlockSpec((tr, hc), lambda i, kc: (i, kc[1]))
    else:
        g_spec = pl.BlockSpec((tr, hc), lambda i, kc: (i, 0))
    same = pl.BlockSpec((tr, hc), lambda i, kc: (i, 0))
    return pl.pallas_call(
        body, name=name,
        grid_spec=pltpu.PrefetchScalarGridSpec(num_scalar_prefetch=1, grid=(nb,), in_specs=[g_spec, same],
                                               out_specs=same),
        out_shape=jax.ShapeDtypeStruct((hr, hc), out_dtype), compiler_params=_cparams("parallel"))(kc, g, recv)


def _chip_sum(own, recv, kc, own_axis, out_axis, name):
    _, sr, sc = recv.shape
    tr = _block_rows(sr, sc, 4, 16)
    nb = sr // tr

    def body(kc_ref, o_ref, r_ref, t_ref):
        k = kc_ref[0]
        own_v = o_ref[...].astype(F32)
        r = [r_ref[m].astype(F32) for m in range(3)]
        terms = []
        for kk in range(4):
            m = jnp.bitwise_xor(k, kk)
            terms.append(jnp.where(m == 0, own_v, jnp.where(m == 1, r[0], jnp.where(m == 2, r[1], r[2]))))
        t_ref[...] = (terms[0] + terms[1]) + (terms[2] + terms[3])

    if own_axis == 0:
        own_spec = pl.BlockSpec((tr, sc), lambda i, kc: (kc[0] * nb + i, 0))
    elif own_axis == 1:
        own_spec = pl.BlockSpec((tr, sc), lambda i, kc: (i, kc[0]))
    else:
        own_spec = pl.BlockSpec((tr, sc), lambda i, kc: (kc[1] * nb + i, 0))
    if out_axis == 0:
        out_full, out_spec = (2 * sr, sc), pl.BlockSpec((tr, sc), lambda i, kc: (kc[1] * nb + i, 0))
    else:
        out_full, out_spec = (sr, 2 * sc), pl.BlockSpec((tr, sc), lambda i, kc: (i, kc[1]))
    return pl.pallas_call(
        body, name=name,
        grid_spec=pltpu.PrefetchScalarGridSpec(
            num_scalar_prefetch=1, grid=(nb,),
            in_specs=[own_spec, pl.BlockSpec((3, tr, sc), lambda i, kc: (0, i, 0))],
            out_specs=out_spec),
        out_shape=jax.ShapeDtypeStruct(out_full, F32), compiler_params=_cparams("parallel"))(kc, own, recv)


def _adamw(w, g, m, v, name):
    r, cdim = w.shape
    tr = _block_rows(r, cdim, 4, 8)
    c1 = 1.0 - ADAM_B1 ** ADAM_STEP
    c2 = 1.0 - ADAM_B2 ** ADAM_STEP

    def body(w_ref, g_ref, m_ref, v_ref, go_ref, d_ref, nm_ref, nv_ref):
        gv = g_ref[...]
        go_ref[...] = gv
        nm = ADAM_B1 * m_ref[...] + (1.0 - ADAM_B1) * gv
        nv = ADAM_B2 * v_ref[...] + (1.0 - ADAM_B2) * (gv * gv)
        d_ref[...] = -ADAM_LR * ((nm / c1) / (jnp.sqrt(nv / c2) + ADAM_EPS) + ADAM_WD * w_ref[...])
        nm_ref[...] = nm
        nv_ref[...] = nv

    spec = _rows(cdim, tr)
    return pl.pallas_call(body, name=name, grid=(r // tr,), in_specs=[spec] * 4, out_specs=[spec] * 4,
                          out_shape=[jax.ShapeDtypeStruct((r, cdim), F32)] * 4,
                          compiler_params=_cparams("parallel"))(w, g, m, v)


def _adamw_whole(ws, gs, ms, vs, name):
    n = len(ws)
    c1 = 1.0 - ADAM_B1 ** ADAM_STEP
    c2 = 1.0 - ADAM_B2 ** ADAM_STEP

    def body(*refs):
        for i in range(n):
            w_ref, g_ref, m_ref, v_ref, d_ref, nm_ref, nv_ref = [refs[j * n + i] for j in range(7)]
            gv = g_ref[...]
            nm = ADAM_B1 * m_ref[...] + (1.0 - ADAM_B1) * gv
            nv = ADAM_B2 * v_ref[...] + (1.0 - ADAM_B2) * (gv * gv)
            d_ref[...] = -ADAM_LR * ((nm / c1) / (jnp.sqrt(nv / c2) + ADAM_EPS) + ADAM_WD * w_ref[...])
            nm_ref[...] = nm
            nv_ref[...] = nv

    vmem = pl.BlockSpec(memory_space=pltpu.VMEM)
    out = pl.pallas_call(body, name=name, in_specs=[vmem] * (4 * n), out_specs=[vmem] * (3 * n),
                         out_shape=[jax.ShapeDtypeStruct(a.shape, F32) for a in ws] * 3,
                         compiler_params=pltpu.CompilerParams(vmem_limit_bytes=VMEM_LIMIT))(*ws, *gs, *ms, *vs)
    return out[:n], out[n:2 * n], out[2 * n:]


SIDE_EFFECT = pltpu.SideEffectType.DATAFLOW_SIDE_EFFECTING


def _descriptors(copies, refs, send_sems, recv_sems, sem_off=0):
    x, y, c = lax.axis_index("x"), lax.axis_index("y"), lax.axis_index("c")
    pos = (x, y, c, 2 * x + y)
    out = []
    for i, (s, d, flip) in enumerate(copies):
        peer = (1 - x if "x" in flip else x, 1 - y if "y" in flip else y, 1 - c if "c" in flip else c)
        out.append(pltpu.make_async_remote_copy(
            src_ref=s(refs, refs, pos), dst_ref=d(refs, refs, pos),
            send_sem=send_sems.at[sem_off + i], recv_sem=recv_sems.at[sem_off + i],
            device_id=peer, device_id_type=MESH))
    return out


def _shifted(copies, off):
    return [(lambda I, O, pos, s=s: s(I[off:], O[off:], pos), lambda I, O, pos, d=d: d(I[off:], O[off:], pos), flip)
            for s, d, flip in copies]


BARRIER_IDS = {"c": (1, 2), "ici": (3, 4)}


def _exchange_start(name, bufs, copies, turns, after=None):
    n, nr = len(bufs), len(copies)
    na = 0 if after is None else 1
    flips = sorted({flip for _, _, flip in copies})
    kind = "c" if flips == ["c"] else "ici"
    collective_id = BARRIER_IDS[kind][turns[kind] % 2]
    turns[kind] += 1

    def body(*refs):
        x, y, c = lax.axis_index("x"), lax.axis_index("y"), lax.axis_index("c")
        barrier = pltpu.get_barrier_semaphore()
        for flip in flips:
            peer = (1 - x if "x" in flip else x, 1 - y if "y" in flip else y, 1 - c if "c" in flip else c)
            pl.semaphore_signal(barrier, inc=1, device_id=peer, device_id_type=MESH)
        pl.semaphore_wait(barrier, len(flips))
        for cp in _descriptors(copies, refs[:n], refs[n + na], refs[n + na + 1]):
            cp.start()
        token = refs[2 * n + na + 2]
        token[...] = jnp.zeros_like(token)

    hbm = pl.BlockSpec(memory_space=pltpu.HBM)
    sem = pl.BlockSpec(memory_space=pltpu.SEMAPHORE)
    out = pl.pallas_call(
        body, name=name,
        in_specs=[hbm] * n + [pl.BlockSpec(memory_space=pl.ANY)] * na,
        out_specs=(sem, sem, *[hbm] * n, pl.BlockSpec(memory_space=pltpu.VMEM)),
        out_shape=(pltpu.SemaphoreType.DMA((nr,)), pltpu.SemaphoreType.DMA((nr,)),
                   *[pltpu.HBM(b.shape, b.dtype) for b in bufs], jax.ShapeDtypeStruct((SUBLANES, LANES), F32)),
        input_output_aliases={i: 2 + i for i in range(n)},
        compiler_params=pltpu.CompilerParams(has_side_effects=SIDE_EFFECT, collective_id=collective_id),
    )(*[pltpu.with_memory_space_constraint(b, pltpu.HBM) for b in bufs], *([after] * na))
    return out[0], out[1], list(out[2:2 + n]), out[2 + n]


def _exchange_wait(name, send_sems, recv_sems, bufs, copies, after, sem_off=0):
    n = len(bufs)

    def body(*refs):
        for cp in _descriptors(copies, refs[:n], refs[n], refs[n + 1], sem_off):
            cp.wait_send()
            cp.wait_recv()

    hbm = pl.BlockSpec(memory_space=pltpu.HBM)
    sem = pl.BlockSpec(memory_space=pltpu.SEMAPHORE)
    out = pl.pallas_call(
        body, name=name,
        in_specs=[hbm] * n + [sem, sem, pl.BlockSpec(memory_space=pl.ANY)],
        out_specs=tuple([hbm] * n),
        out_shape=tuple(pltpu.HBM(b.shape, b.dtype) for b in bufs),
        input_output_aliases={i: i for i in range(n)},
        compiler_params=pltpu.CompilerParams(has_side_effects=SIDE_EFFECT),
    )(*bufs, send_sems, recv_sems, after)
    return list(out)


FIRST = ("w_in",)
MID = ("ssm_w_glu", "w_out")
LATE = ("w_up", "w_down")
GROUPS = {"first": FIRST, "mid": MID, "late": LATE}
ARRIVALS = {"first": FIRST, "mid": MID, "up": ("w_up",), "down": ("w_down",)}


def _gather_copies(names, shard_shapes):
    def region(i, chip, c):
        half_axis, shard_axis = BIG[names[i]]
        ssize = shard_shapes[i][shard_axis]
        hsize = shard_shapes[i][half_axis] // 2
        return lambda ref: _view(_view(ref, shard_axis, chip * ssize, ssize), half_axis, c * hsize, hsize)

    ici, d2d = [], []
    for i in range(len(names)):
        for flip in FLIPS:
            ici.append((lambda I, O, pos, i=i: region(i, pos[3], pos[2])(I[i]),
                        lambda I, O, pos, i=i: region(i, pos[3], pos[2])(O[i]), flip))
            d2d.append((lambda I, O, pos, i=i, flip=flip: region(i, _peer_chip(pos, flip), pos[2])(I[i]),
                        lambda I, O, pos, i=i, flip=flip: region(i, _peer_chip(pos, flip), pos[2])(O[i]), "c"))
    return ici, d2d


def _half_shape(n, shape):
    r, cdim = shape
    return (r // 2, cdim) if BIG[n][0] == 0 else (r, cdim // 2)


def _sub_shape(n, shape):
    hr, hc = _half_shape(n, shape)
    return (hr, hc // 4) if BIG[n][1] == 1 else (hr // 4, hc)


def _pair_copies(names, shapes, with_pack, dst_off):
    n = len(names)

    def other_half(i, ref, pos):
        half_axis = BIG[names[i]][0]
        hsize = shapes[i][half_axis] // 2
        return _view(ref, half_axis, (1 - pos[2]) * hsize, hsize)

    copies = [(lambda I, O, pos, i=i: other_half(i, I[i], pos), lambda I, O, pos, i=i: O[dst_off + i], "c")
              for i in range(n)]
    if with_pack:
        copies.append((lambda I, O, pos: I[n], lambda I, O, pos: O[dst_off + n], "c"))
    return copies


def _chip_copies(names, shapes, pack_rows, dst_off):
    n = len(names)

    def piece(i, ref, chip):
        shard_axis = BIG[names[i]][1]
        ssize = _sub_shape(names[i], shapes[i])[shard_axis]
        return _view(ref, shard_axis, chip * ssize, ssize)

    copies = []
    for i in range(n):
        for slot, flip in enumerate(FLIPS):
            copies.append((lambda I, O, pos, i=i, flip=flip: piece(i, I[i], _peer_chip(pos, flip)),
                           lambda I, O, pos, i=i, slot=slot: O[dst_off + i].at[slot], flip))
    if pack_rows:
        for slot, flip in enumerate(FLIPS):
            copies.append((lambda I, O, pos: _view(I[n], 0, pos[2] * (pack_rows // 2), pack_rows // 2),
                           lambda I, O, pos, slot=slot: O[dst_off + n].at[slot], flip))
    return copies


class _Exchanges:
    def __init__(self, shards, tiny, kc):
        self.kc = kc
        wb = {n: _cast_into_full(shards[n], kc, BIG[n][1], "cast_" + n) for n in BIG_NAMES}
        self.gathering, self.forwarding, self.pairing, self.reducing = {}, {}, {}, {}
        self.turns = {"c": 0, "ici": 0}
        tiny_copies = [(lambda I, O, pos: I[0], lambda I, O, pos: O[1].at[pos[3]], flip) for flip in FLIPS]
        self.gathering["tiny"] = (0, 0, 2, tiny_copies, None)
        bufs, copies = [tiny, lax.empty((4,) + tiny.shape, F32)], list(tiny_copies)
        for group, names in ARRIVALS.items():
            ici, d2d = _gather_copies(names, [shards[n].shape for n in names])
            self.gathering[group] = (len(bufs), len(copies), len(names), ici, d2d)
            copies += _shifted(ici, len(bufs))
            bufs += [wb[n] for n in names]
        self.started = _exchange_start("gather_start", bufs, copies, self.turns)
        self.zero = self.started[3][0, 0]

    def _arrived(self, group, after):
        buf_off, sem_off, n, ici, _ = self.gathering[group]
        send_sems, recv_sems, bufs, _ = self.started
        return _exchange_wait("gather_%s_wait" % group, send_sems, recv_sems, bufs[buf_off:buf_off + n], ici, after,
                              sem_off)

    def small_params(self, kc):
        tiny, got = self._arrived("tiny", self.started[3])
        return lax.dynamic_update_index_in_dim(got, tiny, kc[0], 0)

    def forward(self, group, after):
        d2d = self.gathering[group][4]
        self.forwarding[group] = (_exchange_start("forward_%s_start" % group, self._arrived(group, after), d2d,
                                                  self.turns), d2d)
        return self.forwarding[group][0][3]

    def weights(self, group, after):
        if group not in self.forwarding:
            after = self.forward(group, after)
        (send_sems, recv_sems, bufs, _), d2d = self.forwarding[group]
        full = _exchange_wait("forward_%s_wait" % group, send_sems, recv_sems, bufs, d2d, after)
        return dict(zip(ARRIVALS[group], full))

    def grads_ready(self, group, grads):
        names = GROUPS[group]
        gs = [grads[n] for n in names]
        land = [lax.empty(_half_shape(n, g.shape), F32) for n, g in zip(names, gs)]
        copies = _pair_copies(names, [g.shape for g in gs], False, len(names))
        started = _exchange_start("pair_%s_start" % group, gs + land, copies, self.turns)
        self.pairing[group] = (started, copies)
        return started[3]

    def grads_send(self, group, after):
        names = GROUPS[group]
        n = len(names)
        (send_sems, recv_sems, bufs, _), copies = self.pairing[group]
        bufs = _exchange_wait("pair_%s_wait" % group, send_sems, recv_sems, bufs, copies, after)
        chip = [_pair_sum(bufs[i], bufs[n + i], self.kc, BIG[names[i]][0], "pair_sum_" + names[i], BF16)
                for i in range(n)]
        shapes = [bufs[i].shape for i in range(n)]
        land = [lax.empty((3,) + _sub_shape(names[i], shapes[i]), BF16) for i in range(n)]
        copies = _chip_copies(names, shapes, 0, n)
        started = _exchange_start("reduce_%s_start" % group, chip + land, copies, self.turns)
        self.reducing[group] = (started, copies)
        return started[3]

    def finish_pack(self, pack):
        kc = self.kc
        prow = pack.shape[0] // 2
        recv = _exchange("reduce_d2d", [pack], [jax.ShapeDtypeStruct(pack.shape, F32)], {}, [],
                         _pair_copies((), [], True, 0))
        chip_pack = _pair_sum(pack, recv[0], kc, None, "pair_sum_pack", F32)
        copies = _chip_copies((), [], pack.shape[0], 1)
        land = lax.empty((3, prow, pack.shape[1]), F32)
        pack_sems_s, pack_sems_r, pack_bufs, after = _exchange_start("reduce_pack_start", [chip_pack, land], copies,
                                                                     self.turns)

        names, chips, recvs = (), [], []
        for group, group_names in GROUPS.items():
            (send_sems, recv_sems, bufs, _), group_copies = self.reducing[group]
            bufs = _exchange_wait("reduce_%s_wait" % group, send_sems, recv_sems, bufs, group_copies, after)
            n = len(group_names)
            names, chips, recvs = names + group_names, chips + bufs[:n], recvs + bufs[n:]
            after = bufs[n]
        total = [_chip_sum(chips[i], recvs[i], kc, BIG[n][1], BIG[n][0], "chip_sum_" + n)
                 for i, n in enumerate(names)]

        def my_half(half_axis, ref, pos):
            hsize = ref.shape[half_axis] // 2
            return _view(ref, half_axis, pos[2] * hsize, hsize)

        swap = [(lambda I, O, pos, i=i, n=n: my_half(BIG[n][0], I[i], pos),
                 lambda I, O, pos, i=i, n=n: my_half(BIG[n][0], O[i], pos), "c") for i, n in enumerate(names)]
        self.swapping = (_exchange_start("swap_start", total, swap, self.turns), swap, names)

        chip_pack, recv_pack = _exchange_wait("reduce_pack_wait", pack_sems_s, pack_sems_r, pack_bufs, copies,
                                              self.swapping[0][3])
        total_pack = _chip_sum(chip_pack, recv_pack, kc, None, 0, "chip_sum_pack")
        swap = [(lambda I, O, pos: my_half(0, I[0], pos), lambda I, O, pos: my_half(0, O[0], pos), "c")]
        return _exchange("swap_pack", [total_pack], [jax.ShapeDtypeStruct(pack.shape, F32)], {0: 0}, [], swap)[0]

    def finish_big(self, after):
        (send_sems, recv_sems, bufs, _), swap, names = self.swapping
        return dict(zip(names, _exchange_wait("swap_wait", send_sems, recv_sems, bufs, swap, after)))


WEIGHTS = ("meta_tokens", "norm_mix_g", "w_in", "conv_w", "ssm_lam_re", "ssm_lam_im", "ssm_log_dt", "ssm_b_re",
           "ssm_b_im", "ssm_c_re", "ssm_c_im", "ssm_d", "ssm_w_glu", "gain_conv_out", "gain_ssm_out", "w_out",
           "norm_ffn_g", "w_up", "ffn_conv_w", "ffn_conv_b", "w_down", "norm_final_g")
TINY_SHARDED = ("meta_tokens", "conv_w", "ffn_conv_w")
REPLICATED = tuple(n for n in WEIGHTS if n not in BIG and n not in TINY_SHARDED)
PACK_COLS = 512


def _pack(arrays, row_mult, cols):
    flat = jnp.concatenate([a.reshape(-1).astype(F32) for a in arrays])
    n = flat.shape[0]
    total = -(-n // (row_mult * cols)) * (row_mult * cols)
    return jnp.concatenate([flat, jnp.zeros((total - n,), F32)]).reshape(total // cols, cols)


def _unpack(packed, shapes):
    flat = packed.reshape(-1)
    out, off = [], 0
    for s in shapes:
        n = math.prod(s)
        out.append(flat[off:off + n].reshape(s))
        off += n
    return out


def kernel(x, meta_tokens, norm_mix_g, w_in, conv_w, ssm_lam_re, ssm_lam_im, ssm_log_dt, ssm_b_re, ssm_b_im, ssm_c_re, ssm_c_im, ssm_d, ssm_w_glu, gain_conv_out, gain_ssm_out, w_out, norm_ffn_g, w_up, ffn_conv_w, ffn_conv_b, w_down, norm_final_g, loss_target, m_meta_tokens, m_norm_mix_g, m_w_in, m_conv_w, m_ssm_lam_re, m_ssm_lam_im, m_ssm_log_dt, m_ssm_b_re, m_ssm_b_im, m_ssm_c_re, m_ssm_c_im, m_ssm_d, m_ssm_w_glu, m_gain_conv_out, m_gain_ssm_out, m_w_out, m_norm_ffn_g, m_w_up, m_ffn_conv_w, m_ffn_conv_b, m_w_down, m_norm_final_g, v_meta_tokens, v_norm_mix_g, v_w_in, v_conv_w, v_ssm_lam_re, v_ssm_lam_im, v_ssm_log_dt, v_ssm_b_re, v_ssm_b_im, v_ssm_c_re, v_ssm_c_im, v_ssm_d, v_ssm_w_glu, v_gain_conv_out, v_gain_ssm_out, v_w_out, v_norm_ffn_g, v_w_up, v_ffn_conv_w, v_ffn_conv_b, v_w_down, v_norm_final_g):
    args = dict(locals())
    w = {n: args[n] for n in WEIGHTS}
    mom = {n: args["m_" + n] for n in WEIGHTS}
    var = {n: args["v_" + n] for n in WEIGHTS}
    kx, ky, kc_ = lax.axis_index("x"), lax.axis_index("y"), lax.axis_index("c")
    chip = 2 * kx + ky
    kc = jnp.stack([chip, kc_]).astype(jnp.int32)

    def squeeze(n, a):
        if n == "meta_tokens":
            return a
        if n == "norm_final_g":
            return a.reshape(1, -1)
        a = a[0]
        return a.reshape(1, -1) if a.ndim == 1 else a

    wl = {n: squeeze(n, w[n]) for n in WEIGHTS}
    ml = {n: squeeze(n, mom[n]) for n in WEIGHTS}
    vl = {n: squeeze(n, var[n]) for n in WEIGHTS}

    tiny = _pack([wl[n] for n in TINY_SHARDED], SUBLANES, LANES)
    ex = _Exchanges({n: wl[n] for n in BIG_NAMES}, tiny, kc)
    tiny_shapes = [wl[n].shape for n in TINY_SHARDED]
    tiny_all = ex.small_params(kc)
    tiny_parts = [_unpack(tiny_all[k], tiny_shapes) for k in range(4)]
    p = {n: wl[n] for n in WEIGHTS if n not in BIG}
    for j, n in enumerate(TINY_SHARDED):
        p[n] = jnp.concatenate([tiny_parts[k][j] for k in range(4)], axis=1)
    p["ssm_log_dt"] = wl["ssm_log_dt"].reshape(-1)

    loss_local, grad_x, grads = _local_step(x[0], loss_target[0], p, ex)

    small_names = REPLICATED + TINY_SHARDED
    small_shapes = [tuple(grads[n].shape) for n in small_names] + [(1,)]
    pack = _pack([grads[n] for n in small_names] + [loss_local.reshape(1)], 2 * 16, PACK_COLS)
    g_pack = ex.finish_pack(pack)
    g_small = dict(zip(small_names + ("loss",), _unpack(g_pack, small_shapes)))
    loss = g_small["loss"][0]
    swapped = ("ssm_b_re", "ssm_b_im")

    def view(n, a):
        if n in swapped:
            return jnp.swapaxes(a, -1, -2)
        return a.reshape(1, -1) if a.ndim == 1 else a

    g = {}
    for n in REPLICATED:
        g[n] = g_small[n].reshape(view(n, w[n]).shape)
    for n in TINY_SHARDED:
        cols = wl[n].shape[1]
        g[n] = lax.dynamic_slice_in_dim(g_small[n], chip * cols, cols, axis=1).reshape(w[n].shape)
    delta, new_m, new_v = {}, {}, {}
    small = [[view(n, d[n]) for n in small_names] for d in (w, mom, var)]
    small.insert(1, [g[n] for n in small_names])
    for d, outs in zip((delta, new_m, new_v), _adamw_whole(*small, "adamw_small")):
        d.update(zip(small_names, outs))
    for d in (g, delta, new_m, new_v):
        d.update({n: jnp.swapaxes(d[n], -1, -2) for n in swapped})
    g_big = ex.finish_big(delta[small_names[0]])
    for n in BIG_NAMES:
        g[n], delta[n], new_m[n], new_v[n] = _adamw(wl[n], g_big[n], ml[n], vl[n], "adamw_" + n)

    def like(n, a):
        return a.reshape(w[n].shape)

    return (loss, grad_x[None], *[like(n, g[n]) for n in WEIGHTS], *[like(n, delta[n]) for n in WEIGHTS],
            *[like(n, new_m[n]) for n in WEIGHTS], *[like(n, new_v[n]) for n in WEIGHTS])
```

```python
import functools
import math

import jax
import jax.numpy as jnp
from jax import lax
from jax.experimental import pallas as pl
from jax.experimental.pallas import tpu as pltpu

F32 = jnp.float32
BF16 = jnp.bfloat16
MESH = pl.DeviceIdType.MESH

N_META = 16
N_GROUPS = 32
GROUP = 16
STATE = 64
RMS_EPS = 1e-6
ADAM_LR = 0.001
ADAM_B1 = 0.9
ADAM_B2 = 0.999
ADAM_EPS = 1e-08
ADAM_WD = 0.01
ADAM_STEP = 10

LANES = 128
SUBLANES = 8
ROW_ALIGN = 128
ROW_TILES = 4
VMEM_LIMIT = 52 * 1024 * 1024
MM_VMEM_BUDGET = 40 * 1024 * 1024
GELU_C = math.sqrt(2.0 / math.pi)
GELU_A = 0.044715


def _cparams(*sem):
    return pltpu.CompilerParams(dimension_semantics=sem, vmem_limit_bytes=VMEM_LIMIT)


def _pick_tile(dim, cap, mult):
    best = None
    for t in range(mult, min(dim, cap) + 1, mult):
        if dim % t == 0:
            best = t
    return best if best is not None else dim


def _mm(a, b, mode, name, out_dtype=F32, acc_in=None, after=None):
    if mode == "tn":
        kdim, m = a.shape
    else:
        m, kdim = a.shape
    n = b.shape[0] if mode == "nt" else b.shape[1]
    tm = _pick_tile(m, 1408, LANES if mode == "tn" else 16)
    tk = _pick_tile(kdim, 2816, LANES)
    nk = kdim // tk
    out_bytes = jnp.dtype(out_dtype).itemsize
    for cap in (1408, 1024, 512, 256, LANES):
        tn = _pick_tile(n, cap, LANES)
        blocks = 2 * (tm * tk * 2 + tk * tn * 2 + tm * tn * out_bytes * (2 if acc_in is not None else 1))
        if blocks + (tm * tn * 4 if nk > 1 else 0) <= MM_VMEM_BUDGET:
            break
    has_acc = acc_in is not None

    def body(*refs):
        if after is not None:
            refs = refs[1:]
        if has_acc:
            a_ref, b_ref, c_ref, o_ref = refs[:4]
            rest = refs[4:]
        else:
            a_ref, b_ref, o_ref = refs[:3]
            c_ref = None
            rest = refs[3:]
        if mode == "nn":
            p = jnp.dot(a_ref[...], b_ref[...], preferred_element_type=F32)
        elif mode == "nt":
            p = lax.dot_general(a_ref[...], b_ref[...], (((1,), (1,)), ((), ())), preferred_element_type=F32)
        else:
            p = lax.dot_general(a_ref[...], b_ref[...], (((0,), (0,)), ((), ())), preferred_element_type=F32)
        if nk == 1:
            if has_acc:
                p = p + c_ref[...]
            o_ref[...] = p.astype(out_dtype)
        else:
            acc_ref = rest[0]
            k = pl.program_id(2)

            @pl.when(k == 0)
            def _():
                acc_ref[...] = p + c_ref[...] if has_acc else p

            @pl.when(k > 0)
            def _():
                acc_ref[...] += p

            @pl.when(k == nk - 1)
            def _():
                o_ref[...] = acc_ref[...].astype(out_dtype)

    if mode == "tn":
        a_spec = pl.BlockSpec((tk, tm), lambda i, j, k: (k, i))
    else:
        a_spec = pl.BlockSpec((tm, tk), lambda i, j, k: (i, k))
    if mode == "nt":
        b_spec = pl.BlockSpec((tn, tk), lambda i, j, k: (j, k))
    else:
        b_spec = pl.BlockSpec((tk, tn), lambda i, j, k: (k, j))
    o_spec = pl.BlockSpec((tm, tn), lambda i, j, k: (i, j))
    in_specs = [a_spec, b_spec] + ([o_spec] if has_acc else [])
    args = (a, b) + ((acc_in,) if has_acc else ())
    if after is not None:
        in_specs = [pl.BlockSpec(memory_space=pl.ANY)] + in_specs
        args = (after,) + args
    return pl.pallas_call(
        body, name=name, grid=(m // tm, n // tn, nk),
        in_specs=in_specs, out_specs=o_spec,
        out_shape=jax.ShapeDtypeStruct((m, n), out_dtype),
        scratch_shapes=[pltpu.VMEM((tm, tn), F32)] if nk > 1 else [],
        compiler_params=_cparams("parallel", "parallel", "arbitrary"),
    )(*args)


def _mm_rows(a, b, mode, name, ins, outs, epilogue, scratch=()):
    m, kdim = a.shape
    n = b.shape[0] if mode == "nt" else b.shape[1]
    tm = m // ROW_TILES
    tk = _pick_tile(kdim, 2816, LANES)
    nk = kdim // tk
    ni, no = len(ins), len(outs)

    def body(*refs):
        a_ref, b_ref = refs[:2]
        in_refs, out_refs, rest = refs[2:2 + ni], refs[2 + ni:2 + ni + no], refs[2 + ni + no:]
        i = pl.program_id(0)
        if mode == "nn":
            p = jnp.dot(a_ref[...], b_ref[...], preferred_element_type=F32)
        else:
            p = lax.dot_general(a_ref[...], b_ref[...], (((1,), (1,)), ((), ())), preferred_element_type=F32)
        if nk == 1:
            epilogue(p, i, in_refs, out_refs, rest)
        else:
            acc_ref = rest[0]
            k = pl.program_id(1)

            @pl.when(k == 0)
            def _():
                acc_ref[...] = p

            @pl.when(k > 0)
            def _():
                acc_ref[...] += p

            @pl.when(k == nk - 1)
            def _():
                epilogue(acc_ref[...], i, in_refs, out_refs, rest[1:])

    def spec(shape, kind):
        if kind == "rows":
            return pl.BlockSpec((tm,) + tuple(shape[1:]), lambda i, k: (i,) + (0,) * (len(shape) - 1))
        if kind == "whole":
            return pl.BlockSpec(tuple(shape), lambda i, k: (0,) * len(shape))
        return pl.BlockSpec(memory_space=pl.ANY)

    a_spec = pl.BlockSpec((tm, tk), lambda i, k: (i, k))
    b_spec = pl.BlockSpec((n, tk), lambda i, k: (0, k)) if mode == "nt" else pl.BlockSpec((tk, n), lambda i, k: (k, 0))
    return pl.pallas_call(
        body, name=name, grid=(ROW_TILES, nk),
        in_specs=[a_spec, b_spec] + [spec(x.shape, kind) for x, kind in ins],
        out_specs=[spec(shape, kind) for shape, _, kind in outs],
        out_shape=[jax.ShapeDtypeStruct(shape, dtype) for shape, dtype, _ in outs],
        scratch_shapes=([pltpu.VMEM((tm, n), F32)] if nk > 1 else []) + list(scratch),
        compiler_params=_cparams("arbitrary", "arbitrary"),
    )(a, b, *[x for x, _ in ins])


def _rows(shape_cols, tr, dtype=None):
    return pl.BlockSpec((tr, shape_cols), lambda i: (i, 0))


def _const(shape):
    return pl.BlockSpec(shape, lambda i: (0,) * len(shape))


def _rms(x):
    return lax.rsqrt(jnp.mean(x * x, axis=-1, keepdims=True) + RMS_EPS)


def _rms_bwd(x, r, g, dy):
    xn = x * r
    dxn = dy * g
    dx = r * (dxn - xn * jnp.mean(dxn * xn, axis=-1, keepdims=True))
    return dx, dy * xn


def _gelu(y):
    return 0.5 * y * (1.0 + jnp.tanh(GELU_C * (y + GELU_A * y * y * y)))


def _gelu_grad(y):
    t = jnp.tanh(GELU_C * (y + GELU_A * y * y * y))
    return 0.5 * (1.0 + t) + 0.5 * y * (1.0 - t * t) * GELU_C * (1.0 + 3.0 * GELU_A * y * y)


def _sigmoid(z):
    return 1.0 / (1.0 + jnp.exp(-z))


def _proj_res_norm(a, w, h, g, after, n_real, name):
    tr = h.shape[0] // ROW_TILES

    def epilogue(p, i, ins, outs, _):
        x = ins[0][...] + p
        outs[0][...] = x
        row = i * tr + lax.broadcasted_iota(jnp.int32, x.shape, 0)
        outs[1][...] = jnp.where(row < n_real, x * _rms(x) * ins[1][...], 0.0).astype(BF16)

    return _mm_rows(a, w, "nn", name, [(h, "rows"), (g, "whole"), (after, "hbm")],
                    [(h.shape, F32, "rows"), (h.shape, BF16, "rows")], epilogue)


def _proj_norm_bwd(da, w, h, g, dres, after, name):
    d = h.shape[1]

    def epilogue(p, i, ins, outs, _):
        x = ins[0][...]
        dx, dgs = _rms_bwd(x, _rms(x), ins[1][...], p)
        dh = ins[2][...] + dx
        outs[0][...] = dh
        outs[1][...] = dh.astype(BF16)

        @pl.when(i == 0)
        def _():
            outs[2][...] = jnp.zeros_like(outs[2])

        outs[2][...] += jnp.sum(dgs, axis=0, keepdims=True)

    return _mm_rows(da, w, "nt", name, [(h, "rows"), (g, "whole"), (dres, "rows"), (after, "hbm")],
                    [(h.shape, F32, "rows"), (h.shape, BF16, "rows"), ((1, d), F32, "whole")], epilogue)


def _proj_input_norm_bwd(da, w, h, g, dres, after, n_real, name):
    tp, d = h.shape
    tr = tp // ROW_TILES

    def epilogue(p, i, ins, outs, scratch):
        h_ref, g_ref, dres_ref, _ = ins
        dx_ref, dmeta_ref, dg_ref = outs
        stage, sem = scratch
        x = h_ref[...]
        dx, dgs = _rms_bwd(x, _rms(x), g_ref[...], p)
        stage[...] = dres_ref[...] + dx

        @pl.when(i == 0)
        def _():
            dg_ref[...] = jnp.zeros_like(dg_ref)
            dmeta_ref[...] = stage[:N_META, :]

        dg_ref[...] += jnp.sum(dgs, axis=0, keepdims=True)
        for t in range(ROW_TILES):
            lo, hi = max(t * tr, N_META), min((t + 1) * tr, n_real)
            if hi > lo:
                @pl.when(i == t)
                def _(t=t, lo=lo, hi=hi):
                    cp = pltpu.make_async_copy(stage.at[pl.ds(lo - t * tr, hi - lo), :],
                                               dx_ref.at[pl.ds(lo - N_META, hi - lo), :], sem)
                    cp.start()
                    cp.wait()

    return _mm_rows(da, w, "nt", name, [(h, "rows"), (g, "whole"), (dres, "rows"), (after, "hbm")],
                    [((n_real - N_META, d), F32, "hbm"), ((N_META, d), F32, "whole"), ((1, d), F32, "whole")],
                    epilogue, scratch=[pltpu.VMEM((tr, d), F32), pltpu.SemaphoreType.DMA])


def _load_token_rows(tok_hbm, buf, sem, tr, n_real, head=None, wait=False, i=None):
    i = pl.program_id(0) if i is None else i
    for t in range(ROW_TILES):
        base = t * tr
        lo, hi = max(base, N_META), min(base + tr, n_real)

        @pl.when(i == t)
        def _(base=base, lo=lo, hi=hi):
            if hi > lo:
                cp = pltpu.make_async_copy(tok_hbm.at[pl.ds(lo - N_META, hi - lo), :],
                                           buf.at[pl.ds(lo - base, hi - lo), :], sem)
                if wait:
                    cp.wait()
                    return
                cp.start()
            if wait:
                return
            if base < N_META:
                buf[0:N_META - base, :] = (jnp.zeros((N_META - base, buf.shape[1]), F32) if head is None
                                           else head[base:N_META, :])
            if hi < base + tr:
                buf[max(hi, base) - base:tr, :] = jnp.zeros((base + tr - max(hi, base), buf.shape[1]), F32)


def _input_norm_fwd(x, meta, g, tp, name):
    seq, d = x.shape
    tr = tp // ROW_TILES
    n_real = N_META + seq

    def body(x_hbm, meta_ref, g_ref, h_ref, hn_ref, buf, sem):
        _load_token_rows(x_hbm, buf, sem, tr, n_real, head=meta_ref)
        _load_token_rows(x_hbm, buf, sem, tr, n_real, wait=True)
        h = buf[...]
        h_ref[...] = h
        hn_ref[...] = (h * _rms(h) * g_ref[...]).astype(BF16)

    return pl.pallas_call(
        body, name=name, grid=(ROW_TILES,),
        in_specs=[pl.BlockSpec(memory_space=pl.ANY), _const((N_META, d)), _const((1, d))],
        out_specs=[_rows(d, tr), _rows(d, tr)],
        out_shape=[jax.ShapeDtypeStruct((tp, d), F32), jax.ShapeDtypeStruct((tp, d), BF16)],
        scratch_shapes=[pltpu.VMEM((tr, d), F32), pltpu.SemaphoreType.DMA],
        compiler_params=_cparams("arbitrary"))(x, meta, g)


def _proj_loss_bwd(act, w, h1, target, g, n_real, name):
    tp, d = h1.shape
    tr = tp // ROW_TILES

    def epilogue(p, i, ins, outs, scratch):
        h1_ref, t_hbm, g_ref = ins
        loss_ref, dh_ref, dhb_ref, dg_ref = outs
        t_buf, sem = scratch
        _load_token_rows(t_hbm, t_buf, sem, tr, n_real, i=i)
        x = h1_ref[...] + p
        r = _rms(x)
        row = i * tr + lax.broadcasted_iota(jnp.int32, (tr, d), 0)
        valid = (row >= N_META) & (row < n_real)
        _load_token_rows(t_hbm, t_buf, sem, tr, n_real, wait=True, i=i)
        e = jnp.where(valid, x * r * g_ref[...] - t_buf[...], 0.0)
        dx, dgs = _rms_bwd(x, r, g_ref[...], e * (1.0 / d))
        dh_ref[...] = dx
        dhb_ref[...] = dx.astype(BF16)

        @pl.when(i == 0)
        def _():
            dg_ref[...] = jnp.zeros_like(dg_ref)
            loss_ref[...] = jnp.zeros_like(loss_ref)

        dg_ref[...] += jnp.sum(dgs, axis=0, keepdims=True)
        loss_ref[...] += (0.5 / d) * jnp.sum(jnp.sum(e * e, axis=0, keepdims=True), axis=1, keepdims=True)

    return _mm_rows(act, w, "nn", name, [(h1, "rows"), (target, "hbm"), (g, "whole")],
                    [((1, LANES), F32, "whole"), ((tp, d), F32, "rows"), ((tp, d), BF16, "rows"),
                     ((1, d), F32, "whole")],
                    epilogue, scratch=[pltpu.VMEM((tr, d), F32), pltpu.SemaphoreType.DMA])


def _mix_fwd(co, y, z, gc, gs, name):
    tp, dh = co.shape
    tr = tp // ROW_TILES

    def body(co_ref, y_ref, z_ref, gc_ref, gs_ref, m_ref):
        c = co_ref[...]
        m_ref[:, :dh] = (c * _rms(c) * gc_ref[...]).astype(BF16)
        so = _gelu(y_ref[...]) * _sigmoid(z_ref[...])
        m_ref[:, dh:] = (so * _rms(so) * gs_ref[...]).astype(BF16)

    return pl.pallas_call(
        body, name=name, grid=(ROW_TILES,),
        in_specs=[_rows(dh, tr)] * 3 + [_const((1, dh))] * 2,
        out_specs=_rows(2 * dh, tr),
        out_shape=jax.ShapeDtypeStruct((tp, 2 * dh), BF16),
        compiler_params=_cparams("parallel"))(co, y, z, gc, gs)


def _proj_mix_bwd(dh1b, w, co, y, z, gc, gs, name):
    tp, dh = co.shape

    def epilogue(p, i, ins, outs, _):
        co_ref, y_ref, z_ref, gc_ref, gs_ref = ins
        dco_ref, dz_ref, dgp_ref, dgc_ref, dgs_ref = outs
        c = co_ref[...]
        dco, dgc = _rms_bwd(c, _rms(c), gc_ref[...], p[:, :dh])
        dco_ref[...] = dco
        gl = _gelu(y_ref[...])
        sg = _sigmoid(z_ref[...])
        so = gl * sg
        dso, dgs = _rms_bwd(so, _rms(so), gs_ref[...], p[:, dh:])
        dz_ref[...] = (dso * gl * sg * (1.0 - sg)).astype(BF16)
        dgp_ref[...] = dso * sg

        @pl.when(i == 0)
        def _():
            dgc_ref[...] = jnp.zeros_like(dgc_ref)
            dgs_ref[...] = jnp.zeros_like(dgs_ref)

        dgc_ref[...] += jnp.sum(dgc, axis=0, keepdims=True)
        dgs_ref[...] += jnp.sum(dgs, axis=0, keepdims=True)

    return _mm_rows(dh1b, w, "nt", name,
                    [(co, "rows"), (y, "rows"), (z, "rows"), (gc, "whole"), (gs, "whole")],
                    [((tp, dh), F32, "rows"), ((tp, dh), BF16, "rows"), ((tp, dh), F32, "rows"),
                     ((1, dh), F32, "whole"), ((1, dh), F32, "whole")], epilogue)


def _shift_down(x, k):
    return pltpu.roll(x, k, 0)


def _shift_up(x, k):
    n = x.shape[0]
    row = lax.broadcasted_iota(jnp.int32, x.shape, 0)
    return jnp.where(row < n - k, pltpu.roll(x, n - k, 0), 0.0)


def _dwconv(x, w_ref):
    return w_ref[2:3, :] * x + w_ref[1:2, :] * _shift_down(x, 1) + w_ref[0:1, :] * _shift_down(x, 2)


def _dwconv_bwd(x, dy, w_ref):
    dx = w_ref[2:3, :] * dy + w_ref[1:2, :] * _shift_up(dy, 1) + w_ref[0:1, :] * _shift_up(dy, 2)
    dw = jnp.concatenate([jnp.sum(dy * _shift_down(x, 2), axis=0, keepdims=True),
                          jnp.sum(dy * _shift_down(x, 1), axis=0, keepdims=True),
                          jnp.sum(dy * x, axis=0, keepdims=True)], axis=0)
    return dx, dw


def _interleave(dst, src):
    seg_rows = src.shape[0] // SUBLANES
    for seg in range(SUBLANES):
        dst[pl.ds(seg, seg_rows, stride=SUBLANES), :] = src[seg * seg_rows:(seg + 1) * seg_rows, :]


def _deinterleave(dst, src):
    seg_rows = src.shape[0] // SUBLANES
    for seg in range(SUBLANES):
        dst[seg * seg_rows:(seg + 1) * seg_rows, :] = src[pl.ds(seg, seg_rows, stride=SUBLANES), :]


def _segment_shift(x, reverse):
    row = lax.broadcasted_iota(jnp.int32, x.shape, 0)
    if reverse:
        return jnp.where(row < SUBLANES - 1, pltpu.roll(x, SUBLANES - 1, 0), 0.0)
    return jnp.where(row >= 1, pltpu.roll(x, 1, 0), 0.0)


def _scan(s_re, s_im, pw_ref, reverse, pair=None):
    n_steps = s_re.shape[0] // SUBLANES
    n_strips = s_re.shape[1] // LANES
    sign = -1.0 if reverse else 1.0
    strips = [slice(st * LANES, (st + 1) * LANES) for st in range(n_strips)]

    def rows_of(j):
        step = (n_steps - 1 - j) if reverse else j
        return pl.ds(pl.multiple_of(step * SUBLANES, SUBLANES), SUBLANES)

    a = [(jnp.broadcast_to(pw_ref[0, 0:1, lanes], (SUBLANES, LANES)),
          sign * jnp.broadcast_to(pw_ref[1, 0:1, lanes], (SUBLANES, LANES))) for lanes in strips]

    def local(i, carry):
        for half in range(2):
            rows = rows_of(2 * i + half)
            out = []
            for st, lanes in enumerate(strips):
                (ar, ai), cr, ci = a[st], carry[2 * st], carry[2 * st + 1]
                xr = s_re[rows, lanes] + (ar * cr - ai * ci)
                xi = s_im[rows, lanes] + (ar * ci + ai * cr)
                s_re[rows, lanes] = xr
                s_im[rows, lanes] = xi
                out += [xr, xi]
            carry = tuple(out)
        return carry

    zero = jnp.zeros((SUBLANES, LANES), F32)
    ends = lax.fori_loop(0, n_steps // 2, local, (zero,) * (2 * n_strips))

    entering = []
    row = lax.broadcasted_iota(jnp.int32, (SUBLANES, LANES), 0)
    for st, lanes in enumerate(strips):
        tr, ti = ends[2 * st], ends[2 * st + 1]
        mr = jnp.broadcast_to(pw_ref[0, n_steps - 1:n_steps, lanes], (SUBLANES, LANES))
        mi = sign * jnp.broadcast_to(pw_ref[1, n_steps - 1:n_steps, lanes], (SUBLANES, LANES))
        for k in (1, 2, 4):
            keep = (row < SUBLANES - k) if reverse else (row >= k)
            rr = jnp.where(keep, pltpu.roll(tr, SUBLANES - k if reverse else k, 0), 0.0)
            ri = jnp.where(keep, pltpu.roll(ti, SUBLANES - k if reverse else k, 0), 0.0)
            tr, ti = tr + (mr * rr - mi * ri), ti + (mr * ri + mi * rr)
            mr, mi = mr * mr - mi * mi, 2.0 * mr * mi
        entering += [_segment_shift(tr, reverse), _segment_shift(ti, reverse)]

    def fix(i, carry):
        carry, sums = carry[:2 * n_strips], carry[2 * n_strips:]
        for half in range(2):
            j = 2 * i + half
            rows = rows_of(j)
            out, acc = [], []
            for st, lanes in enumerate(strips):
                (ar, ai), cr, ci = a[st], carry[2 * st], carry[2 * st + 1]
                cr, ci = ar * cr - ai * ci, ar * ci + ai * cr
                xr = s_re[rows, lanes] + cr
                xi = s_im[rows, lanes] + ci
                s_re[rows, lanes] = xr
                s_im[rows, lanes] = xi
                out += [cr, ci]
                if pair is not None:
                    p_rows = rows_of(jnp.minimum(j + 1, n_steps - 1))
                    keep = (j < n_steps - 1).astype(F32)
                    pr = pair[0][p_rows, lanes] * keep
                    pi = pair[1][p_rows, lanes] * keep
                    acc += [sums[2 * st] + (xr * pr + xi * pi), sums[2 * st + 1] + (xi * pr - xr * pi)]
            carry, sums = tuple(out), tuple(acc)
        return carry + sums

    n_sums = 0 if pair is None else 2 * n_strips
    out = lax.fori_loop(0, n_steps // 2, fix, tuple(entering) + (zero,) * n_sums)
    return out[2 * n_strips:]


def _seq_fwd(proj, conv_w, bc_re, bc_im, cc_re, cc_im, dskip, a_pow, name):
    tp = proj.shape[0]
    dh = proj.shape[1] // 4
    nq = dh // LANES
    sw = STATE * N_GROUPS // nq

    def body(b_ref, c_ref, v_ref, u_ref, w_ref, bre_ref, bim_ref, cre_ref, cim_ref, d_ref, pw_ref,
             co_ref, y_ref, g_ref, s_re, s_im, u_il, y_il):
        co_ref[...] = b_ref[...] * _dwconv(c_ref[...] * v_ref[...], w_ref)
        _interleave(u_il, u_ref)
        ub = u_il[...].astype(BF16)
        s_re[...] = jnp.dot(ub, bre_ref[...], preferred_element_type=F32)
        s_im[...] = jnp.dot(ub, bim_ref[...], preferred_element_type=F32)
        _scan(s_re, s_im, pw_ref, False)
        y_il[...] = (jnp.dot(s_re[...].astype(BF16), cre_ref[...], preferred_element_type=F32)
                     - jnp.dot(s_im[...].astype(BF16), cim_ref[...], preferred_element_type=F32))
        _deinterleave(y_ref, y_il)
        y = y_ref[...] + d_ref[...] * u_ref[...]
        y_ref[...] = y
        g_ref[...] = _gelu(y).astype(BF16)

    col = lambda off: pl.BlockSpec((tp, LANES), lambda q, off=off: (0, off * nq + q))
    blk = pl.BlockSpec((tp, LANES), lambda q: (0, q))
    return pl.pallas_call(
        body, name=name, grid=(nq,),
        in_specs=[col(0), col(1), col(2), col(3),
                  pl.BlockSpec((3, LANES), lambda q: (0, q)),
                  pl.BlockSpec((LANES, sw), lambda q: (0, q)), pl.BlockSpec((LANES, sw), lambda q: (0, q)),
                  pl.BlockSpec((sw, LANES), lambda q: (q, 0)), pl.BlockSpec((sw, LANES), lambda q: (q, 0)),
                  pl.BlockSpec((1, LANES), lambda q: (0, q)),
                  pl.BlockSpec((2, tp // SUBLANES, sw), lambda q: (0, 0, q))],
        out_specs=[blk, blk, blk],
        out_shape=[jax.ShapeDtypeStruct((tp, dh), F32), jax.ShapeDtypeStruct((tp, dh), F32),
                   jax.ShapeDtypeStruct((tp, dh), BF16)],
        scratch_shapes=[pltpu.VMEM((tp, sw), F32), pltpu.VMEM((tp, sw), F32),
                        pltpu.VMEM((tp, LANES), F32), pltpu.VMEM((tp, LANES), F32)],
        compiler_params=_cparams("parallel"),
    )(proj, proj, proj, proj, conv_w, bc_re, bc_im, cc_re, cc_im, dskip, a_pow)


def _conv_bwd(proj, dco, conv_w, name):
    tp = proj.shape[0]
    dh = proj.shape[1] // 4
    nq = dh // LANES

    def body(b_ref, c_ref, v_ref, dco_ref, w_ref, dproj_ref, dw_ref, stage, sem):
        q = pl.program_id(0)
        cg = c_ref[...]
        vg = v_ref[...]
        cv = cg * vg
        dco_v = dco_ref[...]
        dcv, dw = _dwconv_bwd(cv, dco_v * b_ref[...], w_ref)
        dw_ref[...] = dw
        stage[0] = (dco_v * _dwconv(cv, w_ref)).astype(BF16)
        stage[1] = (dcv * vg).astype(BF16)
        stage[2] = (dcv * cg).astype(BF16)
        copies = [pltpu.make_async_copy(stage.at[p], dproj_ref.at[:, pl.ds((p * nq + q) * LANES, LANES)], sem.at[p])
                  for p in range(3)]
        for cp in copies:
            cp.start()
        for cp in copies:
            cp.wait()

    col = lambda off: pl.BlockSpec((tp, LANES), lambda q, off=off: (0, off * nq + q))
    return pl.pallas_call(
        body, name=name, grid=(nq,),
        in_specs=[col(0), col(1), col(2), pl.BlockSpec((tp, LANES), lambda q: (0, q)),
                  pl.BlockSpec((3, LANES), lambda q: (0, q))],
        out_specs=[pl.BlockSpec(memory_space=pl.ANY), pl.BlockSpec((3, LANES), lambda q: (0, q))],
        out_shape=[jax.ShapeDtypeStruct((tp, 4 * dh), BF16), jax.ShapeDtypeStruct((3, dh), F32)],
        scratch_shapes=[pltpu.VMEM((3, tp, LANES), BF16), pltpu.SemaphoreType.DMA((3,))],
        compiler_params=_cparams("arbitrary"),
    )(proj, proj, proj, dco, conv_w)


def _ssm_bwd(proj, y, dg, dproj, bc_re, bc_im, cc_re, cc_im, dskip, a_pow, name):
    tp = proj.shape[0]
    dh = proj.shape[1] // 4
    nq = dh // LANES
    sw = STATE * N_GROUPS // nq

    def body(u_ref, y_ref, dg_ref, dproj_in, bre_ref, bim_ref, cre_ref, cim_ref, d_ref, pw_ref,
             dproj_ref, dbre_ref, dbim_ref, dcre_ref, dcim_ref, dd_ref, dar_ref, dai_ref,
             s_re, s_im, l_re, l_im, a_il, b_il, stage, sem):
        del dproj_in
        q = pl.program_id(0)
        nt = (((1,), (1,)), ((), ()))
        tn = (((0,), (0,)), ((), ()))
        _interleave(a_il, u_ref)
        ub = a_il[...].astype(BF16)
        s_re[...] = jnp.dot(ub, bre_ref[...], preferred_element_type=F32)
        s_im[...] = jnp.dot(ub, bim_ref[...], preferred_element_type=F32)
        _scan(s_re, s_im, pw_ref, False)
        dy_rows = dg_ref[...] * _gelu_grad(y_ref[...])
        dd_ref[...] = jnp.sum(dy_rows * u_ref[...], axis=0, keepdims=True)
        _interleave(b_il, dy_rows)
        dy = b_il[...]
        dyb = dy.astype(BF16)
        l_re[...] = lax.dot_general(dyb, cre_ref[...], nt, preferred_element_type=F32)
        l_im[...] = -lax.dot_general(dyb, cim_ref[...], nt, preferred_element_type=F32)
        dcre_ref[...] = lax.dot_general(s_re[...].astype(BF16), dyb, tn, preferred_element_type=F32)
        dcim_ref[...] = -lax.dot_general(s_im[...].astype(BF16), dyb, tn, preferred_element_type=F32)
        sums = _scan(l_re, l_im, pw_ref, True, pair=(s_re, s_im))
        rest = tp - SUBLANES
        for st in range(sw // LANES):
            lanes = slice(st * LANES, (st + 1) * LANES)
            lr0, li0 = l_re[:SUBLANES, lanes], l_im[:SUBLANES, lanes]
            pr0, pi0 = _segment_shift(s_re[rest:, lanes], False), _segment_shift(s_im[rest:, lanes], False)
            dar_ref[:, lanes] = jnp.sum(sums[2 * st] + (lr0 * pr0 + li0 * pi0), axis=0, keepdims=True)
            dai_ref[:, lanes] = jnp.sum(sums[2 * st + 1] + (li0 * pr0 - lr0 * pi0), axis=0, keepdims=True)
        lrb = l_re[...].astype(BF16)
        lib = l_im[...].astype(BF16)
        a_il[...] = (dy * d_ref[...] + lax.dot_general(lrb, bre_ref[...], nt, preferred_element_type=F32)
                     + lax.dot_general(lib, bim_ref[...], nt, preferred_element_type=F32))
        _deinterleave(b_il, a_il)
        stage[...] = b_il[...].astype(BF16)
        dbre_ref[...] = lax.dot_general(ub, lrb, tn, preferred_element_type=F32)
        dbim_ref[...] = lax.dot_general(ub, lib, tn, preferred_element_type=F32)
        cp = pltpu.make_async_copy(stage, dproj_ref.at[:, pl.ds((3 * nq + q) * LANES, LANES)], sem)
        cp.start()
        cp.wait()

    blk = pl.BlockSpec((tp, LANES), lambda q: (0, q))
    bspec = pl.BlockSpec((LANES, sw), lambda q: (0, q))
    cspec = pl.BlockSpec((sw, LANES), lambda q: (q, 0))
    tspec = pl.BlockSpec((2, tp // SUBLANES, sw), lambda q: (0, 0, q))
    nstate = STATE * N_GROUPS
    return pl.pallas_call(
        body, name=name, grid=(nq,),
        in_specs=[pl.BlockSpec((tp, LANES), lambda q: (0, 3 * nq + q)), blk, blk, pl.BlockSpec(memory_space=pl.ANY),
                  bspec, bspec, cspec, cspec, pl.BlockSpec((1, LANES), lambda q: (0, q)), tspec],
        out_specs=[pl.BlockSpec(memory_space=pl.ANY), bspec, bspec, cspec, cspec,
                   pl.BlockSpec((1, LANES), lambda q: (0, q)),
                   pl.BlockSpec((1, sw), lambda q: (0, q)), pl.BlockSpec((1, sw), lambda q: (0, q))],
        out_shape=[jax.ShapeDtypeStruct((tp, 4 * dh), BF16),
                   jax.ShapeDtypeStruct((LANES, nstate), F32), jax.ShapeDtypeStruct((LANES, nstate), F32),
                   jax.ShapeDtypeStruct((nstate, LANES), F32), jax.ShapeDtypeStruct((nstate, LANES), F32),
                   jax.ShapeDtypeStruct((1, dh), F32),
                   jax.ShapeDtypeStruct((1, nstate), F32), jax.ShapeDtypeStruct((1, nstate), F32)],
        input_output_aliases={3: 0},
        scratch_shapes=[pltpu.VMEM((tp, sw), F32)] * 4 + [pltpu.VMEM((tp, LANES), F32)] * 2
        + [pltpu.VMEM((tp, LANES), BF16), pltpu.SemaphoreType.DMA],
        compiler_params=_cparams("arbitrary"),
    )(proj, y, dg, dproj, bc_re, bc_im, cc_re, cc_im, dskip, a_pow)


FFN_TILE = 256


def _ffn_act(up, fw, fb, name):
    tp, two_ff = up.shape
    dff = two_ff // 2
    tc = FFN_TILE
    nj = dff // tc

    def body(ua_ref, uv_ref, wa_ref, wv_ref, ba_ref, bv_ref, act_ref):
        a = _dwconv(ua_ref[...], wa_ref) + ba_ref[...]
        v = _dwconv(uv_ref[...], wv_ref) + bv_ref[...]
        act_ref[...] = (a * _sigmoid(a) * v).astype(BF16)

    lo = lambda r: pl.BlockSpec((r, tc), lambda j: (0, j))
    hi = lambda r: pl.BlockSpec((r, tc), lambda j: (0, nj + j))
    return pl.pallas_call(
        body, name=name, grid=(nj,),
        in_specs=[lo(tp), hi(tp), lo(3), hi(3), lo(1), hi(1)],
        out_specs=lo(tp),
        out_shape=jax.ShapeDtypeStruct((tp, dff), BF16),
        compiler_params=_cparams("parallel"))(up, up, fw, fw, fb, fb)


def _ffn_bwd(up, dact, fw, fb, name):
    tp, two_ff = up.shape
    dff = two_ff // 2
    tc = FFN_TILE
    nj = dff // tc

    def body(ua_ref, uv_ref, da_ref, wa_ref, wv_ref, ba_ref, bv_ref,
             dup_ref, dwa_ref, dwv_ref, dba_ref, dbv_ref, stage, sem):
        j = pl.program_id(0)
        ua = ua_ref[...]
        uv = uv_ref[...]
        a = _dwconv(ua, wa_ref) + ba_ref[...]
        v = _dwconv(uv, wv_ref) + bv_ref[...]
        sg = _sigmoid(a)
        dact_v = da_ref[...]
        da = dact_v * v * sg * (1.0 + a * (1.0 - sg))
        dv = dact_v * a * sg
        dba_ref[...] = jnp.sum(da, axis=0, keepdims=True)
        dbv_ref[...] = jnp.sum(dv, axis=0, keepdims=True)
        dua, dwa = _dwconv_bwd(ua, da, wa_ref)
        duv, dwv = _dwconv_bwd(uv, dv, wv_ref)
        dwa_ref[...] = dwa
        dwv_ref[...] = dwv
        stage[0] = dua.astype(BF16)
        stage[1] = duv.astype(BF16)
        copies = [pltpu.make_async_copy(stage.at[p], dup_ref.at[:, pl.ds((p * nj + j) * tc, tc)], sem.at[p])
                  for p in range(2)]
        for cp in copies:
            cp.start()
        for cp in copies:
            cp.wait()

    lo = lambda r: pl.BlockSpec((r, tc), lambda j: (0, j))
    hi = lambda r: pl.BlockSpec((r, tc), lambda j: (0, nj + j))
    return pl.pallas_call(
        body, name=name, grid=(nj,),
        in_specs=[lo(tp), hi(tp), lo(tp), lo(3), hi(3), lo(1), hi(1)],
        out_specs=[pl.BlockSpec(memory_space=pl.ANY), lo(3), lo(3), lo(1), lo(1)],
        out_shape=[jax.ShapeDtypeStruct((tp, two_ff), BF16),
                   jax.ShapeDtypeStruct((3, dff), F32), jax.ShapeDtypeStruct((3, dff), F32),
                   jax.ShapeDtypeStruct((1, dff), F32), jax.ShapeDtypeStruct((1, dff), F32)],
        scratch_shapes=[pltpu.VMEM((2, tp, tc), BF16), pltpu.SemaphoreType.DMA((2,))],
        compiler_params=_cparams("arbitrary"))(up, up, dact, fw, fw, fb, fb)


def _zoh(lr, li, ld):
    dt = jnp.exp(ld)
    mag = jnp.exp(lr * dt)
    ang = li * dt
    ar = mag * jnp.cos(ang)
    ai = mag * jnp.sin(ang)
    den = lr * lr + li * li
    nr = ar - 1.0
    fr = (nr * lr + ai * li) / den
    fi = (ai * lr - nr * li) / den
    return dt, ar, ai, den, nr, fr, fi


def _s5_prep(lr, li, ld, b_re, b_im, n_pow, name):
    nstate = lr.shape[1]

    def body(lr_ref, li_ref, ld_ref, bre_ref, bim_ref, pw_ref, bcre_ref, bcim_ref):
        _, ar, ai, _, _, fr, fi = _zoh(lr_ref[...], li_ref[...], ld_ref[...])
        bre = bre_ref[...]
        bim = bim_ref[...]
        bcre_ref[...] = (fr * bre - fi * bim).astype(BF16)
        bcim_ref[...] = (fr * bim + fi * bre).astype(BF16)
        row = lax.broadcasted_iota(jnp.int32, (SUBLANES, nstate), 0)
        pr, pi = jnp.zeros((SUBLANES, nstate), F32), jnp.zeros((SUBLANES, nstate), F32)
        cr, ci = ar, ai
        for t in range(SUBLANES):
            pr, pi = jnp.where(row == t, cr, pr), jnp.where(row == t, ci, pi)
            cr, ci = cr * ar - ci * ai, cr * ai + ci * ar
        pw_ref[0, 0:SUBLANES, :] = pr
        pw_ref[1, 0:SUBLANES, :] = pi
        n = SUBLANES
        while n < n_pow:
            m = min(n, n_pow - n)
            tr, ti = pw_ref[0, n - 1:n, :], pw_ref[1, n - 1:n, :]
            xr, xi = pw_ref[0, 0:m, :], pw_ref[1, 0:m, :]
            pw_ref[0, n:n + m, :] = xr * tr - xi * ti
            pw_ref[1, n:n + m, :] = xr * ti + xi * tr
            n += m

    vmem = pl.BlockSpec(memory_space=pltpu.VMEM)
    return pl.pallas_call(
        body, name=name, in_specs=[vmem] * 5, out_specs=[vmem] * 3,
        out_shape=[jax.ShapeDtypeStruct((2, n_pow, nstate), F32)] + [jax.ShapeDtypeStruct(b_re.shape, BF16)] * 2,
        compiler_params=pltpu.CompilerParams(vmem_limit_bytes=VMEM_LIMIT))(lr, li, ld, b_re, b_im)


def _s5_prep_bwd(lr, li, ld, b_re, b_im, da_re, da_im, dbc_re, dbc_im, name):
    def body(lr_ref, li_ref, ld_ref, bre_ref, bim_ref, dar_ref, dai_ref, dbcre_ref, dbcim_ref,
             dlr_ref, dli_ref, dld_ref, dbre_ref, dbim_ref):
        lr, li = lr_ref[...], li_ref[...]
        dt, ar, ai, den, nr, fr, fi = _zoh(lr, li, ld_ref[...])
        bre, bim = bre_ref[...], bim_ref[...]
        gre, gim = dbcre_ref[...], dbcim_ref[...]
        dbre_ref[...] = fr * gre + fi * gim
        dbim_ref[...] = fr * gim - fi * gre
        g_fr = jnp.sum(gre * bre + gim * bim, axis=0, keepdims=True)
        g_fi = jnp.sum(gim * bre - gre * bim, axis=0, keepdims=True)
        g_ar = dar_ref[...] + (g_fr * lr - g_fi * li) / den
        g_ai = dai_ref[...] + (g_fr * li + g_fi * lr) / den
        d_lr = (g_fr * (nr - 2.0 * fr * lr) + g_fi * (ai - 2.0 * fi * lr)) / den
        d_li = (g_fr * (ai - 2.0 * fr * li) - g_fi * (nr + 2.0 * fi * li)) / den
        g_logmag = g_ar * ar + g_ai * ai
        g_ang = g_ai * ar - g_ar * ai
        dlr_ref[...] = d_lr + g_logmag * dt
        dli_ref[...] = d_li + g_ang * dt
        d_ld = (g_logmag * lr + g_ang * li) * dt
        n = d_ld.shape[1]
        sh = 1
        while sh < STATE:
            d_ld = d_ld + pltpu.roll(d_ld, n - sh, 1)
            sh *= 2
        dld_ref[...] = d_ld

    vmem = pl.BlockSpec(memory_space=pltpu.VMEM)
    row = jax.ShapeDtypeStruct(lr.shape, F32)
    return pl.pallas_call(
        body, name=name, in_specs=[vmem] * 9, out_specs=[vmem] * 5,
        out_shape=[row, row, row, jax.ShapeDtypeStruct(b_re.shape, F32), jax.ShapeDtypeStruct(b_re.shape, F32)],
    )(lr, li, ld, b_re, b_im, da_re, da_im, dbc_re, dbc_im)


def _compact_b(bb):
    bq = bb.reshape(N_GROUPS // 8, 8, STATE, GROUP)
    m = jnp.einsum("ab,qbph->qahbp", jnp.eye(8, dtype=bb.dtype), bq).reshape(N_GROUPS // 8, LANES, 8 * STATE)
    return m.transpose(1, 0, 2).reshape(LANES, N_GROUPS * STATE)


def _expand_b(m):
    d = m.reshape(8, GROUP, N_GROUPS // 8, 8, STATE)
    return jnp.einsum("ahqap->qahp", d).reshape(N_GROUPS, GROUP, STATE)


def _compact_c(c):
    cq = c.reshape(N_GROUPS // 8, 8, GROUP, STATE)
    return jnp.einsum("ab,qbhp->qbpah", jnp.eye(8, dtype=c.dtype), cq).reshape(N_GROUPS * STATE, LANES)


def _expand_c(m):
    d = m.reshape(N_GROUPS // 8, 8, STATE, 8, GROUP)
    return jnp.einsum("qbpbh->qbhp", d).reshape(N_GROUPS, GROUP, STATE)


def _local_step(x, target, p, ex):
    seq, d = x.shape
    n_real = N_META + seq
    tp = -(-n_real // ROW_ALIGN) * ROW_ALIGN

    h0, hn1 = _input_norm_fwd(x, p["meta_tokens"], p["norm_mix_g"] + ex.zero, tp, "norm_mix")
    ex.forward("first", hn1)
    nstate = N_GROUPS * STATE
    s5 = (p["ssm_lam_re"].reshape(1, nstate), p["ssm_lam_im"].reshape(1, nstate),
          jnp.repeat(p["ssm_log_dt"].reshape(-1), STATE).reshape(1, nstate),
          _compact_b(p["ssm_b_re"]), _compact_b(p["ssm_b_im"]))
    a_pow, bc_re, bc_im = _s5_prep(*s5, tp // SUBLANES, "s5_prep")
    cc_re = _compact_c(p["ssm_c_re"]).astype(BF16)
    cc_im = _compact_c(p["ssm_c_im"]).astype(BF16)
    dskip = p["ssm_d"].reshape(1, -1)
    first = ex.weights("first", bc_re)
    proj = _mm(hn1, first["w_in"], "nn", "proj")
    started = ex.forward("mid", proj)
    co, y, g = _seq_fwd(proj, p["conv_w"] + started[0, 0], bc_re, bc_im, cc_re, cc_im, dskip, a_pow, "seq_fwd")
    mid = ex.weights("mid", g)
    z = _mm(g, mid["ssm_w_glu"], "nn", "glu")
    mixed = _mix_fwd(co, y, z, p["gain_conv_out"], p["gain_ssm_out"], "mix_fwd")
    started = ex.forward("up", mixed)
    h1, hn2 = _proj_res_norm(mixed, mid["w_out"], h0, p["norm_ffn_g"], started, n_real, "out_proj_norm")
    late = ex.weights("up", hn2)
    up = _mm(hn2, late["w_up"], "nn", "up_proj")
    started = ex.forward("down", up)
    act = _ffn_act(up, p["ffn_conv_w"] + started[0, 0], p["ffn_conv_b"], "ffn_act")
    late.update(ex.weights("down", act))
    loss, dh2, dh2b, d_gfin = _proj_loss_bwd(act, late["w_down"], h1, target, p["norm_final_g"], n_real,
                                             "down_proj_loss")

    g_w_down = _mm(act, dh2b, "tn", "g_w_down")
    dact = _mm(dh2b, late["w_down"], "nt", "d_act")
    dup, dfw_a, dfw_v, dfb_a, dfb_v = _ffn_bwd(up, dact, p["ffn_conv_w"], p["ffn_conv_b"], "ffn_bwd")
    g_w_up = _mm(hn2, dup, "tn", "g_w_up")
    started = ex.grads_ready("late", {"w_up": g_w_up, "w_down": g_w_down})
    dh1, dh1b, d_gffn = _proj_norm_bwd(dup, late["w_up"], h1, p["norm_ffn_g"], dh2, started, "d_hn2_norm_bwd")
    started = ex.grads_send("late", dh1)
    g_w_out = _mm(mixed, dh1b, "tn", "g_w_out", after=started)
    dco, dz, dgp, d_gc, d_gs = _proj_mix_bwd(dh1b, mid["w_out"], co, y, z, p["gain_conv_out"],
                                             p["gain_ssm_out"], "d_mixed_mix_bwd")
    g_w_glu = _mm(g, dz, "tn", "g_w_glu")
    started = ex.grads_ready("mid", {"ssm_w_glu": g_w_glu, "w_out": g_w_out})
    dg = _mm(dz, mid["ssm_w_glu"], "nt", "d_gelu", acc_in=dgp, after=started)
    started = ex.grads_send("mid", dg)
    dproj, d_conv_w = _conv_bwd(proj, dco, p["conv_w"] + started[0, 0], "conv_bwd")
    (dproj, dbc_re, dbc_im, dcc_re, dcc_im, d_dskip, da_re, da_im) = _ssm_bwd(
        proj, y, dg, dproj, bc_re, bc_im, cc_re, cc_im, dskip, a_pow, "ssm_bwd")
    g_w_in = _mm(hn1, dproj, "tn", "g_w_in")
    started = ex.grads_ready("first", {"w_in": g_w_in})
    grad_x, d_meta, d_gmix = _proj_input_norm_bwd(dproj, first["w_in"], h0, p["norm_mix_g"], dh1, started, n_real,
                                                  "d_hn1_norm_bwd")
    started = ex.grads_send("first", d_gmix)

    d_lam_re, d_lam_im, d_log_dt, d_b_re, d_b_im = _s5_prep_bwd(*s5, da_re, da_im, dbc_re, dbc_im, "s5_prep_bwd")
    d_lam_re, d_lam_im = d_lam_re.reshape(N_GROUPS, STATE), d_lam_im.reshape(N_GROUPS, STATE)
    d_log_dt = d_log_dt[0, ::STATE]
    d_b_re, d_b_im = _expand_b(d_b_re), _expand_b(d_b_im)
    grads = {
        "meta_tokens": d_meta, "norm_mix_g": d_gmix, "w_in": g_w_in, "conv_w": d_conv_w,
        "ssm_lam_re": d_lam_re, "ssm_lam_im": d_lam_im, "ssm_log_dt": d_log_dt,
        "ssm_b_re": d_b_re, "ssm_b_im": d_b_im, "ssm_c_re": _expand_c(dcc_re), "ssm_c_im": _expand_c(dcc_im),
        "ssm_d": d_dskip.reshape(N_GROUPS, GROUP), "ssm_w_glu": g_w_glu,
        "gain_conv_out": d_gc, "gain_ssm_out": d_gs, "w_out": g_w_out, "norm_ffn_g": d_gffn,
        "w_up": g_w_up, "ffn_conv_w": jnp.concatenate([dfw_a, dfw_v], axis=1),
        "ffn_conv_b": jnp.concatenate([dfb_a, dfb_v], axis=1), "w_down": g_w_down, "norm_final_g": d_gfin,
    }
    return loss[0, 0] + started[0, 0], grad_x, grads


def _view(ref, axis, start, size):
    idx = [slice(None)] * len(ref.shape)
    idx[axis] = pl.ds(start, size)
    return ref.at[tuple(idx)]


def _exchange(name, ins, outs, aliases, local_copies, remote_copies):
    ni, no = len(ins), len(outs)
    nl, nr = len(local_copies), len(remote_copies)

    def body(*refs):
        in_refs, out_refs = refs[:ni], refs[ni:ni + no]
        send_sems, recv_sems, local_sems = refs[ni + no:]
        x, y, c = lax.axis_index("x"), lax.axis_index("y"), lax.axis_index("c")
        pos = (x, y, c, 2 * x + y)
        locals_ = [pltpu.make_async_copy(s(in_refs, out_refs, pos), d(in_refs, out_refs, pos), local_sems.at[i])
                   for i, (s, d) in enumerate(local_copies)]
        remotes = []
        for i, (s, d, flip) in enumerate(remote_copies):
            peer = (1 - x if "x" in flip else x, 1 - y if "y" in flip else y, 1 - c if "c" in flip else c)
            remotes.append(pltpu.make_async_remote_copy(
                src_ref=s(in_refs, out_refs, pos), dst_ref=d(in_refs, out_refs, pos),
                send_sem=send_sems.at[i], recv_sem=recv_sems.at[i], device_id=peer, device_id_type=MESH))
        for cp in locals_ + remotes:
            cp.start()
        for cp in remotes:
            cp.wait_recv()
        for cp in remotes:
            cp.wait_send()
        for cp in locals_:
            cp.wait()

    hbm = pl.BlockSpec(memory_space=pl.ANY)
    return pl.pallas_call(
        body, name=name, in_specs=[hbm] * ni, out_specs=[hbm] * no, out_shape=outs,
        input_output_aliases=aliases,
        scratch_shapes=[pltpu.SemaphoreType.DMA((nr,)), pltpu.SemaphoreType.DMA((nr,)),
                        pltpu.SemaphoreType.DMA((max(nl, 1),))],
    )(*ins)


BIG = {"w_in": (0, 1), "ssm_w_glu": (1, 0), "w_out": (1, 0), "w_up": (0, 1), "w_down": (1, 0)}
BIG_NAMES = tuple(BIG)
FLIPS = ("y", "x", "xy")


def _peer_chip(pos, flip):
    x, y, _, _ = pos
    return 2 * (1 - x if "x" in flip else x) + (1 - y if "y" in flip else y)


def _block_rows(rows, cols, itemsize, mult):
    return _pick_tile(rows, max(mult, (2 * 1024 * 1024) // (cols * itemsize)), mult)


def _cast_into_full(w, kc, shard_axis, name):
    r, cdim = w.shape
    tr = _block_rows(r, cdim, 4, 16)
    nb = r // tr

    def body(kc_ref, w_ref, o_ref):
        o_ref[...] = w_ref[...].astype(BF16)

    if shard_axis == 1:
        full, o_spec = (r, 4 * cdim), pl.BlockSpec((tr, cdim), lambda i, kc: (i, kc[0]))
    else:
        full, o_spec = (4 * r, cdim), pl.BlockSpec((tr, cdim), lambda i, kc: (kc[0] * nb + i, 0))
    return pl.pallas_call(
        body, name=name,
        grid_spec=pltpu.PrefetchScalarGridSpec(
            num_scalar_prefetch=1, grid=(nb,), in_specs=[pl.BlockSpec((tr, cdim), lambda i, kc: (i, 0))],
            out_specs=o_spec),
        out_shape=jax.ShapeDtypeStruct(full, BF16), compiler_params=_cparams("parallel"))(kc, w)


def _pair_sum(g, recv, kc, half_axis, name, out_dtype):
    hr, hc = recv.shape
    tr = _block_rows(hr, hc, 4, 16)
    nb = hr // tr

    def body(kc_ref, g_ref, r_ref, o_ref):
        o_ref[...] = (g_ref[...] + r_ref[...]).astype(out_dtype)

    if half_axis == 0:
        g_spec = pl.BlockSpec((tr, hc), lambda i, kc: (kc[1] * nb + i, 0))
    elif half_axis == 1:
        g_spec = pl.BlockSpec((tr, hc), lambda i, kc: (i, kc[1]))
    else:
        g_spec = pl.BlockSpec((tr, hc), lambda i, kc: (i, 0))
    same = pl.BlockSpec((tr, hc), lambda i, kc: (i, 0))
    return pl.pallas_call(
        body, name=name,
        grid_spec=pltpu.PrefetchScalarGridSpec(num_scalar_prefetch=1, grid=(nb,), in_specs=[g_spec, same],
                                               out_specs=same),
        out_shape=jax.ShapeDtypeStruct((hr, hc), out_dtype), compiler_params=_cparams("parallel"))(kc, g, recv)


def _chip_sum(own, recv, kc, own_axis, out_axis, name):
    _, sr, sc = recv.shape
    tr = _block_rows(sr, sc, 4, 16)
    nb = sr // tr

    def body(kc_ref, o_ref, r_ref, t_ref):
        k = kc_ref[0]
        own_v = o_ref[...].astype(F32)
        r = [r_ref[m].astype(F32) for m in range(3)]
        terms = []
        for kk in range(4):
            m = jnp.bitwise_xor(k, kk)
            terms.append(jnp.where(m == 0, own_v, jnp.where(m == 1, r[0], jnp.where(m == 2, r[1], r[2]))))
        t_ref[...] = (terms[0] + terms[1]) + (terms[2] + terms[3])

    if own_axis == 0:
        own_spec = pl.BlockSpec((tr, sc), lambda i, kc: (kc[0] * nb + i, 0))
    elif own_axis == 1:
        own_spec = pl.BlockSpec((tr, sc), lambda i, kc: (i, kc[0]))
    else:
        own_spec = pl.BlockSpec((tr, sc), lambda i, kc: (kc[1] * nb + i, 0))
    if out_axis == 0:
        out_full, out_spec = (2 * sr, sc), pl.BlockSpec((tr, sc), lambda i, kc: (kc[1] * nb + i, 0))
    else:
        out_full, out_spec = (sr, 2 * sc), pl.BlockSpec((tr, sc), lambda i, kc: (i, kc[1]))
    return pl.pallas_call(
        body, name=name,
        grid_spec=pltpu.PrefetchScalarGridSpec(
            num_scalar_prefetch=1, grid=(nb,),
            in_specs=[own_spec, pl.BlockSpec((3, tr, sc), lambda i, kc: (0, i, 0))],
            out_specs=out_spec),
        out_shape=jax.ShapeDtypeStruct(out_full, F32), compiler_params=_cparams("parallel"))(kc, own, recv)


def _adamw(w, g, m, v, name):
    r, cdim = w.shape
    tr = _block_rows(r, cdim, 4, 8)
    c1 = 1.0 - ADAM_B1 ** ADAM_STEP
    c2 = 1.0 - ADAM_B2 ** ADAM_STEP

    def body(w_ref, g_ref, m_ref, v_ref, go_ref, d_ref, nm_ref, nv_ref):
        gv = g_ref[...]
        go_ref[...] = gv
        nm = ADAM_B1 * m_ref[...] + (1.0 - ADAM_B1) * gv
        nv = ADAM_B2 * v_ref[...] + (1.0 - ADAM_B2) * (gv * gv)
        d_ref[...] = -ADAM_LR * ((nm / c1) / (jnp.sqrt(nv / c2) + ADAM_EPS) + ADAM_WD * w_ref[...])
        nm_ref[...] = nm
        nv_ref[...] = nv

    spec = _rows(cdim, tr)
    return pl.pallas_call(body, name=name, grid=(r // tr,), in_specs=[spec] * 4, out_specs=[spec] * 4,
                          out_shape=[jax.ShapeDtypeStruct((r, cdim), F32)] * 4,
                          compiler_params=_cparams("parallel"))(w, g, m, v)


def _adamw_whole(ws, gs, ms, vs, name):
    n = len(ws)
    c1 = 1.0 - ADAM_B1 ** ADAM_STEP
    c2 = 1.0 - ADAM_B2 ** ADAM_STEP

    def body(*refs):
        for i in range(n):
            w_ref, g_ref, m_ref, v_ref, d_ref, nm_ref, nv_ref = [refs[j * n + i] for j in range(7)]
            gv = g_ref[...]
            nm = ADAM_B1 * m_ref[...] + (1.0 - ADAM_B1) * gv
            nv = ADAM_B2 * v_ref[...] + (1.0 - ADAM_B2) * (gv * gv)
            d_ref[...] = -ADAM_LR * ((nm / c1) / (jnp.sqrt(nv / c2) + ADAM_EPS) + ADAM_WD * w_ref[...])
            nm_ref[...] = nm
            nv_ref[...] = nv

    vmem = pl.BlockSpec(memory_space=pltpu.VMEM)
    out = pl.pallas_call(body, name=name, in_specs=[vmem] * (4 * n), out_specs=[vmem] * (3 * n),
                         out_shape=[jax.ShapeDtypeStruct(a.shape, F32) for a in ws] * 3,
                         compiler_params=pltpu.CompilerParams(vmem_limit_bytes=VMEM_LIMIT))(*ws, *gs, *ms, *vs)
    return out[:n], out[n:2 * n], out[2 * n:]


SIDE_EFFECT = pltpu.SideEffectType.DATAFLOW_SIDE_EFFECTING


def _descriptors(copies, refs, send_sems, recv_sems, sem_off=0):
    x, y, c = lax.axis_index("x"), lax.axis_index("y"), lax.axis_index("c")
    pos = (x, y, c, 2 * x + y)
    out = []
    for i, (s, d, flip) in enumerate(copies):
        peer = (1 - x if "x" in flip else x, 1 - y if "y" in flip else y, 1 - c if "c" in flip else c)
        out.append(pltpu.make_async_remote_copy(
            src_ref=s(refs, refs, pos), dst_ref=d(refs, refs, pos),
            send_sem=send_sems.at[sem_off + i], recv_sem=recv_sems.at[sem_off + i],
            device_id=peer, device_id_type=MESH))
    return out


def _shifted(copies, off):
    return [(lambda I, O, pos, s=s: s(I[off:], O[off:], pos), lambda I, O, pos, d=d: d(I[off:], O[off:], pos), flip)
            for s, d, flip in copies]


BARRIER_IDS = {"c": (1, 2), "ici": (3, 4)}


def _exchange_start(name, bufs, copies, turns, after=None):
    n, nr = len(bufs), len(copies)
    na = 0 if after is None else 1
    flips = sorted({flip for _, _, flip in copies})
    kind = "c" if flips == ["c"] else "ici"
    collective_id = BARRIER_IDS[kind][turns[kind] % 2]
    turns[kind] += 1

    def body(*refs):
        x, y, c = lax.axis_index("x"), lax.axis_index("y"), lax.axis_index("c")
        barrier = pltpu.get_barrier_semaphore()
        for flip in flips:
            peer = (1 - x if "x" in flip else x, 1 - y if "y" in flip else y, 1 - c if "c" in flip else c)
            pl.semaphore_signal(barrier, inc=1, device_id=peer, device_id_type=MESH)
        pl.semaphore_wait(barrier, len(flips))
        for cp in _descriptors(copies, refs[:n], refs[n + na], refs[n + na + 1]):
            cp.start()
        token = refs[2 * n + na + 2]
        token[...] = jnp.zeros_like(token)

    hbm = pl.BlockSpec(memory_space=pltpu.HBM)
    sem = pl.BlockSpec(memory_space=pltpu.SEMAPHORE)
    out = pl.pallas_call(
        body, name=name,
        in_specs=[hbm] * n + [pl.BlockSpec(memory_space=pl.ANY)] * na,
        out_specs=(sem, sem, *[hbm] * n, pl.BlockSpec(memory_space=pltpu.VMEM)),
        out_shape=(pltpu.SemaphoreType.DMA((nr,)), pltpu.SemaphoreType.DMA((nr,)),
                   *[pltpu.HBM(b.shape, b.dtype) for b in bufs], jax.ShapeDtypeStruct((SUBLANES, LANES), F32)),
        input_output_aliases={i: 2 + i for i in range(n)},
        compiler_params=pltpu.CompilerParams(has_side_effects=SIDE_EFFECT, collective_id=collective_id),
    )(*[pltpu.with_memory_space_constraint(b, pltpu.HBM) for b in bufs], *([after] * na))
    return out[0], out[1], list(out[2:2 + n]), out[2 + n]


def _exchange_wait(name, send_sems, recv_sems, bufs, copies, after, sem_off=0):
    n = len(bufs)

    def body(*refs):
        for cp in _descriptors(copies, refs[:n], refs[n], refs[n + 1], sem_off):
            cp.wait_send()
            cp.wait_recv()

    hbm = pl.BlockSpec(memory_space=pltpu.HBM)
    sem = pl.BlockSpec(memory_space=pltpu.SEMAPHORE)
    out = pl.pallas_call(
        body, name=name,
        in_specs=[hbm] * n + [sem, sem, pl.BlockSpec(memory_space=pl.ANY)],
        out_specs=tuple([hbm] * n),
        out_shape=tuple(pltpu.HBM(b.shape, b.dtype) for b in bufs),
        input_output_aliases={i: i for i in range(n)},
        compiler_params=pltpu.CompilerParams(has_side_effects=SIDE_EFFECT),
    )(*bufs, send_sems, recv_sems, after)
    return list(out)


FIRST = ("w_in",)
MID = ("ssm_w_glu", "w_out")
LATE = ("w_up", "w_down")
GROUPS = {"first": FIRST, "mid": MID, "late": LATE}
ARRIVALS = {"first": FIRST, "mid": MID, "up": ("w_up",), "down": ("w_down",)}


def _gather_copies(names, shard_shapes):
    def region(i, chip, c):
        half_axis, shard_axis = BIG[names[i]]
        ssize = shard_shapes[i][shard_axis]
        hsize = shard_shapes[i][half_axis] // 2
        return lambda ref: _view(_view(ref, shard_axis, chip * ssize, ssize), half_axis, c * hsize, hsize)

    ici, d2d = [], []
    for i in range(len(names)):
        for flip in FLIPS:
            ici.append((lambda I, O, pos, i=i: region(i, pos[3], pos[2])(I[i]),
                        lambda I, O, pos, i=i: region(i, pos[3], pos[2])(O[i]), flip))
            d2d.append((lambda I, O, pos, i=i, flip=flip: region(i, _peer_chip(pos, flip), pos[2])(I[i]),
                        lambda I, O, pos, i=i, flip=flip: region(i, _peer_chip(pos, flip), pos[2])(O[i]), "c"))
    return ici, d2d


def _half_shape(n, shape):
    r, cdim = shape
    return (r // 2, cdim) if BIG[n][0] == 0 else (r, cdim // 2)


def _sub_shape(n, shape):
    hr, hc = _half_shape(n, shape)
    return (hr, hc // 4) if BIG[n][1] == 1 else (hr // 4, hc)


def _pair_copies(names, shapes, with_pack, dst_off):
    n = len(names)

    def other_half(i, ref, pos):
        half_axis = BIG[names[i]][0]
        hsize = shapes[i][half_axis] // 2
        return _view(ref, half_axis, (1 - pos[2]) * hsize, hsize)

    copies = [(lambda I, O, pos, i=i: other_half(i, I[i], pos), lambda I, O, pos, i=i: O[dst_off + i], "c")
              for i in range(n)]
    if with_pack:
        copies.append((lambda I, O, pos: I[n], lambda I, O, pos: O[dst_off + n], "c"))
    return copies


def _chip_copies(names, shapes, pack_rows, dst_off):
    n = len(names)

    def piece(i, ref, chip):
        shard_axis = BIG[names[i]][1]
        ssize = _sub_shape(names[i], shapes[i])[shard_axis]
        return _view(ref, shard_axis, chip * ssize, ssize)

    copies = []
    for i in range(n):
        for slot, flip in enumerate(FLIPS):
            copies.append((lambda I, O, pos, i=i, flip=flip: piece(i, I[i], _peer_chip(pos, flip)),
                           lambda I, O, pos, i=i, slot=slot: O[dst_off + i].at[slot], flip))
    if pack_rows:
        for slot, flip in enumerate(FLIPS):
            copies.append((lambda I, O, pos: _view(I[n], 0, pos[2] * (pack_rows // 2), pack_rows // 2),
                           lambda I, O, pos, slot=slot: O[dst_off + n].at[slot], flip))
    return copies


class _Exchanges:
    def __init__(self, shards, tiny, kc):
        self.kc = kc
        wb = {n: _cast_into_full(shards[n], kc, BIG[n][1], "cast_" + n) for n in BIG_NAMES}
        self.gathering, self.forwarding, self.pairing, self.reducing = {}, {}, {}, {}
        self.turns = {"c": 0, "ici": 0}
        tiny_copies = [(lambda I, O, pos: I[0], lambda I, O, pos: O[1].at[pos[3]], flip) for flip in FLIPS]
        self.gathering["tiny"] = (0, 0, 2, tiny_copies, None)
        bufs, copies = [tiny, lax.empty((4,) + tiny.shape, F32)], list(tiny_copies)
        for group, names in ARRIVALS.items():
            ici, d2d = _gather_copies(names, [shards[n].shape for n in names])
            self.gathering[group] = (len(bufs), len(copies), len(names), ici, d2d)
            copies += _shifted(ici, len(bufs))
            bufs += [wb[n] for n in names]
        self.started = _exchange_start("gather_start", bufs, copies, self.turns)
        self.zero = self.started[3][0, 0]

    def _arrived(self, group, after):
        buf_off, sem_off, n, ici, _ = self.gathering[group]
        send_sems, recv_sems, bufs, _ = self.started
        return _exchange_wait("gather_%s_wait" % group, send_sems, recv_sems, bufs[buf_off:buf_off + n], ici, after,
                              sem_off)

    def small_params(self, kc):
        tiny, got = self._arrived("tiny", self.started[3])
        return lax.dynamic_update_index_in_dim(got, tiny, kc[0], 0)

    def forward(self, group, after):
        d2d = self.gathering[group][4]
        self.forwarding[group] = (_exchange_start("forward_%s_start" % group, self._arrived(group, after), d2d,
                                                  self.turns), d2d)
        return self.forwarding[group][0][3]

    def weights(self, group, after):
        if group not in self.forwarding:
            after = self.forward(group, after)
        (send_sems, recv_sems, bufs, _), d2d = self.forwarding[group]
        full = _exchange_wait("forward_%s_wait" % group, send_sems, recv_sems, bufs, d2d, after)
        return dict(zip(ARRIVALS[group], full))

    def grads_ready(self, group, grads):
        names = GROUPS[group]
        gs = [grads[n] for n in names]
        land = [lax.empty(_half_shape(n, g.shape), F32) for n, g in zip(names, gs)]
        copies = _pair_copies(names, [g.shape for g in gs], False, len(names))
        started = _exchange_start("pair_%s_start" % group, gs + land, copies, self.turns)
        self.pairing[group] = (started, copies)
        return started[3]

    def grads_send(self, group, after):
        names = GROUPS[group]
        n = len(names)
        (send_sems, recv_sems, bufs, _), copies = self.pairing[group]
        bufs = _exchange_wait("pair_%s_wait" % group, send_sems, recv_sems, bufs, copies, after)
        chip = [_pair_sum(bufs[i], bufs[n + i], self.kc, BIG[names[i]][0], "pair_sum_" + names[i], BF16)
                for i in range(n)]
        shapes = [bufs[i].shape for i in range(n)]
        land = [lax.empty((3,) + _sub_shape(names[i], shapes[i]), BF16) for i in range(n)]
        copies = _chip_copies(names, shapes, 0, n)
        started = _exchange_start("reduce_%s_start" % group, chip + land, copies, self.turns)
        self.reducing[group] = (started, copies)
        return started[3]

    def finish_pack(self, pack):
        kc = self.kc
        prow = pack.shape[0] // 2
        recv = _exchange("reduce_d2d", [pack], [jax.ShapeDtypeStruct(pack.shape, F32)], {}, [],
                         _pair_copies((), [], True, 0))
        chip_pack = _pair_sum(pack, recv[0], kc, None, "pair_sum_pack", F32)
        copies = _chip_copies((), [], pack.shape[0], 1)
        land = lax.empty((3, prow, pack.shape[1]), F32)
        pack_sems_s, pack_sems_r, pack_bufs, after = _exchange_start("reduce_pack_start", [chip_pack, land], copies,
                                                                     self.turns)

        names, chips, recvs = (), [], []
        for group, group_names in GROUPS.items():
            (send_sems, recv_sems, bufs, _), group_copies = self.reducing[group]
            bufs = _exchange_wait("reduce_%s_wait" % group, send_sems, recv_sems, bufs, group_copies, after)
            n = len(group_names)
            names, chips, recvs = names + group_names, chips + bufs[:n], recvs + bufs[n:]
            after = bufs[n]
        total = [_chip_sum(chips[i], recvs[i], kc, BIG[n][1], BIG[n][0], "chip_sum_" + n)
                 for i, n in enumerate(names)]

        def my_half(half_axis, ref, pos):
            hsize = ref.shape[half_axis] // 2
            return _view(ref, half_axis, pos[2] * hsize, hsize)

        swap = [(lambda I, O, pos, i=i, n=n: my_half(BIG[n][0], I[i], pos),
                 lambda I, O, pos, i=i, n=n: my_half(BIG[n][0], O[i], pos), "c") for i, n in enumerate(names)]
        self.swapping = (_exchange_start("swap_start", total, swap, self.turns), swap, names)

        chip_pack, recv_pack = _exchange_wait("reduce_pack_wait", pack_sems_s, pack_sems_r, pack_bufs, copies,
                                              self.swapping[0][3])
        total_pack = _chip_sum(chip_pack, recv_pack, kc, None, 0, "chip_sum_pack")
        swap = [(lambda I, O, pos: my_half(0, I[0], pos), lambda I, O, pos: my_half(0, O[0], pos), "c")]
        return _exchange("swap_pack", [total_pack], [jax.ShapeDtypeStruct(pack.shape, F32)], {0: 0}, [], swap)[0]

    def finish_big(self, after):
        (send_sems, recv_sems, bufs, _), swap, names = self.swapping
        return dict(zip(names, _exchange_wait("swap_wait", send_sems, recv_sems, bufs, swap, after)))


WEIGHTS = ("meta_tokens", "norm_mix_g", "w_in", "conv_w", "ssm_lam_re", "ssm_lam_im", "ssm_log_dt", "ssm_b_re",
           "ssm_b_im", "ssm_c_re", "ssm_c_im", "ssm_d", "ssm_w_glu", "gain_conv_out", "gain_ssm_out", "w_out",
           "norm_ffn_g", "w_up", "ffn_conv_w", "ffn_conv_b", "w_down", "norm_final_g")
TINY_SHARDED = ("meta_tokens", "conv_w", "ffn_conv_w")
REPLICATED = tuple(n for n in WEIGHTS if n not in BIG and n not in TINY_SHARDED)
PACK_COLS = 512


def _pack(arrays, row_mult, cols):
    flat = jnp.concatenate([a.reshape(-1).astype(F32) for a in arrays])
    n = flat.shape[0]
    total = -(-n // (row_mult * cols)) * (row_mult * cols)
    return jnp.concatenate([flat, jnp.zeros((total - n,), F32)]).reshape(total // cols, cols)


def _unpack(packed, shapes):
    flat = packed.reshape(-1)
    out, off = [], 0
    for s in shapes:
        n = math.prod(s)
        out.append(flat[off:off + n].reshape(s))
        off += n
    return out


def kernel(x, meta_tokens, norm_mix_g, w_in, conv_w, ssm_lam_re, ssm_lam_im, ssm_log_dt, ssm_b_re, ssm_b_im, ssm_c_re, ssm_c_im, ssm_d, ssm_w_glu, gain_conv_out, gain_ssm_out, w_out, norm_ffn_g, w_up, ffn_conv_w, ffn_conv_b, w_down, norm_final_g, loss_target, m_meta_tokens, m_norm_mix_g, m_w_in, m_conv_w, m_ssm_lam_re, m_ssm_lam_im, m_ssm_log_dt, m_ssm_b_re, m_ssm_b_im, m_ssm_c_re, m_ssm_c_im, m_ssm_d, m_ssm_w_glu, m_gain_conv_out, m_gain_ssm_out, m_w_out, m_norm_ffn_g, m_w_up, m_ffn_conv_w, m_ffn_conv_b, m_w_down, m_norm_final_g, v_meta_tokens, v_norm_mix_g, v_w_in, v_conv_w, v_ssm_lam_re, v_ssm_lam_im, v_ssm_log_dt, v_ssm_b_re, v_ssm_b_im, v_ssm_c_re, v_ssm_c_im, v_ssm_d, v_ssm_w_glu, v_gain_conv_out, v_gain_ssm_out, v_w_out, v_norm_ffn_g, v_w_up, v_ffn_conv_w, v_ffn_conv_b, v_w_down, v_norm_final_g):
    args = dict(locals())
    w = {n: args[n] for n in WEIGHTS}
    mom = {n: args["m_" + n] for n in WEIGHTS}
    var = {n: args["v_" + n] for n in WEIGHTS}
    kx, ky, kc_ = lax.axis_index("x"), lax.axis_index("y"), lax.axis_index("c")
    chip = 2 * kx + ky
    kc = jnp.stack([chip, kc_]).astype(jnp.int32)

    def squeeze(n, a):
        if n == "meta_tokens":
            return a
        if n == "norm_final_g":
            return a.reshape(1, -1)
        a = a[0]
        return a.reshape(1, -1) if a.ndim == 1 else a

    wl = {n: squeeze(n, w[n]) for n in WEIGHTS}
    ml = {n: squeeze(n, mom[n]) for n in WEIGHTS}
    vl = {n: squeeze(n, var[n]) for n in WEIGHTS}

    tiny = _pack([wl[n] for n in TINY_SHARDED], SUBLANES, LANES)
    ex = _Exchanges({n: wl[n] for n in BIG_NAMES}, tiny, kc)
    tiny_shapes = [wl[n].shape for n in TINY_SHARDED]
    tiny_all = ex.small_params(kc)
    tiny_parts = [_unpack(tiny_all[k], tiny_shapes) for k in range(4)]
    p = {n: wl[n] for n in WEIGHTS if n not in BIG}
    for j, n in enumerate(TINY_SHARDED):
        p[n] = jnp.concatenate([tiny_parts[k][j] for k in range(4)], axis=1)
    p["ssm_log_dt"] = wl["ssm_log_dt"].reshape(-1)

    loss_local, grad_x, grads = _local_step(x[0], loss_target[0], p, ex)

    small_names = REPLICATED + TINY_SHARDED
    small_shapes = [tuple(grads[n].shape) for n in small_names] + [(1,)]
    pack = _pack([grads[n] for n in small_names] + [loss_local.reshape(1)], 2 * 16, PACK_COLS)
    g_pack = ex.finish_pack(pack)
    g_small = dict(zip(small_names + ("loss",), _unpack(g_pack, small_shapes)))
    loss = g_small["loss"][0]
    swapped = ("ssm_b_re", "ssm_b_im")

    def view(n, a):
        if n in swapped:
            return jnp.swapaxes(a, -1, -2)
        return a.reshape(1, -1) if a.ndim == 1 else a

    g = {}
    for n in REPLICATED:
        g[n] = g_small[n].reshape(view(n, w[n]).shape)
    for n in TINY_SHARDED:
        cols = wl[n].shape[1]
        g[n] = lax.dynamic_slice_in_dim(g_small[n], chip * cols, cols, axis=1).reshape(w[n].shape)
    delta, new_m, new_v = {}, {}, {}
    small = [[view(n, d[n]) for n in small_names] for d in (w, mom, var)]
    small.insert(1, [g[n] for n in small_names])
    for d, outs in zip((delta, new_m, new_v), _adamw_whole(*small, "adamw_small")):
        d.update(zip(small_names, outs))
    for d in (g, delta, new_m, new_v):
        d.update({n: jnp.swapaxes(d[n], -1, -2) for n in swapped})
    g_big = ex.finish_big(delta[small_names[0]])
    for n in BIG_NAMES:
        g[n], delta[n], new_m[n], new_v[n] = _adamw(wl[n], g_big[n], ml[n], vl[n], "adamw_" + n)

    def like(n, a):
        return a.reshape(w[n].shape)

    return (loss, grad_x[None], *[like(n, g[n]) for n in WEIGHTS], *[like(n, delta[n]) for n in WEIGHTS],
            *[like(n, new_m[n]) for n in WEIGHTS], *[like(n, new_v[n]) for n in WEIGHTS])
```

```python
import functools
import math

import jax
import jax.numpy as jnp
from jax import lax
from jax.experimental import pallas as pl
from jax.experimental.pallas import tpu as pltpu

F32 = jnp.float32
BF16 = jnp.bfloat16
MESH = pl.DeviceIdType.MESH

N_META = 16
N_GROUPS = 32
GROUP = 16
STATE = 64
RMS_EPS = 1e-6
ADAM_LR = 0.001
ADAM_B1 = 0.9
ADAM_B2 = 0.999
ADAM_EPS = 1e-08
ADAM_WD = 0.01
ADAM_STEP = 10

LANES = 128
SUBLANES = 8
ROW_ALIGN = 128
ROW_TILES = 4
VMEM_LIMIT = 52 * 1024 * 1024
MM_VMEM_BUDGET = 40 * 1024 * 1024
GELU_C = math.sqrt(2.0 / math.pi)
GELU_A = 0.044715


def _cparams(*sem):
    return pltpu.CompilerParams(dimension_semantics=sem, vmem_limit_bytes=VMEM_LIMIT)


def _pick_tile(dim, cap, mult):
    best = None
    for t in range(mult, min(dim, cap) + 1, mult):
        if dim % t == 0:
            best = t
    return best if best is not None else dim


def _mm(a, b, mode, name, out_dtype=F32, acc_in=None, after=None):
    if mode == "tn":
        kdim, m = a.shape
    else:
        m, kdim = a.shape
    n = b.shape[0] if mode == "nt" else b.shape[1]
    tm = _pick_tile(m, 1408, LANES if mode == "tn" else 16)
    tk = _pick_tile(kdim, 2816, LANES)
    nk = kdim // tk
    out_bytes = jnp.dtype(out_dtype).itemsize
    for cap in (1408, 1024, 512, 256, LANES):
        tn = _pick_tile(n, cap, LANES)
        blocks = 2 * (tm * tk * 2 + tk * tn * 2 + tm * tn * out_bytes * (2 if acc_in is not None else 1))
        if blocks + (tm * tn * 4 if nk > 1 else 0) <= MM_VMEM_BUDGET:
            break
    has_acc = acc_in is not None

    def body(*refs):
        if after is not None:
            refs = refs[1:]
        if has_acc:
            a_ref, b_ref, c_ref, o_ref = refs[:4]
            rest = refs[4:]
        else:
            a_ref, b_ref, o_ref = refs[:3]
            c_ref = None
            rest = refs[3:]
        if mode == "nn":
            p = jnp.dot(a_ref[...], b_ref[...], preferred_element_type=F32)
        elif mode == "nt":
            p = lax.dot_general(a_ref[...], b_ref[...], (((1,), (1,)), ((), ())), preferred_element_type=F32)
        else:
            p = lax.dot_general(a_ref[...], b_ref[...], (((0,), (0,)), ((), ())), preferred_element_type=F32)
        if nk == 1:
            if has_acc:
                p = p + c_ref[...]
            o_ref[...] = p.astype(out_dtype)
        else:
            acc_ref = rest[0]
            k = pl.program_id(2)

            @pl.when(k == 0)
            def _():
                acc_ref[...] = p + c_ref[...] if has_acc else p

            @pl.when(k > 0)
            def _():
                acc_ref[...] += p

            @pl.when(k == nk - 1)
            def _():
                o_ref[...] = acc_ref[...].astype(out_dtype)

    if mode == "tn":
        a_spec = pl.BlockSpec((tk, tm), lambda i, j, k: (k, i))
    else:
        a_spec = pl.BlockSpec((tm, tk), lambda i, j, k: (i, k))
    if mode == "nt":
        b_spec = pl.BlockSpec((tn, tk), lambda i, j, k: (j, k))
    else:
        b_spec = pl.BlockSpec((tk, tn), lambda i, j, k: (k, j))
    o_spec = pl.BlockSpec((tm, tn), lambda i, j, k: (i, j))
    in_specs = [a_spec, b_spec] + ([o_spec] if has_acc else [])
    args = (a, b) + ((acc_in,) if has_acc else ())
    if after is not None:
        in_specs = [pl.BlockSpec(memory_space=pl.ANY)] + in_specs
        args = (after,) + args
    return pl.pallas_call(
        body, name=name, grid=(m // tm, n // tn, nk),
        in_specs=in_specs, out_specs=o_spec,
        out_shape=jax.ShapeDtypeStruct((m, n), out_dtype),
        scratch_shapes=[pltpu.VMEM((tm, tn), F32)] if nk > 1 else [],
        compiler_params=_cparams("parallel", "parallel", "arbitrary"),
    )(*args)


def _mm_rows(a, b, mode, name, ins, outs, epilogue, scratch=()):
    m, kdim = a.shape
    n = b.shape[0] if mode == "nt" else b.shape[1]
    tm = m // ROW_TILES
    tk = _pick_tile(kdim, 2816, LANES)
    nk = kdim // tk
    ni, no = len(ins), len(outs)

    def body(*refs):
        a_ref, b_ref = refs[:2]
        in_refs, out_refs, rest = refs[2:2 + ni], refs[2 + ni:2 + ni + no], refs[2 + ni + no:]
        i = pl.program_id(0)
        if mode == "nn":
            p = jnp.dot(a_ref[...], b_ref[...], preferred_element_type=F32)
        else:
            p = lax.dot_general(a_ref[...], b_ref[...], (((1,), (1,)), ((), ())), preferred_element_type=F32)
        if nk == 1:
            epilogue(p, i, in_refs, out_refs, rest)
        else:
            acc_ref = rest[0]
            k = pl.program_id(1)

            @pl.when(k == 0)
            def _():
                acc_ref[...] = p

            @pl.when(k > 0)
            def _():
                acc_ref[...] += p

            @pl.when(k == nk - 1)
            def _():
                epilogue(acc_ref[...], i, in_refs, out_refs, rest[1:])

    def spec(shape, kind):
        if kind == "rows":
            return pl.BlockSpec((tm,) + tuple(shape[1:]), lambda i, k: (i,) + (0,) * (len(shape) - 1))
        if kind == "whole":
            return pl.BlockSpec(tuple(shape), lambda i, k: (0,) * len(shape))
        return pl.BlockSpec(memory_space=pl.ANY)

    a_spec = pl.BlockSpec((tm, tk), lambda i, k: (i, k))
    b_spec = pl.BlockSpec((n, tk), lambda i, k: (0, k)) if mode == "nt" else pl.BlockSpec((tk, n), lambda i, k: (k, 0))
    return pl.pallas_call(
        body, name=name, grid=(ROW_TILES, nk),
        in_specs=[a_spec, b_spec] + [spec(x.shape, kind) for x, kind in ins],
        out_specs=[spec(shape, kind) for shape, _, kind in outs],
        out_shape=[jax.ShapeDtypeStruct(shape, dtype) for shape, dtype, _ in outs],
        scratch_shapes=([pltpu.VMEM((tm, n), F32)] if nk > 1 else []) + list(scratch),
        compiler_params=_cparams("arbitrary", "arbitrary"),
    )(a, b, *[x for x, _ in ins])


def _rows(shape_cols, tr, dtype=None):
    return pl.BlockSpec((tr, shape_cols), lambda i: (i, 0))


def _const(shape):
    return pl.BlockSpec(shape, lambda i: (0,) * len(shape))


def _rms(x):
    return lax.rsqrt(jnp.mean(x * x, axis=-1, keepdims=True) + RMS_EPS)


def _rms_bwd(x, r, g, dy):
    xn = x * r
    dxn = dy * g
    dx = r * (dxn - xn * jnp.mean(dxn * xn, axis=-1, keepdims=True))
    return dx, dy * xn


def _gelu(y):
    return 0.5 * y * (1.0 + jnp.tanh(GELU_C * (y + GELU_A * y * y * y)))


def _gelu_grad(y):
    t = jnp.tanh(GELU_C * (y + GELU_A * y * y * y))
    return 0.5 * (1.0 + t) + 0.5 * y * (1.0 - t * t) * GELU_C * (1.0 + 3.0 * GELU_A * y * y)


def _sigmoid(z):
    return 1.0 / (1.0 + jnp.exp(-z))


def _proj_res_norm(a, w, h, g, after, name):
    def epilogue(p, i, ins, outs, _):
        x = ins[0][...] + p
        outs[0][...] = x
        outs[1][...] = (x * _rms(x) * ins[1][...]).astype(BF16)

    return _mm_rows(a, w, "nn", name, [(h, "rows"), (g, "whole"), (after, "hbm")],
                    [(h.shape, F32, "rows"), (h.shape, BF16, "rows")], epilogue)


def _proj_norm_bwd(da, w, h, g, dres, after, name):
    d = h.shape[1]

    def epilogue(p, i, ins, outs, _):
        x = ins[0][...]
        dx, dgs = _rms_bwd(x, _rms(x), ins[1][...], p)
        dh = ins[2][...] + dx
        outs[0][...] = dh
        outs[1][...] = dh.astype(BF16)

        @pl.when(i == 0)
        def _():
            outs[2][...] = jnp.zeros_like(outs[2])

        outs[2][...] += jnp.sum(dgs, axis=0, keepdims=True)

    return _mm_rows(da, w, "nt", name, [(h, "rows"), (g, "whole"), (dres, "rows"), (after, "hbm")],
                    [(h.shape, F32, "rows"), (h.shape, BF16, "rows"), ((1, d), F32, "whole")], epilogue)


def _proj_input_norm_bwd(da, w, h, g, dres, after, n_real, name):
    tp, d = h.shape
    tr = tp // ROW_TILES

    def epilogue(p, i, ins, outs, scratch):
        h_ref, g_ref, dres_ref, _ = ins
        dx_ref, dmeta_ref, dg_ref = outs
        stage, sem = scratch
        x = h_ref[...]
        dx, dgs = _rms_bwd(x, _rms(x), g_ref[...], p)
        stage[...] = dres_ref[...] + dx

        @pl.when(i == 0)
        def _():
            dg_ref[...] = jnp.zeros_like(dg_ref)
            dmeta_ref[...] = stage[:N_META, :]

        dg_ref[...] += jnp.sum(dgs, axis=0, keepdims=True)
        for t in range(ROW_TILES):
            lo, hi = max(t * tr, N_META), min((t + 1) * tr, n_real)
            if hi > lo:
                @pl.when(i == t)
                def _(t=t, lo=lo, hi=hi):
                    cp = pltpu.make_async_copy(stage.at[pl.ds(lo - t * tr, hi - lo), :],
                                               dx_ref.at[pl.ds(lo - N_META, hi - lo), :], sem)
                    cp.start()
                    cp.wait()

    return _mm_rows(da, w, "nt", name, [(h, "rows"), (g, "whole"), (dres, "rows"), (after, "hbm")],
                    [((n_real - N_META, d), F32, "hbm"), ((N_META, d), F32, "whole"), ((1, d), F32, "whole")],
                    epilogue, scratch=[pltpu.VMEM((tr, d), F32), pltpu.SemaphoreType.DMA])


def _load_token_rows(tok_hbm, buf, sem, tr, n_real, head=None, wait=False, i=None):
    i = pl.program_id(0) if i is None else i
    for t in range(ROW_TILES):
        base = t * tr
        lo, hi = max(base, N_META), min(base + tr, n_real)

        @pl.when(i == t)
        def _(base=base, lo=lo, hi=hi):
            if hi > lo:
                cp = pltpu.make_async_copy(tok_hbm.at[pl.ds(lo - N_META, hi - lo), :],
                                           buf.at[pl.ds(lo - base, hi - lo), :], sem)
                if wait:
                    cp.wait()
                    return
                cp.start()
            if wait:
                return
            if base < N_META:
                buf[0:N_META - base, :] = (jnp.zeros((N_META - base, buf.shape[1]), F32) if head is None
                                           else head[base:N_META, :])
            if hi < base + tr:
                buf[max(hi, base) - base:tr, :] = jnp.zeros((base + tr - max(hi, base), buf.shape[1]), F32)


def _input_norm_fwd(x, meta, g, tp, name):
    seq, d = x.shape
    tr = tp // ROW_TILES
    n_real = N_META + seq

    def body(x_hbm, meta_ref, g_ref, h_ref, hn_ref, buf, sem):
        _load_token_rows(x_hbm, buf, sem, tr, n_real, head=meta_ref)
        _load_token_rows(x_hbm, buf, sem, tr, n_real, wait=True)
        h = buf[...]
        h_ref[...] = h
        hn_ref[...] = (h * _rms(h) * g_ref[...]).astype(BF16)

    return pl.pallas_call(
        body, name=name, grid=(ROW_TILES,),
        in_specs=[pl.BlockSpec(memory_space=pl.ANY), _const((N_META, d)), _const((1, d))],
        out_specs=[_rows(d, tr), _rows(d, tr)],
        out_shape=[jax.ShapeDtypeStruct((tp, d), F32), jax.ShapeDtypeStruct((tp, d), BF16)],
        scratch_shapes=[pltpu.VMEM((tr, d), F32), pltpu.SemaphoreType.DMA],
        compiler_params=_cparams("arbitrary"))(x, meta, g)


def _proj_loss_bwd(act, w, h1, target, g, n_real, name):
    tp, d = h1.shape
    tr = tp // ROW_TILES

    def epilogue(p, i, ins, outs, scratch):
        h1_ref, t_hbm, g_ref = ins
        loss_ref, dh_ref, dhb_ref, dg_ref = outs
        t_buf, sem = scratch
        _load_token_rows(t_hbm, t_buf, sem, tr, n_real, i=i)
        x = h1_ref[...] + p
        r = _rms(x)
        row = i * tr + lax.broadcasted_iota(jnp.int32, (tr, d), 0)
        valid = (row >= N_META) & (row < n_real)
        _load_token_rows(t_hbm, t_buf, sem, tr, n_real, wait=True, i=i)
        e = jnp.where(valid, x * r * g_ref[...] - t_buf[...], 0.0)
        dx, dgs = _rms_bwd(x, r, g_ref[...], e * (1.0 / d))
        dh_ref[...] = dx
        dhb_ref[...] = dx.astype(BF16)

        @pl.when(i == 0)
        def _():
            dg_ref[...] = jnp.zeros_like(dg_ref)
            loss_ref[...] = jnp.zeros_like(loss_ref)

        dg_ref[...] += jnp.sum(dgs, axis=0, keepdims=True)
        loss_ref[...] += (0.5 / d) * jnp.sum(jnp.sum(e * e, axis=0, keepdims=True), axis=1, keepdims=True)

    return _mm_rows(act, w, "nn", name, [(h1, "rows"), (target, "hbm"), (g, "whole")],
                    [((1, LANES), F32, "whole"), ((tp, d), F32, "rows"), ((tp, d), BF16, "rows"),
                     ((1, d), F32, "whole")],
                    epilogue, scratch=[pltpu.VMEM((tr, d), F32), pltpu.SemaphoreType.DMA])


def _mix_fwd(co, y, z, gc, gs, name):
    tp, dh = co.shape
    tr = tp // ROW_TILES

    def body(co_ref, y_ref, z_ref, gc_ref, gs_ref, m_ref):
        c = co_ref[...]
        m_ref[:, :dh] = (c * _rms(c) * gc_ref[...]).astype(BF16)
        so = _gelu(y_ref[...]) * _sigmoid(z_ref[...])
        m_ref[:, dh:] = (so * _rms(so) * gs_ref[...]).astype(BF16)

    return pl.pallas_call(
        body, name=name, grid=(ROW_TILES,),
        in_specs=[_rows(dh, tr)] * 3 + [_const((1, dh))] * 2,
        out_specs=_rows(2 * dh, tr),
        out_shape=jax.ShapeDtypeStruct((tp, 2 * dh), BF16),
        compiler_params=_cparams("parallel"))(co, y, z, gc, gs)


def _proj_mix_bwd(dh1b, w, co, y, z, gc, gs, name):
    tp, dh = co.shape

    def epilogue(p, i, ins, outs, _):
        co_ref, y_ref, z_ref, gc_ref, gs_ref = ins
        dco_ref, dz_ref, dgp_ref, dgc_ref, dgs_ref = outs
        c = co_ref[...]
        dco, dgc = _rms_bwd(c, _rms(c), gc_ref[...], p[:, :dh])
        dco_ref[...] = dco
        gl = _gelu(y_ref[...])
        sg = _sigmoid(z_ref[...])
        so = gl * sg
        dso, dgs = _rms_bwd(so, _rms(so), gs_ref[...], p[:, dh:])
        dz_ref[...] = (dso * gl * sg * (1.0 - sg)).astype(BF16)
        dgp_ref[...] = dso * sg

        @pl.when(i == 0)
        def _():
            dgc_ref[...] = jnp.zeros_like(dgc_ref)
            dgs_ref[...] = jnp.zeros_like(dgs_ref)

        dgc_ref[...] += jnp.sum(dgc, axis=0, keepdims=True)
        dgs_ref[...] += jnp.sum(dgs, axis=0, keepdims=True)

    return _mm_rows(dh1b, w, "nt", name,
                    [(co, "rows"), (y, "rows"), (z, "rows"), (gc, "whole"), (gs, "whole")],
                    [((tp, dh), F32, "rows"), ((tp, dh), BF16, "rows"), ((tp, dh), F32, "rows"),
                     ((1, dh), F32, "whole"), ((1, dh), F32, "whole")], epilogue)


def _shift_down(x, k):
    row = lax.broadcasted_iota(jnp.int32, x.shape, 0)
    return jnp.where(row >= k, pltpu.roll(x, k, 0), 0.0)


def _shift_up(x, k):
    n = x.shape[0]
    row = lax.broadcasted_iota(jnp.int32, x.shape, 0)
    return jnp.where(row < n - k, pltpu.roll(x, n - k, 0), 0.0)


def _dwconv(x, w_ref):
    return w_ref[2:3, :] * x + w_ref[1:2, :] * _shift_down(x, 1) + w_ref[0:1, :] * _shift_down(x, 2)


def _dwconv_bwd(x, dy, w_ref):
    dx = w_ref[2:3, :] * dy + w_ref[1:2, :] * _shift_up(dy, 1) + w_ref[0:1, :] * _shift_up(dy, 2)
    dw = jnp.concatenate([jnp.sum(dy * _shift_down(x, 2), axis=0, keepdims=True),
                          jnp.sum(dy * _shift_down(x, 1), axis=0, keepdims=True),
                          jnp.sum(dy * x, axis=0, keepdims=True)], axis=0)
    return dx, dw


def _interleave(dst, src):
    seg_rows = src.shape[0] // SUBLANES
    for seg in range(SUBLANES):
        dst[pl.ds(seg, seg_rows, stride=SUBLANES), :] = src[seg * seg_rows:(seg + 1) * seg_rows, :]


def _deinterleave(dst, src):
    seg_rows = src.shape[0] // SUBLANES
    for seg in range(SUBLANES):
        dst[seg * seg_rows:(seg + 1) * seg_rows, :] = src[pl.ds(seg, seg_rows, stride=SUBLANES), :]


def _segment_shift(x, reverse):
    row = lax.broadcasted_iota(jnp.int32, x.shape, 0)
    if reverse:
        return jnp.where(row < SUBLANES - 1, pltpu.roll(x, SUBLANES - 1, 0), 0.0)
    return jnp.where(row >= 1, pltpu.roll(x, 1, 0), 0.0)


def _scan(s_re, s_im, pw_ref, reverse, pair=None):
    n_steps = s_re.shape[0] // SUBLANES
    n_strips = s_re.shape[1] // LANES
    sign = -1.0 if reverse else 1.0
    strips = [slice(st * LANES, (st + 1) * LANES) for st in range(n_strips)]

    def rows_of(j):
        step = (n_steps - 1 - j) if reverse else j
        return pl.ds(pl.multiple_of(step * SUBLANES, SUBLANES), SUBLANES)

    a = [(jnp.broadcast_to(pw_ref[0, 0:1, lanes], (SUBLANES, LANES)),
          sign * jnp.broadcast_to(pw_ref[1, 0:1, lanes], (SUBLANES, LANES))) for lanes in strips]

    def local(i, carry):
        for half in range(2):
            rows = rows_of(2 * i + half)
            out = []
            for st, lanes in enumerate(strips):
                (ar, ai), cr, ci = a[st], carry[2 * st], carry[2 * st + 1]
                xr = s_re[rows, lanes] + (ar * cr - ai * ci)
                xi = s_im[rows, lanes] + (ar * ci + ai * cr)
                s_re[rows, lanes] = xr
                s_im[rows, lanes] = xi
                out += [xr, xi]
            carry = tuple(out)
        return carry

    zero = jnp.zeros((SUBLANES, LANES), F32)
    ends = lax.fori_loop(0, n_steps // 2, local, (zero,) * (2 * n_strips))

    entering = []
    row = lax.broadcasted_iota(jnp.int32, (SUBLANES, LANES), 0)
    for st, lanes in enumerate(strips):
        tr, ti = ends[2 * st], ends[2 * st + 1]
        mr = jnp.broadcast_to(pw_ref[0, n_steps - 1:n_steps, lanes], (SUBLANES, LANES))
        mi = sign * jnp.broadcast_to(pw_ref[1, n_steps - 1:n_steps, lanes], (SUBLANES, LANES))
        for k in (1, 2, 4):
            keep = (row < SUBLANES - k) if reverse else (row >= k)
            rr = jnp.where(keep, pltpu.roll(tr, SUBLANES - k if reverse else k, 0), 0.0)
            ri = jnp.where(keep, pltpu.roll(ti, SUBLANES - k if reverse else k, 0), 0.0)
            tr, ti = tr + (mr * rr - mi * ri), ti + (mr * ri + mi * rr)
            mr, mi = mr * mr - mi * mi, 2.0 * mr * mi
        entering += [_segment_shift(tr, reverse), _segment_shift(ti, reverse)]

    def fix(i, carry):
        carry, sums = carry[:2 * n_strips], carry[2 * n_strips:]
        for half in range(2):
            j = 2 * i + half
            rows = rows_of(j)
            out, acc = [], []
            for st, lanes in enumerate(strips):
                (ar, ai), cr, ci = a[st], carry[2 * st], carry[2 * st + 1]
                cr, ci = ar * cr - ai * ci, ar * ci + ai * cr
                xr = s_re[rows, lanes] + cr
                xi = s_im[rows, lanes] + ci
                s_re[rows, lanes] = xr
                s_im[rows, lanes] = xi
                out += [cr, ci]
                if pair is not None:
                    p_rows = rows_of(jnp.minimum(j + 1, n_steps - 1))
                    keep = (j < n_steps - 1).astype(F32)
                    pr = pair[0][p_rows, lanes] * keep
                    pi = pair[1][p_rows, lanes] * keep
                    acc += [sums[2 * st] + (xr * pr + xi * pi), sums[2 * st + 1] + (xi * pr - xr * pi)]
            carry, sums = tuple(out), tuple(acc)
        return carry + sums

    n_sums = 0 if pair is None else 2 * n_strips
    out = lax.fori_loop(0, n_steps // 2, fix, tuple(entering) + (zero,) * n_sums)
    return out[2 * n_strips:]


def _seq_fwd(proj, conv_w, bc_re, bc_im, cc_re, cc_im, dskip, a_pow, name):
    tp = proj.shape[0]
    dh = proj.shape[1] // 4
    nq = dh // LANES
    sw = STATE * N_GROUPS // nq

    def body(b_ref, c_ref, v_ref, u_ref, w_ref, bre_ref, bim_ref, cre_ref, cim_ref, d_ref, pw_ref,
             co_ref, y_ref, g_ref, s_re, s_im, u_il, y_il):
        co_ref[...] = b_ref[...] * _dwconv(c_ref[...] * v_ref[...], w_ref)
        _interleave(u_il, u_ref)
        ub = u_il[...].astype(BF16)
        s_re[...] = jnp.dot(ub, bre_ref[...], preferred_element_type=F32)
        s_im[...] = jnp.dot(ub, bim_ref[...], preferred_element_type=F32)
        _scan(s_re, s_im, pw_ref, False)
        y_il[...] = (jnp.dot(s_re[...].astype(BF16), cre_ref[...], preferred_element_type=F32)
                     - jnp.dot(s_im[...].astype(BF16), cim_ref[...], preferred_element_type=F32))
        _deinterleave(y_ref, y_il)
        y = y_ref[...] + d_ref[...] * u_ref[...]
        y_ref[...] = y
        g_ref[...] = _gelu(y).astype(BF16)

    col = lambda off: pl.BlockSpec((tp, LANES), lambda q, off=off: (0, off * nq + q))
    blk = pl.BlockSpec((tp, LANES), lambda q: (0, q))
    return pl.pallas_call(
        body, name=name, grid=(nq,),
        in_specs=[col(0), col(1), col(2), col(3),
                  pl.BlockSpec((3, LANES), lambda q: (0, q)),
                  pl.BlockSpec((LANES, sw), lambda q: (0, q)), pl.BlockSpec((LANES, sw), lambda q: (0, q)),
                  pl.BlockSpec((sw, LANES), lambda q: (q, 0)), pl.BlockSpec((sw, LANES), lambda q: (q, 0)),
                  pl.BlockSpec((1, LANES), lambda q: (0, q)),
                  pl.BlockSpec((2, tp // SUBLANES, sw), lambda q: (0, 0, q))],
        out_specs=[blk, blk, blk, pl.BlockSpec((tp, sw), lambda q: (0, q)), pl.BlockSpec((tp, sw), lambda q: (0, q))],
        out_shape=[jax.ShapeDtypeStruct((tp, dh), F32), jax.ShapeDtypeStruct((tp, dh), F32),
                   jax.ShapeDtypeStruct((tp, dh), BF16),
                   jax.ShapeDtypeStruct((tp, nq * sw), F32), jax.ShapeDtypeStruct((tp, nq * sw), F32)],
        scratch_shapes=[pltpu.VMEM((tp, LANES), F32), pltpu.VMEM((tp, LANES), F32)],
        compiler_params=_cparams("parallel"),
    )(proj, proj, proj, proj, conv_w, bc_re, bc_im, cc_re, cc_im, dskip, a_pow)


def _conv_bwd(proj, dco, conv_w, name):
    tp = proj.shape[0]
    dh = proj.shape[1] // 4
    nq = dh // LANES

    def body(b_ref, c_ref, v_ref, dco_ref, w_ref, dproj_ref, dw_ref, stage, sem):
        q = pl.program_id(0)
        cg = c_ref[...]
        vg = v_ref[...]
        cv = cg * vg
        dco_v = dco_ref[...]
        dcv, dw = _dwconv_bwd(cv, dco_v * b_ref[...], w_ref)
        dw_ref[...] = dw
        stage[0] = (dco_v * _dwconv(cv, w_ref)).astype(BF16)
        stage[1] = (dcv * vg).astype(BF16)
        stage[2] = (dcv * cg).astype(BF16)
        copies = [pltpu.make_async_copy(stage.at[p], dproj_ref.at[:, pl.ds((p * nq + q) * LANES, LANES)], sem.at[p])
                  for p in range(3)]
        for cp in copies:
            cp.start()
        for cp in copies:
            cp.wait()

    col = lambda off: pl.BlockSpec((tp, LANES), lambda q, off=off: (0, off * nq + q))
    return pl.pallas_call(
        body, name=name, grid=(nq,),
        in_specs=[col(0), col(1), col(2), pl.BlockSpec((tp, LANES), lambda q: (0, q)),
                  pl.BlockSpec((3, LANES), lambda q: (0, q))],
        out_specs=[pl.BlockSpec(memory_space=pl.ANY), pl.BlockSpec((3, LANES), lambda q: (0, q))],
        out_shape=[jax.ShapeDtypeStruct((tp, 4 * dh), BF16), jax.ShapeDtypeStruct((3, dh), F32)],
        scratch_shapes=[pltpu.VMEM((3, tp, LANES), BF16), pltpu.SemaphoreType.DMA((3,))],
        compiler_params=_cparams("arbitrary"),
    )(proj, proj, proj, dco, conv_w)


def _ssm_bwd(proj, y, dg, dproj, states, bc_re, bc_im, cc_re, cc_im, dskip, a_pow, name):
    tp = proj.shape[0]
    dh = proj.shape[1] // 4
    nq = dh // LANES
    sw = STATE * N_GROUPS // nq

    def body(u_ref, y_ref, dg_ref, dproj_in, s_re, s_im, bre_ref, bim_ref, cre_ref, cim_ref, d_ref, pw_ref,
             dproj_ref, dbre_ref, dbim_ref, dcre_ref, dcim_ref, dd_ref, dar_ref, dai_ref,
             l_re, l_im, a_il, b_il, stage, sem):
        del dproj_in
        q = pl.program_id(0)
        nt = (((1,), (1,)), ((), ()))
        tn = (((0,), (0,)), ((), ()))
        _interleave(a_il, u_ref)
        ub = a_il[...].astype(BF16)
        dy_rows = dg_ref[...] * _gelu_grad(y_ref[...])
        dd_ref[...] = jnp.sum(dy_rows * u_ref[...], axis=0, keepdims=True)
        _interleave(b_il, dy_rows)
        dy = b_il[...]
        dyb = dy.astype(BF16)
        l_re[...] = lax.dot_general(dyb, cre_ref[...], nt, preferred_element_type=F32)
        l_im[...] = -lax.dot_general(dyb, cim_ref[...], nt, preferred_element_type=F32)
        dcre_ref[...] = lax.dot_general(s_re[...].astype(BF16), dyb, tn, preferred_element_type=F32)
        dcim_ref[...] = -lax.dot_general(s_im[...].astype(BF16), dyb, tn, preferred_element_type=F32)
        sums = _scan(l_re, l_im, pw_ref, True, pair=(s_re, s_im))
        rest = tp - SUBLANES
        for st in range(sw // LANES):
            lanes = slice(st * LANES, (st + 1) * LANES)
            lr0, li0 = l_re[:SUBLANES, lanes], l_im[:SUBLANES, lanes]
            pr0, pi0 = _segment_shift(s_re[rest:, lanes], False), _segment_shift(s_im[rest:, lanes], False)
            dar_ref[:, lanes] = jnp.sum(sums[2 * st] + (lr0 * pr0 + li0 * pi0), axis=0, keepdims=True)
            dai_ref[:, lanes] = jnp.sum(sums[2 * st + 1] + (li0 * pr0 - lr0 * pi0), axis=0, keepdims=True)
        lrb = l_re[...].astype(BF16)
        lib = l_im[...].astype(BF16)
        a_il[...] = (dy * d_ref[...] + lax.dot_general(lrb, bre_ref[...], nt, preferred_element_type=F32)
                     + lax.dot_general(lib, bim_ref[...], nt, preferred_element_type=F32))
        _deinterleave(b_il, a_il)
        stage[...] = b_il[...].astype(BF16)
        dbre_ref[...] = lax.dot_general(ub, lrb, tn, preferred_element_type=F32)
        dbim_ref[...] = lax.dot_general(ub, lib, tn, preferred_element_type=F32)
        cp = pltpu.make_async_copy(stage, dproj_ref.at[:, pl.ds((3 * nq + q) * LANES, LANES)], sem)
        cp.start()
        cp.wait()

    blk = pl.BlockSpec((tp, LANES), lambda q: (0, q))
    bspec = pl.BlockSpec((LANES, sw), lambda q: (0, q))
    cspec = pl.BlockSpec((sw, LANES), lambda q: (q, 0))
    tspec = pl.BlockSpec((2, tp // SUBLANES, sw), lambda q: (0, 0, q))
    nstate = STATE * N_GROUPS
    return pl.pallas_call(
        body, name=name, grid=(nq,),
        in_specs=[pl.BlockSpec((tp, LANES), lambda q: (0, 3 * nq + q)), blk, blk, pl.BlockSpec(memory_space=pl.ANY),
                  pl.BlockSpec((tp, sw), lambda q: (0, q)), pl.BlockSpec((tp, sw), lambda q: (0, q)),
                  bspec, bspec, cspec, cspec, pl.BlockSpec((1, LANES), lambda q: (0, q)), tspec],
        out_specs=[pl.BlockSpec(memory_space=pl.ANY), bspec, bspec, cspec, cspec,
                   pl.BlockSpec((1, LANES), lambda q: (0, q)),
                   pl.BlockSpec((1, sw), lambda q: (0, q)), pl.BlockSpec((1, sw), lambda q: (0, q))],
        out_shape=[jax.ShapeDtypeStruct((tp, 4 * dh), BF16),
                   jax.ShapeDtypeStruct((LANES, nstate), F32), jax.ShapeDtypeStruct((LANES, nstate), F32),
                   jax.ShapeDtypeStruct((nstate, LANES), F32), jax.ShapeDtypeStruct((nstate, LANES), F32),
                   jax.ShapeDtypeStruct((1, dh), F32),
                   jax.ShapeDtypeStruct((1, nstate), F32), jax.ShapeDtypeStruct((1, nstate), F32)],
        input_output_aliases={3: 0},
        scratch_shapes=[pltpu.VMEM((tp, sw), F32)] * 2 + [pltpu.VMEM((tp, LANES), F32)] * 2
        + [pltpu.VMEM((tp, LANES), BF16), pltpu.SemaphoreType.DMA],
        compiler_params=_cparams("arbitrary"),
    )(proj, y, dg, dproj, states[0], states[1], bc_re, bc_im, cc_re, cc_im, dskip, a_pow)


FFN_TILE = 256


def _ffn_act(up, fw, fb, name):
    tp, two_ff = up.shape
    dff = two_ff // 2
    tc = FFN_TILE
    nj = dff // tc

    def body(ua_ref, uv_ref, wa_ref, wv_ref, ba_ref, bv_ref, act_ref):
        a = _dwconv(ua_ref[...], wa_ref) + ba_ref[...]
        v = _dwconv(uv_ref[...], wv_ref) + bv_ref[...]
        act_ref[...] = (a * _sigmoid(a) * v).astype(BF16)

    lo = lambda r: pl.BlockSpec((r, tc), lambda j: (0, j))
    hi = lambda r: pl.BlockSpec((r, tc), lambda j: (0, nj + j))
    return pl.pallas_call(
        body, name=name, grid=(nj,),
        in_specs=[lo(tp), hi(tp), lo(3), hi(3), lo(1), hi(1)],
        out_specs=lo(tp),
        out_shape=jax.ShapeDtypeStruct((tp, dff), BF16),
        compiler_params=_cparams("parallel"))(up, up, fw, fw, fb, fb)


def _ffn_bwd(up, dact, fw, fb, name):
    tp, two_ff = up.shape
    dff = two_ff // 2
    tc = FFN_TILE
    nj = dff // tc

    def body(ua_ref, uv_ref, da_ref, wa_ref, wv_ref, ba_ref, bv_ref,
             dup_ref, dwa_ref, dwv_ref, dba_ref, dbv_ref, stage, sem):
        j = pl.program_id(0)
        ua = ua_ref[...]
        uv = uv_ref[...]
        a = _dwconv(ua, wa_ref) + ba_ref[...]
        v = _dwconv(uv, wv_ref) + bv_ref[...]
        sg = _sigmoid(a)
        dact_v = da_ref[...]
        da = dact_v * v * sg * (1.0 + a * (1.0 - sg))
        dv = dact_v * a * sg
        dba_ref[...] = jnp.sum(da, axis=0, keepdims=True)
        dbv_ref[...] = jnp.sum(dv, axis=0, keepdims=True)
        dua, dwa = _dwconv_bwd(ua, da, wa_ref)
        duv, dwv = _dwconv_bwd(uv, dv, wv_ref)
        dwa_ref[...] = dwa
        dwv_ref[...] = dwv
        stage[0] = dua.astype(BF16)
        stage[1] = duv.astype(BF16)
        copies = [pltpu.make_async_copy(stage.at[p], dup_ref.at[:, pl.ds((p * nj + j) * tc, tc)], sem.at[p])
                  for p in range(2)]
        for cp in copies:
            cp.start()
        for cp in copies:
            cp.wait()

    lo = lambda r: pl.BlockSpec((r, tc), lambda j: (0, j))
    hi = lambda r: pl.BlockSpec((r, tc), lambda j: (0, nj + j))
    return pl.pallas_call(
        body, name=name, grid=(nj,),
        in_specs=[lo(tp), hi(tp), lo(tp), lo(3), hi(3), lo(1), hi(1)],
        out_specs=[pl.BlockSpec(memory_space=pl.ANY), lo(3), lo(3), lo(1), lo(1)],
        out_shape=[jax.ShapeDtypeStruct((tp, two_ff), BF16),
                   jax.ShapeDtypeStruct((3, dff), F32), jax.ShapeDtypeStruct((3, dff), F32),
                   jax.ShapeDtypeStruct((1, dff), F32), jax.ShapeDtypeStruct((1, dff), F32)],
        scratch_shapes=[pltpu.VMEM((2, tp, tc), BF16), pltpu.SemaphoreType.DMA((2,))],
        compiler_params=_cparams("arbitrary"))(up, up, dact, fw, fw, fb, fb)


def _zoh(lr, li, ld):
    dt = jnp.exp(ld)
    mag = jnp.exp(lr * dt)
    ang = li * dt
    ar = mag * jnp.cos(ang)
    ai = mag * jnp.sin(ang)
    den = lr * lr + li * li
    nr = ar - 1.0
    fr = (nr * lr + ai * li) / den
    fi = (ai * lr - nr * li) / den
    return dt, ar, ai, den, nr, fr, fi


def _s5_prep(lr, li, ld, b_re, b_im, n_pow, name):
    nstate = lr.shape[1]

    def body(lr_ref, li_ref, ld_ref, bre_ref, bim_ref, pw_ref, bcre_ref, bcim_ref):
        _, ar, ai, _, _, fr, fi = _zoh(lr_ref[...], li_ref[...], ld_ref[...])
        bre = bre_ref[...]
        bim = bim_ref[...]
        bcre_ref[...] = (fr * bre - fi * bim).astype(BF16)
        bcim_ref[...] = (fr * bim + fi * bre).astype(BF16)
        row = lax.broadcasted_iota(jnp.int32, (SUBLANES, nstate), 0)
        pr, pi = jnp.zeros((SUBLANES, nstate), F32), jnp.zeros((SUBLANES, nstate), F32)
        cr, ci = ar, ai
        for t in range(SUBLANES):
            pr, pi = jnp.where(row == t, cr, pr), jnp.where(row == t, ci, pi)
            cr, ci = cr * ar - ci * ai, cr * ai + ci * ar
        pw_ref[0, 0:SUBLANES, :] = pr
        pw_ref[1, 0:SUBLANES, :] = pi
        n = SUBLANES
        while n < n_pow:
            m = min(n, n_pow - n)
            tr, ti = pw_ref[0, n - 1:n, :], pw_ref[1, n - 1:n, :]
            xr, xi = pw_ref[0, 0:m, :], pw_ref[1, 0:m, :]
            pw_ref[0, n:n + m, :] = xr * tr - xi * ti
            pw_ref[1, n:n + m, :] = xr * ti + xi * tr
            n += m

    vmem = pl.BlockSpec(memory_space=pltpu.VMEM)
    return pl.pallas_call(
        body, name=name, in_specs=[vmem] * 5, out_specs=[vmem] * 3,
        out_shape=[jax.ShapeDtypeStruct((2, n_pow, nstate), F32)] + [jax.ShapeDtypeStruct(b_re.shape, BF16)] * 2,
        compiler_params=pltpu.CompilerParams(vmem_limit_bytes=VMEM_LIMIT))(lr, li, ld, b_re, b_im)


def _s5_prep_bwd(lr, li, ld, b_re, b_im, da_re, da_im, dbc_re, dbc_im, name):
    def body(lr_ref, li_ref, ld_ref, bre_ref, bim_ref, dar_ref, dai_ref, dbcre_ref, dbcim_ref,
             dlr_ref, dli_ref, dld_ref, dbre_ref, dbim_ref):
        lr, li = lr_ref[...], li_ref[...]
        dt, ar, ai, den, nr, fr, fi = _zoh(lr, li, ld_ref[...])
        bre, bim = bre_ref[...], bim_ref[...]
        gre, gim = dbcre_ref[...], dbcim_ref[...]
        dbre_ref[...] = fr * gre + fi * gim
        dbim_ref[...] = fr * gim - fi * gre
        g_fr = jnp.sum(gre * bre + gim * bim, axis=0, keepdims=True)
        g_fi = jnp.sum(gim * bre - gre * bim, axis=0, keepdims=True)
        g_ar = dar_ref[...] + (g_fr * lr - g_fi * li) / den
        g_ai = dai_ref[...] + (g_fr * li + g_fi * lr) / den
        d_lr = (g_fr * (nr - 2.0 * fr * lr) + g_fi * (ai - 2.0 * fi * lr)) / den
        d_li = (g_fr * (ai - 2.0 * fr * li) - g_fi * (nr + 2.0 * fi * li)) / den
        g_logmag = g_ar * ar + g_ai * ai
        g_ang = g_ai * ar - g_ar * ai
        dlr_ref[...] = d_lr + g_logmag * dt
        dli_ref[...] = d_li + g_ang * dt
        d_ld = (g_logmag * lr + g_ang * li) * dt
        n = d_ld.shape[1]
        sh = 1
        while sh < STATE:
            d_ld = d_ld + pltpu.roll(d_ld, n - sh, 1)
            sh *= 2
        dld_ref[...] = d_ld

    vmem = pl.BlockSpec(memory_space=pltpu.VMEM)
    row = jax.ShapeDtypeStruct(lr.shape, F32)
    return pl.pallas_call(
        body, name=name, in_specs=[vmem] * 9, out_specs=[vmem] * 5,
        out_shape=[row, row, row, jax.ShapeDtypeStruct(b_re.shape, F32), jax.ShapeDtypeStruct(b_re.shape, F32)],
    )(lr, li, ld, b_re, b_im, da_re, da_im, dbc_re, dbc_im)


def _compact_b(bb):
    bq = bb.reshape(N_GROUPS // 8, 8, STATE, GROUP)
    m = jnp.einsum("ab,qbph->qahbp", jnp.eye(8, dtype=bb.dtype), bq).reshape(N_GROUPS // 8, LANES, 8 * STATE)
    return m.transpose(1, 0, 2).reshape(LANES, N_GROUPS * STATE)


def _expand_b(m):
    d = m.reshape(8, GROUP, N_GROUPS // 8, 8, STATE)
    return jnp.einsum("ahqap->qahp", d).reshape(N_GROUPS, GROUP, STATE)


def _compact_c(c):
    cq = c.reshape(N_GROUPS // 8, 8, GROUP, STATE)
    return jnp.einsum("ab,qbhp->qbpah", jnp.eye(8, dtype=c.dtype), cq).reshape(N_GROUPS * STATE, LANES)


def _expand_c(m):
    d = m.reshape(N_GROUPS // 8, 8, STATE, 8, GROUP)
    return jnp.einsum("qbpbh->qbhp", d).reshape(N_GROUPS, GROUP, STATE)


def _local_step(x, target, p, ex):
    seq, d = x.shape
    n_real = N_META + seq
    tp = -(-n_real // ROW_ALIGN) * ROW_ALIGN

    h0, hn1 = _input_norm_fwd(x, p["meta_tokens"], p["norm_mix_g"] + ex.zero, tp, "norm_mix")
    ex.forward("first", hn1)
    nstate = N_GROUPS * STATE
    s5 = (p["ssm_lam_re"].reshape(1, nstate), p["ssm_lam_im"].reshape(1, nstate),
          jnp.repeat(p["ssm_log_dt"].reshape(-1), STATE).reshape(1, nstate),
          _compact_b(p["ssm_b_re"]), _compact_b(p["ssm_b_im"]))
    a_pow, bc_re, bc_im = _s5_prep(*s5, tp // SUBLANES, "s5_prep")
    cc_re = _compact_c(p["ssm_c_re"]).astype(BF16)
    cc_im = _compact_c(p["ssm_c_im"]).astype(BF16)
    dskip = p["ssm_d"].reshape(1, -1)
    first = ex.weights("first", bc_re)
    proj = _mm(hn1, first["w_in"], "nn", "proj")
    started = ex.forward("mid", proj)
    co, y, g, *states = _seq_fwd(proj, p["conv_w"] + started[0, 0], bc_re, bc_im, cc_re, cc_im, dskip, a_pow,
                                 "seq_fwd")
    mid = ex.weights("mid", g)
    z = _mm(g, mid["ssm_w_glu"], "nn", "glu")
    mixed = _mix_fwd(co, y, z, p["gain_conv_out"], p["gain_ssm_out"], "mix_fwd")
    started = ex.forward("up", mixed)
    h1, hn2 = _proj_res_norm(mixed, mid["w_out"], h0, p["norm_ffn_g"], started, "out_proj_norm")
    late = ex.weights("up", hn2)
    up = _mm(hn2, late["w_up"], "nn", "up_proj")
    started = ex.forward("down", up)
    act = _ffn_act(up, p["ffn_conv_w"] + started[0, 0], p["ffn_conv_b"], "ffn_act")
    late.update(ex.weights("down", act))
    loss, dh2, dh2b, d_gfin = _proj_loss_bwd(act, late["w_down"], h1, target, p["norm_final_g"], n_real,
                                             "down_proj_loss")

    g_w_down = _mm(act, dh2b, "tn", "g_w_down")
    dact = _mm(dh2b, late["w_down"], "nt", "d_act")
    dup, dfw_a, dfw_v, dfb_a, dfb_v = _ffn_bwd(up, dact, p["ffn_conv_w"], p["ffn_conv_b"], "ffn_bwd")
    g_w_up = _mm(hn2, dup, "tn", "g_w_up")
    started = ex.grads_ready("late", {"w_up": g_w_up, "w_down": g_w_down})
    dh1, dh1b, d_gffn = _proj_norm_bwd(dup, late["w_up"], h1, p["norm_ffn_g"], dh2, started, "d_hn2_norm_bwd")
    started = ex.grads_send("late", dh1)
    g_w_out = _mm(mixed, dh1b, "tn", "g_w_out", after=started)
    dco, dz, dgp, d_gc, d_gs = _proj_mix_bwd(dh1b, mid["w_out"], co, y, z, p["gain_conv_out"],
                                             p["gain_ssm_out"], "d_mixed_mix_bwd")
    g_w_glu = _mm(g, dz, "tn", "g_w_glu")
    started = ex.grads_ready("mid", {"ssm_w_glu": g_w_glu, "w_out": g_w_out})
    dg = _mm(dz, mid["ssm_w_glu"], "nt", "d_gelu", acc_in=dgp, after=started)
    started = ex.grads_send("mid", dg)
    dproj, d_conv_w = _conv_bwd(proj, dco, p["conv_w"] + started[0, 0], "conv_bwd")
    (dproj, dbc_re, dbc_im, dcc_re, dcc_im, d_dskip, da_re, da_im) = _ssm_bwd(
        proj, y, dg, dproj, states, bc_re, bc_im, cc_re, cc_im, dskip, a_pow, "ssm_bwd")
    g_w_in = _mm(hn1, dproj, "tn", "g_w_in")
    started = ex.grads_ready("first", {"w_in": g_w_in})
    grad_x, d_meta, d_gmix = _proj_input_norm_bwd(dproj, first["w_in"], h0, p["norm_mix_g"], dh1, started, n_real,
                                                  "d_hn1_norm_bwd")
    started = ex.grads_send("first", d_gmix)

    d_lam_re, d_lam_im, d_log_dt, d_b_re, d_b_im = _s5_prep_bwd(*s5, da_re, da_im, dbc_re, dbc_im, "s5_prep_bwd")
    d_lam_re, d_lam_im = d_lam_re.reshape(N_GROUPS, STATE), d_lam_im.reshape(N_GROUPS, STATE)
    d_log_dt = d_log_dt[0, ::STATE]
    d_b_re, d_b_im = _expand_b(d_b_re), _expand_b(d_b_im)
    grads = {
        "meta_tokens": d_meta, "norm_mix_g": d_gmix, "w_in": g_w_in, "conv_w": d_conv_w,
        "ssm_lam_re": d_lam_re, "ssm_lam_im": d_lam_im, "ssm_log_dt": d_log_dt,
        "ssm_b_re": d_b_re, "ssm_b_im": d_b_im, "ssm_c_re": _expand_c(dcc_re), "ssm_c_im": _expand_c(dcc_im),
        "ssm_d": d_dskip.reshape(N_GROUPS, GROUP), "ssm_w_glu": g_w_glu,
        "gain_conv_out": d_gc, "gain_ssm_out": d_gs, "w_out": g_w_out, "norm_ffn_g": d_gffn,
        "w_up": g_w_up, "ffn_conv_w": jnp.concatenate([dfw_a, dfw_v], axis=1),
        "ffn_conv_b": jnp.concatenate([dfb_a, dfb_v], axis=1), "w_down": g_w_down, "norm_final_g": d_gfin,
    }
    return loss[0, 0] + started[0, 0], grad_x, grads


def _view(ref, axis, start, size):
    idx = [slice(None)] * len(ref.shape)
    idx[axis] = pl.ds(start, size)
    return ref.at[tuple(idx)]


def _exchange(name, ins, outs, aliases, local_copies, remote_copies):
    ni, no = len(ins), len(outs)
    nl, nr = len(local_copies), len(remote_copies)

    def body(*refs):
        in_refs, out_refs = refs[:ni], refs[ni:ni + no]
        send_sems, recv_sems, local_sems = refs[ni + no:]
        x, y, c = lax.axis_index("x"), lax.axis_index("y"), lax.axis_index("c")
        pos = (x, y, c, 2 * x + y)
        locals_ = [pltpu.make_async_copy(s(in_refs, out_refs, pos), d(in_refs, out_refs, pos), local_sems.at[i])
                   for i, (s, d) in enumerate(local_copies)]
        remotes = []
        for i, (s, d, flip) in enumerate(remote_copies):
            peer = (1 - x if "x" in flip else x, 1 - y if "y" in flip else y, 1 - c if "c" in flip else c)
            remotes.append(pltpu.make_async_remote_copy(
                src_ref=s(in_refs, out_refs, pos), dst_ref=d(in_refs, out_refs, pos),
                send_sem=send_sems.at[i], recv_sem=recv_sems.at[i], device_id=peer, device_id_type=MESH))
        for cp in locals_ + remotes:
            cp.start()
        for cp in remotes:
            cp.wait_recv()
        for cp in remotes:
            cp.wait_send()
        for cp in locals_:
            cp.wait()

    hbm = pl.BlockSpec(memory_space=pl.ANY)
    return pl.pallas_call(
        body, name=name, in_specs=[hbm] * ni, out_specs=[hbm] * no, out_shape=outs,
        input_output_aliases=aliases,
        scratch_shapes=[pltpu.SemaphoreType.DMA((nr,)), pltpu.SemaphoreType.DMA((nr,)),
                        pltpu.SemaphoreType.DMA((max(nl, 1),))],
    )(*ins)


BIG = {"w_in": (0, 1), "ssm_w_glu": (1, 0), "w_out": (1, 0), "w_up": (0, 1), "w_down": (1, 0)}
BIG_NAMES = tuple(BIG)
FLIPS = ("y", "x", "xy")


def _peer_chip(pos, flip):
    x, y, _, _ = pos
    return 2 * (1 - x if "x" in flip else x) + (1 - y if "y" in flip else y)


def _block_rows(rows, cols, itemsize, mult):
    return _pick_tile(rows, max(mult, (2 * 1024 * 1024) // (cols * itemsize)), mult)


def _cast_into_full(w, kc, shard_axis, name):
    r, cdim = w.shape
    tr = _block_rows(r, cdim, 4, 16)
    nb = r // tr

    def body(kc_ref, w_ref, o_ref):
        o_ref[...] = w_ref[...].astype(BF16)

    if shard_axis == 1:
        full, o_spec = (r, 4 * cdim), pl.BlockSpec((tr, cdim), lambda i, kc: (i, kc[0]))
    else:
        full, o_spec = (4 * r, cdim), pl.BlockSpec((tr, cdim), lambda i, kc: (kc[0] * nb + i, 0))
    return pl.pallas_call(
        body, name=name,
        grid_spec=pltpu.PrefetchScalarGridSpec(
            num_scalar_prefetch=1, grid=(nb,), in_specs=[pl.BlockSpec((tr, cdim), lambda i, kc: (i, 0))],
            out_specs=o_spec),
        out_shape=jax.ShapeDtypeStruct(full, BF16), compiler_params=_cparams("parallel"))(kc, w)


def _pair_sum(g, recv, kc, half_axis, name, out_dtype):
    hr, hc = recv.shape
    tr = _block_rows(hr, hc, 4, 16)
    nb = hr // tr

    def body(kc_ref, g_ref, r_ref, o_ref):
        o_ref[...] = (g_ref[...] + r_ref[...]).astype(out_dtype)

    if half_axis == 0:
        g_spec = pl.BlockSpec((tr, hc), lambda i, kc: (kc[1] * nb + i, 0))
    elif half_axis == 1:
        g_spec = pl.BlockSpec((tr, hc), lambda i, kc: (i, kc[1]))
    else:
        g_spec = pl.BlockSpec((tr, hc), lambda i, kc: (i, 0))
    same = pl.BlockSpec((tr, hc), lambda i, kc: (i, 0))
    return pl.pallas_call(
        body, name=name,
        grid_spec=pltpu.PrefetchScalarGridSpec(num_scalar_prefetch=1, grid=(nb,), in_specs=[g_spec, same],
                                               out_specs=same),
        out_shape=jax.ShapeDtypeStruct((hr, hc), out_dtype), compiler_params=_cparams("parallel"))(kc, g, recv)


def _chip_sum(own, recv, kc, own_axis, out_axis, name):
    _, sr, sc = recv.shape
    tr = _block_rows(sr, sc, 4, 16)
    nb = sr // tr

    def body(kc_ref, o_ref, r_ref, t_ref):
        k = kc_ref[0]
        own_v = o_ref[...].astype(F32)
        r = [r_ref[m].astype(F32) for m in range(3)]
        terms = []
        for kk in range(4):
            m = jnp.bitwise_xor(k, kk)
            terms.append(jnp.where(m == 0, own_v, jnp.where(m == 1, r[0], jnp.where(m == 2, r[1], r[2]))))
        t_ref[...] = (terms[0] + terms[1]) + (terms[2] + terms[3])

    if own_axis == 0:
        own_spec = pl.BlockSpec((tr, sc), lambda i, kc: (kc[0] * nb + i, 0))
    elif own_axis == 1:
        own_spec = pl.BlockSpec((tr, sc), lambda i, kc: (i, kc[0]))
    else:
        own_spec = pl.BlockSpec((tr, sc), lambda i, kc: (kc[1] * nb + i, 0))
    if out_axis == 0:
        out_full, out_spec = (2 * sr, sc), pl.BlockSpec((tr, sc), lambda i, kc: (kc[1] * nb + i, 0))
    else:
        out_full, out_spec = (sr, 2 * sc), pl.BlockSpec((tr, sc), lambda i, kc: (i, kc[1]))
    return pl.pallas_call(
        body, name=name,
        grid_spec=pltpu.PrefetchScalarGridSpec(
            num_scalar_prefetch=1, grid=(nb,),
            in_specs=[own_spec, pl.BlockSpec((3, tr, sc), lambda i, kc: (0, i, 0))],
            out_specs=out_spec),
        out_shape=jax.ShapeDtypeStruct(out_full, F32), compiler_params=_cparams("parallel"))(kc, own, recv)


def _adamw(w, g, m, v, name):
    r, cdim = w.shape
    tr = _block_rows(r, cdim, 4, 8)
    c1 = 1.0 - ADAM_B1 ** ADAM_STEP
    c2 = 1.0 - ADAM_B2 ** ADAM_STEP

    def body(w_ref, g_ref, m_ref, v_ref, go_ref, d_ref, nm_ref, nv_ref):
        gv = g_ref[...]
        go_ref[...] = gv
        nm = ADAM_B1 * m_ref[...] + (1.0 - ADAM_B1) * gv
        nv = ADAM_B2 * v_ref[...] + (1.0 - ADAM_B2) * (gv * gv)
        d_ref[...] = -ADAM_LR * ((nm / c1) / (jnp.sqrt(nv / c2) + ADAM_EPS) + ADAM_WD * w_ref[...])
        nm_ref[...] = nm
        nv_ref[...] = nv

    spec = _rows(cdim, tr)
    return pl.pallas_call(body, name=name, grid=(r // tr,), in_specs=[spec] * 4, out_specs=[spec] * 4,
                          out_shape=[jax.ShapeDtypeStruct((r, cdim), F32)] * 4,
                          compiler_params=_cparams("parallel"))(w, g, m, v)


def _adamw_whole(ws, gs, ms, vs, name):
    n = len(ws)
    c1 = 1.0 - ADAM_B1 ** ADAM_STEP
    c2 = 1.0 - ADAM_B2 ** ADAM_STEP

    def body(*refs):
        for i in range(n):
            w_ref, g_ref, m_ref, v_ref, d_ref, nm_ref, nv_ref = [refs[j * n + i] for j in range(7)]
            gv = g_ref[...]
            nm = ADAM_B1 * m_ref[...] + (1.0 - ADAM_B1) * gv
            nv = ADAM_B2 * v_ref[...] + (1.0 - ADAM_B2) * (gv * gv)
            d_ref[...] = -ADAM_LR * ((nm / c1) / (jnp.sqrt(nv / c2) + ADAM_EPS) + ADAM_WD * w_ref[...])
            nm_ref[...] = nm
            nv_ref[...] = nv

    vmem = pl.BlockSpec(memory_space=pltpu.VMEM)
    out = pl.pallas_call(body, name=name, in_specs=[vmem] * (4 * n), out_specs=[vmem] * (3 * n),
                         out_shape=[jax.ShapeDtypeStruct(a.shape, F32) for a in ws] * 3,
                         compiler_params=pltpu.CompilerParams(vmem_limit_bytes=VMEM_LIMIT))(*ws, *gs, *ms, *vs)
    return out[:n], out[n:2 * n], out[2 * n:]


SIDE_EFFECT = pltpu.SideEffectType.DATAFLOW_SIDE_EFFECTING


def _descriptors(copies, refs, send_sems, recv_sems, sem_off=0):
    x, y, c = lax.axis_index("x"), lax.axis_index("y"), lax.axis_index("c")
    pos = (x, y, c, 2 * x + y)
    out = []
    for i, (s, d, flip) in enumerate(copies):
        peer = (1 - x if "x" in flip else x, 1 - y if "y" in flip else y, 1 - c if "c" in flip else c)
        out.append(pltpu.make_async_remote_copy(
            src_ref=s(refs, refs, pos), dst_ref=d(refs, refs, pos),
            send_sem=send_sems.at[sem_off + i], recv_sem=recv_sems.at[sem_off + i],
            device_id=peer, device_id_type=MESH))
    return out


def _shifted(copies, off):
    return [(lambda I, O, pos, s=s: s(I[off:], O[off:], pos), lambda I, O, pos, d=d: d(I[off:], O[off:], pos), flip)
            for s, d, flip in copies]


BARRIER_IDS = {"c": (1, 2), "ici": (3, 4)}


def _exchange_start(name, bufs, copies, turns, after=None):
    n, nr = len(bufs), len(copies)
    na = 0 if after is None else 1
    flips = sorted({flip for _, _, flip in copies})
    kind = "c" if flips == ["c"] else "ici"
    collective_id = BARRIER_IDS[kind][turns[kind] % 2]
    turns[kind] += 1

    def body(*refs):
        x, y, c = lax.axis_index("x"), lax.axis_index("y"), lax.axis_index("c")
        barrier = pltpu.get_barrier_semaphore()
        for flip in flips:
            peer = (1 - x if "x" in flip else x, 1 - y if "y" in flip else y, 1 - c if "c" in flip else c)
            pl.semaphore_signal(barrier, inc=1, device_id=peer, device_id_type=MESH)
        pl.semaphore_wait(barrier, len(flips))
        for cp in _descriptors(copies, refs[:n], refs[n + na], refs[n + na + 1]):
            cp.start()
        token = refs[2 * n + na + 2]
        token[...] = jnp.zeros_like(token)

    hbm = pl.BlockSpec(memory_space=pltpu.HBM)
    sem = pl.BlockSpec(memory_space=pltpu.SEMAPHORE)
    out = pl.pallas_call(
        body, name=name,
        in_specs=[hbm] * n + [pl.BlockSpec(memory_space=pl.ANY)] * na,
        out_specs=(sem, sem, *[hbm] * n, pl.BlockSpec(memory_space=pltpu.VMEM)),
        out_shape=(pltpu.SemaphoreType.DMA((nr,)), pltpu.SemaphoreType.DMA((nr,)),
                   *[pltpu.HBM(b.shape, b.dtype) for b in bufs], jax.ShapeDtypeStruct((SUBLANES, LANES), F32)),
        input_output_aliases={i: 2 + i for i in range(n)},
        compiler_params=pltpu.CompilerParams(has_side_effects=SIDE_EFFECT, collective_id=collective_id),
    )(*[pltpu.with_memory_space_constraint(b, pltpu.HBM) for b in bufs], *([after] * na))
    return out[0], out[1], list(out[2:2 + n]), out[2 + n]


def _exchange_wait(name, send_sems, recv_sems, bufs, copies, after, sem_off=0):
    n = len(bufs)

    def body(*refs):
        for cp in _descriptors(copies, refs[:n], refs[n], refs[n + 1], sem_off):
            cp.wait_send()
            cp.wait_recv()

    hbm = pl.BlockSpec(memory_space=pltpu.HBM)
    sem = pl.BlockSpec(memory_space=pltpu.SEMAPHORE)
    out = pl.pallas_call(
        body, name=name,
        in_specs=[hbm] * n + [sem, sem, pl.BlockSpec(memory_space=pl.ANY)],
        out_specs=tuple([hbm] * n),
        out_shape=tuple(pltpu.HBM(b.shape, b.dtype) for b in bufs),
        input_output_aliases={i: i for i in range(n)},
        compiler_params=pltpu.CompilerParams(has_side_effects=SIDE_EFFECT),
    )(*bufs, send_sems, recv_sems, after)
    return list(out)


FIRST = ("w_in",)
MID = ("ssm_w_glu", "w_out")
LATE = ("w_up", "w_down")
GROUPS = {"first": FIRST, "mid": MID, "late": LATE}
ARRIVALS = {"first": FIRST, "mid": MID, "up": ("w_up",), "down": ("w_down",)}


def _gather_copies(names, shard_shapes):
    def region(i, chip, c):
        half_axis, shard_axis = BIG[names[i]]
        ssize = shard_shapes[i][shard_axis]
        hsize = shard_shapes[i][half_axis] // 2
        return lambda ref: _view(_view(ref, shard_axis, chip * ssize, ssize), half_axis, c * hsize, hsize)

    ici, d2d = [], []
    for i in range(len(names)):
        for flip in FLIPS:
            ici.append((lambda I, O, pos, i=i: region(i, pos[3], pos[2])(I[i]),
                        lambda I, O, pos, i=i: region(i, pos[3], pos[2])(O[i]), flip))
            d2d.append((lambda I, O, pos, i=i, flip=flip: region(i, _peer_chip(pos, flip), pos[2])(I[i]),
                        lambda I, O, pos, i=i, flip=flip: region(i, _peer_chip(pos, flip), pos[2])(O[i]), "c"))
    return ici, d2d


def _half_shape(n, shape):
    r, cdim = shape
    return (r // 2, cdim) if BIG[n][0] == 0 else (r, cdim // 2)


def _sub_shape(n, shape):
    hr, hc = _half_shape(n, shape)
    return (hr, hc // 4) if BIG[n][1] == 1 else (hr // 4, hc)


def _pair_copies(names, shapes, with_pack, dst_off):
    n = len(names)

    def other_half(i, ref, pos):
        half_axis = BIG[names[i]][0]
        hsize = shapes[i][half_axis] // 2
        return _view(ref, half_axis, (1 - pos[2]) * hsize, hsize)

    copies = [(lambda I, O, pos, i=i: other_half(i, I[i], pos), lambda I, O, pos, i=i: O[dst_off + i], "c")
              for i in range(n)]
    if with_pack:
        copies.append((lambda I, O, pos: I[n], lambda I, O, pos: O[dst_off + n], "c"))
    return copies


def _chip_copies(names, shapes, pack_rows, dst_off):
    n = len(names)

    def piece(i, ref, chip):
        shard_axis = BIG[names[i]][1]
        ssize = _sub_shape(names[i], shapes[i])[shard_axis]
        return _view(ref, shard_axis, chip * ssize, ssize)

    copies = []
    for i in range(n):
        for slot, flip in enumerate(FLIPS):
            copies.append((lambda I, O, pos, i=i, flip=flip: piece(i, I[i], _peer_chip(pos, flip)),
                           lambda I, O, pos, i=i, slot=slot: O[dst_off + i].at[slot], flip))
    if pack_rows:
        for slot, flip in enumerate(FLIPS):
            copies.append((lambda I, O, pos: _view(I[n], 0, pos[2] * (pack_rows // 2), pack_rows // 2),
                           lambda I, O, pos, slot=slot: O[dst_off + n].at[slot], flip))
    return copies


class _Exchanges:
    def __init__(self, shards, tiny, kc):
        self.kc = kc
        wb = {n: _cast_into_full(shards[n], kc, BIG[n][1], "cast_" + n) for n in BIG_NAMES}
        self.gathering, self.forwarding, self.pairing, self.reducing = {}, {}, {}, {}
        self.turns = {"c": 0, "ici": 0}
        tiny_copies = [(lambda I, O, pos: I[0], lambda I, O, pos: O[1].at[pos[3]], flip) for flip in FLIPS]
        self.gathering["tiny"] = (0, 0, 2, tiny_copies, None)
        bufs, copies = [tiny, lax.empty((4,) + tiny.shape, F32)], list(tiny_copies)
        for group, names in ARRIVALS.items():
            ici, d2d = _gather_copies(names, [shards[n].shape for n in names])
            self.gathering[group] = (len(bufs), len(copies), len(names), ici, d2d)
            copies += _shifted(ici, len(bufs))
            bufs += [wb[n] for n in names]
        self.started = _exchange_start("gather_start", bufs, copies, self.turns)
        self.zero = self.started[3][0, 0]

    def _arrived(self, group, after):
        buf_off, sem_off, n, ici, _ = self.gathering[group]
        send_sems, recv_sems, bufs, _ = self.started
        return _exchange_wait("gather_%s_wait" % group, send_sems, recv_sems, bufs[buf_off:buf_off + n], ici, after,
                              sem_off)

    def small_params(self, kc):
        tiny, got = self._arrived("tiny", self.started[3])
        return lax.dynamic_update_index_in_dim(got, tiny, kc[0], 0)

    def forward(self, group, after):
        d2d = self.gathering[group][4]
        self.forwarding[group] = (_exchange_start("forward_%s_start" % group, self._arrived(group, after), d2d,
                                                  self.turns), d2d)
        return self.forwarding[group][0][3]

    def weights(self, group, after):
        if group not in self.forwarding:
            after = self.forward(group, after)
        (send_sems, recv_sems, bufs, _), d2d = self.forwarding[group]
        full = _exchange_wait("forward_%s_wait" % group, send_sems, recv_sems, bufs, d2d, after)
        return dict(zip(ARRIVALS[group], full))

    def grads_ready(self, group, grads):
        names = GROUPS[group]
        gs = [grads[n] for n in names]
        land = [lax.empty(_half_shape(n, g.shape), F32) for n, g in zip(names, gs)]
        copies = _pair_copies(names, [g.shape for g in gs], False, len(names))
        started = _exchange_start("pair_%s_start" % group, gs + land, copies, self.turns)
        self.pairing[group] = (started, copies)
        return started[3]

    def grads_send(self, group, after):
        names = GROUPS[group]
        n = len(names)
        (send_sems, recv_sems, bufs, _), copies = self.pairing[group]
        bufs = _exchange_wait("pair_%s_wait" % group, send_sems, recv_sems, bufs, copies, after)
        chip = [_pair_sum(bufs[i], bufs[n + i], self.kc, BIG[names[i]][0], "pair_sum_" + names[i], BF16)
                for i in range(n)]
        shapes = [bufs[i].shape for i in range(n)]
        land = [lax.empty((3,) + _sub_shape(names[i], shapes[i]), BF16) for i in range(n)]
        copies = _chip_copies(names, shapes, 0, n)
        started = _exchange_start("reduce_%s_start" % group, chip + land, copies, self.turns)
        self.reducing[group] = (started, copies)
        return started[3]

    def finish_pack(self, pack):
        kc = self.kc
        prow = pack.shape[0] // 2
        recv = _exchange("reduce_d2d", [pack], [jax.ShapeDtypeStruct(pack.shape, F32)], {}, [],
                         _pair_copies((), [], True, 0))
        chip_pack = _pair_sum(pack, recv[0], kc, None, "pair_sum_pack", F32)
        copies = _chip_copies((), [], pack.shape[0], 1)
        land = lax.empty((3, prow, pack.shape[1]), F32)
        pack_sems_s, pack_sems_r, pack_bufs, after = _exchange_start("reduce_pack_start", [chip_pack, land], copies,
                                                                     self.turns)

        names, chips, recvs = (), [], []
        for group, group_names in GROUPS.items():
            (send_sems, recv_sems, bufs, _), group_copies = self.reducing[group]
            bufs = _exchange_wait("reduce_%s_wait" % group, send_sems, recv_sems, bufs, group_copies, after)
            n = len(group_names)
            names, chips, recvs = names + group_names, chips + bufs[:n], recvs + bufs[n:]
            after = bufs[n]
        total = [_chip_sum(chips[i], recvs[i], kc, BIG[n][1], BIG[n][0], "chip_sum_" + n)
                 for i, n in enumerate(names)]

        def my_half(half_axis, ref, pos):
            hsize = ref.shape[half_axis] // 2
            return _view(ref, half_axis, pos[2] * hsize, hsize)

        swap = [(lambda I, O, pos, i=i, n=n: my_half(BIG[n][0], I[i], pos),
                 lambda I, O, pos, i=i, n=n: my_half(BIG[n][0], O[i], pos), "c") for i, n in enumerate(names)]
        self.swapping = (_exchange_start("swap_start", total, swap, self.turns), swap, names)

        chip_pack, recv_pack = _exchange_wait("reduce_pack_wait", pack_sems_s, pack_sems_r, pack_bufs, copies,
                                              self.swapping[0][3])
        total_pack = _chip_sum(chip_pack, recv_pack, kc, None, 0, "chip_sum_pack")
        swap = [(lambda I, O, pos: my_half(0, I[0], pos), lambda I, O, pos: my_half(0, O[0], pos), "c")]
        return _exchange("swap_pack", [total_pack], [jax.ShapeDtypeStruct(pack.shape, F32)], {0: 0}, [], swap)[0]

    def finish_big(self, after):
        (send_sems, recv_sems, bufs, _), swap, names = self.swapping
        return dict(zip(names, _exchange_wait("swap_wait", send_sems, recv_sems, bufs, swap, after)))


WEIGHTS = ("meta_tokens", "norm_mix_g", "w_in", "conv_w", "ssm_lam_re", "ssm_lam_im", "ssm_log_dt", "ssm_b_re",
           "ssm_b_im", "ssm_c_re", "ssm_c_im", "ssm_d", "ssm_w_glu", "gain_conv_out", "gain_ssm_out", "w_out",
           "norm_ffn_g", "w_up", "ffn_conv_w", "ffn_conv_b", "w_down", "norm_final_g")
TINY_SHARDED = ("meta_tokens", "conv_w", "ffn_conv_w")
REPLICATED = tuple(n for n in WEIGHTS if n not in BIG and n not in TINY_SHARDED)
PACK_COLS = 512


def _pack(arrays, row_mult, cols):
    flat = jnp.concatenate([a.reshape(-1).astype(F32) for a in arrays])
    n = flat.shape[0]
    total = -(-n // (row_mult * cols)) * (row_mult * cols)
    return jnp.concatenate([flat, jnp.zeros((total - n,), F32)]).reshape(total // cols, cols)


def _unpack(packed, shapes):
    flat = packed.reshape(-1)
    out, off = [], 0
    for s in shapes:
        n = math.prod(s)
        out.append(flat[off:off + n].reshape(s))
        off += n
    return out


def kernel(x, meta_tokens, norm_mix_g, w_in, conv_w, ssm_lam_re, ssm_lam_im, ssm_log_dt, ssm_b_re, ssm_b_im, ssm_c_re, ssm_c_im, ssm_d, ssm_w_glu, gain_conv_out, gain_ssm_out, w_out, norm_ffn_g, w_up, ffn_conv_w, ffn_conv_b, w_down, norm_final_g, loss_target, m_meta_tokens, m_norm_mix_g, m_w_in, m_conv_w, m_ssm_lam_re, m_ssm_lam_im, m_ssm_log_dt, m_ssm_b_re, m_ssm_b_im, m_ssm_c_re, m_ssm_c_im, m_ssm_d, m_ssm_w_glu, m_gain_conv_out, m_gain_ssm_out, m_w_out, m_norm_ffn_g, m_w_up, m_ffn_conv_w, m_ffn_conv_b, m_w_down, m_norm_final_g, v_meta_tokens, v_norm_mix_g, v_w_in, v_conv_w, v_ssm_lam_re, v_ssm_lam_im, v_ssm_log_dt, v_ssm_b_re, v_ssm_b_im, v_ssm_c_re, v_ssm_c_im, v_ssm_d, v_ssm_w_glu, v_gain_conv_out, v_gain_ssm_out, v_w_out, v_norm_ffn_g, v_w_up, v_ffn_conv_w, v_ffn_conv_b, v_w_down, v_norm_final_g):
    args = dict(locals())
    w = {n: args[n] for n in WEIGHTS}
    mom = {n: args["m_" + n] for n in WEIGHTS}
    var = {n: args["v_" + n] for n in WEIGHTS}
    kx, ky, kc_ = lax.axis_index("x"), lax.axis_index("y"), lax.axis_index("c")
    chip = 2 * kx + ky
    kc = jnp.stack([chip, kc_]).astype(jnp.int32)

    def squeeze(n, a):
        if n == "meta_tokens":
            return a
        if n == "norm_final_g":
            return a.reshape(1, -1)
        a = a[0]
        return a.reshape(1, -1) if a.ndim == 1 else a

    wl = {n: squeeze(n, w[n]) for n in WEIGHTS}
    ml = {n: squeeze(n, mom[n]) for n in WEIGHTS}
    vl = {n: squeeze(n, var[n]) for n in WEIGHTS}

    tiny = _pack([wl[n] for n in TINY_SHARDED], SUBLANES, LANES)
    ex = _Exchanges({n: wl[n] for n in BIG_NAMES}, tiny, kc)
    tiny_shapes = [wl[n].shape for n in TINY_SHARDED]
    tiny_all = ex.small_params(kc)
    tiny_parts = [_unpack(tiny_all[k], tiny_shapes) for k in range(4)]
    p = {n: wl[n] for n in WEIGHTS if n not in BIG}
    for j, n in enumerate(TINY_SHARDED):
        p[n] = jnp.concatenate([tiny_parts[k][j] for k in range(4)], axis=1)
    p["ssm_log_dt"] = wl["ssm_log_dt"].reshape(-1)

    loss_local, grad_x, grads = _local_step(x[0], loss_target[0], p, ex)

    small_names = REPLICATED + TINY_SHARDED
    small_shapes = [tuple(grads[n].shape) for n in small_names] + [(1,)]
    pack = _pack([grads[n] for n in small_names] + [loss_local.reshape(1)], 2 * 16, PACK_COLS)
    g_pack = ex.finish_pack(pack)
    g_small = dict(zip(small_names + ("loss",), _unpack(g_pack, small_shapes)))
    loss = g_small["loss"][0]
    swapped = ("ssm_b_re", "ssm_b_im")

    def view(n, a):
        if n in swapped:
            return jnp.swapaxes(a, -1, -2)
        return a.reshape(1, -1) if a.ndim == 1 else a

    g = {}
    for n in REPLICATED:
        g[n] = g_small[n].reshape(view(n, w[n]).shape)
    for n in TINY_SHARDED:
        cols = wl[n].shape[1]
        g[n] = lax.dynamic_slice_in_dim(g_small[n], chip * cols, cols, axis=1).reshape(w[n].shape)
    delta, new_m, new_v = {}, {}, {}
    small = [[view(n, d[n]) for n in small_names] for d in (w, mom, var)]
    small.insert(1, [g[n] for n in small_names])
    for d, outs in zip((delta, new_m, new_v), _adamw_whole(*small, "adamw_small")):
        d.update(zip(small_names, outs))
    for d in (g, delta, new_m, new_v):
        d.update({n: jnp.swapaxes(d[n], -1, -2) for n in swapped})
    g_big = ex.finish_big(delta[small_names[0]])
    for n in BIG_NAMES:
        g[n], delta[n], new_m[n], new_v[n] = _adamw(wl[n], g_big[n], ml[n], vl[n], "adamw_" + n)

    def like(n, a):
        return a.reshape(w[n].shape)

    return (loss, grad_x[None], *[like(n, g[n]) for n in WEIGHTS], *[like(n, delta[n]) for n in WEIGHTS],
            *[like(n, new_m[n]) for n in WEIGHTS], *[like(n, new_v[n]) for n in WEIGHTS])
```

```python
import functools
import math

import jax
import jax.numpy as jnp
from jax import lax
from jax.experimental import pallas as pl
from jax.experimental.pallas import tpu as pltpu

F32 = jnp.float32
BF16 = jnp.bfloat16
MESH = pl.DeviceIdType.MESH

N_META = 16
N_GROUPS = 32
GROUP = 16
STATE = 64
RMS_EPS = 1e-6
ADAM_LR = 0.001
ADAM_B1 = 0.9
ADAM_B2 = 0.999
ADAM_EPS = 1e-08
ADAM_WD = 0.01
ADAM_STEP = 10

LANES = 128
SUBLANES = 8
ROW_ALIGN = 128
ROW_TILES = 4
VMEM_LIMIT = 52 * 1024 * 1024
MM_VMEM_BUDGET = 40 * 1024 * 1024
GELU_C = math.sqrt(2.0 / math.pi)
GELU_A = 0.044715


def _cparams(*sem):
    return pltpu.CompilerParams(dimension_semantics=sem, vmem_limit_bytes=VMEM_LIMIT)


def _pick_tile(dim, cap, mult):
    best = None
    for t in range(mult, min(dim, cap) + 1, mult):
        if dim % t == 0:
            best = t
    return best if best is not None else dim


def _mm(a, b, mode, name, out_dtype=F32, acc_in=None, after=None):
    if mode == "tn":
        kdim, m = a.shape
    else:
        m, kdim = a.shape
    n = b.shape[0] if mode == "nt" else b.shape[1]
    tm = _pick_tile(m, 1408, LANES if mode == "tn" else 16)
    tk = _pick_tile(kdim, 2816, LANES)
    nk = kdim // tk
    out_bytes = jnp.dtype(out_dtype).itemsize
    for cap in (1408, 1024, 512, 256, LANES):
        tn = _pick_tile(n, cap, LANES)
        blocks = 2 * (tm * tk * 2 + tk * tn * 2 + tm * tn * out_bytes * (2 if acc_in is not None else 1))
        if blocks + (tm * tn * 4 if nk > 1 else 0) <= MM_VMEM_BUDGET:
            break
    has_acc = acc_in is not None

    def body(*refs):
        if after is not None:
            refs = refs[1:]
        if has_acc:
            a_ref, b_ref, c_ref, o_ref = refs[:4]
            rest = refs[4:]
        else:
            a_ref, b_ref, o_ref = refs[:3]
            c_ref = None
            rest = refs[3:]
        if mode == "nn":
            p = jnp.dot(a_ref[...], b_ref[...], preferred_element_type=F32)
        elif mode == "nt":
            p = lax.dot_general(a_ref[...], b_ref[...], (((1,), (1,)), ((), ())), preferred_element_type=F32)
        else:
            p = lax.dot_general(a_ref[...], b_ref[...], (((0,), (0,)), ((), ())), preferred_element_type=F32)
        if nk == 1:
            if has_acc:
                p = p + c_ref[...]
            o_ref[...] = p.astype(out_dtype)
        else:
            acc_ref = rest[0]
            k = pl.program_id(2)

            @pl.when(k == 0)
            def _():
                acc_ref[...] = p + c_ref[...] if has_acc else p

            @pl.when(k > 0)
            def _():
                acc_ref[...] += p

            @pl.when(k == nk - 1)
            def _():
                o_ref[...] = acc_ref[...].astype(out_dtype)

    if mode == "tn":
        a_spec = pl.BlockSpec((tk, tm), lambda i, j, k: (k, i))
    else:
        a_spec = pl.BlockSpec((tm, tk), lambda i, j, k: (i, k))
    if mode == "nt":
        b_spec = pl.BlockSpec((tn, tk), lambda i, j, k: (j, k))
    else:
        b_spec = pl.BlockSpec((tk, tn), lambda i, j, k: (k, j))
    o_spec = pl.BlockSpec((tm, tn), lambda i, j, k: (i, j))
    in_specs = [a_spec, b_spec] + ([o_spec] if has_acc else [])
    args = (a, b) + ((acc_in,) if has_acc else ())
    if after is not None:
        in_specs = [pl.BlockSpec(memory_space=pl.ANY)] + in_specs
        args = (after,) + args
    return pl.pallas_call(
        body, name=name, grid=(m // tm, n // tn, nk),
        in_specs=in_specs, out_specs=o_spec,
        out_shape=jax.ShapeDtypeStruct((m, n), out_dtype),
        scratch_shapes=[pltpu.VMEM((tm, tn), F32)] if nk > 1 else [],
        compiler_params=_cparams("parallel", "parallel", "arbitrary"),
    )(*args)


def _mm_rows(a, b, mode, name, ins, outs, epilogue, scratch=()):
    m, kdim = a.shape
    n = b.shape[0] if mode == "nt" else b.shape[1]
    tm = m // ROW_TILES
    tk = _pick_tile(kdim, 2816, LANES)
    nk = kdim // tk
    ni, no = len(ins), len(outs)

    def body(*refs):
        a_ref, b_ref = refs[:2]
        in_refs, out_refs, rest = refs[2:2 + ni], refs[2 + ni:2 + ni + no], refs[2 + ni + no:]
        i = pl.program_id(0)
        if mode == "nn":
            p = jnp.dot(a_ref[...], b_ref[...], preferred_element_type=F32)
        else:
            p = lax.dot_general(a_ref[...], b_ref[...], (((1,), (1,)), ((), ())), preferred_element_type=F32)
        if nk == 1:
            epilogue(p, i, in_refs, out_refs, rest)
        else:
            acc_ref = rest[0]
            k = pl.program_id(1)

            @pl.when(k == 0)
            def _():
                acc_ref[...] = p

            @pl.when(k > 0)
            def _():
                acc_ref[...] += p

            @pl.when(k == nk - 1)
            def _():
                epilogue(acc_ref[...], i, in_refs, out_refs, rest[1:])

    def spec(shape, kind):
        if kind == "rows":
            return pl.BlockSpec((tm,) + tuple(shape[1:]), lambda i, k: (i,) + (0,) * (len(shape) - 1))
        if kind == "whole":
            return pl.BlockSpec(tuple(shape), lambda i, k: (0,) * len(shape))
        return pl.BlockSpec(memory_space=pl.ANY)

    a_spec = pl.BlockSpec((tm, tk), lambda i, k: (i, k))
    b_spec = pl.BlockSpec((n, tk), lambda i, k: (0, k)) if mode == "nt" else pl.BlockSpec((tk, n), lambda i, k: (k, 0))
    return pl.pallas_call(
        body, name=name, grid=(ROW_TILES, nk),
        in_specs=[a_spec, b_spec] + [spec(x.shape, kind) for x, kind in ins],
        out_specs=[spec(shape, kind) for shape, _, kind in outs],
        out_shape=[jax.ShapeDtypeStruct(shape, dtype) for shape, dtype, _ in outs],
        scratch_shapes=([pltpu.VMEM((tm, n), F32)] if nk > 1 else []) + list(scratch),
        compiler_params=_cparams("arbitrary", "arbitrary"),
    )(a, b, *[x for x, _ in ins])


def _rows(shape_cols, tr, dtype=None):
    return pl.BlockSpec((tr, shape_cols), lambda i: (i, 0))


def _const(shape):
    return pl.BlockSpec(shape, lambda i: (0,) * len(shape))


def _rms(x):
    return lax.rsqrt(jnp.mean(x * x, axis=-1, keepdims=True) + RMS_EPS)


def _rms_bwd(x, r, g, dy):
    xn = x * r
    dxn = dy * g
    dx = r * (dxn - xn * jnp.mean(dxn * xn, axis=-1, keepdims=True))
    return dx, dy * xn


def _gelu(y):
    return 0.5 * y * (1.0 + jnp.tanh(GELU_C * (y + GELU_A * y * y * y)))


def _gelu_grad(y):
    t = jnp.tanh(GELU_C * (y + GELU_A * y * y * y))
    return 0.5 * (1.0 + t) + 0.5 * y * (1.0 - t * t) * GELU_C * (1.0 + 3.0 * GELU_A * y * y)


def _sigmoid(z):
    return 1.0 / (1.0 + jnp.exp(-z))


def _proj_res_norm(a, w, h, g, after, name):
    def epilogue(p, i, ins, outs, _):
        x = ins[0][...] + p
        outs[0][...] = x
        outs[1][...] = (x * _rms(x) * ins[1][...]).astype(BF16)

    return _mm_rows(a, w, "nn", name, [(h, "rows"), (g, "whole"), (after, "hbm")],
                    [(h.shape, F32, "rows"), (h.shape, BF16, "rows")], epilogue)


def _proj_norm_bwd(da, w, h, g, dres, after, name):
    d = h.shape[1]

    def epilogue(p, i, ins, outs, _):
        x = ins[0][...]
        dx, dgs = _rms_bwd(x, _rms(x), ins[1][...], p)
        dh = ins[2][...] + dx
        outs[0][...] = dh
        outs[1][...] = dh.astype(BF16)

        @pl.when(i == 0)
        def _():
            outs[2][...] = jnp.zeros_like(outs[2])

        outs[2][...] += jnp.sum(dgs, axis=0, keepdims=True)

    return _mm_rows(da, w, "nt", name, [(h, "rows"), (g, "whole"), (dres, "rows"), (after, "hbm")],
                    [(h.shape, F32, "rows"), (h.shape, BF16, "rows"), ((1, d), F32, "whole")], epilogue)


def _proj_input_norm_bwd(da, w, h, g, dres, after, n_real, name):
    tp, d = h.shape
    tr = tp // ROW_TILES

    def epilogue(p, i, ins, outs, scratch):
        h_ref, g_ref, dres_ref, _ = ins
        dx_ref, dmeta_ref, dg_ref = outs
        stage, sem = scratch
        x = h_ref[...]
        dx, dgs = _rms_bwd(x, _rms(x), g_ref[...], p)
        stage[...] = dres_ref[...] + dx

        @pl.when(i == 0)
        def _():
            dg_ref[...] = jnp.zeros_like(dg_ref)
            dmeta_ref[...] = stage[:N_META, :]

        dg_ref[...] += jnp.sum(dgs, axis=0, keepdims=True)
        for t in range(ROW_TILES):
            lo, hi = max(t * tr, N_META), min((t + 1) * tr, n_real)
            if hi > lo:
                @pl.when(i == t)
                def _(t=t, lo=lo, hi=hi):
                    cp = pltpu.make_async_copy(stage.at[pl.ds(lo - t * tr, hi - lo), :],
                                               dx_ref.at[pl.ds(lo - N_META, hi - lo), :], sem)
                    cp.start()
                    cp.wait()

    return _mm_rows(da, w, "nt", name, [(h, "rows"), (g, "whole"), (dres, "rows"), (after, "hbm")],
                    [((n_real - N_META, d), F32, "hbm"), ((N_META, d), F32, "whole"), ((1, d), F32, "whole")],
                    epilogue, scratch=[pltpu.VMEM((tr, d), F32), pltpu.SemaphoreType.DMA])


def _load_token_rows(tok_hbm, buf, sem, tr, n_real, head=None, wait=False, i=None):
    i = pl.program_id(0) if i is None else i
    for t in range(ROW_TILES):
        base = t * tr
        lo, hi = max(base, N_META), min(base + tr, n_real)

        @pl.when(i == t)
        def _(base=base, lo=lo, hi=hi):
            if hi > lo:
                cp = pltpu.make_async_copy(tok_hbm.at[pl.ds(lo - N_META, hi - lo), :],
                                           buf.at[pl.ds(lo - base, hi - lo), :], sem)
                if wait:
                    cp.wait()
                    return
                cp.start()
            if wait:
                return
            if base < N_META:
                buf[0:N_META - base, :] = (jnp.zeros((N_META - base, buf.shape[1]), F32) if head is None
                                           else head[base:N_META, :])
            if hi < base + tr:
                buf[max(hi, base) - base:tr, :] = jnp.zeros((base + tr - max(hi, base), buf.shape[1]), F32)


def _input_norm_fwd(x, meta, g, tp, name):
    seq, d = x.shape
    tr = tp // ROW_TILES
    n_real = N_META + seq

    def body(x_hbm, meta_ref, g_ref, h_ref, hn_ref, buf, sem):
        _load_token_rows(x_hbm, buf, sem, tr, n_real, head=meta_ref)
        _load_token_rows(x_hbm, buf, sem, tr, n_real, wait=True)
        h = buf[...]
        h_ref[...] = h
        hn_ref[...] = (h * _rms(h) * g_ref[...]).astype(BF16)

    return pl.pallas_call(
        body, name=name, grid=(ROW_TILES,),
        in_specs=[pl.BlockSpec(memory_space=pl.ANY), _const((N_META, d)), _const((1, d))],
        out_specs=[_rows(d, tr), _rows(d, tr)],
        out_shape=[jax.ShapeDtypeStruct((tp, d), F32), jax.ShapeDtypeStruct((tp, d), BF16)],
        scratch_shapes=[pltpu.VMEM((tr, d), F32), pltpu.SemaphoreType.DMA],
        compiler_params=_cparams("arbitrary"))(x, meta, g)


def _proj_loss_bwd(act, w, h1, target, g, n_real, name):
    tp, d = h1.shape
    tr = tp // ROW_TILES

    def epilogue(p, i, ins, outs, scratch):
        h1_ref, t_hbm, g_ref = ins
        loss_ref, dh_ref, dhb_ref, dg_ref = outs
        t_buf, sem = scratch
        _load_token_rows(t_hbm, t_buf, sem, tr, n_real, i=i)
        x = h1_ref[...] + p
        r = _rms(x)
        row = i * tr + lax.broadcasted_iota(jnp.int32, (tr, d), 0)
        valid = (row >= N_META) & (row < n_real)
        _load_token_rows(t_hbm, t_buf, sem, tr, n_real, wait=True, i=i)
        e = jnp.where(valid, x * r * g_ref[...] - t_buf[...], 0.0)
        dx, dgs = _rms_bwd(x, r, g_ref[...], e * (1.0 / d))
        dh_ref[...] = dx
        dhb_ref[...] = dx.astype(BF16)

        @pl.when(i == 0)
        def _():
            dg_ref[...] = jnp.zeros_like(dg_ref)
            loss_ref[...] = jnp.zeros_like(loss_ref)

        dg_ref[...] += jnp.sum(dgs, axis=0, keepdims=True)
        loss_ref[...] += (0.5 / d) * jnp.sum(jnp.sum(e * e, axis=0, keepdims=True), axis=1, keepdims=True)

    return _mm_rows(act, w, "nn", name, [(h1, "rows"), (target, "hbm"), (g, "whole")],
                    [((1, LANES), F32, "whole"), ((tp, d), F32, "rows"), ((tp, d), BF16, "rows"),
                     ((1, d), F32, "whole")],
                    epilogue, scratch=[pltpu.VMEM((tr, d), F32), pltpu.SemaphoreType.DMA])


def _mix_fwd(co, y, z, gc, gs, name):
    tp, dh = co.shape
    tr = tp // ROW_TILES

    def body(co_ref, y_ref, z_ref, gc_ref, gs_ref, m_ref):
        c = co_ref[...]
        m_ref[:, :dh] = (c * _rms(c) * gc_ref[...]).astype(BF16)
        so = _gelu(y_ref[...]) * _sigmoid(z_ref[...])
        m_ref[:, dh:] = (so * _rms(so) * gs_ref[...]).astype(BF16)

    return pl.pallas_call(
        body, name=name, grid=(ROW_TILES,),
        in_specs=[_rows(dh, tr)] * 3 + [_const((1, dh))] * 2,
        out_specs=_rows(2 * dh, tr),
        out_shape=jax.ShapeDtypeStruct((tp, 2 * dh), BF16),
        compiler_params=_cparams("parallel"))(co, y, z, gc, gs)


def _proj_mix_bwd(dh1b, w, co, y, z, gc, gs, name):
    tp, dh = co.shape

    def epilogue(p, i, ins, outs, _):
        co_ref, y_ref, z_ref, gc_ref, gs_ref = ins
        dco_ref, dz_ref, dgp_ref, dgc_ref, dgs_ref = outs
        c = co_ref[...]
        dco, dgc = _rms_bwd(c, _rms(c), gc_ref[...], p[:, :dh])
        dco_ref[...] = dco
        gl = _gelu(y_ref[...])
        sg = _sigmoid(z_ref[...])
        so = gl * sg
        dso, dgs = _rms_bwd(so, _rms(so), gs_ref[...], p[:, dh:])
        dz_ref[...] = (dso * gl * sg * (1.0 - sg)).astype(BF16)
        dgp_ref[...] = dso * sg

        @pl.when(i == 0)
        def _():
            dgc_ref[...] = jnp.zeros_like(dgc_ref)
            dgs_ref[...] = jnp.zeros_like(dgs_ref)

        dgc_ref[...] += jnp.sum(dgc, axis=0, keepdims=True)
        dgs_ref[...] += jnp.sum(dgs, axis=0, keepdims=True)

    return _mm_rows(dh1b, w, "nt", name,
                    [(co, "rows"), (y, "rows"), (z, "rows"), (gc, "whole"), (gs, "whole")],
                    [((tp, dh), F32, "rows"), ((tp, dh), BF16, "rows"), ((tp, dh), F32, "rows"),
                     ((1, dh), F32, "whole"), ((1, dh), F32, "whole")], epilogue)


def _shift_down(x, k):
    row = lax.broadcasted_iota(jnp.int32, x.shape, 0)
    return jnp.where(row >= k, pltpu.roll(x, k, 0), 0.0)


def _shift_up(x, k):
    n = x.shape[0]
    row = lax.broadcasted_iota(jnp.int32, x.shape, 0)
    return jnp.where(row < n - k, pltpu.roll(x, n - k, 0), 0.0)


def _dwconv(x, w_ref):
    return w_ref[2:3, :] * x + w_ref[1:2, :] * _shift_down(x, 1) + w_ref[0:1, :] * _shift_down(x, 2)


def _dwconv_bwd(x, dy, w_ref):
    dx = w_ref[2:3, :] * dy + w_ref[1:2, :] * _shift_up(dy, 1) + w_ref[0:1, :] * _shift_up(dy, 2)
    dw = jnp.concatenate([jnp.sum(dy * _shift_down(x, 2), axis=0, keepdims=True),
                          jnp.sum(dy * _shift_down(x, 1), axis=0, keepdims=True),
                          jnp.sum(dy * x, axis=0, keepdims=True)], axis=0)
    return dx, dw


def _interleave(dst, src):
    seg_rows = src.shape[0] // SUBLANES
    for seg in range(SUBLANES):
        dst[pl.ds(seg, seg_rows, stride=SUBLANES), :] = src[seg * seg_rows:(seg + 1) * seg_rows, :]


def _deinterleave(dst, src):
    seg_rows = src.shape[0] // SUBLANES
    for seg in range(SUBLANES):
        dst[seg * seg_rows:(seg + 1) * seg_rows, :] = src[pl.ds(seg, seg_rows, stride=SUBLANES), :]


def _segment_shift(x, reverse):
    row = lax.broadcasted_iota(jnp.int32, x.shape, 0)
    if reverse:
        return jnp.where(row < SUBLANES - 1, pltpu.roll(x, SUBLANES - 1, 0), 0.0)
    return jnp.where(row >= 1, pltpu.roll(x, 1, 0), 0.0)


def _scan(s_re, s_im, pw_ref, reverse, pair=None):
    n_steps = s_re.shape[0] // SUBLANES
    n_strips = s_re.shape[1] // LANES
    sign = -1.0 if reverse else 1.0
    strips = [slice(st * LANES, (st + 1) * LANES) for st in range(n_strips)]

    def rows_of(j):
        step = (n_steps - 1 - j) if reverse else j
        return pl.ds(pl.multiple_of(step * SUBLANES, SUBLANES), SUBLANES)

    a = [(jnp.broadcast_to(pw_ref[0, 0:1, lanes], (SUBLANES, LANES)),
          sign * jnp.broadcast_to(pw_ref[1, 0:1, lanes], (SUBLANES, LANES))) for lanes in strips]

    def local(i, carry):
        for half in range(2):
            rows = rows_of(2 * i + half)
            out = []
            for st, lanes in enumerate(strips):
                (ar, ai), cr, ci = a[st], carry[2 * st], carry[2 * st + 1]
                xr = s_re[rows, lanes] + (ar * cr - ai * ci)
                xi = s_im[rows, lanes] + (ar * ci + ai * cr)
                s_re[rows, lanes] = xr
                s_im[rows, lanes] = xi
                out += [xr, xi]
            carry = tuple(out)
        return carry

    zero = jnp.zeros((SUBLANES, LANES), F32)
    ends = lax.fori_loop(0, n_steps // 2, local, (zero,) * (2 * n_strips))

    entering = []
    row = lax.broadcasted_iota(jnp.int32, (SUBLANES, LANES), 0)
    for st, lanes in enumerate(strips):
        tr, ti = ends[2 * st], ends[2 * st + 1]
        mr = jnp.broadcast_to(pw_ref[0, n_steps - 1:n_steps, lanes], (SUBLANES, LANES))
        mi = sign * jnp.broadcast_to(pw_ref[1, n_steps - 1:n_steps, lanes], (SUBLANES, LANES))
        for k in (1, 2, 4):
            keep = (row < SUBLANES - k) if reverse else (row >= k)
            rr = jnp.where(keep, pltpu.roll(tr, SUBLANES - k if reverse else k, 0), 0.0)
            ri = jnp.where(keep, pltpu.roll(ti, SUBLANES - k if reverse else k, 0), 0.0)
            tr, ti = tr + (mr * rr - mi * ri), ti + (mr * ri + mi * rr)
            mr, mi = mr * mr - mi * mi, 2.0 * mr * mi
        entering += [_segment_shift(tr, reverse), _segment_shift(ti, reverse)]

    def fix(i, carry):
        carry, sums = carry[:2 * n_strips], carry[2 * n_strips:]
        for half in range(2):
            j = 2 * i + half
            rows = rows_of(j)
            out, acc = [], []
            for st, lanes in enumerate(strips):
                (ar, ai), cr, ci = a[st], carry[2 * st], carry[2 * st + 1]
                cr, ci = ar * cr - ai * ci, ar * ci + ai * cr
                xr = s_re[rows, lanes] + cr
                xi = s_im[rows, lanes] + ci
                s_re[rows, lanes] = xr
                s_im[rows, lanes] = xi
                out += [cr, ci]
                if pair is not None:
                    p_rows = rows_of(jnp.minimum(j + 1, n_steps - 1))
                    keep = (j < n_steps - 1).astype(F32)
                    pr = pair[0][p_rows, lanes] * keep
                    pi = pair[1][p_rows, lanes] * keep
                    acc += [sums[2 * st] + (xr * pr + xi * pi), sums[2 * st + 1] + (xi * pr - xr * pi)]
            carry, sums = tuple(out), tuple(acc)
        return carry + sums

    n_sums = 0 if pair is None else 2 * n_strips
    out = lax.fori_loop(0, n_steps // 2, fix, tuple(entering) + (zero,) * n_sums)
    return out[2 * n_strips:]


def _seq_fwd(proj, conv_w, bc_re, bc_im, cc_re, cc_im, dskip, a_pow, name):
    tp = proj.shape[0]
    dh = proj.shape[1] // 4
    nq = dh // LANES
    sw = STATE * N_GROUPS // nq

    def body(b_ref, c_ref, v_ref, u_ref, w_ref, bre_ref, bim_ref, cre_ref, cim_ref, d_ref, pw_ref,
             co_ref, y_ref, g_ref, s_re, s_im, u_il, y_il):
        co_ref[...] = b_ref[...] * _dwconv(c_ref[...] * v_ref[...], w_ref)
        _interleave(u_il, u_ref)
        ub = u_il[...].astype(BF16)
        s_re[...] = jnp.dot(ub, bre_ref[...], preferred_element_type=F32)
        s_im[...] = jnp.dot(ub, bim_ref[...], preferred_element_type=F32)
        _scan(s_re, s_im, pw_ref, False)
        y_il[...] = (jnp.dot(s_re[...].astype(BF16), cre_ref[...], preferred_element_type=F32)
                     - jnp.dot(s_im[...].astype(BF16), cim_ref[...], preferred_element_type=F32))
        _deinterleave(y_ref, y_il)
        y = y_ref[...] + d_ref[...] * u_ref[...]
        y_ref[...] = y
        g_ref[...] = _gelu(y).astype(BF16)

    col = lambda off: pl.BlockSpec((tp, LANES), lambda q, off=off: (0, off * nq + q))
    blk = pl.BlockSpec((tp, LANES), lambda q: (0, q))
    return pl.pallas_call(
        body, name=name, grid=(nq,),
        in_specs=[col(0), col(1), col(2), col(3),
                  pl.BlockSpec((3, LANES), lambda q: (0, q)),
                  pl.BlockSpec((LANES, sw), lambda q: (0, q)), pl.BlockSpec((LANES, sw), lambda q: (0, q)),
                  pl.BlockSpec((sw, LANES), lambda q: (q, 0)), pl.BlockSpec((sw, LANES), lambda q: (q, 0)),
                  pl.BlockSpec((1, LANES), lambda q: (0, q)),
                  pl.BlockSpec((2, tp // SUBLANES, sw), lambda q: (0, 0, q))],
        out_specs=[blk, blk, blk, pl.BlockSpec((tp, sw), lambda q: (0, q)), pl.BlockSpec((tp, sw), lambda q: (0, q))],
        out_shape=[jax.ShapeDtypeStruct((tp, dh), F32), jax.ShapeDtypeStruct((tp, dh), F32),
                   jax.ShapeDtypeStruct((tp, dh), BF16),
                   jax.ShapeDtypeStruct((tp, nq * sw), F32), jax.ShapeDtypeStruct((tp, nq * sw), F32)],
        scratch_shapes=[pltpu.VMEM((tp, LANES), F32), pltpu.VMEM((tp, LANES), F32)],
        compiler_params=_cparams("parallel"),
    )(proj, proj, proj, proj, conv_w, bc_re, bc_im, cc_re, cc_im, dskip, a_pow)


def _conv_bwd(proj, dco, conv_w, name):
    tp = proj.shape[0]
    dh = proj.shape[1] // 4
    nq = dh // LANES

    def body(b_ref, c_ref, v_ref, dco_ref, w_ref, dproj_ref, dw_ref, stage, sem):
        q = pl.program_id(0)
        cg = c_ref[...]
        vg = v_ref[...]
        cv = cg * vg
        dco_v = dco_ref[...]
        dcv, dw = _dwconv_bwd(cv, dco_v * b_ref[...], w_ref)
        dw_ref[...] = dw
        stage[0] = (dco_v * _dwconv(cv, w_ref)).astype(BF16)
        stage[1] = (dcv * vg).astype(BF16)
        stage[2] = (dcv * cg).astype(BF16)
        copies = [pltpu.make_async_copy(stage.at[p], dproj_ref.at[:, pl.ds((p * nq + q) * LANES, LANES)], sem.at[p])
                  for p in range(3)]
        for cp in copies:
            cp.start()
        for cp in copies:
            cp.wait()

    col = lambda off: pl.BlockSpec((tp, LANES), lambda q, off=off: (0, off * nq + q))
    return pl.pallas_call(
        body, name=name, grid=(nq,),
        in_specs=[col(0), col(1), col(2), pl.BlockSpec((tp, LANES), lambda q: (0, q)),
                  pl.BlockSpec((3, LANES), lambda q: (0, q))],
        out_specs=[pl.BlockSpec(memory_space=pl.ANY), pl.BlockSpec((3, LANES), lambda q: (0, q))],
        out_shape=[jax.ShapeDtypeStruct((tp, 4 * dh), BF16), jax.ShapeDtypeStruct((3, dh), F32)],
        scratch_shapes=[pltpu.VMEM((3, tp, LANES), BF16), pltpu.SemaphoreType.DMA((3,))],
        compiler_params=_cparams("arbitrary"),
    )(proj, proj, proj, dco, conv_w)


def _ssm_bwd(proj, y, dg, dproj, states, bc_re, bc_im, cc_re, cc_im, dskip, a_pow, name):
    tp = proj.shape[0]
    dh = proj.shape[1] // 4
    nq = dh // LANES
    sw = STATE * N_GROUPS // nq

    def body(u_ref, y_ref, dg_ref, dproj_in, s_re, s_im, bre_ref, bim_ref, cre_ref, cim_ref, d_ref, pw_ref,
             dproj_ref, dbre_ref, dbim_ref, dcre_ref, dcim_ref, dd_ref, dar_ref, dai_ref,
             l_re, l_im, a_il, b_il, stage, sem):
        del dproj_in
        q = pl.program_id(0)
        nt = (((1,), (1,)), ((), ()))
        tn = (((0,), (0,)), ((), ()))
        _interleave(a_il, u_ref)
        ub = a_il[...].astype(BF16)
        dy_rows = dg_ref[...] * _gelu_grad(y_ref[...])
        dd_ref[...] = jnp.sum(dy_rows * u_ref[...], axis=0, keepdims=True)
        _interleave(b_il, dy_rows)
        dy = b_il[...]
        dyb = dy.astype(BF16)
        l_re[...] = lax.dot_general(dyb, cre_ref[...], nt, preferred_element_type=F32)
        l_im[...] = -lax.dot_general(dyb, cim_ref[...], nt, preferred_element_type=F32)
        dcre_ref[...] = lax.dot_general(s_re[...].astype(BF16), dyb, tn, preferred_element_type=F32)
        dcim_ref[...] = -lax.dot_general(s_im[...].astype(BF16), dyb, tn, preferred_element_type=F32)
        sums = _scan(l_re, l_im, pw_ref, True, pair=(s_re, s_im))
        rest = tp - SUBLANES
        for st in range(sw // LANES):
            lanes = slice(st * LANES, (st + 1) * LANES)
            lr0, li0 = l_re[:SUBLANES, lanes], l_im[:SUBLANES, lanes]
            pr0, pi0 = _segment_shift(s_re[rest:, lanes], False), _segment_shift(s_im[rest:, lanes], False)
            dar_ref[:, lanes] = jnp.sum(sums[2 * st] + (lr0 * pr0 + li0 * pi0), axis=0, keepdims=True)
            dai_ref[:, lanes] = jnp.sum(sums[2 * st + 1] + (li0 * pr0 - lr0 * pi0), axis=0, keepdims=True)
        lrb = l_re[...].astype(BF16)
        lib = l_im[...].astype(BF16)
        a_il[...] = (dy * d_ref[...] + lax.dot_general(lrb, bre_ref[...], nt, preferred_element_type=F32)
                     + lax.dot_general(lib, bim_ref[...], nt, preferred_element_type=F32))
        _deinterleave(b_il, a_il)
        stage[...] = b_il[...].astype(BF16)
        dbre_ref[...] = lax.dot_general(ub, lrb, tn, preferred_element_type=F32)
        dbim_ref[...] = lax.dot_general(ub, lib, tn, preferred_element_type=F32)
        cp = pltpu.make_async_copy(stage, dproj_ref.at[:, pl.ds((3 * nq + q) * LANES, LANES)], sem)
        cp.start()
        cp.wait()

    blk = pl.BlockSpec((tp, LANES), lambda q: (0, q))
    bspec = pl.BlockSpec((LANES, sw), lambda q: (0, q))
    cspec = pl.BlockSpec((sw, LANES), lambda q: (q, 0))
    tspec = pl.BlockSpec((2, tp // SUBLANES, sw), lambda q: (0, 0, q))
    nstate = STATE * N_GROUPS
    return pl.pallas_call(
        body, name=name, grid=(nq,),
        in_specs=[pl.BlockSpec((tp, LANES), lambda q: (0, 3 * nq + q)), blk, blk, pl.BlockSpec(memory_space=pl.ANY),
                  pl.BlockSpec((tp, sw), lambda q: (0, q)), pl.BlockSpec((tp, sw), lambda q: (0, q)),
                  bspec, bspec, cspec, cspec, pl.BlockSpec((1, LANES), lambda q: (0, q)), tspec],
        out_specs=[pl.BlockSpec(memory_space=pl.ANY), bspec, bspec, cspec, cspec,
                   pl.BlockSpec((1, LANES), lambda q: (0, q)),
                   pl.BlockSpec((1, sw), lambda q: (0, q)), pl.BlockSpec((1, sw), lambda q: (0, q))],
        out_shape=[jax.ShapeDtypeStruct((tp, 4 * dh), BF16),
                   jax.ShapeDtypeStruct((LANES, nstate), F32), jax.ShapeDtypeStruct((LANES, nstate), F32),
                   jax.ShapeDtypeStruct((nstate, LANES), F32), jax.ShapeDtypeStruct((nstate, LANES), F32),
                   jax.ShapeDtypeStruct((1, dh), F32),
                   jax.ShapeDtypeStruct((1, nstate), F32), jax.ShapeDtypeStruct((1, nstate), F32)],
        input_output_aliases={3: 0},
        scratch_shapes=[pltpu.VMEM((tp, sw), F32)] * 2 + [pltpu.VMEM((tp, LANES), F32)] * 2
        + [pltpu.VMEM((tp, LANES), BF16), pltpu.SemaphoreType.DMA],
        compiler_params=_cparams("arbitrary"),
    )(proj, y, dg, dproj, states[0], states[1], bc_re, bc_im, cc_re, cc_im, dskip, a_pow)


FFN_TILE = 256
FFN_ROWS = 32


def _taps(x_ref, r0, rows):
    if r0 == 0:
        window = jnp.concatenate([jnp.zeros((SUBLANES, x_ref.shape[1]), F32), x_ref[0:rows, :]], axis=0)
    else:
        window = x_ref[r0 - SUBLANES:r0 + rows, :]
    return window[SUBLANES:], pltpu.roll(window, 1, 0)[SUBLANES:], pltpu.roll(window, 2, 0)[SUBLANES:]


def _conv_taps(taps, w):
    return w[2] * taps[0] + w[1] * taps[1] + w[0] * taps[2]


def _ffn_act(up, fw, fb, name):
    tp, two_ff = up.shape
    dff = two_ff // 2
    tc = FFN_TILE
    nj = dff // tc

    def body(ua_ref, uv_ref, wa_ref, wv_ref, ba_ref, bv_ref, act_ref):
        wa, wv = [[w_ref[k:k + 1, :] for k in range(3)] for w_ref in (wa_ref, wv_ref)]
        ba, bv = ba_ref[...], bv_ref[...]
        for r0 in range(0, tp, FFN_ROWS):
            a = _conv_taps(_taps(ua_ref, r0, FFN_ROWS), wa) + ba
            v = _conv_taps(_taps(uv_ref, r0, FFN_ROWS), wv) + bv
            act_ref[r0:r0 + FFN_ROWS, :] = (a * _sigmoid(a) * v).astype(BF16)

    lo = lambda r: pl.BlockSpec((r, tc), lambda j: (0, j))
    hi = lambda r: pl.BlockSpec((r, tc), lambda j: (0, nj + j))
    return pl.pallas_call(
        body, name=name, grid=(nj,),
        in_specs=[lo(tp), hi(tp), lo(3), hi(3), lo(1), hi(1)],
        out_specs=lo(tp),
        out_shape=jax.ShapeDtypeStruct((tp, dff), BF16),
        compiler_params=_cparams("parallel"))(up, up, fw, fw, fb, fb)


def _ffn_bwd(up, dact, fw, fb, name):
    tp, two_ff = up.shape
    dff = two_ff // 2
    tc = FFN_TILE
    nj = dff // tc

    def body(ua_ref, uv_ref, da_ref, wa_ref, wv_ref, ba_ref, bv_ref,
             dup_ref, dwa_ref, dwv_ref, dba_ref, dbv_ref, stage, sem):
        j = pl.program_id(0)
        rows = FFN_ROWS
        n_e = rows + SUBLANES
        w = [[w_ref[k:k + 1, :] for k in range(3)] for w_ref in (wa_ref, wv_ref)]
        bias = [ba_ref[...], bv_ref[...]]
        piece = jnp.zeros((SUBLANES, tc), F32)
        head = [piece, piece]
        acc = [[piece] * 4 for _ in range(2)]
        fold = lambda x: sum(x[r:r + SUBLANES] for r in range(0, rows, SUBLANES))
        for r0 in reversed(range(0, tp, rows)):
            taps = [_taps(ua_ref, r0, rows), _taps(uv_ref, r0, rows)]
            a, v = [_conv_taps(taps[s], w[s]) + bias[s] for s in range(2)]
            sg = _sigmoid(a)
            dact_v = da_ref[r0:r0 + rows, :]
            d = [dact_v * v * sg * (1.0 + a * (1.0 - sg)), dact_v * a * sg]
            for s in range(2):
                de = jnp.concatenate([d[s], head[s]], axis=0)
                dx = (w[s][2] * d[s] + w[s][1] * pltpu.roll(de, n_e - 1, 0)[:rows]
                      + w[s][0] * pltpu.roll(de, n_e - 2, 0)[:rows])
                stage[s, r0:r0 + rows, :] = dx.astype(BF16)
                for k in range(3):
                    acc[s][k] = acc[s][k] + fold(d[s] * taps[s][2 - k])
                acc[s][3] = acc[s][3] + fold(d[s])
                head[s] = d[s][:SUBLANES]
        for s, (dw_ref, db_ref) in enumerate(((dwa_ref, dba_ref), (dwv_ref, dbv_ref))):
            dw_ref[...] = jnp.concatenate([jnp.sum(x, axis=0, keepdims=True) for x in acc[s][:3]], axis=0)
            db_ref[...] = jnp.sum(acc[s][3], axis=0, keepdims=True)
        copies = [pltpu.make_async_copy(stage.at[p], dup_ref.at[:, pl.ds((p * nj + j) * tc, tc)], sem.at[p])
                  for p in range(2)]
        for cp in copies:
            cp.start()
        for cp in copies:
            cp.wait()

    lo = lambda r: pl.BlockSpec((r, tc), lambda j: (0, j))
    hi = lambda r: pl.BlockSpec((r, tc), lambda j: (0, nj + j))
    return pl.pallas_call(
        body, name=name, grid=(nj,),
        in_specs=[lo(tp), hi(tp), lo(tp), lo(3), hi(3), lo(1), hi(1)],
        out_specs=[pl.BlockSpec(memory_space=pl.ANY), lo(3), lo(3), lo(1), lo(1)],
        out_shape=[jax.ShapeDtypeStruct((tp, two_ff), BF16),
                   jax.ShapeDtypeStruct((3, dff), F32), jax.ShapeDtypeStruct((3, dff), F32),
                   jax.ShapeDtypeStruct((1, dff), F32), jax.ShapeDtypeStruct((1, dff), F32)],
        scratch_shapes=[pltpu.VMEM((2, tp, tc), BF16), pltpu.SemaphoreType.DMA((2,))],
        compiler_params=_cparams("arbitrary"))(up, up, dact, fw, fw, fb, fb)


def _zoh(lr, li, ld):
    dt = jnp.exp(ld)
    mag = jnp.exp(lr * dt)
    ang = li * dt
    ar = mag * jnp.cos(ang)
    ai = mag * jnp.sin(ang)
    den = lr * lr + li * li
    nr = ar - 1.0
    fr = (nr * lr + ai * li) / den
    fi = (ai * lr - nr * li) / den
    return dt, ar, ai, den, nr, fr, fi


def _s5_prep(lr, li, ld, b_re, b_im, n_pow, name):
    nstate = lr.shape[1]

    def body(lr_ref, li_ref, ld_ref, bre_ref, bim_ref, pw_ref, bcre_ref, bcim_ref):
        _, ar, ai, _, _, fr, fi = _zoh(lr_ref[...], li_ref[...], ld_ref[...])
        bre = bre_ref[...]
        bim = bim_ref[...]
        bcre_ref[...] = (fr * bre - fi * bim).astype(BF16)
        bcim_ref[...] = (fr * bim + fi * bre).astype(BF16)
        row = lax.broadcasted_iota(jnp.int32, (SUBLANES, nstate), 0)
        pr, pi = jnp.zeros((SUBLANES, nstate), F32), jnp.zeros((SUBLANES, nstate), F32)
        cr, ci = ar, ai
        for t in range(SUBLANES):
            pr, pi = jnp.where(row == t, cr, pr), jnp.where(row == t, ci, pi)
            cr, ci = cr * ar - ci * ai, cr * ai + ci * ar
        pw_ref[0, 0:SUBLANES, :] = pr
        pw_ref[1, 0:SUBLANES, :] = pi
        n = SUBLANES
        while n < n_pow:
            m = min(n, n_pow - n)
            tr, ti = pw_ref[0, n - 1:n, :], pw_ref[1, n - 1:n, :]
            xr, xi = pw_ref[0, 0:m, :], pw_ref[1, 0:m, :]
            pw_ref[0, n:n + m, :] = xr * tr - xi * ti
            pw_ref[1, n:n + m, :] = xr * ti + xi * tr
            n += m

    vmem = pl.BlockSpec(memory_space=pltpu.VMEM)
    return pl.pallas_call(
        body, name=name, in_specs=[vmem] * 5, out_specs=[vmem] * 3,
        out_shape=[jax.ShapeDtypeStruct((2, n_pow, nstate), F32)] + [jax.ShapeDtypeStruct(b_re.shape, BF16)] * 2,
        compiler_params=pltpu.CompilerParams(vmem_limit_bytes=VMEM_LIMIT))(lr, li, ld, b_re, b_im)


def _s5_prep_bwd(lr, li, ld, b_re, b_im, da_re, da_im, dbc_re, dbc_im, name):
    def body(lr_ref, li_ref, ld_ref, bre_ref, bim_ref, dar_ref, dai_ref, dbcre_ref, dbcim_ref,
             dlr_ref, dli_ref, dld_ref, dbre_ref, dbim_ref):
        lr, li = lr_ref[...], li_ref[...]
        dt, ar, ai, den, nr, fr, fi = _zoh(lr, li, ld_ref[...])
        bre, bim = bre_ref[...], bim_ref[...]
        gre, gim = dbcre_ref[...], dbcim_ref[...]
        dbre_ref[...] = fr * gre + fi * gim
        dbim_ref[...] = fr * gim - fi * gre
        g_fr = jnp.sum(gre * bre + gim * bim, axis=0, keepdims=True)
        g_fi = jnp.sum(gim * bre - gre * bim, axis=0, keepdims=True)
        g_ar = dar_ref[...] + (g_fr * lr - g_fi * li) / den
        g_ai = dai_ref[...] + (g_fr * li + g_fi * lr) / den
        d_lr = (g_fr * (nr - 2.0 * fr * lr) + g_fi * (ai - 2.0 * fi * lr)) / den
        d_li = (g_fr * (ai - 2.0 * fr * li) - g_fi * (nr + 2.0 * fi * li)) / den
        g_logmag = g_ar * ar + g_ai * ai
        g_ang = g_ai * ar - g_ar * ai
        dlr_ref[...] = d_lr + g_logmag * dt
        dli_ref[...] = d_li + g_ang * dt
        d_ld = (g_logmag * lr + g_ang * li) * dt
        n = d_ld.shape[1]
        sh = 1
        while sh < STATE:
            d_ld = d_ld + pltpu.roll(d_ld, n - sh, 1)
            sh *= 2
        dld_ref[...] = d_ld

    vmem = pl.BlockSpec(memory_space=pltpu.VMEM)
    row = jax.ShapeDtypeStruct(lr.shape, F32)
    return pl.pallas_call(
        body, name=name, in_specs=[vmem] * 9, out_specs=[vmem] * 5,
        out_shape=[row, row, row, jax.ShapeDtypeStruct(b_re.shape, F32), jax.ShapeDtypeStruct(b_re.shape, F32)],
    )(lr, li, ld, b_re, b_im, da_re, da_im, dbc_re, dbc_im)


def _compact_b(bb):
    bq = bb.reshape(N_GROUPS // 8, 8, STATE, GROUP)
    m = jnp.einsum("ab,qbph->qahbp", jnp.eye(8, dtype=bb.dtype), bq).reshape(N_GROUPS // 8, LANES, 8 * STATE)
    return m.transpose(1, 0, 2).reshape(LANES, N_GROUPS * STATE)


def _expand_b(m):
    d = m.reshape(8, GROUP, N_GROUPS // 8, 8, STATE)
    return jnp.einsum("ahqap->qahp", d).reshape(N_GROUPS, GROUP, STATE)


def _compact_c(c):
    cq = c.reshape(N_GROUPS // 8, 8, GROUP, STATE)
    return jnp.einsum("ab,qbhp->qbpah", jnp.eye(8, dtype=c.dtype), cq).reshape(N_GROUPS * STATE, LANES)


def _expand_c(m):
    d = m.reshape(N_GROUPS // 8, 8, STATE, 8, GROUP)
    return jnp.einsum("qbpbh->qbhp", d).reshape(N_GROUPS, GROUP, STATE)


def _local_step(x, target, p, ex):
    seq, d = x.shape
    n_real = N_META + seq
    tp = -(-n_real // ROW_ALIGN) * ROW_ALIGN

    h0, hn1 = _input_norm_fwd(x, p["meta_tokens"], p["norm_mix_g"] + ex.zero, tp, "norm_mix")
    ex.forward("first", hn1)
    nstate = N_GROUPS * STATE
    s5 = (p["ssm_lam_re"].reshape(1, nstate), p["ssm_lam_im"].reshape(1, nstate),
          jnp.repeat(p["ssm_log_dt"].reshape(-1), STATE).reshape(1, nstate),
          _compact_b(p["ssm_b_re"]), _compact_b(p["ssm_b_im"]))
    a_pow, bc_re, bc_im = _s5_prep(*s5, tp // SUBLANES, "s5_prep")
    cc_re = _compact_c(p["ssm_c_re"]).astype(BF16)
    cc_im = _compact_c(p["ssm_c_im"]).astype(BF16)
    dskip = p["ssm_d"].reshape(1, -1)
    first = ex.weights("first", bc_re)
    proj = _mm(hn1, first["w_in"], "nn", "proj")
    started = ex.forward("mid", proj)
    co, y, g, *states = _seq_fwd(proj, p["conv_w"] + started[0, 0], bc_re, bc_im, cc_re, cc_im, dskip, a_pow,
                                 "seq_fwd")
    mid = ex.weights("mid", g)
    z = _mm(g, mid["ssm_w_glu"], "nn", "glu")
    mixed = _mix_fwd(co, y, z, p["gain_conv_out"], p["gain_ssm_out"], "mix_fwd")
    started = ex.forward("up", mixed)
    h1, hn2 = _proj_res_norm(mixed, mid["w_out"], h0, p["norm_ffn_g"], started, "out_proj_norm")
    late = ex.weights("up", hn2)
    up = _mm(hn2, late["w_up"], "nn", "up_proj")
    started = ex.forward("down", up)
    act = _ffn_act(up, p["ffn_conv_w"] + started[0, 0], p["ffn_conv_b"], "ffn_act")
    late.update(ex.weights("down", act))
    loss, dh2, dh2b, d_gfin = _proj_loss_bwd(act, late["w_down"], h1, target, p["norm_final_g"], n_real,
                                             "down_proj_loss")

    g_w_down = _mm(act, dh2b, "tn", "g_w_down")
    dact = _mm(dh2b, late["w_down"], "nt", "d_act")
    dup, dfw_a, dfw_v, dfb_a, dfb_v = _ffn_bwd(up, dact, p["ffn_conv_w"], p["ffn_conv_b"], "ffn_bwd")
    g_w_up = _mm(hn2, dup, "tn", "g_w_up")
    started = ex.grads_ready("late", {"w_up": g_w_up, "w_down": g_w_down})
    dh1, dh1b, d_gffn = _proj_norm_bwd(dup, late["w_up"], h1, p["norm_ffn_g"], dh2, started, "d_hn2_norm_bwd")
    started = ex.grads_send("late", dh1)
    g_w_out = _mm(mixed, dh1b, "tn", "g_w_out", after=started)
    dco, dz, dgp, d_gc, d_gs = _proj_mix_bwd(dh1b, mid["w_out"], co, y, z, p["gain_conv_out"],
                                             p["gain_ssm_out"], "d_mixed_mix_bwd")
    g_w_glu = _mm(g, dz, "tn", "g_w_glu")
    started = ex.grads_ready("mid", {"ssm_w_glu": g_w_glu, "w_out": g_w_out})
    dg = _mm(dz, mid["ssm_w_glu"], "nt", "d_gelu", acc_in=dgp, after=started)
    started = ex.grads_send("mid", dg)
    dproj, d_conv_w = _conv_bwd(proj, dco, p["conv_w"] + started[0, 0], "conv_bwd")
    (dproj, dbc_re, dbc_im, dcc_re, dcc_im, d_dskip, da_re, da_im) = _ssm_bwd(
        proj, y, dg, dproj, states, bc_re, bc_im, cc_re, cc_im, dskip, a_pow, "ssm_bwd")
    g_w_in = _mm(hn1, dproj, "tn", "g_w_in")
    started = ex.grads_ready("first", {"w_in": g_w_in})
    grad_x, d_meta, d_gmix = _proj_input_norm_bwd(dproj, first["w_in"], h0, p["norm_mix_g"], dh1, started, n_real,
                                                  "d_hn1_norm_bwd")
    started = ex.grads_send("first", d_gmix)

    d_lam_re, d_lam_im, d_log_dt, d_b_re, d_b_im = _s5_prep_bwd(*s5, da_re, da_im, dbc_re, dbc_im, "s5_prep_bwd")
    d_lam_re, d_lam_im = d_lam_re.reshape(N_GROUPS, STATE), d_lam_im.reshape(N_GROUPS, STATE)
    d_log_dt = d_log_dt[0, ::STATE]
    d_b_re, d_b_im = _expand_b(d_b_re), _expand_b(d_b_im)
    grads = {
        "meta_tokens": d_meta, "norm_mix_g": d_gmix, "w_in": g_w_in, "conv_w": d_conv_w,
        "ssm_lam_re": d_lam_re, "ssm_lam_im": d_lam_im, "ssm_log_dt": d_log_dt,
        "ssm_b_re": d_b_re, "ssm_b_im": d_b_im, "ssm_c_re": _expand_c(dcc_re), "ssm_c_im": _expand_c(dcc_im),
        "ssm_d": d_dskip.reshape(N_GROUPS, GROUP), "ssm_w_glu": g_w_glu,
        "gain_conv_out": d_gc, "gain_ssm_out": d_gs, "w_out": g_w_out, "norm_ffn_g": d_gffn,
        "w_up": g_w_up, "ffn_conv_w": jnp.concatenate([dfw_a, dfw_v], axis=1),
        "ffn_conv_b": jnp.concatenate([dfb_a, dfb_v], axis=1), "w_down": g_w_down, "norm_final_g": d_gfin,
    }
    return loss[0, 0] + started[0, 0], grad_x, grads


def _view(ref, axis, start, size):
    idx = [slice(None)] * len(ref.shape)
    idx[axis] = pl.ds(start, size)
    return ref.at[tuple(idx)]


def _exchange(name, ins, outs, aliases, local_copies, remote_copies):
    ni, no = len(ins), len(outs)
    nl, nr = len(local_copies), len(remote_copies)

    def body(*refs):
        in_refs, out_refs = refs[:ni], refs[ni:ni + no]
        send_sems, recv_sems, local_sems = refs[ni + no:]
        x, y, c = lax.axis_index("x"), lax.axis_index("y"), lax.axis_index("c")
        pos = (x, y, c, 2 * x + y)
        locals_ = [pltpu.make_async_copy(s(in_refs, out_refs, pos), d(in_refs, out_refs, pos), local_sems.at[i])
                   for i, (s, d) in enumerate(local_copies)]
        remotes = []
        for i, (s, d, flip) in enumerate(remote_copies):
            peer = (1 - x if "x" in flip else x, 1 - y if "y" in flip else y, 1 - c if "c" in flip else c)
            remotes.append(pltpu.make_async_remote_copy(
                src_ref=s(in_refs, out_refs, pos), dst_ref=d(in_refs, out_refs, pos),
                send_sem=send_sems.at[i], recv_sem=recv_sems.at[i], device_id=peer, device_id_type=MESH))
        for cp in locals_ + remotes:
            cp.start()
        for cp in remotes:
            cp.wait_recv()
        for cp in remotes:
            cp.wait_send()
        for cp in locals_:
            cp.wait()

    hbm = pl.BlockSpec(memory_space=pl.ANY)
    return pl.pallas_call(
        body, name=name, in_specs=[hbm] * ni, out_specs=[hbm] * no, out_shape=outs,
        input_output_aliases=aliases,
        scratch_shapes=[pltpu.SemaphoreType.DMA((nr,)), pltpu.SemaphoreType.DMA((nr,)),
                        pltpu.SemaphoreType.DMA((max(nl, 1),))],
    )(*ins)


BIG = {"w_in": (0, 1), "ssm_w_glu": (1, 0), "w_out": (1, 0), "w_up": (0, 1), "w_down": (1, 0)}
BIG_NAMES = tuple(BIG)
FLIPS = ("y", "x", "xy")


def _peer_chip(pos, flip):
    x, y, _, _ = pos
    return 2 * (1 - x if "x" in flip else x) + (1 - y if "y" in flip else y)


def _block_rows(rows, cols, itemsize, mult):
    return _pick_tile(rows, max(mult, (2 * 1024 * 1024) // (cols * itemsize)), mult)


def _cast_into_full(w, kc, shard_axis, name):
    r, cdim = w.shape
    tr = _block_rows(r, cdim, 4, 16)
    nb = r // tr

    def body(kc_ref, w_ref, o_ref):
        o_ref[...] = w_ref[...].astype(BF16)

    if shard_axis == 1:
        full, o_spec = (r, 4 * cdim), pl.BlockSpec((tr, cdim), lambda i, kc: (i, kc[0]))
    else:
        full, o_spec = (4 * r, cdim), pl.BlockSpec((tr, cdim), lambda i, kc: (kc[0] * nb + i, 0))
    return pl.pallas_call(
        body, name=name,
        grid_spec=pltpu.PrefetchScalarGridSpec(
            num_scalar_prefetch=1, grid=(nb,), in_specs=[pl.BlockSpec((tr, cdim), lambda i, kc: (i, 0))],
            out_specs=o_spec),
        out_shape=jax.ShapeDtypeStruct(full, BF16), compiler_params=_cparams("parallel"))(kc, w)


def _pair_sum(g, recv, kc, half_axis, name, out_dtype):
    hr, hc = recv.shape
    tr = _block_rows(hr, hc, 4, 16)
    nb = hr // tr

    def body(kc_ref, g_ref, r_ref, o_ref):
        o_ref[...] = (g_ref[...] + r_ref[...]).astype(out_dtype)

    if half_axis == 0:
        g_spec = pl.BlockSpec((tr, hc), lambda i, kc: (kc[1] * nb + i, 0))
    elif half_axis == 1:
        g_spec = pl.BlockSpec((tr, hc), lambda i, kc: (i, kc[1]))
    else:
        g_spec = pl.BlockSpec((tr, hc), lambda i, kc: (i, 0))
    same = pl.BlockSpec((tr, hc), lambda i, kc: (i, 0))
    return pl.pallas_call(
        body, name=name,
        grid_spec=pltpu.PrefetchScalarGridSpec(num_scalar_prefetch=1, grid=(nb,), in_specs=[g_spec, same],
                                               out_specs=same),
        out_shape=jax.ShapeDtypeStruct((hr, hc), out_dtype), compiler_params=_cparams("parallel"))(kc, g, recv)


def _chip_sum(own, recv, kc, own_axis, out_axis, name):
    _, sr, sc = recv.shape
    tr = _block_rows(sr, sc, 4, 16)
    nb = sr // tr

    def body(kc_ref, o_ref, r_ref, t_ref):
        k = kc_ref[0]
        own_v = o_ref[...].astype(F32)
        r = [r_ref[m].astype(F32) for m in range(3)]
        terms = []
        for kk in range(4):
            m = jnp.bitwise_xor(k, kk)
            terms.append(jnp.where(m == 0, own_v, jnp.where(m == 1, r[0], jnp.where(m == 2, r[1], r[2]))))
        t_ref[...] = (terms[0] + terms[1]) + (terms[2] + terms[3])

    if own_axis == 0:
        own_spec = pl.BlockSpec((tr, sc), lambda i, kc: (kc[0] * nb + i, 0))
    elif own_axis == 1:
        own_spec = pl.BlockSpec((tr, sc), lambda i, kc: (i, kc[0]))
    else:
        own_spec = pl.BlockSpec((tr, sc), lambda i, kc: (kc[1] * nb + i, 0))
    if out_axis == 0:
        out_full, out_spec = (2 * sr, sc), pl.BlockSpec((tr, sc), lambda i, kc: (kc[1] * nb + i, 0))
    else:
        out_full, out_spec = (sr, 2 * sc), pl.BlockSpec((tr, sc), lambda i, kc: (i, kc[1]))
    return pl.pallas_call(
        body, name=name,
        grid_spec=pltpu.PrefetchScalarGridSpec(
            num_scalar_prefetch=1, grid=(nb,),
            in_specs=[own_spec, pl.BlockSpec((3, tr, sc), lambda i, kc: (0, i, 0))],
            out_specs=out_spec),
        out_shape=jax.ShapeDtypeStruct(out_full, F32), compiler_params=_cparams("parallel"))(kc, own, recv)


def _adamw(w, g, m, v, name):
    r, cdim = w.shape
    tr = _block_rows(r, cdim, 4, 8)
    c1 = 1.0 - ADAM_B1 ** ADAM_STEP
    c2 = 1.0 - ADAM_B2 ** ADAM_STEP

    def body(w_ref, g_ref, m_ref, v_ref, go_ref, d_ref, nm_ref, nv_ref):
        gv = g_ref[...]
        go_ref[...] = gv
        nm = ADAM_B1 * m_ref[...] + (1.0 - ADAM_B1) * gv
        nv = ADAM_B2 * v_ref[...] + (1.0 - ADAM_B2) * (gv * gv)
        d_ref[...] = -ADAM_LR * ((nm / c1) / (jnp.sqrt(nv / c2) + ADAM_EPS) + ADAM_WD * w_ref[...])
        nm_ref[...] = nm
        nv_ref[...] = nv

    spec = _rows(cdim, tr)
    return pl.pallas_call(body, name=name, grid=(r // tr,), in_specs=[spec] * 4, out_specs=[spec] * 4,
                          out_shape=[jax.ShapeDtypeStruct((r, cdim), F32)] * 4,
                          compiler_params=_cparams("parallel"))(w, g, m, v)


def _adamw_whole(ws, gs, ms, vs, name):
    n = len(ws)
    c1 = 1.0 - ADAM_B1 ** ADAM_STEP
    c2 = 1.0 - ADAM_B2 ** ADAM_STEP

    def body(*refs):
        for i in range(n):
            w_ref, g_ref, m_ref, v_ref, d_ref, nm_ref, nv_ref = [refs[j * n + i] for j in range(7)]
            gv = g_ref[...]
            nm = ADAM_B1 * m_ref[...] + (1.0 - ADAM_B1) * gv
            nv = ADAM_B2 * v_ref[...] + (1.0 - ADAM_B2) * (gv * gv)
            d_ref[...] = -ADAM_LR * ((nm / c1) / (jnp.sqrt(nv / c2) + ADAM_EPS) + ADAM_WD * w_ref[...])
            nm_ref[...] = nm
            nv_ref[...] = nv

    vmem = pl.BlockSpec(memory_space=pltpu.VMEM)
    out = pl.pallas_call(body, name=name, in_specs=[vmem] * (4 * n), out_specs=[vmem] * (3 * n),
                         out_shape=[jax.ShapeDtypeStruct(a.shape, F32) for a in ws] * 3,
                         compiler_params=pltpu.CompilerParams(vmem_limit_bytes=VMEM_LIMIT))(*ws, *gs, *ms, *vs)
    return out[:n], out[n:2 * n], out[2 * n:]


SIDE_EFFECT = pltpu.SideEffectType.DATAFLOW_SIDE_EFFECTING


def _descriptors(copies, refs, send_sems, recv_sems, sem_off=0):
    x, y, c = lax.axis_index("x"), lax.axis_index("y"), lax.axis_index("c")
    pos = (x, y, c, 2 * x + y)
    out = []
    for i, (s, d, flip) in enumerate(copies):
        peer = (1 - x if "x" in flip else x, 1 - y if "y" in flip else y, 1 - c if "c" in flip else c)
        out.append(pltpu.make_async_remote_copy(
            src_ref=s(refs, refs, pos), dst_ref=d(refs, refs, pos),
            send_sem=send_sems.at[sem_off + i], recv_sem=recv_sems.at[sem_off + i],
            device_id=peer, device_id_type=MESH))
    return out


def _shifted(copies, off):
    return [(lambda I, O, pos, s=s: s(I[off:], O[off:], pos), lambda I, O, pos, d=d: d(I[off:], O[off:], pos), flip)
            for s, d, flip in copies]


BARRIER_IDS = {"c": (1, 2), "ici": (3, 4)}


def _exchange_start(name, bufs, copies, turns, after=None):
    n, nr = len(bufs), len(copies)
    na = 0 if after is None else 1
    flips = sorted({flip for _, _, flip in copies})
    kind = "c" if flips == ["c"] else "ici"
    collective_id = BARRIER_IDS[kind][turns[kind] % 2]
    turns[kind] += 1

    def body(*refs):
        x, y, c = lax.axis_index("x"), lax.axis_index("y"), lax.axis_index("c")
        barrier = pltpu.get_barrier_semaphore()
        for flip in flips:
            peer = (1 - x if "x" in flip else x, 1 - y if "y" in flip else y, 1 - c if "c" in flip else c)
            pl.semaphore_signal(barrier, inc=1, device_id=peer, device_id_type=MESH)
        pl.semaphore_wait(barrier, len(flips))
        for cp in _descriptors(copies, refs[:n], refs[n + na], refs[n + na + 1]):
            cp.start()
        token = refs[2 * n + na + 2]
        token[...] = jnp.zeros_like(token)

    hbm = pl.BlockSpec(memory_space=pltpu.HBM)
    sem = pl.BlockSpec(memory_space=pltpu.SEMAPHORE)
    out = pl.pallas_call(
        body, name=name,
        in_specs=[hbm] * n + [pl.BlockSpec(memory_space=pl.ANY)] * na,
        out_specs=(sem, sem, *[hbm] * n, pl.BlockSpec(memory_space=pltpu.VMEM)),
        out_shape=(pltpu.SemaphoreType.DMA((nr,)), pltpu.SemaphoreType.DMA((nr,)),
                   *[pltpu.HBM(b.shape, b.dtype) for b in bufs], jax.ShapeDtypeStruct((SUBLANES, LANES), F32)),
        input_output_aliases={i: 2 + i for i in range(n)},
        compiler_params=pltpu.CompilerParams(has_side_effects=SIDE_EFFECT, collective_id=collective_id),
    )(*[pltpu.with_memory_space_constraint(b, pltpu.HBM) for b in bufs], *([after] * na))
    return out[0], out[1], list(out[2:2 + n]), out[2 + n]


def _exchange_wait(name, send_sems, recv_sems, bufs, copies, after, sem_off=0):
    n = len(bufs)

    def body(*refs):
        for cp in _descriptors(copies, refs[:n], refs[n], refs[n + 1], sem_off):
            cp.wait_send()
            cp.wait_recv()

    hbm = pl.BlockSpec(memory_space=pltpu.HBM)
    sem = pl.BlockSpec(memory_space=pltpu.SEMAPHORE)
    out = pl.pallas_call(
        body, name=name,
        in_specs=[hbm] * n + [sem, sem, pl.BlockSpec(memory_space=pl.ANY)],
        out_specs=tuple([hbm] * n),
        out_shape=tuple(pltpu.HBM(b.shape, b.dtype) for b in bufs),
        input_output_aliases={i: i for i in range(n)},
        compiler_params=pltpu.CompilerParams(has_side_effects=SIDE_EFFECT),
    )(*bufs, send_sems, recv_sems, after)
    return list(out)


FIRST = ("w_in",)
MID = ("ssm_w_glu", "w_out")
LATE = ("w_up", "w_down")
GROUPS = {"first": FIRST, "mid": MID, "late": LATE}
ARRIVALS = {"first": FIRST, "mid": MID, "up": ("w_up",), "down": ("w_down",)}


def _gather_copies(names, shard_shapes):
    def region(i, chip, c):
        half_axis, shard_axis = BIG[names[i]]
        ssize = shard_shapes[i][shard_axis]
        hsize = shard_shapes[i][half_axis] // 2
        return lambda ref: _view(_view(ref, shard_axis, chip * ssize, ssize), half_axis, c * hsize, hsize)

    ici, d2d = [], []
    for i in range(len(names)):
        for flip in FLIPS:
            ici.append((lambda I, O, pos, i=i: region(i, pos[3], pos[2])(I[i]),
                        lambda I, O, pos, i=i: region(i, pos[3], pos[2])(O[i]), flip))
            d2d.append((lambda I, O, pos, i=i, flip=flip: region(i, _peer_chip(pos, flip), pos[2])(I[i]),
                        lambda I, O, pos, i=i, flip=flip: region(i, _peer_chip(pos, flip), pos[2])(O[i]), "c"))
    return ici, d2d


def _half_shape(n, shape):
    r, cdim = shape
    return (r // 2, cdim) if BIG[n][0] == 0 else (r, cdim // 2)


def _sub_shape(n, shape):
    hr, hc = _half_shape(n, shape)
    return (hr, hc // 4) if BIG[n][1] == 1 else (hr // 4, hc)


def _pair_copies(names, shapes, with_pack, dst_off):
    n = len(names)

    def other_half(i, ref, pos):
        half_axis = BIG[names[i]][0]
        hsize = shapes[i][half_axis] // 2
        return _view(ref, half_axis, (1 - pos[2]) * hsize, hsize)

    copies = [(lambda I, O, pos, i=i: other_half(i, I[i], pos), lambda I, O, pos, i=i: O[dst_off + i], "c")
              for i in range(n)]
    if with_pack:
        copies.append((lambda I, O, pos: I[n], lambda I, O, pos: O[dst_off + n], "c"))
    return copies


def _chip_copies(names, shapes, pack_rows, dst_off):
    n = len(names)

    def piece(i, ref, chip):
        shard_axis = BIG[names[i]][1]
        ssize = _sub_shape(names[i], shapes[i])[shard_axis]
        return _view(ref, shard_axis, chip * ssize, ssize)

    copies = []
    for i in range(n):
        for slot, flip in enumerate(FLIPS):
            copies.append((lambda I, O, pos, i=i, flip=flip: piece(i, I[i], _peer_chip(pos, flip)),
                           lambda I, O, pos, i=i, slot=slot: O[dst_off + i].at[slot], flip))
    if pack_rows:
        for slot, flip in enumerate(FLIPS):
            copies.append((lambda I, O, pos: _view(I[n], 0, pos[2] * (pack_rows // 2), pack_rows // 2),
                           lambda I, O, pos, slot=slot: O[dst_off + n].at[slot], flip))
    return copies


class _Exchanges:
    def __init__(self, shards, tiny, kc):
        self.kc = kc
        wb = {n: _cast_into_full(shards[n], kc, BIG[n][1], "cast_" + n) for n in BIG_NAMES}
        self.gathering, self.forwarding, self.pairing, self.reducing = {}, {}, {}, {}
        self.turns = {"c": 0, "ici": 0}
        tiny_copies = [(lambda I, O, pos: I[0], lambda I, O, pos: O[1].at[pos[3]], flip) for flip in FLIPS]
        self.gathering["tiny"] = (0, 0, 2, tiny_copies, None)
        bufs, copies = [tiny, lax.empty((4,) + tiny.shape, F32)], list(tiny_copies)
        for group, names in ARRIVALS.items():
            ici, d2d = _gather_copies(names, [shards[n].shape for n in names])
            self.gathering[group] = (len(bufs), len(copies), len(names), ici, d2d)
            copies += _shifted(ici, len(bufs))
            bufs += [wb[n] for n in names]
        self.started = _exchange_start("gather_start", bufs, copies, self.turns)
        self.zero = self.started[3][0, 0]

    def _arrived(self, group, after):
        buf_off, sem_off, n, ici, _ = self.gathering[group]
        send_sems, recv_sems, bufs, _ = self.started
        return _exchange_wait("gather_%s_wait" % group, send_sems, recv_sems, bufs[buf_off:buf_off + n], ici, after,
                              sem_off)

    def small_params(self, kc):
        tiny, got = self._arrived("tiny", self.started[3])
        return lax.dynamic_update_index_in_dim(got, tiny, kc[0], 0)

    def forward(self, group, after):
        d2d = self.gathering[group][4]
        self.forwarding[group] = (_exchange_start("forward_%s_start" % group, self._arrived(group, after), d2d,
                                                  self.turns), d2d)
        return self.forwarding[group][0][3]

    def weights(self, group, after):
        if group not in self.forwarding:
            after = self.forward(group, after)
        (send_sems, recv_sems, bufs, _), d2d = self.forwarding[group]
        full = _exchange_wait("forward_%s_wait" % group, send_sems, recv_sems, bufs, d2d, after)
        return dict(zip(ARRIVALS[group], full))

    def grads_ready(self, group, grads):
        names = GROUPS[group]
        gs = [grads[n] for n in names]
        land = [lax.empty(_half_shape(n, g.shape), F32) for n, g in zip(names, gs)]
        copies = _pair_copies(names, [g.shape for g in gs], False, len(names))
        started = _exchange_start("pair_%s_start" % group, gs + land, copies, self.turns)
        self.pairing[group] = (started, copies)
        return started[3]

    def grads_send(self, group, after):
        names = GROUPS[group]
        n = len(names)
        (send_sems, recv_sems, bufs, _), copies = self.pairing[group]
        bufs = _exchange_wait("pair_%s_wait" % group, send_sems, recv_sems, bufs, copies, after)
        chip = [_pair_sum(bufs[i], bufs[n + i], self.kc, BIG[names[i]][0], "pair_sum_" + names[i], BF16)
                for i in range(n)]
        shapes = [bufs[i].shape for i in range(n)]
        land = [lax.empty((3,) + _sub_shape(names[i], shapes[i]), BF16) for i in range(n)]
        copies = _chip_copies(names, shapes, 0, n)
        started = _exchange_start("reduce_%s_start" % group, chip + land, copies, self.turns)
        self.reducing[group] = (started, copies)
        return started[3]

    def finish_pack(self, pack):
        kc = self.kc
        prow = pack.shape[0] // 2
        recv = _exchange("reduce_d2d", [pack], [jax.ShapeDtypeStruct(pack.shape, F32)], {}, [],
                         _pair_copies((), [], True, 0))
        chip_pack = _pair_sum(pack, recv[0], kc, None, "pair_sum_pack", F32)
        copies = _chip_copies((), [], pack.shape[0], 1)
        land = lax.empty((3, prow, pack.shape[1]), F32)
        pack_sems_s, pack_sems_r, pack_bufs, after = _exchange_start("reduce_pack_start", [chip_pack, land], copies,
                                                                     self.turns)

        names, chips, recvs = (), [], []
        for group, group_names in GROUPS.items():
            (send_sems, recv_sems, bufs, _), group_copies = self.reducing[group]
            bufs = _exchange_wait("reduce_%s_wait" % group, send_sems, recv_sems, bufs, group_copies, after)
            n = len(group_names)
            names, chips, recvs = names + group_names, chips + bufs[:n], recvs + bufs[n:]
            after = bufs[n]
        total = [_chip_sum(chips[i], recvs[i], kc, BIG[n][1], BIG[n][0], "chip_sum_" + n)
                 for i, n in enumerate(names)]

        def my_half(half_axis, ref, pos):
            hsize = ref.shape[half_axis] // 2
            return _view(ref, half_axis, pos[2] * hsize, hsize)

        swap = [(lambda I, O, pos, i=i, n=n: my_half(BIG[n][0], I[i], pos),
                 lambda I, O, pos, i=i, n=n: my_half(BIG[n][0], O[i], pos), "c") for i, n in enumerate(names)]
        self.swapping = (_exchange_start("swap_start", total, swap, self.turns), swap, names)

        chip_pack, recv_pack = _exchange_wait("reduce_pack_wait", pack_sems_s, pack_sems_r, pack_bufs, copies,
                                              self.swapping[0][3])
        total_pack = _chip_sum(chip_pack, recv_pack, kc, None, 0, "chip_sum_pack")
        swap = [(lambda I, O, pos: my_half(0, I[0], pos), lambda I, O, pos: my_half(0, O[0], pos), "c")]
        return _exchange("swap_pack", [total_pack], [jax.ShapeDtypeStruct(pack.shape, F32)], {0: 0}, [], swap)[0]

    def finish_big(self, after):
        (send_sems, recv_sems, bufs, _), swap, names = self.swapping
        return dict(zip(names, _exchange_wait("swap_wait", send_sems, recv_sems, bufs, swap, after)))


WEIGHTS = ("meta_tokens", "norm_mix_g", "w_in", "conv_w", "ssm_lam_re", "ssm_lam_im", "ssm_log_dt", "ssm_b_re",
           "ssm_b_im", "ssm_c_re", "ssm_c_im", "ssm_d", "ssm_w_glu", "gain_conv_out", "gain_ssm_out", "w_out",
           "norm_ffn_g", "w_up", "ffn_conv_w", "ffn_conv_b", "w_down", "norm_final_g")
TINY_SHARDED = ("meta_tokens", "conv_w", "ffn_conv_w")
REPLICATED = tuple(n for n in WEIGHTS if n not in BIG and n not in TINY_SHARDED)
PACK_COLS = 512


def _pack(arrays, row_mult, cols):
    flat = jnp.concatenate([a.reshape(-1).astype(F32) for a in arrays])
    n = flat.shape[0]
    total = -(-n // (row_mult * cols)) * (row_mult * cols)
    return jnp.concatenate([flat, jnp.zeros((total - n,), F32)]).reshape(total // cols, cols)


def _unpack(packed, shapes):
    flat = packed.reshape(-1)
    out, off = [], 0
    for s in shapes:
        n = math.prod(s)
        out.append(flat[off:off + n].reshape(s))
        off += n
    return out


def kernel(x, meta_tokens, norm_mix_g, w_in, conv_w, ssm_lam_re, ssm_lam_im, ssm_log_dt, ssm_b_re, ssm_b_im, ssm_c_re, ssm_c_im, ssm_d, ssm_w_glu, gain_conv_out, gain_ssm_out, w_out, norm_ffn_g, w_up, ffn_conv_w, ffn_conv_b, w_down, norm_final_g, loss_target, m_meta_tokens, m_norm_mix_g, m_w_in, m_conv_w, m_ssm_lam_re, m_ssm_lam_im, m_ssm_log_dt, m_ssm_b_re, m_ssm_b_im, m_ssm_c_re, m_ssm_c_im, m_ssm_d, m_ssm_w_glu, m_gain_conv_out, m_gain_ssm_out, m_w_out, m_norm_ffn_g, m_w_up, m_ffn_conv_w, m_ffn_conv_b, m_w_down, m_norm_final_g, v_meta_tokens, v_norm_mix_g, v_w_in, v_conv_w, v_ssm_lam_re, v_ssm_lam_im, v_ssm_log_dt, v_ssm_b_re, v_ssm_b_im, v_ssm_c_re, v_ssm_c_im, v_ssm_d, v_ssm_w_glu, v_gain_conv_out, v_gain_ssm_out, v_w_out, v_norm_ffn_g, v_w_up, v_ffn_conv_w, v_ffn_conv_b, v_w_down, v_norm_final_g):
    args = dict(locals())
    w = {n: args[n] for n in WEIGHTS}
    mom = {n: args["m_" + n] for n in WEIGHTS}
    var = {n: args["v_" + n] for n in WEIGHTS}
    kx, ky, kc_ = lax.axis_index("x"), lax.axis_index("y"), lax.axis_index("c")
    chip = 2 * kx + ky
    kc = jnp.stack([chip, kc_]).astype(jnp.int32)

    def squeeze(n, a):
        if n == "meta_tokens":
            return a
        if n == "norm_final_g":
            return a.reshape(1, -1)
        a = a[0]
        return a.reshape(1, -1) if a.ndim == 1 else a

    wl = {n: squeeze(n, w[n]) for n in WEIGHTS}
    ml = {n: squeeze(n, mom[n]) for n in WEIGHTS}
    vl = {n: squeeze(n, var[n]) for n in WEIGHTS}

    tiny = _pack([wl[n] for n in TINY_SHARDED], SUBLANES, LANES)
    ex = _Exchanges({n: wl[n] for n in BIG_NAMES}, tiny, kc)
    tiny_shapes = [wl[n].shape for n in TINY_SHARDED]
    tiny_all = ex.small_params(kc)
    tiny_parts = [_unpack(tiny_all[k], tiny_shapes) for k in range(4)]
    p = {n: wl[n] for n in WEIGHTS if n not in BIG}
    for j, n in enumerate(TINY_SHARDED):
        p[n] = jnp.concatenate([tiny_parts[k][j] for k in range(4)], axis=1)
    p["ssm_log_dt"] = wl["ssm_log_dt"].reshape(-1)

    loss_local, grad_x, grads = _local_step(x[0], loss_target[0], p, ex)

    small_names = REPLICATED + TINY_SHARDED
    small_shapes = [tuple(grads[n].shape) for n in small_names] + [(1,)]
    pack = _pack([grads[n] for n in small_names] + [loss_local.reshape(1)], 2 * 16, PACK_COLS)
    g_pack = ex.finish_pack(pack)
    g_small = dict(zip(small_names + ("loss",), _unpack(g_pack, small_shapes)))
    loss = g_small["loss"][0]
    swapped = ("ssm_b_re", "ssm_b_im")

    def view(n, a):
        if n in swapped:
            return jnp.swapaxes(a, -1, -2)
        return a.reshape(1, -1) if a.ndim == 1 else a

    g = {}
    for n in REPLICATED:
        g[n] = g_small[n].reshape(view(n, w[n]).shape)
    for n in TINY_SHARDED:
        cols = wl[n].shape[1]
        g[n] = lax.dynamic_slice_in_dim(g_small[n], chip * cols, cols, axis=1).reshape(w[n].shape)
    delta, new_m, new_v = {}, {}, {}
    small = [[view(n, d[n]) for n in small_names] for d in (w, mom, var)]
    small.insert(1, [g[n] for n in small_names])
    for d, outs in zip((delta, new_m, new_v), _adamw_whole(*small, "adamw_small")):
        d.update(zip(small_names, outs))
    for d in (g, delta, new_m, new_v):
        d.update({n: jnp.swapaxes(d[n], -1, -2) for n in swapped})
    g_big = ex.finish_big(delta[small_names[0]])
    for n in BIG_NAMES:
        g[n], delta[n], new_m[n], new_v[n] = _adamw(wl[n], g_big[n], ml[n], vl[n], "adamw_" + n)

    def like(n, a):
        return a.reshape(w[n].shape)

    return (loss, grad_x[None], *[like(n, g[n]) for n in WEIGHTS], *[like(n, delta[n]) for n in WEIGHTS],
            *[like(n, new_m[n]) for n in WEIGHTS], *[like(n, new_v[n]) for n in WEIGHTS])
```

```python
import functools
import math

import jax
import jax.numpy as jnp
from jax import lax
from jax.experimental import pallas as pl
from jax.experimental.pallas import tpu as pltpu

F32 = jnp.float32
BF16 = jnp.bfloat16
MESH = pl.DeviceIdType.MESH

N_META = 16
N_GROUPS = 32
GROUP = 16
STATE = 64
RMS_EPS = 1e-6
ADAM_LR = 0.001
ADAM_B1 = 0.9
ADAM_B2 = 0.999
ADAM_EPS = 1e-08
ADAM_WD = 0.01
ADAM_STEP = 10

LANES = 128
SUBLANES = 8
ROW_ALIGN = 128
ROW_TILES = 4
VMEM_LIMIT = 52 * 1024 * 1024
MM_VMEM_BUDGET = 40 * 1024 * 1024
GELU_C = math.sqrt(2.0 / math.pi)
GELU_A = 0.044715


def _cparams(*sem):
    return pltpu.CompilerParams(dimension_semantics=sem, vmem_limit_bytes=VMEM_LIMIT)


def _pick_tile(dim, cap, mult):
    best = None
    for t in range(mult, min(dim, cap) + 1, mult):
        if dim % t == 0:
            best = t
    return best if best is not None else dim


def _mm(a, b, mode, name, out_dtype=F32, acc_in=None, after=None):
    if mode == "tn":
        kdim, m = a.shape
    else:
        m, kdim = a.shape
    n = b.shape[0] if mode == "nt" else b.shape[1]
    tm = _pick_tile(m, 1408, LANES if mode == "tn" else 16)
    tk = _pick_tile(kdim, 2816, LANES)
    nk = kdim // tk
    out_bytes = jnp.dtype(out_dtype).itemsize
    for cap in (1408, 1024, 512, 256, LANES):
        tn = _pick_tile(n, cap, LANES)
        blocks = 2 * (tm * tk * 2 + tk * tn * 2 + tm * tn * out_bytes * (2 if acc_in is not None else 1))
        if blocks + (tm * tn * 4 if nk > 1 else 0) <= MM_VMEM_BUDGET:
            break
    has_acc = acc_in is not None

    def body(*refs):
        if after is not None:
            refs = refs[1:]
        if has_acc:
            a_ref, b_ref, c_ref, o_ref = refs[:4]
            rest = refs[4:]
        else:
            a_ref, b_ref, o_ref = refs[:3]
            c_ref = None
            rest = refs[3:]
        if mode == "nn":
            p = jnp.dot(a_ref[...], b_ref[...], preferred_element_type=F32)
        elif mode == "nt":
            p = lax.dot_general(a_ref[...], b_ref[...], (((1,), (1,)), ((), ())), preferred_element_type=F32)
        else:
            p = lax.dot_general(a_ref[...], b_ref[...], (((0,), (0,)), ((), ())), preferred_element_type=F32)
        if nk == 1:
            if has_acc:
                p = p + c_ref[...]
            o_ref[...] = p.astype(out_dtype)
        else:
            acc_ref = rest[0]
            k = pl.program_id(2)

            @pl.when(k == 0)
            def _():
                acc_ref[...] = p + c_ref[...] if has_acc else p

            @pl.when(k > 0)
            def _():
                acc_ref[...] += p

            @pl.when(k == nk - 1)
            def _():
                o_ref[...] = acc_ref[...].astype(out_dtype)

    if mode == "tn":
        a_spec = pl.BlockSpec((tk, tm), lambda i, j, k: (k, i))
    else:
        a_spec = pl.BlockSpec((tm, tk), lambda i, j, k: (i, k))
    if mode == "nt":
        b_spec = pl.BlockSpec((tn, tk), lambda i, j, k: (j, k))
    else:
        b_spec = pl.BlockSpec((tk, tn), lambda i, j, k: (k, j))
    o_spec = pl.BlockSpec((tm, tn), lambda i, j, k: (i, j))
    in_specs = [a_spec, b_spec] + ([o_spec] if has_acc else [])
    args = (a, b) + ((acc_in,) if has_acc else ())
    if after is not None:
        in_specs = [pl.BlockSpec(memory_space=pl.ANY)] + in_specs
        args = (after,) + args
    return pl.pallas_call(
        body, name=name, grid=(m // tm, n // tn, nk),
        in_specs=in_specs, out_specs=o_spec,
        out_shape=jax.ShapeDtypeStruct((m, n), out_dtype),
        scratch_shapes=[pltpu.VMEM((tm, tn), F32)] if nk > 1 else [],
        compiler_params=_cparams("parallel", "parallel", "arbitrary"),
    )(*args)


def _mm_rows(a, b, mode, name, ins, outs, epilogue, scratch=()):
    m, kdim = a.shape
    n = b.shape[0] if mode == "nt" else b.shape[1]
    tm = m // ROW_TILES
    tk = _pick_tile(kdim, 2816, LANES)
    nk = kdim // tk
    ni, no = len(ins), len(outs)

    def body(*refs):
        a_ref, b_ref = refs[:2]
        in_refs, out_refs, rest = refs[2:2 + ni], refs[2 + ni:2 + ni + no], refs[2 + ni + no:]
        i = pl.program_id(0)
        if mode == "nn":
            p = jnp.dot(a_ref[...], b_ref[...], preferred_element_type=F32)
        else:
            p = lax.dot_general(a_ref[...], b_ref[...], (((1,), (1,)), ((), ())), preferred_element_type=F32)
        if nk == 1:
            epilogue(p, i, in_refs, out_refs, rest)
        else:
            acc_ref = rest[0]
            k = pl.program_id(1)

            @pl.when(k == 0)
            def _():
                acc_ref[...] = p

            @pl.when(k > 0)
            def _():
                acc_ref[...] += p

            @pl.when(k == nk - 1)
            def _():
                epilogue(acc_ref[...], i, in_refs, out_refs, rest[1:])

    def spec(shape, kind):
        if kind == "rows":
            return pl.BlockSpec((tm,) + tuple(shape[1:]), lambda i, k: (i,) + (0,) * (len(shape) - 1))
        if kind == "whole":
            return pl.BlockSpec(tuple(shape), lambda i, k: (0,) * len(shape))
        return pl.BlockSpec(memory_space=pl.ANY)

    a_spec = pl.BlockSpec((tm, tk), lambda i, k: (i, k))
    b_spec = pl.BlockSpec((n, tk), lambda i, k: (0, k)) if mode == "nt" else pl.BlockSpec((tk, n), lambda i, k: (k, 0))
    return pl.pallas_call(
        body, name=name, grid=(ROW_TILES, nk),
        in_specs=[a_spec, b_spec] + [spec(x.shape, kind) for x, kind in ins],
        out_specs=[spec(shape, kind) for shape, _, kind in outs],
        out_shape=[jax.ShapeDtypeStruct(shape, dtype) for shape, dtype, _ in outs],
        scratch_shapes=([pltpu.VMEM((tm, n), F32)] if nk > 1 else []) + list(scratch),
        compiler_params=_cparams("arbitrary", "arbitrary"),
    )(a, b, *[x for x, _ in ins])


def _rows(shape_cols, tr, dtype=None):
    return pl.BlockSpec((tr, shape_cols), lambda i: (i, 0))


def _const(shape):
    return pl.BlockSpec(shape, lambda i: (0,) * len(shape))


def _rms(x):
    return lax.rsqrt(jnp.mean(x * x, axis=-1, keepdims=True) + RMS_EPS)


def _rms_bwd(x, r, g, dy):
    xn = x * r
    dxn = dy * g
    dx = r * (dxn - xn * jnp.mean(dxn * xn, axis=-1, keepdims=True))
    return dx, dy * xn


def _gelu(y):
    return 0.5 * y * (1.0 + jnp.tanh(GELU_C * (y + GELU_A * y * y * y)))


def _gelu_grad(y):
    t = jnp.tanh(GELU_C * (y + GELU_A * y * y * y))
    return 0.5 * (1.0 + t) + 0.5 * y * (1.0 - t * t) * GELU_C * (1.0 + 3.0 * GELU_A * y * y)


def _sigmoid(z):
    return 1.0 / (1.0 + jnp.exp(-z))


def _proj_res_norm(a, w, h, g, after, name):
    def epilogue(p, i, ins, outs, _):
        x = ins[0][...] + p
        outs[0][...] = x
        outs[1][...] = (x * _rms(x) * ins[1][...]).astype(BF16)

    return _mm_rows(a, w, "nn", name, [(h, "rows"), (g, "whole"), (after, "hbm")],
                    [(h.shape, F32, "rows"), (h.shape, BF16, "rows")], epilogue)


def _proj_norm_bwd(da, w, h, g, dres, after, name):
    d = h.shape[1]

    def epilogue(p, i, ins, outs, _):
        x = ins[0][...]
        dx, dgs = _rms_bwd(x, _rms(x), ins[1][...], p)
        dh = ins[2][...] + dx
        outs[0][...] = dh
        outs[1][...] = dh.astype(BF16)

        @pl.when(i == 0)
        def _():
            outs[2][...] = jnp.zeros_like(outs[2])

        outs[2][...] += jnp.sum(dgs, axis=0, keepdims=True)

    return _mm_rows(da, w, "nt", name, [(h, "rows"), (g, "whole"), (dres, "rows"), (after, "hbm")],
                    [(h.shape, F32, "rows"), (h.shape, BF16, "rows"), ((1, d), F32, "whole")], epilogue)


def _proj_input_norm_bwd(da, w, h, g, dres, after, n_real, name):
    tp, d = h.shape
    tr = tp // ROW_TILES

    def epilogue(p, i, ins, outs, scratch):
        h_ref, g_ref, dres_ref, _ = ins
        dx_ref, dmeta_ref, dg_ref = outs
        stage, sem = scratch
        x = h_ref[...]
        dx, dgs = _rms_bwd(x, _rms(x), g_ref[...], p)
        stage[...] = dres_ref[...] + dx

        @pl.when(i == 0)
        def _():
            dg_ref[...] = jnp.zeros_like(dg_ref)
            dmeta_ref[...] = stage[:N_META, :]

        dg_ref[...] += jnp.sum(dgs, axis=0, keepdims=True)
        for t in range(ROW_TILES):
            lo, hi = max(t * tr, N_META), min((t + 1) * tr, n_real)
            if hi > lo:
                @pl.when(i == t)
                def _(t=t, lo=lo, hi=hi):
                    cp = pltpu.make_async_copy(stage.at[pl.ds(lo - t * tr, hi - lo), :],
                                               dx_ref.at[pl.ds(lo - N_META, hi - lo), :], sem)
                    cp.start()
                    cp.wait()

    return _mm_rows(da, w, "nt", name, [(h, "rows"), (g, "whole"), (dres, "rows"), (after, "hbm")],
                    [((n_real - N_META, d), F32, "hbm"), ((N_META, d), F32, "whole"), ((1, d), F32, "whole")],
                    epilogue, scratch=[pltpu.VMEM((tr, d), F32), pltpu.SemaphoreType.DMA])


def _load_token_rows(tok_hbm, buf, sem, tr, n_real, head=None, wait=False, i=None):
    i = pl.program_id(0) if i is None else i
    for t in range(ROW_TILES):
        base = t * tr
        lo, hi = max(base, N_META), min(base + tr, n_real)

        @pl.when(i == t)
        def _(base=base, lo=lo, hi=hi):
            if hi > lo:
                cp = pltpu.make_async_copy(tok_hbm.at[pl.ds(lo - N_META, hi - lo), :],
                                           buf.at[pl.ds(lo - base, hi - lo), :], sem)
                if wait:
                    cp.wait()
                    return
                cp.start()
            if wait:
                return
            if base < N_META:
                buf[0:N_META - base, :] = (jnp.zeros((N_META - base, buf.shape[1]), F32) if head is None
                                           else head[base:N_META, :])
            if hi < base + tr:
                buf[max(hi, base) - base:tr, :] = jnp.zeros((base + tr - max(hi, base), buf.shape[1]), F32)


def _input_norm_fwd(x, meta, g, tp, name):
    seq, d = x.shape
    tr = tp // ROW_TILES
    n_real = N_META + seq

    def body(x_hbm, meta_ref, g_ref, h_ref, hn_ref, buf, sem):
        _load_token_rows(x_hbm, buf, sem, tr, n_real, head=meta_ref)
        _load_token_rows(x_hbm, buf, sem, tr, n_real, wait=True)
        h = buf[...]
        h_ref[...] = h
        hn_ref[...] = (h * _rms(h) * g_ref[...]).astype(BF16)

    return pl.pallas_call(
        body, name=name, grid=(ROW_TILES,),
        in_specs=[pl.BlockSpec(memory_space=pl.ANY), _const((N_META, d)), _const((1, d))],
        out_specs=[_rows(d, tr), _rows(d, tr)],
        out_shape=[jax.ShapeDtypeStruct((tp, d), F32), jax.ShapeDtypeStruct((tp, d), BF16)],
        scratch_shapes=[pltpu.VMEM((tr, d), F32), pltpu.SemaphoreType.DMA],
        compiler_params=_cparams("arbitrary"))(x, meta, g)


def _proj_loss_bwd(act, w, h1, target, g, n_real, name):
    tp, d = h1.shape
    tr = tp // ROW_TILES

    def epilogue(p, i, ins, outs, scratch):
        h1_ref, t_hbm, g_ref = ins
        loss_ref, dh_ref, dhb_ref, dg_ref = outs
        t_buf, sem = scratch
        _load_token_rows(t_hbm, t_buf, sem, tr, n_real, i=i)
        x = h1_ref[...] + p
        r = _rms(x)
        row = i * tr + lax.broadcasted_iota(jnp.int32, (tr, d), 0)
        valid = (row >= N_META) & (row < n_real)
        _load_token_rows(t_hbm, t_buf, sem, tr, n_real, wait=True, i=i)
        e = jnp.where(valid, x * r * g_ref[...] - t_buf[...], 0.0)
        dx, dgs = _rms_bwd(x, r, g_ref[...], e * (1.0 / d))
        dh_ref[...] = dx
        dhb_ref[...] = dx.astype(BF16)

        @pl.when(i == 0)
        def _():
            dg_ref[...] = jnp.zeros_like(dg_ref)
            loss_ref[...] = jnp.zeros_like(loss_ref)

        dg_ref[...] += jnp.sum(dgs, axis=0, keepdims=True)
        loss_ref[...] += (0.5 / d) * jnp.sum(jnp.sum(e * e, axis=0, keepdims=True), axis=1, keepdims=True)

    return _mm_rows(act, w, "nn", name, [(h1, "rows"), (target, "hbm"), (g, "whole")],
                    [((1, LANES), F32, "whole"), ((tp, d), F32, "rows"), ((tp, d), BF16, "rows"),
                     ((1, d), F32, "whole")],
                    epilogue, scratch=[pltpu.VMEM((tr, d), F32), pltpu.SemaphoreType.DMA])


def _mix_fwd(co, y, z, gc, gs, name):
    tp, dh = co.shape
    tr = tp // ROW_TILES

    def body(co_ref, y_ref, z_ref, gc_ref, gs_ref, m_ref):
        c = co_ref[...]
        m_ref[:, :dh] = (c * _rms(c) * gc_ref[...]).astype(BF16)
        so = _gelu(y_ref[...]) * _sigmoid(z_ref[...])
        m_ref[:, dh:] = (so * _rms(so) * gs_ref[...]).astype(BF16)

    return pl.pallas_call(
        body, name=name, grid=(ROW_TILES,),
        in_specs=[_rows(dh, tr)] * 3 + [_const((1, dh))] * 2,
        out_specs=_rows(2 * dh, tr),
        out_shape=jax.ShapeDtypeStruct((tp, 2 * dh), BF16),
        compiler_params=_cparams("parallel"))(co, y, z, gc, gs)


def _proj_mix_bwd(dh1b, w, co, y, z, gc, gs, name):
    tp, dh = co.shape

    def epilogue(p, i, ins, outs, _):
        co_ref, y_ref, z_ref, gc_ref, gs_ref = ins
        dco_ref, dz_ref, dgp_ref, dgc_ref, dgs_ref = outs
        c = co_ref[...]
        dco, dgc = _rms_bwd(c, _rms(c), gc_ref[...], p[:, :dh])
        dco_ref[...] = dco
        gl = _gelu(y_ref[...])
        sg = _sigmoid(z_ref[...])
        so = gl * sg
        dso, dgs = _rms_bwd(so, _rms(so), gs_ref[...], p[:, dh:])
        dz_ref[...] = (dso * gl * sg * (1.0 - sg)).astype(BF16)
        dgp_ref[...] = dso * sg

        @pl.when(i == 0)
        def _():
            dgc_ref[...] = jnp.zeros_like(dgc_ref)
            dgs_ref[...] = jnp.zeros_like(dgs_ref)

        dgc_ref[...] += jnp.sum(dgc, axis=0, keepdims=True)
        dgs_ref[...] += jnp.sum(dgs, axis=0, keepdims=True)

    return _mm_rows(dh1b, w, "nt", name,
                    [(co, "rows"), (y, "rows"), (z, "rows"), (gc, "whole"), (gs, "whole")],
                    [((tp, dh), F32, "rows"), ((tp, dh), BF16, "rows"), ((tp, dh), F32, "rows"),
                     ((1, dh), F32, "whole"), ((1, dh), F32, "whole")], epilogue)


def _shift_down(x, k):
    row = lax.broadcasted_iota(jnp.int32, x.shape, 0)
    return jnp.where(row >= k, pltpu.roll(x, k, 0), 0.0)


def _shift_up(x, k):
    n = x.shape[0]
    row = lax.broadcasted_iota(jnp.int32, x.shape, 0)
    return jnp.where(row < n - k, pltpu.roll(x, n - k, 0), 0.0)


def _dwconv(x, w_ref):
    return w_ref[2:3, :] * x + w_ref[1:2, :] * _shift_down(x, 1) + w_ref[0:1, :] * _shift_down(x, 2)


def _dwconv_bwd(x, dy, w_ref):
    dx = w_ref[2:3, :] * dy + w_ref[1:2, :] * _shift_up(dy, 1) + w_ref[0:1, :] * _shift_up(dy, 2)
    dw = jnp.concatenate([jnp.sum(dy * _shift_down(x, 2), axis=0, keepdims=True),
                          jnp.sum(dy * _shift_down(x, 1), axis=0, keepdims=True),
                          jnp.sum(dy * x, axis=0, keepdims=True)], axis=0)
    return dx, dw


def _interleave(dst, src):
    seg_rows = src.shape[0] // SUBLANES
    for seg in range(SUBLANES):
        dst[pl.ds(seg, seg_rows, stride=SUBLANES), :] = src[seg * seg_rows:(seg + 1) * seg_rows, :]


def _deinterleave(dst, src):
    seg_rows = src.shape[0] // SUBLANES
    for seg in range(SUBLANES):
        dst[seg * seg_rows:(seg + 1) * seg_rows, :] = src[pl.ds(seg, seg_rows, stride=SUBLANES), :]


def _segment_shift(x, reverse):
    row = lax.broadcasted_iota(jnp.int32, x.shape, 0)
    if reverse:
        return jnp.where(row < SUBLANES - 1, pltpu.roll(x, SUBLANES - 1, 0), 0.0)
    return jnp.where(row >= 1, pltpu.roll(x, 1, 0), 0.0)


def _scan(s_re, s_im, pw_ref, reverse, pair=None):
    n_steps = s_re.shape[0] // SUBLANES
    n_strips = s_re.shape[1] // LANES
    sign = -1.0 if reverse else 1.0
    strips = [slice(st * LANES, (st + 1) * LANES) for st in range(n_strips)]

    def rows_of(j):
        step = (n_steps - 1 - j) if reverse else j
        return pl.ds(pl.multiple_of(step * SUBLANES, SUBLANES), SUBLANES)

    a = [(jnp.broadcast_to(pw_ref[0, 0:1, lanes], (SUBLANES, LANES)),
          sign * jnp.broadcast_to(pw_ref[1, 0:1, lanes], (SUBLANES, LANES))) for lanes in strips]

    def local(i, carry):
        for half in range(2):
            rows = rows_of(2 * i + half)
            out = []
            for st, lanes in enumerate(strips):
                (ar, ai), cr, ci = a[st], carry[2 * st], carry[2 * st + 1]
                xr = s_re[rows, lanes] + (ar * cr - ai * ci)
                xi = s_im[rows, lanes] + (ar * ci + ai * cr)
                s_re[rows, lanes] = xr
                s_im[rows, lanes] = xi
                out += [xr, xi]
            carry = tuple(out)
        return carry

    zero = jnp.zeros((SUBLANES, LANES), F32)
    ends = lax.fori_loop(0, n_steps // 2, local, (zero,) * (2 * n_strips))

    entering = []
    row = lax.broadcasted_iota(jnp.int32, (SUBLANES, LANES), 0)
    for st, lanes in enumerate(strips):
        tr, ti = ends[2 * st], ends[2 * st + 1]
        mr = jnp.broadcast_to(pw_ref[0, n_steps - 1:n_steps, lanes], (SUBLANES, LANES))
        mi = sign * jnp.broadcast_to(pw_ref[1, n_steps - 1:n_steps, lanes], (SUBLANES, LANES))
        for k in (1, 2, 4):
            keep = (row < SUBLANES - k) if reverse else (row >= k)
            rr = jnp.where(keep, pltpu.roll(tr, SUBLANES - k if reverse else k, 0), 0.0)
            ri = jnp.where(keep, pltpu.roll(ti, SUBLANES - k if reverse else k, 0), 0.0)
            tr, ti = tr + (mr * rr - mi * ri), ti + (mr * ri + mi * rr)
            mr, mi = mr * mr - mi * mi, 2.0 * mr * mi
        entering += [_segment_shift(tr, reverse), _segment_shift(ti, reverse)]

    def fix(i, carry):
        carry, sums = carry[:2 * n_strips], carry[2 * n_strips:]
        for half in range(2):
            j = 2 * i + half
            rows = rows_of(j)
            out, acc = [], []
            for st, lanes in enumerate(strips):
                (ar, ai), cr, ci = a[st], carry[2 * st], carry[2 * st + 1]
                cr, ci = ar * cr - ai * ci, ar * ci + ai * cr
                xr = s_re[rows, lanes] + cr
                xi = s_im[rows, lanes] + ci
                s_re[rows, lanes] = xr
                s_im[rows, lanes] = xi
                out += [cr, ci]
                if pair is not None:
                    p_rows = rows_of(jnp.minimum(j + 1, n_steps - 1))
                    keep = (j < n_steps - 1).astype(F32)
                    pr = pair[0][p_rows, lanes] * keep
                    pi = pair[1][p_rows, lanes] * keep
                    acc += [sums[2 * st] + (xr * pr + xi * pi), sums[2 * st + 1] + (xi * pr - xr * pi)]
            carry, sums = tuple(out), tuple(acc)
        return carry + sums

    n_sums = 0 if pair is None else 2 * n_strips
    out = lax.fori_loop(0, n_steps // 2, fix, tuple(entering) + (zero,) * n_sums)
    return out[2 * n_strips:]


def _seq_fwd(proj, conv_w, bc_re, bc_im, cc_re, cc_im, dskip, a_pow, name):
    tp = proj.shape[0]
    dh = proj.shape[1] // 4
    nq = dh // LANES
    sw = STATE * N_GROUPS // nq

    def body(b_ref, c_ref, v_ref, u_ref, w_ref, bre_ref, bim_ref, cre_ref, cim_ref, d_ref, pw_ref,
             co_ref, y_ref, g_ref, s_re, s_im, u_il, y_il):
        co_ref[...] = b_ref[...] * _dwconv(c_ref[...] * v_ref[...], w_ref)
        _interleave(u_il, u_ref)
        ub = u_il[...].astype(BF16)
        s_re[...] = jnp.dot(ub, bre_ref[...], preferred_element_type=F32)
        s_im[...] = jnp.dot(ub, bim_ref[...], preferred_element_type=F32)
        _scan(s_re, s_im, pw_ref, False)
        y_il[...] = (jnp.dot(s_re[...].astype(BF16), cre_ref[...], preferred_element_type=F32)
                     - jnp.dot(s_im[...].astype(BF16), cim_ref[...], preferred_element_type=F32))
        _deinterleave(y_ref, y_il)
        y = y_ref[...] + d_ref[...] * u_ref[...]
        y_ref[...] = y
        g_ref[...] = _gelu(y).astype(BF16)

    col = lambda off: pl.BlockSpec((tp, LANES), lambda q, off=off: (0, off * nq + q))
    blk = pl.BlockSpec((tp, LANES), lambda q: (0, q))
    return pl.pallas_call(
        body, name=name, grid=(nq,),
        in_specs=[col(0), col(1), col(2), col(3),
                  pl.BlockSpec((3, LANES), lambda q: (0, q)),
                  pl.BlockSpec((LANES, sw), lambda q: (0, q)), pl.BlockSpec((LANES, sw), lambda q: (0, q)),
                  pl.BlockSpec((sw, LANES), lambda q: (q, 0)), pl.BlockSpec((sw, LANES), lambda q: (q, 0)),
                  pl.BlockSpec((1, LANES), lambda q: (0, q)),
                  pl.BlockSpec((2, tp // SUBLANES, sw), lambda q: (0, 0, q))],
        out_specs=[blk, blk, blk, pl.BlockSpec((tp, sw), lambda q: (0, q)), pl.BlockSpec((tp, sw), lambda q: (0, q))],
        out_shape=[jax.ShapeDtypeStruct((tp, dh), F32), jax.ShapeDtypeStruct((tp, dh), F32),
                   jax.ShapeDtypeStruct((tp, dh), BF16),
                   jax.ShapeDtypeStruct((tp, nq * sw), F32), jax.ShapeDtypeStruct((tp, nq * sw), F32)],
        scratch_shapes=[pltpu.VMEM((tp, LANES), F32), pltpu.VMEM((tp, LANES), F32)],
        compiler_params=_cparams("parallel"),
    )(proj, proj, proj, proj, conv_w, bc_re, bc_im, cc_re, cc_im, dskip, a_pow)


def _conv_bwd(proj, dco, conv_w, name):
    tp = proj.shape[0]
    dh = proj.shape[1] // 4
    nq = dh // LANES

    def body(b_ref, c_ref, v_ref, dco_ref, w_ref, dproj_ref, dw_ref, stage, sem):
        q = pl.program_id(0)
        cg = c_ref[...]
        vg = v_ref[...]
        cv = cg * vg
        dco_v = dco_ref[...]
        dcv, dw = _dwconv_bwd(cv, dco_v * b_ref[...], w_ref)
        dw_ref[...] = dw
        stage[0] = (dco_v * _dwconv(cv, w_ref)).astype(BF16)
        stage[1] = (dcv * vg).astype(BF16)
        stage[2] = (dcv * cg).astype(BF16)
        copies = [pltpu.make_async_copy(stage.at[p], dproj_ref.at[:, pl.ds((p * nq + q) * LANES, LANES)], sem.at[p])
                  for p in range(3)]
        for cp in copies:
            cp.start()
        for cp in copies:
            cp.wait()

    col = lambda off: pl.BlockSpec((tp, LANES), lambda q, off=off: (0, off * nq + q))
    return pl.pallas_call(
        body, name=name, grid=(nq,),
        in_specs=[col(0), col(1), col(2), pl.BlockSpec((tp, LANES), lambda q: (0, q)),
                  pl.BlockSpec((3, LANES), lambda q: (0, q))],
        out_specs=[pl.BlockSpec(memory_space=pl.ANY), pl.BlockSpec((3, LANES), lambda q: (0, q))],
        out_shape=[jax.ShapeDtypeStruct((tp, 4 * dh), BF16), jax.ShapeDtypeStruct((3, dh), F32)],
        scratch_shapes=[pltpu.VMEM((3, tp, LANES), BF16), pltpu.SemaphoreType.DMA((3,))],
        compiler_params=_cparams("arbitrary"),
    )(proj, proj, proj, dco, conv_w)


def _ssm_bwd(proj, y, dg, dproj, states, bc_re, bc_im, cc_re, cc_im, dskip, a_pow, name):
    tp = proj.shape[0]
    dh = proj.shape[1] // 4
    nq = dh // LANES
    sw = STATE * N_GROUPS // nq

    def body(u_ref, y_ref, dg_ref, dproj_in, s_re, s_im, bre_ref, bim_ref, cre_ref, cim_ref, d_ref, pw_ref,
             dproj_ref, dbre_ref, dbim_ref, dcre_ref, dcim_ref, dd_ref, dar_ref, dai_ref,
             l_re, l_im, a_il, b_il, stage, sem):
        del dproj_in
        q = pl.program_id(0)
        nt = (((1,), (1,)), ((), ()))
        tn = (((0,), (0,)), ((), ()))
        _interleave(a_il, u_ref)
        ub = a_il[...].astype(BF16)
        dy_rows = dg_ref[...] * _gelu_grad(y_ref[...])
        dd_ref[...] = jnp.sum(dy_rows * u_ref[...], axis=0, keepdims=True)
        _interleave(b_il, dy_rows)
        dy = b_il[...]
        dyb = dy.astype(BF16)
        l_re[...] = lax.dot_general(dyb, cre_ref[...], nt, preferred_element_type=F32)
        l_im[...] = -lax.dot_general(dyb, cim_ref[...], nt, preferred_element_type=F32)
        dcre_ref[...] = lax.dot_general(s_re[...].astype(BF16), dyb, tn, preferred_element_type=F32)
        dcim_ref[...] = -lax.dot_general(s_im[...].astype(BF16), dyb, tn, preferred_element_type=F32)
        sums = _scan(l_re, l_im, pw_ref, True, pair=(s_re, s_im))
        rest = tp - SUBLANES
        for st in range(sw // LANES):
            lanes = slice(st * LANES, (st + 1) * LANES)
            lr0, li0 = l_re[:SUBLANES, lanes], l_im[:SUBLANES, lanes]
            pr0, pi0 = _segment_shift(s_re[rest:, lanes], False), _segment_shift(s_im[rest:, lanes], False)
            dar_ref[:, lanes] = jnp.sum(sums[2 * st] + (lr0 * pr0 + li0 * pi0), axis=0, keepdims=True)
            dai_ref[:, lanes] = jnp.sum(sums[2 * st + 1] + (li0 * pr0 - lr0 * pi0), axis=0, keepdims=True)
        lrb = l_re[...].astype(BF16)
        lib = l_im[...].astype(BF16)
        a_il[...] = (dy * d_ref[...] + lax.dot_general(lrb, bre_ref[...], nt, preferred_element_type=F32)
                     + lax.dot_general(lib, bim_ref[...], nt, preferred_element_type=F32))
        _deinterleave(b_il, a_il)
        stage[...] = b_il[...].astype(BF16)
        dbre_ref[...] = lax.dot_general(ub, lrb, tn, preferred_element_type=F32)
        dbim_ref[...] = lax.dot_general(ub, lib, tn, preferred_element_type=F32)
        cp = pltpu.make_async_copy(stage, dproj_ref.at[:, pl.ds((3 * nq + q) * LANES, LANES)], sem)
        cp.start()
        cp.wait()

    blk = pl.BlockSpec((tp, LANES), lambda q: (0, q))
    bspec = pl.BlockSpec((LANES, sw), lambda q: (0, q))
    cspec = pl.BlockSpec((sw, LANES), lambda q: (q, 0))
    tspec = pl.BlockSpec((2, tp // SUBLANES, sw), lambda q: (0, 0, q))
    nstate = STATE * N_GROUPS
    return pl.pallas_call(
        body, name=name, grid=(nq,),
        in_specs=[pl.BlockSpec((tp, LANES), lambda q: (0, 3 * nq + q)), blk, blk, pl.BlockSpec(memory_space=pl.ANY),
                  pl.BlockSpec((tp, sw), lambda q: (0, q)), pl.BlockSpec((tp, sw), lambda q: (0, q)),
                  bspec, bspec, cspec, cspec, pl.BlockSpec((1, LANES), lambda q: (0, q)), tspec],
        out_specs=[pl.BlockSpec(memory_space=pl.ANY), bspec, bspec, cspec, cspec,
                   pl.BlockSpec((1, LANES), lambda q: (0, q)),
                   pl.BlockSpec((1, sw), lambda q: (0, q)), pl.BlockSpec((1, sw), lambda q: (0, q))],
        out_shape=[jax.ShapeDtypeStruct((tp, 4 * dh), BF16),
                   jax.ShapeDtypeStruct((LANES, nstate), F32), jax.ShapeDtypeStruct((LANES, nstate), F32),
                   jax.ShapeDtypeStruct((nstate, LANES), F32), jax.ShapeDtypeStruct((nstate, LANES), F32),
                   jax.ShapeDtypeStruct((1, dh), F32),
                   jax.ShapeDtypeStruct((1, nstate), F32), jax.ShapeDtypeStruct((1, nstate), F32)],
        input_output_aliases={3: 0},
        scratch_shapes=[pltpu.VMEM((tp, sw), F32)] * 2 + [pltpu.VMEM((tp, LANES), F32)] * 2
        + [pltpu.VMEM((tp, LANES), BF16), pltpu.SemaphoreType.DMA],
        compiler_params=_cparams("arbitrary"),
    )(proj, y, dg, dproj, states[0], states[1], bc_re, bc_im, cc_re, cc_im, dskip, a_pow)


FFN_TILE = 256
FFN_ROW_TILE = 128
FFN_ROWS = 32


def _window(x_ref, before, r0, rows, cols):
    if r0 == 0:
        return jnp.concatenate([before, x_ref[0:rows, cols]], axis=0)
    return x_ref[r0 - SUBLANES:r0 + rows, cols]


def _taps(window):
    return window[SUBLANES:], pltpu.roll(window, 1, 0)[SUBLANES:], pltpu.roll(window, 2, 0)[SUBLANES:]


def _conv_taps(taps, w):
    return w[2] * taps[0] + w[1] * taps[1] + w[0] * taps[2]


def _ffn_specs(tp, dff):
    pieces = FFN_ROW_TILE // SUBLANES
    main = lambda half: pl.BlockSpec((FFN_ROW_TILE, dff), lambda i: (i, half))
    prev = lambda half: pl.BlockSpec((SUBLANES, dff), lambda i: (jnp.maximum(i * pieces - 1, 0), half))
    nxt = lambda half: pl.BlockSpec((SUBLANES, dff), lambda i: (jnp.minimum((i + 1) * pieces, tp // SUBLANES - 1), half))
    par = lambda r, half: pl.BlockSpec((r, dff), lambda i: (0, half))
    return main, prev, nxt, par


def _ffn_act(up, fw, fb, name):
    tp, two_ff = up.shape
    dff = two_ff // 2
    tr, tc, rows = FFN_ROW_TILE, FFN_TILE, FFN_ROWS

    def body(ua_ref, uv_ref, pa_ref, pv_ref, wa_ref, wv_ref, ba_ref, bv_ref, act_ref):
        first = pl.program_id(0) == 0
        for c0 in range(0, dff, tc):
            cols = slice(c0, c0 + tc)
            wa, wv = [[w_ref[k:k + 1, cols] for k in range(3)] for w_ref in (wa_ref, wv_ref)]
            ba, bv = ba_ref[:, cols], bv_ref[:, cols]
            before_a, before_v = [jnp.where(first, 0.0, p_ref[:, cols]) for p_ref in (pa_ref, pv_ref)]
            for r0 in range(0, tr, rows):
                a = _conv_taps(_taps(_window(ua_ref, before_a, r0, rows, cols)), wa) + ba
                v = _conv_taps(_taps(_window(uv_ref, before_v, r0, rows, cols)), wv) + bv
                act_ref[r0:r0 + rows, cols] = (a * _sigmoid(a) * v).astype(BF16)

    main, prev, _, par = _ffn_specs(tp, dff)
    return pl.pallas_call(
        body, name=name, grid=(tp // tr,),
        in_specs=[main(0), main(1), prev(0), prev(1), par(3, 0), par(3, 1), par(1, 0), par(1, 1)],
        out_specs=main(0),
        out_shape=jax.ShapeDtypeStruct((tp, dff), BF16),
        compiler_params=_cparams("parallel"))(up, up, up, up, fw, fw, fb, fb)


def _ffn_bwd(up, dact, fw, fb, name):
    tp, two_ff = up.shape
    dff = two_ff // 2
    tr, tc, rows = FFN_ROW_TILE, FFN_TILE, FFN_ROWS
    n = tp // tr
    n_e = rows + SUBLANES

    def body(ua_ref, uv_ref, da_ref, pa_ref, pv_ref, na_ref, nv_ref, nd_ref, wa_ref, wv_ref, ba_ref, bv_ref,
             dup_ref, dwa_ref, dwv_ref, dba_ref, dbv_ref):
        i = pl.program_id(0)
        first, last = i == 0, i == n - 1
        sums = ((dwa_ref, dba_ref), (dwv_ref, dbv_ref))

        @pl.when(first)
        def _():
            for dw_ref, db_ref in sums:
                dw_ref[...] = jnp.zeros_like(dw_ref)
                db_ref[...] = jnp.zeros_like(db_ref)

        def gate_bwd(taps, dact_v, w, bias):
            a, v = [_conv_taps(taps[s], w[s]) + bias[s] for s in range(2)]
            sg = _sigmoid(a)
            return [dact_v * v * sg * (1.0 + a * (1.0 - sg)), dact_v * a * sg]

        fold = lambda x: sum(x[r:r + SUBLANES] for r in range(0, rows, SUBLANES))
        for c0 in range(0, dff, tc):
            cols = slice(c0, c0 + tc)
            w = [[w_ref[k:k + 1, cols] for k in range(3)] for w_ref in (wa_ref, wv_ref)]
            bias = [ba_ref[:, cols], bv_ref[:, cols]]
            before = [jnp.where(first, 0.0, p_ref[:, cols]) for p_ref in (pa_ref, pv_ref)]
            after = [_taps(jnp.concatenate([x_ref[tr - SUBLANES:tr, cols], n_ref[:, cols]], axis=0))
                     for x_ref, n_ref in ((ua_ref, na_ref), (uv_ref, nv_ref))]
            head = [jnp.where(last, 0.0, d) for d in gate_bwd(after, nd_ref[:, cols], w, bias)]
            piece = jnp.zeros((SUBLANES, tc), F32)
            acc = [[piece] * 4 for _ in range(2)]
            for r0 in reversed(range(0, tr, rows)):
                taps = [_taps(_window(x_ref, before[s], r0, rows, cols)) for s, x_ref in enumerate((ua_ref, uv_ref))]
                d = gate_bwd(taps, da_ref[r0:r0 + rows, cols], w, bias)
                for s in range(2):
                    de = jnp.concatenate([d[s], head[s]], axis=0)
                    dx = (w[s][2] * d[s] + w[s][1] * pltpu.roll(de, n_e - 1, 0)[:rows]
                          + w[s][0] * pltpu.roll(de, n_e - 2, 0)[:rows])
                    dup_ref[r0:r0 + rows, s * dff + c0:s * dff + c0 + tc] = dx.astype(BF16)
                    for k in range(3):
                        acc[s][k] = acc[s][k] + fold(d[s] * taps[s][2 - k])
                    acc[s][3] = acc[s][3] + fold(d[s])
                    head[s] = d[s][:SUBLANES]
            for s, (dw_ref, db_ref) in enumerate(sums):
                dw_ref[:, cols] = dw_ref[:, cols] + jnp.concatenate(
                    [jnp.sum(x, axis=0, keepdims=True) for x in acc[s][:3]], axis=0)
                db_ref[:, cols] = db_ref[:, cols] + jnp.sum(acc[s][3], axis=0, keepdims=True)

    main, prev, nxt, par = _ffn_specs(tp, dff)
    whole = lambda r: pl.BlockSpec((r, dff), lambda i: (0, 0))
    return pl.pallas_call(
        body, name=name, grid=(n,),
        in_specs=[main(0), main(1), main(0), prev(0), prev(1), nxt(0), nxt(1), nxt(0),
                  par(3, 0), par(3, 1), par(1, 0), par(1, 1)],
        out_specs=[pl.BlockSpec((tr, two_ff), lambda i: (i, 0)), whole(3), whole(3), whole(1), whole(1)],
        out_shape=[jax.ShapeDtypeStruct((tp, two_ff), BF16),
                   jax.ShapeDtypeStruct((3, dff), F32), jax.ShapeDtypeStruct((3, dff), F32),
                   jax.ShapeDtypeStruct((1, dff), F32), jax.ShapeDtypeStruct((1, dff), F32)],
        compiler_params=_cparams("arbitrary"))(up, up, dact, up, up, up, up, dact, fw, fw, fb, fb)


def _zoh(lr, li, ld):
    dt = jnp.exp(ld)
    mag = jnp.exp(lr * dt)
    ang = li * dt
    ar = mag * jnp.cos(ang)
    ai = mag * jnp.sin(ang)
    den = lr * lr + li * li
    nr = ar - 1.0
    fr = (nr * lr + ai * li) / den
    fi = (ai * lr - nr * li) / den
    return dt, ar, ai, den, nr, fr, fi


def _s5_prep(lr, li, ld, b_re, b_im, n_pow, name):
    nstate = lr.shape[1]

    def body(lr_ref, li_ref, ld_ref, bre_ref, bim_ref, pw_ref, bcre_ref, bcim_ref):
        _, ar, ai, _, _, fr, fi = _zoh(lr_ref[...], li_ref[...], ld_ref[...])
        bre = bre_ref[...]
        bim = bim_ref[...]
        bcre_ref[...] = (fr * bre - fi * bim).astype(BF16)
        bcim_ref[...] = (fr * bim + fi * bre).astype(BF16)
        row = lax.broadcasted_iota(jnp.int32, (SUBLANES, nstate), 0)
        pr, pi = jnp.zeros((SUBLANES, nstate), F32), jnp.zeros((SUBLANES, nstate), F32)
        cr, ci = ar, ai
        for t in range(SUBLANES):
            pr, pi = jnp.where(row == t, cr, pr), jnp.where(row == t, ci, pi)
            cr, ci = cr * ar - ci * ai, cr * ai + ci * ar
        pw_ref[0, 0:SUBLANES, :] = pr
        pw_ref[1, 0:SUBLANES, :] = pi
        n = SUBLANES
        while n < n_pow:
            m = min(n, n_pow - n)
            tr, ti = pw_ref[0, n - 1:n, :], pw_ref[1, n - 1:n, :]
            xr, xi = pw_ref[0, 0:m, :], pw_ref[1, 0:m, :]
            pw_ref[0, n:n + m, :] = xr * tr - xi * ti
            pw_ref[1, n:n + m, :] = xr * ti + xi * tr
            n += m

    vmem = pl.BlockSpec(memory_space=pltpu.VMEM)
    return pl.pallas_call(
        body, name=name, in_specs=[vmem] * 5, out_specs=[vmem] * 3,
        out_shape=[jax.ShapeDtypeStruct((2, n_pow, nstate), F32)] + [jax.ShapeDtypeStruct(b_re.shape, BF16)] * 2,
        compiler_params=pltpu.CompilerParams(vmem_limit_bytes=VMEM_LIMIT))(lr, li, ld, b_re, b_im)


def _s5_prep_bwd(lr, li, ld, b_re, b_im, da_re, da_im, dbc_re, dbc_im, name):
    def body(lr_ref, li_ref, ld_ref, bre_ref, bim_ref, dar_ref, dai_ref, dbcre_ref, dbcim_ref,
             dlr_ref, dli_ref, dld_ref, dbre_ref, dbim_ref):
        lr, li = lr_ref[...], li_ref[...]
        dt, ar, ai, den, nr, fr, fi = _zoh(lr, li, ld_ref[...])
        bre, bim = bre_ref[...], bim_ref[...]
        gre, gim = dbcre_ref[...], dbcim_ref[...]
        dbre_ref[...] = fr * gre + fi * gim
        dbim_ref[...] = fr * gim - fi * gre
        g_fr = jnp.sum(gre * bre + gim * bim, axis=0, keepdims=True)
        g_fi = jnp.sum(gim * bre - gre * bim, axis=0, keepdims=True)
        g_ar = dar_ref[...] + (g_fr * lr - g_fi * li) / den
        g_ai = dai_ref[...] + (g_fr * li + g_fi * lr) / den
        d_lr = (g_fr * (nr - 2.0 * fr * lr) + g_fi * (ai - 2.0 * fi * lr)) / den
        d_li = (g_fr * (ai - 2.0 * fr * li) - g_fi * (nr + 2.0 * fi * li)) / den
        g_logmag = g_ar * ar + g_ai * ai
        g_ang = g_ai * ar - g_ar * ai
        dlr_ref[...] = d_lr + g_logmag * dt
        dli_ref[...] = d_li + g_ang * dt
        d_ld = (g_logmag * lr + g_ang * li) * dt
        n = d_ld.shape[1]
        sh = 1
        while sh < STATE:
            d_ld = d_ld + pltpu.roll(d_ld, n - sh, 1)
            sh *= 2
        dld_ref[...] = d_ld

    vmem = pl.BlockSpec(memory_space=pltpu.VMEM)
    row = jax.ShapeDtypeStruct(lr.shape, F32)
    return pl.pallas_call(
        body, name=name, in_specs=[vmem] * 9, out_specs=[vmem] * 5,
        out_shape=[row, row, row, jax.ShapeDtypeStruct(b_re.shape, F32), jax.ShapeDtypeStruct(b_re.shape, F32)],
    )(lr, li, ld, b_re, b_im, da_re, da_im, dbc_re, dbc_im)


def _compact_b(bb):
    bq = bb.reshape(N_GROUPS // 8, 8, STATE, GROUP)
    m = jnp.einsum("ab,qbph->qahbp", jnp.eye(8, dtype=bb.dtype), bq).reshape(N_GROUPS // 8, LANES, 8 * STATE)
    return m.transpose(1, 0, 2).reshape(LANES, N_GROUPS * STATE)


def _expand_b(m):
    d = m.reshape(8, GROUP, N_GROUPS // 8, 8, STATE)
    return jnp.einsum("ahqap->qahp", d).reshape(N_GROUPS, GROUP, STATE)


def _compact_c(c):
    cq = c.reshape(N_GROUPS // 8, 8, GROUP, STATE)
    return jnp.einsum("ab,qbhp->qbpah", jnp.eye(8, dtype=c.dtype), cq).reshape(N_GROUPS * STATE, LANES)


def _expand_c(m):
    d = m.reshape(N_GROUPS // 8, 8, STATE, 8, GROUP)
    return jnp.einsum("qbpbh->qbhp", d).reshape(N_GROUPS, GROUP, STATE)


def _local_step(x, target, p, ex):
    seq, d = x.shape
    n_real = N_META + seq
    tp = -(-n_real // ROW_ALIGN) * ROW_ALIGN

    h0, hn1 = _input_norm_fwd(x, p["meta_tokens"], p["norm_mix_g"] + ex.zero, tp, "norm_mix")
    ex.forward("first", hn1)
    nstate = N_GROUPS * STATE
    s5 = (p["ssm_lam_re"].reshape(1, nstate), p["ssm_lam_im"].reshape(1, nstate),
          jnp.repeat(p["ssm_log_dt"].reshape(-1), STATE).reshape(1, nstate),
          _compact_b(p["ssm_b_re"]), _compact_b(p["ssm_b_im"]))
    a_pow, bc_re, bc_im = _s5_prep(*s5, tp // SUBLANES, "s5_prep")
    cc_re = _compact_c(p["ssm_c_re"]).astype(BF16)
    cc_im = _compact_c(p["ssm_c_im"]).astype(BF16)
    dskip = p["ssm_d"].reshape(1, -1)
    first = ex.weights("first", bc_re)
    proj = _mm(hn1, first["w_in"], "nn", "proj")
    started = ex.forward("mid", proj)
    co, y, g, *states = _seq_fwd(proj, p["conv_w"] + started[0, 0], bc_re, bc_im, cc_re, cc_im, dskip, a_pow,
                                 "seq_fwd")
    mid = ex.weights("mid", g)
    z = _mm(g, mid["ssm_w_glu"], "nn", "glu")
    mixed = _mix_fwd(co, y, z, p["gain_conv_out"], p["gain_ssm_out"], "mix_fwd")
    started = ex.forward("up", mixed)
    h1, hn2 = _proj_res_norm(mixed, mid["w_out"], h0, p["norm_ffn_g"], started, "out_proj_norm")
    late = ex.weights("up", hn2)
    up = _mm(hn2, late["w_up"], "nn", "up_proj")
    started = ex.forward("down", up)
    act = _ffn_act(up, p["ffn_conv_w"] + started[0, 0], p["ffn_conv_b"], "ffn_act")
    late.update(ex.weights("down", act))
    loss, dh2, dh2b, d_gfin = _proj_loss_bwd(act, late["w_down"], h1, target, p["norm_final_g"], n_real,
                                             "down_proj_loss")

    g_w_down = _mm(act, dh2b, "tn", "g_w_down")
    dact = _mm(dh2b, late["w_down"], "nt", "d_act")
    dup, dfw_a, dfw_v, dfb_a, dfb_v = _ffn_bwd(up, dact, p["ffn_conv_w"], p["ffn_conv_b"], "ffn_bwd")
    g_w_up = _mm(hn2, dup, "tn", "g_w_up")
    started = ex.grads_ready("late", {"w_up": g_w_up, "w_down": g_w_down})
    dh1, dh1b, d_gffn = _proj_norm_bwd(dup, late["w_up"], h1, p["norm_ffn_g"], dh2, started, "d_hn2_norm_bwd")
    started = ex.grads_send("late", dh1)
    g_w_out = _mm(mixed, dh1b, "tn", "g_w_out", after=started)
    dco, dz, dgp, d_gc, d_gs = _proj_mix_bwd(dh1b, mid["w_out"], co, y, z, p["gain_conv_out"],
                                             p["gain_ssm_out"], "d_mixed_mix_bwd")
    g_w_glu = _mm(g, dz, "tn", "g_w_glu")
    started = ex.grads_ready("mid", {"ssm_w_glu": g_w_glu, "w_out": g_w_out})
    dg = _mm(dz, mid["ssm_w_glu"], "nt", "d_gelu", acc_in=dgp, after=started)
    started = ex.grads_send("mid", dg)
    dproj, d_conv_w = _conv_bwd(proj, dco, p["conv_w"] + started[0, 0], "conv_bwd")
    (dproj, dbc_re, dbc_im, dcc_re, dcc_im, d_dskip, da_re, da_im) = _ssm_bwd(
        proj, y, dg, dproj, states, bc_re, bc_im, cc_re, cc_im, dskip, a_pow, "ssm_bwd")
    g_w_in = _mm(hn1, dproj, "tn", "g_w_in")
    started = ex.grads_ready("first", {"w_in": g_w_in})
    grad_x, d_meta, d_gmix = _proj_input_norm_bwd(dproj, first["w_in"], h0, p["norm_mix_g"], dh1, started, n_real,
                                                  "d_hn1_norm_bwd")
    started = ex.grads_send("first", d_gmix)

    d_lam_re, d_lam_im, d_log_dt, d_b_re, d_b_im = _s5_prep_bwd(*s5, da_re, da_im, dbc_re, dbc_im, "s5_prep_bwd")
    d_lam_re, d_lam_im = d_lam_re.reshape(N_GROUPS, STATE), d_lam_im.reshape(N_GROUPS, STATE)
    d_log_dt = d_log_dt[0, ::STATE]
    d_b_re, d_b_im = _expand_b(d_b_re), _expand_b(d_b_im)
    grads = {
        "meta_tokens": d_meta, "norm_mix_g": d_gmix, "w_in": g_w_in, "conv_w": d_conv_w,
        "ssm_lam_re": d_lam_re, "ssm_lam_im": d_lam_im, "ssm_log_dt": d_log_dt,
        "ssm_b_re": d_b_re, "ssm_b_im": d_b_im, "ssm_c_re": _expand_c(dcc_re), "ssm_c_im": _expand_c(dcc_im),
        "ssm_d": d_dskip.reshape(N_GROUPS, GROUP), "ssm_w_glu": g_w_glu,
        "gain_conv_out": d_gc, "gain_ssm_out": d_gs, "w_out": g_w_out, "norm_ffn_g": d_gffn,
        "w_up": g_w_up, "ffn_conv_w": jnp.concatenate([dfw_a, dfw_v], axis=1),
        "ffn_conv_b": jnp.concatenate([dfb_a, dfb_v], axis=1), "w_down": g_w_down, "norm_final_g": d_gfin,
    }
    return loss[0, 0] + started[0, 0], grad_x, grads


def _view(ref, axis, start, size):
    idx = [slice(None)] * len(ref.shape)
    idx[axis] = pl.ds(start, size)
    return ref.at[tuple(idx)]


def _exchange(name, ins, outs, aliases, local_copies, remote_copies):
    ni, no = len(ins), len(outs)
    nl, nr = len(local_copies), len(remote_copies)

    def body(*refs):
        in_refs, out_refs = refs[:ni], refs[ni:ni + no]
        send_sems, recv_sems, local_sems = refs[ni + no:]
        x, y, c = lax.axis_index("x"), lax.axis_index("y"), lax.axis_index("c")
        pos = (x, y, c, 2 * x + y)
        locals_ = [pltpu.make_async_copy(s(in_refs, out_refs, pos), d(in_refs, out_refs, pos), local_sems.at[i])
                   for i, (s, d) in enumerate(local_copies)]
        remotes = []
        for i, (s, d, flip) in enumerate(remote_copies):
            peer = (1 - x if "x" in flip else x, 1 - y if "y" in flip else y, 1 - c if "c" in flip else c)
            remotes.append(pltpu.make_async_remote_copy(
                src_ref=s(in_refs, out_refs, pos), dst_ref=d(in_refs, out_refs, pos),
                send_sem=send_sems.at[i], recv_sem=recv_sems.at[i], device_id=peer, device_id_type=MESH))
        for cp in locals_ + remotes:
            cp.start()
        for cp in remotes:
            cp.wait_recv()
        for cp in remotes:
            cp.wait_send()
        for cp in locals_:
            cp.wait()

    hbm = pl.BlockSpec(memory_space=pl.ANY)
    return pl.pallas_call(
        body, name=name, in_specs=[hbm] * ni, out_specs=[hbm] * no, out_shape=outs,
        input_output_aliases=aliases,
        scratch_shapes=[pltpu.SemaphoreType.DMA((nr,)), pltpu.SemaphoreType.DMA((nr,)),
                        pltpu.SemaphoreType.DMA((max(nl, 1),))],
    )(*ins)


BIG = {"w_in": (0, 1), "ssm_w_glu": (1, 0), "w_out": (1, 0), "w_up": (0, 1), "w_down": (1, 0)}
BIG_NAMES = tuple(BIG)
FLIPS = ("y", "x", "xy")


def _peer_chip(pos, flip):
    x, y, _, _ = pos
    return 2 * (1 - x if "x" in flip else x) + (1 - y if "y" in flip else y)


def _block_rows(rows, cols, itemsize, mult):
    return _pick_tile(rows, max(mult, (2 * 1024 * 1024) // (cols * itemsize)), mult)


def _cast_into_full(w, kc, shard_axis, name):
    r, cdim = w.shape
    tr = _block_rows(r, cdim, 4, 16)
    nb = r // tr

    def body(kc_ref, w_ref, o_ref):
        o_ref[...] = w_ref[...].astype(BF16)

    if shard_axis == 1:
        full, o_spec = (r, 4 * cdim), pl.BlockSpec((tr, cdim), lambda i, kc: (i, kc[0]))
    else:
        full, o_spec = (4 * r, cdim), pl.BlockSpec((tr, cdim), lambda i, kc: (kc[0] * nb + i, 0))
    return pl.pallas_call(
        body, name=name,
        grid_spec=pltpu.PrefetchScalarGridSpec(
            num_scalar_prefetch=1, grid=(nb,), in_specs=[pl.BlockSpec((tr, cdim), lambda i, kc: (i, 0))],
            out_specs=o_spec),
        out_shape=jax.ShapeDtypeStruct(full, BF16), compiler_params=_cparams("parallel"))(kc, w)


def _pair_sum(g, recv, kc, half_axis, name, out_dtype):
    hr, hc = recv.shape
    tr = _block_rows(hr, hc, 4, 16)
    nb = hr // tr

    def body(kc_ref, g_ref, r_ref, o_ref):
        o_ref[...] = (g_ref[...] + r_ref[...]).astype(out_dtype)

    if half_axis == 0:
        g_spec = pl.BlockSpec((tr, hc), lambda i, kc: (kc[1] * nb + i, 0))
    elif half_axis == 1:
        g_spec = pl.BlockSpec((tr, hc), lambda i, kc: (i, kc[1]))
    else:
        g_spec = pl.BlockSpec((tr, hc), lambda i, kc: (i, 0))
    same = pl.BlockSpec((tr, hc), lambda i, kc: (i, 0))
    return pl.pallas_call(
        body, name=name,
        grid_spec=pltpu.PrefetchScalarGridSpec(num_scalar_prefetch=1, grid=(nb,), in_specs=[g_spec, same],
                                               out_specs=same),
        out_shape=jax.ShapeDtypeStruct((hr, hc), out_dtype), compiler_params=_cparams("parallel"))(kc, g, recv)


def _chip_sum(own, recv, kc, own_axis, out_axis, name):
    _, sr, sc = recv.shape
    tr = _block_rows(sr, sc, 4, 16)
    nb = sr // tr

    def body(kc_ref, o_ref, r_ref, t_ref):
        k = kc_ref[0]
        own_v = o_ref[...].astype(F32)
        r = [r_ref[m].astype(F32) for m in range(3)]
        terms = []
        for kk in range(4):
            m = jnp.bitwise_xor(k, kk)
            terms.append(jnp.where(m == 0, own_v, jnp.where(m == 1, r[0], jnp.where(m == 2, r[1], r[2]))))
        t_ref[...] = (terms[0] + terms[1]) + (terms[2] + terms[3])

    if own_axis == 0:
        own_spec = pl.BlockSpec((tr, sc), lambda i, kc: (kc[0] * nb + i, 0))
    elif own_axis == 1:
        own_spec = pl.BlockSpec((tr, sc), lambda i, kc: (i, kc[0]))
    else:
        own_spec = pl.BlockSpec((tr, sc), lambda i, kc: (kc[1] * nb + i, 0))
    if out_axis == 0:
        out_full, out_spec = (2 * sr, sc), pl.BlockSpec((tr, sc), lambda i, kc: (kc[1] * nb + i, 0))
    else:
        out_full, out_spec = (sr, 2 * sc), pl.BlockSpec((tr, sc), lambda i, kc: (i, kc[1]))
    return pl.pallas_call(
        body, name=name,
        grid_spec=pltpu.PrefetchScalarGridSpec(
            num_scalar_prefetch=1, grid=(nb,),
            in_specs=[own_spec, pl.BlockSpec((3, tr, sc), lambda i, kc: (0, i, 0))],
            out_specs=out_spec),
        out_shape=jax.ShapeDtypeStruct(out_full, F32), compiler_params=_cparams("parallel"))(kc, own, recv)


def _adamw(w, g, m, v, name):
    r, cdim = w.shape
    tr = _block_rows(r, cdim, 4, 8)
    c1 = 1.0 - ADAM_B1 ** ADAM_STEP
    c2 = 1.0 - ADAM_B2 ** ADAM_STEP

    def body(w_ref, g_ref, m_ref, v_ref, go_ref, d_ref, nm_ref, nv_ref):
        gv = g_ref[...]
        go_ref[...] = gv
        nm = ADAM_B1 * m_ref[...] + (1.0 - ADAM_B1) * gv
        nv = ADAM_B2 * v_ref[...] + (1.0 - ADAM_B2) * (gv * gv)
        d_ref[...] = -ADAM_LR * ((nm / c1) / (jnp.sqrt(nv / c2) + ADAM_EPS) + ADAM_WD * w_ref[...])
        nm_ref[...] = nm
        nv_ref[...] = nv

    spec = _rows(cdim, tr)
    return pl.pallas_call(body, name=name, grid=(r // tr,), in_specs=[spec] * 4, out_specs=[spec] * 4,
                          out_shape=[jax.ShapeDtypeStruct((r, cdim), F32)] * 4,
                          compiler_params=_cparams("parallel"))(w, g, m, v)


def _adamw_whole(ws, gs, ms, vs, name):
    n = len(ws)
    c1 = 1.0 - ADAM_B1 ** ADAM_STEP
    c2 = 1.0 - ADAM_B2 ** ADAM_STEP

    def body(*refs):
        for i in range(n):
            w_ref, g_ref, m_ref, v_ref, d_ref, nm_ref, nv_ref = [refs[j * n + i] for j in range(7)]
            gv = g_ref[...]
            nm = ADAM_B1 * m_ref[...] + (1.0 - ADAM_B1) * gv
            nv = ADAM_B2 * v_ref[...] + (1.0 - ADAM_B2) * (gv * gv)
            d_ref[...] = -ADAM_LR * ((nm / c1) / (jnp.sqrt(nv / c2) + ADAM_EPS) + ADAM_WD * w_ref[...])
            nm_ref[...] = nm
            nv_ref[...] = nv

    vmem = pl.BlockSpec(memory_space=pltpu.VMEM)
    out = pl.pallas_call(body, name=name, in_specs=[vmem] * (4 * n), out_specs=[vmem] * (3 * n),
                         out_shape=[jax.ShapeDtypeStruct(a.shape, F32) for a in ws] * 3,
                         compiler_params=pltpu.CompilerParams(vmem_limit_bytes=VMEM_LIMIT))(*ws, *gs, *ms, *vs)
    return out[:n], out[n:2 * n], out[2 * n:]


SIDE_EFFECT = pltpu.SideEffectType.DATAFLOW_SIDE_EFFECTING


def _descriptors(copies, refs, send_sems, recv_sems, sem_off=0):
    x, y, c = lax.axis_index("x"), lax.axis_index("y"), lax.axis_index("c")
    pos = (x, y, c, 2 * x + y)
    out = []
    for i, (s, d, flip) in enumerate(copies):
        peer = (1 - x if "x" in flip else x, 1 - y if "y" in flip else y, 1 - c if "c" in flip else c)
        out.append(pltpu.make_async_remote_copy(
            src_ref=s(refs, refs, pos), dst_ref=d(refs, refs, pos),
            send_sem=send_sems.at[sem_off + i], recv_sem=recv_sems.at[sem_off + i],
            device_id=peer, device_id_type=MESH))
    return out


def _shifted(copies, off):
    return [(lambda I, O, pos, s=s: s(I[off:], O[off:], pos), lambda I, O, pos, d=d: d(I[off:], O[off:], pos), flip)
            for s, d, flip in copies]


BARRIER_IDS = {"c": (1, 2), "ici": (3, 4)}


def _exchange_start(name, bufs, copies, turns, after=None):
    n, nr = len(bufs), len(copies)
    na = 0 if after is None else 1
    flips = sorted({flip for _, _, flip in copies})
    kind = "c" if flips == ["c"] else "ici"
    collective_id = BARRIER_IDS[kind][turns[kind] % 2]
    turns[kind] += 1

    def body(*refs):
        x, y, c = lax.axis_index("x"), lax.axis_index("y"), lax.axis_index("c")
        barrier = pltpu.get_barrier_semaphore()
        for flip in flips:
            peer = (1 - x if "x" in flip else x, 1 - y if "y" in flip else y, 1 - c if "c" in flip else c)
            pl.semaphore_signal(barrier, inc=1, device_id=peer, device_id_type=MESH)
        pl.semaphore_wait(barrier, len(flips))
        for cp in _descriptors(copies, refs[:n], refs[n + na], refs[n + na + 1]):
            cp.start()
        token = refs[2 * n + na + 2]
        token[...] = jnp.zeros_like(token)

    hbm = pl.BlockSpec(memory_space=pltpu.HBM)
    sem = pl.BlockSpec(memory_space=pltpu.SEMAPHORE)
    out = pl.pallas_call(
        body, name=name,
        in_specs=[hbm] * n + [pl.BlockSpec(memory_space=pl.ANY)] * na,
        out_specs=(sem, sem, *[hbm] * n, pl.BlockSpec(memory_space=pltpu.VMEM)),
        out_shape=(pltpu.SemaphoreType.DMA((nr,)), pltpu.SemaphoreType.DMA((nr,)),
                   *[pltpu.HBM(b.shape, b.dtype) for b in bufs], jax.ShapeDtypeStruct((SUBLANES, LANES), F32)),
        input_output_aliases={i: 2 + i for i in range(n)},
        compiler_params=pltpu.CompilerParams(has_side_effects=SIDE_EFFECT, collective_id=collective_id),
    )(*[pltpu.with_memory_space_constraint(b, pltpu.HBM) for b in bufs], *([after] * na))
    return out[0], out[1], list(out[2:2 + n]), out[2 + n]


def _exchange_wait(name, send_sems, recv_sems, bufs, copies, after, sem_off=0):
    n = len(bufs)

    def body(*refs):
        for cp in _descriptors(copies, refs[:n], refs[n], refs[n + 1], sem_off):
            cp.wait_send()
            cp.wait_recv()

    hbm = pl.BlockSpec(memory_space=pltpu.HBM)
    sem = pl.BlockSpec(memory_space=pltpu.SEMAPHORE)
    out = pl.pallas_call(
        body, name=name,
        in_specs=[hbm] * n + [sem, sem, pl.BlockSpec(memory_space=pl.ANY)],
        out_specs=tuple([hbm] * n),
        out_shape=tuple(pltpu.HBM(b.shape, b.dtype) for b in bufs),
        input_output_aliases={i: i for i in range(n)},
        compiler_params=pltpu.CompilerParams(has_side_effects=SIDE_EFFECT),
    )(*bufs, send_sems, recv_sems, after)
    return list(out)


FIRST = ("w_in",)
MID = ("ssm_w_glu", "w_out")
LATE = ("w_up", "w_down")
GROUPS = {"first": FIRST, "mid": MID, "late": LATE}
ARRIVALS = {"first": FIRST, "mid": MID, "up": ("w_up",), "down": ("w_down",)}


def _gather_copies(names, shard_shapes):
    def region(i, chip, c):
        half_axis, shard_axis = BIG[names[i]]
        ssize = shard_shapes[i][shard_axis]
        hsize = shard_shapes[i][half_axis] // 2
        return lambda ref: _view(_view(ref, shard_axis, chip * ssize, ssize), half_axis, c * hsize, hsize)

    ici, d2d = [], []
    for i in range(len(names)):
        for flip in FLIPS:
            ici.append((lambda I, O, pos, i=i: region(i, pos[3], pos[2])(I[i]),
                        lambda I, O, pos, i=i: region(i, pos[3], pos[2])(O[i]), flip))
            d2d.append((lambda I, O, pos, i=i, flip=flip: region(i, _peer_chip(pos, flip), pos[2])(I[i]),
                        lambda I, O, pos, i=i, flip=flip: region(i, _peer_chip(pos, flip), pos[2])(O[i]), "c"))
    return ici, d2d


def _half_shape(n, shape):
    r, cdim = shape
    return (r // 2, cdim) if BIG[n][0] == 0 else (r, cdim // 2)


def _sub_shape(n, shape):
    hr, hc = _half_shape(n, shape)
    return (hr, hc // 4) if BIG[n][1] == 1 else (hr // 4, hc)


def _pair_copies(names, shapes, with_pack, dst_off):
    n = len(names)

    def other_half(i, ref, pos):
        half_axis = BIG[names[i]][0]
        hsize = shapes[i][half_axis] // 2
        return _view(ref, half_axis, (1 - pos[2]) * hsize, hsize)

    copies = [(lambda I, O, pos, i=i: other_half(i, I[i], pos), lambda I, O, pos, i=i: O[dst_off + i], "c")
              for i in range(n)]
    if with_pack:
        copies.append((lambda I, O, pos: I[n], lambda I, O, pos: O[dst_off + n], "c"))
    return copies


def _chip_copies(names, shapes, pack_rows, dst_off):
    n = len(names)

    def piece(i, ref, chip):
        shard_axis = BIG[names[i]][1]
        ssize = _sub_shape(names[i], shapes[i])[shard_axis]
        return _view(ref, shard_axis, chip * ssize, ssize)

    copies = []
    for i in range(n):
        for slot, flip in enumerate(FLIPS):
            copies.append((lambda I, O, pos, i=i, flip=flip: piece(i, I[i], _peer_chip(pos, flip)),
                           lambda I, O, pos, i=i, slot=slot: O[dst_off + i].at[slot], flip))
    if pack_rows:
        for slot, flip in enumerate(FLIPS):
            copies.append((lambda I, O, pos: _view(I[n], 0, pos[2] * (pack_rows // 2), pack_rows // 2),
                           lambda I, O, pos, slot=slot: O[dst_off + n].at[slot], flip))
    return copies


class _Exchanges:
    def __init__(self, shards, tiny, kc):
        self.kc = kc
        wb = {n: _cast_into_full(shards[n], kc, BIG[n][1], "cast_" + n) for n in BIG_NAMES}
        self.gathering, self.forwarding, self.pairing, self.reducing = {}, {}, {}, {}
        self.turns = {"c": 0, "ici": 0}
        tiny_copies = [(lambda I, O, pos: I[0], lambda I, O, pos: O[1].at[pos[3]], flip) for flip in FLIPS]
        self.gathering["tiny"] = (0, 0, 2, tiny_copies, None)
        bufs, copies = [tiny, lax.empty((4,) + tiny.shape, F32)], list(tiny_copies)
        for group, names in ARRIVALS.items():
            ici, d2d = _gather_copies(names, [shards[n].shape for n in names])
            self.gathering[group] = (len(bufs), len(copies), len(names), ici, d2d)
            copies += _shifted(ici, len(bufs))
            bufs += [wb[n] for n in names]
        self.started = _exchange_start("gather_start", bufs, copies, self.turns)
        self.zero = self.started[3][0, 0]

    def _arrived(self, group, after):
        buf_off, sem_off, n, ici, _ = self.gathering[group]
        send_sems, recv_sems, bufs, _ = self.started
        return _exchange_wait("gather_%s_wait" % group, send_sems, recv_sems, bufs[buf_off:buf_off + n], ici, after,
                              sem_off)

    def small_params(self, kc):
        tiny, got = self._arrived("tiny", self.started[3])
        return lax.dynamic_update_index_in_dim(got, tiny, kc[0], 0)

    def forward(self, group, after):
        d2d = self.gathering[group][4]
        self.forwarding[group] = (_exchange_start("forward_%s_start" % group, self._arrived(group, after), d2d,
                                                  self.turns), d2d)
        return self.forwarding[group][0][3]

    def weights(self, group, after):
        if group not in self.forwarding:
            after = self.forward(group, after)
        (send_sems, recv_sems, bufs, _), d2d = self.forwarding[group]
        full = _exchange_wait("forward_%s_wait" % group, send_sems, recv_sems, bufs, d2d, after)
        return dict(zip(ARRIVALS[group], full))

    def grads_ready(self, group, grads):
        names = GROUPS[group]
        gs = [grads[n] for n in names]
        land = [lax.empty(_half_shape(n, g.shape), F32) for n, g in zip(names, gs)]
        copies = _pair_copies(names, [g.shape for g in gs], False, len(names))
        started = _exchange_start("pair_%s_start" % group, gs + land, copies, self.turns)
        self.pairing[group] = (started, copies)
        return started[3]

    def grads_send(self, group, after):
        names = GROUPS[group]
        n = len(names)
        (send_sems, recv_sems, bufs, _), copies = self.pairing[group]
        bufs = _exchange_wait("pair_%s_wait" % group, send_sems, recv_sems, bufs, copies, after)
        chip = [_pair_sum(bufs[i], bufs[n + i], self.kc, BIG[names[i]][0], "pair_sum_" + names[i], BF16)
                for i in range(n)]
        shapes = [bufs[i].shape for i in range(n)]
        land = [lax.empty((3,) + _sub_shape(names[i], shapes[i]), BF16) for i in range(n)]
        copies = _chip_copies(names, shapes, 0, n)
        started = _exchange_start("reduce_%s_start" % group, chip + land, copies, self.turns)
        self.reducing[group] = (started, copies)
        return started[3]

    def finish_pack(self, pack):
        kc = self.kc
        prow = pack.shape[0] // 2
        recv = _exchange("reduce_d2d", [pack], [jax.ShapeDtypeStruct(pack.shape, F32)], {}, [],
                         _pair_copies((), [], True, 0))
        chip_pack = _pair_sum(pack, recv[0], kc, None, "pair_sum_pack", F32)
        copies = _chip_copies((), [], pack.shape[0], 1)
        land = lax.empty((3, prow, pack.shape[1]), F32)
        pack_sems_s, pack_sems_r, pack_bufs, after = _exchange_start("reduce_pack_start", [chip_pack, land], copies,
                                                                     self.turns)

        names, chips, recvs = (), [], []
        for group, group_names in GROUPS.items():
            (send_sems, recv_sems, bufs, _), group_copies = self.reducing[group]
            bufs = _exchange_wait("reduce_%s_wait" % group, send_sems, recv_sems, bufs, group_copies, after)
            n = len(group_names)
            names, chips, recvs = names + group_names, chips + bufs[:n], recvs + bufs[n:]
            after = bufs[n]
        total = [_chip_sum(chips[i], recvs[i], kc, BIG[n][1], BIG[n][0], "chip_sum_" + n)
                 for i, n in enumerate(names)]

        def my_half(half_axis, ref, pos):
            hsize = ref.shape[half_axis] // 2
            return _view(ref, half_axis, pos[2] * hsize, hsize)

        swap = [(lambda I, O, pos, i=i, n=n: my_half(BIG[n][0], I[i], pos),
                 lambda I, O, pos, i=i, n=n: my_half(BIG[n][0], O[i], pos), "c") for i, n in enumerate(names)]
        self.swapping = (_exchange_start("swap_start", total, swap, self.turns), swap, names)

        chip_pack, recv_pack = _exchange_wait("reduce_pack_wait", pack_sems_s, pack_sems_r, pack_bufs, copies,
                                              self.swapping[0][3])
        total_pack = _chip_sum(chip_pack, recv_pack, kc, None, 0, "chip_sum_pack")
        swap = [(lambda I, O, pos: my_half(0, I[0], pos), lambda I, O, pos: my_half(0, O[0], pos), "c")]
        return _exchange("swap_pack", [total_pack], [jax.ShapeDtypeStruct(pack.shape, F32)], {0: 0}, [], swap)[0]

    def finish_big(self, after):
        (send_sems, recv_sems, bufs, _), swap, names = self.swapping
        return dict(zip(names, _exchange_wait("swap_wait", send_sems, recv_sems, bufs, swap, after)))


WEIGHTS = ("meta_tokens", "norm_mix_g", "w_in", "conv_w", "ssm_lam_re", "ssm_lam_im", "ssm_log_dt", "ssm_b_re",
           "ssm_b_im", "ssm_c_re", "ssm_c_im", "ssm_d", "ssm_w_glu", "gain_conv_out", "gain_ssm_out", "w_out",
           "norm_ffn_g", "w_up", "ffn_conv_w", "ffn_conv_b", "w_down", "norm_final_g")
TINY_SHARDED = ("meta_tokens", "conv_w", "ffn_conv_w")
REPLICATED = tuple(n for n in WEIGHTS if n not in BIG and n not in TINY_SHARDED)
PACK_COLS = 512


def _pack(arrays, row_mult, cols):
    flat = jnp.concatenate([a.reshape(-1).astype(F32) for a in arrays])
    n = flat.shape[0]
    total = -(-n // (row_mult * cols)) * (row_mult * cols)
    return jnp.concatenate([flat, jnp.zeros((total - n,), F32)]).reshape(total // cols, cols)


def _unpack(packed, shapes):
    flat = packed.reshape(-1)
    out, off = [], 0
    for s in shapes:
        n = math.prod(s)
        out.append(flat[off:off + n].reshape(s))
        off += n
    return out


def kernel(x, meta_tokens, norm_mix_g, w_in, conv_w, ssm_lam_re, ssm_lam_im, ssm_log_dt, ssm_b_re, ssm_b_im, ssm_c_re, ssm_c_im, ssm_d, ssm_w_glu, gain_conv_out, gain_ssm_out, w_out, norm_ffn_g, w_up, ffn_conv_w, ffn_conv_b, w_down, norm_final_g, loss_target, m_meta_tokens, m_norm_mix_g, m_w_in, m_conv_w, m_ssm_lam_re, m_ssm_lam_im, m_ssm_log_dt, m_ssm_b_re, m_ssm_b_im, m_ssm_c_re, m_ssm_c_im, m_ssm_d, m_ssm_w_glu, m_gain_conv_out, m_gain_ssm_out, m_w_out, m_norm_ffn_g, m_w_up, m_ffn_conv_w, m_ffn_conv_b, m_w_down, m_norm_final_g, v_meta_tokens, v_norm_mix_g, v_w_in, v_conv_w, v_ssm_lam_re, v_ssm_lam_im, v_ssm_log_dt, v_ssm_b_re, v_ssm_b_im, v_ssm_c_re, v_ssm_c_im, v_ssm_d, v_ssm_w_glu, v_gain_conv_out, v_gain_ssm_out, v_w_out, v_norm_ffn_g, v_w_up, v_ffn_conv_w, v_ffn_conv_b, v_w_down, v_norm_final_g):
    args = dict(locals())
    w = {n: args[n] for n in WEIGHTS}
    mom = {n: args["m_" + n] for n in WEIGHTS}
    var = {n: args["v_" + n] for n in WEIGHTS}
    kx, ky, kc_ = lax.axis_index("x"), lax.axis_index("y"), lax.axis_index("c")
    chip = 2 * kx + ky
    kc = jnp.stack([chip, kc_]).astype(jnp.int32)

    def squeeze(n, a):
        if n == "meta_tokens":
            return a
        if n == "norm_final_g":
            return a.reshape(1, -1)
        a = a[0]
        return a.reshape(1, -1) if a.ndim == 1 else a

    wl = {n: squeeze(n, w[n]) for n in WEIGHTS}
    ml = {n: squeeze(n, mom[n]) for n in WEIGHTS}
    vl = {n: squeeze(n, var[n]) for n in WEIGHTS}

    tiny = _pack([wl[n] for n in TINY_SHARDED], SUBLANES, LANES)
    ex = _Exchanges({n: wl[n] for n in BIG_NAMES}, tiny, kc)
    tiny_shapes = [wl[n].shape for n in TINY_SHARDED]
    tiny_all = ex.small_params(kc)
    tiny_parts = [_unpack(tiny_all[k], tiny_shapes) for k in range(4)]
    p = {n: wl[n] for n in WEIGHTS if n not in BIG}
    for j, n in enumerate(TINY_SHARDED):
        p[n] = jnp.concatenate([tiny_parts[k][j] for k in range(4)], axis=1)
    p["ssm_log_dt"] = wl["ssm_log_dt"].reshape(-1)

    loss_local, grad_x, grads = _local_step(x[0], loss_target[0], p, ex)

    small_names = REPLICATED + TINY_SHARDED
    small_shapes = [tuple(grads[n].shape) for n in small_names] + [(1,)]
    pack = _pack([grads[n] for n in small_names] + [loss_local.reshape(1)], 2 * 16, PACK_COLS)
    g_pack = ex.finish_pack(pack)
    g_small = dict(zip(small_names + ("loss",), _unpack(g_pack, small_shapes)))
    loss = g_small["loss"][0]
    swapped = ("ssm_b_re", "ssm_b_im")

    def view(n, a):
        if n in swapped:
            return jnp.swapaxes(a, -1, -2)
        return a.reshape(1, -1) if a.ndim == 1 else a

    g = {}
    for n in REPLICATED:
        g[n] = g_small[n].reshape(view(n, w[n]).shape)
    for n in TINY_SHARDED:
        cols = wl[n].shape[1]
        g[n] = lax.dynamic_slice_in_dim(g_small[n], chip * cols, cols, axis=1).reshape(w[n].shape)
    delta, new_m, new_v = {}, {}, {}
    small = [[view(n, d[n]) for n in small_names] for d in (w, mom, var)]
    small.insert(1, [g[n] for n in small_names])
    for d, outs in zip((delta, new_m, new_v), _adamw_whole(*small, "adamw_small")):
        d.update(zip(small_names, outs))
    for d in (g, delta, new_m, new_v):
        d.update({n: jnp.swapaxes(d[n], -1, -2) for n in swapped})
    g_big = ex.finish_big(delta[small_names[0]])
    for n in BIG_NAMES:
        g[n], delta[n], new_m[n], new_v[n] = _adamw(wl[n], g_big[n], ml[n], vl[n], "adamw_" + n)

    def like(n, a):
        return a.reshape(w[n].shape)

    return (loss, grad_x[None], *[like(n, g[n]) for n in WEIGHTS], *[like(n, delta[n]) for n in WEIGHTS],
            *[like(n, new_m[n]) for n in WEIGHTS], *[like(n, new_v[n]) for n in WEIGHTS])
```

```python
import functools
import math

import jax
import jax.numpy as jnp
from jax import lax
from jax.experimental import pallas as pl
from jax.experimental.pallas import tpu as pltpu

F32 = jnp.float32
BF16 = jnp.bfloat16
MESH = pl.DeviceIdType.MESH

N_META = 16
N_GROUPS = 32
GROUP = 16
STATE = 64
RMS_EPS = 1e-6
ADAM_LR = 0.001
ADAM_B1 = 0.9
ADAM_B2 = 0.999
ADAM_EPS = 1e-08
ADAM_WD = 0.01
ADAM_STEP = 10

LANES = 128
SUBLANES = 8
ROW_ALIGN = 128
ROW_TILES = 4
VMEM_LIMIT = 52 * 1024 * 1024
MM_VMEM_BUDGET = 40 * 1024 * 1024
GELU_C = math.sqrt(2.0 / math.pi)
GELU_A = 0.044715


def _cparams(*sem):
    return pltpu.CompilerParams(dimension_semantics=sem, vmem_limit_bytes=VMEM_LIMIT)


def _pick_tile(dim, cap, mult):
    best = None
    for t in range(mult, min(dim, cap) + 1, mult):
        if dim % t == 0:
            best = t
    return best if best is not None else dim


def _mm(a, b, mode, name, out_dtype=F32, acc_in=None, after=None):
    if mode == "tn":
        kdim, m = a.shape
    else:
        m, kdim = a.shape
    n = b.shape[0] if mode == "nt" else b.shape[1]
    tm = _pick_tile(m, 1408, LANES if mode == "tn" else 16)
    tk = _pick_tile(kdim, 2816, LANES)
    nk = kdim // tk
    out_bytes = jnp.dtype(out_dtype).itemsize
    for cap in (1408, 1024, 512, 256, LANES):
        tn = _pick_tile(n, cap, LANES)
        blocks = 2 * (tm * tk * 2 + tk * tn * 2 + tm * tn * out_bytes * (2 if acc_in is not None else 1))
        if blocks + (tm * tn * 4 if nk > 1 else 0) <= MM_VMEM_BUDGET:
            break
    has_acc = acc_in is not None

    def body(*refs):
        if after is not None:
            refs = refs[1:]
        if has_acc:
            a_ref, b_ref, c_ref, o_ref = refs[:4]
            rest = refs[4:]
        else:
            a_ref, b_ref, o_ref = refs[:3]
            c_ref = None
            rest = refs[3:]
        if mode == "nn":
            p = jnp.dot(a_ref[...], b_ref[...], preferred_element_type=F32)
        elif mode == "nt":
            p = lax.dot_general(a_ref[...], b_ref[...], (((1,), (1,)), ((), ())), preferred_element_type=F32)
        else:
            p = lax.dot_general(a_ref[...], b_ref[...], (((0,), (0,)), ((), ())), preferred_element_type=F32)
        if nk == 1:
            if has_acc:
                p = p + c_ref[...]
            o_ref[...] = p.astype(out_dtype)
        else:
            acc_ref = rest[0]
            k = pl.program_id(2)

            @pl.when(k == 0)
            def _():
                acc_ref[...] = p + c_ref[...] if has_acc else p

            @pl.when(k > 0)
            def _():
                acc_ref[...] += p

            @pl.when(k == nk - 1)
            def _():
                o_ref[...] = acc_ref[...].astype(out_dtype)

    if mode == "tn":
        a_spec = pl.BlockSpec((tk, tm), lambda i, j, k: (k, i))
    else:
        a_spec = pl.BlockSpec((tm, tk), lambda i, j, k: (i, k))
    if mode == "nt":
        b_spec = pl.BlockSpec((tn, tk), lambda i, j, k: (j, k))
    else:
        b_spec = pl.BlockSpec((tk, tn), lambda i, j, k: (k, j))
    o_spec = pl.BlockSpec((tm, tn), lambda i, j, k: (i, j))
    in_specs = [a_spec, b_spec] + ([o_spec] if has_acc else [])
    args = (a, b) + ((acc_in,) if has_acc else ())
    if after is not None:
        in_specs = [pl.BlockSpec(memory_space=pl.ANY)] + in_specs
        args = (after,) + args
    return pl.pallas_call(
        body, name=name, grid=(m // tm, n // tn, nk),
        in_specs=in_specs, out_specs=o_spec,
        out_shape=jax.ShapeDtypeStruct((m, n), out_dtype),
        scratch_shapes=[pltpu.VMEM((tm, tn), F32)] if nk > 1 else [],
        compiler_params=_cparams("parallel", "parallel", "arbitrary"),
    )(*args)


def _mm_rows(a, b, mode, name, ins, outs, epilogue, scratch=()):
    m, kdim = a.shape
    n = b.shape[0] if mode == "nt" else b.shape[1]
    tm = m // ROW_TILES
    tk = _pick_tile(kdim, 2816, LANES)
    nk = kdim // tk
    ni, no = len(ins), len(outs)

    def body(*refs):
        a_ref, b_ref = refs[:2]
        in_refs, out_refs, rest = refs[2:2 + ni], refs[2 + ni:2 + ni + no], refs[2 + ni + no:]
        i = pl.program_id(0)
        if mode == "nn":
            p = jnp.dot(a_ref[...], b_ref[...], preferred_element_type=F32)
        else:
            p = lax.dot_general(a_ref[...], b_ref[...], (((1,), (1,)), ((), ())), preferred_element_type=F32)
        if nk == 1:
            epilogue(p, i, in_refs, out_refs, rest)
        else:
            acc_ref = rest[0]
            k = pl.program_id(1)

            @pl.when(k == 0)
            def _():
                acc_ref[...] = p

            @pl.when(k > 0)
            def _():
                acc_ref[...] += p

            @pl.when(k == nk - 1)
            def _():
                epilogue(acc_ref[...], i, in_refs, out_refs, rest[1:])

    def spec(shape, kind):
        if kind == "rows":
            return pl.BlockSpec((tm,) + tuple(shape[1:]), lambda i, k: (i,) + (0,) * (len(shape) - 1))
        if kind == "whole":
            return pl.BlockSpec(tuple(shape), lambda i, k: (0,) * len(shape))
        return pl.BlockSpec(memory_space=pl.ANY)

    a_spec = pl.BlockSpec((tm, tk), lambda i, k: (i, k))
    b_spec = pl.BlockSpec((n, tk), lambda i, k: (0, k)) if mode == "nt" else pl.BlockSpec((tk, n), lambda i, k: (k, 0))
    return pl.pallas_call(
        body, name=name, grid=(ROW_TILES, nk),
        in_specs=[a_spec, b_spec] + [spec(x.shape, kind) for x, kind in ins],
        out_specs=[spec(shape, kind) for shape, _, kind in outs],
        out_shape=[jax.ShapeDtypeStruct(shape, dtype) for shape, dtype, _ in outs],
        scratch_shapes=([pltpu.VMEM((tm, n), F32)] if nk > 1 else []) + list(scratch),
        compiler_params=_cparams("arbitrary", "arbitrary"),
    )(a, b, *[x for x, _ in ins])


def _rows(shape_cols, tr, dtype=None):
    return pl.BlockSpec((tr, shape_cols), lambda i: (i, 0))


def _const(shape):
    return pl.BlockSpec(shape, lambda i: (0,) * len(shape))


def _rms(x):
    return lax.rsqrt(jnp.mean(x * x, axis=-1, keepdims=True) + RMS_EPS)


def _rms_bwd(x, r, g, dy):
    xn = x * r
    dxn = dy * g
    dx = r * (dxn - xn * jnp.mean(dxn * xn, axis=-1, keepdims=True))
    return dx, dy * xn


def _gelu(y):
    return 0.5 * y * (1.0 + jnp.tanh(GELU_C * (y + GELU_A * y * y * y)))


def _gelu_grad(y):
    t = jnp.tanh(GELU_C * (y + GELU_A * y * y * y))
    return 0.5 * (1.0 + t) + 0.5 * y * (1.0 - t * t) * GELU_C * (1.0 + 3.0 * GELU_A * y * y)


def _sigmoid(z):
    return 1.0 / (1.0 + jnp.exp(-z))


def _proj_res_norm(a, w, h, g, after, name):
    def epilogue(p, i, ins, outs, _):
        x = ins[0][...] + p
        outs[0][...] = x
        outs[1][...] = (x * _rms(x) * ins[1][...]).astype(BF16)

    return _mm_rows(a, w, "nn", name, [(h, "rows"), (g, "whole"), (after, "hbm")],
                    [(h.shape, F32, "rows"), (h.shape, BF16, "rows")], epilogue)


def _proj_norm_bwd(da, w, h, g, dres, after, name):
    d = h.shape[1]

    def epilogue(p, i, ins, outs, _):
        x = ins[0][...]
        dx, dgs = _rms_bwd(x, _rms(x), ins[1][...], p)
        dh = ins[2][...] + dx
        outs[0][...] = dh
        outs[1][...] = dh.astype(BF16)

        @pl.when(i == 0)
        def _():
            outs[2][...] = jnp.zeros_like(outs[2])

        outs[2][...] += jnp.sum(dgs, axis=0, keepdims=True)

    return _mm_rows(da, w, "nt", name, [(h, "rows"), (g, "whole"), (dres, "rows"), (after, "hbm")],
                    [(h.shape, F32, "rows"), (h.shape, BF16, "rows"), ((1, d), F32, "whole")], epilogue)


def _proj_input_norm_bwd(da, w, h, g, dres, after, n_real, name):
    tp, d = h.shape
    tr = tp // ROW_TILES

    def epilogue(p, i, ins, outs, scratch):
        h_ref, g_ref, dres_ref, _ = ins
        dx_ref, dmeta_ref, dg_ref = outs
        stage, sem = scratch
        x = h_ref[...]
        dx, dgs = _rms_bwd(x, _rms(x), g_ref[...], p)
        stage[...] = dres_ref[...] + dx

        @pl.when(i == 0)
        def _():
            dg_ref[...] = jnp.zeros_like(dg_ref)
            dmeta_ref[...] = stage[:N_META, :]

        dg_ref[...] += jnp.sum(dgs, axis=0, keepdims=True)
        for t in range(ROW_TILES):
            lo, hi = max(t * tr, N_META), min((t + 1) * tr, n_real)
            if hi > lo:
                @pl.when(i == t)
                def _(t=t, lo=lo, hi=hi):
                    cp = pltpu.make_async_copy(stage.at[pl.ds(lo - t * tr, hi - lo), :],
                                               dx_ref.at[pl.ds(lo - N_META, hi - lo), :], sem)
                    cp.start()
                    cp.wait()

    return _mm_rows(da, w, "nt", name, [(h, "rows"), (g, "whole"), (dres, "rows"), (after, "hbm")],
                    [((n_real - N_META, d), F32, "hbm"), ((N_META, d), F32, "whole"), ((1, d), F32, "whole")],
                    epilogue, scratch=[pltpu.VMEM((tr, d), F32), pltpu.SemaphoreType.DMA])


def _load_token_rows(tok_hbm, buf, sem, tr, n_real, head=None, wait=False, i=None):
    i = pl.program_id(0) if i is None else i
    for t in range(ROW_TILES):
        base = t * tr
        lo, hi = max(base, N_META), min(base + tr, n_real)

        @pl.when(i == t)
        def _(base=base, lo=lo, hi=hi):
            if hi > lo:
                cp = pltpu.make_async_copy(tok_hbm.at[pl.ds(lo - N_META, hi - lo), :],
                                           buf.at[pl.ds(lo - base, hi - lo), :], sem)
                if wait:
                    cp.wait()
                    return
                cp.start()
            if wait:
                return
            if base < N_META:
                buf[0:N_META - base, :] = (jnp.zeros((N_META - base, buf.shape[1]), F32) if head is None
                                           else head[base:N_META, :])
            if hi < base + tr:
                buf[max(hi, base) - base:tr, :] = jnp.zeros((base + tr - max(hi, base), buf.shape[1]), F32)


def _input_norm_fwd(x, meta, g, tp, name):
    seq, d = x.shape
    tr = tp // ROW_TILES
    n_real = N_META + seq

    def body(x_hbm, meta_ref, g_ref, h_ref, hn_ref, buf, sem):
        _load_token_rows(x_hbm, buf, sem, tr, n_real, head=meta_ref)
        _load_token_rows(x_hbm, buf, sem, tr, n_real, wait=True)
        h = buf[...]
        h_ref[...] = h
        hn_ref[...] = (h * _rms(h) * g_ref[...]).astype(BF16)

    return pl.pallas_call(
        body, name=name, grid=(ROW_TILES,),
        in_specs=[pl.BlockSpec(memory_space=pl.ANY), _const((N_META, d)), _const((1, d))],
        out_specs=[_rows(d, tr), _rows(d, tr)],
        out_shape=[jax.ShapeDtypeStruct((tp, d), F32), jax.ShapeDtypeStruct((tp, d), BF16)],
        scratch_shapes=[pltpu.VMEM((tr, d), F32), pltpu.SemaphoreType.DMA],
        compiler_params=_cparams("arbitrary"))(x, meta, g)


def _proj_loss_bwd(act, w, h1, target, g, n_real, name):
    tp, d = h1.shape
    tr = tp // ROW_TILES

    def epilogue(p, i, ins, outs, scratch):
        h1_ref, t_hbm, g_ref = ins
        loss_ref, dh_ref, dhb_ref, dg_ref = outs
        t_buf, sem = scratch
        _load_token_rows(t_hbm, t_buf, sem, tr, n_real, i=i)
        x = h1_ref[...] + p
        r = _rms(x)
        row = i * tr + lax.broadcasted_iota(jnp.int32, (tr, d), 0)
        valid = (row >= N_META) & (row < n_real)
        _load_token_rows(t_hbm, t_buf, sem, tr, n_real, wait=True, i=i)
        e = jnp.where(valid, x * r * g_ref[...] - t_buf[...], 0.0)
        dx, dgs = _rms_bwd(x, r, g_ref[...], e * (1.0 / d))
        dh_ref[...] = dx
        dhb_ref[...] = dx.astype(BF16)

        @pl.when(i == 0)
        def _():
            dg_ref[...] = jnp.zeros_like(dg_ref)
            loss_ref[...] = jnp.zeros_like(loss_ref)

        dg_ref[...] += jnp.sum(dgs, axis=0, keepdims=True)
        loss_ref[...] += (0.5 / d) * jnp.sum(jnp.sum(e * e, axis=0, keepdims=True), axis=1, keepdims=True)

    return _mm_rows(act, w, "nn", name, [(h1, "rows"), (target, "hbm"), (g, "whole")],
                    [((1, LANES), F32, "whole"), ((tp, d), F32, "rows"), ((tp, d), BF16, "rows"),
                     ((1, d), F32, "whole")],
                    epilogue, scratch=[pltpu.VMEM((tr, d), F32), pltpu.SemaphoreType.DMA])


def _mix_fwd(co, y, z, gc, gs, name):
    tp, dh = co.shape
    tr = tp // ROW_TILES

    def body(co_ref, y_ref, z_ref, gc_ref, gs_ref, m_ref):
        c = co_ref[...]
        m_ref[:, :dh] = (c * _rms(c) * gc_ref[...]).astype(BF16)
        so = _gelu(y_ref[...]) * _sigmoid(z_ref[...])
        m_ref[:, dh:] = (so * _rms(so) * gs_ref[...]).astype(BF16)

    return pl.pallas_call(
        body, name=name, grid=(ROW_TILES,),
        in_specs=[_rows(dh, tr)] * 3 + [_const((1, dh))] * 2,
        out_specs=_rows(2 * dh, tr),
        out_shape=jax.ShapeDtypeStruct((tp, 2 * dh), BF16),
        compiler_params=_cparams("parallel"))(co, y, z, gc, gs)


def _proj_mix_bwd(dh1b, w, co, y, z, gc, gs, name):
    tp, dh = co.shape

    def epilogue(p, i, ins, outs, _):
        co_ref, y_ref, z_ref, gc_ref, gs_ref = ins
        dco_ref, dz_ref, dgp_ref, dgc_ref, dgs_ref = outs
        c = co_ref[...]
        dco, dgc = _rms_bwd(c, _rms(c), gc_ref[...], p[:, :dh])
        dco_ref[...] = dco
        gl = _gelu(y_ref[...])
        sg = _sigmoid(z_ref[...])
        so = gl * sg
        dso, dgs = _rms_bwd(so, _rms(so), gs_ref[...], p[:, dh:])
        dz_ref[...] = (dso * gl * sg * (1.0 - sg)).astype(BF16)
        dgp_ref[...] = dso * sg

        @pl.when(i == 0)
        def _():
            dgc_ref[...] = jnp.zeros_like(dgc_ref)
            dgs_ref[...] = jnp.zeros_like(dgs_ref)

        dgc_ref[...] += jnp.sum(dgc, axis=0, keepdims=True)
        dgs_ref[...] += jnp.sum(dgs, axis=0, keepdims=True)

    return _mm_rows(dh1b, w, "nt", name,
                    [(co, "rows"), (y, "rows"), (z, "rows"), (gc, "whole"), (gs, "whole")],
                    [((tp, dh), F32, "rows"), ((tp, dh), BF16, "rows"), ((tp, dh), F32, "rows"),
                     ((1, dh), F32, "whole"), ((1, dh), F32, "whole")], epilogue)


def _shift_down(x, k):
    row = lax.broadcasted_iota(jnp.int32, x.shape, 0)
    return jnp.where(row >= k, pltpu.roll(x, k, 0), 0.0)


def _shift_up(x, k):
    n = x.shape[0]
    row = lax.broadcasted_iota(jnp.int32, x.shape, 0)
    return jnp.where(row < n - k, pltpu.roll(x, n - k, 0), 0.0)


def _dwconv(x, w_ref):
    return w_ref[2:3, :] * x + w_ref[1:2, :] * _shift_down(x, 1) + w_ref[0:1, :] * _shift_down(x, 2)


def _dwconv_bwd(x, dy, w_ref):
    dx = w_ref[2:3, :] * dy + w_ref[1:2, :] * _shift_up(dy, 1) + w_ref[0:1, :] * _shift_up(dy, 2)
    dw = jnp.concatenate([jnp.sum(dy * _shift_down(x, 2), axis=0, keepdims=True),
                          jnp.sum(dy * _shift_down(x, 1), axis=0, keepdims=True),
                          jnp.sum(dy * x, axis=0, keepdims=True)], axis=0)
    return dx, dw


def _interleave(dst, src):
    seg_rows = src.shape[0] // SUBLANES
    for seg in range(SUBLANES):
        dst[pl.ds(seg, seg_rows, stride=SUBLANES), :] = src[seg * seg_rows:(seg + 1) * seg_rows, :]


def _deinterleave(dst, src):
    seg_rows = src.shape[0] // SUBLANES
    for seg in range(SUBLANES):
        dst[seg * seg_rows:(seg + 1) * seg_rows, :] = src[pl.ds(seg, seg_rows, stride=SUBLANES), :]


def _segment_shift(x, reverse):
    row = lax.broadcasted_iota(jnp.int32, x.shape, 0)
    if reverse:
        return jnp.where(row < SUBLANES - 1, pltpu.roll(x, SUBLANES - 1, 0), 0.0)
    return jnp.where(row >= 1, pltpu.roll(x, 1, 0), 0.0)


def _scan(s_re, s_im, pw_ref, reverse, pair=None):
    n_steps = s_re.shape[0] // SUBLANES
    n_strips = s_re.shape[1] // LANES
    sign = -1.0 if reverse else 1.0
    strips = [slice(st * LANES, (st + 1) * LANES) for st in range(n_strips)]

    def rows_of(j):
        step = (n_steps - 1 - j) if reverse else j
        return pl.ds(pl.multiple_of(step * SUBLANES, SUBLANES), SUBLANES)

    a = [(jnp.broadcast_to(pw_ref[0, 0:1, lanes], (SUBLANES, LANES)),
          sign * jnp.broadcast_to(pw_ref[1, 0:1, lanes], (SUBLANES, LANES))) for lanes in strips]

    def local(i, carry):
        for half in range(2):
            rows = rows_of(2 * i + half)
            out = []
            for st, lanes in enumerate(strips):
                (ar, ai), cr, ci = a[st], carry[2 * st], carry[2 * st + 1]
                xr = s_re[rows, lanes] + (ar * cr - ai * ci)
                xi = s_im[rows, lanes] + (ar * ci + ai * cr)
                s_re[rows, lanes] = xr
                s_im[rows, lanes] = xi
                out += [xr, xi]
            carry = tuple(out)
        return carry

    zero = jnp.zeros((SUBLANES, LANES), F32)
    ends = lax.fori_loop(0, n_steps // 2, local, (zero,) * (2 * n_strips))

    entering = []
    row = lax.broadcasted_iota(jnp.int32, (SUBLANES, LANES), 0)
    for st, lanes in enumerate(strips):
        tr, ti = ends[2 * st], ends[2 * st + 1]
        mr = jnp.broadcast_to(pw_ref[0, n_steps - 1:n_steps, lanes], (SUBLANES, LANES))
        mi = sign * jnp.broadcast_to(pw_ref[1, n_steps - 1:n_steps, lanes], (SUBLANES, LANES))
        for k in (1, 2, 4):
            keep = (row < SUBLANES - k) if reverse else (row >= k)
            rr = jnp.where(keep, pltpu.roll(tr, SUBLANES - k if reverse else k, 0), 0.0)
            ri = jnp.where(keep, pltpu.roll(ti, SUBLANES - k if reverse else k, 0), 0.0)
            tr, ti = tr + (mr * rr - mi * ri), ti + (mr * ri + mi * rr)
            mr, mi = mr * mr - mi * mi, 2.0 * mr * mi
        entering += [_segment_shift(tr, reverse), _segment_shift(ti, reverse)]

    def fix(i, carry):
        carry, sums = carry[:2 * n_strips], carry[2 * n_strips:]
        for half in range(2):
            j = 2 * i + half
            rows = rows_of(j)
            out, acc = [], []
            for st, lanes in enumerate(strips):
                (ar, ai), cr, ci = a[st], carry[2 * st], carry[2 * st + 1]
                cr, ci = ar * cr - ai * ci, ar * ci + ai * cr
                xr = s_re[rows, lanes] + cr
                xi = s_im[rows, lanes] + ci
                s_re[rows, lanes] = xr
                s_im[rows, lanes] = xi
                out += [cr, ci]
                if pair is not None:
                    p_rows = rows_of(jnp.minimum(j + 1, n_steps - 1))
                    keep = (j < n_steps - 1).astype(F32)
                    pr = pair[0][p_rows, lanes] * keep
                    pi = pair[1][p_rows, lanes] * keep
                    acc += [sums[2 * st] + (xr * pr + xi * pi), sums[2 * st + 1] + (xi * pr - xr * pi)]
            carry, sums = tuple(out), tuple(acc)
        return carry + sums

    n_sums = 0 if pair is None else 2 * n_strips
    out = lax.fori_loop(0, n_steps // 2, fix, tuple(entering) + (zero,) * n_sums)
    return out[2 * n_strips:]


def _seq_fwd(proj, conv_w, bc_re, bc_im, cc_re, cc_im, dskip, a_pow, name):
    tp = proj.shape[0]
    dh = proj.shape[1] // 4
    nq = dh // LANES
    sw = STATE * N_GROUPS // nq

    def body(b_ref, c_ref, v_ref, u_ref, w_ref, bre_ref, bim_ref, cre_ref, cim_ref, d_ref, pw_ref,
             co_ref, y_ref, g_ref, s_re, s_im, u_il, y_il):
        co_ref[...] = b_ref[...] * _dwconv(c_ref[...] * v_ref[...], w_ref)
        _interleave(u_il, u_ref)
        ub = u_il[...].astype(BF16)
        s_re[...] = jnp.dot(ub, bre_ref[...], preferred_element_type=F32)
        s_im[...] = jnp.dot(ub, bim_ref[...], preferred_element_type=F32)
        _scan(s_re, s_im, pw_ref, False)
        y_il[...] = (jnp.dot(s_re[...].astype(BF16), cre_ref[...], preferred_element_type=F32)
                     - jnp.dot(s_im[...].astype(BF16), cim_ref[...], preferred_element_type=F32))
        _deinterleave(y_ref, y_il)
        y = y_ref[...] + d_ref[...] * u_ref[...]
        y_ref[...] = y
        g_ref[...] = _gelu(y).astype(BF16)

    col = lambda off: pl.BlockSpec((tp, LANES), lambda q, off=off: (0, off * nq + q))
    blk = pl.BlockSpec((tp, LANES), lambda q: (0, q))
    return pl.pallas_call(
        body, name=name, grid=(nq,),
        in_specs=[col(0), col(1), col(2), col(3),
                  pl.BlockSpec((3, LANES), lambda q: (0, q)),
                  pl.BlockSpec((LANES, sw), lambda q: (0, q)), pl.BlockSpec((LANES, sw), lambda q: (0, q)),
                  pl.BlockSpec((sw, LANES), lambda q: (q, 0)), pl.BlockSpec((sw, LANES), lambda q: (q, 0)),
                  pl.BlockSpec((1, LANES), lambda q: (0, q)),
                  pl.BlockSpec((2, tp // SUBLANES, sw), lambda q: (0, 0, q))],
        out_specs=[blk, blk, blk, pl.BlockSpec((tp, sw), lambda q: (0, q)), pl.BlockSpec((tp, sw), lambda q: (0, q))],
        out_shape=[jax.ShapeDtypeStruct((tp, dh), F32), jax.ShapeDtypeStruct((tp, dh), F32),
                   jax.ShapeDtypeStruct((tp, dh), BF16),
                   jax.ShapeDtypeStruct((tp, nq * sw), F32), jax.ShapeDtypeStruct((tp, nq * sw), F32)],
        scratch_shapes=[pltpu.VMEM((tp, LANES), F32), pltpu.VMEM((tp, LANES), F32)],
        compiler_params=_cparams("parallel"),
    )(proj, proj, proj, proj, conv_w, bc_re, bc_im, cc_re, cc_im, dskip, a_pow)


def _conv_bwd(proj, dco, conv_w, name):
    tp = proj.shape[0]
    dh = proj.shape[1] // 4
    nq = dh // LANES

    def body(b_ref, c_ref, v_ref, dco_ref, w_ref, dproj_ref, dw_ref, stage, sem):
        q = pl.program_id(0)
        cg = c_ref[...]
        vg = v_ref[...]
        cv = cg * vg
        dco_v = dco_ref[...]
        dcv, dw = _dwconv_bwd(cv, dco_v * b_ref[...], w_ref)
        dw_ref[...] = dw
        stage[0] = (dco_v * _dwconv(cv, w_ref)).astype(BF16)
        stage[1] = (dcv * vg).astype(BF16)
        stage[2] = (dcv * cg).astype(BF16)
        copies = [pltpu.make_async_copy(stage.at[p], dproj_ref.at[:, pl.ds((p * nq + q) * LANES, LANES)], sem.at[p])
                  for p in range(3)]
        for cp in copies:
            cp.start()
        for cp in copies:
            cp.wait()

    col = lambda off: pl.BlockSpec((tp, LANES), lambda q, off=off: (0, off * nq + q))
    return pl.pallas_call(
        body, name=name, grid=(nq,),
        in_specs=[col(0), col(1), col(2), pl.BlockSpec((tp, LANES), lambda q: (0, q)),
                  pl.BlockSpec((3, LANES), lambda q: (0, q))],
        out_specs=[pl.BlockSpec(memory_space=pl.ANY), pl.BlockSpec((3, LANES), lambda q: (0, q))],
        out_shape=[jax.ShapeDtypeStruct((tp, 4 * dh), BF16), jax.ShapeDtypeStruct((3, dh), F32)],
        scratch_shapes=[pltpu.VMEM((3, tp, LANES), BF16), pltpu.SemaphoreType.DMA((3,))],
        compiler_params=_cparams("arbitrary"),
    )(proj, proj, proj, dco, conv_w)


def _ssm_bwd(proj, y, dg, dproj, states, bc_re, bc_im, cc_re, cc_im, dskip, a_pow, name):
    tp = proj.shape[0]
    dh = proj.shape[1] // 4
    nq = dh // LANES
    sw = STATE * N_GROUPS // nq

    def body(u_ref, y_ref, dg_ref, dproj_in, s_re, s_im, bre_ref, bim_ref, cre_ref, cim_ref, d_ref, pw_ref,
             dproj_ref, dbre_ref, dbim_ref, dcre_ref, dcim_ref, dd_ref, dar_ref, dai_ref,
             l_re, l_im, a_il, b_il, stage, sem):
        del dproj_in
        q = pl.program_id(0)
        nt = (((1,), (1,)), ((), ()))
        tn = (((0,), (0,)), ((), ()))
        _interleave(a_il, u_ref)
        ub = a_il[...].astype(BF16)
        dy_rows = dg_ref[...] * _gelu_grad(y_ref[...])
        dd_ref[...] = jnp.sum(dy_rows * u_ref[...], axis=0, keepdims=True)
        _interleave(b_il, dy_rows)
        dy = b_il[...]
        dyb = dy.astype(BF16)
        l_re[...] = lax.dot_general(dyb, cre_ref[...], nt, preferred_element_type=F32)
        l_im[...] = -lax.dot_general(dyb, cim_ref[...], nt, preferred_element_type=F32)
        dcre_ref[...] = lax.dot_general(s_re[...].astype(BF16), dyb, tn, preferred_element_type=F32)
        dcim_ref[...] = -lax.dot_general(s_im[...].astype(BF16), dyb, tn, preferred_element_type=F32)
        sums = _scan(l_re, l_im, pw_ref, True, pair=(s_re, s_im))
        rest = tp - SUBLANES
        for st in range(sw // LANES):
            lanes = slice(st * LANES, (st + 1) * LANES)
            lr0, li0 = l_re[:SUBLANES, lanes], l_im[:SUBLANES, lanes]
            pr0, pi0 = _segment_shift(s_re[rest:, lanes], False), _segment_shift(s_im[rest:, lanes], False)
            dar_ref[:, lanes] = jnp.sum(sums[2 * st] + (lr0 * pr0 + li0 * pi0), axis=0, keepdims=True)
            dai_ref[:, lanes] = jnp.sum(sums[2 * st + 1] + (li0 * pr0 - lr0 * pi0), axis=0, keepdims=True)
        lrb = l_re[...].astype(BF16)
        lib = l_im[...].astype(BF16)
        a_il[...] = (dy * d_ref[...] + lax.dot_general(lrb, bre_ref[...], nt, preferred_element_type=F32)
                     + lax.dot_general(lib, bim_ref[...], nt, preferred_element_type=F32))
        _deinterleave(b_il, a_il)
        stage[...] = b_il[...].astype(BF16)
        dbre_ref[...] = lax.dot_general(ub, lrb, tn, preferred_element_type=F32)
        dbim_ref[...] = lax.dot_general(ub, lib, tn, preferred_element_type=F32)
        cp = pltpu.make_async_copy(stage, dproj_ref.at[:, pl.ds((3 * nq + q) * LANES, LANES)], sem)
        cp.start()
        cp.wait()

    blk = pl.BlockSpec((tp, LANES), lambda q: (0, q))
    bspec = pl.BlockSpec((LANES, sw), lambda q: (0, q))
    cspec = pl.BlockSpec((sw, LANES), lambda q: (q, 0))
    tspec = pl.BlockSpec((2, tp // SUBLANES, sw), lambda q: (0, 0, q))
    nstate = STATE * N_GROUPS
    return pl.pallas_call(
        body, name=name, grid=(nq,),
        in_specs=[pl.BlockSpec((tp, LANES), lambda q: (0, 3 * nq + q)), blk, blk, pl.BlockSpec(memory_space=pl.ANY),
                  pl.BlockSpec((tp, sw), lambda q: (0, q)), pl.BlockSpec((tp, sw), lambda q: (0, q)),
                  bspec, bspec, cspec, cspec, pl.BlockSpec((1, LANES), lambda q: (0, q)), tspec],
        out_specs=[pl.BlockSpec(memory_space=pl.ANY), bspec, bspec, cspec, cspec,
                   pl.BlockSpec((1, LANES), lambda q: (0, q)),
                   pl.BlockSpec((1, sw), lambda q: (0, q)), pl.BlockSpec((1, sw), lambda q: (0, q))],
        out_shape=[jax.ShapeDtypeStruct((tp, 4 * dh), BF16),
                   jax.ShapeDtypeStruct((LANES, nstate), F32), jax.ShapeDtypeStruct((LANES, nstate), F32),
                   jax.ShapeDtypeStruct((nstate, LANES), F32), jax.ShapeDtypeStruct((nstate, LANES), F32),
                   jax.ShapeDtypeStruct((1, dh), F32),
                   jax.ShapeDtypeStruct((1, nstate), F32), jax.ShapeDtypeStruct((1, nstate), F32)],
        input_output_aliases={3: 0},
        scratch_shapes=[pltpu.VMEM((tp, sw), F32)] * 2 + [pltpu.VMEM((tp, LANES), F32)] * 2
        + [pltpu.VMEM((tp, LANES), BF16), pltpu.SemaphoreType.DMA],
        compiler_params=_cparams("arbitrary"),
    )(proj, y, dg, dproj, states[0], states[1], bc_re, bc_im, cc_re, cc_im, dskip, a_pow)


FFN_TILE = 256
FFN_ROW_TILE = 128
FFN_ROWS = 32


def _window(x_ref, before, r0, rows, cols):
    if r0 == 0:
        return jnp.concatenate([before, x_ref[0:rows, cols]], axis=0)
    return x_ref[r0 - SUBLANES:r0 + rows, cols]


def _taps(window):
    return window[SUBLANES:], pltpu.roll(window, 1, 0)[SUBLANES:], pltpu.roll(window, 2, 0)[SUBLANES:]


def _conv_taps(taps, w):
    return w[2] * taps[0] + w[1] * taps[1] + w[0] * taps[2]


def _ffn_specs(tp, dff):
    pieces = FFN_ROW_TILE // SUBLANES
    main = lambda half: pl.BlockSpec((FFN_ROW_TILE, dff), lambda i: (i, half))
    prev = lambda half: pl.BlockSpec((SUBLANES, dff), lambda i: (jnp.maximum(i * pieces - 1, 0), half))
    nxt = lambda half: pl.BlockSpec((SUBLANES, dff), lambda i: (jnp.minimum((i + 1) * pieces, tp // SUBLANES - 1), half))
    par = lambda r, half: pl.BlockSpec((r, dff), lambda i: (0, half))
    return main, prev, nxt, par


def _ffn_act(up, fw, fb, name):
    tp, two_ff = up.shape
    dff = two_ff // 2
    tr, tc, rows = FFN_ROW_TILE, FFN_TILE, FFN_ROWS

    def body(ua_ref, uv_ref, pa_ref, pv_ref, wa_ref, wv_ref, ba_ref, bv_ref, act_ref):
        first = pl.program_id(0) == 0
        for c0 in range(0, dff, tc):
            cols = slice(c0, c0 + tc)
            wa, wv = [[w_ref[k:k + 1, cols] for k in range(3)] for w_ref in (wa_ref, wv_ref)]
            ba, bv = ba_ref[:, cols], bv_ref[:, cols]
            before_a, before_v = [jnp.where(first, 0.0, p_ref[:, cols]) for p_ref in (pa_ref, pv_ref)]
            for r0 in range(0, tr, rows):
                a = _conv_taps(_taps(_window(ua_ref, before_a, r0, rows, cols)), wa) + ba
                v = _conv_taps(_taps(_window(uv_ref, before_v, r0, rows, cols)), wv) + bv
                act_ref[r0:r0 + rows, cols] = (a * _sigmoid(a) * v).astype(BF16)

    main, prev, _, par = _ffn_specs(tp, dff)
    return pl.pallas_call(
        body, name=name, grid=(tp // tr,),
        in_specs=[main(0), main(1), prev(0), prev(1), par(3, 0), par(3, 1), par(1, 0), par(1, 1)],
        out_specs=main(0),
        out_shape=jax.ShapeDtypeStruct((tp, dff), BF16),
        compiler_params=_cparams("parallel"))(up, up, up, up, fw, fw, fb, fb)


FFN_MM_ROWS = 544
FFN_MM_COLS = 1408


def _ffn_bwd(up, dh, w_down, fw, fb, name):
    tp, two_ff = up.shape
    dff = two_ff // 2
    dm = dh.shape[1]
    tr, cw, rows = FFN_MM_ROWS, FFN_MM_COLS, FFN_ROWS
    nr, nc = tp // tr, dff // cw
    n_e = rows + SUBLANES
    pieces = tr // SUBLANES

    def body(ua_ref, uv_ref, pa_ref, pv_ref, dh_ref, wd_ref, wa_ref, wv_ref, ba_ref, bv_ref,
             dup_ref, dwa_ref, dwv_ref, dba_ref, dbv_ref, dact, stage, head_ref, sem):
        j, i = pl.program_id(0), pl.program_id(1)
        step = j * nr + i
        top = i == nr - 1
        sums = ((dwa_ref, dba_ref), (dwv_ref, dbv_ref))

        def out_copies(at):
            r0 = pl.multiple_of((nr - 1 - at % nr) * tr, tr)
            return [pltpu.make_async_copy(
                stage.at[s], dup_ref.at[pl.ds(r0, tr), pl.ds(pl.multiple_of(s * dff + at // nr * cw, LANES), cw)],
                sem.at[s]) for s in range(2)]

        @pl.when(i == 0)
        def _():
            head_ref[...] = jnp.zeros_like(head_ref)
            for dw_ref, db_ref in sums:
                dw_ref[...] = jnp.zeros_like(dw_ref)
                db_ref[...] = jnp.zeros_like(db_ref)

        dact[...] = lax.dot_general(dh_ref[...], wd_ref[...], (((1,), (1,)), ((), ())), preferred_element_type=F32)

        @pl.when(step > 0)
        def _():
            for cp in out_copies(step - 1):
                cp.wait()

        def gate_bwd(taps, dact_v, w, bias):
            a, v = [_conv_taps(taps[s], w[s]) + bias[s] for s in range(2)]
            sg = _sigmoid(a)
            return [dact_v * v * sg * (1.0 + a * (1.0 - sg)), dact_v * a * sg]

        fold = lambda x: sum(x[r:r + SUBLANES] for r in range(0, rows, SUBLANES))
        for c0 in range(0, cw, FFN_TILE):
            cols = slice(c0, min(c0 + FFN_TILE, cw))
            w = [[w_ref[k:k + 1, cols] for k in range(3)] for w_ref in (wa_ref, wv_ref)]
            bias = [ba_ref[:, cols], bv_ref[:, cols]]
            before = [jnp.where(top, 0.0, p_ref[:, cols]) for p_ref in (pa_ref, pv_ref)]
            head = [head_ref[s, :, cols] for s in range(2)]
            piece = jnp.zeros_like(head[0])
            acc = [[piece] * 4 for _ in range(2)]
            for r0 in reversed(range(0, tr, rows)):
                taps = [_taps(_window(x_ref, before[s], r0, rows, cols)) for s, x_ref in enumerate((ua_ref, uv_ref))]
                d = gate_bwd(taps, dact[r0:r0 + rows, cols], w, bias)
                for s in range(2):
                    de = jnp.concatenate([d[s], head[s]], axis=0)
                    dx = (w[s][2] * d[s] + w[s][1] * pltpu.roll(de, n_e - 1, 0)[:rows]
                          + w[s][0] * pltpu.roll(de, n_e - 2, 0)[:rows])
                    stage[s, r0:r0 + rows, cols] = dx.astype(BF16)
                    for k in range(3):
                        acc[s][k] = acc[s][k] + fold(d[s] * taps[s][2 - k])
                    acc[s][3] = acc[s][3] + fold(d[s])
                    head[s] = d[s][:SUBLANES]
            for s, (dw_ref, db_ref) in enumerate(sums):
                head_ref[s, :, cols] = head[s]
                dw_ref[:, cols] = dw_ref[:, cols] + jnp.concatenate(
                    [jnp.sum(x, axis=0, keepdims=True) for x in acc[s][:3]], axis=0)
                db_ref[:, cols] = db_ref[:, cols] + jnp.sum(acc[s][3], axis=0, keepdims=True)

        copies = out_copies(step)
        for cp in copies:
            cp.start()

        @pl.when(step == nc * nr - 1)
        def _():
            for cp in copies:
                cp.wait()

    row = lambda i: nr - 1 - i
    main = lambda half: pl.BlockSpec((tr, cw), lambda j, i: (row(i), half * nc + j))
    prev = lambda half: pl.BlockSpec((SUBLANES, cw), lambda j, i: (jnp.maximum(row(i) * pieces - 1, 0), half * nc + j))
    par = lambda r, half: pl.BlockSpec((r, cw), lambda j, i: (0, half * nc + j))
    acc_spec = lambda r: pl.BlockSpec((r, cw), lambda j, i: (0, j))
    return pl.pallas_call(
        body, name=name, grid=(nc, nr),
        in_specs=[main(0), main(1), prev(0), prev(1),
                  pl.BlockSpec((tr, dm), lambda j, i: (row(i), 0)), pl.BlockSpec((cw, dm), lambda j, i: (j, 0)),
                  par(3, 0), par(3, 1), par(1, 0), par(1, 1)],
        out_specs=[pl.BlockSpec(memory_space=pl.ANY), acc_spec(3), acc_spec(3), acc_spec(1), acc_spec(1)],
        out_shape=[jax.ShapeDtypeStruct((tp, two_ff), BF16),
                   jax.ShapeDtypeStruct((3, dff), F32), jax.ShapeDtypeStruct((3, dff), F32),
                   jax.ShapeDtypeStruct((1, dff), F32), jax.ShapeDtypeStruct((1, dff), F32)],
        scratch_shapes=[pltpu.VMEM((tr, cw), F32), pltpu.VMEM((2, tr, cw), BF16),
                        pltpu.VMEM((2, SUBLANES, cw), F32), pltpu.SemaphoreType.DMA((2,))],
        compiler_params=_cparams("arbitrary", "arbitrary"))(up, up, up, up, dh, w_down, fw, fw, fb, fb)


def _zoh(lr, li, ld):
    dt = jnp.exp(ld)
    mag = jnp.exp(lr * dt)
    ang = li * dt
    ar = mag * jnp.cos(ang)
    ai = mag * jnp.sin(ang)
    den = lr * lr + li * li
    nr = ar - 1.0
    fr = (nr * lr + ai * li) / den
    fi = (ai * lr - nr * li) / den
    return dt, ar, ai, den, nr, fr, fi


def _s5_prep(lr, li, ld, b_re, b_im, n_pow, name):
    nstate = lr.shape[1]

    def body(lr_ref, li_ref, ld_ref, bre_ref, bim_ref, pw_ref, bcre_ref, bcim_ref):
        _, ar, ai, _, _, fr, fi = _zoh(lr_ref[...], li_ref[...], ld_ref[...])
        bre = bre_ref[...]
        bim = bim_ref[...]
        bcre_ref[...] = (fr * bre - fi * bim).astype(BF16)
        bcim_ref[...] = (fr * bim + fi * bre).astype(BF16)
        row = lax.broadcasted_iota(jnp.int32, (SUBLANES, nstate), 0)
        pr, pi = jnp.zeros((SUBLANES, nstate), F32), jnp.zeros((SUBLANES, nstate), F32)
        cr, ci = ar, ai
        for t in range(SUBLANES):
            pr, pi = jnp.where(row == t, cr, pr), jnp.where(row == t, ci, pi)
            cr, ci = cr * ar - ci * ai, cr * ai + ci * ar
        pw_ref[0, 0:SUBLANES, :] = pr
        pw_ref[1, 0:SUBLANES, :] = pi
        n = SUBLANES
        while n < n_pow:
            m = min(n, n_pow - n)
            tr, ti = pw_ref[0, n - 1:n, :], pw_ref[1, n - 1:n, :]
            xr, xi = pw_ref[0, 0:m, :], pw_ref[1, 0:m, :]
            pw_ref[0, n:n + m, :] = xr * tr - xi * ti
            pw_ref[1, n:n + m, :] = xr * ti + xi * tr
            n += m

    vmem = pl.BlockSpec(memory_space=pltpu.VMEM)
    return pl.pallas_call(
        body, name=name, in_specs=[vmem] * 5, out_specs=[vmem] * 3,
        out_shape=[jax.ShapeDtypeStruct((2, n_pow, nstate), F32)] + [jax.ShapeDtypeStruct(b_re.shape, BF16)] * 2,
        compiler_params=pltpu.CompilerParams(vmem_limit_bytes=VMEM_LIMIT))(lr, li, ld, b_re, b_im)


def _s5_prep_bwd(lr, li, ld, b_re, b_im, da_re, da_im, dbc_re, dbc_im, name):
    def body(lr_ref, li_ref, ld_ref, bre_ref, bim_ref, dar_ref, dai_ref, dbcre_ref, dbcim_ref,
             dlr_ref, dli_ref, dld_ref, dbre_ref, dbim_ref):
        lr, li = lr_ref[...], li_ref[...]
        dt, ar, ai, den, nr, fr, fi = _zoh(lr, li, ld_ref[...])
        bre, bim = bre_ref[...], bim_ref[...]
        gre, gim = dbcre_ref[...], dbcim_ref[...]
        dbre_ref[...] = fr * gre + fi * gim
        dbim_ref[...] = fr * gim - fi * gre
        g_fr = jnp.sum(gre * bre + gim * bim, axis=0, keepdims=True)
        g_fi = jnp.sum(gim * bre - gre * bim, axis=0, keepdims=True)
        g_ar = dar_ref[...] + (g_fr * lr - g_fi * li) / den
        g_ai = dai_ref[...] + (g_fr * li + g_fi * lr) / den
        d_lr = (g_fr * (nr - 2.0 * fr * lr) + g_fi * (ai - 2.0 * fi * lr)) / den
        d_li = (g_fr * (ai - 2.0 * fr * li) - g_fi * (nr + 2.0 * fi * li)) / den
        g_logmag = g_ar * ar + g_ai * ai
        g_ang = g_ai * ar - g_ar * ai
        dlr_ref[...] = d_lr + g_logmag * dt
        dli_ref[...] = d_li + g_ang * dt
        d_ld = (g_logmag * lr + g_ang * li) * dt
        n = d_ld.shape[1]
        sh = 1
        while sh < STATE:
            d_ld = d_ld + pltpu.roll(d_ld, n - sh, 1)
            sh *= 2
        dld_ref[...] = d_ld

    vmem = pl.BlockSpec(memory_space=pltpu.VMEM)
    row = jax.ShapeDtypeStruct(lr.shape, F32)
    return pl.pallas_call(
        body, name=name, in_specs=[vmem] * 9, out_specs=[vmem] * 5,
        out_shape=[row, row, row, jax.ShapeDtypeStruct(b_re.shape, F32), jax.ShapeDtypeStruct(b_re.shape, F32)],
    )(lr, li, ld, b_re, b_im, da_re, da_im, dbc_re, dbc_im)


def _compact_b(bb):
    bq = bb.reshape(N_GROUPS // 8, 8, STATE, GROUP)
    m = jnp.einsum("ab,qbph->qahbp", jnp.eye(8, dtype=bb.dtype), bq).reshape(N_GROUPS // 8, LANES, 8 * STATE)
    return m.transpose(1, 0, 2).reshape(LANES, N_GROUPS * STATE)


def _expand_b(m):
    d = m.reshape(8, GROUP, N_GROUPS // 8, 8, STATE)
    return jnp.einsum("ahqap->qahp", d).reshape(N_GROUPS, GROUP, STATE)


def _compact_c(c):
    cq = c.reshape(N_GROUPS // 8, 8, GROUP, STATE)
    return jnp.einsum("ab,qbhp->qbpah", jnp.eye(8, dtype=c.dtype), cq).reshape(N_GROUPS * STATE, LANES)


def _expand_c(m):
    d = m.reshape(N_GROUPS // 8, 8, STATE, 8, GROUP)
    return jnp.einsum("qbpbh->qbhp", d).reshape(N_GROUPS, GROUP, STATE)


def _local_step(x, target, p, ex):
    seq, d = x.shape
    n_real = N_META + seq
    tp = -(-n_real // ROW_ALIGN) * ROW_ALIGN

    h0, hn1 = _input_norm_fwd(x, p["meta_tokens"], p["norm_mix_g"] + ex.zero, tp, "norm_mix")
    ex.forward("first", hn1)
    nstate = N_GROUPS * STATE
    s5 = (p["ssm_lam_re"].reshape(1, nstate), p["ssm_lam_im"].reshape(1, nstate),
          jnp.repeat(p["ssm_log_dt"].reshape(-1), STATE).reshape(1, nstate),
          _compact_b(p["ssm_b_re"]), _compact_b(p["ssm_b_im"]))
    a_pow, bc_re, bc_im = _s5_prep(*s5, tp // SUBLANES, "s5_prep")
    cc_re = _compact_c(p["ssm_c_re"]).astype(BF16)
    cc_im = _compact_c(p["ssm_c_im"]).astype(BF16)
    dskip = p["ssm_d"].reshape(1, -1)
    first = ex.weights("first", bc_re)
    proj = _mm(hn1, first["w_in"], "nn", "proj")
    started = ex.forward("mid", proj)
    co, y, g, *states = _seq_fwd(proj, p["conv_w"] + started[0, 0], bc_re, bc_im, cc_re, cc_im, dskip, a_pow,
                                 "seq_fwd")
    mid = ex.weights("mid", g)
    z = _mm(g, mid["ssm_w_glu"], "nn", "glu")
    mixed = _mix_fwd(co, y, z, p["gain_conv_out"], p["gain_ssm_out"], "mix_fwd")
    started = ex.forward("up", mixed)
    h1, hn2 = _proj_res_norm(mixed, mid["w_out"], h0, p["norm_ffn_g"], started, "out_proj_norm")
    late = ex.weights("up", hn2)
    up = _mm(hn2, late["w_up"], "nn", "up_proj")
    started = ex.forward("down", up)
    act = _ffn_act(up, p["ffn_conv_w"] + started[0, 0], p["ffn_conv_b"], "ffn_act")
    late.update(ex.weights("down", act))
    loss, dh2, dh2b, d_gfin = _proj_loss_bwd(act, late["w_down"], h1, target, p["norm_final_g"], n_real,
                                             "down_proj_loss")

    g_w_down = _mm(act, dh2b, "tn", "g_w_down")
    dup, dfw_a, dfw_v, dfb_a, dfb_v = _ffn_bwd(up, dh2b, late["w_down"], p["ffn_conv_w"], p["ffn_conv_b"], "ffn_bwd")
    g_w_up = _mm(hn2, dup, "tn", "g_w_up")
    started = ex.grads_ready("late", {"w_up": g_w_up, "w_down": g_w_down})
    dh1, dh1b, d_gffn = _proj_norm_bwd(dup, late["w_up"], h1, p["norm_ffn_g"], dh2, started, "d_hn2_norm_bwd")
    started = ex.grads_send("late", dh1)
    g_w_out = _mm(mixed, dh1b, "tn", "g_w_out", after=started)
    dco, dz, dgp, d_gc, d_gs = _proj_mix_bwd(dh1b, mid["w_out"], co, y, z, p["gain_conv_out"],
                                             p["gain_ssm_out"], "d_mixed_mix_bwd")
    g_w_glu = _mm(g, dz, "tn", "g_w_glu")
    started = ex.grads_ready("mid", {"ssm_w_glu": g_w_glu, "w_out": g_w_out})
    dg = _mm(dz, mid["ssm_w_glu"], "nt", "d_gelu", acc_in=dgp, after=started)
    started = ex.grads_send("mid", dg)
    dproj, d_conv_w = _conv_bwd(proj, dco, p["conv_w"] + started[0, 0], "conv_bwd")
    (dproj, dbc_re, dbc_im, dcc_re, dcc_im, d_dskip, da_re, da_im) = _ssm_bwd(
        proj, y, dg, dproj, states, bc_re, bc_im, cc_re, cc_im, dskip, a_pow, "ssm_bwd")
    g_w_in = _mm(hn1, dproj, "tn", "g_w_in")
    started = ex.grads_ready("first", {"w_in": g_w_in})
    grad_x, d_meta, d_gmix = _proj_input_norm_bwd(dproj, first["w_in"], h0, p["norm_mix_g"], dh1, started, n_real,
                                                  "d_hn1_norm_bwd")
    started = ex.grads_send("first", d_gmix)

    d_lam_re, d_lam_im, d_log_dt, d_b_re, d_b_im = _s5_prep_bwd(*s5, da_re, da_im, dbc_re, dbc_im, "s5_prep_bwd")
    d_lam_re, d_lam_im = d_lam_re.reshape(N_GROUPS, STATE), d_lam_im.reshape(N_GROUPS, STATE)
    d_log_dt = d_log_dt[0, ::STATE]
    d_b_re, d_b_im = _expand_b(d_b_re), _expand_b(d_b_im)
    grads = {
        "meta_tokens": d_meta, "norm_mix_g": d_gmix, "w_in": g_w_in, "conv_w": d_conv_w,
        "ssm_lam_re": d_lam_re, "ssm_lam_im": d_lam_im, "ssm_log_dt": d_log_dt,
        "ssm_b_re": d_b_re, "ssm_b_im": d_b_im, "ssm_c_re": _expand_c(dcc_re), "ssm_c_im": _expand_c(dcc_im),
        "ssm_d": d_dskip.reshape(N_GROUPS, GROUP), "ssm_w_glu": g_w_glu,
        "gain_conv_out": d_gc, "gain_ssm_out": d_gs, "w_out": g_w_out, "norm_ffn_g": d_gffn,
        "w_up": g_w_up, "ffn_conv_w": jnp.concatenate([dfw_a, dfw_v], axis=1),
        "ffn_conv_b": jnp.concatenate([dfb_a, dfb_v], axis=1), "w_down": g_w_down, "norm_final_g": d_gfin,
    }
    return loss[0, 0] + started[0, 0], grad_x, grads


def _view(ref, axis, start, size):
    idx = [slice(None)] * len(ref.shape)
    idx[axis] = pl.ds(start, size)
    return ref.at[tuple(idx)]


def _exchange(name, ins, outs, aliases, local_copies, remote_copies):
    ni, no = len(ins), len(outs)
    nl, nr = len(local_copies), len(remote_copies)

    def body(*refs):
        in_refs, out_refs = refs[:ni], refs[ni:ni + no]
        send_sems, recv_sems, local_sems = refs[ni + no:]
        x, y, c = lax.axis_index("x"), lax.axis_index("y"), lax.axis_index("c")
        pos = (x, y, c, 2 * x + y)
        locals_ = [pltpu.make_async_copy(s(in_refs, out_refs, pos), d(in_refs, out_refs, pos), local_sems.at[i])
                   for i, (s, d) in enumerate(local_copies)]
        remotes = []
        for i, (s, d, flip) in enumerate(remote_copies):
            peer = (1 - x if "x" in flip else x, 1 - y if "y" in flip else y, 1 - c if "c" in flip else c)
            remotes.append(pltpu.make_async_remote_copy(
                src_ref=s(in_refs, out_refs, pos), dst_ref=d(in_refs, out_refs, pos),
                send_sem=send_sems.at[i], recv_sem=recv_sems.at[i], device_id=peer, device_id_type=MESH))
        for cp in locals_ + remotes:
            cp.start()
        for cp in remotes:
            cp.wait_recv()
        for cp in remotes:
            cp.wait_send()
        for cp in locals_:
            cp.wait()

    hbm = pl.BlockSpec(memory_space=pl.ANY)
    return pl.pallas_call(
        body, name=name, in_specs=[hbm] * ni, out_specs=[hbm] * no, out_shape=outs,
        input_output_aliases=aliases,
        scratch_shapes=[pltpu.SemaphoreType.DMA((nr,)), pltpu.SemaphoreType.DMA((nr,)),
                        pltpu.SemaphoreType.DMA((max(nl, 1),))],
    )(*ins)


BIG = {"w_in": (0, 1), "ssm_w_glu": (1, 0), "w_out": (1, 0), "w_up": (0, 1), "w_down": (1, 0)}
BIG_NAMES = tuple(BIG)
FLIPS = ("y", "x", "xy")


def _peer_chip(pos, flip):
    x, y, _, _ = pos
    return 2 * (1 - x if "x" in flip else x) + (1 - y if "y" in flip else y)


def _block_rows(rows, cols, itemsize, mult):
    return _pick_tile(rows, max(mult, (2 * 1024 * 1024) // (cols * itemsize)), mult)


def _cast_into_full(w, kc, shard_axis, name):
    r, cdim = w.shape
    tr = _block_rows(r, cdim, 4, 16)
    nb = r // tr

    def body(kc_ref, w_ref, o_ref):
        o_ref[...] = w_ref[...].astype(BF16)

    if shard_axis == 1:
        full, o_spec = (r, 4 * cdim), pl.BlockSpec((tr, cdim), lambda i, kc: (i, kc[0]))
    else:
        full, o_spec = (4 * r, cdim), pl.BlockSpec((tr, cdim), lambda i, kc: (kc[0] * nb + i, 0))
    return pl.pallas_call(
        body, name=name,
        grid_spec=pltpu.PrefetchScalarGridSpec(
            num_scalar_prefetch=1, grid=(nb,), in_specs=[pl.BlockSpec((tr, cdim), lambda i, kc: (i, 0))],
            out_specs=o_spec),
        out_shape=jax.ShapeDtypeStruct(full, BF16), compiler_params=_cparams("parallel"))(kc, w)


def _pair_sum(g, recv, kc, half_axis, name, out_dtype):
    hr, hc = recv.shape
    tr = _block_rows(hr, hc, 4, 16)
    nb = hr // tr

    def body(kc_ref, g_ref, r_ref, o_ref):
        o_ref[...] = (g_ref[...] + r_ref[...]).astype(out_dtype)

    if half_axis == 0:
        g_spec = pl.BlockSpec((tr, hc), lambda i, kc: (kc[1] * nb + i, 0))
    elif half_axis == 1:
        g_spec = pl.BlockSpec((tr, hc), lambda i, kc: (i, kc[1]))
    else:
        g_spec = pl.BlockSpec((tr, hc), lambda i, kc: (i, 0))
    same = pl.BlockSpec((tr, hc), lambda i, kc: (i, 0))
    return pl.pallas_call(
        body, name=name,
        grid_spec=pltpu.PrefetchScalarGridSpec(num_scalar_prefetch=1, grid=(nb,), in_specs=[g_spec, same],
                                               out_specs=same),
        out_shape=jax.ShapeDtypeStruct((hr, hc), out_dtype), compiler_params=_cparams("parallel"))(kc, g, recv)


def _chip_sum(own, recv, kc, own_axis, out_axis, name):
    _, sr, sc = recv.shape
    tr = _block_rows(sr, sc, 4, 16)
    nb = sr // tr

    def body(kc_ref, o_ref, r_ref, t_ref):
        k = kc_ref[0]
        own_v = o_ref[...].astype(F32)
        r = [r_ref[m].astype(F32) for m in range(3)]
        terms = []
        for kk in range(4):
            m = jnp.bitwise_xor(k, kk)
            terms.append(jnp.where(m == 0, own_v, jnp.where(m == 1, r[0], jnp.where(m == 2, r[1], r[2]))))
        t_ref[...] = (terms[0] + terms[1]) + (terms[2] + terms[3])

    if own_axis == 0:
        own_spec = pl.BlockSpec((tr, sc), lambda i, kc: (kc[0] * nb + i, 0))
    elif own_axis == 1:
        own_spec = pl.BlockSpec((tr, sc), lambda i, kc: (i, kc[0]))
    else:
        own_spec = pl.BlockSpec((tr, sc), lambda i, kc: (kc[1] * nb + i, 0))
    if out_axis == 0:
        out_full, out_spec = (2 * sr, sc), pl.BlockSpec((tr, sc), lambda i, kc: (kc[1] * nb + i, 0))
    else:
        out_full, out_spec = (sr, 2 * sc), pl.BlockSpec((tr, sc), lambda i, kc: (i, kc[1]))
    return pl.pallas_call(
        body, name=name,
        grid_spec=pltpu.PrefetchScalarGridSpec(
            num_scalar_prefetch=1, grid=(nb,),
            in_specs=[own_spec, pl.BlockSpec((3, tr, sc), lambda i, kc: (0, i, 0))],
            out_specs=out_spec),
        out_shape=jax.ShapeDtypeStruct(out_full, F32), compiler_params=_cparams("parallel"))(kc, own, recv)


def _adamw(w, g, m, v, name):
    r, cdim = w.shape
    tr = _block_rows(r, cdim, 4, 8)
    c1 = 1.0 - ADAM_B1 ** ADAM_STEP
    c2 = 1.0 - ADAM_B2 ** ADAM_STEP

    def body(w_ref, g_ref, m_ref, v_ref, go_ref, d_ref, nm_ref, nv_ref):
        gv = g_ref[...]
        go_ref[...] = gv
        nm = ADAM_B1 * m_ref[...] + (1.0 - ADAM_B1) * gv
        nv = ADAM_B2 * v_ref[...] + (1.0 - ADAM_B2) * (gv * gv)
        d_ref[...] = -ADAM_LR * ((nm / c1) / (jnp.sqrt(nv / c2) + ADAM_EPS) + ADAM_WD * w_ref[...])
        nm_ref[...] = nm
        nv_ref[...] = nv

    spec = _rows(cdim, tr)
    return pl.pallas_call(body, name=name, grid=(r // tr,), in_specs=[spec] * 4, out_specs=[spec] * 4,
                          out_shape=[jax.ShapeDtypeStruct((r, cdim), F32)] * 4,
                          compiler_params=_cparams("parallel"))(w, g, m, v)


def _adamw_whole(ws, gs, ms, vs, name):
    n = len(ws)
    c1 = 1.0 - ADAM_B1 ** ADAM_STEP
    c2 = 1.0 - ADAM_B2 ** ADAM_STEP

    def body(*refs):
        for i in range(n):
            w_ref, g_ref, m_ref, v_ref, d_ref, nm_ref, nv_ref = [refs[j * n + i] for j in range(7)]
            gv = g_ref[...]
            nm = ADAM_B1 * m_ref[...] + (1.0 - ADAM_B1) * gv
            nv = ADAM_B2 * v_ref[...] + (1.0 - ADAM_B2) * (gv * gv)
            d_ref[...] = -ADAM_LR * ((nm / c1) / (jnp.sqrt(nv / c2) + ADAM_EPS) + ADAM_WD * w_ref[...])
            nm_ref[...] = nm
            nv_ref[...] = nv

    vmem = pl.BlockSpec(memory_space=pltpu.VMEM)
    out = pl.pallas_call(body, name=name, in_specs=[vmem] * (4 * n), out_specs=[vmem] * (3 * n),
                         out_shape=[jax.ShapeDtypeStruct(a.shape, F32) for a in ws] * 3,
                         compiler_params=pltpu.CompilerParams(vmem_limit_bytes=VMEM_LIMIT))(*ws, *gs, *ms, *vs)
    return out[:n], out[n:2 * n], out[2 * n:]


SIDE_EFFECT = pltpu.SideEffectType.DATAFLOW_SIDE_EFFECTING


def _descriptors(copies, refs, send_sems, recv_sems, sem_off=0):
    x, y, c = lax.axis_index("x"), lax.axis_index("y"), lax.axis_index("c")
    pos = (x, y, c, 2 * x + y)
    out = []
    for i, (s, d, flip) in enumerate(copies):
        peer = (1 - x if "x" in flip else x, 1 - y if "y" in flip else y, 1 - c if "c" in flip else c)
        out.append(pltpu.make_async_remote_copy(
            src_ref=s(refs, refs, pos), dst_ref=d(refs, refs, pos),
            send_sem=send_sems.at[sem_off + i], recv_sem=recv_sems.at[sem_off + i],
            device_id=peer, device_id_type=MESH))
    return out


def _shifted(copies, off):
    return [(lambda I, O, pos, s=s: s(I[off:], O[off:], pos), lambda I, O, pos, d=d: d(I[off:], O[off:], pos), flip)
            for s, d, flip in copies]


BARRIER_IDS = {"c": (1, 2), "ici": (3, 4)}


def _exchange_start(name, bufs, copies, turns, after=None):
    n, nr = len(bufs), len(copies)
    na = 0 if after is None else 1
    flips = sorted({flip for _, _, flip in copies})
    kind = "c" if flips == ["c"] else "ici"
    collective_id = BARRIER_IDS[kind][turns[kind] % 2]
    turns[kind] += 1

    def body(*refs):
        x, y, c = lax.axis_index("x"), lax.axis_index("y"), lax.axis_index("c")
        barrier = pltpu.get_barrier_semaphore()
        for flip in flips:
            peer = (1 - x if "x" in flip else x, 1 - y if "y" in flip else y, 1 - c if "c" in flip else c)
            pl.semaphore_signal(barrier, inc=1, device_id=peer, device_id_type=MESH)
        pl.semaphore_wait(barrier, len(flips))
        for cp in _descriptors(copies, refs[:n], refs[n + na], refs[n + na + 1]):
            cp.start()
        token = refs[2 * n + na + 2]
        token[...] = jnp.zeros_like(token)

    hbm = pl.BlockSpec(memory_space=pltpu.HBM)
    sem = pl.BlockSpec(memory_space=pltpu.SEMAPHORE)
    out = pl.pallas_call(
        body, name=name,
        in_specs=[hbm] * n + [pl.BlockSpec(memory_space=pl.ANY)] * na,
        out_specs=(sem, sem, *[hbm] * n, pl.BlockSpec(memory_space=pltpu.VMEM)),
        out_shape=(pltpu.SemaphoreType.DMA((nr,)), pltpu.SemaphoreType.DMA((nr,)),
                   *[pltpu.HBM(b.shape, b.dtype) for b in bufs], jax.ShapeDtypeStruct((SUBLANES, LANES), F32)),
        input_output_aliases={i: 2 + i for i in range(n)},
        compiler_params=pltpu.CompilerParams(has_side_effects=SIDE_EFFECT, collective_id=collective_id),
    )(*[pltpu.with_memory_space_constraint(b, pltpu.HBM) for b in bufs], *([after] * na))
    return out[0], out[1], list(out[2:2 + n]), out[2 + n]


def _exchange_wait(name, send_sems, recv_sems, bufs, copies, after, sem_off=0):
    n = len(bufs)

    def body(*refs):
        for cp in _descriptors(copies, refs[:n], refs[n], refs[n + 1], sem_off):
            cp.wait_send()
            cp.wait_recv()

    hbm = pl.BlockSpec(memory_space=pltpu.HBM)
    sem = pl.BlockSpec(memory_space=pltpu.SEMAPHORE)
    out = pl.pallas_call(
        body, name=name,
        in_specs=[hbm] * n + [sem, sem, pl.BlockSpec(memory_space=pl.ANY)],
        out_specs=tuple([hbm] * n),
        out_shape=tuple(pltpu.HBM(b.shape, b.dtype) for b in bufs),
        input_output_aliases={i: i for i in range(n)},
        compiler_params=pltpu.CompilerParams(has_side_effects=SIDE_EFFECT),
    )(*bufs, send_sems, recv_sems, after)
    return list(out)


FIRST = ("w_in",)
MID = ("ssm_w_glu", "w_out")
LATE = ("w_up", "w_down")
GROUPS = {"first": FIRST, "mid": MID, "late": LATE}
ARRIVALS = {"first": FIRST, "mid": MID, "up": ("w_up",), "down": ("w_down",)}


def _gather_copies(names, shard_shapes):
    def region(i, chip, c):
        half_axis, shard_axis = BIG[names[i]]
        ssize = shard_shapes[i][shard_axis]
        hsize = shard_shapes[i][half_axis] // 2
        return lambda ref: _view(_view(ref, shard_axis, chip * ssize, ssize), half_axis, c * hsize, hsize)

    ici, d2d = [], []
    for i in range(len(names)):
        for flip in FLIPS:
            ici.append((lambda I, O, pos, i=i: region(i, pos[3], pos[2])(I[i]),
                        lambda I, O, pos, i=i: region(i, pos[3], pos[2])(O[i]), flip))
            d2d.append((lambda I, O, pos, i=i, flip=flip: region(i, _peer_chip(pos, flip), pos[2])(I[i]),
                        lambda I, O, pos, i=i, flip=flip: region(i, _peer_chip(pos, flip), pos[2])(O[i]), "c"))
    return ici, d2d


def _half_shape(n, shape):
    r, cdim = shape
    return (r // 2, cdim) if BIG[n][0] == 0 else (r, cdim // 2)


def _sub_shape(n, shape):
    hr, hc = _half_shape(n, shape)
    return (hr, hc // 4) if BIG[n][1] == 1 else (hr // 4, hc)


def _pair_copies(names, shapes, with_pack, dst_off):
    n = len(names)

    def other_half(i, ref, pos):
        half_axis = BIG[names[i]][0]
        hsize = shapes[i][half_axis] // 2
        return _view(ref, half_axis, (1 - pos[2]) * hsize, hsize)

    copies = [(lambda I, O, pos, i=i: other_half(i, I[i], pos), lambda I, O, pos, i=i: O[dst_off + i], "c")
              for i in range(n)]
    if with_pack:
        copies.append((lambda I, O, pos: I[n], lambda I, O, pos: O[dst_off + n], "c"))
    return copies


def _chip_copies(names, shapes, pack_rows, dst_off):
    n = len(names)

    def piece(i, ref, chip):
        shard_axis = BIG[names[i]][1]
        ssize = _sub_shape(names[i], shapes[i])[shard_axis]
        return _view(ref, shard_axis, chip * ssize, ssize)

    copies = []
    for i in range(n):
        for slot, flip in enumerate(FLIPS):
            copies.append((lambda I, O, pos, i=i, flip=flip: piece(i, I[i], _peer_chip(pos, flip)),
                           lambda I, O, pos, i=i, slot=slot: O[dst_off + i].at[slot], flip))
    if pack_rows:
        for slot, flip in enumerate(FLIPS):
            copies.append((lambda I, O, pos: _view(I[n], 0, pos[2] * (pack_rows // 2), pack_rows // 2),
                           lambda I, O, pos, slot=slot: O[dst_off + n].at[slot], flip))
    return copies


class _Exchanges:
    def __init__(self, shards, tiny, kc):
        self.kc = kc
        wb = {n: _cast_into_full(shards[n], kc, BIG[n][1], "cast_" + n) for n in BIG_NAMES}
        self.gathering, self.forwarding, self.pairing, self.reducing = {}, {}, {}, {}
        self.turns = {"c": 0, "ici": 0}
        tiny_copies = [(lambda I, O, pos: I[0], lambda I, O, pos: O[1].at[pos[3]], flip) for flip in FLIPS]
        self.gathering["tiny"] = (0, 0, 2, tiny_copies, None)
        bufs, copies = [tiny, lax.empty((4,) + tiny.shape, F32)], list(tiny_copies)
        for group, names in ARRIVALS.items():
            ici, d2d = _gather_copies(names, [shards[n].shape for n in names])
            self.gathering[group] = (len(bufs), len(copies), len(names), ici, d2d)
            copies += _shifted(ici, len(bufs))
            bufs += [wb[n] for n in names]
        self.started = _exchange_start("gather_start", bufs, copies, self.turns)
        self.zero = self.started[3][0, 0]

    def _arrived(self, group, after):
        buf_off, sem_off, n, ici, _ = self.gathering[group]
        send_sems, recv_sems, bufs, _ = self.started
        return _exchange_wait("gather_%s_wait" % group, send_sems, recv_sems, bufs[buf_off:buf_off + n], ici, after,
                              sem_off)

    def small_params(self, kc):
        tiny, got = self._arrived("tiny", self.started[3])
        return lax.dynamic_update_index_in_dim(got, tiny, kc[0], 0)

    def forward(self, group, after):
        d2d = self.gathering[group][4]
        self.forwarding[group] = (_exchange_start("forward_%s_start" % group, self._arrived(group, after), d2d,
                                                  self.turns), d2d)
        return self.forwarding[group][0][3]

    def weights(self, group, after):
        if group not in self.forwarding:
            after = self.forward(group, after)
        (send_sems, recv_sems, bufs, _), d2d = self.forwarding[group]
        full = _exchange_wait("forward_%s_wait" % group, send_sems, recv_sems, bufs, d2d, after)
        return dict(zip(ARRIVALS[group], full))

    def grads_ready(self, group, grads):
        names = GROUPS[group]
        gs = [grads[n] for n in names]
        land = [lax.empty(_half_shape(n, g.shape), F32) for n, g in zip(names, gs)]
        copies = _pair_copies(names, [g.shape for g in gs], False, len(names))
        started = _exchange_start("pair_%s_start" % group, gs + land, copies, self.turns)
        self.pairing[group] = (started, copies)
        return started[3]

    def grads_send(self, group, after):
        names = GROUPS[group]
        n = len(names)
        (send_sems, recv_sems, bufs, _), copies = self.pairing[group]
        bufs = _exchange_wait("pair_%s_wait" % group, send_sems, recv_sems, bufs, copies, after)
        chip = [_pair_sum(bufs[i], bufs[n + i], self.kc, BIG[names[i]][0], "pair_sum_" + names[i], BF16)
                for i in range(n)]
        shapes = [bufs[i].shape for i in range(n)]
        land = [lax.empty((3,) + _sub_shape(names[i], shapes[i]), BF16) for i in range(n)]
        copies = _chip_copies(names, shapes, 0, n)
        started = _exchange_start("reduce_%s_start" % group, chip + land, copies, self.turns)
        self.reducing[group] = (started, copies)
        return started[3]

    def finish_pack(self, pack):
        kc = self.kc
        prow = pack.shape[0] // 2
        recv = _exchange("reduce_d2d", [pack], [jax.ShapeDtypeStruct(pack.shape, F32)], {}, [],
                         _pair_copies((), [], True, 0))
        chip_pack = _pair_sum(pack, recv[0], kc, None, "pair_sum_pack", F32)
        copies = _chip_copies((), [], pack.shape[0], 1)
        land = lax.empty((3, prow, pack.shape[1]), F32)
        pack_sems_s, pack_sems_r, pack_bufs, after = _exchange_start("reduce_pack_start", [chip_pack, land], copies,
                                                                     self.turns)

        names, chips, recvs = (), [], []
        for group, group_names in GROUPS.items():
            (send_sems, recv_sems, bufs, _), group_copies = self.reducing[group]
            bufs = _exchange_wait("reduce_%s_wait" % group, send_sems, recv_sems, bufs, group_copies, after)
            n = len(group_names)
            names, chips, recvs = names + group_names, chips + bufs[:n], recvs + bufs[n:]
            after = bufs[n]
        total = [_chip_sum(chips[i], recvs[i], kc, BIG[n][1], BIG[n][0], "chip_sum_" + n)
                 for i, n in enumerate(names)]

        def my_half(half_axis, ref, pos):
            hsize = ref.shape[half_axis] // 2
            return _view(ref, half_axis, pos[2] * hsize, hsize)

        swap = [(lambda I, O, pos, i=i, n=n: my_half(BIG[n][0], I[i], pos),
                 lambda I, O, pos, i=i, n=n: my_half(BIG[n][0], O[i], pos), "c") for i, n in enumerate(names)]
        self.swapping = (_exchange_start("swap_start", total, swap, self.turns), swap, names)

        chip_pack, recv_pack = _exchange_wait("reduce_pack_wait", pack_sems_s, pack_sems_r, pack_bufs, copies,
                                              self.swapping[0][3])
        total_pack = _chip_sum(chip_pack, recv_pack, kc, None, 0, "chip_sum_pack")
        swap = [(lambda I, O, pos: my_half(0, I[0], pos), lambda I, O, pos: my_half(0, O[0], pos), "c")]
        return _exchange("swap_pack", [total_pack], [jax.ShapeDtypeStruct(pack.shape, F32)], {0: 0}, [], swap)[0]

    def finish_big(self, after):
        (send_sems, recv_sems, bufs, _), swap, names = self.swapping
        return dict(zip(names, _exchange_wait("swap_wait", send_sems, recv_sems, bufs, swap, after)))


WEIGHTS = ("meta_tokens", "norm_mix_g", "w_in", "conv_w", "ssm_lam_re", "ssm_lam_im", "ssm_log_dt", "ssm_b_re",
           "ssm_b_im", "ssm_c_re", "ssm_c_im", "ssm_d", "ssm_w_glu", "gain_conv_out", "gain_ssm_out", "w_out",
           "norm_ffn_g", "w_up", "ffn_conv_w", "ffn_conv_b", "w_down", "norm_final_g")
TINY_SHARDED = ("meta_tokens", "conv_w", "ffn_conv_w")
REPLICATED = tuple(n for n in WEIGHTS if n not in BIG and n not in TINY_SHARDED)
PACK_COLS = 512


def _pack(arrays, row_mult, cols):
    flat = jnp.concatenate([a.reshape(-1).astype(F32) for a in arrays])
    n = flat.shape[0]
    total = -(-n // (row_mult * cols)) * (row_mult * cols)
    return jnp.concatenate([flat, jnp.zeros((total - n,), F32)]).reshape(total // cols, cols)


def _unpack(packed, shapes):
    flat = packed.reshape(-1)
    out, off = [], 0
    for s in shapes:
        n = math.prod(s)
        out.append(flat[off:off + n].reshape(s))
        off += n
    return out


def kernel(x, meta_tokens, norm_mix_g, w_in, conv_w, ssm_lam_re, ssm_lam_im, ssm_log_dt, ssm_b_re, ssm_b_im, ssm_c_re, ssm_c_im, ssm_d, ssm_w_glu, gain_conv_out, gain_ssm_out, w_out, norm_ffn_g, w_up, ffn_conv_w, ffn_conv_b, w_down, norm_final_g, loss_target, m_meta_tokens, m_norm_mix_g, m_w_in, m_conv_w, m_ssm_lam_re, m_ssm_lam_im, m_ssm_log_dt, m_ssm_b_re, m_ssm_b_im, m_ssm_c_re, m_ssm_c_im, m_ssm_d, m_ssm_w_glu, m_gain_conv_out, m_gain_ssm_out, m_w_out, m_norm_ffn_g, m_w_up, m_ffn_conv_w, m_ffn_conv_b, m_w_down, m_norm_final_g, v_meta_tokens, v_norm_mix_g, v_w_in, v_conv_w, v_ssm_lam_re, v_ssm_lam_im, v_ssm_log_dt, v_ssm_b_re, v_ssm_b_im, v_ssm_c_re, v_ssm_c_im, v_ssm_d, v_ssm_w_glu, v_gain_conv_out, v_gain_ssm_out, v_w_out, v_norm_ffn_g, v_w_up, v_ffn_conv_w, v_ffn_conv_b, v_w_down, v_norm_final_g):
    args = dict(locals())
    w = {n: args[n] for n in WEIGHTS}
    mom = {n: args["m_" + n] for n in WEIGHTS}
    var = {n: args["v_" + n] for n in WEIGHTS}
    kx, ky, kc_ = lax.axis_index("x"), lax.axis_index("y"), lax.axis_index("c")
    chip = 2 * kx + ky
    kc = jnp.stack([chip, kc_]).astype(jnp.int32)

    def squeeze(n, a):
        if n == "meta_tokens":
            return a
        if n == "norm_final_g":
            return a.reshape(1, -1)
        a = a[0]
        return a.reshape(1, -1) if a.ndim == 1 else a

    wl = {n: squeeze(n, w[n]) for n in WEIGHTS}
    ml = {n: squeeze(n, mom[n]) for n in WEIGHTS}
    vl = {n: squeeze(n, var[n]) for n in WEIGHTS}

    tiny = _pack([wl[n] for n in TINY_SHARDED], SUBLANES, LANES)
    ex = _Exchanges({n: wl[n] for n in BIG_NAMES}, tiny, kc)
    tiny_shapes = [wl[n].shape for n in TINY_SHARDED]
    tiny_all = ex.small_params(kc)
    tiny_parts = [_unpack(tiny_all[k], tiny_shapes) for k in range(4)]
    p = {n: wl[n] for n in WEIGHTS if n not in BIG}
    for j, n in enumerate(TINY_SHARDED):
        p[n] = jnp.concatenate([tiny_parts[k][j] for k in range(4)], axis=1)
    p["ssm_log_dt"] = wl["ssm_log_dt"].reshape(-1)

    loss_local, grad_x, grads = _local_step(x[0], loss_target[0], p, ex)

    small_names = REPLICATED + TINY_SHARDED
    small_shapes = [tuple(grads[n].shape) for n in small_names] + [(1,)]
    pack = _pack([grads[n] for n in small_names] + [loss_local.reshape(1)], 2 * 16, PACK_COLS)
    g_pack = ex.finish_pack(pack)
    g_small = dict(zip(small_names + ("loss",), _unpack(g_pack, small_shapes)))
    loss = g_small["loss"][0]
    swapped = ("ssm_b_re", "ssm_b_im")

    def view(n, a):
        if n in swapped:
            return jnp.swapaxes(a, -1, -2)
        return a.reshape(1, -1) if a.ndim == 1 else a

    g = {}
    for n in REPLICATED:
        g[n] = g_small[n].reshape(view(n, w[n]).shape)
    for n in TINY_SHARDED:
        cols = wl[n].shape[1]
        g[n] = lax.dynamic_slice_in_dim(g_small[n], chip * cols, cols, axis=1).reshape(w[n].shape)
    delta, new_m, new_v = {}, {}, {}
    small = [[view(n, d[n]) for n in small_names] for d in (w, mom, var)]
    small.insert(1, [g[n] for n in small_names])
    for d, outs in zip((delta, new_m, new_v), _adamw_whole(*small, "adamw_small")):
        d.update(zip(small_names, outs))
    for d in (g, delta, new_m, new_v):
        d.update({n: jnp.swapaxes(d[n], -1, -2) for n in swapped})
    g_big = ex.finish_big(delta[small_names[0]])
    for n in BIG_NAMES:
        g[n], delta[n], new_m[n], new_v[n] = _adamw(wl[n], g_big[n], ml[n], vl[n], "adamw_" + n)

    def like(n, a):
        return a.reshape(w[n].shape)

    return (loss, grad_x[None], *[like(n, g[n]) for n in WEIGHTS], *[like(n, delta[n]) for n in WEIGHTS],
            *[like(n, new_m[n]) for n in WEIGHTS], *[like(n, new_v[n]) for n in WEIGHTS])
```

```python
import functools
import math

import jax
import jax.numpy as jnp
from jax import lax
from jax.experimental import pallas as pl
from jax.experimental.pallas import tpu as pltpu

F32 = jnp.float32
BF16 = jnp.bfloat16
MESH = pl.DeviceIdType.MESH

N_META = 16
N_GROUPS = 32
GROUP = 16
STATE = 64
RMS_EPS = 1e-6
ADAM_LR = 0.001
ADAM_B1 = 0.9
ADAM_B2 = 0.999
ADAM_EPS = 1e-08
ADAM_WD = 0.01
ADAM_STEP = 10

LANES = 128
SUBLANES = 8
ROW_ALIGN = 128
ROW_TILES = 4
VMEM_LIMIT = 52 * 1024 * 1024
MM_VMEM_BUDGET = 40 * 1024 * 1024
GELU_C = math.sqrt(2.0 / math.pi)
GELU_A = 0.044715


def _cparams(*sem):
    return pltpu.CompilerParams(dimension_semantics=sem, vmem_limit_bytes=VMEM_LIMIT)


def _pick_tile(dim, cap, mult):
    best = None
    for t in range(mult, min(dim, cap) + 1, mult):
        if dim % t == 0:
            best = t
    return best if best is not None else dim


def _mm(a, b, mode, name, out_dtype=F32, acc_in=None, after=None):
    if mode == "tn":
        kdim, m = a.shape
    else:
        m, kdim = a.shape
    n = b.shape[0] if mode == "nt" else b.shape[1]
    tm = _pick_tile(m, 1408, LANES if mode == "tn" else 16)
    tk = _pick_tile(kdim, 2816, LANES)
    nk = kdim // tk
    out_bytes = jnp.dtype(out_dtype).itemsize
    for cap in (1408, 1024, 512, 256, LANES):
        tn = _pick_tile(n, cap, LANES)
        blocks = 2 * (tm * tk * 2 + tk * tn * 2 + tm * tn * out_bytes * (2 if acc_in is not None else 1))
        if blocks + (tm * tn * 4 if nk > 1 else 0) <= MM_VMEM_BUDGET:
            break
    has_acc = acc_in is not None

    def body(*refs):
        if after is not None:
            refs = refs[1:]
        if has_acc:
            a_ref, b_ref, c_ref, o_ref = refs[:4]
            rest = refs[4:]
        else:
            a_ref, b_ref, o_ref = refs[:3]
            c_ref = None
            rest = refs[3:]
        if mode == "nn":
            p = jnp.dot(a_ref[...], b_ref[...], preferred_element_type=F32)
        elif mode == "nt":
            p = lax.dot_general(a_ref[...], b_ref[...], (((1,), (1,)), ((), ())), preferred_element_type=F32)
        else:
            p = lax.dot_general(a_ref[...], b_ref[...], (((0,), (0,)), ((), ())), preferred_element_type=F32)
        if nk == 1:
            if has_acc:
                p = p + c_ref[...]
            o_ref[...] = p.astype(out_dtype)
        else:
            acc_ref = rest[0]
            k = pl.program_id(2)

            @pl.when(k == 0)
            def _():
                acc_ref[...] = p + c_ref[...] if has_acc else p

            @pl.when(k > 0)
            def _():
                acc_ref[...] += p

            @pl.when(k == nk - 1)
            def _():
                o_ref[...] = acc_ref[...].astype(out_dtype)

    if mode == "tn":
        a_spec = pl.BlockSpec((tk, tm), lambda i, j, k: (k, i))
    else:
        a_spec = pl.BlockSpec((tm, tk), lambda i, j, k: (i, k))
    if mode == "nt":
        b_spec = pl.BlockSpec((tn, tk), lambda i, j, k: (j, k))
    else:
        b_spec = pl.BlockSpec((tk, tn), lambda i, j, k: (k, j))
    o_spec = pl.BlockSpec((tm, tn), lambda i, j, k: (i, j))
    in_specs = [a_spec, b_spec] + ([o_spec] if has_acc else [])
    args = (a, b) + ((acc_in,) if has_acc else ())
    if after is not None:
        in_specs = [pl.BlockSpec(memory_space=pl.ANY)] + in_specs
        args = (after,) + args
    return pl.pallas_call(
        body, name=name, grid=(m // tm, n // tn, nk),
        in_specs=in_specs, out_specs=o_spec,
        out_shape=jax.ShapeDtypeStruct((m, n), out_dtype),
        scratch_shapes=[pltpu.VMEM((tm, tn), F32)] if nk > 1 else [],
        compiler_params=_cparams("parallel", "parallel", "arbitrary"),
    )(*args)


def _mm_rows(a, b, mode, name, ins, outs, epilogue, scratch=()):
    m, kdim = a.shape
    n = b.shape[0] if mode == "nt" else b.shape[1]
    tm = m // ROW_TILES
    tk = _pick_tile(kdim, 2816, LANES)
    nk = kdim // tk
    ni, no = len(ins), len(outs)

    def body(*refs):
        a_ref, b_ref = refs[:2]
        in_refs, out_refs, rest = refs[2:2 + ni], refs[2 + ni:2 + ni + no], refs[2 + ni + no:]
        k, i = pl.program_id(0), pl.program_id(1)
        if mode == "nn":
            p = jnp.dot(a_ref[...], b_ref[...], preferred_element_type=F32)
        else:
            p = lax.dot_general(a_ref[...], b_ref[...], (((1,), (1,)), ((), ())), preferred_element_type=F32)
        if nk == 1:
            epilogue(p, i, in_refs, out_refs, rest)
        else:
            acc_ref = rest[0]
            rows = pl.ds(pl.multiple_of(i * tm, SUBLANES), tm)

            @pl.when(k == 0)
            def _():
                acc_ref[rows, :] = p

            @pl.when(jnp.logical_and(k > 0, k < nk - 1))
            def _():
                acc_ref[rows, :] += p

            @pl.when(k == nk - 1)
            def _():
                epilogue(acc_ref[rows, :] + p, i, in_refs, out_refs, rest[1:])

    tile = (lambda k, i: i) if nk == 1 else (lambda k, i: jnp.where(k == nk - 1, i, 0))

    def spec(shape, kind):
        if kind == "rows":
            return pl.BlockSpec((tm,) + tuple(shape[1:]), lambda k, i: (tile(k, i),) + (0,) * (len(shape) - 1))
        if kind == "whole":
            return pl.BlockSpec(tuple(shape), lambda k, i: (0,) * len(shape))
        return pl.BlockSpec(memory_space=pl.ANY)

    a_spec = pl.BlockSpec((tm, tk), lambda k, i: (i, k))
    b_spec = pl.BlockSpec((n, tk), lambda k, i: (0, k)) if mode == "nt" else pl.BlockSpec((tk, n), lambda k, i: (k, 0))
    return pl.pallas_call(
        body, name=name, grid=(nk, ROW_TILES),
        in_specs=[a_spec, b_spec] + [spec(x.shape, kind) for x, kind in ins],
        out_specs=[spec(shape, kind) for shape, _, kind in outs],
        out_shape=[jax.ShapeDtypeStruct(shape, dtype) for shape, dtype, _ in outs],
        scratch_shapes=([pltpu.VMEM((m, n), F32)] if nk > 1 else []) + list(scratch),
        compiler_params=_cparams("arbitrary", "arbitrary"),
    )(a, b, *[x for x, _ in ins])


def _rows(shape_cols, tr, dtype=None):
    return pl.BlockSpec((tr, shape_cols), lambda i: (i, 0))


def _const(shape):
    return pl.BlockSpec(shape, lambda i: (0,) * len(shape))


def _rms(x):
    return lax.rsqrt(jnp.mean(x * x, axis=-1, keepdims=True) + RMS_EPS)


def _rms_bwd(x, r, g, dy):
    xn = x * r
    dxn = dy * g
    dx = r * (dxn - xn * jnp.mean(dxn * xn, axis=-1, keepdims=True))
    return dx, dy * xn


def _gelu(y):
    return 0.5 * y * (1.0 + jnp.tanh(GELU_C * (y + GELU_A * y * y * y)))


def _gelu_grad(y):
    t = jnp.tanh(GELU_C * (y + GELU_A * y * y * y))
    return 0.5 * (1.0 + t) + 0.5 * y * (1.0 - t * t) * GELU_C * (1.0 + 3.0 * GELU_A * y * y)


def _sigmoid(z):
    return 1.0 / (1.0 + jnp.exp(-z))


def _proj_res_norm(a, w, h, g, after, name):
    def epilogue(p, i, ins, outs, _):
        x = ins[0][...] + p
        outs[0][...] = x
        outs[1][...] = (x * _rms(x) * ins[1][...]).astype(BF16)

    return _mm_rows(a, w, "nn", name, [(h, "rows"), (g, "whole"), (after, "hbm")],
                    [(h.shape, F32, "rows"), (h.shape, BF16, "rows")], epilogue)


def _proj_norm_bwd(da, w, h, g, dres, after, name):
    d = h.shape[1]

    def epilogue(p, i, ins, outs, _):
        x = ins[0][...]
        dx, dgs = _rms_bwd(x, _rms(x), ins[1][...], p)
        dh = ins[2][...] + dx
        outs[0][...] = dh
        outs[1][...] = dh.astype(BF16)

        @pl.when(i == 0)
        def _():
            outs[2][...] = jnp.zeros_like(outs[2])

        outs[2][...] += jnp.sum(dgs, axis=0, keepdims=True)

    return _mm_rows(da, w, "nt", name, [(h, "rows"), (g, "whole"), (dres, "rows"), (after, "hbm")],
                    [(h.shape, F32, "rows"), (h.shape, BF16, "rows"), ((1, d), F32, "whole")], epilogue)


def _proj_input_norm_bwd(da, w, h, g, dres, after, n_real, name):
    tp, d = h.shape
    tr = tp // ROW_TILES

    def epilogue(p, i, ins, outs, scratch):
        h_ref, g_ref, dres_ref, _ = ins
        dx_ref, dmeta_ref, dg_ref = outs
        stage, sem = scratch
        x = h_ref[...]
        dx, dgs = _rms_bwd(x, _rms(x), g_ref[...], p)
        stage[...] = dres_ref[...] + dx

        @pl.when(i == 0)
        def _():
            dg_ref[...] = jnp.zeros_like(dg_ref)
            dmeta_ref[...] = stage[:N_META, :]

        dg_ref[...] += jnp.sum(dgs, axis=0, keepdims=True)
        for t in range(ROW_TILES):
            lo, hi = max(t * tr, N_META), min((t + 1) * tr, n_real)
            if hi > lo:
                @pl.when(i == t)
                def _(t=t, lo=lo, hi=hi):
                    cp = pltpu.make_async_copy(stage.at[pl.ds(lo - t * tr, hi - lo), :],
                                               dx_ref.at[pl.ds(lo - N_META, hi - lo), :], sem)
                    cp.start()
                    cp.wait()

    return _mm_rows(da, w, "nt", name, [(h, "rows"), (g, "whole"), (dres, "rows"), (after, "hbm")],
                    [((n_real - N_META, d), F32, "hbm"), ((N_META, d), F32, "whole"), ((1, d), F32, "whole")],
                    epilogue, scratch=[pltpu.VMEM((tr, d), F32), pltpu.SemaphoreType.DMA])


def _load_token_rows(tok_hbm, buf, sem, tr, n_real, head=None, wait=False, i=None):
    i = pl.program_id(0) if i is None else i
    for t in range(ROW_TILES):
        base = t * tr
        lo, hi = max(base, N_META), min(base + tr, n_real)

        @pl.when(i == t)
        def _(base=base, lo=lo, hi=hi):
            if hi > lo:
                cp = pltpu.make_async_copy(tok_hbm.at[pl.ds(lo - N_META, hi - lo), :],
                                           buf.at[pl.ds(lo - base, hi - lo), :], sem)
                if wait:
                    cp.wait()
                    return
                cp.start()
            if wait:
                return
            if base < N_META:
                buf[0:N_META - base, :] = (jnp.zeros((N_META - base, buf.shape[1]), F32) if head is None
                                           else head[base:N_META, :])
            if hi < base + tr:
                buf[max(hi, base) - base:tr, :] = jnp.zeros((base + tr - max(hi, base), buf.shape[1]), F32)


def _input_norm_fwd(x, meta, g, tp, name):
    seq, d = x.shape
    tr = tp // ROW_TILES
    n_real = N_META + seq

    def body(x_hbm, meta_ref, g_ref, h_ref, hn_ref, buf, sem):
        _load_token_rows(x_hbm, buf, sem, tr, n_real, head=meta_ref)
        _load_token_rows(x_hbm, buf, sem, tr, n_real, wait=True)
        h = buf[...]
        h_ref[...] = h
        hn_ref[...] = (h * _rms(h) * g_ref[...]).astype(BF16)

    return pl.pallas_call(
        body, name=name, grid=(ROW_TILES,),
        in_specs=[pl.BlockSpec(memory_space=pl.ANY), _const((N_META, d)), _const((1, d))],
        out_specs=[_rows(d, tr), _rows(d, tr)],
        out_shape=[jax.ShapeDtypeStruct((tp, d), F32), jax.ShapeDtypeStruct((tp, d), BF16)],
        scratch_shapes=[pltpu.VMEM((tr, d), F32), pltpu.SemaphoreType.DMA],
        compiler_params=_cparams("arbitrary"))(x, meta, g)


def _proj_loss_bwd(act, w, h1, target, g, n_real, name):
    tp, d = h1.shape
    tr = tp // ROW_TILES

    def epilogue(p, i, ins, outs, scratch):
        h1_ref, t_hbm, g_ref = ins
        loss_ref, dh_ref, dhb_ref, dg_ref = outs
        t_buf, sem = scratch
        _load_token_rows(t_hbm, t_buf, sem, tr, n_real, i=i)
        x = h1_ref[...] + p
        r = _rms(x)
        row = i * tr + lax.broadcasted_iota(jnp.int32, (tr, d), 0)
        valid = (row >= N_META) & (row < n_real)
        _load_token_rows(t_hbm, t_buf, sem, tr, n_real, wait=True, i=i)
        e = jnp.where(valid, x * r * g_ref[...] - t_buf[...], 0.0)
        dx, dgs = _rms_bwd(x, r, g_ref[...], e * (1.0 / d))
        dh_ref[...] = dx
        dhb_ref[...] = dx.astype(BF16)

        @pl.when(i == 0)
        def _():
            dg_ref[...] = jnp.zeros_like(dg_ref)
            loss_ref[...] = jnp.zeros_like(loss_ref)

        dg_ref[...] += jnp.sum(dgs, axis=0, keepdims=True)
        loss_ref[...] += (0.5 / d) * jnp.sum(jnp.sum(e * e, axis=0, keepdims=True), axis=1, keepdims=True)

    return _mm_rows(act, w, "nn", name, [(h1, "rows"), (target, "hbm"), (g, "whole")],
                    [((1, LANES), F32, "whole"), ((tp, d), F32, "rows"), ((tp, d), BF16, "rows"),
                     ((1, d), F32, "whole")],
                    epilogue, scratch=[pltpu.VMEM((tr, d), F32), pltpu.SemaphoreType.DMA])


def _mix_fwd(co, y, z, gc, gs, name):
    tp, dh = co.shape
    tr = tp // ROW_TILES

    def body(co_ref, y_ref, z_ref, gc_ref, gs_ref, m_ref):
        c = co_ref[...]
        m_ref[:, :dh] = (c * _rms(c) * gc_ref[...]).astype(BF16)
        so = _gelu(y_ref[...]) * _sigmoid(z_ref[...])
        m_ref[:, dh:] = (so * _rms(so) * gs_ref[...]).astype(BF16)

    return pl.pallas_call(
        body, name=name, grid=(ROW_TILES,),
        in_specs=[_rows(dh, tr)] * 3 + [_const((1, dh))] * 2,
        out_specs=_rows(2 * dh, tr),
        out_shape=jax.ShapeDtypeStruct((tp, 2 * dh), BF16),
        compiler_params=_cparams("parallel"))(co, y, z, gc, gs)


def _proj_mix_bwd(dh1b, w, co, y, z, gc, gs, name):
    tp, dh = co.shape

    def epilogue(p, i, ins, outs, _):
        co_ref, y_ref, z_ref, gc_ref, gs_ref = ins
        dco_ref, dz_ref, dgp_ref, dgc_ref, dgs_ref = outs
        c = co_ref[...]
        dco, dgc = _rms_bwd(c, _rms(c), gc_ref[...], p[:, :dh])
        dco_ref[...] = dco
        gl = _gelu(y_ref[...])
        sg = _sigmoid(z_ref[...])
        so = gl * sg
        dso, dgs = _rms_bwd(so, _rms(so), gs_ref[...], p[:, dh:])
        dz_ref[...] = (dso * gl * sg * (1.0 - sg)).astype(BF16)
        dgp_ref[...] = dso * sg

        @pl.when(i == 0)
        def _():
            dgc_ref[...] = jnp.zeros_like(dgc_ref)
            dgs_ref[...] = jnp.zeros_like(dgs_ref)

        dgc_ref[...] += jnp.sum(dgc, axis=0, keepdims=True)
        dgs_ref[...] += jnp.sum(dgs, axis=0, keepdims=True)

    return _mm_rows(dh1b, w, "nt", name,
                    [(co, "rows"), (y, "rows"), (z, "rows"), (gc, "whole"), (gs, "whole")],
                    [((tp, dh), F32, "rows"), ((tp, dh), BF16, "rows"), ((tp, dh), F32, "rows"),
                     ((1, dh), F32, "whole"), ((1, dh), F32, "whole")], epilogue)


def _shift_down(x, k):
    row = lax.broadcasted_iota(jnp.int32, x.shape, 0)
    return jnp.where(row >= k, pltpu.roll(x, k, 0), 0.0)


def _shift_up(x, k):
    n = x.shape[0]
    row = lax.broadcasted_iota(jnp.int32, x.shape, 0)
    return jnp.where(row < n - k, pltpu.roll(x, n - k, 0), 0.0)


def _dwconv(x, w_ref):
    return w_ref[2:3, :] * x + w_ref[1:2, :] * _shift_down(x, 1) + w_ref[0:1, :] * _shift_down(x, 2)


def _dwconv_bwd(x, dy, w_ref):
    dx = w_ref[2:3, :] * dy + w_ref[1:2, :] * _shift_up(dy, 1) + w_ref[0:1, :] * _shift_up(dy, 2)
    dw = jnp.concatenate([jnp.sum(dy * _shift_down(x, 2), axis=0, keepdims=True),
                          jnp.sum(dy * _shift_down(x, 1), axis=0, keepdims=True),
                          jnp.sum(dy * x, axis=0, keepdims=True)], axis=0)
    return dx, dw


def _interleave(dst, src):
    seg_rows = src.shape[0] // SUBLANES
    for seg in range(SUBLANES):
        dst[pl.ds(seg, seg_rows, stride=SUBLANES), :] = src[seg * seg_rows:(seg + 1) * seg_rows, :]


def _deinterleave(dst, src):
    seg_rows = src.shape[0] // SUBLANES
    for seg in range(SUBLANES):
        dst[seg * seg_rows:(seg + 1) * seg_rows, :] = src[pl.ds(seg, seg_rows, stride=SUBLANES), :]


def _segment_shift(x, reverse):
    row = lax.broadcasted_iota(jnp.int32, x.shape, 0)
    if reverse:
        return jnp.where(row < SUBLANES - 1, pltpu.roll(x, SUBLANES - 1, 0), 0.0)
    return jnp.where(row >= 1, pltpu.roll(x, 1, 0), 0.0)


def _scan(s_re, s_im, pw_ref, reverse, pair=None):
    n_steps = s_re.shape[0] // SUBLANES
    n_strips = s_re.shape[1] // LANES
    sign = -1.0 if reverse else 1.0
    strips = [slice(st * LANES, (st + 1) * LANES) for st in range(n_strips)]

    def rows_of(j):
        step = (n_steps - 1 - j) if reverse else j
        return pl.ds(pl.multiple_of(step * SUBLANES, SUBLANES), SUBLANES)

    a = [(jnp.broadcast_to(pw_ref[0, 0:1, lanes], (SUBLANES, LANES)),
          sign * jnp.broadcast_to(pw_ref[1, 0:1, lanes], (SUBLANES, LANES))) for lanes in strips]

    def local(i, carry):
        for half in range(2):
            rows = rows_of(2 * i + half)
            out = []
            for st, lanes in enumerate(strips):
                (ar, ai), cr, ci = a[st], carry[2 * st], carry[2 * st + 1]
                xr = s_re[rows, lanes] + (ar * cr - ai * ci)
                xi = s_im[rows, lanes] + (ar * ci + ai * cr)
                s_re[rows, lanes] = xr
                s_im[rows, lanes] = xi
                out += [xr, xi]
            carry = tuple(out)
        return carry

    zero = jnp.zeros((SUBLANES, LANES), F32)
    ends = lax.fori_loop(0, n_steps // 2, local, (zero,) * (2 * n_strips))

    entering = []
    row = lax.broadcasted_iota(jnp.int32, (SUBLANES, LANES), 0)
    for st, lanes in enumerate(strips):
        tr, ti = ends[2 * st], ends[2 * st + 1]
        mr = jnp.broadcast_to(pw_ref[0, n_steps - 1:n_steps, lanes], (SUBLANES, LANES))
        mi = sign * jnp.broadcast_to(pw_ref[1, n_steps - 1:n_steps, lanes], (SUBLANES, LANES))
        for k in (1, 2, 4):
            keep = (row < SUBLANES - k) if reverse else (row >= k)
            rr = jnp.where(keep, pltpu.roll(tr, SUBLANES - k if reverse else k, 0), 0.0)
            ri = jnp.where(keep, pltpu.roll(ti, SUBLANES - k if reverse else k, 0), 0.0)
            tr, ti = tr + (mr * rr - mi * ri), ti + (mr * ri + mi * rr)
            mr, mi = mr * mr - mi * mi, 2.0 * mr * mi
        entering += [_segment_shift(tr, reverse), _segment_shift(ti, reverse)]

    def fix(i, carry):
        carry, sums = carry[:2 * n_strips], carry[2 * n_strips:]
        for half in range(2):
            j = 2 * i + half
            rows = rows_of(j)
            out, acc = [], []
            for st, lanes in enumerate(strips):
                (ar, ai), cr, ci = a[st], carry[2 * st], carry[2 * st + 1]
                cr, ci = ar * cr - ai * ci, ar * ci + ai * cr
                xr = s_re[rows, lanes] + cr
                xi = s_im[rows, lanes] + ci
                s_re[rows, lanes] = xr
                s_im[rows, lanes] = xi
                out += [cr, ci]
                if pair is not None:
                    p_rows = rows_of(jnp.minimum(j + 1, n_steps - 1))
                    keep = (j < n_steps - 1).astype(F32)
                    pr = pair[0][p_rows, lanes] * keep
                    pi = pair[1][p_rows, lanes] * keep
                    acc += [sums[2 * st] + (xr * pr + xi * pi), sums[2 * st + 1] + (xi * pr - xr * pi)]
            carry, sums = tuple(out), tuple(acc)
        return carry + sums

    n_sums = 0 if pair is None else 2 * n_strips
    out = lax.fori_loop(0, n_steps // 2, fix, tuple(entering) + (zero,) * n_sums)
    return out[2 * n_strips:]


def _seq_fwd(proj, conv_w, bc_re, bc_im, cc_re, cc_im, dskip, a_pow, name):
    tp = proj.shape[0]
    dh = proj.shape[1] // 4
    nq = dh // LANES
    sw = STATE * N_GROUPS // nq

    def body(b_ref, c_ref, v_ref, u_ref, w_ref, bre_ref, bim_ref, cre_ref, cim_ref, d_ref, pw_ref,
             co_ref, y_ref, g_ref, s_re, s_im, u_il, y_il):
        co_ref[...] = b_ref[...] * _dwconv(c_ref[...] * v_ref[...], w_ref)
        _interleave(u_il, u_ref)
        ub = u_il[...].astype(BF16)
        s_re[...] = jnp.dot(ub, bre_ref[...], preferred_element_type=F32)
        s_im[...] = jnp.dot(ub, bim_ref[...], preferred_element_type=F32)
        _scan(s_re, s_im, pw_ref, False)
        y_il[...] = (jnp.dot(s_re[...].astype(BF16), cre_ref[...], preferred_element_type=F32)
                     - jnp.dot(s_im[...].astype(BF16), cim_ref[...], preferred_element_type=F32))
        _deinterleave(y_ref, y_il)
        y = y_ref[...] + d_ref[...] * u_ref[...]
        y_ref[...] = y
        g_ref[...] = _gelu(y).astype(BF16)

    col = lambda off: pl.BlockSpec((tp, LANES), lambda q, off=off: (0, off * nq + q))
    blk = pl.BlockSpec((tp, LANES), lambda q: (0, q))
    return pl.pallas_call(
        body, name=name, grid=(nq,),
        in_specs=[col(0), col(1), col(2), col(3),
                  pl.BlockSpec((3, LANES), lambda q: (0, q)),
                  pl.BlockSpec((LANES, sw), lambda q: (0, q)), pl.BlockSpec((LANES, sw), lambda q: (0, q)),
                  pl.BlockSpec((sw, LANES), lambda q: (q, 0)), pl.BlockSpec((sw, LANES), lambda q: (q, 0)),
                  pl.BlockSpec((1, LANES), lambda q: (0, q)),
                  pl.BlockSpec((2, tp // SUBLANES, sw), lambda q: (0, 0, q))],
        out_specs=[blk, blk, blk, pl.BlockSpec((tp, sw), lambda q: (0, q)), pl.BlockSpec((tp, sw), lambda q: (0, q))],
        out_shape=[jax.ShapeDtypeStruct((tp, dh), F32), jax.ShapeDtypeStruct((tp, dh), F32),
                   jax.ShapeDtypeStruct((tp, dh), BF16),
                   jax.ShapeDtypeStruct((tp, nq * sw), F32), jax.ShapeDtypeStruct((tp, nq * sw), F32)],
        scratch_shapes=[pltpu.VMEM((tp, LANES), F32), pltpu.VMEM((tp, LANES), F32)],
        compiler_params=_cparams("parallel"),
    )(proj, proj, proj, proj, conv_w, bc_re, bc_im, cc_re, cc_im, dskip, a_pow)


def _conv_bwd(proj, dco, conv_w, name):
    tp = proj.shape[0]
    dh = proj.shape[1] // 4
    nq = dh // LANES

    def body(b_ref, c_ref, v_ref, dco_ref, w_ref, dproj_ref, dw_ref, stage, sem):
        q = pl.program_id(0)
        cg = c_ref[...]
        vg = v_ref[...]
        cv = cg * vg
        dco_v = dco_ref[...]
        dcv, dw = _dwconv_bwd(cv, dco_v * b_ref[...], w_ref)
        dw_ref[...] = dw
        stage[0] = (dco_v * _dwconv(cv, w_ref)).astype(BF16)
        stage[1] = (dcv * vg).astype(BF16)
        stage[2] = (dcv * cg).astype(BF16)
        copies = [pltpu.make_async_copy(stage.at[p], dproj_ref.at[:, pl.ds((p * nq + q) * LANES, LANES)], sem.at[p])
                  for p in range(3)]
        for cp in copies:
            cp.start()
        for cp in copies:
            cp.wait()

    col = lambda off: pl.BlockSpec((tp, LANES), lambda q, off=off: (0, off * nq + q))
    return pl.pallas_call(
        body, name=name, grid=(nq,),
        in_specs=[col(0), col(1), col(2), pl.BlockSpec((tp, LANES), lambda q: (0, q)),
                  pl.BlockSpec((3, LANES), lambda q: (0, q))],
        out_specs=[pl.BlockSpec(memory_space=pl.ANY), pl.BlockSpec((3, LANES), lambda q: (0, q))],
        out_shape=[jax.ShapeDtypeStruct((tp, 4 * dh), BF16), jax.ShapeDtypeStruct((3, dh), F32)],
        scratch_shapes=[pltpu.VMEM((3, tp, LANES), BF16), pltpu.SemaphoreType.DMA((3,))],
        compiler_params=_cparams("arbitrary"),
    )(proj, proj, proj, dco, conv_w)


def _ssm_bwd(proj, y, dg, dproj, states, bc_re, bc_im, cc_re, cc_im, dskip, a_pow, name):
    tp = proj.shape[0]
    dh = proj.shape[1] // 4
    nq = dh // LANES
    sw = STATE * N_GROUPS // nq

    def body(u_ref, y_ref, dg_ref, dproj_in, s_re, s_im, bre_ref, bim_ref, cre_ref, cim_ref, d_ref, pw_ref,
             dproj_ref, dbre_ref, dbim_ref, dcre_ref, dcim_ref, dd_ref, dar_ref, dai_ref,
             l_re, l_im, a_il, b_il, stage, sem):
        del dproj_in
        q = pl.program_id(0)
        nt = (((1,), (1,)), ((), ()))
        tn = (((0,), (0,)), ((), ()))
        _interleave(a_il, u_ref)
        ub = a_il[...].astype(BF16)
        dy_rows = dg_ref[...] * _gelu_grad(y_ref[...])
        dd_ref[...] = jnp.sum(dy_rows * u_ref[...], axis=0, keepdims=True)
        _interleave(b_il, dy_rows)
        dy = b_il[...]
        dyb = dy.astype(BF16)
        l_re[...] = lax.dot_general(dyb, cre_ref[...], nt, preferred_element_type=F32)
        l_im[...] = -lax.dot_general(dyb, cim_ref[...], nt, preferred_element_type=F32)
        dcre_ref[...] = lax.dot_general(s_re[...].astype(BF16), dyb, tn, preferred_element_type=F32)
        dcim_ref[...] = -lax.dot_general(s_im[...].astype(BF16), dyb, tn, preferred_element_type=F32)
        sums = _scan(l_re, l_im, pw_ref, True, pair=(s_re, s_im))
        rest = tp - SUBLANES
        for st in range(sw // LANES):
            lanes = slice(st * LANES, (st + 1) * LANES)
            lr0, li0 = l_re[:SUBLANES, lanes], l_im[:SUBLANES, lanes]
            pr0, pi0 = _segment_shift(s_re[rest:, lanes], False), _segment_shift(s_im[rest:, lanes], False)
            dar_ref[:, lanes] = jnp.sum(sums[2 * st] + (lr0 * pr0 + li0 * pi0), axis=0, keepdims=True)
            dai_ref[:, lanes] = jnp.sum(sums[2 * st + 1] + (li0 * pr0 - lr0 * pi0), axis=0, keepdims=True)
        lrb = l_re[...].astype(BF16)
        lib = l_im[...].astype(BF16)
        a_il[...] = (dy * d_ref[...] + lax.dot_general(lrb, bre_ref[...], nt, preferred_element_type=F32)
                     + lax.dot_general(lib, bim_ref[...], nt, preferred_element_type=F32))
        _deinterleave(b_il, a_il)
        stage[...] = b_il[...].astype(BF16)
        dbre_ref[...] = lax.dot_general(ub, lrb, tn, preferred_element_type=F32)
        dbim_ref[...] = lax.dot_general(ub, lib, tn, preferred_element_type=F32)
        cp = pltpu.make_async_copy(stage, dproj_ref.at[:, pl.ds((3 * nq + q) * LANES, LANES)], sem)
        cp.start()
        cp.wait()

    blk = pl.BlockSpec((tp, LANES), lambda q: (0, q))
    bspec = pl.BlockSpec((LANES, sw), lambda q: (0, q))
    cspec = pl.BlockSpec((sw, LANES), lambda q: (q, 0))
    tspec = pl.BlockSpec((2, tp // SUBLANES, sw), lambda q: (0, 0, q))
    nstate = STATE * N_GROUPS
    return pl.pallas_call(
        body, name=name, grid=(nq,),
        in_specs=[pl.BlockSpec((tp, LANES), lambda q: (0, 3 * nq + q)), blk, blk, pl.BlockSpec(memory_space=pl.ANY),
                  pl.BlockSpec((tp, sw), lambda q: (0, q)), pl.BlockSpec((tp, sw), lambda q: (0, q)),
                  bspec, bspec, cspec, cspec, pl.BlockSpec((1, LANES), lambda q: (0, q)), tspec],
        out_specs=[pl.BlockSpec(memory_space=pl.ANY), bspec, bspec, cspec, cspec,
                   pl.BlockSpec((1, LANES), lambda q: (0, q)),
                   pl.BlockSpec((1, sw), lambda q: (0, q)), pl.BlockSpec((1, sw), lambda q: (0, q))],
        out_shape=[jax.ShapeDtypeStruct((tp, 4 * dh), BF16),
                   jax.ShapeDtypeStruct((LANES, nstate), F32), jax.ShapeDtypeStruct((LANES, nstate), F32),
                   jax.ShapeDtypeStruct((nstate, LANES), F32), jax.ShapeDtypeStruct((nstate, LANES), F32),
                   jax.ShapeDtypeStruct((1, dh), F32),
                   jax.ShapeDtypeStruct((1, nstate), F32), jax.ShapeDtypeStruct((1, nstate), F32)],
        input_output_aliases={3: 0},
        scratch_shapes=[pltpu.VMEM((tp, sw), F32)] * 2 + [pltpu.VMEM((tp, LANES), F32)] * 2
        + [pltpu.VMEM((tp, LANES), BF16), pltpu.SemaphoreType.DMA],
        compiler_params=_cparams("arbitrary"),
    )(proj, y, dg, dproj, states[0], states[1], bc_re, bc_im, cc_re, cc_im, dskip, a_pow)


FFN_TILE = 256
FFN_ROW_TILE = 128
FFN_ROWS = 32


def _window(x_ref, before, r0, rows, cols):
    if r0 == 0:
        return jnp.concatenate([before, x_ref[0:rows, cols]], axis=0)
    return x_ref[r0 - SUBLANES:r0 + rows, cols]


def _taps(window):
    return window[SUBLANES:], pltpu.roll(window, 1, 0)[SUBLANES:], pltpu.roll(window, 2, 0)[SUBLANES:]


def _conv_taps(taps, w):
    return w[2] * taps[0] + w[1] * taps[1] + w[0] * taps[2]


def _ffn_specs(tp, dff):
    pieces = FFN_ROW_TILE // SUBLANES
    main = lambda half: pl.BlockSpec((FFN_ROW_TILE, dff), lambda i: (i, half))
    prev = lambda half: pl.BlockSpec((SUBLANES, dff), lambda i: (jnp.maximum(i * pieces - 1, 0), half))
    nxt = lambda half: pl.BlockSpec((SUBLANES, dff), lambda i: (jnp.minimum((i + 1) * pieces, tp // SUBLANES - 1), half))
    par = lambda r, half: pl.BlockSpec((r, dff), lambda i: (0, half))
    return main, prev, nxt, par


def _ffn_act(up, fw, fb, name):
    tp, two_ff = up.shape
    dff = two_ff // 2
    tr, tc, rows = FFN_ROW_TILE, FFN_TILE, FFN_ROWS

    def body(ua_ref, uv_ref, pa_ref, pv_ref, wa_ref, wv_ref, ba_ref, bv_ref, act_ref):
        first = pl.program_id(0) == 0
        for c0 in range(0, dff, tc):
            cols = slice(c0, c0 + tc)
            wa, wv = [[w_ref[k:k + 1, cols] for k in range(3)] for w_ref in (wa_ref, wv_ref)]
            ba, bv = ba_ref[:, cols], bv_ref[:, cols]
            before_a, before_v = [jnp.where(first, 0.0, p_ref[:, cols]) for p_ref in (pa_ref, pv_ref)]
            for r0 in range(0, tr, rows):
                a = _conv_taps(_taps(_window(ua_ref, before_a, r0, rows, cols)), wa) + ba
                v = _conv_taps(_taps(_window(uv_ref, before_v, r0, rows, cols)), wv) + bv
                act_ref[r0:r0 + rows, cols] = (a * _sigmoid(a) * v).astype(BF16)

    main, prev, _, par = _ffn_specs(tp, dff)
    return pl.pallas_call(
        body, name=name, grid=(tp // tr,),
        in_specs=[main(0), main(1), prev(0), prev(1), par(3, 0), par(3, 1), par(1, 0), par(1, 1)],
        out_specs=main(0),
        out_shape=jax.ShapeDtypeStruct((tp, dff), BF16),
        compiler_params=_cparams("parallel"))(up, up, up, up, fw, fw, fb, fb)


FFN_MM_ROWS = 544
FFN_MM_COLS = 1408


def _ffn_bwd(up, dh, w_down, fw, fb, name):
    tp, two_ff = up.shape
    dff = two_ff // 2
    dm = dh.shape[1]
    tr, cw, rows = FFN_MM_ROWS, FFN_MM_COLS, FFN_ROWS
    nr, nc = tp // tr, dff // cw
    n_e = rows + SUBLANES
    pieces = tr // SUBLANES

    def body(ua_ref, uv_ref, pa_ref, pv_ref, dh_ref, wd_ref, wa_ref, wv_ref, ba_ref, bv_ref,
             dup_ref, dwa_ref, dwv_ref, dba_ref, dbv_ref, dact, stage, head_ref, sem):
        j, i = pl.program_id(0), pl.program_id(1)
        step = j * nr + i
        top = i == nr - 1
        sums = ((dwa_ref, dba_ref), (dwv_ref, dbv_ref))

        slot = step % 2

        def out_copies(at):
            r0 = pl.multiple_of((nr - 1 - at % nr) * tr, tr)
            return [pltpu.make_async_copy(
                stage.at[at % 2, s],
                dup_ref.at[pl.ds(r0, tr), pl.ds(pl.multiple_of(s * dff + at // nr * cw, LANES), cw)],
                sem.at[at % 2, s]) for s in range(2)]

        @pl.when(i == 0)
        def _():
            head_ref[...] = jnp.zeros_like(head_ref)
            for dw_ref, db_ref in sums:
                dw_ref[...] = jnp.zeros_like(dw_ref)
                db_ref[...] = jnp.zeros_like(db_ref)

        dact[...] = lax.dot_general(dh_ref[...], wd_ref[...], (((1,), (1,)), ((), ())), preferred_element_type=F32)

        @pl.when(step > 1)
        def _():
            for cp in out_copies(step - 2):
                cp.wait()

        def gate_bwd(taps, dact_v, w, bias):
            a, v = [_conv_taps(taps[s], w[s]) + bias[s] for s in range(2)]
            sg = _sigmoid(a)
            return [dact_v * v * sg * (1.0 + a * (1.0 - sg)), dact_v * a * sg]

        fold = lambda x: sum(x[r:r + SUBLANES] for r in range(0, rows, SUBLANES))
        for c0 in range(0, cw, FFN_TILE):
            cols = slice(c0, min(c0 + FFN_TILE, cw))
            w = [[w_ref[k:k + 1, cols] for k in range(3)] for w_ref in (wa_ref, wv_ref)]
            bias = [ba_ref[:, cols], bv_ref[:, cols]]
            before = [jnp.where(top, 0.0, p_ref[:, cols]) for p_ref in (pa_ref, pv_ref)]
            head = [head_ref[s, :, cols] for s in range(2)]
            piece = jnp.zeros_like(head[0])
            acc = [[piece] * 4 for _ in range(2)]
            for r0 in reversed(range(0, tr, rows)):
                taps = [_taps(_window(x_ref, before[s], r0, rows, cols)) for s, x_ref in enumerate((ua_ref, uv_ref))]
                d = gate_bwd(taps, dact[r0:r0 + rows, cols], w, bias)
                for s in range(2):
                    de = jnp.concatenate([d[s], head[s]], axis=0)
                    dx = (w[s][2] * d[s] + w[s][1] * pltpu.roll(de, n_e - 1, 0)[:rows]
                          + w[s][0] * pltpu.roll(de, n_e - 2, 0)[:rows])
                    stage[slot, s, r0:r0 + rows, cols] = dx.astype(BF16)
                    for k in range(3):
                        acc[s][k] = acc[s][k] + fold(d[s] * taps[s][2 - k])
                    acc[s][3] = acc[s][3] + fold(d[s])
                    head[s] = d[s][:SUBLANES]
            for s, (dw_ref, db_ref) in enumerate(sums):
                head_ref[s, :, cols] = head[s]
                dw_ref[:, cols] = dw_ref[:, cols] + jnp.concatenate(
                    [jnp.sum(x, axis=0, keepdims=True) for x in acc[s][:3]], axis=0)
                db_ref[:, cols] = db_ref[:, cols] + jnp.sum(acc[s][3], axis=0, keepdims=True)

        copies = out_copies(step)
        for cp in copies:
            cp.start()

        @pl.when(step == nc * nr - 1)
        def _():
            for cp in out_copies(step - 1) + copies:
                cp.wait()

    row = lambda i: nr - 1 - i
    main = lambda half: pl.BlockSpec((tr, cw), lambda j, i: (row(i), half * nc + j))
    prev = lambda half: pl.BlockSpec((SUBLANES, cw), lambda j, i: (jnp.maximum(row(i) * pieces - 1, 0), half * nc + j))
    par = lambda r, half: pl.BlockSpec((r, cw), lambda j, i: (0, half * nc + j))
    acc_spec = lambda r: pl.BlockSpec((r, cw), lambda j, i: (0, j))
    return pl.pallas_call(
        body, name=name, grid=(nc, nr),
        in_specs=[main(0), main(1), prev(0), prev(1),
                  pl.BlockSpec((tr, dm), lambda j, i: (row(i), 0)), pl.BlockSpec((cw, dm), lambda j, i: (j, 0)),
                  par(3, 0), par(3, 1), par(1, 0), par(1, 1)],
        out_specs=[pl.BlockSpec(memory_space=pl.ANY), acc_spec(3), acc_spec(3), acc_spec(1), acc_spec(1)],
        out_shape=[jax.ShapeDtypeStruct((tp, two_ff), BF16),
                   jax.ShapeDtypeStruct((3, dff), F32), jax.ShapeDtypeStruct((3, dff), F32),
                   jax.ShapeDtypeStruct((1, dff), F32), jax.ShapeDtypeStruct((1, dff), F32)],
        scratch_shapes=[pltpu.VMEM((tr, cw), F32), pltpu.VMEM((2, 2, tr, cw), BF16),
                        pltpu.VMEM((2, SUBLANES, cw), F32), pltpu.SemaphoreType.DMA((2, 2))],
        compiler_params=_cparams("arbitrary", "arbitrary"))(up, up, up, up, dh, w_down, fw, fw, fb, fb)


def _zoh(lr, li, ld):
    dt = jnp.exp(ld)
    mag = jnp.exp(lr * dt)
    ang = li * dt
    ar = mag * jnp.cos(ang)
    ai = mag * jnp.sin(ang)
    den = lr * lr + li * li
    nr = ar - 1.0
    fr = (nr * lr + ai * li) / den
    fi = (ai * lr - nr * li) / den
    return dt, ar, ai, den, nr, fr, fi


def _s5_prep(lr, li, ld, b_re, b_im, n_pow, name):
    nstate = lr.shape[1]

    def body(lr_ref, li_ref, ld_ref, bre_ref, bim_ref, pw_ref, bcre_ref, bcim_ref):
        _, ar, ai, _, _, fr, fi = _zoh(lr_ref[...], li_ref[...], ld_ref[...])
        bre = bre_ref[...]
        bim = bim_ref[...]
        bcre_ref[...] = (fr * bre - fi * bim).astype(BF16)
        bcim_ref[...] = (fr * bim + fi * bre).astype(BF16)
        row = lax.broadcasted_iota(jnp.int32, (SUBLANES, nstate), 0)
        pr, pi = jnp.zeros((SUBLANES, nstate), F32), jnp.zeros((SUBLANES, nstate), F32)
        cr, ci = ar, ai
        for t in range(SUBLANES):
            pr, pi = jnp.where(row == t, cr, pr), jnp.where(row == t, ci, pi)
            cr, ci = cr * ar - ci * ai, cr * ai + ci * ar
        pw_ref[0, 0:SUBLANES, :] = pr
        pw_ref[1, 0:SUBLANES, :] = pi
        n = SUBLANES
        while n < n_pow:
            m = min(n, n_pow - n)
            tr, ti = pw_ref[0, n - 1:n, :], pw_ref[1, n - 1:n, :]
            xr, xi = pw_ref[0, 0:m, :], pw_ref[1, 0:m, :]
            pw_ref[0, n:n + m, :] = xr * tr - xi * ti
            pw_ref[1, n:n + m, :] = xr * ti + xi * tr
            n += m

    vmem = pl.BlockSpec(memory_space=pltpu.VMEM)
    return pl.pallas_call(
        body, name=name, in_specs=[vmem] * 5, out_specs=[vmem] * 3,
        out_shape=[jax.ShapeDtypeStruct((2, n_pow, nstate), F32)] + [jax.ShapeDtypeStruct(b_re.shape, BF16)] * 2,
        compiler_params=pltpu.CompilerParams(vmem_limit_bytes=VMEM_LIMIT))(lr, li, ld, b_re, b_im)


def _s5_prep_bwd(lr, li, ld, b_re, b_im, da_re, da_im, dbc_re, dbc_im, name):
    def body(lr_ref, li_ref, ld_ref, bre_ref, bim_ref, dar_ref, dai_ref, dbcre_ref, dbcim_ref,
             dlr_ref, dli_ref, dld_ref, dbre_ref, dbim_ref):
        lr, li = lr_ref[...], li_ref[...]
        dt, ar, ai, den, nr, fr, fi = _zoh(lr, li, ld_ref[...])
        bre, bim = bre_ref[...], bim_ref[...]
        gre, gim = dbcre_ref[...], dbcim_ref[...]
        dbre_ref[...] = fr * gre + fi * gim
        dbim_ref[...] = fr * gim - fi * gre
        g_fr = jnp.sum(gre * bre + gim * bim, axis=0, keepdims=True)
        g_fi = jnp.sum(gim * bre - gre * bim, axis=0, keepdims=True)
        g_ar = dar_ref[...] + (g_fr * lr - g_fi * li) / den
        g_ai = dai_ref[...] + (g_fr * li + g_fi * lr) / den
        d_lr = (g_fr * (nr - 2.0 * fr * lr) + g_fi * (ai - 2.0 * fi * lr)) / den
        d_li = (g_fr * (ai - 2.0 * fr * li) - g_fi * (nr + 2.0 * fi * li)) / den
        g_logmag = g_ar * ar + g_ai * ai
        g_ang = g_ai * ar - g_ar * ai
        dlr_ref[...] = d_lr + g_logmag * dt
        dli_ref[...] = d_li + g_ang * dt
        d_ld = (g_logmag * lr + g_ang * li) * dt
        n = d_ld.shape[1]
        sh = 1
        while sh < STATE:
            d_ld = d_ld + pltpu.roll(d_ld, n - sh, 1)
            sh *= 2
        dld_ref[...] = d_ld

    vmem = pl.BlockSpec(memory_space=pltpu.VMEM)
    row = jax.ShapeDtypeStruct(lr.shape, F32)
    return pl.pallas_call(
        body, name=name, in_specs=[vmem] * 9, out_specs=[vmem] * 5,
        out_shape=[row, row, row, jax.ShapeDtypeStruct(b_re.shape, F32), jax.ShapeDtypeStruct(b_re.shape, F32)],
    )(lr, li, ld, b_re, b_im, da_re, da_im, dbc_re, dbc_im)


def _compact_b(bb):
    bq = bb.reshape(N_GROUPS // 8, 8, STATE, GROUP)
    m = jnp.einsum("ab,qbph->qahbp", jnp.eye(8, dtype=bb.dtype), bq).reshape(N_GROUPS // 8, LANES, 8 * STATE)
    return m.transpose(1, 0, 2).reshape(LANES, N_GROUPS * STATE)


def _expand_b(m):
    d = m.reshape(8, GROUP, N_GROUPS // 8, 8, STATE)
    return jnp.einsum("ahqap->qahp", d).reshape(N_GROUPS, GROUP, STATE)


def _compact_c(c):
    cq = c.reshape(N_GROUPS // 8, 8, GROUP, STATE)
    return jnp.einsum("ab,qbhp->qbpah", jnp.eye(8, dtype=c.dtype), cq).reshape(N_GROUPS * STATE, LANES)


def _expand_c(m):
    d = m.reshape(N_GROUPS // 8, 8, STATE, 8, GROUP)
    return jnp.einsum("qbpbh->qbhp", d).reshape(N_GROUPS, GROUP, STATE)


def _local_step(x, target, p, ex):
    seq, d = x.shape
    n_real = N_META + seq
    tp = -(-n_real // ROW_ALIGN) * ROW_ALIGN

    h0, hn1 = _input_norm_fwd(x, p["meta_tokens"], p["norm_mix_g"] + ex.zero, tp, "norm_mix")
    ex.forward("first", hn1)
    nstate = N_GROUPS * STATE
    s5 = (p["ssm_lam_re"].reshape(1, nstate), p["ssm_lam_im"].reshape(1, nstate),
          jnp.repeat(p["ssm_log_dt"].reshape(-1), STATE).reshape(1, nstate),
          _compact_b(p["ssm_b_re"]), _compact_b(p["ssm_b_im"]))
    a_pow, bc_re, bc_im = _s5_prep(*s5, tp // SUBLANES, "s5_prep")
    cc_re = _compact_c(p["ssm_c_re"]).astype(BF16)
    cc_im = _compact_c(p["ssm_c_im"]).astype(BF16)
    dskip = p["ssm_d"].reshape(1, -1)
    first = ex.weights("first", bc_re)
    proj = _mm(hn1, first["w_in"], "nn", "proj")
    started = ex.forward("mid", proj)
    co, y, g, *states = _seq_fwd(proj, p["conv_w"] + started[0, 0], bc_re, bc_im, cc_re, cc_im, dskip, a_pow,
                                 "seq_fwd")
    mid = ex.weights("mid", g)
    z = _mm(g, mid["ssm_w_glu"], "nn", "glu")
    mixed = _mix_fwd(co, y, z, p["gain_conv_out"], p["gain_ssm_out"], "mix_fwd")
    started = ex.forward("up", mixed)
    h1, hn2 = _proj_res_norm(mixed, mid["w_out"], h0, p["norm_ffn_g"], started, "out_proj_norm")
    late = ex.weights("up", hn2)
    up = _mm(hn2, late["w_up"], "nn", "up_proj")
    started = ex.forward("down", up)
    act = _ffn_act(up, p["ffn_conv_w"] + started[0, 0], p["ffn_conv_b"], "ffn_act")
    late.update(ex.weights("down", act))
    loss, dh2, dh2b, d_gfin = _proj_loss_bwd(act, late["w_down"], h1, target, p["norm_final_g"], n_real,
                                             "down_proj_loss")

    g_w_down = _mm(act, dh2b, "tn", "g_w_down")
    dup, dfw_a, dfw_v, dfb_a, dfb_v = _ffn_bwd(up, dh2b, late["w_down"], p["ffn_conv_w"], p["ffn_conv_b"], "ffn_bwd")
    g_w_up = _mm(hn2, dup, "tn", "g_w_up")
    started = ex.grads_ready("late", {"w_up": g_w_up, "w_down": g_w_down})
    dh1, dh1b, d_gffn = _proj_norm_bwd(dup, late["w_up"], h1, p["norm_ffn_g"], dh2, started, "d_hn2_norm_bwd")
    started = ex.grads_send("late", dh1)
    g_w_out = _mm(mixed, dh1b, "tn", "g_w_out", after=started)
    dco, dz, dgp, d_gc, d_gs = _proj_mix_bwd(dh1b, mid["w_out"], co, y, z, p["gain_conv_out"],
                                             p["gain_ssm_out"], "d_mixed_mix_bwd")
    g_w_glu = _mm(g, dz, "tn", "g_w_glu")
    started = ex.grads_ready("mid", {"ssm_w_glu": g_w_glu, "w_out": g_w_out})
    dg = _mm(dz, mid["ssm_w_glu"], "nt", "d_gelu", acc_in=dgp, after=started)
    started = ex.grads_send("mid", dg)
    dproj, d_conv_w = _conv_bwd(proj, dco, p["conv_w"] + started[0, 0], "conv_bwd")
    (dproj, dbc_re, dbc_im, dcc_re, dcc_im, d_dskip, da_re, da_im) = _ssm_bwd(
        proj, y, dg, dproj, states, bc_re, bc_im, cc_re, cc_im, dskip, a_pow, "ssm_bwd")
    g_w_in = _mm(hn1, dproj, "tn", "g_w_in")
    started = ex.grads_ready("first", {"w_in": g_w_in})
    grad_x, d_meta, d_gmix = _proj_input_norm_bwd(dproj, first["w_in"], h0, p["norm_mix_g"], dh1, started, n_real,
                                                  "d_hn1_norm_bwd")
    started = ex.grads_send("first", d_gmix)

    d_lam_re, d_lam_im, d_log_dt, d_b_re, d_b_im = _s5_prep_bwd(*s5, da_re, da_im, dbc_re, dbc_im, "s5_prep_bwd")
    d_lam_re, d_lam_im = d_lam_re.reshape(N_GROUPS, STATE), d_lam_im.reshape(N_GROUPS, STATE)
    d_log_dt = d_log_dt[0, ::STATE]
    d_b_re, d_b_im = _expand_b(d_b_re), _expand_b(d_b_im)
    grads = {
        "meta_tokens": d_meta, "norm_mix_g": d_gmix, "w_in": g_w_in, "conv_w": d_conv_w,
        "ssm_lam_re": d_lam_re, "ssm_lam_im": d_lam_im, "ssm_log_dt": d_log_dt,
        "ssm_b_re": d_b_re, "ssm_b_im": d_b_im, "ssm_c_re": _expand_c(dcc_re), "ssm_c_im": _expand_c(dcc_im),
        "ssm_d": d_dskip.reshape(N_GROUPS, GROUP), "ssm_w_glu": g_w_glu,
        "gain_conv_out": d_gc, "gain_ssm_out": d_gs, "w_out": g_w_out, "norm_ffn_g": d_gffn,
        "w_up": g_w_up, "ffn_conv_w": jnp.concatenate([dfw_a, dfw_v], axis=1),
        "ffn_conv_b": jnp.concatenate([dfb_a, dfb_v], axis=1), "w_down": g_w_down, "norm_final_g": d_gfin,
    }
    return loss[0, 0] + started[0, 0], grad_x, grads


def _view(ref, axis, start, size):
    idx = [slice(None)] * len(ref.shape)
    idx[axis] = pl.ds(start, size)
    return ref.at[tuple(idx)]


def _exchange(name, ins, outs, aliases, local_copies, remote_copies):
    ni, no = len(ins), len(outs)
    nl, nr = len(local_copies), len(remote_copies)

    def body(*refs):
        in_refs, out_refs = refs[:ni], refs[ni:ni + no]
        send_sems, recv_sems, local_sems = refs[ni + no:]
        x, y, c = lax.axis_index("x"), lax.axis_index("y"), lax.axis_index("c")
        pos = (x, y, c, 2 * x + y)
        locals_ = [pltpu.make_async_copy(s(in_refs, out_refs, pos), d(in_refs, out_refs, pos), local_sems.at[i])
                   for i, (s, d) in enumerate(local_copies)]
        remotes = []
        for i, (s, d, flip) in enumerate(remote_copies):
            peer = (1 - x if "x" in flip else x, 1 - y if "y" in flip else y, 1 - c if "c" in flip else c)
            remotes.append(pltpu.make_async_remote_copy(
                src_ref=s(in_refs, out_refs, pos), dst_ref=d(in_refs, out_refs, pos),
                send_sem=send_sems.at[i], recv_sem=recv_sems.at[i], device_id=peer, device_id_type=MESH))
        for cp in locals_ + remotes:
            cp.start()
        for cp in remotes:
            cp.wait_recv()
        for cp in remotes:
            cp.wait_send()
        for cp in locals_:
            cp.wait()

    hbm = pl.BlockSpec(memory_space=pl.ANY)
    return pl.pallas_call(
        body, name=name, in_specs=[hbm] * ni, out_specs=[hbm] * no, out_shape=outs,
        input_output_aliases=aliases,
        scratch_shapes=[pltpu.SemaphoreType.DMA((nr,)), pltpu.SemaphoreType.DMA((nr,)),
                        pltpu.SemaphoreType.DMA((max(nl, 1),))],
    )(*ins)


BIG = {"w_in": (0, 1), "ssm_w_glu": (1, 0), "w_out": (1, 0), "w_up": (0, 1), "w_down": (1, 0)}
BIG_NAMES = tuple(BIG)
FLIPS = ("y", "x", "xy")


def _peer_chip(pos, flip):
    x, y, _, _ = pos
    return 2 * (1 - x if "x" in flip else x) + (1 - y if "y" in flip else y)


def _block_rows(rows, cols, itemsize, mult):
    return _pick_tile(rows, max(mult, (2 * 1024 * 1024) // (cols * itemsize)), mult)


def _cast_into_full(w, kc, shard_axis, name):
    r, cdim = w.shape
    tr = _block_rows(r, cdim, 4, 16)
    nb = r // tr

    def body(kc_ref, w_ref, o_ref):
        o_ref[...] = w_ref[...].astype(BF16)

    if shard_axis == 1:
        full, o_spec = (r, 4 * cdim), pl.BlockSpec((tr, cdim), lambda i, kc: (i, kc[0]))
    else:
        full, o_spec = (4 * r, cdim), pl.BlockSpec((tr, cdim), lambda i, kc: (kc[0] * nb + i, 0))
    return pl.pallas_call(
        body, name=name,
        grid_spec=pltpu.PrefetchScalarGridSpec(
            num_scalar_prefetch=1, grid=(nb,), in_specs=[pl.BlockSpec((tr, cdim), lambda i, kc: (i, 0))],
            out_specs=o_spec),
        out_shape=jax.ShapeDtypeStruct(full, BF16), compiler_params=_cparams("parallel"))(kc, w)


def _pair_sum(g, recv, kc, half_axis, name, out_dtype):
    hr, hc = recv.shape
    tr = _block_rows(hr, hc, 4, 16)
    nb = hr // tr

    def body(kc_ref, g_ref, r_ref, o_ref):
        o_ref[...] = (g_ref[...] + r_ref[...]).astype(out_dtype)

    if half_axis == 0:
        g_spec = pl.BlockSpec((tr, hc), lambda i, kc: (kc[1] * nb + i, 0))
    elif half_axis == 1:
        g_spec = pl.BlockSpec((tr, hc), lambda i, kc: (i, kc[1]))
    else:
        g_spec = pl.BlockSpec((tr, hc), lambda i, kc: (i, 0))
    same = pl.BlockSpec((tr, hc), lambda i, kc: (i, 0))
    return pl.pallas_call(
        body, name=name,
        grid_spec=pltpu.PrefetchScalarGridSpec(num_scalar_prefetch=1, grid=(nb,), in_specs=[g_spec, same],
                                               out_specs=same),
        out_shape=jax.ShapeDtypeStruct((hr, hc), out_dtype), compiler_params=_cparams("parallel"))(kc, g, recv)


def _chip_sum(own, recv, kc, own_axis, out_axis, name):
    _, sr, sc = recv.shape
    tr = _block_rows(sr, sc, 4, 16)
    nb = sr // tr

    def body(kc_ref, o_ref, r_ref, t_ref):
        k = kc_ref[0]
        own_v = o_ref[...].astype(F32)
        r = [r_ref[m].astype(F32) for m in range(3)]
        terms = []
        for kk in range(4):
            m = jnp.bitwise_xor(k, kk)
            terms.append(jnp.where(m == 0, own_v, jnp.where(m == 1, r[0], jnp.where(m == 2, r[1], r[2]))))
        t_ref[...] = (terms[0] + terms[1]) + (terms[2] + terms[3])

    if own_axis == 0:
        own_spec = pl.BlockSpec((tr, sc), lambda i, kc: (kc[0] * nb + i, 0))
    elif own_axis == 1:
        own_spec = pl.BlockSpec((tr, sc), lambda i, kc: (i, kc[0]))
    else:
        own_spec = pl.BlockSpec((tr, sc), lambda i, kc: (kc[1] * nb + i, 0))
    if out_axis == 0:
        out_full, out_spec = (2 * sr, sc), pl.BlockSpec((tr, sc), lambda i, kc: (kc[1] * nb + i, 0))
    else:
        out_full, out_spec = (sr, 2 * sc), pl.BlockSpec((tr, sc), lambda i, kc: (i, kc[1]))
    return pl.pallas_call(
        body, name=name,
        grid_spec=pltpu.PrefetchScalarGridSpec(
            num_scalar_prefetch=1, grid=(nb,),
            in_specs=[own_spec, pl.BlockSpec((3, tr, sc), lambda i, kc: (0, i, 0))],
            out_specs=out_spec),
        out_shape=jax.ShapeDtypeStruct(out_full, F32), compiler_params=_cparams("parallel"))(kc, own, recv)


def _adamw(w, g, m, v, name):
    r, cdim = w.shape
    tr = _block_rows(r, cdim, 4, 8)
    c1 = 1.0 - ADAM_B1 ** ADAM_STEP
    c2 = 1.0 - ADAM_B2 ** ADAM_STEP

    def body(w_ref, g_ref, m_ref, v_ref, go_ref, d_ref, nm_ref, nv_ref):
        gv = g_ref[...]
        go_ref[...] = gv
        nm = ADAM_B1 * m_ref[...] + (1.0 - ADAM_B1) * gv
        nv = ADAM_B2 * v_ref[...] + (1.0 - ADAM_B2) * (gv * gv)
        d_ref[...] = -ADAM_LR * ((nm / c1) / (jnp.sqrt(nv / c2) + ADAM_EPS) + ADAM_WD * w_ref[...])
        nm_ref[...] = nm
        nv_ref[...] = nv

    spec = _rows(cdim, tr)
    return pl.pallas_call(body, name=name, grid=(r // tr,), in_specs=[spec] * 4, out_specs=[spec] * 4,
                          out_shape=[jax.ShapeDtypeStruct((r, cdim), F32)] * 4,
                          compiler_params=_cparams("parallel"))(w, g, m, v)


def _adamw_whole(ws, gs, ms, vs, name):
    n = len(ws)
    c1 = 1.0 - ADAM_B1 ** ADAM_STEP
    c2 = 1.0 - ADAM_B2 ** ADAM_STEP

    def body(*refs):
        for i in range(n):
            w_ref, g_ref, m_ref, v_ref, d_ref, nm_ref, nv_ref = [refs[j * n + i] for j in range(7)]
            gv = g_ref[...]
            nm = ADAM_B1 * m_ref[...] + (1.0 - ADAM_B1) * gv
            nv = ADAM_B2 * v_ref[...] + (1.0 - ADAM_B2) * (gv * gv)
            d_ref[...] = -ADAM_LR * ((nm / c1) / (jnp.sqrt(nv / c2) + ADAM_EPS) + ADAM_WD * w_ref[...])
            nm_ref[...] = nm
            nv_ref[...] = nv

    vmem = pl.BlockSpec(memory_space=pltpu.VMEM)
    out = pl.pallas_call(body, name=name, in_specs=[vmem] * (4 * n), out_specs=[vmem] * (3 * n),
                         out_shape=[jax.ShapeDtypeStruct(a.shape, F32) for a in ws] * 3,
                         compiler_params=pltpu.CompilerParams(vmem_limit_bytes=VMEM_LIMIT))(*ws, *gs, *ms, *vs)
    return out[:n], out[n:2 * n], out[2 * n:]


SIDE_EFFECT = pltpu.SideEffectType.DATAFLOW_SIDE_EFFECTING


def _descriptors(copies, refs, send_sems, recv_sems, sem_off=0):
    x, y, c = lax.axis_index("x"), lax.axis_index("y"), lax.axis_index("c")
    pos = (x, y, c, 2 * x + y)
    out = []
    for i, (s, d, flip) in enumerate(copies):
        peer = (1 - x if "x" in flip else x, 1 - y if "y" in flip else y, 1 - c if "c" in flip else c)
        out.append(pltpu.make_async_remote_copy(
            src_ref=s(refs, refs, pos), dst_ref=d(refs, refs, pos),
            send_sem=send_sems.at[sem_off + i], recv_sem=recv_sems.at[sem_off + i],
            device_id=peer, device_id_type=MESH))
    return out


def _shifted(copies, off):
    return [(lambda I, O, pos, s=s: s(I[off:], O[off:], pos), lambda I, O, pos, d=d: d(I[off:], O[off:], pos), flip)
            for s, d, flip in copies]


BARRIER_IDS = {"c": (1, 2), "ici": (3, 4)}


def _exchange_start(name, bufs, copies, turns, after=None):
    n, nr = len(bufs), len(copies)
    na = 0 if after is None else 1
    flips = sorted({flip for _, _, flip in copies})
    kind = "c" if flips == ["c"] else "ici"
    collective_id = BARRIER_IDS[kind][turns[kind] % 2]
    turns[kind] += 1

    def body(*refs):
        x, y, c = lax.axis_index("x"), lax.axis_index("y"), lax.axis_index("c")
        barrier = pltpu.get_barrier_semaphore()
        for flip in flips:
            peer = (1 - x if "x" in flip else x, 1 - y if "y" in flip else y, 1 - c if "c" in flip else c)
            pl.semaphore_signal(barrier, inc=1, device_id=peer, device_id_type=MESH)
        pl.semaphore_wait(barrier, len(flips))
        for cp in _descriptors(copies, refs[:n], refs[n + na], refs[n + na + 1]):
            cp.start()
        token = refs[2 * n + na + 2]
        token[...] = jnp.zeros_like(token)

    hbm = pl.BlockSpec(memory_space=pltpu.HBM)
    sem = pl.BlockSpec(memory_space=pltpu.SEMAPHORE)
    out = pl.pallas_call(
        body, name=name,
        in_specs=[hbm] * n + [pl.BlockSpec(memory_space=pl.ANY)] * na,
        out_specs=(sem, sem, *[hbm] * n, pl.BlockSpec(memory_space=pltpu.VMEM)),
        out_shape=(pltpu.SemaphoreType.DMA((nr,)), pltpu.SemaphoreType.DMA((nr,)),
                   *[pltpu.HBM(b.shape, b.dtype) for b in bufs], jax.ShapeDtypeStruct((SUBLANES, LANES), F32)),
        input_output_aliases={i: 2 + i for i in range(n)},
        compiler_params=pltpu.CompilerParams(has_side_effects=SIDE_EFFECT, collective_id=collective_id),
    )(*[pltpu.with_memory_space_constraint(b, pltpu.HBM) for b in bufs], *([after] * na))
    return out[0], out[1], list(out[2:2 + n]), out[2 + n]


def _exchange_wait(name, send_sems, recv_sems, bufs, copies, after, sem_off=0):
    n = len(bufs)

    def body(*refs):
        for cp in _descriptors(copies, refs[:n], refs[n], refs[n + 1], sem_off):
            cp.wait_send()
            cp.wait_recv()

    hbm = pl.BlockSpec(memory_space=pltpu.HBM)
    sem = pl.BlockSpec(memory_space=pltpu.SEMAPHORE)
    out = pl.pallas_call(
        body, name=name,
        in_specs=[hbm] * n + [sem, sem, pl.BlockSpec(memory_space=pl.ANY)],
        out_specs=tuple([hbm] * n),
        out_shape=tuple(pltpu.HBM(b.shape, b.dtype) for b in bufs),
        input_output_aliases={i: i for i in range(n)},
        compiler_params=pltpu.CompilerParams(has_side_effects=SIDE_EFFECT),
    )(*bufs, send_sems, recv_sems, after)
    return list(out)


FIRST = ("w_in",)
MID = ("ssm_w_glu", "w_out")
LATE = ("w_up", "w_down")
GROUPS = {"first": FIRST, "mid": MID, "late": LATE}
ARRIVALS = {"first": FIRST, "mid": MID, "up": ("w_up",), "down": ("w_down",)}


def _gather_copies(names, shard_shapes):
    def region(i, chip, c):
        half_axis, shard_axis = BIG[names[i]]
        ssize = shard_shapes[i][shard_axis]
        hsize = shard_shapes[i][half_axis] // 2
        return lambda ref: _view(_view(ref, shard_axis, chip * ssize, ssize), half_axis, c * hsize, hsize)

    ici, d2d = [], []
    for i in range(len(names)):
        for flip in FLIPS:
            ici.append((lambda I, O, pos, i=i: region(i, pos[3], pos[2])(I[i]),
                        lambda I, O, pos, i=i: region(i, pos[3], pos[2])(O[i]), flip))
            d2d.append((lambda I, O, pos, i=i, flip=flip: region(i, _peer_chip(pos, flip), pos[2])(I[i]),
                        lambda I, O, pos, i=i, flip=flip: region(i, _peer_chip(pos, flip), pos[2])(O[i]), "c"))
    return ici, d2d


def _half_shape(n, shape):
    r, cdim = shape
    return (r // 2, cdim) if BIG[n][0] == 0 else (r, cdim // 2)


def _sub_shape(n, shape):
    hr, hc = _half_shape(n, shape)
    return (hr, hc // 4) if BIG[n][1] == 1 else (hr // 4, hc)


def _pair_copies(names, shapes, with_pack, dst_off):
    n = len(names)

    def other_half(i, ref, pos):
        half_axis = BIG[names[i]][0]
        hsize = shapes[i][half_axis] // 2
        return _view(ref, half_axis, (1 - pos[2]) * hsize, hsize)

    copies = [(lambda I, O, pos, i=i: other_half(i, I[i], pos), lambda I, O, pos, i=i: O[dst_off + i], "c")
              for i in range(n)]
    if with_pack:
        copies.append((lambda I, O, pos: I[n], lambda I, O, pos: O[dst_off + n], "c"))
    return copies


def _chip_copies(names, shapes, pack_rows, dst_off):
    n = len(names)

    def piece(i, ref, chip):
        shard_axis = BIG[names[i]][1]
        ssize = _sub_shape(names[i], shapes[i])[shard_axis]
        return _view(ref, shard_axis, chip * ssize, ssize)

    copies = []
    for i in range(n):
        for slot, flip in enumerate(FLIPS):
            copies.append((lambda I, O, pos, i=i, flip=flip: piece(i, I[i], _peer_chip(pos, flip)),
                           lambda I, O, pos, i=i, slot=slot: O[dst_off + i].at[slot], flip))
    if pack_rows:
        for slot, flip in enumerate(FLIPS):
            copies.append((lambda I, O, pos: _view(I[n], 0, pos[2] * (pack_rows // 2), pack_rows // 2),
                           lambda I, O, pos, slot=slot: O[dst_off + n].at[slot], flip))
    return copies


class _Exchanges:
    def __init__(self, shards, tiny, kc):
        self.kc = kc
        wb = {n: _cast_into_full(shards[n], kc, BIG[n][1], "cast_" + n) for n in BIG_NAMES}
        self.gathering, self.forwarding, self.pairing, self.reducing = {}, {}, {}, {}
        self.turns = {"c": 0, "ici": 0}
        tiny_copies = [(lambda I, O, pos: I[0], lambda I, O, pos: O[1].at[pos[3]], flip) for flip in FLIPS]
        self.gathering["tiny"] = (0, 0, 2, tiny_copies, None)
        bufs, copies = [tiny, lax.empty((4,) + tiny.shape, F32)], list(tiny_copies)
        for group, names in ARRIVALS.items():
            ici, d2d = _gather_copies(names, [shards[n].shape for n in names])
            self.gathering[group] = (len(bufs), len(copies), len(names), ici, d2d)
            copies += _shifted(ici, len(bufs))
            bufs += [wb[n] for n in names]
        self.started = _exchange_start("gather_start", bufs, copies, self.turns)
        self.zero = self.started[3][0, 0]

    def _arrived(self, group, after):
        buf_off, sem_off, n, ici, _ = self.gathering[group]
        send_sems, recv_sems, bufs, _ = self.started
        return _exchange_wait("gather_%s_wait" % group, send_sems, recv_sems, bufs[buf_off:buf_off + n], ici, after,
                              sem_off)

    def small_params(self, kc):
        tiny, got = self._arrived("tiny", self.started[3])
        return lax.dynamic_update_index_in_dim(got, tiny, kc[0], 0)

    def forward(self, group, after):
        d2d = self.gathering[group][4]
        self.forwarding[group] = (_exchange_start("forward_%s_start" % group, self._arrived(group, after), d2d,
                                                  self.turns), d2d)
        return self.forwarding[group][0][3]

    def weights(self, group, after):
        if group not in self.forwarding:
            after = self.forward(group, after)
        (send_sems, recv_sems, bufs, _), d2d = self.forwarding[group]
        full = _exchange_wait("forward_%s_wait" % group, send_sems, recv_sems, bufs, d2d, after)
        return dict(zip(ARRIVALS[group], full))

    def grads_ready(self, group, grads):
        names = GROUPS[group]
        gs = [grads[n] for n in names]
        land = [lax.empty(_half_shape(n, g.shape), F32) for n, g in zip(names, gs)]
        copies = _pair_copies(names, [g.shape for g in gs], False, len(names))
        started = _exchange_start("pair_%s_start" % group, gs + land, copies, self.turns)
        self.pairing[group] = (started, copies)
        return started[3]

    def grads_send(self, group, after):
        names = GROUPS[group]
        n = len(names)
        (send_sems, recv_sems, bufs, _), copies = self.pairing[group]
        bufs = _exchange_wait("pair_%s_wait" % group, send_sems, recv_sems, bufs, copies, after)
        chip = [_pair_sum(bufs[i], bufs[n + i], self.kc, BIG[names[i]][0], "pair_sum_" + names[i], BF16)
                for i in range(n)]
        shapes = [bufs[i].shape for i in range(n)]
        land = [lax.empty((3,) + _sub_shape(names[i], shapes[i]), BF16) for i in range(n)]
        copies = _chip_copies(names, shapes, 0, n)
        started = _exchange_start("reduce_%s_start" % group, chip + land, copies, self.turns)
        self.reducing[group] = (started, copies)
        return started[3]

    def finish_pack(self, pack):
        kc = self.kc
        prow = pack.shape[0] // 2
        recv = _exchange("reduce_d2d", [pack], [jax.ShapeDtypeStruct(pack.shape, F32)], {}, [],
                         _pair_copies((), [], True, 0))
        chip_pack = _pair_sum(pack, recv[0], kc, None, "pair_sum_pack", F32)
        copies = _chip_copies((), [], pack.shape[0], 1)
        land = lax.empty((3, prow, pack.shape[1]), F32)
        pack_sems_s, pack_sems_r, pack_bufs, after = _exchange_start("reduce_pack_start", [chip_pack, land], copies,
                                                                     self.turns)

        names, chips, recvs = (), [], []
        for group, group_names in GROUPS.items():
            (send_sems, recv_sems, bufs, _), group_copies = self.reducing[group]
            bufs = _exchange_wait("reduce_%s_wait" % group, send_sems, recv_sems, bufs, group_copies, after)
            n = len(group_names)
            names, chips, recvs = names + group_names, chips + bufs[:n], recvs + bufs[n:]
            after = bufs[n]
        total = [_chip_sum(chips[i], recvs[i], kc, BIG[n][1], BIG[n][0], "chip_sum_" + n)
                 for i, n in enumerate(names)]

        def my_half(half_axis, ref, pos):
            hsize = ref.shape[half_axis] // 2
            return _view(ref, half_axis, pos[2] * hsize, hsize)

        swap = [(lambda I, O, pos, i=i, n=n: my_half(BIG[n][0], I[i], pos),
                 lambda I, O, pos, i=i, n=n: my_half(BIG[n][0], O[i], pos), "c") for i, n in enumerate(names)]
        self.swapping = (_exchange_start("swap_start", total, swap, self.turns), swap, names)

        chip_pack, recv_pack = _exchange_wait("reduce_pack_wait", pack_sems_s, pack_sems_r, pack_bufs, copies,
                                              self.swapping[0][3])
        total_pack = _chip_sum(chip_pack, recv_pack, kc, None, 0, "chip_sum_pack")
        swap = [(lambda I, O, pos: my_half(0, I[0], pos), lambda I, O, pos: my_half(0, O[0], pos), "c")]
        return _exchange("swap_pack", [total_pack], [jax.ShapeDtypeStruct(pack.shape, F32)], {0: 0}, [], swap)[0]

    def finish_big(self, after):
        (send_sems, recv_sems, bufs, _), swap, names = self.swapping
        return dict(zip(names, _exchange_wait("swap_wait", send_sems, recv_sems, bufs, swap, after)))


WEIGHTS = ("meta_tokens", "norm_mix_g", "w_in", "conv_w", "ssm_lam_re", "ssm_lam_im", "ssm_log_dt", "ssm_b_re",
           "ssm_b_im", "ssm_c_re", "ssm_c_im", "ssm_d", "ssm_w_glu", "gain_conv_out", "gain_ssm_out", "w_out",
           "norm_ffn_g", "w_up", "ffn_conv_w", "ffn_conv_b", "w_down", "norm_final_g")
TINY_SHARDED = ("meta_tokens", "conv_w", "ffn_conv_w")
REPLICATED = tuple(n for n in WEIGHTS if n not in BIG and n not in TINY_SHARDED)
PACK_COLS = 512


def _pack(arrays, row_mult, cols):
    flat = jnp.concatenate([a.reshape(-1).astype(F32) for a in arrays])
    n = flat.shape[0]
    total = -(-n // (row_mult * cols)) * (row_mult * cols)
    return jnp.concatenate([flat, jnp.zeros((total - n,), F32)]).reshape(total // cols, cols)


def _unpack(packed, shapes):
    flat = packed.reshape(-1)
    out, off = [], 0
    for s in shapes:
        n = math.prod(s)
        out.append(flat[off:off + n].reshape(s))
        off += n
    return out


def kernel(x, meta_tokens, norm_mix_g, w_in, conv_w, ssm_lam_re, ssm_lam_im, ssm_log_dt, ssm_b_re, ssm_b_im, ssm_c_re, ssm_c_im, ssm_d, ssm_w_glu, gain_conv_out, gain_ssm_out, w_out, norm_ffn_g, w_up, ffn_conv_w, ffn_conv_b, w_down, norm_final_g, loss_target, m_meta_tokens, m_norm_mix_g, m_w_in, m_conv_w, m_ssm_lam_re, m_ssm_lam_im, m_ssm_log_dt, m_ssm_b_re, m_ssm_b_im, m_ssm_c_re, m_ssm_c_im, m_ssm_d, m_ssm_w_glu, m_gain_conv_out, m_gain_ssm_out, m_w_out, m_norm_ffn_g, m_w_up, m_ffn_conv_w, m_ffn_conv_b, m_w_down, m_norm_final_g, v_meta_tokens, v_norm_mix_g, v_w_in, v_conv_w, v_ssm_lam_re, v_ssm_lam_im, v_ssm_log_dt, v_ssm_b_re, v_ssm_b_im, v_ssm_c_re, v_ssm_c_im, v_ssm_d, v_ssm_w_glu, v_gain_conv_out, v_gain_ssm_out, v_w_out, v_norm_ffn_g, v_w_up, v_ffn_conv_w, v_ffn_conv_b, v_w_down, v_norm_final_g):
    args = dict(locals())
    w = {n: args[n] for n in WEIGHTS}
    mom = {n: args["m_" + n] for n in WEIGHTS}
    var = {n: args["v_" + n] for n in WEIGHTS}
    kx, ky, kc_ = lax.axis_index("x"), lax.axis_index("y"), lax.axis_index("c")
    chip = 2 * kx + ky
    kc = jnp.stack([chip, kc_]).astype(jnp.int32)

    def squeeze(n, a):
        if n == "meta_tokens":
            return a
        if n == "norm_final_g":
            return a.reshape(1, -1)
        a = a[0]
        return a.reshape(1, -1) if a.ndim == 1 else a

    wl = {n: squeeze(n, w[n]) for n in WEIGHTS}
    ml = {n: squeeze(n, mom[n]) for n in WEIGHTS}
    vl = {n: squeeze(n, var[n]) for n in WEIGHTS}

    tiny = _pack([wl[n] for n in TINY_SHARDED], SUBLANES, LANES)
    ex = _Exchanges({n: wl[n] for n in BIG_NAMES}, tiny, kc)
    tiny_shapes = [wl[n].shape for n in TINY_SHARDED]
    tiny_all = ex.small_params(kc)
    tiny_parts = [_unpack(tiny_all[k], tiny_shapes) for k in range(4)]
    p = {n: wl[n] for n in WEIGHTS if n not in BIG}
    for j, n in enumerate(TINY_SHARDED):
        p[n] = jnp.concatenate([tiny_parts[k][j] for k in range(4)], axis=1)
    p["ssm_log_dt"] = wl["ssm_log_dt"].reshape(-1)

    loss_local, grad_x, grads = _local_step(x[0], loss_target[0], p, ex)

    small_names = REPLICATED + TINY_SHARDED
    small_shapes = [tuple(grads[n].shape) for n in small_names] + [(1,)]
    pack = _pack([grads[n] for n in small_names] + [loss_local.reshape(1)], 2 * 16, PACK_COLS)
    g_pack = ex.finish_pack(pack)
    g_small = dict(zip(small_names + ("loss",), _unpack(g_pack, small_shapes)))
    loss = g_small["loss"][0]
    swapped = ("ssm_b_re", "ssm_b_im")

    def view(n, a):
        if n in swapped:
            return jnp.swapaxes(a, -1, -2)
        return a.reshape(1, -1) if a.ndim == 1 else a

    g = {}
    for n in REPLICATED:
        g[n] = g_small[n].reshape(view(n, w[n]).shape)
    for n in TINY_SHARDED:
        cols = wl[n].shape[1]
        g[n] = lax.dynamic_slice_in_dim(g_small[n], chip * cols, cols, axis=1).reshape(w[n].shape)
    delta, new_m, new_v = {}, {}, {}
    small = [[view(n, d[n]) for n in small_names] for d in (w, mom, var)]
    small.insert(1, [g[n] for n in small_names])
    for d, outs in zip((delta, new_m, new_v), _adamw_whole(*small, "adamw_small")):
        d.update(zip(small_names, outs))
    for d in (g, delta, new_m, new_v):
        d.update({n: jnp.swapaxes(d[n], -1, -2) for n in swapped})
    g_big = ex.finish_big(delta[small_names[0]])
    for n in BIG_NAMES:
        g[n], delta[n], new_m[n], new_v[n] = _adamw(wl[n], g_big[n], ml[n], vl[n], "adamw_" + n)

    def like(n, a):
        return a.reshape(w[n].shape)

    return (loss, grad_x[None], *[like(n, g[n]) for n in WEIGHTS], *[like(n, delta[n]) for n in WEIGHTS],
            *[like(n, new_m[n]) for n in WEIGHTS], *[like(n, new_v[n]) for n in WEIGHTS])
```

```python
import functools
import math

import jax
import jax.numpy as jnp
from jax import lax
from jax.experimental import pallas as pl
from jax.experimental.pallas import tpu as pltpu

F32 = jnp.float32
BF16 = jnp.bfloat16
MESH = pl.DeviceIdType.MESH

N_META = 16
N_GROUPS = 32
GROUP = 16
STATE = 64
RMS_EPS = 1e-6
ADAM_LR = 0.001
ADAM_B1 = 0.9
ADAM_B2 = 0.999
ADAM_EPS = 1e-08
ADAM_WD = 0.01
ADAM_STEP = 10

LANES = 128
SUBLANES = 8
ROW_ALIGN = 128
ROW_TILES = 4
VMEM_LIMIT = 52 * 1024 * 1024
MM_VMEM_BUDGET = 40 * 1024 * 1024
GELU_C = math.sqrt(2.0 / math.pi)
GELU_A = 0.044715


def _cparams(*sem):
    return pltpu.CompilerParams(dimension_semantics=sem, vmem_limit_bytes=VMEM_LIMIT)


def _pick_tile(dim, cap, mult):
    best = None
    for t in range(mult, min(dim, cap) + 1, mult):
        if dim % t == 0:
            best = t
    return best if best is not None else dim


def _mm(a, b, mode, name, out_dtype=F32, acc_in=None, after=None):
    if mode == "tn":
        kdim, m = a.shape
    else:
        m, kdim = a.shape
    n = b.shape[0] if mode == "nt" else b.shape[1]
    tm = _pick_tile(m, 1408, LANES if mode == "tn" else 16)
    tk = _pick_tile(kdim, 2816, LANES)
    nk = kdim // tk
    out_bytes = jnp.dtype(out_dtype).itemsize
    for cap in (1408, 1024, 512, 256, LANES):
        tn = _pick_tile(n, cap, LANES)
        blocks = 2 * (tm * tk * 2 + tk * tn * 2 + tm * tn * out_bytes * (2 if acc_in is not None else 1))
        if blocks + (tm * tn * 4 if nk > 1 else 0) <= MM_VMEM_BUDGET:
            break
    has_acc = acc_in is not None

    def body(*refs):
        if after is not None:
            refs = refs[1:]
        if has_acc:
            a_ref, b_ref, c_ref, o_ref = refs[:4]
            rest = refs[4:]
        else:
            a_ref, b_ref, o_ref = refs[:3]
            c_ref = None
            rest = refs[3:]
        if mode == "nn":
            p = jnp.dot(a_ref[...], b_ref[...], preferred_element_type=F32)
        elif mode == "nt":
            p = lax.dot_general(a_ref[...], b_ref[...], (((1,), (1,)), ((), ())), preferred_element_type=F32)
        else:
            p = lax.dot_general(a_ref[...], b_ref[...], (((0,), (0,)), ((), ())), preferred_element_type=F32)
        if nk == 1:
            if has_acc:
                p = p + c_ref[...]
            o_ref[...] = p.astype(out_dtype)
        else:
            acc_ref = rest[0]
            k = pl.program_id(2)

            @pl.when(k == 0)
            def _():
                acc_ref[...] = p + c_ref[...] if has_acc else p

            @pl.when(k > 0)
            def _():
                acc_ref[...] += p

            @pl.when(k == nk - 1)
            def _():
                o_ref[...] = acc_ref[...].astype(out_dtype)

    if mode == "tn":
        a_spec = pl.BlockSpec((tk, tm), lambda i, j, k: (k, i))
    else:
        a_spec = pl.BlockSpec((tm, tk), lambda i, j, k: (i, k))
    if mode == "nt":
        b_spec = pl.BlockSpec((tn, tk), lambda i, j, k: (j, k))
    else:
        b_spec = pl.BlockSpec((tk, tn), lambda i, j, k: (k, j))
    o_spec = pl.BlockSpec((tm, tn), lambda i, j, k: (i, j))
    in_specs = [a_spec, b_spec] + ([o_spec] if has_acc else [])
    args = (a, b) + ((acc_in,) if has_acc else ())
    if after is not None:
        in_specs = [pl.BlockSpec(memory_space=pl.ANY)] + in_specs
        args = (after,) + args
    return pl.pallas_call(
        body, name=name, grid=(m // tm, n // tn, nk),
        in_specs=in_specs, out_specs=o_spec,
        out_shape=jax.ShapeDtypeStruct((m, n), out_dtype),
        scratch_shapes=[pltpu.VMEM((tm, tn), F32)] if nk > 1 else [],
        compiler_params=_cparams("parallel", "parallel", "arbitrary"),
    )(*args)


def _mm_rows(a, b, mode, name, ins, outs, epilogue, scratch=()):
    m, kdim = a.shape
    n = b.shape[0] if mode == "nt" else b.shape[1]
    tm = m // ROW_TILES
    tk = _pick_tile(kdim, 2816, LANES)
    nk = kdim // tk
    ni, no = len(ins), len(outs)

    def body(*refs):
        a_ref, b_ref = refs[:2]
        in_refs, out_refs, rest = refs[2:2 + ni], refs[2 + ni:2 + ni + no], refs[2 + ni + no:]
        k, i = pl.program_id(0), pl.program_id(1)
        if mode == "nn":
            p = jnp.dot(a_ref[...], b_ref[...], preferred_element_type=F32)
        else:
            p = lax.dot_general(a_ref[...], b_ref[...], (((1,), (1,)), ((), ())), preferred_element_type=F32)
        if nk == 1:
            epilogue(p, i, in_refs, out_refs, rest)
        else:
            acc_ref = rest[0]
            rows = pl.ds(pl.multiple_of(i * tm, SUBLANES), tm)

            @pl.when(k == 0)
            def _():
                acc_ref[rows, :] = p

            @pl.when(jnp.logical_and(k > 0, k < nk - 1))
            def _():
                acc_ref[rows, :] += p

            @pl.when(k == nk - 1)
            def _():
                epilogue(acc_ref[rows, :] + p, i, in_refs, out_refs, rest[1:])

    tile = (lambda k, i: i) if nk == 1 else (lambda k, i: jnp.where(k == nk - 1, i, 0))

    def spec(shape, kind):
        if kind == "rows":
            return pl.BlockSpec((tm,) + tuple(shape[1:]), lambda k, i: (tile(k, i),) + (0,) * (len(shape) - 1))
        if kind == "whole":
            return pl.BlockSpec(tuple(shape), lambda k, i: (0,) * len(shape))
        return pl.BlockSpec(memory_space=pl.ANY)

    a_spec = pl.BlockSpec((tm, tk), lambda k, i: (i, k))
    b_spec = pl.BlockSpec((n, tk), lambda k, i: (0, k)) if mode == "nt" else pl.BlockSpec((tk, n), lambda k, i: (k, 0))
    return pl.pallas_call(
        body, name=name, grid=(nk, ROW_TILES),
        in_specs=[a_spec, b_spec] + [spec(x.shape, kind) for x, kind in ins],
        out_specs=[spec(shape, kind) for shape, _, kind in outs],
        out_shape=[jax.ShapeDtypeStruct(shape, dtype) for shape, dtype, _ in outs],
        scratch_shapes=([pltpu.VMEM((m, n), F32)] if nk > 1 else []) + list(scratch),
        compiler_params=_cparams("arbitrary", "arbitrary"),
    )(a, b, *[x for x, _ in ins])


def _rows(shape_cols, tr, dtype=None):
    return pl.BlockSpec((tr, shape_cols), lambda i: (i, 0))


def _const(shape):
    return pl.BlockSpec(shape, lambda i: (0,) * len(shape))


def _rms(x):
    return lax.rsqrt(jnp.mean(x * x, axis=-1, keepdims=True) + RMS_EPS)


def _rms_bwd(x, r, g, dy):
    xn = x * r
    dxn = dy * g
    dx = r * (dxn - xn * jnp.mean(dxn * xn, axis=-1, keepdims=True))
    return dx, dy * xn


def _gelu(y):
    return 0.5 * y * (1.0 + jnp.tanh(GELU_C * (y + GELU_A * y * y * y)))


def _gelu_grad(y):
    t = jnp.tanh(GELU_C * (y + GELU_A * y * y * y))
    return 0.5 * (1.0 + t) + 0.5 * y * (1.0 - t * t) * GELU_C * (1.0 + 3.0 * GELU_A * y * y)


def _sigmoid(z):
    return 1.0 / (1.0 + jnp.exp(-z))


def _proj_res_norm(a, w, h, g, after, name):
    def epilogue(p, i, ins, outs, _):
        x = ins[0][...] + p
        outs[0][...] = x
        outs[1][...] = (x * _rms(x) * ins[1][...]).astype(BF16)

    return _mm_rows(a, w, "nn", name, [(h, "rows"), (g, "whole"), (after, "hbm")],
                    [(h.shape, F32, "rows"), (h.shape, BF16, "rows")], epilogue)


def _proj_norm_bwd(da, w, h, g, dres, after, name):
    d = h.shape[1]

    def epilogue(p, i, ins, outs, _):
        x = ins[0][...]
        dx, dgs = _rms_bwd(x, _rms(x), ins[1][...], p)
        dh = ins[2][...] + dx
        outs[0][...] = dh
        outs[1][...] = dh.astype(BF16)

        @pl.when(i == 0)
        def _():
            outs[2][...] = jnp.zeros_like(outs[2])

        outs[2][...] += jnp.sum(dgs, axis=0, keepdims=True)

    return _mm_rows(da, w, "nt", name, [(h, "rows"), (g, "whole"), (dres, "rows"), (after, "hbm")],
                    [(h.shape, F32, "rows"), (h.shape, BF16, "rows"), ((1, d), F32, "whole")], epilogue)


def _proj_input_norm_bwd(da, w, h, g, dres, after, n_real, name):
    tp, d = h.shape
    tr = tp // ROW_TILES

    def epilogue(p, i, ins, outs, scratch):
        h_ref, g_ref, dres_ref, _ = ins
        dx_ref, dmeta_ref, dg_ref = outs
        stage, sem = scratch
        x = h_ref[...]
        dx, dgs = _rms_bwd(x, _rms(x), g_ref[...], p)
        stage[...] = dres_ref[...] + dx

        @pl.when(i == 0)
        def _():
            dg_ref[...] = jnp.zeros_like(dg_ref)
            dmeta_ref[...] = stage[:N_META, :]

        dg_ref[...] += jnp.sum(dgs, axis=0, keepdims=True)
        for t in range(ROW_TILES):
            lo, hi = max(t * tr, N_META), min((t + 1) * tr, n_real)
            if hi > lo:
                @pl.when(i == t)
                def _(t=t, lo=lo, hi=hi):
                    cp = pltpu.make_async_copy(stage.at[pl.ds(lo - t * tr, hi - lo), :],
                                               dx_ref.at[pl.ds(lo - N_META, hi - lo), :], sem)
                    cp.start()
                    cp.wait()

    return _mm_rows(da, w, "nt", name, [(h, "rows"), (g, "whole"), (dres, "rows"), (after, "hbm")],
                    [((n_real - N_META, d), F32, "hbm"), ((N_META, d), F32, "whole"), ((1, d), F32, "whole")],
                    epilogue, scratch=[pltpu.VMEM((tr, d), F32), pltpu.SemaphoreType.DMA])


def _load_token_rows(tok_hbm, buf, sem, tr, n_real, head=None, wait=False, i=None):
    i = pl.program_id(0) if i is None else i
    for t in range(ROW_TILES):
        base = t * tr
        lo, hi = max(base, N_META), min(base + tr, n_real)

        @pl.when(i == t)
        def _(base=base, lo=lo, hi=hi):
            if hi > lo:
                cp = pltpu.make_async_copy(tok_hbm.at[pl.ds(lo - N_META, hi - lo), :],
                                           buf.at[pl.ds(lo - base, hi - lo), :], sem)
                if wait:
                    cp.wait()
                    return
                cp.start()
            if wait:
                return
            if base < N_META:
                buf[0:N_META - base, :] = (jnp.zeros((N_META - base, buf.shape[1]), F32) if head is None
                                           else head[base:N_META, :])
            if hi < base + tr:
                buf[max(hi, base) - base:tr, :] = jnp.zeros((base + tr - max(hi, base), buf.shape[1]), F32)


def _input_norm_fwd(x, meta, g, tp, name):
    seq, d = x.shape
    tr = tp // ROW_TILES
    n_real = N_META + seq

    def body(x_hbm, meta_ref, g_ref, h_ref, hn_ref, buf, sem):
        _load_token_rows(x_hbm, buf, sem, tr, n_real, head=meta_ref)
        _load_token_rows(x_hbm, buf, sem, tr, n_real, wait=True)
        h = buf[...]
        h_ref[...] = h
        hn_ref[...] = (h * _rms(h) * g_ref[...]).astype(BF16)

    return pl.pallas_call(
        body, name=name, grid=(ROW_TILES,),
        in_specs=[pl.BlockSpec(memory_space=pl.ANY), _const((N_META, d)), _const((1, d))],
        out_specs=[_rows(d, tr), _rows(d, tr)],
        out_shape=[jax.ShapeDtypeStruct((tp, d), F32), jax.ShapeDtypeStruct((tp, d), BF16)],
        scratch_shapes=[pltpu.VMEM((tr, d), F32), pltpu.SemaphoreType.DMA],
        compiler_params=_cparams("arbitrary"))(x, meta, g)


def _proj_loss_bwd(act, w, h1, target, g, n_real, name):
    tp, d = h1.shape
    tr = tp // ROW_TILES

    def epilogue(p, i, ins, outs, scratch):
        h1_ref, t_hbm, g_ref = ins
        loss_ref, dh_ref, dhb_ref, dg_ref = outs
        t_buf, sem = scratch
        _load_token_rows(t_hbm, t_buf, sem, tr, n_real, i=i)
        x = h1_ref[...] + p
        r = _rms(x)
        row = i * tr + lax.broadcasted_iota(jnp.int32, (tr, d), 0)
        valid = (row >= N_META) & (row < n_real)
        _load_token_rows(t_hbm, t_buf, sem, tr, n_real, wait=True, i=i)
        e = jnp.where(valid, x * r * g_ref[...] - t_buf[...], 0.0)
        dx, dgs = _rms_bwd(x, r, g_ref[...], e * (1.0 / d))
        dh_ref[...] = dx
        dhb_ref[...] = dx.astype(BF16)

        @pl.when(i == 0)
        def _():
            dg_ref[...] = jnp.zeros_like(dg_ref)
            loss_ref[...] = jnp.zeros_like(loss_ref)

        dg_ref[...] += jnp.sum(dgs, axis=0, keepdims=True)
        loss_ref[...] += (0.5 / d) * jnp.sum(jnp.sum(e * e, axis=0, keepdims=True), axis=1, keepdims=True)

    return _mm_rows(act, w, "nn", name, [(h1, "rows"), (target, "hbm"), (g, "whole")],
                    [((1, LANES), F32, "whole"), ((tp, d), F32, "rows"), ((tp, d), BF16, "rows"),
                     ((1, d), F32, "whole")],
                    epilogue, scratch=[pltpu.VMEM((tr, d), F32), pltpu.SemaphoreType.DMA])


def _mix_fwd(co, y, z, gc, gs, name):
    tp, dh = co.shape
    tr = tp // ROW_TILES

    def body(co_ref, y_ref, z_ref, gc_ref, gs_ref, m_ref):
        c = co_ref[...]
        m_ref[:, :dh] = (c * _rms(c) * gc_ref[...]).astype(BF16)
        so = _gelu(y_ref[...]) * _sigmoid(z_ref[...])
        m_ref[:, dh:] = (so * _rms(so) * gs_ref[...]).astype(BF16)

    return pl.pallas_call(
        body, name=name, grid=(ROW_TILES,),
        in_specs=[_rows(dh, tr)] * 3 + [_const((1, dh))] * 2,
        out_specs=_rows(2 * dh, tr),
        out_shape=jax.ShapeDtypeStruct((tp, 2 * dh), BF16),
        compiler_params=_cparams("parallel"))(co, y, z, gc, gs)


def _proj_mix_bwd(dh1b, w, co, y, z, gc, gs, name):
    tp, dh = co.shape

    def epilogue(p, i, ins, outs, _):
        co_ref, y_ref, z_ref, gc_ref, gs_ref = ins
        dco_ref, dz_ref, dgp_ref, dgc_ref, dgs_ref = outs
        c = co_ref[...]
        dco, dgc = _rms_bwd(c, _rms(c), gc_ref[...], p[:, :dh])
        dco_ref[...] = dco
        gl = _gelu(y_ref[...])
        sg = _sigmoid(z_ref[...])
        so = gl * sg
        dso, dgs = _rms_bwd(so, _rms(so), gs_ref[...], p[:, dh:])
        dz_ref[...] = (dso * gl * sg * (1.0 - sg)).astype(BF16)
        dgp_ref[...] = dso * sg

        @pl.when(i == 0)
        def _():
            dgc_ref[...] = jnp.zeros_like(dgc_ref)
            dgs_ref[...] = jnp.zeros_like(dgs_ref)

        dgc_ref[...] += jnp.sum(dgc, axis=0, keepdims=True)
        dgs_ref[...] += jnp.sum(dgs, axis=0, keepdims=True)

    return _mm_rows(dh1b, w, "nt", name,
                    [(co, "rows"), (y, "rows"), (z, "rows"), (gc, "whole"), (gs, "whole")],
                    [((tp, dh), F32, "rows"), ((tp, dh), BF16, "rows"), ((tp, dh), F32, "rows"),
                     ((1, dh), F32, "whole"), ((1, dh), F32, "whole")], epilogue)


def _shift_down(x, k):
    row = lax.broadcasted_iota(jnp.int32, x.shape, 0)
    return jnp.where(row >= k, pltpu.roll(x, k, 0), 0.0)


def _shift_up(x, k):
    n = x.shape[0]
    row = lax.broadcasted_iota(jnp.int32, x.shape, 0)
    return jnp.where(row < n - k, pltpu.roll(x, n - k, 0), 0.0)


def _dwconv(x, w_ref):
    return w_ref[2:3, :] * x + w_ref[1:2, :] * _shift_down(x, 1) + w_ref[0:1, :] * _shift_down(x, 2)


def _dwconv_bwd(x, dy, w_ref):
    dx = w_ref[2:3, :] * dy + w_ref[1:2, :] * _shift_up(dy, 1) + w_ref[0:1, :] * _shift_up(dy, 2)
    dw = jnp.concatenate([jnp.sum(dy * _shift_down(x, 2), axis=0, keepdims=True),
                          jnp.sum(dy * _shift_down(x, 1), axis=0, keepdims=True),
                          jnp.sum(dy * x, axis=0, keepdims=True)], axis=0)
    return dx, dw


def _interleave(dst, src):
    seg_rows = src.shape[0] // SUBLANES
    for seg in range(SUBLANES):
        dst[pl.ds(seg, seg_rows, stride=SUBLANES), :] = src[seg * seg_rows:(seg + 1) * seg_rows, :]


def _deinterleave(dst, src):
    seg_rows = src.shape[0] // SUBLANES
    for seg in range(SUBLANES):
        dst[seg * seg_rows:(seg + 1) * seg_rows, :] = src[pl.ds(seg, seg_rows, stride=SUBLANES), :]


def _segment_shift(x, reverse):
    row = lax.broadcasted_iota(jnp.int32, x.shape, 0)
    if reverse:
        return jnp.where(row < SUBLANES - 1, pltpu.roll(x, SUBLANES - 1, 0), 0.0)
    return jnp.where(row >= 1, pltpu.roll(x, 1, 0), 0.0)


def _scan(s_re, s_im, pw_ref, reverse, pair=None):
    n_steps = s_re.shape[0] // SUBLANES
    n_strips = s_re.shape[1] // LANES
    sign = -1.0 if reverse else 1.0
    strips = [slice(st * LANES, (st + 1) * LANES) for st in range(n_strips)]

    def rows_of(j):
        step = (n_steps - 1 - j) if reverse else j
        return pl.ds(pl.multiple_of(step * SUBLANES, SUBLANES), SUBLANES)

    a = [(jnp.broadcast_to(pw_ref[0, 0:1, lanes], (SUBLANES, LANES)),
          sign * jnp.broadcast_to(pw_ref[1, 0:1, lanes], (SUBLANES, LANES))) for lanes in strips]

    def local(i, carry):
        for half in range(2):
            rows = rows_of(2 * i + half)
            out = []
            for st, lanes in enumerate(strips):
                (ar, ai), cr, ci = a[st], carry[2 * st], carry[2 * st + 1]
                xr = s_re[rows, lanes] + (ar * cr - ai * ci)
                xi = s_im[rows, lanes] + (ar * ci + ai * cr)
                s_re[rows, lanes] = xr
                s_im[rows, lanes] = xi
                out += [xr, xi]
            carry = tuple(out)
        return carry

    zero = jnp.zeros((SUBLANES, LANES), F32)
    ends = lax.fori_loop(0, n_steps // 2, local, (zero,) * (2 * n_strips))

    entering = []
    row = lax.broadcasted_iota(jnp.int32, (SUBLANES, LANES), 0)
    for st, lanes in enumerate(strips):
        tr, ti = ends[2 * st], ends[2 * st + 1]
        mr = jnp.broadcast_to(pw_ref[0, n_steps - 1:n_steps, lanes], (SUBLANES, LANES))
        mi = sign * jnp.broadcast_to(pw_ref[1, n_steps - 1:n_steps, lanes], (SUBLANES, LANES))
        for k in (1, 2, 4):
            keep = (row < SUBLANES - k) if reverse else (row >= k)
            rr = jnp.where(keep, pltpu.roll(tr, SUBLANES - k if reverse else k, 0), 0.0)
            ri = jnp.where(keep, pltpu.roll(ti, SUBLANES - k if reverse else k, 0), 0.0)
            tr, ti = tr + (mr * rr - mi * ri), ti + (mr * ri + mi * rr)
            mr, mi = mr * mr - mi * mi, 2.0 * mr * mi
        entering += [_segment_shift(tr, reverse), _segment_shift(ti, reverse)]

    def fix(i, carry):
        carry, sums = carry[:2 * n_strips], carry[2 * n_strips:]
        for half in range(2):
            j = 2 * i + half
            rows = rows_of(j)
            out, acc = [], []
            for st, lanes in enumerate(strips):
                (ar, ai), cr, ci = a[st], carry[2 * st], carry[2 * st + 1]
                cr, ci = ar * cr - ai * ci, ar * ci + ai * cr
                xr = s_re[rows, lanes] + cr
                xi = s_im[rows, lanes] + ci
                s_re[rows, lanes] = xr
                s_im[rows, lanes] = xi
                out += [cr, ci]
                if pair is not None:
                    p_rows = rows_of(jnp.minimum(j + 1, n_steps - 1))
                    keep = (j < n_steps - 1).astype(F32)
                    pr = pair[0][p_rows, lanes] * keep
                    pi = pair[1][p_rows, lanes] * keep
                    acc += [sums[2 * st] + (xr * pr + xi * pi), sums[2 * st + 1] + (xi * pr - xr * pi)]
            carry, sums = tuple(out), tuple(acc)
        return carry + sums

    n_sums = 0 if pair is None else 2 * n_strips
    out = lax.fori_loop(0, n_steps // 2, fix, tuple(entering) + (zero,) * n_sums)
    return out[2 * n_strips:]


def _seq_fwd(proj, conv_w, bc_re, bc_im, cc_re, cc_im, dskip, a_pow, name):
    tp = proj.shape[0]
    dh = proj.shape[1] // 4
    nq = dh // LANES
    sw = STATE * N_GROUPS // nq

    def body(b_ref, c_ref, v_ref, u_ref, w_ref, bre_ref, bim_ref, cre_ref, cim_ref, d_ref, pw_ref,
             co_ref, y_ref, g_ref, s_re, s_im, u_il, y_il):
        co_ref[...] = b_ref[...] * _dwconv(c_ref[...] * v_ref[...], w_ref)
        _interleave(u_il, u_ref)
        ub = u_il[...].astype(BF16)
        s_re[...] = jnp.dot(ub, bre_ref[...], preferred_element_type=F32)
        s_im[...] = jnp.dot(ub, bim_ref[...], preferred_element_type=F32)
        _scan(s_re, s_im, pw_ref, False)
        y_il[...] = (jnp.dot(s_re[...].astype(BF16), cre_ref[...], preferred_element_type=F32)
                     - jnp.dot(s_im[...].astype(BF16), cim_ref[...], preferred_element_type=F32))
        _deinterleave(y_ref, y_il)
        y = y_ref[...] + d_ref[...] * u_ref[...]
        y_ref[...] = y
        g_ref[...] = _gelu(y).astype(BF16)

    col = lambda off: pl.BlockSpec((tp, LANES), lambda q, off=off: (0, off * nq + q))
    blk = pl.BlockSpec((tp, LANES), lambda q: (0, q))
    return pl.pallas_call(
        body, name=name, grid=(nq,),
        in_specs=[col(0), col(1), col(2), col(3),
                  pl.BlockSpec((3, LANES), lambda q: (0, q)),
                  pl.BlockSpec((LANES, sw), lambda q: (0, q)), pl.BlockSpec((LANES, sw), lambda q: (0, q)),
                  pl.BlockSpec((sw, LANES), lambda q: (q, 0)), pl.BlockSpec((sw, LANES), lambda q: (q, 0)),
                  pl.BlockSpec((1, LANES), lambda q: (0, q)),
                  pl.BlockSpec((2, tp // SUBLANES, sw), lambda q: (0, 0, q))],
        out_specs=[blk, blk, blk, pl.BlockSpec((tp, sw), lambda q: (0, q)), pl.BlockSpec((tp, sw), lambda q: (0, q))],
        out_shape=[jax.ShapeDtypeStruct((tp, dh), F32), jax.ShapeDtypeStruct((tp, dh), F32),
                   jax.ShapeDtypeStruct((tp, dh), BF16),
                   jax.ShapeDtypeStruct((tp, nq * sw), F32), jax.ShapeDtypeStruct((tp, nq * sw), F32)],
        scratch_shapes=[pltpu.VMEM((tp, LANES), F32), pltpu.VMEM((tp, LANES), F32)],
        compiler_params=_cparams("parallel"),
    )(proj, proj, proj, proj, conv_w, bc_re, bc_im, cc_re, cc_im, dskip, a_pow)


def _conv_bwd(proj, dco, conv_w, name):
    tp = proj.shape[0]
    dh = proj.shape[1] // 4
    nq = dh // LANES

    def body(b_ref, c_ref, v_ref, dco_ref, w_ref, dproj_ref, dw_ref, stage, sem):
        q = pl.program_id(0)
        cg = c_ref[...]
        vg = v_ref[...]
        cv = cg * vg
        dco_v = dco_ref[...]
        dcv, dw = _dwconv_bwd(cv, dco_v * b_ref[...], w_ref)
        dw_ref[...] = dw
        stage[0] = (dco_v * _dwconv(cv, w_ref)).astype(BF16)
        stage[1] = (dcv * vg).astype(BF16)
        stage[2] = (dcv * cg).astype(BF16)
        copies = [pltpu.make_async_copy(stage.at[p], dproj_ref.at[:, pl.ds((p * nq + q) * LANES, LANES)], sem.at[p])
                  for p in range(3)]
        for cp in copies:
            cp.start()
        for cp in copies:
            cp.wait()

    col = lambda off: pl.BlockSpec((tp, LANES), lambda q, off=off: (0, off * nq + q))
    return pl.pallas_call(
        body, name=name, grid=(nq,),
        in_specs=[col(0), col(1), col(2), pl.BlockSpec((tp, LANES), lambda q: (0, q)),
                  pl.BlockSpec((3, LANES), lambda q: (0, q))],
        out_specs=[pl.BlockSpec(memory_space=pl.ANY), pl.BlockSpec((3, LANES), lambda q: (0, q))],
        out_shape=[jax.ShapeDtypeStruct((tp, 4 * dh), BF16), jax.ShapeDtypeStruct((3, dh), F32)],
        scratch_shapes=[pltpu.VMEM((3, tp, LANES), BF16), pltpu.SemaphoreType.DMA((3,))],
        compiler_params=_cparams("arbitrary"),
    )(proj, proj, proj, dco, conv_w)


def _ssm_bwd(proj, y, dg, dproj, states, bc_re, bc_im, cc_re, cc_im, dskip, a_pow, name):
    tp = proj.shape[0]
    dh = proj.shape[1] // 4
    nq = dh // LANES
    sw = STATE * N_GROUPS // nq

    def body(u_ref, y_ref, dg_ref, dproj_in, s_re, s_im, bre_ref, bim_ref, cre_ref, cim_ref, d_ref, pw_ref,
             dproj_ref, dbre_ref, dbim_ref, dcre_ref, dcim_ref, dd_ref, dar_ref, dai_ref,
             l_re, l_im, a_il, b_il, stage, sem):
        del dproj_in
        q = pl.program_id(0)
        nt = (((1,), (1,)), ((), ()))
        tn = (((0,), (0,)), ((), ()))
        _interleave(a_il, u_ref)
        ub = a_il[...].astype(BF16)
        dy_rows = dg_ref[...] * _gelu_grad(y_ref[...])
        dd_ref[...] = jnp.sum(dy_rows * u_ref[...], axis=0, keepdims=True)
        _interleave(b_il, dy_rows)
        dy = b_il[...]
        dyb = dy.astype(BF16)
        l_re[...] = lax.dot_general(dyb, cre_ref[...], nt, preferred_element_type=F32)
        l_im[...] = -lax.dot_general(dyb, cim_ref[...], nt, preferred_element_type=F32)
        dcre_ref[...] = lax.dot_general(s_re[...].astype(BF16), dyb, tn, preferred_element_type=F32)
        dcim_ref[...] = -lax.dot_general(s_im[...].astype(BF16), dyb, tn, preferred_element_type=F32)
        sums = _scan(l_re, l_im, pw_ref, True, pair=(s_re, s_im))
        rest = tp - SUBLANES
        for st in range(sw // LANES):
            lanes = slice(st * LANES, (st + 1) * LANES)
            lr0, li0 = l_re[:SUBLANES, lanes], l_im[:SUBLANES, lanes]
            pr0, pi0 = _segment_shift(s_re[rest:, lanes], False), _segment_shift(s_im[rest:, lanes], False)
            dar_ref[:, lanes] = jnp.sum(sums[2 * st] + (lr0 * pr0 + li0 * pi0), axis=0, keepdims=True)
            dai_ref[:, lanes] = jnp.sum(sums[2 * st + 1] + (li0 * pr0 - lr0 * pi0), axis=0, keepdims=True)
        lrb = l_re[...].astype(BF16)
        lib = l_im[...].astype(BF16)
        a_il[...] = (dy * d_ref[...] + lax.dot_general(lrb, bre_ref[...], nt, preferred_element_type=F32)
                     + lax.dot_general(lib, bim_ref[...], nt, preferred_element_type=F32))
        _deinterleave(b_il, a_il)
        stage[...] = b_il[...].astype(BF16)
        dbre_ref[...] = lax.dot_general(ub, lrb, tn, preferred_element_type=F32)
        dbim_ref[...] = lax.dot_general(ub, lib, tn, preferred_element_type=F32)
        cp = pltpu.make_async_copy(stage, dproj_ref.at[:, pl.ds((3 * nq + q) * LANES, LANES)], sem)
        cp.start()
        cp.wait()

    blk = pl.BlockSpec((tp, LANES), lambda q: (0, q))
    bspec = pl.BlockSpec((LANES, sw), lambda q: (0, q))
    cspec = pl.BlockSpec((sw, LANES), lambda q: (q, 0))
    tspec = pl.BlockSpec((2, tp // SUBLANES, sw), lambda q: (0, 0, q))
    nstate = STATE * N_GROUPS
    return pl.pallas_call(
        body, name=name, grid=(nq,),
        in_specs=[pl.BlockSpec((tp, LANES), lambda q: (0, 3 * nq + q)), blk, blk, pl.BlockSpec(memory_space=pl.ANY),
                  pl.BlockSpec((tp, sw), lambda q: (0, q)), pl.BlockSpec((tp, sw), lambda q: (0, q)),
                  bspec, bspec, cspec, cspec, pl.BlockSpec((1, LANES), lambda q: (0, q)), tspec],
        out_specs=[pl.BlockSpec(memory_space=pl.ANY), bspec, bspec, cspec, cspec,
                   pl.BlockSpec((1, LANES), lambda q: (0, q)),
                   pl.BlockSpec((1, sw), lambda q: (0, q)), pl.BlockSpec((1, sw), lambda q: (0, q))],
        out_shape=[jax.ShapeDtypeStruct((tp, 4 * dh), BF16),
                   jax.ShapeDtypeStruct((LANES, nstate), F32), jax.ShapeDtypeStruct((LANES, nstate), F32),
                   jax.ShapeDtypeStruct((nstate, LANES), F32), jax.ShapeDtypeStruct((nstate, LANES), F32),
                   jax.ShapeDtypeStruct((1, dh), F32),
                   jax.ShapeDtypeStruct((1, nstate), F32), jax.ShapeDtypeStruct((1, nstate), F32)],
        input_output_aliases={3: 0},
        scratch_shapes=[pltpu.VMEM((tp, sw), F32)] * 2 + [pltpu.VMEM((tp, LANES), F32)] * 2
        + [pltpu.VMEM((tp, LANES), BF16), pltpu.SemaphoreType.DMA],
        compiler_params=_cparams("arbitrary"),
    )(proj, y, dg, dproj, states[0], states[1], bc_re, bc_im, cc_re, cc_im, dskip, a_pow)


FFN_TILE = 256
FFN_ROWS = 32


def _window(x_ref, before, r0, rows, cols):
    if r0 == 0:
        return jnp.concatenate([before, x_ref[0:rows, cols]], axis=0)
    return x_ref[r0 - SUBLANES:r0 + rows, cols]


def _taps(window):
    return window[SUBLANES:], pltpu.roll(window, 1, 0)[SUBLANES:], pltpu.roll(window, 2, 0)[SUBLANES:]


def _conv_taps(taps, w):
    return w[2] * taps[0] + w[1] * taps[1] + w[0] * taps[2]


FFN_MM_ROWS = 544
FFN_MM_COLS = 1408


def _ffn_up_act(hn, w_up, fw, fb, col, others, name):
    tp, dm = hn.shape
    dff = w_up.shape[1] // 2
    tr, cw, rows = FFN_MM_ROWS, FFN_MM_COLS, FFN_ROWS
    nc = dff // cw
    n_others = 0 if others is None else 2

    def body(hn_ref, ma_ref, mv_ref, wa_ref, wv_ref, ba_ref, bv_ref, *rest):
        up_ref, act_ref, tail_ref = rest[n_others:]

        @pl.when(pl.program_id(0) == 0)
        def _():
            tail_ref[...] = jnp.zeros_like(tail_ref)

        x = hn_ref[...]
        up_ref[0] = jnp.dot(x, ma_ref[...], preferred_element_type=F32)
        up_ref[1] = jnp.dot(x, mv_ref[...], preferred_element_type=F32)
        for c0 in range(0, cw, FFN_TILE):
            cols = slice(c0, min(c0 + FFN_TILE, cw))
            wa, wv = [[w_ref[k:k + 1, cols] for k in range(3)] for w_ref in (wa_ref, wv_ref)]
            ba, bv = ba_ref[:, cols], bv_ref[:, cols]
            before_a, before_v = tail_ref[0, :, cols], tail_ref[1, :, cols]
            for r0 in range(0, tr, rows):
                a = _conv_taps(_taps(_window(up_ref.at[0], before_a, r0, rows, cols)), wa) + ba
                v = _conv_taps(_taps(_window(up_ref.at[1], before_v, r0, rows, cols)), wv) + bv
                act_ref[r0:r0 + rows, cols] = (a * _sigmoid(a) * v).astype(BF16)
            tail_ref[:, :, cols] = up_ref[:, tr - SUBLANES:tr, cols]

    par = lambda r, half: pl.BlockSpec((r, cw), lambda i: (0, half * nc + col))
    return pl.pallas_call(
        body, name=name, grid=(tp // tr,),
        in_specs=[pl.BlockSpec((tr, dm), lambda i: (i, 0)), par(dm, 0), par(dm, 1),
                  par(3, 0), par(3, 1), par(1, 0), par(1, 1)] + [pl.BlockSpec(memory_space=pl.ANY)] * n_others,
        out_specs=[pl.BlockSpec((2, tr, cw), lambda i: (0, i, col)), pl.BlockSpec((tr, cw), lambda i: (i, col))],
        out_shape=[jax.ShapeDtypeStruct((2, tp, dff), F32), jax.ShapeDtypeStruct((tp, dff), BF16)],
        input_output_aliases={7: 0, 8: 1} if others is not None else {},
        scratch_shapes=[pltpu.VMEM((2, SUBLANES, cw), F32)],
        compiler_params=_cparams("arbitrary"))(hn, w_up, w_up, fw, fw, fb, fb, *(others or ()))


def _ffn_bwd(up, dh, w_down, fw, fb, name):
    _, tp, dff = up.shape
    two_ff = 2 * dff
    dm = dh.shape[1]
    tr, cw, rows = FFN_MM_ROWS, FFN_MM_COLS, FFN_ROWS
    nr, nc = tp // tr, dff // cw
    n_e = rows + SUBLANES
    pieces = tr // SUBLANES

    def body(ua_ref, uv_ref, pa_ref, pv_ref, dh_ref, wd_ref, wa_ref, wv_ref, ba_ref, bv_ref,
             dup_ref, dwa_ref, dwv_ref, dba_ref, dbv_ref, dact, stage, head_ref, sem):
        j, i = pl.program_id(0), pl.program_id(1)
        step = j * nr + i
        top = i == nr - 1
        sums = ((dwa_ref, dba_ref), (dwv_ref, dbv_ref))

        slot = step % 2

        def out_copies(at):
            r0 = pl.multiple_of((nr - 1 - at % nr) * tr, tr)
            return [pltpu.make_async_copy(
                stage.at[at % 2, s],
                dup_ref.at[pl.ds(r0, tr), pl.ds(pl.multiple_of(s * dff + at // nr * cw, LANES), cw)],
                sem.at[at % 2, s]) for s in range(2)]

        @pl.when(i == 0)
        def _():
            head_ref[...] = jnp.zeros_like(head_ref)
            for dw_ref, db_ref in sums:
                dw_ref[...] = jnp.zeros_like(dw_ref)
                db_ref[...] = jnp.zeros_like(db_ref)

        dact[...] = lax.dot_general(dh_ref[...], wd_ref[...], (((1,), (1,)), ((), ())), preferred_element_type=F32)

        @pl.when(step > 1)
        def _():
            for cp in out_copies(step - 2):
                cp.wait()

        def gate_bwd(taps, dact_v, w, bias):
            a, v = [_conv_taps(taps[s], w[s]) + bias[s] for s in range(2)]
            sg = _sigmoid(a)
            return [dact_v * v * sg * (1.0 + a * (1.0 - sg)), dact_v * a * sg]

        fold = lambda x: sum(x[r:r + SUBLANES] for r in range(0, rows, SUBLANES))
        for c0 in range(0, cw, FFN_TILE):
            cols = slice(c0, min(c0 + FFN_TILE, cw))
            w = [[w_ref[k:k + 1, cols] for k in range(3)] for w_ref in (wa_ref, wv_ref)]
            bias = [ba_ref[:, cols], bv_ref[:, cols]]
            before = [jnp.where(top, 0.0, p_ref[:, cols]) for p_ref in (pa_ref, pv_ref)]
            head = [head_ref[s, :, cols] for s in range(2)]
            piece = jnp.zeros_like(head[0])
            acc = [[piece] * 4 for _ in range(2)]
            for r0 in reversed(range(0, tr, rows)):
                taps = [_taps(_window(x_ref, before[s], r0, rows, cols)) for s, x_ref in enumerate((ua_ref, uv_ref))]
                d = gate_bwd(taps, dact[r0:r0 + rows, cols], w, bias)
                for s in range(2):
                    de = jnp.concatenate([d[s], head[s]], axis=0)
                    dx = (w[s][2] * d[s] + w[s][1] * pltpu.roll(de, n_e - 1, 0)[:rows]
                          + w[s][0] * pltpu.roll(de, n_e - 2, 0)[:rows])
                    stage[slot, s, r0:r0 + rows, cols] = dx.astype(BF16)
                    for k in range(3):
                        acc[s][k] = acc[s][k] + fold(d[s] * taps[s][2 - k])
                    acc[s][3] = acc[s][3] + fold(d[s])
                    head[s] = d[s][:SUBLANES]
            for s, (dw_ref, db_ref) in enumerate(sums):
                head_ref[s, :, cols] = head[s]
                dw_ref[:, cols] = dw_ref[:, cols] + jnp.concatenate(
                    [jnp.sum(x, axis=0, keepdims=True) for x in acc[s][:3]], axis=0)
                db_ref[:, cols] = db_ref[:, cols] + jnp.sum(acc[s][3], axis=0, keepdims=True)

        copies = out_copies(step)
        for cp in copies:
            cp.start()

        @pl.when(step == nc * nr - 1)
        def _():
            for cp in out_copies(step - 1) + copies:
                cp.wait()

    row = lambda i: nr - 1 - i
    main = lambda half: pl.BlockSpec((None, tr, cw), lambda j, i: (half, row(i), j))
    prev = lambda half: pl.BlockSpec((None, SUBLANES, cw), lambda j, i: (half, jnp.maximum(row(i) * pieces - 1, 0), j))
    par = lambda r, half: pl.BlockSpec((r, cw), lambda j, i: (0, half * nc + j))
    acc_spec = lambda r: pl.BlockSpec((r, cw), lambda j, i: (0, j))
    return pl.pallas_call(
        body, name=name, grid=(nc, nr),
        in_specs=[main(0), main(1), prev(0), prev(1),
                  pl.BlockSpec((tr, dm), lambda j, i: (row(i), 0)), pl.BlockSpec((cw, dm), lambda j, i: (j, 0)),
                  par(3, 0), par(3, 1), par(1, 0), par(1, 1)],
        out_specs=[pl.BlockSpec(memory_space=pl.ANY), acc_spec(3), acc_spec(3), acc_spec(1), acc_spec(1)],
        out_shape=[jax.ShapeDtypeStruct((tp, two_ff), BF16),
                   jax.ShapeDtypeStruct((3, dff), F32), jax.ShapeDtypeStruct((3, dff), F32),
                   jax.ShapeDtypeStruct((1, dff), F32), jax.ShapeDtypeStruct((1, dff), F32)],
        scratch_shapes=[pltpu.VMEM((tr, cw), F32), pltpu.VMEM((2, 2, tr, cw), BF16),
                        pltpu.VMEM((2, SUBLANES, cw), F32), pltpu.SemaphoreType.DMA((2, 2))],
        compiler_params=_cparams("arbitrary", "arbitrary"))(up, up, up, up, dh, w_down, fw, fw, fb, fb)


def _zoh(lr, li, ld):
    dt = jnp.exp(ld)
    mag = jnp.exp(lr * dt)
    ang = li * dt
    ar = mag * jnp.cos(ang)
    ai = mag * jnp.sin(ang)
    den = lr * lr + li * li
    nr = ar - 1.0
    fr = (nr * lr + ai * li) / den
    fi = (ai * lr - nr * li) / den
    return dt, ar, ai, den, nr, fr, fi


def _s5_prep(lr, li, ld, b_re, b_im, n_pow, name):
    nstate = lr.shape[1]

    def body(lr_ref, li_ref, ld_ref, bre_ref, bim_ref, pw_ref, bcre_ref, bcim_ref):
        _, ar, ai, _, _, fr, fi = _zoh(lr_ref[...], li_ref[...], ld_ref[...])
        bre = bre_ref[...]
        bim = bim_ref[...]
        bcre_ref[...] = (fr * bre - fi * bim).astype(BF16)
        bcim_ref[...] = (fr * bim + fi * bre).astype(BF16)
        row = lax.broadcasted_iota(jnp.int32, (SUBLANES, nstate), 0)
        pr, pi = jnp.zeros((SUBLANES, nstate), F32), jnp.zeros((SUBLANES, nstate), F32)
        cr, ci = ar, ai
        for t in range(SUBLANES):
            pr, pi = jnp.where(row == t, cr, pr), jnp.where(row == t, ci, pi)
            cr, ci = cr * ar - ci * ai, cr * ai + ci * ar
        pw_ref[0, 0:SUBLANES, :] = pr
        pw_ref[1, 0:SUBLANES, :] = pi
        n = SUBLANES
        while n < n_pow:
            m = min(n, n_pow - n)
            tr, ti = pw_ref[0, n - 1:n, :], pw_ref[1, n - 1:n, :]
            xr, xi = pw_ref[0, 0:m, :], pw_ref[1, 0:m, :]
            pw_ref[0, n:n + m, :] = xr * tr - xi * ti
            pw_ref[1, n:n + m, :] = xr * ti + xi * tr
            n += m

    vmem = pl.BlockSpec(memory_space=pltpu.VMEM)
    return pl.pallas_call(
        body, name=name, in_specs=[vmem] * 5, out_specs=[vmem] * 3,
        out_shape=[jax.ShapeDtypeStruct((2, n_pow, nstate), F32)] + [jax.ShapeDtypeStruct(b_re.shape, BF16)] * 2,
        compiler_params=pltpu.CompilerParams(vmem_limit_bytes=VMEM_LIMIT))(lr, li, ld, b_re, b_im)


def _s5_prep_bwd(lr, li, ld, b_re, b_im, da_re, da_im, dbc_re, dbc_im, name):
    def body(lr_ref, li_ref, ld_ref, bre_ref, bim_ref, dar_ref, dai_ref, dbcre_ref, dbcim_ref,
             dlr_ref, dli_ref, dld_ref, dbre_ref, dbim_ref):
        lr, li = lr_ref[...], li_ref[...]
        dt, ar, ai, den, nr, fr, fi = _zoh(lr, li, ld_ref[...])
        bre, bim = bre_ref[...], bim_ref[...]
        gre, gim = dbcre_ref[...], dbcim_ref[...]
        dbre_ref[...] = fr * gre + fi * gim
        dbim_ref[...] = fr * gim - fi * gre
        g_fr = jnp.sum(gre * bre + gim * bim, axis=0, keepdims=True)
        g_fi = jnp.sum(gim * bre - gre * bim, axis=0, keepdims=True)
        g_ar = dar_ref[...] + (g_fr * lr - g_fi * li) / den
        g_ai = dai_ref[...] + (g_fr * li + g_fi * lr) / den
        d_lr = (g_fr * (nr - 2.0 * fr * lr) + g_fi * (ai - 2.0 * fi * lr)) / den
        d_li = (g_fr * (ai - 2.0 * fr * li) - g_fi * (nr + 2.0 * fi * li)) / den
        g_logmag = g_ar * ar + g_ai * ai
        g_ang = g_ai * ar - g_ar * ai
        dlr_ref[...] = d_lr + g_logmag * dt
        dli_ref[...] = d_li + g_ang * dt
        d_ld = (g_logmag * lr + g_ang * li) * dt
        n = d_ld.shape[1]
        sh = 1
        while sh < STATE:
            d_ld = d_ld + pltpu.roll(d_ld, n - sh, 1)
            sh *= 2
        dld_ref[...] = d_ld

    vmem = pl.BlockSpec(memory_space=pltpu.VMEM)
    row = jax.ShapeDtypeStruct(lr.shape, F32)
    return pl.pallas_call(
        body, name=name, in_specs=[vmem] * 9, out_specs=[vmem] * 5,
        out_shape=[row, row, row, jax.ShapeDtypeStruct(b_re.shape, F32), jax.ShapeDtypeStruct(b_re.shape, F32)],
    )(lr, li, ld, b_re, b_im, da_re, da_im, dbc_re, dbc_im)


def _compact_b(bb):
    bq = bb.reshape(N_GROUPS // 8, 8, STATE, GROUP)
    m = jnp.einsum("ab,qbph->qahbp", jnp.eye(8, dtype=bb.dtype), bq).reshape(N_GROUPS // 8, LANES, 8 * STATE)
    return m.transpose(1, 0, 2).reshape(LANES, N_GROUPS * STATE)


def _expand_b(m):
    d = m.reshape(8, GROUP, N_GROUPS // 8, 8, STATE)
    return jnp.einsum("ahqap->qahp", d).reshape(N_GROUPS, GROUP, STATE)


def _compact_c(c):
    cq = c.reshape(N_GROUPS // 8, 8, GROUP, STATE)
    return jnp.einsum("ab,qbhp->qbpah", jnp.eye(8, dtype=c.dtype), cq).reshape(N_GROUPS * STATE, LANES)


def _expand_c(m):
    d = m.reshape(N_GROUPS // 8, 8, STATE, 8, GROUP)
    return jnp.einsum("qbpbh->qbhp", d).reshape(N_GROUPS, GROUP, STATE)


def _local_step(x, target, p, ex):
    seq, d = x.shape
    n_real = N_META + seq
    tp = -(-n_real // ROW_ALIGN) * ROW_ALIGN

    h0, hn1 = _input_norm_fwd(x, p["meta_tokens"], p["norm_mix_g"] + ex.zero, tp, "norm_mix")
    ex.forward("first", hn1)
    nstate = N_GROUPS * STATE
    s5 = (p["ssm_lam_re"].reshape(1, nstate), p["ssm_lam_im"].reshape(1, nstate),
          jnp.repeat(p["ssm_log_dt"].reshape(-1), STATE).reshape(1, nstate),
          _compact_b(p["ssm_b_re"]), _compact_b(p["ssm_b_im"]))
    a_pow, bc_re, bc_im = _s5_prep(*s5, tp // SUBLANES, "s5_prep")
    cc_re = _compact_c(p["ssm_c_re"]).astype(BF16)
    cc_im = _compact_c(p["ssm_c_im"]).astype(BF16)
    dskip = p["ssm_d"].reshape(1, -1)
    first = ex.weights("first", bc_re)
    proj = _mm(hn1, first["w_in"], "nn", "proj")
    started = ex.forward("mid", proj)
    co, y, g, *states = _seq_fwd(proj, p["conv_w"] + started[0, 0], bc_re, bc_im, cc_re, cc_im, dskip, a_pow,
                                 "seq_fwd")
    mid = ex.weights("mid", g)
    z = _mm(g, mid["ssm_w_glu"], "nn", "glu")
    mixed = _mix_fwd(co, y, z, p["gain_conv_out"], p["gain_ssm_out"], "mix_fwd")
    started = ex.forward("up", mixed)
    h1, hn2 = _proj_res_norm(mixed, mid["w_out"], h0, p["norm_ffn_g"], started, "out_proj_norm")
    late = ex.weights("up", hn2)
    part, fw = None, p["ffn_conv_w"]
    for col in range(late["w_up"].shape[1] // (2 * FFN_MM_COLS)):
        part = _ffn_up_act(hn2, late["w_up"], fw, p["ffn_conv_b"], col, part, "ffn_up_act_%d" % col)
        if col == 0:
            fw = fw + ex.forward("down", part[1])[0, 0]
    up, act = part
    late.update(ex.weights("down", act))
    loss, dh2, dh2b, d_gfin = _proj_loss_bwd(act, late["w_down"], h1, target, p["norm_final_g"], n_real,
                                             "down_proj_loss")

    g_w_down = _mm(act, dh2b, "tn", "g_w_down")
    dup, dfw_a, dfw_v, dfb_a, dfb_v = _ffn_bwd(up, dh2b, late["w_down"], p["ffn_conv_w"], p["ffn_conv_b"], "ffn_bwd")
    g_w_up = _mm(hn2, dup, "tn", "g_w_up")
    started = ex.grads_ready("late", {"w_up": g_w_up, "w_down": g_w_down})
    dh1, dh1b, d_gffn = _proj_norm_bwd(dup, late["w_up"], h1, p["norm_ffn_g"], dh2, started, "d_hn2_norm_bwd")
    started = ex.grads_send("late", dh1)
    g_w_out = _mm(mixed, dh1b, "tn", "g_w_out", after=started)
    dco, dz, dgp, d_gc, d_gs = _proj_mix_bwd(dh1b, mid["w_out"], co, y, z, p["gain_conv_out"],
                                             p["gain_ssm_out"], "d_mixed_mix_bwd")
    g_w_glu = _mm(g, dz, "tn", "g_w_glu")
    started = ex.grads_ready("mid", {"ssm_w_glu": g_w_glu, "w_out": g_w_out})
    dg = _mm(dz, mid["ssm_w_glu"], "nt", "d_gelu", acc_in=dgp, after=started)
    started = ex.grads_send("mid", dg)
    dproj, d_conv_w = _conv_bwd(proj, dco, p["conv_w"] + started[0, 0], "conv_bwd")
    (dproj, dbc_re, dbc_im, dcc_re, dcc_im, d_dskip, da_re, da_im) = _ssm_bwd(
        proj, y, dg, dproj, states, bc_re, bc_im, cc_re, cc_im, dskip, a_pow, "ssm_bwd")
    g_w_in = _mm(hn1, dproj, "tn", "g_w_in")
    started = ex.grads_ready("first", {"w_in": g_w_in})
    grad_x, d_meta, d_gmix = _proj_input_norm_bwd(dproj, first["w_in"], h0, p["norm_mix_g"], dh1, started, n_real,
                                                  "d_hn1_norm_bwd")
    started = ex.grads_send("first", d_gmix)

    d_lam_re, d_lam_im, d_log_dt, d_b_re, d_b_im = _s5_prep_bwd(*s5, da_re, da_im, dbc_re, dbc_im, "s5_prep_bwd")
    d_lam_re, d_lam_im = d_lam_re.reshape(N_GROUPS, STATE), d_lam_im.reshape(N_GROUPS, STATE)
    d_log_dt = d_log_dt[0, ::STATE]
    d_b_re, d_b_im = _expand_b(d_b_re), _expand_b(d_b_im)
    grads = {
        "meta_tokens": d_meta, "norm_mix_g": d_gmix, "w_in": g_w_in, "conv_w": d_conv_w,
        "ssm_lam_re": d_lam_re, "ssm_lam_im": d_lam_im, "ssm_log_dt": d_log_dt,
        "ssm_b_re": d_b_re, "ssm_b_im": d_b_im, "ssm_c_re": _expand_c(dcc_re), "ssm_c_im": _expand_c(dcc_im),
        "ssm_d": d_dskip.reshape(N_GROUPS, GROUP), "ssm_w_glu": g_w_glu,
        "gain_conv_out": d_gc, "gain_ssm_out": d_gs, "w_out": g_w_out, "norm_ffn_g": d_gffn,
        "w_up": g_w_up, "ffn_conv_w": jnp.concatenate([dfw_a, dfw_v], axis=1),
        "ffn_conv_b": jnp.concatenate([dfb_a, dfb_v], axis=1), "w_down": g_w_down, "norm_final_g": d_gfin,
    }
    return loss[0, 0] + started[0, 0], grad_x, grads


def _view(ref, axis, start, size):
    idx = [slice(None)] * len(ref.shape)
    idx[axis] = pl.ds(start, size)
    return ref.at[tuple(idx)]


def _exchange(name, ins, outs, aliases, local_copies, remote_copies):
    ni, no = len(ins), len(outs)
    nl, nr = len(local_copies), len(remote_copies)

    def body(*refs):
        in_refs, out_refs = refs[:ni], refs[ni:ni + no]
        send_sems, recv_sems, local_sems = refs[ni + no:]
        x, y, c = lax.axis_index("x"), lax.axis_index("y"), lax.axis_index("c")
        pos = (x, y, c, 2 * x + y)
        locals_ = [pltpu.make_async_copy(s(in_refs, out_refs, pos), d(in_refs, out_refs, pos), local_sems.at[i])
                   for i, (s, d) in enumerate(local_copies)]
        remotes = []
        for i, (s, d, flip) in enumerate(remote_copies):
            peer = (1 - x if "x" in flip else x, 1 - y if "y" in flip else y, 1 - c if "c" in flip else c)
            remotes.append(pltpu.make_async_remote_copy(
                src_ref=s(in_refs, out_refs, pos), dst_ref=d(in_refs, out_refs, pos),
                send_sem=send_sems.at[i], recv_sem=recv_sems.at[i], device_id=peer, device_id_type=MESH))
        for cp in locals_ + remotes:
            cp.start()
        for cp in remotes:
            cp.wait_recv()
        for cp in remotes:
            cp.wait_send()
        for cp in locals_:
            cp.wait()

    hbm = pl.BlockSpec(memory_space=pl.ANY)
    return pl.pallas_call(
        body, name=name, in_specs=[hbm] * ni, out_specs=[hbm] * no, out_shape=outs,
        input_output_aliases=aliases,
        scratch_shapes=[pltpu.SemaphoreType.DMA((nr,)), pltpu.SemaphoreType.DMA((nr,)),
                        pltpu.SemaphoreType.DMA((max(nl, 1),))],
    )(*ins)


BIG = {"w_in": (0, 1), "ssm_w_glu": (1, 0), "w_out": (1, 0), "w_up": (0, 1), "w_down": (1, 0)}
BIG_NAMES = tuple(BIG)
FLIPS = ("y", "x", "xy")


def _peer_chip(pos, flip):
    x, y, _, _ = pos
    return 2 * (1 - x if "x" in flip else x) + (1 - y if "y" in flip else y)


def _block_rows(rows, cols, itemsize, mult):
    return _pick_tile(rows, max(mult, (2 * 1024 * 1024) // (cols * itemsize)), mult)


def _cast_into_full(w, kc, shard_axis, name):
    r, cdim = w.shape
    tr = _block_rows(r, cdim, 4, 16)
    nb = r // tr

    def body(kc_ref, w_ref, o_ref):
        o_ref[...] = w_ref[...].astype(BF16)

    if shard_axis == 1:
        full, o_spec = (r, 4 * cdim), pl.BlockSpec((tr, cdim), lambda i, kc: (i, kc[0]))
    else:
        full, o_spec = (4 * r, cdim), pl.BlockSpec((tr, cdim), lambda i, kc: (kc[0] * nb + i, 0))
    return pl.pallas_call(
        body, name=name,
        grid_spec=pltpu.PrefetchScalarGridSpec(
            num_scalar_prefetch=1, grid=(nb,), in_specs=[pl.BlockSpec((tr, cdim), lambda i, kc: (i, 0))],
            out_specs=o_spec),
        out_shape=jax.ShapeDtypeStruct(full, BF16), compiler_params=_cparams("parallel"))(kc, w)


def _pair_sum(g, recv, kc, half_axis, name, out_dtype):
    hr, hc = recv.shape
    tr = _block_rows(hr, hc, 4, 16)
    nb = hr // tr

    def body(kc_ref, g_ref, r_ref, o_ref):
        o_ref[...] = (g_ref[...] + r_ref[...]).astype(out_dtype)

    if half_axis == 0:
        g_spec = pl.BlockSpec((tr, hc), lambda i, kc: (kc[1] * nb + i, 0))
    elif half_axis == 1:
        g_spec = pl.BlockSpec((tr, hc), lambda i, kc: (i, kc[1]))
    else:
        g_spec = pl.BlockSpec((tr, hc), lambda i, kc: (i, 0))
    same = pl.BlockSpec((tr, hc), lambda i, kc: (i, 0))
    return pl.pallas_call(
        body, name=name,
        grid_spec=pltpu.PrefetchScalarGridSpec(num_scalar_prefetch=1, grid=(nb,), in_specs=[g_spec, same],
                                               out_specs=same),
        out_shape=jax.ShapeDtypeStruct((hr, hc), out_dtype), compiler_params=_cparams("parallel"))(kc, g, recv)


def _chip_sum(own, recv, kc, own_axis, out_axis, name):
    _, sr, sc = recv.shape
    tr = _block_rows(sr, sc, 4, 16)
    nb = sr // tr

    def body(kc_ref, o_ref, r_ref, t_ref):
        k = kc_ref[0]
        own_v = o_ref[...].astype(F32)
        r = [r_ref[m].astype(F32) for m in range(3)]
        terms = []
        for kk in range(4):
            m = jnp.bitwise_xor(k, kk)
            terms.append(jnp.where(m == 0, own_v, jnp.where(m == 1, r[0], jnp.where(m == 2, r[1], r[2]))))
        t_ref[...] = (terms[0] + terms[1]) + (terms[2] + terms[3])

    if own_axis == 0:
        own_spec = pl.BlockSpec((tr, sc), lambda i, kc: (kc[0] * nb + i, 0))
    elif own_axis == 1:
        own_spec = pl.BlockSpec((tr, sc), lambda i, kc: (i, kc[0]))
    else:
        own_spec = pl.BlockSpec((tr, sc), lambda i, kc: (kc[1] * nb + i, 0))
    if out_axis == 0:
        out_full, out_spec = (2 * sr, sc), pl.BlockSpec((tr, sc), lambda i, kc: (kc[1] * nb + i, 0))
    else:
        out_full, out_spec = (sr, 2 * sc), pl.BlockSpec((tr, sc), lambda i, kc: (i, kc[1]))
    return pl.pallas_call(
        body, name=name,
        grid_spec=pltpu.PrefetchScalarGridSpec(
            num_scalar_prefetch=1, grid=(nb,),
            in_specs=[own_spec, pl.BlockSpec((3, tr, sc), lambda i, kc: (0, i, 0))],
            out_specs=out_spec),
        out_shape=jax.ShapeDtypeStruct(out_full, F32), compiler_params=_cparams("parallel"))(kc, own, recv)


def _adamw(w, g, m, v, name):
    r, cdim = w.shape
    tr = _block_rows(r, cdim, 4, 8)
    c1 = 1.0 - ADAM_B1 ** ADAM_STEP
    c2 = 1.0 - ADAM_B2 ** ADAM_STEP

    def body(w_ref, g_ref, m_ref, v_ref, go_ref, d_ref, nm_ref, nv_ref):
        gv = g_ref[...]
        go_ref[...] = gv
        nm = ADAM_B1 * m_ref[...] + (1.0 - ADAM_B1) * gv
        nv = ADAM_B2 * v_ref[...] + (1.0 - ADAM_B2) * (gv * gv)
        d_ref[...] = -ADAM_LR * ((nm / c1) / (jnp.sqrt(nv / c2) + ADAM_EPS) + ADAM_WD * w_ref[...])
        nm_ref[...] = nm
        nv_ref[...] = nv

    spec = _rows(cdim, tr)
    return pl.pallas_call(body, name=name, grid=(r // tr,), in_specs=[spec] * 4, out_specs=[spec] * 4,
                          out_shape=[jax.ShapeDtypeStruct((r, cdim), F32)] * 4,
                          compiler_params=_cparams("parallel"))(w, g, m, v)


def _adamw_whole(ws, gs, ms, vs, name):
    n = len(ws)
    c1 = 1.0 - ADAM_B1 ** ADAM_STEP
    c2 = 1.0 - ADAM_B2 ** ADAM_STEP

    def body(*refs):
        for i in range(n):
            w_ref, g_ref, m_ref, v_ref, d_ref, nm_ref, nv_ref = [refs[j * n + i] for j in range(7)]
            gv = g_ref[...]
            nm = ADAM_B1 * m_ref[...] + (1.0 - ADAM_B1) * gv
            nv = ADAM_B2 * v_ref[...] + (1.0 - ADAM_B2) * (gv * gv)
            d_ref[...] = -ADAM_LR * ((nm / c1) / (jnp.sqrt(nv / c2) + ADAM_EPS) + ADAM_WD * w_ref[...])
            nm_ref[...] = nm
            nv_ref[...] = nv

    vmem = pl.BlockSpec(memory_space=pltpu.VMEM)
    out = pl.pallas_call(body, name=name, in_specs=[vmem] * (4 * n), out_specs=[vmem] * (3 * n),
                         out_shape=[jax.ShapeDtypeStruct(a.shape, F32) for a in ws] * 3,
                         compiler_params=pltpu.CompilerParams(vmem_limit_bytes=VMEM_LIMIT))(*ws, *gs, *ms, *vs)
    return out[:n], out[n:2 * n], out[2 * n:]


SIDE_EFFECT = pltpu.SideEffectType.DATAFLOW_SIDE_EFFECTING


def _descriptors(copies, refs, send_sems, recv_sems, sem_off=0):
    x, y, c = lax.axis_index("x"), lax.axis_index("y"), lax.axis_index("c")
    pos = (x, y, c, 2 * x + y)
    out = []
    for i, (s, d, flip) in enumerate(copies):
        peer = (1 - x if "x" in flip else x, 1 - y if "y" in flip else y, 1 - c if "c" in flip else c)
        out.append(pltpu.make_async_remote_copy(
            src_ref=s(refs, refs, pos), dst_ref=d(refs, refs, pos),
            send_sem=send_sems.at[sem_off + i], recv_sem=recv_sems.at[sem_off + i],
            device_id=peer, device_id_type=MESH))
    return out


def _shifted(copies, off):
    return [(lambda I, O, pos, s=s: s(I[off:], O[off:], pos), lambda I, O, pos, d=d: d(I[off:], O[off:], pos), flip)
            for s, d, flip in copies]


BARRIER_IDS = {"c": (1, 2), "ici": (3, 4)}


def _exchange_start(name, bufs, copies, turns, after=None):
    n, nr = len(bufs), len(copies)
    na = 0 if after is None else 1
    flips = sorted({flip for _, _, flip in copies})
    kind = "c" if flips == ["c"] else "ici"
    collective_id = BARRIER_IDS[kind][turns[kind] % 2]
    turns[kind] += 1

    def body(*refs):
        x, y, c = lax.axis_index("x"), lax.axis_index("y"), lax.axis_index("c")
        barrier = pltpu.get_barrier_semaphore()
        for flip in flips:
            peer = (1 - x if "x" in flip else x, 1 - y if "y" in flip else y, 1 - c if "c" in flip else c)
            pl.semaphore_signal(barrier, inc=1, device_id=peer, device_id_type=MESH)
        pl.semaphore_wait(barrier, len(flips))
        for cp in _descriptors(copies, refs[:n], refs[n + na], refs[n + na + 1]):
            cp.start()
        token = refs[2 * n + na + 2]
        token[...] = jnp.zeros_like(token)

    hbm = pl.BlockSpec(memory_space=pltpu.HBM)
    sem = pl.BlockSpec(memory_space=pltpu.SEMAPHORE)
    out = pl.pallas_call(
        body, name=name,
        in_specs=[hbm] * n + [pl.BlockSpec(memory_space=pl.ANY)] * na,
        out_specs=(sem, sem, *[hbm] * n, pl.BlockSpec(memory_space=pltpu.VMEM)),
        out_shape=(pltpu.SemaphoreType.DMA((nr,)), pltpu.SemaphoreType.DMA((nr,)),
                   *[pltpu.HBM(b.shape, b.dtype) for b in bufs], jax.ShapeDtypeStruct((SUBLANES, LANES), F32)),
        input_output_aliases={i: 2 + i for i in range(n)},
        compiler_params=pltpu.CompilerParams(has_side_effects=SIDE_EFFECT, collective_id=collective_id),
    )(*[pltpu.with_memory_space_constraint(b, pltpu.HBM) for b in bufs], *([after] * na))
    return out[0], out[1], list(out[2:2 + n]), out[2 + n]


def _exchange_wait(name, send_sems, recv_sems, bufs, copies, after, sem_off=0):
    n = len(bufs)

    def body(*refs):
        for cp in _descriptors(copies, refs[:n], refs[n], refs[n + 1], sem_off):
            cp.wait_send()
            cp.wait_recv()

    hbm = pl.BlockSpec(memory_space=pltpu.HBM)
    sem = pl.BlockSpec(memory_space=pltpu.SEMAPHORE)
    out = pl.pallas_call(
        body, name=name,
        in_specs=[hbm] * n + [sem, sem, pl.BlockSpec(memory_space=pl.ANY)],
        out_specs=tuple([hbm] * n),
        out_shape=tuple(pltpu.HBM(b.shape, b.dtype) for b in bufs),
        input_output_aliases={i: i for i in range(n)},
        compiler_params=pltpu.CompilerParams(has_side_effects=SIDE_EFFECT),
    )(*bufs, send_sems, recv_sems, after)
    return list(out)


FIRST = ("w_in",)
MID = ("ssm_w_glu", "w_out")
LATE = ("w_up", "w_down")
GROUPS = {"first": FIRST, "mid": MID, "late": LATE}
ARRIVALS = {"first": FIRST, "mid": MID, "up": ("w_up",), "down": ("w_down",)}


def _gather_copies(names, shard_shapes):
    def region(i, chip, c):
        half_axis, shard_axis = BIG[names[i]]
        ssize = shard_shapes[i][shard_axis]
        hsize = shard_shapes[i][half_axis] // 2
        return lambda ref: _view(_view(ref, shard_axis, chip * ssize, ssize), half_axis, c * hsize, hsize)

    ici, d2d = [], []
    for i in range(len(names)):
        for flip in FLIPS:
            ici.append((lambda I, O, pos, i=i: region(i, pos[3], pos[2])(I[i]),
                        lambda I, O, pos, i=i: region(i, pos[3], pos[2])(O[i]), flip))
            d2d.append((lambda I, O, pos, i=i, flip=flip: region(i, _peer_chip(pos, flip), pos[2])(I[i]),
                        lambda I, O, pos, i=i, flip=flip: region(i, _peer_chip(pos, flip), pos[2])(O[i]), "c"))
    return ici, d2d


def _half_shape(n, shape):
    r, cdim = shape
    return (r // 2, cdim) if BIG[n][0] == 0 else (r, cdim // 2)


def _sub_shape(n, shape):
    hr, hc = _half_shape(n, shape)
    return (hr, hc // 4) if BIG[n][1] == 1 else (hr // 4, hc)


def _pair_copies(names, shapes, with_pack, dst_off):
    n = len(names)

    def other_half(i, ref, pos):
        half_axis = BIG[names[i]][0]
        hsize = shapes[i][half_axis] // 2
        return _view(ref, half_axis, (1 - pos[2]) * hsize, hsize)

    copies = [(lambda I, O, pos, i=i: other_half(i, I[i], pos), lambda I, O, pos, i=i: O[dst_off + i], "c")
              for i in range(n)]
    if with_pack:
        copies.append((lambda I, O, pos: I[n], lambda I, O, pos: O[dst_off + n], "c"))
    return copies


def _chip_copies(names, shapes, pack_rows, dst_off):
    n = len(names)

    def piece(i, ref, chip):
        shard_axis = BIG[names[i]][1]
        ssize = _sub_shape(names[i], shapes[i])[shard_axis]
        return _view(ref, shard_axis, chip * ssize, ssize)

    copies = []
    for i in range(n):
        for slot, flip in enumerate(FLIPS):
            copies.append((lambda I, O, pos, i=i, flip=flip: piece(i, I[i], _peer_chip(pos, flip)),
                           lambda I, O, pos, i=i, slot=slot: O[dst_off + i].at[slot], flip))
    if pack_rows:
        for slot, flip in enumerate(FLIPS):
            copies.append((lambda I, O, pos: _view(I[n], 0, pos[2] * (pack_rows // 2), pack_rows // 2),
                           lambda I, O, pos, slot=slot: O[dst_off + n].at[slot], flip))
    return copies


class _Exchanges:
    def __init__(self, shards, tiny, kc):
        self.kc = kc
        wb = {n: _cast_into_full(shards[n], kc, BIG[n][1], "cast_" + n) for n in BIG_NAMES}
        self.gathering, self.forwarding, self.pairing, self.reducing = {}, {}, {}, {}
        self.turns = {"c": 0, "ici": 0}
        tiny_copies = [(lambda I, O, pos: I[0], lambda I, O, pos: O[1].at[pos[3]], flip) for flip in FLIPS]
        self.gathering["tiny"] = (0, 0, 2, tiny_copies, None)
        bufs, copies = [tiny, lax.empty((4,) + tiny.shape, F32)], list(tiny_copies)
        for group, names in ARRIVALS.items():
            ici, d2d = _gather_copies(names, [shards[n].shape for n in names])
            self.gathering[group] = (len(bufs), len(copies), len(names), ici, d2d)
            copies += _shifted(ici, len(bufs))
            bufs += [wb[n] for n in names]
        self.started = _exchange_start("gather_start", bufs, copies, self.turns)
        self.zero = self.started[3][0, 0]

    def _arrived(self, group, after):
        buf_off, sem_off, n, ici, _ = self.gathering[group]
        send_sems, recv_sems, bufs, _ = self.started
        return _exchange_wait("gather_%s_wait" % group, send_sems, recv_sems, bufs[buf_off:buf_off + n], ici, after,
                              sem_off)

    def small_params(self, kc):
        tiny, got = self._arrived("tiny", self.started[3])
        return lax.dynamic_update_index_in_dim(got, tiny, kc[0], 0)

    def forward(self, group, after):
        d2d = self.gathering[group][4]
        self.forwarding[group] = (_exchange_start("forward_%s_start" % group, self._arrived(group, after), d2d,
                                                  self.turns), d2d)
        return self.forwarding[group][0][3]

    def weights(self, group, after):
        if group not in self.forwarding:
            after = self.forward(group, after)
        (send_sems, recv_sems, bufs, _), d2d = self.forwarding[group]
        full = _exchange_wait("forward_%s_wait" % group, send_sems, recv_sems, bufs, d2d, after)
        return dict(zip(ARRIVALS[group], full))

    def grads_ready(self, group, grads):
        names = GROUPS[group]
        gs = [grads[n] for n in names]
        land = [lax.empty(_half_shape(n, g.shape), F32) for n, g in zip(names, gs)]
        copies = _pair_copies(names, [g.shape for g in gs], False, len(names))
        started = _exchange_start("pair_%s_start" % group, gs + land, copies, self.turns)
        self.pairing[group] = (started, copies)
        return started[3]

    def grads_send(self, group, after):
        names = GROUPS[group]
        n = len(names)
        (send_sems, recv_sems, bufs, _), copies = self.pairing[group]
        bufs = _exchange_wait("pair_%s_wait" % group, send_sems, recv_sems, bufs, copies, after)
        chip = [_pair_sum(bufs[i], bufs[n + i], self.kc, BIG[names[i]][0], "pair_sum_" + names[i], BF16)
                for i in range(n)]
        shapes = [bufs[i].shape for i in range(n)]
        land = [lax.empty((3,) + _sub_shape(names[i], shapes[i]), BF16) for i in range(n)]
        copies = _chip_copies(names, shapes, 0, n)
        started = _exchange_start("reduce_%s_start" % group, chip + land, copies, self.turns)
        self.reducing[group] = (started, copies)
        return started[3]

    def finish_pack(self, pack):
        kc = self.kc
        prow = pack.shape[0] // 2
        recv = _exchange("reduce_d2d", [pack], [jax.ShapeDtypeStruct(pack.shape, F32)], {}, [],
                         _pair_copies((), [], True, 0))
        chip_pack = _pair_sum(pack, recv[0], kc, None, "pair_sum_pack", F32)
        copies = _chip_copies((), [], pack.shape[0], 1)
        land = lax.empty((3, prow, pack.shape[1]), F32)
        pack_sems_s, pack_sems_r, pack_bufs, after = _exchange_start("reduce_pack_start", [chip_pack, land], copies,
                                                                     self.turns)

        names, chips, recvs = (), [], []
        for group, group_names in GROUPS.items():
            (send_sems, recv_sems, bufs, _), group_copies = self.reducing[group]
            bufs = _exchange_wait("reduce_%s_wait" % group, send_sems, recv_sems, bufs, group_copies, after)
            n = len(group_names)
            names, chips, recvs = names + group_names, chips + bufs[:n], recvs + bufs[n:]
            after = bufs[n]
        total = [_chip_sum(chips[i], recvs[i], kc, BIG[n][1], BIG[n][0], "chip_sum_" + n)
                 for i, n in enumerate(names)]

        def my_half(half_axis, ref, pos):
            hsize = ref.shape[half_axis] // 2
            return _view(ref, half_axis, pos[2] * hsize, hsize)

        swap = [(lambda I, O, pos, i=i, n=n: my_half(BIG[n][0], I[i], pos),
                 lambda I, O, pos, i=i, n=n: my_half(BIG[n][0], O[i], pos), "c") for i, n in enumerate(names)]
        self.swapping = (_exchange_start("swap_start", total, swap, self.turns), swap, names)

        chip_pack, recv_pack = _exchange_wait("reduce_pack_wait", pack_sems_s, pack_sems_r, pack_bufs, copies,
                                              self.swapping[0][3])
        total_pack = _chip_sum(chip_pack, recv_pack, kc, None, 0, "chip_sum_pack")
        swap = [(lambda I, O, pos: my_half(0, I[0], pos), lambda I, O, pos: my_half(0, O[0], pos), "c")]
        return _exchange("swap_pack", [total_pack], [jax.ShapeDtypeStruct(pack.shape, F32)], {0: 0}, [], swap)[0]

    def finish_big(self, after):
        (send_sems, recv_sems, bufs, _), swap, names = self.swapping
        return dict(zip(names, _exchange_wait("swap_wait", send_sems, recv_sems, bufs, swap, after)))


WEIGHTS = ("meta_tokens", "norm_mix_g", "w_in", "conv_w", "ssm_lam_re", "ssm_lam_im", "ssm_log_dt", "ssm_b_re",
           "ssm_b_im", "ssm_c_re", "ssm_c_im", "ssm_d", "ssm_w_glu", "gain_conv_out", "gain_ssm_out", "w_out",
           "norm_ffn_g", "w_up", "ffn_conv_w", "ffn_conv_b", "w_down", "norm_final_g")
TINY_SHARDED = ("meta_tokens", "conv_w", "ffn_conv_w")
REPLICATED = tuple(n for n in WEIGHTS if n not in BIG and n not in TINY_SHARDED)
PACK_COLS = 512


def _pack(arrays, row_mult, cols):
    flat = jnp.concatenate([a.reshape(-1).astype(F32) for a in arrays])
    n = flat.shape[0]
    total = -(-n // (row_mult * cols)) * (row_mult * cols)
    return jnp.concatenate([flat, jnp.zeros((total - n,), F32)]).reshape(total // cols, cols)


def _unpack(packed, shapes):
    flat = packed.reshape(-1)
    out, off = [], 0
    for s in shapes:
        n = math.prod(s)
        out.append(flat[off:off + n].reshape(s))
        off += n
    return out


def kernel(x, meta_tokens, norm_mix_g, w_in, conv_w, ssm_lam_re, ssm_lam_im, ssm_log_dt, ssm_b_re, ssm_b_im, ssm_c_re, ssm_c_im, ssm_d, ssm_w_glu, gain_conv_out, gain_ssm_out, w_out, norm_ffn_g, w_up, ffn_conv_w, ffn_conv_b, w_down, norm_final_g, loss_target, m_meta_tokens, m_norm_mix_g, m_w_in, m_conv_w, m_ssm_lam_re, m_ssm_lam_im, m_ssm_log_dt, m_ssm_b_re, m_ssm_b_im, m_ssm_c_re, m_ssm_c_im, m_ssm_d, m_ssm_w_glu, m_gain_conv_out, m_gain_ssm_out, m_w_out, m_norm_ffn_g, m_w_up, m_ffn_conv_w, m_ffn_conv_b, m_w_down, m_norm_final_g, v_meta_tokens, v_norm_mix_g, v_w_in, v_conv_w, v_ssm_lam_re, v_ssm_lam_im, v_ssm_log_dt, v_ssm_b_re, v_ssm_b_im, v_ssm_c_re, v_ssm_c_im, v_ssm_d, v_ssm_w_glu, v_gain_conv_out, v_gain_ssm_out, v_w_out, v_norm_ffn_g, v_w_up, v_ffn_conv_w, v_ffn_conv_b, v_w_down, v_norm_final_g):
    args = dict(locals())
    w = {n: args[n] for n in WEIGHTS}
    mom = {n: args["m_" + n] for n in WEIGHTS}
    var = {n: args["v_" + n] for n in WEIGHTS}
    kx, ky, kc_ = lax.axis_index("x"), lax.axis_index("y"), lax.axis_index("c")
    chip = 2 * kx + ky
    kc = jnp.stack([chip, kc_]).astype(jnp.int32)

    def squeeze(n, a):
        if n == "meta_tokens":
            return a
        if n == "norm_final_g":
            return a.reshape(1, -1)
        a = a[0]
        return a.reshape(1, -1) if a.ndim == 1 else a

    wl = {n: squeeze(n, w[n]) for n in WEIGHTS}
    ml = {n: squeeze(n, mom[n]) for n in WEIGHTS}
    vl = {n: squeeze(n, var[n]) for n in WEIGHTS}

    tiny = _pack([wl[n] for n in TINY_SHARDED], SUBLANES, LANES)
    ex = _Exchanges({n: wl[n] for n in BIG_NAMES}, tiny, kc)
    tiny_shapes = [wl[n].shape for n in TINY_SHARDED]
    tiny_all = ex.small_params(kc)
    tiny_parts = [_unpack(tiny_all[k], tiny_shapes) for k in range(4)]
    p = {n: wl[n] for n in WEIGHTS if n not in BIG}
    for j, n in enumerate(TINY_SHARDED):
        p[n] = jnp.concatenate([tiny_parts[k][j] for k in range(4)], axis=1)
    p["ssm_log_dt"] = wl["ssm_log_dt"].reshape(-1)

    loss_local, grad_x, grads = _local_step(x[0], loss_target[0], p, ex)

    small_names = REPLICATED + TINY_SHARDED
    small_shapes = [tuple(grads[n].shape) for n in small_names] + [(1,)]
    pack = _pack([grads[n] for n in small_names] + [loss_local.reshape(1)], 2 * 16, PACK_COLS)
    g_pack = ex.finish_pack(pack)
    g_small = dict(zip(small_names + ("loss",), _unpack(g_pack, small_shapes)))
    loss = g_small["loss"][0]
    swapped = ("ssm_b_re", "ssm_b_im")

    def view(n, a):
        if n in swapped:
            return jnp.swapaxes(a, -1, -2)
        return a.reshape(1, -1) if a.ndim == 1 else a

    g = {}
    for n in REPLICATED:
        g[n] = g_small[n].reshape(view(n, w[n]).shape)
    for n in TINY_SHARDED:
        cols = wl[n].shape[1]
        g[n] = lax.dynamic_slice_in_dim(g_small[n], chip * cols, cols, axis=1).reshape(w[n].shape)
    delta, new_m, new_v = {}, {}, {}
    small = [[view(n, d[n]) for n in small_names] for d in (w, mom, var)]
    small.insert(1, [g[n] for n in small_names])
    for d, outs in zip((delta, new_m, new_v), _adamw_whole(*small, "adamw_small")):
        d.update(zip(small_names, outs))
    for d in (g, delta, new_m, new_v):
        d.update({n: jnp.swapaxes(d[n], -1, -2) for n in swapped})
    g_big = ex.finish_big(delta[small_names[0]])
    for n in BIG_NAMES:
        g[n], delta[n], new_m[n], new_v[n] = _adamw(wl[n], g_big[n], ml[n], vl[n], "adamw_" + n)

    def like(n, a):
        return a.reshape(w[n].shape)

    return (loss, grad_x[None], *[like(n, g[n]) for n in WEIGHTS], *[like(n, delta[n]) for n in WEIGHTS],
            *[like(n, new_m[n]) for n in WEIGHTS], *[like(n, new_v[n]) for n in WEIGHTS])
```

```python
import functools
import math

import jax
import jax.numpy as jnp
from jax import lax
from jax.experimental import pallas as pl
from jax.experimental.pallas import tpu as pltpu

F32 = jnp.float32
BF16 = jnp.bfloat16
MESH = pl.DeviceIdType.MESH

N_META = 16
N_GROUPS = 32
GROUP = 16
STATE = 64
RMS_EPS = 1e-6
ADAM_LR = 0.001
ADAM_B1 = 0.9
ADAM_B2 = 0.999
ADAM_EPS = 1e-08
ADAM_WD = 0.01
ADAM_STEP = 10

LANES = 128
SUBLANES = 8
ROW_ALIGN = 128
ROW_TILES = 4
VMEM_LIMIT = 52 * 1024 * 1024
MM_VMEM_BUDGET = 40 * 1024 * 1024
GELU_C = math.sqrt(2.0 / math.pi)
GELU_A = 0.044715


def _cparams(*sem):
    return pltpu.CompilerParams(dimension_semantics=sem, vmem_limit_bytes=VMEM_LIMIT)


def _pick_tile(dim, cap, mult):
    best = None
    for t in range(mult, min(dim, cap) + 1, mult):
        if dim % t == 0:
            best = t
    return best if best is not None else dim


def _mm(a, b, mode, name, out_dtype=F32, acc_in=None, after=None):
    if mode == "tn":
        kdim, m = a.shape
    else:
        m, kdim = a.shape
    n = b.shape[0] if mode == "nt" else b.shape[1]
    tm = _pick_tile(m, 1408, LANES if mode == "tn" else 16)
    tk = _pick_tile(kdim, 2816, LANES)
    nk = kdim // tk
    out_bytes = jnp.dtype(out_dtype).itemsize
    for cap in (704, 512, 256, LANES) if m == tm else (1408, 1024, 512, 256, LANES):
        tn = _pick_tile(n, cap, LANES)
        blocks = 2 * (tm * tk * 2 + tk * tn * 2 + tm * tn * out_bytes * (2 if acc_in is not None else 1))
        if blocks + (tm * tn * 4 if nk > 1 else 0) <= MM_VMEM_BUDGET:
            break
    has_acc = acc_in is not None

    def body(*refs):
        if after is not None:
            refs = refs[1:]
        if has_acc:
            a_ref, b_ref, c_ref, o_ref = refs[:4]
            rest = refs[4:]
        else:
            a_ref, b_ref, o_ref = refs[:3]
            c_ref = None
            rest = refs[3:]
        if mode == "nn":
            p = jnp.dot(a_ref[...], b_ref[...], preferred_element_type=F32)
        elif mode == "nt":
            p = lax.dot_general(a_ref[...], b_ref[...], (((1,), (1,)), ((), ())), preferred_element_type=F32)
        else:
            p = lax.dot_general(a_ref[...], b_ref[...], (((0,), (0,)), ((), ())), preferred_element_type=F32)
        if nk == 1:
            if has_acc:
                p = p + c_ref[...]
            o_ref[...] = p.astype(out_dtype)
        else:
            acc_ref = rest[0]
            k = pl.program_id(2)

            @pl.when(k == 0)
            def _():
                acc_ref[...] = p + c_ref[...] if has_acc else p

            @pl.when(k > 0)
            def _():
                acc_ref[...] += p

            @pl.when(k == nk - 1)
            def _():
                o_ref[...] = acc_ref[...].astype(out_dtype)

    if mode == "tn":
        a_spec = pl.BlockSpec((tk, tm), lambda i, j, k: (k, i))
    else:
        a_spec = pl.BlockSpec((tm, tk), lambda i, j, k: (i, k))
    if mode == "nt":
        b_spec = pl.BlockSpec((tn, tk), lambda i, j, k: (j, k))
    else:
        b_spec = pl.BlockSpec((tk, tn), lambda i, j, k: (k, j))
    o_spec = pl.BlockSpec((tm, tn), lambda i, j, k: (i, j))
    in_specs = [a_spec, b_spec] + ([o_spec] if has_acc else [])
    args = (a, b) + ((acc_in,) if has_acc else ())
    if after is not None:
        in_specs = [pl.BlockSpec(memory_space=pl.ANY)] + in_specs
        args = (after,) + args
    return pl.pallas_call(
        body, name=name, grid=(m // tm, n // tn, nk),
        in_specs=in_specs, out_specs=o_spec,
        out_shape=jax.ShapeDtypeStruct((m, n), out_dtype),
        scratch_shapes=[pltpu.VMEM((tm, tn), F32)] if nk > 1 else [],
        compiler_params=_cparams("parallel", "parallel", "arbitrary"),
    )(*args)


def _mm_rows(a, b, mode, name, ins, outs, epilogue, scratch=()):
    m, kdim = a.shape
    n = b.shape[0] if mode == "nt" else b.shape[1]
    tm = m // ROW_TILES
    tk = _pick_tile(kdim, 2816, LANES)
    nk = kdim // tk
    ni, no = len(ins), len(outs)

    def body(*refs):
        a_ref, b_ref = refs[:2]
        in_refs, out_refs, rest = refs[2:2 + ni], refs[2 + ni:2 + ni + no], refs[2 + ni + no:]
        k, i = pl.program_id(0), pl.program_id(1)
        if mode == "nn":
            p = jnp.dot(a_ref[...], b_ref[...], preferred_element_type=F32)
        else:
            p = lax.dot_general(a_ref[...], b_ref[...], (((1,), (1,)), ((), ())), preferred_element_type=F32)
        if nk == 1:
            epilogue(p, i, in_refs, out_refs, rest)
        else:
            acc_ref = rest[0]
            rows = pl.ds(pl.multiple_of(i * tm, SUBLANES), tm)

            @pl.when(k == 0)
            def _():
                acc_ref[rows, :] = p

            @pl.when(jnp.logical_and(k > 0, k < nk - 1))
            def _():
                acc_ref[rows, :] += p

            @pl.when(k == nk - 1)
            def _():
                epilogue(acc_ref[rows, :] + p, i, in_refs, out_refs, rest[1:])

    tile = (lambda k, i: i) if nk == 1 else (lambda k, i: jnp.where(k == nk - 1, i, 0))

    def spec(shape, kind):
        if kind == "rows":
            return pl.BlockSpec((tm,) + tuple(shape[1:]), lambda k, i: (tile(k, i),) + (0,) * (len(shape) - 1))
        if kind == "whole":
            return pl.BlockSpec(tuple(shape), lambda k, i: (0,) * len(shape))
        return pl.BlockSpec(memory_space=pl.ANY)

    a_spec = pl.BlockSpec((tm, tk), lambda k, i: (i, k))
    b_spec = pl.BlockSpec((n, tk), lambda k, i: (0, k)) if mode == "nt" else pl.BlockSpec((tk, n), lambda k, i: (k, 0))
    return pl.pallas_call(
        body, name=name, grid=(nk, ROW_TILES),
        in_specs=[a_spec, b_spec] + [spec(x.shape, kind) for x, kind in ins],
        out_specs=[spec(shape, kind) for shape, _, kind in outs],
        out_shape=[jax.ShapeDtypeStruct(shape, dtype) for shape, dtype, _ in outs],
        scratch_shapes=([pltpu.VMEM((m, n), F32)] if nk > 1 else []) + list(scratch),
        compiler_params=_cparams("arbitrary", "arbitrary"),
    )(a, b, *[x for x, _ in ins])


def _rows(shape_cols, tr, dtype=None):
    return pl.BlockSpec((tr, shape_cols), lambda i: (i, 0))


def _const(shape):
    return pl.BlockSpec(shape, lambda i: (0,) * len(shape))


def _rms(x):
    return lax.rsqrt(jnp.mean(x * x, axis=-1, keepdims=True) + RMS_EPS)


def _rms_bwd(x, r, g, dy):
    xn = x * r
    dxn = dy * g
    dx = r * (dxn - xn * jnp.mean(dxn * xn, axis=-1, keepdims=True))
    return dx, dy * xn


def _gelu(y):
    return 0.5 * y * (1.0 + jnp.tanh(GELU_C * (y + GELU_A * y * y * y)))


def _gelu_grad(y):
    t = jnp.tanh(GELU_C * (y + GELU_A * y * y * y))
    return 0.5 * (1.0 + t) + 0.5 * y * (1.0 - t * t) * GELU_C * (1.0 + 3.0 * GELU_A * y * y)


def _sigmoid(z):
    return 1.0 / (1.0 + jnp.exp(-z))


def _proj_res_norm(a, w, h, g, after, name):
    def epilogue(p, i, ins, outs, _):
        x = ins[0][...] + p
        outs[0][...] = x
        outs[1][...] = (x * _rms(x) * ins[1][...]).astype(BF16)

    return _mm_rows(a, w, "nn", name, [(h, "rows"), (g, "whole"), (after, "hbm")],
                    [(h.shape, F32, "rows"), (h.shape, BF16, "rows")], epilogue)


def _proj_norm_bwd(da, w, h, g, dres, after, name):
    d = h.shape[1]

    def epilogue(p, i, ins, outs, _):
        x = ins[0][...]
        dx, dgs = _rms_bwd(x, _rms(x), ins[1][...], p)
        dh = ins[2][...] + dx
        outs[0][...] = dh
        outs[1][...] = dh.astype(BF16)

        @pl.when(i == 0)
        def _():
            outs[2][...] = jnp.zeros_like(outs[2])

        outs[2][...] += jnp.sum(dgs, axis=0, keepdims=True)

    return _mm_rows(da, w, "nt", name, [(h, "rows"), (g, "whole"), (dres, "rows"), (after, "hbm")],
                    [(h.shape, F32, "rows"), (h.shape, BF16, "rows"), ((1, d), F32, "whole")], epilogue)


def _proj_input_norm_bwd(da, w, h, g, dres, after, n_real, name):
    tp, d = h.shape
    tr = tp // ROW_TILES

    def epilogue(p, i, ins, outs, scratch):
        h_ref, g_ref, dres_ref, _ = ins
        dx_ref, dmeta_ref, dg_ref = outs
        stage, sem = scratch
        x = h_ref[...]
        dx, dgs = _rms_bwd(x, _rms(x), g_ref[...], p)
        stage[...] = dres_ref[...] + dx

        @pl.when(i == 0)
        def _():
            dg_ref[...] = jnp.zeros_like(dg_ref)
            dmeta_ref[...] = stage[:N_META, :]

        dg_ref[...] += jnp.sum(dgs, axis=0, keepdims=True)
        for t in range(ROW_TILES):
            lo, hi = max(t * tr, N_META), min((t + 1) * tr, n_real)
            if hi > lo:
                @pl.when(i == t)
                def _(t=t, lo=lo, hi=hi):
                    cp = pltpu.make_async_copy(stage.at[pl.ds(lo - t * tr, hi - lo), :],
                                               dx_ref.at[pl.ds(lo - N_META, hi - lo), :], sem)
                    cp.start()
                    cp.wait()

    return _mm_rows(da, w, "nt", name, [(h, "rows"), (g, "whole"), (dres, "rows"), (after, "hbm")],
                    [((n_real - N_META, d), F32, "hbm"), ((N_META, d), F32, "whole"), ((1, d), F32, "whole")],
                    epilogue, scratch=[pltpu.VMEM((tr, d), F32), pltpu.SemaphoreType.DMA])


def _load_token_rows(tok_hbm, buf, sem, tr, n_real, head=None, wait=False, i=None):
    i = pl.program_id(0) if i is None else i
    for t in range(ROW_TILES):
        base = t * tr
        lo, hi = max(base, N_META), min(base + tr, n_real)

        @pl.when(i == t)
        def _(base=base, lo=lo, hi=hi):
            if hi > lo:
                cp = pltpu.make_async_copy(tok_hbm.at[pl.ds(lo - N_META, hi - lo), :],
                                           buf.at[pl.ds(lo - base, hi - lo), :], sem)
                if wait:
                    cp.wait()
                    return
                cp.start()
            if wait:
                return
            if base < N_META:
                buf[0:N_META - base, :] = (jnp.zeros((N_META - base, buf.shape[1]), F32) if head is None
                                           else head[base:N_META, :])
            if hi < base + tr:
                buf[max(hi, base) - base:tr, :] = jnp.zeros((base + tr - max(hi, base), buf.shape[1]), F32)


def _input_norm_fwd(x, meta, g, tp, name):
    seq, d = x.shape
    tr = tp // ROW_TILES
    n_real = N_META + seq

    def body(x_hbm, meta_ref, g_ref, h_ref, hn_ref, buf, sem):
        _load_token_rows(x_hbm, buf, sem, tr, n_real, head=meta_ref)
        _load_token_rows(x_hbm, buf, sem, tr, n_real, wait=True)
        h = buf[...]
        h_ref[...] = h
        hn_ref[...] = (h * _rms(h) * g_ref[...]).astype(BF16)

    return pl.pallas_call(
        body, name=name, grid=(ROW_TILES,),
        in_specs=[pl.BlockSpec(memory_space=pl.ANY), _const((N_META, d)), _const((1, d))],
        out_specs=[_rows(d, tr), _rows(d, tr)],
        out_shape=[jax.ShapeDtypeStruct((tp, d), F32), jax.ShapeDtypeStruct((tp, d), BF16)],
        scratch_shapes=[pltpu.VMEM((tr, d), F32), pltpu.SemaphoreType.DMA],
        compiler_params=_cparams("arbitrary"))(x, meta, g)


def _proj_loss_bwd(act, w, h1, target, g, n_real, name):
    tp, d = h1.shape
    tr = tp // ROW_TILES

    def epilogue(p, i, ins, outs, scratch):
        h1_ref, t_hbm, g_ref = ins
        loss_ref, dh_ref, dhb_ref, dg_ref = outs
        t_buf, sem = scratch
        _load_token_rows(t_hbm, t_buf, sem, tr, n_real, i=i)
        x = h1_ref[...] + p
        r = _rms(x)
        row = i * tr + lax.broadcasted_iota(jnp.int32, (tr, d), 0)
        valid = (row >= N_META) & (row < n_real)
        _load_token_rows(t_hbm, t_buf, sem, tr, n_real, wait=True, i=i)
        e = jnp.where(valid, x * r * g_ref[...] - t_buf[...], 0.0)
        dx, dgs = _rms_bwd(x, r, g_ref[...], e * (1.0 / d))
        dh_ref[...] = dx
        dhb_ref[...] = dx.astype(BF16)

        @pl.when(i == 0)
        def _():
            dg_ref[...] = jnp.zeros_like(dg_ref)
            loss_ref[...] = jnp.zeros_like(loss_ref)

        dg_ref[...] += jnp.sum(dgs, axis=0, keepdims=True)
        loss_ref[...] += (0.5 / d) * jnp.sum(jnp.sum(e * e, axis=0, keepdims=True), axis=1, keepdims=True)

    return _mm_rows(act, w, "nn", name, [(h1, "rows"), (target, "hbm"), (g, "whole")],
                    [((1, LANES), F32, "whole"), ((tp, d), F32, "rows"), ((tp, d), BF16, "rows"),
                     ((1, d), F32, "whole")],
                    epilogue, scratch=[pltpu.VMEM((tr, d), F32), pltpu.SemaphoreType.DMA])


def _mix_fwd(co, y, z, gc, gs, name):
    tp, dh = co.shape
    tr = tp // ROW_TILES

    def body(co_ref, y_ref, z_ref, gc_ref, gs_ref, m_ref):
        c = co_ref[...]
        m_ref[:, :dh] = (c * _rms(c) * gc_ref[...]).astype(BF16)
        so = _gelu(y_ref[...]) * _sigmoid(z_ref[...])
        m_ref[:, dh:] = (so * _rms(so) * gs_ref[...]).astype(BF16)

    return pl.pallas_call(
        body, name=name, grid=(ROW_TILES,),
        in_specs=[_rows(dh, tr)] * 3 + [_const((1, dh))] * 2,
        out_specs=_rows(2 * dh, tr),
        out_shape=jax.ShapeDtypeStruct((tp, 2 * dh), BF16),
        compiler_params=_cparams("parallel"))(co, y, z, gc, gs)


def _proj_mix_bwd(dh1b, w, co, y, z, gc, gs, name):
    tp, dh = co.shape

    def epilogue(p, i, ins, outs, _):
        co_ref, y_ref, z_ref, gc_ref, gs_ref = ins
        dco_ref, dz_ref, dgp_ref, dgc_ref, dgs_ref = outs
        c = co_ref[...]
        dco, dgc = _rms_bwd(c, _rms(c), gc_ref[...], p[:, :dh])
        dco_ref[...] = dco
        gl = _gelu(y_ref[...])
        sg = _sigmoid(z_ref[...])
        so = gl * sg
        dso, dgs = _rms_bwd(so, _rms(so), gs_ref[...], p[:, dh:])
        dz_ref[...] = (dso * gl * sg * (1.0 - sg)).astype(BF16)
        dgp_ref[...] = dso * sg

        @pl.when(i == 0)
        def _():
            dgc_ref[...] = jnp.zeros_like(dgc_ref)
            dgs_ref[...] = jnp.zeros_like(dgs_ref)

        dgc_ref[...] += jnp.sum(dgc, axis=0, keepdims=True)
        dgs_ref[...] += jnp.sum(dgs, axis=0, keepdims=True)

    return _mm_rows(dh1b, w, "nt", name,
                    [(co, "rows"), (y, "rows"), (z, "rows"), (gc, "whole"), (gs, "whole")],
                    [((tp, dh), F32, "rows"), ((tp, dh), BF16, "rows"), ((tp, dh), F32, "rows"),
                     ((1, dh), F32, "whole"), ((1, dh), F32, "whole")], epilogue)


def _shift_down(x, k):
    row = lax.broadcasted_iota(jnp.int32, x.shape, 0)
    return jnp.where(row >= k, pltpu.roll(x, k, 0), 0.0)


def _shift_up(x, k):
    n = x.shape[0]
    row = lax.broadcasted_iota(jnp.int32, x.shape, 0)
    return jnp.where(row < n - k, pltpu.roll(x, n - k, 0), 0.0)


def _dwconv(x, w_ref):
    return w_ref[2:3, :] * x + w_ref[1:2, :] * _shift_down(x, 1) + w_ref[0:1, :] * _shift_down(x, 2)


def _dwconv_bwd(x, dy, w_ref):
    dx = w_ref[2:3, :] * dy + w_ref[1:2, :] * _shift_up(dy, 1) + w_ref[0:1, :] * _shift_up(dy, 2)
    dw = jnp.concatenate([jnp.sum(dy * _shift_down(x, 2), axis=0, keepdims=True),
                          jnp.sum(dy * _shift_down(x, 1), axis=0, keepdims=True),
                          jnp.sum(dy * x, axis=0, keepdims=True)], axis=0)
    return dx, dw


def _interleave(dst, src):
    seg_rows = src.shape[0] // SUBLANES
    for seg in range(SUBLANES):
        dst[pl.ds(seg, seg_rows, stride=SUBLANES), :] = src[seg * seg_rows:(seg + 1) * seg_rows, :]


def _deinterleave(dst, src):
    seg_rows = src.shape[0] // SUBLANES
    for seg in range(SUBLANES):
        dst[seg * seg_rows:(seg + 1) * seg_rows, :] = src[pl.ds(seg, seg_rows, stride=SUBLANES), :]


def _segment_shift(x, reverse):
    row = lax.broadcasted_iota(jnp.int32, x.shape, 0)
    if reverse:
        return jnp.where(row < SUBLANES - 1, pltpu.roll(x, SUBLANES - 1, 0), 0.0)
    return jnp.where(row >= 1, pltpu.roll(x, 1, 0), 0.0)


def _scan(s_re, s_im, pw_ref, reverse, pair=None):
    n_steps = s_re.shape[0] // SUBLANES
    n_strips = s_re.shape[1] // LANES
    sign = -1.0 if reverse else 1.0
    strips = [slice(st * LANES, (st + 1) * LANES) for st in range(n_strips)]

    def rows_of(j):
        step = (n_steps - 1 - j) if reverse else j
        return pl.ds(pl.multiple_of(step * SUBLANES, SUBLANES), SUBLANES)

    a = [(jnp.broadcast_to(pw_ref[0, 0:1, lanes], (SUBLANES, LANES)),
          sign * jnp.broadcast_to(pw_ref[1, 0:1, lanes], (SUBLANES, LANES))) for lanes in strips]

    def local(i, carry):
        for half in range(2):
            rows = rows_of(2 * i + half)
            out = []
            for st, lanes in enumerate(strips):
                (ar, ai), cr, ci = a[st], carry[2 * st], carry[2 * st + 1]
                xr = s_re[rows, lanes] + (ar * cr - ai * ci)
                xi = s_im[rows, lanes] + (ar * ci + ai * cr)
                s_re[rows, lanes] = xr
                s_im[rows, lanes] = xi
                out += [xr, xi]
            carry = tuple(out)
        return carry

    zero = jnp.zeros((SUBLANES, LANES), F32)
    ends = lax.fori_loop(0, n_steps // 2, local, (zero,) * (2 * n_strips))

    entering = []
    row = lax.broadcasted_iota(jnp.int32, (SUBLANES, LANES), 0)
    for st, lanes in enumerate(strips):
        tr, ti = ends[2 * st], ends[2 * st + 1]
        mr = jnp.broadcast_to(pw_ref[0, n_steps - 1:n_steps, lanes], (SUBLANES, LANES))
        mi = sign * jnp.broadcast_to(pw_ref[1, n_steps - 1:n_steps, lanes], (SUBLANES, LANES))
        for k in (1, 2, 4):
            keep = (row < SUBLANES - k) if reverse else (row >= k)
            rr = jnp.where(keep, pltpu.roll(tr, SUBLANES - k if reverse else k, 0), 0.0)
            ri = jnp.where(keep, pltpu.roll(ti, SUBLANES - k if reverse else k, 0), 0.0)
            tr, ti = tr + (mr * rr - mi * ri), ti + (mr * ri + mi * rr)
            mr, mi = mr * mr - mi * mi, 2.0 * mr * mi
        entering += [_segment_shift(tr, reverse), _segment_shift(ti, reverse)]

    def fix(i, carry):
        carry, sums = carry[:2 * n_strips], carry[2 * n_strips:]
        for half in range(2):
            j = 2 * i + half
            rows = rows_of(j)
            out, acc = [], []
            for st, lanes in enumerate(strips):
                (ar, ai), cr, ci = a[st], carry[2 * st], carry[2 * st + 1]
                cr, ci = ar * cr - ai * ci, ar * ci + ai * cr
                xr = s_re[rows, lanes] + cr
                xi = s_im[rows, lanes] + ci
                s_re[rows, lanes] = xr
                s_im[rows, lanes] = xi
                out += [cr, ci]
                if pair is not None:
                    p_rows = rows_of(jnp.minimum(j + 1, n_steps - 1))
                    keep = (j < n_steps - 1).astype(F32)
                    pr = pair[0][p_rows, lanes] * keep
                    pi = pair[1][p_rows, lanes] * keep
                    acc += [sums[2 * st] + (xr * pr + xi * pi), sums[2 * st + 1] + (xi * pr - xr * pi)]
            carry, sums = tuple(out), tuple(acc)
        return carry + sums

    n_sums = 0 if pair is None else 2 * n_strips
    out = lax.fori_loop(0, n_steps // 2, fix, tuple(entering) + (zero,) * n_sums)
    return out[2 * n_strips:]


def _seq_fwd(proj, conv_w, bc_re, bc_im, cc_re, cc_im, dskip, a_pow, name):
    tp = proj.shape[0]
    dh = proj.shape[1] // 4
    nq = dh // LANES
    sw = STATE * N_GROUPS // nq

    def body(b_ref, c_ref, v_ref, u_ref, w_ref, bre_ref, bim_ref, cre_ref, cim_ref, d_ref, pw_ref,
             co_ref, y_ref, g_ref, s_re, s_im, u_il, y_il):
        co_ref[...] = b_ref[...] * _dwconv(c_ref[...] * v_ref[...], w_ref)
        _interleave(u_il, u_ref)
        ub = u_il[...].astype(BF16)
        s_re[...] = jnp.dot(ub, bre_ref[...], preferred_element_type=F32)
        s_im[...] = jnp.dot(ub, bim_ref[...], preferred_element_type=F32)
        _scan(s_re, s_im, pw_ref, False)
        y_il[...] = (jnp.dot(s_re[...].astype(BF16), cre_ref[...], preferred_element_type=F32)
                     - jnp.dot(s_im[...].astype(BF16), cim_ref[...], preferred_element_type=F32))
        _deinterleave(y_ref, y_il)
        y = y_ref[...] + d_ref[...] * u_ref[...]
        y_ref[...] = y
        g_ref[...] = _gelu(y).astype(BF16)

    col = lambda off: pl.BlockSpec((tp, LANES), lambda q, off=off: (0, off * nq + q))
    blk = pl.BlockSpec((tp, LANES), lambda q: (0, q))
    return pl.pallas_call(
        body, name=name, grid=(nq,),
        in_specs=[col(0), col(1), col(2), col(3),
                  pl.BlockSpec((3, LANES), lambda q: (0, q)),
                  pl.BlockSpec((LANES, sw), lambda q: (0, q)), pl.BlockSpec((LANES, sw), lambda q: (0, q)),
                  pl.BlockSpec((sw, LANES), lambda q: (q, 0)), pl.BlockSpec((sw, LANES), lambda q: (q, 0)),
                  pl.BlockSpec((1, LANES), lambda q: (0, q)),
                  pl.BlockSpec((2, tp // SUBLANES, sw), lambda q: (0, 0, q))],
        out_specs=[blk, blk, blk, pl.BlockSpec((tp, sw), lambda q: (0, q)), pl.BlockSpec((tp, sw), lambda q: (0, q))],
        out_shape=[jax.ShapeDtypeStruct((tp, dh), F32), jax.ShapeDtypeStruct((tp, dh), F32),
                   jax.ShapeDtypeStruct((tp, dh), BF16),
                   jax.ShapeDtypeStruct((tp, nq * sw), F32), jax.ShapeDtypeStruct((tp, nq * sw), F32)],
        scratch_shapes=[pltpu.VMEM((tp, LANES), F32), pltpu.VMEM((tp, LANES), F32)],
        compiler_params=_cparams("parallel"),
    )(proj, proj, proj, proj, conv_w, bc_re, bc_im, cc_re, cc_im, dskip, a_pow)


def _conv_bwd(proj, dco, conv_w, name):
    tp = proj.shape[0]
    dh = proj.shape[1] // 4
    nq = dh // LANES

    def body(b_ref, c_ref, v_ref, dco_ref, w_ref, dproj_ref, dw_ref, stage, sem):
        q = pl.program_id(0)
        cg = c_ref[...]
        vg = v_ref[...]
        cv = cg * vg
        dco_v = dco_ref[...]
        dcv, dw = _dwconv_bwd(cv, dco_v * b_ref[...], w_ref)
        dw_ref[...] = dw
        stage[0] = (dco_v * _dwconv(cv, w_ref)).astype(BF16)
        stage[1] = (dcv * vg).astype(BF16)
        stage[2] = (dcv * cg).astype(BF16)
        copies = [pltpu.make_async_copy(stage.at[p], dproj_ref.at[:, pl.ds((p * nq + q) * LANES, LANES)], sem.at[p])
                  for p in range(3)]
        for cp in copies:
            cp.start()
        for cp in copies:
            cp.wait()

    col = lambda off: pl.BlockSpec((tp, LANES), lambda q, off=off: (0, off * nq + q))
    return pl.pallas_call(
        body, name=name, grid=(nq,),
        in_specs=[col(0), col(1), col(2), pl.BlockSpec((tp, LANES), lambda q: (0, q)),
                  pl.BlockSpec((3, LANES), lambda q: (0, q))],
        out_specs=[pl.BlockSpec(memory_space=pl.ANY), pl.BlockSpec((3, LANES), lambda q: (0, q))],
        out_shape=[jax.ShapeDtypeStruct((tp, 4 * dh), BF16), jax.ShapeDtypeStruct((3, dh), F32)],
        scratch_shapes=[pltpu.VMEM((3, tp, LANES), BF16), pltpu.SemaphoreType.DMA((3,))],
        compiler_params=_cparams("arbitrary"),
    )(proj, proj, proj, dco, conv_w)


def _ssm_bwd(proj, y, dg, dproj, states, bc_re, bc_im, cc_re, cc_im, dskip, a_pow, name):
    tp = proj.shape[0]
    dh = proj.shape[1] // 4
    nq = dh // LANES
    sw = STATE * N_GROUPS // nq

    def body(u_ref, y_ref, dg_ref, dproj_in, s_re, s_im, bre_ref, bim_ref, cre_ref, cim_ref, d_ref, pw_ref,
             dproj_ref, dbre_ref, dbim_ref, dcre_ref, dcim_ref, dd_ref, dar_ref, dai_ref,
             l_re, l_im, a_il, b_il, stage, sem):
        del dproj_in
        q = pl.program_id(0)
        nt = (((1,), (1,)), ((), ()))
        tn = (((0,), (0,)), ((), ()))
        _interleave(a_il, u_ref)
        ub = a_il[...].astype(BF16)
        dy_rows = dg_ref[...] * _gelu_grad(y_ref[...])
        dd_ref[...] = jnp.sum(dy_rows * u_ref[...], axis=0, keepdims=True)
        _interleave(b_il, dy_rows)
        dy = b_il[...]
        dyb = dy.astype(BF16)
        l_re[...] = lax.dot_general(dyb, cre_ref[...], nt, preferred_element_type=F32)
        l_im[...] = -lax.dot_general(dyb, cim_ref[...], nt, preferred_element_type=F32)
        dcre_ref[...] = lax.dot_general(s_re[...].astype(BF16), dyb, tn, preferred_element_type=F32)
        dcim_ref[...] = -lax.dot_general(s_im[...].astype(BF16), dyb, tn, preferred_element_type=F32)
        sums = _scan(l_re, l_im, pw_ref, True, pair=(s_re, s_im))
        rest = tp - SUBLANES
        for st in range(sw // LANES):
            lanes = slice(st * LANES, (st + 1) * LANES)
            lr0, li0 = l_re[:SUBLANES, lanes], l_im[:SUBLANES, lanes]
            pr0, pi0 = _segment_shift(s_re[rest:, lanes], False), _segment_shift(s_im[rest:, lanes], False)
            dar_ref[:, lanes] = jnp.sum(sums[2 * st] + (lr0 * pr0 + li0 * pi0), axis=0, keepdims=True)
            dai_ref[:, lanes] = jnp.sum(sums[2 * st + 1] + (li0 * pr0 - lr0 * pi0), axis=0, keepdims=True)
        lrb = l_re[...].astype(BF16)
        lib = l_im[...].astype(BF16)
        a_il[...] = (dy * d_ref[...] + lax.dot_general(lrb, bre_ref[...], nt, preferred_element_type=F32)
                     + lax.dot_general(lib, bim_ref[...], nt, preferred_element_type=F32))
        _deinterleave(b_il, a_il)
        stage[...] = b_il[...].astype(BF16)
        dbre_ref[...] = lax.dot_general(ub, lrb, tn, preferred_element_type=F32)
        dbim_ref[...] = lax.dot_general(ub, lib, tn, preferred_element_type=F32)
        cp = pltpu.make_async_copy(stage, dproj_ref.at[:, pl.ds((3 * nq + q) * LANES, LANES)], sem)
        cp.start()
        cp.wait()

    blk = pl.BlockSpec((tp, LANES), lambda q: (0, q))
    bspec = pl.BlockSpec((LANES, sw), lambda q: (0, q))
    cspec = pl.BlockSpec((sw, LANES), lambda q: (q, 0))
    tspec = pl.BlockSpec((2, tp // SUBLANES, sw), lambda q: (0, 0, q))
    nstate = STATE * N_GROUPS
    return pl.pallas_call(
        body, name=name, grid=(nq,),
        in_specs=[pl.BlockSpec((tp, LANES), lambda q: (0, 3 * nq + q)), blk, blk, pl.BlockSpec(memory_space=pl.ANY),
                  pl.BlockSpec((tp, sw), lambda q: (0, q)), pl.BlockSpec((tp, sw), lambda q: (0, q)),
                  bspec, bspec, cspec, cspec, pl.BlockSpec((1, LANES), lambda q: (0, q)), tspec],
        out_specs=[pl.BlockSpec(memory_space=pl.ANY), bspec, bspec, cspec, cspec,
                   pl.BlockSpec((1, LANES), lambda q: (0, q)),
                   pl.BlockSpec((1, sw), lambda q: (0, q)), pl.BlockSpec((1, sw), lambda q: (0, q))],
        out_shape=[jax.ShapeDtypeStruct((tp, 4 * dh), BF16),
                   jax.ShapeDtypeStruct((LANES, nstate), F32), jax.ShapeDtypeStruct((LANES, nstate), F32),
                   jax.ShapeDtypeStruct((nstate, LANES), F32), jax.ShapeDtypeStruct((nstate, LANES), F32),
                   jax.ShapeDtypeStruct((1, dh), F32),
                   jax.ShapeDtypeStruct((1, nstate), F32), jax.ShapeDtypeStruct((1, nstate), F32)],
        input_output_aliases={3: 0},
        scratch_shapes=[pltpu.VMEM((tp, sw), F32)] * 2 + [pltpu.VMEM((tp, LANES), F32)] * 2
        + [pltpu.VMEM((tp, LANES), BF16), pltpu.SemaphoreType.DMA],
        compiler_params=_cparams("arbitrary"),
    )(proj, y, dg, dproj, states[0], states[1], bc_re, bc_im, cc_re, cc_im, dskip, a_pow)


FFN_TILE = 256
FFN_ROWS = 32


def _window(x_ref, before, r0, rows, cols):
    if r0 == 0:
        return jnp.concatenate([before, x_ref[0:rows, cols]], axis=0)
    return x_ref[r0 - SUBLANES:r0 + rows, cols]


def _taps(window):
    return window[SUBLANES:], pltpu.roll(window, 1, 0)[SUBLANES:], pltpu.roll(window, 2, 0)[SUBLANES:]


def _conv_taps(taps, w):
    return w[2] * taps[0] + w[1] * taps[1] + w[0] * taps[2]


FFN_MM_ROWS = 544
FFN_MM_COLS = 1408


def _ffn_up_act(hn, w_up, fw, fb, col, others, name):
    tp, dm = hn.shape
    dff = w_up.shape[1] // 2
    tr, cw, rows = FFN_MM_ROWS, FFN_MM_COLS, FFN_ROWS
    nc = dff // cw
    n_others = 0 if others is None else 2

    def body(hn_ref, ma_ref, mv_ref, wa_ref, wv_ref, ba_ref, bv_ref, *rest):
        up_ref, act_ref, tail_ref = rest[n_others:]

        @pl.when(pl.program_id(0) == 0)
        def _():
            tail_ref[...] = jnp.zeros_like(tail_ref)

        x = hn_ref[...]
        up_ref[0] = jnp.dot(x, ma_ref[...], preferred_element_type=F32)
        up_ref[1] = jnp.dot(x, mv_ref[...], preferred_element_type=F32)
        for c0 in range(0, cw, FFN_TILE):
            cols = slice(c0, min(c0 + FFN_TILE, cw))
            wa, wv = [[w_ref[k:k + 1, cols] for k in range(3)] for w_ref in (wa_ref, wv_ref)]
            ba, bv = ba_ref[:, cols], bv_ref[:, cols]
            before_a, before_v = tail_ref[0, :, cols], tail_ref[1, :, cols]
            for r0 in range(0, tr, rows):
                a = _conv_taps(_taps(_window(up_ref.at[0], before_a, r0, rows, cols)), wa) + ba
                v = _conv_taps(_taps(_window(up_ref.at[1], before_v, r0, rows, cols)), wv) + bv
                act_ref[r0:r0 + rows, cols] = (a * _sigmoid(a) * v).astype(BF16)
            tail_ref[:, :, cols] = up_ref[:, tr - SUBLANES:tr, cols]

    par = lambda r, half: pl.BlockSpec((r, cw), lambda i: (0, half * nc + col))
    return pl.pallas_call(
        body, name=name, grid=(tp // tr,),
        in_specs=[pl.BlockSpec((tr, dm), lambda i: (i, 0)), par(dm, 0), par(dm, 1),
                  par(3, 0), par(3, 1), par(1, 0), par(1, 1)] + [pl.BlockSpec(memory_space=pl.ANY)] * n_others,
        out_specs=[pl.BlockSpec((2, tr, cw), lambda i: (0, i, col)), pl.BlockSpec((tr, cw), lambda i: (i, col))],
        out_shape=[jax.ShapeDtypeStruct((2, tp, dff), F32), jax.ShapeDtypeStruct((tp, dff), BF16)],
        input_output_aliases={7: 0, 8: 1} if others is not None else {},
        scratch_shapes=[pltpu.VMEM((2, SUBLANES, cw), F32)],
        compiler_params=_cparams("arbitrary"))(hn, w_up, w_up, fw, fw, fb, fb, *(others or ()))


def _ffn_bwd(up, dh, w_down, fw, fb, name):
    _, tp, dff = up.shape
    two_ff = 2 * dff
    dm = dh.shape[1]
    tr, cw, rows = FFN_MM_ROWS, FFN_MM_COLS, FFN_ROWS
    nr, nc = tp // tr, dff // cw
    n_e = rows + SUBLANES
    pieces = tr // SUBLANES

    def body(ua_ref, uv_ref, pa_ref, pv_ref, dh_ref, wd_ref, wa_ref, wv_ref, ba_ref, bv_ref,
             dup_ref, dwa_ref, dwv_ref, dba_ref, dbv_ref, dact, stage, head_ref, sem):
        j, i = pl.program_id(0), pl.program_id(1)
        step = j * nr + i
        top = i == nr - 1
        sums = ((dwa_ref, dba_ref), (dwv_ref, dbv_ref))

        slot = step % 2

        def out_copies(at):
            r0 = pl.multiple_of((nr - 1 - at % nr) * tr, tr)
            return [pltpu.make_async_copy(
                stage.at[at % 2, s],
                dup_ref.at[pl.ds(r0, tr), pl.ds(pl.multiple_of(s * dff + at // nr * cw, LANES), cw)],
                sem.at[at % 2, s]) for s in range(2)]

        @pl.when(i == 0)
        def _():
            head_ref[...] = jnp.zeros_like(head_ref)
            for dw_ref, db_ref in sums:
                dw_ref[...] = jnp.zeros_like(dw_ref)
                db_ref[...] = jnp.zeros_like(db_ref)

        dact[...] = lax.dot_general(dh_ref[...], wd_ref[...], (((1,), (1,)), ((), ())), preferred_element_type=F32)

        @pl.when(step > 1)
        def _():
            for cp in out_copies(step - 2):
                cp.wait()

        def gate_bwd(taps, dact_v, w, bias):
            a, v = [_conv_taps(taps[s], w[s]) + bias[s] for s in range(2)]
            sg = _sigmoid(a)
            return [dact_v * v * sg * (1.0 + a * (1.0 - sg)), dact_v * a * sg]

        fold = lambda x: sum(x[r:r + SUBLANES] for r in range(0, rows, SUBLANES))
        for c0 in range(0, cw, FFN_TILE):
            cols = slice(c0, min(c0 + FFN_TILE, cw))
            w = [[w_ref[k:k + 1, cols] for k in range(3)] for w_ref in (wa_ref, wv_ref)]
            bias = [ba_ref[:, cols], bv_ref[:, cols]]
            before = [jnp.where(top, 0.0, p_ref[:, cols]) for p_ref in (pa_ref, pv_ref)]
            head = [head_ref[s, :, cols] for s in range(2)]
            piece = jnp.zeros_like(head[0])
            acc = [[piece] * 4 for _ in range(2)]
            for r0 in reversed(range(0, tr, rows)):
                taps = [_taps(_window(x_ref, before[s], r0, rows, cols)) for s, x_ref in enumerate((ua_ref, uv_ref))]
                d = gate_bwd(taps, dact[r0:r0 + rows, cols], w, bias)
                for s in range(2):
                    de = jnp.concatenate([d[s], head[s]], axis=0)
                    dx = (w[s][2] * d[s] + w[s][1] * pltpu.roll(de, n_e - 1, 0)[:rows]
                          + w[s][0] * pltpu.roll(de, n_e - 2, 0)[:rows])
                    stage[slot, s, r0:r0 + rows, cols] = dx.astype(BF16)
                    for k in range(3):
                        acc[s][k] = acc[s][k] + fold(d[s] * taps[s][2 - k])
                    acc[s][3] = acc[s][3] + fold(d[s])
                    head[s] = d[s][:SUBLANES]
            for s, (dw_ref, db_ref) in enumerate(sums):
                head_ref[s, :, cols] = head[s]
                dw_ref[:, cols] = dw_ref[:, cols] + jnp.concatenate(
                    [jnp.sum(x, axis=0, keepdims=True) for x in acc[s][:3]], axis=0)
                db_ref[:, cols] = db_ref[:, cols] + jnp.sum(acc[s][3], axis=0, keepdims=True)

        copies = out_copies(step)
        for cp in copies:
            cp.start()

        @pl.when(step == nc * nr - 1)
        def _():
            for cp in out_copies(step - 1) + copies:
                cp.wait()

    row = lambda i: nr - 1 - i
    main = lambda half: pl.BlockSpec((None, tr, cw), lambda j, i: (half, row(i), j))
    prev = lambda half: pl.BlockSpec((None, SUBLANES, cw), lambda j, i: (half, jnp.maximum(row(i) * pieces - 1, 0), j))
    par = lambda r, half: pl.BlockSpec((r, cw), lambda j, i: (0, half * nc + j))
    acc_spec = lambda r: pl.BlockSpec((r, cw), lambda j, i: (0, j))
    return pl.pallas_call(
        body, name=name, grid=(nc, nr),
        in_specs=[main(0), main(1), prev(0), prev(1),
                  pl.BlockSpec((tr, dm), lambda j, i: (row(i), 0)), pl.BlockSpec((cw, dm), lambda j, i: (j, 0)),
                  par(3, 0), par(3, 1), par(1, 0), par(1, 1)],
        out_specs=[pl.BlockSpec(memory_space=pl.ANY), acc_spec(3), acc_spec(3), acc_spec(1), acc_spec(1)],
        out_shape=[jax.ShapeDtypeStruct((tp, two_ff), BF16),
                   jax.ShapeDtypeStruct((3, dff), F32), jax.ShapeDtypeStruct((3, dff), F32),
                   jax.ShapeDtypeStruct((1, dff), F32), jax.ShapeDtypeStruct((1, dff), F32)],
        scratch_shapes=[pltpu.VMEM((tr, cw), F32), pltpu.VMEM((2, 2, tr, cw), BF16),
                        pltpu.VMEM((2, SUBLANES, cw), F32), pltpu.SemaphoreType.DMA((2, 2))],
        compiler_params=_cparams("arbitrary", "arbitrary"))(up, up, up, up, dh, w_down, fw, fw, fb, fb)


def _zoh(lr, li, ld):
    dt = jnp.exp(ld)
    mag = jnp.exp(lr * dt)
    ang = li * dt
    ar = mag * jnp.cos(ang)
    ai = mag * jnp.sin(ang)
    den = lr * lr + li * li
    nr = ar - 1.0
    fr = (nr * lr + ai * li) / den
    fi = (ai * lr - nr * li) / den
    return dt, ar, ai, den, nr, fr, fi


def _s5_prep(lr, li, ld, b_re, b_im, n_pow, name):
    nstate = lr.shape[1]

    def body(lr_ref, li_ref, ld_ref, bre_ref, bim_ref, pw_ref, bcre_ref, bcim_ref):
        _, ar, ai, _, _, fr, fi = _zoh(lr_ref[...], li_ref[...], ld_ref[...])
        bre = bre_ref[...]
        bim = bim_ref[...]
        bcre_ref[...] = (fr * bre - fi * bim).astype(BF16)
        bcim_ref[...] = (fr * bim + fi * bre).astype(BF16)
        row = lax.broadcasted_iota(jnp.int32, (SUBLANES, nstate), 0)
        pr, pi = jnp.zeros((SUBLANES, nstate), F32), jnp.zeros((SUBLANES, nstate), F32)
        cr, ci = ar, ai
        for t in range(SUBLANES):
            pr, pi = jnp.where(row == t, cr, pr), jnp.where(row == t, ci, pi)
            cr, ci = cr * ar - ci * ai, cr * ai + ci * ar
        pw_ref[0, 0:SUBLANES, :] = pr
        pw_ref[1, 0:SUBLANES, :] = pi
        n = SUBLANES
        while n < n_pow:
            m = min(n, n_pow - n)
            tr, ti = pw_ref[0, n - 1:n, :], pw_ref[1, n - 1:n, :]
            xr, xi = pw_ref[0, 0:m, :], pw_ref[1, 0:m, :]
            pw_ref[0, n:n + m, :] = xr * tr - xi * ti
            pw_ref[1, n:n + m, :] = xr * ti + xi * tr
            n += m

    vmem = pl.BlockSpec(memory_space=pltpu.VMEM)
    return pl.pallas_call(
        body, name=name, in_specs=[vmem] * 5, out_specs=[vmem] * 3,
        out_shape=[jax.ShapeDtypeStruct((2, n_pow, nstate), F32)] + [jax.ShapeDtypeStruct(b_re.shape, BF16)] * 2,
        compiler_params=pltpu.CompilerParams(vmem_limit_bytes=VMEM_LIMIT))(lr, li, ld, b_re, b_im)


def _s5_prep_bwd(lr, li, ld, b_re, b_im, da_re, da_im, dbc_re, dbc_im, name):
    def body(lr_ref, li_ref, ld_ref, bre_ref, bim_ref, dar_ref, dai_ref, dbcre_ref, dbcim_ref,
             dlr_ref, dli_ref, dld_ref, dbre_ref, dbim_ref):
        lr, li = lr_ref[...], li_ref[...]
        dt, ar, ai, den, nr, fr, fi = _zoh(lr, li, ld_ref[...])
        bre, bim = bre_ref[...], bim_ref[...]
        gre, gim = dbcre_ref[...], dbcim_ref[...]
        dbre_ref[...] = fr * gre + fi * gim
        dbim_ref[...] = fr * gim - fi * gre
        g_fr = jnp.sum(gre * bre + gim * bim, axis=0, keepdims=True)
        g_fi = jnp.sum(gim * bre - gre * bim, axis=0, keepdims=True)
        g_ar = dar_ref[...] + (g_fr * lr - g_fi * li) / den
        g_ai = dai_ref[...] + (g_fr * li + g_fi * lr) / den
        d_lr = (g_fr * (nr - 2.0 * fr * lr) + g_fi * (ai - 2.0 * fi * lr)) / den
        d_li = (g_fr * (ai - 2.0 * fr * li) - g_fi * (nr + 2.0 * fi * li)) / den
        g_logmag = g_ar * ar + g_ai * ai
        g_ang = g_ai * ar - g_ar * ai
        dlr_ref[...] = d_lr + g_logmag * dt
        dli_ref[...] = d_li + g_ang * dt
        d_ld = (g_logmag * lr + g_ang * li) * dt
        n = d_ld.shape[1]
        sh = 1
        while sh < STATE:
            d_ld = d_ld + pltpu.roll(d_ld, n - sh, 1)
            sh *= 2
        dld_ref[...] = d_ld

    vmem = pl.BlockSpec(memory_space=pltpu.VMEM)
    row = jax.ShapeDtypeStruct(lr.shape, F32)
    return pl.pallas_call(
        body, name=name, in_specs=[vmem] * 9, out_specs=[vmem] * 5,
        out_shape=[row, row, row, jax.ShapeDtypeStruct(b_re.shape, F32), jax.ShapeDtypeStruct(b_re.shape, F32)],
    )(lr, li, ld, b_re, b_im, da_re, da_im, dbc_re, dbc_im)


def _compact_b(bb):
    bq = bb.reshape(N_GROUPS // 8, 8, STATE, GROUP)
    m = jnp.einsum("ab,qbph->qahbp", jnp.eye(8, dtype=bb.dtype), bq).reshape(N_GROUPS // 8, LANES, 8 * STATE)
    return m.transpose(1, 0, 2).reshape(LANES, N_GROUPS * STATE)


def _expand_b(m):
    d = m.reshape(8, GROUP, N_GROUPS // 8, 8, STATE)
    return jnp.einsum("ahqap->qahp", d).reshape(N_GROUPS, GROUP, STATE)


def _compact_c(c):
    cq = c.reshape(N_GROUPS // 8, 8, GROUP, STATE)
    return jnp.einsum("ab,qbhp->qbpah", jnp.eye(8, dtype=c.dtype), cq).reshape(N_GROUPS * STATE, LANES)


def _expand_c(m):
    d = m.reshape(N_GROUPS // 8, 8, STATE, 8, GROUP)
    return jnp.einsum("qbpbh->qbhp", d).reshape(N_GROUPS, GROUP, STATE)


def _local_step(x, target, p, ex):
    seq, d = x.shape
    n_real = N_META + seq
    tp = -(-n_real // ROW_ALIGN) * ROW_ALIGN

    h0, hn1 = _input_norm_fwd(x, p["meta_tokens"], p["norm_mix_g"] + ex.zero, tp, "norm_mix")
    ex.forward("first", hn1)
    nstate = N_GROUPS * STATE
    s5 = (p["ssm_lam_re"].reshape(1, nstate), p["ssm_lam_im"].reshape(1, nstate),
          jnp.repeat(p["ssm_log_dt"].reshape(-1), STATE).reshape(1, nstate),
          _compact_b(p["ssm_b_re"]), _compact_b(p["ssm_b_im"]))
    a_pow, bc_re, bc_im = _s5_prep(*s5, tp // SUBLANES, "s5_prep")
    cc_re = _compact_c(p["ssm_c_re"]).astype(BF16)
    cc_im = _compact_c(p["ssm_c_im"]).astype(BF16)
    dskip = p["ssm_d"].reshape(1, -1)
    first = ex.weights("first", bc_re)
    proj = _mm(hn1, first["w_in"], "nn", "proj")
    started = ex.forward("mid", proj)
    co, y, g, *states = _seq_fwd(proj, p["conv_w"] + started[0, 0], bc_re, bc_im, cc_re, cc_im, dskip, a_pow,
                                 "seq_fwd")
    mid = ex.weights("mid", g)
    z = _mm(g, mid["ssm_w_glu"], "nn", "glu")
    mixed = _mix_fwd(co, y, z, p["gain_conv_out"], p["gain_ssm_out"], "mix_fwd")
    started = ex.forward("up", mixed)
    h1, hn2 = _proj_res_norm(mixed, mid["w_out"], h0, p["norm_ffn_g"], started, "out_proj_norm")
    late = ex.weights("up", hn2)
    part, fw = None, p["ffn_conv_w"]
    for col in range(late["w_up"].shape[1] // (2 * FFN_MM_COLS)):
        part = _ffn_up_act(hn2, late["w_up"], fw, p["ffn_conv_b"], col, part, "ffn_up_act_%d" % col)
        if col == 0:
            fw = fw + ex.forward("down", part[1])[0, 0]
    up, act = part
    late.update(ex.weights("down", act))
    loss, dh2, dh2b, d_gfin = _proj_loss_bwd(act, late["w_down"], h1, target, p["norm_final_g"], n_real,
                                             "down_proj_loss")

    g_w_down = _mm(act, dh2b, "tn", "g_w_down")
    dup, dfw_a, dfw_v, dfb_a, dfb_v = _ffn_bwd(up, dh2b, late["w_down"], p["ffn_conv_w"], p["ffn_conv_b"], "ffn_bwd")
    g_w_up = _mm(hn2, dup, "tn", "g_w_up")
    started = ex.grads_ready("late", {"w_up": g_w_up, "w_down": g_w_down})
    dh1, dh1b, d_gffn = _proj_norm_bwd(dup, late["w_up"], h1, p["norm_ffn_g"], dh2, started, "d_hn2_norm_bwd")
    started = ex.grads_send("late", dh1)
    g_w_out = _mm(mixed, dh1b, "tn", "g_w_out", after=started)
    dco, dz, dgp, d_gc, d_gs = _proj_mix_bwd(dh1b, mid["w_out"], co, y, z, p["gain_conv_out"],
                                             p["gain_ssm_out"], "d_mixed_mix_bwd")
    g_w_glu = _mm(g, dz, "tn", "g_w_glu")
    started = ex.grads_ready("mid", {"ssm_w_glu": g_w_glu, "w_out": g_w_out})
    dg = _mm(dz, mid["ssm_w_glu"], "nt", "d_gelu", acc_in=dgp, after=started)
    started = ex.grads_send("mid", dg)
    dproj, d_conv_w = _conv_bwd(proj, dco, p["conv_w"] + started[0, 0], "conv_bwd")
    (dproj, dbc_re, dbc_im, dcc_re, dcc_im, d_dskip, da_re, da_im) = _ssm_bwd(
        proj, y, dg, dproj, states, bc_re, bc_im, cc_re, cc_im, dskip, a_pow, "ssm_bwd")
    g_w_in = _mm(hn1, dproj, "tn", "g_w_in")
    started = ex.grads_ready("first", {"w_in": g_w_in})
    grad_x, d_meta, d_gmix = _proj_input_norm_bwd(dproj, first["w_in"], h0, p["norm_mix_g"], dh1, started, n_real,
                                                  "d_hn1_norm_bwd")
    started = ex.grads_send("first", d_gmix)

    d_lam_re, d_lam_im, d_log_dt, d_b_re, d_b_im = _s5_prep_bwd(*s5, da_re, da_im, dbc_re, dbc_im, "s5_prep_bwd")
    d_lam_re, d_lam_im = d_lam_re.reshape(N_GROUPS, STATE), d_lam_im.reshape(N_GROUPS, STATE)
    d_log_dt = d_log_dt[0, ::STATE]
    d_b_re, d_b_im = _expand_b(d_b_re), _expand_b(d_b_im)
    grads = {
        "meta_tokens": d_meta, "norm_mix_g": d_gmix, "w_in": g_w_in, "conv_w": d_conv_w,
        "ssm_lam_re": d_lam_re, "ssm_lam_im": d_lam_im, "ssm_log_dt": d_log_dt,
        "ssm_b_re": d_b_re, "ssm_b_im": d_b_im, "ssm_c_re": _expand_c(dcc_re), "ssm_c_im": _expand_c(dcc_im),
        "ssm_d": d_dskip.reshape(N_GROUPS, GROUP), "ssm_w_glu": g_w_glu,
        "gain_conv_out": d_gc, "gain_ssm_out": d_gs, "w_out": g_w_out, "norm_ffn_g": d_gffn,
        "w_up": g_w_up, "ffn_conv_w": jnp.concatenate([dfw_a, dfw_v], axis=1),
        "ffn_conv_b": jnp.concatenate([dfb_a, dfb_v], axis=1), "w_down": g_w_down, "norm_final_g": d_gfin,
    }
    return loss[0, 0] + started[0, 0], grad_x, grads


def _view(ref, axis, start, size):
    idx = [slice(None)] * len(ref.shape)
    idx[axis] = pl.ds(start, size)
    return ref.at[tuple(idx)]


def _exchange(name, ins, outs, aliases, local_copies, remote_copies):
    ni, no = len(ins), len(outs)
    nl, nr = len(local_copies), len(remote_copies)

    def body(*refs):
        in_refs, out_refs = refs[:ni], refs[ni:ni + no]
        send_sems, recv_sems, local_sems = refs[ni + no:]
        x, y, c = lax.axis_index("x"), lax.axis_index("y"), lax.axis_index("c")
        pos = (x, y, c, 2 * x + y)
        locals_ = [pltpu.make_async_copy(s(in_refs, out_refs, pos), d(in_refs, out_refs, pos), local_sems.at[i])
                   for i, (s, d) in enumerate(local_copies)]
        remotes = []
        for i, (s, d, flip) in enumerate(remote_copies):
            peer = (1 - x if "x" in flip else x, 1 - y if "y" in flip else y, 1 - c if "c" in flip else c)
            remotes.append(pltpu.make_async_remote_copy(
                src_ref=s(in_refs, out_refs, pos), dst_ref=d(in_refs, out_refs, pos),
                send_sem=send_sems.at[i], recv_sem=recv_sems.at[i], device_id=peer, device_id_type=MESH))
        for cp in locals_ + remotes:
            cp.start()
        for cp in remotes:
            cp.wait_recv()
        for cp in remotes:
            cp.wait_send()
        for cp in locals_:
            cp.wait()

    hbm = pl.BlockSpec(memory_space=pl.ANY)
    return pl.pallas_call(
        body, name=name, in_specs=[hbm] * ni, out_specs=[hbm] * no, out_shape=outs,
        input_output_aliases=aliases,
        scratch_shapes=[pltpu.SemaphoreType.DMA((nr,)), pltpu.SemaphoreType.DMA((nr,)),
                        pltpu.SemaphoreType.DMA((max(nl, 1),))],
    )(*ins)


BIG = {"w_in": (0, 1), "ssm_w_glu": (1, 0), "w_out": (1, 0), "w_up": (0, 1), "w_down": (1, 0)}
BIG_NAMES = tuple(BIG)
FLIPS = ("y", "x", "xy")


def _peer_chip(pos, flip):
    x, y, _, _ = pos
    return 2 * (1 - x if "x" in flip else x) + (1 - y if "y" in flip else y)


def _block_rows(rows, cols, itemsize, mult):
    return _pick_tile(rows, max(mult, (2 * 1024 * 1024) // (cols * itemsize)), mult)


def _cast_into_full(w, kc, shard_axis, name):
    r, cdim = w.shape
    tr = _block_rows(r, cdim, 4, 16)
    nb = r // tr

    def body(kc_ref, w_ref, o_ref):
        o_ref[...] = w_ref[...].astype(BF16)

    if shard_axis == 1:
        full, o_spec = (r, 4 * cdim), pl.BlockSpec((tr, cdim), lambda i, kc: (i, kc[0]))
    else:
        full, o_spec = (4 * r, cdim), pl.BlockSpec((tr, cdim), lambda i, kc: (kc[0] * nb + i, 0))
    return pl.pallas_call(
        body, name=name,
        grid_spec=pltpu.PrefetchScalarGridSpec(
            num_scalar_prefetch=1, grid=(nb,), in_specs=[pl.BlockSpec((tr, cdim), lambda i, kc: (i, 0))],
            out_specs=o_spec),
        out_shape=jax.ShapeDtypeStruct(full, BF16), compiler_params=_cparams("parallel"))(kc, w)


def _pair_sum(g, recv, kc, half_axis, name, out_dtype):
    hr, hc = recv.shape
    tr = _block_rows(hr, hc, 4, 16)
    nb = hr // tr

    def body(kc_ref, g_ref, r_ref, o_ref):
        o_ref[...] = (g_ref[...] + r_ref[...]).astype(out_dtype)

    if half_axis == 0:
        g_spec = pl.BlockSpec((tr, hc), lambda i, kc: (kc[1] * nb + i, 0))
    elif half_axis == 1:
        g_spec = pl.BlockSpec((tr, hc), lambda i, kc: (i, kc[1]))
    else:
        g_spec = pl.BlockSpec((tr, hc), lambda i, kc: (i, 0))
    same = pl.BlockSpec((tr, hc), lambda i, kc: (i, 0))
    return pl.pallas_call(
        body, name=name,
        grid_spec=pltpu.PrefetchScalarGridSpec(num_scalar_prefetch=1, grid=(nb,), in_specs=[g_spec, same],
                                               out_specs=same),
        out_shape=jax.ShapeDtypeStruct((hr, hc), out_dtype), compiler_params=_cparams("parallel"))(kc, g, recv)


def _chip_sum(own, recv, kc, own_axis, out_axis, name):
    _, sr, sc = recv.shape
    tr = _block_rows(sr, sc, 4, 16)
    nb = sr // tr

    def body(kc_ref, o_ref, r_ref, t_ref):
        k = kc_ref[0]
        own_v = o_ref[...].astype(F32)
        r = [r_ref[m].astype(F32) for m in range(3)]
        terms = []
        for kk in range(4):
            m = jnp.bitwise_xor(k, kk)
            terms.append(jnp.where(m == 0, own_v, jnp.where(m == 1, r[0], jnp.where(m == 2, r[1], r[2]))))
        t_ref[...] = (terms[0] + terms[1]) + (terms[2] + terms[3])

    if own_axis == 0:
        own_spec = pl.BlockSpec((tr, sc), lambda i, kc: (kc[0] * nb + i, 0))
    elif own_axis == 1:
        own_spec = pl.BlockSpec((tr, sc), lambda i, kc: (i, kc[0]))
    else:
        own_spec = pl.BlockSpec((tr, sc), lambda i, kc: (kc[1] * nb + i, 0))
    if out_axis == 0:
        out_full, out_spec = (2 * sr, sc), pl.BlockSpec((tr, sc), lambda i, kc: (kc[1] * nb + i, 0))
    else:
        out_full, out_spec = (sr, 2 * sc), pl.BlockSpec((tr, sc), lambda i, kc: (i, kc[1]))
    return pl.pallas_call(
        body, name=name,
        grid_spec=pltpu.PrefetchScalarGridSpec(
            num_scalar_prefetch=1, grid=(nb,),
            in_specs=[own_spec, pl.BlockSpec((3, tr, sc), lambda i, kc: (0, i, 0))],
            out_specs=out_spec),
        out_shape=jax.ShapeDtypeStruct(out_full, F32), compiler_params=_cparams("parallel"))(kc, own, recv)


def _adamw(w, g, m, v, name):
    r, cdim = w.shape
    tr = _block_rows(r, cdim, 4, 8)
    c1 = 1.0 - ADAM_B1 ** ADAM_STEP
    c2 = 1.0 - ADAM_B2 ** ADAM_STEP

    def body(w_ref, g_ref, m_ref, v_ref, go_ref, d_ref, nm_ref, nv_ref):
        gv = g_ref[...]
        go_ref[...] = gv
        nm = ADAM_B1 * m_ref[...] + (1.0 - ADAM_B1) * gv
        nv = ADAM_B2 * v_ref[...] + (1.0 - ADAM_B2) * (gv * gv)
        d_ref[...] = -ADAM_LR * ((nm / c1) / (jnp.sqrt(nv / c2) + ADAM_EPS) + ADAM_WD * w_ref[...])
        nm_ref[...] = nm
        nv_ref[...] = nv

    spec = _rows(cdim, tr)
    return pl.pallas_call(body, name=name, grid=(r // tr,), in_specs=[spec] * 4, out_specs=[spec] * 4,
                          out_shape=[jax.ShapeDtypeStruct((r, cdim), F32)] * 4,
                          compiler_params=_cparams("parallel"))(w, g, m, v)


def _adamw_whole(ws, gs, ms, vs, name):
    n = len(ws)
    c1 = 1.0 - ADAM_B1 ** ADAM_STEP
    c2 = 1.0 - ADAM_B2 ** ADAM_STEP

    def body(*refs):
        for i in range(n):
            w_ref, g_ref, m_ref, v_ref, d_ref, nm_ref, nv_ref = [refs[j * n + i] for j in range(7)]
            gv = g_ref[...]
            nm = ADAM_B1 * m_ref[...] + (1.0 - ADAM_B1) * gv
            nv = ADAM_B2 * v_ref[...] + (1.0 - ADAM_B2) * (gv * gv)
            d_ref[...] = -ADAM_LR * ((nm / c1) / (jnp.sqrt(nv / c2) + ADAM_EPS) + ADAM_WD * w_ref[...])
            nm_ref[...] = nm
            nv_ref[...] = nv

    vmem = pl.BlockSpec(memory_space=pltpu.VMEM)
    out = pl.pallas_call(body, name=name, in_specs=[vmem] * (4 * n), out_specs=[vmem] * (3 * n),
                         out_shape=[jax.ShapeDtypeStruct(a.shape, F32) for a in ws] * 3,
                         compiler_params=pltpu.CompilerParams(vmem_limit_bytes=VMEM_LIMIT))(*ws, *gs, *ms, *vs)
    return out[:n], out[n:2 * n], out[2 * n:]


SIDE_EFFECT = pltpu.SideEffectType.DATAFLOW_SIDE_EFFECTING


def _descriptors(copies, refs, send_sems, recv_sems, sem_off=0):
    x, y, c = lax.axis_index("x"), lax.axis_index("y"), lax.axis_index("c")
    pos = (x, y, c, 2 * x + y)
    out = []
    for i, (s, d, flip) in enumerate(copies):
        peer = (1 - x if "x" in flip else x, 1 - y if "y" in flip else y, 1 - c if "c" in flip else c)
        out.append(pltpu.make_async_remote_copy(
            src_ref=s(refs, refs, pos), dst_ref=d(refs, refs, pos),
            send_sem=send_sems.at[sem_off + i], recv_sem=recv_sems.at[sem_off + i],
            device_id=peer, device_id_type=MESH))
    return out


def _shifted(copies, off):
    return [(lambda I, O, pos, s=s: s(I[off:], O[off:], pos), lambda I, O, pos, d=d: d(I[off:], O[off:], pos), flip)
            for s, d, flip in copies]


BARRIER_IDS = {"c": (1, 2), "ici": (3, 4)}


def _exchange_start(name, bufs, copies, turns, after=None):
    n, nr = len(bufs), len(copies)
    na = 0 if after is None else 1
    flips = sorted({flip for _, _, flip in copies})
    kind = "c" if flips == ["c"] else "ici"
    collective_id = BARRIER_IDS[kind][turns[kind] % 2]
    turns[kind] += 1

    def body(*refs):
        x, y, c = lax.axis_index("x"), lax.axis_index("y"), lax.axis_index("c")
        barrier = pltpu.get_barrier_semaphore()
        for flip in flips:
            peer = (1 - x if "x" in flip else x, 1 - y if "y" in flip else y, 1 - c if "c" in flip else c)
            pl.semaphore_signal(barrier, inc=1, device_id=peer, device_id_type=MESH)
        pl.semaphore_wait(barrier, len(flips))
        for cp in _descriptors(copies, refs[:n], refs[n + na], refs[n + na + 1]):
            cp.start()
        token = refs[2 * n + na + 2]
        token[...] = jnp.zeros_like(token)

    hbm = pl.BlockSpec(memory_space=pltpu.HBM)
    sem = pl.BlockSpec(memory_space=pltpu.SEMAPHORE)
    out = pl.pallas_call(
        body, name=name,
        in_specs=[hbm] * n + [pl.BlockSpec(memory_space=pl.ANY)] * na,
        out_specs=(sem, sem, *[hbm] * n, pl.BlockSpec(memory_space=pltpu.VMEM)),
        out_shape=(pltpu.SemaphoreType.DMA((nr,)), pltpu.SemaphoreType.DMA((nr,)),
                   *[pltpu.HBM(b.shape, b.dtype) for b in bufs], jax.ShapeDtypeStruct((SUBLANES, LANES), F32)),
        input_output_aliases={i: 2 + i for i in range(n)},
        compiler_params=pltpu.CompilerParams(has_side_effects=SIDE_EFFECT, collective_id=collective_id),
    )(*[pltpu.with_memory_space_constraint(b, pltpu.HBM) for b in bufs], *([after] * na))
    return out[0], out[1], list(out[2:2 + n]), out[2 + n]


def _exchange_wait(name, send_sems, recv_sems, bufs, copies, after, sem_off=0):
    n = len(bufs)

    def body(*refs):
        for cp in _descriptors(copies, refs[:n], refs[n], refs[n + 1], sem_off):
            cp.wait_send()
            cp.wait_recv()

    hbm = pl.BlockSpec(memory_space=pltpu.HBM)
    sem = pl.BlockSpec(memory_space=pltpu.SEMAPHORE)
    out = pl.pallas_call(
        body, name=name,
        in_specs=[hbm] * n + [sem, sem, pl.BlockSpec(memory_space=pl.ANY)],
        out_specs=tuple([hbm] * n),
        out_shape=tuple(pltpu.HBM(b.shape, b.dtype) for b in bufs),
        input_output_aliases={i: i for i in range(n)},
        compiler_params=pltpu.CompilerParams(has_side_effects=SIDE_EFFECT),
    )(*bufs, send_sems, recv_sems, after)
    return list(out)


FIRST = ("w_in",)
MID = ("ssm_w_glu", "w_out")
LATE = ("w_up", "w_down")
GROUPS = {"first": FIRST, "mid": MID, "late": LATE}
ARRIVALS = {"first": FIRST, "mid": MID, "up": ("w_up",), "down": ("w_down",)}


def _gather_copies(names, shard_shapes):
    def region(i, chip, c):
        half_axis, shard_axis = BIG[names[i]]
        ssize = shard_shapes[i][shard_axis]
        hsize = shard_shapes[i][half_axis] // 2
        return lambda ref: _view(_view(ref, shard_axis, chip * ssize, ssize), half_axis, c * hsize, hsize)

    ici, d2d = [], []
    for i in range(len(names)):
        for flip in FLIPS:
            ici.append((lambda I, O, pos, i=i: region(i, pos[3], pos[2])(I[i]),
                        lambda I, O, pos, i=i: region(i, pos[3], pos[2])(O[i]), flip))
            d2d.append((lambda I, O, pos, i=i, flip=flip: region(i, _peer_chip(pos, flip), pos[2])(I[i]),
                        lambda I, O, pos, i=i, flip=flip: region(i, _peer_chip(pos, flip), pos[2])(O[i]), "c"))
    return ici, d2d


def _half_shape(n, shape):
    r, cdim = shape
    return (r // 2, cdim) if BIG[n][0] == 0 else (r, cdim // 2)


def _sub_shape(n, shape):
    hr, hc = _half_shape(n, shape)
    return (hr, hc // 4) if BIG[n][1] == 1 else (hr // 4, hc)


def _pair_copies(names, shapes, with_pack, dst_off):
    n = len(names)

    def other_half(i, ref, pos):
        half_axis = BIG[names[i]][0]
        hsize = shapes[i][half_axis] // 2
        return _view(ref, half_axis, (1 - pos[2]) * hsize, hsize)

    copies = [(lambda I, O, pos, i=i: other_half(i, I[i], pos), lambda I, O, pos, i=i: O[dst_off + i], "c")
              for i in range(n)]
    if with_pack:
        copies.append((lambda I, O, pos: I[n], lambda I, O, pos: O[dst_off + n], "c"))
    return copies


def _chip_copies(names, shapes, pack_rows, dst_off):
    n = len(names)

    def piece(i, ref, chip):
        shard_axis = BIG[names[i]][1]
        ssize = _sub_shape(names[i], shapes[i])[shard_axis]
        return _view(ref, shard_axis, chip * ssize, ssize)

    copies = []
    for i in range(n):
        for slot, flip in enumerate(FLIPS):
            copies.append((lambda I, O, pos, i=i, flip=flip: piece(i, I[i], _peer_chip(pos, flip)),
                           lambda I, O, pos, i=i, slot=slot: O[dst_off + i].at[slot], flip))
    if pack_rows:
        for slot, flip in enumerate(FLIPS):
            copies.append((lambda I, O, pos: _view(I[n], 0, pos[2] * (pack_rows // 2), pack_rows // 2),
                           lambda I, O, pos, slot=slot: O[dst_off + n].at[slot], flip))
    return copies


class _Exchanges:
    def __init__(self, shards, tiny, kc):
        self.kc = kc
        wb = {n: _cast_into_full(shards[n], kc, BIG[n][1], "cast_" + n) for n in BIG_NAMES}
        self.gathering, self.forwarding, self.pairing, self.reducing = {}, {}, {}, {}
        self.turns = {"c": 0, "ici": 0}
        tiny_copies = [(lambda I, O, pos: I[0], lambda I, O, pos: O[1].at[pos[3]], flip) for flip in FLIPS]
        self.gathering["tiny"] = (0, 0, 2, tiny_copies, None)
        bufs, copies = [tiny, lax.empty((4,) + tiny.shape, F32)], list(tiny_copies)
        for group, names in ARRIVALS.items():
            ici, d2d = _gather_copies(names, [shards[n].shape for n in names])
            self.gathering[group] = (len(bufs), len(copies), len(names), ici, d2d)
            copies += _shifted(ici, len(bufs))
            bufs += [wb[n] for n in names]
        self.started = _exchange_start("gather_start", bufs, copies, self.turns)
        self.zero = self.started[3][0, 0]

    def _arrived(self, group, after):
        buf_off, sem_off, n, ici, _ = self.gathering[group]
        send_sems, recv_sems, bufs, _ = self.started
        return _exchange_wait("gather_%s_wait" % group, send_sems, recv_sems, bufs[buf_off:buf_off + n], ici, after,
                              sem_off)

    def small_params(self, kc):
        tiny, got = self._arrived("tiny", self.started[3])
        return lax.dynamic_update_index_in_dim(got, tiny, kc[0], 0)

    def forward(self, group, after):
        d2d = self.gathering[group][4]
        self.forwarding[group] = (_exchange_start("forward_%s_start" % group, self._arrived(group, after), d2d,
                                                  self.turns), d2d)
        return self.forwarding[group][0][3]

    def weights(self, group, after):
        if group not in self.forwarding:
            after = self.forward(group, after)
        (send_sems, recv_sems, bufs, _), d2d = self.forwarding[group]
        full = _exchange_wait("forward_%s_wait" % group, send_sems, recv_sems, bufs, d2d, after)
        return dict(zip(ARRIVALS[group], full))

    def grads_ready(self, group, grads):
        names = GROUPS[group]
        gs = [grads[n] for n in names]
        land = [lax.empty(_half_shape(n, g.shape), F32) for n, g in zip(names, gs)]
        copies = _pair_copies(names, [g.shape for g in gs], False, len(names))
        started = _exchange_start("pair_%s_start" % group, gs + land, copies, self.turns)
        self.pairing[group] = (started, copies)
        return started[3]

    def grads_send(self, group, after):
        names = GROUPS[group]
        n = len(names)
        (send_sems, recv_sems, bufs, _), copies = self.pairing[group]
        bufs = _exchange_wait("pair_%s_wait" % group, send_sems, recv_sems, bufs, copies, after)
        chip = [_pair_sum(bufs[i], bufs[n + i], self.kc, BIG[names[i]][0], "pair_sum_" + names[i], BF16)
                for i in range(n)]
        shapes = [bufs[i].shape for i in range(n)]
        land = [lax.empty((3,) + _sub_shape(names[i], shapes[i]), BF16) for i in range(n)]
        copies = _chip_copies(names, shapes, 0, n)
        started = _exchange_start("reduce_%s_start" % group, chip + land, copies, self.turns)
        self.reducing[group] = (started, copies)
        return started[3]

    def finish_pack(self, pack):
        kc = self.kc
        prow = pack.shape[0] // 2
        recv = _exchange("reduce_d2d", [pack], [jax.ShapeDtypeStruct(pack.shape, F32)], {}, [],
                         _pair_copies((), [], True, 0))
        chip_pack = _pair_sum(pack, recv[0], kc, None, "pair_sum_pack", F32)
        copies = _chip_copies((), [], pack.shape[0], 1)
        land = lax.empty((3, prow, pack.shape[1]), F32)
        pack_sems_s, pack_sems_r, pack_bufs, after = _exchange_start("reduce_pack_start", [chip_pack, land], copies,
                                                                     self.turns)

        names, chips, recvs = (), [], []
        for group, group_names in GROUPS.items():
            (send_sems, recv_sems, bufs, _), group_copies = self.reducing[group]
            bufs = _exchange_wait("reduce_%s_wait" % group, send_sems, recv_sems, bufs, group_copies, after)
            n = len(group_names)
            names, chips, recvs = names + group_names, chips + bufs[:n], recvs + bufs[n:]
            after = bufs[n]
        total = [_chip_sum(chips[i], recvs[i], kc, BIG[n][1], BIG[n][0], "chip_sum_" + n)
                 for i, n in enumerate(names)]

        def my_half(half_axis, ref, pos):
            hsize = ref.shape[half_axis] // 2
            return _view(ref, half_axis, pos[2] * hsize, hsize)

        swap = [(lambda I, O, pos, i=i, n=n: my_half(BIG[n][0], I[i], pos),
                 lambda I, O, pos, i=i, n=n: my_half(BIG[n][0], O[i], pos), "c") for i, n in enumerate(names)]
        self.swapping = (_exchange_start("swap_start", total, swap, self.turns), swap, names)

        chip_pack, recv_pack = _exchange_wait("reduce_pack_wait", pack_sems_s, pack_sems_r, pack_bufs, copies,
                                              self.swapping[0][3])
        total_pack = _chip_sum(chip_pack, recv_pack, kc, None, 0, "chip_sum_pack")
        swap = [(lambda I, O, pos: my_half(0, I[0], pos), lambda I, O, pos: my_half(0, O[0], pos), "c")]
        return _exchange("swap_pack", [total_pack], [jax.ShapeDtypeStruct(pack.shape, F32)], {0: 0}, [], swap)[0]

    def finish_big(self, after):
        (send_sems, recv_sems, bufs, _), swap, names = self.swapping
        return dict(zip(names, _exchange_wait("swap_wait", send_sems, recv_sems, bufs, swap, after)))


WEIGHTS = ("meta_tokens", "norm_mix_g", "w_in", "conv_w", "ssm_lam_re", "ssm_lam_im", "ssm_log_dt", "ssm_b_re",
           "ssm_b_im", "ssm_c_re", "ssm_c_im", "ssm_d", "ssm_w_glu", "gain_conv_out", "gain_ssm_out", "w_out",
           "norm_ffn_g", "w_up", "ffn_conv_w", "ffn_conv_b", "w_down", "norm_final_g")
TINY_SHARDED = ("meta_tokens", "conv_w", "ffn_conv_w")
REPLICATED = tuple(n for n in WEIGHTS if n not in BIG and n not in TINY_SHARDED)
PACK_COLS = 512


def _pack(arrays, row_mult, cols):
    flat = jnp.concatenate([a.reshape(-1).astype(F32) for a in arrays])
    n = flat.shape[0]
    total = -(-n // (row_mult * cols)) * (row_mult * cols)
    return jnp.concatenate([flat, jnp.zeros((total - n,), F32)]).reshape(total // cols, cols)


def _unpack(packed, shapes):
    flat = packed.reshape(-1)
    out, off = [], 0
    for s in shapes:
        n = math.prod(s)
        out.append(flat[off:off + n].reshape(s))
        off += n
    return out


def kernel(x, meta_tokens, norm_mix_g, w_in, conv_w, ssm_lam_re, ssm_lam_im, ssm_log_dt, ssm_b_re, ssm_b_im, ssm_c_re, ssm_c_im, ssm_d, ssm_w_glu, gain_conv_out, gain_ssm_out, w_out, norm_ffn_g, w_up, ffn_conv_w, ffn_conv_b, w_down, norm_final_g, loss_target, m_meta_tokens, m_norm_mix_g, m_w_in, m_conv_w, m_ssm_lam_re, m_ssm_lam_im, m_ssm_log_dt, m_ssm_b_re, m_ssm_b_im, m_ssm_c_re, m_ssm_c_im, m_ssm_d, m_ssm_w_glu, m_gain_conv_out, m_gain_ssm_out, m_w_out, m_norm_ffn_g, m_w_up, m_ffn_conv_w, m_ffn_conv_b, m_w_down, m_norm_final_g, v_meta_tokens, v_norm_mix_g, v_w_in, v_conv_w, v_ssm_lam_re, v_ssm_lam_im, v_ssm_log_dt, v_ssm_b_re, v_ssm_b_im, v_ssm_c_re, v_ssm_c_im, v_ssm_d, v_ssm_w_glu, v_gain_conv_out, v_gain_ssm_out, v_w_out, v_norm_ffn_g, v_w_up, v_ffn_conv_w, v_ffn_conv_b, v_w_down, v_norm_final_g):
    args = dict(locals())
    w = {n: args[n] for n in WEIGHTS}
    mom = {n: args["m_" + n] for n in WEIGHTS}
    var = {n: args["v_" + n] for n in WEIGHTS}
    kx, ky, kc_ = lax.axis_index("x"), lax.axis_index("y"), lax.axis_index("c")
    chip = 2 * kx + ky
    kc = jnp.stack([chip, kc_]).astype(jnp.int32)

    def squeeze(n, a):
        if n == "meta_tokens":
            return a
        if n == "norm_final_g":
            return a.reshape(1, -1)
        a = a[0]
        return a.reshape(1, -1) if a.ndim == 1 else a

    wl = {n: squeeze(n, w[n]) for n in WEIGHTS}
    ml = {n: squeeze(n, mom[n]) for n in WEIGHTS}
    vl = {n: squeeze(n, var[n]) for n in WEIGHTS}

    tiny = _pack([wl[n] for n in TINY_SHARDED], SUBLANES, LANES)
    ex = _Exchanges({n: wl[n] for n in BIG_NAMES}, tiny, kc)
    tiny_shapes = [wl[n].shape for n in TINY_SHARDED]
    tiny_all = ex.small_params(kc)
    tiny_parts = [_unpack(tiny_all[k], tiny_shapes) for k in range(4)]
    p = {n: wl[n] for n in WEIGHTS if n not in BIG}
    for j, n in enumerate(TINY_SHARDED):
        p[n] = jnp.concatenate([tiny_parts[k][j] for k in range(4)], axis=1)
    p["ssm_log_dt"] = wl["ssm_log_dt"].reshape(-1)

    loss_local, grad_x, grads = _local_step(x[0], loss_target[0], p, ex)

    small_names = REPLICATED + TINY_SHARDED
    small_shapes = [tuple(grads[n].shape) for n in small_names] + [(1,)]
    pack = _pack([grads[n] for n in small_names] + [loss_local.reshape(1)], 2 * 16, PACK_COLS)
    g_pack = ex.finish_pack(pack)
    g_small = dict(zip(small_names + ("loss",), _unpack(g_pack, small_shapes)))
    loss = g_small["loss"][0]
    swapped = ("ssm_b_re", "ssm_b_im")

    def view(n, a):
        if n in swapped:
            return jnp.swapaxes(a, -1, -2)
        return a.reshape(1, -1) if a.ndim == 1 else a

    g = {}
    for n in REPLICATED:
        g[n] = g_small[n].reshape(view(n, w[n]).shape)
    for n in TINY_SHARDED:
        cols = wl[n].shape[1]
        g[n] = lax.dynamic_slice_in_dim(g_small[n], chip * cols, cols, axis=1).reshape(w[n].shape)
    delta, new_m, new_v = {}, {}, {}
    small = [[view(n, d[n]) for n in small_names] for d in (w, mom, var)]
    small.insert(1, [g[n] for n in small_names])
    for d, outs in zip((delta, new_m, new_v), _adamw_whole(*small, "adamw_small")):
        d.update(zip(small_names, outs))
    for d in (g, delta, new_m, new_v):
        d.update({n: jnp.swapaxes(d[n], -1, -2) for n in swapped})
    g_big = ex.finish_big(delta[small_names[0]])
    for n in BIG_NAMES:
        g[n], delta[n], new_m[n], new_v[n] = _adamw(wl[n], g_big[n], ml[n], vl[n], "adamw_" + n)

    def like(n, a):
        return a.reshape(w[n].shape)

    return (loss, grad_x[None], *[like(n, g[n]) for n in WEIGHTS], *[like(n, delta[n]) for n in WEIGHTS],
            *[like(n, new_m[n]) for n in WEIGHTS], *[like(n, new_v[n]) for n in WEIGHTS])
```

```python
import functools
import math

import jax
import jax.numpy as jnp
from jax import lax
from jax.experimental import pallas as pl
from jax.experimental.pallas import tpu as pltpu

F32 = jnp.float32
BF16 = jnp.bfloat16
MESH = pl.DeviceIdType.MESH

N_META = 16
N_GROUPS = 32
GROUP = 16
STATE = 64
RMS_EPS = 1e-6
ADAM_LR = 0.001
ADAM_B1 = 0.9
ADAM_B2 = 0.999
ADAM_EPS = 1e-08
ADAM_WD = 0.01
ADAM_STEP = 10

LANES = 128
SUBLANES = 8
ROW_ALIGN = 128
ROW_TILES = 4
VMEM_LIMIT = 52 * 1024 * 1024
MM_VMEM_BUDGET = 40 * 1024 * 1024
GELU_C = math.sqrt(2.0 / math.pi)
GELU_A = 0.044715


def _cparams(*sem):
    return pltpu.CompilerParams(dimension_semantics=sem, vmem_limit_bytes=VMEM_LIMIT)


def _pick_tile(dim, cap, mult):
    best = None
    for t in range(mult, min(dim, cap) + 1, mult):
        if dim % t == 0:
            best = t
    return best if best is not None else dim


def _mm(a, b, mode, name, out_dtype=F32, acc_in=None, after=None):
    if mode == "tn":
        kdim, m = a.shape
    else:
        m, kdim = a.shape
    n = b.shape[0] if mode == "nt" else b.shape[1]
    tm = _pick_tile(m, 1408, LANES if mode == "tn" else 16)
    tk = _pick_tile(kdim, 2816, LANES)
    nk = kdim // tk
    out_bytes = jnp.dtype(out_dtype).itemsize
    for cap in (704, 512, 256, LANES) if m == tm else (1408, 1024, 512, 256, LANES):
        tn = _pick_tile(n, cap, LANES)
        blocks = 2 * (tm * tk * 2 + tk * tn * 2 + tm * tn * out_bytes * (2 if acc_in is not None else 1))
        if blocks + (tm * tn * 4 if nk > 1 else 0) <= MM_VMEM_BUDGET:
            break
    if n == tn and m != tm and nk == 1:
        tm = _pick_tile(m, 704, LANES if mode == "tn" else 16)
    has_acc = acc_in is not None

    def body(*refs):
        if after is not None:
            refs = refs[1:]
        if has_acc:
            a_ref, b_ref, c_ref, o_ref = refs[:4]
            rest = refs[4:]
        else:
            a_ref, b_ref, o_ref = refs[:3]
            c_ref = None
            rest = refs[3:]
        if mode == "nn":
            p = jnp.dot(a_ref[...], b_ref[...], preferred_element_type=F32)
        elif mode == "nt":
            p = lax.dot_general(a_ref[...], b_ref[...], (((1,), (1,)), ((), ())), preferred_element_type=F32)
        else:
            p = lax.dot_general(a_ref[...], b_ref[...], (((0,), (0,)), ((), ())), preferred_element_type=F32)
        if nk == 1:
            if has_acc:
                p = p + c_ref[...]
            o_ref[...] = p.astype(out_dtype)
        else:
            acc_ref = rest[0]
            k = pl.program_id(2)

            @pl.when(k == 0)
            def _():
                acc_ref[...] = p + c_ref[...] if has_acc else p

            @pl.when(k > 0)
            def _():
                acc_ref[...] += p

            @pl.when(k == nk - 1)
            def _():
                o_ref[...] = acc_ref[...].astype(out_dtype)

    if mode == "tn":
        a_spec = pl.BlockSpec((tk, tm), lambda i, j, k: (k, i))
    else:
        a_spec = pl.BlockSpec((tm, tk), lambda i, j, k: (i, k))
    if mode == "nt":
        b_spec = pl.BlockSpec((tn, tk), lambda i, j, k: (j, k))
    else:
        b_spec = pl.BlockSpec((tk, tn), lambda i, j, k: (k, j))
    o_spec = pl.BlockSpec((tm, tn), lambda i, j, k: (i, j))
    in_specs = [a_spec, b_spec] + ([o_spec] if has_acc else [])
    args = (a, b) + ((acc_in,) if has_acc else ())
    if after is not None:
        in_specs = [pl.BlockSpec(memory_space=pl.ANY)] + in_specs
        args = (after,) + args
    return pl.pallas_call(
        body, name=name, grid=(m // tm, n // tn, nk),
        in_specs=in_specs, out_specs=o_spec,
        out_shape=jax.ShapeDtypeStruct((m, n), out_dtype),
        scratch_shapes=[pltpu.VMEM((tm, tn), F32)] if nk > 1 else [],
        compiler_params=_cparams("parallel", "parallel", "arbitrary"),
    )(*args)


def _mm_rows(a, b, mode, name, ins, outs, epilogue, scratch=()):
    m, kdim = a.shape
    n = b.shape[0] if mode == "nt" else b.shape[1]
    tm = m // ROW_TILES
    tk = _pick_tile(kdim, 2816, LANES)
    nk = kdim // tk
    ni, no = len(ins), len(outs)

    def body(*refs):
        a_ref, b_ref = refs[:2]
        in_refs, out_refs, rest = refs[2:2 + ni], refs[2 + ni:2 + ni + no], refs[2 + ni + no:]
        k, i = pl.program_id(0), pl.program_id(1)
        if mode == "nn":
            p = jnp.dot(a_ref[...], b_ref[...], preferred_element_type=F32)
        else:
            p = lax.dot_general(a_ref[...], b_ref[...], (((1,), (1,)), ((), ())), preferred_element_type=F32)
        if nk == 1:
            epilogue(p, i, in_refs, out_refs, rest)
        else:
            acc_ref = rest[0]
            rows = pl.ds(pl.multiple_of(i * tm, SUBLANES), tm)

            @pl.when(k == 0)
            def _():
                acc_ref[rows, :] = p

            @pl.when(jnp.logical_and(k > 0, k < nk - 1))
            def _():
                acc_ref[rows, :] += p

            @pl.when(k == nk - 1)
            def _():
                epilogue(acc_ref[rows, :] + p, i, in_refs, out_refs, rest[1:])

    tile = (lambda k, i: i) if nk == 1 else (lambda k, i: jnp.where(k == nk - 1, i, 0))

    def spec(shape, kind):
        if kind == "rows":
            return pl.BlockSpec((tm,) + tuple(shape[1:]), lambda k, i: (tile(k, i),) + (0,) * (len(shape) - 1))
        if kind == "whole":
            return pl.BlockSpec(tuple(shape), lambda k, i: (0,) * len(shape))
        return pl.BlockSpec(memory_space=pl.ANY)

    a_spec = pl.BlockSpec((tm, tk), lambda k, i: (i, k))
    b_spec = pl.BlockSpec((n, tk), lambda k, i: (0, k)) if mode == "nt" else pl.BlockSpec((tk, n), lambda k, i: (k, 0))
    return pl.pallas_call(
        body, name=name, grid=(nk, ROW_TILES),
        in_specs=[a_spec, b_spec] + [spec(x.shape, kind) for x, kind in ins],
        out_specs=[spec(shape, kind) for shape, _, kind in outs],
        out_shape=[jax.ShapeDtypeStruct(shape, dtype) for shape, dtype, _ in outs],
        scratch_shapes=([pltpu.VMEM((m, n), F32)] if nk > 1 else []) + list(scratch),
        compiler_params=_cparams("arbitrary", "arbitrary"),
    )(a, b, *[x for x, _ in ins])


def _rows(shape_cols, tr, dtype=None):
    return pl.BlockSpec((tr, shape_cols), lambda i: (i, 0))


def _const(shape):
    return pl.BlockSpec(shape, lambda i: (0,) * len(shape))


def _rms(x):
    return lax.rsqrt(jnp.mean(x * x, axis=-1, keepdims=True) + RMS_EPS)


def _rms_bwd(x, r, g, dy):
    xn = x * r
    dxn = dy * g
    dx = r * (dxn - xn * jnp.mean(dxn * xn, axis=-1, keepdims=True))
    return dx, dy * xn


def _gelu(y):
    return 0.5 * y * (1.0 + jnp.tanh(GELU_C * (y + GELU_A * y * y * y)))


def _gelu_grad(y):
    t = jnp.tanh(GELU_C * (y + GELU_A * y * y * y))
    return 0.5 * (1.0 + t) + 0.5 * y * (1.0 - t * t) * GELU_C * (1.0 + 3.0 * GELU_A * y * y)


def _sigmoid(z):
    return 1.0 / (1.0 + jnp.exp(-z))


def _proj_res_norm(a, w, h, g, after, name):
    def epilogue(p, i, ins, outs, _):
        x = ins[0][...] + p
        outs[0][...] = x
        outs[1][...] = (x * _rms(x) * ins[1][...]).astype(BF16)

    return _mm_rows(a, w, "nn", name, [(h, "rows"), (g, "whole"), (after, "hbm")],
                    [(h.shape, F32, "rows"), (h.shape, BF16, "rows")], epilogue)


def _proj_norm_bwd(da, w, h, g, dres, after, name):
    d = h.shape[1]

    def epilogue(p, i, ins, outs, _):
        x = ins[0][...]
        dx, dgs = _rms_bwd(x, _rms(x), ins[1][...], p)
        dh = ins[2][...] + dx
        outs[0][...] = dh
        outs[1][...] = dh.astype(BF16)

        @pl.when(i == 0)
        def _():
            outs[2][...] = jnp.zeros_like(outs[2])

        outs[2][...] += jnp.sum(dgs, axis=0, keepdims=True)

    return _mm_rows(da, w, "nt", name, [(h, "rows"), (g, "whole"), (dres, "rows"), (after, "hbm")],
                    [(h.shape, F32, "rows"), (h.shape, BF16, "rows"), ((1, d), F32, "whole")], epilogue)


def _proj_input_norm_bwd(da, w, h, g, dres, after, n_real, name):
    tp, d = h.shape
    tr = tp // ROW_TILES

    def epilogue(p, i, ins, outs, scratch):
        h_ref, g_ref, dres_ref, _ = ins
        dx_ref, dmeta_ref, dg_ref = outs
        stage, sem = scratch
        x = h_ref[...]
        dx, dgs = _rms_bwd(x, _rms(x), g_ref[...], p)
        stage[...] = dres_ref[...] + dx

        @pl.when(i == 0)
        def _():
            dg_ref[...] = jnp.zeros_like(dg_ref)
            dmeta_ref[...] = stage[:N_META, :]

        dg_ref[...] += jnp.sum(dgs, axis=0, keepdims=True)
        for t in range(ROW_TILES):
            lo, hi = max(t * tr, N_META), min((t + 1) * tr, n_real)
            if hi > lo:
                @pl.when(i == t)
                def _(t=t, lo=lo, hi=hi):
                    cp = pltpu.make_async_copy(stage.at[pl.ds(lo - t * tr, hi - lo), :],
                                               dx_ref.at[pl.ds(lo - N_META, hi - lo), :], sem)
                    cp.start()
                    cp.wait()

    return _mm_rows(da, w, "nt", name, [(h, "rows"), (g, "whole"), (dres, "rows"), (after, "hbm")],
                    [((n_real - N_META, d), F32, "hbm"), ((N_META, d), F32, "whole"), ((1, d), F32, "whole")],
                    epilogue, scratch=[pltpu.VMEM((tr, d), F32), pltpu.SemaphoreType.DMA])


def _load_token_rows(tok_hbm, buf, sem, tr, n_real, head=None, wait=False, i=None):
    i = pl.program_id(0) if i is None else i
    for t in range(ROW_TILES):
        base = t * tr
        lo, hi = max(base, N_META), min(base + tr, n_real)

        @pl.when(i == t)
        def _(base=base, lo=lo, hi=hi):
            if hi > lo:
                cp = pltpu.make_async_copy(tok_hbm.at[pl.ds(lo - N_META, hi - lo), :],
                                           buf.at[pl.ds(lo - base, hi - lo), :], sem)
                if wait:
                    cp.wait()
                    return
                cp.start()
            if wait:
                return
            if base < N_META:
                buf[0:N_META - base, :] = (jnp.zeros((N_META - base, buf.shape[1]), F32) if head is None
                                           else head[base:N_META, :])
            if hi < base + tr:
                buf[max(hi, base) - base:tr, :] = jnp.zeros((base + tr - max(hi, base), buf.shape[1]), F32)


def _input_norm_fwd(x, meta, g, tp, name):
    seq, d = x.shape
    tr = tp // ROW_TILES
    n_real = N_META + seq

    def body(x_hbm, meta_ref, g_ref, h_ref, hn_ref, buf, sem):
        _load_token_rows(x_hbm, buf, sem, tr, n_real, head=meta_ref)
        _load_token_rows(x_hbm, buf, sem, tr, n_real, wait=True)
        h = buf[...]
        h_ref[...] = h
        hn_ref[...] = (h * _rms(h) * g_ref[...]).astype(BF16)

    return pl.pallas_call(
        body, name=name, grid=(ROW_TILES,),
        in_specs=[pl.BlockSpec(memory_space=pl.ANY), _const((N_META, d)), _const((1, d))],
        out_specs=[_rows(d, tr), _rows(d, tr)],
        out_shape=[jax.ShapeDtypeStruct((tp, d), F32), jax.ShapeDtypeStruct((tp, d), BF16)],
        scratch_shapes=[pltpu.VMEM((tr, d), F32), pltpu.SemaphoreType.DMA],
        compiler_params=_cparams("arbitrary"))(x, meta, g)


def _proj_loss_bwd(act, w, h1, target, g, n_real, name):
    tp, d = h1.shape
    tr = tp // ROW_TILES

    def epilogue(p, i, ins, outs, scratch):
        h1_ref, t_hbm, g_ref = ins
        loss_ref, dh_ref, dhb_ref, dg_ref = outs
        t_buf, sem = scratch
        _load_token_rows(t_hbm, t_buf, sem, tr, n_real, i=i)
        x = h1_ref[...] + p
        r = _rms(x)
        row = i * tr + lax.broadcasted_iota(jnp.int32, (tr, d), 0)
        valid = (row >= N_META) & (row < n_real)
        _load_token_rows(t_hbm, t_buf, sem, tr, n_real, wait=True, i=i)
        e = jnp.where(valid, x * r * g_ref[...] - t_buf[...], 0.0)
        dx, dgs = _rms_bwd(x, r, g_ref[...], e * (1.0 / d))
        dh_ref[...] = dx
        dhb_ref[...] = dx.astype(BF16)

        @pl.when(i == 0)
        def _():
            dg_ref[...] = jnp.zeros_like(dg_ref)
            loss_ref[...] = jnp.zeros_like(loss_ref)

        dg_ref[...] += jnp.sum(dgs, axis=0, keepdims=True)
        loss_ref[...] += (0.5 / d) * jnp.sum(jnp.sum(e * e, axis=0, keepdims=True), axis=1, keepdims=True)

    return _mm_rows(act, w, "nn", name, [(h1, "rows"), (target, "hbm"), (g, "whole")],
                    [((1, LANES), F32, "whole"), ((tp, d), F32, "rows"), ((tp, d), BF16, "rows"),
                     ((1, d), F32, "whole")],
                    epilogue, scratch=[pltpu.VMEM((tr, d), F32), pltpu.SemaphoreType.DMA])


def _mix_fwd(co, y, z, gc, gs, name):
    tp, dh = co.shape
    tr = tp // ROW_TILES

    def body(co_ref, y_ref, z_ref, gc_ref, gs_ref, m_ref):
        c = co_ref[...]
        m_ref[:, :dh] = (c * _rms(c) * gc_ref[...]).astype(BF16)
        so = _gelu(y_ref[...]) * _sigmoid(z_ref[...])
        m_ref[:, dh:] = (so * _rms(so) * gs_ref[...]).astype(BF16)

    return pl.pallas_call(
        body, name=name, grid=(ROW_TILES,),
        in_specs=[_rows(dh, tr)] * 3 + [_const((1, dh))] * 2,
        out_specs=_rows(2 * dh, tr),
        out_shape=jax.ShapeDtypeStruct((tp, 2 * dh), BF16),
        compiler_params=_cparams("parallel"))(co, y, z, gc, gs)


def _proj_mix_bwd(dh1b, w, co, y, z, gc, gs, name):
    tp, dh = co.shape

    def epilogue(p, i, ins, outs, _):
        co_ref, y_ref, z_ref, gc_ref, gs_ref = ins
        dco_ref, dz_ref, dgp_ref, dgc_ref, dgs_ref = outs
        c = co_ref[...]
        dco, dgc = _rms_bwd(c, _rms(c), gc_ref[...], p[:, :dh])
        dco_ref[...] = dco
        gl = _gelu(y_ref[...])
        sg = _sigmoid(z_ref[...])
        so = gl * sg
        dso, dgs = _rms_bwd(so, _rms(so), gs_ref[...], p[:, dh:])
        dz_ref[...] = (dso * gl * sg * (1.0 - sg)).astype(BF16)
        dgp_ref[...] = dso * sg

        @pl.when(i == 0)
        def _():
            dgc_ref[...] = jnp.zeros_like(dgc_ref)
            dgs_ref[...] = jnp.zeros_like(dgs_ref)

        dgc_ref[...] += jnp.sum(dgc, axis=0, keepdims=True)
        dgs_ref[...] += jnp.sum(dgs, axis=0, keepdims=True)

    return _mm_rows(dh1b, w, "nt", name,
                    [(co, "rows"), (y, "rows"), (z, "rows"), (gc, "whole"), (gs, "whole")],
                    [((tp, dh), F32, "rows"), ((tp, dh), BF16, "rows"), ((tp, dh), F32, "rows"),
                     ((1, dh), F32, "whole"), ((1, dh), F32, "whole")], epilogue)


def _shift_down(x, k):
    row = lax.broadcasted_iota(jnp.int32, x.shape, 0)
    return jnp.where(row >= k, pltpu.roll(x, k, 0), 0.0)


def _shift_up(x, k):
    n = x.shape[0]
    row = lax.broadcasted_iota(jnp.int32, x.shape, 0)
    return jnp.where(row < n - k, pltpu.roll(x, n - k, 0), 0.0)


def _dwconv(x, w_ref):
    return w_ref[2:3, :] * x + w_ref[1:2, :] * _shift_down(x, 1) + w_ref[0:1, :] * _shift_down(x, 2)


def _dwconv_bwd(x, dy, w_ref):
    dx = w_ref[2:3, :] * dy + w_ref[1:2, :] * _shift_up(dy, 1) + w_ref[0:1, :] * _shift_up(dy, 2)
    dw = jnp.concatenate([jnp.sum(dy * _shift_down(x, 2), axis=0, keepdims=True),
                          jnp.sum(dy * _shift_down(x, 1), axis=0, keepdims=True),
                          jnp.sum(dy * x, axis=0, keepdims=True)], axis=0)
    return dx, dw


def _interleave(dst, src):
    seg_rows = src.shape[0] // SUBLANES
    for seg in range(SUBLANES):
        dst[pl.ds(seg, seg_rows, stride=SUBLANES), :] = src[seg * seg_rows:(seg + 1) * seg_rows, :]


def _deinterleave(dst, src):
    seg_rows = src.shape[0] // SUBLANES
    for seg in range(SUBLANES):
        dst[seg * seg_rows:(seg + 1) * seg_rows, :] = src[pl.ds(seg, seg_rows, stride=SUBLANES), :]


def _segment_shift(x, reverse):
    row = lax.broadcasted_iota(jnp.int32, x.shape, 0)
    if reverse:
        return jnp.where(row < SUBLANES - 1, pltpu.roll(x, SUBLANES - 1, 0), 0.0)
    return jnp.where(row >= 1, pltpu.roll(x, 1, 0), 0.0)


def _scan(s_re, s_im, pw_ref, reverse, pair=None):
    n_steps = s_re.shape[0] // SUBLANES
    n_strips = s_re.shape[1] // LANES
    sign = -1.0 if reverse else 1.0
    strips = [slice(st * LANES, (st + 1) * LANES) for st in range(n_strips)]

    def rows_of(j):
        step = (n_steps - 1 - j) if reverse else j
        return pl.ds(pl.multiple_of(step * SUBLANES, SUBLANES), SUBLANES)

    a = [(jnp.broadcast_to(pw_ref[0, 0:1, lanes], (SUBLANES, LANES)),
          sign * jnp.broadcast_to(pw_ref[1, 0:1, lanes], (SUBLANES, LANES))) for lanes in strips]

    def local(i, carry):
        for half in range(2):
            rows = rows_of(2 * i + half)
            out = []
            for st, lanes in enumerate(strips):
                (ar, ai), cr, ci = a[st], carry[2 * st], carry[2 * st + 1]
                xr = s_re[rows, lanes] + (ar * cr - ai * ci)
                xi = s_im[rows, lanes] + (ar * ci + ai * cr)
                s_re[rows, lanes] = xr
                s_im[rows, lanes] = xi
                out += [xr, xi]
            carry = tuple(out)
        return carry

    zero = jnp.zeros((SUBLANES, LANES), F32)
    ends = lax.fori_loop(0, n_steps // 2, local, (zero,) * (2 * n_strips))

    entering = []
    row = lax.broadcasted_iota(jnp.int32, (SUBLANES, LANES), 0)
    for st, lanes in enumerate(strips):
        tr, ti = ends[2 * st], ends[2 * st + 1]
        mr = jnp.broadcast_to(pw_ref[0, n_steps - 1:n_steps, lanes], (SUBLANES, LANES))
        mi = sign * jnp.broadcast_to(pw_ref[1, n_steps - 1:n_steps, lanes], (SUBLANES, LANES))
        for k in (1, 2, 4):
            keep = (row < SUBLANES - k) if reverse else (row >= k)
            rr = jnp.where(keep, pltpu.roll(tr, SUBLANES - k if reverse else k, 0), 0.0)
            ri = jnp.where(keep, pltpu.roll(ti, SUBLANES - k if reverse else k, 0), 0.0)
            tr, ti = tr + (mr * rr - mi * ri), ti + (mr * ri + mi * rr)
            mr, mi = mr * mr - mi * mi, 2.0 * mr * mi
        entering += [_segment_shift(tr, reverse), _segment_shift(ti, reverse)]

    def fix(i, carry):
        carry, sums = carry[:2 * n_strips], carry[2 * n_strips:]
        for half in range(2):
            j = 2 * i + half
            rows = rows_of(j)
            out, acc = [], []
            for st, lanes in enumerate(strips):
                (ar, ai), cr, ci = a[st], carry[2 * st], carry[2 * st + 1]
                cr, ci = ar * cr - ai * ci, ar * ci + ai * cr
                xr = s_re[rows, lanes] + cr
                xi = s_im[rows, lanes] + ci
                s_re[rows, lanes] = xr
                s_im[rows, lanes] = xi
                out += [cr, ci]
                if pair is not None:
                    p_rows = rows_of(jnp.minimum(j + 1, n_steps - 1))
                    keep = (j < n_steps - 1).astype(F32)
                    pr = pair[0][p_rows, lanes] * keep
                    pi = pair[1][p_rows, lanes] * keep
                    acc += [sums[2 * st] + (xr * pr + xi * pi), sums[2 * st + 1] + (xi * pr - xr * pi)]
            carry, sums = tuple(out), tuple(acc)
        return carry + sums

    n_sums = 0 if pair is None else 2 * n_strips
    out = lax.fori_loop(0, n_steps // 2, fix, tuple(entering) + (zero,) * n_sums)
    return out[2 * n_strips:]


def _seq_fwd(proj, conv_w, bc_re, bc_im, cc_re, cc_im, dskip, a_pow, name):
    tp = proj.shape[0]
    dh = proj.shape[1] // 4
    nq = dh // LANES
    sw = STATE * N_GROUPS // nq

    def body(b_ref, c_ref, v_ref, u_ref, w_ref, bre_ref, bim_ref, cre_ref, cim_ref, d_ref, pw_ref,
             co_ref, y_ref, g_ref, s_re, s_im, u_il, y_il):
        co_ref[...] = b_ref[...] * _dwconv(c_ref[...] * v_ref[...], w_ref)
        _interleave(u_il, u_ref)
        ub = u_il[...].astype(BF16)
        s_re[...] = jnp.dot(ub, bre_ref[...], preferred_element_type=F32)
        s_im[...] = jnp.dot(ub, bim_ref[...], preferred_element_type=F32)
        _scan(s_re, s_im, pw_ref, False)
        y_il[...] = (jnp.dot(s_re[...].astype(BF16), cre_ref[...], preferred_element_type=F32)
                     - jnp.dot(s_im[...].astype(BF16), cim_ref[...], preferred_element_type=F32))
        _deinterleave(y_ref, y_il)
        y = y_ref[...] + d_ref[...] * u_ref[...]
        y_ref[...] = y
        g_ref[...] = _gelu(y).astype(BF16)

    col = lambda off: pl.BlockSpec((tp, LANES), lambda q, off=off: (0, off * nq + q))
    blk = pl.BlockSpec((tp, LANES), lambda q: (0, q))
    return pl.pallas_call(
        body, name=name, grid=(nq,),
        in_specs=[col(0), col(1), col(2), col(3),
                  pl.BlockSpec((3, LANES), lambda q: (0, q)),
                  pl.BlockSpec((LANES, sw), lambda q: (0, q)), pl.BlockSpec((LANES, sw), lambda q: (0, q)),
                  pl.BlockSpec((sw, LANES), lambda q: (q, 0)), pl.BlockSpec((sw, LANES), lambda q: (q, 0)),
                  pl.BlockSpec((1, LANES), lambda q: (0, q)),
                  pl.BlockSpec((2, tp // SUBLANES, sw), lambda q: (0, 0, q))],
        out_specs=[blk, blk, blk, pl.BlockSpec((tp, sw), lambda q: (0, q)), pl.BlockSpec((tp, sw), lambda q: (0, q))],
        out_shape=[jax.ShapeDtypeStruct((tp, dh), F32), jax.ShapeDtypeStruct((tp, dh), F32),
                   jax.ShapeDtypeStruct((tp, dh), BF16),
                   jax.ShapeDtypeStruct((tp, nq * sw), F32), jax.ShapeDtypeStruct((tp, nq * sw), F32)],
        scratch_shapes=[pltpu.VMEM((tp, LANES), F32), pltpu.VMEM((tp, LANES), F32)],
        compiler_params=_cparams("parallel"),
    )(proj, proj, proj, proj, conv_w, bc_re, bc_im, cc_re, cc_im, dskip, a_pow)


def _conv_bwd(proj, dco, conv_w, name):
    tp = proj.shape[0]
    dh = proj.shape[1] // 4
    nq = dh // LANES

    def body(b_ref, c_ref, v_ref, dco_ref, w_ref, dproj_ref, dw_ref, stage, sem):
        q = pl.program_id(0)
        cg = c_ref[...]
        vg = v_ref[...]
        cv = cg * vg
        dco_v = dco_ref[...]
        dcv, dw = _dwconv_bwd(cv, dco_v * b_ref[...], w_ref)
        dw_ref[...] = dw
        stage[0] = (dco_v * _dwconv(cv, w_ref)).astype(BF16)
        stage[1] = (dcv * vg).astype(BF16)
        stage[2] = (dcv * cg).astype(BF16)
        copies = [pltpu.make_async_copy(stage.at[p], dproj_ref.at[:, pl.ds((p * nq + q) * LANES, LANES)], sem.at[p])
                  for p in range(3)]
        for cp in copies:
            cp.start()
        for cp in copies:
            cp.wait()

    col = lambda off: pl.BlockSpec((tp, LANES), lambda q, off=off: (0, off * nq + q))
    return pl.pallas_call(
        body, name=name, grid=(nq,),
        in_specs=[col(0), col(1), col(2), pl.BlockSpec((tp, LANES), lambda q: (0, q)),
                  pl.BlockSpec((3, LANES), lambda q: (0, q))],
        out_specs=[pl.BlockSpec(memory_space=pl.ANY), pl.BlockSpec((3, LANES), lambda q: (0, q))],
        out_shape=[jax.ShapeDtypeStruct((tp, 4 * dh), BF16), jax.ShapeDtypeStruct((3, dh), F32)],
        scratch_shapes=[pltpu.VMEM((3, tp, LANES), BF16), pltpu.SemaphoreType.DMA((3,))],
        compiler_params=_cparams("arbitrary"),
    )(proj, proj, proj, dco, conv_w)


def _ssm_bwd(proj, y, dg, dproj, states, bc_re, bc_im, cc_re, cc_im, dskip, a_pow, name):
    tp = proj.shape[0]
    dh = proj.shape[1] // 4
    nq = dh // LANES
    sw = STATE * N_GROUPS // nq

    def body(u_ref, y_ref, dg_ref, dproj_in, s_re, s_im, bre_ref, bim_ref, cre_ref, cim_ref, d_ref, pw_ref,
             dproj_ref, dbre_ref, dbim_ref, dcre_ref, dcim_ref, dd_ref, dar_ref, dai_ref,
             l_re, l_im, a_il, b_il, stage, sem):
        del dproj_in
        q = pl.program_id(0)
        nt = (((1,), (1,)), ((), ()))
        tn = (((0,), (0,)), ((), ()))
        _interleave(a_il, u_ref)
        ub = a_il[...].astype(BF16)
        dy_rows = dg_ref[...] * _gelu_grad(y_ref[...])
        dd_ref[...] = jnp.sum(dy_rows * u_ref[...], axis=0, keepdims=True)
        _interleave(b_il, dy_rows)
        dy = b_il[...]
        dyb = dy.astype(BF16)
        l_re[...] = lax.dot_general(dyb, cre_ref[...], nt, preferred_element_type=F32)
        l_im[...] = -lax.dot_general(dyb, cim_ref[...], nt, preferred_element_type=F32)
        dcre_ref[...] = lax.dot_general(s_re[...].astype(BF16), dyb, tn, preferred_element_type=F32)
        dcim_ref[...] = -lax.dot_general(s_im[...].astype(BF16), dyb, tn, preferred_element_type=F32)
        sums = _scan(l_re, l_im, pw_ref, True, pair=(s_re, s_im))
        rest = tp - SUBLANES
        for st in range(sw // LANES):
            lanes = slice(st * LANES, (st + 1) * LANES)
            lr0, li0 = l_re[:SUBLANES, lanes], l_im[:SUBLANES, lanes]
            pr0, pi0 = _segment_shift(s_re[rest:, lanes], False), _segment_shift(s_im[rest:, lanes], False)
            dar_ref[:, lanes] = jnp.sum(sums[2 * st] + (lr0 * pr0 + li0 * pi0), axis=0, keepdims=True)
            dai_ref[:, lanes] = jnp.sum(sums[2 * st + 1] + (li0 * pr0 - lr0 * pi0), axis=0, keepdims=True)
        lrb = l_re[...].astype(BF16)
        lib = l_im[...].astype(BF16)
        a_il[...] = (dy * d_ref[...] + lax.dot_general(lrb, bre_ref[...], nt, preferred_element_type=F32)
                     + lax.dot_general(lib, bim_ref[...], nt, preferred_element_type=F32))
        _deinterleave(b_il, a_il)
        stage[...] = b_il[...].astype(BF16)
        dbre_ref[...] = lax.dot_general(ub, lrb, tn, preferred_element_type=F32)
        dbim_ref[...] = lax.dot_general(ub, lib, tn, preferred_element_type=F32)
        cp = pltpu.make_async_copy(stage, dproj_ref.at[:, pl.ds((3 * nq + q) * LANES, LANES)], sem)
        cp.start()
        cp.wait()

    blk = pl.BlockSpec((tp, LANES), lambda q: (0, q))
    bspec = pl.BlockSpec((LANES, sw), lambda q: (0, q))
    cspec = pl.BlockSpec((sw, LANES), lambda q: (q, 0))
    tspec = pl.BlockSpec((2, tp // SUBLANES, sw), lambda q: (0, 0, q))
    nstate = STATE * N_GROUPS
    return pl.pallas_call(
        body, name=name, grid=(nq,),
        in_specs=[pl.BlockSpec((tp, LANES), lambda q: (0, 3 * nq + q)), blk, blk, pl.BlockSpec(memory_space=pl.ANY),
                  pl.BlockSpec((tp, sw), lambda q: (0, q)), pl.BlockSpec((tp, sw), lambda q: (0, q)),
                  bspec, bspec, cspec, cspec, pl.BlockSpec((1, LANES), lambda q: (0, q)), tspec],
        out_specs=[pl.BlockSpec(memory_space=pl.ANY), bspec, bspec, cspec, cspec,
                   pl.BlockSpec((1, LANES), lambda q: (0, q)),
                   pl.BlockSpec((1, sw), lambda q: (0, q)), pl.BlockSpec((1, sw), lambda q: (0, q))],
        out_shape=[jax.ShapeDtypeStruct((tp, 4 * dh), BF16),
                   jax.ShapeDtypeStruct((LANES, nstate), F32), jax.ShapeDtypeStruct((LANES, nstate), F32),
                   jax.ShapeDtypeStruct((nstate, LANES), F32), jax.ShapeDtypeStruct((nstate, LANES), F32),
                   jax.ShapeDtypeStruct((1, dh), F32),
                   jax.ShapeDtypeStruct((1, nstate), F32), jax.ShapeDtypeStruct((1, nstate), F32)],
        input_output_aliases={3: 0},
        scratch_shapes=[pltpu.VMEM((tp, sw), F32)] * 2 + [pltpu.VMEM((tp, LANES), F32)] * 2
        + [pltpu.VMEM((tp, LANES), BF16), pltpu.SemaphoreType.DMA],
        compiler_params=_cparams("arbitrary"),
    )(proj, y, dg, dproj, states[0], states[1], bc_re, bc_im, cc_re, cc_im, dskip, a_pow)


FFN_TILE = 256
FFN_ROWS = 32


def _window(x_ref, before, r0, rows, cols):
    if r0 == 0:
        return jnp.concatenate([before, x_ref[0:rows, cols]], axis=0)
    return x_ref[r0 - SUBLANES:r0 + rows, cols]


def _taps(window):
    return window[SUBLANES:], pltpu.roll(window, 1, 0)[SUBLANES:], pltpu.roll(window, 2, 0)[SUBLANES:]


def _conv_taps(taps, w):
    return w[2] * taps[0] + w[1] * taps[1] + w[0] * taps[2]


FFN_MM_ROWS = 544
FFN_MM_COLS = 1408


def _ffn_up_act(hn, w_up, fw, fb, col, others, name):
    tp, dm = hn.shape
    dff = w_up.shape[1] // 2
    tr, cw, rows = FFN_MM_ROWS, FFN_MM_COLS, FFN_ROWS
    nc = dff // cw
    n_others = 0 if others is None else 2

    def body(hn_ref, ma_ref, mv_ref, wa_ref, wv_ref, ba_ref, bv_ref, *rest):
        up_ref, act_ref, tail_ref = rest[n_others:]

        @pl.when(pl.program_id(0) == 0)
        def _():
            tail_ref[...] = jnp.zeros_like(tail_ref)

        x = hn_ref[...]
        up_ref[0] = jnp.dot(x, ma_ref[...], preferred_element_type=F32)
        up_ref[1] = jnp.dot(x, mv_ref[...], preferred_element_type=F32)
        for c0 in range(0, cw, FFN_TILE):
            cols = slice(c0, min(c0 + FFN_TILE, cw))
            wa, wv = [[w_ref[k:k + 1, cols] for k in range(3)] for w_ref in (wa_ref, wv_ref)]
            ba, bv = ba_ref[:, cols], bv_ref[:, cols]
            before_a, before_v = tail_ref[0, :, cols], tail_ref[1, :, cols]
            for r0 in range(0, tr, rows):
                a = _conv_taps(_taps(_window(up_ref.at[0], before_a, r0, rows, cols)), wa) + ba
                v = _conv_taps(_taps(_window(up_ref.at[1], before_v, r0, rows, cols)), wv) + bv
                act_ref[r0:r0 + rows, cols] = (a * _sigmoid(a) * v).astype(BF16)
            tail_ref[:, :, cols] = up_ref[:, tr - SUBLANES:tr, cols]

    par = lambda r, half: pl.BlockSpec((r, cw), lambda i: (0, half * nc + col))
    return pl.pallas_call(
        body, name=name, grid=(tp // tr,),
        in_specs=[pl.BlockSpec((tr, dm), lambda i: (i, 0)), par(dm, 0), par(dm, 1),
                  par(3, 0), par(3, 1), par(1, 0), par(1, 1)] + [pl.BlockSpec(memory_space=pl.ANY)] * n_others,
        out_specs=[pl.BlockSpec((2, tr, cw), lambda i: (0, i, col)), pl.BlockSpec((tr, cw), lambda i: (i, col))],
        out_shape=[jax.ShapeDtypeStruct((2, tp, dff), F32), jax.ShapeDtypeStruct((tp, dff), BF16)],
        input_output_aliases={7: 0, 8: 1} if others is not None else {},
        scratch_shapes=[pltpu.VMEM((2, SUBLANES, cw), F32)],
        compiler_params=_cparams("arbitrary"))(hn, w_up, w_up, fw, fw, fb, fb, *(others or ()))


def _ffn_bwd(up, dh, w_down, fw, fb, name):
    _, tp, dff = up.shape
    two_ff = 2 * dff
    dm = dh.shape[1]
    tr, cw, rows = FFN_MM_ROWS, FFN_MM_COLS, FFN_ROWS
    nr, nc = tp // tr, dff // cw
    n_e = rows + SUBLANES
    pieces = tr // SUBLANES

    def body(ua_ref, uv_ref, pa_ref, pv_ref, dh_ref, wd_ref, wa_ref, wv_ref, ba_ref, bv_ref,
             dup_ref, dwa_ref, dwv_ref, dba_ref, dbv_ref, dact, stage, head_ref, sem):
        j, i = pl.program_id(0), pl.program_id(1)
        step = j * nr + i
        top = i == nr - 1
        sums = ((dwa_ref, dba_ref), (dwv_ref, dbv_ref))

        slot = step % 2

        def out_copies(at):
            r0 = pl.multiple_of((nr - 1 - at % nr) * tr, tr)
            return [pltpu.make_async_copy(
                stage.at[at % 2, s],
                dup_ref.at[pl.ds(r0, tr), pl.ds(pl.multiple_of(s * dff + at // nr * cw, LANES), cw)],
                sem.at[at % 2, s]) for s in range(2)]

        @pl.when(i == 0)
        def _():
            head_ref[...] = jnp.zeros_like(head_ref)
            for dw_ref, db_ref in sums:
                dw_ref[...] = jnp.zeros_like(dw_ref)
                db_ref[...] = jnp.zeros_like(db_ref)

        dact[...] = lax.dot_general(dh_ref[...], wd_ref[...], (((1,), (1,)), ((), ())), preferred_element_type=F32)

        @pl.when(step > 1)
        def _():
            for cp in out_copies(step - 2):
                cp.wait()

        def gate_bwd(taps, dact_v, w, bias):
            a, v = [_conv_taps(taps[s], w[s]) + bias[s] for s in range(2)]
            sg = _sigmoid(a)
            return [dact_v * v * sg * (1.0 + a * (1.0 - sg)), dact_v * a * sg]

        fold = lambda x: sum(x[r:r + SUBLANES] for r in range(0, rows, SUBLANES))
        for c0 in range(0, cw, FFN_TILE):
            cols = slice(c0, min(c0 + FFN_TILE, cw))
            w = [[w_ref[k:k + 1, cols] for k in range(3)] for w_ref in (wa_ref, wv_ref)]
            bias = [ba_ref[:, cols], bv_ref[:, cols]]
            before = [jnp.where(top, 0.0, p_ref[:, cols]) for p_ref in (pa_ref, pv_ref)]
            head = [head_ref[s, :, cols] for s in range(2)]
            piece = jnp.zeros_like(head[0])
            acc = [[piece] * 4 for _ in range(2)]
            for r0 in reversed(range(0, tr, rows)):
                taps = [_taps(_window(x_ref, before[s], r0, rows, cols)) for s, x_ref in enumerate((ua_ref, uv_ref))]
                d = gate_bwd(taps, dact[r0:r0 + rows, cols], w, bias)
                for s in range(2):
                    de = jnp.concatenate([d[s], head[s]], axis=0)
                    dx = (w[s][2] * d[s] + w[s][1] * pltpu.roll(de, n_e - 1, 0)[:rows]
                          + w[s][0] * pltpu.roll(de, n_e - 2, 0)[:rows])
                    stage[slot, s, r0:r0 + rows, cols] = dx.astype(BF16)
                    for k in range(3):
                        acc[s][k] = acc[s][k] + fold(d[s] * taps[s][2 - k])
                    acc[s][3] = acc[s][3] + fold(d[s])
                    head[s] = d[s][:SUBLANES]
            for s, (dw_ref, db_ref) in enumerate(sums):
                head_ref[s, :, cols] = head[s]
                dw_ref[:, cols] = dw_ref[:, cols] + jnp.concatenate(
                    [jnp.sum(x, axis=0, keepdims=True) for x in acc[s][:3]], axis=0)
                db_ref[:, cols] = db_ref[:, cols] + jnp.sum(acc[s][3], axis=0, keepdims=True)

        copies = out_copies(step)
        for cp in copies:
            cp.start()

        @pl.when(step == nc * nr - 1)
        def _():
            for cp in out_copies(step - 1) + copies:
                cp.wait()

    row = lambda i: nr - 1 - i
    main = lambda half: pl.BlockSpec((None, tr, cw), lambda j, i: (half, row(i), j))
    prev = lambda half: pl.BlockSpec((None, SUBLANES, cw), lambda j, i: (half, jnp.maximum(row(i) * pieces - 1, 0), j))
    par = lambda r, half: pl.BlockSpec((r, cw), lambda j, i: (0, half * nc + j))
    acc_spec = lambda r: pl.BlockSpec((r, cw), lambda j, i: (0, j))
    return pl.pallas_call(
        body, name=name, grid=(nc, nr),
        in_specs=[main(0), main(1), prev(0), prev(1),
                  pl.BlockSpec((tr, dm), lambda j, i: (row(i), 0)), pl.BlockSpec((cw, dm), lambda j, i: (j, 0)),
                  par(3, 0), par(3, 1), par(1, 0), par(1, 1)],
        out_specs=[pl.BlockSpec(memory_space=pl.ANY), acc_spec(3), acc_spec(3), acc_spec(1), acc_spec(1)],
        out_shape=[jax.ShapeDtypeStruct((tp, two_ff), BF16),
                   jax.ShapeDtypeStruct((3, dff), F32), jax.ShapeDtypeStruct((3, dff), F32),
                   jax.ShapeDtypeStruct((1, dff), F32), jax.ShapeDtypeStruct((1, dff), F32)],
        scratch_shapes=[pltpu.VMEM((tr, cw), F32), pltpu.VMEM((2, 2, tr, cw), BF16),
                        pltpu.VMEM((2, SUBLANES, cw), F32), pltpu.SemaphoreType.DMA((2, 2))],
        compiler_params=_cparams("arbitrary", "arbitrary"))(up, up, up, up, dh, w_down, fw, fw, fb, fb)


def _zoh(lr, li, ld):
    dt = jnp.exp(ld)
    mag = jnp.exp(lr * dt)
    ang = li * dt
    ar = mag * jnp.cos(ang)
    ai = mag * jnp.sin(ang)
    den = lr * lr + li * li
    nr = ar - 1.0
    fr = (nr * lr + ai * li) / den
    fi = (ai * lr - nr * li) / den
    return dt, ar, ai, den, nr, fr, fi


def _s5_prep(lr, li, ld, b_re, b_im, n_pow, name):
    nstate = lr.shape[1]

    def body(lr_ref, li_ref, ld_ref, bre_ref, bim_ref, pw_ref, bcre_ref, bcim_ref):
        _, ar, ai, _, _, fr, fi = _zoh(lr_ref[...], li_ref[...], ld_ref[...])
        bre = bre_ref[...]
        bim = bim_ref[...]
        bcre_ref[...] = (fr * bre - fi * bim).astype(BF16)
        bcim_ref[...] = (fr * bim + fi * bre).astype(BF16)
        row = lax.broadcasted_iota(jnp.int32, (SUBLANES, nstate), 0)
        pr, pi = jnp.zeros((SUBLANES, nstate), F32), jnp.zeros((SUBLANES, nstate), F32)
        cr, ci = ar, ai
        for t in range(SUBLANES):
            pr, pi = jnp.where(row == t, cr, pr), jnp.where(row == t, ci, pi)
            cr, ci = cr * ar - ci * ai, cr * ai + ci * ar
        pw_ref[0, 0:SUBLANES, :] = pr
        pw_ref[1, 0:SUBLANES, :] = pi
        n = SUBLANES
        while n < n_pow:
            m = min(n, n_pow - n)
            tr, ti = pw_ref[0, n - 1:n, :], pw_ref[1, n - 1:n, :]
            xr, xi = pw_ref[0, 0:m, :], pw_ref[1, 0:m, :]
            pw_ref[0, n:n + m, :] = xr * tr - xi * ti
            pw_ref[1, n:n + m, :] = xr * ti + xi * tr
            n += m

    vmem = pl.BlockSpec(memory_space=pltpu.VMEM)
    return pl.pallas_call(
        body, name=name, in_specs=[vmem] * 5, out_specs=[vmem] * 3,
        out_shape=[jax.ShapeDtypeStruct((2, n_pow, nstate), F32)] + [jax.ShapeDtypeStruct(b_re.shape, BF16)] * 2,
        compiler_params=pltpu.CompilerParams(vmem_limit_bytes=VMEM_LIMIT))(lr, li, ld, b_re, b_im)


def _s5_prep_bwd(lr, li, ld, b_re, b_im, da_re, da_im, dbc_re, dbc_im, name):
    def body(lr_ref, li_ref, ld_ref, bre_ref, bim_ref, dar_ref, dai_ref, dbcre_ref, dbcim_ref,
             dlr_ref, dli_ref, dld_ref, dbre_ref, dbim_ref):
        lr, li = lr_ref[...], li_ref[...]
        dt, ar, ai, den, nr, fr, fi = _zoh(lr, li, ld_ref[...])
        bre, bim = bre_ref[...], bim_ref[...]
        gre, gim = dbcre_ref[...], dbcim_ref[...]
        dbre_ref[...] = fr * gre + fi * gim
        dbim_ref[...] = fr * gim - fi * gre
        g_fr = jnp.sum(gre * bre + gim * bim, axis=0, keepdims=True)
        g_fi = jnp.sum(gim * bre - gre * bim, axis=0, keepdims=True)
        g_ar = dar_ref[...] + (g_fr * lr - g_fi * li) / den
        g_ai = dai_ref[...] + (g_fr * li + g_fi * lr) / den
        d_lr = (g_fr * (nr - 2.0 * fr * lr) + g_fi * (ai - 2.0 * fi * lr)) / den
        d_li = (g_fr * (ai - 2.0 * fr * li) - g_fi * (nr + 2.0 * fi * li)) / den
        g_logmag = g_ar * ar + g_ai * ai
        g_ang = g_ai * ar - g_ar * ai
        dlr_ref[...] = d_lr + g_logmag * dt
        dli_ref[...] = d_li + g_ang * dt
        d_ld = (g_logmag * lr + g_ang * li) * dt
        n = d_ld.shape[1]
        sh = 1
        while sh < STATE:
            d_ld = d_ld + pltpu.roll(d_ld, n - sh, 1)
            sh *= 2
        dld_ref[...] = d_ld

    vmem = pl.BlockSpec(memory_space=pltpu.VMEM)
    row = jax.ShapeDtypeStruct(lr.shape, F32)
    return pl.pallas_call(
        body, name=name, in_specs=[vmem] * 9, out_specs=[vmem] * 5,
        out_shape=[row, row, row, jax.ShapeDtypeStruct(b_re.shape, F32), jax.ShapeDtypeStruct(b_re.shape, F32)],
    )(lr, li, ld, b_re, b_im, da_re, da_im, dbc_re, dbc_im)


def _compact_b(bb):
    bq = bb.reshape(N_GROUPS // 8, 8, STATE, GROUP)
    m = jnp.einsum("ab,qbph->qahbp", jnp.eye(8, dtype=bb.dtype), bq).reshape(N_GROUPS // 8, LANES, 8 * STATE)
    return m.transpose(1, 0, 2).reshape(LANES, N_GROUPS * STATE)


def _expand_b(m):
    d = m.reshape(8, GROUP, N_GROUPS // 8, 8, STATE)
    return jnp.einsum("ahqap->qahp", d).reshape(N_GROUPS, GROUP, STATE)


def _compact_c(c):
    cq = c.reshape(N_GROUPS // 8, 8, GROUP, STATE)
    return jnp.einsum("ab,qbhp->qbpah", jnp.eye(8, dtype=c.dtype), cq).reshape(N_GROUPS * STATE, LANES)


def _expand_c(m):
    d = m.reshape(N_GROUPS // 8, 8, STATE, 8, GROUP)
    return jnp.einsum("qbpbh->qbhp", d).reshape(N_GROUPS, GROUP, STATE)


def _local_step(x, target, p, ex):
    seq, d = x.shape
    n_real = N_META + seq
    tp = -(-n_real // ROW_ALIGN) * ROW_ALIGN

    h0, hn1 = _input_norm_fwd(x, p["meta_tokens"], p["norm_mix_g"] + ex.zero, tp, "norm_mix")
    ex.forward("first", hn1)
    nstate = N_GROUPS * STATE
    s5 = (p["ssm_lam_re"].reshape(1, nstate), p["ssm_lam_im"].reshape(1, nstate),
          jnp.repeat(p["ssm_log_dt"].reshape(-1), STATE).reshape(1, nstate),
          _compact_b(p["ssm_b_re"]), _compact_b(p["ssm_b_im"]))
    a_pow, bc_re, bc_im = _s5_prep(*s5, tp // SUBLANES, "s5_prep")
    cc_re = _compact_c(p["ssm_c_re"]).astype(BF16)
    cc_im = _compact_c(p["ssm_c_im"]).astype(BF16)
    dskip = p["ssm_d"].reshape(1, -1)
    first = ex.weights("first", bc_re)
    proj = _mm(hn1, first["w_in"], "nn", "proj")
    started = ex.forward("mid", proj)
    co, y, g, *states = _seq_fwd(proj, p["conv_w"] + started[0, 0], bc_re, bc_im, cc_re, cc_im, dskip, a_pow,
                                 "seq_fwd")
    mid = ex.weights("mid", g)
    z = _mm(g, mid["ssm_w_glu"], "nn", "glu")
    mixed = _mix_fwd(co, y, z, p["gain_conv_out"], p["gain_ssm_out"], "mix_fwd")
    started = ex.forward("up", mixed)
    h1, hn2 = _proj_res_norm(mixed, mid["w_out"], h0, p["norm_ffn_g"], started, "out_proj_norm")
    late = ex.weights("up", hn2)
    part, fw = None, p["ffn_conv_w"]
    for col in range(late["w_up"].shape[1] // (2 * FFN_MM_COLS)):
        part = _ffn_up_act(hn2, late["w_up"], fw, p["ffn_conv_b"], col, part, "ffn_up_act_%d" % col)
        if col == 0:
            fw = fw + ex.forward("down", part[1])[0, 0]
    up, act = part
    late.update(ex.weights("down", act))
    loss, dh2, dh2b, d_gfin = _proj_loss_bwd(act, late["w_down"], h1, target, p["norm_final_g"], n_real,
                                             "down_proj_loss")

    g_w_down = _mm(act, dh2b, "tn", "g_w_down")
    dup, dfw_a, dfw_v, dfb_a, dfb_v = _ffn_bwd(up, dh2b, late["w_down"], p["ffn_conv_w"], p["ffn_conv_b"], "ffn_bwd")
    g_w_up = _mm(hn2, dup, "tn", "g_w_up")
    started = ex.grads_ready("late", {"w_up": g_w_up, "w_down": g_w_down})
    dh1, dh1b, d_gffn = _proj_norm_bwd(dup, late["w_up"], h1, p["norm_ffn_g"], dh2, started, "d_hn2_norm_bwd")
    started = ex.grads_send("late", dh1)
    g_w_out = _mm(mixed, dh1b, "tn", "g_w_out", after=started)
    dco, dz, dgp, d_gc, d_gs = _proj_mix_bwd(dh1b, mid["w_out"], co, y, z, p["gain_conv_out"],
                                             p["gain_ssm_out"], "d_mixed_mix_bwd")
    g_w_glu = _mm(g, dz, "tn", "g_w_glu")
    started = ex.grads_ready("mid", {"ssm_w_glu": g_w_glu, "w_out": g_w_out})
    dg = _mm(dz, mid["ssm_w_glu"], "nt", "d_gelu", acc_in=dgp, after=started)
    started = ex.grads_send("mid", dg)
    dproj, d_conv_w = _conv_bwd(proj, dco, p["conv_w"] + started[0, 0], "conv_bwd")
    (dproj, dbc_re, dbc_im, dcc_re, dcc_im, d_dskip, da_re, da_im) = _ssm_bwd(
        proj, y, dg, dproj, states, bc_re, bc_im, cc_re, cc_im, dskip, a_pow, "ssm_bwd")
    g_w_in = _mm(hn1, dproj, "tn", "g_w_in")
    started = ex.grads_ready("first", {"w_in": g_w_in})
    grad_x, d_meta, d_gmix = _proj_input_norm_bwd(dproj, first["w_in"], h0, p["norm_mix_g"], dh1, started, n_real,
                                                  "d_hn1_norm_bwd")
    started = ex.grads_send("first", d_gmix)

    d_lam_re, d_lam_im, d_log_dt, d_b_re, d_b_im = _s5_prep_bwd(*s5, da_re, da_im, dbc_re, dbc_im, "s5_prep_bwd")
    d_lam_re, d_lam_im = d_lam_re.reshape(N_GROUPS, STATE), d_lam_im.reshape(N_GROUPS, STATE)
    d_log_dt = d_log_dt[0, ::STATE]
    d_b_re, d_b_im = _expand_b(d_b_re), _expand_b(d_b_im)
    grads = {
        "meta_tokens": d_meta, "norm_mix_g": d_gmix, "w_in": g_w_in, "conv_w": d_conv_w,
        "ssm_lam_re": d_lam_re, "ssm_lam_im": d_lam_im, "ssm_log_dt": d_log_dt,
        "ssm_b_re": d_b_re, "ssm_b_im": d_b_im, "ssm_c_re": _expand_c(dcc_re), "ssm_c_im": _expand_c(dcc_im),
        "ssm_d": d_dskip.reshape(N_GROUPS, GROUP), "ssm_w_glu": g_w_glu,
        "gain_conv_out": d_gc, "gain_ssm_out": d_gs, "w_out": g_w_out, "norm_ffn_g": d_gffn,
        "w_up": g_w_up, "ffn_conv_w": jnp.concatenate([dfw_a, dfw_v], axis=1),
        "ffn_conv_b": jnp.concatenate([dfb_a, dfb_v], axis=1), "w_down": g_w_down, "norm_final_g": d_gfin,
    }
    return loss[0, 0] + started[0, 0], grad_x, grads


def _view(ref, axis, start, size):
    idx = [slice(None)] * len(ref.shape)
    idx[axis] = pl.ds(start, size)
    return ref.at[tuple(idx)]


def _exchange(name, ins, outs, aliases, local_copies, remote_copies):
    ni, no = len(ins), len(outs)
    nl, nr = len(local_copies), len(remote_copies)

    def body(*refs):
        in_refs, out_refs = refs[:ni], refs[ni:ni + no]
        send_sems, recv_sems, local_sems = refs[ni + no:]
        x, y, c = lax.axis_index("x"), lax.axis_index("y"), lax.axis_index("c")
        pos = (x, y, c, 2 * x + y)
        locals_ = [pltpu.make_async_copy(s(in_refs, out_refs, pos), d(in_refs, out_refs, pos), local_sems.at[i])
                   for i, (s, d) in enumerate(local_copies)]
        remotes = []
        for i, (s, d, flip) in enumerate(remote_copies):
            peer = (1 - x if "x" in flip else x, 1 - y if "y" in flip else y, 1 - c if "c" in flip else c)
            remotes.append(pltpu.make_async_remote_copy(
                src_ref=s(in_refs, out_refs, pos), dst_ref=d(in_refs, out_refs, pos),
                send_sem=send_sems.at[i], recv_sem=recv_sems.at[i], device_id=peer, device_id_type=MESH))
        for cp in locals_ + remotes:
            cp.start()
        for cp in remotes:
            cp.wait_recv()
        for cp in remotes:
            cp.wait_send()
        for cp in locals_:
            cp.wait()

    hbm = pl.BlockSpec(memory_space=pl.ANY)
    return pl.pallas_call(
        body, name=name, in_specs=[hbm] * ni, out_specs=[hbm] * no, out_shape=outs,
        input_output_aliases=aliases,
        scratch_shapes=[pltpu.SemaphoreType.DMA((nr,)), pltpu.SemaphoreType.DMA((nr,)),
                        pltpu.SemaphoreType.DMA((max(nl, 1),))],
    )(*ins)


BIG = {"w_in": (0, 1), "ssm_w_glu": (1, 0), "w_out": (1, 0), "w_up": (0, 1), "w_down": (1, 0)}
BIG_NAMES = tuple(BIG)
FLIPS = ("y", "x", "xy")


def _peer_chip(pos, flip):
    x, y, _, _ = pos
    return 2 * (1 - x if "x" in flip else x) + (1 - y if "y" in flip else y)


def _block_rows(rows, cols, itemsize, mult):
    return _pick_tile(rows, max(mult, (2 * 1024 * 1024) // (cols * itemsize)), mult)


def _cast_into_full(w, kc, shard_axis, name):
    r, cdim = w.shape
    tr = _block_rows(r, cdim, 4, 16)
    nb = r // tr

    def body(kc_ref, w_ref, o_ref):
        o_ref[...] = w_ref[...].astype(BF16)

    if shard_axis == 1:
        full, o_spec = (r, 4 * cdim), pl.BlockSpec((tr, cdim), lambda i, kc: (i, kc[0]))
    else:
        full, o_spec = (4 * r, cdim), pl.BlockSpec((tr, cdim), lambda i, kc: (kc[0] * nb + i, 0))
    return pl.pallas_call(
        body, name=name,
        grid_spec=pltpu.PrefetchScalarGridSpec(
            num_scalar_prefetch=1, grid=(nb,), in_specs=[pl.BlockSpec((tr, cdim), lambda i, kc: (i, 0))],
            out_specs=o_spec),
        out_shape=jax.ShapeDtypeStruct(full, BF16), compiler_params=_cparams("parallel"))(kc, w)


def _pair_sum(g, recv, kc, half_axis, name, out_dtype):
    hr, hc = recv.shape
    tr = _block_rows(hr, hc, 4, 16)
    nb = hr // tr

    def body(kc_ref, g_ref, r_ref, o_ref):
        o_ref[...] = (g_ref[...] + r_ref[...]).astype(out_dtype)

    if half_axis == 0:
        g_spec = pl.BlockSpec((tr, hc), lambda i, kc: (kc[1] * nb + i, 0))
    elif half_axis == 1:
        g_spec = pl.BlockSpec((tr, hc), lambda i, kc: (i, kc[1]))
    else:
        g_spec = pl.BlockSpec((tr, hc), lambda i, kc: (i, 0))
    same = pl.BlockSpec((tr, hc), lambda i, kc: (i, 0))
    return pl.pallas_call(
        body, name=name,
        grid_spec=pltpu.PrefetchScalarGridSpec(num_scalar_prefetch=1, grid=(nb,), in_specs=[g_spec, same],
                                               out_specs=same),
        out_shape=jax.ShapeDtypeStruct((hr, hc), out_dtype), compiler_params=_cparams("parallel"))(kc, g, recv)


def _chip_sum(own, recv, kc, own_axis, out_axis, name):
    _, sr, sc = recv.shape
    tr = _block_rows(sr, sc, 4, 16)
    nb = sr // tr

    def body(kc_ref, o_ref, r_ref, t_ref):
        k = kc_ref[0]
        own_v = o_ref[...].astype(F32)
        r = [r_ref[m].astype(F32) for m in range(3)]
        terms = []
        for kk in range(4):
            m = jnp.bitwise_xor(k, kk)
            terms.append(jnp.where(m == 0, own_v, jnp.where(m == 1, r[0], jnp.where(m == 2, r[1], r[2]))))
        t_ref[...] = (terms[0] + terms[1]) + (terms[2] + terms[3])

    if own_axis == 0:
        own_spec = pl.BlockSpec((tr, sc), lambda i, kc: (kc[0] * nb + i, 0))
    elif own_axis == 1:
        own_spec = pl.BlockSpec((tr, sc), lambda i, kc: (i, kc[0]))
    else:
        own_spec = pl.BlockSpec((tr, sc), lambda i, kc: (kc[1] * nb + i, 0))
    if out_axis == 0:
        out_full, out_spec = (2 * sr, sc), pl.BlockSpec((tr, sc), lambda i, kc: (kc[1] * nb + i, 0))
    else:
        out_full, out_spec = (sr, 2 * sc), pl.BlockSpec((tr, sc), lambda i, kc: (i, kc[1]))
    return pl.pallas_call(
        body, name=name,
        grid_spec=pltpu.PrefetchScalarGridSpec(
            num_scalar_prefetch=1, grid=(nb,),
            in_specs=[own_spec, pl.BlockSpec((3, tr, sc), lambda i, kc: (0, i, 0))],
            out_specs=out_spec),
        out_shape=jax.ShapeDtypeStruct(out_full, F32), compiler_params=_cparams("parallel"))(kc, own, recv)


def _adamw(w, g, m, v, name):
    r, cdim = w.shape
    tr = _block_rows(r, cdim, 4, 8)
    c1 = 1.0 - ADAM_B1 ** ADAM_STEP
    c2 = 1.0 - ADAM_B2 ** ADAM_STEP

    def body(w_ref, g_ref, m_ref, v_ref, go_ref, d_ref, nm_ref, nv_ref):
        gv = g_ref[...]
        go_ref[...] = gv
        nm = ADAM_B1 * m_ref[...] + (1.0 - ADAM_B1) * gv
        nv = ADAM_B2 * v_ref[...] + (1.0 - ADAM_B2) * (gv * gv)
        d_ref[...] = -ADAM_LR * ((nm / c1) / (jnp.sqrt(nv / c2) + ADAM_EPS) + ADAM_WD * w_ref[...])
        nm_ref[...] = nm
        nv_ref[...] = nv

    spec = _rows(cdim, tr)
    return pl.pallas_call(body, name=name, grid=(r // tr,), in_specs=[spec] * 4, out_specs=[spec] * 4,
                          out_shape=[jax.ShapeDtypeStruct((r, cdim), F32)] * 4,
                          compiler_params=_cparams("parallel"))(w, g, m, v)


def _adamw_whole(ws, gs, ms, vs, name):
    n = len(ws)
    c1 = 1.0 - ADAM_B1 ** ADAM_STEP
    c2 = 1.0 - ADAM_B2 ** ADAM_STEP

    def body(*refs):
        for i in range(n):
            w_ref, g_ref, m_ref, v_ref, d_ref, nm_ref, nv_ref = [refs[j * n + i] for j in range(7)]
            gv = g_ref[...]
            nm = ADAM_B1 * m_ref[...] + (1.0 - ADAM_B1) * gv
            nv = ADAM_B2 * v_ref[...] + (1.0 - ADAM_B2) * (gv * gv)
            d_ref[...] = -ADAM_LR * ((nm / c1) / (jnp.sqrt(nv / c2) + ADAM_EPS) + ADAM_WD * w_ref[...])
            nm_ref[...] = nm
            nv_ref[...] = nv

    vmem = pl.BlockSpec(memory_space=pltpu.VMEM)
    out = pl.pallas_call(body, name=name, in_specs=[vmem] * (4 * n), out_specs=[vmem] * (3 * n),
                         out_shape=[jax.ShapeDtypeStruct(a.shape, F32) for a in ws] * 3,
                         compiler_params=pltpu.CompilerParams(vmem_limit_bytes=VMEM_LIMIT))(*ws, *gs, *ms, *vs)
    return out[:n], out[n:2 * n], out[2 * n:]


SIDE_EFFECT = pltpu.SideEffectType.DATAFLOW_SIDE_EFFECTING


def _descriptors(copies, refs, send_sems, recv_sems, sem_off=0):
    x, y, c = lax.axis_index("x"), lax.axis_index("y"), lax.axis_index("c")
    pos = (x, y, c, 2 * x + y)
    out = []
    for i, (s, d, flip) in enumerate(copies):
        peer = (1 - x if "x" in flip else x, 1 - y if "y" in flip else y, 1 - c if "c" in flip else c)
        out.append(pltpu.make_async_remote_copy(
            src_ref=s(refs, refs, pos), dst_ref=d(refs, refs, pos),
            send_sem=send_sems.at[sem_off + i], recv_sem=recv_sems.at[sem_off + i],
            device_id=peer, device_id_type=MESH))
    return out


def _shifted(copies, off):
    return [(lambda I, O, pos, s=s: s(I[off:], O[off:], pos), lambda I, O, pos, d=d: d(I[off:], O[off:], pos), flip)
            for s, d, flip in copies]


BARRIER_IDS = {"c": (1, 2), "ici": (3, 4)}


def _exchange_start(name, bufs, copies, turns, after=None):
    n, nr = len(bufs), len(copies)
    na = 0 if after is None else 1
    flips = sorted({flip for _, _, flip in copies})
    kind = "c" if flips == ["c"] else "ici"
    collective_id = BARRIER_IDS[kind][turns[kind] % 2]
    turns[kind] += 1

    def body(*refs):
        x, y, c = lax.axis_index("x"), lax.axis_index("y"), lax.axis_index("c")
        barrier = pltpu.get_barrier_semaphore()
        for flip in flips:
            peer = (1 - x if "x" in flip else x, 1 - y if "y" in flip else y, 1 - c if "c" in flip else c)
            pl.semaphore_signal(barrier, inc=1, device_id=peer, device_id_type=MESH)
        pl.semaphore_wait(barrier, len(flips))
        for cp in _descriptors(copies, refs[:n], refs[n + na], refs[n + na + 1]):
            cp.start()
        token = refs[2 * n + na + 2]
        token[...] = jnp.zeros_like(token)

    hbm = pl.BlockSpec(memory_space=pltpu.HBM)
    sem = pl.BlockSpec(memory_space=pltpu.SEMAPHORE)
    out = pl.pallas_call(
        body, name=name,
        in_specs=[hbm] * n + [pl.BlockSpec(memory_space=pl.ANY)] * na,
        out_specs=(sem, sem, *[hbm] * n, pl.BlockSpec(memory_space=pltpu.VMEM)),
        out_shape=(pltpu.SemaphoreType.DMA((nr,)), pltpu.SemaphoreType.DMA((nr,)),
                   *[pltpu.HBM(b.shape, b.dtype) for b in bufs], jax.ShapeDtypeStruct((SUBLANES, LANES), F32)),
        input_output_aliases={i: 2 + i for i in range(n)},
        compiler_params=pltpu.CompilerParams(has_side_effects=SIDE_EFFECT, collective_id=collective_id),
    )(*[pltpu.with_memory_space_constraint(b, pltpu.HBM) for b in bufs], *([after] * na))
    return out[0], out[1], list(out[2:2 + n]), out[2 + n]


def _exchange_wait(name, send_sems, recv_sems, bufs, copies, after, sem_off=0):
    n = len(bufs)

    def body(*refs):
        for cp in _descriptors(copies, refs[:n], refs[n], refs[n + 1], sem_off):
            cp.wait_send()
            cp.wait_recv()

    hbm = pl.BlockSpec(memory_space=pltpu.HBM)
    sem = pl.BlockSpec(memory_space=pltpu.SEMAPHORE)
    out = pl.pallas_call(
        body, name=name,
        in_specs=[hbm] * n + [sem, sem, pl.BlockSpec(memory_space=pl.ANY)],
        out_specs=tuple([hbm] * n),
        out_shape=tuple(pltpu.HBM(b.shape, b.dtype) for b in bufs),
        input_output_aliases={i: i for i in range(n)},
        compiler_params=pltpu.CompilerParams(has_side_effects=SIDE_EFFECT),
    )(*bufs, send_sems, recv_sems, after)
    return list(out)


FIRST = ("w_in",)
MID = ("ssm_w_glu", "w_out")
LATE = ("w_up", "w_down")
GROUPS = {"first": FIRST, "mid": MID, "late": LATE}
ARRIVALS = {"first": FIRST, "mid": MID, "up": ("w_up",), "down": ("w_down",)}


def _gather_copies(names, shard_shapes):
    def region(i, chip, c):
        half_axis, shard_axis = BIG[names[i]]
        ssize = shard_shapes[i][shard_axis]
        hsize = shard_shapes[i][half_axis] // 2
        return lambda ref: _view(_view(ref, shard_axis, chip * ssize, ssize), half_axis, c * hsize, hsize)

    ici, d2d = [], []
    for i in range(len(names)):
        for flip in FLIPS:
            ici.append((lambda I, O, pos, i=i: region(i, pos[3], pos[2])(I[i]),
                        lambda I, O, pos, i=i: region(i, pos[3], pos[2])(O[i]), flip))
            d2d.append((lambda I, O, pos, i=i, flip=flip: region(i, _peer_chip(pos, flip), pos[2])(I[i]),
                        lambda I, O, pos, i=i, flip=flip: region(i, _peer_chip(pos, flip), pos[2])(O[i]), "c"))
    return ici, d2d


def _half_shape(n, shape):
    r, cdim = shape
    return (r // 2, cdim) if BIG[n][0] == 0 else (r, cdim // 2)


def _sub_shape(n, shape):
    hr, hc = _half_shape(n, shape)
    return (hr, hc // 4) if BIG[n][1] == 1 else (hr // 4, hc)


def _pair_copies(names, shapes, with_pack, dst_off):
    n = len(names)

    def other_half(i, ref, pos):
        half_axis = BIG[names[i]][0]
        hsize = shapes[i][half_axis] // 2
        return _view(ref, half_axis, (1 - pos[2]) * hsize, hsize)

    copies = [(lambda I, O, pos, i=i: other_half(i, I[i], pos), lambda I, O, pos, i=i: O[dst_off + i], "c")
              for i in range(n)]
    if with_pack:
        copies.append((lambda I, O, pos: I[n], lambda I, O, pos: O[dst_off + n], "c"))
    return copies


def _chip_copies(names, shapes, pack_rows, dst_off):
    n = len(names)

    def piece(i, ref, chip):
        shard_axis = BIG[names[i]][1]
        ssize = _sub_shape(names[i], shapes[i])[shard_axis]
        return _view(ref, shard_axis, chip * ssize, ssize)

    copies = []
    for i in range(n):
        for slot, flip in enumerate(FLIPS):
            copies.append((lambda I, O, pos, i=i, flip=flip: piece(i, I[i], _peer_chip(pos, flip)),
                           lambda I, O, pos, i=i, slot=slot: O[dst_off + i].at[slot], flip))
    if pack_rows:
        for slot, flip in enumerate(FLIPS):
            copies.append((lambda I, O, pos: _view(I[n], 0, pos[2] * (pack_rows // 2), pack_rows // 2),
                           lambda I, O, pos, slot=slot: O[dst_off + n].at[slot], flip))
    return copies


class _Exchanges:
    def __init__(self, shards, tiny, kc):
        self.kc = kc
        wb = {n: _cast_into_full(shards[n], kc, BIG[n][1], "cast_" + n) for n in BIG_NAMES}
        self.gathering, self.forwarding, self.pairing, self.reducing = {}, {}, {}, {}
        self.turns = {"c": 0, "ici": 0}
        tiny_copies = [(lambda I, O, pos: I[0], lambda I, O, pos: O[1].at[pos[3]], flip) for flip in FLIPS]
        self.gathering["tiny"] = (0, 0, 2, tiny_copies, None)
        bufs, copies = [tiny, lax.empty((4,) + tiny.shape, F32)], list(tiny_copies)
        for group, names in ARRIVALS.items():
            ici, d2d = _gather_copies(names, [shards[n].shape for n in names])
            self.gathering[group] = (len(bufs), len(copies), len(names), ici, d2d)
            copies += _shifted(ici, len(bufs))
            bufs += [wb[n] for n in names]
        self.started = _exchange_start("gather_start", bufs, copies, self.turns)
        self.zero = self.started[3][0, 0]

    def _arrived(self, group, after):
        buf_off, sem_off, n, ici, _ = self.gathering[group]
        send_sems, recv_sems, bufs, _ = self.started
        return _exchange_wait("gather_%s_wait" % group, send_sems, recv_sems, bufs[buf_off:buf_off + n], ici, after,
                              sem_off)

    def small_params(self, kc):
        tiny, got = self._arrived("tiny", self.started[3])
        return lax.dynamic_update_index_in_dim(got, tiny, kc[0], 0)

    def forward(self, group, after):
        d2d = self.gathering[group][4]
        self.forwarding[group] = (_exchange_start("forward_%s_start" % group, self._arrived(group, after), d2d,
                                                  self.turns), d2d)
        return self.forwarding[group][0][3]

    def weights(self, group, after):
        if group not in self.forwarding:
            after = self.forward(group, after)
        (send_sems, recv_sems, bufs, _), d2d = self.forwarding[group]
        full = _exchange_wait("forward_%s_wait" % group, send_sems, recv_sems, bufs, d2d, after)
        return dict(zip(ARRIVALS[group], full))

    def grads_ready(self, group, grads):
        names = GROUPS[group]
        gs = [grads[n] for n in names]
        land = [lax.empty(_half_shape(n, g.shape), F32) for n, g in zip(names, gs)]
        copies = _pair_copies(names, [g.shape for g in gs], False, len(names))
        started = _exchange_start("pair_%s_start" % group, gs + land, copies, self.turns)
        self.pairing[group] = (started, copies)
        return started[3]

    def grads_send(self, group, after):
        names = GROUPS[group]
        n = len(names)
        (send_sems, recv_sems, bufs, _), copies = self.pairing[group]
        bufs = _exchange_wait("pair_%s_wait" % group, send_sems, recv_sems, bufs, copies, after)
        chip = [_pair_sum(bufs[i], bufs[n + i], self.kc, BIG[names[i]][0], "pair_sum_" + names[i], BF16)
                for i in range(n)]
        shapes = [bufs[i].shape for i in range(n)]
        land = [lax.empty((3,) + _sub_shape(names[i], shapes[i]), BF16) for i in range(n)]
        copies = _chip_copies(names, shapes, 0, n)
        started = _exchange_start("reduce_%s_start" % group, chip + land, copies, self.turns)
        self.reducing[group] = (started, copies)
        return started[3]

    def finish_pack(self, pack):
        kc = self.kc
        prow = pack.shape[0] // 2
        recv = _exchange("reduce_d2d", [pack], [jax.ShapeDtypeStruct(pack.shape, F32)], {}, [],
                         _pair_copies((), [], True, 0))
        chip_pack = _pair_sum(pack, recv[0], kc, None, "pair_sum_pack", F32)
        copies = _chip_copies((), [], pack.shape[0], 1)
        land = lax.empty((3, prow, pack.shape[1]), F32)
        pack_sems_s, pack_sems_r, pack_bufs, after = _exchange_start("reduce_pack_start", [chip_pack, land], copies,
                                                                     self.turns)

        names, chips, recvs = (), [], []
        for group, group_names in GROUPS.items():
            (send_sems, recv_sems, bufs, _), group_copies = self.reducing[group]
            bufs = _exchange_wait("reduce_%s_wait" % group, send_sems, recv_sems, bufs, group_copies, after)
            n = len(group_names)
            names, chips, recvs = names + group_names, chips + bufs[:n], recvs + bufs[n:]
            after = bufs[n]
        total = [_chip_sum(chips[i], recvs[i], kc, BIG[n][1], BIG[n][0], "chip_sum_" + n)
                 for i, n in enumerate(names)]

        def my_half(half_axis, ref, pos):
            hsize = ref.shape[half_axis] // 2
            return _view(ref, half_axis, pos[2] * hsize, hsize)

        swap = [(lambda I, O, pos, i=i, n=n: my_half(BIG[n][0], I[i], pos),
                 lambda I, O, pos, i=i, n=n: my_half(BIG[n][0], O[i], pos), "c") for i, n in enumerate(names)]
        self.swapping = (_exchange_start("swap_start", total, swap, self.turns), swap, names)

        chip_pack, recv_pack = _exchange_wait("reduce_pack_wait", pack_sems_s, pack_sems_r, pack_bufs, copies,
                                              self.swapping[0][3])
        total_pack = _chip_sum(chip_pack, recv_pack, kc, None, 0, "chip_sum_pack")
        swap = [(lambda I, O, pos: my_half(0, I[0], pos), lambda I, O, pos: my_half(0, O[0], pos), "c")]
        return _exchange("swap_pack", [total_pack], [jax.ShapeDtypeStruct(pack.shape, F32)], {0: 0}, [], swap)[0]

    def finish_big(self, after):
        (send_sems, recv_sems, bufs, _), swap, names = self.swapping
        return dict(zip(names, _exchange_wait("swap_wait", send_sems, recv_sems, bufs, swap, after)))


WEIGHTS = ("meta_tokens", "norm_mix_g", "w_in", "conv_w", "ssm_lam_re", "ssm_lam_im", "ssm_log_dt", "ssm_b_re",
           "ssm_b_im", "ssm_c_re", "ssm_c_im", "ssm_d", "ssm_w_glu", "gain_conv_out", "gain_ssm_out", "w_out",
           "norm_ffn_g", "w_up", "ffn_conv_w", "ffn_conv_b", "w_down", "norm_final_g")
TINY_SHARDED = ("meta_tokens", "conv_w", "ffn_conv_w")
REPLICATED = tuple(n for n in WEIGHTS if n not in BIG and n not in TINY_SHARDED)
PACK_COLS = 512


def _pack(arrays, row_mult, cols):
    flat = jnp.concatenate([a.reshape(-1).astype(F32) for a in arrays])
    n = flat.shape[0]
    total = -(-n // (row_mult * cols)) * (row_mult * cols)
    return jnp.concatenate([flat, jnp.zeros((total - n,), F32)]).reshape(total // cols, cols)


def _unpack(packed, shapes):
    flat = packed.reshape(-1)
    out, off = [], 0
    for s in shapes:
        n = math.prod(s)
        out.append(flat[off:off + n].reshape(s))
        off += n
    return out


def kernel(x, meta_tokens, norm_mix_g, w_in, conv_w, ssm_lam_re, ssm_lam_im, ssm_log_dt, ssm_b_re, ssm_b_im, ssm_c_re, ssm_c_im, ssm_d, ssm_w_glu, gain_conv_out, gain_ssm_out, w_out, norm_ffn_g, w_up, ffn_conv_w, ffn_conv_b, w_down, norm_final_g, loss_target, m_meta_tokens, m_norm_mix_g, m_w_in, m_conv_w, m_ssm_lam_re, m_ssm_lam_im, m_ssm_log_dt, m_ssm_b_re, m_ssm_b_im, m_ssm_c_re, m_ssm_c_im, m_ssm_d, m_ssm_w_glu, m_gain_conv_out, m_gain_ssm_out, m_w_out, m_norm_ffn_g, m_w_up, m_ffn_conv_w, m_ffn_conv_b, m_w_down, m_norm_final_g, v_meta_tokens, v_norm_mix_g, v_w_in, v_conv_w, v_ssm_lam_re, v_ssm_lam_im, v_ssm_log_dt, v_ssm_b_re, v_ssm_b_im, v_ssm_c_re, v_ssm_c_im, v_ssm_d, v_ssm_w_glu, v_gain_conv_out, v_gain_ssm_out, v_w_out, v_norm_ffn_g, v_w_up, v_ffn_conv_w, v_ffn_conv_b, v_w_down, v_norm_final_g):
    args = dict(locals())
    w = {n: args[n] for n in WEIGHTS}
    mom = {n: args["m_" + n] for n in WEIGHTS}
    var = {n: args["v_" + n] for n in WEIGHTS}
    kx, ky, kc_ = lax.axis_index("x"), lax.axis_index("y"), lax.axis_index("c")
    chip = 2 * kx + ky
    kc = jnp.stack([chip, kc_]).astype(jnp.int32)

    def squeeze(n, a):
        if n == "meta_tokens":
            return a
        if n == "norm_final_g":
            return a.reshape(1, -1)
        a = a[0]
        return a.reshape(1, -1) if a.ndim == 1 else a

    wl = {n: squeeze(n, w[n]) for n in WEIGHTS}
    ml = {n: squeeze(n, mom[n]) for n in WEIGHTS}
    vl = {n: squeeze(n, var[n]) for n in WEIGHTS}

    tiny = _pack([wl[n] for n in TINY_SHARDED], SUBLANES, LANES)
    ex = _Exchanges({n: wl[n] for n in BIG_NAMES}, tiny, kc)
    tiny_shapes = [wl[n].shape for n in TINY_SHARDED]
    tiny_all = ex.small_params(kc)
    tiny_parts = [_unpack(tiny_all[k], tiny_shapes) for k in range(4)]
    p = {n: wl[n] for n in WEIGHTS if n not in BIG}
    for j, n in enumerate(TINY_SHARDED):
        p[n] = jnp.concatenate([tiny_parts[k][j] for k in range(4)], axis=1)
    p["ssm_log_dt"] = wl["ssm_log_dt"].reshape(-1)

    loss_local, grad_x, grads = _local_step(x[0], loss_target[0], p, ex)

    small_names = REPLICATED + TINY_SHARDED
    small_shapes = [tuple(grads[n].shape) for n in small_names] + [(1,)]
    pack = _pack([grads[n] for n in small_names] + [loss_local.reshape(1)], 2 * 16, PACK_COLS)
    g_pack = ex.finish_pack(pack)
    g_small = dict(zip(small_names + ("loss",), _unpack(g_pack, small_shapes)))
    loss = g_small["loss"][0]
    swapped = ("ssm_b_re", "ssm_b_im")

    def view(n, a):
        if n in swapped:
            return jnp.swapaxes(a, -1, -2)
        return a.reshape(1, -1) if a.ndim == 1 else a

    g = {}
    for n in REPLICATED:
        g[n] = g_small[n].reshape(view(n, w[n]).shape)
    for n in TINY_SHARDED:
        cols = wl[n].shape[1]
        g[n] = lax.dynamic_slice_in_dim(g_small[n], chip * cols, cols, axis=1).reshape(w[n].shape)
    delta, new_m, new_v = {}, {}, {}
    small = [[view(n, d[n]) for n in small_names] for d in (w, mom, var)]
    small.insert(1, [g[n] for n in small_names])
    for d, outs in zip((delta, new_m, new_v), _adamw_whole(*small, "adamw_small")):
        d.update(zip(small_names, outs))
    for d in (g, delta, new_m, new_v):
        d.update({n: jnp.swapaxes(d[n], -1, -2) for n in swapped})
    g_big = ex.finish_big(delta[small_names[0]])
    for n in BIG_NAMES:
        g[n], delta[n], new_m[n], new_v[n] = _adamw(wl[n], g_big[n], ml[n], vl[n], "adamw_" + n)

    def like(n, a):
        return a.reshape(w[n].shape)

    return (loss, grad_x[None], *[like(n, g[n]) for n in WEIGHTS], *[like(n, delta[n]) for n in WEIGHTS],
            *[like(n, new_m[n]) for n in WEIGHTS], *[like(n, new_v[n]) for n in WEIGHTS])
```

```python
import functools
import math

import jax
import jax.numpy as jnp
from jax import lax
from jax.experimental import pallas as pl
from jax.experimental.pallas import tpu as pltpu

F32 = jnp.float32
BF16 = jnp.bfloat16
MESH = pl.DeviceIdType.MESH

N_META = 16
N_GROUPS = 32
GROUP = 16
STATE = 64
RMS_EPS = 1e-6
ADAM_LR = 0.001
ADAM_B1 = 0.9
ADAM_B2 = 0.999
ADAM_EPS = 1e-08
ADAM_WD = 0.01
ADAM_STEP = 10

LANES = 128
SUBLANES = 8
ROW_ALIGN = 128
ROW_TILES = 4
VMEM_LIMIT = 52 * 1024 * 1024
MM_VMEM_BUDGET = 40 * 1024 * 1024
GELU_C = math.sqrt(2.0 / math.pi)
GELU_A = 0.044715


def _cparams(*sem):
    return pltpu.CompilerParams(dimension_semantics=sem, vmem_limit_bytes=VMEM_LIMIT)


def _pick_tile(dim, cap, mult):
    best = None
    for t in range(mult, min(dim, cap) + 1, mult):
        if dim % t == 0:
            best = t
    return best if best is not None else dim


def _mm(a, b, mode, name, out_dtype=F32, acc_in=None, after=None):
    if mode == "tn":
        kdim, m = a.shape
    else:
        m, kdim = a.shape
    n = b.shape[0] if mode == "nt" else b.shape[1]
    tm = _pick_tile(m, 1408, LANES if mode == "tn" else 16)
    tk = _pick_tile(kdim, 2816, LANES)
    nk = kdim // tk
    out_bytes = jnp.dtype(out_dtype).itemsize
    for cap in (704, 512, 256, LANES) if m == tm else (1408, 1024, 512, 256, LANES):
        tn = _pick_tile(n, cap, LANES)
        blocks = 2 * (tm * tk * 2 + tk * tn * 2 + tm * tn * out_bytes * (2 if acc_in is not None else 1))
        if blocks + (tm * tn * 4 if nk > 1 else 0) <= MM_VMEM_BUDGET:
            break
    has_acc = acc_in is not None

    def body(*refs):
        if after is not None:
            refs = refs[1:]
        if has_acc:
            a_ref, b_ref, c_ref, o_ref = refs[:4]
            rest = refs[4:]
        else:
            a_ref, b_ref, o_ref = refs[:3]
            c_ref = None
            rest = refs[3:]
        if mode == "nn":
            p = jnp.dot(a_ref[...], b_ref[...], preferred_element_type=F32)
        elif mode == "nt":
            p = lax.dot_general(a_ref[...], b_ref[...], (((1,), (1,)), ((), ())), preferred_element_type=F32)
        else:
            p = lax.dot_general(a_ref[...], b_ref[...], (((0,), (0,)), ((), ())), preferred_element_type=F32)
        if nk == 1:
            if has_acc:
                p = p + c_ref[...]
            o_ref[...] = p.astype(out_dtype)
        else:
            acc_ref = rest[0]
            k = pl.program_id(2)

            @pl.when(k == 0)
            def _():
                acc_ref[...] = p + c_ref[...] if has_acc else p

            @pl.when(k > 0)
            def _():
                acc_ref[...] += p

            @pl.when(k == nk - 1)
            def _():
                o_ref[...] = acc_ref[...].astype(out_dtype)

    if mode == "tn":
        a_spec = pl.BlockSpec((tk, tm), lambda i, j, k: (k, i))
    else:
        a_spec = pl.BlockSpec((tm, tk), lambda i, j, k: (i, k))
    if mode == "nt":
        b_spec = pl.BlockSpec((tn, tk), lambda i, j, k: (j, k))
    else:
        b_spec = pl.BlockSpec((tk, tn), lambda i, j, k: (k, j))
    o_spec = pl.BlockSpec((tm, tn), lambda i, j, k: (i, j))
    in_specs = [a_spec, b_spec] + ([o_spec] if has_acc else [])
    args = (a, b) + ((acc_in,) if has_acc else ())
    if after is not None:
        in_specs = [pl.BlockSpec(memory_space=pl.ANY)] + in_specs
        args = (after,) + args
    return pl.pallas_call(
        body, name=name, grid=(m // tm, n // tn, nk),
        in_specs=in_specs, out_specs=o_spec,
        out_shape=jax.ShapeDtypeStruct((m, n), out_dtype),
        scratch_shapes=[pltpu.VMEM((tm, tn), F32)] if nk > 1 else [],
        compiler_params=_cparams("parallel", "parallel", "arbitrary"),
    )(*args)


def _mm_rows(a, b, mode, name, ins, outs, epilogue, scratch=()):
    m, kdim = a.shape
    n = b.shape[0] if mode == "nt" else b.shape[1]
    tm = m // ROW_TILES
    tk = _pick_tile(kdim, 2816, LANES)
    nk = kdim // tk
    ni, no = len(ins), len(outs)

    def body(*refs):
        a_ref, b_ref = refs[:2]
        in_refs, out_refs, rest = refs[2:2 + ni], refs[2 + ni:2 + ni + no], refs[2 + ni + no:]
        k, i = pl.program_id(0), pl.program_id(1)
        if mode == "nn":
            p = jnp.dot(a_ref[...], b_ref[...], preferred_element_type=F32)
        else:
            p = lax.dot_general(a_ref[...], b_ref[...], (((1,), (1,)), ((), ())), preferred_element_type=F32)
        if nk == 1:
            epilogue(p, i, in_refs, out_refs, rest)
        else:
            acc_ref = rest[0]
            rows = pl.ds(pl.multiple_of(i * tm, SUBLANES), tm)

            @pl.when(k == 0)
            def _():
                acc_ref[rows, :] = p

            @pl.when(jnp.logical_and(k > 0, k < nk - 1))
            def _():
                acc_ref[rows, :] += p

            @pl.when(k == nk - 1)
            def _():
                epilogue(acc_ref[rows, :] + p, i, in_refs, out_refs, rest[1:])

    tile = (lambda k, i: i) if nk == 1 else (lambda k, i: jnp.where(k == nk - 1, i, 0))

    def spec(shape, kind):
        if kind == "rows":
            return pl.BlockSpec((tm,) + tuple(shape[1:]), lambda k, i: (tile(k, i),) + (0,) * (len(shape) - 1))
        if kind == "whole":
            return pl.BlockSpec(tuple(shape), lambda k, i: (0,) * len(shape))
        return pl.BlockSpec(memory_space=pl.ANY)

    a_spec = pl.BlockSpec((tm, tk), lambda k, i: (i, k))
    b_spec = pl.BlockSpec((n, tk), lambda k, i: (0, k)) if mode == "nt" else pl.BlockSpec((tk, n), lambda k, i: (k, 0))
    return pl.pallas_call(
        body, name=name, grid=(nk, ROW_TILES),
        in_specs=[a_spec, b_spec] + [spec(x.shape, kind) for x, kind in ins],
        out_specs=[spec(shape, kind) for shape, _, kind in outs],
        out_shape=[jax.ShapeDtypeStruct(shape, dtype) for shape, dtype, _ in outs],
        scratch_shapes=([pltpu.VMEM((m, n), F32)] if nk > 1 else []) + list(scratch),
        compiler_params=_cparams("arbitrary", "arbitrary"),
    )(a, b, *[x for x, _ in ins])


def _rows(shape_cols, tr, dtype=None):
    return pl.BlockSpec((tr, shape_cols), lambda i: (i, 0))


def _const(shape):
    return pl.BlockSpec(shape, lambda i: (0,) * len(shape))


def _rms(x):
    return lax.rsqrt(jnp.mean(x * x, axis=-1, keepdims=True) + RMS_EPS)


def _rms_bwd(x, r, g, dy):
    xn = x * r
    dxn = dy * g
    dx = r * (dxn - xn * jnp.mean(dxn * xn, axis=-1, keepdims=True))
    return dx, dy * xn


def _gelu(y):
    return 0.5 * y * (1.0 + jnp.tanh(GELU_C * (y + GELU_A * y * y * y)))


def _gelu_grad(y):
    t = jnp.tanh(GELU_C * (y + GELU_A * y * y * y))
    return 0.5 * (1.0 + t) + 0.5 * y * (1.0 - t * t) * GELU_C * (1.0 + 3.0 * GELU_A * y * y)


def _sigmoid(z):
    return 1.0 / (1.0 + jnp.exp(-z))


def _proj_res_norm(a, w, h, g, after, name):
    def epilogue(p, i, ins, outs, _):
        x = ins[0][...] + p
        outs[0][...] = x
        outs[1][...] = (x * _rms(x) * ins[1][...]).astype(BF16)

    return _mm_rows(a, w, "nn", name, [(h, "rows"), (g, "whole"), (after, "hbm")],
                    [(h.shape, F32, "rows"), (h.shape, BF16, "rows")], epilogue)


def _proj_norm_bwd(da, w, h, g, dres, after, name):
    d = h.shape[1]

    def epilogue(p, i, ins, outs, _):
        x = ins[0][...]
        dx, dgs = _rms_bwd(x, _rms(x), ins[1][...], p)
        dh = ins[2][...] + dx
        outs[0][...] = dh
        outs[1][...] = dh.astype(BF16)

        @pl.when(i == 0)
        def _():
            outs[2][...] = jnp.zeros_like(outs[2])

        outs[2][...] += jnp.sum(dgs, axis=0, keepdims=True)

    return _mm_rows(da, w, "nt", name, [(h, "rows"), (g, "whole"), (dres, "rows"), (after, "hbm")],
                    [(h.shape, F32, "rows"), (h.shape, BF16, "rows"), ((1, d), F32, "whole")], epilogue)


def _proj_input_norm_bwd(da, w, h, g, dres, after, n_real, name):
    tp, d = h.shape
    tr = tp // ROW_TILES

    def epilogue(p, i, ins, outs, scratch):
        h_ref, g_ref, dres_ref, _ = ins
        dx_ref, dmeta_ref, dg_ref = outs
        stage, sem = scratch
        x = h_ref[...]
        dx, dgs = _rms_bwd(x, _rms(x), g_ref[...], p)
        stage[...] = dres_ref[...] + dx

        @pl.when(i == 0)
        def _():
            dg_ref[...] = jnp.zeros_like(dg_ref)
            dmeta_ref[...] = stage[:N_META, :]

        dg_ref[...] += jnp.sum(dgs, axis=0, keepdims=True)
        for t in range(ROW_TILES):
            lo, hi = max(t * tr, N_META), min((t + 1) * tr, n_real)
            if hi > lo:
                @pl.when(i == t)
                def _(t=t, lo=lo, hi=hi):
                    cp = pltpu.make_async_copy(stage.at[pl.ds(lo - t * tr, hi - lo), :],
                                               dx_ref.at[pl.ds(lo - N_META, hi - lo), :], sem)
                    cp.start()
                    cp.wait()

    return _mm_rows(da, w, "nt", name, [(h, "rows"), (g, "whole"), (dres, "rows"), (after, "hbm")],
                    [((n_real - N_META, d), F32, "hbm"), ((N_META, d), F32, "whole"), ((1, d), F32, "whole")],
                    epilogue, scratch=[pltpu.VMEM((tr, d), F32), pltpu.SemaphoreType.DMA])


def _load_token_rows(tok_hbm, buf, sem, tr, n_real, head=None, wait=False, i=None):
    i = pl.program_id(0) if i is None else i
    for t in range(ROW_TILES):
        base = t * tr
        lo, hi = max(base, N_META), min(base + tr, n_real)

        @pl.when(i == t)
        def _(base=base, lo=lo, hi=hi):
            if hi > lo:
                cp = pltpu.make_async_copy(tok_hbm.at[pl.ds(lo - N_META, hi - lo), :],
                                           buf.at[pl.ds(lo - base, hi - lo), :], sem)
                if wait:
                    cp.wait()
                    return
                cp.start()
            if wait:
                return
            if base < N_META:
                buf[0:N_META - base, :] = (jnp.zeros((N_META - base, buf.shape[1]), F32) if head is None
                                           else head[base:N_META, :])
            if hi < base + tr:
                buf[max(hi, base) - base:tr, :] = jnp.zeros((base + tr - max(hi, base), buf.shape[1]), F32)


def _input_norm_fwd(x, meta, g, tp, name):
    seq, d = x.shape
    tr = tp // ROW_TILES
    n_real = N_META + seq

    def body(x_hbm, meta_ref, g_ref, h_ref, hn_ref, buf, sem):
        _load_token_rows(x_hbm, buf, sem, tr, n_real, head=meta_ref)
        _load_token_rows(x_hbm, buf, sem, tr, n_real, wait=True)
        h = buf[...]
        h_ref[...] = h
        hn_ref[...] = (h * _rms(h) * g_ref[...]).astype(BF16)

    return pl.pallas_call(
        body, name=name, grid=(ROW_TILES,),
        in_specs=[pl.BlockSpec(memory_space=pl.ANY), _const((N_META, d)), _const((1, d))],
        out_specs=[_rows(d, tr), _rows(d, tr)],
        out_shape=[jax.ShapeDtypeStruct((tp, d), F32), jax.ShapeDtypeStruct((tp, d), BF16)],
        scratch_shapes=[pltpu.VMEM((tr, d), F32), pltpu.SemaphoreType.DMA],
        compiler_params=_cparams("arbitrary"))(x, meta, g)


def _proj_loss_bwd(act, w, h1, target, g, n_real, name):
    tp, d = h1.shape
    tr = tp // ROW_TILES

    def epilogue(p, i, ins, outs, scratch):
        h1_ref, t_hbm, g_ref = ins
        loss_ref, dh_ref, dhb_ref, dg_ref = outs
        t_buf, sem = scratch
        _load_token_rows(t_hbm, t_buf, sem, tr, n_real, i=i)
        x = h1_ref[...] + p
        r = _rms(x)
        row = i * tr + lax.broadcasted_iota(jnp.int32, (tr, d), 0)
        valid = (row >= N_META) & (row < n_real)
        _load_token_rows(t_hbm, t_buf, sem, tr, n_real, wait=True, i=i)
        e = jnp.where(valid, x * r * g_ref[...] - t_buf[...], 0.0)
        dx, dgs = _rms_bwd(x, r, g_ref[...], e * (1.0 / d))
        dh_ref[...] = dx
        dhb_ref[...] = dx.astype(BF16)

        @pl.when(i == 0)
        def _():
            dg_ref[...] = jnp.zeros_like(dg_ref)
            loss_ref[...] = jnp.zeros_like(loss_ref)

        dg_ref[...] += jnp.sum(dgs, axis=0, keepdims=True)
        loss_ref[...] += (0.5 / d) * jnp.sum(jnp.sum(e * e, axis=0, keepdims=True), axis=1, keepdims=True)

    return _mm_rows(act, w, "nn", name, [(h1, "rows"), (target, "hbm"), (g, "whole")],
                    [((1, LANES), F32, "whole"), ((tp, d), F32, "rows"), ((tp, d), BF16, "rows"),
                     ((1, d), F32, "whole")],
                    epilogue, scratch=[pltpu.VMEM((tr, d), F32), pltpu.SemaphoreType.DMA])


def _mix_fwd(co, y, z, gc, gs, name):
    tp, dh = co.shape
    tr = tp // ROW_TILES

    def body(co_ref, y_ref, z_ref, gc_ref, gs_ref, m_ref):
        c = co_ref[...]
        m_ref[:, :dh] = (c * _rms(c) * gc_ref[...]).astype(BF16)
        so = _gelu(y_ref[...]) * _sigmoid(z_ref[...])
        m_ref[:, dh:] = (so * _rms(so) * gs_ref[...]).astype(BF16)

    return pl.pallas_call(
        body, name=name, grid=(ROW_TILES,),
        in_specs=[_rows(dh, tr)] * 3 + [_const((1, dh))] * 2,
        out_specs=_rows(2 * dh, tr),
        out_shape=jax.ShapeDtypeStruct((tp, 2 * dh), BF16),
        compiler_params=_cparams("parallel"))(co, y, z, gc, gs)


def _proj_mix_bwd(dh1b, w, co, y, z, gc, gs, name):
    tp, dh = co.shape

    def epilogue(p, i, ins, outs, _):
        co_ref, y_ref, z_ref, gc_ref, gs_ref = ins
        dco_ref, dz_ref, dgp_ref, dgc_ref, dgs_ref = outs
        c = co_ref[...]
        dco, dgc = _rms_bwd(c, _rms(c), gc_ref[...], p[:, :dh])
        dco_ref[...] = dco
        gl = _gelu(y_ref[...])
        sg = _sigmoid(z_ref[...])
        so = gl * sg
        dso, dgs = _rms_bwd(so, _rms(so), gs_ref[...], p[:, dh:])
        dz_ref[...] = (dso * gl * sg * (1.0 - sg)).astype(BF16)
        dgp_ref[...] = dso * sg

        @pl.when(i == 0)
        def _():
            dgc_ref[...] = jnp.zeros_like(dgc_ref)
            dgs_ref[...] = jnp.zeros_like(dgs_ref)

        dgc_ref[...] += jnp.sum(dgc, axis=0, keepdims=True)
        dgs_ref[...] += jnp.sum(dgs, axis=0, keepdims=True)

    return _mm_rows(dh1b, w, "nt", name,
                    [(co, "rows"), (y, "rows"), (z, "rows"), (gc, "whole"), (gs, "whole")],
                    [((tp, dh), F32, "rows"), ((tp, dh), BF16, "rows"), ((tp, dh), F32, "rows"),
                     ((1, dh), F32, "whole"), ((1, dh), F32, "whole")], epilogue)


def _shift_down(x, k):
    row = lax.broadcasted_iota(jnp.int32, x.shape, 0)
    return jnp.where(row >= k, pltpu.roll(x, k, 0), 0.0)


def _shift_up(x, k):
    n = x.shape[0]
    row = lax.broadcasted_iota(jnp.int32, x.shape, 0)
    return jnp.where(row < n - k, pltpu.roll(x, n - k, 0), 0.0)


def _dwconv(x, w_ref):
    return w_ref[2:3, :] * x + w_ref[1:2, :] * _shift_down(x, 1) + w_ref[0:1, :] * _shift_down(x, 2)


def _dwconv_bwd(x, dy, w_ref):
    dx = w_ref[2:3, :] * dy + w_ref[1:2, :] * _shift_up(dy, 1) + w_ref[0:1, :] * _shift_up(dy, 2)
    dw = jnp.concatenate([jnp.sum(dy * _shift_down(x, 2), axis=0, keepdims=True),
                          jnp.sum(dy * _shift_down(x, 1), axis=0, keepdims=True),
                          jnp.sum(dy * x, axis=0, keepdims=True)], axis=0)
    return dx, dw


def _interleave(dst, src):
    seg_rows = src.shape[0] // SUBLANES
    for seg in range(SUBLANES):
        dst[pl.ds(seg, seg_rows, stride=SUBLANES), :] = src[seg * seg_rows:(seg + 1) * seg_rows, :]


def _deinterleave(dst, src):
    seg_rows = src.shape[0] // SUBLANES
    for seg in range(SUBLANES):
        dst[seg * seg_rows:(seg + 1) * seg_rows, :] = src[pl.ds(seg, seg_rows, stride=SUBLANES), :]


def _segment_shift(x, reverse):
    row = lax.broadcasted_iota(jnp.int32, x.shape, 0)
    if reverse:
        return jnp.where(row < SUBLANES - 1, pltpu.roll(x, SUBLANES - 1, 0), 0.0)
    return jnp.where(row >= 1, pltpu.roll(x, 1, 0), 0.0)


def _scan(s_re, s_im, pw_ref, reverse, pair=None):
    n_steps = s_re.shape[0] // SUBLANES
    n_strips = s_re.shape[1] // LANES
    sign = -1.0 if reverse else 1.0
    strips = [slice(st * LANES, (st + 1) * LANES) for st in range(n_strips)]

    def rows_of(j):
        step = (n_steps - 1 - j) if reverse else j
        return pl.ds(pl.multiple_of(step * SUBLANES, SUBLANES), SUBLANES)

    a = [(jnp.broadcast_to(pw_ref[0, 0:1, lanes], (SUBLANES, LANES)),
          sign * jnp.broadcast_to(pw_ref[1, 0:1, lanes], (SUBLANES, LANES))) for lanes in strips]

    def local(i, carry):
        for half in range(2):
            rows = rows_of(2 * i + half)
            out = []
            for st, lanes in enumerate(strips):
                (ar, ai), cr, ci = a[st], carry[2 * st], carry[2 * st + 1]
                xr = s_re[rows, lanes] + (ar * cr - ai * ci)
                xi = s_im[rows, lanes] + (ar * ci + ai * cr)
                s_re[rows, lanes] = xr
                s_im[rows, lanes] = xi
                out += [xr, xi]
            carry = tuple(out)
        return carry

    zero = jnp.zeros((SUBLANES, LANES), F32)
    ends = lax.fori_loop(0, n_steps // 2, local, (zero,) * (2 * n_strips))

    entering = []
    row = lax.broadcasted_iota(jnp.int32, (SUBLANES, LANES), 0)
    for st, lanes in enumerate(strips):
        tr, ti = ends[2 * st], ends[2 * st + 1]
        mr = jnp.broadcast_to(pw_ref[0, n_steps - 1:n_steps, lanes], (SUBLANES, LANES))
        mi = sign * jnp.broadcast_to(pw_ref[1, n_steps - 1:n_steps, lanes], (SUBLANES, LANES))
        for k in (1, 2, 4):
            keep = (row < SUBLANES - k) if reverse else (row >= k)
            rr = jnp.where(keep, pltpu.roll(tr, SUBLANES - k if reverse else k, 0), 0.0)
            ri = jnp.where(keep, pltpu.roll(ti, SUBLANES - k if reverse else k, 0), 0.0)
            tr, ti = tr + (mr * rr - mi * ri), ti + (mr * ri + mi * rr)
            mr, mi = mr * mr - mi * mi, 2.0 * mr * mi
        entering += [_segment_shift(tr, reverse), _segment_shift(ti, reverse)]

    def fix(i, carry):
        carry, sums = carry[:2 * n_strips], carry[2 * n_strips:]
        for half in range(2):
            j = 2 * i + half
            rows = rows_of(j)
            out, acc = [], []
            for st, lanes in enumerate(strips):
                (ar, ai), cr, ci = a[st], carry[2 * st], carry[2 * st + 1]
                cr, ci = ar * cr - ai * ci, ar * ci + ai * cr
                xr = s_re[rows, lanes] + cr
                xi = s_im[rows, lanes] + ci
                s_re[rows, lanes] = xr
                s_im[rows, lanes] = xi
                out += [cr, ci]
                if pair is not None:
                    p_rows = rows_of(jnp.minimum(j + 1, n_steps - 1))
                    keep = (j < n_steps - 1).astype(F32)
                    pr = pair[0][p_rows, lanes] * keep
                    pi = pair[1][p_rows, lanes] * keep
                    acc += [sums[2 * st] + (xr * pr + xi * pi), sums[2 * st + 1] + (xi * pr - xr * pi)]
            carry, sums = tuple(out), tuple(acc)
        return carry + sums

    n_sums = 0 if pair is None else 2 * n_strips
    out = lax.fori_loop(0, n_steps // 2, fix, tuple(entering) + (zero,) * n_sums)
    return out[2 * n_strips:]


def _seq_fwd(proj, conv_w, bc_re, bc_im, cc_re, cc_im, dskip, a_pow, name):
    tp = proj.shape[0]
    dh = proj.shape[1] // 4
    nq = dh // LANES
    sw = STATE * N_GROUPS // nq

    def body(b_ref, c_ref, v_ref, u_ref, w_ref, bre_ref, bim_ref, cre_ref, cim_ref, d_ref, pw_ref,
             co_ref, y_ref, g_ref, s_re, s_im, u_il, y_il):
        co_ref[...] = b_ref[...] * _dwconv(c_ref[...] * v_ref[...], w_ref)
        _interleave(u_il, u_ref)
        ub = u_il[...].astype(BF16)
        s_re[...] = jnp.dot(ub, bre_ref[...], preferred_element_type=F32)
        s_im[...] = jnp.dot(ub, bim_ref[...], preferred_element_type=F32)
        _scan(s_re, s_im, pw_ref, False)
        y_il[...] = (jnp.dot(s_re[...].astype(BF16), cre_ref[...], preferred_element_type=F32)
                     - jnp.dot(s_im[...].astype(BF16), cim_ref[...], preferred_element_type=F32))
        _deinterleave(y_ref, y_il)
        y = y_ref[...] + d_ref[...] * u_ref[...]
        y_ref[...] = y
        g_ref[...] = _gelu(y).astype(BF16)

    col = lambda off: pl.BlockSpec((tp, LANES), lambda q, off=off: (0, off * nq + q))
    blk = pl.BlockSpec((tp, LANES), lambda q: (0, q))
    return pl.pallas_call(
        body, name=name, grid=(nq,),
        in_specs=[col(0), col(1), col(2), col(3),
                  pl.BlockSpec((3, LANES), lambda q: (0, q)),
                  pl.BlockSpec((LANES, sw), lambda q: (0, q)), pl.BlockSpec((LANES, sw), lambda q: (0, q)),
                  pl.BlockSpec((sw, LANES), lambda q: (q, 0)), pl.BlockSpec((sw, LANES), lambda q: (q, 0)),
                  pl.BlockSpec((1, LANES), lambda q: (0, q)),
                  pl.BlockSpec((2, tp // SUBLANES, sw), lambda q: (0, 0, q))],
        out_specs=[blk, blk, blk, pl.BlockSpec((tp, sw), lambda q: (0, q)), pl.BlockSpec((tp, sw), lambda q: (0, q))],
        out_shape=[jax.ShapeDtypeStruct((tp, dh), F32), jax.ShapeDtypeStruct((tp, dh), F32),
                   jax.ShapeDtypeStruct((tp, dh), BF16),
                   jax.ShapeDtypeStruct((tp, nq * sw), F32), jax.ShapeDtypeStruct((tp, nq * sw), F32)],
        scratch_shapes=[pltpu.VMEM((tp, LANES), F32), pltpu.VMEM((tp, LANES), F32)],
        compiler_params=_cparams("parallel"),
    )(proj, proj, proj, proj, conv_w, bc_re, bc_im, cc_re, cc_im, dskip, a_pow)


def _conv_bwd(proj, dco, conv_w, name):
    tp = proj.shape[0]
    dh = proj.shape[1] // 4
    nq = dh // LANES

    def body(b_ref, c_ref, v_ref, dco_ref, w_ref, dproj_ref, dw_ref, stage, sem):
        q = pl.program_id(0)
        cg = c_ref[...]
        vg = v_ref[...]
        cv = cg * vg
        dco_v = dco_ref[...]
        dcv, dw = _dwconv_bwd(cv, dco_v * b_ref[...], w_ref)
        dw_ref[...] = dw
        stage[0] = (dco_v * _dwconv(cv, w_ref)).astype(BF16)
        stage[1] = (dcv * vg).astype(BF16)
        stage[2] = (dcv * cg).astype(BF16)
        copies = [pltpu.make_async_copy(stage.at[p], dproj_ref.at[:, pl.ds((p * nq + q) * LANES, LANES)], sem.at[p])
                  for p in range(3)]
        for cp in copies:
            cp.start()
        for cp in copies:
            cp.wait()

    col = lambda off: pl.BlockSpec((tp, LANES), lambda q, off=off: (0, off * nq + q))
    return pl.pallas_call(
        body, name=name, grid=(nq,),
        in_specs=[col(0), col(1), col(2), pl.BlockSpec((tp, LANES), lambda q: (0, q)),
                  pl.BlockSpec((3, LANES), lambda q: (0, q))],
        out_specs=[pl.BlockSpec(memory_space=pl.ANY), pl.BlockSpec((3, LANES), lambda q: (0, q))],
        out_shape=[jax.ShapeDtypeStruct((tp, 4 * dh), BF16), jax.ShapeDtypeStruct((3, dh), F32)],
        scratch_shapes=[pltpu.VMEM((3, tp, LANES), BF16), pltpu.SemaphoreType.DMA((3,))],
        compiler_params=_cparams("arbitrary"),
    )(proj, proj, proj, dco, conv_w)


def _ssm_bwd(proj, y, dg, dproj, states, bc_re, bc_im, cc_re, cc_im, dskip, a_pow, name):
    tp = proj.shape[0]
    dh = proj.shape[1] // 4
    nq = dh // LANES
    sw = STATE * N_GROUPS // nq

    def body(u_ref, y_ref, dg_ref, dproj_in, s_re, s_im, bre_ref, bim_ref, cre_ref, cim_ref, d_ref, pw_ref,
             dproj_ref, dbre_ref, dbim_ref, dcre_ref, dcim_ref, dd_ref, dar_ref, dai_ref,
             l_re, l_im, a_il, b_il, stage, sem):
        del dproj_in
        q = pl.program_id(0)
        nt = (((1,), (1,)), ((), ()))
        tn = (((0,), (0,)), ((), ()))
        _interleave(a_il, u_ref)
        ub = a_il[...].astype(BF16)
        dy_rows = dg_ref[...] * _gelu_grad(y_ref[...])
        dd_ref[...] = jnp.sum(dy_rows * u_ref[...], axis=0, keepdims=True)
        _interleave(b_il, dy_rows)
        dy = b_il[...]
        dyb = dy.astype(BF16)
        l_re[...] = lax.dot_general(dyb, cre_ref[...], nt, preferred_element_type=F32)
        l_im[...] = -lax.dot_general(dyb, cim_ref[...], nt, preferred_element_type=F32)
        dcre_ref[...] = lax.dot_general(s_re[...].astype(BF16), dyb, tn, preferred_element_type=F32)
        dcim_ref[...] = -lax.dot_general(s_im[...].astype(BF16), dyb, tn, preferred_element_type=F32)
        sums = _scan(l_re, l_im, pw_ref, True, pair=(s_re, s_im))
        rest = tp - SUBLANES
        for st in range(sw // LANES):
            lanes = slice(st * LANES, (st + 1) * LANES)
            lr0, li0 = l_re[:SUBLANES, lanes], l_im[:SUBLANES, lanes]
            pr0, pi0 = _segment_shift(s_re[rest:, lanes], False), _segment_shift(s_im[rest:, lanes], False)
            dar_ref[:, lanes] = jnp.sum(sums[2 * st] + (lr0 * pr0 + li0 * pi0), axis=0, keepdims=True)
            dai_ref[:, lanes] = jnp.sum(sums[2 * st + 1] + (li0 * pr0 - lr0 * pi0), axis=0, keepdims=True)
        lrb = l_re[...].astype(BF16)
        lib = l_im[...].astype(BF16)
        a_il[...] = (dy * d_ref[...] + lax.dot_general(lrb, bre_ref[...], nt, preferred_element_type=F32)
                     + lax.dot_general(lib, bim_ref[...], nt, preferred_element_type=F32))
        _deinterleave(b_il, a_il)
        stage[...] = b_il[...].astype(BF16)
        dbre_ref[...] = lax.dot_general(ub, lrb, tn, preferred_element_type=F32)
        dbim_ref[...] = lax.dot_general(ub, lib, tn, preferred_element_type=F32)
        cp = pltpu.make_async_copy(stage, dproj_ref.at[:, pl.ds((3 * nq + q) * LANES, LANES)], sem)
        cp.start()
        cp.wait()

    blk = pl.BlockSpec((tp, LANES), lambda q: (0, q))
    bspec = pl.BlockSpec((LANES, sw), lambda q: (0, q))
    cspec = pl.BlockSpec((sw, LANES), lambda q: (q, 0))
    tspec = pl.BlockSpec((2, tp // SUBLANES, sw), lambda q: (0, 0, q))
    nstate = STATE * N_GROUPS
    return pl.pallas_call(
        body, name=name, grid=(nq,),
        in_specs=[pl.BlockSpec((tp, LANES), lambda q: (0, 3 * nq + q)), blk, blk, pl.BlockSpec(memory_space=pl.ANY),
                  pl.BlockSpec((tp, sw), lambda q: (0, q)), pl.BlockSpec((tp, sw), lambda q: (0, q)),
                  bspec, bspec, cspec, cspec, pl.BlockSpec((1, LANES), lambda q: (0, q)), tspec],
        out_specs=[pl.BlockSpec(memory_space=pl.ANY), bspec, bspec, cspec, cspec,
                   pl.BlockSpec((1, LANES), lambda q: (0, q)),
                   pl.BlockSpec((1, sw), lambda q: (0, q)), pl.BlockSpec((1, sw), lambda q: (0, q))],
        out_shape=[jax.ShapeDtypeStruct((tp, 4 * dh), BF16),
                   jax.ShapeDtypeStruct((LANES, nstate), F32), jax.ShapeDtypeStruct((LANES, nstate), F32),
                   jax.ShapeDtypeStruct((nstate, LANES), F32), jax.ShapeDtypeStruct((nstate, LANES), F32),
                   jax.ShapeDtypeStruct((1, dh), F32),
                   jax.ShapeDtypeStruct((1, nstate), F32), jax.ShapeDtypeStruct((1, nstate), F32)],
        input_output_aliases={3: 0},
        scratch_shapes=[pltpu.VMEM((tp, sw), F32)] * 2 + [pltpu.VMEM((tp, LANES), F32)] * 2
        + [pltpu.VMEM((tp, LANES), BF16), pltpu.SemaphoreType.DMA],
        compiler_params=_cparams("arbitrary"),
    )(proj, y, dg, dproj, states[0], states[1], bc_re, bc_im, cc_re, cc_im, dskip, a_pow)


FFN_TILE = 256
FFN_ROWS = 32


def _window(x_ref, before, r0, rows, cols):
    if r0 == 0:
        return jnp.concatenate([before, x_ref[0:rows, cols]], axis=0)
    return x_ref[r0 - SUBLANES:r0 + rows, cols]


def _taps(window):
    return window[SUBLANES:], pltpu.roll(window, 1, 0)[SUBLANES:], pltpu.roll(window, 2, 0)[SUBLANES:]


def _conv_taps(taps, w):
    return w[2] * taps[0] + w[1] * taps[1] + w[0] * taps[2]


FFN_MM_ROWS = 544
FFN_MM_COLS = 1408


def _ffn_up_act(hn, w_up, fw, fb, col, others, name):
    tp, dm = hn.shape
    dff = w_up.shape[1] // 2
    tr, cw, rows = FFN_MM_ROWS, FFN_MM_COLS, FFN_ROWS
    nc = dff // cw
    n_others = 0 if others is None else 2

    def body(hn_ref, ma_ref, mv_ref, wa_ref, wv_ref, ba_ref, bv_ref, *rest):
        up_ref, act_ref, tail_ref = rest[n_others:]

        @pl.when(pl.program_id(0) == 0)
        def _():
            tail_ref[...] = jnp.zeros_like(tail_ref)

        x = hn_ref[...]
        up_ref[0] = jnp.dot(x, ma_ref[...], preferred_element_type=F32)
        up_ref[1] = jnp.dot(x, mv_ref[...], preferred_element_type=F32)
        for c0 in range(0, cw, FFN_TILE):
            cols = slice(c0, min(c0 + FFN_TILE, cw))
            wa, wv = [[w_ref[k:k + 1, cols] for k in range(3)] for w_ref in (wa_ref, wv_ref)]
            ba, bv = ba_ref[:, cols], bv_ref[:, cols]
            before_a, before_v = tail_ref[0, :, cols], tail_ref[1, :, cols]
            for r0 in range(0, tr, rows):
                a = _conv_taps(_taps(_window(up_ref.at[0], before_a, r0, rows, cols)), wa) + ba
                v = _conv_taps(_taps(_window(up_ref.at[1], before_v, r0, rows, cols)), wv) + bv
                act_ref[r0:r0 + rows, cols] = (a * _sigmoid(a) * v).astype(BF16)
            tail_ref[:, :, cols] = up_ref[:, tr - SUBLANES:tr, cols]

    par = lambda r, half: pl.BlockSpec((r, cw), lambda i: (0, half * nc + col))
    return pl.pallas_call(
        body, name=name, grid=(tp // tr,),
        in_specs=[pl.BlockSpec((tr, dm), lambda i: (i, 0)), par(dm, 0), par(dm, 1),
                  par(3, 0), par(3, 1), par(1, 0), par(1, 1)] + [pl.BlockSpec(memory_space=pl.ANY)] * n_others,
        out_specs=[pl.BlockSpec((2, tr, cw), lambda i: (0, i, col)), pl.BlockSpec((tr, cw), lambda i: (i, col))],
        out_shape=[jax.ShapeDtypeStruct((2, tp, dff), F32), jax.ShapeDtypeStruct((tp, dff), BF16)],
        input_output_aliases={7: 0, 8: 1} if others is not None else {},
        scratch_shapes=[pltpu.VMEM((2, SUBLANES, cw), F32)],
        compiler_params=_cparams("arbitrary"))(hn, w_up, w_up, fw, fw, fb, fb, *(others or ()))


def _ffn_bwd(up, dh, w_down, fw, fb, name):
    _, tp, dff = up.shape
    two_ff = 2 * dff
    dm = dh.shape[1]
    tr, cw, rows = FFN_MM_ROWS, FFN_MM_COLS, FFN_ROWS
    nr, nc = tp // tr, dff // cw
    n_e = rows + SUBLANES
    pieces = tr // SUBLANES

    def body(ua_ref, uv_ref, pa_ref, pv_ref, dh_ref, wd_ref, wa_ref, wv_ref, ba_ref, bv_ref,
             dup_ref, dwa_ref, dwv_ref, dba_ref, dbv_ref, dact, stage, head_ref, sem):
        j, i = pl.program_id(0), pl.program_id(1)
        step = j * nr + i
        top = i == nr - 1
        sums = ((dwa_ref, dba_ref), (dwv_ref, dbv_ref))

        slot = step % 2

        def out_copies(at):
            r0 = pl.multiple_of((nr - 1 - at % nr) * tr, tr)
            return [pltpu.make_async_copy(
                stage.at[at % 2, s],
                dup_ref.at[pl.ds(r0, tr), pl.ds(pl.multiple_of(s * dff + at // nr * cw, LANES), cw)],
                sem.at[at % 2, s]) for s in range(2)]

        @pl.when(i == 0)
        def _():
            head_ref[...] = jnp.zeros_like(head_ref)
            for dw_ref, db_ref in sums:
                dw_ref[...] = jnp.zeros_like(dw_ref)
                db_ref[...] = jnp.zeros_like(db_ref)

        dact[...] = lax.dot_general(dh_ref[...], wd_ref[...], (((1,), (1,)), ((), ())), preferred_element_type=F32)

        @pl.when(step > 1)
        def _():
            for cp in out_copies(step - 2):
                cp.wait()

        def gate_bwd(taps, dact_v, w, bias):
            a, v = [_conv_taps(taps[s], w[s]) + bias[s] for s in range(2)]
            sg = _sigmoid(a)
            return [dact_v * v * sg * (1.0 + a * (1.0 - sg)), dact_v * a * sg]

        fold = lambda x: sum(x[r:r + SUBLANES] for r in range(0, rows, SUBLANES))
        for c0 in range(0, cw, FFN_TILE):
            cols = slice(c0, min(c0 + FFN_TILE, cw))
            w = [[w_ref[k:k + 1, cols] for k in range(3)] for w_ref in (wa_ref, wv_ref)]
            bias = [ba_ref[:, cols], bv_ref[:, cols]]
            before = [jnp.where(top, 0.0, p_ref[:, cols]) for p_ref in (pa_ref, pv_ref)]
            head = [head_ref[s, :, cols] for s in range(2)]
            piece = jnp.zeros_like(head[0])
            acc = [[piece] * 4 for _ in range(2)]
            for r0 in reversed(range(0, tr, rows)):
                taps = [_taps(_window(x_ref, before[s], r0, rows, cols)) for s, x_ref in enumerate((ua_ref, uv_ref))]
                d = gate_bwd(taps, dact[r0:r0 + rows, cols], w, bias)
                for s in range(2):
                    de = jnp.concatenate([d[s], head[s]], axis=0)
                    dx = (w[s][2] * d[s] + w[s][1] * pltpu.roll(de, n_e - 1, 0)[:rows]
                          + w[s][0] * pltpu.roll(de, n_e - 2, 0)[:rows])
                    stage[slot, s, r0:r0 + rows, cols] = dx.astype(BF16)
                    for k in range(3):
                        acc[s][k] = acc[s][k] + fold(d[s] * taps[s][2 - k])
                    acc[s][3] = acc[s][3] + fold(d[s])
                    head[s] = d[s][:SUBLANES]
            for s, (dw_ref, db_ref) in enumerate(sums):
                head_ref[s, :, cols] = head[s]
                dw_ref[:, cols] = dw_ref[:, cols] + jnp.concatenate(
                    [jnp.sum(x, axis=0, keepdims=True) for x in acc[s][:3]], axis=0)
                db_ref[:, cols] = db_ref[:, cols] + jnp.sum(acc[s][3], axis=0, keepdims=True)

        copies = out_copies(step)
        for cp in copies:
            cp.start()

        @pl.when(step == nc * nr - 1)
        def _():
            for cp in out_copies(step - 1) + copies:
                cp.wait()

    row = lambda i: nr - 1 - i
    main = lambda half: pl.BlockSpec((None, tr, cw), lambda j, i: (half, row(i), j))
    prev = lambda half: pl.BlockSpec((None, SUBLANES, cw), lambda j, i: (half, jnp.maximum(row(i) * pieces - 1, 0), j))
    par = lambda r, half: pl.BlockSpec((r, cw), lambda j, i: (0, half * nc + j))
    acc_spec = lambda r: pl.BlockSpec((r, cw), lambda j, i: (0, j))
    return pl.pallas_call(
        body, name=name, grid=(nc, nr),
        in_specs=[main(0), main(1), prev(0), prev(1),
                  pl.BlockSpec((tr, dm), lambda j, i: (row(i), 0)), pl.BlockSpec((cw, dm), lambda j, i: (j, 0)),
                  par(3, 0), par(3, 1), par(1, 0), par(1, 1)],
        out_specs=[pl.BlockSpec(memory_space=pl.ANY), acc_spec(3), acc_spec(3), acc_spec(1), acc_spec(1)],
        out_shape=[jax.ShapeDtypeStruct((tp, two_ff), BF16),
                   jax.ShapeDtypeStruct((3, dff), F32), jax.ShapeDtypeStruct((3, dff), F32),
                   jax.ShapeDtypeStruct((1, dff), F32), jax.ShapeDtypeStruct((1, dff), F32)],
        scratch_shapes=[pltpu.VMEM((tr, cw), F32), pltpu.VMEM((2, 2, tr, cw), BF16),
                        pltpu.VMEM((2, SUBLANES, cw), F32), pltpu.SemaphoreType.DMA((2, 2))],
        compiler_params=_cparams("arbitrary", "arbitrary"))(up, up, up, up, dh, w_down, fw, fw, fb, fb)


def _zoh(lr, li, ld):
    dt = jnp.exp(ld)
    mag = jnp.exp(lr * dt)
    ang = li * dt
    ar = mag * jnp.cos(ang)
    ai = mag * jnp.sin(ang)
    den = lr * lr + li * li
    nr = ar - 1.0
    fr = (nr * lr + ai * li) / den
    fi = (ai * lr - nr * li) / den
    return dt, ar, ai, den, nr, fr, fi


def _s5_prep(lr, li, ld, b_re, b_im, n_pow, name):
    nstate = lr.shape[1]

    def body(lr_ref, li_ref, ld_ref, bre_ref, bim_ref, pw_ref, bcre_ref, bcim_ref):
        _, ar, ai, _, _, fr, fi = _zoh(lr_ref[...], li_ref[...], ld_ref[...])
        bre = bre_ref[...]
        bim = bim_ref[...]
        bcre_ref[...] = (fr * bre - fi * bim).astype(BF16)
        bcim_ref[...] = (fr * bim + fi * bre).astype(BF16)
        row = lax.broadcasted_iota(jnp.int32, (SUBLANES, nstate), 0)
        pr, pi = jnp.zeros((SUBLANES, nstate), F32), jnp.zeros((SUBLANES, nstate), F32)
        cr, ci = ar, ai
        for t in range(SUBLANES):
            pr, pi = jnp.where(row == t, cr, pr), jnp.where(row == t, ci, pi)
            cr, ci = cr * ar - ci * ai, cr * ai + ci * ar
        pw_ref[0, 0:SUBLANES, :] = pr
        pw_ref[1, 0:SUBLANES, :] = pi
        n = SUBLANES
        while n < n_pow:
            m = min(n, n_pow - n)
            tr, ti = pw_ref[0, n - 1:n, :], pw_ref[1, n - 1:n, :]
            xr, xi = pw_ref[0, 0:m, :], pw_ref[1, 0:m, :]
            pw_ref[0, n:n + m, :] = xr * tr - xi * ti
            pw_ref[1, n:n + m, :] = xr * ti + xi * tr
            n += m

    vmem = pl.BlockSpec(memory_space=pltpu.VMEM)
    return pl.pallas_call(
        body, name=name, in_specs=[vmem] * 5, out_specs=[vmem] * 3,
        out_shape=[jax.ShapeDtypeStruct((2, n_pow, nstate), F32)] + [jax.ShapeDtypeStruct(b_re.shape, BF16)] * 2,
        compiler_params=pltpu.CompilerParams(vmem_limit_bytes=VMEM_LIMIT))(lr, li, ld, b_re, b_im)


def _s5_prep_bwd(lr, li, ld, b_re, b_im, da_re, da_im, dbc_re, dbc_im, name):
    def body(lr_ref, li_ref, ld_ref, bre_ref, bim_ref, dar_ref, dai_ref, dbcre_ref, dbcim_ref,
             dlr_ref, dli_ref, dld_ref, dbre_ref, dbim_ref):
        lr, li = lr_ref[...], li_ref[...]
        dt, ar, ai, den, nr, fr, fi = _zoh(lr, li, ld_ref[...])
        bre, bim = bre_ref[...], bim_ref[...]
        gre, gim = dbcre_ref[...], dbcim_ref[...]
        dbre_ref[...] = fr * gre + fi * gim
        dbim_ref[...] = fr * gim - fi * gre
        g_fr = jnp.sum(gre * bre + gim * bim, axis=0, keepdims=True)
        g_fi = jnp.sum(gim * bre - gre * bim, axis=0, keepdims=True)
        g_ar = dar_ref[...] + (g_fr * lr - g_fi * li) / den
        g_ai = dai_ref[...] + (g_fr * li + g_fi * lr) / den
        d_lr = (g_fr * (nr - 2.0 * fr * lr) + g_fi * (ai - 2.0 * fi * lr)) / den
        d_li = (g_fr * (ai - 2.0 * fr * li) - g_fi * (nr + 2.0 * fi * li)) / den
        g_logmag = g_ar * ar + g_ai * ai
        g_ang = g_ai * ar - g_ar * ai
        dlr_ref[...] = d_lr + g_logmag * dt
        dli_ref[...] = d_li + g_ang * dt
        d_ld = (g_logmag * lr + g_ang * li) * dt
        n = d_ld.shape[1]
        sh = 1
        while sh < STATE:
            d_ld = d_ld + pltpu.roll(d_ld, n - sh, 1)
            sh *= 2
        dld_ref[...] = d_ld

    vmem = pl.BlockSpec(memory_space=pltpu.VMEM)
    row = jax.ShapeDtypeStruct(lr.shape, F32)
    return pl.pallas_call(
        body, name=name, in_specs=[vmem] * 9, out_specs=[vmem] * 5,
        out_shape=[row, row, row, jax.ShapeDtypeStruct(b_re.shape, F32), jax.ShapeDtypeStruct(b_re.shape, F32)],
    )(lr, li, ld, b_re, b_im, da_re, da_im, dbc_re, dbc_im)


def _compact_b(bb):
    bq = bb.reshape(N_GROUPS // 8, 8, STATE, GROUP)
    m = jnp.einsum("ab,qbph->qahbp", jnp.eye(8, dtype=bb.dtype), bq).reshape(N_GROUPS // 8, LANES, 8 * STATE)
    return m.transpose(1, 0, 2).reshape(LANES, N_GROUPS * STATE)


def _expand_b(m):
    d = m.reshape(8, GROUP, N_GROUPS // 8, 8, STATE)
    return jnp.einsum("ahqap->qahp", d).reshape(N_GROUPS, GROUP, STATE)


def _compact_c(c):
    cq = c.reshape(N_GROUPS // 8, 8, GROUP, STATE)
    return jnp.einsum("ab,qbhp->qbpah", jnp.eye(8, dtype=c.dtype), cq).reshape(N_GROUPS * STATE, LANES)


def _expand_c(m):
    d = m.reshape(N_GROUPS // 8, 8, STATE, 8, GROUP)
    return jnp.einsum("qbpbh->qbhp", d).reshape(N_GROUPS, GROUP, STATE)


def _local_step(x, target, p, ex):
    seq, d = x.shape
    n_real = N_META + seq
    tp = -(-n_real // ROW_ALIGN) * ROW_ALIGN

    h0, hn1 = _input_norm_fwd(x, p["meta_tokens"], p["norm_mix_g"] + ex.zero, tp, "norm_mix")
    ex.forward("first", hn1)
    nstate = N_GROUPS * STATE
    s5 = (p["ssm_lam_re"].reshape(1, nstate), p["ssm_lam_im"].reshape(1, nstate),
          jnp.repeat(p["ssm_log_dt"].reshape(-1), STATE).reshape(1, nstate),
          _compact_b(p["ssm_b_re"]), _compact_b(p["ssm_b_im"]))
    a_pow, bc_re, bc_im = _s5_prep(*s5, tp // SUBLANES, "s5_prep")
    cc_re = _compact_c(p["ssm_c_re"]).astype(BF16)
    cc_im = _compact_c(p["ssm_c_im"]).astype(BF16)
    dskip = p["ssm_d"].reshape(1, -1)
    first = ex.weights("first", bc_re)
    proj = _mm(hn1, first["w_in"], "nn", "proj")
    started = ex.forward("mid", proj)
    co, y, g, *states = _seq_fwd(proj, p["conv_w"] + started[0, 0], bc_re, bc_im, cc_re, cc_im, dskip, a_pow,
                                 "seq_fwd")
    mid = ex.weights("mid", g)
    z = _mm(g, mid["ssm_w_glu"], "nn", "glu")
    mixed = _mix_fwd(co, y, z, p["gain_conv_out"], p["gain_ssm_out"], "mix_fwd")
    started = ex.forward("up", mixed)
    h1, hn2 = _proj_res_norm(mixed, mid["w_out"], h0, p["norm_ffn_g"], started, "out_proj_norm")
    late = ex.weights("up", hn2)
    part, fw = None, p["ffn_conv_w"]
    for col in range(late["w_up"].shape[1] // (2 * FFN_MM_COLS)):
        part = _ffn_up_act(hn2, late["w_up"], fw, p["ffn_conv_b"], col, part, "ffn_up_act_%d" % col)
        if col == 0:
            fw = fw + ex.forward("down", part[1])[0, 0]
    up, act = part
    late.update(ex.weights("down", act))
    loss, dh2, dh2b, d_gfin = _proj_loss_bwd(act, late["w_down"], h1, target, p["norm_final_g"], n_real,
                                             "down_proj_loss")

    g_w_down = _mm(act, dh2b, "tn", "g_w_down")
    dup, dfw_a, dfw_v, dfb_a, dfb_v = _ffn_bwd(up, dh2b, late["w_down"], p["ffn_conv_w"], p["ffn_conv_b"], "ffn_bwd")
    g_w_up = _mm(hn2, dup, "tn", "g_w_up")
    started = ex.grads_ready("late", {"w_up": g_w_up, "w_down": g_w_down})
    dh1, dh1b, d_gffn = _proj_norm_bwd(dup, late["w_up"], h1, p["norm_ffn_g"], dh2, started, "d_hn2_norm_bwd")
    started = ex.grads_send("late", dh1)
    g_w_out = _mm(mixed, dh1b, "tn", "g_w_out", after=started)
    dco, dz, dgp, d_gc, d_gs = _proj_mix_bwd(dh1b, mid["w_out"], co, y, z, p["gain_conv_out"],
                                             p["gain_ssm_out"], "d_mixed_mix_bwd")
    g_w_glu = _mm(g, dz, "tn", "g_w_glu")
    started = ex.grads_ready("mid", {"ssm_w_glu": g_w_glu, "w_out": g_w_out})
    dg = _mm(dz, mid["ssm_w_glu"], "nt", "d_gelu", acc_in=dgp, after=started)
    started = ex.grads_send("mid", dg)
    dproj, d_conv_w = _conv_bwd(proj, dco, p["conv_w"] + started[0, 0], "conv_bwd")
    (dproj, dbc_re, dbc_im, dcc_re, dcc_im, d_dskip, da_re, da_im) = _ssm_bwd(
        proj, y, dg, dproj, states, bc_re, bc_im, cc_re, cc_im, dskip, a_pow, "ssm_bwd")
    g_w_in = _mm(hn1, dproj, "tn", "g_w_in")
    started = ex.grads_ready("first", {"w_in": g_w_in})
    grad_x, d_meta, d_gmix = _proj_input_norm_bwd(dproj, first["w_in"], h0, p["norm_mix_g"], dh1, started, n_real,
                                                  "d_hn1_norm_bwd")
    started = ex.grads_send("first", d_gmix)

    d_lam_re, d_lam_im, d_log_dt, d_b_re, d_b_im = _s5_prep_bwd(*s5, da_re, da_im, dbc_re, dbc_im, "s5_prep_bwd")
    d_lam_re, d_lam_im = d_lam_re.reshape(N_GROUPS, STATE), d_lam_im.reshape(N_GROUPS, STATE)
    d_log_dt = d_log_dt[0, ::STATE]
    d_b_re, d_b_im = _expand_b(d_b_re), _expand_b(d_b_im)
    grads = {
        "meta_tokens": d_meta, "norm_mix_g": d_gmix, "w_in": g_w_in, "conv_w": d_conv_w,
        "ssm_lam_re": d_lam_re, "ssm_lam_im": d_lam_im, "ssm_log_dt": d_log_dt,
        "ssm_b_re": d_b_re, "ssm_b_im": d_b_im, "ssm_c_re": _expand_c(dcc_re), "ssm_c_im": _expand_c(dcc_im),
        "ssm_d": d_dskip.reshape(N_GROUPS, GROUP), "ssm_w_glu": g_w_glu,
        "gain_conv_out": d_gc, "gain_ssm_out": d_gs, "w_out": g_w_out, "norm_ffn_g": d_gffn,
        "w_up": g_w_up, "ffn_conv_w": jnp.concatenate([dfw_a, dfw_v], axis=1),
        "ffn_conv_b": jnp.concatenate([dfb_a, dfb_v], axis=1), "w_down": g_w_down, "norm_final_g": d_gfin,
    }
    return loss[0, 0] + started[0, 0], grad_x, grads


def _view(ref, axis, start, size):
    idx = [slice(None)] * len(ref.shape)
    idx[axis] = pl.ds(start, size)
    return ref.at[tuple(idx)]


def _exchange(name, ins, outs, aliases, local_copies, remote_copies):
    ni, no = len(ins), len(outs)
    nl, nr = len(local_copies), len(remote_copies)

    def body(*refs):
        in_refs, out_refs = refs[:ni], refs[ni:ni + no]
        send_sems, recv_sems, local_sems = refs[ni + no:]
        x, y, c = lax.axis_index("x"), lax.axis_index("y"), lax.axis_index("c")
        pos = (x, y, c, 2 * x + y)
        locals_ = [pltpu.make_async_copy(s(in_refs, out_refs, pos), d(in_refs, out_refs, pos), local_sems.at[i])
                   for i, (s, d) in enumerate(local_copies)]
        remotes = []
        for i, (s, d, flip) in enumerate(remote_copies):
            peer = (1 - x if "x" in flip else x, 1 - y if "y" in flip else y, 1 - c if "c" in flip else c)
            remotes.append(pltpu.make_async_remote_copy(
                src_ref=s(in_refs, out_refs, pos), dst_ref=d(in_refs, out_refs, pos),
                send_sem=send_sems.at[i], recv_sem=recv_sems.at[i], device_id=peer, device_id_type=MESH))
        for cp in locals_ + remotes:
            cp.start()
        for cp in remotes:
            cp.wait_recv()
        for cp in remotes:
            cp.wait_send()
        for cp in locals_:
            cp.wait()

    hbm = pl.BlockSpec(memory_space=pl.ANY)
    return pl.pallas_call(
        body, name=name, in_specs=[hbm] * ni, out_specs=[hbm] * no, out_shape=outs,
        input_output_aliases=aliases,
        scratch_shapes=[pltpu.SemaphoreType.DMA((nr,)), pltpu.SemaphoreType.DMA((nr,)),
                        pltpu.SemaphoreType.DMA((max(nl, 1),))],
    )(*ins)


BIG = {"w_in": (0, 1), "ssm_w_glu": (1, 0), "w_out": (1, 0), "w_up": (0, 1), "w_down": (1, 0)}
BIG_NAMES = tuple(BIG)
FLIPS = ("y", "x", "xy")


def _peer_chip(pos, flip):
    x, y, _, _ = pos
    return 2 * (1 - x if "x" in flip else x) + (1 - y if "y" in flip else y)


def _block_rows(rows, cols, itemsize, mult):
    return _pick_tile(rows, max(mult, (2 * 1024 * 1024) // (cols * itemsize)), mult)


def _cast_into_full(w, kc, shard_axis, name):
    r, cdim = w.shape
    tr = _block_rows(r, cdim, 4, 16)
    nb = r // tr

    def body(kc_ref, w_ref, o_ref):
        o_ref[...] = w_ref[...].astype(BF16)

    if shard_axis == 1:
        full, o_spec = (r, 4 * cdim), pl.BlockSpec((tr, cdim), lambda i, kc: (i, kc[0]))
    else:
        full, o_spec = (4 * r, cdim), pl.BlockSpec((tr, cdim), lambda i, kc: (kc[0] * nb + i, 0))
    return pl.pallas_call(
        body, name=name,
        grid_spec=pltpu.PrefetchScalarGridSpec(
            num_scalar_prefetch=1, grid=(nb,), in_specs=[pl.BlockSpec((tr, cdim), lambda i, kc: (i, 0))],
            out_specs=o_spec),
        out_shape=jax.ShapeDtypeStruct(full, BF16), compiler_params=_cparams("parallel"))(kc, w)


def _pair_sum(g, recv, kc, half_axis, name, out_dtype):
    hr, hc = recv.shape
    tr = _block_rows(hr, hc, 4, 16)
    nb = hr // tr

    def body(kc_ref, g_ref, r_ref, o_ref):
        o_ref[...] = (g_ref[...] + r_ref[...]).astype(out_dtype)

    if half_axis == 0:
        g_spec = pl.BlockSpec((tr, hc), lambda i, kc: (kc[1] * nb + i, 0))
    elif half_axis == 1:
        g_spec = pl.BlockSpec((tr, hc), lambda i, kc: (i, kc[1]))
    else:
        g_spec = pl.BlockSpec((tr, hc), lambda i, kc: (i, 0))
    same = pl.BlockSpec((tr, hc), lambda i, kc: (i, 0))
    return pl.pallas_call(
        body, name=name,
        grid_spec=pltpu.PrefetchScalarGridSpec(num_scalar_prefetch=1, grid=(nb,), in_specs=[g_spec, same],
                                               out_specs=same),
        out_shape=jax.ShapeDtypeStruct((hr, hc), out_dtype), compiler_params=_cparams("parallel"))(kc, g, recv)


def _chip_sum(own, recv, kc, own_axis, out_axis, name):
    _, sr, sc = recv.shape
    tr = _block_rows(sr, sc, 4, 16)
    nb = sr // tr

    def body(kc_ref, o_ref, r_ref, t_ref):
        k = kc_ref[0]
        own_v = o_ref[...].astype(F32)
        r = [r_ref[m].astype(F32) for m in range(3)]
        terms = []
        for kk in range(4):
            m = jnp.bitwise_xor(k, kk)
            terms.append(jnp.where(m == 0, own_v, jnp.where(m == 1, r[0], jnp.where(m == 2, r[1], r[2]))))
        t_ref[...] = (terms[0] + terms[1]) + (terms[2] + terms[3])

    if own_axis == 0:
        own_spec = pl.BlockSpec((tr, sc), lambda i, kc: (kc[0] * nb + i, 0))
    elif own_axis == 1:
        own_spec = pl.BlockSpec((tr, sc), lambda i, kc: (i, kc[0]))
    else:
        own_spec = pl.BlockSpec((tr, sc), lambda i, kc: (kc[1] * nb + i, 0))
    if out_axis == 0:
        out_full, out_spec = (2 * sr, sc), pl.BlockSpec((tr, sc), lambda i, kc: (kc[1] * nb + i, 0))
    else:
        out_full, out_spec = (sr, 2 * sc), pl.BlockSpec((tr, sc), lambda i, kc: (i, kc[1]))
    return pl.pallas_call(
        body, name=name,
        grid_spec=pltpu.PrefetchScalarGridSpec(
            num_scalar_prefetch=1, grid=(nb,),
            in_specs=[own_spec, pl.BlockSpec((3, tr, sc), lambda i, kc: (0, i, 0))],
            out_specs=out_spec),
        out_shape=jax.ShapeDtypeStruct(out_full, F32), compiler_params=_cparams("parallel"))(kc, own, recv)


def _adamw(w, g, m, v, name):
    r, cdim = w.shape
    tr = _block_rows(r, cdim, 4, 8)
    c1 = 1.0 - ADAM_B1 ** ADAM_STEP
    c2 = 1.0 - ADAM_B2 ** ADAM_STEP

    def body(w_ref, g_ref, m_ref, v_ref, go_ref, d_ref, nm_ref, nv_ref):
        gv = g_ref[...]
        go_ref[...] = gv
        nm = ADAM_B1 * m_ref[...] + (1.0 - ADAM_B1) * gv
        nv = ADAM_B2 * v_ref[...] + (1.0 - ADAM_B2) * (gv * gv)
        d_ref[...] = -ADAM_LR * ((nm / c1) / (jnp.sqrt(nv / c2) + ADAM_EPS) + ADAM_WD * w_ref[...])
        nm_ref[...] = nm
        nv_ref[...] = nv

    spec = _rows(cdim, tr)
    return pl.pallas_call(body, name=name, grid=(r // tr,), in_specs=[spec] * 4, out_specs=[spec] * 4,
                          out_shape=[jax.ShapeDtypeStruct((r, cdim), F32)] * 4,
                          compiler_params=_cparams("parallel"))(w, g, m, v)


def _adamw_whole(ws, gs, ms, vs, name):
    n = len(ws)
    c1 = 1.0 - ADAM_B1 ** ADAM_STEP
    c2 = 1.0 - ADAM_B2 ** ADAM_STEP

    def body(*refs):
        for i in range(n):
            w_ref, g_ref, m_ref, v_ref, d_ref, nm_ref, nv_ref = [refs[j * n + i] for j in range(7)]
            gv = g_ref[...]
            nm = ADAM_B1 * m_ref[...] + (1.0 - ADAM_B1) * gv
            nv = ADAM_B2 * v_ref[...] + (1.0 - ADAM_B2) * (gv * gv)
            d_ref[...] = -ADAM_LR * ((nm / c1) / (jnp.sqrt(nv / c2) + ADAM_EPS) + ADAM_WD * w_ref[...])
            nm_ref[...] = nm
            nv_ref[...] = nv

    vmem = pl.BlockSpec(memory_space=pltpu.VMEM)
    out = pl.pallas_call(body, name=name, in_specs=[vmem] * (4 * n), out_specs=[vmem] * (3 * n),
                         out_shape=[jax.ShapeDtypeStruct(a.shape, F32) for a in ws] * 3,
                         compiler_params=pltpu.CompilerParams(vmem_limit_bytes=VMEM_LIMIT))(*ws, *gs, *ms, *vs)
    return out[:n], out[n:2 * n], out[2 * n:]


SIDE_EFFECT = pltpu.SideEffectType.DATAFLOW_SIDE_EFFECTING


def _descriptors(copies, refs, send_sems, recv_sems, sem_off=0):
    x, y, c = lax.axis_index("x"), lax.axis_index("y"), lax.axis_index("c")
    pos = (x, y, c, 2 * x + y)
    out = []
    for i, (s, d, flip) in enumerate(copies):
        peer = (1 - x if "x" in flip else x, 1 - y if "y" in flip else y, 1 - c if "c" in flip else c)
        out.append(pltpu.make_async_remote_copy(
            src_ref=s(refs, refs, pos), dst_ref=d(refs, refs, pos),
            send_sem=send_sems.at[sem_off + i], recv_sem=recv_sems.at[sem_off + i],
            device_id=peer, device_id_type=MESH))
    return out


def _shifted(copies, off):
    return [(lambda I, O, pos, s=s: s(I[off:], O[off:], pos), lambda I, O, pos, d=d: d(I[off:], O[off:], pos), flip)
            for s, d, flip in copies]


BARRIER_IDS = {"c": (1, 2), "ici": (3, 4)}


def _exchange_start(name, bufs, copies, turns, after=None, arrived=None):
    n, nr = len(bufs), len(copies)
    na = 0 if after is None else 1
    nw = 0 if arrived is None else 2
    flips = sorted({flip for _, _, flip in copies})
    kind = "c" if flips == ["c"] else "ici"
    collective_id = BARRIER_IDS[kind][turns[kind] % 2]
    turns[kind] += 1

    def body(*refs):
        x, y, c = lax.axis_index("x"), lax.axis_index("y"), lax.axis_index("c")
        barrier = pltpu.get_barrier_semaphore()
        for flip in flips:
            peer = (1 - x if "x" in flip else x, 1 - y if "y" in flip else y, 1 - c if "c" in flip else c)
            pl.semaphore_signal(barrier, inc=1, device_id=peer, device_id_type=MESH)
        pl.semaphore_wait(barrier, len(flips))
        if arrived is not None:
            for cp in _descriptors(arrived[2], refs[:n], refs[n + na], refs[n + na + 1], arrived[3]):
                cp.wait_send()
                cp.wait_recv()
        for cp in _descriptors(copies, refs[:n], refs[n + na + nw], refs[n + na + nw + 1]):
            cp.start()
        token = refs[2 * n + na + nw + 2]
        token[...] = jnp.zeros_like(token)

    hbm = pl.BlockSpec(memory_space=pltpu.HBM)
    sem = pl.BlockSpec(memory_space=pltpu.SEMAPHORE)
    out = pl.pallas_call(
        body, name=name,
        in_specs=[hbm] * n + [pl.BlockSpec(memory_space=pl.ANY)] * na + [sem] * nw,
        out_specs=(sem, sem, *[hbm] * n, pl.BlockSpec(memory_space=pltpu.VMEM)),
        out_shape=(pltpu.SemaphoreType.DMA((nr,)), pltpu.SemaphoreType.DMA((nr,)),
                   *[pltpu.HBM(b.shape, b.dtype) for b in bufs], jax.ShapeDtypeStruct((SUBLANES, LANES), F32)),
        input_output_aliases={i: 2 + i for i in range(n)},
        compiler_params=pltpu.CompilerParams(has_side_effects=SIDE_EFFECT, collective_id=collective_id),
    )(*[pltpu.with_memory_space_constraint(b, pltpu.HBM) for b in bufs], *([after] * na), *(arrived or ())[:2])
    return out[0], out[1], list(out[2:2 + n]), out[2 + n]


def _exchange_wait(name, send_sems, recv_sems, bufs, copies, after, sem_off=0):
    n = len(bufs)

    def body(*refs):
        for cp in _descriptors(copies, refs[:n], refs[n], refs[n + 1], sem_off):
            cp.wait_send()
            cp.wait_recv()

    hbm = pl.BlockSpec(memory_space=pltpu.HBM)
    sem = pl.BlockSpec(memory_space=pltpu.SEMAPHORE)
    out = pl.pallas_call(
        body, name=name,
        in_specs=[hbm] * n + [sem, sem, pl.BlockSpec(memory_space=pl.ANY)],
        out_specs=tuple([hbm] * n),
        out_shape=tuple(pltpu.HBM(b.shape, b.dtype) for b in bufs),
        input_output_aliases={i: i for i in range(n)},
        compiler_params=pltpu.CompilerParams(has_side_effects=SIDE_EFFECT),
    )(*bufs, send_sems, recv_sems, after)
    return list(out)


FIRST = ("w_in",)
MID = ("ssm_w_glu", "w_out")
LATE = ("w_up", "w_down")
GROUPS = {"first": FIRST, "mid": MID, "late": LATE}
ARRIVALS = {"first": FIRST, "mid": MID, "up": ("w_up",), "down": ("w_down",)}


def _gather_copies(names, shard_shapes):
    def region(i, chip, c):
        half_axis, shard_axis = BIG[names[i]]
        ssize = shard_shapes[i][shard_axis]
        hsize = shard_shapes[i][half_axis] // 2
        return lambda ref: _view(_view(ref, shard_axis, chip * ssize, ssize), half_axis, c * hsize, hsize)

    ici, d2d = [], []
    for i in range(len(names)):
        for flip in FLIPS:
            ici.append((lambda I, O, pos, i=i: region(i, pos[3], pos[2])(I[i]),
                        lambda I, O, pos, i=i: region(i, pos[3], pos[2])(O[i]), flip))
            d2d.append((lambda I, O, pos, i=i, flip=flip: region(i, _peer_chip(pos, flip), pos[2])(I[i]),
                        lambda I, O, pos, i=i, flip=flip: region(i, _peer_chip(pos, flip), pos[2])(O[i]), "c"))
    return ici, d2d


def _half_shape(n, shape):
    r, cdim = shape
    return (r // 2, cdim) if BIG[n][0] == 0 else (r, cdim // 2)


def _sub_shape(n, shape):
    hr, hc = _half_shape(n, shape)
    return (hr, hc // 4) if BIG[n][1] == 1 else (hr // 4, hc)


def _pair_copies(names, shapes, with_pack, dst_off):
    n = len(names)

    def other_half(i, ref, pos):
        half_axis = BIG[names[i]][0]
        hsize = shapes[i][half_axis] // 2
        return _view(ref, half_axis, (1 - pos[2]) * hsize, hsize)

    copies = [(lambda I, O, pos, i=i: other_half(i, I[i], pos), lambda I, O, pos, i=i: O[dst_off + i], "c")
              for i in range(n)]
    if with_pack:
        copies.append((lambda I, O, pos: I[n], lambda I, O, pos: O[dst_off + n], "c"))
    return copies


def _chip_copies(names, shapes, pack_rows, dst_off):
    n = len(names)

    def piece(i, ref, chip):
        shard_axis = BIG[names[i]][1]
        ssize = _sub_shape(names[i], shapes[i])[shard_axis]
        return _view(ref, shard_axis, chip * ssize, ssize)

    copies = []
    for i in range(n):
        for slot, flip in enumerate(FLIPS):
            copies.append((lambda I, O, pos, i=i, flip=flip: piece(i, I[i], _peer_chip(pos, flip)),
                           lambda I, O, pos, i=i, slot=slot: O[dst_off + i].at[slot], flip))
    if pack_rows:
        for slot, flip in enumerate(FLIPS):
            copies.append((lambda I, O, pos: _view(I[n], 0, pos[2] * (pack_rows // 2), pack_rows // 2),
                           lambda I, O, pos, slot=slot: O[dst_off + n].at[slot], flip))
    return copies


class _Exchanges:
    def __init__(self, shards, tiny, kc):
        self.kc = kc
        wb = {n: _cast_into_full(shards[n], kc, BIG[n][1], "cast_" + n) for n in BIG_NAMES}
        self.gathering, self.forwarding, self.pairing, self.reducing = {}, {}, {}, {}
        self.turns = {"c": 0, "ici": 0}
        tiny_copies = [(lambda I, O, pos: I[0], lambda I, O, pos: O[1].at[pos[3]], flip) for flip in FLIPS]
        self.gathering["tiny"] = (0, 0, 2, tiny_copies, None)
        bufs, copies = [tiny, lax.empty((4,) + tiny.shape, F32)], list(tiny_copies)
        for group, names in ARRIVALS.items():
            ici, d2d = _gather_copies(names, [shards[n].shape for n in names])
            self.gathering[group] = (len(bufs), len(copies), len(names), ici, d2d)
            copies += _shifted(ici, len(bufs))
            bufs += [wb[n] for n in names]
        self.started = _exchange_start("gather_start", bufs, copies, self.turns)
        self.zero = self.started[3][0, 0]

    def _arrived(self, group, after):
        buf_off, sem_off, n, ici, _ = self.gathering[group]
        send_sems, recv_sems, bufs, _ = self.started
        return _exchange_wait("gather_%s_wait" % group, send_sems, recv_sems, bufs[buf_off:buf_off + n], ici, after,
                              sem_off)

    def small_params(self, kc):
        tiny, got = self._arrived("tiny", self.started[3])
        return lax.dynamic_update_index_in_dim(got, tiny, kc[0], 0)

    def forward(self, group, after):
        buf_off, sem_off, n, ici, d2d = self.gathering[group]
        send_sems, recv_sems, bufs, _ = self.started
        self.forwarding[group] = (_exchange_start("forward_%s_start" % group, bufs[buf_off:buf_off + n], d2d,
                                                  self.turns, after, (send_sems, recv_sems, ici, sem_off)), d2d)
        return self.forwarding[group][0][3]

    def weights(self, group, after):
        if group not in self.forwarding:
            after = self.forward(group, after)
        (send_sems, recv_sems, bufs, _), d2d = self.forwarding[group]
        full = _exchange_wait("forward_%s_wait" % group, send_sems, recv_sems, bufs, d2d, after)
        return dict(zip(ARRIVALS[group], full))

    def grads_ready(self, group, grads):
        names = GROUPS[group]
        gs = [grads[n] for n in names]
        land = [lax.empty(_half_shape(n, g.shape), F32) for n, g in zip(names, gs)]
        copies = _pair_copies(names, [g.shape for g in gs], False, len(names))
        started = _exchange_start("pair_%s_start" % group, gs + land, copies, self.turns)
        self.pairing[group] = (started, copies)
        return started[3]

    def grads_send(self, group, after):
        names = GROUPS[group]
        n = len(names)
        (send_sems, recv_sems, bufs, _), copies = self.pairing[group]
        bufs = _exchange_wait("pair_%s_wait" % group, send_sems, recv_sems, bufs, copies, after)
        chip = [_pair_sum(bufs[i], bufs[n + i], self.kc, BIG[names[i]][0], "pair_sum_" + names[i], BF16)
                for i in range(n)]
        shapes = [bufs[i].shape for i in range(n)]
        land = [lax.empty((3,) + _sub_shape(names[i], shapes[i]), BF16) for i in range(n)]
        copies = _chip_copies(names, shapes, 0, n)
        started = _exchange_start("reduce_%s_start" % group, chip + land, copies, self.turns)
        self.reducing[group] = (started, copies)
        return started[3]

    def finish_pack(self, pack):
        kc = self.kc
        prow = pack.shape[0] // 2
        recv = _exchange("reduce_d2d", [pack], [jax.ShapeDtypeStruct(pack.shape, F32)], {}, [],
                         _pair_copies((), [], True, 0))
        chip_pack = _pair_sum(pack, recv[0], kc, None, "pair_sum_pack", F32)
        copies = _chip_copies((), [], pack.shape[0], 1)
        land = lax.empty((3, prow, pack.shape[1]), F32)
        pack_sems_s, pack_sems_r, pack_bufs, after = _exchange_start("reduce_pack_start", [chip_pack, land], copies,
                                                                     self.turns)

        names, chips, recvs = (), [], []
        for group, group_names in GROUPS.items():
            (send_sems, recv_sems, bufs, _), group_copies = self.reducing[group]
            bufs = _exchange_wait("reduce_%s_wait" % group, send_sems, recv_sems, bufs, group_copies, after)
            n = len(group_names)
            names, chips, recvs = names + group_names, chips + bufs[:n], recvs + bufs[n:]
            after = bufs[n]
        total = [_chip_sum(chips[i], recvs[i], kc, BIG[n][1], BIG[n][0], "chip_sum_" + n)
                 for i, n in enumerate(names)]

        def my_half(half_axis, ref, pos):
            hsize = ref.shape[half_axis] // 2
            return _view(ref, half_axis, pos[2] * hsize, hsize)

        swap = [(lambda I, O, pos, i=i, n=n: my_half(BIG[n][0], I[i], pos),
                 lambda I, O, pos, i=i, n=n: my_half(BIG[n][0], O[i], pos), "c") for i, n in enumerate(names)]
        self.swapping = (_exchange_start("swap_start", total, swap, self.turns), swap, names)

        chip_pack, recv_pack = _exchange_wait("reduce_pack_wait", pack_sems_s, pack_sems_r, pack_bufs, copies,
                                              self.swapping[0][3])
        total_pack = _chip_sum(chip_pack, recv_pack, kc, None, 0, "chip_sum_pack")
        swap = [(lambda I, O, pos: my_half(0, I[0], pos), lambda I, O, pos: my_half(0, O[0], pos), "c")]
        return _exchange("swap_pack", [total_pack], [jax.ShapeDtypeStruct(pack.shape, F32)], {0: 0}, [], swap)[0]

    def finish_big(self, after):
        (send_sems, recv_sems, bufs, _), swap, names = self.swapping
        return dict(zip(names, _exchange_wait("swap_wait", send_sems, recv_sems, bufs, swap, after)))


WEIGHTS = ("meta_tokens", "norm_mix_g", "w_in", "conv_w", "ssm_lam_re", "ssm_lam_im", "ssm_log_dt", "ssm_b_re",
           "ssm_b_im", "ssm_c_re", "ssm_c_im", "ssm_d", "ssm_w_glu", "gain_conv_out", "gain_ssm_out", "w_out",
           "norm_ffn_g", "w_up", "ffn_conv_w", "ffn_conv_b", "w_down", "norm_final_g")
TINY_SHARDED = ("meta_tokens", "conv_w", "ffn_conv_w")
REPLICATED = tuple(n for n in WEIGHTS if n not in BIG and n not in TINY_SHARDED)
PACK_COLS = 512


def _pack(arrays, row_mult, cols):
    flat = jnp.concatenate([a.reshape(-1).astype(F32) for a in arrays])
    n = flat.shape[0]
    total = -(-n // (row_mult * cols)) * (row_mult * cols)
    return jnp.concatenate([flat, jnp.zeros((total - n,), F32)]).reshape(total // cols, cols)


def _unpack(packed, shapes):
    flat = packed.reshape(-1)
    out, off = [], 0
    for s in shapes:
        n = math.prod(s)
        out.append(flat[off:off + n].reshape(s))
        off += n
    return out


def kernel(x, meta_tokens, norm_mix_g, w_in, conv_w, ssm_lam_re, ssm_lam_im, ssm_log_dt, ssm_b_re, ssm_b_im, ssm_c_re, ssm_c_im, ssm_d, ssm_w_glu, gain_conv_out, gain_ssm_out, w_out, norm_ffn_g, w_up, ffn_conv_w, ffn_conv_b, w_down, norm_final_g, loss_target, m_meta_tokens, m_norm_mix_g, m_w_in, m_conv_w, m_ssm_lam_re, m_ssm_lam_im, m_ssm_log_dt, m_ssm_b_re, m_ssm_b_im, m_ssm_c_re, m_ssm_c_im, m_ssm_d, m_ssm_w_glu, m_gain_conv_out, m_gain_ssm_out, m_w_out, m_norm_ffn_g, m_w_up, m_ffn_conv_w, m_ffn_conv_b, m_w_down, m_norm_final_g, v_meta_tokens, v_norm_mix_g, v_w_in, v_conv_w, v_ssm_lam_re, v_ssm_lam_im, v_ssm_log_dt, v_ssm_b_re, v_ssm_b_im, v_ssm_c_re, v_ssm_c_im, v_ssm_d, v_ssm_w_glu, v_gain_conv_out, v_gain_ssm_out, v_w_out, v_norm_ffn_g, v_w_up, v_ffn_conv_w, v_ffn_conv_b, v_w_down, v_norm_final_g):
    args = dict(locals())
    w = {n: args[n] for n in WEIGHTS}
    mom = {n: args["m_" + n] for n in WEIGHTS}
    var = {n: args["v_" + n] for n in WEIGHTS}
    kx, ky, kc_ = lax.axis_index("x"), lax.axis_index("y"), lax.axis_index("c")
    chip = 2 * kx + ky
    kc = jnp.stack([chip, kc_]).astype(jnp.int32)

    def squeeze(n, a):
        if n == "meta_tokens":
            return a
        if n == "norm_final_g":
            return a.reshape(1, -1)
        a = a[0]
        return a.reshape(1, -1) if a.ndim == 1 else a

    wl = {n: squeeze(n, w[n]) for n in WEIGHTS}
    ml = {n: squeeze(n, mom[n]) for n in WEIGHTS}
    vl = {n: squeeze(n, var[n]) for n in WEIGHTS}

    tiny = _pack([wl[n] for n in TINY_SHARDED], SUBLANES, LANES)
    ex = _Exchanges({n: wl[n] for n in BIG_NAMES}, tiny, kc)
    tiny_shapes = [wl[n].shape for n in TINY_SHARDED]
    tiny_all = ex.small_params(kc)
    tiny_parts = [_unpack(tiny_all[k], tiny_shapes) for k in range(4)]
    p = {n: wl[n] for n in WEIGHTS if n not in BIG}
    for j, n in enumerate(TINY_SHARDED):
        p[n] = jnp.concatenate([tiny_parts[k][j] for k in range(4)], axis=1)
    p["ssm_log_dt"] = wl["ssm_log_dt"].reshape(-1)

    loss_local, grad_x, grads = _local_step(x[0], loss_target[0], p, ex)

    small_names = REPLICATED + TINY_SHARDED
    small_shapes = [tuple(grads[n].shape) for n in small_names] + [(1,)]
    pack = _pack([grads[n] for n in small_names] + [loss_local.reshape(1)], 2 * 16, PACK_COLS)
    g_pack = ex.finish_pack(pack)
    g_small = dict(zip(small_names + ("loss",), _unpack(g_pack, small_shapes)))
    loss = g_small["loss"][0]
    swapped = ("ssm_b_re", "ssm_b_im")

    def view(n, a):
        if n in swapped:
            return jnp.swapaxes(a, -1, -2)
        return a.reshape(1, -1) if a.ndim == 1 else a

    g = {}
    for n in REPLICATED:
        g[n] = g_small[n].reshape(view(n, w[n]).shape)
    for n in TINY_SHARDED:
        cols = wl[n].shape[1]
        g[n] = lax.dynamic_slice_in_dim(g_small[n], chip * cols, cols, axis=1).reshape(w[n].shape)
    delta, new_m, new_v = {}, {}, {}
    small = [[view(n, d[n]) for n in small_names] for d in (w, mom, var)]
    small.insert(1, [g[n] for n in small_names])
    for d, outs in zip((delta, new_m, new_v), _adamw_whole(*small, "adamw_small")):
        d.update(zip(small_names, outs))
    for d in (g, delta, new_m, new_v):
        d.update({n: jnp.swapaxes(d[n], -1, -2) for n in swapped})
    g_big = ex.finish_big(delta[small_names[0]])
    for n in BIG_NAMES:
        g[n], delta[n], new_m[n], new_v[n] = _adamw(wl[n], g_big[n], ml[n], vl[n], "adamw_" + n)

    def like(n, a):
        return a.reshape(w[n].shape)

    return (loss, grad_x[None], *[like(n, g[n]) for n in WEIGHTS], *[like(n, delta[n]) for n in WEIGHTS],
            *[like(n, new_m[n]) for n in WEIGHTS], *[like(n, new_v[n]) for n in WEIGHTS])
```

```python
import functools
import math

import jax
import jax.numpy as jnp
from jax import lax
from jax.experimental import pallas as pl
from jax.experimental.pallas import tpu as pltpu

F32 = jnp.float32
BF16 = jnp.bfloat16
MESH = pl.DeviceIdType.MESH

N_META = 16
N_GROUPS = 32
GROUP = 16
STATE = 64
RMS_EPS = 1e-6
ADAM_LR = 0.001
ADAM_B1 = 0.9
ADAM_B2 = 0.999
ADAM_EPS = 1e-08
ADAM_WD = 0.01
ADAM_STEP = 10

LANES = 128
SUBLANES = 8
ROW_ALIGN = 128
ROW_TILES = 4
VMEM_LIMIT = 52 * 1024 * 1024
MM_VMEM_BUDGET = 40 * 1024 * 1024
GELU_C = math.sqrt(2.0 / math.pi)
GELU_A = 0.044715


def _cparams(*sem):
    return pltpu.CompilerParams(dimension_semantics=sem, vmem_limit_bytes=VMEM_LIMIT)


def _pick_tile(dim, cap, mult):
    best = None
    for t in range(mult, min(dim, cap) + 1, mult):
        if dim % t == 0:
            best = t
    return best if best is not None else dim


def _mm(a, b, mode, name, out_dtype=F32, acc_in=None, after=None):
    if mode == "tn":
        kdim, m = a.shape
    else:
        m, kdim = a.shape
    n = b.shape[0] if mode == "nt" else b.shape[1]
    tm = _pick_tile(m, 1408, LANES if mode == "tn" else 16)
    tk = _pick_tile(kdim, 2816, LANES)
    nk = kdim // tk
    out_bytes = jnp.dtype(out_dtype).itemsize
    for cap in (704, 512, 256, LANES) if m == tm else (1408, 1024, 512, 256, LANES):
        tn = _pick_tile(n, cap, LANES)
        blocks = 2 * (tm * tk * 2 + tk * tn * 2 + tm * tn * out_bytes * (2 if acc_in is not None else 1))
        if blocks + (tm * tn * 4 if nk > 1 else 0) <= MM_VMEM_BUDGET:
            break
    has_acc = acc_in is not None

    def body(*refs):
        if after is not None:
            refs = refs[1:]
        if has_acc:
            a_ref, b_ref, c_ref, o_ref = refs[:4]
            rest = refs[4:]
        else:
            a_ref, b_ref, o_ref = refs[:3]
            c_ref = None
            rest = refs[3:]
        if mode == "nn":
            p = jnp.dot(a_ref[...], b_ref[...], preferred_element_type=F32)
        elif mode == "nt":
            p = lax.dot_general(a_ref[...], b_ref[...], (((1,), (1,)), ((), ())), preferred_element_type=F32)
        else:
            p = lax.dot_general(a_ref[...], b_ref[...], (((0,), (0,)), ((), ())), preferred_element_type=F32)
        if nk == 1:
            if has_acc:
                p = p + c_ref[...]
            o_ref[...] = p.astype(out_dtype)
        else:
            acc_ref = rest[0]
            k = pl.program_id(2)

            @pl.when(k == 0)
            def _():
                acc_ref[...] = p + c_ref[...] if has_acc else p

            @pl.when(k > 0)
            def _():
                acc_ref[...] += p

            @pl.when(k == nk - 1)
            def _():
                o_ref[...] = acc_ref[...].astype(out_dtype)

    if mode == "tn":
        a_spec = pl.BlockSpec((tk, tm), lambda i, j, k: (k, i))
    else:
        a_spec = pl.BlockSpec((tm, tk), lambda i, j, k: (i, k))
    if mode == "nt":
        b_spec = pl.BlockSpec((tn, tk), lambda i, j, k: (j, k))
    else:
        b_spec = pl.BlockSpec((tk, tn), lambda i, j, k: (k, j))
    o_spec = pl.BlockSpec((tm, tn), lambda i, j, k: (i, j))
    in_specs = [a_spec, b_spec] + ([o_spec] if has_acc else [])
    args = (a, b) + ((acc_in,) if has_acc else ())
    if after is not None:
        in_specs = [pl.BlockSpec(memory_space=pl.ANY)] + in_specs
        args = (after,) + args
    return pl.pallas_call(
        body, name=name, grid=(m // tm, n // tn, nk),
        in_specs=in_specs, out_specs=o_spec,
        out_shape=jax.ShapeDtypeStruct((m, n), out_dtype),
        scratch_shapes=[pltpu.VMEM((tm, tn), F32)] if nk > 1 else [],
        compiler_params=_cparams("parallel", "parallel", "arbitrary"),
    )(*args)


def _mm_rows(a, b, mode, name, ins, outs, epilogue, scratch=()):
    m, kdim = a.shape
    n = b.shape[0] if mode == "nt" else b.shape[1]
    tm = m // ROW_TILES
    tk = _pick_tile(kdim, 2816, LANES)
    nk = kdim // tk
    ni, no = len(ins), len(outs)

    def body(*refs):
        a_ref, b_ref = refs[:2]
        in_refs, out_refs, rest = refs[2:2 + ni], refs[2 + ni:2 + ni + no], refs[2 + ni + no:]
        k, i = pl.program_id(0), pl.program_id(1)
        if mode == "nn":
            p = jnp.dot(a_ref[...], b_ref[...], preferred_element_type=F32)
        else:
            p = lax.dot_general(a_ref[...], b_ref[...], (((1,), (1,)), ((), ())), preferred_element_type=F32)
        if nk == 1:
            epilogue(p, i, in_refs, out_refs, rest)
        else:
            acc_ref = rest[0]
            rows = pl.ds(pl.multiple_of(i * tm, SUBLANES), tm)

            @pl.when(k == 0)
            def _():
                acc_ref[rows, :] = p

            @pl.when(jnp.logical_and(k > 0, k < nk - 1))
            def _():
                acc_ref[rows, :] += p

            @pl.when(k == nk - 1)
            def _():
                epilogue(acc_ref[rows, :] + p, i, in_refs, out_refs, rest[1:])

    tile = (lambda k, i: i) if nk == 1 else (lambda k, i: jnp.where(k == nk - 1, i, 0))

    def spec(shape, kind):
        if kind == "rows":
            return pl.BlockSpec((tm,) + tuple(shape[1:]), lambda k, i: (tile(k, i),) + (0,) * (len(shape) - 1))
        if kind == "whole":
            return pl.BlockSpec(tuple(shape), lambda k, i: (0,) * len(shape))
        return pl.BlockSpec(memory_space=pl.ANY)

    a_spec = pl.BlockSpec((tm, tk), lambda k, i: (i, k))
    b_spec = pl.BlockSpec((n, tk), lambda k, i: (0, k)) if mode == "nt" else pl.BlockSpec((tk, n), lambda k, i: (k, 0))
    return pl.pallas_call(
        body, name=name, grid=(nk, ROW_TILES),
        in_specs=[a_spec, b_spec] + [spec(x.shape, kind) for x, kind in ins],
        out_specs=[spec(shape, kind) for shape, _, kind in outs],
        out_shape=[jax.ShapeDtypeStruct(shape, dtype) for shape, dtype, _ in outs],
        scratch_shapes=([pltpu.VMEM((m, n), F32)] if nk > 1 else []) + list(scratch),
        compiler_params=_cparams("arbitrary", "arbitrary"),
    )(a, b, *[x for x, _ in ins])


def _rows(shape_cols, tr, dtype=None):
    return pl.BlockSpec((tr, shape_cols), lambda i: (i, 0))


def _const(shape):
    return pl.BlockSpec(shape, lambda i: (0,) * len(shape))


def _rms(x):
    return lax.rsqrt(jnp.mean(x * x, axis=-1, keepdims=True) + RMS_EPS)


def _rms_bwd(x, r, g, dy):
    xn = x * r
    dxn = dy * g
    dx = r * (dxn - xn * jnp.mean(dxn * xn, axis=-1, keepdims=True))
    return dx, dy * xn


def _gelu(y):
    return 0.5 * y * (1.0 + jnp.tanh(GELU_C * (y + GELU_A * y * y * y)))


def _gelu_grad(y):
    t = jnp.tanh(GELU_C * (y + GELU_A * y * y * y))
    return 0.5 * (1.0 + t) + 0.5 * y * (1.0 - t * t) * GELU_C * (1.0 + 3.0 * GELU_A * y * y)


def _sigmoid(z):
    return 1.0 / (1.0 + jnp.exp(-z))


def _proj_res_norm(a, w, h, g, after, name):
    def epilogue(p, i, ins, outs, _):
        x = ins[0][...] + p
        outs[0][...] = x
        outs[1][...] = (x * _rms(x) * ins[1][...]).astype(BF16)

    return _mm_rows(a, w, "nn", name, [(h, "rows"), (g, "whole"), (after, "hbm")],
                    [(h.shape, F32, "rows"), (h.shape, BF16, "rows")], epilogue)


def _proj_norm_bwd(da, w, h, g, dres, after, name):
    d = h.shape[1]

    def epilogue(p, i, ins, outs, _):
        x = ins[0][...]
        dx, dgs = _rms_bwd(x, _rms(x), ins[1][...], p)
        dh = ins[2][...] + dx
        outs[0][...] = dh
        outs[1][...] = dh.astype(BF16)

        @pl.when(i == 0)
        def _():
            outs[2][...] = jnp.zeros_like(outs[2])

        outs[2][...] += jnp.sum(dgs, axis=0, keepdims=True)

    return _mm_rows(da, w, "nt", name, [(h, "rows"), (g, "whole"), (dres, "rows"), (after, "hbm")],
                    [(h.shape, F32, "rows"), (h.shape, BF16, "rows"), ((1, d), F32, "whole")], epilogue)


def _proj_input_norm_bwd(da, w, h, g, dres, after, n_real, name):
    tp, d = h.shape
    tr = tp // ROW_TILES

    def epilogue(p, i, ins, outs, scratch):
        h_ref, g_ref, dres_ref, _ = ins
        dx_ref, dmeta_ref, dg_ref = outs
        stage, sem = scratch
        x = h_ref[...]
        dx, dgs = _rms_bwd(x, _rms(x), g_ref[...], p)
        stage[...] = dres_ref[...] + dx

        @pl.when(i == 0)
        def _():
            dg_ref[...] = jnp.zeros_like(dg_ref)
            dmeta_ref[...] = stage[:N_META, :]

        dg_ref[...] += jnp.sum(dgs, axis=0, keepdims=True)
        for t in range(ROW_TILES):
            lo, hi = max(t * tr, N_META), min((t + 1) * tr, n_real)
            if hi > lo:
                @pl.when(i == t)
                def _(t=t, lo=lo, hi=hi):
                    cp = pltpu.make_async_copy(stage.at[pl.ds(lo - t * tr, hi - lo), :],
                                               dx_ref.at[pl.ds(lo - N_META, hi - lo), :], sem)
                    cp.start()
                    cp.wait()

    return _mm_rows(da, w, "nt", name, [(h, "rows"), (g, "whole"), (dres, "rows"), (after, "hbm")],
                    [((n_real - N_META, d), F32, "hbm"), ((N_META, d), F32, "whole"), ((1, d), F32, "whole")],
                    epilogue, scratch=[pltpu.VMEM((tr, d), F32), pltpu.SemaphoreType.DMA])


def _load_token_rows(tok_hbm, buf, sem, tr, n_real, head=None, wait=False, i=None):
    i = pl.program_id(0) if i is None else i
    for t in range(ROW_TILES):
        base = t * tr
        lo, hi = max(base, N_META), min(base + tr, n_real)

        @pl.when(i == t)
        def _(base=base, lo=lo, hi=hi):
            if hi > lo:
                cp = pltpu.make_async_copy(tok_hbm.at[pl.ds(lo - N_META, hi - lo), :],
                                           buf.at[pl.ds(lo - base, hi - lo), :], sem)
                if wait:
                    cp.wait()
                    return
                cp.start()
            if wait:
                return
            if base < N_META:
                buf[0:N_META - base, :] = (jnp.zeros((N_META - base, buf.shape[1]), F32) if head is None
                                           else head[base:N_META, :])
            if hi < base + tr:
                buf[max(hi, base) - base:tr, :] = jnp.zeros((base + tr - max(hi, base), buf.shape[1]), F32)


def _input_norm_fwd(x, meta, g, tp, name):
    seq, d = x.shape
    tr = tp // ROW_TILES
    n_real = N_META + seq

    def body(x_hbm, meta_ref, g_ref, h_ref, hn_ref, buf, sem):
        _load_token_rows(x_hbm, buf, sem, tr, n_real, head=meta_ref)
        _load_token_rows(x_hbm, buf, sem, tr, n_real, wait=True)
        h = buf[...]
        h_ref[...] = h
        hn_ref[...] = (h * _rms(h) * g_ref[...]).astype(BF16)

    return pl.pallas_call(
        body, name=name, grid=(ROW_TILES,),
        in_specs=[pl.BlockSpec(memory_space=pl.ANY), _const((N_META, d)), _const((1, d))],
        out_specs=[_rows(d, tr), _rows(d, tr)],
        out_shape=[jax.ShapeDtypeStruct((tp, d), F32), jax.ShapeDtypeStruct((tp, d), BF16)],
        scratch_shapes=[pltpu.VMEM((tr, d), F32), pltpu.SemaphoreType.DMA],
        compiler_params=_cparams("arbitrary"))(x, meta, g)


def _proj_loss_bwd(act, w, h1, target, g, n_real, name):
    tp, d = h1.shape
    tr = tp // ROW_TILES

    def epilogue(p, i, ins, outs, scratch):
        h1_ref, t_hbm, g_ref = ins
        loss_ref, dh_ref, dhb_ref, dg_ref = outs
        t_buf, sem = scratch
        _load_token_rows(t_hbm, t_buf, sem, tr, n_real, i=i)
        x = h1_ref[...] + p
        r = _rms(x)
        row = i * tr + lax.broadcasted_iota(jnp.int32, (tr, d), 0)
        valid = (row >= N_META) & (row < n_real)
        _load_token_rows(t_hbm, t_buf, sem, tr, n_real, wait=True, i=i)
        e = jnp.where(valid, x * r * g_ref[...] - t_buf[...], 0.0)
        dx, dgs = _rms_bwd(x, r, g_ref[...], e * (1.0 / d))
        dh_ref[...] = dx
        dhb_ref[...] = dx.astype(BF16)

        @pl.when(i == 0)
        def _():
            dg_ref[...] = jnp.zeros_like(dg_ref)
            loss_ref[...] = jnp.zeros_like(loss_ref)

        dg_ref[...] += jnp.sum(dgs, axis=0, keepdims=True)
        loss_ref[...] += (0.5 / d) * jnp.sum(jnp.sum(e * e, axis=0, keepdims=True), axis=1, keepdims=True)

    return _mm_rows(act, w, "nn", name, [(h1, "rows"), (target, "hbm"), (g, "whole")],
                    [((1, LANES), F32, "whole"), ((tp, d), F32, "rows"), ((tp, d), BF16, "rows"),
                     ((1, d), F32, "whole")],
                    epilogue, scratch=[pltpu.VMEM((tr, d), F32), pltpu.SemaphoreType.DMA])


def _mix_fwd(co, y, z, gc, gs, name):
    tp, dh = co.shape
    tr = tp // ROW_TILES

    def body(co_ref, y_ref, z_ref, gc_ref, gs_ref, m_ref):
        c = co_ref[...]
        m_ref[:, :dh] = (c * _rms(c) * gc_ref[...]).astype(BF16)
        so = _gelu(y_ref[...]) * _sigmoid(z_ref[...])
        m_ref[:, dh:] = (so * _rms(so) * gs_ref[...]).astype(BF16)

    return pl.pallas_call(
        body, name=name, grid=(ROW_TILES,),
        in_specs=[_rows(dh, tr)] * 3 + [_const((1, dh))] * 2,
        out_specs=_rows(2 * dh, tr),
        out_shape=jax.ShapeDtypeStruct((tp, 2 * dh), BF16),
        compiler_params=_cparams("parallel"))(co, y, z, gc, gs)


def _proj_mix_bwd(dh1b, w, co, y, z, gc, gs, name):
    tp, dh = co.shape

    def epilogue(p, i, ins, outs, _):
        co_ref, y_ref, z_ref, gc_ref, gs_ref = ins
        dco_ref, dz_ref, dgp_ref, dgc_ref, dgs_ref = outs
        c = co_ref[...]
        dco, dgc = _rms_bwd(c, _rms(c), gc_ref[...], p[:, :dh])
        dco_ref[...] = dco
        gl = _gelu(y_ref[...])
        sg = _sigmoid(z_ref[...])
        so = gl * sg
        dso, dgs = _rms_bwd(so, _rms(so), gs_ref[...], p[:, dh:])
        dz_ref[...] = (dso * gl * sg * (1.0 - sg)).astype(BF16)
        dgp_ref[...] = dso * sg

        @pl.when(i == 0)
        def _():
            dgc_ref[...] = jnp.zeros_like(dgc_ref)
            dgs_ref[...] = jnp.zeros_like(dgs_ref)

        dgc_ref[...] += jnp.sum(dgc, axis=0, keepdims=True)
        dgs_ref[...] += jnp.sum(dgs, axis=0, keepdims=True)

    return _mm_rows(dh1b, w, "nt", name,
                    [(co, "rows"), (y, "rows"), (z, "rows"), (gc, "whole"), (gs, "whole")],
                    [((tp, dh), F32, "rows"), ((tp, dh), BF16, "rows"), ((tp, dh), F32, "rows"),
                     ((1, dh), F32, "whole"), ((1, dh), F32, "whole")], epilogue)


def _shift_down(x, k):
    row = lax.broadcasted_iota(jnp.int32, x.shape, 0)
    return jnp.where(row >= k, pltpu.roll(x, k, 0), 0.0)


def _shift_up(x, k):
    n = x.shape[0]
    row = lax.broadcasted_iota(jnp.int32, x.shape, 0)
    return jnp.where(row < n - k, pltpu.roll(x, n - k, 0), 0.0)


def _dwconv(x, w_ref):
    return w_ref[2:3, :] * x + w_ref[1:2, :] * _shift_down(x, 1) + w_ref[0:1, :] * _shift_down(x, 2)


def _dwconv_bwd(x, dy, w_ref):
    dx = w_ref[2:3, :] * dy + w_ref[1:2, :] * _shift_up(dy, 1) + w_ref[0:1, :] * _shift_up(dy, 2)
    dw = jnp.concatenate([jnp.sum(dy * _shift_down(x, 2), axis=0, keepdims=True),
                          jnp.sum(dy * _shift_down(x, 1), axis=0, keepdims=True),
                          jnp.sum(dy * x, axis=0, keepdims=True)], axis=0)
    return dx, dw


def _interleave(dst, src):
    seg_rows = src.shape[0] // SUBLANES
    for seg in range(SUBLANES):
        dst[pl.ds(seg, seg_rows, stride=SUBLANES), :] = src[seg * seg_rows:(seg + 1) * seg_rows, :]


def _deinterleave(dst, src):
    seg_rows = src.shape[0] // SUBLANES
    for seg in range(SUBLANES):
        dst[seg * seg_rows:(seg + 1) * seg_rows, :] = src[pl.ds(seg, seg_rows, stride=SUBLANES), :]


def _segment_shift(x, reverse):
    row = lax.broadcasted_iota(jnp.int32, x.shape, 0)
    if reverse:
        return jnp.where(row < SUBLANES - 1, pltpu.roll(x, SUBLANES - 1, 0), 0.0)
    return jnp.where(row >= 1, pltpu.roll(x, 1, 0), 0.0)


def _scan(s_re, s_im, pw_ref, reverse, pair=None):
    n_steps = s_re.shape[0] // SUBLANES
    n_strips = s_re.shape[1] // LANES
    sign = -1.0 if reverse else 1.0
    strips = [slice(st * LANES, (st + 1) * LANES) for st in range(n_strips)]

    def rows_of(j):
        step = (n_steps - 1 - j) if reverse else j
        return pl.ds(pl.multiple_of(step * SUBLANES, SUBLANES), SUBLANES)

    a = [(jnp.broadcast_to(pw_ref[0, 0:1, lanes], (SUBLANES, LANES)),
          sign * jnp.broadcast_to(pw_ref[1, 0:1, lanes], (SUBLANES, LANES))) for lanes in strips]

    def local(i, carry):
        for half in range(2):
            rows = rows_of(2 * i + half)
            out = []
            for st, lanes in enumerate(strips):
                (ar, ai), cr, ci = a[st], carry[2 * st], carry[2 * st + 1]
                xr = s_re[rows, lanes] + (ar * cr - ai * ci)
                xi = s_im[rows, lanes] + (ar * ci + ai * cr)
                s_re[rows, lanes] = xr
                s_im[rows, lanes] = xi
                out += [xr, xi]
            carry = tuple(out)
        return carry

    zero = jnp.zeros((SUBLANES, LANES), F32)
    ends = lax.fori_loop(0, n_steps // 2, local, (zero,) * (2 * n_strips))

    entering = []
    row = lax.broadcasted_iota(jnp.int32, (SUBLANES, LANES), 0)
    for st, lanes in enumerate(strips):
        tr, ti = ends[2 * st], ends[2 * st + 1]
        mr = jnp.broadcast_to(pw_ref[0, n_steps - 1:n_steps, lanes], (SUBLANES, LANES))
        mi = sign * jnp.broadcast_to(pw_ref[1, n_steps - 1:n_steps, lanes], (SUBLANES, LANES))
        for k in (1, 2, 4):
            keep = (row < SUBLANES - k) if reverse else (row >= k)
            rr = jnp.where(keep, pltpu.roll(tr, SUBLANES - k if reverse else k, 0), 0.0)
            ri = jnp.where(keep, pltpu.roll(ti, SUBLANES - k if reverse else k, 0), 0.0)
            tr, ti = tr + (mr * rr - mi * ri), ti + (mr * ri + mi * rr)
            mr, mi = mr * mr - mi * mi, 2.0 * mr * mi
        entering += [_segment_shift(tr, reverse), _segment_shift(ti, reverse)]

    def fix(i, carry):
        carry, sums = carry[:2 * n_strips], carry[2 * n_strips:]
        for half in range(2):
            j = 2 * i + half
            rows = rows_of(j)
            out, acc = [], []
            for st, lanes in enumerate(strips):
                (ar, ai), cr, ci = a[st], carry[2 * st], carry[2 * st + 1]
                cr, ci = ar * cr - ai * ci, ar * ci + ai * cr
                xr = s_re[rows, lanes] + cr
                xi = s_im[rows, lanes] + ci
                s_re[rows, lanes] = xr
                s_im[rows, lanes] = xi
                out += [cr, ci]
                if pair is not None:
                    p_rows = rows_of(jnp.minimum(j + 1, n_steps - 1))
                    keep = (j < n_steps - 1).astype(F32)
                    pr = pair[0][p_rows, lanes] * keep
                    pi = pair[1][p_rows, lanes] * keep
                    acc += [sums[2 * st] + (xr * pr + xi * pi), sums[2 * st + 1] + (xi * pr - xr * pi)]
            carry, sums = tuple(out), tuple(acc)
        return carry + sums

    n_sums = 0 if pair is None else 2 * n_strips
    out = lax.fori_loop(0, n_steps // 2, fix, tuple(entering) + (zero,) * n_sums)
    return out[2 * n_strips:]


def _seq_fwd(proj, conv_w, bc_re, bc_im, cc_re, cc_im, dskip, a_pow, name):
    tp = proj.shape[0]
    dh = proj.shape[1] // 4
    nq = dh // LANES
    sw = STATE * N_GROUPS // nq

    def body(b_ref, c_ref, v_ref, u_ref, w_ref, bre_ref, bim_ref, cre_ref, cim_ref, d_ref, pw_ref,
             co_ref, y_ref, g_ref, s_re, s_im, u_il, y_il):
        co_ref[...] = b_ref[...] * _dwconv(c_ref[...] * v_ref[...], w_ref)
        _interleave(u_il, u_ref)
        ub = u_il[...].astype(BF16)
        s_re[...] = jnp.dot(ub, bre_ref[...], preferred_element_type=F32)
        s_im[...] = jnp.dot(ub, bim_ref[...], preferred_element_type=F32)
        _scan(s_re, s_im, pw_ref, False)
        y_il[...] = (jnp.dot(s_re[...].astype(BF16), cre_ref[...], preferred_element_type=F32)
                     - jnp.dot(s_im[...].astype(BF16), cim_ref[...], preferred_element_type=F32))
        _deinterleave(y_ref, y_il)
        y = y_ref[...] + d_ref[...] * u_ref[...]
        y_ref[...] = y
        g_ref[...] = _gelu(y).astype(BF16)

    col = lambda off: pl.BlockSpec((tp, LANES), lambda q, off=off: (0, off * nq + q))
    blk = pl.BlockSpec((tp, LANES), lambda q: (0, q))
    return pl.pallas_call(
        body, name=name, grid=(nq,),
        in_specs=[col(0), col(1), col(2), col(3),
                  pl.BlockSpec((3, LANES), lambda q: (0, q)),
                  pl.BlockSpec((LANES, sw), lambda q: (0, q)), pl.BlockSpec((LANES, sw), lambda q: (0, q)),
                  pl.BlockSpec((sw, LANES), lambda q: (q, 0)), pl.BlockSpec((sw, LANES), lambda q: (q, 0)),
                  pl.BlockSpec((1, LANES), lambda q: (0, q)),
                  pl.BlockSpec((2, tp // SUBLANES, sw), lambda q: (0, 0, q))],
        out_specs=[blk, blk, blk, pl.BlockSpec((tp, sw), lambda q: (0, q)), pl.BlockSpec((tp, sw), lambda q: (0, q))],
        out_shape=[jax.ShapeDtypeStruct((tp, dh), F32), jax.ShapeDtypeStruct((tp, dh), F32),
                   jax.ShapeDtypeStruct((tp, dh), BF16),
                   jax.ShapeDtypeStruct((tp, nq * sw), F32), jax.ShapeDtypeStruct((tp, nq * sw), F32)],
        scratch_shapes=[pltpu.VMEM((tp, LANES), F32), pltpu.VMEM((tp, LANES), F32)],
        compiler_params=_cparams("parallel"),
    )(proj, proj, proj, proj, conv_w, bc_re, bc_im, cc_re, cc_im, dskip, a_pow)


def _conv_bwd(proj, dco, conv_w, name):
    tp = proj.shape[0]
    dh = proj.shape[1] // 4
    nq = dh // LANES

    def body(b_ref, c_ref, v_ref, dco_ref, w_ref, dproj_ref, dw_ref, stage, sem):
        q = pl.program_id(0)
        cg = c_ref[...]
        vg = v_ref[...]
        cv = cg * vg
        dco_v = dco_ref[...]
        dcv, dw = _dwconv_bwd(cv, dco_v * b_ref[...], w_ref)
        dw_ref[...] = dw
        stage[0] = (dco_v * _dwconv(cv, w_ref)).astype(BF16)
        stage[1] = (dcv * vg).astype(BF16)
        stage[2] = (dcv * cg).astype(BF16)
        copies = [pltpu.make_async_copy(stage.at[p], dproj_ref.at[:, pl.ds((p * nq + q) * LANES, LANES)], sem.at[p])
                  for p in range(3)]
        for cp in copies:
            cp.start()
        for cp in copies:
            cp.wait()

    col = lambda off: pl.BlockSpec((tp, LANES), lambda q, off=off: (0, off * nq + q))
    return pl.pallas_call(
        body, name=name, grid=(nq,),
        in_specs=[col(0), col(1), col(2), pl.BlockSpec((tp, LANES), lambda q: (0, q)),
                  pl.BlockSpec((3, LANES), lambda q: (0, q))],
        out_specs=[pl.BlockSpec(memory_space=pl.ANY), pl.BlockSpec((3, LANES), lambda q: (0, q))],
        out_shape=[jax.ShapeDtypeStruct((tp, 4 * dh), BF16), jax.ShapeDtypeStruct((3, dh), F32)],
        scratch_shapes=[pltpu.VMEM((3, tp, LANES), BF16), pltpu.SemaphoreType.DMA((3,))],
        compiler_params=_cparams("arbitrary"),
    )(proj, proj, proj, dco, conv_w)


def _ssm_bwd(proj, y, dg, dproj, states, bc_re, bc_im, cc_re, cc_im, dskip, a_pow, name):
    tp = proj.shape[0]
    dh = proj.shape[1] // 4
    nq = dh // LANES
    sw = STATE * N_GROUPS // nq

    def body(u_ref, y_ref, dg_ref, dproj_in, s_re, s_im, bre_ref, bim_ref, cre_ref, cim_ref, d_ref, pw_ref,
             dproj_ref, dbre_ref, dbim_ref, dcre_ref, dcim_ref, dd_ref, dar_ref, dai_ref,
             l_re, l_im, a_il, b_il, stage, sem):
        del dproj_in
        q = pl.program_id(0)
        nt = (((1,), (1,)), ((), ()))
        tn = (((0,), (0,)), ((), ()))
        _interleave(a_il, u_ref)
        ub = a_il[...].astype(BF16)
        dy_rows = dg_ref[...] * _gelu_grad(y_ref[...])
        dd_ref[...] = jnp.sum(dy_rows * u_ref[...], axis=0, keepdims=True)
        _interleave(b_il, dy_rows)
        dy = b_il[...]
        dyb = dy.astype(BF16)
        l_re[...] = lax.dot_general(dyb, cre_ref[...], nt, preferred_element_type=F32)
        l_im[...] = -lax.dot_general(dyb, cim_ref[...], nt, preferred_element_type=F32)
        dcre_ref[...] = lax.dot_general(s_re[...].astype(BF16), dyb, tn, preferred_element_type=F32)
        dcim_ref[...] = -lax.dot_general(s_im[...].astype(BF16), dyb, tn, preferred_element_type=F32)
        sums = _scan(l_re, l_im, pw_ref, True, pair=(s_re, s_im))
        rest = tp - SUBLANES
        for st in range(sw // LANES):
            lanes = slice(st * LANES, (st + 1) * LANES)
            lr0, li0 = l_re[:SUBLANES, lanes], l_im[:SUBLANES, lanes]
            pr0, pi0 = _segment_shift(s_re[rest:, lanes], False), _segment_shift(s_im[rest:, lanes], False)
            dar_ref[:, lanes] = jnp.sum(sums[2 * st] + (lr0 * pr0 + li0 * pi0), axis=0, keepdims=True)
            dai_ref[:, lanes] = jnp.sum(sums[2 * st + 1] + (li0 * pr0 - lr0 * pi0), axis=0, keepdims=True)
        lrb = l_re[...].astype(BF16)
        lib = l_im[...].astype(BF16)
        a_il[...] = (dy * d_ref[...] + lax.dot_general(lrb, bre_ref[...], nt, preferred_element_type=F32)
                     + lax.dot_general(lib, bim_ref[...], nt, preferred_element_type=F32))
        _deinterleave(b_il, a_il)
        stage[...] = b_il[...].astype(BF16)
        dbre_ref[...] = lax.dot_general(ub, lrb, tn, preferred_element_type=F32)
        dbim_ref[...] = lax.dot_general(ub, lib, tn, preferred_element_type=F32)
        cp = pltpu.make_async_copy(stage, dproj_ref.at[:, pl.ds((3 * nq + q) * LANES, LANES)], sem)
        cp.start()
        cp.wait()

    blk = pl.BlockSpec((tp, LANES), lambda q: (0, q))
    bspec = pl.BlockSpec((LANES, sw), lambda q: (0, q))
    cspec = pl.BlockSpec((sw, LANES), lambda q: (q, 0))
    tspec = pl.BlockSpec((2, tp // SUBLANES, sw), lambda q: (0, 0, q))
    nstate = STATE * N_GROUPS
    return pl.pallas_call(
        body, name=name, grid=(nq,),
        in_specs=[pl.BlockSpec((tp, LANES), lambda q: (0, 3 * nq + q)), blk, blk, pl.BlockSpec(memory_space=pl.ANY),
                  pl.BlockSpec((tp, sw), lambda q: (0, q)), pl.BlockSpec((tp, sw), lambda q: (0, q)),
                  bspec, bspec, cspec, cspec, pl.BlockSpec((1, LANES), lambda q: (0, q)), tspec],
        out_specs=[pl.BlockSpec(memory_space=pl.ANY), bspec, bspec, cspec, cspec,
                   pl.BlockSpec((1, LANES), lambda q: (0, q)),
                   pl.BlockSpec((1, sw), lambda q: (0, q)), pl.BlockSpec((1, sw), lambda q: (0, q))],
        out_shape=[jax.ShapeDtypeStruct((tp, 4 * dh), BF16),
                   jax.ShapeDtypeStruct((LANES, nstate), F32), jax.ShapeDtypeStruct((LANES, nstate), F32),
                   jax.ShapeDtypeStruct((nstate, LANES), F32), jax.ShapeDtypeStruct((nstate, LANES), F32),
                   jax.ShapeDtypeStruct((1, dh), F32),
                   jax.ShapeDtypeStruct((1, nstate), F32), jax.ShapeDtypeStruct((1, nstate), F32)],
        input_output_aliases={3: 0},
        scratch_shapes=[pltpu.VMEM((tp, sw), F32)] * 2 + [pltpu.VMEM((tp, LANES), F32)] * 2
        + [pltpu.VMEM((tp, LANES), BF16), pltpu.SemaphoreType.DMA],
        compiler_params=_cparams("arbitrary"),
    )(proj, y, dg, dproj, states[0], states[1], bc_re, bc_im, cc_re, cc_im, dskip, a_pow)


FFN_TILE = 256
FFN_ROWS = 32


def _window(x_ref, before, r0, rows, cols):
    if r0 == 0:
        return jnp.concatenate([before, x_ref[0:rows, cols]], axis=0)
    return x_ref[r0 - SUBLANES:r0 + rows, cols]


def _taps(window):
    return window[SUBLANES:], pltpu.roll(window, 1, 0)[SUBLANES:], pltpu.roll(window, 2, 0)[SUBLANES:]


def _conv_taps(taps, w):
    return w[2] * taps[0] + w[1] * taps[1] + w[0] * taps[2]


FFN_MM_ROWS = 544
FFN_MM_COLS = 1408


def _ffn_up_act(hn, w_up, fw, fb, col, others, name):
    tp, dm = hn.shape
    dff = w_up.shape[1] // 2
    tr, cw, rows = FFN_MM_ROWS, FFN_MM_COLS, FFN_ROWS
    nc = dff // cw
    n_others = 0 if others is None else 2

    pieces = [(c0, min(c0 + 2 * FFN_TILE, cw)) for c0 in range(0, cw, 2 * FFN_TILE)]

    def body(hn_ref, w_hbm, wa_ref, wv_ref, ba_ref, bv_ref, *rest):
        up_ref, act_ref, tail_ref, w_vmem, w_sem = rest[n_others:]
        first = pl.program_id(0) == 0

        def w_copy(half, k):
            c0, c1 = pieces[k]
            return pltpu.make_async_copy(w_hbm.at[:, pl.ds((half * nc + col) * cw + c0, c1 - c0)],
                                         w_vmem.at[half, :, pl.ds(c0, c1 - c0)], w_sem.at[half, k])

        @pl.when(first)
        def _():
            for k in range(len(pieces)):
                for half in range(2):
                    w_copy(half, k).start()
            tail_ref[...] = jnp.zeros_like(tail_ref)

        x = hn_ref[...]
        for c0 in range(0, cw, FFN_TILE):
            cols = slice(c0, min(c0 + FFN_TILE, cw))
            if c0 in [p0 for p0, _ in pieces]:
                k = [p0 for p0, _ in pieces].index(c0)
                wide = slice(*pieces[k])

                @pl.when(first)
                def _():
                    for half in range(2):
                        w_copy(half, k).wait()

                for half in range(2):
                    up_ref[half, :, wide] = jnp.dot(x, w_vmem[half, :, wide], preferred_element_type=F32)
            wa, wv = [[w_ref[t:t + 1, cols] for t in range(3)] for w_ref in (wa_ref, wv_ref)]
            ba, bv = ba_ref[:, cols], bv_ref[:, cols]
            before_a, before_v = tail_ref[0, :, cols], tail_ref[1, :, cols]
            for r0 in range(0, tr, rows):
                a = _conv_taps(_taps(_window(up_ref.at[0], before_a, r0, rows, cols)), wa) + ba
                v = _conv_taps(_taps(_window(up_ref.at[1], before_v, r0, rows, cols)), wv) + bv
                act_ref[r0:r0 + rows, cols] = (a * _sigmoid(a) * v).astype(BF16)
            tail_ref[:, :, cols] = up_ref[:, tr - SUBLANES:tr, cols]

    par = lambda r, half: pl.BlockSpec((r, cw), lambda i: (0, half * nc + col))
    return pl.pallas_call(
        body, name=name, grid=(tp // tr,),
        in_specs=[pl.BlockSpec((tr, dm), lambda i: (i, 0)), pl.BlockSpec(memory_space=pl.ANY),
                  par(3, 0), par(3, 1), par(1, 0), par(1, 1)] + [pl.BlockSpec(memory_space=pl.ANY)] * n_others,
        out_specs=[pl.BlockSpec((2, tr, cw), lambda i: (0, i, col)), pl.BlockSpec((tr, cw), lambda i: (i, col))],
        out_shape=[jax.ShapeDtypeStruct((2, tp, dff), F32), jax.ShapeDtypeStruct((tp, dff), BF16)],
        input_output_aliases={6: 0, 7: 1} if others is not None else {},
        scratch_shapes=[pltpu.VMEM((2, SUBLANES, cw), F32), pltpu.VMEM((2, dm, cw), BF16),
                        pltpu.SemaphoreType.DMA((2, len(pieces)))],
        compiler_params=_cparams("arbitrary"))(hn, w_up, fw, fw, fb, fb, *(others or ()))


def _ffn_bwd(up, dh, w_down, fw, fb, name):
    _, tp, dff = up.shape
    two_ff = 2 * dff
    dm = dh.shape[1]
    tr, cw, rows = FFN_MM_ROWS, FFN_MM_COLS, FFN_ROWS
    nr, nc = tp // tr, dff // cw
    n_e = rows + SUBLANES
    pieces = tr // SUBLANES

    def body(ua_ref, uv_ref, pa_ref, pv_ref, dh_ref, wd_ref, wa_ref, wv_ref, ba_ref, bv_ref,
             dup_ref, dwa_ref, dwv_ref, dba_ref, dbv_ref, dact, stage, head_ref, sem):
        j, i = pl.program_id(0), pl.program_id(1)
        step = j * nr + i
        top = i == nr - 1
        sums = ((dwa_ref, dba_ref), (dwv_ref, dbv_ref))

        slot = step % 2

        def out_copies(at):
            r0 = pl.multiple_of((nr - 1 - at % nr) * tr, tr)
            return [pltpu.make_async_copy(
                stage.at[at % 2, s],
                dup_ref.at[pl.ds(r0, tr), pl.ds(pl.multiple_of(s * dff + at // nr * cw, LANES), cw)],
                sem.at[at % 2, s]) for s in range(2)]

        @pl.when(i == 0)
        def _():
            head_ref[...] = jnp.zeros_like(head_ref)
            for dw_ref, db_ref in sums:
                dw_ref[...] = jnp.zeros_like(dw_ref)
                db_ref[...] = jnp.zeros_like(db_ref)

        dact[...] = lax.dot_general(dh_ref[...], wd_ref[...], (((1,), (1,)), ((), ())), preferred_element_type=F32)

        @pl.when(step > 1)
        def _():
            for cp in out_copies(step - 2):
                cp.wait()

        def gate_bwd(taps, dact_v, w, bias):
            a, v = [_conv_taps(taps[s], w[s]) + bias[s] for s in range(2)]
            sg = _sigmoid(a)
            return [dact_v * v * sg * (1.0 + a * (1.0 - sg)), dact_v * a * sg]

        fold = lambda x: sum(x[r:r + SUBLANES] for r in range(0, rows, SUBLANES))
        for c0 in range(0, cw, FFN_TILE):
            cols = slice(c0, min(c0 + FFN_TILE, cw))
            w = [[w_ref[k:k + 1, cols] for k in range(3)] for w_ref in (wa_ref, wv_ref)]
            bias = [ba_ref[:, cols], bv_ref[:, cols]]
            before = [jnp.where(top, 0.0, p_ref[:, cols]) for p_ref in (pa_ref, pv_ref)]
            head = [head_ref[s, :, cols] for s in range(2)]
            piece = jnp.zeros_like(head[0])
            acc = [[piece] * 4 for _ in range(2)]
            for r0 in reversed(range(0, tr, rows)):
                taps = [_taps(_window(x_ref, before[s], r0, rows, cols)) for s, x_ref in enumerate((ua_ref, uv_ref))]
                d = gate_bwd(taps, dact[r0:r0 + rows, cols], w, bias)
                for s in range(2):
                    de = jnp.concatenate([d[s], head[s]], axis=0)
                    dx = (w[s][2] * d[s] + w[s][1] * pltpu.roll(de, n_e - 1, 0)[:rows]
                          + w[s][0] * pltpu.roll(de, n_e - 2, 0)[:rows])
                    stage[slot, s, r0:r0 + rows, cols] = dx.astype(BF16)
                    for k in range(3):
                        acc[s][k] = acc[s][k] + fold(d[s] * taps[s][2 - k])
                    acc[s][3] = acc[s][3] + fold(d[s])
                    head[s] = d[s][:SUBLANES]
            for s, (dw_ref, db_ref) in enumerate(sums):
                head_ref[s, :, cols] = head[s]
                dw_ref[:, cols] = dw_ref[:, cols] + jnp.concatenate(
                    [jnp.sum(x, axis=0, keepdims=True) for x in acc[s][:3]], axis=0)
                db_ref[:, cols] = db_ref[:, cols] + jnp.sum(acc[s][3], axis=0, keepdims=True)

        copies = out_copies(step)
        for cp in copies:
            cp.start()

        @pl.when(step == nc * nr - 1)
        def _():
            for cp in out_copies(step - 1) + copies:
                cp.wait()

    row = lambda i: nr - 1 - i
    main = lambda half: pl.BlockSpec((None, tr, cw), lambda j, i: (half, row(i), j))
    prev = lambda half: pl.BlockSpec((None, SUBLANES, cw), lambda j, i: (half, jnp.maximum(row(i) * pieces - 1, 0), j))
    par = lambda r, half: pl.BlockSpec((r, cw), lambda j, i: (0, half * nc + j))
    acc_spec = lambda r: pl.BlockSpec((r, cw), lambda j, i: (0, j))
    return pl.pallas_call(
        body, name=name, grid=(nc, nr),
        in_specs=[main(0), main(1), prev(0), prev(1),
                  pl.BlockSpec((tr, dm), lambda j, i: (row(i), 0)), pl.BlockSpec((cw, dm), lambda j, i: (j, 0)),
                  par(3, 0), par(3, 1), par(1, 0), par(1, 1)],
        out_specs=[pl.BlockSpec(memory_space=pl.ANY), acc_spec(3), acc_spec(3), acc_spec(1), acc_spec(1)],
        out_shape=[jax.ShapeDtypeStruct((tp, two_ff), BF16),
                   jax.ShapeDtypeStruct((3, dff), F32), jax.ShapeDtypeStruct((3, dff), F32),
                   jax.ShapeDtypeStruct((1, dff), F32), jax.ShapeDtypeStruct((1, dff), F32)],
        scratch_shapes=[pltpu.VMEM((tr, cw), F32), pltpu.VMEM((2, 2, tr, cw), BF16),
                        pltpu.VMEM((2, SUBLANES, cw), F32), pltpu.SemaphoreType.DMA((2, 2))],
        compiler_params=_cparams("arbitrary", "arbitrary"))(up, up, up, up, dh, w_down, fw, fw, fb, fb)


def _zoh(lr, li, ld):
    dt = jnp.exp(ld)
    mag = jnp.exp(lr * dt)
    ang = li * dt
    ar = mag * jnp.cos(ang)
    ai = mag * jnp.sin(ang)
    den = lr * lr + li * li
    nr = ar - 1.0
    fr = (nr * lr + ai * li) / den
    fi = (ai * lr - nr * li) / den
    return dt, ar, ai, den, nr, fr, fi


def _s5_prep(lr, li, ld, b_re, b_im, n_pow, name):
    nstate = lr.shape[1]

    def body(lr_ref, li_ref, ld_ref, bre_ref, bim_ref, pw_ref, bcre_ref, bcim_ref):
        _, ar, ai, _, _, fr, fi = _zoh(lr_ref[...], li_ref[...], ld_ref[...])
        bre = bre_ref[...]
        bim = bim_ref[...]
        bcre_ref[...] = (fr * bre - fi * bim).astype(BF16)
        bcim_ref[...] = (fr * bim + fi * bre).astype(BF16)
        row = lax.broadcasted_iota(jnp.int32, (SUBLANES, nstate), 0)
        pr, pi = jnp.zeros((SUBLANES, nstate), F32), jnp.zeros((SUBLANES, nstate), F32)
        cr, ci = ar, ai
        for t in range(SUBLANES):
            pr, pi = jnp.where(row == t, cr, pr), jnp.where(row == t, ci, pi)
            cr, ci = cr * ar - ci * ai, cr * ai + ci * ar
        pw_ref[0, 0:SUBLANES, :] = pr
        pw_ref[1, 0:SUBLANES, :] = pi
        n = SUBLANES
        while n < n_pow:
            m = min(n, n_pow - n)
            tr, ti = pw_ref[0, n - 1:n, :], pw_ref[1, n - 1:n, :]
            xr, xi = pw_ref[0, 0:m, :], pw_ref[1, 0:m, :]
            pw_ref[0, n:n + m, :] = xr * tr - xi * ti
            pw_ref[1, n:n + m, :] = xr * ti + xi * tr
            n += m

    vmem = pl.BlockSpec(memory_space=pltpu.VMEM)
    return pl.pallas_call(
        body, name=name, in_specs=[vmem] * 5, out_specs=[vmem] * 3,
        out_shape=[jax.ShapeDtypeStruct((2, n_pow, nstate), F32)] + [jax.ShapeDtypeStruct(b_re.shape, BF16)] * 2,
        compiler_params=pltpu.CompilerParams(vmem_limit_bytes=VMEM_LIMIT))(lr, li, ld, b_re, b_im)


def _s5_prep_bwd(lr, li, ld, b_re, b_im, da_re, da_im, dbc_re, dbc_im, name):
    def body(lr_ref, li_ref, ld_ref, bre_ref, bim_ref, dar_ref, dai_ref, dbcre_ref, dbcim_ref,
             dlr_ref, dli_ref, dld_ref, dbre_ref, dbim_ref):
        lr, li = lr_ref[...], li_ref[...]
        dt, ar, ai, den, nr, fr, fi = _zoh(lr, li, ld_ref[...])
        bre, bim = bre_ref[...], bim_ref[...]
        gre, gim = dbcre_ref[...], dbcim_ref[...]
        dbre_ref[...] = fr * gre + fi * gim
        dbim_ref[...] = fr * gim - fi * gre
        g_fr = jnp.sum(gre * bre + gim * bim, axis=0, keepdims=True)
        g_fi = jnp.sum(gim * bre - gre * bim, axis=0, keepdims=True)
        g_ar = dar_ref[...] + (g_fr * lr - g_fi * li) / den
        g_ai = dai_ref[...] + (g_fr * li + g_fi * lr) / den
        d_lr = (g_fr * (nr - 2.0 * fr * lr) + g_fi * (ai - 2.0 * fi * lr)) / den
        d_li = (g_fr * (ai - 2.0 * fr * li) - g_fi * (nr + 2.0 * fi * li)) / den
        g_logmag = g_ar * ar + g_ai * ai
        g_ang = g_ai * ar - g_ar * ai
        dlr_ref[...] = d_lr + g_logmag * dt
        dli_ref[...] = d_li + g_ang * dt
        d_ld = (g_logmag * lr + g_ang * li) * dt
        n = d_ld.shape[1]
        sh = 1
        while sh < STATE:
            d_ld = d_ld + pltpu.roll(d_ld, n - sh, 1)
            sh *= 2
        dld_ref[...] = d_ld

    vmem = pl.BlockSpec(memory_space=pltpu.VMEM)
    row = jax.ShapeDtypeStruct(lr.shape, F32)
    return pl.pallas_call(
        body, name=name, in_specs=[vmem] * 9, out_specs=[vmem] * 5,
        out_shape=[row, row, row, jax.ShapeDtypeStruct(b_re.shape, F32), jax.ShapeDtypeStruct(b_re.shape, F32)],
    )(lr, li, ld, b_re, b_im, da_re, da_im, dbc_re, dbc_im)


def _compact_b(bb):
    bq = bb.reshape(N_GROUPS // 8, 8, STATE, GROUP)
    m = jnp.einsum("ab,qbph->qahbp", jnp.eye(8, dtype=bb.dtype), bq).reshape(N_GROUPS // 8, LANES, 8 * STATE)
    return m.transpose(1, 0, 2).reshape(LANES, N_GROUPS * STATE)


def _expand_b(m):
    d = m.reshape(8, GROUP, N_GROUPS // 8, 8, STATE)
    return jnp.einsum("ahqap->qahp", d).reshape(N_GROUPS, GROUP, STATE)


def _compact_c(c):
    cq = c.reshape(N_GROUPS // 8, 8, GROUP, STATE)
    return jnp.einsum("ab,qbhp->qbpah", jnp.eye(8, dtype=c.dtype), cq).reshape(N_GROUPS * STATE, LANES)


def _expand_c(m):
    d = m.reshape(N_GROUPS // 8, 8, STATE, 8, GROUP)
    return jnp.einsum("qbpbh->qbhp", d).reshape(N_GROUPS, GROUP, STATE)


def _local_step(x, target, p, ex):
    seq, d = x.shape
    n_real = N_META + seq
    tp = -(-n_real // ROW_ALIGN) * ROW_ALIGN

    h0, hn1 = _input_norm_fwd(x, p["meta_tokens"], p["norm_mix_g"] + ex.zero, tp, "norm_mix")
    ex.forward("first", hn1)
    nstate = N_GROUPS * STATE
    s5 = (p["ssm_lam_re"].reshape(1, nstate), p["ssm_lam_im"].reshape(1, nstate),
          jnp.repeat(p["ssm_log_dt"].reshape(-1), STATE).reshape(1, nstate),
          _compact_b(p["ssm_b_re"]), _compact_b(p["ssm_b_im"]))
    a_pow, bc_re, bc_im = _s5_prep(*s5, tp // SUBLANES, "s5_prep")
    cc_re = _compact_c(p["ssm_c_re"]).astype(BF16)
    cc_im = _compact_c(p["ssm_c_im"]).astype(BF16)
    dskip = p["ssm_d"].reshape(1, -1)
    first = ex.weights("first", bc_re)
    proj = _mm(hn1, first["w_in"], "nn", "proj")
    started = ex.forward("mid", proj)
    co, y, g, *states = _seq_fwd(proj, p["conv_w"] + started[0, 0], bc_re, bc_im, cc_re, cc_im, dskip, a_pow,
                                 "seq_fwd")
    mid = ex.weights("mid", g)
    z = _mm(g, mid["ssm_w_glu"], "nn", "glu")
    mixed = _mix_fwd(co, y, z, p["gain_conv_out"], p["gain_ssm_out"], "mix_fwd")
    started = ex.forward("up", mixed)
    h1, hn2 = _proj_res_norm(mixed, mid["w_out"], h0, p["norm_ffn_g"], started, "out_proj_norm")
    late = ex.weights("up", hn2)
    part, fw = None, p["ffn_conv_w"]
    for col in range(late["w_up"].shape[1] // (2 * FFN_MM_COLS)):
        part = _ffn_up_act(hn2, late["w_up"], fw, p["ffn_conv_b"], col, part, "ffn_up_act_%d" % col)
        if col == 0:
            fw = fw + ex.forward("down", part[1])[0, 0]
    up, act = part
    late.update(ex.weights("down", act))
    loss, dh2, dh2b, d_gfin = _proj_loss_bwd(act, late["w_down"], h1, target, p["norm_final_g"], n_real,
                                             "down_proj_loss")

    g_w_down = _mm(act, dh2b, "tn", "g_w_down")
    dup, dfw_a, dfw_v, dfb_a, dfb_v = _ffn_bwd(up, dh2b, late["w_down"], p["ffn_conv_w"], p["ffn_conv_b"], "ffn_bwd")
    g_w_up = _mm(hn2, dup, "tn", "g_w_up")
    started = ex.grads_ready("late", {"w_up": g_w_up, "w_down": g_w_down})
    dh1, dh1b, d_gffn = _proj_norm_bwd(dup, late["w_up"], h1, p["norm_ffn_g"], dh2, started, "d_hn2_norm_bwd")
    started = ex.grads_send("late", dh1)
    g_w_out = _mm(mixed, dh1b, "tn", "g_w_out", after=started)
    dco, dz, dgp, d_gc, d_gs = _proj_mix_bwd(dh1b, mid["w_out"], co, y, z, p["gain_conv_out"],
                                             p["gain_ssm_out"], "d_mixed_mix_bwd")
    g_w_glu = _mm(g, dz, "tn", "g_w_glu")
    started = ex.grads_ready("mid", {"ssm_w_glu": g_w_glu, "w_out": g_w_out})
    dg = _mm(dz, mid["ssm_w_glu"], "nt", "d_gelu", acc_in=dgp, after=started)
    started = ex.grads_send("mid", dg)
    dproj, d_conv_w = _conv_bwd(proj, dco, p["conv_w"] + started[0, 0], "conv_bwd")
    (dproj, dbc_re, dbc_im, dcc_re, dcc_im, d_dskip, da_re, da_im) = _ssm_bwd(
        proj, y, dg, dproj, states, bc_re, bc_im, cc_re, cc_im, dskip, a_pow, "ssm_bwd")
    g_w_in = _mm(hn1, dproj, "tn", "g_w_in")
    started = ex.grads_ready("first", {"w_in": g_w_in})
    grad_x, d_meta, d_gmix = _proj_input_norm_bwd(dproj, first["w_in"], h0, p["norm_mix_g"], dh1, started, n_real,
                                                  "d_hn1_norm_bwd")
    started = ex.grads_send("first", d_gmix)

    d_lam_re, d_lam_im, d_log_dt, d_b_re, d_b_im = _s5_prep_bwd(*s5, da_re, da_im, dbc_re, dbc_im, "s5_prep_bwd")
    d_lam_re, d_lam_im = d_lam_re.reshape(N_GROUPS, STATE), d_lam_im.reshape(N_GROUPS, STATE)
    d_log_dt = d_log_dt[0, ::STATE]
    d_b_re, d_b_im = _expand_b(d_b_re), _expand_b(d_b_im)
    grads = {
        "meta_tokens": d_meta, "norm_mix_g": d_gmix, "w_in": g_w_in, "conv_w": d_conv_w,
        "ssm_lam_re": d_lam_re, "ssm_lam_im": d_lam_im, "ssm_log_dt": d_log_dt,
        "ssm_b_re": d_b_re, "ssm_b_im": d_b_im, "ssm_c_re": _expand_c(dcc_re), "ssm_c_im": _expand_c(dcc_im),
        "ssm_d": d_dskip.reshape(N_GROUPS, GROUP), "ssm_w_glu": g_w_glu,
        "gain_conv_out": d_gc, "gain_ssm_out": d_gs, "w_out": g_w_out, "norm_ffn_g": d_gffn,
        "w_up": g_w_up, "ffn_conv_w": jnp.concatenate([dfw_a, dfw_v], axis=1),
        "ffn_conv_b": jnp.concatenate([dfb_a, dfb_v], axis=1), "w_down": g_w_down, "norm_final_g": d_gfin,
    }
    return loss[0, 0] + started[0, 0], grad_x, grads


def _view(ref, axis, start, size):
    idx = [slice(None)] * len(ref.shape)
    idx[axis] = pl.ds(start, size)
    return ref.at[tuple(idx)]


def _exchange(name, ins, outs, aliases, local_copies, remote_copies):
    ni, no = len(ins), len(outs)
    nl, nr = len(local_copies), len(remote_copies)

    def body(*refs):
        in_refs, out_refs = refs[:ni], refs[ni:ni + no]
        send_sems, recv_sems, local_sems = refs[ni + no:]
        x, y, c = lax.axis_index("x"), lax.axis_index("y"), lax.axis_index("c")
        pos = (x, y, c, 2 * x + y)
        locals_ = [pltpu.make_async_copy(s(in_refs, out_refs, pos), d(in_refs, out_refs, pos), local_sems.at[i])
                   for i, (s, d) in enumerate(local_copies)]
        remotes = []
        for i, (s, d, flip) in enumerate(remote_copies):
            peer = (1 - x if "x" in flip else x, 1 - y if "y" in flip else y, 1 - c if "c" in flip else c)
            remotes.append(pltpu.make_async_remote_copy(
                src_ref=s(in_refs, out_refs, pos), dst_ref=d(in_refs, out_refs, pos),
                send_sem=send_sems.at[i], recv_sem=recv_sems.at[i], device_id=peer, device_id_type=MESH))
        for cp in locals_ + remotes:
            cp.start()
        for cp in remotes:
            cp.wait_recv()
        for cp in remotes:
            cp.wait_send()
        for cp in locals_:
            cp.wait()

    hbm = pl.BlockSpec(memory_space=pl.ANY)
    return pl.pallas_call(
        body, name=name, in_specs=[hbm] * ni, out_specs=[hbm] * no, out_shape=outs,
        input_output_aliases=aliases,
        scratch_shapes=[pltpu.SemaphoreType.DMA((nr,)), pltpu.SemaphoreType.DMA((nr,)),
                        pltpu.SemaphoreType.DMA((max(nl, 1),))],
    )(*ins)


BIG = {"w_in": (0, 1), "ssm_w_glu": (1, 0), "w_out": (1, 0), "w_up": (0, 1), "w_down": (1, 0)}
BIG_NAMES = tuple(BIG)
FLIPS = ("y", "x", "xy")


def _peer_chip(pos, flip):
    x, y, _, _ = pos
    return 2 * (1 - x if "x" in flip else x) + (1 - y if "y" in flip else y)


def _block_rows(rows, cols, itemsize, mult):
    return _pick_tile(rows, max(mult, (2 * 1024 * 1024) // (cols * itemsize)), mult)


def _cast_into_full(w, kc, shard_axis, name):
    r, cdim = w.shape
    tr = _block_rows(r, cdim, 4, 16)
    nb = r // tr

    def body(kc_ref, w_ref, o_ref):
        o_ref[...] = w_ref[...].astype(BF16)

    if shard_axis == 1:
        full, o_spec = (r, 4 * cdim), pl.BlockSpec((tr, cdim), lambda i, kc: (i, kc[0]))
    else:
        full, o_spec = (4 * r, cdim), pl.BlockSpec((tr, cdim), lambda i, kc: (kc[0] * nb + i, 0))
    return pl.pallas_call(
        body, name=name,
        grid_spec=pltpu.PrefetchScalarGridSpec(
            num_scalar_prefetch=1, grid=(nb,), in_specs=[pl.BlockSpec((tr, cdim), lambda i, kc: (i, 0))],
            out_specs=o_spec),
        out_shape=jax.ShapeDtypeStruct(full, BF16), compiler_params=_cparams("parallel"))(kc, w)


def _pair_sum(g, recv, kc, half_axis, name, out_dtype):
    hr, hc = recv.shape
    tr = _block_rows(hr, hc, 4, 16)
    nb = hr // tr

    def body(kc_ref, g_ref, r_ref, o_ref):
        o_ref[...] = (g_ref[...] + r_ref[...]).astype(out_dtype)

    if half_axis == 0:
        g_spec = pl.BlockSpec((tr, hc), lambda i, kc: (kc[1] * nb + i, 0))
    elif half_axis == 1:
        g_spec = pl.BlockSpec((tr, hc), lambda i, kc: (i, kc[1]))
    else:
        g_spec = pl.BlockSpec((tr, hc), lambda i, kc: (i, 0))
    same = pl.BlockSpec((tr, hc), lambda i, kc: (i, 0))
    return pl.pallas_call(
        body, name=name,
        grid_spec=pltpu.PrefetchScalarGridSpec(num_scalar_prefetch=1, grid=(nb,), in_specs=[g_spec, same],
                                               out_specs=same),
        out_shape=jax.ShapeDtypeStruct((hr, hc), out_dtype), compiler_params=_cparams("parallel"))(kc, g, recv)


def _chip_sum(own, recv, kc, own_axis, out_axis, name):
    _, sr, sc = recv.shape
    tr = _block_rows(sr, sc, 4, 16)
    nb = sr // tr

    def body(kc_ref, o_ref, r_ref, t_ref):
        k = kc_ref[0]
        own_v = o_ref[...].astype(F32)
        r = [r_ref[m].astype(F32) for m in range(3)]
        terms = []
        for kk in range(4):
            m = jnp.bitwise_xor(k, kk)
            terms.append(jnp.where(m == 0, own_v, jnp.where(m == 1, r[0], jnp.where(m == 2, r[1], r[2]))))
        t_ref[...] = (terms[0] + terms[1]) + (terms[2] + terms[3])

    if own_axis == 0:
        own_spec = pl.BlockSpec((tr, sc), lambda i, kc: (kc[0] * nb + i, 0))
    elif own_axis == 1:
        own_spec = pl.BlockSpec((tr, sc), lambda i, kc: (i, kc[0]))
    else:
        own_spec = pl.BlockSpec((tr, sc), lambda i, kc: (kc[1] * nb + i, 0))
    if out_axis == 0:
        out_full, out_spec = (2 * sr, sc), pl.BlockSpec((tr, sc), lambda i, kc: (kc[1] * nb + i, 0))
    else:
        out_full, out_spec = (sr, 2 * sc), pl.BlockSpec((tr, sc), lambda i, kc: (i, kc[1]))
    return pl.pallas_call(
        body, name=name,
        grid_spec=pltpu.PrefetchScalarGridSpec(
            num_scalar_prefetch=1, grid=(nb,),
            in_specs=[own_spec, pl.BlockSpec((3, tr, sc), lambda i, kc: (0, i, 0))],
            out_specs=out_spec),
        out_shape=jax.ShapeDtypeStruct(out_full, F32), compiler_params=_cparams("parallel"))(kc, own, recv)


def _adamw(w, g, m, v, name):
    r, cdim = w.shape
    tr = _block_rows(r, cdim, 4, 8)
    c1 = 1.0 - ADAM_B1 ** ADAM_STEP
    c2 = 1.0 - ADAM_B2 ** ADAM_STEP

    def body(w_ref, g_ref, m_ref, v_ref, go_ref, d_ref, nm_ref, nv_ref):
        gv = g_ref[...]
        go_ref[...] = gv
        nm = ADAM_B1 * m_ref[...] + (1.0 - ADAM_B1) * gv
        nv = ADAM_B2 * v_ref[...] + (1.0 - ADAM_B2) * (gv * gv)
        d_ref[...] = -ADAM_LR * ((nm / c1) / (jnp.sqrt(nv / c2) + ADAM_EPS) + ADAM_WD * w_ref[...])
        nm_ref[...] = nm
        nv_ref[...] = nv

    spec = _rows(cdim, tr)
    return pl.pallas_call(body, name=name, grid=(r // tr,), in_specs=[spec] * 4, out_specs=[spec] * 4,
                          out_shape=[jax.ShapeDtypeStruct((r, cdim), F32)] * 4,
                          compiler_params=_cparams("parallel"))(w, g, m, v)


def _adamw_whole(ws, gs, ms, vs, name):
    n = len(ws)
    c1 = 1.0 - ADAM_B1 ** ADAM_STEP
    c2 = 1.0 - ADAM_B2 ** ADAM_STEP

    def body(*refs):
        for i in range(n):
            w_ref, g_ref, m_ref, v_ref, d_ref, nm_ref, nv_ref = [refs[j * n + i] for j in range(7)]
            gv = g_ref[...]
            nm = ADAM_B1 * m_ref[...] + (1.0 - ADAM_B1) * gv
            nv = ADAM_B2 * v_ref[...] + (1.0 - ADAM_B2) * (gv * gv)
            d_ref[...] = -ADAM_LR * ((nm / c1) / (jnp.sqrt(nv / c2) + ADAM_EPS) + ADAM_WD * w_ref[...])
            nm_ref[...] = nm
            nv_ref[...] = nv

    vmem = pl.BlockSpec(memory_space=pltpu.VMEM)
    out = pl.pallas_call(body, name=name, in_specs=[vmem] * (4 * n), out_specs=[vmem] * (3 * n),
                         out_shape=[jax.ShapeDtypeStruct(a.shape, F32) for a in ws] * 3,
                         compiler_params=pltpu.CompilerParams(vmem_limit_bytes=VMEM_LIMIT))(*ws, *gs, *ms, *vs)
    return out[:n], out[n:2 * n], out[2 * n:]


SIDE_EFFECT = pltpu.SideEffectType.DATAFLOW_SIDE_EFFECTING


def _descriptors(copies, refs, send_sems, recv_sems, sem_off=0):
    x, y, c = lax.axis_index("x"), lax.axis_index("y"), lax.axis_index("c")
    pos = (x, y, c, 2 * x + y)
    out = []
    for i, (s, d, flip) in enumerate(copies):
        peer = (1 - x if "x" in flip else x, 1 - y if "y" in flip else y, 1 - c if "c" in flip else c)
        out.append(pltpu.make_async_remote_copy(
            src_ref=s(refs, refs, pos), dst_ref=d(refs, refs, pos),
            send_sem=send_sems.at[sem_off + i], recv_sem=recv_sems.at[sem_off + i],
            device_id=peer, device_id_type=MESH))
    return out


def _shifted(copies, off):
    return [(lambda I, O, pos, s=s: s(I[off:], O[off:], pos), lambda I, O, pos, d=d: d(I[off:], O[off:], pos), flip)
            for s, d, flip in copies]


BARRIER_IDS = {"c": (1, 2), "ici": (3, 4)}


def _exchange_start(name, bufs, copies, turns, after=None):
    n, nr = len(bufs), len(copies)
    na = 0 if after is None else 1
    flips = sorted({flip for _, _, flip in copies})
    kind = "c" if flips == ["c"] else "ici"
    collective_id = BARRIER_IDS[kind][turns[kind] % 2]
    turns[kind] += 1

    def body(*refs):
        x, y, c = lax.axis_index("x"), lax.axis_index("y"), lax.axis_index("c")
        barrier = pltpu.get_barrier_semaphore()
        for flip in flips:
            peer = (1 - x if "x" in flip else x, 1 - y if "y" in flip else y, 1 - c if "c" in flip else c)
            pl.semaphore_signal(barrier, inc=1, device_id=peer, device_id_type=MESH)
        pl.semaphore_wait(barrier, len(flips))
        for cp in _descriptors(copies, refs[:n], refs[n + na], refs[n + na + 1]):
            cp.start()
        token = refs[2 * n + na + 2]
        token[...] = jnp.zeros_like(token)

    hbm = pl.BlockSpec(memory_space=pltpu.HBM)
    sem = pl.BlockSpec(memory_space=pltpu.SEMAPHORE)
    out = pl.pallas_call(
        body, name=name,
        in_specs=[hbm] * n + [pl.BlockSpec(memory_space=pl.ANY)] * na,
        out_specs=(sem, sem, *[hbm] * n, pl.BlockSpec(memory_space=pltpu.VMEM)),
        out_shape=(pltpu.SemaphoreType.DMA((nr,)), pltpu.SemaphoreType.DMA((nr,)),
                   *[pltpu.HBM(b.shape, b.dtype) for b in bufs], jax.ShapeDtypeStruct((SUBLANES, LANES), F32)),
        input_output_aliases={i: 2 + i for i in range(n)},
        compiler_params=pltpu.CompilerParams(has_side_effects=SIDE_EFFECT, collective_id=collective_id),
    )(*[pltpu.with_memory_space_constraint(b, pltpu.HBM) for b in bufs], *([after] * na))
    return out[0], out[1], list(out[2:2 + n]), out[2 + n]


def _exchange_wait(name, send_sems, recv_sems, bufs, copies, after, sem_off=0):
    n = len(bufs)

    def body(*refs):
        for cp in _descriptors(copies, refs[:n], refs[n], refs[n + 1], sem_off):
            cp.wait_send()
            cp.wait_recv()

    hbm = pl.BlockSpec(memory_space=pltpu.HBM)
    sem = pl.BlockSpec(memory_space=pltpu.SEMAPHORE)
    out = pl.pallas_call(
        body, name=name,
        in_specs=[hbm] * n + [sem, sem, pl.BlockSpec(memory_space=pl.ANY)],
        out_specs=tuple([hbm] * n),
        out_shape=tuple(pltpu.HBM(b.shape, b.dtype) for b in bufs),
        input_output_aliases={i: i for i in range(n)},
        compiler_params=pltpu.CompilerParams(has_side_effects=SIDE_EFFECT),
    )(*bufs, send_sems, recv_sems, after)
    return list(out)


FIRST = ("w_in",)
MID = ("ssm_w_glu", "w_out")
LATE = ("w_up", "w_down")
GROUPS = {"first": FIRST, "mid": MID, "late": LATE}
ARRIVALS = {"first": FIRST, "mid": MID, "up": ("w_up",), "down": ("w_down",)}


def _gather_copies(names, shard_shapes):
    def region(i, chip, c):
        half_axis, shard_axis = BIG[names[i]]
        ssize = shard_shapes[i][shard_axis]
        hsize = shard_shapes[i][half_axis] // 2
        return lambda ref: _view(_view(ref, shard_axis, chip * ssize, ssize), half_axis, c * hsize, hsize)

    ici, d2d = [], []
    for i in range(len(names)):
        for flip in FLIPS:
            ici.append((lambda I, O, pos, i=i: region(i, pos[3], pos[2])(I[i]),
                        lambda I, O, pos, i=i: region(i, pos[3], pos[2])(O[i]), flip))
            d2d.append((lambda I, O, pos, i=i, flip=flip: region(i, _peer_chip(pos, flip), pos[2])(I[i]),
                        lambda I, O, pos, i=i, flip=flip: region(i, _peer_chip(pos, flip), pos[2])(O[i]), "c"))
    return ici, d2d


def _half_shape(n, shape):
    r, cdim = shape
    return (r // 2, cdim) if BIG[n][0] == 0 else (r, cdim // 2)


def _sub_shape(n, shape):
    hr, hc = _half_shape(n, shape)
    return (hr, hc // 4) if BIG[n][1] == 1 else (hr // 4, hc)


def _pair_copies(names, shapes, with_pack, dst_off):
    n = len(names)

    def other_half(i, ref, pos):
        half_axis = BIG[names[i]][0]
        hsize = shapes[i][half_axis] // 2
        return _view(ref, half_axis, (1 - pos[2]) * hsize, hsize)

    copies = [(lambda I, O, pos, i=i: other_half(i, I[i], pos), lambda I, O, pos, i=i: O[dst_off + i], "c")
              for i in range(n)]
    if with_pack:
        copies.append((lambda I, O, pos: I[n], lambda I, O, pos: O[dst_off + n], "c"))
    return copies


def _chip_copies(names, shapes, pack_rows, dst_off):
    n = len(names)

    def piece(i, ref, chip):
        shard_axis = BIG[names[i]][1]
        ssize = _sub_shape(names[i], shapes[i])[shard_axis]
        return _view(ref, shard_axis, chip * ssize, ssize)

    copies = []
    for i in range(n):
        for slot, flip in enumerate(FLIPS):
            copies.append((lambda I, O, pos, i=i, flip=flip: piece(i, I[i], _peer_chip(pos, flip)),
                           lambda I, O, pos, i=i, slot=slot: O[dst_off + i].at[slot], flip))
    if pack_rows:
        for slot, flip in enumerate(FLIPS):
            copies.append((lambda I, O, pos: _view(I[n], 0, pos[2] * (pack_rows // 2), pack_rows // 2),
                           lambda I, O, pos, slot=slot: O[dst_off + n].at[slot], flip))
    return copies


class _Exchanges:
    def __init__(self, shards, tiny, kc):
        self.kc = kc
        wb = {n: _cast_into_full(shards[n], kc, BIG[n][1], "cast_" + n) for n in BIG_NAMES}
        self.gathering, self.forwarding, self.pairing, self.reducing = {}, {}, {}, {}
        self.turns = {"c": 0, "ici": 0}
        tiny_copies = [(lambda I, O, pos: I[0], lambda I, O, pos: O[1].at[pos[3]], flip) for flip in FLIPS]
        self.gathering["tiny"] = (0, 0, 2, tiny_copies, None)
        bufs, copies = [tiny, lax.empty((4,) + tiny.shape, F32)], list(tiny_copies)
        for group, names in ARRIVALS.items():
            ici, d2d = _gather_copies(names, [shards[n].shape for n in names])
            self.gathering[group] = (len(bufs), len(copies), len(names), ici, d2d)
            copies += _shifted(ici, len(bufs))
            bufs += [wb[n] for n in names]
        self.started = _exchange_start("gather_start", bufs, copies, self.turns)
        self.zero = self.started[3][0, 0]

    def _arrived(self, group, after):
        buf_off, sem_off, n, ici, _ = self.gathering[group]
        send_sems, recv_sems, bufs, _ = self.started
        return _exchange_wait("gather_%s_wait" % group, send_sems, recv_sems, bufs[buf_off:buf_off + n], ici, after,
                              sem_off)

    def small_params(self, kc):
        tiny, got = self._arrived("tiny", self.started[3])
        return lax.dynamic_update_index_in_dim(got, tiny, kc[0], 0)

    def forward(self, group, after):
        d2d = self.gathering[group][4]
        self.forwarding[group] = (_exchange_start("forward_%s_start" % group, self._arrived(group, after), d2d,
                                                  self.turns), d2d)
        return self.forwarding[group][0][3]

    def weights(self, group, after):
        if group not in self.forwarding:
            after = self.forward(group, after)
        (send_sems, recv_sems, bufs, _), d2d = self.forwarding[group]
        full = _exchange_wait("forward_%s_wait" % group, send_sems, recv_sems, bufs, d2d, after)
        return dict(zip(ARRIVALS[group], full))

    def grads_ready(self, group, grads):
        names = GROUPS[group]
        gs = [grads[n] for n in names]
        land = [lax.empty(_half_shape(n, g.shape), F32) for n, g in zip(names, gs)]
        copies = _pair_copies(names, [g.shape for g in gs], False, len(names))
        started = _exchange_start("pair_%s_start" % group, gs + land, copies, self.turns)
        self.pairing[group] = (started, copies)
        return started[3]

    def grads_send(self, group, after):
        names = GROUPS[group]
        n = len(names)
        (send_sems, recv_sems, bufs, _), copies = self.pairing[group]
        bufs = _exchange_wait("pair_%s_wait" % group, send_sems, recv_sems, bufs, copies, after)
        chip = [_pair_sum(bufs[i], bufs[n + i], self.kc, BIG[names[i]][0], "pair_sum_" + names[i], BF16)
                for i in range(n)]
        shapes = [bufs[i].shape for i in range(n)]
        land = [lax.empty((3,) + _sub_shape(names[i], shapes[i]), BF16) for i in range(n)]
        copies = _chip_copies(names, shapes, 0, n)
        started = _exchange_start("reduce_%s_start" % group, chip + land, copies, self.turns)
        self.reducing[group] = (started, copies)
        return started[3]

    def finish_pack(self, pack):
        kc = self.kc
        prow = pack.shape[0] // 2
        recv = _exchange("reduce_d2d", [pack], [jax.ShapeDtypeStruct(pack.shape, F32)], {}, [],
                         _pair_copies((), [], True, 0))
        chip_pack = _pair_sum(pack, recv[0], kc, None, "pair_sum_pack", F32)
        copies = _chip_copies((), [], pack.shape[0], 1)
        land = lax.empty((3, prow, pack.shape[1]), F32)
        pack_sems_s, pack_sems_r, pack_bufs, after = _exchange_start("reduce_pack_start", [chip_pack, land], copies,
                                                                     self.turns)

        names, chips, recvs = (), [], []
        for group, group_names in GROUPS.items():
            (send_sems, recv_sems, bufs, _), group_copies = self.reducing[group]
            bufs = _exchange_wait("reduce_%s_wait" % group, send_sems, recv_sems, bufs, group_copies, after)
            n = len(group_names)
            names, chips, recvs = names + group_names, chips + bufs[:n], recvs + bufs[n:]
            after = bufs[n]
        total = [_chip_sum(chips[i], recvs[i], kc, BIG[n][1], BIG[n][0], "chip_sum_" + n)
                 for i, n in enumerate(names)]

        def my_half(half_axis, ref, pos):
            hsize = ref.shape[half_axis] // 2
            return _view(ref, half_axis, pos[2] * hsize, hsize)

        swap = [(lambda I, O, pos, i=i, n=n: my_half(BIG[n][0], I[i], pos),
                 lambda I, O, pos, i=i, n=n: my_half(BIG[n][0], O[i], pos), "c") for i, n in enumerate(names)]
        self.swapping = (_exchange_start("swap_start", total, swap, self.turns), swap, names)

        chip_pack, recv_pack = _exchange_wait("reduce_pack_wait", pack_sems_s, pack_sems_r, pack_bufs, copies,
                                              self.swapping[0][3])
        total_pack = _chip_sum(chip_pack, recv_pack, kc, None, 0, "chip_sum_pack")
        swap = [(lambda I, O, pos: my_half(0, I[0], pos), lambda I, O, pos: my_half(0, O[0], pos), "c")]
        return _exchange("swap_pack", [total_pack], [jax.ShapeDtypeStruct(pack.shape, F32)], {0: 0}, [], swap)[0]

    def finish_big(self, after):
        (send_sems, recv_sems, bufs, _), swap, names = self.swapping
        return dict(zip(names, _exchange_wait("swap_wait", send_sems, recv_sems, bufs, swap, after)))


WEIGHTS = ("meta_tokens", "norm_mix_g", "w_in", "conv_w", "ssm_lam_re", "ssm_lam_im", "ssm_log_dt", "ssm_b_re",
           "ssm_b_im", "ssm_c_re", "ssm_c_im", "ssm_d", "ssm_w_glu", "gain_conv_out", "gain_ssm_out", "w_out",
           "norm_ffn_g", "w_up", "ffn_conv_w", "ffn_conv_b", "w_down", "norm_final_g")
TINY_SHARDED = ("meta_tokens", "conv_w", "ffn_conv_w")
REPLICATED = tuple(n for n in WEIGHTS if n not in BIG and n not in TINY_SHARDED)
PACK_COLS = 512


def _pack(arrays, row_mult, cols):
    flat = jnp.concatenate([a.reshape(-1).astype(F32) for a in arrays])
    n = flat.shape[0]
    total = -(-n // (row_mult * cols)) * (row_mult * cols)
    return jnp.concatenate([flat, jnp.zeros((total - n,), F32)]).reshape(total // cols, cols)


def _unpack(packed, shapes):
    flat = packed.reshape(-1)
    out, off = [], 0
    for s in shapes:
        n = math.prod(s)
        out.append(flat[off:off + n].reshape(s))
        off += n
    return out


def kernel(x, meta_tokens, norm_mix_g, w_in, conv_w, ssm_lam_re, ssm_lam_im, ssm_log_dt, ssm_b_re, ssm_b_im, ssm_c_re, ssm_c_im, ssm_d, ssm_w_glu, gain_conv_out, gain_ssm_out, w_out, norm_ffn_g, w_up, ffn_conv_w, ffn_conv_b, w_down, norm_final_g, loss_target, m_meta_tokens, m_norm_mix_g, m_w_in, m_conv_w, m_ssm_lam_re, m_ssm_lam_im, m_ssm_log_dt, m_ssm_b_re, m_ssm_b_im, m_ssm_c_re, m_ssm_c_im, m_ssm_d, m_ssm_w_glu, m_gain_conv_out, m_gain_ssm_out, m_w_out, m_norm_ffn_g, m_w_up, m_ffn_conv_w, m_ffn_conv_b, m_w_down, m_norm_final_g, v_meta_tokens, v_norm_mix_g, v_w_in, v_conv_w, v_ssm_lam_re, v_ssm_lam_im, v_ssm_log_dt, v_ssm_b_re, v_ssm_b_im, v_ssm_c_re, v_ssm_c_im, v_ssm_d, v_ssm_w_glu, v_gain_conv_out, v_gain_ssm_out, v_w_out, v_norm_ffn_g, v_w_up, v_ffn_conv_w, v_ffn_conv_b, v_w_down, v_norm_final_g):
    args = dict(locals())
    w = {n: args[n] for n in WEIGHTS}
    mom = {n: args["m_" + n] for n in WEIGHTS}
    var = {n: args["v_" + n] for n in WEIGHTS}
    kx, ky, kc_ = lax.axis_index("x"), lax.axis_index("y"), lax.axis_index("c")
    chip = 2 * kx + ky
    kc = jnp.stack([chip, kc_]).astype(jnp.int32)

    def squeeze(n, a):
        if n == "meta_tokens":
            return a
        if n == "norm_final_g":
            return a.reshape(1, -1)
        a = a[0]
        return a.reshape(1, -1) if a.ndim == 1 else a

    wl = {n: squeeze(n, w[n]) for n in WEIGHTS}
    ml = {n: squeeze(n, mom[n]) for n in WEIGHTS}
    vl = {n: squeeze(n, var[n]) for n in WEIGHTS}

    tiny = _pack([wl[n] for n in TINY_SHARDED], SUBLANES, LANES)
    ex = _Exchanges({n: wl[n] for n in BIG_NAMES}, tiny, kc)
    tiny_shapes = [wl[n].shape for n in TINY_SHARDED]
    tiny_all = ex.small_params(kc)
    tiny_parts = [_unpack(tiny_all[k], tiny_shapes) for k in range(4)]
    p = {n: wl[n] for n in WEIGHTS if n not in BIG}
    for j, n in enumerate(TINY_SHARDED):
        p[n] = jnp.concatenate([tiny_parts[k][j] for k in range(4)], axis=1)
    p["ssm_log_dt"] = wl["ssm_log_dt"].reshape(-1)

    loss_local, grad_x, grads = _local_step(x[0], loss_target[0], p, ex)

    small_names = REPLICATED + TINY_SHARDED
    small_shapes = [tuple(grads[n].shape) for n in small_names] + [(1,)]
    pack = _pack([grads[n] for n in small_names] + [loss_local.reshape(1)], 2 * 16, PACK_COLS)
    g_pack = ex.finish_pack(pack)
    g_small = dict(zip(small_names + ("loss",), _unpack(g_pack, small_shapes)))
    loss = g_small["loss"][0]
    swapped = ("ssm_b_re", "ssm_b_im")

    def view(n, a):
        if n in swapped:
            return jnp.swapaxes(a, -1, -2)
        return a.reshape(1, -1) if a.ndim == 1 else a

    g = {}
    for n in REPLICATED:
        g[n] = g_small[n].reshape(view(n, w[n]).shape)
    for n in TINY_SHARDED:
        cols = wl[n].shape[1]
        g[n] = lax.dynamic_slice_in_dim(g_small[n], chip * cols, cols, axis=1).reshape(w[n].shape)
    delta, new_m, new_v = {}, {}, {}
    small = [[view(n, d[n]) for n in small_names] for d in (w, mom, var)]
    small.insert(1, [g[n] for n in small_names])
    for d, outs in zip((delta, new_m, new_v), _adamw_whole(*small, "adamw_small")):
        d.update(zip(small_names, outs))
    for d in (g, delta, new_m, new_v):
        d.update({n: jnp.swapaxes(d[n], -1, -2) for n in swapped})
    g_big = ex.finish_big(delta[small_names[0]])
    for n in BIG_NAMES:
        g[n], delta[n], new_m[n], new_v[n] = _adamw(wl[n], g_big[n], ml[n], vl[n], "adamw_" + n)

    def like(n, a):
        return a.reshape(w[n].shape)

    return (loss, grad_x[None], *[like(n, g[n]) for n in WEIGHTS], *[like(n, delta[n]) for n in WEIGHTS],
            *[like(n, new_m[n]) for n in WEIGHTS], *[like(n, new_v[n]) for n in WEIGHTS])
```

```python
import functools
import math

import jax
import jax.numpy as jnp
from jax import lax
from jax.experimental import pallas as pl
from jax.experimental.pallas import tpu as pltpu

F32 = jnp.float32
BF16 = jnp.bfloat16
MESH = pl.DeviceIdType.MESH

N_META = 16
N_GROUPS = 32
GROUP = 16
STATE = 64
RMS_EPS = 1e-6
ADAM_LR = 0.001
ADAM_B1 = 0.9
ADAM_B2 = 0.999
ADAM_EPS = 1e-08
ADAM_WD = 0.01
ADAM_STEP = 10

LANES = 128
SUBLANES = 8
ROW_ALIGN = 128
ROW_TILES = 4
VMEM_LIMIT = 52 * 1024 * 1024
MM_VMEM_BUDGET = 40 * 1024 * 1024
GELU_C = math.sqrt(2.0 / math.pi)
GELU_A = 0.044715


def _cparams(*sem):
    return pltpu.CompilerParams(dimension_semantics=sem, vmem_limit_bytes=VMEM_LIMIT)


def _pick_tile(dim, cap, mult):
    best = None
    for t in range(mult, min(dim, cap) + 1, mult):
        if dim % t == 0:
            best = t
    return best if best is not None else dim


def _mm(a, b, mode, name, out_dtype=F32, acc_in=None, after=None):
    if mode == "tn":
        kdim, m = a.shape
    else:
        m, kdim = a.shape
    n = b.shape[0] if mode == "nt" else b.shape[1]
    tm = _pick_tile(m, 1408, LANES if mode == "tn" else 16)
    tk = _pick_tile(kdim, 2816, LANES)
    nk = kdim // tk
    out_bytes = jnp.dtype(out_dtype).itemsize
    for cap in (704, 512, 256, LANES) if m == tm else (1408, 1024, 512, 256, LANES):
        tn = _pick_tile(n, cap, LANES)
        blocks = 2 * (tm * tk * 2 + tk * tn * 2 + tm * tn * out_bytes * (2 if acc_in is not None else 1))
        if blocks + (tm * tn * 4 if nk > 1 else 0) <= MM_VMEM_BUDGET:
            break
    has_acc = acc_in is not None

    def body(*refs):
        if after is not None:
            refs = refs[1:]
        if has_acc:
            a_ref, b_ref, c_ref, o_ref = refs[:4]
            rest = refs[4:]
        else:
            a_ref, b_ref, o_ref = refs[:3]
            c_ref = None
            rest = refs[3:]
        if mode == "nn":
            p = jnp.dot(a_ref[...], b_ref[...], preferred_element_type=F32)
        elif mode == "nt":
            p = lax.dot_general(a_ref[...], b_ref[...], (((1,), (1,)), ((), ())), preferred_element_type=F32)
        else:
            p = lax.dot_general(a_ref[...], b_ref[...], (((0,), (0,)), ((), ())), preferred_element_type=F32)
        if nk == 1:
            if has_acc:
                p = p + c_ref[...]
            o_ref[...] = p.astype(out_dtype)
        else:
            acc_ref = rest[0]
            k = pl.program_id(2)

            @pl.when(k == 0)
            def _():
                acc_ref[...] = p + c_ref[...] if has_acc else p

            @pl.when(k > 0)
            def _():
                acc_ref[...] += p

            @pl.when(k == nk - 1)
            def _():
                o_ref[...] = acc_ref[...].astype(out_dtype)

    if mode == "tn":
        a_spec = pl.BlockSpec((tk, tm), lambda i, j, k: (k, i))
    else:
        a_spec = pl.BlockSpec((tm, tk), lambda i, j, k: (i, k))
    if mode == "nt":
        b_spec = pl.BlockSpec((tn, tk), lambda i, j, k: (j, k))
    else:
        b_spec = pl.BlockSpec((tk, tn), lambda i, j, k: (k, j))
    o_spec = pl.BlockSpec((tm, tn), lambda i, j, k: (i, j))
    in_specs = [a_spec, b_spec] + ([o_spec] if has_acc else [])
    args = (a, b) + ((acc_in,) if has_acc else ())
    if after is not None:
        in_specs = [pl.BlockSpec(memory_space=pl.ANY)] + in_specs
        args = (after,) + args
    return pl.pallas_call(
        body, name=name, grid=(m // tm, n // tn, nk),
        in_specs=in_specs, out_specs=o_spec,
        out_shape=jax.ShapeDtypeStruct((m, n), out_dtype),
        scratch_shapes=[pltpu.VMEM((tm, tn), F32)] if nk > 1 else [],
        compiler_params=_cparams("parallel", "parallel", "arbitrary"),
    )(*args)


def _mm_rows(a, b, mode, name, ins, outs, epilogue, scratch=()):
    m, kdim = a.shape
    n = b.shape[0] if mode == "nt" else b.shape[1]
    tm = m // ROW_TILES
    tk = _pick_tile(kdim, 2816, LANES)
    nk = kdim // tk
    ni, no = len(ins), len(outs)

    def body(*refs):
        a_ref, b_ref = refs[:2]
        in_refs, out_refs, rest = refs[2:2 + ni], refs[2 + ni:2 + ni + no], refs[2 + ni + no:]
        k, i = pl.program_id(0), pl.program_id(1)
        if mode == "nn":
            p = jnp.dot(a_ref[...], b_ref[...], preferred_element_type=F32)
        else:
            p = lax.dot_general(a_ref[...], b_ref[...], (((1,), (1,)), ((), ())), preferred_element_type=F32)
        if nk == 1:
            epilogue(p, i, in_refs, out_refs, rest)
        else:
            acc_ref = rest[0]
            rows = pl.ds(pl.multiple_of(i * tm, SUBLANES), tm)

            @pl.when(k == 0)
            def _():
                acc_ref[rows, :] = p

            @pl.when(jnp.logical_and(k > 0, k < nk - 1))
            def _():
                acc_ref[rows, :] += p

            @pl.when(k == nk - 1)
            def _():
                epilogue(acc_ref[rows, :] + p, i, in_refs, out_refs, rest[1:])

    tile = (lambda k, i: i) if nk == 1 else (lambda k, i: jnp.where(k == nk - 1, i, 0))

    def spec(shape, kind):
        if kind == "rows":
            return pl.BlockSpec((tm,) + tuple(shape[1:]), lambda k, i: (tile(k, i),) + (0,) * (len(shape) - 1))
        if kind == "whole":
            return pl.BlockSpec(tuple(shape), lambda k, i: (0,) * len(shape))
        return pl.BlockSpec(memory_space=pl.ANY)

    a_spec = pl.BlockSpec((tm, tk), lambda k, i: (i, k))
    b_spec = pl.BlockSpec((n, tk), lambda k, i: (0, k)) if mode == "nt" else pl.BlockSpec((tk, n), lambda k, i: (k, 0))
    return pl.pallas_call(
        body, name=name, grid=(nk, ROW_TILES),
        in_specs=[a_spec, b_spec] + [spec(x.shape, kind) for x, kind in ins],
        out_specs=[spec(shape, kind) for shape, _, kind in outs],
        out_shape=[jax.ShapeDtypeStruct(shape, dtype) for shape, dtype, _ in outs],
        scratch_shapes=([pltpu.VMEM((m, n), F32)] if nk > 1 else []) + list(scratch),
        compiler_params=_cparams("arbitrary", "arbitrary"),
    )(a, b, *[x for x, _ in ins])


def _rows(shape_cols, tr, dtype=None):
    return pl.BlockSpec((tr, shape_cols), lambda i: (i, 0))


def _const(shape):
    return pl.BlockSpec(shape, lambda i: (0,) * len(shape))


def _rms(x):
    return lax.rsqrt(jnp.mean(x * x, axis=-1, keepdims=True) + RMS_EPS)


def _rms_bwd(x, r, g, dy):
    xn = x * r
    dxn = dy * g
    dx = r * (dxn - xn * jnp.mean(dxn * xn, axis=-1, keepdims=True))
    return dx, dy * xn


def _gelu(y):
    return 0.5 * y * (1.0 + jnp.tanh(GELU_C * (y + GELU_A * y * y * y)))


def _gelu_grad(y):
    t = jnp.tanh(GELU_C * (y + GELU_A * y * y * y))
    return 0.5 * (1.0 + t) + 0.5 * y * (1.0 - t * t) * GELU_C * (1.0 + 3.0 * GELU_A * y * y)


def _sigmoid(z):
    return 1.0 / (1.0 + jnp.exp(-z))


def _proj_res_norm(a, w, h, g, after, name):
    def epilogue(p, i, ins, outs, _):
        x = ins[0][...] + p
        outs[0][...] = x
        outs[1][...] = (x * _rms(x) * ins[1][...]).astype(BF16)

    return _mm_rows(a, w, "nn", name, [(h, "rows"), (g, "whole"), (after, "hbm")],
                    [(h.shape, F32, "rows"), (h.shape, BF16, "rows")], epilogue)


def _proj_norm_bwd(da, w, h, g, dres, after, name):
    d = h.shape[1]

    def epilogue(p, i, ins, outs, _):
        x = ins[0][...]
        dx, dgs = _rms_bwd(x, _rms(x), ins[1][...], p)
        dh = ins[2][...] + dx
        outs[0][...] = dh
        outs[1][...] = dh.astype(BF16)

        @pl.when(i == 0)
        def _():
            outs[2][...] = jnp.zeros_like(outs[2])

        outs[2][...] += jnp.sum(dgs, axis=0, keepdims=True)

    return _mm_rows(da, w, "nt", name, [(h, "rows"), (g, "whole"), (dres, "rows"), (after, "hbm")],
                    [(h.shape, F32, "rows"), (h.shape, BF16, "rows"), ((1, d), F32, "whole")], epilogue)


def _proj_input_norm_bwd(da, w, h, g, dres, after, n_real, name):
    tp, d = h.shape
    tr = tp // ROW_TILES

    def epilogue(p, i, ins, outs, scratch):
        h_ref, g_ref, dres_ref, _ = ins
        dx_ref, dmeta_ref, dg_ref = outs
        stage, sem = scratch
        x = h_ref[...]
        dx, dgs = _rms_bwd(x, _rms(x), g_ref[...], p)
        stage[...] = dres_ref[...] + dx

        @pl.when(i == 0)
        def _():
            dg_ref[...] = jnp.zeros_like(dg_ref)
            dmeta_ref[...] = stage[:N_META, :]

        dg_ref[...] += jnp.sum(dgs, axis=0, keepdims=True)
        for t in range(ROW_TILES):
            lo, hi = max(t * tr, N_META), min((t + 1) * tr, n_real)
            if hi > lo:
                @pl.when(i == t)
                def _(t=t, lo=lo, hi=hi):
                    cp = pltpu.make_async_copy(stage.at[pl.ds(lo - t * tr, hi - lo), :],
                                               dx_ref.at[pl.ds(lo - N_META, hi - lo), :], sem)
                    cp.start()
                    cp.wait()

    return _mm_rows(da, w, "nt", name, [(h, "rows"), (g, "whole"), (dres, "rows"), (after, "hbm")],
                    [((n_real - N_META, d), F32, "hbm"), ((N_META, d), F32, "whole"), ((1, d), F32, "whole")],
                    epilogue, scratch=[pltpu.VMEM((tr, d), F32), pltpu.SemaphoreType.DMA])


def _load_token_rows(tok_hbm, buf, sem, tr, n_real, head=None, wait=False, i=None):
    i = pl.program_id(0) if i is None else i
    for t in range(ROW_TILES):
        base = t * tr
        lo, hi = max(base, N_META), min(base + tr, n_real)

        @pl.when(i == t)
        def _(base=base, lo=lo, hi=hi):
            if hi > lo:
                cp = pltpu.make_async_copy(tok_hbm.at[pl.ds(lo - N_META, hi - lo), :],
                                           buf.at[pl.ds(lo - base, hi - lo), :], sem)
                if wait:
                    cp.wait()
                    return
                cp.start()
            if wait:
                return
            if base < N_META:
                buf[0:N_META - base, :] = (jnp.zeros((N_META - base, buf.shape[1]), F32) if head is None
                                           else head[base:N_META, :])
            if hi < base + tr:
                buf[max(hi, base) - base:tr, :] = jnp.zeros((base + tr - max(hi, base), buf.shape[1]), F32)


def _input_norm_fwd(x, meta, g, tp, name):
    seq, d = x.shape
    tr = tp // ROW_TILES
    n_real = N_META + seq

    def body(x_hbm, meta_ref, g_ref, h_ref, hn_ref, buf, sem):
        _load_token_rows(x_hbm, buf, sem, tr, n_real, head=meta_ref)
        _load_token_rows(x_hbm, buf, sem, tr, n_real, wait=True)
        h = buf[...]
        h_ref[...] = h
        hn_ref[...] = (h * _rms(h) * g_ref[...]).astype(BF16)

    return pl.pallas_call(
        body, name=name, grid=(ROW_TILES,),
        in_specs=[pl.BlockSpec(memory_space=pl.ANY), _const((N_META, d)), _const((1, d))],
        out_specs=[_rows(d, tr), _rows(d, tr)],
        out_shape=[jax.ShapeDtypeStruct((tp, d), F32), jax.ShapeDtypeStruct((tp, d), BF16)],
        scratch_shapes=[pltpu.VMEM((tr, d), F32), pltpu.SemaphoreType.DMA],
        compiler_params=_cparams("arbitrary"))(x, meta, g)


def _proj_loss_bwd(act, w, h1, target, g, n_real, name):
    tp, d = h1.shape
    tr = tp // ROW_TILES

    def epilogue(p, i, ins, outs, scratch):
        h1_ref, t_hbm, g_ref = ins
        loss_ref, dh_ref, dhb_ref, dg_ref = outs
        t_buf, sem = scratch
        _load_token_rows(t_hbm, t_buf, sem, tr, n_real, i=i)
        x = h1_ref[...] + p
        r = _rms(x)
        row = i * tr + lax.broadcasted_iota(jnp.int32, (tr, d), 0)
        valid = (row >= N_META) & (row < n_real)
        _load_token_rows(t_hbm, t_buf, sem, tr, n_real, wait=True, i=i)
        e = jnp.where(valid, x * r * g_ref[...] - t_buf[...], 0.0)
        dx, dgs = _rms_bwd(x, r, g_ref[...], e * (1.0 / d))
        dh_ref[...] = dx
        dhb_ref[...] = dx.astype(BF16)

        @pl.when(i == 0)
        def _():
            dg_ref[...] = jnp.zeros_like(dg_ref)
            loss_ref[...] = jnp.zeros_like(loss_ref)

        dg_ref[...] += jnp.sum(dgs, axis=0, keepdims=True)
        loss_ref[...] += (0.5 / d) * jnp.sum(jnp.sum(e * e, axis=0, keepdims=True), axis=1, keepdims=True)

    return _mm_rows(act, w, "nn", name, [(h1, "rows"), (target, "hbm"), (g, "whole")],
                    [((1, LANES), F32, "whole"), ((tp, d), F32, "rows"), ((tp, d), BF16, "rows"),
                     ((1, d), F32, "whole")],
                    epilogue, scratch=[pltpu.VMEM((tr, d), F32), pltpu.SemaphoreType.DMA])


def _mix_fwd(co, y, z, gc, gs, name):
    tp, dh = co.shape
    tr = tp // ROW_TILES

    def body(co_ref, y_ref, z_ref, gc_ref, gs_ref, m_ref):
        c = co_ref[...]
        m_ref[:, :dh] = (c * _rms(c) * gc_ref[...]).astype(BF16)
        so = _gelu(y_ref[...]) * _sigmoid(z_ref[...])
        m_ref[:, dh:] = (so * _rms(so) * gs_ref[...]).astype(BF16)

    return pl.pallas_call(
        body, name=name, grid=(ROW_TILES,),
        in_specs=[_rows(dh, tr)] * 3 + [_const((1, dh))] * 2,
        out_specs=_rows(2 * dh, tr),
        out_shape=jax.ShapeDtypeStruct((tp, 2 * dh), BF16),
        compiler_params=_cparams("parallel"))(co, y, z, gc, gs)


def _proj_mix_bwd(dh1b, w, co, y, z, gc, gs, name):
    tp, dh = co.shape

    def epilogue(p, i, ins, outs, _):
        co_ref, y_ref, z_ref, gc_ref, gs_ref = ins
        dco_ref, dz_ref, dgp_ref, dgc_ref, dgs_ref = outs
        c = co_ref[...]
        dco, dgc = _rms_bwd(c, _rms(c), gc_ref[...], p[:, :dh])
        dco_ref[...] = dco
        gl = _gelu(y_ref[...])
        sg = _sigmoid(z_ref[...])
        so = gl * sg
        dso, dgs = _rms_bwd(so, _rms(so), gs_ref[...], p[:, dh:])
        dz_ref[...] = (dso * gl * sg * (1.0 - sg)).astype(BF16)
        dgp_ref[...] = dso * sg

        @pl.when(i == 0)
        def _():
            dgc_ref[...] = jnp.zeros_like(dgc_ref)
            dgs_ref[...] = jnp.zeros_like(dgs_ref)

        dgc_ref[...] += jnp.sum(dgc, axis=0, keepdims=True)
        dgs_ref[...] += jnp.sum(dgs, axis=0, keepdims=True)

    return _mm_rows(dh1b, w, "nt", name,
                    [(co, "rows"), (y, "rows"), (z, "rows"), (gc, "whole"), (gs, "whole")],
                    [((tp, dh), F32, "rows"), ((tp, dh), BF16, "rows"), ((tp, dh), F32, "rows"),
                     ((1, dh), F32, "whole"), ((1, dh), F32, "whole")], epilogue)


def _shift_down(x, k):
    row = lax.broadcasted_iota(jnp.int32, x.shape, 0)
    return jnp.where(row >= k, pltpu.roll(x, k, 0), 0.0)


def _shift_up(x, k):
    n = x.shape[0]
    row = lax.broadcasted_iota(jnp.int32, x.shape, 0)
    return jnp.where(row < n - k, pltpu.roll(x, n - k, 0), 0.0)


def _dwconv(x, w_ref):
    return w_ref[2:3, :] * x + w_ref[1:2, :] * _shift_down(x, 1) + w_ref[0:1, :] * _shift_down(x, 2)


def _dwconv_bwd(x, dy, w_ref):
    dx = w_ref[2:3, :] * dy + w_ref[1:2, :] * _shift_up(dy, 1) + w_ref[0:1, :] * _shift_up(dy, 2)
    dw = jnp.concatenate([jnp.sum(dy * _shift_down(x, 2), axis=0, keepdims=True),
                          jnp.sum(dy * _shift_down(x, 1), axis=0, keepdims=True),
                          jnp.sum(dy * x, axis=0, keepdims=True)], axis=0)
    return dx, dw


def _interleave(dst, src):
    seg_rows = src.shape[0] // SUBLANES
    for seg in range(SUBLANES):
        dst[pl.ds(seg, seg_rows, stride=SUBLANES), :] = src[seg * seg_rows:(seg + 1) * seg_rows, :]


def _deinterleave(dst, src):
    seg_rows = src.shape[0] // SUBLANES
    for seg in range(SUBLANES):
        dst[seg * seg_rows:(seg + 1) * seg_rows, :] = src[pl.ds(seg, seg_rows, stride=SUBLANES), :]


def _segment_shift(x, reverse):
    row = lax.broadcasted_iota(jnp.int32, x.shape, 0)
    if reverse:
        return jnp.where(row < SUBLANES - 1, pltpu.roll(x, SUBLANES - 1, 0), 0.0)
    return jnp.where(row >= 1, pltpu.roll(x, 1, 0), 0.0)


def _scan(s_re, s_im, pw_ref, reverse, pair=None):
    n_steps = s_re.shape[0] // SUBLANES
    n_strips = s_re.shape[1] // LANES
    sign = -1.0 if reverse else 1.0
    strips = [slice(st * LANES, (st + 1) * LANES) for st in range(n_strips)]

    def rows_of(j):
        step = (n_steps - 1 - j) if reverse else j
        return pl.ds(pl.multiple_of(step * SUBLANES, SUBLANES), SUBLANES)

    a = [(jnp.broadcast_to(pw_ref[0, 0:1, lanes], (SUBLANES, LANES)),
          sign * jnp.broadcast_to(pw_ref[1, 0:1, lanes], (SUBLANES, LANES))) for lanes in strips]

    def local(i, carry):
        for half in range(2):
            rows = rows_of(2 * i + half)
            out = []
            for st, lanes in enumerate(strips):
                (ar, ai), cr, ci = a[st], carry[2 * st], carry[2 * st + 1]
                xr = s_re[rows, lanes] + (ar * cr - ai * ci)
                xi = s_im[rows, lanes] + (ar * ci + ai * cr)
                s_re[rows, lanes] = xr
                s_im[rows, lanes] = xi
                out += [xr, xi]
            carry = tuple(out)
        return carry

    zero = jnp.zeros((SUBLANES, LANES), F32)
    ends = lax.fori_loop(0, n_steps // 2, local, (zero,) * (2 * n_strips))

    entering = []
    row = lax.broadcasted_iota(jnp.int32, (SUBLANES, LANES), 0)
    for st, lanes in enumerate(strips):
        tr, ti = ends[2 * st], ends[2 * st + 1]
        mr = jnp.broadcast_to(pw_ref[0, n_steps - 1:n_steps, lanes], (SUBLANES, LANES))
        mi = sign * jnp.broadcast_to(pw_ref[1, n_steps - 1:n_steps, lanes], (SUBLANES, LANES))
        for k in (1, 2, 4):
            keep = (row < SUBLANES - k) if reverse else (row >= k)
            rr = jnp.where(keep, pltpu.roll(tr, SUBLANES - k if reverse else k, 0), 0.0)
            ri = jnp.where(keep, pltpu.roll(ti, SUBLANES - k if reverse else k, 0), 0.0)
            tr, ti = tr + (mr * rr - mi * ri), ti + (mr * ri + mi * rr)
            mr, mi = mr * mr - mi * mi, 2.0 * mr * mi
        entering += [_segment_shift(tr, reverse), _segment_shift(ti, reverse)]

    def fix(i, carry):
        carry, sums = carry[:2 * n_strips], carry[2 * n_strips:]
        for half in range(2):
            j = 2 * i + half
            rows = rows_of(j)
            out, acc = [], []
            for st, lanes in enumerate(strips):
                (ar, ai), cr, ci = a[st], carry[2 * st], carry[2 * st + 1]
                cr, ci = ar * cr - ai * ci, ar * ci + ai * cr
                xr = s_re[rows, lanes] + cr
                xi = s_im[rows, lanes] + ci
                s_re[rows, lanes] = xr
                s_im[rows, lanes] = xi
                out += [cr, ci]
                if pair is not None:
                    p_rows = rows_of(jnp.minimum(j + 1, n_steps - 1))
                    keep = (j < n_steps - 1).astype(F32)
                    pr = pair[0][p_rows, lanes] * keep
                    pi = pair[1][p_rows, lanes] * keep
                    acc += [sums[2 * st] + (xr * pr + xi * pi), sums[2 * st + 1] + (xi * pr - xr * pi)]
            carry, sums = tuple(out), tuple(acc)
        return carry + sums

    n_sums = 0 if pair is None else 2 * n_strips
    out = lax.fori_loop(0, n_steps // 2, fix, tuple(entering) + (zero,) * n_sums)
    return out[2 * n_strips:]


def _seq_fwd(proj, conv_w, bc_re, bc_im, cc_re, cc_im, dskip, a_pow, name):
    tp = proj.shape[0]
    dh = proj.shape[1] // 4
    nq = dh // LANES
    sw = STATE * N_GROUPS // nq

    def body(b_ref, c_ref, v_ref, u_ref, w_ref, bre_ref, bim_ref, cre_ref, cim_ref, d_ref, pw_ref,
             co_ref, y_ref, g_ref, s_re, s_im, u_il, y_il):
        co_ref[...] = b_ref[...] * _dwconv(c_ref[...] * v_ref[...], w_ref)
        _interleave(u_il, u_ref)
        ub = u_il[...].astype(BF16)
        s_re[...] = jnp.dot(ub, bre_ref[...], preferred_element_type=F32)
        s_im[...] = jnp.dot(ub, bim_ref[...], preferred_element_type=F32)
        _scan(s_re, s_im, pw_ref, False)
        y_il[...] = (jnp.dot(s_re[...].astype(BF16), cre_ref[...], preferred_element_type=F32)
                     - jnp.dot(s_im[...].astype(BF16), cim_ref[...], preferred_element_type=F32))
        _deinterleave(y_ref, y_il)
        y = y_ref[...] + d_ref[...] * u_ref[...]
        y_ref[...] = y
        g_ref[...] = _gelu(y).astype(BF16)

    col = lambda off: pl.BlockSpec((tp, LANES), lambda q, off=off: (0, off * nq + q))
    blk = pl.BlockSpec((tp, LANES), lambda q: (0, q))
    return pl.pallas_call(
        body, name=name, grid=(nq,),
        in_specs=[col(0), col(1), col(2), col(3),
                  pl.BlockSpec((3, LANES), lambda q: (0, q)),
                  pl.BlockSpec((LANES, sw), lambda q: (0, q)), pl.BlockSpec((LANES, sw), lambda q: (0, q)),
                  pl.BlockSpec((sw, LANES), lambda q: (q, 0)), pl.BlockSpec((sw, LANES), lambda q: (q, 0)),
                  pl.BlockSpec((1, LANES), lambda q: (0, q)),
                  pl.BlockSpec((2, tp // SUBLANES, sw), lambda q: (0, 0, q))],
        out_specs=[blk, blk, blk, pl.BlockSpec((tp, sw), lambda q: (0, q)), pl.BlockSpec((tp, sw), lambda q: (0, q))],
        out_shape=[jax.ShapeDtypeStruct((tp, dh), F32), jax.ShapeDtypeStruct((tp, dh), F32),
                   jax.ShapeDtypeStruct((tp, dh), BF16),
                   jax.ShapeDtypeStruct((tp, nq * sw), F32), jax.ShapeDtypeStruct((tp, nq * sw), F32)],
        scratch_shapes=[pltpu.VMEM((tp, LANES), F32), pltpu.VMEM((tp, LANES), F32)],
        compiler_params=_cparams("parallel"),
    )(proj, proj, proj, proj, conv_w, bc_re, bc_im, cc_re, cc_im, dskip, a_pow)


def _conv_bwd(proj, dco, conv_w, name):
    tp = proj.shape[0]
    dh = proj.shape[1] // 4
    nq = dh // LANES

    def body(b_ref, c_ref, v_ref, dco_ref, w_ref, dproj_ref, dw_ref, stage, sem):
        q = pl.program_id(0)
        cg = c_ref[...]
        vg = v_ref[...]
        cv = cg * vg
        dco_v = dco_ref[...]
        dcv, dw = _dwconv_bwd(cv, dco_v * b_ref[...], w_ref)
        dw_ref[...] = dw
        stage[0] = (dco_v * _dwconv(cv, w_ref)).astype(BF16)
        stage[1] = (dcv * vg).astype(BF16)
        stage[2] = (dcv * cg).astype(BF16)
        copies = [pltpu.make_async_copy(stage.at[p], dproj_ref.at[:, pl.ds((p * nq + q) * LANES, LANES)], sem.at[p])
                  for p in range(3)]
        for cp in copies:
            cp.start()
        for cp in copies:
            cp.wait()

    col = lambda off: pl.BlockSpec((tp, LANES), lambda q, off=off: (0, off * nq + q))
    return pl.pallas_call(
        body, name=name, grid=(nq,),
        in_specs=[col(0), col(1), col(2), pl.BlockSpec((tp, LANES), lambda q: (0, q)),
                  pl.BlockSpec((3, LANES), lambda q: (0, q))],
        out_specs=[pl.BlockSpec(memory_space=pl.ANY), pl.BlockSpec((3, LANES), lambda q: (0, q))],
        out_shape=[jax.ShapeDtypeStruct((tp, 4 * dh), BF16), jax.ShapeDtypeStruct((3, dh), F32)],
        scratch_shapes=[pltpu.VMEM((3, tp, LANES), BF16), pltpu.SemaphoreType.DMA((3,))],
        compiler_params=_cparams("arbitrary"),
    )(proj, proj, proj, dco, conv_w)


def _ssm_bwd(proj, y, dg, dproj, states, bc_re, bc_im, cc_re, cc_im, dskip, a_pow, name):
    tp = proj.shape[0]
    dh = proj.shape[1] // 4
    nq = dh // LANES
    sw = STATE * N_GROUPS // nq

    def body(u_ref, y_ref, dg_ref, dproj_in, s_re, s_im, bre_ref, bim_ref, cre_ref, cim_ref, d_ref, pw_ref,
             dproj_ref, dbre_ref, dbim_ref, dcre_ref, dcim_ref, dd_ref, dar_ref, dai_ref,
             l_re, l_im, a_il, b_il, stage, sem):
        del dproj_in
        q = pl.program_id(0)
        nt = (((1,), (1,)), ((), ()))
        tn = (((0,), (0,)), ((), ()))
        _interleave(a_il, u_ref)
        ub = a_il[...].astype(BF16)
        dy_rows = dg_ref[...] * _gelu_grad(y_ref[...])
        dd_ref[...] = jnp.sum(dy_rows * u_ref[...], axis=0, keepdims=True)
        _interleave(b_il, dy_rows)
        dy = b_il[...]
        dyb = dy.astype(BF16)
        l_re[...] = lax.dot_general(dyb, cre_ref[...], nt, preferred_element_type=F32)
        l_im[...] = -lax.dot_general(dyb, cim_ref[...], nt, preferred_element_type=F32)
        dcre_ref[...] = lax.dot_general(s_re[...].astype(BF16), dyb, tn, preferred_element_type=F32)
        dcim_ref[...] = -lax.dot_general(s_im[...].astype(BF16), dyb, tn, preferred_element_type=F32)
        sums = _scan(l_re, l_im, pw_ref, True, pair=(s_re, s_im))
        rest = tp - SUBLANES
        for st in range(sw // LANES):
            lanes = slice(st * LANES, (st + 1) * LANES)
            lr0, li0 = l_re[:SUBLANES, lanes], l_im[:SUBLANES, lanes]
            pr0, pi0 = _segment_shift(s_re[rest:, lanes], False), _segment_shift(s_im[rest:, lanes], False)
            dar_ref[:, lanes] = jnp.sum(sums[2 * st] + (lr0 * pr0 + li0 * pi0), axis=0, keepdims=True)
            dai_ref[:, lanes] = jnp.sum(sums[2 * st + 1] + (li0 * pr0 - lr0 * pi0), axis=0, keepdims=True)
        lrb = l_re[...].astype(BF16)
        lib = l_im[...].astype(BF16)
        a_il[...] = (dy * d_ref[...] + lax.dot_general(lrb, bre_ref[...], nt, preferred_element_type=F32)
                     + lax.dot_general(lib, bim_ref[...], nt, preferred_element_type=F32))
        _deinterleave(b_il, a_il)
        stage[...] = b_il[...].astype(BF16)
        dbre_ref[...] = lax.dot_general(ub, lrb, tn, preferred_element_type=F32)
        dbim_ref[...] = lax.dot_general(ub, lib, tn, preferred_element_type=F32)
        cp = pltpu.make_async_copy(stage, dproj_ref.at[:, pl.ds((3 * nq + q) * LANES, LANES)], sem)
        cp.start()
        cp.wait()

    blk = pl.BlockSpec((tp, LANES), lambda q: (0, q))
    bspec = pl.BlockSpec((LANES, sw), lambda q: (0, q))
    cspec = pl.BlockSpec((sw, LANES), lambda q: (q, 0))
    tspec = pl.BlockSpec((2, tp // SUBLANES, sw), lambda q: (0, 0, q))
    nstate = STATE * N_GROUPS
    return pl.pallas_call(
        body, name=name, grid=(nq,),
        in_specs=[pl.BlockSpec((tp, LANES), lambda q: (0, 3 * nq + q)), blk, blk, pl.BlockSpec(memory_space=pl.ANY),
                  pl.BlockSpec((tp, sw), lambda q: (0, q)), pl.BlockSpec((tp, sw), lambda q: (0, q)),
                  bspec, bspec, cspec, cspec, pl.BlockSpec((1, LANES), lambda q: (0, q)), tspec],
        out_specs=[pl.BlockSpec(memory_space=pl.ANY), bspec, bspec, cspec, cspec,
                   pl.BlockSpec((1, LANES), lambda q: (0, q)),
                   pl.BlockSpec((1, sw), lambda q: (0, q)), pl.BlockSpec((1, sw), lambda q: (0, q))],
        out_shape=[jax.ShapeDtypeStruct((tp, 4 * dh), BF16),
                   jax.ShapeDtypeStruct((LANES, nstate), F32), jax.ShapeDtypeStruct((LANES, nstate), F32),
                   jax.ShapeDtypeStruct((nstate, LANES), F32), jax.ShapeDtypeStruct((nstate, LANES), F32),
                   jax.ShapeDtypeStruct((1, dh), F32),
                   jax.ShapeDtypeStruct((1, nstate), F32), jax.ShapeDtypeStruct((1, nstate), F32)],
        input_output_aliases={3: 0},
        scratch_shapes=[pltpu.VMEM((tp, sw), F32)] * 2 + [pltpu.VMEM((tp, LANES), F32)] * 2
        + [pltpu.VMEM((tp, LANES), BF16), pltpu.SemaphoreType.DMA],
        compiler_params=_cparams("arbitrary"),
    )(proj, y, dg, dproj, states[0], states[1], bc_re, bc_im, cc_re, cc_im, dskip, a_pow)


FFN_TILE = 256
FFN_ROWS = 32


def _window(x_ref, before, r0, rows, cols):
    if r0 == 0:
        return jnp.concatenate([before, x_ref[0:rows, cols]], axis=0)
    return x_ref[r0 - SUBLANES:r0 + rows, cols]


def _taps(window):
    return window[SUBLANES:], pltpu.roll(window, 1, 0)[SUBLANES:], pltpu.roll(window, 2, 0)[SUBLANES:]


def _conv_taps(taps, w):
    return w[2] * taps[0] + w[1] * taps[1] + w[0] * taps[2]


FFN_MM_ROWS = 544
FFN_MM_COLS = 1408


def _ffn_up_act(hn, w_up, fw, fb, col, others, name):
    tp, dm = hn.shape
    dff = w_up.shape[1] // 2
    tr, cw, rows = FFN_MM_ROWS, FFN_MM_COLS, FFN_ROWS
    nc = dff // cw
    n_others = 0 if others is None else 2

    def body(hn_ref, ma_ref, mv_ref, wa_ref, wv_ref, ba_ref, bv_ref, *rest):
        up_ref, act_ref, tail_ref = rest[n_others:]

        @pl.when(pl.program_id(0) == 0)
        def _():
            tail_ref[...] = jnp.zeros_like(tail_ref)

        x = hn_ref[...]
        up_ref[0] = jnp.dot(x, ma_ref[...], preferred_element_type=F32)
        up_ref[1] = jnp.dot(x, mv_ref[...], preferred_element_type=F32)
        for c0 in range(0, cw, FFN_TILE):
            cols = slice(c0, min(c0 + FFN_TILE, cw))
            wa, wv = [[w_ref[k:k + 1, cols] for k in range(3)] for w_ref in (wa_ref, wv_ref)]
            ba, bv = ba_ref[:, cols], bv_ref[:, cols]
            before_a, before_v = tail_ref[0, :, cols], tail_ref[1, :, cols]
            for r0 in range(0, tr, rows):
                a = _conv_taps(_taps(_window(up_ref.at[0], before_a, r0, rows, cols)), wa) + ba
                v = _conv_taps(_taps(_window(up_ref.at[1], before_v, r0, rows, cols)), wv) + bv
                act_ref[r0:r0 + rows, cols] = (a * _sigmoid(a) * v).astype(BF16)
            tail_ref[:, :, cols] = up_ref[:, tr - SUBLANES:tr, cols]

    par = lambda r, half: pl.BlockSpec((r, cw), lambda i: (0, half * nc + col))
    return pl.pallas_call(
        body, name=name, grid=(tp // tr,),
        in_specs=[pl.BlockSpec((tr, dm), lambda i: (i, 0)), par(dm, 0), par(dm, 1),
                  par(3, 0), par(3, 1), par(1, 0), par(1, 1)] + [pl.BlockSpec(memory_space=pl.ANY)] * n_others,
        out_specs=[pl.BlockSpec((2, tr, cw), lambda i: (0, i, col)), pl.BlockSpec((tr, cw), lambda i: (i, col))],
        out_shape=[jax.ShapeDtypeStruct((2, tp, dff), F32), jax.ShapeDtypeStruct((tp, dff), BF16)],
        input_output_aliases={7: 0, 8: 1} if others is not None else {},
        scratch_shapes=[pltpu.VMEM((2, SUBLANES, cw), F32)],
        compiler_params=_cparams("arbitrary"))(hn, w_up, w_up, fw, fw, fb, fb, *(others or ()))


def _ffn_bwd(up, dh, w_down, fw, fb, name):
    _, tp, dff = up.shape
    two_ff = 2 * dff
    dm = dh.shape[1]
    tr, cw, rows = FFN_MM_ROWS, FFN_MM_COLS, FFN_ROWS
    nr, nc = tp // tr, dff // cw
    n_e = rows + SUBLANES
    pieces = tr // SUBLANES

    def body(ua_ref, uv_ref, pa_ref, pv_ref, dh_ref, wd_ref, wa_ref, wv_ref, ba_ref, bv_ref,
             dup_ref, dwa_ref, dwv_ref, dba_ref, dbv_ref, dact, stage, head_ref, sem):
        j, i = pl.program_id(0), pl.program_id(1)
        step = j * nr + i
        top = i == nr - 1
        sums = ((dwa_ref, dba_ref), (dwv_ref, dbv_ref))

        slot = step % 2

        def out_copies(at):
            r0 = pl.multiple_of((nr - 1 - at % nr) * tr, tr)
            return [pltpu.make_async_copy(
                stage.at[at % 2, s],
                dup_ref.at[pl.ds(r0, tr), pl.ds(pl.multiple_of(s * dff + at // nr * cw, LANES), cw)],
                sem.at[at % 2, s]) for s in range(2)]

        @pl.when(i == 0)
        def _():
            head_ref[...] = jnp.zeros_like(head_ref)
            for dw_ref, db_ref in sums:
                dw_ref[...] = jnp.zeros_like(dw_ref)
                db_ref[...] = jnp.zeros_like(db_ref)

        dact[...] = lax.dot_general(dh_ref[...], wd_ref[...], (((1,), (1,)), ((), ())), preferred_element_type=F32)

        @pl.when(step > 1)
        def _():
            for cp in out_copies(step - 2):
                cp.wait()

        def gate_bwd(taps, dact_v, w, bias):
            a, v = [_conv_taps(taps[s], w[s]) + bias[s] for s in range(2)]
            sg = _sigmoid(a)
            silu = a * sg
            return [dact_v * v * (sg + silu * (1.0 - sg)), dact_v * silu]

        fold = lambda x: sum(x[r:r + SUBLANES] for r in range(0, rows, SUBLANES))
        for c0 in range(0, cw, FFN_TILE):
            cols = slice(c0, min(c0 + FFN_TILE, cw))
            w = [[w_ref[k:k + 1, cols] for k in range(3)] for w_ref in (wa_ref, wv_ref)]
            bias = [ba_ref[:, cols], bv_ref[:, cols]]
            before = [jnp.where(top, 0.0, p_ref[:, cols]) for p_ref in (pa_ref, pv_ref)]
            head = [head_ref[s, :, cols] for s in range(2)]
            piece = jnp.zeros_like(head[0])
            acc = [[piece] * 4 for _ in range(2)]
            for r0 in reversed(range(0, tr, rows)):
                taps = [_taps(_window(x_ref, before[s], r0, rows, cols)) for s, x_ref in enumerate((ua_ref, uv_ref))]
                d = gate_bwd(taps, dact[r0:r0 + rows, cols], w, bias)
                for s in range(2):
                    de = jnp.concatenate([d[s], head[s]], axis=0)
                    dx = (w[s][2] * d[s] + w[s][1] * pltpu.roll(de, n_e - 1, 0)[:rows]
                          + w[s][0] * pltpu.roll(de, n_e - 2, 0)[:rows])
                    stage[slot, s, r0:r0 + rows, cols] = dx.astype(BF16)
                    for k in range(3):
                        acc[s][k] = acc[s][k] + fold(d[s] * taps[s][2 - k])
                    acc[s][3] = acc[s][3] + fold(d[s])
                    head[s] = d[s][:SUBLANES]
            for s, (dw_ref, db_ref) in enumerate(sums):
                head_ref[s, :, cols] = head[s]
                dw_ref[:, cols] = dw_ref[:, cols] + jnp.concatenate(
                    [jnp.sum(x, axis=0, keepdims=True) for x in acc[s][:3]], axis=0)
                db_ref[:, cols] = db_ref[:, cols] + jnp.sum(acc[s][3], axis=0, keepdims=True)

        copies = out_copies(step)
        for cp in copies:
            cp.start()

        @pl.when(step == nc * nr - 1)
        def _():
            for cp in out_copies(step - 1) + copies:
                cp.wait()

    row = lambda i: nr - 1 - i
    main = lambda half: pl.BlockSpec((None, tr, cw), lambda j, i: (half, row(i), j))
    prev = lambda half: pl.BlockSpec((None, SUBLANES, cw), lambda j, i: (half, jnp.maximum(row(i) * pieces - 1, 0), j))
    par = lambda r, half: pl.BlockSpec((r, cw), lambda j, i: (0, half * nc + j))
    acc_spec = lambda r: pl.BlockSpec((r, cw), lambda j, i: (0, j))
    return pl.pallas_call(
        body, name=name, grid=(nc, nr),
        in_specs=[main(0), main(1), prev(0), prev(1),
                  pl.BlockSpec((tr, dm), lambda j, i: (row(i), 0)), pl.BlockSpec((cw, dm), lambda j, i: (j, 0)),
                  par(3, 0), par(3, 1), par(1, 0), par(1, 1)],
        out_specs=[pl.BlockSpec(memory_space=pl.ANY), acc_spec(3), acc_spec(3), acc_spec(1), acc_spec(1)],
        out_shape=[jax.ShapeDtypeStruct((tp, two_ff), BF16),
                   jax.ShapeDtypeStruct((3, dff), F32), jax.ShapeDtypeStruct((3, dff), F32),
                   jax.ShapeDtypeStruct((1, dff), F32), jax.ShapeDtypeStruct((1, dff), F32)],
        scratch_shapes=[pltpu.VMEM((tr, cw), F32), pltpu.VMEM((2, 2, tr, cw), BF16),
                        pltpu.VMEM((2, SUBLANES, cw), F32), pltpu.SemaphoreType.DMA((2, 2))],
        compiler_params=_cparams("arbitrary", "arbitrary"))(up, up, up, up, dh, w_down, fw, fw, fb, fb)


def _zoh(lr, li, ld):
    dt = jnp.exp(ld)
    mag = jnp.exp(lr * dt)
    ang = li * dt
    ar = mag * jnp.cos(ang)
    ai = mag * jnp.sin(ang)
    den = lr * lr + li * li
    nr = ar - 1.0
    fr = (nr * lr + ai * li) / den
    fi = (ai * lr - nr * li) / den
    return dt, ar, ai, den, nr, fr, fi


def _s5_prep(lr, li, ld, b_re, b_im, n_pow, name):
    nstate = lr.shape[1]

    def body(lr_ref, li_ref, ld_ref, bre_ref, bim_ref, pw_ref, bcre_ref, bcim_ref):
        _, ar, ai, _, _, fr, fi = _zoh(lr_ref[...], li_ref[...], ld_ref[...])
        bre = bre_ref[...]
        bim = bim_ref[...]
        bcre_ref[...] = (fr * bre - fi * bim).astype(BF16)
        bcim_ref[...] = (fr * bim + fi * bre).astype(BF16)
        row = lax.broadcasted_iota(jnp.int32, (SUBLANES, nstate), 0)
        pr, pi = jnp.zeros((SUBLANES, nstate), F32), jnp.zeros((SUBLANES, nstate), F32)
        cr, ci = ar, ai
        for t in range(SUBLANES):
            pr, pi = jnp.where(row == t, cr, pr), jnp.where(row == t, ci, pi)
            cr, ci = cr * ar - ci * ai, cr * ai + ci * ar
        pw_ref[0, 0:SUBLANES, :] = pr
        pw_ref[1, 0:SUBLANES, :] = pi
        n = SUBLANES
        while n < n_pow:
            m = min(n, n_pow - n)
            tr, ti = pw_ref[0, n - 1:n, :], pw_ref[1, n - 1:n, :]
            xr, xi = pw_ref[0, 0:m, :], pw_ref[1, 0:m, :]
            pw_ref[0, n:n + m, :] = xr * tr - xi * ti
            pw_ref[1, n:n + m, :] = xr * ti + xi * tr
            n += m

    vmem = pl.BlockSpec(memory_space=pltpu.VMEM)
    return pl.pallas_call(
        body, name=name, in_specs=[vmem] * 5, out_specs=[vmem] * 3,
        out_shape=[jax.ShapeDtypeStruct((2, n_pow, nstate), F32)] + [jax.ShapeDtypeStruct(b_re.shape, BF16)] * 2,
        compiler_params=pltpu.CompilerParams(vmem_limit_bytes=VMEM_LIMIT))(lr, li, ld, b_re, b_im)


def _s5_prep_bwd(lr, li, ld, b_re, b_im, da_re, da_im, dbc_re, dbc_im, name):
    def body(lr_ref, li_ref, ld_ref, bre_ref, bim_ref, dar_ref, dai_ref, dbcre_ref, dbcim_ref,
             dlr_ref, dli_ref, dld_ref, dbre_ref, dbim_ref):
        lr, li = lr_ref[...], li_ref[...]
        dt, ar, ai, den, nr, fr, fi = _zoh(lr, li, ld_ref[...])
        bre, bim = bre_ref[...], bim_ref[...]
        gre, gim = dbcre_ref[...], dbcim_ref[...]
        dbre_ref[...] = fr * gre + fi * gim
        dbim_ref[...] = fr * gim - fi * gre
        g_fr = jnp.sum(gre * bre + gim * bim, axis=0, keepdims=True)
        g_fi = jnp.sum(gim * bre - gre * bim, axis=0, keepdims=True)
        g_ar = dar_ref[...] + (g_fr * lr - g_fi * li) / den
        g_ai = dai_ref[...] + (g_fr * li + g_fi * lr) / den
        d_lr = (g_fr * (nr - 2.0 * fr * lr) + g_fi * (ai - 2.0 * fi * lr)) / den
        d_li = (g_fr * (ai - 2.0 * fr * li) - g_fi * (nr + 2.0 * fi * li)) / den
        g_logmag = g_ar * ar + g_ai * ai
        g_ang = g_ai * ar - g_ar * ai
        dlr_ref[...] = d_lr + g_logmag * dt
        dli_ref[...] = d_li + g_ang * dt
        d_ld = (g_logmag * lr + g_ang * li) * dt
        n = d_ld.shape[1]
        sh = 1
        while sh < STATE:
            d_ld = d_ld + pltpu.roll(d_ld, n - sh, 1)
            sh *= 2
        dld_ref[...] = d_ld

    vmem = pl.BlockSpec(memory_space=pltpu.VMEM)
    row = jax.ShapeDtypeStruct(lr.shape, F32)
    return pl.pallas_call(
        body, name=name, in_specs=[vmem] * 9, out_specs=[vmem] * 5,
        out_shape=[row, row, row, jax.ShapeDtypeStruct(b_re.shape, F32), jax.ShapeDtypeStruct(b_re.shape, F32)],
    )(lr, li, ld, b_re, b_im, da_re, da_im, dbc_re, dbc_im)


def _compact_b(bb):
    bq = bb.reshape(N_GROUPS // 8, 8, STATE, GROUP)
    m = jnp.einsum("ab,qbph->qahbp", jnp.eye(8, dtype=bb.dtype), bq).reshape(N_GROUPS // 8, LANES, 8 * STATE)
    return m.transpose(1, 0, 2).reshape(LANES, N_GROUPS * STATE)


def _expand_b(m):
    d = m.reshape(8, GROUP, N_GROUPS // 8, 8, STATE)
    return jnp.einsum("ahqap->qahp", d).reshape(N_GROUPS, GROUP, STATE)


def _compact_c(c):
    cq = c.reshape(N_GROUPS // 8, 8, GROUP, STATE)
    return jnp.einsum("ab,qbhp->qbpah", jnp.eye(8, dtype=c.dtype), cq).reshape(N_GROUPS * STATE, LANES)


def _expand_c(m):
    d = m.reshape(N_GROUPS // 8, 8, STATE, 8, GROUP)
    return jnp.einsum("qbpbh->qbhp", d).reshape(N_GROUPS, GROUP, STATE)


def _local_step(x, target, p, ex):
    seq, d = x.shape
    n_real = N_META + seq
    tp = -(-n_real // ROW_ALIGN) * ROW_ALIGN

    h0, hn1 = _input_norm_fwd(x, p["meta_tokens"], p["norm_mix_g"] + ex.zero, tp, "norm_mix")
    ex.forward("first", hn1)
    nstate = N_GROUPS * STATE
    s5 = (p["ssm_lam_re"].reshape(1, nstate), p["ssm_lam_im"].reshape(1, nstate),
          jnp.repeat(p["ssm_log_dt"].reshape(-1), STATE).reshape(1, nstate),
          _compact_b(p["ssm_b_re"]), _compact_b(p["ssm_b_im"]))
    a_pow, bc_re, bc_im = _s5_prep(*s5, tp // SUBLANES, "s5_prep")
    cc_re = _compact_c(p["ssm_c_re"]).astype(BF16)
    cc_im = _compact_c(p["ssm_c_im"]).astype(BF16)
    dskip = p["ssm_d"].reshape(1, -1)
    first = ex.weights("first", bc_re)
    proj = _mm(hn1, first["w_in"], "nn", "proj")
    started = ex.forward("mid", proj)
    co, y, g, *states = _seq_fwd(proj, p["conv_w"] + started[0, 0], bc_re, bc_im, cc_re, cc_im, dskip, a_pow,
                                 "seq_fwd")
    mid = ex.weights("mid", g)
    z = _mm(g, mid["ssm_w_glu"], "nn", "glu")
    mixed = _mix_fwd(co, y, z, p["gain_conv_out"], p["gain_ssm_out"], "mix_fwd")
    started = ex.forward("up", mixed)
    h1, hn2 = _proj_res_norm(mixed, mid["w_out"], h0, p["norm_ffn_g"], started, "out_proj_norm")
    late = ex.weights("up", hn2)
    part, fw = None, p["ffn_conv_w"]
    for col in range(late["w_up"].shape[1] // (2 * FFN_MM_COLS)):
        part = _ffn_up_act(hn2, late["w_up"], fw, p["ffn_conv_b"], col, part, "ffn_up_act_%d" % col)
        if col == 0:
            fw = fw + ex.forward("down", part[1])[0, 0]
    up, act = part
    late.update(ex.weights("down", act))
    loss, dh2, dh2b, d_gfin = _proj_loss_bwd(act, late["w_down"], h1, target, p["norm_final_g"], n_real,
                                             "down_proj_loss")

    g_w_down = _mm(act, dh2b, "tn", "g_w_down")
    dup, dfw_a, dfw_v, dfb_a, dfb_v = _ffn_bwd(up, dh2b, late["w_down"], p["ffn_conv_w"], p["ffn_conv_b"], "ffn_bwd")
    g_w_up = _mm(hn2, dup, "tn", "g_w_up")
    started = ex.grads_ready("late", {"w_up": g_w_up, "w_down": g_w_down})
    dh1, dh1b, d_gffn = _proj_norm_bwd(dup, late["w_up"], h1, p["norm_ffn_g"], dh2, started, "d_hn2_norm_bwd")
    started = ex.grads_send("late", dh1)
    g_w_out = _mm(mixed, dh1b, "tn", "g_w_out", after=started)
    dco, dz, dgp, d_gc, d_gs = _proj_mix_bwd(dh1b, mid["w_out"], co, y, z, p["gain_conv_out"],
                                             p["gain_ssm_out"], "d_mixed_mix_bwd")
    g_w_glu = _mm(g, dz, "tn", "g_w_glu")
    started = ex.grads_ready("mid", {"ssm_w_glu": g_w_glu, "w_out": g_w_out})
    dg = _mm(dz, mid["ssm_w_glu"], "nt", "d_gelu", acc_in=dgp, after=started)
    started = ex.grads_send("mid", dg)
    dproj, d_conv_w = _conv_bwd(proj, dco, p["conv_w"] + started[0, 0], "conv_bwd")
    (dproj, dbc_re, dbc_im, dcc_re, dcc_im, d_dskip, da_re, da_im) = _ssm_bwd(
        proj, y, dg, dproj, states, bc_re, bc_im, cc_re, cc_im, dskip, a_pow, "ssm_bwd")
    g_w_in = _mm(hn1, dproj, "tn", "g_w_in")
    started = ex.grads_ready("first", {"w_in": g_w_in})
    grad_x, d_meta, d_gmix = _proj_input_norm_bwd(dproj, first["w_in"], h0, p["norm_mix_g"], dh1, started, n_real,
                                                  "d_hn1_norm_bwd")
    started = ex.grads_send("first", d_gmix)

    d_lam_re, d_lam_im, d_log_dt, d_b_re, d_b_im = _s5_prep_bwd(*s5, da_re, da_im, dbc_re, dbc_im, "s5_prep_bwd")
    d_lam_re, d_lam_im = d_lam_re.reshape(N_GROUPS, STATE), d_lam_im.reshape(N_GROUPS, STATE)
    d_log_dt = d_log_dt[0, ::STATE]
    d_b_re, d_b_im = _expand_b(d_b_re), _expand_b(d_b_im)
    grads = {
        "meta_tokens": d_meta, "norm_mix_g": d_gmix, "w_in": g_w_in, "conv_w": d_conv_w,
        "ssm_lam_re": d_lam_re, "ssm_lam_im": d_lam_im, "ssm_log_dt": d_log_dt,
        "ssm_b_re": d_b_re, "ssm_b_im": d_b_im, "ssm_c_re": _expand_c(dcc_re), "ssm_c_im": _expand_c(dcc_im),
        "ssm_d": d_dskip.reshape(N_GROUPS, GROUP), "ssm_w_glu": g_w_glu,
        "gain_conv_out": d_gc, "gain_ssm_out": d_gs, "w_out": g_w_out, "norm_ffn_g": d_gffn,
        "w_up": g_w_up, "ffn_conv_w": jnp.concatenate([dfw_a, dfw_v], axis=1),
        "ffn_conv_b": jnp.concatenate([dfb_a, dfb_v], axis=1), "w_down": g_w_down, "norm_final_g": d_gfin,
    }
    return loss[0, 0] + started[0, 0], grad_x, grads


def _view(ref, axis, start, size):
    idx = [slice(None)] * len(ref.shape)
    idx[axis] = pl.ds(start, size)
    return ref.at[tuple(idx)]


def _exchange(name, ins, outs, aliases, local_copies, remote_copies):
    ni, no = len(ins), len(outs)
    nl, nr = len(local_copies), len(remote_copies)

    def body(*refs):
        in_refs, out_refs = refs[:ni], refs[ni:ni + no]
        send_sems, recv_sems, local_sems = refs[ni + no:]
        x, y, c = lax.axis_index("x"), lax.axis_index("y"), lax.axis_index("c")
        pos = (x, y, c, 2 * x + y)
        locals_ = [pltpu.make_async_copy(s(in_refs, out_refs, pos), d(in_refs, out_refs, pos), local_sems.at[i])
                   for i, (s, d) in enumerate(local_copies)]
        remotes = []
        for i, (s, d, flip) in enumerate(remote_copies):
            peer = (1 - x if "x" in flip else x, 1 - y if "y" in flip else y, 1 - c if "c" in flip else c)
            remotes.append(pltpu.make_async_remote_copy(
                src_ref=s(in_refs, out_refs, pos), dst_ref=d(in_refs, out_refs, pos),
                send_sem=send_sems.at[i], recv_sem=recv_sems.at[i], device_id=peer, device_id_type=MESH))
        for cp in locals_ + remotes:
            cp.start()
        for cp in remotes:
            cp.wait_recv()
        for cp in remotes:
            cp.wait_send()
        for cp in locals_:
            cp.wait()

    hbm = pl.BlockSpec(memory_space=pl.ANY)
    return pl.pallas_call(
        body, name=name, in_specs=[hbm] * ni, out_specs=[hbm] * no, out_shape=outs,
        input_output_aliases=aliases,
        scratch_shapes=[pltpu.SemaphoreType.DMA((nr,)), pltpu.SemaphoreType.DMA((nr,)),
                        pltpu.SemaphoreType.DMA((max(nl, 1),))],
    )(*ins)


BIG = {"w_in": (0, 1), "ssm_w_glu": (1, 0), "w_out": (1, 0), "w_up": (0, 1), "w_down": (1, 0)}
BIG_NAMES = tuple(BIG)
FLIPS = ("y", "x", "xy")


def _peer_chip(pos, flip):
    x, y, _, _ = pos
    return 2 * (1 - x if "x" in flip else x) + (1 - y if "y" in flip else y)


def _block_rows(rows, cols, itemsize, mult):
    return _pick_tile(rows, max(mult, (2 * 1024 * 1024) // (cols * itemsize)), mult)


def _cast_into_full(w, kc, shard_axis, name):
    r, cdim = w.shape
    tr = _block_rows(r, cdim, 4, 16)
    nb = r // tr

    def body(kc_ref, w_ref, o_ref):
        o_ref[...] = w_ref[...].astype(BF16)

    if shard_axis == 1:
        full, o_spec = (r, 4 * cdim), pl.BlockSpec((tr, cdim), lambda i, kc: (i, kc[0]))
    else:
        full, o_spec = (4 * r, cdim), pl.BlockSpec((tr, cdim), lambda i, kc: (kc[0] * nb + i, 0))
    return pl.pallas_call(
        body, name=name,
        grid_spec=pltpu.PrefetchScalarGridSpec(
            num_scalar_prefetch=1, grid=(nb,), in_specs=[pl.BlockSpec((tr, cdim), lambda i, kc: (i, 0))],
            out_specs=o_spec),
        out_shape=jax.ShapeDtypeStruct(full, BF16), compiler_params=_cparams("parallel"))(kc, w)


def _pair_sum(g, recv, kc, half_axis, name, out_dtype):
    hr, hc = recv.shape
    tr = _block_rows(hr, hc, 4, 16)
    nb = hr // tr

    def body(kc_ref, g_ref, r_ref, o_ref):
        o_ref[...] = (g_ref[...] + r_ref[...]).astype(out_dtype)

    if half_axis == 0:
        g_spec = pl.BlockSpec((tr, hc), lambda i, kc: (kc[1] * nb + i, 0))
    elif half_axis == 1:
        g_spec = pl.BlockSpec((tr, hc), lambda i, kc: (i, kc[1]))
    else:
        g_spec = pl.BlockSpec((tr, hc), lambda i, kc: (i, 0))
    same = pl.BlockSpec((tr, hc), lambda i, kc: (i, 0))
    return pl.pallas_call(
        body, name=name,
        grid_spec=pltpu.PrefetchScalarGridSpec(num_scalar_prefetch=1, grid=(nb,), in_specs=[g_spec, same],
                                               out_specs=same),
        out_shape=jax.ShapeDtypeStruct((hr, hc), out_dtype), compiler_params=_cparams("parallel"))(kc, g, recv)


def _chip_sum(own, recv, kc, own_axis, out_axis, name):
    _, sr, sc = recv.shape
    tr = _block_rows(sr, sc, 4, 16)
    nb = sr // tr

    def body(kc_ref, o_ref, r_ref, t_ref):
        k = kc_ref[0]
        own_v = o_ref[...].astype(F32)
        r = [r_ref[m].astype(F32) for m in range(3)]
        terms = []
        for kk in range(4):
            m = jnp.bitwise_xor(k, kk)
            terms.append(jnp.where(m == 0, own_v, jnp.where(m == 1, r[0], jnp.where(m == 2, r[1], r[2]))))
        t_ref[...] = (terms[0] + terms[1]) + (terms[2] + terms[3])

    if own_axis == 0:
        own_spec = pl.BlockSpec((tr, sc), lambda i, kc: (kc[0] * nb + i, 0))
    elif own_axis == 1:
        own_spec = pl.BlockSpec((tr, sc), lambda i, kc: (i, kc[0]))
    else:
        own_spec = pl.BlockSpec((tr, sc), lambda i, kc: (kc[1] * nb + i, 0))
    if out_axis == 0:
        out_full, out_spec = (2 * sr, sc), pl.BlockSpec((tr, sc), lambda i, kc: (kc[1] * nb + i, 0))
    else:
        out_full, out_spec = (sr, 2 * sc), pl.BlockSpec((tr, sc), lambda i, kc: (i, kc[1]))
    return pl.pallas_call(
        body, name=name,
        grid_spec=pltpu.PrefetchScalarGridSpec(
            num_scalar_prefetch=1, grid=(nb,),
            in_specs=[own_spec, pl.BlockSpec((3, tr, sc), lambda i, kc: (0, i, 0))],
            out_specs=out_spec),
        out_shape=jax.ShapeDtypeStruct(out_full, F32), compiler_params=_cparams("parallel"))(kc, own, recv)


def _adamw(w, g, m, v, name):
    r, cdim = w.shape
    tr = _block_rows(r, cdim, 4, 8)
    c1 = 1.0 - ADAM_B1 ** ADAM_STEP
    c2 = 1.0 - ADAM_B2 ** ADAM_STEP

    def body(w_ref, g_ref, m_ref, v_ref, go_ref, d_ref, nm_ref, nv_ref):
        gv = g_ref[...]
        go_ref[...] = gv
        nm = ADAM_B1 * m_ref[...] + (1.0 - ADAM_B1) * gv
        nv = ADAM_B2 * v_ref[...] + (1.0 - ADAM_B2) * (gv * gv)
        d_ref[...] = -ADAM_LR * ((nm / c1) / (jnp.sqrt(nv / c2) + ADAM_EPS) + ADAM_WD * w_ref[...])
        nm_ref[...] = nm
        nv_ref[...] = nv

    spec = _rows(cdim, tr)
    return pl.pallas_call(body, name=name, grid=(r // tr,), in_specs=[spec] * 4, out_specs=[spec] * 4,
                          out_shape=[jax.ShapeDtypeStruct((r, cdim), F32)] * 4,
                          compiler_params=_cparams("parallel"))(w, g, m, v)


def _adamw_whole(ws, gs, ms, vs, name):
    n = len(ws)
    c1 = 1.0 - ADAM_B1 ** ADAM_STEP
    c2 = 1.0 - ADAM_B2 ** ADAM_STEP

    def body(*refs):
        for i in range(n):
            w_ref, g_ref, m_ref, v_ref, d_ref, nm_ref, nv_ref = [refs[j * n + i] for j in range(7)]
            gv = g_ref[...]
            nm = ADAM_B1 * m_ref[...] + (1.0 - ADAM_B1) * gv
            nv = ADAM_B2 * v_ref[...] + (1.0 - ADAM_B2) * (gv * gv)
            d_ref[...] = -ADAM_LR * ((nm / c1) / (jnp.sqrt(nv / c2) + ADAM_EPS) + ADAM_WD * w_ref[...])
            nm_ref[...] = nm
            nv_ref[...] = nv

    vmem = pl.BlockSpec(memory_space=pltpu.VMEM)
    out = pl.pallas_call(body, name=name, in_specs=[vmem] * (4 * n), out_specs=[vmem] * (3 * n),
                         out_shape=[jax.ShapeDtypeStruct(a.shape, F32) for a in ws] * 3,
                         compiler_params=pltpu.CompilerParams(vmem_limit_bytes=VMEM_LIMIT))(*ws, *gs, *ms, *vs)
    return out[:n], out[n:2 * n], out[2 * n:]


SIDE_EFFECT = pltpu.SideEffectType.DATAFLOW_SIDE_EFFECTING


def _descriptors(copies, refs, send_sems, recv_sems, sem_off=0):
    x, y, c = lax.axis_index("x"), lax.axis_index("y"), lax.axis_index("c")
    pos = (x, y, c, 2 * x + y)
    out = []
    for i, (s, d, flip) in enumerate(copies):
        peer = (1 - x if "x" in flip else x, 1 - y if "y" in flip else y, 1 - c if "c" in flip else c)
        out.append(pltpu.make_async_remote_copy(
            src_ref=s(refs, refs, pos), dst_ref=d(refs, refs, pos),
            send_sem=send_sems.at[sem_off + i], recv_sem=recv_sems.at[sem_off + i],
            device_id=peer, device_id_type=MESH))
    return out


def _shifted(copies, off):
    return [(lambda I, O, pos, s=s: s(I[off:], O[off:], pos), lambda I, O, pos, d=d: d(I[off:], O[off:], pos), flip)
            for s, d, flip in copies]


BARRIER_IDS = {"c": (1, 2), "ici": (3, 4)}


def _exchange_start(name, bufs, copies, turns, after=None):
    n, nr = len(bufs), len(copies)
    na = 0 if after is None else 1
    flips = sorted({flip for _, _, flip in copies})
    kind = "c" if flips == ["c"] else "ici"
    collective_id = BARRIER_IDS[kind][turns[kind] % 2]
    turns[kind] += 1

    def body(*refs):
        x, y, c = lax.axis_index("x"), lax.axis_index("y"), lax.axis_index("c")
        barrier = pltpu.get_barrier_semaphore()
        for flip in flips:
            peer = (1 - x if "x" in flip else x, 1 - y if "y" in flip else y, 1 - c if "c" in flip else c)
            pl.semaphore_signal(barrier, inc=1, device_id=peer, device_id_type=MESH)
        pl.semaphore_wait(barrier, len(flips))
        for cp in _descriptors(copies, refs[:n], refs[n + na], refs[n + na + 1]):
            cp.start()
        token = refs[2 * n + na + 2]
        token[...] = jnp.zeros_like(token)

    hbm = pl.BlockSpec(memory_space=pltpu.HBM)
    sem = pl.BlockSpec(memory_space=pltpu.SEMAPHORE)
    out = pl.pallas_call(
        body, name=name,
        in_specs=[hbm] * n + [pl.BlockSpec(memory_space=pl.ANY)] * na,
        out_specs=(sem, sem, *[hbm] * n, pl.BlockSpec(memory_space=pltpu.VMEM)),
        out_shape=(pltpu.SemaphoreType.DMA((nr,)), pltpu.SemaphoreType.DMA((nr,)),
                   *[pltpu.HBM(b.shape, b.dtype) for b in bufs], jax.ShapeDtypeStruct((SUBLANES, LANES), F32)),
        input_output_aliases={i: 2 + i for i in range(n)},
        compiler_params=pltpu.CompilerParams(has_side_effects=SIDE_EFFECT, collective_id=collective_id),
    )(*[pltpu.with_memory_space_constraint(b, pltpu.HBM) for b in bufs], *([after] * na))
    return out[0], out[1], list(out[2:2 + n]), out[2 + n]


def _exchange_wait(name, send_sems, recv_sems, bufs, copies, after, sem_off=0):
    n = len(bufs)

    def body(*refs):
        for cp in _descriptors(copies, refs[:n], refs[n], refs[n + 1], sem_off):
            cp.wait_send()
            cp.wait_recv()

    hbm = pl.BlockSpec(memory_space=pltpu.HBM)
    sem = pl.BlockSpec(memory_space=pltpu.SEMAPHORE)
    out = pl.pallas_call(
        body, name=name,
        in_specs=[hbm] * n + [sem, sem, pl.BlockSpec(memory_space=pl.ANY)],
        out_specs=tuple([hbm] * n),
        out_shape=tuple(pltpu.HBM(b.shape, b.dtype) for b in bufs),
        input_output_aliases={i: i for i in range(n)},
        compiler_params=pltpu.CompilerParams(has_side_effects=SIDE_EFFECT),
    )(*bufs, send_sems, recv_sems, after)
    return list(out)


FIRST = ("w_in",)
MID = ("ssm_w_glu", "w_out")
LATE = ("w_up", "w_down")
GROUPS = {"first": FIRST, "mid": MID, "late": LATE}
ARRIVALS = {"first": FIRST, "mid": MID, "up": ("w_up",), "down": ("w_down",)}


def _gather_copies(names, shard_shapes):
    def region(i, chip, c):
        half_axis, shard_axis = BIG[names[i]]
        ssize = shard_shapes[i][shard_axis]
        hsize = shard_shapes[i][half_axis] // 2
        return lambda ref: _view(_view(ref, shard_axis, chip * ssize, ssize), half_axis, c * hsize, hsize)

    ici, d2d = [], []
    for i in range(len(names)):
        for flip in FLIPS:
            ici.append((lambda I, O, pos, i=i: region(i, pos[3], pos[2])(I[i]),
                        lambda I, O, pos, i=i: region(i, pos[3], pos[2])(O[i]), flip))
            d2d.append((lambda I, O, pos, i=i, flip=flip: region(i, _peer_chip(pos, flip), pos[2])(I[i]),
                        lambda I, O, pos, i=i, flip=flip: region(i, _peer_chip(pos, flip), pos[2])(O[i]), "c"))
    return ici, d2d


def _half_shape(n, shape):
    r, cdim = shape
    return (r // 2, cdim) if BIG[n][0] == 0 else (r, cdim // 2)


def _sub_shape(n, shape):
    hr, hc = _half_shape(n, shape)
    return (hr, hc // 4) if BIG[n][1] == 1 else (hr // 4, hc)


def _pair_copies(names, shapes, with_pack, dst_off):
    n = len(names)

    def other_half(i, ref, pos):
        half_axis = BIG[names[i]][0]
        hsize = shapes[i][half_axis] // 2
        return _view(ref, half_axis, (1 - pos[2]) * hsize, hsize)

    copies = [(lambda I, O, pos, i=i: other_half(i, I[i], pos), lambda I, O, pos, i=i: O[dst_off + i], "c")
              for i in range(n)]
    if with_pack:
        copies.append((lambda I, O, pos: I[n], lambda I, O, pos: O[dst_off + n], "c"))
    return copies


def _chip_copies(names, shapes, pack_rows, dst_off):
    n = len(names)

    def piece(i, ref, chip):
        shard_axis = BIG[names[i]][1]
        ssize = _sub_shape(names[i], shapes[i])[shard_axis]
        return _view(ref, shard_axis, chip * ssize, ssize)

    copies = []
    for i in range(n):
        for slot, flip in enumerate(FLIPS):
            copies.append((lambda I, O, pos, i=i, flip=flip: piece(i, I[i], _peer_chip(pos, flip)),
                           lambda I, O, pos, i=i, slot=slot: O[dst_off + i].at[slot], flip))
    if pack_rows:
        for slot, flip in enumerate(FLIPS):
            copies.append((lambda I, O, pos: _view(I[n], 0, pos[2] * (pack_rows // 2), pack_rows // 2),
                           lambda I, O, pos, slot=slot: O[dst_off + n].at[slot], flip))
    return copies


class _Exchanges:
    def __init__(self, shards, tiny, kc):
        self.kc = kc
        wb = {n: _cast_into_full(shards[n], kc, BIG[n][1], "cast_" + n) for n in BIG_NAMES}
        self.gathering, self.forwarding, self.pairing, self.reducing = {}, {}, {}, {}
        self.turns = {"c": 0, "ici": 0}
        tiny_copies = [(lambda I, O, pos: I[0], lambda I, O, pos: O[1].at[pos[3]], flip) for flip in FLIPS]
        self.gathering["tiny"] = (0, 0, 2, tiny_copies, None)
        bufs, copies = [tiny, lax.empty((4,) + tiny.shape, F32)], list(tiny_copies)
        for group, names in ARRIVALS.items():
            ici, d2d = _gather_copies(names, [shards[n].shape for n in names])
            self.gathering[group] = (len(bufs), len(copies), len(names), ici, d2d)
            copies += _shifted(ici, len(bufs))
            bufs += [wb[n] for n in names]
        self.started = _exchange_start("gather_start", bufs, copies, self.turns)
        self.zero = self.started[3][0, 0]

    def _arrived(self, group, after):
        buf_off, sem_off, n, ici, _ = self.gathering[group]
        send_sems, recv_sems, bufs, _ = self.started
        return _exchange_wait("gather_%s_wait" % group, send_sems, recv_sems, bufs[buf_off:buf_off + n], ici, after,
                              sem_off)

    def small_params(self, kc):
        tiny, got = self._arrived("tiny", self.started[3])
        return lax.dynamic_update_index_in_dim(got, tiny, kc[0], 0)

    def forward(self, group, after):
        d2d = self.gathering[group][4]
        self.forwarding[group] = (_exchange_start("forward_%s_start" % group, self._arrived(group, after), d2d,
                                                  self.turns), d2d)
        return self.forwarding[group][0][3]

    def weights(self, group, after):
        if group not in self.forwarding:
            after = self.forward(group, after)
        (send_sems, recv_sems, bufs, _), d2d = self.forwarding[group]
        full = _exchange_wait("forward_%s_wait" % group, send_sems, recv_sems, bufs, d2d, after)
        return dict(zip(ARRIVALS[group], full))

    def grads_ready(self, group, grads):
        names = GROUPS[group]
        gs = [grads[n] for n in names]
        land = [lax.empty(_half_shape(n, g.shape), F32) for n, g in zip(names, gs)]
        copies = _pair_copies(names, [g.shape for g in gs], False, len(names))
        started = _exchange_start("pair_%s_start" % group, gs + land, copies, self.turns)
        self.pairing[group] = (started, copies)
        return started[3]

    def grads_send(self, group, after):
        names = GROUPS[group]
        n = len(names)
        (send_sems, recv_sems, bufs, _), copies = self.pairing[group]
        bufs = _exchange_wait("pair_%s_wait" % group, send_sems, recv_sems, bufs, copies, after)
        chip = [_pair_sum(bufs[i], bufs[n + i], self.kc, BIG[names[i]][0], "pair_sum_" + names[i], BF16)
                for i in range(n)]
        shapes = [bufs[i].shape for i in range(n)]
        land = [lax.empty((3,) + _sub_shape(names[i], shapes[i]), BF16) for i in range(n)]
        copies = _chip_copies(names, shapes, 0, n)
        started = _exchange_start("reduce_%s_start" % group, chip + land, copies, self.turns)
        self.reducing[group] = (started, copies)
        return started[3]

    def finish_pack(self, pack):
        kc = self.kc
        prow = pack.shape[0] // 2
        recv = _exchange("reduce_d2d", [pack], [jax.ShapeDtypeStruct(pack.shape, F32)], {}, [],
                         _pair_copies((), [], True, 0))
        chip_pack = _pair_sum(pack, recv[0], kc, None, "pair_sum_pack", F32)
        copies = _chip_copies((), [], pack.shape[0], 1)
        land = lax.empty((3, prow, pack.shape[1]), F32)
        pack_sems_s, pack_sems_r, pack_bufs, after = _exchange_start("reduce_pack_start", [chip_pack, land], copies,
                                                                     self.turns)

        names, chips, recvs = (), [], []
        for group, group_names in GROUPS.items():
            (send_sems, recv_sems, bufs, _), group_copies = self.reducing[group]
            bufs = _exchange_wait("reduce_%s_wait" % group, send_sems, recv_sems, bufs, group_copies, after)
            n = len(group_names)
            names, chips, recvs = names + group_names, chips + bufs[:n], recvs + bufs[n:]
            after = bufs[n]
        total = [_chip_sum(chips[i], recvs[i], kc, BIG[n][1], BIG[n][0], "chip_sum_" + n)
                 for i, n in enumerate(names)]

        def my_half(half_axis, ref, pos):
            hsize = ref.shape[half_axis] // 2
            return _view(ref, half_axis, pos[2] * hsize, hsize)

        swap = [(lambda I, O, pos, i=i, n=n: my_half(BIG[n][0], I[i], pos),
                 lambda I, O, pos, i=i, n=n: my_half(BIG[n][0], O[i], pos), "c") for i, n in enumerate(names)]
        self.swapping = (_exchange_start("swap_start", total, swap, self.turns), swap, names)

        chip_pack, recv_pack = _exchange_wait("reduce_pack_wait", pack_sems_s, pack_sems_r, pack_bufs, copies,
                                              self.swapping[0][3])
        total_pack = _chip_sum(chip_pack, recv_pack, kc, None, 0, "chip_sum_pack")
        swap = [(lambda I, O, pos: my_half(0, I[0], pos), lambda I, O, pos: my_half(0, O[0], pos), "c")]
        return _exchange("swap_pack", [total_pack], [jax.ShapeDtypeStruct(pack.shape, F32)], {0: 0}, [], swap)[0]

    def finish_big(self, after):
        (send_sems, recv_sems, bufs, _), swap, names = self.swapping
        return dict(zip(names, _exchange_wait("swap_wait", send_sems, recv_sems, bufs, swap, after)))


WEIGHTS = ("meta_tokens", "norm_mix_g", "w_in", "conv_w", "ssm_lam_re", "ssm_lam_im", "ssm_log_dt", "ssm_b_re",
           "ssm_b_im", "ssm_c_re", "ssm_c_im", "ssm_d", "ssm_w_glu", "gain_conv_out", "gain_ssm_out", "w_out",
           "norm_ffn_g", "w_up", "ffn_conv_w", "ffn_conv_b", "w_down", "norm_final_g")
TINY_SHARDED = ("meta_tokens", "conv_w", "ffn_conv_w")
REPLICATED = tuple(n for n in WEIGHTS if n not in BIG and n not in TINY_SHARDED)
PACK_COLS = 512


def _pack(arrays, row_mult, cols):
    flat = jnp.concatenate([a.reshape(-1).astype(F32) for a in arrays])
    n = flat.shape[0]
    total = -(-n // (row_mult * cols)) * (row_mult * cols)
    return jnp.concatenate([flat, jnp.zeros((total - n,), F32)]).reshape(total // cols, cols)


def _unpack(packed, shapes):
    flat = packed.reshape(-1)
    out, off = [], 0
    for s in shapes:
        n = math.prod(s)
        out.append(flat[off:off + n].reshape(s))
        off += n
    return out


def kernel(x, meta_tokens, norm_mix_g, w_in, conv_w, ssm_lam_re, ssm_lam_im, ssm_log_dt, ssm_b_re, ssm_b_im, ssm_c_re, ssm_c_im, ssm_d, ssm_w_glu, gain_conv_out, gain_ssm_out, w_out, norm_ffn_g, w_up, ffn_conv_w, ffn_conv_b, w_down, norm_final_g, loss_target, m_meta_tokens, m_norm_mix_g, m_w_in, m_conv_w, m_ssm_lam_re, m_ssm_lam_im, m_ssm_log_dt, m_ssm_b_re, m_ssm_b_im, m_ssm_c_re, m_ssm_c_im, m_ssm_d, m_ssm_w_glu, m_gain_conv_out, m_gain_ssm_out, m_w_out, m_norm_ffn_g, m_w_up, m_ffn_conv_w, m_ffn_conv_b, m_w_down, m_norm_final_g, v_meta_tokens, v_norm_mix_g, v_w_in, v_conv_w, v_ssm_lam_re, v_ssm_lam_im, v_ssm_log_dt, v_ssm_b_re, v_ssm_b_im, v_ssm_c_re, v_ssm_c_im, v_ssm_d, v_ssm_w_glu, v_gain_conv_out, v_gain_ssm_out, v_w_out, v_norm_ffn_g, v_w_up, v_ffn_conv_w, v_ffn_conv_b, v_w_down, v_norm_final_g):
    args = dict(locals())
    w = {n: args[n] for n in WEIGHTS}
    mom = {n: args["m_" + n] for n in WEIGHTS}
    var = {n: args["v_" + n] for n in WEIGHTS}
    kx, ky, kc_ = lax.axis_index("x"), lax.axis_index("y"), lax.axis_index("c")
    chip = 2 * kx + ky
    kc = jnp.stack([chip, kc_]).astype(jnp.int32)

    def squeeze(n, a):
        if n == "meta_tokens":
            return a
        if n == "norm_final_g":
            return a.reshape(1, -1)
        a = a[0]
        return a.reshape(1, -1) if a.ndim == 1 else a

    wl = {n: squeeze(n, w[n]) for n in WEIGHTS}
    ml = {n: squeeze(n, mom[n]) for n in WEIGHTS}
    vl = {n: squeeze(n, var[n]) for n in WEIGHTS}

    tiny = _pack([wl[n] for n in TINY_SHARDED], SUBLANES, LANES)
    ex = _Exchanges({n: wl[n] for n in BIG_NAMES}, tiny, kc)
    tiny_shapes = [wl[n].shape for n in TINY_SHARDED]
    tiny_all = ex.small_params(kc)
    tiny_parts = [_unpack(tiny_all[k], tiny_shapes) for k in range(4)]
    p = {n: wl[n] for n in WEIGHTS if n not in BIG}
    for j, n in enumerate(TINY_SHARDED):
        p[n] = jnp.concatenate([tiny_parts[k][j] for k in range(4)], axis=1)
    p["ssm_log_dt"] = wl["ssm_log_dt"].reshape(-1)

    loss_local, grad_x, grads = _local_step(x[0], loss_target[0], p, ex)

    small_names = REPLICATED + TINY_SHARDED
    small_shapes = [tuple(grads[n].shape) for n in small_names] + [(1,)]
    pack = _pack([grads[n] for n in small_names] + [loss_local.reshape(1)], 2 * 16, PACK_COLS)
    g_pack = ex.finish_pack(pack)
    g_small = dict(zip(small_names + ("loss",), _unpack(g_pack, small_shapes)))
    loss = g_small["loss"][0]
    swapped = ("ssm_b_re", "ssm_b_im")

    def view(n, a):
        if n in swapped:
            return jnp.swapaxes(a, -1, -2)
        return a.reshape(1, -1) if a.ndim == 1 else a

    g = {}
    for n in REPLICATED:
        g[n] = g_small[n].reshape(view(n, w[n]).shape)
    for n in TINY_SHARDED:
        cols = wl[n].shape[1]
        g[n] = lax.dynamic_slice_in_dim(g_small[n], chip * cols, cols, axis=1).reshape(w[n].shape)
    delta, new_m, new_v = {}, {}, {}
    small = [[view(n, d[n]) for n in small_names] for d in (w, mom, var)]
    small.insert(1, [g[n] for n in small_names])
    for d, outs in zip((delta, new_m, new_v), _adamw_whole(*small, "adamw_small")):
        d.update(zip(small_names, outs))
    for d in (g, delta, new_m, new_v):
        d.update({n: jnp.swapaxes(d[n], -1, -2) for n in swapped})
    g_big = ex.finish_big(delta[small_names[0]])
    for n in BIG_NAMES:
        g[n], delta[n], new_m[n], new_v[n] = _adamw(wl[n], g_big[n], ml[n], vl[n], "adamw_" + n)

    def like(n, a):
        return a.reshape(w[n].shape)

    return (loss, grad_x[None], *[like(n, g[n]) for n in WEIGHTS], *[like(n, delta[n]) for n in WEIGHTS],
            *[like(n, new_m[n]) for n in WEIGHTS], *[like(n, new_v[n]) for n in WEIGHTS])
```

```python
import functools
import math

import jax
import jax.numpy as jnp
from jax import lax
from jax.experimental import pallas as pl
from jax.experimental.pallas import tpu as pltpu

F32 = jnp.float32
BF16 = jnp.bfloat16
MESH = pl.DeviceIdType.MESH

N_META = 16
N_GROUPS = 32
GROUP = 16
STATE = 64
RMS_EPS = 1e-6
ADAM_LR = 0.001
ADAM_B1 = 0.9
ADAM_B2 = 0.999
ADAM_EPS = 1e-08
ADAM_WD = 0.01
ADAM_STEP = 10

LANES = 128
SUBLANES = 8
ROW_ALIGN = 128
ROW_TILES = 4
VMEM_LIMIT = 52 * 1024 * 1024
MM_VMEM_BUDGET = 40 * 1024 * 1024
GELU_C = math.sqrt(2.0 / math.pi)
GELU_A = 0.044715


def _cparams(*sem):
    return pltpu.CompilerParams(dimension_semantics=sem, vmem_limit_bytes=VMEM_LIMIT)


def _pick_tile(dim, cap, mult):
    best = None
    for t in range(mult, min(dim, cap) + 1, mult):
        if dim % t == 0:
            best = t
    return best if best is not None else dim


def _mm(a, b, mode, name, out_dtype=F32, acc_in=None, after=None):
    if mode == "tn":
        kdim, m = a.shape
    else:
        m, kdim = a.shape
    n = b.shape[0] if mode == "nt" else b.shape[1]
    tm = _pick_tile(m, 1408, LANES if mode == "tn" else 16)
    tk = _pick_tile(kdim, 2816, LANES)
    nk = kdim // tk
    out_bytes = jnp.dtype(out_dtype).itemsize
    for cap in (704, 512, 256, LANES) if m == tm else (1408, 1024, 512, 256, LANES):
        tn = _pick_tile(n, cap, LANES)
        blocks = 2 * (tm * tk * 2 + tk * tn * 2 + tm * tn * out_bytes * (2 if acc_in is not None else 1))
        if blocks + (tm * tn * 4 if nk > 1 else 0) <= MM_VMEM_BUDGET:
            break
    has_acc = acc_in is not None

    def body(*refs):
        if after is not None:
            refs = refs[1:]
        if has_acc:
            a_ref, b_ref, c_ref, o_ref = refs[:4]
            rest = refs[4:]
        else:
            a_ref, b_ref, o_ref = refs[:3]
            c_ref = None
            rest = refs[3:]
        if mode == "nn":
            p = jnp.dot(a_ref[...], b_ref[...], preferred_element_type=F32)
        elif mode == "nt":
            p = lax.dot_general(a_ref[...], b_ref[...], (((1,), (1,)), ((), ())), preferred_element_type=F32)
        else:
            p = lax.dot_general(a_ref[...], b_ref[...], (((0,), (0,)), ((), ())), preferred_element_type=F32)
        if nk == 1:
            if has_acc:
                p = p + c_ref[...]
            o_ref[...] = p.astype(out_dtype)
        else:
            acc_ref = rest[0]
            k = pl.program_id(2)

            @pl.when(k == 0)
            def _():
                acc_ref[...] = p + c_ref[...] if has_acc else p

            @pl.when(k > 0)
            def _():
                acc_ref[...] += p

            @pl.when(k == nk - 1)
            def _():
                o_ref[...] = acc_ref[...].astype(out_dtype)

    if mode == "tn":
        a_spec = pl.BlockSpec((tk, tm), lambda i, j, k: (k, i))
    else:
        a_spec = pl.BlockSpec((tm, tk), lambda i, j, k: (i, k))
    if mode == "nt":
        b_spec = pl.BlockSpec((tn, tk), lambda i, j, k: (j, k))
    else:
        b_spec = pl.BlockSpec((tk, tn), lambda i, j, k: (k, j))
    o_spec = pl.BlockSpec((tm, tn), lambda i, j, k: (i, j))
    in_specs = [a_spec, b_spec] + ([o_spec] if has_acc else [])
    args = (a, b) + ((acc_in,) if has_acc else ())
    if after is not None:
        in_specs = [pl.BlockSpec(memory_space=pl.ANY)] + in_specs
        args = (after,) + args
    return pl.pallas_call(
        body, name=name, grid=(m // tm, n // tn, nk),
        in_specs=in_specs, out_specs=o_spec,
        out_shape=jax.ShapeDtypeStruct((m, n), out_dtype),
        scratch_shapes=[pltpu.VMEM((tm, tn), F32)] if nk > 1 else [],
        compiler_params=_cparams("parallel", "parallel", "arbitrary"),
    )(*args)


def _mm_rows(a, b, mode, name, ins, outs, epilogue, scratch=()):
    m, kdim = a.shape
    n = b.shape[0] if mode == "nt" else b.shape[1]
    tm = m // ROW_TILES
    tk = _pick_tile(kdim, 2816, LANES)
    nk = kdim // tk
    ni, no = len(ins), len(outs)

    def body(*refs):
        a_ref, b_ref = refs[:2]
        in_refs, out_refs, rest = refs[2:2 + ni], refs[2 + ni:2 + ni + no], refs[2 + ni + no:]
        k, i = pl.program_id(0), pl.program_id(1)
        if mode == "nn":
            p = jnp.dot(a_ref[...], b_ref[...], preferred_element_type=F32)
        else:
            p = lax.dot_general(a_ref[...], b_ref[...], (((1,), (1,)), ((), ())), preferred_element_type=F32)
        if nk == 1:
            epilogue(p, i, in_refs, out_refs, rest)
        else:
            acc_ref = rest[0]
            rows = pl.ds(pl.multiple_of(i * tm, SUBLANES), tm)

            @pl.when(k == 0)
            def _():
                acc_ref[rows, :] = p

            @pl.when(jnp.logical_and(k > 0, k < nk - 1))
            def _():
                acc_ref[rows, :] += p

            @pl.when(k == nk - 1)
            def _():
                epilogue(acc_ref[rows, :] + p, i, in_refs, out_refs, rest[1:])

    tile = (lambda k, i: i) if nk == 1 else (lambda k, i: jnp.where(k == nk - 1, i, 0))

    def spec(shape, kind):
        if kind == "rows":
            return pl.BlockSpec((tm,) + tuple(shape[1:]), lambda k, i: (tile(k, i),) + (0,) * (len(shape) - 1))
        if kind == "whole":
            return pl.BlockSpec(tuple(shape), lambda k, i: (0,) * len(shape))
        return pl.BlockSpec(memory_space=pl.ANY)

    a_spec = pl.BlockSpec((tm, tk), lambda k, i: (i, k))
    b_spec = pl.BlockSpec((n, tk), lambda k, i: (0, k)) if mode == "nt" else pl.BlockSpec((tk, n), lambda k, i: (k, 0))
    return pl.pallas_call(
        body, name=name, grid=(nk, ROW_TILES),
        in_specs=[a_spec, b_spec] + [spec(x.shape, kind) for x, kind in ins],
        out_specs=[spec(shape, kind) for shape, _, kind in outs],
        out_shape=[jax.ShapeDtypeStruct(shape, dtype) for shape, dtype, _ in outs],
        scratch_shapes=([pltpu.VMEM((m, n), F32)] if nk > 1 else []) + list(scratch),
        compiler_params=_cparams("arbitrary", "arbitrary"),
    )(a, b, *[x for x, _ in ins])


def _rows(shape_cols, tr, dtype=None):
    return pl.BlockSpec((tr, shape_cols), lambda i: (i, 0))


def _const(shape):
    return pl.BlockSpec(shape, lambda i: (0,) * len(shape))


def _rms(x):
    return lax.rsqrt(jnp.mean(x * x, axis=-1, keepdims=True) + RMS_EPS)


def _rms_bwd(x, r, g, dy):
    xn = x * r
    dxn = dy * g
    dx = r * (dxn - xn * jnp.mean(dxn * xn, axis=-1, keepdims=True))
    return dx, dy * xn


def _gelu(y):
    return 0.5 * y * (1.0 + jnp.tanh(GELU_C * (y + GELU_A * y * y * y)))


def _gelu_grad(y):
    t = jnp.tanh(GELU_C * (y + GELU_A * y * y * y))
    return 0.5 * (1.0 + t) + 0.5 * y * (1.0 - t * t) * GELU_C * (1.0 + 3.0 * GELU_A * y * y)


def _sigmoid(z):
    return 1.0 / (1.0 + jnp.exp(-z))


def _proj_res_norm(a, w, h, g, after, name):
    def epilogue(p, i, ins, outs, _):
        x = ins[0][...] + p
        outs[0][...] = x
        outs[1][...] = (x * _rms(x) * ins[1][...]).astype(BF16)

    return _mm_rows(a, w, "nn", name, [(h, "rows"), (g, "whole"), (after, "hbm")],
                    [(h.shape, F32, "rows"), (h.shape, BF16, "rows")], epilogue)


def _proj_norm_bwd(da, w, h, g, dres, after, name):
    d = h.shape[1]

    def epilogue(p, i, ins, outs, _):
        x = ins[0][...]
        dx, dgs = _rms_bwd(x, _rms(x), ins[1][...], p)
        dh = ins[2][...] + dx
        outs[0][...] = dh
        outs[1][...] = dh.astype(BF16)

        @pl.when(i == 0)
        def _():
            outs[2][...] = jnp.zeros_like(outs[2])

        outs[2][...] += jnp.sum(dgs, axis=0, keepdims=True)

    return _mm_rows(da, w, "nt", name, [(h, "rows"), (g, "whole"), (dres, "rows"), (after, "hbm")],
                    [(h.shape, F32, "rows"), (h.shape, BF16, "rows"), ((1, d), F32, "whole")], epilogue)


def _proj_input_norm_bwd(da, w, h, g, dres, after, n_real, name):
    tp, d = h.shape
    tr = tp // ROW_TILES

    def epilogue(p, i, ins, outs, scratch):
        h_ref, g_ref, dres_ref, _ = ins
        dx_ref, dmeta_ref, dg_ref = outs
        stage, sem = scratch
        x = h_ref[...]
        dx, dgs = _rms_bwd(x, _rms(x), g_ref[...], p)
        stage[...] = dres_ref[...] + dx

        @pl.when(i == 0)
        def _():
            dg_ref[...] = jnp.zeros_like(dg_ref)
            dmeta_ref[...] = stage[:N_META, :]

        dg_ref[...] += jnp.sum(dgs, axis=0, keepdims=True)
        for t in range(ROW_TILES):
            lo, hi = max(t * tr, N_META), min((t + 1) * tr, n_real)
            if hi > lo:
                @pl.when(i == t)
                def _(t=t, lo=lo, hi=hi):
                    cp = pltpu.make_async_copy(stage.at[pl.ds(lo - t * tr, hi - lo), :],
                                               dx_ref.at[pl.ds(lo - N_META, hi - lo), :], sem)
                    cp.start()
                    cp.wait()

    return _mm_rows(da, w, "nt", name, [(h, "rows"), (g, "whole"), (dres, "rows"), (after, "hbm")],
                    [((n_real - N_META, d), F32, "hbm"), ((N_META, d), F32, "whole"), ((1, d), F32, "whole")],
                    epilogue, scratch=[pltpu.VMEM((tr, d), F32), pltpu.SemaphoreType.DMA])


def _load_token_rows(tok_hbm, buf, sem, tr, n_real, head=None, wait=False, i=None):
    i = pl.program_id(0) if i is None else i
    for t in range(ROW_TILES):
        base = t * tr
        lo, hi = max(base, N_META), min(base + tr, n_real)

        @pl.when(i == t)
        def _(base=base, lo=lo, hi=hi):
            if hi > lo:
                cp = pltpu.make_async_copy(tok_hbm.at[pl.ds(lo - N_META, hi - lo), :],
                                           buf.at[pl.ds(lo - base, hi - lo), :], sem)
                if wait:
                    cp.wait()
                    return
                cp.start()
            if wait:
                return
            if base < N_META:
                buf[0:N_META - base, :] = (jnp.zeros((N_META - base, buf.shape[1]), F32) if head is None
                                           else head[base:N_META, :])
            if hi < base + tr:
                buf[max(hi, base) - base:tr, :] = jnp.zeros((base + tr - max(hi, base), buf.shape[1]), F32)


def _input_norm_fwd(x, meta, g, tp, name):
    seq, d = x.shape
    tr = tp // ROW_TILES
    n_real = N_META + seq

    def body(x_hbm, meta_ref, g_ref, h_ref, hn_ref, buf, sem):
        _load_token_rows(x_hbm, buf, sem, tr, n_real, head=meta_ref)
        _load_token_rows(x_hbm, buf, sem, tr, n_real, wait=True)
        h = buf[...]
        h_ref[...] = h
        hn_ref[...] = (h * _rms(h) * g_ref[...]).astype(BF16)

    return pl.pallas_call(
        body, name=name, grid=(ROW_TILES,),
        in_specs=[pl.BlockSpec(memory_space=pl.ANY), _const((N_META, d)), _const((1, d))],
        out_specs=[_rows(d, tr), _rows(d, tr)],
        out_shape=[jax.ShapeDtypeStruct((tp, d), F32), jax.ShapeDtypeStruct((tp, d), BF16)],
        scratch_shapes=[pltpu.VMEM((tr, d), F32), pltpu.SemaphoreType.DMA],
        compiler_params=_cparams("arbitrary"))(x, meta, g)


def _proj_loss_bwd(act, w, h1, target, g, n_real, name):
    tp, d = h1.shape
    tr = tp // ROW_TILES

    def epilogue(p, i, ins, outs, scratch):
        h1_ref, t_hbm, g_ref = ins
        loss_ref, dh_ref, dhb_ref, dg_ref = outs
        t_buf, sem = scratch
        _load_token_rows(t_hbm, t_buf, sem, tr, n_real, i=i)
        x = h1_ref[...] + p
        r = _rms(x)
        row = i * tr + lax.broadcasted_iota(jnp.int32, (tr, d), 0)
        valid = (row >= N_META) & (row < n_real)
        _load_token_rows(t_hbm, t_buf, sem, tr, n_real, wait=True, i=i)
        e = jnp.where(valid, x * r * g_ref[...] - t_buf[...], 0.0)
        dx, dgs = _rms_bwd(x, r, g_ref[...], e * (1.0 / d))
        dh_ref[...] = dx
        dhb_ref[...] = dx.astype(BF16)

        @pl.when(i == 0)
        def _():
            dg_ref[...] = jnp.zeros_like(dg_ref)
            loss_ref[...] = jnp.zeros_like(loss_ref)

        dg_ref[...] += jnp.sum(dgs, axis=0, keepdims=True)
        loss_ref[...] += (0.5 / d) * jnp.sum(jnp.sum(e * e, axis=0, keepdims=True), axis=1, keepdims=True)

    return _mm_rows(act, w, "nn", name, [(h1, "rows"), (target, "hbm"), (g, "whole")],
                    [((1, LANES), F32, "whole"), ((tp, d), F32, "rows"), ((tp, d), BF16, "rows"),
                     ((1, d), F32, "whole")],
                    epilogue, scratch=[pltpu.VMEM((tr, d), F32), pltpu.SemaphoreType.DMA])


def _mix_fwd(co, y, z, gc, gs, name):
    tp, dh = co.shape
    tr = tp // ROW_TILES

    def body(co_ref, y_ref, z_ref, gc_ref, gs_ref, m_ref):
        c = co_ref[...]
        m_ref[:, :dh] = (c * _rms(c) * gc_ref[...]).astype(BF16)
        so = _gelu(y_ref[...]) * _sigmoid(z_ref[...])
        m_ref[:, dh:] = (so * _rms(so) * gs_ref[...]).astype(BF16)

    return pl.pallas_call(
        body, name=name, grid=(ROW_TILES,),
        in_specs=[_rows(dh, tr)] * 3 + [_const((1, dh))] * 2,
        out_specs=_rows(2 * dh, tr),
        out_shape=jax.ShapeDtypeStruct((tp, 2 * dh), BF16),
        compiler_params=_cparams("parallel"))(co, y, z, gc, gs)


def _proj_mix_bwd(dh1b, w, co, y, z, gc, gs, name):
    tp, dh = co.shape

    def epilogue(p, i, ins, outs, _):
        co_ref, y_ref, z_ref, gc_ref, gs_ref = ins
        dco_ref, dz_ref, dgp_ref, dgc_ref, dgs_ref = outs
        c = co_ref[...]
        dco, dgc = _rms_bwd(c, _rms(c), gc_ref[...], p[:, :dh])
        dco_ref[...] = dco
        gl = _gelu(y_ref[...])
        sg = _sigmoid(z_ref[...])
        so = gl * sg
        dso, dgs = _rms_bwd(so, _rms(so), gs_ref[...], p[:, dh:])
        dz_ref[...] = (dso * gl * sg * (1.0 - sg)).astype(BF16)
        dgp_ref[...] = dso * sg

        @pl.when(i == 0)
        def _():
            dgc_ref[...] = jnp.zeros_like(dgc_ref)
            dgs_ref[...] = jnp.zeros_like(dgs_ref)

        dgc_ref[...] += jnp.sum(dgc, axis=0, keepdims=True)
        dgs_ref[...] += jnp.sum(dgs, axis=0, keepdims=True)

    return _mm_rows(dh1b, w, "nt", name,
                    [(co, "rows"), (y, "rows"), (z, "rows"), (gc, "whole"), (gs, "whole")],
                    [((tp, dh), F32, "rows"), ((tp, dh), BF16, "rows"), ((tp, dh), F32, "rows"),
                     ((1, dh), F32, "whole"), ((1, dh), F32, "whole")], epilogue)


def _shift_down(x, k):
    row = lax.broadcasted_iota(jnp.int32, x.shape, 0)
    return jnp.where(row >= k, pltpu.roll(x, k, 0), 0.0)


def _shift_up(x, k):
    n = x.shape[0]
    row = lax.broadcasted_iota(jnp.int32, x.shape, 0)
    return jnp.where(row < n - k, pltpu.roll(x, n - k, 0), 0.0)


def _dwconv(x, w_ref):
    return w_ref[2:3, :] * x + w_ref[1:2, :] * _shift_down(x, 1) + w_ref[0:1, :] * _shift_down(x, 2)


def _dwconv_bwd(x, dy, w_ref):
    dx = w_ref[2:3, :] * dy + w_ref[1:2, :] * _shift_up(dy, 1) + w_ref[0:1, :] * _shift_up(dy, 2)
    dw = jnp.concatenate([jnp.sum(dy * _shift_down(x, 2), axis=0, keepdims=True),
                          jnp.sum(dy * _shift_down(x, 1), axis=0, keepdims=True),
                          jnp.sum(dy * x, axis=0, keepdims=True)], axis=0)
    return dx, dw


def _interleave(dst, src):
    seg_rows = src.shape[0] // SUBLANES
    for seg in range(SUBLANES):
        dst[pl.ds(seg, seg_rows, stride=SUBLANES), :] = src[seg * seg_rows:(seg + 1) * seg_rows, :]


def _deinterleave(dst, src):
    seg_rows = src.shape[0] // SUBLANES
    for seg in range(SUBLANES):
        dst[seg * seg_rows:(seg + 1) * seg_rows, :] = src[pl.ds(seg, seg_rows, stride=SUBLANES), :]


def _segment_shift(x, reverse):
    row = lax.broadcasted_iota(jnp.int32, x.shape, 0)
    if reverse:
        return jnp.where(row < SUBLANES - 1, pltpu.roll(x, SUBLANES - 1, 0), 0.0)
    return jnp.where(row >= 1, pltpu.roll(x, 1, 0), 0.0)


def _scan(s_re, s_im, pw_ref, reverse, pair=None):
    n_steps = s_re.shape[0] // SUBLANES
    n_strips = s_re.shape[1] // LANES
    sign = -1.0 if reverse else 1.0
    strips = [slice(st * LANES, (st + 1) * LANES) for st in range(n_strips)]

    def rows_of(j):
        step = (n_steps - 1 - j) if reverse else j
        return pl.ds(pl.multiple_of(step * SUBLANES, SUBLANES), SUBLANES)

    a = [(jnp.broadcast_to(pw_ref[0, 0:1, lanes], (SUBLANES, LANES)),
          sign * jnp.broadcast_to(pw_ref[1, 0:1, lanes], (SUBLANES, LANES))) for lanes in strips]

    def local(i, carry):
        for half in range(2):
            rows = rows_of(2 * i + half)
            out = []
            for st, lanes in enumerate(strips):
                (ar, ai), cr, ci = a[st], carry[2 * st], carry[2 * st + 1]
                xr = s_re[rows, lanes] + (ar * cr - ai * ci)
                xi = s_im[rows, lanes] + (ar * ci + ai * cr)
                s_re[rows, lanes] = xr
                s_im[rows, lanes] = xi
                out += [xr, xi]
            carry = tuple(out)
        return carry

    zero = jnp.zeros((SUBLANES, LANES), F32)
    ends = lax.fori_loop(0, n_steps // 2, local, (zero,) * (2 * n_strips))

    entering = []
    row = lax.broadcasted_iota(jnp.int32, (SUBLANES, LANES), 0)
    for st, lanes in enumerate(strips):
        tr, ti = ends[2 * st], ends[2 * st + 1]
        mr = jnp.broadcast_to(pw_ref[0, n_steps - 1:n_steps, lanes], (SUBLANES, LANES))
        mi = sign * jnp.broadcast_to(pw_ref[1, n_steps - 1:n_steps, lanes], (SUBLANES, LANES))
        for k in (1, 2, 4):
            keep = (row < SUBLANES - k) if reverse else (row >= k)
            rr = jnp.where(keep, pltpu.roll(tr, SUBLANES - k if reverse else k, 0), 0.0)
            ri = jnp.where(keep, pltpu.roll(ti, SUBLANES - k if reverse else k, 0), 0.0)
            tr, ti = tr + (mr * rr - mi * ri), ti + (mr * ri + mi * rr)
            mr, mi = mr * mr - mi * mi, 2.0 * mr * mi
        entering += [_segment_shift(tr, reverse), _segment_shift(ti, reverse)]

    def fix(i, carry):
        carry, sums = carry[:2 * n_strips], carry[2 * n_strips:]
        for half in range(2):
            j = 2 * i + half
            rows = rows_of(j)
            out, acc = [], []
            for st, lanes in enumerate(strips):
                (ar, ai), cr, ci = a[st], carry[2 * st], carry[2 * st + 1]
                cr, ci = ar * cr - ai * ci, ar * ci + ai * cr
                xr = s_re[rows, lanes] + cr
                xi = s_im[rows, lanes] + ci
                s_re[rows, lanes] = xr
                s_im[rows, lanes] = xi
                out += [cr, ci]
                if pair is not None:
                    p_rows = rows_of(jnp.minimum(j + 1, n_steps - 1))
                    keep = (j < n_steps - 1).astype(F32)
                    pr = pair[0][p_rows, lanes] * keep
                    pi = pair[1][p_rows, lanes] * keep
                    acc += [sums[2 * st] + (xr * pr + xi * pi), sums[2 * st + 1] + (xi * pr - xr * pi)]
            carry, sums = tuple(out), tuple(acc)
        return carry + sums

    n_sums = 0 if pair is None else 2 * n_strips
    out = lax.fori_loop(0, n_steps // 2, fix, tuple(entering) + (zero,) * n_sums)
    return out[2 * n_strips:]


def _seq_fwd(proj, conv_w, bc_re, bc_im, cc_re, cc_im, dskip, a_pow, name):
    tp = proj.shape[0]
    dh = proj.shape[1] // 4
    nq = dh // LANES
    sw = STATE * N_GROUPS // nq

    def body(b_ref, c_ref, v_ref, u_ref, w_ref, bre_ref, bim_ref, cre_ref, cim_ref, d_ref, pw_ref,
             co_ref, y_ref, g_ref, s_re, s_im, u_il, y_il):
        co_ref[...] = b_ref[...] * _dwconv(c_ref[...] * v_ref[...], w_ref)
        _interleave(u_il, u_ref)
        ub = u_il[...].astype(BF16)
        s_re[...] = jnp.dot(ub, bre_ref[...], preferred_element_type=F32)
        s_im[...] = jnp.dot(ub, bim_ref[...], preferred_element_type=F32)
        _scan(s_re, s_im, pw_ref, False)
        y_il[...] = (jnp.dot(s_re[...].astype(BF16), cre_ref[...], preferred_element_type=F32)
                     - jnp.dot(s_im[...].astype(BF16), cim_ref[...], preferred_element_type=F32))
        _deinterleave(y_ref, y_il)
        y = y_ref[...] + d_ref[...] * u_ref[...]
        y_ref[...] = y
        g_ref[...] = _gelu(y).astype(BF16)

    col = lambda off: pl.BlockSpec((tp, LANES), lambda q, off=off: (0, off * nq + q))
    blk = pl.BlockSpec((tp, LANES), lambda q: (0, q))
    return pl.pallas_call(
        body, name=name, grid=(nq,),
        in_specs=[col(0), col(1), col(2), col(3),
                  pl.BlockSpec((3, LANES), lambda q: (0, q)),
                  pl.BlockSpec((LANES, sw), lambda q: (0, q)), pl.BlockSpec((LANES, sw), lambda q: (0, q)),
                  pl.BlockSpec((sw, LANES), lambda q: (q, 0)), pl.BlockSpec((sw, LANES), lambda q: (q, 0)),
                  pl.BlockSpec((1, LANES), lambda q: (0, q)),
                  pl.BlockSpec((2, tp // SUBLANES, sw), lambda q: (0, 0, q))],
        out_specs=[blk, blk, blk, pl.BlockSpec((tp, sw), lambda q: (0, q)), pl.BlockSpec((tp, sw), lambda q: (0, q))],
        out_shape=[jax.ShapeDtypeStruct((tp, dh), F32), jax.ShapeDtypeStruct((tp, dh), F32),
                   jax.ShapeDtypeStruct((tp, dh), BF16),
                   jax.ShapeDtypeStruct((tp, nq * sw), F32), jax.ShapeDtypeStruct((tp, nq * sw), F32)],
        scratch_shapes=[pltpu.VMEM((tp, LANES), F32), pltpu.VMEM((tp, LANES), F32)],
        compiler_params=_cparams("parallel"),
    )(proj, proj, proj, proj, conv_w, bc_re, bc_im, cc_re, cc_im, dskip, a_pow)


def _conv_bwd(proj, dco, conv_w, name):
    tp = proj.shape[0]
    dh = proj.shape[1] // 4
    nq = dh // LANES

    def body(b_ref, c_ref, v_ref, dco_ref, w_ref, dproj_ref, dw_ref, stage, sem):
        q = pl.program_id(0)
        cg = c_ref[...]
        vg = v_ref[...]
        cv = cg * vg
        dco_v = dco_ref[...]
        dcv, dw = _dwconv_bwd(cv, dco_v * b_ref[...], w_ref)
        dw_ref[...] = dw
        stage[0] = (dco_v * _dwconv(cv, w_ref)).astype(BF16)
        stage[1] = (dcv * vg).astype(BF16)
        stage[2] = (dcv * cg).astype(BF16)
        copies = [pltpu.make_async_copy(stage.at[p], dproj_ref.at[:, pl.ds((p * nq + q) * LANES, LANES)], sem.at[p])
                  for p in range(3)]
        for cp in copies:
            cp.start()
        for cp in copies:
            cp.wait()

    col = lambda off: pl.BlockSpec((tp, LANES), lambda q, off=off: (0, off * nq + q))
    return pl.pallas_call(
        body, name=name, grid=(nq,),
        in_specs=[col(0), col(1), col(2), pl.BlockSpec((tp, LANES), lambda q: (0, q)),
                  pl.BlockSpec((3, LANES), lambda q: (0, q))],
        out_specs=[pl.BlockSpec(memory_space=pl.ANY), pl.BlockSpec((3, LANES), lambda q: (0, q))],
        out_shape=[jax.ShapeDtypeStruct((tp, 4 * dh), BF16), jax.ShapeDtypeStruct((3, dh), F32)],
        scratch_shapes=[pltpu.VMEM((3, tp, LANES), BF16), pltpu.SemaphoreType.DMA((3,))],
        compiler_params=_cparams("arbitrary"),
    )(proj, proj, proj, dco, conv_w)


def _ssm_bwd(proj, y, dg, dproj, states, bc_re, bc_im, cc_re, cc_im, dskip, a_pow, name):
    tp = proj.shape[0]
    dh = proj.shape[1] // 4
    nq = dh // LANES
    sw = STATE * N_GROUPS // nq

    def body(u_ref, y_ref, dg_ref, dproj_in, s_re, s_im, bre_ref, bim_ref, cre_ref, cim_ref, d_ref, pw_ref,
             dproj_ref, dbre_ref, dbim_ref, dcre_ref, dcim_ref, dd_ref, dar_ref, dai_ref,
             l_re, l_im, a_il, b_il, stage, sem):
        del dproj_in
        q = pl.program_id(0)
        nt = (((1,), (1,)), ((), ()))
        tn = (((0,), (0,)), ((), ()))
        _interleave(a_il, u_ref)
        ub = a_il[...].astype(BF16)
        dy_rows = dg_ref[...] * _gelu_grad(y_ref[...])
        dd_ref[...] = jnp.sum(dy_rows * u_ref[...], axis=0, keepdims=True)
        _interleave(b_il, dy_rows)
        dy = b_il[...]
        dyb = dy.astype(BF16)
        l_re[...] = lax.dot_general(dyb, cre_ref[...], nt, preferred_element_type=F32)
        l_im[...] = -lax.dot_general(dyb, cim_ref[...], nt, preferred_element_type=F32)
        dcre_ref[...] = lax.dot_general(s_re[...].astype(BF16), dyb, tn, preferred_element_type=F32)
        dcim_ref[...] = -lax.dot_general(s_im[...].astype(BF16), dyb, tn, preferred_element_type=F32)
        sums = _scan(l_re, l_im, pw_ref, True, pair=(s_re, s_im))
        rest = tp - SUBLANES
        for st in range(sw // LANES):
            lanes = slice(st * LANES, (st + 1) * LANES)
            lr0, li0 = l_re[:SUBLANES, lanes], l_im[:SUBLANES, lanes]
            pr0, pi0 = _segment_shift(s_re[rest:, lanes], False), _segment_shift(s_im[rest:, lanes], False)
            dar_ref[:, lanes] = jnp.sum(sums[2 * st] + (lr0 * pr0 + li0 * pi0), axis=0, keepdims=True)
            dai_ref[:, lanes] = jnp.sum(sums[2 * st + 1] + (li0 * pr0 - lr0 * pi0), axis=0, keepdims=True)
        lrb = l_re[...].astype(BF16)
        lib = l_im[...].astype(BF16)
        a_il[...] = (dy * d_ref[...] + lax.dot_general(lrb, bre_ref[...], nt, preferred_element_type=F32)
                     + lax.dot_general(lib, bim_ref[...], nt, preferred_element_type=F32))
        _deinterleave(b_il, a_il)
        stage[...] = b_il[...].astype(BF16)
        dbre_ref[...] = lax.dot_general(ub, lrb, tn, preferred_element_type=F32)
        dbim_ref[...] = lax.dot_general(ub, lib, tn, preferred_element_type=F32)
        cp = pltpu.make_async_copy(stage, dproj_ref.at[:, pl.ds((3 * nq + q) * LANES, LANES)], sem)
        cp.start()
        cp.wait()

    blk = pl.BlockSpec((tp, LANES), lambda q: (0, q))
    bspec = pl.BlockSpec((LANES, sw), lambda q: (0, q))
    cspec = pl.BlockSpec((sw, LANES), lambda q: (q, 0))
    tspec = pl.BlockSpec((2, tp // SUBLANES, sw), lambda q: (0, 0, q))
    nstate = STATE * N_GROUPS
    return pl.pallas_call(
        body, name=name, grid=(nq,),
        in_specs=[pl.BlockSpec((tp, LANES), lambda q: (0, 3 * nq + q)), blk, blk, pl.BlockSpec(memory_space=pl.ANY),
                  pl.BlockSpec((tp, sw), lambda q: (0, q)), pl.BlockSpec((tp, sw), lambda q: (0, q)),
                  bspec, bspec, cspec, cspec, pl.BlockSpec((1, LANES), lambda q: (0, q)), tspec],
        out_specs=[pl.BlockSpec(memory_space=pl.ANY), bspec, bspec, cspec, cspec,
                   pl.BlockSpec((1, LANES), lambda q: (0, q)),
                   pl.BlockSpec((1, sw), lambda q: (0, q)), pl.BlockSpec((1, sw), lambda q: (0, q))],
        out_shape=[jax.ShapeDtypeStruct((tp, 4 * dh), BF16),
                   jax.ShapeDtypeStruct((LANES, nstate), F32), jax.ShapeDtypeStruct((LANES, nstate), F32),
                   jax.ShapeDtypeStruct((nstate, LANES), F32), jax.ShapeDtypeStruct((nstate, LANES), F32),
                   jax.ShapeDtypeStruct((1, dh), F32),
                   jax.ShapeDtypeStruct((1, nstate), F32), jax.ShapeDtypeStruct((1, nstate), F32)],
        input_output_aliases={3: 0},
        scratch_shapes=[pltpu.VMEM((tp, sw), F32)] * 2 + [pltpu.VMEM((tp, LANES), F32)] * 2
        + [pltpu.VMEM((tp, LANES), BF16), pltpu.SemaphoreType.DMA],
        compiler_params=_cparams("arbitrary"),
    )(proj, y, dg, dproj, states[0], states[1], bc_re, bc_im, cc_re, cc_im, dskip, a_pow)


FFN_TILE = 256
FFN_ROWS = 32


def _window(x_ref, before, r0, rows, cols):
    if r0 == 0:
        return jnp.concatenate([before, x_ref[0:rows, cols]], axis=0)
    return x_ref[r0 - SUBLANES:r0 + rows, cols]


def _taps(window):
    return window[SUBLANES:], pltpu.roll(window, 1, 0)[SUBLANES:], pltpu.roll(window, 2, 0)[SUBLANES:]


def _conv_taps(taps, w):
    return w[2] * taps[0] + w[1] * taps[1] + w[0] * taps[2]


FFN_MM_ROWS = 544
FFN_MM_COLS = 1408


def _ffn_up_act(hn, w_up, fw, fb, col, others, name):
    tp, dm = hn.shape
    dff = w_up.shape[1] // 2
    tr, cw, rows = FFN_MM_ROWS, FFN_MM_COLS, FFN_ROWS
    nc = dff // cw
    n_others = 0 if others is None else 2

    def body(hn_ref, ma_ref, mv_ref, wa_ref, wv_ref, ba_ref, bv_ref, *rest):
        up_ref, act_ref, tail_ref = rest[n_others:]

        @pl.when(pl.program_id(1) == 0)
        def _():
            tail_ref[...] = jnp.zeros_like(tail_ref)

        x = hn_ref[...]
        up_ref[0] = jnp.dot(x, ma_ref[...], preferred_element_type=F32)
        up_ref[1] = jnp.dot(x, mv_ref[...], preferred_element_type=F32)
        for c0 in range(0, cw, FFN_TILE):
            cols = slice(c0, min(c0 + FFN_TILE, cw))
            wa, wv = [[w_ref[k:k + 1, cols] for k in range(3)] for w_ref in (wa_ref, wv_ref)]
            ba, bv = ba_ref[:, cols], bv_ref[:, cols]
            before_a, before_v = tail_ref[0, :, cols], tail_ref[1, :, cols]
            for r0 in range(0, tr, rows):
                a = _conv_taps(_taps(_window(up_ref.at[0], before_a, r0, rows, cols)), wa) + ba
                v = _conv_taps(_taps(_window(up_ref.at[1], before_v, r0, rows, cols)), wv) + bv
                act_ref[r0:r0 + rows, cols] = (a * _sigmoid(a) * v).astype(BF16)
            tail_ref[:, :, cols] = up_ref[:, tr - SUBLANES:tr, cols]

    par = lambda r, half: pl.BlockSpec((r, cw), lambda j, i: (0, half * nc + j))
    return pl.pallas_call(
        body, name=name, grid=(nc, tp // tr),
        in_specs=[pl.BlockSpec((tr, dm), lambda j, i: (i, 0)), par(dm, 0), par(dm, 1),
                  par(3, 0), par(3, 1), par(1, 0), par(1, 1)],
        out_specs=[pl.BlockSpec((2, tr, cw), lambda j, i: (0, i, j)), pl.BlockSpec((tr, cw), lambda j, i: (i, j))],
        out_shape=[jax.ShapeDtypeStruct((2, tp, dff), F32), jax.ShapeDtypeStruct((tp, dff), BF16)],
        scratch_shapes=[pltpu.VMEM((2, SUBLANES, cw), F32)],
        compiler_params=_cparams("arbitrary", "arbitrary"))(hn, w_up, w_up, fw, fw, fb, fb)


def _ffn_bwd(up, dh, w_down, fw, fb, name):
    _, tp, dff = up.shape
    two_ff = 2 * dff
    dm = dh.shape[1]
    tr, cw, rows = FFN_MM_ROWS, FFN_MM_COLS, FFN_ROWS
    nr, nc = tp // tr, dff // cw
    n_e = rows + SUBLANES
    pieces = tr // SUBLANES

    def body(ua_ref, uv_ref, pa_ref, pv_ref, dh_ref, wd_ref, wa_ref, wv_ref, ba_ref, bv_ref,
             dup_ref, dwa_ref, dwv_ref, dba_ref, dbv_ref, dact, stage, head_ref, sem):
        j, i = pl.program_id(0), pl.program_id(1)
        step = j * nr + i
        top = i == nr - 1
        sums = ((dwa_ref, dba_ref), (dwv_ref, dbv_ref))

        slot = step % 2

        def out_copies(at):
            r0 = pl.multiple_of((nr - 1 - at % nr) * tr, tr)
            return [pltpu.make_async_copy(
                stage.at[at % 2, s],
                dup_ref.at[pl.ds(r0, tr), pl.ds(pl.multiple_of(s * dff + at // nr * cw, LANES), cw)],
                sem.at[at % 2, s]) for s in range(2)]

        @pl.when(i == 0)
        def _():
            head_ref[...] = jnp.zeros_like(head_ref)
            for dw_ref, db_ref in sums:
                dw_ref[...] = jnp.zeros_like(dw_ref)
                db_ref[...] = jnp.zeros_like(db_ref)

        dact[...] = lax.dot_general(dh_ref[...], wd_ref[...], (((1,), (1,)), ((), ())), preferred_element_type=F32)

        @pl.when(step > 1)
        def _():
            for cp in out_copies(step - 2):
                cp.wait()

        def gate_bwd(taps, dact_v, w, bias):
            a, v = [_conv_taps(taps[s], w[s]) + bias[s] for s in range(2)]
            sg = _sigmoid(a)
            return [dact_v * v * sg * (1.0 + a * (1.0 - sg)), dact_v * a * sg]

        fold = lambda x: sum(x[r:r + SUBLANES] for r in range(0, rows, SUBLANES))
        for c0 in range(0, cw, FFN_TILE):
            cols = slice(c0, min(c0 + FFN_TILE, cw))
            w = [[w_ref[k:k + 1, cols] for k in range(3)] for w_ref in (wa_ref, wv_ref)]
            bias = [ba_ref[:, cols], bv_ref[:, cols]]
            before = [jnp.where(top, 0.0, p_ref[:, cols]) for p_ref in (pa_ref, pv_ref)]
            head = [head_ref[s, :, cols] for s in range(2)]
            piece = jnp.zeros_like(head[0])
            acc = [[piece] * 4 for _ in range(2)]
            for r0 in reversed(range(0, tr, rows)):
                taps = [_taps(_window(x_ref, before[s], r0, rows, cols)) for s, x_ref in enumerate((ua_ref, uv_ref))]
                d = gate_bwd(taps, dact[r0:r0 + rows, cols], w, bias)
                for s in range(2):
                    de = jnp.concatenate([d[s], head[s]], axis=0)
                    dx = (w[s][2] * d[s] + w[s][1] * pltpu.roll(de, n_e - 1, 0)[:rows]
                          + w[s][0] * pltpu.roll(de, n_e - 2, 0)[:rows])
                    stage[slot, s, r0:r0 + rows, cols] = dx.astype(BF16)
                    for k in range(3):
                        acc[s][k] = acc[s][k] + fold(d[s] * taps[s][2 - k])
                    acc[s][3] = acc[s][3] + fold(d[s])
                    head[s] = d[s][:SUBLANES]
            for s, (dw_ref, db_ref) in enumerate(sums):
                head_ref[s, :, cols] = head[s]
                dw_ref[:, cols] = dw_ref[:, cols] + jnp.concatenate(
                    [jnp.sum(x, axis=0, keepdims=True) for x in acc[s][:3]], axis=0)
                db_ref[:, cols] = db_ref[:, cols] + jnp.sum(acc[s][3], axis=0, keepdims=True)

        copies = out_copies(step)
        for cp in copies:
            cp.start()

        @pl.when(step == nc * nr - 1)
        def _():
            for cp in out_copies(step - 1) + copies:
                cp.wait()

    row = lambda i: nr - 1 - i
    main = lambda half: pl.BlockSpec((None, tr, cw), lambda j, i: (half, row(i), j))
    prev = lambda half: pl.BlockSpec((None, SUBLANES, cw), lambda j, i: (half, jnp.maximum(row(i) * pieces - 1, 0), j))
    par = lambda r, half: pl.BlockSpec((r, cw), lambda j, i: (0, half * nc + j))
    acc_spec = lambda r: pl.BlockSpec((r, cw), lambda j, i: (0, j))
    return pl.pallas_call(
        body, name=name, grid=(nc, nr),
        in_specs=[main(0), main(1), prev(0), prev(1),
                  pl.BlockSpec((tr, dm), lambda j, i: (row(i), 0)), pl.BlockSpec((cw, dm), lambda j, i: (j, 0)),
                  par(3, 0), par(3, 1), par(1, 0), par(1, 1)],
        out_specs=[pl.BlockSpec(memory_space=pl.ANY), acc_spec(3), acc_spec(3), acc_spec(1), acc_spec(1)],
        out_shape=[jax.ShapeDtypeStruct((tp, two_ff), BF16),
                   jax.ShapeDtypeStruct((3, dff), F32), jax.ShapeDtypeStruct((3, dff), F32),
                   jax.ShapeDtypeStruct((1, dff), F32), jax.ShapeDtypeStruct((1, dff), F32)],
        scratch_shapes=[pltpu.VMEM((tr, cw), F32), pltpu.VMEM((2, 2, tr, cw), BF16),
                        pltpu.VMEM((2, SUBLANES, cw), F32), pltpu.SemaphoreType.DMA((2, 2))],
        compiler_params=_cparams("arbitrary", "arbitrary"))(up, up, up, up, dh, w_down, fw, fw, fb, fb)


def _zoh(lr, li, ld):
    dt = jnp.exp(ld)
    mag = jnp.exp(lr * dt)
    ang = li * dt
    ar = mag * jnp.cos(ang)
    ai = mag * jnp.sin(ang)
    den = lr * lr + li * li
    nr = ar - 1.0
    fr = (nr * lr + ai * li) / den
    fi = (ai * lr - nr * li) / den
    return dt, ar, ai, den, nr, fr, fi


def _s5_prep(lr, li, ld, b_re, b_im, n_pow, name):
    nstate = lr.shape[1]

    def body(lr_ref, li_ref, ld_ref, bre_ref, bim_ref, pw_ref, bcre_ref, bcim_ref):
        _, ar, ai, _, _, fr, fi = _zoh(lr_ref[...], li_ref[...], ld_ref[...])
        bre = bre_ref[...]
        bim = bim_ref[...]
        bcre_ref[...] = (fr * bre - fi * bim).astype(BF16)
        bcim_ref[...] = (fr * bim + fi * bre).astype(BF16)
        row = lax.broadcasted_iota(jnp.int32, (SUBLANES, nstate), 0)
        pr, pi = jnp.zeros((SUBLANES, nstate), F32), jnp.zeros((SUBLANES, nstate), F32)
        cr, ci = ar, ai
        for t in range(SUBLANES):
            pr, pi = jnp.where(row == t, cr, pr), jnp.where(row == t, ci, pi)
            cr, ci = cr * ar - ci * ai, cr * ai + ci * ar
        pw_ref[0, 0:SUBLANES, :] = pr
        pw_ref[1, 0:SUBLANES, :] = pi
        n = SUBLANES
        while n < n_pow:
            m = min(n, n_pow - n)
            tr, ti = pw_ref[0, n - 1:n, :], pw_ref[1, n - 1:n, :]
            xr, xi = pw_ref[0, 0:m, :], pw_ref[1, 0:m, :]
            pw_ref[0, n:n + m, :] = xr * tr - xi * ti
            pw_ref[1, n:n + m, :] = xr * ti + xi * tr
            n += m

    vmem = pl.BlockSpec(memory_space=pltpu.VMEM)
    return pl.pallas_call(
        body, name=name, in_specs=[vmem] * 5, out_specs=[vmem] * 3,
        out_shape=[jax.ShapeDtypeStruct((2, n_pow, nstate), F32)] + [jax.ShapeDtypeStruct(b_re.shape, BF16)] * 2,
        compiler_params=pltpu.CompilerParams(vmem_limit_bytes=VMEM_LIMIT))(lr, li, ld, b_re, b_im)


def _s5_prep_bwd(lr, li, ld, b_re, b_im, da_re, da_im, dbc_re, dbc_im, name):
    def body(lr_ref, li_ref, ld_ref, bre_ref, bim_ref, dar_ref, dai_ref, dbcre_ref, dbcim_ref,
             dlr_ref, dli_ref, dld_ref, dbre_ref, dbim_ref):
        lr, li = lr_ref[...], li_ref[...]
        dt, ar, ai, den, nr, fr, fi = _zoh(lr, li, ld_ref[...])
        bre, bim = bre_ref[...], bim_ref[...]
        gre, gim = dbcre_ref[...], dbcim_ref[...]
        dbre_ref[...] = fr * gre + fi * gim
        dbim_ref[...] = fr * gim - fi * gre
        g_fr = jnp.sum(gre * bre + gim * bim, axis=0, keepdims=True)
        g_fi = jnp.sum(gim * bre - gre * bim, axis=0, keepdims=True)
        g_ar = dar_ref[...] + (g_fr * lr - g_fi * li) / den
        g_ai = dai_ref[...] + (g_fr * li + g_fi * lr) / den
        d_lr = (g_fr * (nr - 2.0 * fr * lr) + g_fi * (ai - 2.0 * fi * lr)) / den
        d_li = (g_fr * (ai - 2.0 * fr * li) - g_fi * (nr + 2.0 * fi * li)) / den
        g_logmag = g_ar * ar + g_ai * ai
        g_ang = g_ai * ar - g_ar * ai
        dlr_ref[...] = d_lr + g_logmag * dt
        dli_ref[...] = d_li + g_ang * dt
        d_ld = (g_logmag * lr + g_ang * li) * dt
        n = d_ld.shape[1]
        sh = 1
        while sh < STATE:
            d_ld = d_ld + pltpu.roll(d_ld, n - sh, 1)
            sh *= 2
        dld_ref[...] = d_ld

    vmem = pl.BlockSpec(memory_space=pltpu.VMEM)
    row = jax.ShapeDtypeStruct(lr.shape, F32)
    return pl.pallas_call(
        body, name=name, in_specs=[vmem] * 9, out_specs=[vmem] * 5,
        out_shape=[row, row, row, jax.ShapeDtypeStruct(b_re.shape, F32), jax.ShapeDtypeStruct(b_re.shape, F32)],
    )(lr, li, ld, b_re, b_im, da_re, da_im, dbc_re, dbc_im)


def _compact_b(bb):
    bq = bb.reshape(N_GROUPS // 8, 8, STATE, GROUP)
    m = jnp.einsum("ab,qbph->qahbp", jnp.eye(8, dtype=bb.dtype), bq).reshape(N_GROUPS // 8, LANES, 8 * STATE)
    return m.transpose(1, 0, 2).reshape(LANES, N_GROUPS * STATE)


def _expand_b(m):
    d = m.reshape(8, GROUP, N_GROUPS // 8, 8, STATE)
    return jnp.einsum("ahqap->qahp", d).reshape(N_GROUPS, GROUP, STATE)


def _compact_c(c):
    cq = c.reshape(N_GROUPS // 8, 8, GROUP, STATE)
    return jnp.einsum("ab,qbhp->qbpah", jnp.eye(8, dtype=c.dtype), cq).reshape(N_GROUPS * STATE, LANES)


def _expand_c(m):
    d = m.reshape(N_GROUPS // 8, 8, STATE, 8, GROUP)
    return jnp.einsum("qbpbh->qbhp", d).reshape(N_GROUPS, GROUP, STATE)


def _local_step(x, target, p, ex):
    seq, d = x.shape
    n_real = N_META + seq
    tp = -(-n_real // ROW_ALIGN) * ROW_ALIGN

    h0, hn1 = _input_norm_fwd(x, p["meta_tokens"], p["norm_mix_g"] + ex.zero, tp, "norm_mix")
    ex.forward("first", hn1)
    nstate = N_GROUPS * STATE
    s5 = (p["ssm_lam_re"].reshape(1, nstate), p["ssm_lam_im"].reshape(1, nstate),
          jnp.repeat(p["ssm_log_dt"].reshape(-1), STATE).reshape(1, nstate),
          _compact_b(p["ssm_b_re"]), _compact_b(p["ssm_b_im"]))
    a_pow, bc_re, bc_im = _s5_prep(*s5, tp // SUBLANES, "s5_prep")
    cc_re = _compact_c(p["ssm_c_re"]).astype(BF16)
    cc_im = _compact_c(p["ssm_c_im"]).astype(BF16)
    dskip = p["ssm_d"].reshape(1, -1)
    first = ex.weights("first", bc_re)
    proj = _mm(hn1, first["w_in"], "nn", "proj")
    started = ex.forward("mid", proj)
    co, y, g, *states = _seq_fwd(proj, p["conv_w"] + started[0, 0], bc_re, bc_im, cc_re, cc_im, dskip, a_pow,
                                 "seq_fwd")
    mid = ex.weights("mid", g)
    z = _mm(g, mid["ssm_w_glu"], "nn", "glu")
    mixed = _mix_fwd(co, y, z, p["gain_conv_out"], p["gain_ssm_out"], "mix_fwd")
    started = ex.forward("up", mixed)
    h1, hn2 = _proj_res_norm(mixed, mid["w_out"], h0, p["norm_ffn_g"], started, "out_proj_norm")
    late = ex.weights("up", hn2)
    up, act = _ffn_up_act(hn2, late["w_up"], p["ffn_conv_w"], p["ffn_conv_b"], 0, None, "ffn_up_act")
    late.update(ex.weights("down", act))
    loss, dh2, dh2b, d_gfin = _proj_loss_bwd(act, late["w_down"], h1, target, p["norm_final_g"], n_real,
                                             "down_proj_loss")

    g_w_down = _mm(act, dh2b, "tn", "g_w_down")
    dup, dfw_a, dfw_v, dfb_a, dfb_v = _ffn_bwd(up, dh2b, late["w_down"], p["ffn_conv_w"], p["ffn_conv_b"], "ffn_bwd")
    g_w_up = _mm(hn2, dup, "tn", "g_w_up")
    started = ex.grads_ready("late", {"w_up": g_w_up, "w_down": g_w_down})
    dh1, dh1b, d_gffn = _proj_norm_bwd(dup, late["w_up"], h1, p["norm_ffn_g"], dh2, started, "d_hn2_norm_bwd")
    started = ex.grads_send("late", dh1)
    g_w_out = _mm(mixed, dh1b, "tn", "g_w_out", after=started)
    dco, dz, dgp, d_gc, d_gs = _proj_mix_bwd(dh1b, mid["w_out"], co, y, z, p["gain_conv_out"],
                                             p["gain_ssm_out"], "d_mixed_mix_bwd")
    g_w_glu = _mm(g, dz, "tn", "g_w_glu")
    started = ex.grads_ready("mid", {"ssm_w_glu": g_w_glu, "w_out": g_w_out})
    dg = _mm(dz, mid["ssm_w_glu"], "nt", "d_gelu", acc_in=dgp, after=started)
    started = ex.grads_send("mid", dg)
    dproj, d_conv_w = _conv_bwd(proj, dco, p["conv_w"] + started[0, 0], "conv_bwd")
    (dproj, dbc_re, dbc_im, dcc_re, dcc_im, d_dskip, da_re, da_im) = _ssm_bwd(
        proj, y, dg, dproj, states, bc_re, bc_im, cc_re, cc_im, dskip, a_pow, "ssm_bwd")
    g_w_in = _mm(hn1, dproj, "tn", "g_w_in")
    started = ex.grads_ready("first", {"w_in": g_w_in})
    grad_x, d_meta, d_gmix = _proj_input_norm_bwd(dproj, first["w_in"], h0, p["norm_mix_g"], dh1, started, n_real,
                                                  "d_hn1_norm_bwd")
    started = ex.grads_send("first", d_gmix)

    d_lam_re, d_lam_im, d_log_dt, d_b_re, d_b_im = _s5_prep_bwd(*s5, da_re, da_im, dbc_re, dbc_im, "s5_prep_bwd")
    d_lam_re, d_lam_im = d_lam_re.reshape(N_GROUPS, STATE), d_lam_im.reshape(N_GROUPS, STATE)
    d_log_dt = d_log_dt[0, ::STATE]
    d_b_re, d_b_im = _expand_b(d_b_re), _expand_b(d_b_im)
    grads = {
        "meta_tokens": d_meta, "norm_mix_g": d_gmix, "w_in": g_w_in, "conv_w": d_conv_w,
        "ssm_lam_re": d_lam_re, "ssm_lam_im": d_lam_im, "ssm_log_dt": d_log_dt,
        "ssm_b_re": d_b_re, "ssm_b_im": d_b_im, "ssm_c_re": _expand_c(dcc_re), "ssm_c_im": _expand_c(dcc_im),
        "ssm_d": d_dskip.reshape(N_GROUPS, GROUP), "ssm_w_glu": g_w_glu,
        "gain_conv_out": d_gc, "gain_ssm_out": d_gs, "w_out": g_w_out, "norm_ffn_g": d_gffn,
        "w_up": g_w_up, "ffn_conv_w": jnp.concatenate([dfw_a, dfw_v], axis=1),
        "ffn_conv_b": jnp.concatenate([dfb_a, dfb_v], axis=1), "w_down": g_w_down, "norm_final_g": d_gfin,
    }
    return loss[0, 0] + started[0, 0], grad_x, grads


def _view(ref, axis, start, size):
    idx = [slice(None)] * len(ref.shape)
    idx[axis] = pl.ds(start, size)
    return ref.at[tuple(idx)]


def _exchange(name, ins, outs, aliases, local_copies, remote_copies):
    ni, no = len(ins), len(outs)
    nl, nr = len(local_copies), len(remote_copies)

    def body(*refs):
        in_refs, out_refs = refs[:ni], refs[ni:ni + no]
        send_sems, recv_sems, local_sems = refs[ni + no:]
        x, y, c = lax.axis_index("x"), lax.axis_index("y"), lax.axis_index("c")
        pos = (x, y, c, 2 * x + y)
        locals_ = [pltpu.make_async_copy(s(in_refs, out_refs, pos), d(in_refs, out_refs, pos), local_sems.at[i])
                   for i, (s, d) in enumerate(local_copies)]
        remotes = []
        for i, (s, d, flip) in enumerate(remote_copies):
            peer = (1 - x if "x" in flip else x, 1 - y if "y" in flip else y, 1 - c if "c" in flip else c)
            remotes.append(pltpu.make_async_remote_copy(
                src_ref=s(in_refs, out_refs, pos), dst_ref=d(in_refs, out_refs, pos),
                send_sem=send_sems.at[i], recv_sem=recv_sems.at[i], device_id=peer, device_id_type=MESH))
        for cp in locals_ + remotes:
            cp.start()
        for cp in remotes:
            cp.wait_recv()
        for cp in remotes:
            cp.wait_send()
        for cp in locals_:
            cp.wait()

    hbm = pl.BlockSpec(memory_space=pl.ANY)
    return pl.pallas_call(
        body, name=name, in_specs=[hbm] * ni, out_specs=[hbm] * no, out_shape=outs,
        input_output_aliases=aliases,
        scratch_shapes=[pltpu.SemaphoreType.DMA((nr,)), pltpu.SemaphoreType.DMA((nr,)),
                        pltpu.SemaphoreType.DMA((max(nl, 1),))],
    )(*ins)


BIG = {"w_in": (0, 1), "ssm_w_glu": (1, 0), "w_out": (1, 0), "w_up": (0, 1), "w_down": (1, 0)}
BIG_NAMES = tuple(BIG)
FLIPS = ("y", "x", "xy")


def _peer_chip(pos, flip):
    x, y, _, _ = pos
    return 2 * (1 - x if "x" in flip else x) + (1 - y if "y" in flip else y)


def _block_rows(rows, cols, itemsize, mult):
    return _pick_tile(rows, max(mult, (2 * 1024 * 1024) // (cols * itemsize)), mult)


def _cast_into_full(w, kc, shard_axis, name):
    r, cdim = w.shape
    tr = _block_rows(r, cdim, 4, 16)
    nb = r // tr

    def body(kc_ref, w_ref, o_ref):
        o_ref[...] = w_ref[...].astype(BF16)

    if shard_axis == 1:
        full, o_spec = (r, 4 * cdim), pl.BlockSpec((tr, cdim), lambda i, kc: (i, kc[0]))
    else:
        full, o_spec = (4 * r, cdim), pl.BlockSpec((tr, cdim), lambda i, kc: (kc[0] * nb + i, 0))
    return pl.pallas_call(
        body, name=name,
        grid_spec=pltpu.PrefetchScalarGridSpec(
            num_scalar_prefetch=1, grid=(nb,), in_specs=[pl.BlockSpec((tr, cdim), lambda i, kc: (i, 0))],
            out_specs=o_spec),
        out_shape=jax.ShapeDtypeStruct(full, BF16), compiler_params=_cparams("parallel"))(kc, w)


def _pair_sum(g, recv, kc, half_axis, name, out_dtype):
    hr, hc = recv.shape
    tr = _block_rows(hr, hc, 4, 16)
    nb = hr // tr

    def body(kc_ref, g_ref, r_ref, o_ref):
        o_ref[...] = (g_ref[...] + r_ref[...]).astype(out_dtype)

    if half_axis == 0:
        g_spec = pl.BlockSpec((tr, hc), lambda i, kc: (kc[1] * nb + i, 0))
    elif half_axis == 1:
        g_spec = pl.BlockSpec((tr, hc), lambda i, kc: (i, kc[1]))
    else:
        g_spec = pl.BlockSpec((tr, hc), lambda i, kc: (i, 0))
    same = pl.BlockSpec((tr, hc), lambda i, kc: (i, 0))
    return pl.pallas_call(
        body, name=name,
        grid_spec=pltpu.PrefetchScalarGridSpec(num_scalar_prefetch=1, grid=(nb,), in_specs=[g_spec, same],
                                               out_specs=same),
        out_shape=jax.ShapeDtypeStruct((hr, hc), out_dtype), compiler_params=_cparams("parallel"))(kc, g, recv)


def _chip_sum(own, recv, kc, own_axis, out_axis, name):
    _, sr, sc = recv.shape
    tr = _block_rows(sr, sc, 4, 16)
    nb = sr // tr

    def body(kc_ref, o_ref, r_ref, t_ref):
        k = kc_ref[0]
        own_v = o_ref[...].astype(F32)
        r = [r_ref[m].astype(F32) for m in range(3)]
        terms = []
        for kk in range(4):
            m = jnp.bitwise_xor(k, kk)
            terms.append(jnp.where(m == 0, own_v, jnp.where(m == 1, r[0], jnp.where(m == 2, r[1], r[2]))))
        t_ref[...] = (terms[0] + terms[1]) + (terms[2] + terms[3])

    if own_axis == 0:
        own_spec = pl.BlockSpec((tr, sc), lambda i, kc: (kc[0] * nb + i, 0))
    elif own_axis == 1:
        own_spec = pl.BlockSpec((tr, sc), lambda i, kc: (i, kc[0]))
    else:
        own_spec = pl.BlockSpec((tr, sc), lambda i, kc: (kc[1] * nb + i, 0))
    if out_axis == 0:
        out_full, out_spec = (2 * sr, sc), pl.BlockSpec((tr, sc), lambda i, kc: (kc[1] * nb + i, 0))
    else:
        out_full, out_spec = (sr, 2 * sc), pl.BlockSpec((tr, sc), lambda i, kc: (i, kc[1]))
    return pl.pallas_call(
        body, name=name,
        grid_spec=pltpu.PrefetchScalarGridSpec(
            num_scalar_prefetch=1, grid=(nb,),
            in_specs=[own_spec, pl.BlockSpec((3, tr, sc), lambda i, kc: (0, i, 0))],
            out_specs=out_spec),
        out_shape=jax.ShapeDtypeStruct(out_full, F32), compiler_params=_cparams("parallel"))(kc, own, recv)


def _adamw(w, g, m, v, name):
    r, cdim = w.shape
    tr = _block_rows(r, cdim, 4, 8)
    c1 = 1.0 - ADAM_B1 ** ADAM_STEP
    c2 = 1.0 - ADAM_B2 ** ADAM_STEP

    def body(w_ref, g_ref, m_ref, v_ref, go_ref, d_ref, nm_ref, nv_ref):
        gv = g_ref[...]
        go_ref[...] = gv
        nm = ADAM_B1 * m_ref[...] + (1.0 - ADAM_B1) * gv
        nv = ADAM_B2 * v_ref[...] + (1.0 - ADAM_B2) * (gv * gv)
        d_ref[...] = -ADAM_LR * ((nm / c1) / (jnp.sqrt(nv / c2) + ADAM_EPS) + ADAM_WD * w_ref[...])
        nm_ref[...] = nm
        nv_ref[...] = nv

    spec = _rows(cdim, tr)
    return pl.pallas_call(body, name=name, grid=(r // tr,), in_specs=[spec] * 4, out_specs=[spec] * 4,
                          out_shape=[jax.ShapeDtypeStruct((r, cdim), F32)] * 4,
                          compiler_params=_cparams("parallel"))(w, g, m, v)


def _adamw_whole(ws, gs, ms, vs, name):
    n = len(ws)
    c1 = 1.0 - ADAM_B1 ** ADAM_STEP
    c2 = 1.0 - ADAM_B2 ** ADAM_STEP

    def body(*refs):
        for i in range(n):
            w_ref, g_ref, m_ref, v_ref, d_ref, nm_ref, nv_ref = [refs[j * n + i] for j in range(7)]
            gv = g_ref[...]
            nm = ADAM_B1 * m_ref[...] + (1.0 - ADAM_B1) * gv
            nv = ADAM_B2 * v_ref[...] + (1.0 - ADAM_B2) * (gv * gv)
            d_ref[...] = -ADAM_LR * ((nm / c1) / (jnp.sqrt(nv / c2) + ADAM_EPS) + ADAM_WD * w_ref[...])
            nm_ref[...] = nm
            nv_ref[...] = nv

    vmem = pl.BlockSpec(memory_space=pltpu.VMEM)
    out = pl.pallas_call(body, name=name, in_specs=[vmem] * (4 * n), out_specs=[vmem] * (3 * n),
                         out_shape=[jax.ShapeDtypeStruct(a.shape, F32) for a in ws] * 3,
                         compiler_params=pltpu.CompilerParams(vmem_limit_bytes=VMEM_LIMIT))(*ws, *gs, *ms, *vs)
    return out[:n], out[n:2 * n], out[2 * n:]


SIDE_EFFECT = pltpu.SideEffectType.DATAFLOW_SIDE_EFFECTING


def _descriptors(copies, refs, send_sems, recv_sems, sem_off=0):
    x, y, c = lax.axis_index("x"), lax.axis_index("y"), lax.axis_index("c")
    pos = (x, y, c, 2 * x + y)
    out = []
    for i, (s, d, flip) in enumerate(copies):
        peer = (1 - x if "x" in flip else x, 1 - y if "y" in flip else y, 1 - c if "c" in flip else c)
        out.append(pltpu.make_async_remote_copy(
            src_ref=s(refs, refs, pos), dst_ref=d(refs, refs, pos),
            send_sem=send_sems.at[sem_off + i], recv_sem=recv_sems.at[sem_off + i],
            device_id=peer, device_id_type=MESH))
    return out


def _shifted(copies, off):
    return [(lambda I, O, pos, s=s: s(I[off:], O[off:], pos), lambda I, O, pos, d=d: d(I[off:], O[off:], pos), flip)
            for s, d, flip in copies]


BARRIER_IDS = {"c": (1, 2), "ici": (3, 4)}


def _exchange_start(name, bufs, copies, turns, after=None):
    n, nr = len(bufs), len(copies)
    na = 0 if after is None else 1
    flips = sorted({flip for _, _, flip in copies})
    kind = "c" if flips == ["c"] else "ici"
    collective_id = BARRIER_IDS[kind][turns[kind] % 2]
    turns[kind] += 1

    def body(*refs):
        x, y, c = lax.axis_index("x"), lax.axis_index("y"), lax.axis_index("c")
        barrier = pltpu.get_barrier_semaphore()
        for flip in flips:
            peer = (1 - x if "x" in flip else x, 1 - y if "y" in flip else y, 1 - c if "c" in flip else c)
            pl.semaphore_signal(barrier, inc=1, device_id=peer, device_id_type=MESH)
        pl.semaphore_wait(barrier, len(flips))
        for cp in _descriptors(copies, refs[:n], refs[n + na], refs[n + na + 1]):
            cp.start()
        token = refs[2 * n + na + 2]
        token[...] = jnp.zeros_like(token)

    hbm = pl.BlockSpec(memory_space=pltpu.HBM)
    sem = pl.BlockSpec(memory_space=pltpu.SEMAPHORE)
    out = pl.pallas_call(
        body, name=name,
        in_specs=[hbm] * n + [pl.BlockSpec(memory_space=pl.ANY)] * na,
        out_specs=(sem, sem, *[hbm] * n, pl.BlockSpec(memory_space=pltpu.VMEM)),
        out_shape=(pltpu.SemaphoreType.DMA((nr,)), pltpu.SemaphoreType.DMA((nr,)),
                   *[pltpu.HBM(b.shape, b.dtype) for b in bufs], jax.ShapeDtypeStruct((SUBLANES, LANES), F32)),
        input_output_aliases={i: 2 + i for i in range(n)},
        compiler_params=pltpu.CompilerParams(has_side_effects=SIDE_EFFECT, collective_id=collective_id),
    )(*[pltpu.with_memory_space_constraint(b, pltpu.HBM) for b in bufs], *([after] * na))
    return out[0], out[1], list(out[2:2 + n]), out[2 + n]


def _exchange_wait(name, send_sems, recv_sems, bufs, copies, after, sem_off=0):
    n = len(bufs)

    def body(*refs):
        for cp in _descriptors(copies, refs[:n], refs[n], refs[n + 1], sem_off):
            cp.wait_send()
            cp.wait_recv()

    hbm = pl.BlockSpec(memory_space=pltpu.HBM)
    sem = pl.BlockSpec(memory_space=pltpu.SEMAPHORE)
    out = pl.pallas_call(
        body, name=name,
        in_specs=[hbm] * n + [sem, sem, pl.BlockSpec(memory_space=pl.ANY)],
        out_specs=tuple([hbm] * n),
        out_shape=tuple(pltpu.HBM(b.shape, b.dtype) for b in bufs),
        input_output_aliases={i: i for i in range(n)},
        compiler_params=pltpu.CompilerParams(has_side_effects=SIDE_EFFECT),
    )(*bufs, send_sems, recv_sems, after)
    return list(out)


FIRST = ("w_in",)
MID = ("ssm_w_glu", "w_out")
LATE = ("w_up", "w_down")
GROUPS = {"first": FIRST, "mid": MID, "late": LATE}
ARRIVALS = {"first": FIRST, "mid": MID, "up": ("w_up",), "down": ("w_down",)}


def _gather_copies(names, shard_shapes):
    def region(i, chip, c):
        half_axis, shard_axis = BIG[names[i]]
        ssize = shard_shapes[i][shard_axis]
        hsize = shard_shapes[i][half_axis] // 2
        return lambda ref: _view(_view(ref, shard_axis, chip * ssize, ssize), half_axis, c * hsize, hsize)

    ici, d2d = [], []
    for i in range(len(names)):
        for flip in FLIPS:
            ici.append((lambda I, O, pos, i=i: region(i, pos[3], pos[2])(I[i]),
                        lambda I, O, pos, i=i: region(i, pos[3], pos[2])(O[i]), flip))
            d2d.append((lambda I, O, pos, i=i, flip=flip: region(i, _peer_chip(pos, flip), pos[2])(I[i]),
                        lambda I, O, pos, i=i, flip=flip: region(i, _peer_chip(pos, flip), pos[2])(O[i]), "c"))
    return ici, d2d


def _half_shape(n, shape):
    r, cdim = shape
    return (r // 2, cdim) if BIG[n][0] == 0 else (r, cdim // 2)


def _sub_shape(n, shape):
    hr, hc = _half_shape(n, shape)
    return (hr, hc // 4) if BIG[n][1] == 1 else (hr // 4, hc)


def _pair_copies(names, shapes, with_pack, dst_off):
    n = len(names)

    def other_half(i, ref, pos):
        half_axis = BIG[names[i]][0]
        hsize = shapes[i][half_axis] // 2
        return _view(ref, half_axis, (1 - pos[2]) * hsize, hsize)

    copies = [(lambda I, O, pos, i=i: other_half(i, I[i], pos), lambda I, O, pos, i=i: O[dst_off + i], "c")
              for i in range(n)]
    if with_pack:
        copies.append((lambda I, O, pos: I[n], lambda I, O, pos: O[dst_off + n], "c"))
    return copies


def _chip_copies(names, shapes, pack_rows, dst_off):
    n = len(names)

    def piece(i, ref, chip):
        shard_axis = BIG[names[i]][1]
        ssize = _sub_shape(names[i], shapes[i])[shard_axis]
        return _view(ref, shard_axis, chip * ssize, ssize)

    copies = []
    for i in range(n):
        for slot, flip in enumerate(FLIPS):
            copies.append((lambda I, O, pos, i=i, flip=flip: piece(i, I[i], _peer_chip(pos, flip)),
                           lambda I, O, pos, i=i, slot=slot: O[dst_off + i].at[slot], flip))
    if pack_rows:
        for slot, flip in enumerate(FLIPS):
            copies.append((lambda I, O, pos: _view(I[n], 0, pos[2] * (pack_rows // 2), pack_rows // 2),
                           lambda I, O, pos, slot=slot: O[dst_off + n].at[slot], flip))
    return copies


class _Exchanges:
    def __init__(self, shards, tiny, kc):
        self.kc = kc
        wb = {n: _cast_into_full(shards[n], kc, BIG[n][1], "cast_" + n) for n in BIG_NAMES}
        self.gathering, self.forwarding, self.pairing, self.reducing = {}, {}, {}, {}
        self.turns = {"c": 0, "ici": 0}
        tiny_copies = [(lambda I, O, pos: I[0], lambda I, O, pos: O[1].at[pos[3]], flip) for flip in FLIPS]
        self.gathering["tiny"] = (0, 0, 2, tiny_copies, None)
        bufs, copies = [tiny, lax.empty((4,) + tiny.shape, F32)], list(tiny_copies)
        for group, names in ARRIVALS.items():
            ici, d2d = _gather_copies(names, [shards[n].shape for n in names])
            self.gathering[group] = (len(bufs), len(copies), len(names), ici, d2d)
            copies += _shifted(ici, len(bufs))
            bufs += [wb[n] for n in names]
        self.started = _exchange_start("gather_start", bufs, copies, self.turns)
        self.zero = self.started[3][0, 0]

    def _arrived(self, group, after):
        buf_off, sem_off, n, ici, _ = self.gathering[group]
        send_sems, recv_sems, bufs, _ = self.started
        return _exchange_wait("gather_%s_wait" % group, send_sems, recv_sems, bufs[buf_off:buf_off + n], ici, after,
                              sem_off)

    def small_params(self, kc):
        tiny, got = self._arrived("tiny", self.started[3])
        return lax.dynamic_update_index_in_dim(got, tiny, kc[0], 0)

    def forward(self, group, after):
        d2d = self.gathering[group][4]
        self.forwarding[group] = (_exchange_start("forward_%s_start" % group, self._arrived(group, after), d2d,
                                                  self.turns), d2d)
        return self.forwarding[group][0][3]

    def weights(self, group, after):
        if group not in self.forwarding:
            after = self.forward(group, after)
        (send_sems, recv_sems, bufs, _), d2d = self.forwarding[group]
        full = _exchange_wait("forward_%s_wait" % group, send_sems, recv_sems, bufs, d2d, after)
        return dict(zip(ARRIVALS[group], full))

    def grads_ready(self, group, grads):
        names = GROUPS[group]
        gs = [grads[n] for n in names]
        land = [lax.empty(_half_shape(n, g.shape), F32) for n, g in zip(names, gs)]
        copies = _pair_copies(names, [g.shape for g in gs], False, len(names))
        started = _exchange_start("pair_%s_start" % group, gs + land, copies, self.turns)
        self.pairing[group] = (started, copies)
        return started[3]

    def grads_send(self, group, after):
        names = GROUPS[group]
        n = len(names)
        (send_sems, recv_sems, bufs, _), copies = self.pairing[group]
        bufs = _exchange_wait("pair_%s_wait" % group, send_sems, recv_sems, bufs, copies, after)
        chip = [_pair_sum(bufs[i], bufs[n + i], self.kc, BIG[names[i]][0], "pair_sum_" + names[i], BF16)
                for i in range(n)]
        shapes = [bufs[i].shape for i in range(n)]
        land = [lax.empty((3,) + _sub_shape(names[i], shapes[i]), BF16) for i in range(n)]
        copies = _chip_copies(names, shapes, 0, n)
        started = _exchange_start("reduce_%s_start" % group, chip + land, copies, self.turns)
        self.reducing[group] = (started, copies)
        return started[3]

    def finish_pack(self, pack):
        kc = self.kc
        prow = pack.shape[0] // 2
        recv = _exchange("reduce_d2d", [pack], [jax.ShapeDtypeStruct(pack.shape, F32)], {}, [],
                         _pair_copies((), [], True, 0))
        chip_pack = _pair_sum(pack, recv[0], kc, None, "pair_sum_pack", F32)
        copies = _chip_copies((), [], pack.shape[0], 1)
        land = lax.empty((3, prow, pack.shape[1]), F32)
        pack_sems_s, pack_sems_r, pack_bufs, after = _exchange_start("reduce_pack_start", [chip_pack, land], copies,
                                                                     self.turns)

        names, chips, recvs = (), [], []
        for group, group_names in GROUPS.items():
            (send_sems, recv_sems, bufs, _), group_copies = self.reducing[group]
            bufs = _exchange_wait("reduce_%s_wait" % group, send_sems, recv_sems, bufs, group_copies, after)
            n = len(group_names)
            names, chips, recvs = names + group_names, chips + bufs[:n], recvs + bufs[n:]
            after = bufs[n]
        total = [_chip_sum(chips[i], recvs[i], kc, BIG[n][1], BIG[n][0], "chip_sum_" + n)
                 for i, n in enumerate(names)]

        def my_half(half_axis, ref, pos):
            hsize = ref.shape[half_axis] // 2
            return _view(ref, half_axis, pos[2] * hsize, hsize)

        swap = [(lambda I, O, pos, i=i, n=n: my_half(BIG[n][0], I[i], pos),
                 lambda I, O, pos, i=i, n=n: my_half(BIG[n][0], O[i], pos), "c") for i, n in enumerate(names)]
        self.swapping = (_exchange_start("swap_start", total, swap, self.turns), swap, names)

        chip_pack, recv_pack = _exchange_wait("reduce_pack_wait", pack_sems_s, pack_sems_r, pack_bufs, copies,
                                              self.swapping[0][3])
        total_pack = _chip_sum(chip_pack, recv_pack, kc, None, 0, "chip_sum_pack")
        swap = [(lambda I, O, pos: my_half(0, I[0], pos), lambda I, O, pos: my_half(0, O[0], pos), "c")]
        return _exchange("swap_pack", [total_pack], [jax.ShapeDtypeStruct(pack.shape, F32)], {0: 0}, [], swap)[0]

    def finish_big(self, after):
        (send_sems, recv_sems, bufs, _), swap, names = self.swapping
        return dict(zip(names, _exchange_wait("swap_wait", send_sems, recv_sems, bufs, swap, after)))


WEIGHTS = ("meta_tokens", "norm_mix_g", "w_in", "conv_w", "ssm_lam_re", "ssm_lam_im", "ssm_log_dt", "ssm_b_re",
           "ssm_b_im", "ssm_c_re", "ssm_c_im", "ssm_d", "ssm_w_glu", "gain_conv_out", "gain_ssm_out", "w_out",
           "norm_ffn_g", "w_up", "ffn_conv_w", "ffn_conv_b", "w_down", "norm_final_g")
TINY_SHARDED = ("meta_tokens", "conv_w", "ffn_conv_w")
REPLICATED = tuple(n for n in WEIGHTS if n not in BIG and n not in TINY_SHARDED)
PACK_COLS = 512


def _pack(arrays, row_mult, cols):
    flat = jnp.concatenate([a.reshape(-1).astype(F32) for a in arrays])
    n = flat.shape[0]
    total = -(-n // (row_mult * cols)) * (row_mult * cols)
    return jnp.concatenate([flat, jnp.zeros((total - n,), F32)]).reshape(total // cols, cols)


def _unpack(packed, shapes):
    flat = packed.reshape(-1)
    out, off = [], 0
    for s in shapes:
        n = math.prod(s)
        out.append(flat[off:off + n].reshape(s))
        off += n
    return out


def kernel(x, meta_tokens, norm_mix_g, w_in, conv_w, ssm_lam_re, ssm_lam_im, ssm_log_dt, ssm_b_re, ssm_b_im, ssm_c_re, ssm_c_im, ssm_d, ssm_w_glu, gain_conv_out, gain_ssm_out, w_out, norm_ffn_g, w_up, ffn_conv_w, ffn_conv_b, w_down, norm_final_g, loss_target, m_meta_tokens, m_norm_mix_g, m_w_in, m_conv_w, m_ssm_lam_re, m_ssm_lam_im, m_ssm_log_dt, m_ssm_b_re, m_ssm_b_im, m_ssm_c_re, m_ssm_c_im, m_ssm_d, m_ssm_w_glu, m_gain_conv_out, m_gain_ssm_out, m_w_out, m_norm_ffn_g, m_w_up, m_ffn_conv_w, m_ffn_conv_b, m_w_down, m_norm_final_g, v_meta_tokens, v_norm_mix_g, v_w_in, v_conv_w, v_ssm_lam_re, v_ssm_lam_im, v_ssm_log_dt, v_ssm_b_re, v_ssm_b_im, v_ssm_c_re, v_ssm_c_im, v_ssm_d, v_ssm_w_glu, v_gain_conv_out, v_gain_ssm_out, v_w_out, v_norm_ffn_g, v_w_up, v_ffn_conv_w, v_ffn_conv_b, v_w_down, v_norm_final_g):
    args = dict(locals())
    w = {n: args[n] for n in WEIGHTS}
    mom = {n: args["m_" + n] for n in WEIGHTS}
    var = {n: args["v_" + n] for n in WEIGHTS}
    kx, ky, kc_ = lax.axis_index("x"), lax.axis_index("y"), lax.axis_index("c")
    chip = 2 * kx + ky
    kc = jnp.stack([chip, kc_]).astype(jnp.int32)

    def squeeze(n, a):
        if n == "meta_tokens":
            return a
        if n == "norm_final_g":
            return a.reshape(1, -1)
        a = a[0]
        return a.reshape(1, -1) if a.ndim == 1 else a

    wl = {n: squeeze(n, w[n]) for n in WEIGHTS}
    ml = {n: squeeze(n, mom[n]) for n in WEIGHTS}
    vl = {n: squeeze(n, var[n]) for n in WEIGHTS}

    tiny = _pack([wl[n] for n in TINY_SHARDED], SUBLANES, LANES)
    ex = _Exchanges({n: wl[n] for n in BIG_NAMES}, tiny, kc)
    tiny_shapes = [wl[n].shape for n in TINY_SHARDED]
    tiny_all = ex.small_params(kc)
    tiny_parts = [_unpack(tiny_all[k], tiny_shapes) for k in range(4)]
    p = {n: wl[n] for n in WEIGHTS if n not in BIG}
    for j, n in enumerate(TINY_SHARDED):
        p[n] = jnp.concatenate([tiny_parts[k][j] for k in range(4)], axis=1)
    p["ssm_log_dt"] = wl["ssm_log_dt"].reshape(-1)

    loss_local, grad_x, grads = _local_step(x[0], loss_target[0], p, ex)

    small_names = REPLICATED + TINY_SHARDED
    small_shapes = [tuple(grads[n].shape) for n in small_names] + [(1,)]
    pack = _pack([grads[n] for n in small_names] + [loss_local.reshape(1)], 2 * 16, PACK_COLS)
    g_pack = ex.finish_pack(pack)
    g_small = dict(zip(small_names + ("loss",), _unpack(g_pack, small_shapes)))
    loss = g_small["loss"][0]
    swapped = ("ssm_b_re", "ssm_b_im")

    def view(n, a):
        if n in swapped:
            return jnp.swapaxes(a, -1, -2)
        return a.reshape(1, -1) if a.ndim == 1 else a

    g = {}
    for n in REPLICATED:
        g[n] = g_small[n].reshape(view(n, w[n]).shape)
    for n in TINY_SHARDED:
        cols = wl[n].shape[1]
        g[n] = lax.dynamic_slice_in_dim(g_small[n], chip * cols, cols, axis=1).reshape(w[n].shape)
    delta, new_m, new_v = {}, {}, {}
    small = [[view(n, d[n]) for n in small_names] for d in (w, mom, var)]
    small.insert(1, [g[n] for n in small_names])
    for d, outs in zip((delta, new_m, new_v), _adamw_whole(*small, "adamw_small")):
        d.update(zip(small_names, outs))
    for d in (g, delta, new_m, new_v):
        d.update({n: jnp.swapaxes(d[n], -1, -2) for n in swapped})
    g_big = ex.finish_big(delta[small_names[0]])
    for n in BIG_NAMES:
        g[n], delta[n], new_m[n], new_v[n] = _adamw(wl[n], g_big[n], ml[n], vl[n], "adamw_" + n)

    def like(n, a):
        return a.reshape(w[n].shape)

    return (loss, grad_x[None], *[like(n, g[n]) for n in WEIGHTS], *[like(n, delta[n]) for n in WEIGHTS],
            *[like(n, new_m[n]) for n in WEIGHTS], *[like(n, new_v[n]) for n in WEIGHTS])
```
